```python
import math
import jax
import jax.numpy as jnp
from jax import lax
import numpy as np


D_MODEL = 1024
BATCH = 8
SEQ = 4096
DEPTH = 2

CHUNK = 64
N_BRANCH = 4
BRANCH_W = 512
SGU_BLOCK = 128
SGU_GROUPS = 4
SGU_GW = BRANCH_W // SGU_GROUPS
LRU_W = BRANCH_W
LRU_HEADS = 8
LRU_HD = LRU_W // LRU_HEADS
LRU_CONV = 4
LRU_C = 8.0
GDN_HEADS = 4
GDN_DK = 128
GDN_DV = 128
GDN_CONV = 4
POOL_WINDOWS = (2, 4, 8, 16)
POOL_GROUPS = 4
POOL_GW = BRANCH_W // POOL_GROUPS
D_FF = 2816
EPS = 1e-6
IN_SIZES = (BRANCH_W, BRANCH_W, LRU_W, LRU_W, GDN_HEADS * GDN_DK, GDN_HEADS * GDN_DK,
            GDN_HEADS * GDN_DV, GDN_HEADS * GDN_DV, GDN_HEADS, GDN_HEADS, BRANCH_W,
            N_BRANCH * D_MODEL)
P_IN = sum(IN_SIZES)

kernel_name = 'chunk_causal_hybrid_gated_merge'

F32 = jnp.float32


def rms_norm(x, g):
    xf = x.astype(F32)
    y = xf * lax.rsqrt(jnp.mean(xf * xf, axis=-1, keepdims=True) + EPS)
    return (y * g.astype(F32)).astype(x.dtype)


def swiglu(h, wg, wu, wd):
    a = jnp.einsum('bsd,df->bsf', h, wg)
    b = jnp.einsum('bsd,df->bsf', h, wu)
    return jnp.einsum('bsf,fd->bsd', jax.nn.silu(a) * b, wd)


def causal_dwconv(x, w):
    K = w.shape[0]
    S = x.shape[1]
    xp = jnp.pad(x, ((0, 0), (K - 1, 0), (0, 0)))
    y = xp[:, 0:S] * w[0]
    for k in range(1, K):
        y = y + xp[:, k:k + S] * w[k]
    return y


def l2norm(t):
    return t * lax.rsqrt(jnp.sum(t * t, axis=-1, keepdims=True) + EPS)


def sgu_mixer(u, v, ln_g, ln_b, w_s, b_s):
    B, S, _ = u.shape
    u = jax.nn.gelu(u)
    vf = jax.nn.gelu(v).astype(F32)
    mu = jnp.mean(vf, axis=-1, keepdims=True)
    var = jnp.mean(jnp.square(vf - mu), axis=-1, keepdims=True)
    vn = (vf - mu) * lax.rsqrt(var + EPS) * ln_g.astype(F32) + ln_b.astype(F32)
    n = S // SGU_BLOCK
    vb = vn.reshape(B, n, SGU_BLOCK, SGU_GROUPS, SGU_GW)
    pos_chunk = jnp.arange(SGU_BLOCK) // CHUNK
    mask = pos_chunk[:, None] >= pos_chunk[None, :]
    w = jnp.where(mask[None], w_s.astype(F32), 0.0)
    mixed = jnp.einsum('gij,bnjgc->bnigc', w, vb) + b_s.astype(F32).T[None, None, :, :, None]
    return u * mixed.reshape(B, S, BRANCH_W).astype(u.dtype)


def rglru_mixer(xb, gate, conv_w, conv_b, wa, ba, wx, bx, lam):
    B, S, _ = xb.shape
    xc = (causal_dwconv(xb, conv_w) + conv_b).astype(F32)
    xh = xc.reshape(B, S, LRU_HEADS, LRU_HD)
    r = jax.nn.sigmoid(jnp.einsum('bshi,hij->bshj', xh, wa.astype(F32)).reshape(B, S, LRU_W) + ba.astype(F32))
    i = jax.nn.sigmoid(jnp.einsum('bshi,hij->bshj', xh, wx.astype(F32)).reshape(B, S, LRU_W) + bx.astype(F32))
    log_a = -LRU_C * r * jax.nn.softplus(-lam.astype(F32))
    a = jnp.exp(log_a)
    mult = jnp.sqrt(-jnp.expm1(2.0 * log_a))
    b = mult * (i * xc)

    def combine(left, right):
        a1, b1 = left
        a2, b2 = right
        return a1 * a2, a2 * b1 + b2

    _, h = lax.associative_scan(combine, (a, b), axis=1)
    return (h * jax.nn.gelu(gate.astype(F32))).astype(xb.dtype)


def gated_deltanet_mixer(q, k, v, z, beta_pre, alpha_pre, conv_w, a_log, dt_bias, norm_g):
    B, S, _ = q.shape
    H, C = GDN_HEADS, CHUNK
    n = S // C
    qkv = jax.nn.silu(causal_dwconv(jnp.concatenate([q, k, v], axis=-1), conv_w)).astype(F32)
    q, k, v = jnp.split(qkv, [H * GDN_DK, 2 * H * GDN_DK], axis=-1)
    q = l2norm(q.reshape(B, S, H, GDN_DK)) * (GDN_DK ** -0.5)
    k = l2norm(k.reshape(B, S, H, GDN_DK))
    v = v.reshape(B, S, H, GDN_DV)
    beta = jax.nn.sigmoid(beta_pre.astype(F32))
    g = -jnp.exp(a_log.astype(F32)) * jax.nn.softplus(alpha_pre.astype(F32) + dt_bias.astype(F32))

    def chunks(t):
        t = t.reshape((B, n, C, H) + t.shape[3:])
        return jnp.moveaxis(t, 3, 1)

    qc, kc, vc = chunks(q), chunks(k), chunks(v)
    bc, gc = chunks(beta), chunks(g)
    gcum = jnp.cumsum(gc, axis=-1)
    idx = jnp.arange(C)
    incl = idx[:, None] >= idx[None, :]
    strict = idx[:, None] > idx[None, :]
    decay = jnp.exp(jnp.where(incl, gcum[..., :, None] - gcum[..., None, :], -jnp.inf))
    kb = kc * bc[..., None]
    a_mat = jnp.where(strict, jnp.einsum('bhnid,bhnjd->bhnij', kb, kc) * decay, 0.0)
    eye = jnp.eye(C, dtype=F32)
    t_mat = lax.linalg.triangular_solve(eye + a_mat, jnp.broadcast_to(eye, a_mat.shape),
                                        left_side=True, lower=True, unit_diagonal=True)
    u = jnp.einsum('bhnij,bhnjd->bhnid', t_mat, vc * bc[..., None])
    w = jnp.einsum('bhnij,bhnjd->bhnid', t_mat, kb * jnp.exp(gcum)[..., None])
    attn = jnp.where(incl, jnp.einsum('bhnid,bhnjd->bhnij', qc, kc) * decay, 0.0)

    def step(state, inp):
        q_i, k_i, u_i, w_i, g_i, attn_i = inp
        v_new = u_i - jnp.einsum('bhcd,bhde->bhce', w_i, state)
        o = (jnp.einsum('bhcd,bhde->bhce', q_i * jnp.exp(g_i)[..., None], state)
             + jnp.einsum('bhij,bhje->bhie', attn_i, v_new))
        g_last = g_i[..., -1]
        state = (state * jnp.exp(g_last)[..., None, None]
                 + jnp.einsum('bhcd,bhce->bhde', k_i * jnp.exp(g_last[..., None] - g_i)[..., None], v_new))
        return state, o

    xs = tuple(jnp.moveaxis(t, 2, 0) for t in (qc, kc, u, w, gcum, attn))
    state0 = jnp.zeros((B, H, GDN_DK, GDN_DV), F32)
    _, o = lax.scan(step, state0, xs)
    o = jnp.transpose(o, (1, 0, 3, 2, 4)).reshape(B, S, H, GDN_DV)
    o = o * lax.rsqrt(jnp.mean(o * o, axis=-1, keepdims=True) + EPS) * norm_g.astype(F32)
    o = o * jax.nn.silu(z.astype(F32).reshape(B, S, H, GDN_DV))
    return o.reshape(B, S, H * GDN_DV).astype(z.dtype)


def pool_mixer(xd, w_pool, scale):
    B, S, _ = xd.shape
    xf = xd.astype(F32).reshape(B, S, POOL_GROUPS, POOL_GW)
    cs = jnp.cumsum(xf, axis=1)
    t = jnp.arange(S)
    outs = []
    for gi, win in enumerate(POOL_WINDOWS):
        c = cs[:, :, gi]
        prev = jnp.pad(c, ((0, 0), (win, 0), (0, 0)))[:, :S]
        cnt = jnp.minimum(t + 1, win).astype(F32)[None, :, None]
        outs.append((c - prev) / cnt - xf[:, :, gi])
    pooled = jnp.stack(outs, axis=2)
    y = jnp.einsum('bsgc,gcd->bsgd', pooled, w_pool.astype(F32)).reshape(B, S, BRANCH_W)
    return (y * scale.astype(F32)).astype(xd.dtype)


def _fwd_setup_inputs(seed: int = 0) -> dict:
    key = jax.random.key(seed)
    ks = iter(jax.random.split(key, 40))
    L, D = DEPTH, D_MODEL

    def nrm(shape, scale):
        return jax.random.normal(next(ks), shape, F32) * scale

    def gain(shape):
        return 1.0 + 0.02 * jax.random.normal(next(ks), shape, F32)

    x = jax.random.normal(next(ks), (BATCH, SEQ, D), F32)
    ff1_norm = gain((L, D))
    ff1_wg = nrm((L, D, D_FF), D ** -0.5)
    ff1_wu = nrm((L, D, D_FF), D ** -0.5)
    ff1_wd = nrm((L, D_FF, D), D_FF ** -0.5)
    mix_norm = gain((L, D))
    w_in = nrm((L, D, P_IN), D ** -0.5)
    sgu_ln_g = gain((L, BRANCH_W))
    sgu_ln_b = nrm((L, BRANCH_W), 0.02)
    sgu_w = nrm((L, SGU_GROUPS, SGU_BLOCK, SGU_BLOCK), 0.5 * SGU_BLOCK ** -0.5)
    sgu_b = 1.0 + nrm((L, SGU_GROUPS, SGU_BLOCK), 0.1)
    lru_conv_w = nrm((L, LRU_CONV, LRU_W), LRU_CONV ** -0.5)
    lru_conv_b = nrm((L, LRU_W), 0.02)
    lru_wa = nrm((L, LRU_HEADS, LRU_HD, LRU_HD), LRU_HD ** -0.5)
    lru_ba = nrm((L, LRU_W), 0.02)
    lru_wx = nrm((L, LRU_HEADS, LRU_HD, LRU_HD), LRU_HD ** -0.5)
    lru_bx = nrm((L, LRU_W), 0.02)
    a_c = jax.random.uniform(next(ks), (L, LRU_W), F32, minval=0.9, maxval=0.999)
    a_base = a_c ** (1.0 / LRU_C)
    lru_lambda = jnp.log(a_base) - jnp.log1p(-a_base)
    gdn_conv_w = nrm((L, GDN_CONV, GDN_HEADS * (2 * GDN_DK + GDN_DV)), GDN_CONV ** -0.5)
    gdn_a_log = jnp.log(jax.random.uniform(next(ks), (L, GDN_HEADS), F32, minval=1.0, maxval=16.0))
    lo, hi = math.log(1e-3), math.log(1e-1)
    dt = jnp.exp(jax.random.uniform(next(ks), (L, GDN_HEADS), F32) * (hi - lo) + lo)
    gdn_dt_bias = dt + jnp.log(-jnp.expm1(-dt))
    gdn_norm_g = gain((L, GDN_DV))
    pool_w = nrm((L, POOL_GROUPS, POOL_GW, POOL_GW), POOL_GW ** -0.5)
    pool_scale = 1.0 + nrm((L, BRANCH_W), 0.1)
    w_branch = nrm((L, N_BRANCH, BRANCH_W, D), BRANCH_W ** -0.5)
    w_out = nrm((L, D, D), D ** -0.5)
    ff2_norm = gain((L, D))
    ff2_wg = nrm((L, D, D_FF), D ** -0.5)
    ff2_wu = nrm((L, D, D_FF), D ** -0.5)
    ff2_wd = nrm((L, D_FF, D), D_FF ** -0.5)
    final_norm = gain((D,))
    return {'x': x, 'ff1_norm': ff1_norm, 'ff1_wg': ff1_wg, 'ff1_wu': ff1_wu, 'ff1_wd': ff1_wd,
            'mix_norm': mix_norm, 'w_in': w_in, 'sgu_ln_g': sgu_ln_g, 'sgu_ln_b': sgu_ln_b,
            'sgu_w': sgu_w, 'sgu_b': sgu_b, 'lru_conv_w': lru_conv_w, 'lru_conv_b': lru_conv_b,
            'lru_wa': lru_wa, 'lru_ba': lru_ba, 'lru_wx': lru_wx, 'lru_bx': lru_bx,
            'lru_lambda': lru_lambda, 'gdn_conv_w': gdn_conv_w, 'gdn_a_log': gdn_a_log,
            'gdn_dt_bias': gdn_dt_bias, 'gdn_norm_g': gdn_norm_g, 'pool_w': pool_w,
            'pool_scale': pool_scale, 'w_branch': w_branch, 'w_out': w_out,
            'ff2_norm': ff2_norm, 'ff2_wg': ff2_wg, 'ff2_wu': ff2_wu, 'ff2_wd': ff2_wd,
            'final_norm': final_norm}


def _fwd_reference(x, ff1_norm, ff1_wg, ff1_wu, ff1_wd, mix_norm, w_in, sgu_ln_g, sgu_ln_b,
              sgu_w, sgu_b, lru_conv_w, lru_conv_b, lru_wa, lru_ba, lru_wx, lru_bx,
              lru_lambda, gdn_conv_w, gdn_a_log, gdn_dt_bias, gdn_norm_g, pool_w,
              pool_scale, w_branch, w_out, ff2_norm, ff2_wg, ff2_wu, ff2_wd, final_norm):
    B, S, _ = x.shape
    split_at = np.cumsum(IN_SIZES)[:-1].tolist()
    for l in range(DEPTH):
        x = x + 0.5 * swiglu(rms_norm(x, ff1_norm[l]), ff1_wg[l], ff1_wu[l], ff1_wd[l])
        h = rms_norm(x, mix_norm[l])
        proj = jnp.einsum('bsd,dp->bsp', h, w_in[l])
        (a_u, a_v, b_x, b_g, c_q, c_k, c_v, c_z, c_beta, c_alpha, d_x,
         gate_pre) = jnp.split(proj, split_at, axis=-1)
        y_a = sgu_mixer(a_u, a_v, sgu_ln_g[l], sgu_ln_b[l], sgu_w[l], sgu_b[l])
        y_b = rglru_mixer(b_x, b_g, lru_conv_w[l], lru_conv_b[l], lru_wa[l], lru_ba[l],
                          lru_wx[l], lru_bx[l], lru_lambda[l])
        y_c = gated_deltanet_mixer(c_q, c_k, c_v, c_z, c_beta, c_alpha, gdn_conv_w[l],
                                   gdn_a_log[l], gdn_dt_bias[l], gdn_norm_g[l])
        y_d = pool_mixer(d_x, pool_w[l], pool_scale[l])
        ys = jnp.stack([y_a, y_b, y_c, y_d], axis=2)
        br = jnp.einsum('bsgc,gcd->bsgd', ys, w_branch[l])
        gates = jax.nn.sigmoid(gate_pre.astype(F32)).astype(x.dtype).reshape(B, S, N_BRANCH, D_MODEL)
        merged = jnp.sum(gates * br, axis=2)
        x = x + jnp.einsum('bsd,de->bse', merged, w_out[l])
        x = x + 0.5 * swiglu(rms_norm(x, ff2_norm[l]), ff2_wg[l], ff2_wu[l], ff2_wd[l])
    return rms_norm(x, final_norm)


import jax as _jax
import jax.numpy as _jnp

TWIN_FORMAT = 'train_step'
FWD_PARAMS = ['x', 'ff1_norm', 'ff1_wg', 'ff1_wu', 'ff1_wd', 'mix_norm', 'w_in', 'sgu_ln_g', 'sgu_ln_b', 'sgu_w', 'sgu_b', 'lru_conv_w', 'lru_conv_b', 'lru_wa', 'lru_ba', 'lru_wx', 'lru_bx', 'lru_lambda', 'gdn_conv_w', 'gdn_a_log', 'gdn_dt_bias', 'gdn_norm_g', 'pool_w', 'pool_scale', 'w_branch', 'w_out', 'ff2_norm', 'ff2_wg', 'ff2_wu', 'ff2_wd', 'final_norm']
TWIN_WEIGHTS = ['ff1_norm', 'ff1_wg', 'ff1_wu', 'ff1_wd', 'mix_norm', 'w_in', 'sgu_ln_g', 'sgu_ln_b', 'sgu_w', 'sgu_b', 'lru_conv_w', 'lru_conv_b', 'lru_wa', 'lru_ba', 'lru_wx', 'lru_bx', 'lru_lambda', 'gdn_conv_w', 'gdn_a_log', 'gdn_dt_bias', 'gdn_norm_g', 'pool_w', 'pool_scale', 'w_branch', 'w_out', 'ff2_norm', 'ff2_wg', 'ff2_wu', 'ff2_wd', 'final_norm']
TWIN_DIFF_INPUT = 'x'
TWIN_INPUTS = ['x', 'ff1_norm', 'ff1_wg', 'ff1_wu', 'ff1_wd', 'mix_norm', 'w_in', 'sgu_ln_g', 'sgu_ln_b', 'sgu_w', 'sgu_b', 'lru_conv_w', 'lru_conv_b', 'lru_wa', 'lru_ba', 'lru_wx', 'lru_bx', 'lru_lambda', 'gdn_conv_w', 'gdn_a_log', 'gdn_dt_bias', 'gdn_norm_g', 'pool_w', 'pool_scale', 'w_branch', 'w_out', 'ff2_norm', 'ff2_wg', 'ff2_wu', 'ff2_wd', 'final_norm', 'loss_target', 'm_ff1_norm', 'm_ff1_wg', 'm_ff1_wu', 'm_ff1_wd', 'm_mix_norm', 'm_w_in', 'm_sgu_ln_g', 'm_sgu_ln_b', 'm_sgu_w', 'm_sgu_b', 'm_lru_conv_w', 'm_lru_conv_b', 'm_lru_wa', 'm_lru_ba', 'm_lru_wx', 'm_lru_bx', 'm_lru_lambda', 'm_gdn_conv_w', 'm_gdn_a_log', 'm_gdn_dt_bias', 'm_gdn_norm_g', 'm_pool_w', 'm_pool_scale', 'm_w_branch', 'm_w_out', 'm_ff2_norm', 'm_ff2_wg', 'm_ff2_wu', 'm_ff2_wd', 'm_final_norm', 'v_ff1_norm', 'v_ff1_wg', 'v_ff1_wu', 'v_ff1_wd', 'v_mix_norm', 'v_w_in', 'v_sgu_ln_g', 'v_sgu_ln_b', 'v_sgu_w', 'v_sgu_b', 'v_lru_conv_w', 'v_lru_conv_b', 'v_lru_wa', 'v_lru_ba', 'v_lru_wx', 'v_lru_bx', 'v_lru_lambda', 'v_gdn_conv_w', 'v_gdn_a_log', 'v_gdn_dt_bias', 'v_gdn_norm_g', 'v_pool_w', 'v_pool_scale', 'v_w_branch', 'v_w_out', 'v_ff2_norm', 'v_ff2_wg', 'v_ff2_wu', 'v_ff2_wd', 'v_final_norm']
TWIN_OUTPUTS = ['loss', 'grad_x', 'grad_ff1_norm', 'grad_ff1_wg', 'grad_ff1_wu', 'grad_ff1_wd', 'grad_mix_norm', 'grad_w_in', 'grad_sgu_ln_g', 'grad_sgu_ln_b', 'grad_sgu_w', 'grad_sgu_b', 'grad_lru_conv_w', 'grad_lru_conv_b', 'grad_lru_wa', 'grad_lru_ba', 'grad_lru_wx', 'grad_lru_bx', 'grad_lru_lambda', 'grad_gdn_conv_w', 'grad_gdn_a_log', 'grad_gdn_dt_bias', 'grad_gdn_norm_g', 'grad_pool_w', 'grad_pool_scale', 'grad_w_branch', 'grad_w_out', 'grad_ff2_norm', 'grad_ff2_wg', 'grad_ff2_wu', 'grad_ff2_wd', 'grad_final_norm', 'delta_ff1_norm', 'delta_ff1_wg', 'delta_ff1_wu', 'delta_ff1_wd', 'delta_mix_norm', 'delta_w_in', 'delta_sgu_ln_g', 'delta_sgu_ln_b', 'delta_sgu_w', 'delta_sgu_b', 'delta_lru_conv_w', 'delta_lru_conv_b', 'delta_lru_wa', 'delta_lru_ba', 'delta_lru_wx', 'delta_lru_bx', 'delta_lru_lambda', 'delta_gdn_conv_w', 'delta_gdn_a_log', 'delta_gdn_dt_bias', 'delta_gdn_norm_g', 'delta_pool_w', 'delta_pool_scale', 'delta_w_branch', 'delta_w_out', 'delta_ff2_norm', 'delta_ff2_wg', 'delta_ff2_wu', 'delta_ff2_wd', 'delta_final_norm', 'new_m_ff1_norm', 'new_m_ff1_wg', 'new_m_ff1_wu', 'new_m_ff1_wd', 'new_m_mix_norm', 'new_m_w_in', 'new_m_sgu_ln_g', 'new_m_sgu_ln_b', 'new_m_sgu_w', 'new_m_sgu_b', 'new_m_lru_conv_w', 'new_m_lru_conv_b', 'new_m_lru_wa', 'new_m_lru_ba', 'new_m_lru_wx', 'new_m_lru_bx', 'new_m_lru_lambda', 'new_m_gdn_conv_w', 'new_m_gdn_a_log', 'new_m_gdn_dt_bias', 'new_m_gdn_norm_g', 'new_m_pool_w', 'new_m_pool_scale', 'new_m_w_branch', 'new_m_w_out', 'new_m_ff2_norm', 'new_m_ff2_wg', 'new_m_ff2_wu', 'new_m_ff2_wd', 'new_m_final_norm', 'new_v_ff1_norm', 'new_v_ff1_wg', 'new_v_ff1_wu', 'new_v_ff1_wd', 'new_v_mix_norm', 'new_v_w_in', 'new_v_sgu_ln_g', 'new_v_sgu_ln_b', 'new_v_sgu_w', 'new_v_sgu_b', 'new_v_lru_conv_w', 'new_v_lru_conv_b', 'new_v_lru_wa', 'new_v_lru_ba', 'new_v_lru_wx', 'new_v_lru_bx', 'new_v_lru_lambda', 'new_v_gdn_conv_w', 'new_v_gdn_a_log', 'new_v_gdn_dt_bias', 'new_v_gdn_norm_g', 'new_v_pool_w', 'new_v_pool_scale', 'new_v_w_branch', 'new_v_w_out', 'new_v_ff2_norm', 'new_v_ff2_wg', 'new_v_ff2_wu', 'new_v_ff2_wd', 'new_v_final_norm']
TWIN_LEAF_KINDS = {'loss': 'loss', 'grad_x': 'grad_x', 'grad_ff1_norm': 'grad_w', 'grad_ff1_wg': 'grad_w', 'grad_ff1_wu': 'grad_w', 'grad_ff1_wd': 'grad_w', 'grad_mix_norm': 'grad_w', 'grad_w_in': 'grad_w', 'grad_sgu_ln_g': 'grad_w', 'grad_sgu_ln_b': 'grad_w', 'grad_sgu_w': 'grad_w', 'grad_sgu_b': 'grad_w', 'grad_lru_conv_w': 'grad_w', 'grad_lru_conv_b': 'grad_w', 'grad_lru_wa': 'grad_w', 'grad_lru_ba': 'grad_w', 'grad_lru_wx': 'grad_w', 'grad_lru_bx': 'grad_w', 'grad_lru_lambda': 'grad_w', 'grad_gdn_conv_w': 'grad_w', 'grad_gdn_a_log': 'grad_w', 'grad_gdn_dt_bias': 'grad_w', 'grad_gdn_norm_g': 'grad_w', 'grad_pool_w': 'grad_w', 'grad_pool_scale': 'grad_w', 'grad_w_branch': 'grad_w', 'grad_w_out': 'grad_w', 'grad_ff2_norm': 'grad_w', 'grad_ff2_wg': 'grad_w', 'grad_ff2_wu': 'grad_w', 'grad_ff2_wd': 'grad_w', 'grad_final_norm': 'grad_w', 'delta_ff1_norm': 'delta_w', 'delta_ff1_wg': 'delta_w', 'delta_ff1_wu': 'delta_w', 'delta_ff1_wd': 'delta_w', 'delta_mix_norm': 'delta_w', 'delta_w_in': 'delta_w', 'delta_sgu_ln_g': 'delta_w', 'delta_sgu_ln_b': 'delta_w', 'delta_sgu_w': 'delta_w', 'delta_sgu_b': 'delta_w', 'delta_lru_conv_w': 'delta_w', 'delta_lru_conv_b': 'delta_w', 'delta_lru_wa': 'delta_w', 'delta_lru_ba': 'delta_w', 'delta_lru_wx': 'delta_w', 'delta_lru_bx': 'delta_w', 'delta_lru_lambda': 'delta_w', 'delta_gdn_conv_w': 'delta_w', 'delta_gdn_a_log': 'delta_w', 'delta_gdn_dt_bias': 'delta_w', 'delta_gdn_norm_g': 'delta_w', 'delta_pool_w': 'delta_w', 'delta_pool_scale': 'delta_w', 'delta_w_branch': 'delta_w', 'delta_w_out': 'delta_w', 'delta_ff2_norm': 'delta_w', 'delta_ff2_wg': 'delta_w', 'delta_ff2_wu': 'delta_w', 'delta_ff2_wd': 'delta_w', 'delta_final_norm': 'delta_w', 'new_m_ff1_norm': 'new_m', 'new_m_ff1_wg': 'new_m', 'new_m_ff1_wu': 'new_m', 'new_m_ff1_wd': 'new_m', 'new_m_mix_norm': 'new_m', 'new_m_w_in': 'new_m', 'new_m_sgu_ln_g': 'new_m', 'new_m_sgu_ln_b': 'new_m', 'new_m_sgu_w': 'new_m', 'new_m_sgu_b': 'new_m', 'new_m_lru_conv_w': 'new_m', 'new_m_lru_conv_b': 'new_m', 'new_m_lru_wa': 'new_m', 'new_m_lru_ba': 'new_m', 'new_m_lru_wx': 'new_m', 'new_m_lru_bx': 'new_m', 'new_m_lru_lambda': 'new_m', 'new_m_gdn_conv_w': 'new_m', 'new_m_gdn_a_log': 'new_m', 'new_m_gdn_dt_bias': 'new_m', 'new_m_gdn_norm_g': 'new_m', 'new_m_pool_w': 'new_m', 'new_m_pool_scale': 'new_m', 'new_m_w_branch': 'new_m', 'new_m_w_out': 'new_m', 'new_m_ff2_norm': 'new_m', 'new_m_ff2_wg': 'new_m', 'new_m_ff2_wu': 'new_m', 'new_m_ff2_wd': 'new_m', 'new_m_final_norm': 'new_m', 'new_v_ff1_norm': 'new_v', 'new_v_ff1_wg': 'new_v', 'new_v_ff1_wu': 'new_v', 'new_v_ff1_wd': 'new_v', 'new_v_mix_norm': 'new_v', 'new_v_w_in': 'new_v', 'new_v_sgu_ln_g': 'new_v', 'new_v_sgu_ln_b': 'new_v', 'new_v_sgu_w': 'new_v', 'new_v_sgu_b': 'new_v', 'new_v_lru_conv_w': 'new_v', 'new_v_lru_conv_b': 'new_v', 'new_v_lru_wa': 'new_v', 'new_v_lru_ba': 'new_v', 'new_v_lru_wx': 'new_v', 'new_v_lru_bx': 'new_v', 'new_v_lru_lambda': 'new_v', 'new_v_gdn_conv_w': 'new_v', 'new_v_gdn_a_log': 'new_v', 'new_v_gdn_dt_bias': 'new_v', 'new_v_gdn_norm_g': 'new_v', 'new_v_pool_w': 'new_v', 'new_v_pool_scale': 'new_v', 'new_v_w_branch': 'new_v', 'new_v_w_out': 'new_v', 'new_v_ff2_norm': 'new_v', 'new_v_ff2_wg': 'new_v', 'new_v_ff2_wu': 'new_v', 'new_v_ff2_wd': 'new_v', 'new_v_final_norm': 'new_v'}


def _forward(args):
    return _fwd_reference(*[args[k] for k in FWD_PARAMS])


def _output_shape():
    out = _jax.eval_shape(lambda: _forward(_fwd_setup_inputs(0)))
    return out.shape, out.dtype

N_MICROBATCH = 1
ADAM_LR = 0.001
ADAM_B1 = 0.9
ADAM_B2 = 0.999
ADAM_EPS = 1e-08
ADAM_WD = 0.01
ADAM_STEP = 10
PER_EXAMPLE_BATCH_AXIS = {'x': 0, 'loss_target': 0}
SHARED_INPUTS = []
_WEIGHT_DTYPES = {'ff1_norm': _jnp.float32, 'ff1_wg': _jnp.float32, 'ff1_wu': _jnp.float32, 'ff1_wd': _jnp.float32, 'mix_norm': _jnp.float32, 'w_in': _jnp.float32, 'sgu_ln_g': _jnp.float32, 'sgu_ln_b': _jnp.float32, 'sgu_w': _jnp.float32, 'sgu_b': _jnp.float32, 'lru_conv_w': _jnp.float32, 'lru_conv_b': _jnp.float32, 'lru_wa': _jnp.float32, 'lru_ba': _jnp.float32, 'lru_wx': _jnp.float32, 'lru_bx': _jnp.float32, 'lru_lambda': _jnp.float32, 'gdn_conv_w': _jnp.float32, 'gdn_a_log': _jnp.float32, 'gdn_dt_bias': _jnp.float32, 'gdn_norm_g': _jnp.float32, 'pool_w': _jnp.float32, 'pool_scale': _jnp.float32, 'w_branch': _jnp.float32, 'w_out': _jnp.float32, 'ff2_norm': _jnp.float32, 'ff2_wg': _jnp.float32, 'ff2_wu': _jnp.float32, 'ff2_wd': _jnp.float32, 'final_norm': _jnp.float32}
MOMENT_SCALE = {'ff1_norm': 8.096842e-02, 'ff1_wg': 3.426067e-02, 'ff1_wu': 3.312581e-02, 'ff1_wd': 5.490497e-02, 'mix_norm': 1.476549e-01, 'w_in': 4.734818e-02, 'sgu_ln_g': 3.066145e-02, 'sgu_ln_b': 3.089124e-02, 'sgu_w': 6.103829e-02, 'sgu_b': 7.139596e-02, 'lru_conv_w': 6.983735e-02, 'lru_conv_b': 7.064703e-01, 'lru_wa': 2.337589e-02, 'lru_ba': 2.039689e-02, 'lru_wx': 4.293709e-02, 'lru_bx': 2.401087e-02, 'lru_lambda': 3.791538e-02, 'gdn_conv_w': 4.826058e-02, 'gdn_a_log': 6.751195e-01, 'gdn_dt_bias': 6.703460e-01, 'gdn_norm_g': 1.307671e-01, 'pool_w': 9.416826e-02, 'pool_scale': 9.501886e-02, 'w_branch': 5.489182e-02, 'w_out': 1.103458e-01, 'ff2_norm': 5.829655e-02, 'ff2_wg': 2.508049e-02, 'ff2_wu': 2.422469e-02, 'ff2_wd': 4.023957e-02, 'final_norm': 3.206643e+01}


def _to_microbatches(a, axis):
    t = _jnp.moveaxis(a, axis, 0)
    t = t.reshape((N_MICROBATCH, t.shape[0] // N_MICROBATCH) + t.shape[1:])
    return _jnp.moveaxis(t, 1, axis + 1)


def setup_inputs(seed: int = 0) -> dict:
    inp = _fwd_setup_inputs(seed)
    key = _jax.random.fold_in(_jax.random.key(seed), 7919)
    shape, _ = _output_shape()
    out = dict(inp)
    out["loss_target"] = _jax.random.normal(_jax.random.fold_in(key, 0), shape, _jnp.float32)
    for i, name in enumerate(TWIN_WEIGHTS):
        w = inp[name].astype(_jnp.float32)
        if MOMENT_SCALE is None:
            s = _jnp.sqrt(_jnp.mean(_jnp.square(w)) + 1e-30)
        else:
            s = MOMENT_SCALE[name]
        km, kv = _jax.random.split(_jax.random.fold_in(key, i + 1))
        out[name] = w
        out["m_" + name] = s * _jax.random.normal(km, w.shape, _jnp.float32)
        out["v_" + name] = (s * s) * _jax.random.uniform(kv, w.shape, _jnp.float32, 0.5, 1.5)
    if N_MICROBATCH > 1:
        for name, axis in PER_EXAMPLE_BATCH_AXIS.items():
            out[name] = _to_microbatches(out[name], axis)
    return {'x': out['x'], 'ff1_norm': out['ff1_norm'], 'ff1_wg': out['ff1_wg'], 'ff1_wu': out['ff1_wu'], 'ff1_wd': out['ff1_wd'], 'mix_norm': out['mix_norm'], 'w_in': out['w_in'], 'sgu_ln_g': out['sgu_ln_g'], 'sgu_ln_b': out['sgu_ln_b'], 'sgu_w': out['sgu_w'], 'sgu_b': out['sgu_b'], 'lru_conv_w': out['lru_conv_w'], 'lru_conv_b': out['lru_conv_b'], 'lru_wa': out['lru_wa'], 'lru_ba': out['lru_ba'], 'lru_wx': out['lru_wx'], 'lru_bx': out['lru_bx'], 'lru_lambda': out['lru_lambda'], 'gdn_conv_w': out['gdn_conv_w'], 'gdn_a_log': out['gdn_a_log'], 'gdn_dt_bias': out['gdn_dt_bias'], 'gdn_norm_g': out['gdn_norm_g'], 'pool_w': out['pool_w'], 'pool_scale': out['pool_scale'], 'w_branch': out['w_branch'], 'w_out': out['w_out'], 'ff2_norm': out['ff2_norm'], 'ff2_wg': out['ff2_wg'], 'ff2_wu': out['ff2_wu'], 'ff2_wd': out['ff2_wd'], 'final_norm': out['final_norm'], 'loss_target': out['loss_target'], 'm_ff1_norm': out['m_ff1_norm'], 'm_ff1_wg': out['m_ff1_wg'], 'm_ff1_wu': out['m_ff1_wu'], 'm_ff1_wd': out['m_ff1_wd'], 'm_mix_norm': out['m_mix_norm'], 'm_w_in': out['m_w_in'], 'm_sgu_ln_g': out['m_sgu_ln_g'], 'm_sgu_ln_b': out['m_sgu_ln_b'], 'm_sgu_w': out['m_sgu_w'], 'm_sgu_b': out['m_sgu_b'], 'm_lru_conv_w': out['m_lru_conv_w'], 'm_lru_conv_b': out['m_lru_conv_b'], 'm_lru_wa': out['m_lru_wa'], 'm_lru_ba': out['m_lru_ba'], 'm_lru_wx': out['m_lru_wx'], 'm_lru_bx': out['m_lru_bx'], 'm_lru_lambda': out['m_lru_lambda'], 'm_gdn_conv_w': out['m_gdn_conv_w'], 'm_gdn_a_log': out['m_gdn_a_log'], 'm_gdn_dt_bias': out['m_gdn_dt_bias'], 'm_gdn_norm_g': out['m_gdn_norm_g'], 'm_pool_w': out['m_pool_w'], 'm_pool_scale': out['m_pool_scale'], 'm_w_branch': out['m_w_branch'], 'm_w_out': out['m_w_out'], 'm_ff2_norm': out['m_ff2_norm'], 'm_ff2_wg': out['m_ff2_wg'], 'm_ff2_wu': out['m_ff2_wu'], 'm_ff2_wd': out['m_ff2_wd'], 'm_final_norm': out['m_final_norm'], 'v_ff1_norm': out['v_ff1_norm'], 'v_ff1_wg': out['v_ff1_wg'], 'v_ff1_wu': out['v_ff1_wu'], 'v_ff1_wd': out['v_ff1_wd'], 'v_mix_norm': out['v_mix_norm'], 'v_w_in': out['v_w_in'], 'v_sgu_ln_g': out['v_sgu_ln_g'], 'v_sgu_ln_b': out['v_sgu_ln_b'], 'v_sgu_w': out['v_sgu_w'], 'v_sgu_b': out['v_sgu_b'], 'v_lru_conv_w': out['v_lru_conv_w'], 'v_lru_conv_b': out['v_lru_conv_b'], 'v_lru_wa': out['v_lru_wa'], 'v_lru_ba': out['v_lru_ba'], 'v_lru_wx': out['v_lru_wx'], 'v_lru_bx': out['v_lru_bx'], 'v_lru_lambda': out['v_lru_lambda'], 'v_gdn_conv_w': out['v_gdn_conv_w'], 'v_gdn_a_log': out['v_gdn_a_log'], 'v_gdn_dt_bias': out['v_gdn_dt_bias'], 'v_gdn_norm_g': out['v_gdn_norm_g'], 'v_pool_w': out['v_pool_w'], 'v_pool_scale': out['v_pool_scale'], 'v_w_branch': out['v_w_branch'], 'v_w_out': out['v_w_out'], 'v_ff2_norm': out['v_ff2_norm'], 'v_ff2_wg': out['v_ff2_wg'], 'v_ff2_wu': out['v_ff2_wu'], 'v_ff2_wd': out['v_ff2_wd'], 'v_final_norm': out['v_final_norm']}


def _loss(weights, diff, rest, loss_target):
    with _jax.named_scope("forward"):
        args = {**rest, TWIN_DIFF_INPUT: diff, **{k: w.astype(_WEIGHT_DTYPES[k]) for k, w in weights.items()}}
        y = _forward(args)
    with _jax.named_scope("loss_head"):
        err = _jnp.square(y.astype(_jnp.float32) - loss_target)
        return 0.5 * _jnp.sum(_jnp.mean(err, axis=-1)) if err.ndim else 0.5 * err


def _adamw(w, g, m, v):
    m = ADAM_B1 * m + (1.0 - ADAM_B1) * g
    v = ADAM_B2 * v + (1.0 - ADAM_B2) * _jnp.square(g)
    m_hat = m / (1.0 - ADAM_B1 ** ADAM_STEP)
    v_hat = v / (1.0 - ADAM_B2 ** ADAM_STEP)
    delta = -ADAM_LR * (m_hat / (_jnp.sqrt(v_hat) + ADAM_EPS) + ADAM_WD * w)
    return delta, m, v


def reference(x, ff1_norm, ff1_wg, ff1_wu, ff1_wd, mix_norm, w_in, sgu_ln_g, sgu_ln_b, sgu_w, sgu_b, lru_conv_w, lru_conv_b, lru_wa, lru_ba, lru_wx, lru_bx, lru_lambda, gdn_conv_w, gdn_a_log, gdn_dt_bias, gdn_norm_g, pool_w, pool_scale, w_branch, w_out, ff2_norm, ff2_wg, ff2_wu, ff2_wd, final_norm, loss_target, m_ff1_norm, m_ff1_wg, m_ff1_wu, m_ff1_wd, m_mix_norm, m_w_in, m_sgu_ln_g, m_sgu_ln_b, m_sgu_w, m_sgu_b, m_lru_conv_w, m_lru_conv_b, m_lru_wa, m_lru_ba, m_lru_wx, m_lru_bx, m_lru_lambda, m_gdn_conv_w, m_gdn_a_log, m_gdn_dt_bias, m_gdn_norm_g, m_pool_w, m_pool_scale, m_w_branch, m_w_out, m_ff2_norm, m_ff2_wg, m_ff2_wu, m_ff2_wd, m_final_norm, v_ff1_norm, v_ff1_wg, v_ff1_wu, v_ff1_wd, v_mix_norm, v_w_in, v_sgu_ln_g, v_sgu_ln_b, v_sgu_w, v_sgu_b, v_lru_conv_w, v_lru_conv_b, v_lru_wa, v_lru_ba, v_lru_wx, v_lru_bx, v_lru_lambda, v_gdn_conv_w, v_gdn_a_log, v_gdn_dt_bias, v_gdn_norm_g, v_pool_w, v_pool_scale, v_w_branch, v_w_out, v_ff2_norm, v_ff2_wg, v_ff2_wu, v_ff2_wd, v_final_norm):
    given = dict(x=x, ff1_norm=ff1_norm, ff1_wg=ff1_wg, ff1_wu=ff1_wu, ff1_wd=ff1_wd, mix_norm=mix_norm, w_in=w_in, sgu_ln_g=sgu_ln_g, sgu_ln_b=sgu_ln_b, sgu_w=sgu_w, sgu_b=sgu_b, lru_conv_w=lru_conv_w, lru_conv_b=lru_conv_b, lru_wa=lru_wa, lru_ba=lru_ba, lru_wx=lru_wx, lru_bx=lru_bx, lru_lambda=lru_lambda, gdn_conv_w=gdn_conv_w, gdn_a_log=gdn_a_log, gdn_dt_bias=gdn_dt_bias, gdn_norm_g=gdn_norm_g, pool_w=pool_w, pool_scale=pool_scale, w_branch=w_branch, w_out=w_out, ff2_norm=ff2_norm, ff2_wg=ff2_wg, ff2_wu=ff2_wu, ff2_wd=ff2_wd, final_norm=final_norm, loss_target=loss_target, m_ff1_norm=m_ff1_norm, m_ff1_wg=m_ff1_wg, m_ff1_wu=m_ff1_wu, m_ff1_wd=m_ff1_wd, m_mix_norm=m_mix_norm, m_w_in=m_w_in, m_sgu_ln_g=m_sgu_ln_g, m_sgu_ln_b=m_sgu_ln_b, m_sgu_w=m_sgu_w, m_sgu_b=m_sgu_b, m_lru_conv_w=m_lru_conv_w, m_lru_conv_b=m_lru_conv_b, m_lru_wa=m_lru_wa, m_lru_ba=m_lru_ba, m_lru_wx=m_lru_wx, m_lru_bx=m_lru_bx, m_lru_lambda=m_lru_lambda, m_gdn_conv_w=m_gdn_conv_w, m_gdn_a_log=m_gdn_a_log, m_gdn_dt_bias=m_gdn_dt_bias, m_gdn_norm_g=m_gdn_norm_g, m_pool_w=m_pool_w, m_pool_scale=m_pool_scale, m_w_branch=m_w_branch, m_w_out=m_w_out, m_ff2_norm=m_ff2_norm, m_ff2_wg=m_ff2_wg, m_ff2_wu=m_ff2_wu, m_ff2_wd=m_ff2_wd, m_final_norm=m_final_norm, v_ff1_norm=v_ff1_norm, v_ff1_wg=v_ff1_wg, v_ff1_wu=v_ff1_wu, v_ff1_wd=v_ff1_wd, v_mix_norm=v_mix_norm, v_w_in=v_w_in, v_sgu_ln_g=v_sgu_ln_g, v_sgu_ln_b=v_sgu_ln_b, v_sgu_w=v_sgu_w, v_sgu_b=v_sgu_b, v_lru_conv_w=v_lru_conv_w, v_lru_conv_b=v_lru_conv_b, v_lru_wa=v_lru_wa, v_lru_ba=v_lru_ba, v_lru_wx=v_lru_wx, v_lru_bx=v_lru_bx, v_lru_lambda=v_lru_lambda, v_gdn_conv_w=v_gdn_conv_w, v_gdn_a_log=v_gdn_a_log, v_gdn_dt_bias=v_gdn_dt_bias, v_gdn_norm_g=v_gdn_norm_g, v_pool_w=v_pool_w, v_pool_scale=v_pool_scale, v_w_branch=v_w_branch, v_w_out=v_w_out, v_ff2_norm=v_ff2_norm, v_ff2_wg=v_ff2_wg, v_ff2_wu=v_ff2_wu, v_ff2_wd=v_ff2_wd, v_final_norm=v_final_norm)
    weights = {n: given[n] for n in TWIN_WEIGHTS}
    shared = {n: given[n] for n in SHARED_INPUTS}
    per_example = {n: given[n] for n in ['x']}
    grad_fn = _jax.value_and_grad(_loss, argnums=(0, 1))

    def one_microbatch(ex, loss_target):
        ex = dict(ex)
        diff = ex.pop(TWIN_DIFF_INPUT)
        return grad_fn(weights, diff, {**shared, **ex}, loss_target)

    if N_MICROBATCH == 1:
        loss, (grad_w, grad_x) = one_microbatch(per_example, given["loss_target"])
    else:
        def body(carry, xs):
            loss_sum, grad_sum = carry
            l_k, (gw_k, gx_k) = one_microbatch(xs[0], xs[1])
            with _jax.named_scope("update"):
                return (loss_sum + l_k, _jax.tree.map(_jnp.add, grad_sum, gw_k)), gx_k

        init = (_jnp.zeros((), _jnp.float32), _jax.tree.map(_jnp.zeros_like, weights))
        (loss, grad_w), grad_x = _jax.lax.scan(body, init, (per_example, given["loss_target"]))
    with _jax.named_scope("update"):
        delta_w, new_m, new_v = {}, {}, {}
        for n in TWIN_WEIGHTS:
            delta_w[n], new_m[n], new_v[n] = _adamw(weights[n], grad_w[n], given["m_" + n], given["v_" + n])
    return (loss, grad_x, *[grad_w[n] for n in TWIN_WEIGHTS], *[delta_w[n] for n in TWIN_WEIGHTS],
            *[new_m[n] for n in TWIN_WEIGHTS], *[new_v[n] for n in TWIN_WEIGHTS])
```

```python
import functools

import jax
import jax.numpy as jnp
from jax import lax
from jax.experimental import pallas as pl
from jax.experimental.pallas import tpu as pltpu

F32 = jnp.float32
BF16 = jnp.bfloat16
HI = lax.Precision.HIGHEST

N_DEV = 8
D = 1024
FF = 2816
BW = 512
NBR = 4
CHUNK = 64
EPS = 1e-6
LRU_C = 8.0
GDN_DK = 128

COL_AU, COL_AV, COL_BX, COL_BG = 0, 512, 1024, 1536
COL_CQ, COL_CK, COL_CV, COL_CZ = 2048, 2560, 3072, 3584
COL_DX, COL_GATE, COL_TAIL = 4096, 4608, 8704
PW = 9216
P_IN = 8712

ADAM_LR, ADAM_B1, ADAM_B2, ADAM_EPS, ADAM_WD, ADAM_STEP = 0.001, 0.9, 0.999, 1e-08, 0.01, 10

VMEM_LIMIT_V7X = 56 * 1024 * 1024

_NN = (((1,), (0,)), ((), ()))
_NT = (((1,), (1,)), ((), ()))
_TN = (((0,), (0,)), ((), ()))


def _cp(*sem):
    return pltpu.CompilerParams(dimension_semantics=tuple(sem), vmem_limit_bytes=VMEM_LIMIT_V7X)


def _dot(a, b, dims=_NN):
    return lax.dot_general(a.astype(BF16), b.astype(BF16), dims, preferred_element_type=F32)


def _dot_hi(a, b, dims=_NN):
    return lax.dot_general(a, b, dims, precision=HI, preferred_element_type=F32)


def _pick(n, cands):
    for c in cands:
        if n % c == 0:
            return c
    return n


@jax.custom_jvp
def _log1p(x):
    u = 1.0 + x
    return jnp.where(u == 1.0, x, x * jnp.log(u) / jnp.where(u == 1.0, 1.0, u - 1.0))


@_log1p.defjvp
def _log1p_jvp(p, t):
    (x,), (dx,) = p, t
    return _log1p(x), dx / (1.0 + x)


@jax.custom_jvp
def _expm1(x):
    u = jnp.exp(x)
    lu = jnp.log(u)
    small = (u == 1.0) | (lu == 0.0)
    return jnp.where(small, x, (u - 1.0) * x / jnp.where(small, 1.0, lu))


@_expm1.defjvp
def _expm1_jvp(p, t):
    (x,), (dx,) = p, t
    return _expm1(x), dx * jnp.exp(x)


def _softplus(x):
    return jnp.maximum(x, 0.0) + _log1p(jnp.exp(-jnp.abs(x)))


def _sigmoid(x):
    return jax.nn.sigmoid(x)


def _silu(x):
    return x * jax.nn.sigmoid(x)


def _gelu(x):
    return jax.nn.gelu(x)


@functools.partial(jax.custom_vjp, nondiff_argnums=(1,))
def _shift(x, s):
    return x if s == 0 else pltpu.roll(x, s, 0)


def _shift_fwd(x, s):
    return _shift(x, s), None


def _shift_bwd(s, _, g):
    n = g.shape[0]
    return (g if s == 0 else pltpu.roll(g, n - s, 0),)


_shift.defvjp(_shift_fwd, _shift_bwd)


def _scan_steps(a, b, reverse):
    n = a.shape[0]
    row = lax.broadcasted_iota(jnp.int32, a.shape, 0)
    k = 1
    while k < n:
        sh = n - k if reverse else k
        m = (row < n - k) if reverse else (row >= k)
        a_s = jnp.where(m, pltpu.roll(a, sh, 0), 1.0)
        b_s = jnp.where(m, pltpu.roll(b, sh, 0), 0.0)
        b = a * b_s + b
        a = a * a_s
        k *= 2
    return b


@jax.custom_vjp
def _scan(a, b):
    return _scan_steps(a, b, False)


def _scan_fwd(a, b):
    h = _scan_steps(a, b, False)
    return h, (a, h)


def _scan_bwd(res, dh):
    a, h = res
    n = a.shape[0]
    row = lax.broadcasted_iota(jnp.int32, a.shape, 0)
    a_next = jnp.where(row < n - 1, pltpu.roll(a, n - 1, 0), 0.0)
    g = _scan_steps(a_next, dh, True)
    h_prev = jnp.where(row >= 1, pltpu.roll(h, 1, 0), 0.0)
    return g * h_prev, g


_scan.defvjp(_scan_fwd, _scan_bwd)


def _mm(name, pairs, mode, out_dtype, *, res=None, scale=1.0, bm=None, bn=None, bk=None):
    a0, b0 = pairs[0]
    if mode == "nn":
        (M, K), N = a0.shape, b0.shape[1]
    elif mode == "nt":
        (M, K), N = a0.shape, b0.shape[0]
    else:
        (K, M), N = a0.shape, b0.shape[1]
    bm = bm or _pick(M, (512, 256, 128))
    bn = bn or _pick(N, (512, 256, 128))
    bk = bk or _pick(K, (1024, 512, 1408, 256, 128))
    nk = K // bk
    npair = len(pairs)
    dims = {"nn": _NN, "nt": _NT, "tn": _TN}[mode]

    def body(*refs):
        ab = refs[:2 * npair]
        pos = 2 * npair
        r_ref = None
        if res is not None:
            r_ref = refs[pos]
            pos += 1
        o_ref = refs[pos]
        part = None
        for p in range(npair):
            d = _dot(ab[2 * p][...], ab[2 * p + 1][...], dims)
            part = d if part is None else part + d

        def finish(acc):
            out = acc if scale == 1.0 else acc * scale
            if r_ref is not None:
                out = out + r_ref[...]
            o_ref[...] = out.astype(out_dtype)

        if nk == 1:
            finish(part)
        else:
            acc_ref = refs[pos + 1]
            k = pl.program_id(2)

            @pl.when(k == 0)
            def _():
                acc_ref[...] = part

            @pl.when(k > 0)
            def _():
                acc_ref[...] += part

            @pl.when(k == nk - 1)
            def _():
                finish(acc_ref[...])

    if mode == "nn":
        a_spec = pl.BlockSpec((bm, bk), lambda i, j, k: (i, k))
        b_spec = pl.BlockSpec((bk, bn), lambda i, j, k: (k, j))
    elif mode == "nt":
        a_spec = pl.BlockSpec((bm, bk), lambda i, j, k: (i, k))
        b_spec = pl.BlockSpec((bn, bk), lambda i, j, k: (j, k))
    else:
        a_spec = pl.BlockSpec((bk, bm), lambda i, j, k: (k, i))
        b_spec = pl.BlockSpec((bk, bn), lambda i, j, k: (k, j))
    o_spec = pl.BlockSpec((bm, bn), lambda i, j, k: (i, j))
    in_specs, args = [], []
    for a, b in pairs:
        in_specs += [a_spec, b_spec]
        args += [a, b]
    if res is not None:
        in_specs.append(o_spec)
        args.append(res)
    return pl.pallas_call(
        body, name=name, grid=(M // bm, N // bn, nk),
        in_specs=in_specs, out_specs=o_spec,
        out_shape=jax.ShapeDtypeStruct((M, N), out_dtype),
        scratch_shapes=[pltpu.VMEM((bm, bn), F32)] if nk > 1 else [],
        compiler_params=_cp("parallel", "parallel", "arbitrary"),
    )(*args)


def _rms_fwd(name, x, g):
    T = x.shape[0]
    bm = _pick(T, (512, 256, 128))

    def body(x_ref, g_ref, o_ref):
        xv = x_ref[...]
        r = lax.rsqrt(jnp.mean(xv * xv, axis=-1, keepdims=True) + EPS)
        o_ref[...] = (xv * r * g_ref[...]).astype(BF16)

    return pl.pallas_call(
        body, name=name, grid=(T // bm,),
        in_specs=[pl.BlockSpec((bm, D), lambda i: (i, 0)), pl.BlockSpec((1, D), lambda i: (0, 0))],
        out_specs=pl.BlockSpec((bm, D), lambda i: (i, 0)),
        out_shape=jax.ShapeDtypeStruct((T, D), BF16),
        compiler_params=_cp("parallel"),
    )(x, g)


def _rms_bwd(name, x, g, dh, dres):
    T = x.shape[0]
    bm = _pick(T, (512, 256, 128))

    def body(x_ref, g_ref, dh_ref, dres_ref, dx_ref, dg_ref):
        xv = x_ref[...]
        r = lax.rsqrt(jnp.mean(xv * xv, axis=-1, keepdims=True) + EPS)
        xh = xv * r
        dhv = dh_ref[...]
        dxh = dhv * g_ref[...]
        dx_ref[...] = dres_ref[...] + r * (dxh - xh * jnp.mean(dxh * xh, axis=-1, keepdims=True))
        part = jnp.sum(dhv * xh, axis=0, keepdims=True)

        @pl.when(pl.program_id(0) == 0)
        def _():
            dg_ref[...] = part

        @pl.when(pl.program_id(0) > 0)
        def _():
            dg_ref[...] += part

    row = pl.BlockSpec((bm, D), lambda i: (i, 0))
    vec = pl.BlockSpec((1, D), lambda i: (0, 0))
    return pl.pallas_call(
        body, name=name, grid=(T // bm,),
        in_specs=[row, vec, row, row], out_specs=[row, vec],
        out_shape=[jax.ShapeDtypeStruct((T, D), F32), jax.ShapeDtypeStruct((1, D), F32)],
        compiler_params=_cp("arbitrary"),
    )(x, g, dh, dres)


def _final_loss(name, x, g, tgt):
    T = x.shape[0]
    bm = _pick(T, (512, 256, 128))

    def body(x_ref, g_ref, t_ref, loss_ref, dx_ref, dg_ref):
        xv = x_ref[...]
        gv = g_ref[...]
        r = lax.rsqrt(jnp.mean(xv * xv, axis=-1, keepdims=True) + EPS)
        xh = xv * r
        e = xh * gv - t_ref[...]
        lpart = jnp.broadcast_to(0.5 * jnp.sum(jnp.mean(e * e, axis=-1, keepdims=True), axis=0, keepdims=True), (1, 128))
        dy = e * (1.0 / D)
        dxh = dy * gv
        dx_ref[...] = r * (dxh - xh * jnp.mean(dxh * xh, axis=-1, keepdims=True))
        gpart = jnp.sum(dy * xh, axis=0, keepdims=True)

        @pl.when(pl.program_id(0) == 0)
        def _():
            loss_ref[...] = lpart
            dg_ref[...] = gpart

        @pl.when(pl.program_id(0) > 0)
        def _():
            loss_ref[...] += lpart
            dg_ref[...] += gpart

    row = pl.BlockSpec((bm, D), lambda i: (i, 0))
    vec = pl.BlockSpec((1, D), lambda i: (0, 0))
    return pl.pallas_call(
        body, name=name, grid=(T // bm,),
        in_specs=[row, vec, row],
        out_specs=[pl.BlockSpec((1, 128), lambda i: (0, 0)), row, vec],
        out_shape=[jax.ShapeDtypeStruct((1, 128), F32), jax.ShapeDtypeStruct((T, D), F32),
                   jax.ShapeDtypeStruct((1, D), F32)],
        compiler_params=_cp("arbitrary"),
    )(x, g, tgt)


def _ffn_up(name, h, wg, wu):
    T = h.shape[0]
    bm = _pick(T, (512, 256, 128))
    bn = 256

    def body(h_ref, wg_ref, wu_ref, a_ref, b_ref, act_ref):
        hv = h_ref[...]
        a = _dot(hv, wg_ref[...])
        b = _dot(hv, wu_ref[...])
        a_ref[...] = a
        b_ref[...] = b
        act_ref[...] = (_silu(a) * b).astype(BF16)

    w_spec = pl.BlockSpec((D, bn), lambda i, j: (0, j))
    o_spec = pl.BlockSpec((bm, bn), lambda i, j: (i, j))
    return pl.pallas_call(
        body, name=name, grid=(T // bm, FF // bn),
        in_specs=[pl.BlockSpec((bm, D), lambda i, j: (i, 0)), w_spec, w_spec],
        out_specs=[o_spec, o_spec, o_spec],
        out_shape=[jax.ShapeDtypeStruct((T, FF), F32), jax.ShapeDtypeStruct((T, FF), F32),
                   jax.ShapeDtypeStruct((T, FF), BF16)],
        compiler_params=_cp("parallel", "parallel"),
    )(h, wg, wu)


def _ffn_dact(name, dy, wd, a, b):
    T = dy.shape[0]
    bm = _pick(T, (512, 256, 128))
    bn = 256

    def body(dy_ref, wd_ref, a_ref, b_ref, da_ref, db_ref):
        dact = 0.5 * _dot(dy_ref[...], wd_ref[...], _NT)
        av = a_ref[...]
        s = _sigmoid(av)
        da_ref[...] = (dact * b_ref[...] * (s * (1.0 + av * (1.0 - s)))).astype(BF16)
        db_ref[...] = (dact * (av * s)).astype(BF16)

    t_spec = pl.BlockSpec((bm, bn), lambda i, j: (i, j))
    return pl.pallas_call(
        body, name=name, grid=(T // bm, FF // bn),
        in_specs=[pl.BlockSpec((bm, D), lambda i, j: (i, 0)), pl.BlockSpec((bn, D), lambda i, j: (j, 0)),
                  t_spec, t_spec],
        out_specs=[t_spec, t_spec],
        out_shape=[jax.ShapeDtypeStruct((T, FF), BF16), jax.ShapeDtypeStruct((T, FF), BF16)],
        compiler_params=_cp("parallel", "parallel"),
    )(dy, wd, a, b)


def _merge_specs(T, bm, bn):
    y_spec = pl.BlockSpec((bm, BW), lambda i, j: (i, 0))
    wb_spec = pl.BlockSpec((NBR, BW, bn), lambda i, j: (0, 0, j))
    gate_specs = [pl.BlockSpec((bm, bn), functools.partial(lambda i, j, o: (i, o + j), o=(COL_GATE + g * D) // bn))
                  for g in range(NBR)]
    t_spec = pl.BlockSpec((bm, bn), lambda i, j: (i, j))
    return y_spec, wb_spec, gate_specs, t_spec


def _merge_fwd(name, ys, wb, proj):
    T = proj.shape[0]
    bm = _pick(T, (512, 256, 128))
    bn = 512
    y_spec, wb_spec, gate_specs, t_spec = _merge_specs(T, bm, bn)

    def body(y0, y1, y2, y3, wb_ref, g0, g1, g2, g3, o_ref):
        acc = None
        for g, (y_ref, g_ref) in enumerate(((y0, g0), (y1, g1), (y2, g2), (y3, g3))):
            t = _sigmoid(g_ref[...]) * _dot(y_ref[...], wb_ref[g])
            acc = t if acc is None else acc + t
        o_ref[...] = acc.astype(BF16)

    return pl.pallas_call(
        body, name=name, grid=(T // bm, D // bn),
        in_specs=[y_spec] * NBR + [wb_spec] + gate_specs, out_specs=t_spec,
        out_shape=jax.ShapeDtypeStruct((T, D), BF16),
        compiler_params=_cp("parallel", "parallel"),
    )(*ys, wb, proj, proj, proj, proj)


def _merge_bwd(name, dm, ys, wb, proj):
    T = proj.shape[0]
    bm = _pick(T, (512, 256, 128))
    bn = 512
    y_spec, wb_spec, gate_specs, t_spec = _merge_specs(T, bm, bn)

    def body(dm_ref, y0, y1, y2, y3, wb_ref, g0, g1, g2, g3, *outs):
        dmv = dm_ref[...]
        for g, (y_ref, g_ref) in enumerate(((y0, g0), (y1, g1), (y2, g2), (y3, g3))):
            br = _dot(y_ref[...], wb_ref[g])
            s = _sigmoid(g_ref[...])
            outs[g][...] = (dmv * br * (s * (1.0 - s))).astype(BF16)
            outs[NBR + g][...] = (dmv * s).astype(BF16)

    return pl.pallas_call(
        body, name=name, grid=(T // bm, D // bn),
        in_specs=[t_spec] + [y_spec] * NBR + [wb_spec] + gate_specs, out_specs=[t_spec] * (2 * NBR),
        out_shape=[jax.ShapeDtypeStruct((T, D), BF16)] * (2 * NBR),
        compiler_params=_cp("parallel", "parallel"),
    )(dm, *ys, wb, proj, proj, proj, proj)


def _sgu_block(u_pre, v_pre, ln_g, ln_b, w, bias):
    u = _gelu(u_pre)
    vf = _gelu(v_pre)
    mu = jnp.mean(vf, axis=-1, keepdims=True)
    var = jnp.mean(jnp.square(vf - mu), axis=-1, keepdims=True)
    vn = (vf - mu) * lax.rsqrt(var + EPS) * ln_g + ln_b
    ri = lax.broadcasted_iota(jnp.int32, (128, 128), 0)
    ci = lax.broadcasted_iota(jnp.int32, (128, 128), 1)
    mask = (ri // CHUNK) >= (ci // CHUNK)
    outs = [_dot(jnp.where(mask, w[g], 0.0), vn[:, g * 128:(g + 1) * 128]) for g in range(4)]
    mixed = jnp.concatenate(outs, axis=1) + bias
    return u * mixed


def _sgu_param_specs():
    return [pl.BlockSpec((1, BW), lambda i: (0, 0)), pl.BlockSpec((1, BW), lambda i: (0, 0)),
            pl.BlockSpec((4, 128, 128), lambda i: (0, 0, 0)), pl.BlockSpec((128, BW), lambda i: (0, 0))]


def _sgu_fwd(name, proj, ln_g, ln_b, w, bias):
    T = proj.shape[0]
    rb = _pick(T, (256, 128))

    def body(u_ref, v_ref, g_ref, b_ref, w_ref, bias_ref, y_ref):
        for n in range(rb // 128):
            rows = slice(n * 128, (n + 1) * 128)
            y = _sgu_block(u_ref[rows, :], v_ref[rows, :], g_ref[...], b_ref[...], w_ref[...], bias_ref[...])
            y_ref[rows, :] = y.astype(BF16)

    return pl.pallas_call(
        body, name=name, grid=(T // rb,),
        in_specs=[pl.BlockSpec((rb, BW), lambda i: (i, COL_AU // BW)), pl.BlockSpec((rb, BW), lambda i: (i, COL_AV // BW))]
        + _sgu_param_specs(),
        out_specs=pl.BlockSpec((rb, BW), lambda i: (i, 0)),
        out_shape=jax.ShapeDtypeStruct((T, BW), BF16),
        compiler_params=_cp("parallel"),
    )(proj, proj, ln_g, ln_b, w, bias)


def _sgu_bwd(name, proj, dy, ln_g, ln_b, w, bias):
    T = proj.shape[0]
    rb = _pick(T, (256, 128))

    def body(u_ref, v_ref, dy_ref, g_ref, b_ref, w_ref, bias_ref, du_ref, dv_ref, dg_ref, db_ref, dw_ref, dbias_ref):
        acc = None
        for n in range(rb // 128):
            rows = slice(n * 128, (n + 1) * 128)
            _, vjp = jax.vjp(_sgu_block, u_ref[rows, :], v_ref[rows, :], g_ref[...], b_ref[...], w_ref[...],
                             bias_ref[...])
            du, dv, *dp = vjp(dy_ref[rows, :])
            du_ref[rows, :] = du.astype(BF16)
            dv_ref[rows, :] = dv.astype(BF16)
            acc = dp if acc is None else [p + q for p, q in zip(acc, dp)]

        @pl.when(pl.program_id(0) == 0)
        def _():
            for r, p in zip((dg_ref, db_ref, dw_ref, dbias_ref), acc):
                r[...] = p

        @pl.when(pl.program_id(0) > 0)
        def _():
            for r, p in zip((dg_ref, db_ref, dw_ref, dbias_ref), acc):
                r[...] += p

    row = pl.BlockSpec((rb, BW), lambda i: (i, 0))
    return pl.pallas_call(
        body, name=name, grid=(T // rb,),
        in_specs=[pl.BlockSpec((rb, BW), lambda i: (i, COL_AU // BW)), pl.BlockSpec((rb, BW), lambda i: (i, COL_AV // BW)),
                  row] + _sgu_param_specs(),
        out_specs=[row, row] + _sgu_param_specs(),
        out_shape=[jax.ShapeDtypeStruct((T, BW), BF16), jax.ShapeDtypeStruct((T, BW), BF16),
                   jax.ShapeDtypeStruct((1, BW), F32), jax.ShapeDtypeStruct((1, BW), F32),
                   jax.ShapeDtypeStruct((4, 128, 128), F32), jax.ShapeDtypeStruct((128, BW), F32)],
        compiler_params=_cp("arbitrary"),
    )(proj, proj, dy, ln_g, ln_b, w, bias)


def _halo_block(ref, i, rblk, halo):
    r0 = pl.multiple_of(i * rblk, rblk)
    h0 = pl.multiple_of(jnp.maximum(r0 - halo, 0), halo)
    top = jnp.where(i > 0, ref[pl.ds(h0, halo), :], 0.0)
    return jnp.concatenate([top, ref[pl.ds(r0, rblk), :]], axis=0)


def _with_halo_grad(dfull, pending, halo, rblk):
    tail = jnp.concatenate([jnp.zeros((rblk - halo, 128), F32), pending], axis=0)
    return dfull[halo:] + tail


def _conv4(xfull, rows):
    acc = None
    for k in range(4):
        t = rows[k] * _shift(xfull, 3 - k)[8:]
        acc = t if acc is None else acc + t
    return acc


def _lru_block(xfull, gate, h0, c0, c1, c2, c3, cb, wa, ba, wx, bx, lam):
    n = gate.shape[0]
    xc = _conv4(xfull, (c0, c1, c2, c3)) + cb
    r = _sigmoid(_dot(xc, wa) + ba)
    ig = _sigmoid(_dot(xc, wx) + bx)
    log_a = -LRU_C * r * _softplus(-lam)
    a = jnp.exp(log_a)
    mult = jnp.sqrt(-_expm1(2.0 * log_a))
    b = mult * (ig * xc)
    row = lax.broadcasted_iota(jnp.int32, (n, 128), 0)
    b = b + jnp.where(row == 0, a * h0, 0.0)
    h = _scan(a, b)
    out = h * _gelu(gate)
    h_last = jnp.sum(jnp.where(row == n - 1, h, 0.0), axis=0, keepdims=True)
    return out, h_last


def _lru_param_specs():
    vec = pl.BlockSpec((1, 128), lambda g: (0, g))
    mat = pl.BlockSpec((None, 128, 128), lambda g: (g, 0, 0))
    return [pl.BlockSpec((4, 128), lambda g: (0, g)), vec, mat, vec, mat, vec, vec]


def _lru_load_params(cw_ref, cb_ref, wa_ref, ba_ref, wx_ref, bx_ref, lam_ref):
    return (cw_ref[0:1, :], cw_ref[1:2, :], cw_ref[2:3, :], cw_ref[3:4, :], cb_ref[...], wa_ref[...], ba_ref[...],
            wx_ref[...], bx_ref[...], lam_ref[...])


def _lru_fwd(name, proj, cw, cb, wa, ba, wx, bx, lam):
    T = proj.shape[0]
    rblk = _pick(T, (256, 128))
    nblk = T // rblk

    def body(x_ref, gt_ref, cw_ref, cb_ref, wa_ref, ba_ref, wx_ref, bx_ref, lam_ref, y_ref, hc_ref):
        params = _lru_load_params(cw_ref, cb_ref, wa_ref, ba_ref, wx_ref, bx_ref, lam_ref)

        def step(i, h0):
            r0 = pl.multiple_of(i * rblk, rblk)
            out, h_last = _lru_block(_halo_block(x_ref, i, rblk, 8), gt_ref[pl.ds(r0, rblk), :], h0, *params)
            y_ref[pl.ds(r0, rblk), :] = out.astype(BF16)
            hc_ref[pl.ds(pl.multiple_of(i * 8, 8), 8), :] = jnp.broadcast_to(h0, (8, 128))
            return h_last

        lax.fori_loop(0, nblk, step, jnp.zeros((1, 128), F32))

    return pl.pallas_call(
        body, name=name, grid=(4,),
        in_specs=[pl.BlockSpec((T, 128), lambda g: (0, COL_BX // 128 + g)),
                  pl.BlockSpec((T, 128), lambda g: (0, COL_BG // 128 + g))] + _lru_param_specs(),
        out_specs=[pl.BlockSpec((T, 128), lambda g: (0, g)), pl.BlockSpec((nblk * 8, 128), lambda g: (0, g))],
        out_shape=[jax.ShapeDtypeStruct((T, BW), BF16), jax.ShapeDtypeStruct((nblk * 8, BW), F32)],
        compiler_params=_cp("parallel"),
    )(proj, proj, cw, cb, wa, ba, wx, bx, lam)


def _lru_bwd(name, proj, dy, hc, cw, cb, wa, ba, wx, bx, lam):
    T = proj.shape[0]
    rblk = _pick(T, (256, 128))
    nblk = T // rblk

    def body(x_ref, gt_ref, dy_ref, hc_ref, cw_ref, cb_ref, wa_ref, ba_ref, wx_ref, bx_ref, lam_ref,
             dx_ref, dgt_ref, dcw_ref, dcb_ref, dwa_ref, dba_ref, dwx_ref, dbx_ref, dlam_ref):
        params = _lru_load_params(cw_ref, cb_ref, wa_ref, ba_ref, wx_ref, bx_ref, lam_ref)

        def step(it, carry):
            dh_last, pending, acc = carry
            i = nblk - 1 - it
            r0 = pl.multiple_of(i * rblk, rblk)
            h0 = hc_ref[pl.ds(pl.multiple_of(i * 8, 8), 1), :]
            _, vjp = jax.vjp(_lru_block, _halo_block(x_ref, i, rblk, 8), gt_ref[pl.ds(r0, rblk), :], h0, *params)
            dfull, dgate, dh0, *dp = vjp((dy_ref[pl.ds(r0, rblk), :], dh_last))
            dx_ref[pl.ds(r0, rblk), :] = _with_halo_grad(dfull, pending, 8, rblk).astype(BF16)
            dgt_ref[pl.ds(r0, rblk), :] = dgate.astype(BF16)
            return dh0, dfull[:8], tuple(p + q for p, q in zip(acc, dp))

        zeros = tuple(jnp.zeros(p.shape, F32) for p in params)
        _, _, acc = lax.fori_loop(0, nblk, step, (jnp.zeros((1, 128), F32), jnp.zeros((8, 128), F32), zeros))
        for k in range(4):
            dcw_ref[k:k + 1, :] = acc[k]
        for r, p in zip((dcb_ref, dwa_ref, dba_ref, dwx_ref, dbx_ref, dlam_ref), acc[4:]):
            r[...] = p

    col = pl.BlockSpec((T, 128), lambda g: (0, g))
    return pl.pallas_call(
        body, name=name, grid=(4,),
        in_specs=[pl.BlockSpec((T, 128), lambda g: (0, COL_BX // 128 + g)),
                  pl.BlockSpec((T, 128), lambda g: (0, COL_BG // 128 + g)), col,
                  pl.BlockSpec((nblk * 8, 128), lambda g: (0, g))] + _lru_param_specs(),
        out_specs=[col, col] + _lru_param_specs(),
        out_shape=[jax.ShapeDtypeStruct((T, BW), BF16), jax.ShapeDtypeStruct((T, BW), BF16),
                   jax.ShapeDtypeStruct((4, BW), F32), jax.ShapeDtypeStruct((1, BW), F32),
                   jax.ShapeDtypeStruct((4, 128, 128), F32), jax.ShapeDtypeStruct((1, BW), F32),
                   jax.ShapeDtypeStruct((4, 128, 128), F32), jax.ShapeDtypeStruct((1, BW), F32),
                   jax.ShapeDtypeStruct((1, BW), F32)],
        compiler_params=_cp("parallel"),
    )(proj, proj, dy, hc, cw, cb, wa, ba, wx, bx, lam)


def _conv_block(xfull, c0, c1, c2, c3):
    return _silu(_conv4(xfull, (c0, c1, c2, c3)))


def _conv_fwd(name, proj, col0, cw, cw_col0):
    T = proj.shape[0]
    rblk = _pick(T, (256, 128))
    nblk = T // rblk

    def body(x_ref, cw_ref, y_ref):
        rows = (cw_ref[0:1, :], cw_ref[1:2, :], cw_ref[2:3, :], cw_ref[3:4, :])

        def step(i, c):
            r0 = pl.multiple_of(i * rblk, rblk)
            y_ref[pl.ds(r0, rblk), :] = _conv_block(_halo_block(x_ref, i, rblk, 8), *rows)
            return c

        lax.fori_loop(0, nblk, step, 0)

    return pl.pallas_call(
        body, name=name, grid=(4,),
        in_specs=[pl.BlockSpec((T, 128), lambda g: (0, col0 // 128 + g)),
                  pl.BlockSpec((4, 128), lambda g: (0, cw_col0 // 128 + g))],
        out_specs=pl.BlockSpec((T, 128), lambda g: (0, g)),
        out_shape=jax.ShapeDtypeStruct((T, BW), F32),
        compiler_params=_cp("parallel"),
    )(proj, cw)


def _conv_bwd(name, proj, col0, dy, cw, cw_col0):
    T = proj.shape[0]
    rblk = _pick(T, (256, 128))
    nblk = T // rblk

    def body(x_ref, dy_ref, cw_ref, dx_ref, dcw_ref):
        rows = (cw_ref[0:1, :], cw_ref[1:2, :], cw_ref[2:3, :], cw_ref[3:4, :])

        def step(it, carry):
            pending, acc = carry
            i = nblk - 1 - it
            r0 = pl.multiple_of(i * rblk, rblk)
            _, vjp = jax.vjp(_conv_block, _halo_block(x_ref, i, rblk, 8), *rows)
            dfull, *dp = vjp(dy_ref[pl.ds(r0, rblk), :])
            dx_ref[pl.ds(r0, rblk), :] = _with_halo_grad(dfull, pending, 8, rblk).astype(BF16)
            return dfull[:8], tuple(p + q for p, q in zip(acc, dp))

        zeros = tuple(jnp.zeros((1, 128), F32) for _ in range(4))
        _, acc = lax.fori_loop(0, nblk, step, (jnp.zeros((8, 128), F32), zeros))
        for k in range(4):
            dcw_ref[k:k + 1, :] = acc[k]

    col = pl.BlockSpec((T, 128), lambda g: (0, g))
    return pl.pallas_call(
        body, name=name, grid=(4,),
        in_specs=[pl.BlockSpec((T, 128), lambda g: (0, col0 // 128 + g)), col,
                  pl.BlockSpec((4, 128), lambda g: (0, cw_col0 // 128 + g))],
        out_specs=[col, pl.BlockSpec((4, 128), lambda g: (0, g))],
        out_shape=[jax.ShapeDtypeStruct((T, BW), BF16), jax.ShapeDtypeStruct((4, BW), F32)],
        compiler_params=_cp("parallel"),
    )(proj, dy, cw)


def _pool_block(xfull, pw, sc, t0, gi):
    n = xfull.shape[0] - 16
    s2 = xfull + _shift(xfull, 1)
    s4 = s2 + _shift(s2, 2)
    s8 = s4 + _shift(s4, 4)
    s16 = s8 + _shift(s8, 8)
    s = jnp.where(gi == 0, s2, jnp.where(gi == 1, s4, jnp.where(gi == 2, s8, s16)))[16:]
    t = t0 + lax.broadcasted_iota(jnp.int32, (n, 128), 0)
    cnt = jnp.minimum(t + 1, lax.shift_left(jnp.int32(2), gi)).astype(F32)
    pooled = s / cnt - xfull[16:]
    return _dot(pooled, pw) * sc


def _pool_fwd(name, proj, pw, sc):
    T = proj.shape[0]
    rblk = _pick(T, (256, 128))
    nblk = T // rblk

    def body(x_ref, pw_ref, sc_ref, y_ref):
        gi = pl.program_id(0)

        def step(i, c):
            r0 = pl.multiple_of(i * rblk, rblk)
            y = _pool_block(_halo_block(x_ref, i, rblk, 16), pw_ref[...], sc_ref[...], r0, gi)
            y_ref[pl.ds(r0, rblk), :] = y.astype(BF16)
            return c

        lax.fori_loop(0, nblk, step, 0)

    return pl.pallas_call(
        body, name=name, grid=(4,),
        in_specs=[pl.BlockSpec((T, 128), lambda g: (0, COL_DX // 128 + g)),
                  pl.BlockSpec((None, 128, 128), lambda g: (g, 0, 0)), pl.BlockSpec((1, 128), lambda g: (0, g))],
        out_specs=pl.BlockSpec((T, 128), lambda g: (0, g)),
        out_shape=jax.ShapeDtypeStruct((T, BW), BF16),
        compiler_params=_cp("parallel"),
    )(proj, pw, sc)


def _pool_bwd(name, proj, dy, pw, sc):
    T = proj.shape[0]
    rblk = _pick(T, (256, 128))
    nblk = T // rblk

    def body(x_ref, dy_ref, pw_ref, sc_ref, dx_ref, dpw_ref, dsc_ref):
        gi = pl.program_id(0)

        def step(it, carry):
            pending, apw, asc = carry
            i = nblk - 1 - it
            r0 = pl.multiple_of(i * rblk, rblk)
            _, vjp = jax.vjp(lambda xf, w, s: _pool_block(xf, w, s, r0, gi), _halo_block(x_ref, i, rblk, 16),
                             pw_ref[...], sc_ref[...])
            dfull, dw, ds = vjp(dy_ref[pl.ds(r0, rblk), :])
            dx_ref[pl.ds(r0, rblk), :] = _with_halo_grad(dfull, pending, 16, rblk).astype(BF16)
            return dfull[:16], apw + dw, asc + ds

        _, apw, asc = lax.fori_loop(0, nblk, step, (jnp.zeros((16, 128), F32), jnp.zeros((128, 128), F32),
                                                    jnp.zeros((1, 128), F32)))
        dpw_ref[...] = apw
        dsc_ref[...] = asc

    col = pl.BlockSpec((T, 128), lambda g: (0, g))
    mat = pl.BlockSpec((None, 128, 128), lambda g: (g, 0, 0))
    vec = pl.BlockSpec((1, 128), lambda g: (0, g))
    return pl.pallas_call(
        body, name=name, grid=(4,),
        in_specs=[pl.BlockSpec((T, 128), lambda g: (0, COL_DX // 128 + g)), col, mat, vec],
        out_specs=[col, mat, vec],
        out_shape=[jax.ShapeDtypeStruct((T, BW), BF16), jax.ShapeDtypeStruct((4, 128, 128), F32),
                   jax.ShapeDtypeStruct((1, BW), F32)],
        compiler_params=_cp("parallel"),
    )(proj, dy, pw, sc)


def _tri_inv(a):
    n = a.shape[0]
    eye = (lax.broadcasted_iota(jnp.int32, (n, n), 0) == lax.broadcasted_iota(jnp.int32, (n, n), 1)).astype(F32)
    p = eye - a
    m = a
    k = 2
    while k < n:
        m = _dot_hi(m, m)
        p = p + _dot_hi(p, m)
        k *= 2
    return p


def _gdn_chunk(S, qc, kc, vc, z, tail, alog, dtb, ng, h):
    C = CHUNK
    lane = lax.broadcasted_iota(jnp.int32, (C, 128), 1)
    row = lax.broadcasted_iota(jnp.int32, (C, 128), 0)
    beta = jnp.sum(jnp.where(lane == h, _sigmoid(tail), 0.0), axis=1, keepdims=True)
    gfull = -jnp.exp(alog) * _softplus(tail + dtb)
    g = jnp.sum(jnp.where(lane == h + 4, gfull, 0.0), axis=1, keepdims=True)
    q = qc * lax.rsqrt(jnp.sum(qc * qc, axis=-1, keepdims=True) + EPS) * (GDN_DK ** -0.5)
    k = kc * lax.rsqrt(jnp.sum(kc * kc, axis=-1, keepdims=True) + EPS)
    ri = lax.broadcasted_iota(jnp.int32, (C, C), 0)
    ci = lax.broadcasted_iota(jnp.int32, (C, C), 1)
    incl = ri >= ci
    ltri = incl.astype(F32)
    g64 = jnp.broadcast_to(g, (C, C))
    gc_col = _dot_hi(ltri, g64)
    gc_row = _dot_hi(g64, (ri <= ci).astype(F32), _TN)
    gc = _dot_hi(ltri, jnp.broadcast_to(g, (C, 128)))
    decay = jnp.exp(jnp.where(incl, gc_col - gc_row, -1e30))
    kb = k * beta
    a_mat = jnp.where(ri > ci, _dot(kb, k, _NT) * decay, 0.0)
    t_mat = _tri_inv(a_mat)
    egc = jnp.exp(gc)
    u = _dot(t_mat, vc * beta)
    w = _dot(t_mat, kb * egc)
    attn = jnp.where(incl, _dot(q, k, _NT) * decay, 0.0)
    v_new = u - _dot(w, S)
    o = _dot(q * egc, S) + _dot(attn, v_new)
    g_last = jnp.sum(jnp.where(row == C - 1, gc, 0.0), axis=0, keepdims=True)
    S_next = S * jnp.exp(g_last) + _dot(k * jnp.exp(g_last - gc), v_new, _TN)
    on = o * lax.rsqrt(jnp.mean(o * o, axis=-1, keepdims=True) + EPS) * ng
    return S_next, on * _silu(z)


def _gdn_blocks(T):
    tb = _pick(T, (512, 256, 128, 64))
    return tb, T // tb, tb // CHUNK


def _gdn_fwd(name, qa, ka, va, proj, alog, dtb, ng):
    T = proj.shape[0]
    tb, nb, ncb = _gdn_blocks(T)

    def body(q_ref, k_ref, v_ref, z_ref, tail_ref, alog_ref, dtb_ref, ng_ref, y_ref, sh_ref, state):
        h = pl.program_id(1)

        @pl.when(pl.program_id(0) == 0)
        def _():
            state[h] = jnp.zeros((128, 128), F32)

        def step(c, S):
            r0 = pl.multiple_of(c * CHUNK, CHUNK)
            rows = pl.ds(r0, CHUNK)
            sh_ref[c] = S
            S_next, y = _gdn_chunk(S, q_ref[rows, :], k_ref[rows, :], v_ref[rows, :], z_ref[rows, :], tail_ref[rows, :],
                                   alog_ref[...], dtb_ref[...], ng_ref[...], h)
            y_ref[rows, :] = y.astype(BF16)
            return S_next

        state[h] = lax.fori_loop(0, ncb, step, state[h])

    blk = pl.BlockSpec((tb, 128), lambda j, h: (j, h))
    vec = pl.BlockSpec((1, 128), lambda j, h: (0, 0))
    return pl.pallas_call(
        body, name=name, grid=(nb, 4),
        in_specs=[blk, blk, blk, pl.BlockSpec((tb, 128), lambda j, h: (j, COL_CZ // 128 + h)),
                  pl.BlockSpec((tb, 128), lambda j, h: (j, COL_TAIL // 128)), vec, vec, vec],
        out_specs=[blk, pl.BlockSpec((None, ncb, 128, 128), lambda j, h: (h, j, 0, 0))],
        out_shape=[jax.ShapeDtypeStruct((T, BW), BF16), jax.ShapeDtypeStruct((4, T // CHUNK, 128, 128), F32)],
        scratch_shapes=[pltpu.VMEM((4, 128, 128), F32)],
        compiler_params=_cp("arbitrary", "arbitrary"),
    )(qa, ka, va, proj, proj, alog, dtb, ng)


def _gdn_bwd(name, qa, ka, va, proj, dy, sh, alog, dtb, ng):
    T = proj.shape[0]
    tb, nb, ncb = _gdn_blocks(T)

    def body(q_ref, k_ref, v_ref, z_ref, tail_ref, dy_ref, sh_ref, alog_ref, dtb_ref, ng_ref,
             dq_ref, dk_ref, dv_ref, dz_ref, dtail_ref, dalog_ref, ddtb_ref, dng_ref, dstate):
        h = pl.program_id(1)
        first = (pl.program_id(0) == 0) & (h == 0)

        @pl.when(pl.program_id(0) == 0)
        def _():
            dstate[h] = jnp.zeros((128, 128), F32)

        def step(it, carry):
            dS, pa, pd, pn = carry
            c = ncb - 1 - it
            rows = pl.ds(pl.multiple_of(c * CHUNK, CHUNK), CHUNK)
            _, vjp = jax.vjp(functools.partial(_gdn_chunk, h=h), sh_ref[c], q_ref[rows, :], k_ref[rows, :],
                             v_ref[rows, :], z_ref[rows, :], tail_ref[rows, :], alog_ref[...], dtb_ref[...],
                             ng_ref[...])
            dS_prev, dq, dk, dv, dz, dtl, da, dd, dn = vjp((dS, dy_ref[rows, :]))
            dq_ref[rows, :] = dq
            dk_ref[rows, :] = dk
            dv_ref[rows, :] = dv
            dz_ref[rows, :] = dz.astype(BF16)

            @pl.when(h == 0)
            def _():
                dtail_ref[rows, :] = dtl.astype(BF16)

            @pl.when(h > 0)
            def _():
                dtail_ref[rows, :] = (dtail_ref[rows, :].astype(F32) + dtl).astype(BF16)

            return dS_prev, pa + da, pd + dd, pn + dn

        zv = jnp.zeros((1, 128), F32)
        dS, pa, pd, pn = lax.fori_loop(0, ncb, step, (dstate[h], zv, zv, zv))
        dstate[h] = dS

        @pl.when(first)
        def _():
            dalog_ref[...] = pa
            ddtb_ref[...] = pd
            dng_ref[...] = pn

        @pl.when(jnp.logical_not(first))
        def _():
            dalog_ref[...] += pa
            ddtb_ref[...] += pd
            dng_ref[...] += pn

    blk = pl.BlockSpec((tb, 128), lambda j, h: (nb - 1 - j, h))
    vec = pl.BlockSpec((1, 128), lambda j, h: (0, 0))
    return pl.pallas_call(
        body, name=name, grid=(nb, 4),
        in_specs=[blk, blk, blk, pl.BlockSpec((tb, 128), lambda j, h: (nb - 1 - j, COL_CZ // 128 + h)),
                  pl.BlockSpec((tb, 128), lambda j, h: (nb - 1 - j, COL_TAIL // 128)), blk,
                  pl.BlockSpec((None, ncb, 128, 128), lambda j, h: (h, nb - 1 - j, 0, 0)), vec, vec, vec],
        out_specs=[blk, blk, blk, blk, pl.BlockSpec((tb, 128), lambda j, h: (nb - 1 - j, 0)), vec, vec, vec],
        out_shape=[jax.ShapeDtypeStruct((T, BW), F32)] * 3
        + [jax.ShapeDtypeStruct((T, BW), BF16), jax.ShapeDtypeStruct((T, 128), BF16)]
        + [jax.ShapeDtypeStruct((1, 128), F32)] * 3,
        scratch_shapes=[pltpu.VMEM((4, 128, 128), F32)],
        compiler_params=_cp("arbitrary", "arbitrary"),
    )(qa, ka, va, proj, proj, dy, sh, alog, dtb, ng)


def _adamw(name, w, g, m, v):
    R, C = w.shape
    br = _pick(R, (512, 256, 240, 128, 64, 8))

    def body(w_ref, g_ref, m_ref, v_ref, d_ref, nm_ref, nv_ref):
        gv = g_ref[...]
        m2 = ADAM_B1 * m_ref[...] + (1.0 - ADAM_B1) * gv
        v2 = ADAM_B2 * v_ref[...] + (1.0 - ADAM_B2) * jnp.square(gv)
        m_hat = m2 / (1.0 - ADAM_B1 ** ADAM_STEP)
        v_hat = v2 / (1.0 - ADAM_B2 ** ADAM_STEP)
        d_ref[...] = -ADAM_LR * (m_hat / (jnp.sqrt(v_hat) + ADAM_EPS) + ADAM_WD * w_ref[...])
        nm_ref[...] = m2
        nv_ref[...] = v2

    spec = pl.BlockSpec((br, C), lambda i: (i, 0))
    return pl.pallas_call(
        body, name=name, grid=(R // br,),
        in_specs=[spec] * 4, out_specs=[spec] * 3,
        out_shape=[jax.ShapeDtypeStruct((R, C), F32)] * 3,
        compiler_params=_cp("parallel"),
    )(w, g, m, v)


def _sum8(name, parts):
    _, R, C = parts.shape
    br = _pick(R, (240, 256, 128, 64, 16, 8))

    def body(p_ref, o_ref):
        acc = p_ref[0].astype(F32)
        for d in range(1, N_DEV):
            acc = acc + p_ref[d].astype(F32)
        o_ref[...] = acc

    return pl.pallas_call(
        body, name=name, grid=(R // br,),
        in_specs=[pl.BlockSpec((N_DEV, br, C), lambda i: (0, i, 0))],
        out_specs=pl.BlockSpec((br, C), lambda i: (i, 0)),
        out_shape=jax.ShapeDtypeStruct((R, C), F32),
        compiler_params=_cp("parallel"),
    )(parts)


_ANY = pl.BlockSpec(memory_space=pl.ANY)
_MESH = pl.DeviceIdType.MESH


def _all_gather(name, shard):
    R, C = shard.shape

    def body(x_ref, out_ref, send_sems, recv_sems, local_sem):
        x, y, c = lax.axis_index("x"), lax.axis_index("y"), lax.axis_index("c")
        me, sibling = (x, y, c), (x, y, 1 - c)
        chips = [(1 - x, y), (x, 1 - y), (1 - x, 1 - y)]

        def slot(px, py, pc):
            return out_ref.at[4 * px + 2 * py + pc]

        def copy(k, block, to, src=None):
            return pltpu.make_async_remote_copy(
                src_ref=slot(*block) if src is None else src, dst_ref=slot(*block),
                send_sem=send_sems.at[k], recv_sem=recv_sems.at[k], device_id=to, device_id_type=_MESH)

        mine = pltpu.make_async_copy(x_ref, slot(*me), local_sem)
        mine.start()
        first = [copy(0, me, sibling, src=x_ref)]
        first += [copy(1 + j, me, (*chip, c), src=x_ref) for j, chip in enumerate(chips)]
        for cp in first:
            cp.start()
        passed = [copy(4 + j, (*chip, c), sibling) for j, chip in enumerate(chips)]
        for j, chip in enumerate(chips):
            copy(1 + j, (*chip, c), me).wait_recv()
            passed[j].start()
        copy(0, sibling, me).wait_recv()
        for j, chip in enumerate(chips):
            copy(4 + j, (*chip, 1 - c), me).wait_recv()
        for cp in first + passed:
            cp.wait_send()
        mine.wait()

    return pl.pallas_call(
        body, name=name,
        in_specs=[_ANY], out_specs=_ANY,
        out_shape=jax.ShapeDtypeStruct((N_DEV, R, C), shard.dtype),
        scratch_shapes=[pltpu.SemaphoreType.DMA((7,)), pltpu.SemaphoreType.DMA((7,)), pltpu.SemaphoreType.DMA],
    )(shard)


def _scatter_direct(name, parts):
    _, R, C = parts.shape

    def body(g_ref, land_ref, send_sems, recv_sems, local_sem):
        x, y, c = lax.axis_index("x"), lax.axis_index("y"), lax.axis_index("c")
        me = 4 * x + 2 * y + c
        mine = pltpu.make_async_copy(g_ref.at[me], land_ref.at[me], local_sem)
        mine.start()
        copies = []
        for k in range(1, N_DEV):
            px, py, pc = x ^ ((k >> 2) & 1), y ^ ((k >> 1) & 1), c ^ (k & 1)
            p = 4 * px + 2 * py + pc
            cp = pltpu.make_async_remote_copy(
                src_ref=g_ref.at[p], dst_ref=land_ref.at[me], send_sem=send_sems.at[k - 1],
                recv_sem=recv_sems.at[k - 1], device_id=(px, py, pc), device_id_type=_MESH)
            cp.start()
            copies.append(cp)
        for cp in copies:
            cp.wait()
        mine.wait()

    return pl.pallas_call(
        body, name=name,
        in_specs=[_ANY], out_specs=_ANY,
        out_shape=jax.ShapeDtypeStruct((N_DEV, R, C), parts.dtype),
        scratch_shapes=[pltpu.SemaphoreType.DMA((7,)), pltpu.SemaphoreType.DMA((7,)), pltpu.SemaphoreType.DMA],
    )(parts)


def _rows(a):
    return a.reshape(-1, 1024)


def _cols_to_parts(full):
    n = full.shape[-1] // N_DEV
    t = full.reshape(full.shape[:-1] + (N_DEV, n))
    return jnp.moveaxis(t, -2, 0)


def _parts_to_cols(parts):
    t = jnp.moveaxis(parts, 0, -2)
    return t.reshape(t.shape[:-2] + (t.shape[-2] * t.shape[-1],))


def _join(parts, axis=0):
    total = sum(p.shape[axis] for p in parts)
    out, off = None, 0
    for p in parts:
        cfg = [(0, 0)] * p.ndim
        cfg[axis] = (off, total - off - p.shape[axis])
        t = jnp.pad(p, cfg)
        out = t if out is None else out + t
        off += p.shape[axis]
    return out


def _w_in_to_layout(w):
    tail = jnp.pad(w[:, 4096:4104], ((0, 0), (0, PW - COL_TAIL - 8)))
    return jnp.concatenate([w[:, :4096], w[:, 4104:P_IN], tail], axis=1)


def _w_in_from_layout(g):
    return _join([g[:, :4096], g[:, COL_TAIL:COL_TAIL + 8], g[:, 4096:COL_TAIL]], axis=1)


def _block_diag(w):
    w = w.reshape(4, 2, 64, 64)
    return jnp.pad(w[:, 0], ((0, 0), (0, 64), (0, 64))) + jnp.pad(w[:, 1], ((0, 0), (64, 0), (64, 0)))


def _block_diag_grad(g):
    return jnp.stack([g[:, :64, :64], g[:, 64:, 64:]], axis=1).reshape(8, 64, 64)


def _ffn_forward(tag, x, norm, wg, wu, wd):
    h = _rms_fwd(tag + "_norm", x, norm)
    a, b, act = _ffn_up(tag + "_up", h, wg, wu)
    x_out = _mm(tag + "_down", [(act, wd)], "nn", F32, res=x, scale=0.5)
    return x_out, (x, h, a, b, act)


def _ffn_backward(tag, dx_out, saved, norm, wg, wu, wd):
    x, h, a, b, act = saved
    da, db = _ffn_dact(tag + "_dact", dx_out, wd, a, b)
    dwd = _mm(tag + "_dwd", [(act, dx_out)], "tn", F32, scale=0.5, bm=256)
    dwg = _mm(tag + "_dwg", [(h, da)], "tn", F32, bn=256)
    dwu = _mm(tag + "_dwu", [(h, db)], "tn", F32, bn=256)
    dh = _mm(tag + "_dh", [(da, wg), (db, wu)], "nt", F32)
    dx, dnorm = _rms_bwd(tag + "_dnorm", x, norm, dh, dx_out)
    return dx, dnorm, dwg, dwu, dwd


def _mixer_params(p):
    alog = jnp.pad(p["gdn_a_log"], (4, 120))[None]
    dtb = jnp.pad(p["gdn_dt_bias"], (4, 120))[None]
    bias = jnp.repeat(p["sgu_b"].T, 128, axis=1)
    return dict(
        ln_g=p["sgu_ln_g"][None], ln_b=p["sgu_ln_b"][None], sgu_w=p["sgu_w"], sgu_bias=bias,
        lru_cw=p["lru_conv_w"], lru_cb=p["lru_conv_b"][None], wa=_block_diag(p["lru_wa"]), ba=p["lru_ba"][None],
        wx=_block_diag(p["lru_wx"]), bx=p["lru_bx"][None], lam=p["lru_lambda"][None],
        gdn_cw=p["gdn_conv_w"], alog=alog, dtb=dtb, ng=p["gdn_norm_g"][None],
        pool_w=p["pool_w"], pool_sc=p["pool_scale"][None])


def _mix_forward(tag, x, p, mp):
    h = _rms_fwd(tag + "_norm", x, p["mix_norm"][None])
    proj = _mm(tag + "_proj", [(h, p["w_in"])], "nn", F32)
    y_a = _sgu_fwd(tag + "_sgu", proj, mp["ln_g"], mp["ln_b"], mp["sgu_w"], mp["sgu_bias"])
    y_b, hc = _lru_fwd(tag + "_lru", proj, mp["lru_cw"], mp["lru_cb"], mp["wa"], mp["ba"], mp["wx"], mp["bx"],
                       mp["lam"])
    qa = _conv_fwd(tag + "_convq", proj, COL_CQ, mp["gdn_cw"], 0)
    ka = _conv_fwd(tag + "_convk", proj, COL_CK, mp["gdn_cw"], 512)
    va = _conv_fwd(tag + "_convv", proj, COL_CV, mp["gdn_cw"], 1024)
    y_c, sh = _gdn_fwd(tag + "_gdn", qa, ka, va, proj, mp["alog"], mp["dtb"], mp["ng"])
    y_d = _pool_fwd(tag + "_pool", proj, mp["pool_w"], mp["pool_sc"])
    ys = (y_a, y_b, y_c, y_d)
    merged = _merge_fwd(tag + "_merge", ys, p["w_branch"], proj)
    x_out = _mm(tag + "_out", [(merged, p["w_out"])], "nn", F32, res=x)
    return x_out, (x, h, proj, hc, qa, ka, va, sh, ys, merged)


def _mix_backward(tag, dx_out, saved, p, mp):
    x, h, proj, hc, qa, ka, va, sh, ys, merged = saved
    T = x.shape[0]
    g = {}
    dmerged = _mm(tag + "_dmerged", [(dx_out, p["w_out"])], "nt", F32)
    g["w_out"] = _mm(tag + "_dwout", [(merged, dx_out)], "tn", F32)
    outs = _merge_bwd(tag + "_dmerge", dmerged, ys, p["w_branch"], proj)
    dgates, dbrs = outs[:NBR], outs[NBR:]
    dys = [_mm(f"{tag}_dy{i}", [(dbrs[i], p["w_branch"][i])], "nt", F32) for i in range(NBR)]
    g["w_branch"] = jnp.stack([_mm(f"{tag}_dwb{i}", [(ys[i], dbrs[i])], "tn", F32) for i in range(NBR)])

    du, dv, dln_g, dln_b, dsgu_w, dbias = _sgu_bwd(tag + "_dsgu", proj, dys[0], mp["ln_g"], mp["ln_b"], mp["sgu_w"],
                                                  mp["sgu_bias"])
    g["sgu_ln_g"], g["sgu_ln_b"], g["sgu_w"] = dln_g[0], dln_b[0], dsgu_w
    g["sgu_b"] = dbias.reshape(128, 4, 128).sum(axis=2).T

    (dbx, dbg, dcw, dcb, dwa, dba, dwx, dbxb, dlam) = _lru_bwd(
        tag + "_dlru", proj, dys[1], hc, mp["lru_cw"], mp["lru_cb"], mp["wa"], mp["ba"], mp["wx"], mp["bx"], mp["lam"])
    g["lru_conv_w"], g["lru_conv_b"], g["lru_ba"], g["lru_bx"], g["lru_lambda"] = dcw, dcb[0], dba[0], dbxb[0], dlam[0]
    g["lru_wa"], g["lru_wx"] = _block_diag_grad(dwa), _block_diag_grad(dwx)

    dqa, dka, dva, dz, dtail, dalog, ddtb, dng = _gdn_bwd(tag + "_dgdn", qa, ka, va, proj, dys[2], sh, mp["alog"],
                                                         mp["dtb"], mp["ng"])
    g["gdn_a_log"], g["gdn_dt_bias"], g["gdn_norm_g"] = dalog[0, 4:8], ddtb[0, 4:8], dng[0]
    dq, dcwq = _conv_bwd(tag + "_dconvq", proj, COL_CQ, dqa, mp["gdn_cw"], 0)
    dk, dcwk = _conv_bwd(tag + "_dconvk", proj, COL_CK, dka, mp["gdn_cw"], 512)
    dv_, dcwv = _conv_bwd(tag + "_dconvv", proj, COL_CV, dva, mp["gdn_cw"], 1024)
    g["gdn_conv_w"] = jnp.concatenate([dcwq, dcwk, dcwv], axis=1)

    dd, dpw, dsc = _pool_bwd(tag + "_dpool", proj, dys[3], mp["pool_w"], mp["pool_sc"])
    g["pool_w"], g["pool_scale"] = dpw, dsc[0]

    dproj = jnp.concatenate([du, dv, dbx, dbg, dq, dk, dv_, dz, dd, *dgates, dtail,
                             jnp.zeros((T, PW - COL_TAIL - 128), BF16)], axis=1)
    g["w_in"] = _mm(tag + "_dwin", [(h, dproj)], "tn", F32)
    dh = _mm(tag + "_dh", [(dproj, p["w_in"])], "nt", F32)
    dx, dnorm = _rms_bwd(tag + "_dnorm", x, p["mix_norm"][None], dh, dx_out)
    g["mix_norm"] = dnorm[0]
    return dx, g


_BIG = ("ff1_wg", "ff1_wu", "ff1_wd", "w_in", "w_branch", "w_out", "ff2_wg", "ff2_wu", "ff2_wd")
_COL_SHARDED = ("ff1_wg", "ff1_wu", "w_in", "w_branch", "ff2_wg", "ff2_wu")
_SMALL = ("ff1_norm", "mix_norm", "sgu_ln_g", "sgu_ln_b", "sgu_w", "sgu_b", "lru_conv_w", "lru_conv_b", "lru_wa",
          "lru_ba", "lru_wx", "lru_bx", "lru_lambda", "gdn_conv_w", "gdn_a_log", "gdn_dt_bias", "gdn_norm_g", "pool_w",
          "pool_scale", "ff2_norm", "final_norm")
_WEIGHTS = ("ff1_norm", "ff1_wg", "ff1_wu", "ff1_wd", "mix_norm", "w_in", "sgu_ln_g", "sgu_ln_b", "sgu_w", "sgu_b",
            "lru_conv_w", "lru_conv_b", "lru_wa", "lru_ba", "lru_wx", "lru_bx", "lru_lambda", "gdn_conv_w", "gdn_a_log",
            "gdn_dt_bias", "gdn_norm_g", "pool_w", "pool_scale", "w_branch", "w_out", "ff2_norm", "ff2_wg", "ff2_wu",
            "ff2_wd", "final_norm")
_CONV_SHARDED = ("lru_conv_w", "gdn_conv_w")
PACK_ROW_ALIGN = 48


def _pad_rows(a, mult):
    pad = (-a.shape[-2]) % mult
    if pad == 0:
        return a
    return jnp.pad(a, [(0, 0)] * (a.ndim - 2) + [(0, pad), (0, 0)])


def _gather_weights(w):
    pieces = [_rows(w[n].astype(BF16)) for n in _BIG]
    sizes = [p.shape[0] for p in pieces]
    pieces = [_pad_rows(p, PACK_ROW_ALIGN) for p in pieces]
    got = _all_gather("gather_weights", jnp.concatenate(pieces, axis=0))
    full, r = {}, 0
    for n, sz, pc in zip(_BIG, sizes, pieces):
        parts = got[:, r:r + sz].reshape((N_DEV,) + w[n].shape)
        r += pc.shape[0]
        if n in _COL_SHARDED:
            full[n] = _parts_to_cols(parts)
        else:
            full[n] = jnp.moveaxis(parts, 0, 1).reshape((w[n].shape[0], -1, w[n].shape[2]))
    conv = _pad_rows(jnp.concatenate([w[n].reshape(1, -1) for n in _CONV_SHARDED], axis=1), 8)
    gconv = _all_gather("gather_conv", conv)[:, 0]
    r = 0
    for n in _CONV_SHARDED:
        sz = w[n].size
        full[n] = _parts_to_cols(gconv[:, r:r + sz].reshape((N_DEV,) + w[n].shape))
        r += sz
    return full


def _layer_params(w, full, l):
    p = {n: w[n][l] for n in _SMALL if n != "final_norm"}
    for n in _BIG:
        p[n] = full[n][l]
    for n in _CONV_SHARDED:
        p[n] = full[n][l]
    p["w_in"] = _w_in_to_layout(p["w_in"])
    return p


def _forward_backward(x, tgt, w, full):
    saved, params = [], []
    for l in range(2):
        p = _layer_params(w, full, l)
        mp = _mixer_params(p)
        x, s1 = _ffn_forward(f"l{l}_ff1", x, p["ff1_norm"][None], p["ff1_wg"], p["ff1_wu"], p["ff1_wd"])
        x, s2 = _mix_forward(f"l{l}_mix", x, p, mp)
        x, s3 = _ffn_forward(f"l{l}_ff2", x, p["ff2_norm"][None], p["ff2_wg"], p["ff2_wu"], p["ff2_wd"])
        saved.append((s1, s2, s3))
        params.append((p, mp))
    loss, dx, dfinal = _final_loss("loss_head", x, w["final_norm"][None], tgt)
    grads = [None, None]
    for l in (1, 0):
        p, mp = params[l]
        s1, s2, s3 = saved[l]
        g = {}
        dx, dn, g["ff2_wg"], g["ff2_wu"], g["ff2_wd"] = _ffn_backward(
            f"l{l}_ff2", dx, s3, p["ff2_norm"][None], p["ff2_wg"], p["ff2_wu"], p["ff2_wd"])
        g["ff2_norm"] = dn[0]
        dx, gm = _mix_backward(f"l{l}_mix", dx, s2, p, mp)
        g.update(gm)
        g["w_in"] = _w_in_from_layout(g["w_in"])
        dx, dn, g["ff1_wg"], g["ff1_wu"], g["ff1_wd"] = _ffn_backward(
            f"l{l}_ff1", dx, s1, p["ff1_norm"][None], p["ff1_wg"], p["ff1_wu"], p["ff1_wd"])
        g["ff1_norm"] = dn[0]
        grads[l] = g
    full_g = {}
    for n in grads[0]:
        if n in _BIG:
            full_g[n] = jnp.stack([grads[0][n], grads[1][n]])
        else:
            full_g[n] = _join([grads[0][n].reshape(-1), grads[1][n].reshape(-1)]).reshape((2,) + grads[0][n].shape)
    full_g["final_norm"] = dfinal[0]
    return loss, dx, full_g


def _reduce_big(full_g, w):
    pieces, sizes = [], []
    for n in _BIG:
        g = full_g[n].astype(BF16)
        if n in _COL_SHARDED:
            parts = _cols_to_parts(g)
        else:
            parts = jnp.moveaxis(g.reshape((g.shape[0], N_DEV, -1, g.shape[2])), 1, 0)
        parts = parts.reshape(N_DEV, -1, 1024)
        sizes.append(parts.shape[1])
        pieces.append(_pad_rows(parts, PACK_ROW_ALIGN))
    summed = _sum8("sum_big", _scatter_direct("scatter_grads", jnp.concatenate(pieces, axis=1)))
    out, r = {}, 0
    for n, sz, pc in zip(_BIG, sizes, pieces):
        out[n] = summed[r:r + sz].reshape(w[n].shape)
        r += pc.shape[1]
    return out


SMALL_PIECE = 8 * 1024


def _pack_small(d, names):
    pieces = []
    for n in names:
        flat = d[n].reshape(-1)
        pieces.append(jnp.pad(flat, (0, (-flat.size) % SMALL_PIECE)).reshape(-1, 1024))
    return jnp.concatenate(pieces, axis=0)


def _unpack_small(pack, shapes, names):
    out, r = {}, 0
    for n in names:
        size = 1
        for s in shapes[n]:
            size *= s
        rows = -(-size // SMALL_PIECE) * 8
        out[n] = pack[r:r + rows].reshape(-1)[:size].reshape(shapes[n])
        r += rows
    return out


def _reduce_small(full_g, loss, w):
    d = dict(full_g)
    d["loss"] = loss[0, :1]
    names = _SMALL + ("loss",)
    summed = _sum8("sum_small", _all_gather("gather_small", _pack_small(d, names)))
    out = _unpack_small(summed, {n: d[n].shape for n in names}, names)
    return out, out["loss"][0]


def _as2d(a):
    if a.ndim == 1:
        return a.reshape(1, -1)
    return a.reshape(-1, a.shape[-1])


def kernel(x, ff1_norm, ff1_wg, ff1_wu, ff1_wd, mix_norm, w_in, sgu_ln_g, sgu_ln_b, sgu_w, sgu_b, lru_conv_w, lru_conv_b, lru_wa, lru_ba, lru_wx, lru_bx, lru_lambda, gdn_conv_w, gdn_a_log, gdn_dt_bias, gdn_norm_g, pool_w, pool_scale, w_branch, w_out, ff2_norm, ff2_wg, ff2_wu, ff2_wd, final_norm, loss_target, m_ff1_norm, m_ff1_wg, m_ff1_wu, m_ff1_wd, m_mix_norm, m_w_in, m_sgu_ln_g, m_sgu_ln_b, m_sgu_w, m_sgu_b, m_lru_conv_w, m_lru_conv_b, m_lru_wa, m_lru_ba, m_lru_wx, m_lru_bx, m_lru_lambda, m_gdn_conv_w, m_gdn_a_log, m_gdn_dt_bias, m_gdn_norm_g, m_pool_w, m_pool_scale, m_w_branch, m_w_out, m_ff2_norm, m_ff2_wg, m_ff2_wu, m_ff2_wd, m_final_norm, v_ff1_norm, v_ff1_wg, v_ff1_wu, v_ff1_wd, v_mix_norm, v_w_in, v_sgu_ln_g, v_sgu_ln_b, v_sgu_w, v_sgu_b, v_lru_conv_w, v_lru_conv_b, v_lru_wa, v_lru_ba, v_lru_wx, v_lru_bx, v_lru_lambda, v_gdn_conv_w, v_gdn_a_log, v_gdn_dt_bias, v_gdn_norm_g, v_pool_w, v_pool_scale, v_w_branch, v_w_out, v_ff2_norm, v_ff2_wg, v_ff2_wu, v_ff2_wd, v_final_norm):
    w = dict(ff1_norm=ff1_norm, ff1_wg=ff1_wg, ff1_wu=ff1_wu, ff1_wd=ff1_wd, mix_norm=mix_norm, w_in=w_in,
             sgu_ln_g=sgu_ln_g, sgu_ln_b=sgu_ln_b, sgu_w=sgu_w, sgu_b=sgu_b, lru_conv_w=lru_conv_w,
             lru_conv_b=lru_conv_b, lru_wa=lru_wa, lru_ba=lru_ba, lru_wx=lru_wx, lru_bx=lru_bx, lru_lambda=lru_lambda,
             gdn_conv_w=gdn_conv_w, gdn_a_log=gdn_a_log, gdn_dt_bias=gdn_dt_bias, gdn_norm_g=gdn_norm_g, pool_w=pool_w,
             pool_scale=pool_scale, w_branch=w_branch, w_out=w_out, ff2_norm=ff2_norm, ff2_wg=ff2_wg, ff2_wu=ff2_wu,
             ff2_wd=ff2_wd, final_norm=final_norm)
    m = dict(ff1_norm=m_ff1_norm, ff1_wg=m_ff1_wg, ff1_wu=m_ff1_wu, ff1_wd=m_ff1_wd, mix_norm=m_mix_norm, w_in=m_w_in,
             sgu_ln_g=m_sgu_ln_g, sgu_ln_b=m_sgu_ln_b, sgu_w=m_sgu_w, sgu_b=m_sgu_b, lru_conv_w=m_lru_conv_w,
             lru_conv_b=m_lru_conv_b, lru_wa=m_lru_wa, lru_ba=m_lru_ba, lru_wx=m_lru_wx, lru_bx=m_lru_bx,
             lru_lambda=m_lru_lambda, gdn_conv_w=m_gdn_conv_w, gdn_a_log=m_gdn_a_log, gdn_dt_bias=m_gdn_dt_bias,
             gdn_norm_g=m_gdn_norm_g, pool_w=m_pool_w, pool_scale=m_pool_scale, w_branch=m_w_branch, w_out=m_w_out,
             ff2_norm=m_ff2_norm, ff2_wg=m_ff2_wg, ff2_wu=m_ff2_wu, ff2_wd=m_ff2_wd, final_norm=m_final_norm)
    v = dict(ff1_norm=v_ff1_norm, ff1_wg=v_ff1_wg, ff1_wu=v_ff1_wu, ff1_wd=v_ff1_wd, mix_norm=v_mix_norm, w_in=v_w_in,
             sgu_ln_g=v_sgu_ln_g, sgu_ln_b=v_sgu_ln_b, sgu_w=v_sgu_w, sgu_b=v_sgu_b, lru_conv_w=v_lru_conv_w,
             lru_conv_b=v_lru_conv_b, lru_wa=v_lru_wa, lru_ba=v_lru_ba, lru_wx=v_lru_wx, lru_bx=v_lru_bx,
             lru_lambda=v_lru_lambda, gdn_conv_w=v_gdn_conv_w, gdn_a_log=v_gdn_a_log, gdn_dt_bias=v_gdn_dt_bias,
             gdn_norm_g=v_gdn_norm_g, pool_w=v_pool_w, pool_scale=v_pool_scale, w_branch=v_w_branch, w_out=v_w_out,
             ff2_norm=v_ff2_norm, ff2_wg=v_ff2_wg, ff2_wu=v_ff2_wu, ff2_wd=v_ff2_wd, final_norm=v_final_norm)

    full = _gather_weights(w)
    T = x.shape[1]
    loss_share, dx, full_g = _forward_backward(x.reshape(T, D), loss_target.reshape(T, D), w, full)
    grad = _reduce_big(full_g, w)
    small, loss = _reduce_small(full_g, loss_share, w)
    me = 4 * lax.axis_index("x") + 2 * lax.axis_index("y") + lax.axis_index("c")
    for n in _SMALL:
        if n in _CONV_SHARDED:
            width = w[n].shape[-1]
            grad[n] = lax.dynamic_slice_in_dim(small[n], me * width, width, axis=2)
        else:
            grad[n] = small[n]

    delta, new_m, new_v = {}, {}, {}
    for n in _BIG:
        d_, m_, v_ = _adamw("adamw_" + n, _as2d(w[n]), _as2d(grad[n]), _as2d(m[n]), _as2d(v[n]))
        delta[n], new_m[n], new_v[n] = (t.reshape(w[n].shape) for t in (d_, m_, v_))

    outs = _adamw("adamw_small", *[_pack_small(t, _SMALL) for t in (w, grad, m, v)])
    for dst, packed in zip((delta, new_m, new_v), outs):
        dst.update(_unpack_small(packed, {n: w[n].shape for n in _SMALL}, _SMALL))

    return (loss, dx.reshape(x.shape), *[grad[n] for n in _WEIGHTS], *[delta[n] for n in _WEIGHTS],
            *[new_m[n] for n in _WEIGHTS], *[new_v[n] for n in _WEIGHTS])
```

```python
import functools

import jax
import jax.numpy as jnp
from jax import lax
from jax.experimental import pallas as pl
from jax.experimental.pallas import tpu as pltpu

F32 = jnp.float32
BF16 = jnp.bfloat16
HI = lax.Precision.HIGHEST

N_DEV = 8
D = 1024
FF = 2816
BW = 512
NBR = 4
CHUNK = 64
EPS = 1e-6
LRU_C = 8.0
GDN_DK = 128

COL_AU, COL_AV, COL_BX, COL_BG = 0, 512, 1024, 1536
COL_CQ, COL_CK, COL_CV, COL_CZ = 2048, 2560, 3072, 3584
COL_DX, COL_GATE, COL_TAIL = 4096, 4608, 8704
PW = 9216
P_IN = 8712

ADAM_LR, ADAM_B1, ADAM_B2, ADAM_EPS, ADAM_WD, ADAM_STEP = 0.001, 0.9, 0.999, 1e-08, 0.01, 10

VMEM_LIMIT_V7X = 56 * 1024 * 1024

_NN = (((1,), (0,)), ((), ()))
_NT = (((1,), (1,)), ((), ()))
_TN = (((0,), (0,)), ((), ()))


def _cp(*sem):
    return pltpu.CompilerParams(dimension_semantics=tuple(sem), vmem_limit_bytes=VMEM_LIMIT_V7X)


def _dot(a, b, dims=_NN):
    return lax.dot_general(a.astype(BF16), b.astype(BF16), dims, preferred_element_type=F32)


def _dot_hi(a, b, dims=_NN):
    return lax.dot_general(a, b, dims, precision=HI, preferred_element_type=F32)


def _pick(n, cands):
    for c in cands:
        if n % c == 0:
            return c
    return n


@jax.custom_jvp
def _log1p(x):
    u = 1.0 + x
    return jnp.where(u == 1.0, x, x * jnp.log(u) / jnp.where(u == 1.0, 1.0, u - 1.0))


@_log1p.defjvp
def _log1p_jvp(p, t):
    (x,), (dx,) = p, t
    return _log1p(x), dx / (1.0 + x)


@jax.custom_jvp
def _expm1(x):
    u = jnp.exp(x)
    lu = jnp.log(u)
    small = (u == 1.0) | (lu == 0.0)
    return jnp.where(small, x, (u - 1.0) * x / jnp.where(small, 1.0, lu))


@_expm1.defjvp
def _expm1_jvp(p, t):
    (x,), (dx,) = p, t
    return _expm1(x), dx * jnp.exp(x)


def _softplus(x):
    return jnp.maximum(x, 0.0) + _log1p(jnp.exp(-jnp.abs(x)))


def _sigmoid(x):
    return jax.nn.sigmoid(x)


def _silu(x):
    return x * jax.nn.sigmoid(x)


def _gelu(x):
    return jax.nn.gelu(x)


@functools.partial(jax.custom_vjp, nondiff_argnums=(1,))
def _shift(x, s):
    return x if s == 0 else pltpu.roll(x, s, 0)


def _shift_fwd(x, s):
    return _shift(x, s), None


def _shift_bwd(s, _, g):
    n = g.shape[0]
    return (g if s == 0 else pltpu.roll(g, n - s, 0),)


_shift.defvjp(_shift_fwd, _shift_bwd)


def _scan_steps(a, b, reverse):
    n = a.shape[0]
    row = lax.broadcasted_iota(jnp.int32, a.shape, 0)
    k = 1
    while k < n:
        sh = n - k if reverse else k
        m = (row < n - k) if reverse else (row >= k)
        a_s = jnp.where(m, pltpu.roll(a, sh, 0), 1.0)
        b_s = jnp.where(m, pltpu.roll(b, sh, 0), 0.0)
        b = a * b_s + b
        a = a * a_s
        k *= 2
    return b


@jax.custom_vjp
def _scan(a, b):
    return _scan_steps(a, b, False)


def _scan_fwd(a, b):
    h = _scan_steps(a, b, False)
    return h, (a, h)


def _scan_bwd(res, dh):
    a, h = res
    n = a.shape[0]
    row = lax.broadcasted_iota(jnp.int32, a.shape, 0)
    a_next = jnp.where(row < n - 1, pltpu.roll(a, n - 1, 0), 0.0)
    g = _scan_steps(a_next, dh, True)
    h_prev = jnp.where(row >= 1, pltpu.roll(h, 1, 0), 0.0)
    return g * h_prev, g


_scan.defvjp(_scan_fwd, _scan_bwd)


def _mm(name, pairs, mode, out_dtype, *, res=None, scale=1.0, bm=None, bn=None, bk=None):
    a0, b0 = pairs[0]
    if mode == "nn":
        (M, K), N = a0.shape, b0.shape[1]
    elif mode == "nt":
        (M, K), N = a0.shape, b0.shape[0]
    else:
        (K, M), N = a0.shape, b0.shape[1]
    bm = bm or _pick(M, (512, 256, 128))
    bn = bn or _pick(N, (512, 256, 128))
    bk = bk or _pick(K, (1024, 512, 1408, 256, 128))
    nk = K // bk
    npair = len(pairs)
    dims = {"nn": _NN, "nt": _NT, "tn": _TN}[mode]

    def body(*refs):
        ab = refs[:2 * npair]
        pos = 2 * npair
        r_ref = None
        if res is not None:
            r_ref = refs[pos]
            pos += 1
        o_ref = refs[pos]
        part = None
        for p in range(npair):
            d = _dot(ab[2 * p][...], ab[2 * p + 1][...], dims)
            part = d if part is None else part + d

        def finish(acc):
            out = acc if scale == 1.0 else acc * scale
            if r_ref is not None:
                out = out + r_ref[...]
            o_ref[...] = out.astype(out_dtype)

        if nk == 1:
            finish(part)
        else:
            acc_ref = refs[pos + 1]
            k = pl.program_id(2)

            @pl.when(k == 0)
            def _():
                acc_ref[...] = part

            @pl.when(k > 0)
            def _():
                acc_ref[...] += part

            @pl.when(k == nk - 1)
            def _():
                finish(acc_ref[...])

    if mode == "nn":
        a_spec = pl.BlockSpec((bm, bk), lambda i, j, k: (i, k))
        b_spec = pl.BlockSpec((bk, bn), lambda i, j, k: (k, j))
    elif mode == "nt":
        a_spec = pl.BlockSpec((bm, bk), lambda i, j, k: (i, k))
        b_spec = pl.BlockSpec((bn, bk), lambda i, j, k: (j, k))
    else:
        a_spec = pl.BlockSpec((bk, bm), lambda i, j, k: (k, i))
        b_spec = pl.BlockSpec((bk, bn), lambda i, j, k: (k, j))
    o_spec = pl.BlockSpec((bm, bn), lambda i, j, k: (i, j))
    in_specs, args = [], []
    for a, b in pairs:
        in_specs += [a_spec, b_spec]
        args += [a, b]
    if res is not None:
        in_specs.append(o_spec)
        args.append(res)
    return pl.pallas_call(
        body, name=name, grid=(M // bm, N // bn, nk),
        in_specs=in_specs, out_specs=o_spec,
        out_shape=jax.ShapeDtypeStruct((M, N), out_dtype),
        scratch_shapes=[pltpu.VMEM((bm, bn), F32)] if nk > 1 else [],
        compiler_params=_cp("parallel", "parallel", "arbitrary"),
    )(*args)


def _rms_fwd(name, x, g):
    T = x.shape[0]
    bm = _pick(T, (512, 256, 128))

    def body(x_ref, g_ref, o_ref):
        xv = x_ref[...]
        r = lax.rsqrt(jnp.mean(xv * xv, axis=-1, keepdims=True) + EPS)
        o_ref[...] = (xv * r * g_ref[...]).astype(BF16)

    return pl.pallas_call(
        body, name=name, grid=(T // bm,),
        in_specs=[pl.BlockSpec((bm, D), lambda i: (i, 0)), pl.BlockSpec((1, D), lambda i: (0, 0))],
        out_specs=pl.BlockSpec((bm, D), lambda i: (i, 0)),
        out_shape=jax.ShapeDtypeStruct((T, D), BF16),
        compiler_params=_cp("parallel"),
    )(x, g)


def _rms_bwd(name, x, g, dh, dres):
    T = x.shape[0]
    bm = _pick(T, (512, 256, 128))

    def body(x_ref, g_ref, dh_ref, dres_ref, dx_ref, dg_ref):
        xv = x_ref[...]
        r = lax.rsqrt(jnp.mean(xv * xv, axis=-1, keepdims=True) + EPS)
        xh = xv * r
        dhv = dh_ref[...]
        dxh = dhv * g_ref[...]
        dx_ref[...] = dres_ref[...] + r * (dxh - xh * jnp.mean(dxh * xh, axis=-1, keepdims=True))
        part = jnp.sum(dhv * xh, axis=0, keepdims=True)

        @pl.when(pl.program_id(0) == 0)
        def _():
            dg_ref[...] = part

        @pl.when(pl.program_id(0) > 0)
        def _():
            dg_ref[...] += part

    row = pl.BlockSpec((bm, D), lambda i: (i, 0))
    vec = pl.BlockSpec((1, D), lambda i: (0, 0))
    return pl.pallas_call(
        body, name=name, grid=(T // bm,),
        in_specs=[row, vec, row, row], out_specs=[row, vec],
        out_shape=[jax.ShapeDtypeStruct((T, D), F32), jax.ShapeDtypeStruct((1, D), F32)],
        compiler_params=_cp("arbitrary"),
    )(x, g, dh, dres)


def _final_loss(name, x, g, tgt):
    T = x.shape[0]
    bm = _pick(T, (512, 256, 128))

    def body(x_ref, g_ref, t_ref, loss_ref, dx_ref, dg_ref):
        xv = x_ref[...]
        gv = g_ref[...]
        r = lax.rsqrt(jnp.mean(xv * xv, axis=-1, keepdims=True) + EPS)
        xh = xv * r
        e = xh * gv - t_ref[...]
        lpart = jnp.broadcast_to(0.5 * jnp.sum(jnp.mean(e * e, axis=-1, keepdims=True), axis=0, keepdims=True), (1, 128))
        dy = e * (1.0 / D)
        dxh = dy * gv
        dx_ref[...] = r * (dxh - xh * jnp.mean(dxh * xh, axis=-1, keepdims=True))
        gpart = jnp.sum(dy * xh, axis=0, keepdims=True)

        @pl.when(pl.program_id(0) == 0)
        def _():
            loss_ref[...] = lpart
            dg_ref[...] = gpart

        @pl.when(pl.program_id(0) > 0)
        def _():
            loss_ref[...] += lpart
            dg_ref[...] += gpart

    row = pl.BlockSpec((bm, D), lambda i: (i, 0))
    vec = pl.BlockSpec((1, D), lambda i: (0, 0))
    return pl.pallas_call(
        body, name=name, grid=(T // bm,),
        in_specs=[row, vec, row],
        out_specs=[pl.BlockSpec((1, 128), lambda i: (0, 0)), row, vec],
        out_shape=[jax.ShapeDtypeStruct((1, 128), F32), jax.ShapeDtypeStruct((T, D), F32),
                   jax.ShapeDtypeStruct((1, D), F32)],
        compiler_params=_cp("arbitrary"),
    )(x, g, tgt)


def _ffn_up(name, h, wg, wu):
    T = h.shape[0]
    bm = _pick(T, (512, 256, 128))
    bn = 256

    def body(h_ref, wg_ref, wu_ref, a_ref, b_ref, act_ref):
        hv = h_ref[...]
        a = _dot(hv, wg_ref[...], _NT)
        b = _dot(hv, wu_ref[...], _NT)
        a_ref[...] = a
        b_ref[...] = b
        act_ref[...] = (_silu(a) * b).astype(BF16)

    w_spec = pl.BlockSpec((bn, D), lambda i, j: (j, 0))
    o_spec = pl.BlockSpec((bm, bn), lambda i, j: (i, j))
    return pl.pallas_call(
        body, name=name, grid=(T // bm, FF // bn),
        in_specs=[pl.BlockSpec((bm, D), lambda i, j: (i, 0)), w_spec, w_spec],
        out_specs=[o_spec, o_spec, o_spec],
        out_shape=[jax.ShapeDtypeStruct((T, FF), F32), jax.ShapeDtypeStruct((T, FF), F32),
                   jax.ShapeDtypeStruct((T, FF), BF16)],
        compiler_params=_cp("parallel", "parallel"),
    )(h, wg, wu)


def _ffn_dact(name, dy, wd, a, b):
    T = dy.shape[0]
    bm = _pick(T, (512, 256, 128))
    bn = 256

    def body(dy_ref, wd_ref, a_ref, b_ref, da_ref, db_ref):
        dact = 0.5 * _dot(dy_ref[...], wd_ref[...], _NT)
        av = a_ref[...]
        s = _sigmoid(av)
        da_ref[...] = (dact * b_ref[...] * (s * (1.0 + av * (1.0 - s)))).astype(BF16)
        db_ref[...] = (dact * (av * s)).astype(BF16)

    t_spec = pl.BlockSpec((bm, bn), lambda i, j: (i, j))
    return pl.pallas_call(
        body, name=name, grid=(T // bm, FF // bn),
        in_specs=[pl.BlockSpec((bm, D), lambda i, j: (i, 0)), pl.BlockSpec((bn, D), lambda i, j: (j, 0)),
                  t_spec, t_spec],
        out_specs=[t_spec, t_spec],
        out_shape=[jax.ShapeDtypeStruct((T, FF), BF16), jax.ShapeDtypeStruct((T, FF), BF16)],
        compiler_params=_cp("parallel", "parallel"),
    )(dy, wd, a, b)


def _merge_specs(T, bm, bn):
    y_spec = pl.BlockSpec((bm, BW), lambda i, j: (i, 0))
    wb_spec = pl.BlockSpec((NBR, bn, BW), lambda i, j: (0, j, 0))
    gate_specs = [pl.BlockSpec((bm, bn), functools.partial(lambda i, j, o: (i, o + j), o=(COL_GATE + g * D) // bn))
                  for g in range(NBR)]
    t_spec = pl.BlockSpec((bm, bn), lambda i, j: (i, j))
    return y_spec, wb_spec, gate_specs, t_spec


def _merge_fwd(name, ys, wb, proj):
    T = proj.shape[0]
    bm = _pick(T, (512, 256, 128))
    bn = 512
    y_spec, wb_spec, gate_specs, t_spec = _merge_specs(T, bm, bn)

    def body(y0, y1, y2, y3, wb_ref, g0, g1, g2, g3, o_ref):
        acc = None
        for g, (y_ref, g_ref) in enumerate(((y0, g0), (y1, g1), (y2, g2), (y3, g3))):
            t = _sigmoid(g_ref[...]) * _dot(y_ref[...], wb_ref[g], _NT)
            acc = t if acc is None else acc + t
        o_ref[...] = acc.astype(BF16)

    return pl.pallas_call(
        body, name=name, grid=(T // bm, D // bn),
        in_specs=[y_spec] * NBR + [wb_spec] + gate_specs, out_specs=t_spec,
        out_shape=jax.ShapeDtypeStruct((T, D), BF16),
        compiler_params=_cp("parallel", "parallel"),
    )(*ys, wb, proj, proj, proj, proj)


def _merge_bwd(name, dm, ys, wb, proj):
    T = proj.shape[0]
    bm = _pick(T, (512, 256, 128))
    bn = 512
    y_spec, wb_spec, gate_specs, t_spec = _merge_specs(T, bm, bn)

    def body(dm_ref, y0, y1, y2, y3, wb_ref, g0, g1, g2, g3, *outs):
        dmv = dm_ref[...]
        for g, (y_ref, g_ref) in enumerate(((y0, g0), (y1, g1), (y2, g2), (y3, g3))):
            br = _dot(y_ref[...], wb_ref[g], _NT)
            s = _sigmoid(g_ref[...])
            outs[g][...] = (dmv * br * (s * (1.0 - s))).astype(BF16)
            outs[NBR + g][...] = (dmv * s).astype(BF16)

    return pl.pallas_call(
        body, name=name, grid=(T // bm, D // bn),
        in_specs=[t_spec] + [y_spec] * NBR + [wb_spec] + gate_specs, out_specs=[t_spec] * (2 * NBR),
        out_shape=[jax.ShapeDtypeStruct((T, D), BF16)] * (2 * NBR),
        compiler_params=_cp("parallel", "parallel"),
    )(dm, *ys, wb, proj, proj, proj, proj)


def _sgu_block(u_pre, v_pre, ln_g, ln_b, w, bias):
    u = _gelu(u_pre)
    vf = _gelu(v_pre)
    mu = jnp.mean(vf, axis=-1, keepdims=True)
    var = jnp.mean(jnp.square(vf - mu), axis=-1, keepdims=True)
    vn = (vf - mu) * lax.rsqrt(var + EPS) * ln_g + ln_b
    ri = lax.broadcasted_iota(jnp.int32, (128, 128), 0)
    ci = lax.broadcasted_iota(jnp.int32, (128, 128), 1)
    mask = (ri // CHUNK) >= (ci // CHUNK)
    outs = [_dot(jnp.where(mask, w[g], 0.0), vn[:, g * 128:(g + 1) * 128]) for g in range(4)]
    mixed = jnp.concatenate(outs, axis=1) + bias
    return u * mixed


def _sgu_param_specs():
    return [pl.BlockSpec((1, BW), lambda i: (0, 0)), pl.BlockSpec((1, BW), lambda i: (0, 0)),
            pl.BlockSpec((4, 128, 128), lambda i: (0, 0, 0)), pl.BlockSpec((128, BW), lambda i: (0, 0))]


def _sgu_fwd(name, proj, ln_g, ln_b, w, bias):
    T = proj.shape[0]
    rb = _pick(T, (256, 128))

    def body(u_ref, v_ref, g_ref, b_ref, w_ref, bias_ref, y_ref):
        for n in range(rb // 128):
            rows = slice(n * 128, (n + 1) * 128)
            y = _sgu_block(u_ref[rows, :], v_ref[rows, :], g_ref[...], b_ref[...], w_ref[...], bias_ref[...])
            y_ref[rows, :] = y.astype(BF16)

    return pl.pallas_call(
        body, name=name, grid=(T // rb,),
        in_specs=[pl.BlockSpec((rb, BW), lambda i: (i, COL_AU // BW)), pl.BlockSpec((rb, BW), lambda i: (i, COL_AV // BW))]
        + _sgu_param_specs(),
        out_specs=pl.BlockSpec((rb, BW), lambda i: (i, 0)),
        out_shape=jax.ShapeDtypeStruct((T, BW), BF16),
        compiler_params=_cp("parallel"),
    )(proj, proj, ln_g, ln_b, w, bias)


def _sgu_bwd(name, proj, dy, ln_g, ln_b, w, bias):
    T = proj.shape[0]
    rb = _pick(T, (256, 128))

    def body(u_ref, v_ref, dy_ref, g_ref, b_ref, w_ref, bias_ref, du_ref, dv_ref, dg_ref, db_ref, dw_ref, dbias_ref):
        acc = None
        for n in range(rb // 128):
            rows = slice(n * 128, (n + 1) * 128)
            _, vjp = jax.vjp(_sgu_block, u_ref[rows, :], v_ref[rows, :], g_ref[...], b_ref[...], w_ref[...],
                             bias_ref[...])
            du, dv, *dp = vjp(dy_ref[rows, :])
            du_ref[rows, :] = du.astype(BF16)
            dv_ref[rows, :] = dv.astype(BF16)
            acc = dp if acc is None else [p + q for p, q in zip(acc, dp)]

        @pl.when(pl.program_id(0) == 0)
        def _():
            for r, p in zip((dg_ref, db_ref, dw_ref, dbias_ref), acc):
                r[...] = p

        @pl.when(pl.program_id(0) > 0)
        def _():
            for r, p in zip((dg_ref, db_ref, dw_ref, dbias_ref), acc):
                r[...] += p

    row = pl.BlockSpec((rb, BW), lambda i: (i, 0))
    return pl.pallas_call(
        body, name=name, grid=(T // rb,),
        in_specs=[pl.BlockSpec((rb, BW), lambda i: (i, COL_AU // BW)), pl.BlockSpec((rb, BW), lambda i: (i, COL_AV // BW)),
                  row] + _sgu_param_specs(),
        out_specs=[row, row] + _sgu_param_specs(),
        out_shape=[jax.ShapeDtypeStruct((T, BW), BF16), jax.ShapeDtypeStruct((T, BW), BF16),
                   jax.ShapeDtypeStruct((1, BW), F32), jax.ShapeDtypeStruct((1, BW), F32),
                   jax.ShapeDtypeStruct((4, 128, 128), F32), jax.ShapeDtypeStruct((128, BW), F32)],
        compiler_params=_cp("arbitrary"),
    )(proj, proj, dy, ln_g, ln_b, w, bias)


def _halo_block(ref, i, rblk, halo):
    r0 = pl.multiple_of(i * rblk, rblk)
    h0 = pl.multiple_of(jnp.maximum(r0 - halo, 0), halo)
    top = jnp.where(i > 0, ref[pl.ds(h0, halo), :], 0.0)
    return jnp.concatenate([top, ref[pl.ds(r0, rblk), :]], axis=0)


def _with_halo_grad(dfull, pending, halo, rblk):
    tail = jnp.concatenate([jnp.zeros((rblk - halo, 128), F32), pending], axis=0)
    return dfull[halo:] + tail


def _conv4(xfull, rows):
    acc = None
    for k in range(4):
        t = rows[k] * _shift(xfull, 3 - k)[8:]
        acc = t if acc is None else acc + t
    return acc


def _lru_block(xfull, gate, h0, c0, c1, c2, c3, cb, wa, ba, wx, bx, lam):
    n = gate.shape[0]
    xc = _conv4(xfull, (c0, c1, c2, c3)) + cb
    r = _sigmoid(_dot(xc, wa) + ba)
    ig = _sigmoid(_dot(xc, wx) + bx)
    log_a = -LRU_C * r * _softplus(-lam)
    a = jnp.exp(log_a)
    mult = jnp.sqrt(-_expm1(2.0 * log_a))
    b = mult * (ig * xc)
    row = lax.broadcasted_iota(jnp.int32, (n, 128), 0)
    b = b + jnp.where(row == 0, a * h0, 0.0)
    h = _scan(a, b)
    out = h * _gelu(gate)
    h_last = jnp.sum(jnp.where(row == n - 1, h, 0.0), axis=0, keepdims=True)
    return out, h_last


def _lru_param_specs():
    vec = pl.BlockSpec((1, 128), lambda g: (0, g))
    mat = pl.BlockSpec((None, 128, 128), lambda g: (g, 0, 0))
    return [pl.BlockSpec((4, 128), lambda g: (0, g)), vec, mat, vec, mat, vec, vec]


def _lru_load_params(cw_ref, cb_ref, wa_ref, ba_ref, wx_ref, bx_ref, lam_ref):
    return (cw_ref[0:1, :], cw_ref[1:2, :], cw_ref[2:3, :], cw_ref[3:4, :], cb_ref[...], wa_ref[...], ba_ref[...],
            wx_ref[...], bx_ref[...], lam_ref[...])


def _lru_fwd(name, proj, cw, cb, wa, ba, wx, bx, lam):
    T = proj.shape[0]
    rblk = _pick(T, (256, 128))
    nblk = T // rblk

    def body(x_ref, gt_ref, cw_ref, cb_ref, wa_ref, ba_ref, wx_ref, bx_ref, lam_ref, y_ref, hc_ref):
        params = _lru_load_params(cw_ref, cb_ref, wa_ref, ba_ref, wx_ref, bx_ref, lam_ref)

        def step(i, h0):
            r0 = pl.multiple_of(i * rblk, rblk)
            out, h_last = _lru_block(_halo_block(x_ref, i, rblk, 8), gt_ref[pl.ds(r0, rblk), :], h0, *params)
            y_ref[pl.ds(r0, rblk), :] = out.astype(BF16)
            hc_ref[pl.ds(pl.multiple_of(i * 8, 8), 8), :] = jnp.broadcast_to(h0, (8, 128))
            return h_last

        lax.fori_loop(0, nblk, step, jnp.zeros((1, 128), F32))

    return pl.pallas_call(
        body, name=name, grid=(4,),
        in_specs=[pl.BlockSpec((T, 128), lambda g: (0, COL_BX // 128 + g)),
                  pl.BlockSpec((T, 128), lambda g: (0, COL_BG // 128 + g))] + _lru_param_specs(),
        out_specs=[pl.BlockSpec((T, 128), lambda g: (0, g)), pl.BlockSpec((nblk * 8, 128), lambda g: (0, g))],
        out_shape=[jax.ShapeDtypeStruct((T, BW), BF16), jax.ShapeDtypeStruct((nblk * 8, BW), F32)],
        compiler_params=_cp("parallel"),
    )(proj, proj, cw, cb, wa, ba, wx, bx, lam)


def _lru_bwd(name, proj, dy, hc, cw, cb, wa, ba, wx, bx, lam):
    T = proj.shape[0]
    rblk = _pick(T, (256, 128))
    nblk = T // rblk

    def body(x_ref, gt_ref, dy_ref, hc_ref, cw_ref, cb_ref, wa_ref, ba_ref, wx_ref, bx_ref, lam_ref,
             dx_ref, dgt_ref, dcw_ref, dcb_ref, dwa_ref, dba_ref, dwx_ref, dbx_ref, dlam_ref):
        params = _lru_load_params(cw_ref, cb_ref, wa_ref, ba_ref, wx_ref, bx_ref, lam_ref)

        def step(it, carry):
            dh_last, pending, acc = carry
            i = nblk - 1 - it
            r0 = pl.multiple_of(i * rblk, rblk)
            h0 = hc_ref[pl.ds(pl.multiple_of(i * 8, 8), 1), :]
            _, vjp = jax.vjp(_lru_block, _halo_block(x_ref, i, rblk, 8), gt_ref[pl.ds(r0, rblk), :], h0, *params)
            dfull, dgate, dh0, *dp = vjp((dy_ref[pl.ds(r0, rblk), :], dh_last))
            dx_ref[pl.ds(r0, rblk), :] = _with_halo_grad(dfull, pending, 8, rblk).astype(BF16)
            dgt_ref[pl.ds(r0, rblk), :] = dgate.astype(BF16)
            return dh0, dfull[:8], tuple(p + q for p, q in zip(acc, dp))

        zeros = tuple(jnp.zeros(p.shape, F32) for p in params)
        _, _, acc = lax.fori_loop(0, nblk, step, (jnp.zeros((1, 128), F32), jnp.zeros((8, 128), F32), zeros))
        for k in range(4):
            dcw_ref[k:k + 1, :] = acc[k]
        for r, p in zip((dcb_ref, dwa_ref, dba_ref, dwx_ref, dbx_ref, dlam_ref), acc[4:]):
            r[...] = p

    col = pl.BlockSpec((T, 128), lambda g: (0, g))
    return pl.pallas_call(
        body, name=name, grid=(4,),
        in_specs=[pl.BlockSpec((T, 128), lambda g: (0, COL_BX // 128 + g)),
                  pl.BlockSpec((T, 128), lambda g: (0, COL_BG // 128 + g)), col,
                  pl.BlockSpec((nblk * 8, 128), lambda g: (0, g))] + _lru_param_specs(),
        out_specs=[col, col] + _lru_param_specs(),
        out_shape=[jax.ShapeDtypeStruct((T, BW), BF16), jax.ShapeDtypeStruct((T, BW), BF16),
                   jax.ShapeDtypeStruct((4, BW), F32), jax.ShapeDtypeStruct((1, BW), F32),
                   jax.ShapeDtypeStruct((4, 128, 128), F32), jax.ShapeDtypeStruct((1, BW), F32),
                   jax.ShapeDtypeStruct((4, 128, 128), F32), jax.ShapeDtypeStruct((1, BW), F32),
                   jax.ShapeDtypeStruct((1, BW), F32)],
        compiler_params=_cp("parallel"),
    )(proj, proj, dy, hc, cw, cb, wa, ba, wx, bx, lam)


def _conv_block(xfull, c0, c1, c2, c3):
    return _silu(_conv4(xfull, (c0, c1, c2, c3)))


def _conv_fwd(name, proj, col0, cw, cw_col0):
    T = proj.shape[0]
    rblk = _pick(T, (256, 128))
    nblk = T // rblk

    def body(x_ref, cw_ref, y_ref):
        rows = (cw_ref[0:1, :], cw_ref[1:2, :], cw_ref[2:3, :], cw_ref[3:4, :])

        def step(i, c):
            r0 = pl.multiple_of(i * rblk, rblk)
            y_ref[pl.ds(r0, rblk), :] = _conv_block(_halo_block(x_ref, i, rblk, 8), *rows)
            return c

        lax.fori_loop(0, nblk, step, 0)

    return pl.pallas_call(
        body, name=name, grid=(4,),
        in_specs=[pl.BlockSpec((T, 128), lambda g: (0, col0 // 128 + g)),
                  pl.BlockSpec((4, 128), lambda g: (0, cw_col0 // 128 + g))],
        out_specs=pl.BlockSpec((T, 128), lambda g: (0, g)),
        out_shape=jax.ShapeDtypeStruct((T, BW), F32),
        compiler_params=_cp("parallel"),
    )(proj, cw)


def _conv_bwd(name, proj, col0, dy, cw, cw_col0):
    T = proj.shape[0]
    rblk = _pick(T, (256, 128))
    nblk = T // rblk

    def body(x_ref, dy_ref, cw_ref, dx_ref, dcw_ref):
        rows = (cw_ref[0:1, :], cw_ref[1:2, :], cw_ref[2:3, :], cw_ref[3:4, :])

        def step(it, carry):
            pending, acc = carry
            i = nblk - 1 - it
            r0 = pl.multiple_of(i * rblk, rblk)
            _, vjp = jax.vjp(_conv_block, _halo_block(x_ref, i, rblk, 8), *rows)
            dfull, *dp = vjp(dy_ref[pl.ds(r0, rblk), :])
            dx_ref[pl.ds(r0, rblk), :] = _with_halo_grad(dfull, pending, 8, rblk).astype(BF16)
            return dfull[:8], tuple(p + q for p, q in zip(acc, dp))

        zeros = tuple(jnp.zeros((1, 128), F32) for _ in range(4))
        _, acc = lax.fori_loop(0, nblk, step, (jnp.zeros((8, 128), F32), zeros))
        for k in range(4):
            dcw_ref[k:k + 1, :] = acc[k]

    col = pl.BlockSpec((T, 128), lambda g: (0, g))
    return pl.pallas_call(
        body, name=name, grid=(4,),
        in_specs=[pl.BlockSpec((T, 128), lambda g: (0, col0 // 128 + g)), col,
                  pl.BlockSpec((4, 128), lambda g: (0, cw_col0 // 128 + g))],
        out_specs=[col, pl.BlockSpec((4, 128), lambda g: (0, g))],
        out_shape=[jax.ShapeDtypeStruct((T, BW), BF16), jax.ShapeDtypeStruct((4, BW), F32)],
        compiler_params=_cp("parallel"),
    )(proj, dy, cw)


def _pool_block(xfull, pw, sc, t0, gi):
    n = xfull.shape[0] - 16
    s2 = xfull + _shift(xfull, 1)
    s4 = s2 + _shift(s2, 2)
    s8 = s4 + _shift(s4, 4)
    s16 = s8 + _shift(s8, 8)
    s = jnp.where(gi == 0, s2, jnp.where(gi == 1, s4, jnp.where(gi == 2, s8, s16)))[16:]
    t = t0 + lax.broadcasted_iota(jnp.int32, (n, 128), 0)
    cnt = jnp.minimum(t + 1, lax.shift_left(jnp.int32(2), gi)).astype(F32)
    pooled = s / cnt - xfull[16:]
    return _dot(pooled, pw) * sc


def _pool_fwd(name, proj, pw, sc):
    T = proj.shape[0]
    rblk = _pick(T, (256, 128))
    nblk = T // rblk

    def body(x_ref, pw_ref, sc_ref, y_ref):
        gi = pl.program_id(0)

        def step(i, c):
            r0 = pl.multiple_of(i * rblk, rblk)
            y = _pool_block(_halo_block(x_ref, i, rblk, 16), pw_ref[...], sc_ref[...], r0, gi)
            y_ref[pl.ds(r0, rblk), :] = y.astype(BF16)
            return c

        lax.fori_loop(0, nblk, step, 0)

    return pl.pallas_call(
        body, name=name, grid=(4,),
        in_specs=[pl.BlockSpec((T, 128), lambda g: (0, COL_DX // 128 + g)),
                  pl.BlockSpec((None, 128, 128), lambda g: (g, 0, 0)), pl.BlockSpec((1, 128), lambda g: (0, g))],
        out_specs=pl.BlockSpec((T, 128), lambda g: (0, g)),
        out_shape=jax.ShapeDtypeStruct((T, BW), BF16),
        compiler_params=_cp("parallel"),
    )(proj, pw, sc)


def _pool_bwd(name, proj, dy, pw, sc):
    T = proj.shape[0]
    rblk = _pick(T, (256, 128))
    nblk = T // rblk

    def body(x_ref, dy_ref, pw_ref, sc_ref, dx_ref, dpw_ref, dsc_ref):
        gi = pl.program_id(0)

        def step(it, carry):
            pending, apw, asc = carry
            i = nblk - 1 - it
            r0 = pl.multiple_of(i * rblk, rblk)
            _, vjp = jax.vjp(lambda xf, w, s: _pool_block(xf, w, s, r0, gi), _halo_block(x_ref, i, rblk, 16),
                             pw_ref[...], sc_ref[...])
            dfull, dw, ds = vjp(dy_ref[pl.ds(r0, rblk), :])
            dx_ref[pl.ds(r0, rblk), :] = _with_halo_grad(dfull, pending, 16, rblk).astype(BF16)
            return dfull[:16], apw + dw, asc + ds

        _, apw, asc = lax.fori_loop(0, nblk, step, (jnp.zeros((16, 128), F32), jnp.zeros((128, 128), F32),
                                                    jnp.zeros((1, 128), F32)))
        dpw_ref[...] = apw
        dsc_ref[...] = asc

    col = pl.BlockSpec((T, 128), lambda g: (0, g))
    mat = pl.BlockSpec((None, 128, 128), lambda g: (g, 0, 0))
    vec = pl.BlockSpec((1, 128), lambda g: (0, g))
    return pl.pallas_call(
        body, name=name, grid=(4,),
        in_specs=[pl.BlockSpec((T, 128), lambda g: (0, COL_DX // 128 + g)), col, mat, vec],
        out_specs=[col, mat, vec],
        out_shape=[jax.ShapeDtypeStruct((T, BW), BF16), jax.ShapeDtypeStruct((4, 128, 128), F32),
                   jax.ShapeDtypeStruct((1, BW), F32)],
        compiler_params=_cp("parallel"),
    )(proj, dy, pw, sc)


def _dot3(a, b):
    ah = a.astype(BF16)
    al = (a - ah.astype(F32)).astype(BF16)
    bh = b.astype(BF16)
    bl = (b - bh.astype(F32)).astype(BF16)

    def d(x, y):
        return lax.dot_general(x, y, _NN, preferred_element_type=F32)

    return d(ah, bh) + (d(ah, bl) + d(al, bh))


def _tri_inv(mats):
    n = mats[0].shape[0]
    eye = (lax.broadcasted_iota(jnp.int32, (n, n), 0) == lax.broadcasted_iota(jnp.int32, (n, n), 1)).astype(F32)
    ps = [eye - a for a in mats]
    ms = list(mats)
    k = 2
    while k < n:
        ms = [_dot3(m, m) for m in ms]
        ps = [p + _dot3(p, m) for p, m in zip(ps, ms)]
        k *= 2
    return ps


def _cumsum_rows(x):
    n = x.shape[0]
    row = lax.broadcasted_iota(jnp.int32, x.shape, 0)
    k = 1
    while k < n:
        x = x + jnp.where(row >= k, _shift(x, k), 0.0)
        k *= 2
    return x


def _gdn_chunk(states, qc, kc, vc, z, tail, alog, dtb, ng):
    C, H = CHUNK, 4
    hs = range(H)
    lane = lax.broadcasted_iota(jnp.int32, (C, 128), 1)
    row = lax.broadcasted_iota(jnp.int32, (C, 128), 0)
    ri = lax.broadcasted_iota(jnp.int32, (C, C), 0)
    ci = lax.broadcasted_iota(jnp.int32, (C, C), 1)
    incl = ri >= ci
    sig = _sigmoid(tail)
    gfull = -jnp.exp(alog) * _softplus(tail + dtb)
    beta = [jnp.sum(jnp.where(lane == h, sig, 0.0), axis=1, keepdims=True) for h in hs]
    g = [jnp.sum(jnp.where(lane == h + 4, gfull, 0.0), axis=1, keepdims=True) for h in hs]
    qs = [qc[:, h * 128:(h + 1) * 128] for h in hs]
    ks = [kc[:, h * 128:(h + 1) * 128] for h in hs]
    vs = [vc[:, h * 128:(h + 1) * 128] for h in hs]
    q = [t * lax.rsqrt(jnp.sum(t * t, axis=-1, keepdims=True) + EPS) * (GDN_DK ** -0.5) for t in qs]
    k = [t * lax.rsqrt(jnp.sum(t * t, axis=-1, keepdims=True) + EPS) for t in ks]
    gc = [_cumsum_rows(jnp.broadcast_to(t, (C, 128))) for t in g]
    gc_row = [jnp.transpose(t)[:C, :] for t in gc]
    gc_col = [jnp.sum(jnp.where(lane == 0, t, 0.0), axis=1, keepdims=True) for t in gc]
    decay = [jnp.exp(jnp.where(incl, gc_col[h] - gc_row[h], -1e30)) for h in hs]
    kb = [k[h] * beta[h] for h in hs]
    kk = [_dot(kb[h], k[h], _NT) for h in hs]
    t_mat = _tri_inv([jnp.where(ri > ci, kk[h] * decay[h], 0.0) for h in hs])
    egc = [jnp.exp(t) for t in gc]
    u = [_dot(t_mat[h], vs[h] * beta[h]) for h in hs]
    w = [_dot(t_mat[h], kb[h] * egc[h]) for h in hs]
    qk = [_dot(q[h], k[h], _NT) for h in hs]
    attn = [jnp.where(incl, qk[h] * decay[h], 0.0) for h in hs]
    ws = [_dot(w[h], states[h]) for h in hs]
    qs_ = [_dot(q[h] * egc[h], states[h]) for h in hs]
    v_new = [u[h] - ws[h] for h in hs]
    av = [_dot(attn[h], v_new[h]) for h in hs]
    g_last = [jnp.sum(jnp.where(row == C - 1, t, 0.0), axis=0, keepdims=True) for t in gc]
    kv = [_dot(k[h] * jnp.exp(g_last[h] - gc[h]), v_new[h], _TN) for h in hs]
    nxt = tuple(states[h] * jnp.exp(g_last[h]) + kv[h] for h in hs)
    o = [qs_[h] + av[h] for h in hs]
    on = [t * lax.rsqrt(jnp.mean(t * t, axis=-1, keepdims=True) + EPS) * ng for t in o]
    return nxt, jnp.concatenate(on, axis=1) * _silu(z)


def _gdn_blocks(T):
    tb = _pick(T, (512, 256, 128, 64))
    return tb, T // tb, tb // CHUNK


def _gdn_fwd(name, qa, ka, va, proj, alog, dtb, ng):
    T = proj.shape[0]
    tb, nb, ncb = _gdn_blocks(T)

    def body(q_ref, k_ref, v_ref, z_ref, tail_ref, alog_ref, dtb_ref, ng_ref, y_ref, sh_ref, state):
        @pl.when(pl.program_id(0) == 0)
        def _():
            state[...] = jnp.zeros((4, 128, 128), F32)

        def step(c, states):
            rows = pl.ds(pl.multiple_of(c * CHUNK, CHUNK), CHUNK)
            for h in range(4):
                sh_ref[h, c] = states[h]
            nxt, y = _gdn_chunk(states, q_ref[rows, :], k_ref[rows, :], v_ref[rows, :], z_ref[rows, :],
                                tail_ref[rows, :], alog_ref[...], dtb_ref[...], ng_ref[...])
            y_ref[rows, :] = y.astype(BF16)
            return nxt

        states = lax.fori_loop(0, ncb, step, tuple(state[h] for h in range(4)))
        for h in range(4):
            state[h] = states[h]

    blk = pl.BlockSpec((tb, BW), lambda j: (j, 0))
    vec = pl.BlockSpec((1, 128), lambda j: (0, 0))
    return pl.pallas_call(
        body, name=name, grid=(nb,),
        in_specs=[blk, blk, blk, pl.BlockSpec((tb, BW), lambda j: (j, COL_CZ // BW)),
                  pl.BlockSpec((tb, 128), lambda j: (j, COL_TAIL // 128)), vec, vec, vec],
        out_specs=[blk, pl.BlockSpec((4, ncb, 128, 128), lambda j: (0, j, 0, 0))],
        out_shape=[jax.ShapeDtypeStruct((T, BW), BF16), jax.ShapeDtypeStruct((4, T // CHUNK, 128, 128), F32)],
        scratch_shapes=[pltpu.VMEM((4, 128, 128), F32)],
        compiler_params=_cp("arbitrary"),
    )(qa, ka, va, proj, proj, alog, dtb, ng)


def _gdn_bwd(name, qa, ka, va, proj, dy, sh, alog, dtb, ng):
    T = proj.shape[0]
    tb, nb, ncb = _gdn_blocks(T)

    def body(q_ref, k_ref, v_ref, z_ref, tail_ref, dy_ref, sh_ref, alog_ref, dtb_ref, ng_ref,
             dq_ref, dk_ref, dv_ref, dz_ref, dtail_ref, dalog_ref, ddtb_ref, dng_ref, dstate):
        first = pl.program_id(0) == 0

        @pl.when(first)
        def _():
            dstate[...] = jnp.zeros((4, 128, 128), F32)

        def step(it, carry):
            dstates, pa, pd, pn = carry
            c = ncb - 1 - it
            rows = pl.ds(pl.multiple_of(c * CHUNK, CHUNK), CHUNK)
            _, vjp = jax.vjp(_gdn_chunk, tuple(sh_ref[h, c] for h in range(4)), q_ref[rows, :], k_ref[rows, :],
                             v_ref[rows, :], z_ref[rows, :], tail_ref[rows, :], alog_ref[...], dtb_ref[...], ng_ref[...])
            nxt, dq, dk, dv, dz, dtail, da, dd, dn = vjp((dstates, dy_ref[rows, :]))
            dq_ref[rows, :] = dq
            dk_ref[rows, :] = dk
            dv_ref[rows, :] = dv
            dz_ref[rows, :] = dz.astype(BF16)
            dtail_ref[rows, :] = dtail.astype(BF16)
            return nxt, pa + da, pd + dd, pn + dn

        zv = jnp.zeros((1, 128), F32)
        dstates, pa, pd, pn = lax.fori_loop(0, ncb, step, (tuple(dstate[h] for h in range(4)), zv, zv, zv))
        for h in range(4):
            dstate[h] = dstates[h]

        @pl.when(first)
        def _():
            dalog_ref[...] = pa
            ddtb_ref[...] = pd
            dng_ref[...] = pn

        @pl.when(jnp.logical_not(first))
        def _():
            dalog_ref[...] += pa
            ddtb_ref[...] += pd
            dng_ref[...] += pn

    blk = pl.BlockSpec((tb, BW), lambda j: (nb - 1 - j, 0))
    vec = pl.BlockSpec((1, 128), lambda j: (0, 0))
    return pl.pallas_call(
        body, name=name, grid=(nb,),
        in_specs=[blk, blk, blk, pl.BlockSpec((tb, BW), lambda j: (nb - 1 - j, COL_CZ // BW)),
                  pl.BlockSpec((tb, 128), lambda j: (nb - 1 - j, COL_TAIL // 128)), blk,
                  pl.BlockSpec((4, ncb, 128, 128), lambda j: (0, nb - 1 - j, 0, 0)), vec, vec, vec],
        out_specs=[blk, blk, blk, blk, pl.BlockSpec((tb, 128), lambda j: (nb - 1 - j, 0)), vec, vec, vec],
        out_shape=[jax.ShapeDtypeStruct((T, BW), F32)] * 3
        + [jax.ShapeDtypeStruct((T, BW), BF16), jax.ShapeDtypeStruct((T, 128), BF16)]
        + [jax.ShapeDtypeStruct((1, 128), F32)] * 3,
        scratch_shapes=[pltpu.VMEM((4, 128, 128), F32)],
        compiler_params=_cp("arbitrary"),
    )(qa, ka, va, proj, proj, dy, sh, alog, dtb, ng)


def _adamw(name, w, g, m, v):
    R, C = w.shape
    br = _pick(R, (512, 256, 240, 128, 64, 8))

    def body(w_ref, g_ref, m_ref, v_ref, d_ref, nm_ref, nv_ref):
        gv = g_ref[...]
        m2 = ADAM_B1 * m_ref[...] + (1.0 - ADAM_B1) * gv
        v2 = ADAM_B2 * v_ref[...] + (1.0 - ADAM_B2) * jnp.square(gv)
        m_hat = m2 / (1.0 - ADAM_B1 ** ADAM_STEP)
        v_hat = v2 / (1.0 - ADAM_B2 ** ADAM_STEP)
        d_ref[...] = -ADAM_LR * (m_hat / (jnp.sqrt(v_hat) + ADAM_EPS) + ADAM_WD * w_ref[...])
        nm_ref[...] = m2
        nv_ref[...] = v2

    spec = pl.BlockSpec((br, C), lambda i: (i, 0))
    return pl.pallas_call(
        body, name=name, grid=(R // br,),
        in_specs=[spec] * 4, out_specs=[spec] * 3,
        out_shape=[jax.ShapeDtypeStruct((R, C), F32)] * 3,
        compiler_params=_cp("parallel"),
    )(w, g, m, v)


def _sum8(name, parts):
    _, R, C = parts.shape
    br = _pick(R, (240, 256, 128, 64, 16, 8))

    def body(p_ref, o_ref):
        acc = p_ref[0].astype(F32)
        for d in range(1, N_DEV):
            acc = acc + p_ref[d].astype(F32)
        o_ref[...] = acc

    return pl.pallas_call(
        body, name=name, grid=(R // br,),
        in_specs=[pl.BlockSpec((N_DEV, br, C), lambda i: (0, i, 0))],
        out_specs=pl.BlockSpec((br, C), lambda i: (i, 0)),
        out_shape=jax.ShapeDtypeStruct((R, C), F32),
        compiler_params=_cp("parallel"),
    )(parts)


_ANY = pl.BlockSpec(memory_space=pl.ANY)
_MESH = pl.DeviceIdType.MESH


def _all_gather(name, shard):
    R, C = shard.shape

    def body(x_ref, out_ref, send_sems, recv_sems, local_sem):
        x, y, c = lax.axis_index("x"), lax.axis_index("y"), lax.axis_index("c")
        me, sibling = (x, y, c), (x, y, 1 - c)
        chips = [(1 - x, y), (x, 1 - y), (1 - x, 1 - y)]

        def slot(px, py, pc):
            return out_ref.at[4 * px + 2 * py + pc]

        def copy(k, block, to, src=None):
            return pltpu.make_async_remote_copy(
                src_ref=slot(*block) if src is None else src, dst_ref=slot(*block),
                send_sem=send_sems.at[k], recv_sem=recv_sems.at[k], device_id=to, device_id_type=_MESH)

        mine = pltpu.make_async_copy(x_ref, slot(*me), local_sem)
        mine.start()
        first = [copy(0, me, sibling, src=x_ref)]
        first += [copy(1 + j, me, (*chip, c), src=x_ref) for j, chip in enumerate(chips)]
        for cp in first:
            cp.start()
        passed = [copy(4 + j, (*chip, c), sibling) for j, chip in enumerate(chips)]
        for j, chip in enumerate(chips):
            copy(1 + j, (*chip, c), me).wait_recv()
            passed[j].start()
        copy(0, sibling, me).wait_recv()
        for j, chip in enumerate(chips):
            copy(4 + j, (*chip, 1 - c), me).wait_recv()
        for cp in first + passed:
            cp.wait_send()
        mine.wait()

    return pl.pallas_call(
        body, name=name,
        in_specs=[_ANY], out_specs=_ANY,
        out_shape=jax.ShapeDtypeStruct((N_DEV, R, C), shard.dtype),
        scratch_shapes=[pltpu.SemaphoreType.DMA((7,)), pltpu.SemaphoreType.DMA((7,)), pltpu.SemaphoreType.DMA],
    )(shard)


def _scatter_direct(name, parts):
    _, R, C = parts.shape

    def body(g_ref, land_ref, send_sems, recv_sems, local_sem):
        x, y, c = lax.axis_index("x"), lax.axis_index("y"), lax.axis_index("c")
        me = 4 * x + 2 * y + c
        mine = pltpu.make_async_copy(g_ref.at[me], land_ref.at[me], local_sem)
        mine.start()
        copies = []
        for k in range(1, N_DEV):
            px, py, pc = x ^ ((k >> 2) & 1), y ^ ((k >> 1) & 1), c ^ (k & 1)
            p = 4 * px + 2 * py + pc
            cp = pltpu.make_async_remote_copy(
                src_ref=g_ref.at[p], dst_ref=land_ref.at[me], send_sem=send_sems.at[k - 1],
                recv_sem=recv_sems.at[k - 1], device_id=(px, py, pc), device_id_type=_MESH)
            cp.start()
            copies.append(cp)
        for cp in copies:
            cp.wait()
        mine.wait()

    return pl.pallas_call(
        body, name=name,
        in_specs=[_ANY], out_specs=_ANY,
        out_shape=jax.ShapeDtypeStruct((N_DEV, R, C), parts.dtype),
        scratch_shapes=[pltpu.SemaphoreType.DMA((7,)), pltpu.SemaphoreType.DMA((7,)), pltpu.SemaphoreType.DMA],
    )(parts)


def _rows(a):
    return a.reshape(-1, 1024)


def _rows_to_parts(full):
    n = full.shape[-2] // N_DEV
    t = full.reshape(full.shape[:-2] + (N_DEV, n, full.shape[-1]))
    return jnp.moveaxis(t, -3, 0)


def _parts_to_rows(parts):
    t = jnp.moveaxis(parts, 0, -3)
    return t.reshape(t.shape[:-3] + (t.shape[-3] * t.shape[-2], t.shape[-1]))


def _parts_to_cols(parts):
    t = jnp.moveaxis(parts, 0, -2)
    return t.reshape(t.shape[:-2] + (t.shape[-2] * t.shape[-1],))


def _join(parts, axis=0):
    total = sum(p.shape[axis] for p in parts)
    out, off = None, 0
    for p in parts:
        cfg = [(0, 0)] * p.ndim
        cfg[axis] = (off, total - off - p.shape[axis])
        t = jnp.pad(p, cfg)
        out = t if out is None else out + t
        off += p.shape[axis]
    return out


def _w_in_to_layout(w):
    tail = jnp.pad(w[4096:4104], ((0, PW - COL_TAIL - 8), (0, 0)))
    return jnp.concatenate([w[:4096], w[4104:P_IN], tail], axis=0)


def _w_in_from_layout(g):
    return _join([g[:4096], g[COL_TAIL:COL_TAIL + 8], g[4096:COL_TAIL]], axis=0)


def _block_diag(w):
    w = w.reshape(4, 2, 64, 64)
    return jnp.pad(w[:, 0], ((0, 0), (0, 64), (0, 64))) + jnp.pad(w[:, 1], ((0, 0), (64, 0), (64, 0)))


def _block_diag_grad(g):
    return jnp.stack([g[:, :64, :64], g[:, 64:, 64:]], axis=1).reshape(8, 64, 64)


def _ffn_forward(tag, x, norm, wg, wu, wd):
    h = _rms_fwd(tag + "_norm", x, norm)
    a, b, act = _ffn_up(tag + "_up", h, wg, wu)
    x_out = _mm(tag + "_down", [(act, wd)], "nn", F32, res=x, scale=0.5)
    return x_out, (x, h, a, b, act)


def _ffn_backward(tag, dx_out, saved, norm, wg, wu, wd):
    x, h, a, b, act = saved
    da, db = _ffn_dact(tag + "_dact", dx_out, wd, a, b)
    dwd = _mm(tag + "_dwd", [(act, dx_out)], "tn", F32, scale=0.5, bm=256)
    dwg = _mm(tag + "_dwg", [(da, h)], "tn", F32, bm=256)
    dwu = _mm(tag + "_dwu", [(db, h)], "tn", F32, bm=256)
    dh = _mm(tag + "_dh", [(da, wg), (db, wu)], "nn", F32)
    dx, dnorm = _rms_bwd(tag + "_dnorm", x, norm, dh, dx_out)
    return dx, dnorm, dwg, dwu, dwd


def _mixer_params(p):
    alog = jnp.pad(p["gdn_a_log"], (4, 120))[None]
    dtb = jnp.pad(p["gdn_dt_bias"], (4, 120))[None]
    bias = jnp.repeat(p["sgu_b"].T, 128, axis=1)
    return dict(
        ln_g=p["sgu_ln_g"][None], ln_b=p["sgu_ln_b"][None], sgu_w=p["sgu_w"], sgu_bias=bias,
        lru_cw=p["lru_conv_w"], lru_cb=p["lru_conv_b"][None], wa=_block_diag(p["lru_wa"]), ba=p["lru_ba"][None],
        wx=_block_diag(p["lru_wx"]), bx=p["lru_bx"][None], lam=p["lru_lambda"][None],
        gdn_cw=p["gdn_conv_w"], alog=alog, dtb=dtb, ng=p["gdn_norm_g"][None],
        pool_w=p["pool_w"], pool_sc=p["pool_scale"][None])


def _mix_forward(tag, x, p, mp):
    h = _rms_fwd(tag + "_norm", x, p["mix_norm"][None])
    proj = _mm(tag + "_proj", [(h, p["w_in"])], "nt", F32)
    y_a = _sgu_fwd(tag + "_sgu", proj, mp["ln_g"], mp["ln_b"], mp["sgu_w"], mp["sgu_bias"])
    y_b, hc = _lru_fwd(tag + "_lru", proj, mp["lru_cw"], mp["lru_cb"], mp["wa"], mp["ba"], mp["wx"], mp["bx"],
                       mp["lam"])
    qa = _conv_fwd(tag + "_convq", proj, COL_CQ, mp["gdn_cw"], 0)
    ka = _conv_fwd(tag + "_convk", proj, COL_CK, mp["gdn_cw"], 512)
    va = _conv_fwd(tag + "_convv", proj, COL_CV, mp["gdn_cw"], 1024)
    y_c, sh = _gdn_fwd(tag + "_gdn", qa, ka, va, proj, mp["alog"], mp["dtb"], mp["ng"])
    y_d = _pool_fwd(tag + "_pool", proj, mp["pool_w"], mp["pool_sc"])
    ys = (y_a, y_b, y_c, y_d)
    merged = _merge_fwd(tag + "_merge", ys, p["w_branch"], proj)
    x_out = _mm(tag + "_out", [(merged, p["w_out"])], "nn", F32, res=x)
    return x_out, (x, h, proj, hc, qa, ka, va, sh, ys, merged)


def _mix_backward(tag, dx_out, saved, p, mp):
    x, h, proj, hc, qa, ka, va, sh, ys, merged = saved
    T = x.shape[0]
    g = {}
    dmerged = _mm(tag + "_dmerged", [(dx_out, p["w_out"])], "nt", F32)
    g["w_out"] = _mm(tag + "_dwout", [(merged, dx_out)], "tn", F32)
    outs = _merge_bwd(tag + "_dmerge", dmerged, ys, p["w_branch"], proj)
    dgates, dbrs = outs[:NBR], outs[NBR:]
    dys = [_mm(f"{tag}_dy{i}", [(dbrs[i], p["w_branch"][i])], "nn", F32) for i in range(NBR)]
    g["w_branch"] = jnp.stack([_mm(f"{tag}_dwb{i}", [(dbrs[i], ys[i])], "tn", F32) for i in range(NBR)])

    du, dv, dln_g, dln_b, dsgu_w, dbias = _sgu_bwd(tag + "_dsgu", proj, dys[0], mp["ln_g"], mp["ln_b"], mp["sgu_w"],
                                                  mp["sgu_bias"])
    g["sgu_ln_g"], g["sgu_ln_b"], g["sgu_w"] = dln_g[0], dln_b[0], dsgu_w
    g["sgu_b"] = dbias.reshape(128, 4, 128).sum(axis=2).T

    (dbx, dbg, dcw, dcb, dwa, dba, dwx, dbxb, dlam) = _lru_bwd(
        tag + "_dlru", proj, dys[1], hc, mp["lru_cw"], mp["lru_cb"], mp["wa"], mp["ba"], mp["wx"], mp["bx"], mp["lam"])
    g["lru_conv_w"], g["lru_conv_b"], g["lru_ba"], g["lru_bx"], g["lru_lambda"] = dcw, dcb[0], dba[0], dbxb[0], dlam[0]
    g["lru_wa"], g["lru_wx"] = _block_diag_grad(dwa), _block_diag_grad(dwx)

    dqa, dka, dva, dz, dtail, dalog, ddtb, dng = _gdn_bwd(tag + "_dgdn", qa, ka, va, proj, dys[2], sh, mp["alog"],
                                                         mp["dtb"], mp["ng"])
    g["gdn_a_log"], g["gdn_dt_bias"], g["gdn_norm_g"] = dalog[0, 4:8], ddtb[0, 4:8], dng[0]
    dq, dcwq = _conv_bwd(tag + "_dconvq", proj, COL_CQ, dqa, mp["gdn_cw"], 0)
    dk, dcwk = _conv_bwd(tag + "_dconvk", proj, COL_CK, dka, mp["gdn_cw"], 512)
    dv_, dcwv = _conv_bwd(tag + "_dconvv", proj, COL_CV, dva, mp["gdn_cw"], 1024)
    g["gdn_conv_w"] = jnp.concatenate([dcwq, dcwk, dcwv], axis=1)

    dd, dpw, dsc = _pool_bwd(tag + "_dpool", proj, dys[3], mp["pool_w"], mp["pool_sc"])
    g["pool_w"], g["pool_scale"] = dpw, dsc[0]

    dproj = jnp.concatenate([du, dv, dbx, dbg, dq, dk, dv_, dz, dd, *dgates, dtail,
                             jnp.zeros((T, PW - COL_TAIL - 128), BF16)], axis=1)
    g["w_in"] = _mm(tag + "_dwin", [(dproj, h)], "tn", F32)
    dh = _mm(tag + "_dh", [(dproj, p["w_in"])], "nn", F32)
    dx, dnorm = _rms_bwd(tag + "_dnorm", x, p["mix_norm"][None], dh, dx_out)
    g["mix_norm"] = dnorm[0]
    return dx, g


_BIG = ("ff1_wg", "ff1_wu", "ff1_wd", "w_in", "w_branch", "w_out", "ff2_wg", "ff2_wu", "ff2_wd")
_COL_SHARDED = ("ff1_wg", "ff1_wu", "w_in", "w_branch", "ff2_wg", "ff2_wu")
_SMALL = ("ff1_norm", "mix_norm", "sgu_ln_g", "sgu_ln_b", "sgu_w", "sgu_b", "lru_conv_w", "lru_conv_b", "lru_wa",
          "lru_ba", "lru_wx", "lru_bx", "lru_lambda", "gdn_conv_w", "gdn_a_log", "gdn_dt_bias", "gdn_norm_g", "pool_w",
          "pool_scale", "ff2_norm", "final_norm")
_WEIGHTS = ("ff1_norm", "ff1_wg", "ff1_wu", "ff1_wd", "mix_norm", "w_in", "sgu_ln_g", "sgu_ln_b", "sgu_w", "sgu_b",
            "lru_conv_w", "lru_conv_b", "lru_wa", "lru_ba", "lru_wx", "lru_bx", "lru_lambda", "gdn_conv_w", "gdn_a_log",
            "gdn_dt_bias", "gdn_norm_g", "pool_w", "pool_scale", "w_branch", "w_out", "ff2_norm", "ff2_wg", "ff2_wu",
            "ff2_wd", "final_norm")
_CONV_SHARDED = ("lru_conv_w", "gdn_conv_w")
PACK_ROW_ALIGN = 48


def _pad_rows(a, mult):
    pad = (-a.shape[-2]) % mult
    if pad == 0:
        return a
    return jnp.pad(a, [(0, 0)] * (a.ndim - 2) + [(0, pad), (0, 0)])


def _gather_weights(w):
    shard = {n: (jnp.swapaxes(w[n], -1, -2) if n in _COL_SHARDED else w[n]) for n in _BIG}
    pieces = [_rows(shard[n].astype(BF16)) for n in _BIG]
    sizes = [p.shape[0] for p in pieces]
    pieces = [_pad_rows(p, PACK_ROW_ALIGN) for p in pieces]
    got = _all_gather("gather_weights", jnp.concatenate(pieces, axis=0))
    full, r = {}, 0
    for n, sz, pc in zip(_BIG, sizes, pieces):
        full[n] = _parts_to_rows(got[:, r:r + sz].reshape((N_DEV,) + shard[n].shape))
        r += pc.shape[0]
    conv = _pad_rows(jnp.concatenate([w[n].reshape(1, -1) for n in _CONV_SHARDED], axis=1), 8)
    gconv = _all_gather("gather_conv", conv)[:, 0]
    r = 0
    for n in _CONV_SHARDED:
        sz = w[n].size
        full[n] = _parts_to_cols(gconv[:, r:r + sz].reshape((N_DEV,) + w[n].shape))
        r += sz
    return full


def _layer_params(w, full, l):
    p = {n: w[n][l] for n in _SMALL if n != "final_norm"}
    for n in _BIG:
        p[n] = full[n][l]
    for n in _CONV_SHARDED:
        p[n] = full[n][l]
    p["w_in"] = _w_in_to_layout(p["w_in"])
    return p


def _forward_backward(x, tgt, w, full):
    saved, params = [], []
    for l in range(2):
        p = _layer_params(w, full, l)
        mp = _mixer_params(p)
        x, s1 = _ffn_forward(f"l{l}_ff1", x, p["ff1_norm"][None], p["ff1_wg"], p["ff1_wu"], p["ff1_wd"])
        x, s2 = _mix_forward(f"l{l}_mix", x, p, mp)
        x, s3 = _ffn_forward(f"l{l}_ff2", x, p["ff2_norm"][None], p["ff2_wg"], p["ff2_wu"], p["ff2_wd"])
        saved.append((s1, s2, s3))
        params.append((p, mp))
    loss, dx, dfinal = _final_loss("loss_head", x, w["final_norm"][None], tgt)
    grads = [None, None]
    for l in (1, 0):
        p, mp = params[l]
        s1, s2, s3 = saved[l]
        g = {}
        dx, dn, g["ff2_wg"], g["ff2_wu"], g["ff2_wd"] = _ffn_backward(
            f"l{l}_ff2", dx, s3, p["ff2_norm"][None], p["ff2_wg"], p["ff2_wu"], p["ff2_wd"])
        g["ff2_norm"] = dn[0]
        dx, gm = _mix_backward(f"l{l}_mix", dx, s2, p, mp)
        g.update(gm)
        g["w_in"] = _w_in_from_layout(g["w_in"])
        dx, dn, g["ff1_wg"], g["ff1_wu"], g["ff1_wd"] = _ffn_backward(
            f"l{l}_ff1", dx, s1, p["ff1_norm"][None], p["ff1_wg"], p["ff1_wu"], p["ff1_wd"])
        g["ff1_norm"] = dn[0]
        grads[l] = g
    full_g = {}
    for n in grads[0]:
        if n in _BIG:
            full_g[n] = jnp.stack([grads[0][n], grads[1][n]])
        else:
            full_g[n] = _join([grads[0][n].reshape(-1), grads[1][n].reshape(-1)]).reshape((2,) + grads[0][n].shape)
    full_g["final_norm"] = dfinal[0]
    return loss, dx, full_g


def _reduce_big(full_g, w):
    pieces, sizes = [], []
    for n in _BIG:
        parts = _rows_to_parts(full_g[n].astype(BF16)).reshape(N_DEV, -1, 1024)
        sizes.append(parts.shape[1])
        pieces.append(_pad_rows(parts, PACK_ROW_ALIGN))
    summed = _sum8("sum_big", _scatter_direct("scatter_grads", jnp.concatenate(pieces, axis=1)))
    out, r = {}, 0
    for n, sz, pc in zip(_BIG, sizes, pieces):
        if n in _COL_SHARDED:
            out[n] = jnp.swapaxes(summed[r:r + sz].reshape(w[n].shape[:-2] + w[n].shape[:-3:-1]), -1, -2)
        else:
            out[n] = summed[r:r + sz].reshape(w[n].shape)
        r += pc.shape[1]
    return out


SMALL_PIECE = 8 * 1024


def _pack_small(d, names):
    pieces = []
    for n in names:
        flat = d[n].reshape(-1)
        pieces.append(jnp.pad(flat, (0, (-flat.size) % SMALL_PIECE)).reshape(-1, 1024))
    return jnp.concatenate(pieces, axis=0)


def _unpack_small(pack, shapes, names):
    out, r = {}, 0
    for n in names:
        size = 1
        for s in shapes[n]:
            size *= s
        rows = -(-size // SMALL_PIECE) * 8
        out[n] = pack[r:r + rows].reshape(-1)[:size].reshape(shapes[n])
        r += rows
    return out


def _reduce_small(full_g, loss, w):
    d = dict(full_g)
    d["loss"] = loss[0, :1]
    names = _SMALL + ("loss",)
    summed = _sum8("sum_small", _all_gather("gather_small", _pack_small(d, names)))
    out = _unpack_small(summed, {n: d[n].shape for n in names}, names)
    return out, out["loss"][0]


def _as2d(a):
    if a.ndim == 1:
        return a.reshape(1, -1)
    return a.reshape(-1, a.shape[-1])


def kernel(x, ff1_norm, ff1_wg, ff1_wu, ff1_wd, mix_norm, w_in, sgu_ln_g, sgu_ln_b, sgu_w, sgu_b, lru_conv_w, lru_conv_b, lru_wa, lru_ba, lru_wx, lru_bx, lru_lambda, gdn_conv_w, gdn_a_log, gdn_dt_bias, gdn_norm_g, pool_w, pool_scale, w_branch, w_out, ff2_norm, ff2_wg, ff2_wu, ff2_wd, final_norm, loss_target, m_ff1_norm, m_ff1_wg, m_ff1_wu, m_ff1_wd, m_mix_norm, m_w_in, m_sgu_ln_g, m_sgu_ln_b, m_sgu_w, m_sgu_b, m_lru_conv_w, m_lru_conv_b, m_lru_wa, m_lru_ba, m_lru_wx, m_lru_bx, m_lru_lambda, m_gdn_conv_w, m_gdn_a_log, m_gdn_dt_bias, m_gdn_norm_g, m_pool_w, m_pool_scale, m_w_branch, m_w_out, m_ff2_norm, m_ff2_wg, m_ff2_wu, m_ff2_wd, m_final_norm, v_ff1_norm, v_ff1_wg, v_ff1_wu, v_ff1_wd, v_mix_norm, v_w_in, v_sgu_ln_g, v_sgu_ln_b, v_sgu_w, v_sgu_b, v_lru_conv_w, v_lru_conv_b, v_lru_wa, v_lru_ba, v_lru_wx, v_lru_bx, v_lru_lambda, v_gdn_conv_w, v_gdn_a_log, v_gdn_dt_bias, v_gdn_norm_g, v_pool_w, v_pool_scale, v_w_branch, v_w_out, v_ff2_norm, v_ff2_wg, v_ff2_wu, v_ff2_wd, v_final_norm):
    w = dict(ff1_norm=ff1_norm, ff1_wg=ff1_wg, ff1_wu=ff1_wu, ff1_wd=ff1_wd, mix_norm=mix_norm, w_in=w_in,
             sgu_ln_g=sgu_ln_g, sgu_ln_b=sgu_ln_b, sgu_w=sgu_w, sgu_b=sgu_b, lru_conv_w=lru_conv_w,
             lru_conv_b=lru_conv_b, lru_wa=lru_wa, lru_ba=lru_ba, lru_wx=lru_wx, lru_bx=lru_bx, lru_lambda=lru_lambda,
             gdn_conv_w=gdn_conv_w, gdn_a_log=gdn_a_log, gdn_dt_bias=gdn_dt_bias, gdn_norm_g=gdn_norm_g, pool_w=pool_w,
             pool_scale=pool_scale, w_branch=w_branch, w_out=w_out, ff2_norm=ff2_norm, ff2_wg=ff2_wg, ff2_wu=ff2_wu,
             ff2_wd=ff2_wd, final_norm=final_norm)
    m = dict(ff1_norm=m_ff1_norm, ff1_wg=m_ff1_wg, ff1_wu=m_ff1_wu, ff1_wd=m_ff1_wd, mix_norm=m_mix_norm, w_in=m_w_in,
             sgu_ln_g=m_sgu_ln_g, sgu_ln_b=m_sgu_ln_b, sgu_w=m_sgu_w, sgu_b=m_sgu_b, lru_conv_w=m_lru_conv_w,
             lru_conv_b=m_lru_conv_b, lru_wa=m_lru_wa, lru_ba=m_lru_ba, lru_wx=m_lru_wx, lru_bx=m_lru_bx,
             lru_lambda=m_lru_lambda, gdn_conv_w=m_gdn_conv_w, gdn_a_log=m_gdn_a_log, gdn_dt_bias=m_gdn_dt_bias,
             gdn_norm_g=m_gdn_norm_g, pool_w=m_pool_w, pool_scale=m_pool_scale, w_branch=m_w_branch, w_out=m_w_out,
             ff2_norm=m_ff2_norm, ff2_wg=m_ff2_wg, ff2_wu=m_ff2_wu, ff2_wd=m_ff2_wd, final_norm=m_final_norm)
    v = dict(ff1_norm=v_ff1_norm, ff1_wg=v_ff1_wg, ff1_wu=v_ff1_wu, ff1_wd=v_ff1_wd, mix_norm=v_mix_norm, w_in=v_w_in,
             sgu_ln_g=v_sgu_ln_g, sgu_ln_b=v_sgu_ln_b, sgu_w=v_sgu_w, sgu_b=v_sgu_b, lru_conv_w=v_lru_conv_w,
             lru_conv_b=v_lru_conv_b, lru_wa=v_lru_wa, lru_ba=v_lru_ba, lru_wx=v_lru_wx, lru_bx=v_lru_bx,
             lru_lambda=v_lru_lambda, gdn_conv_w=v_gdn_conv_w, gdn_a_log=v_gdn_a_log, gdn_dt_bias=v_gdn_dt_bias,
             gdn_norm_g=v_gdn_norm_g, pool_w=v_pool_w, pool_scale=v_pool_scale, w_branch=v_w_branch, w_out=v_w_out,
             ff2_norm=v_ff2_norm, ff2_wg=v_ff2_wg, ff2_wu=v_ff2_wu, ff2_wd=v_ff2_wd, final_norm=v_final_norm)

    full = _gather_weights(w)
    T = x.shape[1]
    loss_share, dx, full_g = _forward_backward(x.reshape(T, D), loss_target.reshape(T, D), w, full)
    grad = _reduce_big(full_g, w)
    small, loss = _reduce_small(full_g, loss_share, w)
    me = 4 * lax.axis_index("x") + 2 * lax.axis_index("y") + lax.axis_index("c")
    for n in _SMALL:
        if n in _CONV_SHARDED:
            width = w[n].shape[-1]
            grad[n] = lax.dynamic_slice_in_dim(small[n], me * width, width, axis=2)
        else:
            grad[n] = small[n]

    delta, new_m, new_v = {}, {}, {}
    for n in _BIG:
        d_, m_, v_ = _adamw("adamw_" + n, _as2d(w[n]), _as2d(grad[n]), _as2d(m[n]), _as2d(v[n]))
        delta[n], new_m[n], new_v[n] = (t.reshape(w[n].shape) for t in (d_, m_, v_))

    outs = _adamw("adamw_small", *[_pack_small(t, _SMALL) for t in (w, grad, m, v)])
    for dst, packed in zip((delta, new_m, new_v), outs):
        dst.update(_unpack_small(packed, {n: w[n].shape for n in _SMALL}, _SMALL))

    return (loss, dx.reshape(x.shape), *[grad[n] for n in _WEIGHTS], *[delta[n] for n in _WEIGHTS],
            *[new_m[n] for n in _WEIGHTS], *[new_v[n] for n in _WEIGHTS])
```

```python
import functools

import jax
import jax.numpy as jnp
from jax import lax
from jax.experimental import pallas as pl
from jax.experimental.pallas import tpu as pltpu

F32 = jnp.float32
BF16 = jnp.bfloat16
HI = lax.Precision.HIGHEST

N_DEV = 8
D = 1024
FF = 2816
BW = 512
NBR = 4
CHUNK = 64
EPS = 1e-6
LRU_C = 8.0
GDN_DK = 128

COL_AU, COL_AV, COL_BX, COL_BG = 0, 512, 1024, 1536
COL_CQ, COL_CK, COL_CV, COL_CZ = 2048, 2560, 3072, 3584
COL_DX, COL_GATE, COL_TAIL = 4096, 4608, 8704
PW = 9216
P_IN = 8712

ADAM_LR, ADAM_B1, ADAM_B2, ADAM_EPS, ADAM_WD, ADAM_STEP = 0.001, 0.9, 0.999, 1e-08, 0.01, 10

VMEM_LIMIT_V7X = 56 * 1024 * 1024

_NN = (((1,), (0,)), ((), ()))
_NT = (((1,), (1,)), ((), ()))
_TN = (((0,), (0,)), ((), ()))


def _cp(*sem):
    return pltpu.CompilerParams(dimension_semantics=tuple(sem), vmem_limit_bytes=VMEM_LIMIT_V7X)


def _dot(a, b, dims=_NN):
    return lax.dot_general(a.astype(BF16), b.astype(BF16), dims, preferred_element_type=F32)


def _dot_hi(a, b, dims=_NN):
    return lax.dot_general(a, b, dims, precision=HI, preferred_element_type=F32)


def _pick(n, cands):
    for c in cands:
        if n % c == 0:
            return c
    return n


@jax.custom_jvp
def _log1p(x):
    u = 1.0 + x
    return jnp.where(u == 1.0, x, x * jnp.log(u) / jnp.where(u == 1.0, 1.0, u - 1.0))


@_log1p.defjvp
def _log1p_jvp(p, t):
    (x,), (dx,) = p, t
    return _log1p(x), dx / (1.0 + x)


@jax.custom_jvp
def _expm1(x):
    u = jnp.exp(x)
    lu = jnp.log(u)
    small = (u == 1.0) | (lu == 0.0)
    return jnp.where(small, x, (u - 1.0) * x / jnp.where(small, 1.0, lu))


@_expm1.defjvp
def _expm1_jvp(p, t):
    (x,), (dx,) = p, t
    return _expm1(x), dx * jnp.exp(x)


def _softplus(x):
    return jnp.maximum(x, 0.0) + _log1p(jnp.exp(-jnp.abs(x)))


def _sigmoid(x):
    return jax.nn.sigmoid(x)


def _silu(x):
    return x * jax.nn.sigmoid(x)


def _gelu(x):
    return jax.nn.gelu(x)


@functools.partial(jax.custom_vjp, nondiff_argnums=(1,))
def _shift(x, s):
    return x if s == 0 else pltpu.roll(x, s, 0)


def _shift_fwd(x, s):
    return _shift(x, s), None


def _shift_bwd(s, _, g):
    n = g.shape[0]
    return (g if s == 0 else pltpu.roll(g, n - s, 0),)


_shift.defvjp(_shift_fwd, _shift_bwd)


def _scan_steps(a, b, reverse):
    n = a.shape[0]
    row = lax.broadcasted_iota(jnp.int32, a.shape, 0)
    k = 1
    while k < n:
        sh = n - k if reverse else k
        m = (row < n - k) if reverse else (row >= k)
        a_s = jnp.where(m, pltpu.roll(a, sh, 0), 1.0)
        b_s = jnp.where(m, pltpu.roll(b, sh, 0), 0.0)
        b = a * b_s + b
        a = a * a_s
        k *= 2
    return b


@jax.custom_vjp
def _scan(a, b):
    return _scan_steps(a, b, False)


def _scan_fwd(a, b):
    h = _scan_steps(a, b, False)
    return h, (a, h)


def _scan_bwd(res, dh):
    a, h = res
    n = a.shape[0]
    row = lax.broadcasted_iota(jnp.int32, a.shape, 0)
    a_next = jnp.where(row < n - 1, pltpu.roll(a, n - 1, 0), 0.0)
    g = _scan_steps(a_next, dh, True)
    h_prev = jnp.where(row >= 1, pltpu.roll(h, 1, 0), 0.0)
    return g * h_prev, g


_scan.defvjp(_scan_fwd, _scan_bwd)


def _mm(name, pairs, mode, out_dtype, *, res=None, scale=1.0, bm=None, bn=None, bk=None):
    a0, b0 = pairs[0]
    if mode == "nn":
        (M, K), N = a0.shape, b0.shape[1]
    elif mode == "nt":
        (M, K), N = a0.shape, b0.shape[0]
    else:
        (K, M), N = a0.shape, b0.shape[1]
    bm = bm or _pick(M, (512, 256, 128))
    bn = bn or _pick(N, (512, 256, 128))
    bk = bk or _pick(K, (1024, 512, 1408, 256, 128))
    nk = K // bk
    npair = len(pairs)
    dims = {"nn": _NN, "nt": _NT, "tn": _TN}[mode]

    def body(*refs):
        ab = refs[:2 * npair]
        pos = 2 * npair
        r_ref = None
        if res is not None:
            r_ref = refs[pos]
            pos += 1
        o_ref = refs[pos]
        part = None
        for p in range(npair):
            d = _dot(ab[2 * p][...], ab[2 * p + 1][...], dims)
            part = d if part is None else part + d

        def finish(acc):
            out = acc if scale == 1.0 else acc * scale
            if r_ref is not None:
                out = out + r_ref[...]
            o_ref[...] = out.astype(out_dtype)

        if nk == 1:
            finish(part)
        else:
            acc_ref = refs[pos + 1]
            k = pl.program_id(2)

            @pl.when(k == 0)
            def _():
                acc_ref[...] = part

            @pl.when(k > 0)
            def _():
                acc_ref[...] += part

            @pl.when(k == nk - 1)
            def _():
                finish(acc_ref[...])

    if mode == "nn":
        a_spec = pl.BlockSpec((bm, bk), lambda i, j, k: (i, k))
        b_spec = pl.BlockSpec((bk, bn), lambda i, j, k: (k, j))
    elif mode == "nt":
        a_spec = pl.BlockSpec((bm, bk), lambda i, j, k: (i, k))
        b_spec = pl.BlockSpec((bn, bk), lambda i, j, k: (j, k))
    else:
        a_spec = pl.BlockSpec((bk, bm), lambda i, j, k: (k, i))
        b_spec = pl.BlockSpec((bk, bn), lambda i, j, k: (k, j))
    o_spec = pl.BlockSpec((bm, bn), lambda i, j, k: (i, j))
    in_specs, args = [], []
    for a, b in pairs:
        in_specs += [a_spec, b_spec]
        args += [a, b]
    if res is not None:
        in_specs.append(o_spec)
        args.append(res)
    return pl.pallas_call(
        body, name=name, grid=(M // bm, N // bn, nk),
        in_specs=in_specs, out_specs=o_spec,
        out_shape=jax.ShapeDtypeStruct((M, N), out_dtype),
        scratch_shapes=[pltpu.VMEM((bm, bn), F32)] if nk > 1 else [],
        compiler_params=_cp("parallel", "parallel", "arbitrary"),
    )(*args)


def _rms_fwd(name, x, g):
    T = x.shape[0]
    bm = _pick(T, (512, 256, 128))

    def body(x_ref, g_ref, o_ref):
        xv = x_ref[...]
        r = lax.rsqrt(jnp.mean(xv * xv, axis=-1, keepdims=True) + EPS)
        o_ref[...] = (xv * r * g_ref[...]).astype(BF16)

    return pl.pallas_call(
        body, name=name, grid=(T // bm,),
        in_specs=[pl.BlockSpec((bm, D), lambda i: (i, 0)), pl.BlockSpec((1, D), lambda i: (0, 0))],
        out_specs=pl.BlockSpec((bm, D), lambda i: (i, 0)),
        out_shape=jax.ShapeDtypeStruct((T, D), BF16),
        compiler_params=_cp("parallel"),
    )(x, g)


def _rms_bwd(name, x, g, dh, dres):
    T = x.shape[0]
    bm = _pick(T, (512, 256, 128))

    def body(x_ref, g_ref, dh_ref, dres_ref, dx_ref, dg_ref):
        xv = x_ref[...]
        r = lax.rsqrt(jnp.mean(xv * xv, axis=-1, keepdims=True) + EPS)
        xh = xv * r
        dhv = dh_ref[...]
        dxh = dhv * g_ref[...]
        dx_ref[...] = dres_ref[...] + r * (dxh - xh * jnp.mean(dxh * xh, axis=-1, keepdims=True))
        part = jnp.sum(dhv * xh, axis=0, keepdims=True)

        @pl.when(pl.program_id(0) == 0)
        def _():
            dg_ref[...] = part

        @pl.when(pl.program_id(0) > 0)
        def _():
            dg_ref[...] += part

    row = pl.BlockSpec((bm, D), lambda i: (i, 0))
    vec = pl.BlockSpec((1, D), lambda i: (0, 0))
    return pl.pallas_call(
        body, name=name, grid=(T // bm,),
        in_specs=[row, vec, row, row], out_specs=[row, vec],
        out_shape=[jax.ShapeDtypeStruct((T, D), F32), jax.ShapeDtypeStruct((1, D), F32)],
        compiler_params=_cp("arbitrary"),
    )(x, g, dh, dres)


def _final_loss(name, x, g, tgt):
    T = x.shape[0]
    bm = _pick(T, (512, 256, 128))

    def body(x_ref, g_ref, t_ref, loss_ref, dx_ref, dg_ref):
        xv = x_ref[...]
        gv = g_ref[...]
        r = lax.rsqrt(jnp.mean(xv * xv, axis=-1, keepdims=True) + EPS)
        xh = xv * r
        e = xh * gv - t_ref[...]
        lpart = jnp.broadcast_to(0.5 * jnp.sum(jnp.mean(e * e, axis=-1, keepdims=True), axis=0, keepdims=True), (1, 128))
        dy = e * (1.0 / D)
        dxh = dy * gv
        dx_ref[...] = r * (dxh - xh * jnp.mean(dxh * xh, axis=-1, keepdims=True))
        gpart = jnp.sum(dy * xh, axis=0, keepdims=True)

        @pl.when(pl.program_id(0) == 0)
        def _():
            loss_ref[...] = lpart
            dg_ref[...] = gpart

        @pl.when(pl.program_id(0) > 0)
        def _():
            loss_ref[...] += lpart
            dg_ref[...] += gpart

    row = pl.BlockSpec((bm, D), lambda i: (i, 0))
    vec = pl.BlockSpec((1, D), lambda i: (0, 0))
    return pl.pallas_call(
        body, name=name, grid=(T // bm,),
        in_specs=[row, vec, row],
        out_specs=[pl.BlockSpec((1, 128), lambda i: (0, 0)), row, vec],
        out_shape=[jax.ShapeDtypeStruct((1, 128), F32), jax.ShapeDtypeStruct((T, D), F32),
                   jax.ShapeDtypeStruct((1, D), F32)],
        compiler_params=_cp("arbitrary"),
    )(x, g, tgt)


def _ffn_up(name, h, wg, wu):
    T = h.shape[0]
    bm = _pick(T, (512, 256, 128))
    bn = 256

    def body(h_ref, wg_ref, wu_ref, a_ref, b_ref, act_ref):
        hv = h_ref[...]
        a = _dot(hv, wg_ref[...], _NT)
        b = _dot(hv, wu_ref[...], _NT)
        a_ref[...] = a
        b_ref[...] = b
        act_ref[...] = (_silu(a) * b).astype(BF16)

    w_spec = pl.BlockSpec((bn, D), lambda i, j: (j, 0))
    o_spec = pl.BlockSpec((bm, bn), lambda i, j: (i, j))
    return pl.pallas_call(
        body, name=name, grid=(T // bm, FF // bn),
        in_specs=[pl.BlockSpec((bm, D), lambda i, j: (i, 0)), w_spec, w_spec],
        out_specs=[o_spec, o_spec, o_spec],
        out_shape=[jax.ShapeDtypeStruct((T, FF), F32), jax.ShapeDtypeStruct((T, FF), F32),
                   jax.ShapeDtypeStruct((T, FF), BF16)],
        compiler_params=_cp("parallel", "parallel"),
    )(h, wg, wu)


def _ffn_dact(name, dy, wd, a, b):
    T = dy.shape[0]
    bm = _pick(T, (512, 256, 128))
    bn = 256

    def body(dy_ref, wd_ref, a_ref, b_ref, da_ref, db_ref):
        dact = 0.5 * _dot(dy_ref[...], wd_ref[...], _NT)
        av = a_ref[...]
        s = _sigmoid(av)
        da_ref[...] = (dact * b_ref[...] * (s * (1.0 + av * (1.0 - s)))).astype(BF16)
        db_ref[...] = (dact * (av * s)).astype(BF16)

    t_spec = pl.BlockSpec((bm, bn), lambda i, j: (i, j))
    return pl.pallas_call(
        body, name=name, grid=(T // bm, FF // bn),
        in_specs=[pl.BlockSpec((bm, D), lambda i, j: (i, 0)), pl.BlockSpec((bn, D), lambda i, j: (j, 0)),
                  t_spec, t_spec],
        out_specs=[t_spec, t_spec],
        out_shape=[jax.ShapeDtypeStruct((T, FF), BF16), jax.ShapeDtypeStruct((T, FF), BF16)],
        compiler_params=_cp("parallel", "parallel"),
    )(dy, wd, a, b)


def _merge_specs(T, bm, bn):
    y_spec = pl.BlockSpec((bm, BW), lambda i, j: (i, 0))
    wb_spec = pl.BlockSpec((NBR, bn, BW), lambda i, j: (0, j, 0))
    gate_specs = [pl.BlockSpec((bm, bn), functools.partial(lambda i, j, o: (i, o + j), o=(COL_GATE + g * D) // bn))
                  for g in range(NBR)]
    t_spec = pl.BlockSpec((bm, bn), lambda i, j: (i, j))
    return y_spec, wb_spec, gate_specs, t_spec


def _merge_fwd(name, ys, wb, proj):
    T = proj.shape[0]
    bm = _pick(T, (512, 256, 128))
    bn = 512
    y_spec, wb_spec, gate_specs, t_spec = _merge_specs(T, bm, bn)

    def body(y0, y1, y2, y3, wb_ref, g0, g1, g2, g3, o_ref):
        acc = None
        for g, (y_ref, g_ref) in enumerate(((y0, g0), (y1, g1), (y2, g2), (y3, g3))):
            t = _sigmoid(g_ref[...]) * _dot(y_ref[...], wb_ref[g], _NT)
            acc = t if acc is None else acc + t
        o_ref[...] = acc.astype(BF16)

    return pl.pallas_call(
        body, name=name, grid=(T // bm, D // bn),
        in_specs=[y_spec] * NBR + [wb_spec] + gate_specs, out_specs=t_spec,
        out_shape=jax.ShapeDtypeStruct((T, D), BF16),
        compiler_params=_cp("parallel", "parallel"),
    )(*ys, wb, proj, proj, proj, proj)


def _merge_bwd(name, dm, ys, wb, proj):
    T = proj.shape[0]
    bm = _pick(T, (512, 256, 128))
    bn = 512
    y_spec, wb_spec, gate_specs, t_spec = _merge_specs(T, bm, bn)

    def body(dm_ref, y0, y1, y2, y3, wb_ref, g0, g1, g2, g3, *outs):
        dmv = dm_ref[...]
        for g, (y_ref, g_ref) in enumerate(((y0, g0), (y1, g1), (y2, g2), (y3, g3))):
            br = _dot(y_ref[...], wb_ref[g], _NT)
            s = _sigmoid(g_ref[...])
            outs[g][...] = (dmv * br * (s * (1.0 - s))).astype(BF16)
            outs[NBR + g][...] = (dmv * s).astype(BF16)

    return pl.pallas_call(
        body, name=name, grid=(T // bm, D // bn),
        in_specs=[t_spec] + [y_spec] * NBR + [wb_spec] + gate_specs, out_specs=[t_spec] * (2 * NBR),
        out_shape=[jax.ShapeDtypeStruct((T, D), BF16)] * (2 * NBR),
        compiler_params=_cp("parallel", "parallel"),
    )(dm, *ys, wb, proj, proj, proj, proj)


def _sgu_block(u_pre, v_pre, ln_g, ln_b, w, bias):
    u = _gelu(u_pre)
    vf = _gelu(v_pre)
    mu = jnp.mean(vf, axis=-1, keepdims=True)
    var = jnp.mean(jnp.square(vf - mu), axis=-1, keepdims=True)
    vn = (vf - mu) * lax.rsqrt(var + EPS) * ln_g + ln_b
    ri = lax.broadcasted_iota(jnp.int32, (128, 128), 0)
    ci = lax.broadcasted_iota(jnp.int32, (128, 128), 1)
    mask = (ri // CHUNK) >= (ci // CHUNK)
    outs = [_dot(jnp.where(mask, w[g], 0.0), vn[:, g * 128:(g + 1) * 128]) for g in range(4)]
    mixed = jnp.concatenate(outs, axis=1) + bias
    return u * mixed


def _sgu_param_specs():
    return [pl.BlockSpec((1, BW), lambda i: (0, 0)), pl.BlockSpec((1, BW), lambda i: (0, 0)),
            pl.BlockSpec((4, 128, 128), lambda i: (0, 0, 0)), pl.BlockSpec((128, BW), lambda i: (0, 0))]


def _sgu_fwd(name, proj, ln_g, ln_b, w, bias):
    T = proj.shape[0]
    rb = _pick(T, (256, 128))

    def body(u_ref, v_ref, g_ref, b_ref, w_ref, bias_ref, y_ref):
        for n in range(rb // 128):
            rows = slice(n * 128, (n + 1) * 128)
            y = _sgu_block(u_ref[rows, :], v_ref[rows, :], g_ref[...], b_ref[...], w_ref[...], bias_ref[...])
            y_ref[rows, :] = y.astype(BF16)

    return pl.pallas_call(
        body, name=name, grid=(T // rb,),
        in_specs=[pl.BlockSpec((rb, BW), lambda i: (i, COL_AU // BW)), pl.BlockSpec((rb, BW), lambda i: (i, COL_AV // BW))]
        + _sgu_param_specs(),
        out_specs=pl.BlockSpec((rb, BW), lambda i: (i, 0)),
        out_shape=jax.ShapeDtypeStruct((T, BW), BF16),
        compiler_params=_cp("parallel"),
    )(proj, proj, ln_g, ln_b, w, bias)


def _sgu_bwd(name, proj, dy, ln_g, ln_b, w, bias):
    T = proj.shape[0]
    rb = _pick(T, (256, 128))

    def body(u_ref, v_ref, dy_ref, g_ref, b_ref, w_ref, bias_ref, du_ref, dv_ref, dg_ref, db_ref, dw_ref, dbias_ref):
        acc = None
        for n in range(rb // 128):
            rows = slice(n * 128, (n + 1) * 128)
            _, vjp = jax.vjp(_sgu_block, u_ref[rows, :], v_ref[rows, :], g_ref[...], b_ref[...], w_ref[...],
                             bias_ref[...])
            du, dv, *dp = vjp(dy_ref[rows, :])
            du_ref[rows, :] = du.astype(BF16)
            dv_ref[rows, :] = dv.astype(BF16)
            acc = dp if acc is None else [p + q for p, q in zip(acc, dp)]

        @pl.when(pl.program_id(0) == 0)
        def _():
            for r, p in zip((dg_ref, db_ref, dw_ref, dbias_ref), acc):
                r[...] = p

        @pl.when(pl.program_id(0) > 0)
        def _():
            for r, p in zip((dg_ref, db_ref, dw_ref, dbias_ref), acc):
                r[...] += p

    row = pl.BlockSpec((rb, BW), lambda i: (i, 0))
    return pl.pallas_call(
        body, name=name, grid=(T // rb,),
        in_specs=[pl.BlockSpec((rb, BW), lambda i: (i, COL_AU // BW)), pl.BlockSpec((rb, BW), lambda i: (i, COL_AV // BW)),
                  row] + _sgu_param_specs(),
        out_specs=[row, row] + _sgu_param_specs(),
        out_shape=[jax.ShapeDtypeStruct((T, BW), BF16), jax.ShapeDtypeStruct((T, BW), BF16),
                   jax.ShapeDtypeStruct((1, BW), F32), jax.ShapeDtypeStruct((1, BW), F32),
                   jax.ShapeDtypeStruct((4, 128, 128), F32), jax.ShapeDtypeStruct((128, BW), F32)],
        compiler_params=_cp("arbitrary"),
    )(proj, proj, dy, ln_g, ln_b, w, bias)


def _halo_block(ref, i, rblk, halo):
    r0 = pl.multiple_of(i * rblk, rblk)
    h0 = pl.multiple_of(jnp.maximum(r0 - halo, 0), halo)
    top = jnp.where(i > 0, ref[pl.ds(h0, halo), :], 0.0)
    return jnp.concatenate([top, ref[pl.ds(r0, rblk), :]], axis=0)


def _with_halo_grad(dfull, pending, halo, rblk):
    tail = jnp.concatenate([jnp.zeros((rblk - halo, 128), F32), pending], axis=0)
    return dfull[halo:] + tail


def _conv4(xfull, rows):
    acc = None
    for k in range(4):
        t = rows[k] * _shift(xfull, 3 - k)[8:]
        acc = t if acc is None else acc + t
    return acc


def _lru_block(xfull, gate, h0, c0, c1, c2, c3, cb, wa, ba, wx, bx, lam):
    n = gate.shape[0]
    xc = _conv4(xfull, (c0, c1, c2, c3)) + cb
    r = _sigmoid(_dot(xc, wa) + ba)
    ig = _sigmoid(_dot(xc, wx) + bx)
    log_a = -LRU_C * r * _softplus(-lam)
    a = jnp.exp(log_a)
    mult = jnp.sqrt(-_expm1(2.0 * log_a))
    b = mult * (ig * xc)
    row = lax.broadcasted_iota(jnp.int32, (n, 128), 0)
    b = b + jnp.where(row == 0, a * h0, 0.0)
    h = _scan(a, b)
    out = h * _gelu(gate)
    h_last = jnp.sum(jnp.where(row == n - 1, h, 0.0), axis=0, keepdims=True)
    return out, h_last


def _lru_param_specs():
    vec = pl.BlockSpec((1, 128), lambda g: (0, g))
    mat = pl.BlockSpec((None, 128, 128), lambda g: (g, 0, 0))
    return [pl.BlockSpec((4, 128), lambda g: (0, g)), vec, mat, vec, mat, vec, vec]


def _lru_load_params(cw_ref, cb_ref, wa_ref, ba_ref, wx_ref, bx_ref, lam_ref):
    return (cw_ref[0:1, :], cw_ref[1:2, :], cw_ref[2:3, :], cw_ref[3:4, :], cb_ref[...], wa_ref[...], ba_ref[...],
            wx_ref[...], bx_ref[...], lam_ref[...])


def _lru_fwd(name, proj, cw, cb, wa, ba, wx, bx, lam):
    T = proj.shape[0]
    rblk = _pick(T, (256, 128))
    nblk = T // rblk

    def body(x_ref, gt_ref, cw_ref, cb_ref, wa_ref, ba_ref, wx_ref, bx_ref, lam_ref, y_ref, hc_ref):
        params = _lru_load_params(cw_ref, cb_ref, wa_ref, ba_ref, wx_ref, bx_ref, lam_ref)

        def step(i, h0):
            r0 = pl.multiple_of(i * rblk, rblk)
            out, h_last = _lru_block(_halo_block(x_ref, i, rblk, 8), gt_ref[pl.ds(r0, rblk), :], h0, *params)
            y_ref[pl.ds(r0, rblk), :] = out.astype(BF16)
            hc_ref[pl.ds(pl.multiple_of(i * 8, 8), 8), :] = jnp.broadcast_to(h0, (8, 128))
            return h_last

        lax.fori_loop(0, nblk, step, jnp.zeros((1, 128), F32))

    return pl.pallas_call(
        body, name=name, grid=(4,),
        in_specs=[pl.BlockSpec((T, 128), lambda g: (0, COL_BX // 128 + g)),
                  pl.BlockSpec((T, 128), lambda g: (0, COL_BG // 128 + g))] + _lru_param_specs(),
        out_specs=[pl.BlockSpec((T, 128), lambda g: (0, g)), pl.BlockSpec((nblk * 8, 128), lambda g: (0, g))],
        out_shape=[jax.ShapeDtypeStruct((T, BW), BF16), jax.ShapeDtypeStruct((nblk * 8, BW), F32)],
        compiler_params=_cp("parallel"),
    )(proj, proj, cw, cb, wa, ba, wx, bx, lam)


def _lru_bwd(name, proj, dy, hc, cw, cb, wa, ba, wx, bx, lam):
    T = proj.shape[0]
    rblk = _pick(T, (256, 128))
    nblk = T // rblk

    def body(x_ref, gt_ref, dy_ref, hc_ref, cw_ref, cb_ref, wa_ref, ba_ref, wx_ref, bx_ref, lam_ref,
             dx_ref, dgt_ref, dcw_ref, dcb_ref, dwa_ref, dba_ref, dwx_ref, dbx_ref, dlam_ref):
        params = _lru_load_params(cw_ref, cb_ref, wa_ref, ba_ref, wx_ref, bx_ref, lam_ref)

        def step(it, carry):
            dh_last, pending, acc = carry
            i = nblk - 1 - it
            r0 = pl.multiple_of(i * rblk, rblk)
            h0 = hc_ref[pl.ds(pl.multiple_of(i * 8, 8), 1), :]
            _, vjp = jax.vjp(_lru_block, _halo_block(x_ref, i, rblk, 8), gt_ref[pl.ds(r0, rblk), :], h0, *params)
            dfull, dgate, dh0, *dp = vjp((dy_ref[pl.ds(r0, rblk), :], dh_last))
            dx_ref[pl.ds(r0, rblk), :] = _with_halo_grad(dfull, pending, 8, rblk).astype(BF16)
            dgt_ref[pl.ds(r0, rblk), :] = dgate.astype(BF16)
            return dh0, dfull[:8], tuple(p + q for p, q in zip(acc, dp))

        zeros = tuple(jnp.zeros(p.shape, F32) for p in params)
        _, _, acc = lax.fori_loop(0, nblk, step, (jnp.zeros((1, 128), F32), jnp.zeros((8, 128), F32), zeros))
        for k in range(4):
            dcw_ref[k:k + 1, :] = acc[k]
        for r, p in zip((dcb_ref, dwa_ref, dba_ref, dwx_ref, dbx_ref, dlam_ref), acc[4:]):
            r[...] = p

    col = pl.BlockSpec((T, 128), lambda g: (0, g))
    return pl.pallas_call(
        body, name=name, grid=(4,),
        in_specs=[pl.BlockSpec((T, 128), lambda g: (0, COL_BX // 128 + g)),
                  pl.BlockSpec((T, 128), lambda g: (0, COL_BG // 128 + g)), col,
                  pl.BlockSpec((nblk * 8, 128), lambda g: (0, g))] + _lru_param_specs(),
        out_specs=[col, col] + _lru_param_specs(),
        out_shape=[jax.ShapeDtypeStruct((T, BW), BF16), jax.ShapeDtypeStruct((T, BW), BF16),
                   jax.ShapeDtypeStruct((4, BW), F32), jax.ShapeDtypeStruct((1, BW), F32),
                   jax.ShapeDtypeStruct((4, 128, 128), F32), jax.ShapeDtypeStruct((1, BW), F32),
                   jax.ShapeDtypeStruct((4, 128, 128), F32), jax.ShapeDtypeStruct((1, BW), F32),
                   jax.ShapeDtypeStruct((1, BW), F32)],
        compiler_params=_cp("parallel"),
    )(proj, proj, dy, hc, cw, cb, wa, ba, wx, bx, lam)


def _conv_block(xfull, c0, c1, c2, c3):
    return _silu(_conv4(xfull, (c0, c1, c2, c3)))


def _conv_fwd(name, proj, col0, cw, cw_col0):
    T = proj.shape[0]
    rblk = _pick(T, (256, 128))
    nblk = T // rblk

    def body(x_ref, cw_ref, y_ref):
        rows = (cw_ref[0:1, :], cw_ref[1:2, :], cw_ref[2:3, :], cw_ref[3:4, :])

        def step(i, c):
            r0 = pl.multiple_of(i * rblk, rblk)
            y_ref[pl.ds(r0, rblk), :] = _conv_block(_halo_block(x_ref, i, rblk, 8), *rows)
            return c

        lax.fori_loop(0, nblk, step, 0)

    return pl.pallas_call(
        body, name=name, grid=(4,),
        in_specs=[pl.BlockSpec((T, 128), lambda g: (0, col0 // 128 + g)),
                  pl.BlockSpec((4, 128), lambda g: (0, cw_col0 // 128 + g))],
        out_specs=pl.BlockSpec((T, 128), lambda g: (0, g)),
        out_shape=jax.ShapeDtypeStruct((T, BW), F32),
        compiler_params=_cp("parallel"),
    )(proj, cw)


def _conv_bwd(name, proj, col0, dy, cw, cw_col0):
    T = proj.shape[0]
    rblk = _pick(T, (256, 128))
    nblk = T // rblk

    def body(x_ref, dy_ref, cw_ref, dx_ref, dcw_ref):
        rows = (cw_ref[0:1, :], cw_ref[1:2, :], cw_ref[2:3, :], cw_ref[3:4, :])

        def step(it, carry):
            pending, acc = carry
            i = nblk - 1 - it
            r0 = pl.multiple_of(i * rblk, rblk)
            _, vjp = jax.vjp(_conv_block, _halo_block(x_ref, i, rblk, 8), *rows)
            dfull, *dp = vjp(dy_ref[pl.ds(r0, rblk), :])
            dx_ref[pl.ds(r0, rblk), :] = _with_halo_grad(dfull, pending, 8, rblk).astype(BF16)
            return dfull[:8], tuple(p + q for p, q in zip(acc, dp))

        zeros = tuple(jnp.zeros((1, 128), F32) for _ in range(4))
        _, acc = lax.fori_loop(0, nblk, step, (jnp.zeros((8, 128), F32), zeros))
        for k in range(4):
            dcw_ref[k:k + 1, :] = acc[k]

    col = pl.BlockSpec((T, 128), lambda g: (0, g))
    return pl.pallas_call(
        body, name=name, grid=(4,),
        in_specs=[pl.BlockSpec((T, 128), lambda g: (0, col0 // 128 + g)), col,
                  pl.BlockSpec((4, 128), lambda g: (0, cw_col0 // 128 + g))],
        out_specs=[col, pl.BlockSpec((4, 128), lambda g: (0, g))],
        out_shape=[jax.ShapeDtypeStruct((T, BW), BF16), jax.ShapeDtypeStruct((4, BW), F32)],
        compiler_params=_cp("parallel"),
    )(proj, dy, cw)


def _pool_block(xfull, pw, sc, t0, gi):
    n = xfull.shape[0] - 16
    s2 = xfull + _shift(xfull, 1)
    s4 = s2 + _shift(s2, 2)
    s8 = s4 + _shift(s4, 4)
    s16 = s8 + _shift(s8, 8)
    s = jnp.where(gi == 0, s2, jnp.where(gi == 1, s4, jnp.where(gi == 2, s8, s16)))[16:]
    t = t0 + lax.broadcasted_iota(jnp.int32, (n, 128), 0)
    cnt = jnp.minimum(t + 1, lax.shift_left(jnp.int32(2), gi)).astype(F32)
    pooled = s / cnt - xfull[16:]
    return _dot(pooled, pw) * sc


def _pool_fwd(name, proj, pw, sc):
    T = proj.shape[0]
    rblk = _pick(T, (256, 128))
    nblk = T // rblk

    def body(x_ref, pw_ref, sc_ref, y_ref):
        gi = pl.program_id(0)

        def step(i, c):
            r0 = pl.multiple_of(i * rblk, rblk)
            y = _pool_block(_halo_block(x_ref, i, rblk, 16), pw_ref[...], sc_ref[...], r0, gi)
            y_ref[pl.ds(r0, rblk), :] = y.astype(BF16)
            return c

        lax.fori_loop(0, nblk, step, 0)

    return pl.pallas_call(
        body, name=name, grid=(4,),
        in_specs=[pl.BlockSpec((T, 128), lambda g: (0, COL_DX // 128 + g)),
                  pl.BlockSpec((None, 128, 128), lambda g: (g, 0, 0)), pl.BlockSpec((1, 128), lambda g: (0, g))],
        out_specs=pl.BlockSpec((T, 128), lambda g: (0, g)),
        out_shape=jax.ShapeDtypeStruct((T, BW), BF16),
        compiler_params=_cp("parallel"),
    )(proj, pw, sc)


def _pool_bwd(name, proj, dy, pw, sc):
    T = proj.shape[0]
    rblk = _pick(T, (256, 128))
    nblk = T // rblk

    def body(x_ref, dy_ref, pw_ref, sc_ref, dx_ref, dpw_ref, dsc_ref):
        gi = pl.program_id(0)

        def step(it, carry):
            pending, apw, asc = carry
            i = nblk - 1 - it
            r0 = pl.multiple_of(i * rblk, rblk)
            _, vjp = jax.vjp(lambda xf, w, s: _pool_block(xf, w, s, r0, gi), _halo_block(x_ref, i, rblk, 16),
                             pw_ref[...], sc_ref[...])
            dfull, dw, ds = vjp(dy_ref[pl.ds(r0, rblk), :])
            dx_ref[pl.ds(r0, rblk), :] = _with_halo_grad(dfull, pending, 16, rblk).astype(BF16)
            return dfull[:16], apw + dw, asc + ds

        _, apw, asc = lax.fori_loop(0, nblk, step, (jnp.zeros((16, 128), F32), jnp.zeros((128, 128), F32),
                                                    jnp.zeros((1, 128), F32)))
        dpw_ref[...] = apw
        dsc_ref[...] = asc

    col = pl.BlockSpec((T, 128), lambda g: (0, g))
    mat = pl.BlockSpec((None, 128, 128), lambda g: (g, 0, 0))
    vec = pl.BlockSpec((1, 128), lambda g: (0, g))
    return pl.pallas_call(
        body, name=name, grid=(4,),
        in_specs=[pl.BlockSpec((T, 128), lambda g: (0, COL_DX // 128 + g)), col, mat, vec],
        out_specs=[col, mat, vec],
        out_shape=[jax.ShapeDtypeStruct((T, BW), BF16), jax.ShapeDtypeStruct((4, 128, 128), F32),
                   jax.ShapeDtypeStruct((1, BW), F32)],
        compiler_params=_cp("parallel"),
    )(proj, dy, pw, sc)


def _dot3(a, b):
    ah = a.astype(BF16)
    al = (a - ah.astype(F32)).astype(BF16)
    bh = b.astype(BF16)
    bl = (b - bh.astype(F32)).astype(BF16)

    def d(x, y):
        return lax.dot_general(x, y, _NN, preferred_element_type=F32)

    return d(ah, bh) + (d(ah, bl) + d(al, bh))


def _tri_inv(mats):
    n = mats[0].shape[0]
    eye = (lax.broadcasted_iota(jnp.int32, (n, n), 0) == lax.broadcasted_iota(jnp.int32, (n, n), 1)).astype(F32)
    ps = [eye - a for a in mats]
    ms = list(mats)
    k = 2
    while k < n:
        ms = [_dot3(m, m) for m in ms]
        ps = [p + _dot3(p, m) for p, m in zip(ps, ms)]
        k *= 2
    return ps


def _cumsum_rows(x):
    n = x.shape[0]
    row = lax.broadcasted_iota(jnp.int32, x.shape, 0)
    k = 1
    while k < n:
        x = x + jnp.where(row >= k, _shift(x, k), 0.0)
        k *= 2
    return x


def _gdn_chunk(states, qc, kc, vc, z, tail, alog, dtb, ng):
    C, H = CHUNK, 4
    hs = range(H)
    lane = lax.broadcasted_iota(jnp.int32, (C, 128), 1)
    row = lax.broadcasted_iota(jnp.int32, (C, 128), 0)
    ri = lax.broadcasted_iota(jnp.int32, (C, C), 0)
    ci = lax.broadcasted_iota(jnp.int32, (C, C), 1)
    incl = ri >= ci
    sig = _sigmoid(tail)
    gfull = -jnp.exp(alog) * _softplus(tail + dtb)
    beta = [jnp.sum(jnp.where(lane == h, sig, 0.0), axis=1, keepdims=True) for h in hs]
    g = [jnp.sum(jnp.where(lane == h + 4, gfull, 0.0), axis=1, keepdims=True) for h in hs]
    qs = [qc[:, h * 128:(h + 1) * 128] for h in hs]
    ks = [kc[:, h * 128:(h + 1) * 128] for h in hs]
    vs = [vc[:, h * 128:(h + 1) * 128] for h in hs]
    q = [t * lax.rsqrt(jnp.sum(t * t, axis=-1, keepdims=True) + EPS) * (GDN_DK ** -0.5) for t in qs]
    k = [t * lax.rsqrt(jnp.sum(t * t, axis=-1, keepdims=True) + EPS) for t in ks]
    gc = [_cumsum_rows(jnp.broadcast_to(t, (C, 128))) for t in g]
    gc_row = [jnp.transpose(t)[:C, :] for t in gc]
    gc_col = [jnp.sum(jnp.where(lane == 0, t, 0.0), axis=1, keepdims=True) for t in gc]
    decay = [jnp.exp(jnp.where(incl, gc_col[h] - gc_row[h], -1e30)) for h in hs]
    kb = [k[h] * beta[h] for h in hs]
    kk = [_dot(kb[h], k[h], _NT) for h in hs]
    t_mat = _tri_inv([jnp.where(ri > ci, kk[h] * decay[h], 0.0) for h in hs])
    egc = [jnp.exp(t) for t in gc]
    u = [_dot(t_mat[h], vs[h] * beta[h]) for h in hs]
    w = [_dot(t_mat[h], kb[h] * egc[h]) for h in hs]
    qk = [_dot(q[h], k[h], _NT) for h in hs]
    attn = [jnp.where(incl, qk[h] * decay[h], 0.0) for h in hs]
    ws = [_dot(w[h], states[h]) for h in hs]
    qs_ = [_dot(q[h] * egc[h], states[h]) for h in hs]
    v_new = [u[h] - ws[h] for h in hs]
    av = [_dot(attn[h], v_new[h]) for h in hs]
    g_last = [jnp.sum(jnp.where(row == C - 1, t, 0.0), axis=0, keepdims=True) for t in gc]
    kv = [_dot(k[h] * jnp.exp(g_last[h] - gc[h]), v_new[h], _TN) for h in hs]
    nxt = tuple(states[h] * jnp.exp(g_last[h]) + kv[h] for h in hs)
    o = [qs_[h] + av[h] for h in hs]
    on = [t * lax.rsqrt(jnp.mean(t * t, axis=-1, keepdims=True) + EPS) * ng for t in o]
    return nxt, jnp.concatenate(on, axis=1) * _silu(z)


def _gdn_blocks(T):
    tb = _pick(T, (512, 256, 128, 64))
    return tb, T // tb, tb // CHUNK


def _gdn_fwd(name, qa, ka, va, proj, alog, dtb, ng):
    T = proj.shape[0]
    tb, nb, ncb = _gdn_blocks(T)

    def body(q_ref, k_ref, v_ref, z_ref, tail_ref, alog_ref, dtb_ref, ng_ref, y_ref, sh_ref, state):
        @pl.when(pl.program_id(0) == 0)
        def _():
            state[...] = jnp.zeros((4, 128, 128), F32)

        def step(c, states):
            rows = pl.ds(pl.multiple_of(c * CHUNK, CHUNK), CHUNK)
            for h in range(4):
                sh_ref[h, c] = states[h]
            nxt, y = _gdn_chunk(states, q_ref[rows, :], k_ref[rows, :], v_ref[rows, :], z_ref[rows, :],
                                tail_ref[rows, :], alog_ref[...], dtb_ref[...], ng_ref[...])
            y_ref[rows, :] = y.astype(BF16)
            return nxt

        states = lax.fori_loop(0, ncb, step, tuple(state[h] for h in range(4)))
        for h in range(4):
            state[h] = states[h]

    blk = pl.BlockSpec((tb, BW), lambda j: (j, 0))
    vec = pl.BlockSpec((1, 128), lambda j: (0, 0))
    return pl.pallas_call(
        body, name=name, grid=(nb,),
        in_specs=[blk, blk, blk, pl.BlockSpec((tb, BW), lambda j: (j, COL_CZ // BW)),
                  pl.BlockSpec((tb, 128), lambda j: (j, COL_TAIL // 128)), vec, vec, vec],
        out_specs=[blk, pl.BlockSpec((4, ncb, 128, 128), lambda j: (0, j, 0, 0))],
        out_shape=[jax.ShapeDtypeStruct((T, BW), BF16), jax.ShapeDtypeStruct((4, T // CHUNK, 128, 128), F32)],
        scratch_shapes=[pltpu.VMEM((4, 128, 128), F32)],
        compiler_params=_cp("arbitrary"),
    )(qa, ka, va, proj, proj, alog, dtb, ng)


def _gdn_bwd(name, qa, ka, va, proj, dy, sh, alog, dtb, ng):
    T = proj.shape[0]
    tb, nb, ncb = _gdn_blocks(T)

    def body(q_ref, k_ref, v_ref, z_ref, tail_ref, dy_ref, sh_ref, alog_ref, dtb_ref, ng_ref,
             dq_ref, dk_ref, dv_ref, dz_ref, dtail_ref, dalog_ref, ddtb_ref, dng_ref, dstate):
        first = pl.program_id(0) == 0

        @pl.when(first)
        def _():
            dstate[...] = jnp.zeros((4, 128, 128), F32)

        def step(it, carry):
            dstates, pa, pd, pn = carry
            c = ncb - 1 - it
            rows = pl.ds(pl.multiple_of(c * CHUNK, CHUNK), CHUNK)
            _, vjp = jax.vjp(_gdn_chunk, tuple(sh_ref[h, c] for h in range(4)), q_ref[rows, :], k_ref[rows, :],
                             v_ref[rows, :], z_ref[rows, :], tail_ref[rows, :], alog_ref[...], dtb_ref[...], ng_ref[...])
            nxt, dq, dk, dv, dz, dtail, da, dd, dn = vjp((dstates, dy_ref[rows, :]))
            dq_ref[rows, :] = dq
            dk_ref[rows, :] = dk
            dv_ref[rows, :] = dv
            dz_ref[rows, :] = dz.astype(BF16)
            dtail_ref[rows, :] = dtail.astype(BF16)
            return nxt, pa + da, pd + dd, pn + dn

        zv = jnp.zeros((1, 128), F32)
        dstates, pa, pd, pn = lax.fori_loop(0, ncb, step, (tuple(dstate[h] for h in range(4)), zv, zv, zv))
        for h in range(4):
            dstate[h] = dstates[h]

        @pl.when(first)
        def _():
            dalog_ref[...] = pa
            ddtb_ref[...] = pd
            dng_ref[...] = pn

        @pl.when(jnp.logical_not(first))
        def _():
            dalog_ref[...] += pa
            ddtb_ref[...] += pd
            dng_ref[...] += pn

    blk = pl.BlockSpec((tb, BW), lambda j: (nb - 1 - j, 0))
    vec = pl.BlockSpec((1, 128), lambda j: (0, 0))
    return pl.pallas_call(
        body, name=name, grid=(nb,),
        in_specs=[blk, blk, blk, pl.BlockSpec((tb, BW), lambda j: (nb - 1 - j, COL_CZ // BW)),
                  pl.BlockSpec((tb, 128), lambda j: (nb - 1 - j, COL_TAIL // 128)), blk,
                  pl.BlockSpec((4, ncb, 128, 128), lambda j: (0, nb - 1 - j, 0, 0)), vec, vec, vec],
        out_specs=[blk, blk, blk, blk, pl.BlockSpec((tb, 128), lambda j: (nb - 1 - j, 0)), vec, vec, vec],
        out_shape=[jax.ShapeDtypeStruct((T, BW), F32)] * 3
        + [jax.ShapeDtypeStruct((T, BW), BF16), jax.ShapeDtypeStruct((T, 128), BF16)]
        + [jax.ShapeDtypeStruct((1, 128), F32)] * 3,
        scratch_shapes=[pltpu.VMEM((4, 128, 128), F32)],
        compiler_params=_cp("arbitrary"),
    )(qa, ka, va, proj, proj, dy, sh, alog, dtb, ng)


def _adamw(name, w, g, m, v):
    R, C = w.shape
    br = _pick(R, (512, 256, 240, 128, 64, 8))

    def body(w_ref, g_ref, m_ref, v_ref, d_ref, nm_ref, nv_ref):
        gv = g_ref[...]
        m2 = ADAM_B1 * m_ref[...] + (1.0 - ADAM_B1) * gv
        v2 = ADAM_B2 * v_ref[...] + (1.0 - ADAM_B2) * jnp.square(gv)
        m_hat = m2 / (1.0 - ADAM_B1 ** ADAM_STEP)
        v_hat = v2 / (1.0 - ADAM_B2 ** ADAM_STEP)
        d_ref[...] = -ADAM_LR * (m_hat / (jnp.sqrt(v_hat) + ADAM_EPS) + ADAM_WD * w_ref[...])
        nm_ref[...] = m2
        nv_ref[...] = v2

    spec = pl.BlockSpec((br, C), lambda i: (i, 0))
    return pl.pallas_call(
        body, name=name, grid=(R // br,),
        in_specs=[spec] * 4, out_specs=[spec] * 3,
        out_shape=[jax.ShapeDtypeStruct((R, C), F32)] * 3,
        compiler_params=_cp("parallel"),
    )(w, g, m, v)


def _sum8(name, parts):
    _, R, C = parts.shape
    br = _pick(R, (352, 496, 256, 128, 64, 16, 8))

    def body(p_ref, o_ref):
        acc = p_ref[0].astype(F32)
        for d in range(1, N_DEV):
            acc = acc + p_ref[d].astype(F32)
        o_ref[...] = acc

    return pl.pallas_call(
        body, name=name, grid=(R // br,),
        in_specs=[pl.BlockSpec((N_DEV, br, C), lambda i: (0, i, 0))],
        out_specs=pl.BlockSpec((br, C), lambda i: (i, 0)),
        out_shape=jax.ShapeDtypeStruct((R, C), F32),
        compiler_params=_cp("parallel"),
    )(parts)


_ANY = pl.BlockSpec(memory_space=pl.ANY)
_MESH = pl.DeviceIdType.MESH


def _all_gather(name, shard):
    R, C = shard.shape

    def body(x_ref, out_ref, send_sems, recv_sems, local_sem):
        x, y, c = lax.axis_index("x"), lax.axis_index("y"), lax.axis_index("c")
        me, sibling = (x, y, c), (x, y, 1 - c)
        chips = [(1 - x, y), (x, 1 - y), (1 - x, 1 - y)]

        def slot(px, py, pc):
            return out_ref.at[4 * px + 2 * py + pc]

        def copy(k, block, to, src=None):
            return pltpu.make_async_remote_copy(
                src_ref=slot(*block) if src is None else src, dst_ref=slot(*block),
                send_sem=send_sems.at[k], recv_sem=recv_sems.at[k], device_id=to, device_id_type=_MESH)

        mine = pltpu.make_async_copy(x_ref, slot(*me), local_sem)
        mine.start()
        first = [copy(0, me, sibling, src=x_ref)]
        first += [copy(1 + j, me, (*chip, c), src=x_ref) for j, chip in enumerate(chips)]
        for cp in first:
            cp.start()
        passed = [copy(4 + j, (*chip, c), sibling) for j, chip in enumerate(chips)]
        for j, chip in enumerate(chips):
            copy(1 + j, (*chip, c), me).wait_recv()
            passed[j].start()
        copy(0, sibling, me).wait_recv()
        for j, chip in enumerate(chips):
            copy(4 + j, (*chip, 1 - c), me).wait_recv()
        for cp in first + passed:
            cp.wait_send()
        mine.wait()

    return pl.pallas_call(
        body, name=name,
        in_specs=[_ANY], out_specs=_ANY,
        out_shape=jax.ShapeDtypeStruct((N_DEV, R, C), shard.dtype),
        scratch_shapes=[pltpu.SemaphoreType.DMA((7,)), pltpu.SemaphoreType.DMA((7,)), pltpu.SemaphoreType.DMA],
    )(shard)


_HBM = pl.BlockSpec(memory_space=pltpu.HBM)
_SEM = pl.BlockSpec(memory_space=pltpu.SEMAPHORE)
_EFFECT = pltpu.SideEffectType.DATAFLOW_SIDE_EFFECTING


def _exchange_copies(src_ref, land_ref, send_sems, recv_sems, scatter):
    x, y, c = lax.axis_index("x"), lax.axis_index("y"), lax.axis_index("c")
    me = 4 * x + 2 * y + c
    copies = []
    for k in range(1, N_DEV):
        px, py, pc = x ^ ((k >> 2) & 1), y ^ ((k >> 1) & 1), c ^ (k & 1)
        src = src_ref.at[4 * px + 2 * py + pc] if scatter else src_ref
        copies.append(pltpu.make_async_remote_copy(
            src_ref=src, dst_ref=land_ref.at[me], send_sem=send_sems.at[k - 1], recv_sem=recv_sems.at[k - 1],
            device_id=(px, py, pc), device_id_type=_MESH))
    return copies


def _exchange_start(name, srcs, lands, scatter):
    n = len(srcs)

    def body(*refs):
        src_refs, land_refs = refs[:n], refs[n:2 * n]
        send, recv = refs[2 * n:3 * n], refs[3 * n:4 * n]
        token = refs[-1]
        for g in range(n):
            for cp in _exchange_copies(src_refs[g], land_refs[g], send[g], recv[g], scatter):
                cp.start()
        token[...] = jnp.zeros_like(token)

    outs = pl.pallas_call(
        body, name=name,
        out_shape=tuple([pltpu.SemaphoreType.DMA((N_DEV - 1,))] * (2 * n)
                        + [pltpu.HBM(a.shape, a.dtype) for a in list(srcs) + list(lands)]
                        + [jax.ShapeDtypeStruct((8, 128), F32)]),
        in_specs=[_HBM] * (2 * n),
        out_specs=tuple([_SEM] * (2 * n) + [_HBM] * (2 * n) + [pl.BlockSpec(memory_space=pltpu.VMEM)]),
        input_output_aliases={i: 2 * n + i for i in range(2 * n)},
        compiler_params=pltpu.CompilerParams(has_side_effects=_EFFECT),
    )(*[pltpu.with_memory_space_constraint(a, pltpu.HBM) for a in list(srcs) + list(lands)])
    handles = [(outs[2 * n + g], outs[3 * n + g], outs[g], outs[n + g]) for g in range(n)]
    return handles, outs[-1]


def _exchange_wait(name, src, land, send_sems, recv_sems, after, scatter):
    def body(src_ref, land_ref, send, recv, after_ref, src_out, land_out):
        for cp in _exchange_copies(src_ref, land_ref, send, recv, scatter):
            cp.wait_send()
            cp.wait_recv()

    return pl.pallas_call(
        body, name=name,
        out_shape=(pltpu.HBM(src.shape, src.dtype), pltpu.HBM(land.shape, land.dtype)),
        in_specs=(_HBM, _HBM, _SEM, _SEM, _ANY), out_specs=(_HBM, _HBM),
        input_output_aliases={0: 0, 1: 1},
        compiler_params=pltpu.CompilerParams(has_side_effects=_EFFECT),
    )(src, land, send_sems, recv_sems, after)[1]


def _rows(a):
    return a.reshape(-1, 1024)


def _rows_to_parts(full):
    n = full.shape[-2] // N_DEV
    t = full.reshape(full.shape[:-2] + (N_DEV, n, full.shape[-1]))
    return jnp.moveaxis(t, -3, 0)


def _parts_to_rows(parts):
    t = jnp.moveaxis(parts, 0, -3)
    return t.reshape(t.shape[:-3] + (t.shape[-3] * t.shape[-2], t.shape[-1]))


def _parts_to_cols(parts):
    t = jnp.moveaxis(parts, 0, -2)
    return t.reshape(t.shape[:-2] + (t.shape[-2] * t.shape[-1],))


def _join(parts, axis=0):
    total = sum(p.shape[axis] for p in parts)
    out, off = None, 0
    for p in parts:
        cfg = [(0, 0)] * p.ndim
        cfg[axis] = (off, total - off - p.shape[axis])
        t = jnp.pad(p, cfg)
        out = t if out is None else out + t
        off += p.shape[axis]
    return out


def _w_in_to_layout(w):
    tail = jnp.pad(w[4096:4104], ((0, PW - COL_TAIL - 8), (0, 0)))
    return jnp.concatenate([w[:4096], w[4104:P_IN], tail], axis=0)


def _w_in_from_layout(g):
    return _join([g[:4096], g[COL_TAIL:COL_TAIL + 8], g[4096:COL_TAIL]], axis=0)


def _block_diag(w):
    w = w.reshape(4, 2, 64, 64)
    return jnp.pad(w[:, 0], ((0, 0), (0, 64), (0, 64))) + jnp.pad(w[:, 1], ((0, 0), (64, 0), (64, 0)))


def _block_diag_grad(g):
    return jnp.stack([g[:, :64, :64], g[:, 64:, 64:]], axis=1).reshape(8, 64, 64)


def _ffn_forward(tag, x, norm, wg, wu, wd):
    h = _rms_fwd(tag + "_norm", x, norm)
    a, b, act = _ffn_up(tag + "_up", h, wg, wu)
    x_out = _mm(tag + "_down", [(act, wd)], "nn", F32, res=x, scale=0.5)
    return x_out, (x, h, a, b, act)


def _ffn_backward(tag, dx_out, saved, norm, wg, wu, wd):
    x, h, a, b, act = saved
    da, db = _ffn_dact(tag + "_dact", dx_out, wd, a, b)
    dwd = _mm(tag + "_dwd", [(act, dx_out)], "tn", F32, scale=0.5, bm=256)
    dwg = _mm(tag + "_dwg", [(da, h)], "tn", F32, bm=256)
    dwu = _mm(tag + "_dwu", [(db, h)], "tn", F32, bm=256)
    dh = _mm(tag + "_dh", [(da, wg), (db, wu)], "nn", F32)
    dx, dnorm = _rms_bwd(tag + "_dnorm", x, norm, dh, dx_out)
    return dx, dnorm, dwg, dwu, dwd


def _mixer_params(p):
    alog = jnp.pad(p["gdn_a_log"], (4, 120))[None]
    dtb = jnp.pad(p["gdn_dt_bias"], (4, 120))[None]
    bias = jnp.repeat(p["sgu_b"].T, 128, axis=1)
    return dict(
        ln_g=p["sgu_ln_g"][None], ln_b=p["sgu_ln_b"][None], sgu_w=p["sgu_w"], sgu_bias=bias,
        lru_cw=p["lru_conv_w"], lru_cb=p["lru_conv_b"][None], wa=_block_diag(p["lru_wa"]), ba=p["lru_ba"][None],
        wx=_block_diag(p["lru_wx"]), bx=p["lru_bx"][None], lam=p["lru_lambda"][None],
        gdn_cw=p["gdn_conv_w"], alog=alog, dtb=dtb, ng=p["gdn_norm_g"][None],
        pool_w=p["pool_w"], pool_sc=p["pool_scale"][None])


def _mix_forward(tag, x, p, mp):
    h = _rms_fwd(tag + "_norm", x, p["mix_norm"][None])
    proj = _mm(tag + "_proj", [(h, p["w_in"])], "nt", F32)
    y_a = _sgu_fwd(tag + "_sgu", proj, mp["ln_g"], mp["ln_b"], mp["sgu_w"], mp["sgu_bias"])
    y_b, hc = _lru_fwd(tag + "_lru", proj, mp["lru_cw"], mp["lru_cb"], mp["wa"], mp["ba"], mp["wx"], mp["bx"],
                       mp["lam"])
    qa = _conv_fwd(tag + "_convq", proj, COL_CQ, mp["gdn_cw"], 0)
    ka = _conv_fwd(tag + "_convk", proj, COL_CK, mp["gdn_cw"], 512)
    va = _conv_fwd(tag + "_convv", proj, COL_CV, mp["gdn_cw"], 1024)
    y_c, sh = _gdn_fwd(tag + "_gdn", qa, ka, va, proj, mp["alog"], mp["dtb"], mp["ng"])
    y_d = _pool_fwd(tag + "_pool", proj, mp["pool_w"], mp["pool_sc"])
    ys = (y_a, y_b, y_c, y_d)
    merged = _merge_fwd(tag + "_merge", ys, p["w_branch"], proj)
    x_out = _mm(tag + "_out", [(merged, p["w_out"])], "nn", F32, res=x)
    return x_out, (x, h, proj, hc, qa, ka, va, sh, ys, merged)


def _mix_backward(tag, dx_out, saved, p, mp):
    x, h, proj, hc, qa, ka, va, sh, ys, merged = saved
    T = x.shape[0]
    g = {}
    dmerged = _mm(tag + "_dmerged", [(dx_out, p["w_out"])], "nt", F32)
    g["w_out"] = _mm(tag + "_dwout", [(merged, dx_out)], "tn", F32)
    outs = _merge_bwd(tag + "_dmerge", dmerged, ys, p["w_branch"], proj)
    dgates, dbrs = outs[:NBR], outs[NBR:]
    dys = [_mm(f"{tag}_dy{i}", [(dbrs[i], p["w_branch"][i])], "nn", F32) for i in range(NBR)]
    g["w_branch"] = jnp.stack([_mm(f"{tag}_dwb{i}", [(dbrs[i], ys[i])], "tn", F32) for i in range(NBR)])

    du, dv, dln_g, dln_b, dsgu_w, dbias = _sgu_bwd(tag + "_dsgu", proj, dys[0], mp["ln_g"], mp["ln_b"], mp["sgu_w"],
                                                  mp["sgu_bias"])
    g["sgu_ln_g"], g["sgu_ln_b"], g["sgu_w"] = dln_g[0], dln_b[0], dsgu_w
    g["sgu_b"] = dbias.reshape(128, 4, 128).sum(axis=2).T

    (dbx, dbg, dcw, dcb, dwa, dba, dwx, dbxb, dlam) = _lru_bwd(
        tag + "_dlru", proj, dys[1], hc, mp["lru_cw"], mp["lru_cb"], mp["wa"], mp["ba"], mp["wx"], mp["bx"], mp["lam"])
    g["lru_conv_w"], g["lru_conv_b"], g["lru_ba"], g["lru_bx"], g["lru_lambda"] = dcw, dcb[0], dba[0], dbxb[0], dlam[0]
    g["lru_wa"], g["lru_wx"] = _block_diag_grad(dwa), _block_diag_grad(dwx)

    dqa, dka, dva, dz, dtail, dalog, ddtb, dng = _gdn_bwd(tag + "_dgdn", qa, ka, va, proj, dys[2], sh, mp["alog"],
                                                         mp["dtb"], mp["ng"])
    g["gdn_a_log"], g["gdn_dt_bias"], g["gdn_norm_g"] = dalog[0, 4:8], ddtb[0, 4:8], dng[0]
    dq, dcwq = _conv_bwd(tag + "_dconvq", proj, COL_CQ, dqa, mp["gdn_cw"], 0)
    dk, dcwk = _conv_bwd(tag + "_dconvk", proj, COL_CK, dka, mp["gdn_cw"], 512)
    dv_, dcwv = _conv_bwd(tag + "_dconvv", proj, COL_CV, dva, mp["gdn_cw"], 1024)
    g["gdn_conv_w"] = jnp.concatenate([dcwq, dcwk, dcwv], axis=1)

    dd, dpw, dsc = _pool_bwd(tag + "_dpool", proj, dys[3], mp["pool_w"], mp["pool_sc"])
    g["pool_w"], g["pool_scale"] = dpw, dsc[0]

    dproj = jnp.concatenate([du, dv, dbx, dbg, dq, dk, dv_, dz, dd, *dgates, dtail,
                             jnp.zeros((T, PW - COL_TAIL - 128), BF16)], axis=1)
    g["w_in"] = _mm(tag + "_dwin", [(dproj, h)], "tn", F32)
    dh = _mm(tag + "_dh", [(dproj, p["w_in"])], "nn", F32)
    dx, dnorm = _rms_bwd(tag + "_dnorm", x, p["mix_norm"][None], dh, dx_out)
    g["mix_norm"] = dnorm[0]
    return dx, g


_BIG = ("ff1_wg", "ff1_wu", "ff1_wd", "w_in", "w_branch", "w_out", "ff2_wg", "ff2_wu", "ff2_wd")
_COL_SHARDED = ("ff1_wg", "ff1_wu", "w_in", "w_branch", "ff2_wg", "ff2_wu")
_SMALL = ("ff1_norm", "mix_norm", "sgu_ln_g", "sgu_ln_b", "sgu_w", "sgu_b", "lru_conv_w", "lru_conv_b", "lru_wa",
          "lru_ba", "lru_wx", "lru_bx", "lru_lambda", "gdn_conv_w", "gdn_a_log", "gdn_dt_bias", "gdn_norm_g", "pool_w",
          "pool_scale", "ff2_norm", "final_norm")
_WEIGHTS = ("ff1_norm", "ff1_wg", "ff1_wu", "ff1_wd", "mix_norm", "w_in", "sgu_ln_g", "sgu_ln_b", "sgu_w", "sgu_b",
            "lru_conv_w", "lru_conv_b", "lru_wa", "lru_ba", "lru_wx", "lru_bx", "lru_lambda", "gdn_conv_w", "gdn_a_log",
            "gdn_dt_bias", "gdn_norm_g", "pool_w", "pool_scale", "w_branch", "w_out", "ff2_norm", "ff2_wg", "ff2_wu",
            "ff2_wd", "final_norm")
_CONV_SHARDED = ("lru_conv_w", "gdn_conv_w")
PACK_ROW_ALIGN = 16
_GROUPS = (("ff1", ("ff1_wg", "ff1_wu", "ff1_wd")), ("mix", ("w_in", "w_branch", "w_out")),
           ("ff2", ("ff2_wg", "ff2_wu", "ff2_wd")))


def _pad_rows(a, mult):
    pad = (-a.shape[-2]) % mult
    if pad == 0:
        return a
    return jnp.pad(a, [(0, 0)] * (a.ndim - 2) + [(0, pad), (0, 0)])


def _my_index():
    return 4 * lax.axis_index("x") + 2 * lax.axis_index("y") + lax.axis_index("c")


def _landing(own):
    return lax.dynamic_update_index_in_dim(lax.empty((N_DEV,) + own.shape, own.dtype), own, _my_index(), 0)


def _stored(n, a):
    return jnp.swapaxes(a, -1, -2) if n in _COL_SHARDED else a


def _gather_start(w):
    packs, metas = [], []
    for l in range(2):
        for _, names in _GROUPS:
            pieces, meta = [], []
            for n in names:
                a = _stored(n, w[n][l])
                rows = _pad_rows(_rows(a.astype(BF16)), PACK_ROW_ALIGN)
                pieces.append(rows)
                meta.append((n, a.size // 1024, rows.shape[0], a.shape))
            packs.append(jnp.concatenate(pieces, axis=0))
            metas.append(meta)
    handles, token = _exchange_start("gather_start", packs, [_landing(p) for p in packs], scatter=False)
    return handles, metas, token


def _gather_finish(l, sub, handles, metas, after):
    gi = 3 * l + sub
    got = _exchange_wait(f"gather_wait{gi}", *handles[gi], after, scatter=False)
    out, r = {}, 0
    for n, size, padded, shape in metas[gi]:
        out[n] = _parts_to_rows(got[:, r:r + size].reshape((N_DEV,) + shape))
        r += padded
    if "w_in" in out:
        out["w_in"] = _w_in_to_layout(out["w_in"])
    return out


def _scatter_start(l, sub, grads):
    pieces, meta = [], []
    for n in _GROUPS[sub][1]:
        parts = _rows_to_parts(grads[n].astype(BF16))
        shape = parts.shape[1:]
        parts = parts.reshape(N_DEV, -1, 1024)
        padded = _pad_rows(parts, PACK_ROW_ALIGN)
        pieces.append(padded)
        meta.append((n, parts.shape[1], padded.shape[1], shape))
    pack = jnp.concatenate(pieces, axis=1)
    own = lax.dynamic_index_in_dim(pack, _my_index(), 0, keepdims=False)
    handles, token = _exchange_start(f"scatter_start{3 * l + sub}", [pack], [_landing(own)], scatter=True)
    return handles[0], meta, token


def _scatter_finish(l, sub, handle, meta, after):
    gi = 3 * l + sub
    landed = _exchange_wait(f"scatter_wait{gi}", *handle, after, scatter=True)
    summed = _sum8(f"sum_grads{gi}", landed)
    out, r = {}, 0
    for n, size, padded, shape in meta:
        out[n] = _stored(n, summed[r:r + size].reshape(shape))
        r += padded
    return out


def _gather_conv(w):
    conv = _pad_rows(jnp.concatenate([w[n].reshape(1, -1) for n in _CONV_SHARDED], axis=1), 8)
    gconv = _all_gather("gather_conv", conv)[:, 0]
    full, r = {}, 0
    for n in _CONV_SHARDED:
        sz = w[n].size
        full[n] = _parts_to_cols(gconv[:, r:r + sz].reshape((N_DEV,) + w[n].shape))
        r += sz
    return full


def _forward_backward(x, tgt, w, conv, get_weights, put_grads, token):
    saved, params = [], []
    for l in range(2):
        p = {n: w[n][l] for n in _SMALL if n != "final_norm"}
        for n in _CONV_SHARDED:
            p[n] = conv[n][l]
        mp = _mixer_params(p)
        tok = token[:1, :1] if l == 0 else 0.0
        p.update(get_weights(l, 0, x))
        x, s1 = _ffn_forward(f"l{l}_ff1", x, p["ff1_norm"][None] + tok, p["ff1_wg"], p["ff1_wu"], p["ff1_wd"])
        p.update(get_weights(l, 1, x))
        x, s2 = _mix_forward(f"l{l}_mix", x, p, mp)
        p.update(get_weights(l, 2, x))
        x, s3 = _ffn_forward(f"l{l}_ff2", x, p["ff2_norm"][None], p["ff2_wg"], p["ff2_wu"], p["ff2_wd"])
        saved.append((s1, s2, s3))
        params.append((p, mp))
    loss, dx, dfinal = _final_loss("loss_head", x, w["final_norm"][None], tgt)
    grads = [None, None]
    tok = 0.0
    for l in (1, 0):
        p, mp = params[l]
        s1, s2, s3 = saved[l]
        g = {}
        dx, dn, dwg, dwu, dwd = _ffn_backward(f"l{l}_ff2", dx, s3, p["ff2_norm"][None] + tok, p["ff2_wg"], p["ff2_wu"],
                                              p["ff2_wd"])
        g["ff2_norm"] = dn[0]
        tok = put_grads(l, 2, dict(ff2_wg=dwg, ff2_wu=dwu, ff2_wd=dwd))[:1, :1]
        p["mix_norm"] = p["mix_norm"] + tok[0]
        dx, gm = _mix_backward(f"l{l}_mix", dx, s2, p, mp)
        big = dict(w_in=_w_in_from_layout(gm.pop("w_in")), w_branch=gm.pop("w_branch"), w_out=gm.pop("w_out"))
        g.update(gm)
        tok = put_grads(l, 1, big)[:1, :1]
        dx, dn, dwg, dwu, dwd = _ffn_backward(f"l{l}_ff1", dx, s1, p["ff1_norm"][None] + tok, p["ff1_wg"], p["ff1_wu"],
                                              p["ff1_wd"])
        g["ff1_norm"] = dn[0]
        tok = put_grads(l, 0, dict(ff1_wg=dwg, ff1_wu=dwu, ff1_wd=dwd))[:1, :1]
        grads[l] = g
    small_g = {n: _join([grads[0][n].reshape(-1), grads[1][n].reshape(-1)]).reshape((2,) + grads[0][n].shape)
               for n in grads[0]}
    small_g["final_norm"] = dfinal[0]
    return loss, dx, small_g


SMALL_PIECE = 8 * 1024


def _pack_small(d, names):
    pieces = []
    for n in names:
        flat = d[n].reshape(-1)
        pieces.append(jnp.pad(flat, (0, (-flat.size) % SMALL_PIECE)).reshape(-1, 1024))
    return jnp.concatenate(pieces, axis=0)


def _unpack_small(pack, shapes, names):
    out, r = {}, 0
    for n in names:
        size = 1
        for s in shapes[n]:
            size *= s
        rows = -(-size // SMALL_PIECE) * 8
        out[n] = pack[r:r + rows].reshape(-1)[:size].reshape(shapes[n])
        r += rows
    return out


def _reduce_small(full_g, loss, w):
    d = dict(full_g)
    d["loss"] = loss[0, :1]
    names = _SMALL + ("loss",)
    summed = _sum8("sum_small", _all_gather("gather_small", _pack_small(d, names)))
    out = _unpack_small(summed, {n: d[n].shape for n in names}, names)
    return out, out["loss"][0]


def _as2d(a):
    if a.ndim == 1:
        return a.reshape(1, -1)
    return a.reshape(-1, a.shape[-1])


def kernel(x, ff1_norm, ff1_wg, ff1_wu, ff1_wd, mix_norm, w_in, sgu_ln_g, sgu_ln_b, sgu_w, sgu_b, lru_conv_w, lru_conv_b, lru_wa, lru_ba, lru_wx, lru_bx, lru_lambda, gdn_conv_w, gdn_a_log, gdn_dt_bias, gdn_norm_g, pool_w, pool_scale, w_branch, w_out, ff2_norm, ff2_wg, ff2_wu, ff2_wd, final_norm, loss_target, m_ff1_norm, m_ff1_wg, m_ff1_wu, m_ff1_wd, m_mix_norm, m_w_in, m_sgu_ln_g, m_sgu_ln_b, m_sgu_w, m_sgu_b, m_lru_conv_w, m_lru_conv_b, m_lru_wa, m_lru_ba, m_lru_wx, m_lru_bx, m_lru_lambda, m_gdn_conv_w, m_gdn_a_log, m_gdn_dt_bias, m_gdn_norm_g, m_pool_w, m_pool_scale, m_w_branch, m_w_out, m_ff2_norm, m_ff2_wg, m_ff2_wu, m_ff2_wd, m_final_norm, v_ff1_norm, v_ff1_wg, v_ff1_wu, v_ff1_wd, v_mix_norm, v_w_in, v_sgu_ln_g, v_sgu_ln_b, v_sgu_w, v_sgu_b, v_lru_conv_w, v_lru_conv_b, v_lru_wa, v_lru_ba, v_lru_wx, v_lru_bx, v_lru_lambda, v_gdn_conv_w, v_gdn_a_log, v_gdn_dt_bias, v_gdn_norm_g, v_pool_w, v_pool_scale, v_w_branch, v_w_out, v_ff2_norm, v_ff2_wg, v_ff2_wu, v_ff2_wd, v_final_norm):
    w = dict(ff1_norm=ff1_norm, ff1_wg=ff1_wg, ff1_wu=ff1_wu, ff1_wd=ff1_wd, mix_norm=mix_norm, w_in=w_in,
             sgu_ln_g=sgu_ln_g, sgu_ln_b=sgu_ln_b, sgu_w=sgu_w, sgu_b=sgu_b, lru_conv_w=lru_conv_w,
             lru_conv_b=lru_conv_b, lru_wa=lru_wa, lru_ba=lru_ba, lru_wx=lru_wx, lru_bx=lru_bx, lru_lambda=lru_lambda,
             gdn_conv_w=gdn_conv_w, gdn_a_log=gdn_a_log, gdn_dt_bias=gdn_dt_bias, gdn_norm_g=gdn_norm_g, pool_w=pool_w,
             pool_scale=pool_scale, w_branch=w_branch, w_out=w_out, ff2_norm=ff2_norm, ff2_wg=ff2_wg, ff2_wu=ff2_wu,
             ff2_wd=ff2_wd, final_norm=final_norm)
    m = dict(ff1_norm=m_ff1_norm, ff1_wg=m_ff1_wg, ff1_wu=m_ff1_wu, ff1_wd=m_ff1_wd, mix_norm=m_mix_norm, w_in=m_w_in,
             sgu_ln_g=m_sgu_ln_g, sgu_ln_b=m_sgu_ln_b, sgu_w=m_sgu_w, sgu_b=m_sgu_b, lru_conv_w=m_lru_conv_w,
             lru_conv_b=m_lru_conv_b, lru_wa=m_lru_wa, lru_ba=m_lru_ba, lru_wx=m_lru_wx, lru_bx=m_lru_bx,
             lru_lambda=m_lru_lambda, gdn_conv_w=m_gdn_conv_w, gdn_a_log=m_gdn_a_log, gdn_dt_bias=m_gdn_dt_bias,
             gdn_norm_g=m_gdn_norm_g, pool_w=m_pool_w, pool_scale=m_pool_scale, w_branch=m_w_branch, w_out=m_w_out,
             ff2_norm=m_ff2_norm, ff2_wg=m_ff2_wg, ff2_wu=m_ff2_wu, ff2_wd=m_ff2_wd, final_norm=m_final_norm)
    v = dict(ff1_norm=v_ff1_norm, ff1_wg=v_ff1_wg, ff1_wu=v_ff1_wu, ff1_wd=v_ff1_wd, mix_norm=v_mix_norm, w_in=v_w_in,
             sgu_ln_g=v_sgu_ln_g, sgu_ln_b=v_sgu_ln_b, sgu_w=v_sgu_w, sgu_b=v_sgu_b, lru_conv_w=v_lru_conv_w,
             lru_conv_b=v_lru_conv_b, lru_wa=v_lru_wa, lru_ba=v_lru_ba, lru_wx=v_lru_wx, lru_bx=v_lru_bx,
             lru_lambda=v_lru_lambda, gdn_conv_w=v_gdn_conv_w, gdn_a_log=v_gdn_a_log, gdn_dt_bias=v_gdn_dt_bias,
             gdn_norm_g=v_gdn_norm_g, pool_w=v_pool_w, pool_scale=v_pool_scale, w_branch=v_w_branch, w_out=v_w_out,
             ff2_norm=v_ff2_norm, ff2_wg=v_ff2_wg, ff2_wu=v_ff2_wu, ff2_wd=v_ff2_wd, final_norm=v_final_norm)

    handles, metas, token = _gather_start(w)
    conv = _gather_conv(w)
    pending = {}

    def get_weights(l, sub, after):
        return _gather_finish(l, sub, handles, metas, after)

    def put_grads(l, sub, grads):
        handle, meta, tok = _scatter_start(l, sub, grads)
        pending[(l, sub)] = (handle, meta)
        return tok

    T = x.shape[1]
    loss_share, dx, small_g = _forward_backward(x.reshape(T, D), loss_target.reshape(T, D), w, conv, get_weights,
                                                put_grads, token)
    per = {key: _scatter_finish(*key, *pending[key], dx) for key in pending}
    grad = {n: jnp.stack([per[(0, sub)][n], per[(1, sub)][n]]) for sub, (_, names) in enumerate(_GROUPS) for n in names}
    small, loss = _reduce_small(small_g, loss_share, w)
    me = _my_index()
    for n in _SMALL:
        if n in _CONV_SHARDED:
            width = w[n].shape[-1]
            grad[n] = lax.dynamic_slice_in_dim(small[n], me * width, width, axis=2)
        else:
            grad[n] = small[n]

    delta, new_m, new_v = {}, {}, {}
    for n in _BIG:
        d_, m_, v_ = _adamw("adamw_" + n, _as2d(w[n]), _as2d(grad[n]), _as2d(m[n]), _as2d(v[n]))
        delta[n], new_m[n], new_v[n] = (t.reshape(w[n].shape) for t in (d_, m_, v_))

    outs = _adamw("adamw_small", *[_pack_small(t, _SMALL) for t in (w, grad, m, v)])
    for dst, packed in zip((delta, new_m, new_v), outs):
        dst.update(_unpack_small(packed, {n: w[n].shape for n in _SMALL}, _SMALL))

    return (loss, dx.reshape(x.shape), *[grad[n] for n in _WEIGHTS], *[delta[n] for n in _WEIGHTS],
            *[new_m[n] for n in _WEIGHTS], *[new_v[n] for n in _WEIGHTS])
```

```python
import functools

import jax
import jax.numpy as jnp
from jax import lax
from jax.experimental import pallas as pl
from jax.experimental.pallas import tpu as pltpu

F32 = jnp.float32
BF16 = jnp.bfloat16
HI = lax.Precision.HIGHEST

N_DEV = 8
D = 1024
FF = 2816
BW = 512
NBR = 4
CHUNK = 64
EPS = 1e-6
LRU_C = 8.0
GDN_DK = 128

COL_AU, COL_AV, COL_BX, COL_BG = 0, 512, 1024, 1536
COL_CQ, COL_CK, COL_CV, COL_CZ = 2048, 2560, 3072, 3584
COL_DX, COL_GATE, COL_TAIL = 4096, 4608, 8704
PW = 9216
P_IN = 8712

ADAM_LR, ADAM_B1, ADAM_B2, ADAM_EPS, ADAM_WD, ADAM_STEP = 0.001, 0.9, 0.999, 1e-08, 0.01, 10

VMEM_LIMIT_V7X = 56 * 1024 * 1024

_NN = (((1,), (0,)), ((), ()))
_NT = (((1,), (1,)), ((), ()))
_TN = (((0,), (0,)), ((), ()))


def _cp(*sem):
    return pltpu.CompilerParams(dimension_semantics=tuple(sem), vmem_limit_bytes=VMEM_LIMIT_V7X)


def _dot(a, b, dims=_NN):
    return lax.dot_general(a.astype(BF16), b.astype(BF16), dims, preferred_element_type=F32)


def _dot_hi(a, b, dims=_NN):
    return lax.dot_general(a, b, dims, precision=HI, preferred_element_type=F32)


def _pick(n, cands):
    for c in cands:
        if n % c == 0:
            return c
    return n


@jax.custom_jvp
def _log1p(x):
    u = 1.0 + x
    return jnp.where(u == 1.0, x, x * jnp.log(u) / jnp.where(u == 1.0, 1.0, u - 1.0))


@_log1p.defjvp
def _log1p_jvp(p, t):
    (x,), (dx,) = p, t
    return _log1p(x), dx / (1.0 + x)


@jax.custom_jvp
def _expm1(x):
    u = jnp.exp(x)
    lu = jnp.log(u)
    small = (u == 1.0) | (lu == 0.0)
    return jnp.where(small, x, (u - 1.0) * x / jnp.where(small, 1.0, lu))


@_expm1.defjvp
def _expm1_jvp(p, t):
    (x,), (dx,) = p, t
    return _expm1(x), dx * jnp.exp(x)


def _softplus(x):
    return jnp.maximum(x, 0.0) + _log1p(jnp.exp(-jnp.abs(x)))


def _sigmoid(x):
    return jax.nn.sigmoid(x)


def _silu(x):
    return x * jax.nn.sigmoid(x)


def _gelu(x):
    return jax.nn.gelu(x)


@functools.partial(jax.custom_vjp, nondiff_argnums=(1,))
def _shift(x, s):
    return x if s == 0 else pltpu.roll(x, s, 0)


def _shift_fwd(x, s):
    return _shift(x, s), None


def _shift_bwd(s, _, g):
    n = g.shape[0]
    return (g if s == 0 else pltpu.roll(g, n - s, 0),)


_shift.defvjp(_shift_fwd, _shift_bwd)


def _scan_steps(a, b, reverse):
    n = a.shape[0]
    row = lax.broadcasted_iota(jnp.int32, a.shape, 0)
    k = 1
    while k < n:
        sh = n - k if reverse else k
        m = (row < n - k) if reverse else (row >= k)
        a_s = jnp.where(m, pltpu.roll(a, sh, 0), 1.0)
        b_s = jnp.where(m, pltpu.roll(b, sh, 0), 0.0)
        b = a * b_s + b
        a = a * a_s
        k *= 2
    return b


@jax.custom_vjp
def _scan(a, b):
    return _scan_steps(a, b, False)


def _scan_fwd(a, b):
    h = _scan_steps(a, b, False)
    return h, (a, h)


def _scan_bwd(res, dh):
    a, h = res
    n = a.shape[0]
    row = lax.broadcasted_iota(jnp.int32, a.shape, 0)
    a_next = jnp.where(row < n - 1, pltpu.roll(a, n - 1, 0), 0.0)
    g = _scan_steps(a_next, dh, True)
    h_prev = jnp.where(row >= 1, pltpu.roll(h, 1, 0), 0.0)
    return g * h_prev, g


_scan.defvjp(_scan_fwd, _scan_bwd)


def _mm(name, pairs, mode, out_dtype, *, res=None, scale=1.0, bm=None, bn=None, bk=None):
    a0, b0 = pairs[0]
    if mode == "nn":
        (M, K), N = a0.shape, b0.shape[1]
    elif mode == "nt":
        (M, K), N = a0.shape, b0.shape[0]
    else:
        (K, M), N = a0.shape, b0.shape[1]
    bm = bm or _pick(M, (1024, 512, 256, 128))
    bn = bn or _pick(N, (1024, 512, 256, 128))
    bk = bk or _pick(K, (1024, 512, 1408, 256, 128))
    nk = K // bk
    npair = len(pairs)
    dims = {"nn": _NN, "nt": _NT, "tn": _TN}[mode]

    def body(*refs):
        ab = refs[:2 * npair]
        pos = 2 * npair
        r_ref = None
        if res is not None:
            r_ref = refs[pos]
            pos += 1
        o_ref = refs[pos]
        part = None
        for p in range(npair):
            d = _dot(ab[2 * p][...], ab[2 * p + 1][...], dims)
            part = d if part is None else part + d

        def finish(acc):
            out = acc if scale == 1.0 else acc * scale
            if r_ref is not None:
                out = out + r_ref[...]
            o_ref[...] = out.astype(out_dtype)

        if nk == 1:
            finish(part)
        else:
            acc_ref = refs[pos + 1]
            k = pl.program_id(2)

            @pl.when(k == 0)
            def _():
                acc_ref[...] = part

            @pl.when(k > 0)
            def _():
                acc_ref[...] += part

            @pl.when(k == nk - 1)
            def _():
                finish(acc_ref[...])

    if mode == "nn":
        a_spec = pl.BlockSpec((bm, bk), lambda i, j, k: (i, k))
        b_spec = pl.BlockSpec((bk, bn), lambda i, j, k: (k, j))
    elif mode == "nt":
        a_spec = pl.BlockSpec((bm, bk), lambda i, j, k: (i, k))
        b_spec = pl.BlockSpec((bn, bk), lambda i, j, k: (j, k))
    else:
        a_spec = pl.BlockSpec((bk, bm), lambda i, j, k: (k, i))
        b_spec = pl.BlockSpec((bk, bn), lambda i, j, k: (k, j))
    o_spec = pl.BlockSpec((bm, bn), lambda i, j, k: (i, j))
    in_specs, args = [], []
    for a, b in pairs:
        in_specs += [a_spec, b_spec]
        args += [a, b]
    if res is not None:
        in_specs.append(o_spec)
        args.append(res)
    return pl.pallas_call(
        body, name=name, grid=(M // bm, N // bn, nk),
        in_specs=in_specs, out_specs=o_spec,
        out_shape=jax.ShapeDtypeStruct((M, N), out_dtype),
        scratch_shapes=[pltpu.VMEM((bm, bn), F32)] if nk > 1 else [],
        compiler_params=_cp("parallel", "parallel", "arbitrary"),
    )(*args)


def _rms_fwd(name, x, g):
    T = x.shape[0]
    bm = _pick(T, (512, 256, 128))

    def body(x_ref, g_ref, o_ref):
        xv = x_ref[...]
        r = lax.rsqrt(jnp.mean(xv * xv, axis=-1, keepdims=True) + EPS)
        o_ref[...] = (xv * r * g_ref[...]).astype(BF16)

    return pl.pallas_call(
        body, name=name, grid=(T // bm,),
        in_specs=[pl.BlockSpec((bm, D), lambda i: (i, 0)), pl.BlockSpec((1, D), lambda i: (0, 0))],
        out_specs=pl.BlockSpec((bm, D), lambda i: (i, 0)),
        out_shape=jax.ShapeDtypeStruct((T, D), BF16),
        compiler_params=_cp("parallel"),
    )(x, g)


def _rms_bwd(name, x, g, dh, dres):
    T = x.shape[0]
    bm = _pick(T, (512, 256, 128))

    def body(x_ref, g_ref, dh_ref, dres_ref, dx_ref, dg_ref):
        xv = x_ref[...]
        r = lax.rsqrt(jnp.mean(xv * xv, axis=-1, keepdims=True) + EPS)
        xh = xv * r
        dhv = dh_ref[...]
        dxh = dhv * g_ref[...]
        dx_ref[...] = dres_ref[...] + r * (dxh - xh * jnp.mean(dxh * xh, axis=-1, keepdims=True))
        part = jnp.sum(dhv * xh, axis=0, keepdims=True)

        @pl.when(pl.program_id(0) == 0)
        def _():
            dg_ref[...] = part

        @pl.when(pl.program_id(0) > 0)
        def _():
            dg_ref[...] += part

    row = pl.BlockSpec((bm, D), lambda i: (i, 0))
    vec = pl.BlockSpec((1, D), lambda i: (0, 0))
    return pl.pallas_call(
        body, name=name, grid=(T // bm,),
        in_specs=[row, vec, row, row], out_specs=[row, vec],
        out_shape=[jax.ShapeDtypeStruct((T, D), F32), jax.ShapeDtypeStruct((1, D), F32)],
        compiler_params=_cp("arbitrary"),
    )(x, g, dh, dres)


def _final_loss(name, x, g, tgt):
    T = x.shape[0]
    bm = _pick(T, (512, 256, 128))

    def body(x_ref, g_ref, t_ref, loss_ref, dx_ref, dg_ref):
        xv = x_ref[...]
        gv = g_ref[...]
        r = lax.rsqrt(jnp.mean(xv * xv, axis=-1, keepdims=True) + EPS)
        xh = xv * r
        e = xh * gv - t_ref[...]
        lpart = jnp.broadcast_to(0.5 * jnp.sum(jnp.mean(e * e, axis=-1, keepdims=True), axis=0, keepdims=True), (1, 128))
        dy = e * (1.0 / D)
        dxh = dy * gv
        dx_ref[...] = r * (dxh - xh * jnp.mean(dxh * xh, axis=-1, keepdims=True))
        gpart = jnp.sum(dy * xh, axis=0, keepdims=True)

        @pl.when(pl.program_id(0) == 0)
        def _():
            loss_ref[...] = lpart
            dg_ref[...] = gpart

        @pl.when(pl.program_id(0) > 0)
        def _():
            loss_ref[...] += lpart
            dg_ref[...] += gpart

    row = pl.BlockSpec((bm, D), lambda i: (i, 0))
    vec = pl.BlockSpec((1, D), lambda i: (0, 0))
    return pl.pallas_call(
        body, name=name, grid=(T // bm,),
        in_specs=[row, vec, row],
        out_specs=[pl.BlockSpec((1, 128), lambda i: (0, 0)), row, vec],
        out_shape=[jax.ShapeDtypeStruct((1, 128), F32), jax.ShapeDtypeStruct((T, D), F32),
                   jax.ShapeDtypeStruct((1, D), F32)],
        compiler_params=_cp("arbitrary"),
    )(x, g, tgt)


def _ffn_up(name, h, wg, wu):
    T = h.shape[0]
    bm = _pick(T, (2048, 1024, 512, 256, 128))
    bn = 256

    def body(h_ref, wg_ref, wu_ref, a_ref, b_ref, act_ref):
        hv = h_ref[...]
        a = _dot(hv, wg_ref[...], _NT)
        b = _dot(hv, wu_ref[...], _NT)
        a_ref[...] = a.astype(BF16)
        b_ref[...] = b.astype(BF16)
        act_ref[...] = (_silu(a) * b).astype(BF16)

    w_spec = pl.BlockSpec((bn, D), lambda i, j: (j, 0))
    o_spec = pl.BlockSpec((bm, bn), lambda i, j: (i, j))
    return pl.pallas_call(
        body, name=name, grid=(T // bm, FF // bn),
        in_specs=[pl.BlockSpec((bm, D), lambda i, j: (i, 0)), w_spec, w_spec],
        out_specs=[o_spec, o_spec, o_spec],
        out_shape=[jax.ShapeDtypeStruct((T, FF), BF16)] * 3,
        compiler_params=_cp("parallel", "parallel"),
    )(h, wg, wu)


def _ffn_dact(name, dy, wd, a, b):
    T = dy.shape[0]
    bm = _pick(T, (2048, 1024, 512, 256, 128))
    bn = 256

    def body(dy_ref, wd_ref, a_ref, b_ref, da_ref, db_ref):
        dact = 0.5 * _dot(dy_ref[...], wd_ref[...], _NT)
        av = a_ref[...].astype(F32)
        s = _sigmoid(av)
        da_ref[...] = (dact * b_ref[...].astype(F32) * (s * (1.0 + av * (1.0 - s)))).astype(BF16)
        db_ref[...] = (dact * (av * s)).astype(BF16)

    t_spec = pl.BlockSpec((bm, bn), lambda i, j: (i, j))
    return pl.pallas_call(
        body, name=name, grid=(T // bm, FF // bn),
        in_specs=[pl.BlockSpec((bm, D), lambda i, j: (i, 0)), pl.BlockSpec((bn, D), lambda i, j: (j, 0)),
                  t_spec, t_spec],
        out_specs=[t_spec, t_spec],
        out_shape=[jax.ShapeDtypeStruct((T, FF), BF16), jax.ShapeDtypeStruct((T, FF), BF16)],
        compiler_params=_cp("parallel", "parallel"),
    )(dy, wd, a, b)


def _merge_specs(T, bm, bn):
    y_spec = pl.BlockSpec((bm, BW), lambda i, j: (i, 0))
    wb_spec = pl.BlockSpec((NBR, bn, BW), lambda i, j: (0, j, 0))
    gate_specs = [pl.BlockSpec((bm, bn), functools.partial(lambda i, j, o: (i, o + j), o=(COL_GATE + g * D) // bn))
                  for g in range(NBR)]
    t_spec = pl.BlockSpec((bm, bn), lambda i, j: (i, j))
    return y_spec, wb_spec, gate_specs, t_spec


def _merge_fwd(name, ys, wb, proj):
    T = proj.shape[0]
    bm = _pick(T, (512, 256, 128))
    bn = 512
    y_spec, wb_spec, gate_specs, t_spec = _merge_specs(T, bm, bn)

    def body(y0, y1, y2, y3, wb_ref, g0, g1, g2, g3, o_ref):
        acc = None
        for g, (y_ref, g_ref) in enumerate(((y0, g0), (y1, g1), (y2, g2), (y3, g3))):
            t = _sigmoid(g_ref[...]) * _dot(y_ref[...], wb_ref[g], _NT)
            acc = t if acc is None else acc + t
        o_ref[...] = acc.astype(BF16)

    return pl.pallas_call(
        body, name=name, grid=(T // bm, D // bn),
        in_specs=[y_spec] * NBR + [wb_spec] + gate_specs, out_specs=t_spec,
        out_shape=jax.ShapeDtypeStruct((T, D), BF16),
        compiler_params=_cp("parallel", "parallel"),
    )(*ys, wb, proj, proj, proj, proj)


def _merge_bwd(name, dm, ys, wb, proj):
    T = proj.shape[0]
    bm = _pick(T, (512, 256, 128))
    bn = 512
    y_spec, wb_spec, gate_specs, t_spec = _merge_specs(T, bm, bn)

    def body(dm_ref, y0, y1, y2, y3, wb_ref, g0, g1, g2, g3, *outs):
        dmv = dm_ref[...]
        for g, (y_ref, g_ref) in enumerate(((y0, g0), (y1, g1), (y2, g2), (y3, g3))):
            br = _dot(y_ref[...], wb_ref[g], _NT)
            s = _sigmoid(g_ref[...])
            outs[g][...] = (dmv * br * (s * (1.0 - s))).astype(BF16)
            outs[NBR + g][...] = (dmv * s).astype(BF16)

    return pl.pallas_call(
        body, name=name, grid=(T // bm, D // bn),
        in_specs=[t_spec] + [y_spec] * NBR + [wb_spec] + gate_specs, out_specs=[t_spec] * (2 * NBR),
        out_shape=[jax.ShapeDtypeStruct((T, D), BF16)] * (2 * NBR),
        compiler_params=_cp("parallel", "parallel"),
    )(dm, *ys, wb, proj, proj, proj, proj)


def _sgu_block(u_pre, v_pre, ln_g, ln_b, w, bias):
    u = _gelu(u_pre)
    vf = _gelu(v_pre)
    mu = jnp.mean(vf, axis=-1, keepdims=True)
    var = jnp.mean(jnp.square(vf - mu), axis=-1, keepdims=True)
    vn = (vf - mu) * lax.rsqrt(var + EPS) * ln_g + ln_b
    ri = lax.broadcasted_iota(jnp.int32, (128, 128), 0)
    ci = lax.broadcasted_iota(jnp.int32, (128, 128), 1)
    mask = (ri // CHUNK) >= (ci // CHUNK)
    outs = [_dot(jnp.where(mask, w[g], 0.0), vn[:, g * 128:(g + 1) * 128]) for g in range(4)]
    mixed = jnp.concatenate(outs, axis=1) + bias
    return u * mixed


def _sgu_param_specs():
    return [pl.BlockSpec((1, BW), lambda i: (0, 0)), pl.BlockSpec((1, BW), lambda i: (0, 0)),
            pl.BlockSpec((4, 128, 128), lambda i: (0, 0, 0)), pl.BlockSpec((128, BW), lambda i: (0, 0))]


def _sgu_fwd(name, proj, ln_g, ln_b, w, bias):
    T = proj.shape[0]
    rb = _pick(T, (256, 128))

    def body(u_ref, v_ref, g_ref, b_ref, w_ref, bias_ref, y_ref):
        for n in range(rb // 128):
            rows = slice(n * 128, (n + 1) * 128)
            y = _sgu_block(u_ref[rows, :], v_ref[rows, :], g_ref[...], b_ref[...], w_ref[...], bias_ref[...])
            y_ref[rows, :] = y.astype(BF16)

    return pl.pallas_call(
        body, name=name, grid=(T // rb,),
        in_specs=[pl.BlockSpec((rb, BW), lambda i: (i, COL_AU // BW)), pl.BlockSpec((rb, BW), lambda i: (i, COL_AV // BW))]
        + _sgu_param_specs(),
        out_specs=pl.BlockSpec((rb, BW), lambda i: (i, 0)),
        out_shape=jax.ShapeDtypeStruct((T, BW), BF16),
        compiler_params=_cp("parallel"),
    )(proj, proj, ln_g, ln_b, w, bias)


def _sgu_bwd(name, proj, dy, ln_g, ln_b, w, bias):
    T = proj.shape[0]
    rb = _pick(T, (256, 128))

    def body(u_ref, v_ref, dy_ref, g_ref, b_ref, w_ref, bias_ref, du_ref, dv_ref, dg_ref, db_ref, dw_ref, dbias_ref):
        acc = None
        for n in range(rb // 128):
            rows = slice(n * 128, (n + 1) * 128)
            _, vjp = jax.vjp(_sgu_block, u_ref[rows, :], v_ref[rows, :], g_ref[...], b_ref[...], w_ref[...],
                             bias_ref[...])
            du, dv, *dp = vjp(dy_ref[rows, :])
            du_ref[rows, :] = du.astype(BF16)
            dv_ref[rows, :] = dv.astype(BF16)
            acc = dp if acc is None else [p + q for p, q in zip(acc, dp)]

        @pl.when(pl.program_id(0) == 0)
        def _():
            for r, p in zip((dg_ref, db_ref, dw_ref, dbias_ref), acc):
                r[...] = p

        @pl.when(pl.program_id(0) > 0)
        def _():
            for r, p in zip((dg_ref, db_ref, dw_ref, dbias_ref), acc):
                r[...] += p

    row = pl.BlockSpec((rb, BW), lambda i: (i, 0))
    return pl.pallas_call(
        body, name=name, grid=(T // rb,),
        in_specs=[pl.BlockSpec((rb, BW), lambda i: (i, COL_AU // BW)), pl.BlockSpec((rb, BW), lambda i: (i, COL_AV // BW)),
                  row] + _sgu_param_specs(),
        out_specs=[row, row] + _sgu_param_specs(),
        out_shape=[jax.ShapeDtypeStruct((T, BW), BF16), jax.ShapeDtypeStruct((T, BW), BF16),
                   jax.ShapeDtypeStruct((1, BW), F32), jax.ShapeDtypeStruct((1, BW), F32),
                   jax.ShapeDtypeStruct((4, 128, 128), F32), jax.ShapeDtypeStruct((128, BW), F32)],
        compiler_params=_cp("arbitrary"),
    )(proj, proj, dy, ln_g, ln_b, w, bias)


def _halo_block(ref, i, rblk, halo):
    r0 = pl.multiple_of(i * rblk, rblk)
    h0 = pl.multiple_of(jnp.maximum(r0 - halo, 0), halo)
    top = jnp.where(i > 0, ref[pl.ds(h0, halo), :], 0.0)
    return jnp.concatenate([top, ref[pl.ds(r0, rblk), :]], axis=0)


def _with_halo_grad(dfull, pending, halo, rblk):
    tail = jnp.concatenate([jnp.zeros((rblk - halo, 128), F32), pending], axis=0)
    return dfull[halo:] + tail


def _conv4(xfull, rows):
    acc = None
    for k in range(4):
        t = rows[k] * _shift(xfull, 3 - k)[8:]
        acc = t if acc is None else acc + t
    return acc


def _lru_block(xfull, gate, h0, c0, c1, c2, c3, cb, wa, ba, wx, bx, lam):
    n = gate.shape[0]
    xc = _conv4(xfull, (c0, c1, c2, c3)) + cb
    r = _sigmoid(_dot(xc, wa) + ba)
    ig = _sigmoid(_dot(xc, wx) + bx)
    log_a = -LRU_C * r * _softplus(-lam)
    a = jnp.exp(log_a)
    mult = jnp.sqrt(-_expm1(2.0 * log_a))
    b = mult * (ig * xc)
    row = lax.broadcasted_iota(jnp.int32, (n, 128), 0)
    b = b + jnp.where(row == 0, a * h0, 0.0)
    h = _scan(a, b)
    out = h * _gelu(gate)
    h_last = jnp.sum(jnp.where(row == n - 1, h, 0.0), axis=0, keepdims=True)
    return out, h_last


def _lru_param_specs():
    vec = pl.BlockSpec((1, 128), lambda g: (0, g))
    mat = pl.BlockSpec((None, 128, 128), lambda g: (g, 0, 0))
    return [pl.BlockSpec((4, 128), lambda g: (0, g)), vec, mat, vec, mat, vec, vec]


def _lru_load_params(cw_ref, cb_ref, wa_ref, ba_ref, wx_ref, bx_ref, lam_ref):
    return (cw_ref[0:1, :], cw_ref[1:2, :], cw_ref[2:3, :], cw_ref[3:4, :], cb_ref[...], wa_ref[...], ba_ref[...],
            wx_ref[...], bx_ref[...], lam_ref[...])


def _lru_fwd(name, proj, cw, cb, wa, ba, wx, bx, lam):
    T = proj.shape[0]
    rblk = _pick(T, (256, 128))
    nblk = T // rblk

    def body(x_ref, gt_ref, cw_ref, cb_ref, wa_ref, ba_ref, wx_ref, bx_ref, lam_ref, y_ref, hc_ref):
        params = _lru_load_params(cw_ref, cb_ref, wa_ref, ba_ref, wx_ref, bx_ref, lam_ref)

        def step(i, h0):
            r0 = pl.multiple_of(i * rblk, rblk)
            out, h_last = _lru_block(_halo_block(x_ref, i, rblk, 8), gt_ref[pl.ds(r0, rblk), :], h0, *params)
            y_ref[pl.ds(r0, rblk), :] = out.astype(BF16)
            hc_ref[pl.ds(pl.multiple_of(i * 8, 8), 8), :] = jnp.broadcast_to(h0, (8, 128))
            return h_last

        lax.fori_loop(0, nblk, step, jnp.zeros((1, 128), F32))

    return pl.pallas_call(
        body, name=name, grid=(4,),
        in_specs=[pl.BlockSpec((T, 128), lambda g: (0, COL_BX // 128 + g)),
                  pl.BlockSpec((T, 128), lambda g: (0, COL_BG // 128 + g))] + _lru_param_specs(),
        out_specs=[pl.BlockSpec((T, 128), lambda g: (0, g)), pl.BlockSpec((nblk * 8, 128), lambda g: (0, g))],
        out_shape=[jax.ShapeDtypeStruct((T, BW), BF16), jax.ShapeDtypeStruct((nblk * 8, BW), F32)],
        compiler_params=_cp("parallel"),
    )(proj, proj, cw, cb, wa, ba, wx, bx, lam)


def _lru_bwd(name, proj, dy, hc, cw, cb, wa, ba, wx, bx, lam):
    T = proj.shape[0]
    rblk = _pick(T, (256, 128))
    nblk = T // rblk

    def body(x_ref, gt_ref, dy_ref, hc_ref, cw_ref, cb_ref, wa_ref, ba_ref, wx_ref, bx_ref, lam_ref,
             dx_ref, dgt_ref, dcw_ref, dcb_ref, dwa_ref, dba_ref, dwx_ref, dbx_ref, dlam_ref):
        params = _lru_load_params(cw_ref, cb_ref, wa_ref, ba_ref, wx_ref, bx_ref, lam_ref)

        def step(it, carry):
            dh_last, pending, acc = carry
            i = nblk - 1 - it
            r0 = pl.multiple_of(i * rblk, rblk)
            h0 = hc_ref[pl.ds(pl.multiple_of(i * 8, 8), 1), :]
            _, vjp = jax.vjp(_lru_block, _halo_block(x_ref, i, rblk, 8), gt_ref[pl.ds(r0, rblk), :], h0, *params)
            dfull, dgate, dh0, *dp = vjp((dy_ref[pl.ds(r0, rblk), :], dh_last))
            dx_ref[pl.ds(r0, rblk), :] = _with_halo_grad(dfull, pending, 8, rblk).astype(BF16)
            dgt_ref[pl.ds(r0, rblk), :] = dgate.astype(BF16)
            return dh0, dfull[:8], tuple(p + q for p, q in zip(acc, dp))

        zeros = tuple(jnp.zeros(p.shape, F32) for p in params)
        _, _, acc = lax.fori_loop(0, nblk, step, (jnp.zeros((1, 128), F32), jnp.zeros((8, 128), F32), zeros))
        for k in range(4):
            dcw_ref[k:k + 1, :] = acc[k]
        for r, p in zip((dcb_ref, dwa_ref, dba_ref, dwx_ref, dbx_ref, dlam_ref), acc[4:]):
            r[...] = p

    col = pl.BlockSpec((T, 128), lambda g: (0, g))
    return pl.pallas_call(
        body, name=name, grid=(4,),
        in_specs=[pl.BlockSpec((T, 128), lambda g: (0, COL_BX // 128 + g)),
                  pl.BlockSpec((T, 128), lambda g: (0, COL_BG // 128 + g)), col,
                  pl.BlockSpec((nblk * 8, 128), lambda g: (0, g))] + _lru_param_specs(),
        out_specs=[col, col] + _lru_param_specs(),
        out_shape=[jax.ShapeDtypeStruct((T, BW), BF16), jax.ShapeDtypeStruct((T, BW), BF16),
                   jax.ShapeDtypeStruct((4, BW), F32), jax.ShapeDtypeStruct((1, BW), F32),
                   jax.ShapeDtypeStruct((4, 128, 128), F32), jax.ShapeDtypeStruct((1, BW), F32),
                   jax.ShapeDtypeStruct((4, 128, 128), F32), jax.ShapeDtypeStruct((1, BW), F32),
                   jax.ShapeDtypeStruct((1, BW), F32)],
        compiler_params=_cp("parallel"),
    )(proj, proj, dy, hc, cw, cb, wa, ba, wx, bx, lam)


def _conv_block(xfull, c0, c1, c2, c3):
    return _silu(_conv4(xfull, (c0, c1, c2, c3)))


def _conv_fwd(name, proj, col0, cw, cw_col0):
    T = proj.shape[0]
    rblk = _pick(T, (256, 128))
    nblk = T // rblk

    def body(x_ref, cw_ref, y_ref):
        rows = (cw_ref[0:1, :], cw_ref[1:2, :], cw_ref[2:3, :], cw_ref[3:4, :])

        def step(i, c):
            r0 = pl.multiple_of(i * rblk, rblk)
            y_ref[pl.ds(r0, rblk), :] = _conv_block(_halo_block(x_ref, i, rblk, 8), *rows)
            return c

        lax.fori_loop(0, nblk, step, 0)

    return pl.pallas_call(
        body, name=name, grid=(4,),
        in_specs=[pl.BlockSpec((T, 128), lambda g: (0, col0 // 128 + g)),
                  pl.BlockSpec((4, 128), lambda g: (0, cw_col0 // 128 + g))],
        out_specs=pl.BlockSpec((T, 128), lambda g: (0, g)),
        out_shape=jax.ShapeDtypeStruct((T, BW), F32),
        compiler_params=_cp("parallel"),
    )(proj, cw)


def _conv_bwd(name, proj, col0, dy, cw, cw_col0):
    T = proj.shape[0]
    rblk = _pick(T, (256, 128))
    nblk = T // rblk

    def body(x_ref, dy_ref, cw_ref, dx_ref, dcw_ref):
        rows = (cw_ref[0:1, :], cw_ref[1:2, :], cw_ref[2:3, :], cw_ref[3:4, :])

        def step(it, carry):
            pending, acc = carry
            i = nblk - 1 - it
            r0 = pl.multiple_of(i * rblk, rblk)
            _, vjp = jax.vjp(_conv_block, _halo_block(x_ref, i, rblk, 8), *rows)
            dfull, *dp = vjp(dy_ref[pl.ds(r0, rblk), :])
            dx_ref[pl.ds(r0, rblk), :] = _with_halo_grad(dfull, pending, 8, rblk).astype(BF16)
            return dfull[:8], tuple(p + q for p, q in zip(acc, dp))

        zeros = tuple(jnp.zeros((1, 128), F32) for _ in range(4))
        _, acc = lax.fori_loop(0, nblk, step, (jnp.zeros((8, 128), F32), zeros))
        for k in range(4):
            dcw_ref[k:k + 1, :] = acc[k]

    col = pl.BlockSpec((T, 128), lambda g: (0, g))
    return pl.pallas_call(
        body, name=name, grid=(4,),
        in_specs=[pl.BlockSpec((T, 128), lambda g: (0, col0 // 128 + g)), col,
                  pl.BlockSpec((4, 128), lambda g: (0, cw_col0 // 128 + g))],
        out_specs=[col, pl.BlockSpec((4, 128), lambda g: (0, g))],
        out_shape=[jax.ShapeDtypeStruct((T, BW), BF16), jax.ShapeDtypeStruct((4, BW), F32)],
        compiler_params=_cp("parallel"),
    )(proj, dy, cw)


def _pool_block(xfull, pw, sc, t0, gi):
    n = xfull.shape[0] - 16
    s2 = xfull + _shift(xfull, 1)
    s4 = s2 + _shift(s2, 2)
    s8 = s4 + _shift(s4, 4)
    s16 = s8 + _shift(s8, 8)
    s = jnp.where(gi == 0, s2, jnp.where(gi == 1, s4, jnp.where(gi == 2, s8, s16)))[16:]
    t = t0 + lax.broadcasted_iota(jnp.int32, (n, 128), 0)
    cnt = jnp.minimum(t + 1, lax.shift_left(jnp.int32(2), gi)).astype(F32)
    pooled = s / cnt - xfull[16:]
    return _dot(pooled, pw) * sc


def _pool_fwd(name, proj, pw, sc):
    T = proj.shape[0]
    rblk = _pick(T, (256, 128))
    nblk = T // rblk

    def body(x_ref, pw_ref, sc_ref, y_ref):
        gi = pl.program_id(0)

        def step(i, c):
            r0 = pl.multiple_of(i * rblk, rblk)
            y = _pool_block(_halo_block(x_ref, i, rblk, 16), pw_ref[...], sc_ref[...], r0, gi)
            y_ref[pl.ds(r0, rblk), :] = y.astype(BF16)
            return c

        lax.fori_loop(0, nblk, step, 0)

    return pl.pallas_call(
        body, name=name, grid=(4,),
        in_specs=[pl.BlockSpec((T, 128), lambda g: (0, COL_DX // 128 + g)),
                  pl.BlockSpec((None, 128, 128), lambda g: (g, 0, 0)), pl.BlockSpec((1, 128), lambda g: (0, g))],
        out_specs=pl.BlockSpec((T, 128), lambda g: (0, g)),
        out_shape=jax.ShapeDtypeStruct((T, BW), BF16),
        compiler_params=_cp("parallel"),
    )(proj, pw, sc)


def _pool_bwd(name, proj, dy, pw, sc):
    T = proj.shape[0]
    rblk = _pick(T, (256, 128))
    nblk = T // rblk

    def body(x_ref, dy_ref, pw_ref, sc_ref, dx_ref, dpw_ref, dsc_ref):
        gi = pl.program_id(0)

        def step(it, carry):
            pending, apw, asc = carry
            i = nblk - 1 - it
            r0 = pl.multiple_of(i * rblk, rblk)
            _, vjp = jax.vjp(lambda xf, w, s: _pool_block(xf, w, s, r0, gi), _halo_block(x_ref, i, rblk, 16),
                             pw_ref[...], sc_ref[...])
            dfull, dw, ds = vjp(dy_ref[pl.ds(r0, rblk), :])
            dx_ref[pl.ds(r0, rblk), :] = _with_halo_grad(dfull, pending, 16, rblk).astype(BF16)
            return dfull[:16], apw + dw, asc + ds

        _, apw, asc = lax.fori_loop(0, nblk, step, (jnp.zeros((16, 128), F32), jnp.zeros((128, 128), F32),
                                                    jnp.zeros((1, 128), F32)))
        dpw_ref[...] = apw
        dsc_ref[...] = asc

    col = pl.BlockSpec((T, 128), lambda g: (0, g))
    mat = pl.BlockSpec((None, 128, 128), lambda g: (g, 0, 0))
    vec = pl.BlockSpec((1, 128), lambda g: (0, g))
    return pl.pallas_call(
        body, name=name, grid=(4,),
        in_specs=[pl.BlockSpec((T, 128), lambda g: (0, COL_DX // 128 + g)), col, mat, vec],
        out_specs=[col, mat, vec],
        out_shape=[jax.ShapeDtypeStruct((T, BW), BF16), jax.ShapeDtypeStruct((4, 128, 128), F32),
                   jax.ShapeDtypeStruct((1, BW), F32)],
        compiler_params=_cp("parallel"),
    )(proj, dy, pw, sc)


def _dot3(a, b):
    ah = a.astype(BF16)
    al = (a - ah.astype(F32)).astype(BF16)
    bh = b.astype(BF16)
    bl = (b - bh.astype(F32)).astype(BF16)

    def d(x, y):
        return lax.dot_general(x, y, _NN, preferred_element_type=F32)

    return d(ah, bh) + (d(ah, bl) + d(al, bh))


def _tri_inv(mats):
    n = mats[0].shape[0]
    eye = (lax.broadcasted_iota(jnp.int32, (n, n), 0) == lax.broadcasted_iota(jnp.int32, (n, n), 1)).astype(F32)
    ps = [eye - a for a in mats]
    ms = list(mats)
    k = 2
    while k < n:
        ms = [_dot3(m, m) for m in ms]
        ps = [p + _dot3(p, m) for p, m in zip(ps, ms)]
        k *= 2
    return ps


def _cumsum_rows(x):
    n = x.shape[0]
    row = lax.broadcasted_iota(jnp.int32, x.shape, 0)
    k = 1
    while k < n:
        x = x + jnp.where(row >= k, _shift(x, k), 0.0)
        k *= 2
    return x


def _gdn_chunk(states, qc, kc, vc, z, tail, alog, dtb, ng):
    C, H = CHUNK, 4
    hs = range(H)
    lane = lax.broadcasted_iota(jnp.int32, (C, 128), 1)
    row = lax.broadcasted_iota(jnp.int32, (C, 128), 0)
    ri = lax.broadcasted_iota(jnp.int32, (C, C), 0)
    ci = lax.broadcasted_iota(jnp.int32, (C, C), 1)
    incl = ri >= ci
    sig = _sigmoid(tail)
    gfull = -jnp.exp(alog) * _softplus(tail + dtb)
    beta = [jnp.sum(jnp.where(lane == h, sig, 0.0), axis=1, keepdims=True) for h in hs]
    g = [jnp.sum(jnp.where(lane == h + 4, gfull, 0.0), axis=1, keepdims=True) for h in hs]
    qs = [qc[:, h * 128:(h + 1) * 128] for h in hs]
    ks = [kc[:, h * 128:(h + 1) * 128] for h in hs]
    vs = [vc[:, h * 128:(h + 1) * 128] for h in hs]
    q = [t * lax.rsqrt(jnp.sum(t * t, axis=-1, keepdims=True) + EPS) * (GDN_DK ** -0.5) for t in qs]
    k = [t * lax.rsqrt(jnp.sum(t * t, axis=-1, keepdims=True) + EPS) for t in ks]
    gc = [_cumsum_rows(jnp.broadcast_to(t, (C, 128))) for t in g]
    gc_row = [jnp.transpose(t)[:C, :] for t in gc]
    gc_col = [jnp.sum(jnp.where(lane == 0, t, 0.0), axis=1, keepdims=True) for t in gc]
    decay = [jnp.exp(jnp.where(incl, gc_col[h] - gc_row[h], -1e30)) for h in hs]
    kb = [k[h] * beta[h] for h in hs]
    kk = [_dot(kb[h], k[h], _NT) for h in hs]
    t_mat = _tri_inv([jnp.where(ri > ci, kk[h] * decay[h], 0.0) for h in hs])
    egc = [jnp.exp(t) for t in gc]
    u = [_dot(t_mat[h], vs[h] * beta[h]) for h in hs]
    w = [_dot(t_mat[h], kb[h] * egc[h]) for h in hs]
    qk = [_dot(q[h], k[h], _NT) for h in hs]
    attn = [jnp.where(incl, qk[h] * decay[h], 0.0) for h in hs]
    ws = [_dot(w[h], states[h]) for h in hs]
    qs_ = [_dot(q[h] * egc[h], states[h]) for h in hs]
    v_new = [u[h] - ws[h] for h in hs]
    av = [_dot(attn[h], v_new[h]) for h in hs]
    g_last = [jnp.sum(jnp.where(row == C - 1, t, 0.0), axis=0, keepdims=True) for t in gc]
    kv = [_dot(k[h] * jnp.exp(g_last[h] - gc[h]), v_new[h], _TN) for h in hs]
    nxt = tuple(states[h] * jnp.exp(g_last[h]) + kv[h] for h in hs)
    o = [qs_[h] + av[h] for h in hs]
    on = [t * lax.rsqrt(jnp.mean(t * t, axis=-1, keepdims=True) + EPS) * ng for t in o]
    return nxt, jnp.concatenate(on, axis=1) * _silu(z)


def _gdn_blocks(T):
    tb = _pick(T, (512, 256, 128, 64))
    return tb, T // tb, tb // CHUNK


def _gdn_fwd(name, qa, ka, va, proj, alog, dtb, ng):
    T = proj.shape[0]
    tb, nb, ncb = _gdn_blocks(T)

    def body(q_ref, k_ref, v_ref, z_ref, tail_ref, alog_ref, dtb_ref, ng_ref, y_ref, sh_ref, state):
        @pl.when(pl.program_id(0) == 0)
        def _():
            state[...] = jnp.zeros((4, 128, 128), F32)

        def step(c, states):
            rows = pl.ds(pl.multiple_of(c * CHUNK, CHUNK), CHUNK)
            for h in range(4):
                sh_ref[h, c] = states[h]
            nxt, y = _gdn_chunk(states, q_ref[rows, :], k_ref[rows, :], v_ref[rows, :], z_ref[rows, :],
                                tail_ref[rows, :], alog_ref[...], dtb_ref[...], ng_ref[...])
            y_ref[rows, :] = y.astype(BF16)
            return nxt

        states = lax.fori_loop(0, ncb, step, tuple(state[h] for h in range(4)))
        for h in range(4):
            state[h] = states[h]

    blk = pl.BlockSpec((tb, BW), lambda j: (j, 0))
    vec = pl.BlockSpec((1, 128), lambda j: (0, 0))
    return pl.pallas_call(
        body, name=name, grid=(nb,),
        in_specs=[blk, blk, blk, pl.BlockSpec((tb, BW), lambda j: (j, COL_CZ // BW)),
                  pl.BlockSpec((tb, 128), lambda j: (j, COL_TAIL // 128)), vec, vec, vec],
        out_specs=[blk, pl.BlockSpec((4, ncb, 128, 128), lambda j: (0, j, 0, 0))],
        out_shape=[jax.ShapeDtypeStruct((T, BW), BF16), jax.ShapeDtypeStruct((4, T // CHUNK, 128, 128), F32)],
        scratch_shapes=[pltpu.VMEM((4, 128, 128), F32)],
        compiler_params=_cp("arbitrary"),
    )(qa, ka, va, proj, proj, alog, dtb, ng)


def _gdn_bwd(name, qa, ka, va, proj, dy, sh, alog, dtb, ng):
    T = proj.shape[0]
    tb, nb, ncb = _gdn_blocks(T)

    def body(q_ref, k_ref, v_ref, z_ref, tail_ref, dy_ref, sh_ref, alog_ref, dtb_ref, ng_ref,
             dq_ref, dk_ref, dv_ref, dz_ref, dtail_ref, dalog_ref, ddtb_ref, dng_ref, dstate):
        first = pl.program_id(0) == 0

        @pl.when(first)
        def _():
            dstate[...] = jnp.zeros((4, 128, 128), F32)

        def step(it, carry):
            dstates, pa, pd, pn = carry
            c = ncb - 1 - it
            rows = pl.ds(pl.multiple_of(c * CHUNK, CHUNK), CHUNK)
            _, vjp = jax.vjp(_gdn_chunk, tuple(sh_ref[h, c] for h in range(4)), q_ref[rows, :], k_ref[rows, :],
                             v_ref[rows, :], z_ref[rows, :], tail_ref[rows, :], alog_ref[...], dtb_ref[...], ng_ref[...])
            nxt, dq, dk, dv, dz, dtail, da, dd, dn = vjp((dstates, dy_ref[rows, :]))
            dq_ref[rows, :] = dq
            dk_ref[rows, :] = dk
            dv_ref[rows, :] = dv
            dz_ref[rows, :] = dz.astype(BF16)
            dtail_ref[rows, :] = dtail.astype(BF16)
            return nxt, pa + da, pd + dd, pn + dn

        zv = jnp.zeros((1, 128), F32)
        dstates, pa, pd, pn = lax.fori_loop(0, ncb, step, (tuple(dstate[h] for h in range(4)), zv, zv, zv))
        for h in range(4):
            dstate[h] = dstates[h]

        @pl.when(first)
        def _():
            dalog_ref[...] = pa
            ddtb_ref[...] = pd
            dng_ref[...] = pn

        @pl.when(jnp.logical_not(first))
        def _():
            dalog_ref[...] += pa
            ddtb_ref[...] += pd
            dng_ref[...] += pn

    blk = pl.BlockSpec((tb, BW), lambda j: (nb - 1 - j, 0))
    vec = pl.BlockSpec((1, 128), lambda j: (0, 0))
    return pl.pallas_call(
        body, name=name, grid=(nb,),
        in_specs=[blk, blk, blk, pl.BlockSpec((tb, BW), lambda j: (nb - 1 - j, COL_CZ // BW)),
                  pl.BlockSpec((tb, 128), lambda j: (nb - 1 - j, COL_TAIL // 128)), blk,
                  pl.BlockSpec((4, ncb, 128, 128), lambda j: (0, nb - 1 - j, 0, 0)), vec, vec, vec],
        out_specs=[blk, blk, blk, blk, pl.BlockSpec((tb, 128), lambda j: (nb - 1 - j, 0)), vec, vec, vec],
        out_shape=[jax.ShapeDtypeStruct((T, BW), F32)] * 3
        + [jax.ShapeDtypeStruct((T, BW), BF16), jax.ShapeDtypeStruct((T, 128), BF16)]
        + [jax.ShapeDtypeStruct((1, 128), F32)] * 3,
        scratch_shapes=[pltpu.VMEM((4, 128, 128), F32)],
        compiler_params=_cp("arbitrary"),
    )(qa, ka, va, proj, proj, dy, sh, alog, dtb, ng)


def _adamw(name, w, g, m, v):
    R, C = w.shape
    br = _pick(R, (512, 256, 240, 128, 64, 8))

    def body(w_ref, g_ref, m_ref, v_ref, d_ref, nm_ref, nv_ref):
        gv = g_ref[...]
        m2 = ADAM_B1 * m_ref[...] + (1.0 - ADAM_B1) * gv
        v2 = ADAM_B2 * v_ref[...] + (1.0 - ADAM_B2) * jnp.square(gv)
        m_hat = m2 / (1.0 - ADAM_B1 ** ADAM_STEP)
        v_hat = v2 / (1.0 - ADAM_B2 ** ADAM_STEP)
        d_ref[...] = -ADAM_LR * (m_hat / (jnp.sqrt(v_hat) + ADAM_EPS) + ADAM_WD * w_ref[...])
        nm_ref[...] = m2
        nv_ref[...] = v2

    spec = pl.BlockSpec((br, C), lambda i: (i, 0))
    return pl.pallas_call(
        body, name=name, grid=(R // br,),
        in_specs=[spec] * 4, out_specs=[spec] * 3,
        out_shape=[jax.ShapeDtypeStruct((R, C), F32)] * 3,
        compiler_params=_cp("parallel"),
    )(w, g, m, v)


def _sum8(name, parts):
    _, R, C = parts.shape
    br = _pick(R, (352, 496, 256, 128, 64, 16, 8))

    def body(p_ref, o_ref):
        acc = p_ref[0].astype(F32)
        for d in range(1, N_DEV):
            acc = acc + p_ref[d].astype(F32)
        o_ref[...] = acc

    return pl.pallas_call(
        body, name=name, grid=(R // br,),
        in_specs=[pl.BlockSpec((N_DEV, br, C), lambda i: (0, i, 0))],
        out_specs=pl.BlockSpec((br, C), lambda i: (i, 0)),
        out_shape=jax.ShapeDtypeStruct((R, C), F32),
        compiler_params=_cp("parallel"),
    )(parts)


_ANY = pl.BlockSpec(memory_space=pl.ANY)
_MESH = pl.DeviceIdType.MESH


def _all_gather(name, shard):
    R, C = shard.shape

    def body(x_ref, out_ref, send_sems, recv_sems, local_sem):
        x, y, c = lax.axis_index("x"), lax.axis_index("y"), lax.axis_index("c")
        me, sibling = (x, y, c), (x, y, 1 - c)
        chips = [(1 - x, y), (x, 1 - y), (1 - x, 1 - y)]

        def slot(px, py, pc):
            return out_ref.at[4 * px + 2 * py + pc]

        def copy(k, block, to, src=None):
            return pltpu.make_async_remote_copy(
                src_ref=slot(*block) if src is None else src, dst_ref=slot(*block),
                send_sem=send_sems.at[k], recv_sem=recv_sems.at[k], device_id=to, device_id_type=_MESH)

        mine = pltpu.make_async_copy(x_ref, slot(*me), local_sem)
        mine.start()
        first = [copy(0, me, sibling, src=x_ref)]
        first += [copy(1 + j, me, (*chip, c), src=x_ref) for j, chip in enumerate(chips)]
        for cp in first:
            cp.start()
        passed = [copy(4 + j, (*chip, c), sibling) for j, chip in enumerate(chips)]
        for j, chip in enumerate(chips):
            copy(1 + j, (*chip, c), me).wait_recv()
            passed[j].start()
        copy(0, sibling, me).wait_recv()
        for j, chip in enumerate(chips):
            copy(4 + j, (*chip, 1 - c), me).wait_recv()
        for cp in first + passed:
            cp.wait_send()
        mine.wait()

    return pl.pallas_call(
        body, name=name,
        in_specs=[_ANY], out_specs=_ANY,
        out_shape=jax.ShapeDtypeStruct((N_DEV, R, C), shard.dtype),
        scratch_shapes=[pltpu.SemaphoreType.DMA((7,)), pltpu.SemaphoreType.DMA((7,)), pltpu.SemaphoreType.DMA],
    )(shard)


_HBM = pl.BlockSpec(memory_space=pltpu.HBM)
_SEM = pl.BlockSpec(memory_space=pltpu.SEMAPHORE)
_EFFECT = pltpu.SideEffectType.DATAFLOW_SIDE_EFFECTING


def _exchange_copies(src_ref, land_ref, send_sems, recv_sems, scatter):
    x, y, c = lax.axis_index("x"), lax.axis_index("y"), lax.axis_index("c")
    me = 4 * x + 2 * y + c
    copies = []
    for k in range(1, N_DEV):
        px, py, pc = x ^ ((k >> 2) & 1), y ^ ((k >> 1) & 1), c ^ (k & 1)
        src = src_ref.at[4 * px + 2 * py + pc] if scatter else src_ref
        copies.append(pltpu.make_async_remote_copy(
            src_ref=src, dst_ref=land_ref.at[me], send_sem=send_sems.at[k - 1], recv_sem=recv_sems.at[k - 1],
            device_id=(px, py, pc), device_id_type=_MESH))
    return copies


def _exchange_start(name, srcs, lands, scatter):
    n = len(srcs)

    def body(*refs):
        src_refs, land_refs = refs[:n], refs[n:2 * n]
        send, recv = refs[2 * n:3 * n], refs[3 * n:4 * n]
        token = refs[-1]
        for g in range(n):
            for cp in _exchange_copies(src_refs[g], land_refs[g], send[g], recv[g], scatter):
                cp.start()
        token[...] = jnp.zeros_like(token)

    outs = pl.pallas_call(
        body, name=name,
        out_shape=tuple([pltpu.SemaphoreType.DMA((N_DEV - 1,))] * (2 * n)
                        + [pltpu.HBM(a.shape, a.dtype) for a in list(srcs) + list(lands)]
                        + [jax.ShapeDtypeStruct((8, 128), F32)]),
        in_specs=[_HBM] * (2 * n),
        out_specs=tuple([_SEM] * (2 * n) + [_HBM] * (2 * n) + [pl.BlockSpec(memory_space=pltpu.VMEM)]),
        input_output_aliases={i: 2 * n + i for i in range(2 * n)},
        compiler_params=pltpu.CompilerParams(has_side_effects=_EFFECT),
    )(*[pltpu.with_memory_space_constraint(a, pltpu.HBM) for a in list(srcs) + list(lands)])
    handles = [(outs[2 * n + g], outs[3 * n + g], outs[g], outs[n + g]) for g in range(n)]
    return handles, outs[-1]


def _exchange_wait(name, src, land, send_sems, recv_sems, after, scatter):
    def body(src_ref, land_ref, send, recv, after_ref, src_out, land_out):
        for cp in _exchange_copies(src_ref, land_ref, send, recv, scatter):
            cp.wait_send()
            cp.wait_recv()

    return pl.pallas_call(
        body, name=name,
        out_shape=(pltpu.HBM(src.shape, src.dtype), pltpu.HBM(land.shape, land.dtype)),
        in_specs=(_HBM, _HBM, _SEM, _SEM, _ANY), out_specs=(_HBM, _HBM),
        input_output_aliases={0: 0, 1: 1},
        compiler_params=pltpu.CompilerParams(has_side_effects=_EFFECT),
    )(src, land, send_sems, recv_sems, after)[1]


def _rows(a):
    return a.reshape(-1, 1024)


def _rows_to_parts(full):
    n = full.shape[-2] // N_DEV
    t = full.reshape(full.shape[:-2] + (N_DEV, n, full.shape[-1]))
    return jnp.moveaxis(t, -3, 0)


def _parts_to_rows(parts):
    t = jnp.moveaxis(parts, 0, -3)
    return t.reshape(t.shape[:-3] + (t.shape[-3] * t.shape[-2], t.shape[-1]))


def _parts_to_cols(parts):
    t = jnp.moveaxis(parts, 0, -2)
    return t.reshape(t.shape[:-2] + (t.shape[-2] * t.shape[-1],))


def _join(parts, axis=0):
    total = sum(p.shape[axis] for p in parts)
    out, off = None, 0
    for p in parts:
        cfg = [(0, 0)] * p.ndim
        cfg[axis] = (off, total - off - p.shape[axis])
        t = jnp.pad(p, cfg)
        out = t if out is None else out + t
        off += p.shape[axis]
    return out


def _w_in_to_layout(w):
    tail = jnp.pad(w[4096:4104], ((0, PW - COL_TAIL - 8), (0, 0)))
    return jnp.concatenate([w[:4096], w[4104:P_IN], tail], axis=0)


def _w_in_from_layout(g):
    return _join([g[:4096], g[COL_TAIL:COL_TAIL + 8], g[4096:COL_TAIL]], axis=0)


def _block_diag(w):
    w = w.reshape(4, 2, 64, 64)
    return jnp.pad(w[:, 0], ((0, 0), (0, 64), (0, 64))) + jnp.pad(w[:, 1], ((0, 0), (64, 0), (64, 0)))


def _block_diag_grad(g):
    return jnp.stack([g[:, :64, :64], g[:, 64:, 64:]], axis=1).reshape(8, 64, 64)


def _ffn_forward(tag, x, norm, wg, wu, wd):
    h = _rms_fwd(tag + "_norm", x, norm)
    a, b, act = _ffn_up(tag + "_up", h, wg, wu)
    x_out = _mm(tag + "_down", [(act, wd)], "nn", F32, res=x, scale=0.5)
    return x_out, (x, h, a, b, act)


def _ffn_backward(tag, dx_out, saved, norm, wg, wu, wd, put):
    x, h, a, b, act = saved
    da, db = _ffn_dact(tag + "_dact", dx_out, wd, a, b)
    dwd = _mm(tag + "_dwd", [(act, dx_out)], "tn", F32, scale=0.5, bm=FF // 2)
    dwg = _mm(tag + "_dwg", [(da, h)], "tn", F32, bm=FF // 2)
    dwu = _mm(tag + "_dwu", [(db, h)], "tn", F32, bm=FF // 2)
    tok = put(dwg, dwu, dwd)
    dh = _mm(tag + "_dh", [(da, wg), (db, wu)], "nn", F32)
    dx, dnorm = _rms_bwd(tag + "_dnorm", x, norm + tok, dh, dx_out)
    return dx, dnorm


def _mixer_params(p):
    alog = jnp.pad(p["gdn_a_log"], (4, 120))[None]
    dtb = jnp.pad(p["gdn_dt_bias"], (4, 120))[None]
    bias = jnp.repeat(p["sgu_b"].T, 128, axis=1)
    return dict(
        ln_g=p["sgu_ln_g"][None], ln_b=p["sgu_ln_b"][None], sgu_w=p["sgu_w"], sgu_bias=bias,
        lru_cw=p["lru_conv_w"], lru_cb=p["lru_conv_b"][None], wa=_block_diag(p["lru_wa"]), ba=p["lru_ba"][None],
        wx=_block_diag(p["lru_wx"]), bx=p["lru_bx"][None], lam=p["lru_lambda"][None],
        gdn_cw=p["gdn_conv_w"], alog=alog, dtb=dtb, ng=p["gdn_norm_g"][None],
        pool_w=p["pool_w"], pool_sc=p["pool_scale"][None])


def _mix_forward(tag, x, p, mp):
    h = _rms_fwd(tag + "_norm", x, p["mix_norm"][None])
    proj = _mm(tag + "_proj", [(h, p["w_in"])], "nt", F32, bm=_pick(x.shape[0], (2048, 1024, 512, 256, 128)))
    y_a = _sgu_fwd(tag + "_sgu", proj, mp["ln_g"], mp["ln_b"], mp["sgu_w"], mp["sgu_bias"])
    y_b, hc = _lru_fwd(tag + "_lru", proj, mp["lru_cw"], mp["lru_cb"], mp["wa"], mp["ba"], mp["wx"], mp["bx"],
                       mp["lam"])
    qa = _conv_fwd(tag + "_convq", proj, COL_CQ, mp["gdn_cw"], 0)
    ka = _conv_fwd(tag + "_convk", proj, COL_CK, mp["gdn_cw"], 512)
    va = _conv_fwd(tag + "_convv", proj, COL_CV, mp["gdn_cw"], 1024)
    y_c, sh = _gdn_fwd(tag + "_gdn", qa, ka, va, proj, mp["alog"], mp["dtb"], mp["ng"])
    y_d = _pool_fwd(tag + "_pool", proj, mp["pool_w"], mp["pool_sc"])
    ys = (y_a, y_b, y_c, y_d)
    merged = _merge_fwd(tag + "_merge", ys, p["w_branch"], proj)
    x_out = _mm(tag + "_out", [(merged, p["w_out"])], "nn", F32, res=x)
    return x_out, (x, h, proj, hc, qa, ka, va, sh, ys, merged)


def _mix_backward(tag, dx_out, saved, p, mp, put):
    x, h, proj, hc, qa, ka, va, sh, ys, merged = saved
    T = x.shape[0]
    g = {}
    dmerged = _mm(tag + "_dmerged", [(dx_out, p["w_out"])], "nt", F32)
    g["w_out"] = _mm(tag + "_dwout", [(merged, dx_out)], "tn", F32)
    outs = _merge_bwd(tag + "_dmerge", dmerged, ys, p["w_branch"], proj)
    dgates, dbrs = outs[:NBR], outs[NBR:]
    dys = [_mm(f"{tag}_dy{i}", [(dbrs[i], p["w_branch"][i])], "nn", F32) for i in range(NBR)]
    g["w_branch"] = jnp.stack([_mm(f"{tag}_dwb{i}", [(dbrs[i], ys[i])], "tn", F32) for i in range(NBR)])

    du, dv, dln_g, dln_b, dsgu_w, dbias = _sgu_bwd(tag + "_dsgu", proj, dys[0], mp["ln_g"], mp["ln_b"], mp["sgu_w"],
                                                  mp["sgu_bias"])
    g["sgu_ln_g"], g["sgu_ln_b"], g["sgu_w"] = dln_g[0], dln_b[0], dsgu_w
    g["sgu_b"] = dbias.reshape(128, 4, 128).sum(axis=2).T

    (dbx, dbg, dcw, dcb, dwa, dba, dwx, dbxb, dlam) = _lru_bwd(
        tag + "_dlru", proj, dys[1], hc, mp["lru_cw"], mp["lru_cb"], mp["wa"], mp["ba"], mp["wx"], mp["bx"], mp["lam"])
    g["lru_conv_w"], g["lru_conv_b"], g["lru_ba"], g["lru_bx"], g["lru_lambda"] = dcw, dcb[0], dba[0], dbxb[0], dlam[0]
    g["lru_wa"], g["lru_wx"] = _block_diag_grad(dwa), _block_diag_grad(dwx)

    dqa, dka, dva, dz, dtail, dalog, ddtb, dng = _gdn_bwd(tag + "_dgdn", qa, ka, va, proj, dys[2], sh, mp["alog"],
                                                         mp["dtb"], mp["ng"])
    g["gdn_a_log"], g["gdn_dt_bias"], g["gdn_norm_g"] = dalog[0, 4:8], ddtb[0, 4:8], dng[0]
    dq, dcwq = _conv_bwd(tag + "_dconvq", proj, COL_CQ, dqa, mp["gdn_cw"], 0)
    dk, dcwk = _conv_bwd(tag + "_dconvk", proj, COL_CK, dka, mp["gdn_cw"], 512)
    dv_, dcwv = _conv_bwd(tag + "_dconvv", proj, COL_CV, dva, mp["gdn_cw"], 1024)
    g["gdn_conv_w"] = jnp.concatenate([dcwq, dcwk, dcwv], axis=1)

    dd, dpw, dsc = _pool_bwd(tag + "_dpool", proj, dys[3], mp["pool_w"], mp["pool_sc"])
    g["pool_w"], g["pool_scale"] = dpw, dsc[0]

    dproj = jnp.concatenate([du, dv, dbx, dbg, dq, dk, dv_, dz, dd, *dgates, dtail,
                             jnp.zeros((T, PW - COL_TAIL - 128), BF16)], axis=1)
    dw_in = _mm(tag + "_dwin", [(dproj, h)], "tn", F32)
    tok = put(_w_in_from_layout(dw_in), g.pop("w_branch"), g.pop("w_out"))
    dh = _mm(tag + "_dh", [(dproj, p["w_in"])], "nn", F32, bm=_pick(T, (2048, 1024, 512, 256, 128)))
    dx, dnorm = _rms_bwd(tag + "_dnorm", x, p["mix_norm"][None] + tok, dh, dx_out)
    g["mix_norm"] = dnorm[0]
    return dx, g


_BIG = ("ff1_wg", "ff1_wu", "ff1_wd", "w_in", "w_branch", "w_out", "ff2_wg", "ff2_wu", "ff2_wd")
_COL_SHARDED = ("ff1_wg", "ff1_wu", "w_in", "w_branch", "ff2_wg", "ff2_wu")
_SMALL = ("ff1_norm", "mix_norm", "sgu_ln_g", "sgu_ln_b", "sgu_w", "sgu_b", "lru_conv_w", "lru_conv_b", "lru_wa",
          "lru_ba", "lru_wx", "lru_bx", "lru_lambda", "gdn_conv_w", "gdn_a_log", "gdn_dt_bias", "gdn_norm_g", "pool_w",
          "pool_scale", "ff2_norm", "final_norm")
_WEIGHTS = ("ff1_norm", "ff1_wg", "ff1_wu", "ff1_wd", "mix_norm", "w_in", "sgu_ln_g", "sgu_ln_b", "sgu_w", "sgu_b",
            "lru_conv_w", "lru_conv_b", "lru_wa", "lru_ba", "lru_wx", "lru_bx", "lru_lambda", "gdn_conv_w", "gdn_a_log",
            "gdn_dt_bias", "gdn_norm_g", "pool_w", "pool_scale", "w_branch", "w_out", "ff2_norm", "ff2_wg", "ff2_wu",
            "ff2_wd", "final_norm")
_CONV_SHARDED = ("lru_conv_w", "gdn_conv_w")
PACK_ROW_ALIGN = 16
_GROUPS = (("ff1", ("ff1_wg", "ff1_wu", "ff1_wd")), ("mix", ("w_in", "w_branch", "w_out")),
           ("ff2", ("ff2_wg", "ff2_wu", "ff2_wd")))


def _pad_rows(a, mult):
    pad = (-a.shape[-2]) % mult
    if pad == 0:
        return a
    return jnp.pad(a, [(0, 0)] * (a.ndim - 2) + [(0, pad), (0, 0)])


def _my_index():
    return 4 * lax.axis_index("x") + 2 * lax.axis_index("y") + lax.axis_index("c")


def _landing(own):
    return lax.dynamic_update_index_in_dim(lax.empty((N_DEV,) + own.shape, own.dtype), own, _my_index(), 0)


def _stored(n, a):
    return jnp.swapaxes(a, -1, -2) if n in _COL_SHARDED else a


def _gather_start(w):
    conv = _pad_rows(jnp.concatenate([w[n].reshape(1, -1) for n in _CONV_SHARDED], axis=1), 8)
    packs, metas = [conv], [None]
    for l in range(2):
        for _, names in _GROUPS:
            pieces, meta = [], []
            for n in names:
                a = _stored(n, w[n][l])
                rows = _pad_rows(_rows(a.astype(BF16)), PACK_ROW_ALIGN)
                pieces.append(rows)
                meta.append((n, a.size // 1024, rows.shape[0], a.shape))
            packs.append(jnp.concatenate(pieces, axis=0))
            metas.append(meta)
    handles, token = _exchange_start("gather_start", packs, [_landing(p) for p in packs], scatter=False)
    return handles, metas, token


def _gather_finish(l, sub, handles, metas, after):
    gi = 1 + 3 * l + sub
    got = _exchange_wait(f"gather_wait{gi}", *handles[gi], after, scatter=False)
    out, r = {}, 0
    for n, size, padded, shape in metas[gi]:
        out[n] = _parts_to_rows(got[:, r:r + size].reshape((N_DEV,) + shape))
        r += padded
    if "w_in" in out:
        out["w_in"] = _w_in_to_layout(out["w_in"])
    return out


def _scatter_start(l, sub, grads):
    pieces, meta = [], []
    for n in _GROUPS[sub][1]:
        parts = _rows_to_parts(grads[n].astype(BF16))
        shape = parts.shape[1:]
        parts = parts.reshape(N_DEV, -1, 1024)
        padded = _pad_rows(parts, PACK_ROW_ALIGN)
        pieces.append(padded)
        meta.append((n, parts.shape[1], padded.shape[1], shape))
    pack = jnp.concatenate(pieces, axis=1)
    own = lax.dynamic_index_in_dim(pack, _my_index(), 0, keepdims=False)
    handles, token = _exchange_start(f"scatter_start{3 * l + sub}", [pack], [_landing(own)], scatter=True)
    return handles[0], meta, token


def _scatter_finish(l, sub, handle, meta, after):
    gi = 3 * l + sub
    landed = _exchange_wait(f"scatter_wait{gi}", *handle, after, scatter=True)
    summed = _sum8(f"sum_grads{gi}", landed)
    out, r = {}, 0
    for n, size, padded, shape in meta:
        out[n] = _stored(n, summed[r:r + size].reshape(shape))
        r += padded
    return out


def _gather_conv_finish(w, handles, after):
    gconv = _exchange_wait("gather_wait0", *handles[0], after, scatter=False)[:, 0]
    full, r = {}, 0
    for n in _CONV_SHARDED:
        sz = w[n].size
        full[n] = _parts_to_cols(gconv[:, r:r + sz].reshape((N_DEV,) + w[n].shape))
        r += sz
    return full


def _forward_backward(x, tgt, w, conv, get_weights, put_grads, token):
    saved, params = [], []
    for l in range(2):
        p = {n: w[n][l] for n in _SMALL if n != "final_norm"}
        for n in _CONV_SHARDED:
            p[n] = conv[n][l]
        mp = _mixer_params(p)
        tok = token[:1, :1] if l == 0 else 0.0
        p.update(get_weights(l, 0, x))
        x, s1 = _ffn_forward(f"l{l}_ff1", x, p["ff1_norm"][None] + tok, p["ff1_wg"], p["ff1_wu"], p["ff1_wd"])
        p.update(get_weights(l, 1, x))
        x, s2 = _mix_forward(f"l{l}_mix", x, p, mp)
        p.update(get_weights(l, 2, x))
        x, s3 = _ffn_forward(f"l{l}_ff2", x, p["ff2_norm"][None], p["ff2_wg"], p["ff2_wu"], p["ff2_wd"])
        saved.append((s1, s2, s3))
        params.append((p, mp))
    loss, dx, dfinal = _final_loss("loss_head", x, w["final_norm"][None], tgt)
    grads = [None, None]
    for l in (1, 0):
        p, mp = params[l]
        s1, s2, s3 = saved[l]
        g = {}

        def put(sub):
            names = _GROUPS[sub][1]
            return lambda *gs, l=l: put_grads(l, sub, dict(zip(names, gs)))[:1, :1]

        dx, dn = _ffn_backward(f"l{l}_ff2", dx, s3, p["ff2_norm"][None], p["ff2_wg"], p["ff2_wu"], p["ff2_wd"], put(2))
        g["ff2_norm"] = dn[0]
        dx, gm = _mix_backward(f"l{l}_mix", dx, s2, p, mp, put(1))
        g.update(gm)
        dx, dn = _ffn_backward(f"l{l}_ff1", dx, s1, p["ff1_norm"][None], p["ff1_wg"], p["ff1_wu"], p["ff1_wd"], put(0))
        g["ff1_norm"] = dn[0]
        grads[l] = g
    small_g = {n: _join([grads[0][n].reshape(-1), grads[1][n].reshape(-1)]).reshape((2,) + grads[0][n].shape)
               for n in grads[0]}
    small_g["final_norm"] = dfinal[0]
    return loss, dx, small_g


SMALL_PIECE = 8 * 1024


def _pack_small(d, names):
    pieces = []
    for n in names:
        flat = d[n].reshape(-1)
        pieces.append(jnp.pad(flat, (0, (-flat.size) % SMALL_PIECE)).reshape(-1, 1024))
    return jnp.concatenate(pieces, axis=0)


def _unpack_small(pack, shapes, names):
    out, r = {}, 0
    for n in names:
        size = 1
        for s in shapes[n]:
            size *= s
        rows = -(-size // SMALL_PIECE) * 8
        out[n] = pack[r:r + rows].reshape(-1)[:size].reshape(shapes[n])
        r += rows
    return out


def _reduce_small(full_g, loss, w):
    d = dict(full_g)
    d["loss"] = loss[0, :1]
    names = _SMALL + ("loss",)
    summed = _sum8("sum_small", _all_gather("gather_small", _pack_small(d, names)))
    out = _unpack_small(summed, {n: d[n].shape for n in names}, names)
    return out, out["loss"][0]


def _as2d(a):
    if a.ndim == 1:
        return a.reshape(1, -1)
    return a.reshape(-1, a.shape[-1])


def kernel(x, ff1_norm, ff1_wg, ff1_wu, ff1_wd, mix_norm, w_in, sgu_ln_g, sgu_ln_b, sgu_w, sgu_b, lru_conv_w, lru_conv_b, lru_wa, lru_ba, lru_wx, lru_bx, lru_lambda, gdn_conv_w, gdn_a_log, gdn_dt_bias, gdn_norm_g, pool_w, pool_scale, w_branch, w_out, ff2_norm, ff2_wg, ff2_wu, ff2_wd, final_norm, loss_target, m_ff1_norm, m_ff1_wg, m_ff1_wu, m_ff1_wd, m_mix_norm, m_w_in, m_sgu_ln_g, m_sgu_ln_b, m_sgu_w, m_sgu_b, m_lru_conv_w, m_lru_conv_b, m_lru_wa, m_lru_ba, m_lru_wx, m_lru_bx, m_lru_lambda, m_gdn_conv_w, m_gdn_a_log, m_gdn_dt_bias, m_gdn_norm_g, m_pool_w, m_pool_scale, m_w_branch, m_w_out, m_ff2_norm, m_ff2_wg, m_ff2_wu, m_ff2_wd, m_final_norm, v_ff1_norm, v_ff1_wg, v_ff1_wu, v_ff1_wd, v_mix_norm, v_w_in, v_sgu_ln_g, v_sgu_ln_b, v_sgu_w, v_sgu_b, v_lru_conv_w, v_lru_conv_b, v_lru_wa, v_lru_ba, v_lru_wx, v_lru_bx, v_lru_lambda, v_gdn_conv_w, v_gdn_a_log, v_gdn_dt_bias, v_gdn_norm_g, v_pool_w, v_pool_scale, v_w_branch, v_w_out, v_ff2_norm, v_ff2_wg, v_ff2_wu, v_ff2_wd, v_final_norm):
    w = dict(ff1_norm=ff1_norm, ff1_wg=ff1_wg, ff1_wu=ff1_wu, ff1_wd=ff1_wd, mix_norm=mix_norm, w_in=w_in,
             sgu_ln_g=sgu_ln_g, sgu_ln_b=sgu_ln_b, sgu_w=sgu_w, sgu_b=sgu_b, lru_conv_w=lru_conv_w,
             lru_conv_b=lru_conv_b, lru_wa=lru_wa, lru_ba=lru_ba, lru_wx=lru_wx, lru_bx=lru_bx, lru_lambda=lru_lambda,
             gdn_conv_w=gdn_conv_w, gdn_a_log=gdn_a_log, gdn_dt_bias=gdn_dt_bias, gdn_norm_g=gdn_norm_g, pool_w=pool_w,
             pool_scale=pool_scale, w_branch=w_branch, w_out=w_out, ff2_norm=ff2_norm, ff2_wg=ff2_wg, ff2_wu=ff2_wu,
             ff2_wd=ff2_wd, final_norm=final_norm)
    m = dict(ff1_norm=m_ff1_norm, ff1_wg=m_ff1_wg, ff1_wu=m_ff1_wu, ff1_wd=m_ff1_wd, mix_norm=m_mix_norm, w_in=m_w_in,
             sgu_ln_g=m_sgu_ln_g, sgu_ln_b=m_sgu_ln_b, sgu_w=m_sgu_w, sgu_b=m_sgu_b, lru_conv_w=m_lru_conv_w,
             lru_conv_b=m_lru_conv_b, lru_wa=m_lru_wa, lru_ba=m_lru_ba, lru_wx=m_lru_wx, lru_bx=m_lru_bx,
             lru_lambda=m_lru_lambda, gdn_conv_w=m_gdn_conv_w, gdn_a_log=m_gdn_a_log, gdn_dt_bias=m_gdn_dt_bias,
             gdn_norm_g=m_gdn_norm_g, pool_w=m_pool_w, pool_scale=m_pool_scale, w_branch=m_w_branch, w_out=m_w_out,
             ff2_norm=m_ff2_norm, ff2_wg=m_ff2_wg, ff2_wu=m_ff2_wu, ff2_wd=m_ff2_wd, final_norm=m_final_norm)
    v = dict(ff1_norm=v_ff1_norm, ff1_wg=v_ff1_wg, ff1_wu=v_ff1_wu, ff1_wd=v_ff1_wd, mix_norm=v_mix_norm, w_in=v_w_in,
             sgu_ln_g=v_sgu_ln_g, sgu_ln_b=v_sgu_ln_b, sgu_w=v_sgu_w, sgu_b=v_sgu_b, lru_conv_w=v_lru_conv_w,
             lru_conv_b=v_lru_conv_b, lru_wa=v_lru_wa, lru_ba=v_lru_ba, lru_wx=v_lru_wx, lru_bx=v_lru_bx,
             lru_lambda=v_lru_lambda, gdn_conv_w=v_gdn_conv_w, gdn_a_log=v_gdn_a_log, gdn_dt_bias=v_gdn_dt_bias,
             gdn_norm_g=v_gdn_norm_g, pool_w=v_pool_w, pool_scale=v_pool_scale, w_branch=v_w_branch, w_out=v_w_out,
             ff2_norm=v_ff2_norm, ff2_wg=v_ff2_wg, ff2_wu=v_ff2_wu, ff2_wd=v_ff2_wd, final_norm=v_final_norm)

    handles, metas, token = _gather_start(w)
    conv = _gather_conv_finish(w, handles, token)
    pending = {}

    def get_weights(l, sub, after):
        return _gather_finish(l, sub, handles, metas, after)

    def put_grads(l, sub, grads):
        handle, meta, tok = _scatter_start(l, sub, grads)
        pending[(l, sub)] = (handle, meta)
        return tok

    T = x.shape[1]
    loss_share, dx, small_g = _forward_backward(x.reshape(T, D), loss_target.reshape(T, D), w, conv, get_weights,
                                                put_grads, token)
    per = {key: _scatter_finish(*key, *pending[key], dx) for key in pending}
    grad = {n: jnp.stack([per[(0, sub)][n], per[(1, sub)][n]]) for sub, (_, names) in enumerate(_GROUPS) for n in names}
    small, loss = _reduce_small(small_g, loss_share, w)
    me = _my_index()
    for n in _SMALL:
        if n in _CONV_SHARDED:
            width = w[n].shape[-1]
            grad[n] = lax.dynamic_slice_in_dim(small[n], me * width, width, axis=2)
        else:
            grad[n] = small[n]

    delta, new_m, new_v = {}, {}, {}
    for n in _BIG:
        d_, m_, v_ = _adamw("adamw_" + n, _as2d(w[n]), _as2d(grad[n]), _as2d(m[n]), _as2d(v[n]))
        delta[n], new_m[n], new_v[n] = (t.reshape(w[n].shape) for t in (d_, m_, v_))

    outs = _adamw("adamw_small", *[_pack_small(t, _SMALL) for t in (w, grad, m, v)])
    for dst, packed in zip((delta, new_m, new_v), outs):
        dst.update(_unpack_small(packed, {n: w[n].shape for n in _SMALL}, _SMALL))

    return (loss, dx.reshape(x.shape), *[grad[n] for n in _WEIGHTS], *[delta[n] for n in _WEIGHTS],
            *[new_m[n] for n in _WEIGHTS], *[new_v[n] for n in _WEIGHTS])
```

```python
import functools

import jax
import jax.numpy as jnp
from jax import lax
from jax.experimental import pallas as pl
from jax.experimental.pallas import tpu as pltpu

F32 = jnp.float32
BF16 = jnp.bfloat16
HI = lax.Precision.HIGHEST

N_DEV = 8
D = 1024
FF = 2816
BW = 512
NBR = 4
CHUNK = 64
EPS = 1e-6
LRU_C = 8.0
GDN_DK = 128

COL_AU, COL_AV, COL_BX, COL_BG = 0, 512, 1024, 1536
COL_CQ, COL_CK, COL_CV, COL_CZ = 2048, 2560, 3072, 3584
COL_DX, COL_GATE, COL_TAIL = 4096, 4608, 8704
PW = 9216
P_IN = 8712

ADAM_LR, ADAM_B1, ADAM_B2, ADAM_EPS, ADAM_WD, ADAM_STEP = 0.001, 0.9, 0.999, 1e-08, 0.01, 10

VMEM_LIMIT_V7X = 56 * 1024 * 1024

_NN = (((1,), (0,)), ((), ()))
_NT = (((1,), (1,)), ((), ()))
_TN = (((0,), (0,)), ((), ()))


def _cp(*sem):
    return pltpu.CompilerParams(dimension_semantics=tuple(sem), vmem_limit_bytes=VMEM_LIMIT_V7X)


def _dot(a, b, dims=_NN):
    return lax.dot_general(a.astype(BF16), b.astype(BF16), dims, preferred_element_type=F32)


def _dot_hi(a, b, dims=_NN):
    return lax.dot_general(a, b, dims, precision=HI, preferred_element_type=F32)


def _pick(n, cands):
    for c in cands:
        if n % c == 0:
            return c
    return n


@jax.custom_jvp
def _log1p(x):
    u = 1.0 + x
    return jnp.where(u == 1.0, x, x * jnp.log(u) / jnp.where(u == 1.0, 1.0, u - 1.0))


@_log1p.defjvp
def _log1p_jvp(p, t):
    (x,), (dx,) = p, t
    return _log1p(x), dx / (1.0 + x)


@jax.custom_jvp
def _expm1(x):
    u = jnp.exp(x)
    lu = jnp.log(u)
    small = (u == 1.0) | (lu == 0.0)
    return jnp.where(small, x, (u - 1.0) * x / jnp.where(small, 1.0, lu))


@_expm1.defjvp
def _expm1_jvp(p, t):
    (x,), (dx,) = p, t
    return _expm1(x), dx * jnp.exp(x)


def _softplus(x):
    return jnp.maximum(x, 0.0) + _log1p(jnp.exp(-jnp.abs(x)))


def _sigmoid(x):
    return jax.nn.sigmoid(x)


def _silu(x):
    return x * jax.nn.sigmoid(x)


def _gelu(x):
    return jax.nn.gelu(x)


@functools.partial(jax.custom_vjp, nondiff_argnums=(1,))
def _shift(x, s):
    return x if s == 0 else pltpu.roll(x, s, 0)


def _shift_fwd(x, s):
    return _shift(x, s), None


def _shift_bwd(s, _, g):
    n = g.shape[0]
    return (g if s == 0 else pltpu.roll(g, n - s, 0),)


_shift.defvjp(_shift_fwd, _shift_bwd)


def _scan_steps(a, b, reverse):
    n = a.shape[0]
    row = lax.broadcasted_iota(jnp.int32, a.shape, 0)
    k = 1
    while k < n:
        sh = n - k if reverse else k
        m = (row < n - k) if reverse else (row >= k)
        a_s = jnp.where(m, pltpu.roll(a, sh, 0), 1.0)
        b_s = jnp.where(m, pltpu.roll(b, sh, 0), 0.0)
        b = a * b_s + b
        a = a * a_s
        k *= 2
    return b


@jax.custom_vjp
def _scan(a, b):
    return _scan_steps(a, b, False)


def _scan_fwd(a, b):
    h = _scan_steps(a, b, False)
    return h, (a, h)


def _scan_bwd(res, dh):
    a, h = res
    n = a.shape[0]
    row = lax.broadcasted_iota(jnp.int32, a.shape, 0)
    a_next = jnp.where(row < n - 1, pltpu.roll(a, n - 1, 0), 0.0)
    g = _scan_steps(a_next, dh, True)
    h_prev = jnp.where(row >= 1, pltpu.roll(h, 1, 0), 0.0)
    return g * h_prev, g


_scan.defvjp(_scan_fwd, _scan_bwd)


def _mm(name, pairs, mode, out_dtype, *, res=None, scale=1.0, bm=None, bn=None, bk=None):
    a0, b0 = pairs[0]
    if mode == "nn":
        (M, K), N = a0.shape, b0.shape[1]
    elif mode == "nt":
        (M, K), N = a0.shape, b0.shape[0]
    else:
        (K, M), N = a0.shape, b0.shape[1]
    bm = bm or _pick(M, (1024, 512, 256, 128))
    bn = bn or _pick(N, (1024, 512, 256, 128))
    bk = bk or _pick(K, (1024, 512, 1408, 256, 128))
    nk = K // bk
    npair = len(pairs)
    dims = {"nn": _NN, "nt": _NT, "tn": _TN}[mode]

    def body(*refs):
        ab = refs[:2 * npair]
        pos = 2 * npair
        r_ref = None
        if res is not None:
            r_ref = refs[pos]
            pos += 1
        o_ref = refs[pos]
        part = None
        for p in range(npair):
            d = _dot(ab[2 * p][...], ab[2 * p + 1][...], dims)
            part = d if part is None else part + d

        def finish(acc):
            out = acc if scale == 1.0 else acc * scale
            if r_ref is not None:
                out = out + r_ref[...]
            o_ref[...] = out.astype(out_dtype)

        if nk == 1:
            finish(part)
        else:
            acc_ref = refs[pos + 1]
            k = pl.program_id(2)

            @pl.when(k == 0)
            def _():
                acc_ref[...] = part

            @pl.when(k > 0)
            def _():
                acc_ref[...] += part

            @pl.when(k == nk - 1)
            def _():
                finish(acc_ref[...])

    if mode == "nn":
        a_spec = pl.BlockSpec((bm, bk), lambda i, j, k: (i, k))
        b_spec = pl.BlockSpec((bk, bn), lambda i, j, k: (k, j))
    elif mode == "nt":
        a_spec = pl.BlockSpec((bm, bk), lambda i, j, k: (i, k))
        b_spec = pl.BlockSpec((bn, bk), lambda i, j, k: (j, k))
    else:
        a_spec = pl.BlockSpec((bk, bm), lambda i, j, k: (k, i))
        b_spec = pl.BlockSpec((bk, bn), lambda i, j, k: (k, j))
    o_spec = pl.BlockSpec((bm, bn), lambda i, j, k: (i, j))
    in_specs, args = [], []
    for a, b in pairs:
        in_specs += [a_spec, b_spec]
        args += [a, b]
    if res is not None:
        in_specs.append(o_spec)
        args.append(res)
    return pl.pallas_call(
        body, name=name, grid=(M // bm, N // bn, nk),
        in_specs=in_specs, out_specs=o_spec,
        out_shape=jax.ShapeDtypeStruct((M, N), out_dtype),
        scratch_shapes=[pltpu.VMEM((bm, bn), F32)] if nk > 1 else [],
        compiler_params=_cp("parallel", "parallel", "arbitrary"),
    )(*args)


def _rms_fwd(name, x, g):
    T = x.shape[0]
    bm = _pick(T, (512, 256, 128))

    def body(x_ref, g_ref, o_ref):
        xv = x_ref[...]
        r = lax.rsqrt(jnp.mean(xv * xv, axis=-1, keepdims=True) + EPS)
        o_ref[...] = (xv * r * g_ref[...]).astype(BF16)

    return pl.pallas_call(
        body, name=name, grid=(T // bm,),
        in_specs=[pl.BlockSpec((bm, D), lambda i: (i, 0)), pl.BlockSpec((1, D), lambda i: (0, 0))],
        out_specs=pl.BlockSpec((bm, D), lambda i: (i, 0)),
        out_shape=jax.ShapeDtypeStruct((T, D), BF16),
        compiler_params=_cp("parallel"),
    )(x, g)


def _rms_bwd(name, x, g, dh, dres):
    T = x.shape[0]
    bm = _pick(T, (512, 256, 128))

    def body(x_ref, g_ref, dh_ref, dres_ref, dx_ref, dg_ref):
        xv = x_ref[...]
        r = lax.rsqrt(jnp.mean(xv * xv, axis=-1, keepdims=True) + EPS)
        xh = xv * r
        dhv = dh_ref[...]
        dxh = dhv * g_ref[...]
        dx_ref[...] = dres_ref[...] + r * (dxh - xh * jnp.mean(dxh * xh, axis=-1, keepdims=True))
        part = jnp.sum(dhv * xh, axis=0, keepdims=True)

        @pl.when(pl.program_id(0) == 0)
        def _():
            dg_ref[...] = part

        @pl.when(pl.program_id(0) > 0)
        def _():
            dg_ref[...] += part

    row = pl.BlockSpec((bm, D), lambda i: (i, 0))
    vec = pl.BlockSpec((1, D), lambda i: (0, 0))
    return pl.pallas_call(
        body, name=name, grid=(T // bm,),
        in_specs=[row, vec, row, row], out_specs=[row, vec],
        out_shape=[jax.ShapeDtypeStruct((T, D), F32), jax.ShapeDtypeStruct((1, D), F32)],
        compiler_params=_cp("arbitrary"),
    )(x, g, dh, dres)


def _final_loss(name, x, g, tgt):
    T = x.shape[0]
    bm = _pick(T, (512, 256, 128))

    def body(x_ref, g_ref, t_ref, loss_ref, dx_ref, dg_ref):
        xv = x_ref[...]
        gv = g_ref[...]
        r = lax.rsqrt(jnp.mean(xv * xv, axis=-1, keepdims=True) + EPS)
        xh = xv * r
        e = xh * gv - t_ref[...]
        lpart = jnp.broadcast_to(0.5 * jnp.sum(jnp.mean(e * e, axis=-1, keepdims=True), axis=0, keepdims=True), (1, 128))
        dy = e * (1.0 / D)
        dxh = dy * gv
        dx_ref[...] = r * (dxh - xh * jnp.mean(dxh * xh, axis=-1, keepdims=True))
        gpart = jnp.sum(dy * xh, axis=0, keepdims=True)

        @pl.when(pl.program_id(0) == 0)
        def _():
            loss_ref[...] = lpart
            dg_ref[...] = gpart

        @pl.when(pl.program_id(0) > 0)
        def _():
            loss_ref[...] += lpart
            dg_ref[...] += gpart

    row = pl.BlockSpec((bm, D), lambda i: (i, 0))
    vec = pl.BlockSpec((1, D), lambda i: (0, 0))
    return pl.pallas_call(
        body, name=name, grid=(T // bm,),
        in_specs=[row, vec, row],
        out_specs=[pl.BlockSpec((1, 128), lambda i: (0, 0)), row, vec],
        out_shape=[jax.ShapeDtypeStruct((1, 128), F32), jax.ShapeDtypeStruct((T, D), F32),
                   jax.ShapeDtypeStruct((1, D), F32)],
        compiler_params=_cp("arbitrary"),
    )(x, g, tgt)


def _ffn_up(name, h, wg, wu):
    T = h.shape[0]
    bm = _pick(T, (2048, 1024, 512, 256, 128))
    bn = 256

    def body(h_ref, wg_ref, wu_ref, a_ref, b_ref, act_ref):
        hv = h_ref[...]
        a = _dot(hv, wg_ref[...], _NT)
        b = _dot(hv, wu_ref[...], _NT)
        a_ref[...] = a.astype(BF16)
        b_ref[...] = b.astype(BF16)
        act_ref[...] = (_silu(a) * b).astype(BF16)

    w_spec = pl.BlockSpec((bn, D), lambda i, j: (j, 0))
    o_spec = pl.BlockSpec((bm, bn), lambda i, j: (i, j))
    return pl.pallas_call(
        body, name=name, grid=(T // bm, FF // bn),
        in_specs=[pl.BlockSpec((bm, D), lambda i, j: (i, 0)), w_spec, w_spec],
        out_specs=[o_spec, o_spec, o_spec],
        out_shape=[jax.ShapeDtypeStruct((T, FF), BF16)] * 3,
        compiler_params=_cp("parallel", "parallel"),
    )(h, wg, wu)


def _ffn_dact(name, dy, wd, a, b):
    T = dy.shape[0]
    bm = _pick(T, (2048, 1024, 512, 256, 128))
    bn = 256

    def body(dy_ref, wd_ref, a_ref, b_ref, da_ref, db_ref):
        dact = 0.5 * _dot(dy_ref[...], wd_ref[...], _NT)
        av = a_ref[...].astype(F32)
        s = _sigmoid(av)
        da_ref[...] = (dact * b_ref[...].astype(F32) * (s * (1.0 + av * (1.0 - s)))).astype(BF16)
        db_ref[...] = (dact * (av * s)).astype(BF16)

    t_spec = pl.BlockSpec((bm, bn), lambda i, j: (i, j))
    return pl.pallas_call(
        body, name=name, grid=(T // bm, FF // bn),
        in_specs=[pl.BlockSpec((bm, D), lambda i, j: (i, 0)), pl.BlockSpec((bn, D), lambda i, j: (j, 0)),
                  t_spec, t_spec],
        out_specs=[t_spec, t_spec],
        out_shape=[jax.ShapeDtypeStruct((T, FF), BF16), jax.ShapeDtypeStruct((T, FF), BF16)],
        compiler_params=_cp("parallel", "parallel"),
    )(dy, wd, a, b)


def _merge_specs(T, bm, bn):
    y_spec = pl.BlockSpec((bm, BW), lambda i, j: (i, 0))
    wb_spec = pl.BlockSpec((NBR, bn, BW), lambda i, j: (0, j, 0))
    gate_specs = [pl.BlockSpec((bm, bn), functools.partial(lambda i, j, o: (i, o + j), o=(COL_GATE + g * D) // bn))
                  for g in range(NBR)]
    t_spec = pl.BlockSpec((bm, bn), lambda i, j: (i, j))
    return y_spec, wb_spec, gate_specs, t_spec


def _merge_fwd(name, ys, wb, proj):
    T = proj.shape[0]
    bm = _pick(T, (512, 256, 128))
    bn = 512
    y_spec, wb_spec, gate_specs, t_spec = _merge_specs(T, bm, bn)

    def body(y0, y1, y2, y3, wb_ref, g0, g1, g2, g3, o_ref):
        acc = None
        for g, (y_ref, g_ref) in enumerate(((y0, g0), (y1, g1), (y2, g2), (y3, g3))):
            t = _sigmoid(g_ref[...]) * _dot(y_ref[...], wb_ref[g], _NT)
            acc = t if acc is None else acc + t
        o_ref[...] = acc.astype(BF16)

    return pl.pallas_call(
        body, name=name, grid=(T // bm, D // bn),
        in_specs=[y_spec] * NBR + [wb_spec] + gate_specs, out_specs=t_spec,
        out_shape=jax.ShapeDtypeStruct((T, D), BF16),
        compiler_params=_cp("parallel", "parallel"),
    )(*ys, wb, proj, proj, proj, proj)


def _merge_bwd(name, dm, ys, wb, proj):
    T = proj.shape[0]
    bm = _pick(T, (512, 256, 128))
    bn = 512
    y_spec, wb_spec, gate_specs, t_spec = _merge_specs(T, bm, bn)

    def body(dm_ref, y0, y1, y2, y3, wb_ref, g0, g1, g2, g3, *outs):
        dmv = dm_ref[...]
        for g, (y_ref, g_ref) in enumerate(((y0, g0), (y1, g1), (y2, g2), (y3, g3))):
            br = _dot(y_ref[...], wb_ref[g], _NT)
            s = _sigmoid(g_ref[...])
            outs[g][...] = (dmv * br * (s * (1.0 - s))).astype(BF16)
            outs[NBR + g][...] = (dmv * s).astype(BF16)

    return pl.pallas_call(
        body, name=name, grid=(T // bm, D // bn),
        in_specs=[t_spec] + [y_spec] * NBR + [wb_spec] + gate_specs, out_specs=[t_spec] * (2 * NBR),
        out_shape=[jax.ShapeDtypeStruct((T, D), BF16)] * (2 * NBR),
        compiler_params=_cp("parallel", "parallel"),
    )(dm, *ys, wb, proj, proj, proj, proj)


def _sgu_block(u_pre, v_pre, ln_g, ln_b, w, bias):
    u = _gelu(u_pre)
    vf = _gelu(v_pre)
    mu = jnp.mean(vf, axis=-1, keepdims=True)
    var = jnp.mean(jnp.square(vf - mu), axis=-1, keepdims=True)
    vn = (vf - mu) * lax.rsqrt(var + EPS) * ln_g + ln_b
    ri = lax.broadcasted_iota(jnp.int32, (128, 128), 0)
    ci = lax.broadcasted_iota(jnp.int32, (128, 128), 1)
    mask = (ri // CHUNK) >= (ci // CHUNK)
    outs = [_dot(jnp.where(mask, w[g], 0.0), vn[:, g * 128:(g + 1) * 128]) for g in range(4)]
    mixed = jnp.concatenate(outs, axis=1) + bias
    return u * mixed


def _sgu_param_specs():
    return [pl.BlockSpec((1, BW), lambda i: (0, 0)), pl.BlockSpec((1, BW), lambda i: (0, 0)),
            pl.BlockSpec((4, 128, 128), lambda i: (0, 0, 0)), pl.BlockSpec((128, BW), lambda i: (0, 0))]


def _sgu_fwd(name, proj, ln_g, ln_b, w, bias):
    T = proj.shape[0]
    rb = _pick(T, (256, 128))

    def body(u_ref, v_ref, g_ref, b_ref, w_ref, bias_ref, y_ref):
        for n in range(rb // 128):
            rows = slice(n * 128, (n + 1) * 128)
            y = _sgu_block(u_ref[rows, :], v_ref[rows, :], g_ref[...], b_ref[...], w_ref[...], bias_ref[...])
            y_ref[rows, :] = y.astype(BF16)

    return pl.pallas_call(
        body, name=name, grid=(T // rb,),
        in_specs=[pl.BlockSpec((rb, BW), lambda i: (i, COL_AU // BW)), pl.BlockSpec((rb, BW), lambda i: (i, COL_AV // BW))]
        + _sgu_param_specs(),
        out_specs=pl.BlockSpec((rb, BW), lambda i: (i, 0)),
        out_shape=jax.ShapeDtypeStruct((T, BW), BF16),
        compiler_params=_cp("parallel"),
    )(proj, proj, ln_g, ln_b, w, bias)


def _sgu_bwd(name, proj, dy, ln_g, ln_b, w, bias):
    T = proj.shape[0]
    rb = _pick(T, (256, 128))

    def body(u_ref, v_ref, dy_ref, g_ref, b_ref, w_ref, bias_ref, du_ref, dv_ref, dg_ref, db_ref, dw_ref, dbias_ref):
        acc = None
        for n in range(rb // 128):
            rows = slice(n * 128, (n + 1) * 128)
            _, vjp = jax.vjp(_sgu_block, u_ref[rows, :], v_ref[rows, :], g_ref[...], b_ref[...], w_ref[...],
                             bias_ref[...])
            du, dv, *dp = vjp(dy_ref[rows, :])
            du_ref[rows, :] = du.astype(BF16)
            dv_ref[rows, :] = dv.astype(BF16)
            acc = dp if acc is None else [p + q for p, q in zip(acc, dp)]

        @pl.when(pl.program_id(0) == 0)
        def _():
            for r, p in zip((dg_ref, db_ref, dw_ref, dbias_ref), acc):
                r[...] = p

        @pl.when(pl.program_id(0) > 0)
        def _():
            for r, p in zip((dg_ref, db_ref, dw_ref, dbias_ref), acc):
                r[...] += p

    row = pl.BlockSpec((rb, BW), lambda i: (i, 0))
    return pl.pallas_call(
        body, name=name, grid=(T // rb,),
        in_specs=[pl.BlockSpec((rb, BW), lambda i: (i, COL_AU // BW)), pl.BlockSpec((rb, BW), lambda i: (i, COL_AV // BW)),
                  row] + _sgu_param_specs(),
        out_specs=[row, row] + _sgu_param_specs(),
        out_shape=[jax.ShapeDtypeStruct((T, BW), BF16), jax.ShapeDtypeStruct((T, BW), BF16),
                   jax.ShapeDtypeStruct((1, BW), F32), jax.ShapeDtypeStruct((1, BW), F32),
                   jax.ShapeDtypeStruct((4, 128, 128), F32), jax.ShapeDtypeStruct((128, BW), F32)],
        compiler_params=_cp("arbitrary"),
    )(proj, proj, dy, ln_g, ln_b, w, bias)


def _halo_block(ref, i, rblk, halo):
    r0 = pl.multiple_of(i * rblk, rblk)
    h0 = pl.multiple_of(jnp.maximum(r0 - halo, 0), halo)
    top = jnp.where(i > 0, ref[pl.ds(h0, halo), :], 0.0)
    return jnp.concatenate([top, ref[pl.ds(r0, rblk), :]], axis=0)


def _with_halo_grad(dfull, pending, halo, rblk):
    tail = jnp.concatenate([jnp.zeros((rblk - halo, 128), F32), pending], axis=0)
    return dfull[halo:] + tail


def _conv4(xfull, rows):
    acc = None
    for k in range(4):
        t = rows[k] * _shift(xfull, 3 - k)[8:]
        acc = t if acc is None else acc + t
    return acc


def _lru_block(xfull, gate, h0, c0, c1, c2, c3, cb, wa, ba, wx, bx, lam):
    n = gate.shape[0]
    xc = _conv4(xfull, (c0, c1, c2, c3)) + cb
    r = _sigmoid(_dot(xc, wa) + ba)
    ig = _sigmoid(_dot(xc, wx) + bx)
    log_a = -LRU_C * r * _softplus(-lam)
    a = jnp.exp(log_a)
    mult = jnp.sqrt(-_expm1(2.0 * log_a))
    b = mult * (ig * xc)
    row = lax.broadcasted_iota(jnp.int32, (n, 128), 0)
    b = b + jnp.where(row == 0, a * h0, 0.0)
    h = _scan(a, b)
    out = h * _gelu(gate)
    h_last = jnp.sum(jnp.where(row == n - 1, h, 0.0), axis=0, keepdims=True)
    return out, h_last


def _lru_param_specs():
    vec = pl.BlockSpec((1, 128), lambda g: (0, g))
    mat = pl.BlockSpec((None, 128, 128), lambda g: (g, 0, 0))
    return [pl.BlockSpec((4, 128), lambda g: (0, g)), vec, mat, vec, mat, vec, vec]


def _lru_load_params(cw_ref, cb_ref, wa_ref, ba_ref, wx_ref, bx_ref, lam_ref):
    return (cw_ref[0:1, :], cw_ref[1:2, :], cw_ref[2:3, :], cw_ref[3:4, :], cb_ref[...], wa_ref[...], ba_ref[...],
            wx_ref[...], bx_ref[...], lam_ref[...])


def _lru_fwd(name, proj, cw, cb, wa, ba, wx, bx, lam):
    T = proj.shape[0]
    rblk = _pick(T, (256, 128))
    nblk = T // rblk

    def body(x_ref, gt_ref, cw_ref, cb_ref, wa_ref, ba_ref, wx_ref, bx_ref, lam_ref, y_ref, hc_ref):
        params = _lru_load_params(cw_ref, cb_ref, wa_ref, ba_ref, wx_ref, bx_ref, lam_ref)

        def step(i, h0):
            r0 = pl.multiple_of(i * rblk, rblk)
            out, h_last = _lru_block(_halo_block(x_ref, i, rblk, 8), gt_ref[pl.ds(r0, rblk), :], h0, *params)
            y_ref[pl.ds(r0, rblk), :] = out.astype(BF16)
            hc_ref[pl.ds(pl.multiple_of(i * 8, 8), 8), :] = jnp.broadcast_to(h0, (8, 128))
            return h_last

        lax.fori_loop(0, nblk, step, jnp.zeros((1, 128), F32))

    return pl.pallas_call(
        body, name=name, grid=(4,),
        in_specs=[pl.BlockSpec((T, 128), lambda g: (0, COL_BX // 128 + g)),
                  pl.BlockSpec((T, 128), lambda g: (0, COL_BG // 128 + g))] + _lru_param_specs(),
        out_specs=[pl.BlockSpec((T, 128), lambda g: (0, g)), pl.BlockSpec((nblk * 8, 128), lambda g: (0, g))],
        out_shape=[jax.ShapeDtypeStruct((T, BW), BF16), jax.ShapeDtypeStruct((nblk * 8, BW), F32)],
        compiler_params=_cp("parallel"),
    )(proj, proj, cw, cb, wa, ba, wx, bx, lam)


def _lru_bwd(name, proj, dy, hc, cw, cb, wa, ba, wx, bx, lam):
    T = proj.shape[0]
    rblk = _pick(T, (256, 128))
    nblk = T // rblk

    def body(x_ref, gt_ref, dy_ref, hc_ref, cw_ref, cb_ref, wa_ref, ba_ref, wx_ref, bx_ref, lam_ref,
             dx_ref, dgt_ref, dcw_ref, dcb_ref, dwa_ref, dba_ref, dwx_ref, dbx_ref, dlam_ref):
        params = _lru_load_params(cw_ref, cb_ref, wa_ref, ba_ref, wx_ref, bx_ref, lam_ref)

        def step(it, carry):
            dh_last, pending, acc = carry
            i = nblk - 1 - it
            r0 = pl.multiple_of(i * rblk, rblk)
            h0 = hc_ref[pl.ds(pl.multiple_of(i * 8, 8), 1), :]
            _, vjp = jax.vjp(_lru_block, _halo_block(x_ref, i, rblk, 8), gt_ref[pl.ds(r0, rblk), :], h0, *params)
            dfull, dgate, dh0, *dp = vjp((dy_ref[pl.ds(r0, rblk), :], dh_last))
            dx_ref[pl.ds(r0, rblk), :] = _with_halo_grad(dfull, pending, 8, rblk).astype(BF16)
            dgt_ref[pl.ds(r0, rblk), :] = dgate.astype(BF16)
            return dh0, dfull[:8], tuple(p + q for p, q in zip(acc, dp))

        zeros = tuple(jnp.zeros(p.shape, F32) for p in params)
        _, _, acc = lax.fori_loop(0, nblk, step, (jnp.zeros((1, 128), F32), jnp.zeros((8, 128), F32), zeros))
        for k in range(4):
            dcw_ref[k:k + 1, :] = acc[k]
        for r, p in zip((dcb_ref, dwa_ref, dba_ref, dwx_ref, dbx_ref, dlam_ref), acc[4:]):
            r[...] = p

    col = pl.BlockSpec((T, 128), lambda g: (0, g))
    return pl.pallas_call(
        body, name=name, grid=(4,),
        in_specs=[pl.BlockSpec((T, 128), lambda g: (0, COL_BX // 128 + g)),
                  pl.BlockSpec((T, 128), lambda g: (0, COL_BG // 128 + g)), col,
                  pl.BlockSpec((nblk * 8, 128), lambda g: (0, g))] + _lru_param_specs(),
        out_specs=[col, col] + _lru_param_specs(),
        out_shape=[jax.ShapeDtypeStruct((T, BW), BF16), jax.ShapeDtypeStruct((T, BW), BF16),
                   jax.ShapeDtypeStruct((4, BW), F32), jax.ShapeDtypeStruct((1, BW), F32),
                   jax.ShapeDtypeStruct((4, 128, 128), F32), jax.ShapeDtypeStruct((1, BW), F32),
                   jax.ShapeDtypeStruct((4, 128, 128), F32), jax.ShapeDtypeStruct((1, BW), F32),
                   jax.ShapeDtypeStruct((1, BW), F32)],
        compiler_params=_cp("parallel"),
    )(proj, proj, dy, hc, cw, cb, wa, ba, wx, bx, lam)


def _conv_block(xfull, c0, c1, c2, c3):
    return _silu(_conv4(xfull, (c0, c1, c2, c3)))


def _conv_fwd(name, proj, col0, cw, cw_col0):
    T = proj.shape[0]
    rblk = _pick(T, (256, 128))
    nblk = T // rblk

    def body(x_ref, cw_ref, y_ref):
        rows = (cw_ref[0:1, :], cw_ref[1:2, :], cw_ref[2:3, :], cw_ref[3:4, :])

        def step(i, c):
            r0 = pl.multiple_of(i * rblk, rblk)
            y_ref[pl.ds(r0, rblk), :] = _conv_block(_halo_block(x_ref, i, rblk, 8), *rows)
            return c

        lax.fori_loop(0, nblk, step, 0)

    return pl.pallas_call(
        body, name=name, grid=(4,),
        in_specs=[pl.BlockSpec((T, 128), lambda g: (0, col0 // 128 + g)),
                  pl.BlockSpec((4, 128), lambda g: (0, cw_col0 // 128 + g))],
        out_specs=pl.BlockSpec((T, 128), lambda g: (0, g)),
        out_shape=jax.ShapeDtypeStruct((T, BW), F32),
        compiler_params=_cp("parallel"),
    )(proj, cw)


def _conv_bwd(name, proj, col0, dy, cw, cw_col0):
    T = proj.shape[0]
    rblk = _pick(T, (256, 128))
    nblk = T // rblk

    def body(x_ref, dy_ref, cw_ref, dx_ref, dcw_ref):
        rows = (cw_ref[0:1, :], cw_ref[1:2, :], cw_ref[2:3, :], cw_ref[3:4, :])

        def step(it, carry):
            pending, acc = carry
            i = nblk - 1 - it
            r0 = pl.multiple_of(i * rblk, rblk)
            _, vjp = jax.vjp(_conv_block, _halo_block(x_ref, i, rblk, 8), *rows)
            dfull, *dp = vjp(dy_ref[pl.ds(r0, rblk), :])
            dx_ref[pl.ds(r0, rblk), :] = _with_halo_grad(dfull, pending, 8, rblk).astype(BF16)
            return dfull[:8], tuple(p + q for p, q in zip(acc, dp))

        zeros = tuple(jnp.zeros((1, 128), F32) for _ in range(4))
        _, acc = lax.fori_loop(0, nblk, step, (jnp.zeros((8, 128), F32), zeros))
        for k in range(4):
            dcw_ref[k:k + 1, :] = acc[k]

    col = pl.BlockSpec((T, 128), lambda g: (0, g))
    return pl.pallas_call(
        body, name=name, grid=(4,),
        in_specs=[pl.BlockSpec((T, 128), lambda g: (0, col0 // 128 + g)), col,
                  pl.BlockSpec((4, 128), lambda g: (0, cw_col0 // 128 + g))],
        out_specs=[col, pl.BlockSpec((4, 128), lambda g: (0, g))],
        out_shape=[jax.ShapeDtypeStruct((T, BW), BF16), jax.ShapeDtypeStruct((4, BW), F32)],
        compiler_params=_cp("parallel"),
    )(proj, dy, cw)


def _pool_block(xfull, pw, sc, t0, gi):
    n = xfull.shape[0] - 16
    s2 = xfull + _shift(xfull, 1)
    s4 = s2 + _shift(s2, 2)
    s8 = s4 + _shift(s4, 4)
    s16 = s8 + _shift(s8, 8)
    s = jnp.where(gi == 0, s2, jnp.where(gi == 1, s4, jnp.where(gi == 2, s8, s16)))[16:]
    t = t0 + lax.broadcasted_iota(jnp.int32, (n, 128), 0)
    cnt = jnp.minimum(t + 1, lax.shift_left(jnp.int32(2), gi)).astype(F32)
    pooled = s / cnt - xfull[16:]
    return _dot(pooled, pw) * sc


def _pool_fwd(name, proj, pw, sc):
    T = proj.shape[0]
    rblk = _pick(T, (256, 128))
    nblk = T // rblk

    def body(x_ref, pw_ref, sc_ref, y_ref):
        gi = pl.program_id(0)

        def step(i, c):
            r0 = pl.multiple_of(i * rblk, rblk)
            y = _pool_block(_halo_block(x_ref, i, rblk, 16), pw_ref[...], sc_ref[...], r0, gi)
            y_ref[pl.ds(r0, rblk), :] = y.astype(BF16)
            return c

        lax.fori_loop(0, nblk, step, 0)

    return pl.pallas_call(
        body, name=name, grid=(4,),
        in_specs=[pl.BlockSpec((T, 128), lambda g: (0, COL_DX // 128 + g)),
                  pl.BlockSpec((None, 128, 128), lambda g: (g, 0, 0)), pl.BlockSpec((1, 128), lambda g: (0, g))],
        out_specs=pl.BlockSpec((T, 128), lambda g: (0, g)),
        out_shape=jax.ShapeDtypeStruct((T, BW), BF16),
        compiler_params=_cp("parallel"),
    )(proj, pw, sc)


def _pool_bwd(name, proj, dy, pw, sc):
    T = proj.shape[0]
    rblk = _pick(T, (256, 128))
    nblk = T // rblk

    def body(x_ref, dy_ref, pw_ref, sc_ref, dx_ref, dpw_ref, dsc_ref):
        gi = pl.program_id(0)

        def step(it, carry):
            pending, apw, asc = carry
            i = nblk - 1 - it
            r0 = pl.multiple_of(i * rblk, rblk)
            _, vjp = jax.vjp(lambda xf, w, s: _pool_block(xf, w, s, r0, gi), _halo_block(x_ref, i, rblk, 16),
                             pw_ref[...], sc_ref[...])
            dfull, dw, ds = vjp(dy_ref[pl.ds(r0, rblk), :])
            dx_ref[pl.ds(r0, rblk), :] = _with_halo_grad(dfull, pending, 16, rblk).astype(BF16)
            return dfull[:16], apw + dw, asc + ds

        _, apw, asc = lax.fori_loop(0, nblk, step, (jnp.zeros((16, 128), F32), jnp.zeros((128, 128), F32),
                                                    jnp.zeros((1, 128), F32)))
        dpw_ref[...] = apw
        dsc_ref[...] = asc

    col = pl.BlockSpec((T, 128), lambda g: (0, g))
    mat = pl.BlockSpec((None, 128, 128), lambda g: (g, 0, 0))
    vec = pl.BlockSpec((1, 128), lambda g: (0, g))
    return pl.pallas_call(
        body, name=name, grid=(4,),
        in_specs=[pl.BlockSpec((T, 128), lambda g: (0, COL_DX // 128 + g)), col, mat, vec],
        out_specs=[col, mat, vec],
        out_shape=[jax.ShapeDtypeStruct((T, BW), BF16), jax.ShapeDtypeStruct((4, 128, 128), F32),
                   jax.ShapeDtypeStruct((1, BW), F32)],
        compiler_params=_cp("parallel"),
    )(proj, dy, pw, sc)


def _dot3(a, b):
    ah = a.astype(BF16)
    al = (a - ah.astype(F32)).astype(BF16)
    bh = b.astype(BF16)
    bl = (b - bh.astype(F32)).astype(BF16)

    def d(x, y):
        return lax.dot_general(x, y, _NN, preferred_element_type=F32)

    return d(ah, bh) + (d(ah, bl) + d(al, bh))


def _tri_inv(mats):
    n = mats[0].shape[0]
    eye = (lax.broadcasted_iota(jnp.int32, (n, n), 0) == lax.broadcasted_iota(jnp.int32, (n, n), 1)).astype(F32)
    ps = [eye - a for a in mats]
    ms = list(mats)
    k = 2
    while k < n:
        ms = [_dot3(m, m) for m in ms]
        ps = [p + _dot3(p, m) for p, m in zip(ps, ms)]
        k *= 2
    return ps


def _cumsum_rows(x):
    n = x.shape[0]
    row = lax.broadcasted_iota(jnp.int32, x.shape, 0)
    k = 1
    while k < n:
        x = x + jnp.where(row >= k, _shift(x, k), 0.0)
        k *= 2
    return x


def _gdn_chunk(states, qc, kc, vc, z, tail, alog, dtb, ng):
    C, H = CHUNK, 4
    hs = range(H)
    lane = lax.broadcasted_iota(jnp.int32, (C, 128), 1)
    row = lax.broadcasted_iota(jnp.int32, (C, 128), 0)
    ri = lax.broadcasted_iota(jnp.int32, (C, C), 0)
    ci = lax.broadcasted_iota(jnp.int32, (C, C), 1)
    incl = ri >= ci
    sig = _sigmoid(tail)
    gfull = -jnp.exp(alog) * _softplus(tail + dtb)
    beta = [jnp.sum(jnp.where(lane == h, sig, 0.0), axis=1, keepdims=True) for h in hs]
    g = [jnp.sum(jnp.where(lane == h + 4, gfull, 0.0), axis=1, keepdims=True) for h in hs]
    qs = [qc[:, h * 128:(h + 1) * 128] for h in hs]
    ks = [kc[:, h * 128:(h + 1) * 128] for h in hs]
    vs = [vc[:, h * 128:(h + 1) * 128] for h in hs]
    q = [t * lax.rsqrt(jnp.sum(t * t, axis=-1, keepdims=True) + EPS) * (GDN_DK ** -0.5) for t in qs]
    k = [t * lax.rsqrt(jnp.sum(t * t, axis=-1, keepdims=True) + EPS) for t in ks]
    gc = [_cumsum_rows(jnp.broadcast_to(t, (C, 128))) for t in g]
    gc_row = [jnp.transpose(t)[:C, :] for t in gc]
    gc_col = [jnp.sum(jnp.where(lane == 0, t, 0.0), axis=1, keepdims=True) for t in gc]
    decay = [jnp.exp(jnp.where(incl, gc_col[h] - gc_row[h], -1e30)) for h in hs]
    kb = [k[h] * beta[h] for h in hs]
    kk = [_dot(kb[h], k[h], _NT) for h in hs]
    t_mat = _tri_inv([jnp.where(ri > ci, kk[h] * decay[h], 0.0) for h in hs])
    egc = [jnp.exp(t) for t in gc]
    u = [_dot(t_mat[h], vs[h] * beta[h]) for h in hs]
    w = [_dot(t_mat[h], kb[h] * egc[h]) for h in hs]
    qk = [_dot(q[h], k[h], _NT) for h in hs]
    attn = [jnp.where(incl, qk[h] * decay[h], 0.0) for h in hs]
    ws = [_dot(w[h], states[h]) for h in hs]
    qs_ = [_dot(q[h] * egc[h], states[h]) for h in hs]
    v_new = [u[h] - ws[h] for h in hs]
    av = [_dot(attn[h], v_new[h]) for h in hs]
    g_last = [jnp.sum(jnp.where(row == C - 1, t, 0.0), axis=0, keepdims=True) for t in gc]
    kv = [_dot(k[h] * jnp.exp(g_last[h] - gc[h]), v_new[h], _TN) for h in hs]
    nxt = tuple(states[h] * jnp.exp(g_last[h]) + kv[h] for h in hs)
    o = [qs_[h] + av[h] for h in hs]
    on = [t * lax.rsqrt(jnp.mean(t * t, axis=-1, keepdims=True) + EPS) * ng for t in o]
    return nxt, jnp.concatenate(on, axis=1) * _silu(z)


def _gdn_blocks(T):
    tb = _pick(T, (512, 256, 128, 64))
    return tb, T // tb, tb // CHUNK


def _gdn_fwd(name, qa, ka, va, proj, alog, dtb, ng):
    T = proj.shape[0]
    tb, nb, ncb = _gdn_blocks(T)

    def body(q_ref, k_ref, v_ref, z_ref, tail_ref, alog_ref, dtb_ref, ng_ref, y_ref, sh_ref, state):
        @pl.when(pl.program_id(0) == 0)
        def _():
            state[...] = jnp.zeros((4, 128, 128), F32)

        def step(c, states):
            rows = pl.ds(pl.multiple_of(c * CHUNK, CHUNK), CHUNK)
            for h in range(4):
                sh_ref[h, c] = states[h]
            nxt, y = _gdn_chunk(states, q_ref[rows, :], k_ref[rows, :], v_ref[rows, :], z_ref[rows, :],
                                tail_ref[rows, :], alog_ref[...], dtb_ref[...], ng_ref[...])
            y_ref[rows, :] = y.astype(BF16)
            return nxt

        states = lax.fori_loop(0, ncb, step, tuple(state[h] for h in range(4)))
        for h in range(4):
            state[h] = states[h]

    blk = pl.BlockSpec((tb, BW), lambda j: (j, 0))
    vec = pl.BlockSpec((1, 128), lambda j: (0, 0))
    return pl.pallas_call(
        body, name=name, grid=(nb,),
        in_specs=[blk, blk, blk, pl.BlockSpec((tb, BW), lambda j: (j, COL_CZ // BW)),
                  pl.BlockSpec((tb, 128), lambda j: (j, COL_TAIL // 128)), vec, vec, vec],
        out_specs=[blk, pl.BlockSpec((4, ncb, 128, 128), lambda j: (0, j, 0, 0))],
        out_shape=[jax.ShapeDtypeStruct((T, BW), BF16), jax.ShapeDtypeStruct((4, T // CHUNK, 128, 128), F32)],
        scratch_shapes=[pltpu.VMEM((4, 128, 128), F32)],
        compiler_params=_cp("arbitrary"),
    )(qa, ka, va, proj, proj, alog, dtb, ng)


def _gdn_bwd(name, qa, ka, va, proj, dy, sh, alog, dtb, ng):
    T = proj.shape[0]
    tb, nb, ncb = _gdn_blocks(T)

    def body(q_ref, k_ref, v_ref, z_ref, tail_ref, dy_ref, sh_ref, alog_ref, dtb_ref, ng_ref,
             dq_ref, dk_ref, dv_ref, dz_ref, dtail_ref, dalog_ref, ddtb_ref, dng_ref, dstate):
        first = pl.program_id(0) == 0

        @pl.when(first)
        def _():
            dstate[...] = jnp.zeros((4, 128, 128), F32)

        def step(it, carry):
            dstates, pa, pd, pn = carry
            c = ncb - 1 - it
            rows = pl.ds(pl.multiple_of(c * CHUNK, CHUNK), CHUNK)
            _, vjp = jax.vjp(_gdn_chunk, tuple(sh_ref[h, c] for h in range(4)), q_ref[rows, :], k_ref[rows, :],
                             v_ref[rows, :], z_ref[rows, :], tail_ref[rows, :], alog_ref[...], dtb_ref[...], ng_ref[...])
            nxt, dq, dk, dv, dz, dtail, da, dd, dn = vjp((dstates, dy_ref[rows, :]))
            dq_ref[rows, :] = dq
            dk_ref[rows, :] = dk
            dv_ref[rows, :] = dv
            dz_ref[rows, :] = dz.astype(BF16)
            dtail_ref[rows, :] = dtail.astype(BF16)
            return nxt, pa + da, pd + dd, pn + dn

        zv = jnp.zeros((1, 128), F32)
        dstates, pa, pd, pn = lax.fori_loop(0, ncb, step, (tuple(dstate[h] for h in range(4)), zv, zv, zv))
        for h in range(4):
            dstate[h] = dstates[h]

        @pl.when(first)
        def _():
            dalog_ref[...] = pa
            ddtb_ref[...] = pd
            dng_ref[...] = pn

        @pl.when(jnp.logical_not(first))
        def _():
            dalog_ref[...] += pa
            ddtb_ref[...] += pd
            dng_ref[...] += pn

    blk = pl.BlockSpec((tb, BW), lambda j: (nb - 1 - j, 0))
    vec = pl.BlockSpec((1, 128), lambda j: (0, 0))
    return pl.pallas_call(
        body, name=name, grid=(nb,),
        in_specs=[blk, blk, blk, pl.BlockSpec((tb, BW), lambda j: (nb - 1 - j, COL_CZ // BW)),
                  pl.BlockSpec((tb, 128), lambda j: (nb - 1 - j, COL_TAIL // 128)), blk,
                  pl.BlockSpec((4, ncb, 128, 128), lambda j: (0, nb - 1 - j, 0, 0)), vec, vec, vec],
        out_specs=[blk, blk, blk, blk, pl.BlockSpec((tb, 128), lambda j: (nb - 1 - j, 0)), vec, vec, vec],
        out_shape=[jax.ShapeDtypeStruct((T, BW), F32)] * 3
        + [jax.ShapeDtypeStruct((T, BW), BF16), jax.ShapeDtypeStruct((T, 128), BF16)]
        + [jax.ShapeDtypeStruct((1, 128), F32)] * 3,
        scratch_shapes=[pltpu.VMEM((4, 128, 128), F32)],
        compiler_params=_cp("arbitrary"),
    )(qa, ka, va, proj, proj, dy, sh, alog, dtb, ng)


def _adamw(name, w, g, m, v):
    R, C = w.shape
    br = _pick(R, (512, 256, 240, 128, 64, 8))

    def body(w_ref, g_ref, m_ref, v_ref, d_ref, nm_ref, nv_ref):
        gv = g_ref[...]
        m2 = ADAM_B1 * m_ref[...] + (1.0 - ADAM_B1) * gv
        v2 = ADAM_B2 * v_ref[...] + (1.0 - ADAM_B2) * jnp.square(gv)
        m_hat = m2 / (1.0 - ADAM_B1 ** ADAM_STEP)
        v_hat = v2 / (1.0 - ADAM_B2 ** ADAM_STEP)
        d_ref[...] = -ADAM_LR * (m_hat / (jnp.sqrt(v_hat) + ADAM_EPS) + ADAM_WD * w_ref[...])
        nm_ref[...] = m2
        nv_ref[...] = v2

    spec = pl.BlockSpec((br, C), lambda i: (i, 0))
    return pl.pallas_call(
        body, name=name, grid=(R // br,),
        in_specs=[spec] * 4, out_specs=[spec] * 3,
        out_shape=[jax.ShapeDtypeStruct((R, C), F32)] * 3,
        compiler_params=_cp("parallel"),
    )(w, g, m, v)


def _sum8(name, parts):
    _, R, C = parts.shape
    br = _pick(R, (352, 368, 256, 128, 64, 16, 8))

    def body(p_ref, o_ref):
        acc = p_ref[0].astype(F32)
        for d in range(1, N_DEV):
            acc = acc + p_ref[d].astype(F32)
        o_ref[...] = acc

    return pl.pallas_call(
        body, name=name, grid=(R // br,),
        in_specs=[pl.BlockSpec((N_DEV, br, C), lambda i: (0, i, 0))],
        out_specs=pl.BlockSpec((br, C), lambda i: (i, 0)),
        out_shape=jax.ShapeDtypeStruct((R, C), F32),
        compiler_params=_cp("parallel"),
    )(parts)


_ANY = pl.BlockSpec(memory_space=pl.ANY)
_MESH = pl.DeviceIdType.MESH


def _all_gather(name, shard):
    R, C = shard.shape

    def body(x_ref, out_ref, send_sems, recv_sems, local_sem):
        x, y, c = lax.axis_index("x"), lax.axis_index("y"), lax.axis_index("c")
        me, sibling = (x, y, c), (x, y, 1 - c)
        chips = [(1 - x, y), (x, 1 - y), (1 - x, 1 - y)]

        def slot(px, py, pc):
            return out_ref.at[4 * px + 2 * py + pc]

        def copy(k, block, to, src=None):
            return pltpu.make_async_remote_copy(
                src_ref=slot(*block) if src is None else src, dst_ref=slot(*block),
                send_sem=send_sems.at[k], recv_sem=recv_sems.at[k], device_id=to, device_id_type=_MESH)

        mine = pltpu.make_async_copy(x_ref, slot(*me), local_sem)
        mine.start()
        first = [copy(0, me, sibling, src=x_ref)]
        first += [copy(1 + j, me, (*chip, c), src=x_ref) for j, chip in enumerate(chips)]
        for cp in first:
            cp.start()
        passed = [copy(4 + j, (*chip, c), sibling) for j, chip in enumerate(chips)]
        for j, chip in enumerate(chips):
            copy(1 + j, (*chip, c), me).wait_recv()
            passed[j].start()
        copy(0, sibling, me).wait_recv()
        for j, chip in enumerate(chips):
            copy(4 + j, (*chip, 1 - c), me).wait_recv()
        for cp in first + passed:
            cp.wait_send()
        mine.wait()

    return pl.pallas_call(
        body, name=name,
        in_specs=[_ANY], out_specs=_ANY,
        out_shape=jax.ShapeDtypeStruct((N_DEV, R, C), shard.dtype),
        scratch_shapes=[pltpu.SemaphoreType.DMA((7,)), pltpu.SemaphoreType.DMA((7,)), pltpu.SemaphoreType.DMA],
    )(shard)


_HBM = pl.BlockSpec(memory_space=pltpu.HBM)
_SEM = pl.BlockSpec(memory_space=pltpu.SEMAPHORE)
_EFFECT = pltpu.SideEffectType.DATAFLOW_SIDE_EFFECTING


def _exchange_copies(src_ref, land_ref, send_sems, recv_sems, scatter):
    x, y, c = lax.axis_index("x"), lax.axis_index("y"), lax.axis_index("c")
    me = 4 * x + 2 * y + c
    copies = []
    for k in range(1, N_DEV):
        px, py, pc = x ^ ((k >> 2) & 1), y ^ ((k >> 1) & 1), c ^ (k & 1)
        src = src_ref.at[4 * px + 2 * py + pc] if scatter else src_ref
        copies.append(pltpu.make_async_remote_copy(
            src_ref=src, dst_ref=land_ref.at[me], send_sem=send_sems.at[k - 1], recv_sem=recv_sems.at[k - 1],
            device_id=(px, py, pc), device_id_type=_MESH))
    return copies


def _exchange_start(name, srcs, lands, scatter):
    n = len(srcs)

    def body(*refs):
        src_refs, land_refs = refs[:n], refs[n:2 * n]
        send, recv = refs[2 * n:3 * n], refs[3 * n:4 * n]
        token = refs[-1]
        for g in range(n):
            for cp in _exchange_copies(src_refs[g], land_refs[g], send[g], recv[g], scatter):
                cp.start()
        token[...] = jnp.zeros_like(token)

    outs = pl.pallas_call(
        body, name=name,
        out_shape=tuple([pltpu.SemaphoreType.DMA((N_DEV - 1,))] * (2 * n)
                        + [pltpu.HBM(a.shape, a.dtype) for a in list(srcs) + list(lands)]
                        + [jax.ShapeDtypeStruct((8, 128), F32)]),
        in_specs=[_HBM] * (2 * n),
        out_specs=tuple([_SEM] * (2 * n) + [_HBM] * (2 * n) + [pl.BlockSpec(memory_space=pltpu.VMEM)]),
        input_output_aliases={i: 2 * n + i for i in range(2 * n)},
        compiler_params=pltpu.CompilerParams(has_side_effects=_EFFECT),
    )(*[pltpu.with_memory_space_constraint(a, pltpu.HBM) for a in list(srcs) + list(lands)])
    handles = [(outs[2 * n + g], outs[3 * n + g], outs[g], outs[n + g]) for g in range(n)]
    return handles, outs[-1]


def _exchange_wait(name, handles, after, scatter):
    n = len(handles)
    srcs, lands, sends, recvs = ([h[i] for h in handles] for i in range(4))

    def body(*refs):
        src_refs, land_refs = refs[:n], refs[n:2 * n]
        send, recv = refs[2 * n:3 * n], refs[3 * n:4 * n]
        for g in range(n):
            for cp in _exchange_copies(src_refs[g], land_refs[g], send[g], recv[g], scatter):
                cp.wait_send()
                cp.wait_recv()

    outs = pl.pallas_call(
        body, name=name,
        out_shape=tuple(pltpu.HBM(a.shape, a.dtype) for a in srcs + lands),
        in_specs=tuple([_HBM] * (2 * n) + [_SEM] * (2 * n) + [_ANY]), out_specs=tuple([_HBM] * (2 * n)),
        input_output_aliases={i: i for i in range(2 * n)},
        compiler_params=pltpu.CompilerParams(has_side_effects=_EFFECT),
    )(*srcs, *lands, *sends, *recvs, after)
    return list(outs[n:])


def _rows(a):
    return a.reshape(-1, 1024)


def _rows_to_parts(full):
    n = full.shape[-2] // N_DEV
    t = full.reshape(full.shape[:-2] + (N_DEV, n, full.shape[-1]))
    return jnp.moveaxis(t, -3, 0)


def _parts_to_rows(parts):
    t = jnp.moveaxis(parts, 0, -3)
    return t.reshape(t.shape[:-3] + (t.shape[-3] * t.shape[-2], t.shape[-1]))


def _parts_to_cols(parts):
    t = jnp.moveaxis(parts, 0, -2)
    return t.reshape(t.shape[:-2] + (t.shape[-2] * t.shape[-1],))


def _join(parts, axis=0):
    total = sum(p.shape[axis] for p in parts)
    out, off = None, 0
    for p in parts:
        cfg = [(0, 0)] * p.ndim
        cfg[axis] = (off, total - off - p.shape[axis])
        t = jnp.pad(p, cfg)
        out = t if out is None else out + t
        off += p.shape[axis]
    return out


def _w_in_to_layout(w):
    tail = jnp.pad(w[4096:4104], ((0, PW - COL_TAIL - 8), (0, 0)))
    return jnp.concatenate([w[:4096], w[4104:P_IN], tail], axis=0)


def _w_in_from_layout(g):
    return _join([g[:4096], g[COL_TAIL:COL_TAIL + 8], g[4096:COL_TAIL]], axis=0)


def _block_diag(w):
    w = w.reshape(4, 2, 64, 64)
    return jnp.pad(w[:, 0], ((0, 0), (0, 64), (0, 64))) + jnp.pad(w[:, 1], ((0, 0), (64, 0), (64, 0)))


def _block_diag_grad(g):
    return jnp.stack([g[:, :64, :64], g[:, 64:, 64:]], axis=1).reshape(8, 64, 64)


def _ffn_forward(tag, x, norm, wg, wu, wd):
    h = _rms_fwd(tag + "_norm", x, norm)
    a, b, act = _ffn_up(tag + "_up", h, wg, wu)
    x_out = _mm(tag + "_down", [(act, wd)], "nn", F32, res=x, scale=0.5)
    return x_out, (x, h, a, b, act)


def _ffn_backward(tag, dx_out, saved, norm, wg, wu, wd, put):
    x, h, a, b, act = saved
    da, db = _ffn_dact(tag + "_dact", dx_out, wd, a, b)
    dwd = _mm(tag + "_dwd", [(act, dx_out)], "tn", BF16, scale=0.5, bm=FF // 2)
    dwg = _mm(tag + "_dwg", [(da, h)], "tn", BF16, bm=FF // 2)
    dwu = _mm(tag + "_dwu", [(db, h)], "tn", BF16, bm=FF // 2)
    tok = put(dwg, dwu, dwd)
    dh = _mm(tag + "_dh", [(da, wg), (db, wu)], "nn", F32)
    dx, dnorm = _rms_bwd(tag + "_dnorm", x, norm + tok, dh, dx_out)
    return dx, dnorm


def _mixer_params(p):
    alog = jnp.pad(p["gdn_a_log"], (4, 120))[None]
    dtb = jnp.pad(p["gdn_dt_bias"], (4, 120))[None]
    bias = jnp.repeat(p["sgu_b"].T, 128, axis=1)
    return dict(
        ln_g=p["sgu_ln_g"][None], ln_b=p["sgu_ln_b"][None], sgu_w=p["sgu_w"], sgu_bias=bias,
        lru_cw=p["lru_conv_w"], lru_cb=p["lru_conv_b"][None], wa=_block_diag(p["lru_wa"]), ba=p["lru_ba"][None],
        wx=_block_diag(p["lru_wx"]), bx=p["lru_bx"][None], lam=p["lru_lambda"][None],
        gdn_cw=p["gdn_conv_w"], alog=alog, dtb=dtb, ng=p["gdn_norm_g"][None],
        pool_w=p["pool_w"], pool_sc=p["pool_scale"][None])


def _mix_forward(tag, x, p, mp):
    h = _rms_fwd(tag + "_norm", x, p["mix_norm"][None])
    proj = _mm(tag + "_proj", [(h, p["w_in"])], "nt", F32, bm=_pick(x.shape[0], (2048, 1024, 512, 256, 128)))
    y_a = _sgu_fwd(tag + "_sgu", proj, mp["ln_g"], mp["ln_b"], mp["sgu_w"], mp["sgu_bias"])
    y_b, hc = _lru_fwd(tag + "_lru", proj, mp["lru_cw"], mp["lru_cb"], mp["wa"], mp["ba"], mp["wx"], mp["bx"],
                       mp["lam"])
    qa = _conv_fwd(tag + "_convq", proj, COL_CQ, mp["gdn_cw"], 0)
    ka = _conv_fwd(tag + "_convk", proj, COL_CK, mp["gdn_cw"], 512)
    va = _conv_fwd(tag + "_convv", proj, COL_CV, mp["gdn_cw"], 1024)
    y_c, sh = _gdn_fwd(tag + "_gdn", qa, ka, va, proj, mp["alog"], mp["dtb"], mp["ng"])
    y_d = _pool_fwd(tag + "_pool", proj, mp["pool_w"], mp["pool_sc"])
    ys = (y_a, y_b, y_c, y_d)
    merged = _merge_fwd(tag + "_merge", ys, p["w_branch"], proj)
    x_out = _mm(tag + "_out", [(merged, p["w_out"])], "nn", F32, res=x)
    return x_out, (x, h, proj, hc, qa, ka, va, sh, ys, merged)


def _mix_backward(tag, dx_out, saved, p, mp, put):
    x, h, proj, hc, qa, ka, va, sh, ys, merged = saved
    T = x.shape[0]
    g = {}
    dmerged = _mm(tag + "_dmerged", [(dx_out, p["w_out"])], "nt", F32)
    g["w_out"] = _mm(tag + "_dwout", [(merged, dx_out)], "tn", BF16)
    outs = _merge_bwd(tag + "_dmerge", dmerged, ys, p["w_branch"], proj)
    dgates, dbrs = outs[:NBR], outs[NBR:]
    dys = [_mm(f"{tag}_dy{i}", [(dbrs[i], p["w_branch"][i])], "nn", F32) for i in range(NBR)]
    g["w_branch"] = jnp.stack([_mm(f"{tag}_dwb{i}", [(dbrs[i], ys[i])], "tn", BF16) for i in range(NBR)])

    du, dv, dln_g, dln_b, dsgu_w, dbias = _sgu_bwd(tag + "_dsgu", proj, dys[0], mp["ln_g"], mp["ln_b"], mp["sgu_w"],
                                                  mp["sgu_bias"])
    g["sgu_ln_g"], g["sgu_ln_b"], g["sgu_w"] = dln_g[0], dln_b[0], dsgu_w
    g["sgu_b"] = dbias.reshape(128, 4, 128).sum(axis=2).T

    (dbx, dbg, dcw, dcb, dwa, dba, dwx, dbxb, dlam) = _lru_bwd(
        tag + "_dlru", proj, dys[1], hc, mp["lru_cw"], mp["lru_cb"], mp["wa"], mp["ba"], mp["wx"], mp["bx"], mp["lam"])
    g["lru_conv_w"], g["lru_conv_b"], g["lru_ba"], g["lru_bx"], g["lru_lambda"] = dcw, dcb[0], dba[0], dbxb[0], dlam[0]
    g["lru_wa"], g["lru_wx"] = _block_diag_grad(dwa), _block_diag_grad(dwx)

    dqa, dka, dva, dz, dtail, dalog, ddtb, dng = _gdn_bwd(tag + "_dgdn", qa, ka, va, proj, dys[2], sh, mp["alog"],
                                                         mp["dtb"], mp["ng"])
    g["gdn_a_log"], g["gdn_dt_bias"], g["gdn_norm_g"] = dalog[0, 4:8], ddtb[0, 4:8], dng[0]
    dq, dcwq = _conv_bwd(tag + "_dconvq", proj, COL_CQ, dqa, mp["gdn_cw"], 0)
    dk, dcwk = _conv_bwd(tag + "_dconvk", proj, COL_CK, dka, mp["gdn_cw"], 512)
    dv_, dcwv = _conv_bwd(tag + "_dconvv", proj, COL_CV, dva, mp["gdn_cw"], 1024)
    g["gdn_conv_w"] = jnp.concatenate([dcwq, dcwk, dcwv], axis=1)

    dd, dpw, dsc = _pool_bwd(tag + "_dpool", proj, dys[3], mp["pool_w"], mp["pool_sc"])
    g["pool_w"], g["pool_scale"] = dpw, dsc[0]

    dproj = jnp.concatenate([du, dv, dbx, dbg, dq, dk, dv_, dz, dd, *dgates, dtail,
                             jnp.zeros((T, PW - COL_TAIL - 128), BF16)], axis=1)
    dw_in = _mm(tag + "_dwin", [(dproj, h)], "tn", BF16)
    tok = put(_w_in_from_layout(dw_in), g.pop("w_branch"), g.pop("w_out"))
    dh = _mm(tag + "_dh", [(dproj, p["w_in"])], "nn", F32, bm=_pick(T, (2048, 1024, 512, 256, 128)))
    dx, dnorm = _rms_bwd(tag + "_dnorm", x, p["mix_norm"][None] + tok, dh, dx_out)
    g["mix_norm"] = dnorm[0]
    return dx, g


_BIG = ("ff1_wg", "ff1_wu", "ff1_wd", "w_in", "w_branch", "w_out", "ff2_wg", "ff2_wu", "ff2_wd")
_COL_SHARDED = ("ff1_wg", "ff1_wu", "w_in", "w_branch", "ff2_wg", "ff2_wu")
_SMALL = ("ff1_norm", "mix_norm", "sgu_ln_g", "sgu_ln_b", "sgu_w", "sgu_b", "lru_conv_w", "lru_conv_b", "lru_wa",
          "lru_ba", "lru_wx", "lru_bx", "lru_lambda", "gdn_conv_w", "gdn_a_log", "gdn_dt_bias", "gdn_norm_g", "pool_w",
          "pool_scale", "ff2_norm", "final_norm")
_WEIGHTS = ("ff1_norm", "ff1_wg", "ff1_wu", "ff1_wd", "mix_norm", "w_in", "sgu_ln_g", "sgu_ln_b", "sgu_w", "sgu_b",
            "lru_conv_w", "lru_conv_b", "lru_wa", "lru_ba", "lru_wx", "lru_bx", "lru_lambda", "gdn_conv_w", "gdn_a_log",
            "gdn_dt_bias", "gdn_norm_g", "pool_w", "pool_scale", "w_branch", "w_out", "ff2_norm", "ff2_wg", "ff2_wu",
            "ff2_wd", "final_norm")
_CONV_SHARDED = ("lru_conv_w", "gdn_conv_w")
PACK_ROW_ALIGN = 16
_GROUPS = (("ff1", ("ff1_wg", "ff1_wu", "ff1_wd")), ("mix", ("w_in", "w_branch", "w_out")),
           ("ff2", ("ff2_wg", "ff2_wu", "ff2_wd")))


def _pad_rows(a, mult):
    pad = (-a.shape[-2]) % mult
    if pad == 0:
        return a
    return jnp.pad(a, [(0, 0)] * (a.ndim - 2) + [(0, pad), (0, 0)])


def _my_index():
    return 4 * lax.axis_index("x") + 2 * lax.axis_index("y") + lax.axis_index("c")


def _landing(own):
    return lax.dynamic_update_index_in_dim(lax.empty((N_DEV,) + own.shape, own.dtype), own, _my_index(), 0)


def _stored(n, a):
    return jnp.swapaxes(a, -1, -2) if n in _COL_SHARDED else a


def _gather_first(w):
    names = _GROUPS[0][1]
    shards = [_rows(_stored(n, w[n][0]).astype(BF16)) for n in names]
    got = _all_gather("gather_first", jnp.concatenate(shards, axis=0))
    out, r = {}, 0
    for n, s in zip(names, shards):
        out[n] = got[:, r:r + s.shape[0]].reshape(-1, 1024)
        r += s.shape[0]
    return out


def _gather_start(w):
    conv = _pad_rows(jnp.concatenate([w[n].reshape(1, -1) for n in _CONV_SHARDED], axis=1), 8)
    keys, srcs = ["conv"], [conv]
    for l in range(2):
        for sub, (_, names) in enumerate(_GROUPS):
            if (l, sub) != (0, 0):
                for n in names:
                    keys.append((l, sub, n))
                    srcs.append(_stored(n, w[n][l]).astype(BF16))
    handles, token = _exchange_start("gather_start", srcs, [_landing(s) for s in srcs], scatter=False)
    return dict(zip(keys, handles)), token


def _gather_finish(l, sub, handles, first, after):
    names = _GROUPS[sub][1]
    if (l, sub) == (0, 0):
        out = dict(first)
    else:
        lands = _exchange_wait(f"gather_wait_{l}{sub}", [handles[(l, sub, n)] for n in names], after, scatter=False)
        out = {n: _parts_to_rows(land) for n, land in zip(names, lands)}
    if "w_in" in out:
        out["w_in"] = _w_in_to_layout(out["w_in"])
    return out


def _scatter_start(l, sub, grads):
    srcs, shapes = [], []
    for n in _GROUPS[sub][1]:
        parts = _rows_to_parts(grads[n])
        shapes.append(parts.shape[1:])
        srcs.append(_pad_rows(parts.reshape(N_DEV, -1, 1024), PACK_ROW_ALIGN))
    me = _my_index()
    lands = [_landing(lax.dynamic_index_in_dim(s, me, 0, keepdims=False)) for s in srcs]
    handles, token = _exchange_start(f"scatter_start_{l}{sub}", srcs, lands, scatter=True)
    return handles, shapes, token


def _scatter_finish(l, sub, handles, shapes, after):
    lands = _exchange_wait(f"scatter_wait_{l}{sub}", handles, after, scatter=True)
    out = {}
    for n, land, shape in zip(_GROUPS[sub][1], lands, shapes):
        size = 1
        for s in shape:
            size *= s
        summed = _sum8(f"sum_{l}{sub}_{n}", land)
        out[n] = _stored(n, summed[:size // 1024].reshape(shape))
    return out


def _gather_conv_finish(w, handles, after):
    gconv = _exchange_wait("gather_wait_conv", [handles["conv"]], after, scatter=False)[0][:, 0]
    full, r = {}, 0
    for n in _CONV_SHARDED:
        sz = w[n].size
        full[n] = _parts_to_cols(gconv[:, r:r + sz].reshape((N_DEV,) + w[n].shape))
        r += sz
    return full


def _forward_backward(x, tgt, w, conv, get_weights, put_grads, put_small, token):
    saved, params = [], []
    for l in range(2):
        p = {n: w[n][l] for n in _SMALL if n != "final_norm"}
        for n in _CONV_SHARDED:
            p[n] = conv[n][l]
        mp = _mixer_params(p)
        tok = token[:1, :1] if l == 0 else 0.0
        p.update(get_weights(l, 0, x))
        x, s1 = _ffn_forward(f"l{l}_ff1", x, p["ff1_norm"][None] + tok, p["ff1_wg"], p["ff1_wu"], p["ff1_wd"])
        p.update(get_weights(l, 1, x))
        x, s2 = _mix_forward(f"l{l}_mix", x, p, mp)
        p.update(get_weights(l, 2, x))
        x, s3 = _ffn_forward(f"l{l}_ff2", x, p["ff2_norm"][None], p["ff2_wg"], p["ff2_wu"], p["ff2_wd"])
        saved.append((s1, s2, s3))
        params.append((p, mp))
    loss, dx, dfinal = _final_loss("loss_head", x, w["final_norm"][None], tgt)
    tok = 0.0
    for l in (1, 0):
        p, mp = params[l]
        s1, s2, s3 = saved[l]
        g = {}

        def put(sub):
            names = _GROUPS[sub][1]
            return lambda *gs, l=l: put_grads(l, sub, dict(zip(names, gs)))[:1, :1]

        dx, dn = _ffn_backward(f"l{l}_ff2", dx, s3, p["ff2_norm"][None] + tok, p["ff2_wg"], p["ff2_wu"], p["ff2_wd"],
                               put(2))
        g["ff2_norm"] = dn[0]
        dx, gm = _mix_backward(f"l{l}_mix", dx, s2, p, mp, put(1))
        g.update(gm)
        dx, dn = _ffn_backward(f"l{l}_ff1", dx, s1, p["ff1_norm"][None], p["ff1_wg"], p["ff1_wu"], p["ff1_wd"], put(0))
        g["ff1_norm"] = dn[0]
        if l == 1:
            g["final_norm"] = dfinal[0]
            g["loss"] = loss[0, :1]
        tok = put_small(l, g)[:1, :1]
    return dx


SMALL_PIECE = 8 * 1024


def _pack_small(d, names):
    pieces = []
    for n in names:
        flat = d[n].reshape(-1)
        pieces.append(jnp.pad(flat, (0, (-flat.size) % SMALL_PIECE)).reshape(-1, 1024))
    return jnp.concatenate(pieces, axis=0)


def _unpack_small(pack, shapes, names):
    out, r = {}, 0
    for n in names:
        size = 1
        for s in shapes[n]:
            size *= s
        rows = -(-size // SMALL_PIECE) * 8
        out[n] = pack[r:r + rows].reshape(-1)[:size].reshape(shapes[n])
        r += rows
    return out


def _small_names(l):
    names = tuple(n for n in _SMALL if n != "final_norm")
    return names + ("final_norm", "loss") if l == 1 else names


def _small_start(l, grads):
    pack = _pack_small(grads, _small_names(l))
    handles, token = _exchange_start(f"small_start{l}", [pack], [_landing(pack)], scatter=False)
    return handles, {n: grads[n].shape for n in _small_names(l)}, token


def _small_finish(l, handles, shapes, after):
    landed = _exchange_wait(f"small_wait{l}", handles, after, scatter=False)[0]
    return _unpack_small(_sum8(f"sum_small{l}", landed), shapes, _small_names(l))


def _as2d(a):
    if a.ndim == 1:
        return a.reshape(1, -1)
    return a.reshape(-1, a.shape[-1])


def kernel(x, ff1_norm, ff1_wg, ff1_wu, ff1_wd, mix_norm, w_in, sgu_ln_g, sgu_ln_b, sgu_w, sgu_b, lru_conv_w, lru_conv_b, lru_wa, lru_ba, lru_wx, lru_bx, lru_lambda, gdn_conv_w, gdn_a_log, gdn_dt_bias, gdn_norm_g, pool_w, pool_scale, w_branch, w_out, ff2_norm, ff2_wg, ff2_wu, ff2_wd, final_norm, loss_target, m_ff1_norm, m_ff1_wg, m_ff1_wu, m_ff1_wd, m_mix_norm, m_w_in, m_sgu_ln_g, m_sgu_ln_b, m_sgu_w, m_sgu_b, m_lru_conv_w, m_lru_conv_b, m_lru_wa, m_lru_ba, m_lru_wx, m_lru_bx, m_lru_lambda, m_gdn_conv_w, m_gdn_a_log, m_gdn_dt_bias, m_gdn_norm_g, m_pool_w, m_pool_scale, m_w_branch, m_w_out, m_ff2_norm, m_ff2_wg, m_ff2_wu, m_ff2_wd, m_final_norm, v_ff1_norm, v_ff1_wg, v_ff1_wu, v_ff1_wd, v_mix_norm, v_w_in, v_sgu_ln_g, v_sgu_ln_b, v_sgu_w, v_sgu_b, v_lru_conv_w, v_lru_conv_b, v_lru_wa, v_lru_ba, v_lru_wx, v_lru_bx, v_lru_lambda, v_gdn_conv_w, v_gdn_a_log, v_gdn_dt_bias, v_gdn_norm_g, v_pool_w, v_pool_scale, v_w_branch, v_w_out, v_ff2_norm, v_ff2_wg, v_ff2_wu, v_ff2_wd, v_final_norm):
    w = dict(ff1_norm=ff1_norm, ff1_wg=ff1_wg, ff1_wu=ff1_wu, ff1_wd=ff1_wd, mix_norm=mix_norm, w_in=w_in,
             sgu_ln_g=sgu_ln_g, sgu_ln_b=sgu_ln_b, sgu_w=sgu_w, sgu_b=sgu_b, lru_conv_w=lru_conv_w,
             lru_conv_b=lru_conv_b, lru_wa=lru_wa, lru_ba=lru_ba, lru_wx=lru_wx, lru_bx=lru_bx, lru_lambda=lru_lambda,
             gdn_conv_w=gdn_conv_w, gdn_a_log=gdn_a_log, gdn_dt_bias=gdn_dt_bias, gdn_norm_g=gdn_norm_g, pool_w=pool_w,
             pool_scale=pool_scale, w_branch=w_branch, w_out=w_out, ff2_norm=ff2_norm, ff2_wg=ff2_wg, ff2_wu=ff2_wu,
             ff2_wd=ff2_wd, final_norm=final_norm)
    m = dict(ff1_norm=m_ff1_norm, ff1_wg=m_ff1_wg, ff1_wu=m_ff1_wu, ff1_wd=m_ff1_wd, mix_norm=m_mix_norm, w_in=m_w_in,
             sgu_ln_g=m_sgu_ln_g, sgu_ln_b=m_sgu_ln_b, sgu_w=m_sgu_w, sgu_b=m_sgu_b, lru_conv_w=m_lru_conv_w,
             lru_conv_b=m_lru_conv_b, lru_wa=m_lru_wa, lru_ba=m_lru_ba, lru_wx=m_lru_wx, lru_bx=m_lru_bx,
             lru_lambda=m_lru_lambda, gdn_conv_w=m_gdn_conv_w, gdn_a_log=m_gdn_a_log, gdn_dt_bias=m_gdn_dt_bias,
             gdn_norm_g=m_gdn_norm_g, pool_w=m_pool_w, pool_scale=m_pool_scale, w_branch=m_w_branch, w_out=m_w_out,
             ff2_norm=m_ff2_norm, ff2_wg=m_ff2_wg, ff2_wu=m_ff2_wu, ff2_wd=m_ff2_wd, final_norm=m_final_norm)
    v = dict(ff1_norm=v_ff1_norm, ff1_wg=v_ff1_wg, ff1_wu=v_ff1_wu, ff1_wd=v_ff1_wd, mix_norm=v_mix_norm, w_in=v_w_in,
             sgu_ln_g=v_sgu_ln_g, sgu_ln_b=v_sgu_ln_b, sgu_w=v_sgu_w, sgu_b=v_sgu_b, lru_conv_w=v_lru_conv_w,
             lru_conv_b=v_lru_conv_b, lru_wa=v_lru_wa, lru_ba=v_lru_ba, lru_wx=v_lru_wx, lru_bx=v_lru_bx,
             lru_lambda=v_lru_lambda, gdn_conv_w=v_gdn_conv_w, gdn_a_log=v_gdn_a_log, gdn_dt_bias=v_gdn_dt_bias,
             gdn_norm_g=v_gdn_norm_g, pool_w=v_pool_w, pool_scale=v_pool_scale, w_branch=v_w_branch, w_out=v_w_out,
             ff2_norm=v_ff2_norm, ff2_wg=v_ff2_wg, ff2_wu=v_ff2_wu, ff2_wd=v_ff2_wd, final_norm=v_final_norm)

    first = _gather_first(w)
    handles, token = _gather_start(w)
    conv = _gather_conv_finish(w, handles, token)
    pending = {}

    def get_weights(l, sub, after):
        return _gather_finish(l, sub, handles, first, after)

    def put_grads(l, sub, grads):
        hs, shapes, tok = _scatter_start(l, sub, grads)
        pending[(l, sub)] = (hs, shapes)
        return tok

    def put_small(l, grads):
        hs, shapes, tok = _small_start(l, grads)
        pending[l] = (hs, shapes)
        return tok

    T = x.shape[1]
    dx = _forward_backward(x.reshape(T, D), loss_target.reshape(T, D), w, conv, get_weights, put_grads, put_small,
                           token)
    per = {key: (_scatter_finish(*key, *pending[key], dx) if isinstance(key, tuple) else
                 _small_finish(key, *pending[key], dx)) for key in pending}
    grad = {n: jnp.stack([per[(0, sub)][n], per[(1, sub)][n]]) for sub, (_, names) in enumerate(_GROUPS) for n in names}
    small = {n: _join([per[0][n].reshape(-1), per[1][n].reshape(-1)]).reshape((2,) + per[0][n].shape)
             for n in _small_names(0)}
    small["final_norm"] = per[1]["final_norm"]
    loss = per[1]["loss"][0]
    me = _my_index()
    for n in _SMALL:
        if n in _CONV_SHARDED:
            width = w[n].shape[-1]
            grad[n] = lax.dynamic_slice_in_dim(small[n], me * width, width, axis=2)
        else:
            grad[n] = small[n]

    delta, new_m, new_v = {}, {}, {}
    for n in _BIG:
        d_, m_, v_ = _adamw("adamw_" + n, _as2d(w[n]), _as2d(grad[n]), _as2d(m[n]), _as2d(v[n]))
        delta[n], new_m[n], new_v[n] = (t.reshape(w[n].shape) for t in (d_, m_, v_))

    outs = _adamw("adamw_small", *[_pack_small(t, _SMALL) for t in (w, grad, m, v)])
    for dst, packed in zip((delta, new_m, new_v), outs):
        dst.update(_unpack_small(packed, {n: w[n].shape for n in _SMALL}, _SMALL))

    return (loss, dx.reshape(x.shape), *[grad[n] for n in _WEIGHTS], *[delta[n] for n in _WEIGHTS],
            *[new_m[n] for n in _WEIGHTS], *[new_v[n] for n in _WEIGHTS])
```

```python
import functools

import jax
import jax.numpy as jnp
from jax import lax
from jax.experimental import pallas as pl
from jax.experimental.pallas import tpu as pltpu

F32 = jnp.float32
BF16 = jnp.bfloat16
HI = lax.Precision.HIGHEST

N_DEV = 8
D = 1024
FF = 2816
BW = 512
NBR = 4
CHUNK = 64
EPS = 1e-6
LRU_C = 8.0
GDN_DK = 128

COL_AU, COL_AV, COL_BX, COL_BG = 0, 512, 1024, 1536
COL_CQ, COL_CK, COL_CV, COL_CZ = 2048, 2560, 3072, 3584
COL_DX, COL_GATE, COL_TAIL = 4096, 4608, 8704
PW = 9216
P_IN = 8712

ADAM_LR, ADAM_B1, ADAM_B2, ADAM_EPS, ADAM_WD, ADAM_STEP = 0.001, 0.9, 0.999, 1e-08, 0.01, 10

VMEM_LIMIT_V7X = 56 * 1024 * 1024

_NN = (((1,), (0,)), ((), ()))
_NT = (((1,), (1,)), ((), ()))
_TN = (((0,), (0,)), ((), ()))


def _cp(*sem):
    return pltpu.CompilerParams(dimension_semantics=tuple(sem), vmem_limit_bytes=VMEM_LIMIT_V7X)


def _dot(a, b, dims=_NN):
    return lax.dot_general(a.astype(BF16), b.astype(BF16), dims, preferred_element_type=F32)


def _dot_hi(a, b, dims=_NN):
    return lax.dot_general(a, b, dims, precision=HI, preferred_element_type=F32)


def _pick(n, cands):
    for c in cands:
        if n % c == 0:
            return c
    return n


@jax.custom_jvp
def _log1p(x):
    u = 1.0 + x
    return jnp.where(u == 1.0, x, x * jnp.log(u) / jnp.where(u == 1.0, 1.0, u - 1.0))


@_log1p.defjvp
def _log1p_jvp(p, t):
    (x,), (dx,) = p, t
    return _log1p(x), dx / (1.0 + x)


@jax.custom_jvp
def _expm1(x):
    u = jnp.exp(x)
    lu = jnp.log(u)
    small = (u == 1.0) | (lu == 0.0)
    return jnp.where(small, x, (u - 1.0) * x / jnp.where(small, 1.0, lu))


@_expm1.defjvp
def _expm1_jvp(p, t):
    (x,), (dx,) = p, t
    return _expm1(x), dx * jnp.exp(x)


def _softplus(x):
    return jnp.maximum(x, 0.0) + _log1p(jnp.exp(-jnp.abs(x)))


def _sigmoid(x):
    return jax.nn.sigmoid(x)


def _silu(x):
    return x * jax.nn.sigmoid(x)


def _gelu(x):
    return jax.nn.gelu(x)


@functools.partial(jax.custom_vjp, nondiff_argnums=(1,))
def _shift(x, s):
    return x if s == 0 else pltpu.roll(x, s, 0)


def _shift_fwd(x, s):
    return _shift(x, s), None


def _shift_bwd(s, _, g):
    n = g.shape[0]
    return (g if s == 0 else pltpu.roll(g, n - s, 0),)


_shift.defvjp(_shift_fwd, _shift_bwd)


def _scan_steps(a, b, reverse):
    n = a.shape[0]
    row = lax.broadcasted_iota(jnp.int32, a.shape, 0)
    k = 1
    while k < n:
        sh = n - k if reverse else k
        m = (row < n - k) if reverse else (row >= k)
        a_s = jnp.where(m, pltpu.roll(a, sh, 0), 1.0)
        b_s = jnp.where(m, pltpu.roll(b, sh, 0), 0.0)
        b = a * b_s + b
        a = a * a_s
        k *= 2
    return b


@jax.custom_vjp
def _scan(a, b):
    return _scan_steps(a, b, False)


def _scan_fwd(a, b):
    h = _scan_steps(a, b, False)
    return h, (a, h)


def _scan_bwd(res, dh):
    a, h = res
    n = a.shape[0]
    row = lax.broadcasted_iota(jnp.int32, a.shape, 0)
    a_next = jnp.where(row < n - 1, pltpu.roll(a, n - 1, 0), 0.0)
    g = _scan_steps(a_next, dh, True)
    h_prev = jnp.where(row >= 1, pltpu.roll(h, 1, 0), 0.0)
    return g * h_prev, g


_scan.defvjp(_scan_fwd, _scan_bwd)


def _mm(name, pairs, mode, out_dtype, *, res=None, scale=1.0, bm=None, bn=None, bk=None):
    a0, b0 = pairs[0]
    if mode == "nn":
        (M, K), N = a0.shape, b0.shape[1]
    elif mode == "nt":
        (M, K), N = a0.shape, b0.shape[0]
    else:
        (K, M), N = a0.shape, b0.shape[1]
    bm = bm or _pick(M, (1024, 512, 256, 128))
    bn = bn or _pick(N, (1024, 512, 256, 128))
    bk = bk or _pick(K, (1024, 512, 1408, 256, 128))
    nk = K // bk
    npair = len(pairs)
    dims = {"nn": _NN, "nt": _NT, "tn": _TN}[mode]

    def body(*refs):
        ab = refs[:2 * npair]
        pos = 2 * npair
        r_ref = None
        if res is not None:
            r_ref = refs[pos]
            pos += 1
        o_ref = refs[pos]
        part = None
        for p in range(npair):
            d = _dot(ab[2 * p][...], ab[2 * p + 1][...], dims)
            part = d if part is None else part + d

        def finish(acc):
            out = acc if scale == 1.0 else acc * scale
            if r_ref is not None:
                out = out + r_ref[...]
            o_ref[...] = out.astype(out_dtype)

        if nk == 1:
            finish(part)
        else:
            acc_ref = refs[pos + 1]
            k = pl.program_id(2)

            @pl.when(k == 0)
            def _():
                acc_ref[...] = part

            @pl.when(k > 0)
            def _():
                acc_ref[...] += part

            @pl.when(k == nk - 1)
            def _():
                finish(acc_ref[...])

    if mode == "nn":
        a_spec = pl.BlockSpec((bm, bk), lambda i, j, k: (i, k))
        b_spec = pl.BlockSpec((bk, bn), lambda i, j, k: (k, j))
    elif mode == "nt":
        a_spec = pl.BlockSpec((bm, bk), lambda i, j, k: (i, k))
        b_spec = pl.BlockSpec((bn, bk), lambda i, j, k: (j, k))
    else:
        a_spec = pl.BlockSpec((bk, bm), lambda i, j, k: (k, i))
        b_spec = pl.BlockSpec((bk, bn), lambda i, j, k: (k, j))
    o_spec = pl.BlockSpec((bm, bn), lambda i, j, k: (i, j))
    in_specs, args = [], []
    for a, b in pairs:
        in_specs += [a_spec, b_spec]
        args += [a, b]
    if res is not None:
        in_specs.append(o_spec)
        args.append(res)
    return pl.pallas_call(
        body, name=name, grid=(M // bm, N // bn, nk),
        in_specs=in_specs, out_specs=o_spec,
        out_shape=jax.ShapeDtypeStruct((M, N), out_dtype),
        scratch_shapes=[pltpu.VMEM((bm, bn), F32)] if nk > 1 else [],
        compiler_params=_cp("parallel", "parallel", "arbitrary"),
    )(*args)


def _rms_fwd(name, x, g):
    T = x.shape[0]
    bm = _pick(T, (512, 256, 128))

    def body(x_ref, g_ref, o_ref):
        xv = x_ref[...]
        r = lax.rsqrt(jnp.mean(xv * xv, axis=-1, keepdims=True) + EPS)
        o_ref[...] = (xv * r * g_ref[...]).astype(BF16)

    return pl.pallas_call(
        body, name=name, grid=(T // bm,),
        in_specs=[pl.BlockSpec((bm, D), lambda i: (i, 0)), pl.BlockSpec((1, D), lambda i: (0, 0))],
        out_specs=pl.BlockSpec((bm, D), lambda i: (i, 0)),
        out_shape=jax.ShapeDtypeStruct((T, D), BF16),
        compiler_params=_cp("parallel"),
    )(x, g)


def _rms_bwd(name, x, g, dh, dres):
    T = x.shape[0]
    bm = _pick(T, (512, 256, 128))

    def body(x_ref, g_ref, dh_ref, dres_ref, dx_ref, dg_ref):
        xv = x_ref[...]
        r = lax.rsqrt(jnp.mean(xv * xv, axis=-1, keepdims=True) + EPS)
        xh = xv * r
        dhv = dh_ref[...]
        dxh = dhv * g_ref[...]
        dx_ref[...] = dres_ref[...] + r * (dxh - xh * jnp.mean(dxh * xh, axis=-1, keepdims=True))
        part = jnp.sum(dhv * xh, axis=0, keepdims=True)

        @pl.when(pl.program_id(0) == 0)
        def _():
            dg_ref[...] = part

        @pl.when(pl.program_id(0) > 0)
        def _():
            dg_ref[...] += part

    row = pl.BlockSpec((bm, D), lambda i: (i, 0))
    vec = pl.BlockSpec((1, D), lambda i: (0, 0))
    return pl.pallas_call(
        body, name=name, grid=(T // bm,),
        in_specs=[row, vec, row, row], out_specs=[row, vec],
        out_shape=[jax.ShapeDtypeStruct((T, D), F32), jax.ShapeDtypeStruct((1, D), F32)],
        compiler_params=_cp("arbitrary"),
    )(x, g, dh, dres)


def _final_loss(name, x, g, tgt):
    T = x.shape[0]
    bm = _pick(T, (512, 256, 128))

    def body(x_ref, g_ref, t_ref, loss_ref, dx_ref, dg_ref):
        xv = x_ref[...]
        gv = g_ref[...]
        r = lax.rsqrt(jnp.mean(xv * xv, axis=-1, keepdims=True) + EPS)
        xh = xv * r
        e = xh * gv - t_ref[...]
        lpart = jnp.broadcast_to(0.5 * jnp.sum(jnp.mean(e * e, axis=-1, keepdims=True), axis=0, keepdims=True), (1, 128))
        dy = e * (1.0 / D)
        dxh = dy * gv
        dx_ref[...] = r * (dxh - xh * jnp.mean(dxh * xh, axis=-1, keepdims=True))
        gpart = jnp.sum(dy * xh, axis=0, keepdims=True)

        @pl.when(pl.program_id(0) == 0)
        def _():
            loss_ref[...] = lpart
            dg_ref[...] = gpart

        @pl.when(pl.program_id(0) > 0)
        def _():
            loss_ref[...] += lpart
            dg_ref[...] += gpart

    row = pl.BlockSpec((bm, D), lambda i: (i, 0))
    vec = pl.BlockSpec((1, D), lambda i: (0, 0))
    return pl.pallas_call(
        body, name=name, grid=(T // bm,),
        in_specs=[row, vec, row],
        out_specs=[pl.BlockSpec((1, 128), lambda i: (0, 0)), row, vec],
        out_shape=[jax.ShapeDtypeStruct((1, 128), F32), jax.ShapeDtypeStruct((T, D), F32),
                   jax.ShapeDtypeStruct((1, D), F32)],
        compiler_params=_cp("arbitrary"),
    )(x, g, tgt)


def _ffn_up(name, h, wg, wu):
    T = h.shape[0]
    bm = _pick(T, (2048, 1024, 512, 256, 128))
    bn = 256

    def body(h_ref, wg_ref, wu_ref, a_ref, b_ref, act_ref):
        hv = h_ref[...]
        a = _dot(hv, wg_ref[...], _NT)
        b = _dot(hv, wu_ref[...], _NT)
        a_ref[...] = a.astype(BF16)
        b_ref[...] = b.astype(BF16)
        act_ref[...] = (_silu(a) * b).astype(BF16)

    w_spec = pl.BlockSpec((bn, D), lambda i, j: (j, 0))
    o_spec = pl.BlockSpec((bm, bn), lambda i, j: (i, j))
    return pl.pallas_call(
        body, name=name, grid=(T // bm, FF // bn),
        in_specs=[pl.BlockSpec((bm, D), lambda i, j: (i, 0)), w_spec, w_spec],
        out_specs=[o_spec, o_spec, o_spec],
        out_shape=[jax.ShapeDtypeStruct((T, FF), BF16)] * 3,
        compiler_params=_cp("parallel", "parallel"),
    )(h, wg, wu)


def _ffn_dact(name, dy, wd, a, b):
    T = dy.shape[0]
    bm = _pick(T, (2048, 1024, 512, 256, 128))
    bn = 256

    def body(dy_ref, wd_ref, a_ref, b_ref, da_ref, db_ref, dy_bf):
        @pl.when(pl.program_id(1) == 0)
        def _():
            dy_bf[...] = dy_ref[...].astype(BF16)

        dact = 0.5 * _dot(dy_bf[...], wd_ref[...], _NT)
        av = a_ref[...].astype(F32)
        s = _sigmoid(av)
        da_ref[...] = (dact * b_ref[...].astype(F32) * (s * (1.0 + av * (1.0 - s)))).astype(BF16)
        db_ref[...] = (dact * (av * s)).astype(BF16)

    t_spec = pl.BlockSpec((bm, bn), lambda i, j: (i, j))
    return pl.pallas_call(
        body, name=name, grid=(T // bm, FF // bn),
        in_specs=[pl.BlockSpec((bm, D), lambda i, j: (i, 0)), pl.BlockSpec((bn, D), lambda i, j: (j, 0)),
                  t_spec, t_spec],
        out_specs=[t_spec, t_spec],
        out_shape=[jax.ShapeDtypeStruct((T, FF), BF16), jax.ShapeDtypeStruct((T, FF), BF16)],
        scratch_shapes=[pltpu.VMEM((bm, D), BF16)],
        compiler_params=_cp("parallel", "arbitrary"),
    )(dy, wd, a, b)


def _merge_specs(T, bm, bn):
    y_spec = pl.BlockSpec((bm, BW), lambda i, j: (i, 0))
    wb_spec = pl.BlockSpec((NBR, bn, BW), lambda i, j: (0, j, 0))
    gate_specs = [pl.BlockSpec((bm, bn), functools.partial(lambda i, j, o: (i, o + j), o=(COL_GATE + g * D) // bn))
                  for g in range(NBR)]
    t_spec = pl.BlockSpec((bm, bn), lambda i, j: (i, j))
    return y_spec, wb_spec, gate_specs, t_spec


def _merge_fwd(name, ys, wb, proj):
    T = proj.shape[0]
    bm = _pick(T, (512, 256, 128))
    bn = 512
    y_spec, wb_spec, gate_specs, t_spec = _merge_specs(T, bm, bn)

    def body(y0, y1, y2, y3, wb_ref, g0, g1, g2, g3, o_ref):
        acc = None
        for g, (y_ref, g_ref) in enumerate(((y0, g0), (y1, g1), (y2, g2), (y3, g3))):
            t = _sigmoid(g_ref[...]) * _dot(y_ref[...], wb_ref[g], _NT)
            acc = t if acc is None else acc + t
        o_ref[...] = acc.astype(BF16)

    return pl.pallas_call(
        body, name=name, grid=(T // bm, D // bn),
        in_specs=[y_spec] * NBR + [wb_spec] + gate_specs, out_specs=t_spec,
        out_shape=jax.ShapeDtypeStruct((T, D), BF16),
        compiler_params=_cp("parallel", "parallel"),
    )(*ys, wb, proj, proj, proj, proj)


def _merge_bwd(name, dm, ys, wb, proj):
    T = proj.shape[0]
    bm = _pick(T, (512, 256, 128))
    bn = 512
    y_spec, wb_spec, gate_specs, t_spec = _merge_specs(T, bm, bn)

    def body(dm_ref, y0, y1, y2, y3, wb_ref, g0, g1, g2, g3, *outs):
        dmv = dm_ref[...]
        for g, (y_ref, g_ref) in enumerate(((y0, g0), (y1, g1), (y2, g2), (y3, g3))):
            br = _dot(y_ref[...], wb_ref[g], _NT)
            s = _sigmoid(g_ref[...])
            outs[g][...] = (dmv * br * (s * (1.0 - s))).astype(BF16)
            outs[NBR + g][...] = (dmv * s).astype(BF16)

    return pl.pallas_call(
        body, name=name, grid=(T // bm, D // bn),
        in_specs=[t_spec] + [y_spec] * NBR + [wb_spec] + gate_specs, out_specs=[t_spec] * (2 * NBR),
        out_shape=[jax.ShapeDtypeStruct((T, D), BF16)] * (2 * NBR),
        compiler_params=_cp("parallel", "parallel"),
    )(dm, *ys, wb, proj, proj, proj, proj)


def _sgu_block(u_pre, v_pre, ln_g, ln_b, w, bias):
    u = _gelu(u_pre)
    vf = _gelu(v_pre)
    mu = jnp.mean(vf, axis=-1, keepdims=True)
    var = jnp.mean(jnp.square(vf - mu), axis=-1, keepdims=True)
    vn = (vf - mu) * lax.rsqrt(var + EPS) * ln_g + ln_b
    ri = lax.broadcasted_iota(jnp.int32, (128, 128), 0)
    ci = lax.broadcasted_iota(jnp.int32, (128, 128), 1)
    mask = (ri // CHUNK) >= (ci // CHUNK)
    outs = [_dot(jnp.where(mask, w[g], 0.0), vn[:, g * 128:(g + 1) * 128]) for g in range(4)]
    mixed = jnp.concatenate(outs, axis=1) + bias
    return u * mixed


def _sgu_param_specs():
    return [pl.BlockSpec((1, BW), lambda i: (0, 0)), pl.BlockSpec((1, BW), lambda i: (0, 0)),
            pl.BlockSpec((4, 128, 128), lambda i: (0, 0, 0)), pl.BlockSpec((128, BW), lambda i: (0, 0))]


def _sgu_fwd(name, proj, ln_g, ln_b, w, bias):
    T = proj.shape[0]
    rb = _pick(T, (256, 128))

    def body(u_ref, v_ref, g_ref, b_ref, w_ref, bias_ref, y_ref):
        for n in range(rb // 128):
            rows = slice(n * 128, (n + 1) * 128)
            y = _sgu_block(u_ref[rows, :], v_ref[rows, :], g_ref[...], b_ref[...], w_ref[...], bias_ref[...])
            y_ref[rows, :] = y.astype(BF16)

    return pl.pallas_call(
        body, name=name, grid=(T // rb,),
        in_specs=[pl.BlockSpec((rb, BW), lambda i: (i, COL_AU // BW)), pl.BlockSpec((rb, BW), lambda i: (i, COL_AV // BW))]
        + _sgu_param_specs(),
        out_specs=pl.BlockSpec((rb, BW), lambda i: (i, 0)),
        out_shape=jax.ShapeDtypeStruct((T, BW), BF16),
        compiler_params=_cp("parallel"),
    )(proj, proj, ln_g, ln_b, w, bias)


def _sgu_bwd(name, proj, dy, ln_g, ln_b, w, bias):
    T = proj.shape[0]
    rb = _pick(T, (256, 128))

    def body(u_ref, v_ref, dy_ref, g_ref, b_ref, w_ref, bias_ref, du_ref, dv_ref, dg_ref, db_ref, dw_ref, dbias_ref):
        acc = None
        for n in range(rb // 128):
            rows = slice(n * 128, (n + 1) * 128)
            _, vjp = jax.vjp(_sgu_block, u_ref[rows, :], v_ref[rows, :], g_ref[...], b_ref[...], w_ref[...],
                             bias_ref[...])
            du, dv, *dp = vjp(dy_ref[rows, :])
            du_ref[rows, :] = du.astype(BF16)
            dv_ref[rows, :] = dv.astype(BF16)
            acc = dp if acc is None else [p + q for p, q in zip(acc, dp)]

        @pl.when(pl.program_id(0) == 0)
        def _():
            for r, p in zip((dg_ref, db_ref, dw_ref, dbias_ref), acc):
                r[...] = p

        @pl.when(pl.program_id(0) > 0)
        def _():
            for r, p in zip((dg_ref, db_ref, dw_ref, dbias_ref), acc):
                r[...] += p

    row = pl.BlockSpec((rb, BW), lambda i: (i, 0))
    return pl.pallas_call(
        body, name=name, grid=(T // rb,),
        in_specs=[pl.BlockSpec((rb, BW), lambda i: (i, COL_AU // BW)), pl.BlockSpec((rb, BW), lambda i: (i, COL_AV // BW)),
                  row] + _sgu_param_specs(),
        out_specs=[row, row] + _sgu_param_specs(),
        out_shape=[jax.ShapeDtypeStruct((T, BW), BF16), jax.ShapeDtypeStruct((T, BW), BF16),
                   jax.ShapeDtypeStruct((1, BW), F32), jax.ShapeDtypeStruct((1, BW), F32),
                   jax.ShapeDtypeStruct((4, 128, 128), F32), jax.ShapeDtypeStruct((128, BW), F32)],
        compiler_params=_cp("arbitrary"),
    )(proj, proj, dy, ln_g, ln_b, w, bias)


def _halo_block(ref, i, rblk, halo):
    r0 = pl.multiple_of(i * rblk, rblk)
    h0 = pl.multiple_of(jnp.maximum(r0 - halo, 0), halo)
    top = jnp.where(i > 0, ref[pl.ds(h0, halo), :], 0.0)
    return jnp.concatenate([top, ref[pl.ds(r0, rblk), :]], axis=0)


def _with_halo_grad(dfull, pending, halo, rblk):
    tail = jnp.concatenate([jnp.zeros((rblk - halo, 128), F32), pending], axis=0)
    return dfull[halo:] + tail


def _conv4(xfull, rows):
    acc = None
    for k in range(4):
        t = rows[k] * _shift(xfull, 3 - k)[8:]
        acc = t if acc is None else acc + t
    return acc


def _lru_block(xfull, gate, h0, c0, c1, c2, c3, cb, wa, ba, wx, bx, lam):
    n = gate.shape[0]
    xc = _conv4(xfull, (c0, c1, c2, c3)) + cb
    r = _sigmoid(_dot(xc, wa) + ba)
    ig = _sigmoid(_dot(xc, wx) + bx)
    log_a = -LRU_C * r * _softplus(-lam)
    a = jnp.exp(log_a)
    mult = jnp.sqrt(-_expm1(2.0 * log_a))
    b = mult * (ig * xc)
    row = lax.broadcasted_iota(jnp.int32, (n, 128), 0)
    b = b + jnp.where(row == 0, a * h0, 0.0)
    h = _scan(a, b)
    out = h * _gelu(gate)
    h_last = jnp.sum(jnp.where(row == n - 1, h, 0.0), axis=0, keepdims=True)
    return out, h_last


def _lru_param_specs():
    vec = pl.BlockSpec((1, 128), lambda g: (0, g))
    mat = pl.BlockSpec((None, 128, 128), lambda g: (g, 0, 0))
    return [pl.BlockSpec((4, 128), lambda g: (0, g)), vec, mat, vec, mat, vec, vec]


def _lru_load_params(cw_ref, cb_ref, wa_ref, ba_ref, wx_ref, bx_ref, lam_ref):
    return (cw_ref[0:1, :], cw_ref[1:2, :], cw_ref[2:3, :], cw_ref[3:4, :], cb_ref[...], wa_ref[...], ba_ref[...],
            wx_ref[...], bx_ref[...], lam_ref[...])


def _lru_fwd(name, proj, cw, cb, wa, ba, wx, bx, lam):
    T = proj.shape[0]
    rblk = _pick(T, (256, 128))
    nblk = T // rblk

    def body(x_ref, gt_ref, cw_ref, cb_ref, wa_ref, ba_ref, wx_ref, bx_ref, lam_ref, y_ref, hc_ref):
        params = _lru_load_params(cw_ref, cb_ref, wa_ref, ba_ref, wx_ref, bx_ref, lam_ref)

        def step(i, h0):
            r0 = pl.multiple_of(i * rblk, rblk)
            out, h_last = _lru_block(_halo_block(x_ref, i, rblk, 8), gt_ref[pl.ds(r0, rblk), :], h0, *params)
            y_ref[pl.ds(r0, rblk), :] = out.astype(BF16)
            hc_ref[pl.ds(pl.multiple_of(i * 8, 8), 8), :] = jnp.broadcast_to(h0, (8, 128))
            return h_last

        lax.fori_loop(0, nblk, step, jnp.zeros((1, 128), F32))

    return pl.pallas_call(
        body, name=name, grid=(4,),
        in_specs=[pl.BlockSpec((T, 128), lambda g: (0, COL_BX // 128 + g)),
                  pl.BlockSpec((T, 128), lambda g: (0, COL_BG // 128 + g))] + _lru_param_specs(),
        out_specs=[pl.BlockSpec((T, 128), lambda g: (0, g)), pl.BlockSpec((nblk * 8, 128), lambda g: (0, g))],
        out_shape=[jax.ShapeDtypeStruct((T, BW), BF16), jax.ShapeDtypeStruct((nblk * 8, BW), F32)],
        compiler_params=_cp("parallel"),
    )(proj, proj, cw, cb, wa, ba, wx, bx, lam)


def _lru_bwd(name, proj, dy, hc, cw, cb, wa, ba, wx, bx, lam):
    T = proj.shape[0]
    rblk = _pick(T, (256, 128))
    nblk = T // rblk

    def body(x_ref, gt_ref, dy_ref, hc_ref, cw_ref, cb_ref, wa_ref, ba_ref, wx_ref, bx_ref, lam_ref,
             dx_ref, dgt_ref, dcw_ref, dcb_ref, dwa_ref, dba_ref, dwx_ref, dbx_ref, dlam_ref):
        params = _lru_load_params(cw_ref, cb_ref, wa_ref, ba_ref, wx_ref, bx_ref, lam_ref)

        def step(it, carry):
            dh_last, pending, acc = carry
            i = nblk - 1 - it
            r0 = pl.multiple_of(i * rblk, rblk)
            h0 = hc_ref[pl.ds(pl.multiple_of(i * 8, 8), 1), :]
            _, vjp = jax.vjp(_lru_block, _halo_block(x_ref, i, rblk, 8), gt_ref[pl.ds(r0, rblk), :], h0, *params)
            dfull, dgate, dh0, *dp = vjp((dy_ref[pl.ds(r0, rblk), :], dh_last))
            dx_ref[pl.ds(r0, rblk), :] = _with_halo_grad(dfull, pending, 8, rblk).astype(BF16)
            dgt_ref[pl.ds(r0, rblk), :] = dgate.astype(BF16)
            return dh0, dfull[:8], tuple(p + q for p, q in zip(acc, dp))

        zeros = tuple(jnp.zeros(p.shape, F32) for p in params)
        _, _, acc = lax.fori_loop(0, nblk, step, (jnp.zeros((1, 128), F32), jnp.zeros((8, 128), F32), zeros))
        for k in range(4):
            dcw_ref[k:k + 1, :] = acc[k]
        for r, p in zip((dcb_ref, dwa_ref, dba_ref, dwx_ref, dbx_ref, dlam_ref), acc[4:]):
            r[...] = p

    col = pl.BlockSpec((T, 128), lambda g: (0, g))
    return pl.pallas_call(
        body, name=name, grid=(4,),
        in_specs=[pl.BlockSpec((T, 128), lambda g: (0, COL_BX // 128 + g)),
                  pl.BlockSpec((T, 128), lambda g: (0, COL_BG // 128 + g)), col,
                  pl.BlockSpec((nblk * 8, 128), lambda g: (0, g))] + _lru_param_specs(),
        out_specs=[col, col] + _lru_param_specs(),
        out_shape=[jax.ShapeDtypeStruct((T, BW), BF16), jax.ShapeDtypeStruct((T, BW), BF16),
                   jax.ShapeDtypeStruct((4, BW), F32), jax.ShapeDtypeStruct((1, BW), F32),
                   jax.ShapeDtypeStruct((4, 128, 128), F32), jax.ShapeDtypeStruct((1, BW), F32),
                   jax.ShapeDtypeStruct((4, 128, 128), F32), jax.ShapeDtypeStruct((1, BW), F32),
                   jax.ShapeDtypeStruct((1, BW), F32)],
        compiler_params=_cp("parallel"),
    )(proj, proj, dy, hc, cw, cb, wa, ba, wx, bx, lam)


def _conv_block(xfull, c0, c1, c2, c3):
    return _silu(_conv4(xfull, (c0, c1, c2, c3)))


def _conv_fwd(name, proj, col0, cw, cw_col0):
    T = proj.shape[0]
    rblk = _pick(T, (256, 128))
    nblk = T // rblk

    def body(x_ref, cw_ref, y_ref):
        rows = (cw_ref[0:1, :], cw_ref[1:2, :], cw_ref[2:3, :], cw_ref[3:4, :])

        def step(i, c):
            r0 = pl.multiple_of(i * rblk, rblk)
            y_ref[pl.ds(r0, rblk), :] = _conv_block(_halo_block(x_ref, i, rblk, 8), *rows)
            return c

        lax.fori_loop(0, nblk, step, 0)

    return pl.pallas_call(
        body, name=name, grid=(4,),
        in_specs=[pl.BlockSpec((T, 128), lambda g: (0, col0 // 128 + g)),
                  pl.BlockSpec((4, 128), lambda g: (0, cw_col0 // 128 + g))],
        out_specs=pl.BlockSpec((T, 128), lambda g: (0, g)),
        out_shape=jax.ShapeDtypeStruct((T, BW), F32),
        compiler_params=_cp("parallel"),
    )(proj, cw)


def _conv_bwd(name, proj, col0, dy, cw, cw_col0):
    T = proj.shape[0]
    rblk = _pick(T, (256, 128))
    nblk = T // rblk

    def body(x_ref, dy_ref, cw_ref, dx_ref, dcw_ref):
        rows = (cw_ref[0:1, :], cw_ref[1:2, :], cw_ref[2:3, :], cw_ref[3:4, :])

        def step(it, carry):
            pending, acc = carry
            i = nblk - 1 - it
            r0 = pl.multiple_of(i * rblk, rblk)
            _, vjp = jax.vjp(_conv_block, _halo_block(x_ref, i, rblk, 8), *rows)
            dfull, *dp = vjp(dy_ref[pl.ds(r0, rblk), :])
            dx_ref[pl.ds(r0, rblk), :] = _with_halo_grad(dfull, pending, 8, rblk).astype(BF16)
            return dfull[:8], tuple(p + q for p, q in zip(acc, dp))

        zeros = tuple(jnp.zeros((1, 128), F32) for _ in range(4))
        _, acc = lax.fori_loop(0, nblk, step, (jnp.zeros((8, 128), F32), zeros))
        for k in range(4):
            dcw_ref[k:k + 1, :] = acc[k]

    col = pl.BlockSpec((T, 128), lambda g: (0, g))
    return pl.pallas_call(
        body, name=name, grid=(4,),
        in_specs=[pl.BlockSpec((T, 128), lambda g: (0, col0 // 128 + g)), col,
                  pl.BlockSpec((4, 128), lambda g: (0, cw_col0 // 128 + g))],
        out_specs=[col, pl.BlockSpec((4, 128), lambda g: (0, g))],
        out_shape=[jax.ShapeDtypeStruct((T, BW), BF16), jax.ShapeDtypeStruct((4, BW), F32)],
        compiler_params=_cp("parallel"),
    )(proj, dy, cw)


def _pool_block(xfull, pw, sc, t0, gi):
    n = xfull.shape[0] - 16
    s2 = xfull + _shift(xfull, 1)
    s4 = s2 + _shift(s2, 2)
    s8 = s4 + _shift(s4, 4)
    s16 = s8 + _shift(s8, 8)
    s = jnp.where(gi == 0, s2, jnp.where(gi == 1, s4, jnp.where(gi == 2, s8, s16)))[16:]
    t = t0 + lax.broadcasted_iota(jnp.int32, (n, 128), 0)
    cnt = jnp.minimum(t + 1, lax.shift_left(jnp.int32(2), gi)).astype(F32)
    pooled = s / cnt - xfull[16:]
    return _dot(pooled, pw) * sc


def _pool_fwd(name, proj, pw, sc):
    T = proj.shape[0]
    rblk = _pick(T, (256, 128))
    nblk = T // rblk

    def body(x_ref, pw_ref, sc_ref, y_ref):
        gi = pl.program_id(0)

        def step(i, c):
            r0 = pl.multiple_of(i * rblk, rblk)
            y = _pool_block(_halo_block(x_ref, i, rblk, 16), pw_ref[...], sc_ref[...], r0, gi)
            y_ref[pl.ds(r0, rblk), :] = y.astype(BF16)
            return c

        lax.fori_loop(0, nblk, step, 0)

    return pl.pallas_call(
        body, name=name, grid=(4,),
        in_specs=[pl.BlockSpec((T, 128), lambda g: (0, COL_DX // 128 + g)),
                  pl.BlockSpec((None, 128, 128), lambda g: (g, 0, 0)), pl.BlockSpec((1, 128), lambda g: (0, g))],
        out_specs=pl.BlockSpec((T, 128), lambda g: (0, g)),
        out_shape=jax.ShapeDtypeStruct((T, BW), BF16),
        compiler_params=_cp("parallel"),
    )(proj, pw, sc)


def _pool_bwd(name, proj, dy, pw, sc):
    T = proj.shape[0]
    rblk = _pick(T, (256, 128))
    nblk = T // rblk

    def body(x_ref, dy_ref, pw_ref, sc_ref, dx_ref, dpw_ref, dsc_ref):
        gi = pl.program_id(0)

        def step(it, carry):
            pending, apw, asc = carry
            i = nblk - 1 - it
            r0 = pl.multiple_of(i * rblk, rblk)
            _, vjp = jax.vjp(lambda xf, w, s: _pool_block(xf, w, s, r0, gi), _halo_block(x_ref, i, rblk, 16),
                             pw_ref[...], sc_ref[...])
            dfull, dw, ds = vjp(dy_ref[pl.ds(r0, rblk), :])
            dx_ref[pl.ds(r0, rblk), :] = _with_halo_grad(dfull, pending, 16, rblk).astype(BF16)
            return dfull[:16], apw + dw, asc + ds

        _, apw, asc = lax.fori_loop(0, nblk, step, (jnp.zeros((16, 128), F32), jnp.zeros((128, 128), F32),
                                                    jnp.zeros((1, 128), F32)))
        dpw_ref[...] = apw
        dsc_ref[...] = asc

    col = pl.BlockSpec((T, 128), lambda g: (0, g))
    mat = pl.BlockSpec((None, 128, 128), lambda g: (g, 0, 0))
    vec = pl.BlockSpec((1, 128), lambda g: (0, g))
    return pl.pallas_call(
        body, name=name, grid=(4,),
        in_specs=[pl.BlockSpec((T, 128), lambda g: (0, COL_DX // 128 + g)), col, mat, vec],
        out_specs=[col, mat, vec],
        out_shape=[jax.ShapeDtypeStruct((T, BW), BF16), jax.ShapeDtypeStruct((4, 128, 128), F32),
                   jax.ShapeDtypeStruct((1, BW), F32)],
        compiler_params=_cp("parallel"),
    )(proj, dy, pw, sc)


@jax.custom_vjp
def _dot3(a, b):
    ah = a.astype(BF16)
    al = (a - ah.astype(F32)).astype(BF16)
    bh = b.astype(BF16)
    bl = (b - bh.astype(F32)).astype(BF16)

    def d(x, y):
        return lax.dot_general(x, y, _NN, preferred_element_type=F32)

    return d(ah, bh) + (d(ah, bl) + d(al, bh))


def _dot3_fwd(a, b):
    return _dot3(a, b), (a, b)


def _dot3_bwd(res, g):
    a, b = res
    return _dot(g, b, _NT), _dot(a, g, _TN)


_dot3.defvjp(_dot3_fwd, _dot3_bwd)


def _tri_inv(mats):
    n = mats[0].shape[0]
    eye = (lax.broadcasted_iota(jnp.int32, (n, n), 0) == lax.broadcasted_iota(jnp.int32, (n, n), 1)).astype(F32)
    ps = [eye - a for a in mats]
    ms = list(mats)
    k = 2
    while k < n:
        ms = [_dot3(m, m) for m in ms]
        ps = [p + _dot3(p, m) for p, m in zip(ps, ms)]
        k *= 2
    return ps


def _cumsum_rows(x):
    n = x.shape[0]
    row = lax.broadcasted_iota(jnp.int32, x.shape, 0)
    k = 1
    while k < n:
        x = x + jnp.where(row >= k, _shift(x, k), 0.0)
        k *= 2
    return x


def _gdn_chunk(states, qc, kc, vc, z, tail, alog, dtb, ng):
    C, H = CHUNK, 4
    hs = range(H)
    lane = lax.broadcasted_iota(jnp.int32, (C, 128), 1)
    row = lax.broadcasted_iota(jnp.int32, (C, 128), 0)
    ri = lax.broadcasted_iota(jnp.int32, (C, C), 0)
    ci = lax.broadcasted_iota(jnp.int32, (C, C), 1)
    incl = ri >= ci
    sig = _sigmoid(tail)
    gfull = -jnp.exp(alog) * _softplus(tail + dtb)
    beta = [jnp.sum(jnp.where(lane == h, sig, 0.0), axis=1, keepdims=True) for h in hs]
    g = [jnp.sum(jnp.where(lane == h + 4, gfull, 0.0), axis=1, keepdims=True) for h in hs]
    qs = [qc[:, h * 128:(h + 1) * 128] for h in hs]
    ks = [kc[:, h * 128:(h + 1) * 128] for h in hs]
    vs = [vc[:, h * 128:(h + 1) * 128] for h in hs]
    q = [t * lax.rsqrt(jnp.sum(t * t, axis=-1, keepdims=True) + EPS) * (GDN_DK ** -0.5) for t in qs]
    k = [t * lax.rsqrt(jnp.sum(t * t, axis=-1, keepdims=True) + EPS) for t in ks]
    gc = [_cumsum_rows(jnp.broadcast_to(t, (C, 128))) for t in g]
    gc_row = [jnp.transpose(t)[:C, :] for t in gc]
    gc_col = [jnp.sum(jnp.where(lane == 0, t, 0.0), axis=1, keepdims=True) for t in gc]
    decay = [jnp.exp(jnp.where(incl, gc_col[h] - gc_row[h], -1e30)) for h in hs]
    kb = [k[h] * beta[h] for h in hs]
    kk = [_dot(kb[h], k[h], _NT) for h in hs]
    t_mat = _tri_inv([jnp.where(ri > ci, kk[h] * decay[h], 0.0) for h in hs])
    egc = [jnp.exp(t) for t in gc]
    u = [_dot(t_mat[h], vs[h] * beta[h]) for h in hs]
    w = [_dot(t_mat[h], kb[h] * egc[h]) for h in hs]
    qk = [_dot(q[h], k[h], _NT) for h in hs]
    attn = [jnp.where(incl, qk[h] * decay[h], 0.0) for h in hs]
    ws = [_dot(w[h], states[h]) for h in hs]
    qs_ = [_dot(q[h] * egc[h], states[h]) for h in hs]
    v_new = [u[h] - ws[h] for h in hs]
    av = [_dot(attn[h], v_new[h]) for h in hs]
    g_last = [jnp.sum(jnp.where(row == C - 1, t, 0.0), axis=0, keepdims=True) for t in gc]
    kv = [_dot(k[h] * jnp.exp(g_last[h] - gc[h]), v_new[h], _TN) for h in hs]
    nxt = tuple(states[h] * jnp.exp(g_last[h]) + kv[h] for h in hs)
    o = [qs_[h] + av[h] for h in hs]
    on = [t * lax.rsqrt(jnp.mean(t * t, axis=-1, keepdims=True) + EPS) * ng for t in o]
    return nxt, jnp.concatenate(on, axis=1) * _silu(z)


def _gdn_blocks(T):
    tb = _pick(T, (512, 256, 128, 64))
    return tb, T // tb, tb // CHUNK


def _gdn_fwd(name, qa, ka, va, proj, alog, dtb, ng):
    T = proj.shape[0]
    tb, nb, ncb = _gdn_blocks(T)

    def body(q_ref, k_ref, v_ref, z_ref, tail_ref, alog_ref, dtb_ref, ng_ref, y_ref, sh_ref, state):
        @pl.when(pl.program_id(0) == 0)
        def _():
            state[...] = jnp.zeros((4, 128, 128), F32)

        def step(c, states):
            rows = pl.ds(pl.multiple_of(c * CHUNK, CHUNK), CHUNK)
            for h in range(4):
                sh_ref[h, c] = states[h]
            nxt, y = _gdn_chunk(states, q_ref[rows, :], k_ref[rows, :], v_ref[rows, :], z_ref[rows, :],
                                tail_ref[rows, :], alog_ref[...], dtb_ref[...], ng_ref[...])
            y_ref[rows, :] = y.astype(BF16)
            return nxt

        states = lax.fori_loop(0, ncb, step, tuple(state[h] for h in range(4)))
        for h in range(4):
            state[h] = states[h]

    blk = pl.BlockSpec((tb, BW), lambda j: (j, 0))
    vec = pl.BlockSpec((1, 128), lambda j: (0, 0))
    return pl.pallas_call(
        body, name=name, grid=(nb,),
        in_specs=[blk, blk, blk, pl.BlockSpec((tb, BW), lambda j: (j, COL_CZ // BW)),
                  pl.BlockSpec((tb, 128), lambda j: (j, COL_TAIL // 128)), vec, vec, vec],
        out_specs=[blk, pl.BlockSpec((4, ncb, 128, 128), lambda j: (0, j, 0, 0))],
        out_shape=[jax.ShapeDtypeStruct((T, BW), BF16), jax.ShapeDtypeStruct((4, T // CHUNK, 128, 128), F32)],
        scratch_shapes=[pltpu.VMEM((4, 128, 128), F32)],
        compiler_params=_cp("arbitrary"),
    )(qa, ka, va, proj, proj, alog, dtb, ng)


def _gdn_bwd(name, qa, ka, va, proj, dy, sh, alog, dtb, ng):
    T = proj.shape[0]
    tb, nb, ncb = _gdn_blocks(T)

    def body(q_ref, k_ref, v_ref, z_ref, tail_ref, dy_ref, sh_ref, alog_ref, dtb_ref, ng_ref,
             dq_ref, dk_ref, dv_ref, dz_ref, dtail_ref, dalog_ref, ddtb_ref, dng_ref, dstate):
        first = pl.program_id(0) == 0

        @pl.when(first)
        def _():
            dstate[...] = jnp.zeros((4, 128, 128), F32)

        def step(it, carry):
            dstates, pa, pd, pn = carry
            c = ncb - 1 - it
            rows = pl.ds(pl.multiple_of(c * CHUNK, CHUNK), CHUNK)
            _, vjp = jax.vjp(_gdn_chunk, tuple(sh_ref[h, c] for h in range(4)), q_ref[rows, :], k_ref[rows, :],
                             v_ref[rows, :], z_ref[rows, :], tail_ref[rows, :], alog_ref[...], dtb_ref[...], ng_ref[...])
            nxt, dq, dk, dv, dz, dtail, da, dd, dn = vjp((dstates, dy_ref[rows, :]))
            dq_ref[rows, :] = dq
            dk_ref[rows, :] = dk
            dv_ref[rows, :] = dv
            dz_ref[rows, :] = dz.astype(BF16)
            dtail_ref[rows, :] = dtail.astype(BF16)
            return nxt, pa + da, pd + dd, pn + dn

        zv = jnp.zeros((1, 128), F32)
        dstates, pa, pd, pn = lax.fori_loop(0, ncb, step, (tuple(dstate[h] for h in range(4)), zv, zv, zv))
        for h in range(4):
            dstate[h] = dstates[h]

        @pl.when(first)
        def _():
            dalog_ref[...] = pa
            ddtb_ref[...] = pd
            dng_ref[...] = pn

        @pl.when(jnp.logical_not(first))
        def _():
            dalog_ref[...] += pa
            ddtb_ref[...] += pd
            dng_ref[...] += pn

    blk = pl.BlockSpec((tb, BW), lambda j: (nb - 1 - j, 0))
    vec = pl.BlockSpec((1, 128), lambda j: (0, 0))
    return pl.pallas_call(
        body, name=name, grid=(nb,),
        in_specs=[blk, blk, blk, pl.BlockSpec((tb, BW), lambda j: (nb - 1 - j, COL_CZ // BW)),
                  pl.BlockSpec((tb, 128), lambda j: (nb - 1 - j, COL_TAIL // 128)), blk,
                  pl.BlockSpec((4, ncb, 128, 128), lambda j: (0, nb - 1 - j, 0, 0)), vec, vec, vec],
        out_specs=[blk, blk, blk, blk, pl.BlockSpec((tb, 128), lambda j: (nb - 1 - j, 0)), vec, vec, vec],
        out_shape=[jax.ShapeDtypeStruct((T, BW), F32)] * 3
        + [jax.ShapeDtypeStruct((T, BW), BF16), jax.ShapeDtypeStruct((T, 128), BF16)]
        + [jax.ShapeDtypeStruct((1, 128), F32)] * 3,
        scratch_shapes=[pltpu.VMEM((4, 128, 128), F32)],
        compiler_params=_cp("arbitrary"),
    )(qa, ka, va, proj, proj, dy, sh, alog, dtb, ng)


def _adamw_update(w_ref, g_ref, m_ref, v_ref, d_ref, nm_ref, nv_ref):
    gv = g_ref[...]
    m2 = ADAM_B1 * m_ref[...] + (1.0 - ADAM_B1) * gv
    v2 = ADAM_B2 * v_ref[...] + (1.0 - ADAM_B2) * jnp.square(gv)
    m_hat = m2 / (1.0 - ADAM_B1 ** ADAM_STEP)
    v_hat = v2 / (1.0 - ADAM_B2 ** ADAM_STEP)
    d_ref[...] = -ADAM_LR * (m_hat / (jnp.sqrt(v_hat) + ADAM_EPS) + ADAM_WD * w_ref[...])
    nm_ref[...] = m2
    nv_ref[...] = v2


def _adamw_many(name, ws, gs, ms, vs):
    n = len(ws)

    def body(*refs):
        for i in range(n):
            _adamw_update(*[refs[k * n + i] for k in range(7)])

    return pl.pallas_call(
        body, name=name,
        out_shape=[jax.ShapeDtypeStruct(a.shape, F32) for a in ws] * 3,
        compiler_params=_cp(),
    )(*ws, *gs, *ms, *vs)


def _adamw(name, w, g, m, v):
    R, C = w.shape
    br = _pick(R, (512, 256, 240, 128, 64, 8))
    body = functools.partial(_adamw_update)
    spec = pl.BlockSpec((br, C), lambda i: (i, 0))
    return pl.pallas_call(
        body, name=name, grid=(R // br,),
        in_specs=[spec] * 4, out_specs=[spec] * 3,
        out_shape=[jax.ShapeDtypeStruct((R, C), F32)] * 3,
        compiler_params=_cp("parallel"),
    )(w, g, m, v)


def _sum8(name, parts):
    _, R, C = parts.shape
    br = _pick(R, (352, 368, 256, 128, 64, 16, 8))

    def body(p_ref, o_ref):
        acc = p_ref[0].astype(F32)
        for d in range(1, N_DEV):
            acc = acc + p_ref[d].astype(F32)
        o_ref[...] = acc

    return pl.pallas_call(
        body, name=name, grid=(R // br,),
        in_specs=[pl.BlockSpec((N_DEV, br, C), lambda i: (0, i, 0))],
        out_specs=pl.BlockSpec((br, C), lambda i: (i, 0)),
        out_shape=jax.ShapeDtypeStruct((R, C), F32),
        compiler_params=_cp("parallel"),
    )(parts)


_ANY = pl.BlockSpec(memory_space=pl.ANY)
_MESH = pl.DeviceIdType.MESH


def _all_gather(name, shard):
    R, C = shard.shape

    def body(x_ref, out_ref, send_sems, recv_sems, local_sem):
        x, y, c = lax.axis_index("x"), lax.axis_index("y"), lax.axis_index("c")
        me, sibling = (x, y, c), (x, y, 1 - c)
        chips = [(1 - x, y), (x, 1 - y), (1 - x, 1 - y)]

        def slot(px, py, pc):
            return out_ref.at[4 * px + 2 * py + pc]

        def copy(k, block, to, src=None):
            return pltpu.make_async_remote_copy(
                src_ref=slot(*block) if src is None else src, dst_ref=slot(*block),
                send_sem=send_sems.at[k], recv_sem=recv_sems.at[k], device_id=to, device_id_type=_MESH)

        mine = pltpu.make_async_copy(x_ref, slot(*me), local_sem)
        mine.start()
        first = [copy(0, me, sibling, src=x_ref)]
        first += [copy(1 + j, me, (*chip, c), src=x_ref) for j, chip in enumerate(chips)]
        for cp in first:
            cp.start()
        passed = [copy(4 + j, (*chip, c), sibling) for j, chip in enumerate(chips)]
        for j, chip in enumerate(chips):
            copy(1 + j, (*chip, c), me).wait_recv()
            passed[j].start()
        copy(0, sibling, me).wait_recv()
        for j, chip in enumerate(chips):
            copy(4 + j, (*chip, 1 - c), me).wait_recv()
        for cp in first + passed:
            cp.wait_send()
        mine.wait()

    return pl.pallas_call(
        body, name=name,
        in_specs=[_ANY], out_specs=_ANY,
        out_shape=jax.ShapeDtypeStruct((N_DEV, R, C), shard.dtype),
        scratch_shapes=[pltpu.SemaphoreType.DMA((7,)), pltpu.SemaphoreType.DMA((7,)), pltpu.SemaphoreType.DMA],
    )(shard)


_HBM = pl.BlockSpec(memory_space=pltpu.HBM)
_SEM = pl.BlockSpec(memory_space=pltpu.SEMAPHORE)
_EFFECT = pltpu.SideEffectType.DATAFLOW_SIDE_EFFECTING


def _exchange_copies(src_ref, land_ref, send_sems, recv_sems, scatter):
    x, y, c = lax.axis_index("x"), lax.axis_index("y"), lax.axis_index("c")
    me = 4 * x + 2 * y + c
    copies = []
    for k in range(1, N_DEV):
        px, py, pc = x ^ ((k >> 2) & 1), y ^ ((k >> 1) & 1), c ^ (k & 1)
        src = src_ref.at[4 * px + 2 * py + pc] if scatter else src_ref
        copies.append(pltpu.make_async_remote_copy(
            src_ref=src, dst_ref=land_ref.at[me], send_sem=send_sems.at[k - 1], recv_sem=recv_sems.at[k - 1],
            device_id=(px, py, pc), device_id_type=_MESH))
    return copies


def _exchange_start(name, srcs, lands, scatter, after=None):
    n = len(srcs)

    def body(*refs):
        src_refs, land_refs = refs[:n], refs[n:2 * n]
        outs = refs[2 * n + (after is not None):]
        send, recv = outs[:n], outs[n:2 * n]
        token = refs[-1]
        for g in range(n):
            for cp in _exchange_copies(src_refs[g], land_refs[g], send[g], recv[g], scatter):
                cp.start()
        token[...] = jnp.zeros_like(token)

    outs = pl.pallas_call(
        body, name=name,
        out_shape=tuple([pltpu.SemaphoreType.DMA((N_DEV - 1,))] * (2 * n)
                        + [pltpu.HBM(a.shape, a.dtype) for a in list(srcs) + list(lands)]
                        + [jax.ShapeDtypeStruct((8, 128), F32)]),
        in_specs=[_HBM] * (2 * n) + [_ANY] * (after is not None),
        out_specs=tuple([_SEM] * (2 * n) + [_HBM] * (2 * n) + [pl.BlockSpec(memory_space=pltpu.VMEM)]),
        input_output_aliases={i: 2 * n + i for i in range(2 * n)},
        compiler_params=pltpu.CompilerParams(has_side_effects=_EFFECT),
    )(*[pltpu.with_memory_space_constraint(a, pltpu.HBM) for a in list(srcs) + list(lands)],
      *([after] if after is not None else []))
    handles = [(outs[2 * n + g], outs[3 * n + g], outs[g], outs[n + g]) for g in range(n)]
    return handles, outs[-1]


def _exchange_wait(name, handles, after, scatter):
    n = len(handles)
    srcs, lands, sends, recvs = ([h[i] for h in handles] for i in range(4))

    def body(*refs):
        src_refs, land_refs = refs[:n], refs[n:2 * n]
        send, recv = refs[2 * n:3 * n], refs[3 * n:4 * n]
        for g in range(n):
            for cp in _exchange_copies(src_refs[g], land_refs[g], send[g], recv[g], scatter):
                cp.wait_send()
                cp.wait_recv()

    outs = pl.pallas_call(
        body, name=name,
        out_shape=tuple(pltpu.HBM(a.shape, a.dtype) for a in srcs + lands),
        in_specs=tuple([_HBM] * (2 * n) + [_SEM] * (2 * n) + [_ANY]), out_specs=tuple([_HBM] * (2 * n)),
        input_output_aliases={i: i for i in range(2 * n)},
        compiler_params=pltpu.CompilerParams(has_side_effects=_EFFECT),
    )(*srcs, *lands, *sends, *recvs, after)
    return list(outs[n:])


def _rows(a):
    return a.reshape(-1, 1024)


def _rows_to_parts(full):
    n = full.shape[-2] // N_DEV
    t = full.reshape(full.shape[:-2] + (N_DEV, n, full.shape[-1]))
    return jnp.moveaxis(t, -3, 0)


def _parts_to_rows(parts):
    t = jnp.moveaxis(parts, 0, -3)
    return t.reshape(t.shape[:-3] + (t.shape[-3] * t.shape[-2], t.shape[-1]))


def _parts_to_cols(parts):
    t = jnp.moveaxis(parts, 0, -2)
    return t.reshape(t.shape[:-2] + (t.shape[-2] * t.shape[-1],))


def _join(parts, axis=0):
    total = sum(p.shape[axis] for p in parts)
    out, off = None, 0
    for p in parts:
        cfg = [(0, 0)] * p.ndim
        cfg[axis] = (off, total - off - p.shape[axis])
        t = jnp.pad(p, cfg)
        out = t if out is None else out + t
        off += p.shape[axis]
    return out


def _w_in_to_layout(w):
    tail = jnp.pad(w[4096:4104], ((0, PW - COL_TAIL - 8), (0, 0)))
    return jnp.concatenate([w[:4096], w[4104:P_IN], tail], axis=0)


def _w_in_from_layout(g):
    return _join([g[:4096], g[COL_TAIL:COL_TAIL + 8], g[4096:COL_TAIL]], axis=0)


def _block_diag(w):
    w = w.reshape(4, 2, 64, 64)
    return jnp.pad(w[:, 0], ((0, 0), (0, 64), (0, 64))) + jnp.pad(w[:, 1], ((0, 0), (64, 0), (64, 0)))


def _block_diag_grad(g):
    return jnp.stack([g[:, :64, :64], g[:, 64:, 64:]], axis=1).reshape(8, 64, 64)


def _ffn_forward(tag, x, norm, wg, wu, wd):
    h = _rms_fwd(tag + "_norm", x, norm)
    a, b, act = _ffn_up(tag + "_up", h, wg, wu)
    x_out = _mm(tag + "_down", [(act, wd)], "nn", F32, res=x, scale=0.5)
    return x_out, (x, h, a, b, act)


def _ffn_backward(tag, dx_out, saved, norm, wg, wu, wd, put):
    x, h, a, b, act = saved
    da, db = _ffn_dact(tag + "_dact", dx_out, wd, a, b)
    dwd = _mm(tag + "_dwd", [(act, dx_out)], "tn", BF16, scale=0.5, bm=FF // 2)
    dwg = _mm(tag + "_dwg", [(da, h)], "tn", BF16, bm=FF // 2)
    dwu = _mm(tag + "_dwu", [(db, h)], "tn", BF16, bm=FF // 2)
    tok = put(dwg, dwu, dwd)
    dh = _mm(tag + "_dh", [(da, wg), (db, wu)], "nn", F32)
    dx, dnorm = _rms_bwd(tag + "_dnorm", x, norm + tok, dh, dx_out)
    return dx, dnorm


def _mixer_params(p):
    alog = jnp.pad(p["gdn_a_log"], (4, 120))[None]
    dtb = jnp.pad(p["gdn_dt_bias"], (4, 120))[None]
    bias = jnp.repeat(p["sgu_b"].T, 128, axis=1)
    return dict(
        ln_g=p["sgu_ln_g"][None], ln_b=p["sgu_ln_b"][None], sgu_w=p["sgu_w"], sgu_bias=bias,
        lru_cw=p["lru_conv_w"], lru_cb=p["lru_conv_b"][None], wa=_block_diag(p["lru_wa"]), ba=p["lru_ba"][None],
        wx=_block_diag(p["lru_wx"]), bx=p["lru_bx"][None], lam=p["lru_lambda"][None],
        gdn_cw=p["gdn_conv_w"], alog=alog, dtb=dtb, ng=p["gdn_norm_g"][None],
        pool_w=p["pool_w"], pool_sc=p["pool_scale"][None])


def _mix_forward(tag, x, p, mp):
    h = _rms_fwd(tag + "_norm", x, p["mix_norm"][None])
    proj = _mm(tag + "_proj", [(h, p["w_in"])], "nt", F32, bm=_pick(x.shape[0], (2048, 1024, 512, 256, 128)))
    y_a = _sgu_fwd(tag + "_sgu", proj, mp["ln_g"], mp["ln_b"], mp["sgu_w"], mp["sgu_bias"])
    y_b, hc = _lru_fwd(tag + "_lru", proj, mp["lru_cw"], mp["lru_cb"], mp["wa"], mp["ba"], mp["wx"], mp["bx"],
                       mp["lam"])
    qa = _conv_fwd(tag + "_convq", proj, COL_CQ, mp["gdn_cw"], 0)
    ka = _conv_fwd(tag + "_convk", proj, COL_CK, mp["gdn_cw"], 512)
    va = _conv_fwd(tag + "_convv", proj, COL_CV, mp["gdn_cw"], 1024)
    y_c, sh = _gdn_fwd(tag + "_gdn", qa, ka, va, proj, mp["alog"], mp["dtb"], mp["ng"])
    y_d = _pool_fwd(tag + "_pool", proj, mp["pool_w"], mp["pool_sc"])
    ys = (y_a, y_b, y_c, y_d)
    merged = _merge_fwd(tag + "_merge", ys, p["w_branch"], proj)
    x_out = _mm(tag + "_out", [(merged, p["w_out"])], "nn", F32, res=x)
    return x_out, (x, h, proj, hc, qa, ka, va, sh, ys, merged)


def _mix_backward(tag, dx_out, saved, p, mp, put):
    x, h, proj, hc, qa, ka, va, sh, ys, merged = saved
    T = x.shape[0]
    g = {}
    dmerged = _mm(tag + "_dmerged", [(dx_out, p["w_out"])], "nt", F32)
    g["w_out"] = _mm(tag + "_dwout", [(merged, dx_out)], "tn", BF16)
    outs = _merge_bwd(tag + "_dmerge", dmerged, ys, p["w_branch"], proj)
    dgates, dbrs = outs[:NBR], outs[NBR:]
    dys = [_mm(f"{tag}_dy{i}", [(dbrs[i], p["w_branch"][i])], "nn", F32) for i in range(NBR)]
    g["w_branch"] = jnp.stack([_mm(f"{tag}_dwb{i}", [(dbrs[i], ys[i])], "tn", BF16) for i in range(NBR)])

    du, dv, dln_g, dln_b, dsgu_w, dbias = _sgu_bwd(tag + "_dsgu", proj, dys[0], mp["ln_g"], mp["ln_b"], mp["sgu_w"],
                                                  mp["sgu_bias"])
    g["sgu_ln_g"], g["sgu_ln_b"], g["sgu_w"] = dln_g[0], dln_b[0], dsgu_w
    g["sgu_b"] = dbias.reshape(128, 4, 128).sum(axis=2).T

    (dbx, dbg, dcw, dcb, dwa, dba, dwx, dbxb, dlam) = _lru_bwd(
        tag + "_dlru", proj, dys[1], hc, mp["lru_cw"], mp["lru_cb"], mp["wa"], mp["ba"], mp["wx"], mp["bx"], mp["lam"])
    g["lru_conv_w"], g["lru_conv_b"], g["lru_ba"], g["lru_bx"], g["lru_lambda"] = dcw, dcb[0], dba[0], dbxb[0], dlam[0]
    g["lru_wa"], g["lru_wx"] = _block_diag_grad(dwa), _block_diag_grad(dwx)

    dqa, dka, dva, dz, dtail, dalog, ddtb, dng = _gdn_bwd(tag + "_dgdn", qa, ka, va, proj, dys[2], sh, mp["alog"],
                                                         mp["dtb"], mp["ng"])
    g["gdn_a_log"], g["gdn_dt_bias"], g["gdn_norm_g"] = dalog[0, 4:8], ddtb[0, 4:8], dng[0]
    dq, dcwq = _conv_bwd(tag + "_dconvq", proj, COL_CQ, dqa, mp["gdn_cw"], 0)
    dk, dcwk = _conv_bwd(tag + "_dconvk", proj, COL_CK, dka, mp["gdn_cw"], 512)
    dv_, dcwv = _conv_bwd(tag + "_dconvv", proj, COL_CV, dva, mp["gdn_cw"], 1024)
    g["gdn_conv_w"] = jnp.concatenate([dcwq, dcwk, dcwv], axis=1)

    dd, dpw, dsc = _pool_bwd(tag + "_dpool", proj, dys[3], mp["pool_w"], mp["pool_sc"])
    g["pool_w"], g["pool_scale"] = dpw, dsc[0]

    dproj = jnp.concatenate([du, dv, dbx, dbg, dq, dk, dv_, dz, dd, *dgates, dtail,
                             jnp.zeros((T, PW - COL_TAIL - 128), BF16)], axis=1)
    dw_in = _mm(tag + "_dwin", [(dproj, h)], "tn", BF16)
    tok = put(_w_in_from_layout(dw_in), g.pop("w_branch"), g.pop("w_out"))
    dh = _mm(tag + "_dh", [(dproj, p["w_in"])], "nn", F32, bm=_pick(T, (2048, 1024, 512, 256, 128)))
    dx, dnorm = _rms_bwd(tag + "_dnorm", x, p["mix_norm"][None] + tok, dh, dx_out)
    g["mix_norm"] = dnorm[0]
    return dx, g


_BIG = ("ff1_wg", "ff1_wu", "ff1_wd", "w_in", "w_branch", "w_out", "ff2_wg", "ff2_wu", "ff2_wd")
_COL_SHARDED = ("ff1_wg", "ff1_wu", "w_in", "w_branch", "ff2_wg", "ff2_wu")
_SMALL = ("ff1_norm", "mix_norm", "sgu_ln_g", "sgu_ln_b", "sgu_w", "sgu_b", "lru_conv_w", "lru_conv_b", "lru_wa",
          "lru_ba", "lru_wx", "lru_bx", "lru_lambda", "gdn_conv_w", "gdn_a_log", "gdn_dt_bias", "gdn_norm_g", "pool_w",
          "pool_scale", "ff2_norm", "final_norm")
_WEIGHTS = ("ff1_norm", "ff1_wg", "ff1_wu", "ff1_wd", "mix_norm", "w_in", "sgu_ln_g", "sgu_ln_b", "sgu_w", "sgu_b",
            "lru_conv_w", "lru_conv_b", "lru_wa", "lru_ba", "lru_wx", "lru_bx", "lru_lambda", "gdn_conv_w", "gdn_a_log",
            "gdn_dt_bias", "gdn_norm_g", "pool_w", "pool_scale", "w_branch", "w_out", "ff2_norm", "ff2_wg", "ff2_wu",
            "ff2_wd", "final_norm")
_CONV_SHARDED = ("lru_conv_w", "gdn_conv_w")
PACK_ROW_ALIGN = 16
_GROUPS = (("ff1", ("ff1_wg", "ff1_wu", "ff1_wd")), ("mix", ("w_in", "w_branch", "w_out")),
           ("ff2", ("ff2_wg", "ff2_wu", "ff2_wd")))


def _pad_rows(a, mult):
    pad = (-a.shape[-2]) % mult
    if pad == 0:
        return a
    return jnp.pad(a, [(0, 0)] * (a.ndim - 2) + [(0, pad), (0, 0)])


def _my_index():
    return 4 * lax.axis_index("x") + 2 * lax.axis_index("y") + lax.axis_index("c")


def _landing(own):
    return lax.dynamic_update_index_in_dim(lax.empty((N_DEV,) + own.shape, own.dtype), own, _my_index(), 0)


def _stored(n, a):
    return jnp.swapaxes(a, -1, -2) if n in _COL_SHARDED else a


def _gather_first(w):
    names = _GROUPS[0][1]
    shards = [_rows(_stored(n, w[n][0]).astype(BF16)) for n in names]
    got = _all_gather("gather_first", jnp.concatenate(shards, axis=0))
    out, r = {}, 0
    for n, s in zip(names, shards):
        out[n] = got[:, r:r + s.shape[0]].reshape(-1, 1024)
        r += s.shape[0]
    return out, got


def _gather_start(w, after):
    conv = _pad_rows(jnp.concatenate([w[n].reshape(1, -1) for n in _CONV_SHARDED], axis=1), 8)
    keys, srcs = ["conv"], [conv]
    for l in range(2):
        for sub, (_, names) in enumerate(_GROUPS):
            if (l, sub) != (0, 0):
                for n in names:
                    keys.append((l, sub, n))
                    srcs.append(_stored(n, w[n][l]).astype(BF16))
    handles, token = _exchange_start("gather_start", srcs, [_landing(s) for s in srcs], scatter=False, after=after)
    return dict(zip(keys, handles)), token


def _gather_finish(l, sub, handles, first, after):
    names = _GROUPS[sub][1]
    if (l, sub) == (0, 0):
        out = dict(first)
    else:
        lands = _exchange_wait(f"gather_wait_{l}{sub}", [handles[(l, sub, n)] for n in names], after, scatter=False)
        out = {n: _parts_to_rows(land) for n, land in zip(names, lands)}
    if "w_in" in out:
        out["w_in"] = _w_in_to_layout(out["w_in"])
    return out


def _scatter_start(l, sub, grads):
    srcs, shapes = [], []
    for n in _GROUPS[sub][1]:
        parts = _rows_to_parts(grads[n])
        shapes.append(parts.shape[1:])
        srcs.append(_pad_rows(parts.reshape(N_DEV, -1, 1024), PACK_ROW_ALIGN))
    me = _my_index()
    lands = [_landing(lax.dynamic_index_in_dim(s, me, 0, keepdims=False)) for s in srcs]
    handles, token = _exchange_start(f"scatter_start_{l}{sub}", srcs, lands, scatter=True)
    return handles, shapes, token


def _scatter_finish(l, sub, handles, shapes, after):
    lands = _exchange_wait(f"scatter_wait_{l}{sub}", handles, after, scatter=True)
    out = {}
    for n, land, shape in zip(_GROUPS[sub][1], lands, shapes):
        size = 1
        for s in shape:
            size *= s
        summed = _sum8(f"sum_{l}{sub}_{n}", land)
        out[n] = _stored(n, summed[:size // 1024].reshape(shape))
    return out


def _gather_conv_finish(w, handles, after):
    gconv = _exchange_wait("gather_wait_conv", [handles["conv"]], after, scatter=False)[0][:, 0]
    full, r = {}, 0
    for n in _CONV_SHARDED:
        sz = w[n].size
        full[n] = _parts_to_cols(gconv[:, r:r + sz].reshape((N_DEV,) + w[n].shape))
        r += sz
    return full


def _forward_backward(x, tgt, w, conv, get_weights, put_grads, put_small, token):
    saved, params = [], []
    for l in range(2):
        p = {n: w[n][l] for n in _SMALL if n != "final_norm"}
        for n in _CONV_SHARDED:
            p[n] = conv[n][l]
        mp = _mixer_params(p)
        tok = token[:1, :1] if l == 0 else 0.0
        p.update(get_weights(l, 0, x))
        x, s1 = _ffn_forward(f"l{l}_ff1", x, p["ff1_norm"][None] + tok, p["ff1_wg"], p["ff1_wu"], p["ff1_wd"])
        p.update(get_weights(l, 1, x))
        x, s2 = _mix_forward(f"l{l}_mix", x, p, mp)
        p.update(get_weights(l, 2, x))
        x, s3 = _ffn_forward(f"l{l}_ff2", x, p["ff2_norm"][None], p["ff2_wg"], p["ff2_wu"], p["ff2_wd"])
        saved.append((s1, s2, s3))
        params.append((p, mp))
    loss, dx, dfinal = _final_loss("loss_head", x, w["final_norm"][None], tgt)
    tok = 0.0
    for l in (1, 0):
        p, mp = params[l]
        s1, s2, s3 = saved[l]
        g = {}

        def put(sub):
            names = _GROUPS[sub][1]
            return lambda *gs, l=l: put_grads(l, sub, dict(zip(names, gs)))[:1, :1]

        dx, dn = _ffn_backward(f"l{l}_ff2", dx, s3, p["ff2_norm"][None] + tok, p["ff2_wg"], p["ff2_wu"], p["ff2_wd"],
                               put(2))
        g["ff2_norm"] = dn[0]
        dx, gm = _mix_backward(f"l{l}_mix", dx, s2, p, mp, put(1))
        g.update(gm)
        dx, dn = _ffn_backward(f"l{l}_ff1", dx, s1, p["ff1_norm"][None], p["ff1_wg"], p["ff1_wu"], p["ff1_wd"], put(0))
        g["ff1_norm"] = dn[0]
        if l == 1:
            g["final_norm"] = dfinal[0]
            g["loss"] = loss[0, :1]
        tok = put_small(l, g)[:1, :1]
    return dx


SMALL_PIECE = 8 * 1024


def _pack_small(d, names):
    pieces = []
    for n in names:
        flat = d[n].reshape(-1)
        pieces.append(jnp.pad(flat, (0, (-flat.size) % SMALL_PIECE)).reshape(-1, 1024))
    return jnp.concatenate(pieces, axis=0)


def _unpack_small(pack, shapes, names):
    out, r = {}, 0
    for n in names:
        size = 1
        for s in shapes[n]:
            size *= s
        rows = -(-size // SMALL_PIECE) * 8
        out[n] = pack[r:r + rows].reshape(-1)[:size].reshape(shapes[n])
        r += rows
    return out


def _small_names(l):
    names = tuple(n for n in _SMALL if n != "final_norm")
    return names + ("final_norm", "loss") if l == 1 else names


def _small_start(l, grads):
    pack = _pack_small(grads, _small_names(l))
    handles, token = _exchange_start(f"small_start{l}", [pack], [_landing(pack)], scatter=False)
    return handles, {n: grads[n].shape for n in _small_names(l)}, token


def _small_finish(l, handles, shapes, after):
    landed = _exchange_wait(f"small_wait{l}", handles, after, scatter=False)[0]
    return _unpack_small(_sum8(f"sum_small{l}", landed), shapes, _small_names(l))


def _as2d(a):
    if a.ndim == 1:
        return a.reshape(1, -1)
    return a.reshape(-1, a.shape[-1])


def kernel(x, ff1_norm, ff1_wg, ff1_wu, ff1_wd, mix_norm, w_in, sgu_ln_g, sgu_ln_b, sgu_w, sgu_b, lru_conv_w, lru_conv_b, lru_wa, lru_ba, lru_wx, lru_bx, lru_lambda, gdn_conv_w, gdn_a_log, gdn_dt_bias, gdn_norm_g, pool_w, pool_scale, w_branch, w_out, ff2_norm, ff2_wg, ff2_wu, ff2_wd, final_norm, loss_target, m_ff1_norm, m_ff1_wg, m_ff1_wu, m_ff1_wd, m_mix_norm, m_w_in, m_sgu_ln_g, m_sgu_ln_b, m_sgu_w, m_sgu_b, m_lru_conv_w, m_lru_conv_b, m_lru_wa, m_lru_ba, m_lru_wx, m_lru_bx, m_lru_lambda, m_gdn_conv_w, m_gdn_a_log, m_gdn_dt_bias, m_gdn_norm_g, m_pool_w, m_pool_scale, m_w_branch, m_w_out, m_ff2_norm, m_ff2_wg, m_ff2_wu, m_ff2_wd, m_final_norm, v_ff1_norm, v_ff1_wg, v_ff1_wu, v_ff1_wd, v_mix_norm, v_w_in, v_sgu_ln_g, v_sgu_ln_b, v_sgu_w, v_sgu_b, v_lru_conv_w, v_lru_conv_b, v_lru_wa, v_lru_ba, v_lru_wx, v_lru_bx, v_lru_lambda, v_gdn_conv_w, v_gdn_a_log, v_gdn_dt_bias, v_gdn_norm_g, v_pool_w, v_pool_scale, v_w_branch, v_w_out, v_ff2_norm, v_ff2_wg, v_ff2_wu, v_ff2_wd, v_final_norm):
    w = dict(ff1_norm=ff1_norm, ff1_wg=ff1_wg, ff1_wu=ff1_wu, ff1_wd=ff1_wd, mix_norm=mix_norm, w_in=w_in,
             sgu_ln_g=sgu_ln_g, sgu_ln_b=sgu_ln_b, sgu_w=sgu_w, sgu_b=sgu_b, lru_conv_w=lru_conv_w,
             lru_conv_b=lru_conv_b, lru_wa=lru_wa, lru_ba=lru_ba, lru_wx=lru_wx, lru_bx=lru_bx, lru_lambda=lru_lambda,
             gdn_conv_w=gdn_conv_w, gdn_a_log=gdn_a_log, gdn_dt_bias=gdn_dt_bias, gdn_norm_g=gdn_norm_g, pool_w=pool_w,
             pool_scale=pool_scale, w_branch=w_branch, w_out=w_out, ff2_norm=ff2_norm, ff2_wg=ff2_wg, ff2_wu=ff2_wu,
             ff2_wd=ff2_wd, final_norm=final_norm)
    m = dict(ff1_norm=m_ff1_norm, ff1_wg=m_ff1_wg, ff1_wu=m_ff1_wu, ff1_wd=m_ff1_wd, mix_norm=m_mix_norm, w_in=m_w_in,
             sgu_ln_g=m_sgu_ln_g, sgu_ln_b=m_sgu_ln_b, sgu_w=m_sgu_w, sgu_b=m_sgu_b, lru_conv_w=m_lru_conv_w,
             lru_conv_b=m_lru_conv_b, lru_wa=m_lru_wa, lru_ba=m_lru_ba, lru_wx=m_lru_wx, lru_bx=m_lru_bx,
             lru_lambda=m_lru_lambda, gdn_conv_w=m_gdn_conv_w, gdn_a_log=m_gdn_a_log, gdn_dt_bias=m_gdn_dt_bias,
             gdn_norm_g=m_gdn_norm_g, pool_w=m_pool_w, pool_scale=m_pool_scale, w_branch=m_w_branch, w_out=m_w_out,
             ff2_norm=m_ff2_norm, ff2_wg=m_ff2_wg, ff2_wu=m_ff2_wu, ff2_wd=m_ff2_wd, final_norm=m_final_norm)
    v = dict(ff1_norm=v_ff1_norm, ff1_wg=v_ff1_wg, ff1_wu=v_ff1_wu, ff1_wd=v_ff1_wd, mix_norm=v_mix_norm, w_in=v_w_in,
             sgu_ln_g=v_sgu_ln_g, sgu_ln_b=v_sgu_ln_b, sgu_w=v_sgu_w, sgu_b=v_sgu_b, lru_conv_w=v_lru_conv_w,
             lru_conv_b=v_lru_conv_b, lru_wa=v_lru_wa, lru_ba=v_lru_ba, lru_wx=v_lru_wx, lru_bx=v_lru_bx,
             lru_lambda=v_lru_lambda, gdn_conv_w=v_gdn_conv_w, gdn_a_log=v_gdn_a_log, gdn_dt_bias=v_gdn_dt_bias,
             gdn_norm_g=v_gdn_norm_g, pool_w=v_pool_w, pool_scale=v_pool_scale, w_branch=v_w_branch, w_out=v_w_out,
             ff2_norm=v_ff2_norm, ff2_wg=v_ff2_wg, ff2_wu=v_ff2_wu, ff2_wd=v_ff2_wd, final_norm=v_final_norm)

    first, got_first = _gather_first(w)
    handles, token = _gather_start(w, got_first)
    conv = _gather_conv_finish(w, handles, token)
    pending = {}

    def get_weights(l, sub, after):
        return _gather_finish(l, sub, handles, first, after)

    def put_grads(l, sub, grads):
        hs, shapes, tok = _scatter_start(l, sub, grads)
        pending[(l, sub)] = (hs, shapes)
        return tok

    def put_small(l, grads):
        hs, shapes, tok = _small_start(l, grads)
        pending[l] = (hs, shapes)
        return tok

    T = x.shape[1]
    dx = _forward_backward(x.reshape(T, D), loss_target.reshape(T, D), w, conv, get_weights, put_grads, put_small,
                           token)
    per = {key: (_scatter_finish(*key, *pending[key], dx) if isinstance(key, tuple) else
                 _small_finish(key, *pending[key], dx)) for key in pending}
    grad = {n: jnp.stack([per[(0, sub)][n], per[(1, sub)][n]]) for sub, (_, names) in enumerate(_GROUPS) for n in names}
    small = {n: _join([per[0][n].reshape(-1), per[1][n].reshape(-1)]).reshape((2,) + per[0][n].shape)
             for n in _small_names(0)}
    small["final_norm"] = per[1]["final_norm"]
    loss = per[1]["loss"][0]
    me = _my_index()
    for n in _SMALL:
        if n in _CONV_SHARDED:
            width = w[n].shape[-1]
            grad[n] = lax.dynamic_slice_in_dim(small[n], me * width, width, axis=2)
        else:
            grad[n] = small[n]

    delta, new_m, new_v = {}, {}, {}
    for n in _BIG:
        d_, m_, v_ = _adamw("adamw_" + n, _as2d(w[n]), _as2d(grad[n]), _as2d(m[n]), _as2d(v[n]))
        delta[n], new_m[n], new_v[n] = (t.reshape(w[n].shape) for t in (d_, m_, v_))

    outs = _adamw_many("adamw_small", *[[_as2d(t[n]) for n in _SMALL] for t in (w, grad, m, v)])
    for k, dst in enumerate((delta, new_m, new_v)):
        for i, n in enumerate(_SMALL):
            dst[n] = outs[k * len(_SMALL) + i].reshape(w[n].shape)

    return (loss, dx.reshape(x.shape), *[grad[n] for n in _WEIGHTS], *[delta[n] for n in _WEIGHTS],
            *[new_m[n] for n in _WEIGHTS], *[new_v[n] for n in _WEIGHTS])
```

```python
import functools

import jax
import jax.numpy as jnp
from jax import lax
from jax.experimental import pallas as pl
from jax.experimental.pallas import tpu as pltpu

F32 = jnp.float32
BF16 = jnp.bfloat16
HI = lax.Precision.HIGHEST

N_DEV = 8
D = 1024
FF = 2816
BW = 512
NBR = 4
CHUNK = 64
EPS = 1e-6
LRU_C = 8.0
GDN_DK = 128

COL_AU, COL_AV, COL_BX, COL_BG = 0, 512, 1024, 1536
COL_CQ, COL_CK, COL_CV, COL_CZ = 2048, 2560, 3072, 3584
COL_DX, COL_GATE, COL_TAIL = 4096, 4608, 8704
PW = 9216
P_IN = 8712

ADAM_LR, ADAM_B1, ADAM_B2, ADAM_EPS, ADAM_WD, ADAM_STEP = 0.001, 0.9, 0.999, 1e-08, 0.01, 10

VMEM_LIMIT_V7X = 56 * 1024 * 1024

_NN = (((1,), (0,)), ((), ()))
_NT = (((1,), (1,)), ((), ()))
_TN = (((0,), (0,)), ((), ()))


def _cp(*sem):
    return pltpu.CompilerParams(dimension_semantics=tuple(sem), vmem_limit_bytes=VMEM_LIMIT_V7X)


def _dot(a, b, dims=_NN):
    return lax.dot_general(a.astype(BF16), b.astype(BF16), dims, preferred_element_type=F32)


def _dot_hi(a, b, dims=_NN):
    return lax.dot_general(a, b, dims, precision=HI, preferred_element_type=F32)


def _pick(n, cands):
    for c in cands:
        if n % c == 0:
            return c
    return n


@jax.custom_jvp
def _log1p(x):
    u = 1.0 + x
    return jnp.where(u == 1.0, x, x * jnp.log(u) / jnp.where(u == 1.0, 1.0, u - 1.0))


@_log1p.defjvp
def _log1p_jvp(p, t):
    (x,), (dx,) = p, t
    return _log1p(x), dx / (1.0 + x)


@jax.custom_jvp
def _expm1(x):
    u = jnp.exp(x)
    lu = jnp.log(u)
    small = (u == 1.0) | (lu == 0.0)
    return jnp.where(small, x, (u - 1.0) * x / jnp.where(small, 1.0, lu))


@_expm1.defjvp
def _expm1_jvp(p, t):
    (x,), (dx,) = p, t
    return _expm1(x), dx * jnp.exp(x)


def _softplus(x):
    return jnp.maximum(x, 0.0) + _log1p(jnp.exp(-jnp.abs(x)))


def _sigmoid(x):
    return jax.nn.sigmoid(x)


def _silu(x):
    return x * jax.nn.sigmoid(x)


def _gelu(x):
    return jax.nn.gelu(x)


@functools.partial(jax.custom_vjp, nondiff_argnums=(1,))
def _shift(x, s):
    return x if s == 0 else pltpu.roll(x, s, 0)


def _shift_fwd(x, s):
    return _shift(x, s), None


def _shift_bwd(s, _, g):
    n = g.shape[0]
    return (g if s == 0 else pltpu.roll(g, n - s, 0),)


_shift.defvjp(_shift_fwd, _shift_bwd)


def _scan_steps(a, b, reverse):
    n = a.shape[0]
    row = lax.broadcasted_iota(jnp.int32, a.shape, 0)
    k = 1
    while k < n:
        sh = n - k if reverse else k
        m = (row < n - k) if reverse else (row >= k)
        a_s = jnp.where(m, pltpu.roll(a, sh, 0), 1.0)
        b_s = jnp.where(m, pltpu.roll(b, sh, 0), 0.0)
        b = a * b_s + b
        a = a * a_s
        k *= 2
    return b


@jax.custom_vjp
def _scan(a, b):
    return _scan_steps(a, b, False)


def _scan_fwd(a, b):
    h = _scan_steps(a, b, False)
    return h, (a, h)


def _scan_bwd(res, dh):
    a, h = res
    n = a.shape[0]
    row = lax.broadcasted_iota(jnp.int32, a.shape, 0)
    a_next = jnp.where(row < n - 1, pltpu.roll(a, n - 1, 0), 0.0)
    g = _scan_steps(a_next, dh, True)
    h_prev = jnp.where(row >= 1, pltpu.roll(h, 1, 0), 0.0)
    return g * h_prev, g


_scan.defvjp(_scan_fwd, _scan_bwd)


def _mm(name, pairs, mode, out_dtype, *, res=None, scale=1.0, bm=None, bn=None, bk=None):
    a0, b0 = pairs[0]
    if mode == "nn":
        (M, K), N = a0.shape, b0.shape[1]
    elif mode == "nt":
        (M, K), N = a0.shape, b0.shape[0]
    else:
        (K, M), N = a0.shape, b0.shape[1]
    bm = bm or _pick(M, (1024, 512, 256, 128))
    bn = bn or _pick(N, (1024, 512, 256, 128))
    bk = bk or _pick(K, (1024, 512, 1408, 256, 128))
    nk = K // bk
    npair = len(pairs)
    dims = {"nn": _NN, "nt": _NT, "tn": _TN}[mode]

    def body(*refs):
        ab = refs[:2 * npair]
        pos = 2 * npair
        r_ref = None
        if res is not None:
            r_ref = refs[pos]
            pos += 1
        o_ref = refs[pos]
        part = None
        for p in range(npair):
            d = _dot(ab[2 * p][...], ab[2 * p + 1][...], dims)
            part = d if part is None else part + d

        def finish(acc):
            out = acc if scale == 1.0 else acc * scale
            if r_ref is not None:
                out = out + r_ref[...]
            o_ref[...] = out.astype(out_dtype)

        if nk == 1:
            finish(part)
        else:
            acc_ref = refs[pos + 1]
            k = pl.program_id(2)

            @pl.when(k == 0)
            def _():
                acc_ref[...] = part

            @pl.when(k > 0)
            def _():
                acc_ref[...] += part

            @pl.when(k == nk - 1)
            def _():
                finish(acc_ref[...])

    if mode == "nn":
        a_spec = pl.BlockSpec((bm, bk), lambda i, j, k: (i, k))
        b_spec = pl.BlockSpec((bk, bn), lambda i, j, k: (k, j))
    elif mode == "nt":
        a_spec = pl.BlockSpec((bm, bk), lambda i, j, k: (i, k))
        b_spec = pl.BlockSpec((bn, bk), lambda i, j, k: (j, k))
    else:
        a_spec = pl.BlockSpec((bk, bm), lambda i, j, k: (k, i))
        b_spec = pl.BlockSpec((bk, bn), lambda i, j, k: (k, j))
    o_spec = pl.BlockSpec((bm, bn), lambda i, j, k: (i, j))
    in_specs, args = [], []
    for a, b in pairs:
        in_specs += [a_spec, b_spec]
        args += [a, b]
    if res is not None:
        in_specs.append(o_spec)
        args.append(res)
    return pl.pallas_call(
        body, name=name, grid=(M // bm, N // bn, nk),
        in_specs=in_specs, out_specs=o_spec,
        out_shape=jax.ShapeDtypeStruct((M, N), out_dtype),
        scratch_shapes=[pltpu.VMEM((bm, bn), F32)] if nk > 1 else [],
        compiler_params=_cp("parallel", "parallel", "arbitrary"),
    )(*args)


def _rms_fwd(name, x, g):
    T = x.shape[0]
    bm = _pick(T, (512, 256, 128))

    def body(x_ref, g_ref, o_ref):
        xv = x_ref[...]
        r = lax.rsqrt(jnp.mean(xv * xv, axis=-1, keepdims=True) + EPS)
        o_ref[...] = (xv * r * g_ref[...]).astype(BF16)

    return pl.pallas_call(
        body, name=name, grid=(T // bm,),
        in_specs=[pl.BlockSpec((bm, D), lambda i: (i, 0)), pl.BlockSpec((1, D), lambda i: (0, 0))],
        out_specs=pl.BlockSpec((bm, D), lambda i: (i, 0)),
        out_shape=jax.ShapeDtypeStruct((T, D), BF16),
        compiler_params=_cp("parallel"),
    )(x, g)


def _rms_bwd(name, x, g, dh, dres):
    T = x.shape[0]
    bm = _pick(T, (512, 256, 128))

    def body(x_ref, g_ref, dh_ref, dres_ref, dx_ref, dg_ref):
        xv = x_ref[...]
        r = lax.rsqrt(jnp.mean(xv * xv, axis=-1, keepdims=True) + EPS)
        xh = xv * r
        dhv = dh_ref[...]
        dxh = dhv * g_ref[...]
        dx_ref[...] = dres_ref[...] + r * (dxh - xh * jnp.mean(dxh * xh, axis=-1, keepdims=True))
        part = jnp.sum(dhv * xh, axis=0, keepdims=True)

        @pl.when(pl.program_id(0) == 0)
        def _():
            dg_ref[...] = part

        @pl.when(pl.program_id(0) > 0)
        def _():
            dg_ref[...] += part

    row = pl.BlockSpec((bm, D), lambda i: (i, 0))
    vec = pl.BlockSpec((1, D), lambda i: (0, 0))
    return pl.pallas_call(
        body, name=name, grid=(T // bm,),
        in_specs=[row, vec, row, row], out_specs=[row, vec],
        out_shape=[jax.ShapeDtypeStruct((T, D), F32), jax.ShapeDtypeStruct((1, D), F32)],
        compiler_params=_cp("arbitrary"),
    )(x, g, dh, dres)


def _final_loss(name, x, g, tgt):
    T = x.shape[0]
    bm = _pick(T, (512, 256, 128))

    def body(x_ref, g_ref, t_ref, loss_ref, dx_ref, dg_ref):
        xv = x_ref[...]
        gv = g_ref[...]
        r = lax.rsqrt(jnp.mean(xv * xv, axis=-1, keepdims=True) + EPS)
        xh = xv * r
        e = xh * gv - t_ref[...]
        lpart = jnp.broadcast_to(0.5 * jnp.sum(jnp.mean(e * e, axis=-1, keepdims=True), axis=0, keepdims=True), (1, 128))
        dy = e * (1.0 / D)
        dxh = dy * gv
        dx_ref[...] = r * (dxh - xh * jnp.mean(dxh * xh, axis=-1, keepdims=True))
        gpart = jnp.sum(dy * xh, axis=0, keepdims=True)

        @pl.when(pl.program_id(0) == 0)
        def _():
            loss_ref[...] = lpart
            dg_ref[...] = gpart

        @pl.when(pl.program_id(0) > 0)
        def _():
            loss_ref[...] += lpart
            dg_ref[...] += gpart

    row = pl.BlockSpec((bm, D), lambda i: (i, 0))
    vec = pl.BlockSpec((1, D), lambda i: (0, 0))
    return pl.pallas_call(
        body, name=name, grid=(T // bm,),
        in_specs=[row, vec, row],
        out_specs=[pl.BlockSpec((1, 128), lambda i: (0, 0)), row, vec],
        out_shape=[jax.ShapeDtypeStruct((1, 128), F32), jax.ShapeDtypeStruct((T, D), F32),
                   jax.ShapeDtypeStruct((1, D), F32)],
        compiler_params=_cp("arbitrary"),
    )(x, g, tgt)


def _ffn_up(name, h, wg, wu):
    T = h.shape[0]
    bm = _pick(T, (2048, 1024, 512, 256, 128))
    bn = 256

    def body(h_ref, wg_ref, wu_ref, a_ref, b_ref, act_ref):
        hv = h_ref[...]
        a = _dot(hv, wg_ref[...], _NT)
        b = _dot(hv, wu_ref[...], _NT)
        a_ref[...] = a.astype(BF16)
        b_ref[...] = b.astype(BF16)
        act_ref[...] = (_silu(a) * b).astype(BF16)

    w_spec = pl.BlockSpec((bn, D), lambda i, j: (j, 0))
    o_spec = pl.BlockSpec((bm, bn), lambda i, j: (i, j))
    return pl.pallas_call(
        body, name=name, grid=(T // bm, FF // bn),
        in_specs=[pl.BlockSpec((bm, D), lambda i, j: (i, 0)), w_spec, w_spec],
        out_specs=[o_spec, o_spec, o_spec],
        out_shape=[jax.ShapeDtypeStruct((T, FF), BF16)] * 3,
        compiler_params=_cp("parallel", "parallel"),
    )(h, wg, wu)


def _ffn_dact(name, dy, wd, a, b):
    T = dy.shape[0]
    bm = _pick(T, (2048, 1024, 512, 256, 128))
    bn = 256

    def body(dy_ref, wd_ref, a_ref, b_ref, da_ref, db_ref, dy_bf):
        @pl.when(pl.program_id(1) == 0)
        def _():
            dy_bf[...] = dy_ref[...].astype(BF16)

        dact = 0.5 * _dot(dy_bf[...], wd_ref[...], _NT)
        av = a_ref[...].astype(F32)
        s = _sigmoid(av)
        da_ref[...] = (dact * b_ref[...].astype(F32) * (s * (1.0 + av * (1.0 - s)))).astype(BF16)
        db_ref[...] = (dact * (av * s)).astype(BF16)

    t_spec = pl.BlockSpec((bm, bn), lambda i, j: (i, j))
    return pl.pallas_call(
        body, name=name, grid=(T // bm, FF // bn),
        in_specs=[pl.BlockSpec((bm, D), lambda i, j: (i, 0)), pl.BlockSpec((bn, D), lambda i, j: (j, 0)),
                  t_spec, t_spec],
        out_specs=[t_spec, t_spec],
        out_shape=[jax.ShapeDtypeStruct((T, FF), BF16), jax.ShapeDtypeStruct((T, FF), BF16)],
        scratch_shapes=[pltpu.VMEM((bm, D), BF16)],
        compiler_params=_cp("parallel", "arbitrary"),
    )(dy, wd, a, b)


def _merge_specs(T, bm, bn):
    y_spec = pl.BlockSpec((bm, BW), lambda i, j: (i, 0))
    wb_spec = pl.BlockSpec((NBR, bn, BW), lambda i, j: (0, j, 0))
    gate_specs = [pl.BlockSpec((bm, bn), functools.partial(lambda i, j, o: (i, o + j), o=(COL_GATE + g * D) // bn))
                  for g in range(NBR)]
    t_spec = pl.BlockSpec((bm, bn), lambda i, j: (i, j))
    return y_spec, wb_spec, gate_specs, t_spec


def _merge_fwd(name, ys, wb, proj):
    T = proj.shape[0]
    bm = _pick(T, (512, 256, 128))
    bn = 512
    y_spec, wb_spec, gate_specs, t_spec = _merge_specs(T, bm, bn)

    def body(y0, y1, y2, y3, wb_ref, g0, g1, g2, g3, o_ref):
        acc = None
        for g, (y_ref, g_ref) in enumerate(((y0, g0), (y1, g1), (y2, g2), (y3, g3))):
            t = _sigmoid(g_ref[...]) * _dot(y_ref[...], wb_ref[g], _NT)
            acc = t if acc is None else acc + t
        o_ref[...] = acc.astype(BF16)

    return pl.pallas_call(
        body, name=name, grid=(T // bm, D // bn),
        in_specs=[y_spec] * NBR + [wb_spec] + gate_specs, out_specs=t_spec,
        out_shape=jax.ShapeDtypeStruct((T, D), BF16),
        compiler_params=_cp("parallel", "parallel"),
    )(*ys, wb, proj, proj, proj, proj)


def _merge_bwd(name, dm, ys, wb, proj):
    T = proj.shape[0]
    bm = _pick(T, (512, 256, 128))
    bn = 512
    y_spec, wb_spec, gate_specs, t_spec = _merge_specs(T, bm, bn)

    def body(dm_ref, y0, y1, y2, y3, wb_ref, g0, g1, g2, g3, *outs):
        dmv = dm_ref[...]
        for g, (y_ref, g_ref) in enumerate(((y0, g0), (y1, g1), (y2, g2), (y3, g3))):
            br = _dot(y_ref[...], wb_ref[g], _NT)
            s = _sigmoid(g_ref[...])
            outs[g][...] = (dmv * br * (s * (1.0 - s))).astype(BF16)
            outs[NBR + g][...] = (dmv * s).astype(BF16)

    return pl.pallas_call(
        body, name=name, grid=(T // bm, D // bn),
        in_specs=[t_spec] + [y_spec] * NBR + [wb_spec] + gate_specs, out_specs=[t_spec] * (2 * NBR),
        out_shape=[jax.ShapeDtypeStruct((T, D), BF16)] * (2 * NBR),
        compiler_params=_cp("parallel", "parallel"),
    )(dm, *ys, wb, proj, proj, proj, proj)


def _sgu_block(u_pre, v_pre, ln_g, ln_b, w, bias):
    u = _gelu(u_pre)
    vf = _gelu(v_pre)
    mu = jnp.mean(vf, axis=-1, keepdims=True)
    var = jnp.mean(jnp.square(vf - mu), axis=-1, keepdims=True)
    vn = (vf - mu) * lax.rsqrt(var + EPS) * ln_g + ln_b
    ri = lax.broadcasted_iota(jnp.int32, (128, 128), 0)
    ci = lax.broadcasted_iota(jnp.int32, (128, 128), 1)
    mask = (ri // CHUNK) >= (ci // CHUNK)
    outs = [_dot(jnp.where(mask, w[g], 0.0), vn[:, g * 128:(g + 1) * 128]) for g in range(4)]
    mixed = jnp.concatenate(outs, axis=1) + bias
    return u * mixed


def _sgu_param_specs():
    return [pl.BlockSpec((1, BW), lambda i: (0, 0)), pl.BlockSpec((1, BW), lambda i: (0, 0)),
            pl.BlockSpec((4, 128, 128), lambda i: (0, 0, 0)), pl.BlockSpec((128, BW), lambda i: (0, 0))]


def _sgu_fwd(name, proj, ln_g, ln_b, w, bias):
    T = proj.shape[0]
    rb = _pick(T, (256, 128))

    def body(u_ref, v_ref, g_ref, b_ref, w_ref, bias_ref, y_ref):
        for n in range(rb // 128):
            rows = slice(n * 128, (n + 1) * 128)
            y = _sgu_block(u_ref[rows, :], v_ref[rows, :], g_ref[...], b_ref[...], w_ref[...], bias_ref[...])
            y_ref[rows, :] = y.astype(BF16)

    return pl.pallas_call(
        body, name=name, grid=(T // rb,),
        in_specs=[pl.BlockSpec((rb, BW), lambda i: (i, COL_AU // BW)), pl.BlockSpec((rb, BW), lambda i: (i, COL_AV // BW))]
        + _sgu_param_specs(),
        out_specs=pl.BlockSpec((rb, BW), lambda i: (i, 0)),
        out_shape=jax.ShapeDtypeStruct((T, BW), BF16),
        compiler_params=_cp("parallel"),
    )(proj, proj, ln_g, ln_b, w, bias)


def _sgu_bwd(name, proj, dy, ln_g, ln_b, w, bias):
    T = proj.shape[0]
    rb = _pick(T, (256, 128))

    def body(u_ref, v_ref, dy_ref, g_ref, b_ref, w_ref, bias_ref, du_ref, dv_ref, dg_ref, db_ref, dw_ref, dbias_ref):
        acc = None
        for n in range(rb // 128):
            rows = slice(n * 128, (n + 1) * 128)
            _, vjp = jax.vjp(_sgu_block, u_ref[rows, :], v_ref[rows, :], g_ref[...], b_ref[...], w_ref[...],
                             bias_ref[...])
            du, dv, *dp = vjp(dy_ref[rows, :])
            du_ref[rows, :] = du.astype(BF16)
            dv_ref[rows, :] = dv.astype(BF16)
            acc = dp if acc is None else [p + q for p, q in zip(acc, dp)]

        @pl.when(pl.program_id(0) == 0)
        def _():
            for r, p in zip((dg_ref, db_ref, dw_ref, dbias_ref), acc):
                r[...] = p

        @pl.when(pl.program_id(0) > 0)
        def _():
            for r, p in zip((dg_ref, db_ref, dw_ref, dbias_ref), acc):
                r[...] += p

    row = pl.BlockSpec((rb, BW), lambda i: (i, 0))
    return pl.pallas_call(
        body, name=name, grid=(T // rb,),
        in_specs=[pl.BlockSpec((rb, BW), lambda i: (i, COL_AU // BW)), pl.BlockSpec((rb, BW), lambda i: (i, COL_AV // BW)),
                  row] + _sgu_param_specs(),
        out_specs=[row, row] + _sgu_param_specs(),
        out_shape=[jax.ShapeDtypeStruct((T, BW), BF16), jax.ShapeDtypeStruct((T, BW), BF16),
                   jax.ShapeDtypeStruct((1, BW), F32), jax.ShapeDtypeStruct((1, BW), F32),
                   jax.ShapeDtypeStruct((4, 128, 128), F32), jax.ShapeDtypeStruct((128, BW), F32)],
        compiler_params=_cp("arbitrary"),
    )(proj, proj, dy, ln_g, ln_b, w, bias)


def _halo_block(ref, i, rblk, halo):
    r0 = pl.multiple_of(i * rblk, rblk)
    h0 = pl.multiple_of(jnp.maximum(r0 - halo, 0), halo)
    top = jnp.where(i > 0, ref[pl.ds(h0, halo), :], 0.0)
    return jnp.concatenate([top, ref[pl.ds(r0, rblk), :]], axis=0)


def _with_halo_grad(dfull, pending, halo, rblk):
    tail = jnp.concatenate([jnp.zeros((rblk - halo, 128), F32), pending], axis=0)
    return dfull[halo:] + tail


def _conv4(xfull, rows):
    acc = None
    for k in range(4):
        t = rows[k] * _shift(xfull, 3 - k)[8:]
        acc = t if acc is None else acc + t
    return acc


def _lru_block(xfull, gate, h0, c0, c1, c2, c3, cb, wa, ba, wx, bx, lam):
    n = gate.shape[0]
    xc = _conv4(xfull, (c0, c1, c2, c3)) + cb
    r = _sigmoid(_dot(xc, wa) + ba)
    ig = _sigmoid(_dot(xc, wx) + bx)
    log_a = -LRU_C * r * _softplus(-lam)
    a = jnp.exp(log_a)
    mult = jnp.sqrt(-_expm1(2.0 * log_a))
    b = mult * (ig * xc)
    row = lax.broadcasted_iota(jnp.int32, (n, 128), 0)
    b = b + jnp.where(row == 0, a * h0, 0.0)
    h = _scan(a, b)
    out = h * _gelu(gate)
    h_last = jnp.sum(jnp.where(row == n - 1, h, 0.0), axis=0, keepdims=True)
    return out, h_last


def _lru_param_specs():
    vec = pl.BlockSpec((1, 128), lambda g: (0, g))
    mat = pl.BlockSpec((None, 128, 128), lambda g: (g, 0, 0))
    return [pl.BlockSpec((4, 128), lambda g: (0, g)), vec, mat, vec, mat, vec, vec]


def _lru_load_params(cw_ref, cb_ref, wa_ref, ba_ref, wx_ref, bx_ref, lam_ref):
    return (cw_ref[0:1, :], cw_ref[1:2, :], cw_ref[2:3, :], cw_ref[3:4, :], cb_ref[...], wa_ref[...], ba_ref[...],
            wx_ref[...], bx_ref[...], lam_ref[...])


def _lru_fwd(name, proj, cw, cb, wa, ba, wx, bx, lam):
    T = proj.shape[0]
    rblk = _pick(T, (256, 128))
    nblk = T // rblk

    def body(x_ref, gt_ref, cw_ref, cb_ref, wa_ref, ba_ref, wx_ref, bx_ref, lam_ref, y_ref, hc_ref):
        params = _lru_load_params(cw_ref, cb_ref, wa_ref, ba_ref, wx_ref, bx_ref, lam_ref)

        def step(i, h0):
            r0 = pl.multiple_of(i * rblk, rblk)
            out, h_last = _lru_block(_halo_block(x_ref, i, rblk, 8), gt_ref[pl.ds(r0, rblk), :], h0, *params)
            y_ref[pl.ds(r0, rblk), :] = out.astype(BF16)
            hc_ref[pl.ds(pl.multiple_of(i * 8, 8), 8), :] = jnp.broadcast_to(h0, (8, 128))
            return h_last

        lax.fori_loop(0, nblk, step, jnp.zeros((1, 128), F32))

    return pl.pallas_call(
        body, name=name, grid=(4,),
        in_specs=[pl.BlockSpec((T, 128), lambda g: (0, COL_BX // 128 + g)),
                  pl.BlockSpec((T, 128), lambda g: (0, COL_BG // 128 + g))] + _lru_param_specs(),
        out_specs=[pl.BlockSpec((T, 128), lambda g: (0, g)), pl.BlockSpec((nblk * 8, 128), lambda g: (0, g))],
        out_shape=[jax.ShapeDtypeStruct((T, BW), BF16), jax.ShapeDtypeStruct((nblk * 8, BW), F32)],
        compiler_params=_cp("parallel"),
    )(proj, proj, cw, cb, wa, ba, wx, bx, lam)


def _lru_bwd(name, proj, dy, hc, cw, cb, wa, ba, wx, bx, lam):
    T = proj.shape[0]
    rblk = _pick(T, (256, 128))
    nblk = T // rblk

    def body(x_ref, gt_ref, dy_ref, hc_ref, cw_ref, cb_ref, wa_ref, ba_ref, wx_ref, bx_ref, lam_ref,
             dx_ref, dgt_ref, dcw_ref, dcb_ref, dwa_ref, dba_ref, dwx_ref, dbx_ref, dlam_ref):
        params = _lru_load_params(cw_ref, cb_ref, wa_ref, ba_ref, wx_ref, bx_ref, lam_ref)

        def step(it, carry):
            dh_last, pending, acc = carry
            i = nblk - 1 - it
            r0 = pl.multiple_of(i * rblk, rblk)
            h0 = hc_ref[pl.ds(pl.multiple_of(i * 8, 8), 1), :]
            _, vjp = jax.vjp(_lru_block, _halo_block(x_ref, i, rblk, 8), gt_ref[pl.ds(r0, rblk), :], h0, *params)
            dfull, dgate, dh0, *dp = vjp((dy_ref[pl.ds(r0, rblk), :], dh_last))
            dx_ref[pl.ds(r0, rblk), :] = _with_halo_grad(dfull, pending, 8, rblk).astype(BF16)
            dgt_ref[pl.ds(r0, rblk), :] = dgate.astype(BF16)
            return dh0, dfull[:8], tuple(p + q for p, q in zip(acc, dp))

        zeros = tuple(jnp.zeros(p.shape, F32) for p in params)
        _, _, acc = lax.fori_loop(0, nblk, step, (jnp.zeros((1, 128), F32), jnp.zeros((8, 128), F32), zeros))
        for k in range(4):
            dcw_ref[k:k + 1, :] = acc[k]
        for r, p in zip((dcb_ref, dwa_ref, dba_ref, dwx_ref, dbx_ref, dlam_ref), acc[4:]):
            r[...] = p

    col = pl.BlockSpec((T, 128), lambda g: (0, g))
    return pl.pallas_call(
        body, name=name, grid=(4,),
        in_specs=[pl.BlockSpec((T, 128), lambda g: (0, COL_BX // 128 + g)),
                  pl.BlockSpec((T, 128), lambda g: (0, COL_BG // 128 + g)), col,
                  pl.BlockSpec((nblk * 8, 128), lambda g: (0, g))] + _lru_param_specs(),
        out_specs=[col, col] + _lru_param_specs(),
        out_shape=[jax.ShapeDtypeStruct((T, BW), BF16), jax.ShapeDtypeStruct((T, BW), BF16),
                   jax.ShapeDtypeStruct((4, BW), F32), jax.ShapeDtypeStruct((1, BW), F32),
                   jax.ShapeDtypeStruct((4, 128, 128), F32), jax.ShapeDtypeStruct((1, BW), F32),
                   jax.ShapeDtypeStruct((4, 128, 128), F32), jax.ShapeDtypeStruct((1, BW), F32),
                   jax.ShapeDtypeStruct((1, BW), F32)],
        compiler_params=_cp("parallel"),
    )(proj, proj, dy, hc, cw, cb, wa, ba, wx, bx, lam)


def _conv_block(xfull, c0, c1, c2, c3):
    return _silu(_conv4(xfull, (c0, c1, c2, c3)))


def _conv_fwd(name, proj, col0, cw, cw_col0):
    T = proj.shape[0]
    rblk = _pick(T, (256, 128))
    nblk = T // rblk

    def body(x_ref, cw_ref, y_ref):
        rows = (cw_ref[0:1, :], cw_ref[1:2, :], cw_ref[2:3, :], cw_ref[3:4, :])

        def step(i, c):
            r0 = pl.multiple_of(i * rblk, rblk)
            y_ref[pl.ds(r0, rblk), :] = _conv_block(_halo_block(x_ref, i, rblk, 8), *rows)
            return c

        lax.fori_loop(0, nblk, step, 0)

    return pl.pallas_call(
        body, name=name, grid=(4,),
        in_specs=[pl.BlockSpec((T, 128), lambda g: (0, col0 // 128 + g)),
                  pl.BlockSpec((4, 128), lambda g: (0, cw_col0 // 128 + g))],
        out_specs=pl.BlockSpec((T, 128), lambda g: (0, g)),
        out_shape=jax.ShapeDtypeStruct((T, BW), F32),
        compiler_params=_cp("parallel"),
    )(proj, cw)


def _conv_bwd(name, proj, col0, dy, cw, cw_col0):
    T = proj.shape[0]
    rblk = _pick(T, (256, 128))
    nblk = T // rblk

    def body(x_ref, dy_ref, cw_ref, dx_ref, dcw_ref):
        rows = (cw_ref[0:1, :], cw_ref[1:2, :], cw_ref[2:3, :], cw_ref[3:4, :])

        def step(it, carry):
            pending, acc = carry
            i = nblk - 1 - it
            r0 = pl.multiple_of(i * rblk, rblk)
            _, vjp = jax.vjp(_conv_block, _halo_block(x_ref, i, rblk, 8), *rows)
            dfull, *dp = vjp(dy_ref[pl.ds(r0, rblk), :])
            dx_ref[pl.ds(r0, rblk), :] = _with_halo_grad(dfull, pending, 8, rblk).astype(BF16)
            return dfull[:8], tuple(p + q for p, q in zip(acc, dp))

        zeros = tuple(jnp.zeros((1, 128), F32) for _ in range(4))
        _, acc = lax.fori_loop(0, nblk, step, (jnp.zeros((8, 128), F32), zeros))
        for k in range(4):
            dcw_ref[k:k + 1, :] = acc[k]

    col = pl.BlockSpec((T, 128), lambda g: (0, g))
    return pl.pallas_call(
        body, name=name, grid=(4,),
        in_specs=[pl.BlockSpec((T, 128), lambda g: (0, col0 // 128 + g)), col,
                  pl.BlockSpec((4, 128), lambda g: (0, cw_col0 // 128 + g))],
        out_specs=[col, pl.BlockSpec((4, 128), lambda g: (0, g))],
        out_shape=[jax.ShapeDtypeStruct((T, BW), BF16), jax.ShapeDtypeStruct((4, BW), F32)],
        compiler_params=_cp("parallel"),
    )(proj, dy, cw)


def _pool_block(xfull, pw, sc, t0, gi):
    n = xfull.shape[0] - 16
    s2 = xfull + _shift(xfull, 1)
    s4 = s2 + _shift(s2, 2)
    s8 = s4 + _shift(s4, 4)
    s16 = s8 + _shift(s8, 8)
    s = jnp.where(gi == 0, s2, jnp.where(gi == 1, s4, jnp.where(gi == 2, s8, s16)))[16:]
    t = t0 + lax.broadcasted_iota(jnp.int32, (n, 128), 0)
    cnt = jnp.minimum(t + 1, lax.shift_left(jnp.int32(2), gi)).astype(F32)
    pooled = s / cnt - xfull[16:]
    return _dot(pooled, pw) * sc


def _pool_fwd(name, proj, pw, sc):
    T = proj.shape[0]
    rblk = _pick(T, (256, 128))
    nblk = T // rblk

    def body(x_ref, pw_ref, sc_ref, y_ref):
        gi = pl.program_id(0)

        def step(i, c):
            r0 = pl.multiple_of(i * rblk, rblk)
            y = _pool_block(_halo_block(x_ref, i, rblk, 16), pw_ref[...], sc_ref[...], r0, gi)
            y_ref[pl.ds(r0, rblk), :] = y.astype(BF16)
            return c

        lax.fori_loop(0, nblk, step, 0)

    return pl.pallas_call(
        body, name=name, grid=(4,),
        in_specs=[pl.BlockSpec((T, 128), lambda g: (0, COL_DX // 128 + g)),
                  pl.BlockSpec((None, 128, 128), lambda g: (g, 0, 0)), pl.BlockSpec((1, 128), lambda g: (0, g))],
        out_specs=pl.BlockSpec((T, 128), lambda g: (0, g)),
        out_shape=jax.ShapeDtypeStruct((T, BW), BF16),
        compiler_params=_cp("parallel"),
    )(proj, pw, sc)


def _pool_bwd(name, proj, dy, pw, sc):
    T = proj.shape[0]
    rblk = _pick(T, (256, 128))
    nblk = T // rblk

    def body(x_ref, dy_ref, pw_ref, sc_ref, dx_ref, dpw_ref, dsc_ref):
        gi = pl.program_id(0)

        def step(it, carry):
            pending, apw, asc = carry
            i = nblk - 1 - it
            r0 = pl.multiple_of(i * rblk, rblk)
            _, vjp = jax.vjp(lambda xf, w, s: _pool_block(xf, w, s, r0, gi), _halo_block(x_ref, i, rblk, 16),
                             pw_ref[...], sc_ref[...])
            dfull, dw, ds = vjp(dy_ref[pl.ds(r0, rblk), :])
            dx_ref[pl.ds(r0, rblk), :] = _with_halo_grad(dfull, pending, 16, rblk).astype(BF16)
            return dfull[:16], apw + dw, asc + ds

        _, apw, asc = lax.fori_loop(0, nblk, step, (jnp.zeros((16, 128), F32), jnp.zeros((128, 128), F32),
                                                    jnp.zeros((1, 128), F32)))
        dpw_ref[...] = apw
        dsc_ref[...] = asc

    col = pl.BlockSpec((T, 128), lambda g: (0, g))
    mat = pl.BlockSpec((None, 128, 128), lambda g: (g, 0, 0))
    vec = pl.BlockSpec((1, 128), lambda g: (0, g))
    return pl.pallas_call(
        body, name=name, grid=(4,),
        in_specs=[pl.BlockSpec((T, 128), lambda g: (0, COL_DX // 128 + g)), col, mat, vec],
        out_specs=[col, mat, vec],
        out_shape=[jax.ShapeDtypeStruct((T, BW), BF16), jax.ShapeDtypeStruct((4, 128, 128), F32),
                   jax.ShapeDtypeStruct((1, BW), F32)],
        compiler_params=_cp("parallel"),
    )(proj, dy, pw, sc)


@jax.custom_vjp
def _dot3(a, b):
    ah = a.astype(BF16)
    al = (a - ah.astype(F32)).astype(BF16)
    bh = b.astype(BF16)
    bl = (b - bh.astype(F32)).astype(BF16)

    def d(x, y):
        return lax.dot_general(x, y, _NN, preferred_element_type=F32)

    return d(ah, bh) + (d(ah, bl) + d(al, bh))


def _dot3_fwd(a, b):
    return _dot3(a, b), (a, b)


def _dot3_bwd(res, g):
    a, b = res
    return _dot(g, b, _NT), _dot(a, g, _TN)


_dot3.defvjp(_dot3_fwd, _dot3_bwd)


def _pad_rows2(x):
    return jnp.concatenate([x, jnp.zeros_like(x)], axis=0)


def _tri_inv(mats):
    n = mats[0].shape[0]
    eye = (lax.broadcasted_iota(jnp.int32, (n, n), 0) == lax.broadcasted_iota(jnp.int32, (n, n), 1)).astype(F32)
    ps = [eye - a for a in mats]
    ms = list(mats)
    k = 2
    while k < n:
        ms = [_dot3(t, t) for t in ms]
        ps = [p + _dot3(p, t) for p, t in zip(ps, ms)]
        k *= 2
    return ps


def _cumsum_rows(x):
    n = x.shape[0]
    row = lax.broadcasted_iota(jnp.int32, x.shape, 0)
    k = 1
    while k < n:
        x = x + jnp.where(row >= k, _shift(x, k), 0.0)
        k *= 2
    return x


def _gdn_prep(qcs, kcs, vcs, tails, alog, dtb):
    C = CHUNK
    pairs = [(c, h) for c in range(len(qcs)) for h in range(4)]
    lane = lax.broadcasted_iota(jnp.int32, (C, 128), 1)
    row = lax.broadcasted_iota(jnp.int32, (C, 128), 0)
    incl = row >= lane
    sig = [_sigmoid(t) for t in tails]
    gfull = [-jnp.exp(alog) * _softplus(t + dtb) for t in tails]
    beta = [jnp.sum(jnp.where(lane == h, sig[c], 0.0), axis=1, keepdims=True) for c, h in pairs]
    g = [jnp.sum(jnp.where(lane == h + 4, gfull[c], 0.0), axis=1, keepdims=True) for c, h in pairs]
    qs = [qcs[c][:, h * 128:(h + 1) * 128] for c, h in pairs]
    ks = [kcs[c][:, h * 128:(h + 1) * 128] for c, h in pairs]
    vs = [vcs[c][:, h * 128:(h + 1) * 128] for c, h in pairs]
    q = [t * lax.rsqrt(jnp.sum(t * t, axis=-1, keepdims=True) + EPS) * (GDN_DK ** -0.5) for t in qs]
    k = [t * lax.rsqrt(jnp.sum(t * t, axis=-1, keepdims=True) + EPS) for t in ks]
    gc = [_cumsum_rows(jnp.broadcast_to(t, (C, 128))) for t in g]
    gc_t = [jnp.transpose(jnp.concatenate([t, t], axis=0)) for t in gc]
    gc_col = [jnp.sum(jnp.where(lane == 0, t, 0.0), axis=1, keepdims=True) for t in gc]
    ri = lax.broadcasted_iota(jnp.int32, (C, C), 0)
    ci = lax.broadcasted_iota(jnp.int32, (C, C), 1)
    decay = [jnp.exp(jnp.where(incl, a - b[:C, :], -1e30)) for a, b in zip(gc, gc_t)]
    decay_sq = [jnp.exp(jnp.where(ri > ci, a - jnp.transpose(b)[:C, :], -1e30)) for a, b in zip(gc_col, gc)]
    kb = [a * b for a, b in zip(k, beta)]
    kk = [_dot(a, b, _NT) for a, b in zip(kb, k)]
    t_mat = _tri_inv([jnp.where(ri > ci, a * b, 0.0) for a, b in zip(kk, decay_sq)])
    egc = [jnp.exp(t) for t in gc]
    u = [_dot(t, a * b) for t, a, b in zip(t_mat, vs, beta)]
    w = [_dot(t, a * b) for t, a, b in zip(t_mat, kb, egc)]
    qk = [_dot(a, _pad_rows2(b), _NT) for a, b in zip(q, k)]
    attn = [jnp.where(incl, a * b, 0.0) for a, b in zip(qk, decay)]
    g_last = [jnp.sum(jnp.where(row == C - 1, t, 0.0), axis=0, keepdims=True) for t in gc]
    qe = [a * b for a, b in zip(q, egc)]
    kd = [a * jnp.exp(b - c_) for a, b, c_ in zip(k, g_last, gc)]
    egl = [jnp.exp(t) for t in g_last]

    def per_chunk(vals):
        return [jnp.concatenate(vals[4 * c:4 * c + 4], axis=1) for c in range(len(qcs))]

    return tuple(per_chunk(t) for t in (u, w, qe, kd, attn, egl))


def _gdn_scan_chunk(states, u, w, qe, kd, attn, egl, z, ng):
    hs = range(4)

    def sl(t, h):
        return t[:, h * 128:(h + 1) * 128]

    ws = [_dot(sl(w, h), states[h]) for h in hs]
    qs = [_dot(sl(qe, h), states[h]) for h in hs]
    v_new = [sl(u, h) - ws[h] for h in hs]
    av = [_dot(sl(attn, h), _pad_rows2(v_new[h])) for h in hs]
    kv = [_dot(sl(kd, h), v_new[h], _TN) for h in hs]
    nxt = tuple(states[h] * sl(egl, h) + kv[h] for h in hs)
    o = [qs[h] + av[h] for h in hs]
    on = [t * lax.rsqrt(jnp.mean(t * t, axis=-1, keepdims=True) + EPS) * ng for t in o]
    return nxt, jnp.concatenate(on, axis=1) * _silu(z)


def _gdn_blocks(T):
    tb = _pick(T, (512, 256, 128, 64))
    return tb, T // tb, tb // CHUNK


PREP_CHUNKS = 2


def _chunk_rows(i, n):
    return [pl.ds(pl.multiple_of((i * n + j) * CHUNK, CHUNK), CHUNK) for j in range(n)]


def _egl_rows(i, n, size):
    return [pl.ds(pl.multiple_of((i * n + j) * 8, 8), size) for j in range(n)]


def _gdn_prep_fwd(name, qa, ka, va, proj, alog, dtb):
    T = proj.shape[0]
    tb, nb, ncb = _gdn_blocks(T)
    n = PREP_CHUNKS if ncb % PREP_CHUNKS == 0 else 1

    def body(q_ref, k_ref, v_ref, tail_ref, alog_ref, dtb_ref, u_ref, w_ref, qe_ref, kd_ref, at_ref, egl_ref):
        def step(i, c):
            rows = _chunk_rows(i, n)
            u, w, qe, kd, at, egl = _gdn_prep([q_ref[r, :] for r in rows], [k_ref[r, :] for r in rows],
                                              [v_ref[r, :] for r in rows], [tail_ref[r, :] for r in rows],
                                              alog_ref[...], dtb_ref[...])
            for j, (r, e) in enumerate(zip(rows, _egl_rows(i, n, 8))):
                u_ref[r, :] = u[j]
                w_ref[r, :] = w[j].astype(BF16)
                qe_ref[r, :] = qe[j].astype(BF16)
                kd_ref[r, :] = kd[j].astype(BF16)
                at_ref[r, :] = at[j].astype(BF16)
                egl_ref[e, :] = jnp.broadcast_to(egl[j], (8, BW))
            return c

        lax.fori_loop(0, ncb // n, step, 0)

    blk = pl.BlockSpec((tb, BW), lambda j: (j, 0))
    vec = pl.BlockSpec((1, 128), lambda j: (0, 0))
    return pl.pallas_call(
        body, name=name, grid=(nb,),
        in_specs=[blk, blk, blk, pl.BlockSpec((tb, 128), lambda j: (j, COL_TAIL // 128)), vec, vec],
        out_specs=[blk] * 5 + [pl.BlockSpec((ncb * 8, BW), lambda j: (j, 0))],
        out_shape=[jax.ShapeDtypeStruct((T, BW), F32)] + [jax.ShapeDtypeStruct((T, BW), BF16)] * 4
        + [jax.ShapeDtypeStruct((T // 8, BW), F32)],
        compiler_params=_cp("parallel"),
    )(qa, ka, va, proj, alog, dtb)


def _gdn_prep_bwd(name, qa, ka, va, proj, alog, dtb, du, dw, dqe, dkd, dat, degl):
    T = proj.shape[0]
    tb, nb, ncb = _gdn_blocks(T)
    n = PREP_CHUNKS if ncb % PREP_CHUNKS == 0 else 1

    def body(q_ref, k_ref, v_ref, tail_ref, alog_ref, dtb_ref, du_ref, dw_ref, dqe_ref, dkd_ref, dat_ref, degl_ref,
             dq_ref, dk_ref, dv_ref, dtail_ref, dalog_ref, ddtb_ref):
        first = pl.program_id(0) == 0

        def step(i, carry):
            pa, pd = carry
            rows = _chunk_rows(i, n)
            _, vjp = jax.vjp(_gdn_prep, [q_ref[r, :] for r in rows], [k_ref[r, :] for r in rows],
                             [v_ref[r, :] for r in rows], [tail_ref[r, :] for r in rows], alog_ref[...], dtb_ref[...])
            cot = tuple([ref[r, :] for r in rows] for ref in (du_ref, dw_ref, dqe_ref, dkd_ref, dat_ref))
            dq, dk, dv, dtail, da, dd = vjp(cot + ([degl_ref[e, :] for e in _egl_rows(i, n, 1)],))
            for j, r in enumerate(rows):
                dq_ref[r, :] = dq[j]
                dk_ref[r, :] = dk[j]
                dv_ref[r, :] = dv[j]
                dtail_ref[r, :] = dtail[j].astype(BF16)
            return pa + da, pd + dd

        zv = jnp.zeros((1, 128), F32)
        pa, pd = lax.fori_loop(0, ncb // n, step, (zv, zv))

        @pl.when(first)
        def _():
            dalog_ref[...] = pa
            ddtb_ref[...] = pd

        @pl.when(jnp.logical_not(first))
        def _():
            dalog_ref[...] += pa
            ddtb_ref[...] += pd

    blk = pl.BlockSpec((tb, BW), lambda j: (j, 0))
    vec = pl.BlockSpec((1, 128), lambda j: (0, 0))
    return pl.pallas_call(
        body, name=name, grid=(nb,),
        in_specs=[blk, blk, blk, pl.BlockSpec((tb, 128), lambda j: (j, COL_TAIL // 128)), vec, vec]
        + [blk] * 5 + [pl.BlockSpec((ncb * 8, BW), lambda j: (j, 0))],
        out_specs=[blk, blk, blk, pl.BlockSpec((tb, 128), lambda j: (j, 0)), vec, vec],
        out_shape=[jax.ShapeDtypeStruct((T, BW), F32)] * 3 + [jax.ShapeDtypeStruct((T, 128), BF16)]
        + [jax.ShapeDtypeStruct((1, 128), F32)] * 2,
        compiler_params=_cp("arbitrary"),
    )(qa, ka, va, proj, alog, dtb, du, dw, dqe, dkd, dat, degl)


def _gdn_fwd(name, u, w, qe, kd, at, egl, proj, ng):
    T = proj.shape[0]
    tb, nb, ncb = _gdn_blocks(T)

    def body(u_ref, w_ref, qe_ref, kd_ref, at_ref, egl_ref, z_ref, ng_ref, y_ref, sh_ref, state):
        @pl.when(pl.program_id(0) == 0)
        def _():
            state[...] = jnp.zeros((4, 128, 128), F32)

        def step(c, states):
            rows = pl.ds(pl.multiple_of(c * CHUNK, CHUNK), CHUNK)
            for h in range(4):
                sh_ref[h, c] = states[h]
            nxt, y = _gdn_scan_chunk(states, u_ref[rows, :], w_ref[rows, :], qe_ref[rows, :], kd_ref[rows, :],
                                     at_ref[rows, :], egl_ref[pl.ds(pl.multiple_of(c * 8, 8), 1), :], z_ref[rows, :],
                                     ng_ref[...])
            y_ref[rows, :] = y.astype(BF16)
            return nxt

        states = lax.fori_loop(0, ncb, step, tuple(state[h] for h in range(4)))
        for h in range(4):
            state[h] = states[h]

    blk = pl.BlockSpec((tb, BW), lambda j: (j, 0))
    vec = pl.BlockSpec((1, 128), lambda j: (0, 0))
    return pl.pallas_call(
        body, name=name, grid=(nb,),
        in_specs=[blk] * 5 + [pl.BlockSpec((ncb * 8, BW), lambda j: (j, 0)),
                              pl.BlockSpec((tb, BW), lambda j: (j, COL_CZ // BW)), vec],
        out_specs=[blk, pl.BlockSpec((4, ncb, 128, 128), lambda j: (0, j, 0, 0))],
        out_shape=[jax.ShapeDtypeStruct((T, BW), BF16), jax.ShapeDtypeStruct((4, T // CHUNK, 128, 128), F32)],
        scratch_shapes=[pltpu.VMEM((4, 128, 128), F32)],
        compiler_params=_cp("arbitrary"),
    )(u, w, qe, kd, at, egl, proj, ng)


def _gdn_bwd(name, u, w, qe, kd, at, egl, proj, dy, sh, ng):
    T = proj.shape[0]
    tb, nb, ncb = _gdn_blocks(T)

    def body(u_ref, w_ref, qe_ref, kd_ref, at_ref, egl_ref, z_ref, dy_ref, sh_ref, ng_ref,
             du_ref, dw_ref, dqe_ref, dkd_ref, dat_ref, degl_ref, dz_ref, dng_ref, dstate):
        first = pl.program_id(0) == 0

        @pl.when(first)
        def _():
            dstate[...] = jnp.zeros((4, 128, 128), F32)

        def step(it, carry):
            dstates, pn = carry
            c = ncb - 1 - it
            rows = pl.ds(pl.multiple_of(c * CHUNK, CHUNK), CHUNK)
            erow = pl.multiple_of(c * 8, 8)
            _, vjp = jax.vjp(_gdn_scan_chunk, tuple(sh_ref[h, c] for h in range(4)), u_ref[rows, :],
                             w_ref[rows, :].astype(F32), qe_ref[rows, :].astype(F32), kd_ref[rows, :].astype(F32),
                             at_ref[rows, :].astype(F32), egl_ref[pl.ds(erow, 1), :], z_ref[rows, :], ng_ref[...])
            nxt, du, dw, dqe, dkd, dat, degl, dz, dn = vjp((dstates, dy_ref[rows, :]))
            du_ref[rows, :] = du
            dw_ref[rows, :] = dw
            dqe_ref[rows, :] = dqe
            dkd_ref[rows, :] = dkd
            dat_ref[rows, :] = dat
            degl_ref[pl.ds(erow, 8), :] = jnp.broadcast_to(degl, (8, BW))
            dz_ref[rows, :] = dz.astype(BF16)
            return nxt, pn + dn

        dstates, pn = lax.fori_loop(0, ncb, step, (tuple(dstate[h] for h in range(4)), jnp.zeros((1, 128), F32)))
        for h in range(4):
            dstate[h] = dstates[h]

        @pl.when(first)
        def _():
            dng_ref[...] = pn

        @pl.when(jnp.logical_not(first))
        def _():
            dng_ref[...] += pn

    blk = pl.BlockSpec((tb, BW), lambda j: (nb - 1 - j, 0))
    eblk = pl.BlockSpec((ncb * 8, BW), lambda j: (nb - 1 - j, 0))
    vec = pl.BlockSpec((1, 128), lambda j: (0, 0))
    return pl.pallas_call(
        body, name=name, grid=(nb,),
        in_specs=[blk] * 5 + [eblk, pl.BlockSpec((tb, BW), lambda j: (nb - 1 - j, COL_CZ // BW)), blk,
                              pl.BlockSpec((4, ncb, 128, 128), lambda j: (0, nb - 1 - j, 0, 0)), vec],
        out_specs=[blk] * 5 + [eblk, blk, vec],
        out_shape=[jax.ShapeDtypeStruct((T, BW), F32)] * 5 + [jax.ShapeDtypeStruct((T // 8, BW), F32),
                                                              jax.ShapeDtypeStruct((T, BW), BF16),
                                                              jax.ShapeDtypeStruct((1, 128), F32)],
        scratch_shapes=[pltpu.VMEM((4, 128, 128), F32)],
        compiler_params=_cp("arbitrary"),
    )(u, w, qe, kd, at, egl, proj, dy, sh, ng)


def _adamw_update(w_ref, g_ref, m_ref, v_ref, d_ref, nm_ref, nv_ref):
    gv = g_ref[...]
    m2 = ADAM_B1 * m_ref[...] + (1.0 - ADAM_B1) * gv
    v2 = ADAM_B2 * v_ref[...] + (1.0 - ADAM_B2) * jnp.square(gv)
    m_hat = m2 / (1.0 - ADAM_B1 ** ADAM_STEP)
    v_hat = v2 / (1.0 - ADAM_B2 ** ADAM_STEP)
    d_ref[...] = -ADAM_LR * (m_hat / (jnp.sqrt(v_hat) + ADAM_EPS) + ADAM_WD * w_ref[...])
    nm_ref[...] = m2
    nv_ref[...] = v2


def _adamw_many(name, ws, gs, ms, vs):
    n = len(ws)

    def body(*refs):
        for i in range(n):
            _adamw_update(*[refs[k * n + i] for k in range(7)])

    return pl.pallas_call(
        body, name=name,
        out_shape=[jax.ShapeDtypeStruct(a.shape, F32) for a in ws] * 3,
        compiler_params=_cp(),
    )(*ws, *gs, *ms, *vs)


def _adamw(name, w, g, m, v):
    R, C = w.shape
    br = _pick(R, (512, 256, 240, 128, 64, 8))
    body = functools.partial(_adamw_update)
    spec = pl.BlockSpec((br, C), lambda i: (i, 0))
    return pl.pallas_call(
        body, name=name, grid=(R // br,),
        in_specs=[spec] * 4, out_specs=[spec] * 3,
        out_shape=[jax.ShapeDtypeStruct((R, C), F32)] * 3,
        compiler_params=_cp("parallel"),
    )(w, g, m, v)


def _sum8(name, parts):
    _, R, C = parts.shape
    br = _pick(R, (352, 368, 256, 128, 64, 16, 8))

    def body(p_ref, o_ref):
        acc = p_ref[0].astype(F32)
        for d in range(1, N_DEV):
            acc = acc + p_ref[d].astype(F32)
        o_ref[...] = acc

    return pl.pallas_call(
        body, name=name, grid=(R // br,),
        in_specs=[pl.BlockSpec((N_DEV, br, C), lambda i: (0, i, 0))],
        out_specs=pl.BlockSpec((br, C), lambda i: (i, 0)),
        out_shape=jax.ShapeDtypeStruct((R, C), F32),
        compiler_params=_cp("parallel"),
    )(parts)


_ANY = pl.BlockSpec(memory_space=pl.ANY)
_MESH = pl.DeviceIdType.MESH


def _all_gather(name, shard):
    R, C = shard.shape

    def body(x_ref, out_ref, send_sems, recv_sems, local_sem):
        x, y, c = lax.axis_index("x"), lax.axis_index("y"), lax.axis_index("c")
        me, sibling = (x, y, c), (x, y, 1 - c)
        chips = [(1 - x, y), (x, 1 - y), (1 - x, 1 - y)]

        def slot(px, py, pc):
            return out_ref.at[4 * px + 2 * py + pc]

        def copy(k, block, to, src=None):
            return pltpu.make_async_remote_copy(
                src_ref=slot(*block) if src is None else src, dst_ref=slot(*block),
                send_sem=send_sems.at[k], recv_sem=recv_sems.at[k], device_id=to, device_id_type=_MESH)

        mine = pltpu.make_async_copy(x_ref, slot(*me), local_sem)
        mine.start()
        first = [copy(0, me, sibling, src=x_ref)]
        first += [copy(1 + j, me, (*chip, c), src=x_ref) for j, chip in enumerate(chips)]
        for cp in first:
            cp.start()
        passed = [copy(4 + j, (*chip, c), sibling) for j, chip in enumerate(chips)]
        for j, chip in enumerate(chips):
            copy(1 + j, (*chip, c), me).wait_recv()
            passed[j].start()
        copy(0, sibling, me).wait_recv()
        for j, chip in enumerate(chips):
            copy(4 + j, (*chip, 1 - c), me).wait_recv()
        for cp in first + passed:
            cp.wait_send()
        mine.wait()

    return pl.pallas_call(
        body, name=name,
        in_specs=[_ANY], out_specs=_ANY,
        out_shape=jax.ShapeDtypeStruct((N_DEV, R, C), shard.dtype),
        scratch_shapes=[pltpu.SemaphoreType.DMA((7,)), pltpu.SemaphoreType.DMA((7,)), pltpu.SemaphoreType.DMA],
    )(shard)


_HBM = pl.BlockSpec(memory_space=pltpu.HBM)
_SEM = pl.BlockSpec(memory_space=pltpu.SEMAPHORE)
_EFFECT = pltpu.SideEffectType.DATAFLOW_SIDE_EFFECTING


def _exchange_copies(src_ref, land_ref, send_sems, recv_sems, scatter):
    x, y, c = lax.axis_index("x"), lax.axis_index("y"), lax.axis_index("c")
    me = 4 * x + 2 * y + c
    copies = []
    for k in range(1, N_DEV):
        px, py, pc = x ^ ((k >> 2) & 1), y ^ ((k >> 1) & 1), c ^ (k & 1)
        src = src_ref.at[4 * px + 2 * py + pc] if scatter else src_ref
        copies.append(pltpu.make_async_remote_copy(
            src_ref=src, dst_ref=land_ref.at[me], send_sem=send_sems.at[k - 1], recv_sem=recv_sems.at[k - 1],
            device_id=(px, py, pc), device_id_type=_MESH))
    return copies


def _exchange_start(name, srcs, lands, scatter, after=None):
    n = len(srcs)

    def body(*refs):
        src_refs, land_refs = refs[:n], refs[n:2 * n]
        outs = refs[2 * n + (after is not None):]
        send, recv = outs[:n], outs[n:2 * n]
        token = refs[-1]
        for g in range(n):
            for cp in _exchange_copies(src_refs[g], land_refs[g], send[g], recv[g], scatter):
                cp.start()
        token[...] = jnp.zeros_like(token)

    outs = pl.pallas_call(
        body, name=name,
        out_shape=tuple([pltpu.SemaphoreType.DMA((N_DEV - 1,))] * (2 * n)
                        + [pltpu.HBM(a.shape, a.dtype) for a in list(srcs) + list(lands)]
                        + [jax.ShapeDtypeStruct((8, 128), F32)]),
        in_specs=[_HBM] * (2 * n) + [_ANY] * (after is not None),
        out_specs=tuple([_SEM] * (2 * n) + [_HBM] * (2 * n) + [pl.BlockSpec(memory_space=pltpu.VMEM)]),
        input_output_aliases={i: 2 * n + i for i in range(2 * n)},
        compiler_params=pltpu.CompilerParams(has_side_effects=_EFFECT),
    )(*[pltpu.with_memory_space_constraint(a, pltpu.HBM) for a in list(srcs) + list(lands)],
      *([after] if after is not None else []))
    handles = [(outs[2 * n + g], outs[3 * n + g], outs[g], outs[n + g]) for g in range(n)]
    return handles, outs[-1]


def _exchange_wait(name, handles, after, scatter):
    n = len(handles)
    srcs, lands, sends, recvs = ([h[i] for h in handles] for i in range(4))

    def body(*refs):
        src_refs, land_refs = refs[:n], refs[n:2 * n]
        send, recv = refs[2 * n:3 * n], refs[3 * n:4 * n]
        for g in range(n):
            for cp in _exchange_copies(src_refs[g], land_refs[g], send[g], recv[g], scatter):
                cp.wait_send()
                cp.wait_recv()

    outs = pl.pallas_call(
        body, name=name,
        out_shape=tuple(pltpu.HBM(a.shape, a.dtype) for a in srcs + lands),
        in_specs=tuple([_HBM] * (2 * n) + [_SEM] * (2 * n) + [_ANY]), out_specs=tuple([_HBM] * (2 * n)),
        input_output_aliases={i: i for i in range(2 * n)},
        compiler_params=pltpu.CompilerParams(has_side_effects=_EFFECT),
    )(*srcs, *lands, *sends, *recvs, after)
    return list(outs[n:])


def _rows(a):
    return a.reshape(-1, 1024)


def _rows_to_parts(full):
    n = full.shape[-2] // N_DEV
    t = full.reshape(full.shape[:-2] + (N_DEV, n, full.shape[-1]))
    return jnp.moveaxis(t, -3, 0)


def _parts_to_rows(parts):
    t = jnp.moveaxis(parts, 0, -3)
    return t.reshape(t.shape[:-3] + (t.shape[-3] * t.shape[-2], t.shape[-1]))


def _parts_to_cols(parts):
    t = jnp.moveaxis(parts, 0, -2)
    return t.reshape(t.shape[:-2] + (t.shape[-2] * t.shape[-1],))


def _join(parts, axis=0):
    total = sum(p.shape[axis] for p in parts)
    out, off = None, 0
    for p in parts:
        cfg = [(0, 0)] * p.ndim
        cfg[axis] = (off, total - off - p.shape[axis])
        t = jnp.pad(p, cfg)
        out = t if out is None else out + t
        off += p.shape[axis]
    return out


def _w_in_to_layout(w):
    tail = jnp.pad(w[4096:4104], ((0, PW - COL_TAIL - 8), (0, 0)))
    return jnp.concatenate([w[:4096], w[4104:P_IN], tail], axis=0)


def _w_in_from_layout(g):
    return _join([g[:4096], g[COL_TAIL:COL_TAIL + 8], g[4096:COL_TAIL]], axis=0)


def _block_diag(w):
    w = w.reshape(4, 2, 64, 64)
    return jnp.pad(w[:, 0], ((0, 0), (0, 64), (0, 64))) + jnp.pad(w[:, 1], ((0, 0), (64, 0), (64, 0)))


def _block_diag_grad(g):
    return jnp.stack([g[:, :64, :64], g[:, 64:, 64:]], axis=1).reshape(8, 64, 64)


def _ffn_forward(tag, x, norm, wg, wu, wd):
    h = _rms_fwd(tag + "_norm", x, norm)
    a, b, act = _ffn_up(tag + "_up", h, wg, wu)
    x_out = _mm(tag + "_down", [(act, wd)], "nn", F32, res=x, scale=0.5)
    return x_out, (x, h, a, b, act)


def _ffn_backward(tag, dx_out, saved, norm, wg, wu, wd, put):
    x, h, a, b, act = saved
    da, db = _ffn_dact(tag + "_dact", dx_out, wd, a, b)
    dwd = _mm(tag + "_dwd", [(act, dx_out)], "tn", BF16, scale=0.5, bm=FF // 2)
    dwg = _mm(tag + "_dwg", [(da, h)], "tn", BF16, bm=FF // 2)
    dwu = _mm(tag + "_dwu", [(db, h)], "tn", BF16, bm=FF // 2)
    tok = put(dwg, dwu, dwd)
    dh = _mm(tag + "_dh", [(da, wg), (db, wu)], "nn", F32)
    dx, dnorm = _rms_bwd(tag + "_dnorm", x, norm + tok, dh, dx_out)
    return dx, dnorm


def _mixer_params(p):
    alog = jnp.pad(p["gdn_a_log"], (4, 120))[None]
    dtb = jnp.pad(p["gdn_dt_bias"], (4, 120))[None]
    bias = jnp.repeat(p["sgu_b"].T, 128, axis=1)
    return dict(
        ln_g=p["sgu_ln_g"][None], ln_b=p["sgu_ln_b"][None], sgu_w=p["sgu_w"], sgu_bias=bias,
        lru_cw=p["lru_conv_w"], lru_cb=p["lru_conv_b"][None], wa=_block_diag(p["lru_wa"]), ba=p["lru_ba"][None],
        wx=_block_diag(p["lru_wx"]), bx=p["lru_bx"][None], lam=p["lru_lambda"][None],
        gdn_cw=p["gdn_conv_w"], alog=alog, dtb=dtb, ng=p["gdn_norm_g"][None],
        pool_w=p["pool_w"], pool_sc=p["pool_scale"][None])


def _mix_forward(tag, x, p, mp):
    h = _rms_fwd(tag + "_norm", x, p["mix_norm"][None])
    proj = _mm(tag + "_proj", [(h, p["w_in"])], "nt", F32, bm=_pick(x.shape[0], (2048, 1024, 512, 256, 128)))
    y_a = _sgu_fwd(tag + "_sgu", proj, mp["ln_g"], mp["ln_b"], mp["sgu_w"], mp["sgu_bias"])
    y_b, hc = _lru_fwd(tag + "_lru", proj, mp["lru_cw"], mp["lru_cb"], mp["wa"], mp["ba"], mp["wx"], mp["bx"],
                       mp["lam"])
    qa = _conv_fwd(tag + "_convq", proj, COL_CQ, mp["gdn_cw"], 0)
    ka = _conv_fwd(tag + "_convk", proj, COL_CK, mp["gdn_cw"], 512)
    va = _conv_fwd(tag + "_convv", proj, COL_CV, mp["gdn_cw"], 1024)
    prep = _gdn_prep_fwd(tag + "_gdnprep", qa, ka, va, proj, mp["alog"], mp["dtb"])
    y_c, sh = _gdn_fwd(tag + "_gdn", *prep, proj, mp["ng"])
    y_d = _pool_fwd(tag + "_pool", proj, mp["pool_w"], mp["pool_sc"])
    ys = (y_a, y_b, y_c, y_d)
    merged = _merge_fwd(tag + "_merge", ys, p["w_branch"], proj)
    x_out = _mm(tag + "_out", [(merged, p["w_out"])], "nn", F32, res=x)
    return x_out, (x, h, proj, hc, qa, ka, va, prep, sh, ys, merged)


def _mix_backward(tag, dx_out, saved, p, mp, put):
    x, h, proj, hc, qa, ka, va, prep, sh, ys, merged = saved
    T = x.shape[0]
    g = {}
    dmerged = _mm(tag + "_dmerged", [(dx_out, p["w_out"])], "nt", F32)
    g["w_out"] = _mm(tag + "_dwout", [(merged, dx_out)], "tn", BF16)
    outs = _merge_bwd(tag + "_dmerge", dmerged, ys, p["w_branch"], proj)
    dgates, dbrs = outs[:NBR], outs[NBR:]
    dys = [_mm(f"{tag}_dy{i}", [(dbrs[i], p["w_branch"][i])], "nn", F32) for i in range(NBR)]
    g["w_branch"] = jnp.stack([_mm(f"{tag}_dwb{i}", [(dbrs[i], ys[i])], "tn", BF16) for i in range(NBR)])

    du, dv, dln_g, dln_b, dsgu_w, dbias = _sgu_bwd(tag + "_dsgu", proj, dys[0], mp["ln_g"], mp["ln_b"], mp["sgu_w"],
                                                  mp["sgu_bias"])
    g["sgu_ln_g"], g["sgu_ln_b"], g["sgu_w"] = dln_g[0], dln_b[0], dsgu_w
    g["sgu_b"] = dbias.reshape(128, 4, 128).sum(axis=2).T

    (dbx, dbg, dcw, dcb, dwa, dba, dwx, dbxb, dlam) = _lru_bwd(
        tag + "_dlru", proj, dys[1], hc, mp["lru_cw"], mp["lru_cb"], mp["wa"], mp["ba"], mp["wx"], mp["bx"], mp["lam"])
    g["lru_conv_w"], g["lru_conv_b"], g["lru_ba"], g["lru_bx"], g["lru_lambda"] = dcw, dcb[0], dba[0], dbxb[0], dlam[0]
    g["lru_wa"], g["lru_wx"] = _block_diag_grad(dwa), _block_diag_grad(dwx)

    *dprep, dz, dng = _gdn_bwd(tag + "_dgdn", *prep, proj, dys[2], sh, mp["ng"])
    dqa, dka, dva, dtail, dalog, ddtb = _gdn_prep_bwd(tag + "_dgdnprep", qa, ka, va, proj, mp["alog"], mp["dtb"], *dprep)
    g["gdn_a_log"], g["gdn_dt_bias"], g["gdn_norm_g"] = dalog[0, 4:8], ddtb[0, 4:8], dng[0]
    dq, dcwq = _conv_bwd(tag + "_dconvq", proj, COL_CQ, dqa, mp["gdn_cw"], 0)
    dk, dcwk = _conv_bwd(tag + "_dconvk", proj, COL_CK, dka, mp["gdn_cw"], 512)
    dv_, dcwv = _conv_bwd(tag + "_dconvv", proj, COL_CV, dva, mp["gdn_cw"], 1024)
    g["gdn_conv_w"] = jnp.concatenate([dcwq, dcwk, dcwv], axis=1)

    dd, dpw, dsc = _pool_bwd(tag + "_dpool", proj, dys[3], mp["pool_w"], mp["pool_sc"])
    g["pool_w"], g["pool_scale"] = dpw, dsc[0]

    dproj = jnp.concatenate([du, dv, dbx, dbg, dq, dk, dv_, dz, dd, *dgates, dtail,
                             jnp.zeros((T, PW - COL_TAIL - 128), BF16)], axis=1)
    dw_in = _mm(tag + "_dwin", [(dproj, h)], "tn", BF16)
    tok = put(_w_in_from_layout(dw_in), g.pop("w_branch"), g.pop("w_out"))
    dh = _mm(tag + "_dh", [(dproj, p["w_in"])], "nn", F32, bm=_pick(T, (2048, 1024, 512, 256, 128)))
    dx, dnorm = _rms_bwd(tag + "_dnorm", x, p["mix_norm"][None] + tok, dh, dx_out)
    g["mix_norm"] = dnorm[0]
    return dx, g


_BIG = ("ff1_wg", "ff1_wu", "ff1_wd", "w_in", "w_branch", "w_out", "ff2_wg", "ff2_wu", "ff2_wd")
_COL_SHARDED = ("ff1_wg", "ff1_wu", "w_in", "w_branch", "ff2_wg", "ff2_wu")
_SMALL = ("ff1_norm", "mix_norm", "sgu_ln_g", "sgu_ln_b", "sgu_w", "sgu_b", "lru_conv_w", "lru_conv_b", "lru_wa",
          "lru_ba", "lru_wx", "lru_bx", "lru_lambda", "gdn_conv_w", "gdn_a_log", "gdn_dt_bias", "gdn_norm_g", "pool_w",
          "pool_scale", "ff2_norm", "final_norm")
_WEIGHTS = ("ff1_norm", "ff1_wg", "ff1_wu", "ff1_wd", "mix_norm", "w_in", "sgu_ln_g", "sgu_ln_b", "sgu_w", "sgu_b",
            "lru_conv_w", "lru_conv_b", "lru_wa", "lru_ba", "lru_wx", "lru_bx", "lru_lambda", "gdn_conv_w", "gdn_a_log",
            "gdn_dt_bias", "gdn_norm_g", "pool_w", "pool_scale", "w_branch", "w_out", "ff2_norm", "ff2_wg", "ff2_wu",
            "ff2_wd", "final_norm")
_CONV_SHARDED = ("lru_conv_w", "gdn_conv_w")
PACK_ROW_ALIGN = 16
_GROUPS = (("ff1", ("ff1_wg", "ff1_wu", "ff1_wd")), ("mix", ("w_in", "w_branch", "w_out")),
           ("ff2", ("ff2_wg", "ff2_wu", "ff2_wd")))


def _pad_rows(a, mult):
    pad = (-a.shape[-2]) % mult
    if pad == 0:
        return a
    return jnp.pad(a, [(0, 0)] * (a.ndim - 2) + [(0, pad), (0, 0)])


def _my_index():
    return 4 * lax.axis_index("x") + 2 * lax.axis_index("y") + lax.axis_index("c")


def _landing(own):
    return lax.dynamic_update_index_in_dim(lax.empty((N_DEV,) + own.shape, own.dtype), own, _my_index(), 0)


def _stored(n, a):
    return jnp.swapaxes(a, -1, -2) if n in _COL_SHARDED else a


def _gather_first(w):
    names = _GROUPS[0][1]
    shards = [_rows(_stored(n, w[n][0]).astype(BF16)) for n in names]
    got = _all_gather("gather_first", jnp.concatenate(shards, axis=0))
    out, r = {}, 0
    for n, s in zip(names, shards):
        out[n] = got[:, r:r + s.shape[0]].reshape(-1, 1024)
        r += s.shape[0]
    return out, got


def _gather_start(w, after):
    conv = _pad_rows(jnp.concatenate([w[n].reshape(1, -1) for n in _CONV_SHARDED], axis=1), 8)
    keys, srcs = ["conv"], [conv]
    for l in range(2):
        for sub, (_, names) in enumerate(_GROUPS):
            if (l, sub) != (0, 0):
                for n in names:
                    keys.append((l, sub, n))
                    srcs.append(_stored(n, w[n][l]).astype(BF16))
    handles, token = _exchange_start("gather_start", srcs, [_landing(s) for s in srcs], scatter=False, after=after)
    return dict(zip(keys, handles)), token


def _gather_finish(l, sub, handles, first, after):
    names = _GROUPS[sub][1]
    if (l, sub) == (0, 0):
        out = dict(first)
    else:
        lands = _exchange_wait(f"gather_wait_{l}{sub}", [handles[(l, sub, n)] for n in names], after, scatter=False)
        out = {n: _parts_to_rows(land) for n, land in zip(names, lands)}
    if "w_in" in out:
        out["w_in"] = _w_in_to_layout(out["w_in"])
    return out


def _scatter_start(l, sub, grads):
    srcs, shapes = [], []
    for n in _GROUPS[sub][1]:
        parts = _rows_to_parts(grads[n])
        shapes.append(parts.shape[1:])
        srcs.append(_pad_rows(parts.reshape(N_DEV, -1, 1024), PACK_ROW_ALIGN))
    me = _my_index()
    lands = [_landing(lax.dynamic_index_in_dim(s, me, 0, keepdims=False)) for s in srcs]
    handles, token = _exchange_start(f"scatter_start_{l}{sub}", srcs, lands, scatter=True)
    return handles, shapes, token


def _scatter_finish(l, sub, handles, shapes, after):
    lands = _exchange_wait(f"scatter_wait_{l}{sub}", handles, after, scatter=True)
    out = {}
    for n, land, shape in zip(_GROUPS[sub][1], lands, shapes):
        size = 1
        for s in shape:
            size *= s
        summed = _sum8(f"sum_{l}{sub}_{n}", land)
        out[n] = _stored(n, summed[:size // 1024].reshape(shape))
    return out


def _gather_conv_finish(w, handles, after):
    gconv = _exchange_wait("gather_wait_conv", [handles["conv"]], after, scatter=False)[0][:, 0]
    full, r = {}, 0
    for n in _CONV_SHARDED:
        sz = w[n].size
        full[n] = _parts_to_cols(gconv[:, r:r + sz].reshape((N_DEV,) + w[n].shape))
        r += sz
    return full


def _forward_backward(x, tgt, w, conv, get_weights, put_grads, put_small, token):
    saved, params = [], []
    for l in range(2):
        p = {n: w[n][l] for n in _SMALL if n != "final_norm"}
        for n in _CONV_SHARDED:
            p[n] = conv[n][l]
        mp = _mixer_params(p)
        tok = token[:1, :1] if l == 0 else 0.0
        p.update(get_weights(l, 0, x))
        x, s1 = _ffn_forward(f"l{l}_ff1", x, p["ff1_norm"][None] + tok, p["ff1_wg"], p["ff1_wu"], p["ff1_wd"])
        p.update(get_weights(l, 1, x))
        x, s2 = _mix_forward(f"l{l}_mix", x, p, mp)
        p.update(get_weights(l, 2, x))
        x, s3 = _ffn_forward(f"l{l}_ff2", x, p["ff2_norm"][None], p["ff2_wg"], p["ff2_wu"], p["ff2_wd"])
        saved.append((s1, s2, s3))
        params.append((p, mp))
    loss, dx, dfinal = _final_loss("loss_head", x, w["final_norm"][None], tgt)
    tok = 0.0
    for l in (1, 0):
        p, mp = params[l]
        s1, s2, s3 = saved[l]
        g = {}

        def put(sub):
            names = _GROUPS[sub][1]
            return lambda *gs, l=l: put_grads(l, sub, dict(zip(names, gs)))[:1, :1]

        dx, dn = _ffn_backward(f"l{l}_ff2", dx, s3, p["ff2_norm"][None] + tok, p["ff2_wg"], p["ff2_wu"], p["ff2_wd"],
                               put(2))
        g["ff2_norm"] = dn[0]
        dx, gm = _mix_backward(f"l{l}_mix", dx, s2, p, mp, put(1))
        g.update(gm)
        tok = 0.0
        if l == 0:
            tok = put_small("0a", g)[:1, :1]
            g = {}
        dx, dn = _ffn_backward(f"l{l}_ff1", dx, s1, p["ff1_norm"][None] + tok, p["ff1_wg"], p["ff1_wu"], p["ff1_wd"],
                               put(0))
        g["ff1_norm"] = dn[0]
        if l == 1:
            g["final_norm"] = dfinal[0]
            g["loss"] = loss[0, :1]
        tok = put_small("1" if l == 1 else "0b", g)[:1, :1]
    return dx


SMALL_PIECE = 8 * 1024


def _pack_small(d, names):
    pieces = []
    for n in names:
        flat = d[n].reshape(-1)
        pieces.append(jnp.pad(flat, (0, (-flat.size) % SMALL_PIECE)).reshape(-1, 1024))
    return jnp.concatenate(pieces, axis=0)


def _unpack_small(pack, shapes, names):
    out, r = {}, 0
    for n in names:
        size = 1
        for s in shapes[n]:
            size *= s
        rows = -(-size // SMALL_PIECE) * 8
        out[n] = pack[r:r + rows].reshape(-1)[:size].reshape(shapes[n])
        r += rows
    return out


def _small_names(grads):
    return tuple(n for n in _SMALL + ("loss",) if n in grads)


def _small_start(tag, grads):
    pack = _pack_small(grads, _small_names(grads))
    handles, token = _exchange_start(f"small_start_{tag}", [pack], [_landing(pack)], scatter=False)
    return handles, {n: grads[n].shape for n in _small_names(grads)}, token


def _small_finish(tag, handles, shapes, after):
    landed = _exchange_wait(f"small_wait_{tag}", handles, after, scatter=False)[0]
    return _unpack_small(_sum8(f"sum_small_{tag}", landed), shapes, _small_names(shapes))


def _as2d(a):
    if a.ndim == 1:
        return a.reshape(1, -1)
    return a.reshape(-1, a.shape[-1])


def kernel(x, ff1_norm, ff1_wg, ff1_wu, ff1_wd, mix_norm, w_in, sgu_ln_g, sgu_ln_b, sgu_w, sgu_b, lru_conv_w, lru_conv_b, lru_wa, lru_ba, lru_wx, lru_bx, lru_lambda, gdn_conv_w, gdn_a_log, gdn_dt_bias, gdn_norm_g, pool_w, pool_scale, w_branch, w_out, ff2_norm, ff2_wg, ff2_wu, ff2_wd, final_norm, loss_target, m_ff1_norm, m_ff1_wg, m_ff1_wu, m_ff1_wd, m_mix_norm, m_w_in, m_sgu_ln_g, m_sgu_ln_b, m_sgu_w, m_sgu_b, m_lru_conv_w, m_lru_conv_b, m_lru_wa, m_lru_ba, m_lru_wx, m_lru_bx, m_lru_lambda, m_gdn_conv_w, m_gdn_a_log, m_gdn_dt_bias, m_gdn_norm_g, m_pool_w, m_pool_scale, m_w_branch, m_w_out, m_ff2_norm, m_ff2_wg, m_ff2_wu, m_ff2_wd, m_final_norm, v_ff1_norm, v_ff1_wg, v_ff1_wu, v_ff1_wd, v_mix_norm, v_w_in, v_sgu_ln_g, v_sgu_ln_b, v_sgu_w, v_sgu_b, v_lru_conv_w, v_lru_conv_b, v_lru_wa, v_lru_ba, v_lru_wx, v_lru_bx, v_lru_lambda, v_gdn_conv_w, v_gdn_a_log, v_gdn_dt_bias, v_gdn_norm_g, v_pool_w, v_pool_scale, v_w_branch, v_w_out, v_ff2_norm, v_ff2_wg, v_ff2_wu, v_ff2_wd, v_final_norm):
    w = dict(ff1_norm=ff1_norm, ff1_wg=ff1_wg, ff1_wu=ff1_wu, ff1_wd=ff1_wd, mix_norm=mix_norm, w_in=w_in,
             sgu_ln_g=sgu_ln_g, sgu_ln_b=sgu_ln_b, sgu_w=sgu_w, sgu_b=sgu_b, lru_conv_w=lru_conv_w,
             lru_conv_b=lru_conv_b, lru_wa=lru_wa, lru_ba=lru_ba, lru_wx=lru_wx, lru_bx=lru_bx, lru_lambda=lru_lambda,
             gdn_conv_w=gdn_conv_w, gdn_a_log=gdn_a_log, gdn_dt_bias=gdn_dt_bias, gdn_norm_g=gdn_norm_g, pool_w=pool_w,
             pool_scale=pool_scale, w_branch=w_branch, w_out=w_out, ff2_norm=ff2_norm, ff2_wg=ff2_wg, ff2_wu=ff2_wu,
             ff2_wd=ff2_wd, final_norm=final_norm)
    m = dict(ff1_norm=m_ff1_norm, ff1_wg=m_ff1_wg, ff1_wu=m_ff1_wu, ff1_wd=m_ff1_wd, mix_norm=m_mix_norm, w_in=m_w_in,
             sgu_ln_g=m_sgu_ln_g, sgu_ln_b=m_sgu_ln_b, sgu_w=m_sgu_w, sgu_b=m_sgu_b, lru_conv_w=m_lru_conv_w,
             lru_conv_b=m_lru_conv_b, lru_wa=m_lru_wa, lru_ba=m_lru_ba, lru_wx=m_lru_wx, lru_bx=m_lru_bx,
             lru_lambda=m_lru_lambda, gdn_conv_w=m_gdn_conv_w, gdn_a_log=m_gdn_a_log, gdn_dt_bias=m_gdn_dt_bias,
             gdn_norm_g=m_gdn_norm_g, pool_w=m_pool_w, pool_scale=m_pool_scale, w_branch=m_w_branch, w_out=m_w_out,
             ff2_norm=m_ff2_norm, ff2_wg=m_ff2_wg, ff2_wu=m_ff2_wu, ff2_wd=m_ff2_wd, final_norm=m_final_norm)
    v = dict(ff1_norm=v_ff1_norm, ff1_wg=v_ff1_wg, ff1_wu=v_ff1_wu, ff1_wd=v_ff1_wd, mix_norm=v_mix_norm, w_in=v_w_in,
             sgu_ln_g=v_sgu_ln_g, sgu_ln_b=v_sgu_ln_b, sgu_w=v_sgu_w, sgu_b=v_sgu_b, lru_conv_w=v_lru_conv_w,
             lru_conv_b=v_lru_conv_b, lru_wa=v_lru_wa, lru_ba=v_lru_ba, lru_wx=v_lru_wx, lru_bx=v_lru_bx,
             lru_lambda=v_lru_lambda, gdn_conv_w=v_gdn_conv_w, gdn_a_log=v_gdn_a_log, gdn_dt_bias=v_gdn_dt_bias,
             gdn_norm_g=v_gdn_norm_g, pool_w=v_pool_w, pool_scale=v_pool_scale, w_branch=v_w_branch, w_out=v_w_out,
             ff2_norm=v_ff2_norm, ff2_wg=v_ff2_wg, ff2_wu=v_ff2_wu, ff2_wd=v_ff2_wd, final_norm=v_final_norm)

    first, got_first = _gather_first(w)
    handles, token = _gather_start(w, got_first)
    conv = _gather_conv_finish(w, handles, token)
    pending = {}

    def get_weights(l, sub, after):
        return _gather_finish(l, sub, handles, first, after)

    def put_grads(l, sub, grads):
        hs, shapes, tok = _scatter_start(l, sub, grads)
        pending[(l, sub)] = (hs, shapes)
        return tok

    def put_small(tag, grads):
        hs, shapes, tok = _small_start(tag, grads)
        pending[tag] = (hs, shapes)
        return tok

    T = x.shape[1]
    dx = _forward_backward(x.reshape(T, D), loss_target.reshape(T, D), w, conv, get_weights, put_grads, put_small,
                           token)
    per = {key: (_scatter_finish(*key, *pending[key], dx) if isinstance(key, tuple) else
                 _small_finish(key, *pending[key], dx)) for key in pending}
    grad = {n: jnp.stack([per[(0, sub)][n], per[(1, sub)][n]]) for sub, (_, names) in enumerate(_GROUPS) for n in names}
    layer0 = {**per["0a"], **per["0b"]}
    small = {n: _join([layer0[n].reshape(-1), per["1"][n].reshape(-1)]).reshape((2,) + layer0[n].shape)
             for n in layer0}
    small["final_norm"] = per["1"]["final_norm"]
    loss = per["1"]["loss"][0]
    me = _my_index()
    for n in _SMALL:
        if n in _CONV_SHARDED:
            width = w[n].shape[-1]
            grad[n] = lax.dynamic_slice_in_dim(small[n], me * width, width, axis=2)
        else:
            grad[n] = small[n]

    delta, new_m, new_v = {}, {}, {}
    for n in _BIG:
        d_, m_, v_ = _adamw("adamw_" + n, _as2d(w[n]), _as2d(grad[n]), _as2d(m[n]), _as2d(v[n]))
        delta[n], new_m[n], new_v[n] = (t.reshape(w[n].shape) for t in (d_, m_, v_))

    outs = _adamw_many("adamw_small", *[[_as2d(t[n]) for n in _SMALL] for t in (w, grad, m, v)])
    for k, dst in enumerate((delta, new_m, new_v)):
        for i, n in enumerate(_SMALL):
            dst[n] = outs[k * len(_SMALL) + i].reshape(w[n].shape)

    return (loss, dx.reshape(x.shape), *[grad[n] for n in _WEIGHTS], *[delta[n] for n in _WEIGHTS],
            *[new_m[n] for n in _WEIGHTS], *[new_v[n] for n in _WEIGHTS])
```

```python
import functools

import jax
import jax.numpy as jnp
from jax import lax
from jax.experimental import pallas as pl
from jax.experimental.pallas import tpu as pltpu

F32 = jnp.float32
BF16 = jnp.bfloat16
HI = lax.Precision.HIGHEST

N_DEV = 8
D = 1024
FF = 2816
BW = 512
NBR = 4
CHUNK = 64
EPS = 1e-6
LRU_C = 8.0
GDN_DK = 128

COL_AU, COL_AV, COL_BX, COL_BG = 0, 512, 1024, 1536
COL_CQ, COL_CK, COL_CV, COL_CZ = 2048, 2560, 3072, 3584
COL_DX, COL_GATE, COL_TAIL = 4096, 4608, 8704
PW = 9216
P_IN = 8712

ADAM_LR, ADAM_B1, ADAM_B2, ADAM_EPS, ADAM_WD, ADAM_STEP = 0.001, 0.9, 0.999, 1e-08, 0.01, 10

VMEM_LIMIT_V7X = 56 * 1024 * 1024

_NN = (((1,), (0,)), ((), ()))
_NT = (((1,), (1,)), ((), ()))
_TN = (((0,), (0,)), ((), ()))


def _cp(*sem):
    return pltpu.CompilerParams(dimension_semantics=tuple(sem), vmem_limit_bytes=VMEM_LIMIT_V7X)


def _dot(a, b, dims=_NN):
    return lax.dot_general(a.astype(BF16), b.astype(BF16), dims, preferred_element_type=F32)


def _dot_hi(a, b, dims=_NN):
    return lax.dot_general(a, b, dims, precision=HI, preferred_element_type=F32)


def _pick(n, cands):
    for c in cands:
        if n % c == 0:
            return c
    return n


@jax.custom_jvp
def _log1p(x):
    u = 1.0 + x
    return jnp.where(u == 1.0, x, x * jnp.log(u) / jnp.where(u == 1.0, 1.0, u - 1.0))


@_log1p.defjvp
def _log1p_jvp(p, t):
    (x,), (dx,) = p, t
    return _log1p(x), dx / (1.0 + x)


@jax.custom_jvp
def _expm1(x):
    u = jnp.exp(x)
    lu = jnp.log(u)
    small = (u == 1.0) | (lu == 0.0)
    return jnp.where(small, x, (u - 1.0) * x / jnp.where(small, 1.0, lu))


@_expm1.defjvp
def _expm1_jvp(p, t):
    (x,), (dx,) = p, t
    return _expm1(x), dx * jnp.exp(x)


def _softplus(x):
    return jnp.maximum(x, 0.0) + _log1p(jnp.exp(-jnp.abs(x)))


def _sigmoid(x):
    return jax.nn.sigmoid(x)


def _silu(x):
    return x * jax.nn.sigmoid(x)


def _gelu(x):
    return jax.nn.gelu(x)


@functools.partial(jax.custom_vjp, nondiff_argnums=(1,))
def _shift(x, s):
    return x if s == 0 else pltpu.roll(x, s, 0)


def _shift_fwd(x, s):
    return _shift(x, s), None


def _shift_bwd(s, _, g):
    n = g.shape[0]
    return (g if s == 0 else pltpu.roll(g, n - s, 0),)


_shift.defvjp(_shift_fwd, _shift_bwd)


def _scan_steps(a, b, reverse):
    n = a.shape[0]
    row = lax.broadcasted_iota(jnp.int32, a.shape, 0)
    k = 1
    while k < n:
        sh = n - k if reverse else k
        m = (row < n - k) if reverse else (row >= k)
        a_s = jnp.where(m, pltpu.roll(a, sh, 0), 1.0)
        b_s = jnp.where(m, pltpu.roll(b, sh, 0), 0.0)
        b = a * b_s + b
        a = a * a_s
        k *= 2
    return b


@jax.custom_vjp
def _scan(a, b):
    return _scan_steps(a, b, False)


def _scan_fwd(a, b):
    h = _scan_steps(a, b, False)
    return h, (a, h)


def _scan_bwd(res, dh):
    a, h = res
    n = a.shape[0]
    row = lax.broadcasted_iota(jnp.int32, a.shape, 0)
    a_next = jnp.where(row < n - 1, pltpu.roll(a, n - 1, 0), 0.0)
    g = _scan_steps(a_next, dh, True)
    h_prev = jnp.where(row >= 1, pltpu.roll(h, 1, 0), 0.0)
    return g * h_prev, g


_scan.defvjp(_scan_fwd, _scan_bwd)


def _mm(name, pairs, mode, out_dtype, *, res=None, scale=1.0, bm=None, bn=None, bk=None, after=None):
    a0, b0 = pairs[0]
    if mode == "nn":
        (M, K), N = a0.shape, b0.shape[1]
    elif mode == "nt":
        (M, K), N = a0.shape, b0.shape[0]
    else:
        (K, M), N = a0.shape, b0.shape[1]
    bm = bm or _pick(M, (1024, 512, 256, 128))
    bn = bn or _pick(N, (1024, 512, 256, 128))
    bk = bk or _pick(K, (1024, 512, 1408, 256, 128))
    nk = K // bk
    npair = len(pairs)
    dims = {"nn": _NN, "nt": _NT, "tn": _TN}[mode]

    def body(*refs):
        ab = refs[:2 * npair]
        pos = 2 * npair
        r_ref = None
        if res is not None:
            r_ref = refs[pos]
            pos += 1
        pos += after is not None
        o_ref = refs[pos]
        part = None
        for p in range(npair):
            d = _dot(ab[2 * p][...], ab[2 * p + 1][...], dims)
            part = d if part is None else part + d

        def finish(acc):
            out = acc if scale == 1.0 else acc * scale
            if r_ref is not None:
                out = out + r_ref[...]
            o_ref[...] = out.astype(out_dtype)

        if nk == 1:
            finish(part)
        else:
            acc_ref = refs[pos + 1]
            k = pl.program_id(2)

            @pl.when(k == 0)
            def _():
                acc_ref[...] = part

            @pl.when(k > 0)
            def _():
                acc_ref[...] += part

            @pl.when(k == nk - 1)
            def _():
                finish(acc_ref[...])

    if mode == "nn":
        a_spec = pl.BlockSpec((bm, bk), lambda i, j, k: (i, k))
        b_spec = pl.BlockSpec((bk, bn), lambda i, j, k: (k, j))
    elif mode == "nt":
        a_spec = pl.BlockSpec((bm, bk), lambda i, j, k: (i, k))
        b_spec = pl.BlockSpec((bn, bk), lambda i, j, k: (j, k))
    else:
        a_spec = pl.BlockSpec((bk, bm), lambda i, j, k: (k, i))
        b_spec = pl.BlockSpec((bk, bn), lambda i, j, k: (k, j))
    o_spec = pl.BlockSpec((bm, bn), lambda i, j, k: (i, j))
    in_specs, args = [], []
    for a, b in pairs:
        in_specs += [a_spec, b_spec]
        args += [a, b]
    if res is not None:
        in_specs.append(o_spec)
        args.append(res)
    if after is not None:
        in_specs.append(_ANY)
        args.append(after)
    return pl.pallas_call(
        body, name=name, grid=(M // bm, N // bn, nk),
        in_specs=in_specs, out_specs=o_spec,
        out_shape=jax.ShapeDtypeStruct((M, N), out_dtype),
        scratch_shapes=[pltpu.VMEM((bm, bn), F32)] if nk > 1 else [],
        compiler_params=_cp("parallel", "parallel", "arbitrary"),
    )(*args)


def _rms_fwd(name, x, g):
    T = x.shape[0]
    bm = _pick(T, (512, 256, 128))

    def body(x_ref, g_ref, o_ref):
        xv = x_ref[...]
        r = lax.rsqrt(jnp.mean(xv * xv, axis=-1, keepdims=True) + EPS)
        o_ref[...] = (xv * r * g_ref[...]).astype(BF16)

    return pl.pallas_call(
        body, name=name, grid=(T // bm,),
        in_specs=[pl.BlockSpec((bm, D), lambda i: (i, 0)), pl.BlockSpec((1, D), lambda i: (0, 0))],
        out_specs=pl.BlockSpec((bm, D), lambda i: (i, 0)),
        out_shape=jax.ShapeDtypeStruct((T, D), BF16),
        compiler_params=_cp("parallel"),
    )(x, g)


def _rms_bwd(name, x, g, dh, dres):
    T = x.shape[0]
    bm = _pick(T, (512, 256, 128))

    def body(x_ref, g_ref, dh_ref, dres_ref, dx_ref, dg_ref):
        xv = x_ref[...]
        r = lax.rsqrt(jnp.mean(xv * xv, axis=-1, keepdims=True) + EPS)
        xh = xv * r
        dhv = dh_ref[...]
        dxh = dhv * g_ref[...]
        dx_ref[...] = dres_ref[...] + r * (dxh - xh * jnp.mean(dxh * xh, axis=-1, keepdims=True))
        part = jnp.sum(dhv * xh, axis=0, keepdims=True)

        @pl.when(pl.program_id(0) == 0)
        def _():
            dg_ref[...] = part

        @pl.when(pl.program_id(0) > 0)
        def _():
            dg_ref[...] += part

    row = pl.BlockSpec((bm, D), lambda i: (i, 0))
    vec = pl.BlockSpec((1, D), lambda i: (0, 0))
    return pl.pallas_call(
        body, name=name, grid=(T // bm,),
        in_specs=[row, vec, row, row], out_specs=[row, vec],
        out_shape=[jax.ShapeDtypeStruct((T, D), F32), jax.ShapeDtypeStruct((1, D), F32)],
        compiler_params=_cp("arbitrary"),
    )(x, g, dh, dres)


def _final_loss(name, x, g, tgt):
    T = x.shape[0]
    bm = _pick(T, (512, 256, 128))

    def body(x_ref, g_ref, t_ref, loss_ref, dx_ref, dg_ref):
        xv = x_ref[...]
        gv = g_ref[...]
        r = lax.rsqrt(jnp.mean(xv * xv, axis=-1, keepdims=True) + EPS)
        xh = xv * r
        e = xh * gv - t_ref[...]
        lpart = jnp.broadcast_to(0.5 * jnp.sum(jnp.mean(e * e, axis=-1, keepdims=True), axis=0, keepdims=True), (1, 128))
        dy = e * (1.0 / D)
        dxh = dy * gv
        dx_ref[...] = r * (dxh - xh * jnp.mean(dxh * xh, axis=-1, keepdims=True))
        gpart = jnp.sum(dy * xh, axis=0, keepdims=True)

        @pl.when(pl.program_id(0) == 0)
        def _():
            loss_ref[...] = lpart
            dg_ref[...] = gpart

        @pl.when(pl.program_id(0) > 0)
        def _():
            loss_ref[...] += lpart
            dg_ref[...] += gpart

    row = pl.BlockSpec((bm, D), lambda i: (i, 0))
    vec = pl.BlockSpec((1, D), lambda i: (0, 0))
    return pl.pallas_call(
        body, name=name, grid=(T // bm,),
        in_specs=[row, vec, row],
        out_specs=[pl.BlockSpec((1, 128), lambda i: (0, 0)), row, vec],
        out_shape=[jax.ShapeDtypeStruct((1, 128), F32), jax.ShapeDtypeStruct((T, D), F32),
                   jax.ShapeDtypeStruct((1, D), F32)],
        compiler_params=_cp("arbitrary"),
    )(x, g, tgt)


def _ffn_up(name, h, wg, wu):
    T = h.shape[0]
    bm = _pick(T, (2048, 1024, 512, 256, 128))
    bn = 256

    def body(h_ref, wg_ref, wu_ref, a_ref, b_ref, act_ref):
        hv = h_ref[...]
        a = _dot(hv, wg_ref[...], _NT)
        b = _dot(hv, wu_ref[...], _NT)
        a_ref[...] = a.astype(BF16)
        b_ref[...] = b.astype(BF16)
        act_ref[...] = (_silu(a) * b).astype(BF16)

    w_spec = pl.BlockSpec((bn, D), lambda i, j: (j, 0))
    o_spec = pl.BlockSpec((bm, bn), lambda i, j: (i, j))
    return pl.pallas_call(
        body, name=name, grid=(T // bm, FF // bn),
        in_specs=[pl.BlockSpec((bm, D), lambda i, j: (i, 0)), w_spec, w_spec],
        out_specs=[o_spec, o_spec, o_spec],
        out_shape=[jax.ShapeDtypeStruct((T, FF), BF16)] * 3,
        compiler_params=_cp("parallel", "parallel"),
    )(h, wg, wu)


def _ffn_dact(name, dy, wd, a, b, after=None):
    T = dy.shape[0]
    bm = _pick(T, (2048, 1024, 512, 256, 128))
    bn = 256

    def body(dy_ref, wd_ref, a_ref, b_ref, *rest):
        da_ref, db_ref, dy_bf = rest[-3:]

        @pl.when(pl.program_id(1) == 0)
        def _():
            dy_bf[...] = dy_ref[...].astype(BF16)

        dact = 0.5 * _dot(dy_bf[...], wd_ref[...], _NT)
        av = a_ref[...].astype(F32)
        s = _sigmoid(av)
        da_ref[...] = (dact * b_ref[...].astype(F32) * (s * (1.0 + av * (1.0 - s)))).astype(BF16)
        db_ref[...] = (dact * (av * s)).astype(BF16)

    t_spec = pl.BlockSpec((bm, bn), lambda i, j: (i, j))
    return pl.pallas_call(
        body, name=name, grid=(T // bm, FF // bn),
        in_specs=[pl.BlockSpec((bm, D), lambda i, j: (i, 0)), pl.BlockSpec((bn, D), lambda i, j: (j, 0)),
                  t_spec, t_spec] + [_ANY] * (after is not None),
        out_specs=[t_spec, t_spec],
        out_shape=[jax.ShapeDtypeStruct((T, FF), BF16), jax.ShapeDtypeStruct((T, FF), BF16)],
        scratch_shapes=[pltpu.VMEM((bm, D), BF16)],
        compiler_params=_cp("parallel", "arbitrary"),
    )(dy, wd, a, b, *([after] if after is not None else []))


def _merge_specs(T, bm, bn):
    y_spec = pl.BlockSpec((bm, BW), lambda i, j: (i, 0))
    wb_spec = pl.BlockSpec((NBR, bn, BW), lambda i, j: (0, j, 0))
    gate_specs = [pl.BlockSpec((bm, bn), functools.partial(lambda i, j, o: (i, o + j), o=(COL_GATE + g * D) // bn))
                  for g in range(NBR)]
    t_spec = pl.BlockSpec((bm, bn), lambda i, j: (i, j))
    return y_spec, wb_spec, gate_specs, t_spec


def _merge_fwd(name, ys, wb, proj):
    T = proj.shape[0]
    bm = _pick(T, (512, 256, 128))
    bn = 512
    y_spec, wb_spec, gate_specs, t_spec = _merge_specs(T, bm, bn)

    def body(y0, y1, y2, y3, wb_ref, g0, g1, g2, g3, o_ref):
        acc = None
        for g, (y_ref, g_ref) in enumerate(((y0, g0), (y1, g1), (y2, g2), (y3, g3))):
            t = _sigmoid(g_ref[...]) * _dot(y_ref[...], wb_ref[g], _NT)
            acc = t if acc is None else acc + t
        o_ref[...] = acc.astype(BF16)

    return pl.pallas_call(
        body, name=name, grid=(T // bm, D // bn),
        in_specs=[y_spec] * NBR + [wb_spec] + gate_specs, out_specs=t_spec,
        out_shape=jax.ShapeDtypeStruct((T, D), BF16),
        compiler_params=_cp("parallel", "parallel"),
    )(*ys, wb, proj, proj, proj, proj)


def _merge_bwd(name, dm, ys, wb, proj):
    T = proj.shape[0]
    bm = _pick(T, (512, 256, 128))
    bn = 512
    y_spec, wb_spec, gate_specs, t_spec = _merge_specs(T, bm, bn)

    def body(dm_ref, y0, y1, y2, y3, wb_ref, g0, g1, g2, g3, *outs):
        dmv = dm_ref[...]
        for g, (y_ref, g_ref) in enumerate(((y0, g0), (y1, g1), (y2, g2), (y3, g3))):
            br = _dot(y_ref[...], wb_ref[g], _NT)
            s = _sigmoid(g_ref[...])
            outs[g][...] = (dmv * br * (s * (1.0 - s))).astype(BF16)
            outs[NBR + g][...] = (dmv * s).astype(BF16)

    return pl.pallas_call(
        body, name=name, grid=(T // bm, D // bn),
        in_specs=[t_spec] + [y_spec] * NBR + [wb_spec] + gate_specs, out_specs=[t_spec] * (2 * NBR),
        out_shape=[jax.ShapeDtypeStruct((T, D), BF16)] * (2 * NBR),
        compiler_params=_cp("parallel", "parallel"),
    )(dm, *ys, wb, proj, proj, proj, proj)


def _sgu_block(u_pre, v_pre, ln_g, ln_b, w, bias):
    u = _gelu(u_pre)
    vf = _gelu(v_pre)
    mu = jnp.mean(vf, axis=-1, keepdims=True)
    var = jnp.mean(jnp.square(vf - mu), axis=-1, keepdims=True)
    vn = (vf - mu) * lax.rsqrt(var + EPS) * ln_g + ln_b
    ri = lax.broadcasted_iota(jnp.int32, (128, 128), 0)
    ci = lax.broadcasted_iota(jnp.int32, (128, 128), 1)
    mask = (ri // CHUNK) >= (ci // CHUNK)
    outs = [_dot(jnp.where(mask, w[g], 0.0), vn[:, g * 128:(g + 1) * 128]) for g in range(4)]
    mixed = jnp.concatenate(outs, axis=1) + bias
    return u * mixed


def _sgu_param_specs():
    return [pl.BlockSpec((1, BW), lambda i: (0, 0)), pl.BlockSpec((1, BW), lambda i: (0, 0)),
            pl.BlockSpec((4, 128, 128), lambda i: (0, 0, 0)), pl.BlockSpec((128, BW), lambda i: (0, 0))]


def _sgu_fwd(name, proj, ln_g, ln_b, w, bias):
    T = proj.shape[0]
    rb = _pick(T, (256, 128))

    def body(u_ref, v_ref, g_ref, b_ref, w_ref, bias_ref, y_ref):
        for n in range(rb // 128):
            rows = slice(n * 128, (n + 1) * 128)
            y = _sgu_block(u_ref[rows, :], v_ref[rows, :], g_ref[...], b_ref[...], w_ref[...], bias_ref[...])
            y_ref[rows, :] = y.astype(BF16)

    return pl.pallas_call(
        body, name=name, grid=(T // rb,),
        in_specs=[pl.BlockSpec((rb, BW), lambda i: (i, COL_AU // BW)), pl.BlockSpec((rb, BW), lambda i: (i, COL_AV // BW))]
        + _sgu_param_specs(),
        out_specs=pl.BlockSpec((rb, BW), lambda i: (i, 0)),
        out_shape=jax.ShapeDtypeStruct((T, BW), BF16),
        compiler_params=_cp("parallel"),
    )(proj, proj, ln_g, ln_b, w, bias)


def _sgu_bwd(name, proj, dy, ln_g, ln_b, w, bias):
    T = proj.shape[0]
    rb = _pick(T, (256, 128))

    def body(u_ref, v_ref, dy_ref, g_ref, b_ref, w_ref, bias_ref, du_ref, dv_ref, dg_ref, db_ref, dw_ref, dbias_ref):
        acc = None
        for n in range(rb // 128):
            rows = slice(n * 128, (n + 1) * 128)
            _, vjp = jax.vjp(_sgu_block, u_ref[rows, :], v_ref[rows, :], g_ref[...], b_ref[...], w_ref[...],
                             bias_ref[...])
            du, dv, *dp = vjp(dy_ref[rows, :])
            du_ref[rows, :] = du.astype(BF16)
            dv_ref[rows, :] = dv.astype(BF16)
            acc = dp if acc is None else [p + q for p, q in zip(acc, dp)]

        @pl.when(pl.program_id(0) == 0)
        def _():
            for r, p in zip((dg_ref, db_ref, dw_ref, dbias_ref), acc):
                r[...] = p

        @pl.when(pl.program_id(0) > 0)
        def _():
            for r, p in zip((dg_ref, db_ref, dw_ref, dbias_ref), acc):
                r[...] += p

    row = pl.BlockSpec((rb, BW), lambda i: (i, 0))
    return pl.pallas_call(
        body, name=name, grid=(T // rb,),
        in_specs=[pl.BlockSpec((rb, BW), lambda i: (i, COL_AU // BW)), pl.BlockSpec((rb, BW), lambda i: (i, COL_AV // BW)),
                  row] + _sgu_param_specs(),
        out_specs=[row, row] + _sgu_param_specs(),
        out_shape=[jax.ShapeDtypeStruct((T, BW), BF16), jax.ShapeDtypeStruct((T, BW), BF16),
                   jax.ShapeDtypeStruct((1, BW), F32), jax.ShapeDtypeStruct((1, BW), F32),
                   jax.ShapeDtypeStruct((4, 128, 128), F32), jax.ShapeDtypeStruct((128, BW), F32)],
        compiler_params=_cp("arbitrary"),
    )(proj, proj, dy, ln_g, ln_b, w, bias)


def _halo_block(ref, i, rblk, halo):
    r0 = pl.multiple_of(i * rblk, rblk)
    h0 = pl.multiple_of(jnp.maximum(r0 - halo, 0), halo)
    top = jnp.where(i > 0, ref[pl.ds(h0, halo), :], 0.0)
    return jnp.concatenate([top, ref[pl.ds(r0, rblk), :]], axis=0)


def _with_halo_grad(dfull, pending, halo, rblk):
    tail = jnp.concatenate([jnp.zeros((rblk - halo, 128), F32), pending], axis=0)
    return dfull[halo:] + tail


def _conv4(xfull, rows):
    acc = None
    for k in range(4):
        t = rows[k] * _shift(xfull, 3 - k)[8:]
        acc = t if acc is None else acc + t
    return acc


def _lru_block(xfull, gate, h0, c0, c1, c2, c3, cb, wa, ba, wx, bx, lam):
    n = gate.shape[0]
    xc = _conv4(xfull, (c0, c1, c2, c3)) + cb
    r = _sigmoid(_dot(xc, wa) + ba)
    ig = _sigmoid(_dot(xc, wx) + bx)
    log_a = -LRU_C * r * _softplus(-lam)
    a = jnp.exp(log_a)
    mult = jnp.sqrt(-_expm1(2.0 * log_a))
    b = mult * (ig * xc)
    row = lax.broadcasted_iota(jnp.int32, (n, 128), 0)
    b = b + jnp.where(row == 0, a * h0, 0.0)
    h = _scan(a, b)
    out = h * _gelu(gate)
    h_last = jnp.sum(jnp.where(row == n - 1, h, 0.0), axis=0, keepdims=True)
    return out, h_last


def _lru_param_specs():
    vec = pl.BlockSpec((1, 128), lambda g: (0, g))
    mat = pl.BlockSpec((None, 128, 128), lambda g: (g, 0, 0))
    return [pl.BlockSpec((4, 128), lambda g: (0, g)), vec, mat, vec, mat, vec, vec]


def _lru_load_params(cw_ref, cb_ref, wa_ref, ba_ref, wx_ref, bx_ref, lam_ref):
    return (cw_ref[0:1, :], cw_ref[1:2, :], cw_ref[2:3, :], cw_ref[3:4, :], cb_ref[...], wa_ref[...], ba_ref[...],
            wx_ref[...], bx_ref[...], lam_ref[...])


def _lru_fwd(name, proj, cw, cb, wa, ba, wx, bx, lam):
    T = proj.shape[0]
    rblk = _pick(T, (256, 128))
    nblk = T // rblk

    def body(x_ref, gt_ref, cw_ref, cb_ref, wa_ref, ba_ref, wx_ref, bx_ref, lam_ref, y_ref, hc_ref):
        params = _lru_load_params(cw_ref, cb_ref, wa_ref, ba_ref, wx_ref, bx_ref, lam_ref)

        def step(i, h0):
            r0 = pl.multiple_of(i * rblk, rblk)
            out, h_last = _lru_block(_halo_block(x_ref, i, rblk, 8), gt_ref[pl.ds(r0, rblk), :], h0, *params)
            y_ref[pl.ds(r0, rblk), :] = out.astype(BF16)
            hc_ref[pl.ds(pl.multiple_of(i * 8, 8), 8), :] = jnp.broadcast_to(h0, (8, 128))
            return h_last

        lax.fori_loop(0, nblk, step, jnp.zeros((1, 128), F32))

    return pl.pallas_call(
        body, name=name, grid=(4,),
        in_specs=[pl.BlockSpec((T, 128), lambda g: (0, COL_BX // 128 + g)),
                  pl.BlockSpec((T, 128), lambda g: (0, COL_BG // 128 + g))] + _lru_param_specs(),
        out_specs=[pl.BlockSpec((T, 128), lambda g: (0, g)), pl.BlockSpec((nblk * 8, 128), lambda g: (0, g))],
        out_shape=[jax.ShapeDtypeStruct((T, BW), BF16), jax.ShapeDtypeStruct((nblk * 8, BW), F32)],
        compiler_params=_cp("parallel"),
    )(proj, proj, cw, cb, wa, ba, wx, bx, lam)


def _lru_bwd(name, proj, dy, hc, cw, cb, wa, ba, wx, bx, lam):
    T = proj.shape[0]
    rblk = _pick(T, (256, 128))
    nblk = T // rblk

    def body(x_ref, gt_ref, dy_ref, hc_ref, cw_ref, cb_ref, wa_ref, ba_ref, wx_ref, bx_ref, lam_ref,
             dx_ref, dgt_ref, dcw_ref, dcb_ref, dwa_ref, dba_ref, dwx_ref, dbx_ref, dlam_ref):
        params = _lru_load_params(cw_ref, cb_ref, wa_ref, ba_ref, wx_ref, bx_ref, lam_ref)

        def step(it, carry):
            dh_last, pending, acc = carry
            i = nblk - 1 - it
            r0 = pl.multiple_of(i * rblk, rblk)
            h0 = hc_ref[pl.ds(pl.multiple_of(i * 8, 8), 1), :]
            _, vjp = jax.vjp(_lru_block, _halo_block(x_ref, i, rblk, 8), gt_ref[pl.ds(r0, rblk), :], h0, *params)
            dfull, dgate, dh0, *dp = vjp((dy_ref[pl.ds(r0, rblk), :], dh_last))
            dx_ref[pl.ds(r0, rblk), :] = _with_halo_grad(dfull, pending, 8, rblk).astype(BF16)
            dgt_ref[pl.ds(r0, rblk), :] = dgate.astype(BF16)
            return dh0, dfull[:8], tuple(p + q for p, q in zip(acc, dp))

        zeros = tuple(jnp.zeros(p.shape, F32) for p in params)
        _, _, acc = lax.fori_loop(0, nblk, step, (jnp.zeros((1, 128), F32), jnp.zeros((8, 128), F32), zeros))
        for k in range(4):
            dcw_ref[k:k + 1, :] = acc[k]
        for r, p in zip((dcb_ref, dwa_ref, dba_ref, dwx_ref, dbx_ref, dlam_ref), acc[4:]):
            r[...] = p

    col = pl.BlockSpec((T, 128), lambda g: (0, g))
    return pl.pallas_call(
        body, name=name, grid=(4,),
        in_specs=[pl.BlockSpec((T, 128), lambda g: (0, COL_BX // 128 + g)),
                  pl.BlockSpec((T, 128), lambda g: (0, COL_BG // 128 + g)), col,
                  pl.BlockSpec((nblk * 8, 128), lambda g: (0, g))] + _lru_param_specs(),
        out_specs=[col, col] + _lru_param_specs(),
        out_shape=[jax.ShapeDtypeStruct((T, BW), BF16), jax.ShapeDtypeStruct((T, BW), BF16),
                   jax.ShapeDtypeStruct((4, BW), F32), jax.ShapeDtypeStruct((1, BW), F32),
                   jax.ShapeDtypeStruct((4, 128, 128), F32), jax.ShapeDtypeStruct((1, BW), F32),
                   jax.ShapeDtypeStruct((4, 128, 128), F32), jax.ShapeDtypeStruct((1, BW), F32),
                   jax.ShapeDtypeStruct((1, BW), F32)],
        compiler_params=_cp("parallel"),
    )(proj, proj, dy, hc, cw, cb, wa, ba, wx, bx, lam)


def _conv_block(xfull, c0, c1, c2, c3):
    return _silu(_conv4(xfull, (c0, c1, c2, c3)))


def _conv_fwd(name, proj, col0, cw, cw_col0):
    T = proj.shape[0]
    rblk = _pick(T, (256, 128))
    nblk = T // rblk

    def body(x_ref, cw_ref, y_ref):
        rows = (cw_ref[0:1, :], cw_ref[1:2, :], cw_ref[2:3, :], cw_ref[3:4, :])

        def step(i, c):
            r0 = pl.multiple_of(i * rblk, rblk)
            y_ref[pl.ds(r0, rblk), :] = _conv_block(_halo_block(x_ref, i, rblk, 8), *rows)
            return c

        lax.fori_loop(0, nblk, step, 0)

    return pl.pallas_call(
        body, name=name, grid=(4,),
        in_specs=[pl.BlockSpec((T, 128), lambda g: (0, col0 // 128 + g)),
                  pl.BlockSpec((4, 128), lambda g: (0, cw_col0 // 128 + g))],
        out_specs=pl.BlockSpec((T, 128), lambda g: (0, g)),
        out_shape=jax.ShapeDtypeStruct((T, BW), F32),
        compiler_params=_cp("parallel"),
    )(proj, cw)


def _conv_bwd(name, proj, col0, dy, cw, cw_col0):
    T = proj.shape[0]
    rblk = _pick(T, (256, 128))
    nblk = T // rblk

    def body(x_ref, dy_ref, cw_ref, dx_ref, dcw_ref):
        rows = (cw_ref[0:1, :], cw_ref[1:2, :], cw_ref[2:3, :], cw_ref[3:4, :])

        def step(it, carry):
            pending, acc = carry
            i = nblk - 1 - it
            r0 = pl.multiple_of(i * rblk, rblk)
            _, vjp = jax.vjp(_conv_block, _halo_block(x_ref, i, rblk, 8), *rows)
            dfull, *dp = vjp(dy_ref[pl.ds(r0, rblk), :])
            dx_ref[pl.ds(r0, rblk), :] = _with_halo_grad(dfull, pending, 8, rblk).astype(BF16)
            return dfull[:8], tuple(p + q for p, q in zip(acc, dp))

        zeros = tuple(jnp.zeros((1, 128), F32) for _ in range(4))
        _, acc = lax.fori_loop(0, nblk, step, (jnp.zeros((8, 128), F32), zeros))
        for k in range(4):
            dcw_ref[k:k + 1, :] = acc[k]

    col = pl.BlockSpec((T, 128), lambda g: (0, g))
    return pl.pallas_call(
        body, name=name, grid=(4,),
        in_specs=[pl.BlockSpec((T, 128), lambda g: (0, col0 // 128 + g)), col,
                  pl.BlockSpec((4, 128), lambda g: (0, cw_col0 // 128 + g))],
        out_specs=[col, pl.BlockSpec((4, 128), lambda g: (0, g))],
        out_shape=[jax.ShapeDtypeStruct((T, BW), BF16), jax.ShapeDtypeStruct((4, BW), F32)],
        compiler_params=_cp("parallel"),
    )(proj, dy, cw)


def _pool_block(xfull, pw, sc, t0, gi):
    n = xfull.shape[0] - 16
    s2 = xfull + _shift(xfull, 1)
    s4 = s2 + _shift(s2, 2)
    s8 = s4 + _shift(s4, 4)
    s16 = s8 + _shift(s8, 8)
    s = jnp.where(gi == 0, s2, jnp.where(gi == 1, s4, jnp.where(gi == 2, s8, s16)))[16:]
    t = t0 + lax.broadcasted_iota(jnp.int32, (n, 128), 0)
    cnt = jnp.minimum(t + 1, lax.shift_left(jnp.int32(2), gi)).astype(F32)
    pooled = s / cnt - xfull[16:]
    return _dot(pooled, pw) * sc


def _pool_fwd(name, proj, pw, sc):
    T = proj.shape[0]
    rblk = _pick(T, (256, 128))
    nblk = T // rblk

    def body(x_ref, pw_ref, sc_ref, y_ref):
        gi = pl.program_id(0)

        def step(i, c):
            r0 = pl.multiple_of(i * rblk, rblk)
            y = _pool_block(_halo_block(x_ref, i, rblk, 16), pw_ref[...], sc_ref[...], r0, gi)
            y_ref[pl.ds(r0, rblk), :] = y.astype(BF16)
            return c

        lax.fori_loop(0, nblk, step, 0)

    return pl.pallas_call(
        body, name=name, grid=(4,),
        in_specs=[pl.BlockSpec((T, 128), lambda g: (0, COL_DX // 128 + g)),
                  pl.BlockSpec((None, 128, 128), lambda g: (g, 0, 0)), pl.BlockSpec((1, 128), lambda g: (0, g))],
        out_specs=pl.BlockSpec((T, 128), lambda g: (0, g)),
        out_shape=jax.ShapeDtypeStruct((T, BW), BF16),
        compiler_params=_cp("parallel"),
    )(proj, pw, sc)


def _pool_bwd(name, proj, dy, pw, sc):
    T = proj.shape[0]
    rblk = _pick(T, (256, 128))
    nblk = T // rblk

    def body(x_ref, dy_ref, pw_ref, sc_ref, dx_ref, dpw_ref, dsc_ref):
        gi = pl.program_id(0)

        def step(it, carry):
            pending, apw, asc = carry
            i = nblk - 1 - it
            r0 = pl.multiple_of(i * rblk, rblk)
            _, vjp = jax.vjp(lambda xf, w, s: _pool_block(xf, w, s, r0, gi), _halo_block(x_ref, i, rblk, 16),
                             pw_ref[...], sc_ref[...])
            dfull, dw, ds = vjp(dy_ref[pl.ds(r0, rblk), :])
            dx_ref[pl.ds(r0, rblk), :] = _with_halo_grad(dfull, pending, 16, rblk).astype(BF16)
            return dfull[:16], apw + dw, asc + ds

        _, apw, asc = lax.fori_loop(0, nblk, step, (jnp.zeros((16, 128), F32), jnp.zeros((128, 128), F32),
                                                    jnp.zeros((1, 128), F32)))
        dpw_ref[...] = apw
        dsc_ref[...] = asc

    col = pl.BlockSpec((T, 128), lambda g: (0, g))
    mat = pl.BlockSpec((None, 128, 128), lambda g: (g, 0, 0))
    vec = pl.BlockSpec((1, 128), lambda g: (0, g))
    return pl.pallas_call(
        body, name=name, grid=(4,),
        in_specs=[pl.BlockSpec((T, 128), lambda g: (0, COL_DX // 128 + g)), col, mat, vec],
        out_specs=[col, mat, vec],
        out_shape=[jax.ShapeDtypeStruct((T, BW), BF16), jax.ShapeDtypeStruct((4, 128, 128), F32),
                   jax.ShapeDtypeStruct((1, BW), F32)],
        compiler_params=_cp("parallel"),
    )(proj, dy, pw, sc)


@jax.custom_vjp
def _dot3(a, b):
    ah = a.astype(BF16)
    al = (a - ah.astype(F32)).astype(BF16)
    bh = b.astype(BF16)
    bl = (b - bh.astype(F32)).astype(BF16)

    def d(x, y):
        return lax.dot_general(x, y, _NN, preferred_element_type=F32)

    return d(ah, bh) + (d(ah, bl) + d(al, bh))


def _dot3_fwd(a, b):
    return _dot3(a, b), (a, b)


def _dot3_bwd(res, g):
    a, b = res
    return _dot(g, b, _NT), _dot(a, g, _TN)


_dot3.defvjp(_dot3_fwd, _dot3_bwd)


def _pad_rows2(x):
    return jnp.concatenate([x, jnp.zeros_like(x)], axis=0)


def _tri_inv(mats):
    n = mats[0].shape[0]
    eye = (lax.broadcasted_iota(jnp.int32, (n, n), 0) == lax.broadcasted_iota(jnp.int32, (n, n), 1)).astype(F32)
    ps = [eye - a for a in mats]
    ms = list(mats)
    k = 2
    while k < n:
        ms = [_dot3(t, t) for t in ms]
        ps = [p + _dot3(p, t) for p, t in zip(ps, ms)]
        k *= 2
    return ps


def _cumsum_rows(x):
    n = x.shape[0]
    row = lax.broadcasted_iota(jnp.int32, x.shape, 0)
    k = 1
    while k < n:
        x = x + jnp.where(row >= k, _shift(x, k), 0.0)
        k *= 2
    return x


def _gdn_prep(qcs, kcs, vcs, tails, alog, dtb):
    C = CHUNK
    pairs = [(c, h) for c in range(len(qcs)) for h in range(4)]
    lane = lax.broadcasted_iota(jnp.int32, (C, 128), 1)
    row = lax.broadcasted_iota(jnp.int32, (C, 128), 0)
    incl = row >= lane
    sig = [_sigmoid(t) for t in tails]
    gfull = [-jnp.exp(alog) * _softplus(t + dtb) for t in tails]
    beta = [jnp.sum(jnp.where(lane == h, sig[c], 0.0), axis=1, keepdims=True) for c, h in pairs]
    g = [jnp.sum(jnp.where(lane == h + 4, gfull[c], 0.0), axis=1, keepdims=True) for c, h in pairs]
    qs = [qcs[c][:, h * 128:(h + 1) * 128] for c, h in pairs]
    ks = [kcs[c][:, h * 128:(h + 1) * 128] for c, h in pairs]
    vs = [vcs[c][:, h * 128:(h + 1) * 128] for c, h in pairs]
    q = [t * lax.rsqrt(jnp.sum(t * t, axis=-1, keepdims=True) + EPS) * (GDN_DK ** -0.5) for t in qs]
    k = [t * lax.rsqrt(jnp.sum(t * t, axis=-1, keepdims=True) + EPS) for t in ks]
    gc = [_cumsum_rows(jnp.broadcast_to(t, (C, 128))) for t in g]
    gc_t = [jnp.transpose(jnp.concatenate([t, t], axis=0)) for t in gc]
    gc_col = [jnp.sum(jnp.where(lane == 0, t, 0.0), axis=1, keepdims=True) for t in gc]
    ri = lax.broadcasted_iota(jnp.int32, (C, C), 0)
    ci = lax.broadcasted_iota(jnp.int32, (C, C), 1)
    decay = [jnp.exp(jnp.where(incl, a - b[:C, :], -1e30)) for a, b in zip(gc, gc_t)]
    decay_sq = [jnp.exp(jnp.where(ri > ci, a - jnp.transpose(b)[:C, :], -1e30)) for a, b in zip(gc_col, gc)]
    kb = [a * b for a, b in zip(k, beta)]
    kk = [_dot(a, b, _NT) for a, b in zip(kb, k)]
    t_mat = _tri_inv([jnp.where(ri > ci, a * b, 0.0) for a, b in zip(kk, decay_sq)])
    egc = [jnp.exp(t) for t in gc]
    u = [_dot(t, a * b) for t, a, b in zip(t_mat, vs, beta)]
    w = [_dot(t, a * b) for t, a, b in zip(t_mat, kb, egc)]
    qk = [_dot(a, _pad_rows2(b), _NT) for a, b in zip(q, k)]
    attn = [jnp.where(incl, a * b, 0.0) for a, b in zip(qk, decay)]
    g_last = [jnp.sum(jnp.where(row == C - 1, t, 0.0), axis=0, keepdims=True) for t in gc]
    qe = [a * b for a, b in zip(q, egc)]
    kd = [a * jnp.exp(b - c_) for a, b, c_ in zip(k, g_last, gc)]
    egl = [jnp.exp(t) for t in g_last]

    def per_chunk(vals):
        return [jnp.concatenate(vals[4 * c:4 * c + 4], axis=1) for c in range(len(qcs))]

    return tuple(per_chunk(t) for t in (u, w, qe, kd, attn, egl))


def _gdn_scan_chunk(states, u, w, qe, kd, attn, egl, z, ng):
    hs = range(4)

    def sl(t, h):
        return t[:, h * 128:(h + 1) * 128]

    ws = [_dot(sl(w, h), states[h]) for h in hs]
    qs = [_dot(sl(qe, h), states[h]) for h in hs]
    v_new = [sl(u, h) - ws[h] for h in hs]
    av = [_dot(sl(attn, h), _pad_rows2(v_new[h])) for h in hs]
    kv = [_dot(sl(kd, h), v_new[h], _TN) for h in hs]
    nxt = tuple(states[h] * sl(egl, h) + kv[h] for h in hs)
    o = [qs[h] + av[h] for h in hs]
    on = [t * lax.rsqrt(jnp.mean(t * t, axis=-1, keepdims=True) + EPS) * ng for t in o]
    return nxt, jnp.concatenate(on, axis=1) * _silu(z)


def _gdn_blocks(T):
    tb = _pick(T, (512, 256, 128, 64))
    return tb, T // tb, tb // CHUNK


PREP_CHUNKS = 4


def _chunk_rows(i, n):
    return [pl.ds(pl.multiple_of((i * n + j) * CHUNK, CHUNK), CHUNK) for j in range(n)]


def _egl_rows(i, n, size):
    return [pl.ds(pl.multiple_of((i * n + j) * 8, 8), size) for j in range(n)]


def _gdn_prep_fwd(name, qa, ka, va, proj, alog, dtb):
    T = proj.shape[0]
    tb, nb, ncb = _gdn_blocks(T)
    n = PREP_CHUNKS if ncb % PREP_CHUNKS == 0 else 1

    def body(q_ref, k_ref, v_ref, tail_ref, alog_ref, dtb_ref, u_ref, w_ref, qe_ref, kd_ref, at_ref, egl_ref):
        def step(i, c):
            rows = _chunk_rows(i, n)
            u, w, qe, kd, at, egl = _gdn_prep([q_ref[r, :] for r in rows], [k_ref[r, :] for r in rows],
                                              [v_ref[r, :] for r in rows], [tail_ref[r, :] for r in rows],
                                              alog_ref[...], dtb_ref[...])
            for j, (r, e) in enumerate(zip(rows, _egl_rows(i, n, 8))):
                u_ref[r, :] = u[j]
                w_ref[r, :] = w[j].astype(BF16)
                qe_ref[r, :] = qe[j].astype(BF16)
                kd_ref[r, :] = kd[j].astype(BF16)
                at_ref[r, :] = at[j].astype(BF16)
                egl_ref[e, :] = jnp.broadcast_to(egl[j], (8, BW))
            return c

        lax.fori_loop(0, ncb // n, step, 0)

    blk = pl.BlockSpec((tb, BW), lambda j: (j, 0))
    vec = pl.BlockSpec((1, 128), lambda j: (0, 0))
    return pl.pallas_call(
        body, name=name, grid=(nb,),
        in_specs=[blk, blk, blk, pl.BlockSpec((tb, 128), lambda j: (j, COL_TAIL // 128)), vec, vec],
        out_specs=[blk] * 5 + [pl.BlockSpec((ncb * 8, BW), lambda j: (j, 0))],
        out_shape=[jax.ShapeDtypeStruct((T, BW), F32)] + [jax.ShapeDtypeStruct((T, BW), BF16)] * 4
        + [jax.ShapeDtypeStruct((T // 8, BW), F32)],
        compiler_params=_cp("parallel"),
    )(qa, ka, va, proj, alog, dtb)


def _gdn_prep_bwd(name, qa, ka, va, proj, alog, dtb, du, dw, dqe, dkd, dat, degl):
    T = proj.shape[0]
    tb, nb, ncb = _gdn_blocks(T)
    n = PREP_CHUNKS if ncb % PREP_CHUNKS == 0 else 1

    def body(q_ref, k_ref, v_ref, tail_ref, alog_ref, dtb_ref, du_ref, dw_ref, dqe_ref, dkd_ref, dat_ref, degl_ref,
             dq_ref, dk_ref, dv_ref, dtail_ref, dalog_ref, ddtb_ref):
        first = pl.program_id(0) == 0

        def step(i, carry):
            pa, pd = carry
            rows = _chunk_rows(i, n)
            _, vjp = jax.vjp(_gdn_prep, [q_ref[r, :] for r in rows], [k_ref[r, :] for r in rows],
                             [v_ref[r, :] for r in rows], [tail_ref[r, :] for r in rows], alog_ref[...], dtb_ref[...])
            cot = tuple([ref[r, :] for r in rows] for ref in (du_ref, dw_ref, dqe_ref, dkd_ref, dat_ref))
            dq, dk, dv, dtail, da, dd = vjp(cot + ([degl_ref[e, :] for e in _egl_rows(i, n, 1)],))
            for j, r in enumerate(rows):
                dq_ref[r, :] = dq[j]
                dk_ref[r, :] = dk[j]
                dv_ref[r, :] = dv[j]
                dtail_ref[r, :] = dtail[j].astype(BF16)
            return pa + da, pd + dd

        zv = jnp.zeros((1, 128), F32)
        pa, pd = lax.fori_loop(0, ncb // n, step, (zv, zv))

        @pl.when(first)
        def _():
            dalog_ref[...] = pa
            ddtb_ref[...] = pd

        @pl.when(jnp.logical_not(first))
        def _():
            dalog_ref[...] += pa
            ddtb_ref[...] += pd

    blk = pl.BlockSpec((tb, BW), lambda j: (j, 0))
    vec = pl.BlockSpec((1, 128), lambda j: (0, 0))
    return pl.pallas_call(
        body, name=name, grid=(nb,),
        in_specs=[blk, blk, blk, pl.BlockSpec((tb, 128), lambda j: (j, COL_TAIL // 128)), vec, vec]
        + [blk] * 5 + [pl.BlockSpec((ncb * 8, BW), lambda j: (j, 0))],
        out_specs=[blk, blk, blk, pl.BlockSpec((tb, 128), lambda j: (j, 0)), vec, vec],
        out_shape=[jax.ShapeDtypeStruct((T, BW), F32)] * 3 + [jax.ShapeDtypeStruct((T, 128), BF16)]
        + [jax.ShapeDtypeStruct((1, 128), F32)] * 2,
        compiler_params=_cp("arbitrary"),
    )(qa, ka, va, proj, alog, dtb, du, dw, dqe, dkd, dat, degl)


def _gdn_fwd(name, u, w, qe, kd, at, egl, proj, ng):
    T = proj.shape[0]
    tb, nb, ncb = _gdn_blocks(T)

    def body(u_ref, w_ref, qe_ref, kd_ref, at_ref, egl_ref, z_ref, ng_ref, y_ref, sh_ref, state):
        @pl.when(pl.program_id(0) == 0)
        def _():
            state[...] = jnp.zeros((4, 128, 128), F32)

        def step(c, states):
            rows = pl.ds(pl.multiple_of(c * CHUNK, CHUNK), CHUNK)
            for h in range(4):
                sh_ref[h, c] = states[h]
            nxt, y = _gdn_scan_chunk(states, u_ref[rows, :], w_ref[rows, :], qe_ref[rows, :], kd_ref[rows, :],
                                     at_ref[rows, :], egl_ref[pl.ds(pl.multiple_of(c * 8, 8), 1), :], z_ref[rows, :],
                                     ng_ref[...])
            y_ref[rows, :] = y.astype(BF16)
            return nxt

        states = lax.fori_loop(0, ncb, step, tuple(state[h] for h in range(4)))
        for h in range(4):
            state[h] = states[h]

    blk = pl.BlockSpec((tb, BW), lambda j: (j, 0))
    vec = pl.BlockSpec((1, 128), lambda j: (0, 0))
    return pl.pallas_call(
        body, name=name, grid=(nb,),
        in_specs=[blk] * 5 + [pl.BlockSpec((ncb * 8, BW), lambda j: (j, 0)),
                              pl.BlockSpec((tb, BW), lambda j: (j, COL_CZ // BW)), vec],
        out_specs=[blk, pl.BlockSpec((4, ncb, 128, 128), lambda j: (0, j, 0, 0))],
        out_shape=[jax.ShapeDtypeStruct((T, BW), BF16), jax.ShapeDtypeStruct((4, T // CHUNK, 128, 128), F32)],
        scratch_shapes=[pltpu.VMEM((4, 128, 128), F32)],
        compiler_params=_cp("arbitrary"),
    )(u, w, qe, kd, at, egl, proj, ng)


def _gdn_bwd(name, u, w, qe, kd, at, egl, proj, dy, sh, ng):
    T = proj.shape[0]
    tb, nb, ncb = _gdn_blocks(T)

    def body(u_ref, w_ref, qe_ref, kd_ref, at_ref, egl_ref, z_ref, dy_ref, sh_ref, ng_ref,
             du_ref, dw_ref, dqe_ref, dkd_ref, dat_ref, degl_ref, dz_ref, dng_ref, dstate):
        first = pl.program_id(0) == 0

        @pl.when(first)
        def _():
            dstate[...] = jnp.zeros((4, 128, 128), F32)

        def step(it, carry):
            dstates, pn = carry
            c = ncb - 1 - it
            rows = pl.ds(pl.multiple_of(c * CHUNK, CHUNK), CHUNK)
            erow = pl.multiple_of(c * 8, 8)
            _, vjp = jax.vjp(_gdn_scan_chunk, tuple(sh_ref[h, c] for h in range(4)), u_ref[rows, :],
                             w_ref[rows, :].astype(F32), qe_ref[rows, :].astype(F32), kd_ref[rows, :].astype(F32),
                             at_ref[rows, :].astype(F32), egl_ref[pl.ds(erow, 1), :], z_ref[rows, :], ng_ref[...])
            nxt, du, dw, dqe, dkd, dat, degl, dz, dn = vjp((dstates, dy_ref[rows, :]))
            du_ref[rows, :] = du
            dw_ref[rows, :] = dw
            dqe_ref[rows, :] = dqe
            dkd_ref[rows, :] = dkd
            dat_ref[rows, :] = dat
            degl_ref[pl.ds(erow, 8), :] = jnp.broadcast_to(degl, (8, BW))
            dz_ref[rows, :] = dz.astype(BF16)
            return nxt, pn + dn

        dstates, pn = lax.fori_loop(0, ncb, step, (tuple(dstate[h] for h in range(4)), jnp.zeros((1, 128), F32)))
        for h in range(4):
            dstate[h] = dstates[h]

        @pl.when(first)
        def _():
            dng_ref[...] = pn

        @pl.when(jnp.logical_not(first))
        def _():
            dng_ref[...] += pn

    blk = pl.BlockSpec((tb, BW), lambda j: (nb - 1 - j, 0))
    eblk = pl.BlockSpec((ncb * 8, BW), lambda j: (nb - 1 - j, 0))
    vec = pl.BlockSpec((1, 128), lambda j: (0, 0))
    return pl.pallas_call(
        body, name=name, grid=(nb,),
        in_specs=[blk] * 5 + [eblk, pl.BlockSpec((tb, BW), lambda j: (nb - 1 - j, COL_CZ // BW)), blk,
                              pl.BlockSpec((4, ncb, 128, 128), lambda j: (0, nb - 1 - j, 0, 0)), vec],
        out_specs=[blk] * 5 + [eblk, blk, vec],
        out_shape=[jax.ShapeDtypeStruct((T, BW), F32)] * 5 + [jax.ShapeDtypeStruct((T // 8, BW), F32),
                                                              jax.ShapeDtypeStruct((T, BW), BF16),
                                                              jax.ShapeDtypeStruct((1, 128), F32)],
        scratch_shapes=[pltpu.VMEM((4, 128, 128), F32)],
        compiler_params=_cp("arbitrary"),
    )(u, w, qe, kd, at, egl, proj, dy, sh, ng)


def _adamw_update(w_ref, g_ref, m_ref, v_ref, d_ref, nm_ref, nv_ref):
    gv = g_ref[...]
    m2 = ADAM_B1 * m_ref[...] + (1.0 - ADAM_B1) * gv
    v2 = ADAM_B2 * v_ref[...] + (1.0 - ADAM_B2) * jnp.square(gv)
    m_hat = m2 / (1.0 - ADAM_B1 ** ADAM_STEP)
    v_hat = v2 / (1.0 - ADAM_B2 ** ADAM_STEP)
    d_ref[...] = -ADAM_LR * (m_hat / (jnp.sqrt(v_hat) + ADAM_EPS) + ADAM_WD * w_ref[...])
    nm_ref[...] = m2
    nv_ref[...] = v2


def _adamw_many(name, ws, gs, ms, vs):
    n = len(ws)

    def body(*refs):
        for i in range(n):
            _adamw_update(*[refs[k * n + i] for k in range(7)])

    return pl.pallas_call(
        body, name=name,
        out_shape=[jax.ShapeDtypeStruct(a.shape, F32) for a in ws] * 3,
        compiler_params=_cp(),
    )(*ws, *gs, *ms, *vs)


def _adamw(name, w, g, m, v):
    R, C = w.shape
    br = _pick(R, (512, 256, 240, 128, 64, 8))
    body = functools.partial(_adamw_update)
    spec = pl.BlockSpec((br, C), lambda i: (i, 0))
    return pl.pallas_call(
        body, name=name, grid=(R // br,),
        in_specs=[spec] * 4, out_specs=[spec] * 3,
        out_shape=[jax.ShapeDtypeStruct((R, C), F32)] * 3,
        compiler_params=_cp("parallel"),
    )(w, g, m, v)


def _sum8(name, parts):
    _, R, C = parts.shape
    br = _pick(R, (352, 368, 256, 128, 64, 16, 8))

    def body(p_ref, o_ref):
        acc = p_ref[0].astype(F32)
        for d in range(1, N_DEV):
            acc = acc + p_ref[d].astype(F32)
        o_ref[...] = acc

    return pl.pallas_call(
        body, name=name, grid=(R // br,),
        in_specs=[pl.BlockSpec((N_DEV, br, C), lambda i: (0, i, 0))],
        out_specs=pl.BlockSpec((br, C), lambda i: (i, 0)),
        out_shape=jax.ShapeDtypeStruct((R, C), F32),
        compiler_params=_cp("parallel"),
    )(parts)


_ANY = pl.BlockSpec(memory_space=pl.ANY)
_MESH = pl.DeviceIdType.MESH


def _all_gather(name, shard):
    R, C = shard.shape

    def body(x_ref, out_ref, send_sems, recv_sems, local_sem):
        x, y, c = lax.axis_index("x"), lax.axis_index("y"), lax.axis_index("c")
        me, sibling = (x, y, c), (x, y, 1 - c)
        chips = [(1 - x, y), (x, 1 - y), (1 - x, 1 - y)]

        def slot(px, py, pc):
            return out_ref.at[4 * px + 2 * py + pc]

        def copy(k, block, to, src=None):
            return pltpu.make_async_remote_copy(
                src_ref=slot(*block) if src is None else src, dst_ref=slot(*block),
                send_sem=send_sems.at[k], recv_sem=recv_sems.at[k], device_id=to, device_id_type=_MESH)

        mine = pltpu.make_async_copy(x_ref, slot(*me), local_sem)
        mine.start()
        first = [copy(0, me, sibling, src=x_ref)]
        first += [copy(1 + j, me, (*chip, c), src=x_ref) for j, chip in enumerate(chips)]
        for cp in first:
            cp.start()
        passed = [copy(4 + j, (*chip, c), sibling) for j, chip in enumerate(chips)]
        for j, chip in enumerate(chips):
            copy(1 + j, (*chip, c), me).wait_recv()
            passed[j].start()
        copy(0, sibling, me).wait_recv()
        for j, chip in enumerate(chips):
            copy(4 + j, (*chip, 1 - c), me).wait_recv()
        for cp in first + passed:
            cp.wait_send()
        mine.wait()

    return pl.pallas_call(
        body, name=name,
        in_specs=[_ANY], out_specs=_ANY,
        out_shape=jax.ShapeDtypeStruct((N_DEV, R, C), shard.dtype),
        scratch_shapes=[pltpu.SemaphoreType.DMA((7,)), pltpu.SemaphoreType.DMA((7,)), pltpu.SemaphoreType.DMA],
    )(shard)


_HBM = pl.BlockSpec(memory_space=pltpu.HBM)
_SEM = pl.BlockSpec(memory_space=pltpu.SEMAPHORE)
_EFFECT = pltpu.SideEffectType.DATAFLOW_SIDE_EFFECTING


def _exchange_copies(src_ref, land_ref, send_sems, recv_sems, scatter):
    x, y, c = lax.axis_index("x"), lax.axis_index("y"), lax.axis_index("c")
    me = 4 * x + 2 * y + c
    copies = []
    for k in range(1, N_DEV):
        px, py, pc = x ^ ((k >> 2) & 1), y ^ ((k >> 1) & 1), c ^ (k & 1)
        src = src_ref.at[4 * px + 2 * py + pc] if scatter else src_ref
        copies.append(pltpu.make_async_remote_copy(
            src_ref=src, dst_ref=land_ref.at[me], send_sem=send_sems.at[k - 1], recv_sem=recv_sems.at[k - 1],
            device_id=(px, py, pc), device_id_type=_MESH))
    return copies


def _exchange_start(name, srcs, lands, scatter, after=None):
    n = len(srcs)

    def body(*refs):
        src_refs, land_refs = refs[:n], refs[n:2 * n]
        outs = refs[2 * n + (after is not None):]
        send, recv = outs[:n], outs[n:2 * n]
        token = refs[-1]
        for g in range(n):
            for cp in _exchange_copies(src_refs[g], land_refs[g], send[g], recv[g], scatter):
                cp.start()
        token[...] = jnp.zeros_like(token)

    outs = pl.pallas_call(
        body, name=name,
        out_shape=tuple([pltpu.SemaphoreType.DMA((N_DEV - 1,))] * (2 * n)
                        + [pltpu.HBM(a.shape, a.dtype) for a in list(srcs) + list(lands)]
                        + [jax.ShapeDtypeStruct((8, 128), F32)]),
        in_specs=[_HBM] * (2 * n) + [_ANY] * (after is not None),
        out_specs=tuple([_SEM] * (2 * n) + [_HBM] * (2 * n) + [pl.BlockSpec(memory_space=pltpu.VMEM)]),
        input_output_aliases={i: 2 * n + i for i in range(2 * n)},
        compiler_params=pltpu.CompilerParams(has_side_effects=_EFFECT),
    )(*[pltpu.with_memory_space_constraint(a, pltpu.HBM) for a in list(srcs) + list(lands)],
      *([after] if after is not None else []))
    handles = [(outs[2 * n + g], outs[3 * n + g], outs[g], outs[n + g]) for g in range(n)]
    return handles, outs[-1]


def _exchange_wait(name, handles, after, scatter):
    n = len(handles)
    srcs, lands, sends, recvs = ([h[i] for h in handles] for i in range(4))

    def body(*refs):
        src_refs, land_refs = refs[:n], refs[n:2 * n]
        send, recv = refs[2 * n:3 * n], refs[3 * n:4 * n]
        for g in range(n):
            for cp in _exchange_copies(src_refs[g], land_refs[g], send[g], recv[g], scatter):
                cp.wait_send()
                cp.wait_recv()

    outs = pl.pallas_call(
        body, name=name,
        out_shape=tuple(pltpu.HBM(a.shape, a.dtype) for a in srcs + lands),
        in_specs=tuple([_HBM] * (2 * n) + [_SEM] * (2 * n) + [_ANY]), out_specs=tuple([_HBM] * (2 * n)),
        input_output_aliases={i: i for i in range(2 * n)},
        compiler_params=pltpu.CompilerParams(has_side_effects=_EFFECT),
    )(*srcs, *lands, *sends, *recvs, after)
    return list(outs[n:])


def _rows(a):
    return a.reshape(-1, 1024)


def _rows_to_parts(full):
    n = full.shape[-2] // N_DEV
    t = full.reshape(full.shape[:-2] + (N_DEV, n, full.shape[-1]))
    return jnp.moveaxis(t, -3, 0)


def _parts_to_rows(parts):
    t = jnp.moveaxis(parts, 0, -3)
    return t.reshape(t.shape[:-3] + (t.shape[-3] * t.shape[-2], t.shape[-1]))


def _parts_to_cols(parts):
    t = jnp.moveaxis(parts, 0, -2)
    return t.reshape(t.shape[:-2] + (t.shape[-2] * t.shape[-1],))


def _join(parts, axis=0):
    total = sum(p.shape[axis] for p in parts)
    out, off = None, 0
    for p in parts:
        cfg = [(0, 0)] * p.ndim
        cfg[axis] = (off, total - off - p.shape[axis])
        t = jnp.pad(p, cfg)
        out = t if out is None else out + t
        off += p.shape[axis]
    return out


def _w_in_to_layout(w):
    tail = jnp.pad(w[4096:4104], ((0, PW - COL_TAIL - 8), (0, 0)))
    return jnp.concatenate([w[:4096], w[4104:P_IN], tail], axis=0)


def _w_in_from_layout(g):
    return _join([g[:4096], g[COL_TAIL:COL_TAIL + 8], g[4096:COL_TAIL]], axis=0)


def _block_diag(w):
    w = w.reshape(4, 2, 64, 64)
    return jnp.pad(w[:, 0], ((0, 0), (0, 64), (0, 64))) + jnp.pad(w[:, 1], ((0, 0), (64, 0), (64, 0)))


def _block_diag_grad(g):
    return jnp.stack([g[:, :64, :64], g[:, 64:, 64:]], axis=1).reshape(8, 64, 64)


def _ffn_forward(tag, x, norm, wg, wu, wd):
    h = _rms_fwd(tag + "_norm", x, norm)
    a, b, act = _ffn_up(tag + "_up", h, wg, wu)
    x_out = _mm(tag + "_down", [(act, wd)], "nn", F32, res=x, scale=0.5)
    return x_out, (x, h, a, b, act)


def _ffn_backward(tag, dx_out, saved, norm, wg, wu, wd, put, names, split=False):
    x, h, a, b, act = saved
    n_wg, n_wu, n_wd = names
    dwd = _mm(tag + "_dwd", [(act, dx_out)], "tn", BF16, scale=0.5, bm=FF // 2)
    tok = put({n_wd: dwd}) if split else None
    da, db = _ffn_dact(tag + "_dact", dx_out, wd, a, b, after=tok)
    dwg = _mm(tag + "_dwg", [(da, h)], "tn", BF16, bm=FF // 2)
    if split:
        tok = tok + put({n_wg: dwg})
    dwu = _mm(tag + "_dwu", [(db, h)], "tn", BF16, bm=FF // 2, after=tok)
    tok = tok + put({n_wu: dwu}) if split else put({n_wg: dwg, n_wu: dwu, n_wd: dwd})
    dh = _mm(tag + "_dh", [(da, wg), (db, wu)], "nn", F32, after=tok)
    dx, dnorm = _rms_bwd(tag + "_dnorm", x, norm + tok, dh, dx_out)
    return dx, dnorm


def _mixer_params(p):
    alog = jnp.pad(p["gdn_a_log"], (4, 120))[None]
    dtb = jnp.pad(p["gdn_dt_bias"], (4, 120))[None]
    bias = jnp.repeat(p["sgu_b"].T, 128, axis=1)
    return dict(
        ln_g=p["sgu_ln_g"][None], ln_b=p["sgu_ln_b"][None], sgu_w=p["sgu_w"], sgu_bias=bias,
        lru_cw=p["lru_conv_w"], lru_cb=p["lru_conv_b"][None], wa=_block_diag(p["lru_wa"]), ba=p["lru_ba"][None],
        wx=_block_diag(p["lru_wx"]), bx=p["lru_bx"][None], lam=p["lru_lambda"][None],
        gdn_cw=p["gdn_conv_w"], alog=alog, dtb=dtb, ng=p["gdn_norm_g"][None],
        pool_w=p["pool_w"], pool_sc=p["pool_scale"][None])


def _mix_forward(tag, x, p, mp):
    h = _rms_fwd(tag + "_norm", x, p["mix_norm"][None])
    proj = _mm(tag + "_proj", [(h, p["w_in"])], "nt", F32, bm=_pick(x.shape[0], (2048, 1024, 512, 256, 128)))
    y_a = _sgu_fwd(tag + "_sgu", proj, mp["ln_g"], mp["ln_b"], mp["sgu_w"], mp["sgu_bias"])
    y_b, hc = _lru_fwd(tag + "_lru", proj, mp["lru_cw"], mp["lru_cb"], mp["wa"], mp["ba"], mp["wx"], mp["bx"],
                       mp["lam"])
    qa = _conv_fwd(tag + "_convq", proj, COL_CQ, mp["gdn_cw"], 0)
    ka = _conv_fwd(tag + "_convk", proj, COL_CK, mp["gdn_cw"], 512)
    va = _conv_fwd(tag + "_convv", proj, COL_CV, mp["gdn_cw"], 1024)
    prep = _gdn_prep_fwd(tag + "_gdnprep", qa, ka, va, proj, mp["alog"], mp["dtb"])
    y_c, sh = _gdn_fwd(tag + "_gdn", *prep, proj, mp["ng"])
    y_d = _pool_fwd(tag + "_pool", proj, mp["pool_w"], mp["pool_sc"])
    ys = (y_a, y_b, y_c, y_d)
    merged = _merge_fwd(tag + "_merge", ys, p["w_branch"], proj)
    x_out = _mm(tag + "_out", [(merged, p["w_out"])], "nn", F32, res=x)
    return x_out, (x, h, proj, hc, qa, ka, va, prep, sh, ys, merged)


def _mix_backward(tag, dx_out, saved, p, mp, put):
    x, h, proj, hc, qa, ka, va, prep, sh, ys, merged = saved
    T = x.shape[0]
    g = {}
    dmerged = _mm(tag + "_dmerged", [(dx_out, p["w_out"])], "nt", F32)
    g["w_out"] = _mm(tag + "_dwout", [(merged, dx_out)], "tn", BF16)
    outs = _merge_bwd(tag + "_dmerge", dmerged, ys, p["w_branch"], proj)
    dgates, dbrs = outs[:NBR], outs[NBR:]
    dys = [_mm(f"{tag}_dy{i}", [(dbrs[i], p["w_branch"][i])], "nn", F32) for i in range(NBR)]
    g["w_branch"] = jnp.stack([_mm(f"{tag}_dwb{i}", [(dbrs[i], ys[i])], "tn", BF16) for i in range(NBR)])

    du, dv, dln_g, dln_b, dsgu_w, dbias = _sgu_bwd(tag + "_dsgu", proj, dys[0], mp["ln_g"], mp["ln_b"], mp["sgu_w"],
                                                  mp["sgu_bias"])
    g["sgu_ln_g"], g["sgu_ln_b"], g["sgu_w"] = dln_g[0], dln_b[0], dsgu_w
    g["sgu_b"] = dbias.reshape(128, 4, 128).sum(axis=2).T

    (dbx, dbg, dcw, dcb, dwa, dba, dwx, dbxb, dlam) = _lru_bwd(
        tag + "_dlru", proj, dys[1], hc, mp["lru_cw"], mp["lru_cb"], mp["wa"], mp["ba"], mp["wx"], mp["bx"], mp["lam"])
    g["lru_conv_w"], g["lru_conv_b"], g["lru_ba"], g["lru_bx"], g["lru_lambda"] = dcw, dcb[0], dba[0], dbxb[0], dlam[0]
    g["lru_wa"], g["lru_wx"] = _block_diag_grad(dwa), _block_diag_grad(dwx)

    *dprep, dz, dng = _gdn_bwd(tag + "_dgdn", *prep, proj, dys[2], sh, mp["ng"])
    dqa, dka, dva, dtail, dalog, ddtb = _gdn_prep_bwd(tag + "_dgdnprep", qa, ka, va, proj, mp["alog"], mp["dtb"], *dprep)
    g["gdn_a_log"], g["gdn_dt_bias"], g["gdn_norm_g"] = dalog[0, 4:8], ddtb[0, 4:8], dng[0]
    dq, dcwq = _conv_bwd(tag + "_dconvq", proj, COL_CQ, dqa, mp["gdn_cw"], 0)
    dk, dcwk = _conv_bwd(tag + "_dconvk", proj, COL_CK, dka, mp["gdn_cw"], 512)
    dv_, dcwv = _conv_bwd(tag + "_dconvv", proj, COL_CV, dva, mp["gdn_cw"], 1024)
    g["gdn_conv_w"] = jnp.concatenate([dcwq, dcwk, dcwv], axis=1)

    dd, dpw, dsc = _pool_bwd(tag + "_dpool", proj, dys[3], mp["pool_w"], mp["pool_sc"])
    g["pool_w"], g["pool_scale"] = dpw, dsc[0]

    dproj = jnp.concatenate([du, dv, dbx, dbg, dq, dk, dv_, dz, dd, *dgates, dtail,
                             jnp.zeros((T, PW - COL_TAIL - 128), BF16)], axis=1)
    dw_in = _mm(tag + "_dwin", [(dproj, h)], "tn", BF16)
    tok = put(dict(w_in=_w_in_from_layout(dw_in), w_branch=g.pop("w_branch"), w_out=g.pop("w_out")))
    dh = _mm(tag + "_dh", [(dproj, p["w_in"])], "nn", F32, bm=_pick(T, (2048, 1024, 512, 256, 128)), after=tok)
    dx, dnorm = _rms_bwd(tag + "_dnorm", x, p["mix_norm"][None] + tok, dh, dx_out)
    g["mix_norm"] = dnorm[0]
    return dx, g


_BIG = ("ff1_wg", "ff1_wu", "ff1_wd", "w_in", "w_branch", "w_out", "ff2_wg", "ff2_wu", "ff2_wd")
_COL_SHARDED = ("ff1_wg", "ff1_wu", "w_in", "w_branch", "ff2_wg", "ff2_wu")
_SMALL = ("ff1_norm", "mix_norm", "sgu_ln_g", "sgu_ln_b", "sgu_w", "sgu_b", "lru_conv_w", "lru_conv_b", "lru_wa",
          "lru_ba", "lru_wx", "lru_bx", "lru_lambda", "gdn_conv_w", "gdn_a_log", "gdn_dt_bias", "gdn_norm_g", "pool_w",
          "pool_scale", "ff2_norm", "final_norm")
_WEIGHTS = ("ff1_norm", "ff1_wg", "ff1_wu", "ff1_wd", "mix_norm", "w_in", "sgu_ln_g", "sgu_ln_b", "sgu_w", "sgu_b",
            "lru_conv_w", "lru_conv_b", "lru_wa", "lru_ba", "lru_wx", "lru_bx", "lru_lambda", "gdn_conv_w", "gdn_a_log",
            "gdn_dt_bias", "gdn_norm_g", "pool_w", "pool_scale", "w_branch", "w_out", "ff2_norm", "ff2_wg", "ff2_wu",
            "ff2_wd", "final_norm")
_CONV_SHARDED = ("lru_conv_w", "gdn_conv_w")
PACK_ROW_ALIGN = 16
_GROUPS = (("ff1", ("ff1_wg", "ff1_wu", "ff1_wd")), ("mix", ("w_in", "w_branch", "w_out")),
           ("ff2", ("ff2_wg", "ff2_wu", "ff2_wd")))


def _pad_rows(a, mult):
    pad = (-a.shape[-2]) % mult
    if pad == 0:
        return a
    return jnp.pad(a, [(0, 0)] * (a.ndim - 2) + [(0, pad), (0, 0)])


def _my_index():
    return 4 * lax.axis_index("x") + 2 * lax.axis_index("y") + lax.axis_index("c")


def _landing(own):
    return lax.dynamic_update_index_in_dim(lax.empty((N_DEV,) + own.shape, own.dtype), own, _my_index(), 0)


def _stored(n, a):
    return jnp.swapaxes(a, -1, -2) if n in _COL_SHARDED else a


def _gather_first(w):
    names = _GROUPS[0][1]
    shards = [_rows(_stored(n, w[n][0]).astype(BF16)) for n in names]
    got = _all_gather("gather_first", jnp.concatenate(shards, axis=0))
    out, r = {}, 0
    for n, s in zip(names, shards):
        out[n] = got[:, r:r + s.shape[0]].reshape(-1, 1024)
        r += s.shape[0]
    return out, got


def _gather_start(w, after):
    conv = _pad_rows(jnp.concatenate([w[n].reshape(1, -1) for n in _CONV_SHARDED], axis=1), 8)
    keys, srcs = ["conv"], [conv]
    for l in range(2):
        for sub, (_, names) in enumerate(_GROUPS):
            if (l, sub) != (0, 0):
                for n in names:
                    keys.append((l, sub, n))
                    srcs.append(_stored(n, w[n][l]).astype(BF16))
    handles, token = _exchange_start("gather_start", srcs, [_landing(s) for s in srcs], scatter=False, after=after)
    return dict(zip(keys, handles)), token


def _gather_finish(l, sub, handles, first, after):
    names = _GROUPS[sub][1]
    if (l, sub) == (0, 0):
        out = dict(first)
    else:
        lands = _exchange_wait(f"gather_wait_{l}{sub}", [handles[(l, sub, n)] for n in names], after, scatter=False)
        out = {n: _parts_to_rows(land) for n, land in zip(names, lands)}
    if "w_in" in out:
        out["w_in"] = _w_in_to_layout(out["w_in"])
    return out


def _scatter_start(l, sub, grads):
    srcs, shapes = [], []
    for n in grads:
        parts = _rows_to_parts(grads[n])
        shapes.append(parts.shape[1:])
        srcs.append(_pad_rows(parts.reshape(N_DEV, -1, 1024), PACK_ROW_ALIGN))
    me = _my_index()
    lands = [_landing(lax.dynamic_index_in_dim(s, me, 0, keepdims=False)) for s in srcs]
    tag = f"{l}{sub}" + ("" if len(grads) == len(_GROUPS[sub][1]) else "_" + "_".join(grads))
    handles, token = _exchange_start(f"scatter_start_{tag}", srcs, lands, scatter=True)
    return handles, (tag, tuple(grads), shapes), token


def _scatter_finish(l, sub, handles, meta, after):
    tag, names, shapes = meta
    lands = _exchange_wait(f"scatter_wait_{tag}", handles, after, scatter=True)
    out = {}
    for n, land, shape in zip(names, lands, shapes):
        size = 1
        for s in shape:
            size *= s
        summed = _sum8(f"sum_{l}{sub}_{n}", land)
        out[n] = _stored(n, summed[:size // 1024].reshape(shape))
    return out


def _gather_conv_finish(w, handles, after):
    gconv = _exchange_wait("gather_wait_conv", [handles["conv"]], after, scatter=False)[0][:, 0]
    full, r = {}, 0
    for n in _CONV_SHARDED:
        sz = w[n].size
        full[n] = _parts_to_cols(gconv[:, r:r + sz].reshape((N_DEV,) + w[n].shape))
        r += sz
    return full


def _forward_backward(x, tgt, w, conv, get_weights, put_grads, put_small, token):
    saved, params = [], []
    for l in range(2):
        p = {n: w[n][l] for n in _SMALL if n != "final_norm"}
        for n in _CONV_SHARDED:
            p[n] = conv[n][l]
        mp = _mixer_params(p)
        tok = token[:1, :1] if l == 0 else 0.0
        p.update(get_weights(l, 0, x))
        x, s1 = _ffn_forward(f"l{l}_ff1", x, p["ff1_norm"][None] + tok, p["ff1_wg"], p["ff1_wu"], p["ff1_wd"])
        p.update(get_weights(l, 1, x))
        x, s2 = _mix_forward(f"l{l}_mix", x, p, mp)
        p.update(get_weights(l, 2, x))
        x, s3 = _ffn_forward(f"l{l}_ff2", x, p["ff2_norm"][None], p["ff2_wg"], p["ff2_wu"], p["ff2_wd"])
        saved.append((s1, s2, s3))
        params.append((p, mp))
    loss, dx, dfinal = _final_loss("loss_head", x, w["final_norm"][None], tgt)
    tok = 0.0
    for l in (1, 0):
        p, mp = params[l]
        s1, s2, s3 = saved[l]
        g = {}

        def put(sub):
            return lambda grads, l=l: put_grads(l, sub, grads)[:1, :1]

        dx, dn = _ffn_backward(f"l{l}_ff2", dx, s3, p["ff2_norm"][None] + tok, p["ff2_wg"], p["ff2_wu"], p["ff2_wd"],
                               put(2), _GROUPS[2][1])
        g["ff2_norm"] = dn[0]
        dx, gm = _mix_backward(f"l{l}_mix", dx, s2, p, mp, put(1))
        g.update(gm)
        tok = 0.0
        if l == 0:
            tok = put_small("0a", g)[:1, :1]
            g = {}
        dx, dn = _ffn_backward(f"l{l}_ff1", dx, s1, p["ff1_norm"][None] + tok, p["ff1_wg"], p["ff1_wu"], p["ff1_wd"],
                               put(0), _GROUPS[0][1], split=(l == 0))
        g["ff1_norm"] = dn[0]
        if l == 1:
            g["final_norm"] = dfinal[0]
            g["loss"] = loss[0, :1]
        tok = put_small("1" if l == 1 else "0b", g)[:1, :1]
    return dx


SMALL_PIECE = 8 * 1024


def _pack_small(d, names):
    pieces = []
    for n in names:
        flat = d[n].reshape(-1)
        pieces.append(jnp.pad(flat, (0, (-flat.size) % SMALL_PIECE)).reshape(-1, 1024))
    return jnp.concatenate(pieces, axis=0)


def _unpack_small(pack, shapes, names):
    out, r = {}, 0
    for n in names:
        size = 1
        for s in shapes[n]:
            size *= s
        rows = -(-size // SMALL_PIECE) * 8
        out[n] = pack[r:r + rows].reshape(-1)[:size].reshape(shapes[n])
        r += rows
    return out


def _small_names(grads):
    return tuple(n for n in _SMALL + ("loss",) if n in grads)


def _small_start(tag, grads):
    pack = _pack_small(grads, _small_names(grads))
    handles, token = _exchange_start(f"small_start_{tag}", [pack], [_landing(pack)], scatter=False)
    return handles, {n: grads[n].shape for n in _small_names(grads)}, token


def _small_finish(tag, handles, shapes, after):
    landed = _exchange_wait(f"small_wait_{tag}", handles, after, scatter=False)[0]
    return _unpack_small(_sum8(f"sum_small_{tag}", landed), shapes, _small_names(shapes))


def _as2d(a):
    if a.ndim == 1:
        return a.reshape(1, -1)
    return a.reshape(-1, a.shape[-1])


def kernel(x, ff1_norm, ff1_wg, ff1_wu, ff1_wd, mix_norm, w_in, sgu_ln_g, sgu_ln_b, sgu_w, sgu_b, lru_conv_w, lru_conv_b, lru_wa, lru_ba, lru_wx, lru_bx, lru_lambda, gdn_conv_w, gdn_a_log, gdn_dt_bias, gdn_norm_g, pool_w, pool_scale, w_branch, w_out, ff2_norm, ff2_wg, ff2_wu, ff2_wd, final_norm, loss_target, m_ff1_norm, m_ff1_wg, m_ff1_wu, m_ff1_wd, m_mix_norm, m_w_in, m_sgu_ln_g, m_sgu_ln_b, m_sgu_w, m_sgu_b, m_lru_conv_w, m_lru_conv_b, m_lru_wa, m_lru_ba, m_lru_wx, m_lru_bx, m_lru_lambda, m_gdn_conv_w, m_gdn_a_log, m_gdn_dt_bias, m_gdn_norm_g, m_pool_w, m_pool_scale, m_w_branch, m_w_out, m_ff2_norm, m_ff2_wg, m_ff2_wu, m_ff2_wd, m_final_norm, v_ff1_norm, v_ff1_wg, v_ff1_wu, v_ff1_wd, v_mix_norm, v_w_in, v_sgu_ln_g, v_sgu_ln_b, v_sgu_w, v_sgu_b, v_lru_conv_w, v_lru_conv_b, v_lru_wa, v_lru_ba, v_lru_wx, v_lru_bx, v_lru_lambda, v_gdn_conv_w, v_gdn_a_log, v_gdn_dt_bias, v_gdn_norm_g, v_pool_w, v_pool_scale, v_w_branch, v_w_out, v_ff2_norm, v_ff2_wg, v_ff2_wu, v_ff2_wd, v_final_norm):
    w = dict(ff1_norm=ff1_norm, ff1_wg=ff1_wg, ff1_wu=ff1_wu, ff1_wd=ff1_wd, mix_norm=mix_norm, w_in=w_in,
             sgu_ln_g=sgu_ln_g, sgu_ln_b=sgu_ln_b, sgu_w=sgu_w, sgu_b=sgu_b, lru_conv_w=lru_conv_w,
             lru_conv_b=lru_conv_b, lru_wa=lru_wa, lru_ba=lru_ba, lru_wx=lru_wx, lru_bx=lru_bx, lru_lambda=lru_lambda,
             gdn_conv_w=gdn_conv_w, gdn_a_log=gdn_a_log, gdn_dt_bias=gdn_dt_bias, gdn_norm_g=gdn_norm_g, pool_w=pool_w,
             pool_scale=pool_scale, w_branch=w_branch, w_out=w_out, ff2_norm=ff2_norm, ff2_wg=ff2_wg, ff2_wu=ff2_wu,
             ff2_wd=ff2_wd, final_norm=final_norm)
    m = dict(ff1_norm=m_ff1_norm, ff1_wg=m_ff1_wg, ff1_wu=m_ff1_wu, ff1_wd=m_ff1_wd, mix_norm=m_mix_norm, w_in=m_w_in,
             sgu_ln_g=m_sgu_ln_g, sgu_ln_b=m_sgu_ln_b, sgu_w=m_sgu_w, sgu_b=m_sgu_b, lru_conv_w=m_lru_conv_w,
             lru_conv_b=m_lru_conv_b, lru_wa=m_lru_wa, lru_ba=m_lru_ba, lru_wx=m_lru_wx, lru_bx=m_lru_bx,
             lru_lambda=m_lru_lambda, gdn_conv_w=m_gdn_conv_w, gdn_a_log=m_gdn_a_log, gdn_dt_bias=m_gdn_dt_bias,
             gdn_norm_g=m_gdn_norm_g, pool_w=m_pool_w, pool_scale=m_pool_scale, w_branch=m_w_branch, w_out=m_w_out,
             ff2_norm=m_ff2_norm, ff2_wg=m_ff2_wg, ff2_wu=m_ff2_wu, ff2_wd=m_ff2_wd, final_norm=m_final_norm)
    v = dict(ff1_norm=v_ff1_norm, ff1_wg=v_ff1_wg, ff1_wu=v_ff1_wu, ff1_wd=v_ff1_wd, mix_norm=v_mix_norm, w_in=v_w_in,
             sgu_ln_g=v_sgu_ln_g, sgu_ln_b=v_sgu_ln_b, sgu_w=v_sgu_w, sgu_b=v_sgu_b, lru_conv_w=v_lru_conv_w,
             lru_conv_b=v_lru_conv_b, lru_wa=v_lru_wa, lru_ba=v_lru_ba, lru_wx=v_lru_wx, lru_bx=v_lru_bx,
             lru_lambda=v_lru_lambda, gdn_conv_w=v_gdn_conv_w, gdn_a_log=v_gdn_a_log, gdn_dt_bias=v_gdn_dt_bias,
             gdn_norm_g=v_gdn_norm_g, pool_w=v_pool_w, pool_scale=v_pool_scale, w_branch=v_w_branch, w_out=v_w_out,
             ff2_norm=v_ff2_norm, ff2_wg=v_ff2_wg, ff2_wu=v_ff2_wu, ff2_wd=v_ff2_wd, final_norm=v_final_norm)

    first, got_first = _gather_first(w)
    handles, token = _gather_start(w, got_first)
    conv = _gather_conv_finish(w, handles, token)
    pending = {}

    def get_weights(l, sub, after):
        return _gather_finish(l, sub, handles, first, after)

    def put_grads(l, sub, grads):
        hs, meta, tok = _scatter_start(l, sub, grads)
        pending[(l, sub, meta[0])] = (hs, meta)
        return tok

    def put_small(tag, grads):
        hs, shapes, tok = _small_start(tag, grads)
        pending[tag] = (hs, shapes)
        return tok

    T = x.shape[1]
    dx = _forward_backward(x.reshape(T, D), loss_target.reshape(T, D), w, conv, get_weights, put_grads, put_small,
                           token)
    per = {}
    for key in pending:
        if isinstance(key, tuple):
            per.setdefault(key[:2], {}).update(_scatter_finish(*key[:2], *pending[key], dx))
        else:
            per[key] = _small_finish(key, *pending[key], dx)
    grad = {n: jnp.stack([per[(0, sub)][n], per[(1, sub)][n]]) for sub, (_, names) in enumerate(_GROUPS) for n in names}
    layer0 = {**per["0a"], **per["0b"]}
    small = {n: _join([layer0[n].reshape(-1), per["1"][n].reshape(-1)]).reshape((2,) + layer0[n].shape)
             for n in layer0}
    small["final_norm"] = per["1"]["final_norm"]
    loss = per["1"]["loss"][0]
    me = _my_index()
    for n in _SMALL:
        if n in _CONV_SHARDED:
            width = w[n].shape[-1]
            grad[n] = lax.dynamic_slice_in_dim(small[n], me * width, width, axis=2)
        else:
            grad[n] = small[n]

    delta, new_m, new_v = {}, {}, {}
    for n in _BIG:
        d_, m_, v_ = _adamw("adamw_" + n, _as2d(w[n]), _as2d(grad[n]), _as2d(m[n]), _as2d(v[n]))
        delta[n], new_m[n], new_v[n] = (t.reshape(w[n].shape) for t in (d_, m_, v_))

    outs = _adamw_many("adamw_small", *[[_as2d(t[n]) for n in _SMALL] for t in (w, grad, m, v)])
    for k, dst in enumerate((delta, new_m, new_v)):
        for i, n in enumerate(_SMALL):
            dst[n] = outs[k * len(_SMALL) + i].reshape(w[n].shape)

    return (loss, dx.reshape(x.shape), *[grad[n] for n in _WEIGHTS], *[delta[n] for n in _WEIGHTS],
            *[new_m[n] for n in _WEIGHTS], *[new_v[n] for n in _WEIGHTS])
```

```python
import functools

import jax
import jax.numpy as jnp
from jax import lax
from jax.experimental import pallas as pl
from jax.experimental.pallas import tpu as pltpu

F32 = jnp.float32
BF16 = jnp.bfloat16
HI = lax.Precision.HIGHEST

N_DEV = 8
D = 1024
FF = 2816
BW = 512
NBR = 4
CHUNK = 64
EPS = 1e-6
LRU_C = 8.0
GDN_DK = 128

COL_AU, COL_AV, COL_BX, COL_BG = 0, 512, 1024, 1536
COL_CQ, COL_CK, COL_CV, COL_CZ = 2048, 2560, 3072, 3584
COL_DX, COL_GATE, COL_TAIL = 4096, 4608, 8704
PW = 9216
P_IN = 8712

ADAM_LR, ADAM_B1, ADAM_B2, ADAM_EPS, ADAM_WD, ADAM_STEP = 0.001, 0.9, 0.999, 1e-08, 0.01, 10

VMEM_LIMIT_V7X = 56 * 1024 * 1024

_NN = (((1,), (0,)), ((), ()))
_NT = (((1,), (1,)), ((), ()))
_TN = (((0,), (0,)), ((), ()))


def _cp(*sem):
    return pltpu.CompilerParams(dimension_semantics=tuple(sem), vmem_limit_bytes=VMEM_LIMIT_V7X)


def _dot(a, b, dims=_NN):
    return lax.dot_general(a.astype(BF16), b.astype(BF16), dims, preferred_element_type=F32)


def _dot_hi(a, b, dims=_NN):
    return lax.dot_general(a, b, dims, precision=HI, preferred_element_type=F32)


def _pick(n, cands):
    for c in cands:
        if n % c == 0:
            return c
    return n


@jax.custom_jvp
def _log1p(x):
    u = 1.0 + x
    return jnp.where(u == 1.0, x, x * jnp.log(u) / jnp.where(u == 1.0, 1.0, u - 1.0))


@_log1p.defjvp
def _log1p_jvp(p, t):
    (x,), (dx,) = p, t
    return _log1p(x), dx / (1.0 + x)


@jax.custom_jvp
def _expm1(x):
    u = jnp.exp(x)
    lu = jnp.log(u)
    small = (u == 1.0) | (lu == 0.0)
    return jnp.where(small, x, (u - 1.0) * x / jnp.where(small, 1.0, lu))


@_expm1.defjvp
def _expm1_jvp(p, t):
    (x,), (dx,) = p, t
    return _expm1(x), dx * jnp.exp(x)


def _softplus(x):
    return jnp.maximum(x, 0.0) + _log1p(jnp.exp(-jnp.abs(x)))


def _sigmoid(x):
    return jax.nn.sigmoid(x)


def _silu(x):
    return x * jax.nn.sigmoid(x)


def _gelu(x):
    return jax.nn.gelu(x)


@functools.partial(jax.custom_vjp, nondiff_argnums=(1,))
def _shift(x, s):
    return x if s == 0 else pltpu.roll(x, s, 0)


def _shift_fwd(x, s):
    return _shift(x, s), None


def _shift_bwd(s, _, g):
    n = g.shape[0]
    return (g if s == 0 else pltpu.roll(g, n - s, 0),)


_shift.defvjp(_shift_fwd, _shift_bwd)


def _scan_steps(a, b, reverse):
    n = a.shape[0]
    row = lax.broadcasted_iota(jnp.int32, a.shape, 0)
    k = 1
    while k < n:
        sh = n - k if reverse else k
        m = (row < n - k) if reverse else (row >= k)
        a_s = jnp.where(m, pltpu.roll(a, sh, 0), 1.0)
        b_s = jnp.where(m, pltpu.roll(b, sh, 0), 0.0)
        b = a * b_s + b
        a = a * a_s
        k *= 2
    return b


@jax.custom_vjp
def _scan(a, b):
    return _scan_steps(a, b, False)


def _scan_fwd(a, b):
    h = _scan_steps(a, b, False)
    return h, (a, h)


def _scan_bwd(res, dh):
    a, h = res
    n = a.shape[0]
    row = lax.broadcasted_iota(jnp.int32, a.shape, 0)
    a_next = jnp.where(row < n - 1, pltpu.roll(a, n - 1, 0), 0.0)
    g = _scan_steps(a_next, dh, True)
    h_prev = jnp.where(row >= 1, pltpu.roll(h, 1, 0), 0.0)
    return g * h_prev, g


_scan.defvjp(_scan_fwd, _scan_bwd)


def _mm(name, pairs, mode, out_dtype, *, res=None, scale=1.0, bm=None, bn=None, bk=None, after=None):
    a0, b0 = pairs[0]
    if mode == "nn":
        (M, K), N = a0.shape, b0.shape[1]
    elif mode == "nt":
        (M, K), N = a0.shape, b0.shape[0]
    else:
        (K, M), N = a0.shape, b0.shape[1]
    bm = bm or _pick(M, (1024, 512, 256, 128))
    bn = bn or _pick(N, (1024, 512, 256, 128))
    bk = bk or _pick(K, (1024, 512, 1408, 256, 128))
    nk = K // bk
    npair = len(pairs)
    dims = {"nn": _NN, "nt": _NT, "tn": _TN}[mode]

    def body(*refs):
        ab = refs[:2 * npair]
        pos = 2 * npair
        r_ref = None
        if res is not None:
            r_ref = refs[pos]
            pos += 1
        pos += after is not None
        o_ref = refs[pos]
        part = None
        for p in range(npair):
            d = _dot(ab[2 * p][...], ab[2 * p + 1][...], dims)
            part = d if part is None else part + d

        def finish(acc):
            out = acc if scale == 1.0 else acc * scale
            if r_ref is not None:
                out = out + r_ref[...]
            o_ref[...] = out.astype(out_dtype)

        if nk == 1:
            finish(part)
        else:
            acc_ref = refs[pos + 1]
            k = pl.program_id(2)

            @pl.when(k == 0)
            def _():
                acc_ref[...] = part

            @pl.when(k > 0)
            def _():
                acc_ref[...] += part

            @pl.when(k == nk - 1)
            def _():
                finish(acc_ref[...])

    if mode == "nn":
        a_spec = pl.BlockSpec((bm, bk), lambda i, j, k: (i, k))
        b_spec = pl.BlockSpec((bk, bn), lambda i, j, k: (k, j))
    elif mode == "nt":
        a_spec = pl.BlockSpec((bm, bk), lambda i, j, k: (i, k))
        b_spec = pl.BlockSpec((bn, bk), lambda i, j, k: (j, k))
    else:
        a_spec = pl.BlockSpec((bk, bm), lambda i, j, k: (k, i))
        b_spec = pl.BlockSpec((bk, bn), lambda i, j, k: (k, j))
    o_spec = pl.BlockSpec((bm, bn), lambda i, j, k: (i, j))
    in_specs, args = [], []
    for a, b in pairs:
        in_specs += [a_spec, b_spec]
        args += [a, b]
    if res is not None:
        in_specs.append(o_spec)
        args.append(res)
    if after is not None:
        in_specs.append(_ANY)
        args.append(after)
    return pl.pallas_call(
        body, name=name, grid=(M // bm, N // bn, nk),
        in_specs=in_specs, out_specs=o_spec,
        out_shape=jax.ShapeDtypeStruct((M, N), out_dtype),
        scratch_shapes=[pltpu.VMEM((bm, bn), F32)] if nk > 1 else [],
        compiler_params=_cp("parallel", "parallel", "arbitrary"),
    )(*args)


def _rms_fwd(name, x, g):
    T = x.shape[0]
    bm = _pick(T, (512, 256, 128))

    def body(x_ref, g_ref, o_ref):
        xv = x_ref[...]
        r = lax.rsqrt(jnp.mean(xv * xv, axis=-1, keepdims=True) + EPS)
        o_ref[...] = (xv * r * g_ref[...]).astype(BF16)

    return pl.pallas_call(
        body, name=name, grid=(T // bm,),
        in_specs=[pl.BlockSpec((bm, D), lambda i: (i, 0)), pl.BlockSpec((1, D), lambda i: (0, 0))],
        out_specs=pl.BlockSpec((bm, D), lambda i: (i, 0)),
        out_shape=jax.ShapeDtypeStruct((T, D), BF16),
        compiler_params=_cp("parallel"),
    )(x, g)


def _rms_bwd(name, x, g, dh, dres):
    T = x.shape[0]
    bm = _pick(T, (512, 256, 128))

    def body(x_ref, g_ref, dh_ref, dres_ref, dx_ref, dg_ref):
        xv = x_ref[...]
        r = lax.rsqrt(jnp.mean(xv * xv, axis=-1, keepdims=True) + EPS)
        xh = xv * r
        dhv = dh_ref[...]
        dxh = dhv * g_ref[...]
        dx_ref[...] = dres_ref[...] + r * (dxh - xh * jnp.mean(dxh * xh, axis=-1, keepdims=True))
        part = jnp.sum(dhv * xh, axis=0, keepdims=True)

        @pl.when(pl.program_id(0) == 0)
        def _():
            dg_ref[...] = part

        @pl.when(pl.program_id(0) > 0)
        def _():
            dg_ref[...] += part

    row = pl.BlockSpec((bm, D), lambda i: (i, 0))
    vec = pl.BlockSpec((1, D), lambda i: (0, 0))
    return pl.pallas_call(
        body, name=name, grid=(T // bm,),
        in_specs=[row, vec, row, row], out_specs=[row, vec],
        out_shape=[jax.ShapeDtypeStruct((T, D), F32), jax.ShapeDtypeStruct((1, D), F32)],
        compiler_params=_cp("arbitrary"),
    )(x, g, dh, dres)


def _final_loss(name, x, g, tgt):
    T = x.shape[0]
    bm = _pick(T, (512, 256, 128))

    def body(x_ref, g_ref, t_ref, loss_ref, dx_ref, dg_ref):
        xv = x_ref[...]
        gv = g_ref[...]
        r = lax.rsqrt(jnp.mean(xv * xv, axis=-1, keepdims=True) + EPS)
        xh = xv * r
        e = xh * gv - t_ref[...]
        lpart = jnp.broadcast_to(0.5 * jnp.sum(jnp.mean(e * e, axis=-1, keepdims=True), axis=0, keepdims=True), (1, 128))
        dy = e * (1.0 / D)
        dxh = dy * gv
        dx_ref[...] = r * (dxh - xh * jnp.mean(dxh * xh, axis=-1, keepdims=True))
        gpart = jnp.sum(dy * xh, axis=0, keepdims=True)

        @pl.when(pl.program_id(0) == 0)
        def _():
            loss_ref[...] = lpart
            dg_ref[...] = gpart

        @pl.when(pl.program_id(0) > 0)
        def _():
            loss_ref[...] += lpart
            dg_ref[...] += gpart

    row = pl.BlockSpec((bm, D), lambda i: (i, 0))
    vec = pl.BlockSpec((1, D), lambda i: (0, 0))
    return pl.pallas_call(
        body, name=name, grid=(T // bm,),
        in_specs=[row, vec, row],
        out_specs=[pl.BlockSpec((1, 128), lambda i: (0, 0)), row, vec],
        out_shape=[jax.ShapeDtypeStruct((1, 128), F32), jax.ShapeDtypeStruct((T, D), F32),
                   jax.ShapeDtypeStruct((1, D), F32)],
        compiler_params=_cp("arbitrary"),
    )(x, g, tgt)


def _ffn_up(name, h, wg, wu):
    T = h.shape[0]
    bm = _pick(T, (2048, 1024, 512, 256, 128))
    bn = 256

    def body(h_ref, wg_ref, wu_ref, a_ref, b_ref, act_ref):
        hv = h_ref[...]
        a = _dot(hv, wg_ref[...], _NT)
        b = _dot(hv, wu_ref[...], _NT)
        a_ref[...] = a.astype(BF16)
        b_ref[...] = b.astype(BF16)
        act_ref[...] = (_silu(a) * b).astype(BF16)

    w_spec = pl.BlockSpec((bn, D), lambda i, j: (j, 0))
    o_spec = pl.BlockSpec((bm, bn), lambda i, j: (i, j))
    return pl.pallas_call(
        body, name=name, grid=(T // bm, FF // bn),
        in_specs=[pl.BlockSpec((bm, D), lambda i, j: (i, 0)), w_spec, w_spec],
        out_specs=[o_spec, o_spec, o_spec],
        out_shape=[jax.ShapeDtypeStruct((T, FF), BF16)] * 3,
        compiler_params=_cp("parallel", "parallel"),
    )(h, wg, wu)


def _ffn_dact(name, dy, wd, a, b, after=None):
    T = dy.shape[0]
    bm = _pick(T, (2048, 1024, 512, 256, 128))
    bn = 256

    def body(dy_ref, wd_ref, a_ref, b_ref, *rest):
        da_ref, db_ref, dy_bf = rest[-3:]

        @pl.when(pl.program_id(1) == 0)
        def _():
            dy_bf[...] = dy_ref[...].astype(BF16)

        dact = 0.5 * _dot(dy_bf[...], wd_ref[...], _NT)
        av = a_ref[...].astype(F32)
        s = _sigmoid(av)
        da_ref[...] = (dact * b_ref[...].astype(F32) * (s * (1.0 + av * (1.0 - s)))).astype(BF16)
        db_ref[...] = (dact * (av * s)).astype(BF16)

    t_spec = pl.BlockSpec((bm, bn), lambda i, j: (i, j))
    return pl.pallas_call(
        body, name=name, grid=(T // bm, FF // bn),
        in_specs=[pl.BlockSpec((bm, D), lambda i, j: (i, 0)), pl.BlockSpec((bn, D), lambda i, j: (j, 0)),
                  t_spec, t_spec] + [_ANY] * (after is not None),
        out_specs=[t_spec, t_spec],
        out_shape=[jax.ShapeDtypeStruct((T, FF), BF16), jax.ShapeDtypeStruct((T, FF), BF16)],
        scratch_shapes=[pltpu.VMEM((bm, D), BF16)],
        compiler_params=_cp("parallel", "arbitrary"),
    )(dy, wd, a, b, *([after] if after is not None else []))


def _merge_specs(T, bm, bn):
    y_spec = pl.BlockSpec((bm, BW), lambda i, j: (i, 0))
    wb_spec = pl.BlockSpec((NBR, bn, BW), lambda i, j: (0, j, 0))
    gate_specs = [pl.BlockSpec((bm, bn), functools.partial(lambda i, j, o: (i, o + j), o=(COL_GATE + g * D) // bn))
                  for g in range(NBR)]
    t_spec = pl.BlockSpec((bm, bn), lambda i, j: (i, j))
    return y_spec, wb_spec, gate_specs, t_spec


def _merge_fwd(name, ys, wb, proj):
    T = proj.shape[0]
    bm = _pick(T, (512, 256, 128))
    bn = 512
    y_spec, wb_spec, gate_specs, t_spec = _merge_specs(T, bm, bn)

    def body(y0, y1, y2, y3, wb_ref, g0, g1, g2, g3, o_ref):
        acc = None
        for g, (y_ref, g_ref) in enumerate(((y0, g0), (y1, g1), (y2, g2), (y3, g3))):
            t = _sigmoid(g_ref[...]) * _dot(y_ref[...], wb_ref[g], _NT)
            acc = t if acc is None else acc + t
        o_ref[...] = acc.astype(BF16)

    return pl.pallas_call(
        body, name=name, grid=(T // bm, D // bn),
        in_specs=[y_spec] * NBR + [wb_spec] + gate_specs, out_specs=t_spec,
        out_shape=jax.ShapeDtypeStruct((T, D), BF16),
        compiler_params=_cp("parallel", "parallel"),
    )(*ys, wb, proj, proj, proj, proj)


def _merge_bwd(name, dm, ys, wb, proj):
    T = proj.shape[0]
    bm = _pick(T, (512, 256, 128))
    bn = 512
    y_spec, wb_spec, gate_specs, t_spec = _merge_specs(T, bm, bn)

    def body(dm_ref, y0, y1, y2, y3, wb_ref, g0, g1, g2, g3, *outs):
        dmv = dm_ref[...]
        j = pl.program_id(1)
        for g, (y_ref, g_ref) in enumerate(((y0, g0), (y1, g1), (y2, g2), (y3, g3))):
            br = _dot(y_ref[...], wb_ref[g], _NT)
            s = _sigmoid(g_ref[...])
            outs[g][...] = (dmv * br * (s * (1.0 - s))).astype(BF16)
            dbr = (dmv * s).astype(BF16)
            outs[NBR + g][...] = dbr
            part = _dot(dbr, wb_ref[g])
            dy_ref = outs[2 * NBR + g]

            @pl.when(j == 0)
            def _():
                dy_ref[...] = part

            @pl.when(j > 0)
            def _():
                dy_ref[...] += part

    return pl.pallas_call(
        body, name=name, grid=(T // bm, D // bn),
        in_specs=[t_spec] + [y_spec] * NBR + [wb_spec] + gate_specs, out_specs=[t_spec] * (2 * NBR) + [y_spec] * NBR,
        out_shape=[jax.ShapeDtypeStruct((T, D), BF16)] * (2 * NBR) + [jax.ShapeDtypeStruct((T, BW), F32)] * NBR,
        compiler_params=_cp("parallel", "arbitrary"),
    )(dm, *ys, wb, proj, proj, proj, proj)


def _sgu_block(u_pre, v_pre, ln_g, ln_b, w, bias):
    u = _gelu(u_pre)
    vf = _gelu(v_pre)
    mu = jnp.mean(vf, axis=-1, keepdims=True)
    var = jnp.mean(jnp.square(vf - mu), axis=-1, keepdims=True)
    vn = (vf - mu) * lax.rsqrt(var + EPS) * ln_g + ln_b
    ri = lax.broadcasted_iota(jnp.int32, (128, 128), 0)
    ci = lax.broadcasted_iota(jnp.int32, (128, 128), 1)
    mask = (ri // CHUNK) >= (ci // CHUNK)
    outs = [_dot(jnp.where(mask, w[g], 0.0), vn[:, g * 128:(g + 1) * 128]) for g in range(4)]
    mixed = jnp.concatenate(outs, axis=1) + bias
    return u * mixed


def _sgu_param_specs():
    return [pl.BlockSpec((1, BW), lambda i: (0, 0)), pl.BlockSpec((1, BW), lambda i: (0, 0)),
            pl.BlockSpec((4, 128, 128), lambda i: (0, 0, 0)), pl.BlockSpec((128, BW), lambda i: (0, 0))]


def _sgu_fwd(name, proj, ln_g, ln_b, w, bias):
    T = proj.shape[0]
    rb = _pick(T, (256, 128))

    def body(u_ref, v_ref, g_ref, b_ref, w_ref, bias_ref, y_ref):
        for n in range(rb // 128):
            rows = slice(n * 128, (n + 1) * 128)
            y = _sgu_block(u_ref[rows, :], v_ref[rows, :], g_ref[...], b_ref[...], w_ref[...], bias_ref[...])
            y_ref[rows, :] = y.astype(BF16)

    return pl.pallas_call(
        body, name=name, grid=(T // rb,),
        in_specs=[pl.BlockSpec((rb, BW), lambda i: (i, COL_AU // BW)), pl.BlockSpec((rb, BW), lambda i: (i, COL_AV // BW))]
        + _sgu_param_specs(),
        out_specs=pl.BlockSpec((rb, BW), lambda i: (i, 0)),
        out_shape=jax.ShapeDtypeStruct((T, BW), BF16),
        compiler_params=_cp("parallel"),
    )(proj, proj, ln_g, ln_b, w, bias)


def _sgu_bwd(name, proj, dy, ln_g, ln_b, w, bias):
    T = proj.shape[0]
    rb = _pick(T, (256, 128))

    def body(u_ref, v_ref, dy_ref, g_ref, b_ref, w_ref, bias_ref, du_ref, dv_ref, dg_ref, db_ref, dw_ref, dbias_ref):
        acc = None
        for n in range(rb // 128):
            rows = slice(n * 128, (n + 1) * 128)
            _, vjp = jax.vjp(_sgu_block, u_ref[rows, :], v_ref[rows, :], g_ref[...], b_ref[...], w_ref[...],
                             bias_ref[...])
            du, dv, *dp = vjp(dy_ref[rows, :])
            du_ref[rows, :] = du.astype(BF16)
            dv_ref[rows, :] = dv.astype(BF16)
            acc = dp if acc is None else [p + q for p, q in zip(acc, dp)]

        @pl.when(pl.program_id(0) == 0)
        def _():
            for r, p in zip((dg_ref, db_ref, dw_ref, dbias_ref), acc):
                r[...] = p

        @pl.when(pl.program_id(0) > 0)
        def _():
            for r, p in zip((dg_ref, db_ref, dw_ref, dbias_ref), acc):
                r[...] += p

    row = pl.BlockSpec((rb, BW), lambda i: (i, 0))
    return pl.pallas_call(
        body, name=name, grid=(T // rb,),
        in_specs=[pl.BlockSpec((rb, BW), lambda i: (i, COL_AU // BW)), pl.BlockSpec((rb, BW), lambda i: (i, COL_AV // BW)),
                  row] + _sgu_param_specs(),
        out_specs=[row, row] + _sgu_param_specs(),
        out_shape=[jax.ShapeDtypeStruct((T, BW), BF16), jax.ShapeDtypeStruct((T, BW), BF16),
                   jax.ShapeDtypeStruct((1, BW), F32), jax.ShapeDtypeStruct((1, BW), F32),
                   jax.ShapeDtypeStruct((4, 128, 128), F32), jax.ShapeDtypeStruct((128, BW), F32)],
        compiler_params=_cp("arbitrary"),
    )(proj, proj, dy, ln_g, ln_b, w, bias)


def _halo_block(ref, i, rblk, halo):
    r0 = pl.multiple_of(i * rblk, rblk)
    h0 = pl.multiple_of(jnp.maximum(r0 - halo, 0), halo)
    top = jnp.where(i > 0, ref[pl.ds(h0, halo), :], 0.0)
    return jnp.concatenate([top, ref[pl.ds(r0, rblk), :]], axis=0)


def _with_halo_grad(dfull, pending, halo, rblk):
    tail = jnp.concatenate([jnp.zeros((rblk - halo, 128), F32), pending], axis=0)
    return dfull[halo:] + tail


def _conv4(xfull, rows):
    acc = None
    for k in range(4):
        t = rows[k] * _shift(xfull, 3 - k)[8:]
        acc = t if acc is None else acc + t
    return acc


def _lru_block(xfull, gate, h0, c0, c1, c2, c3, cb, wa, ba, wx, bx, lam):
    n = gate.shape[0]
    xc = _conv4(xfull, (c0, c1, c2, c3)) + cb
    r = _sigmoid(_dot(xc, wa) + ba)
    ig = _sigmoid(_dot(xc, wx) + bx)
    log_a = -LRU_C * r * _softplus(-lam)
    a = jnp.exp(log_a)
    mult = jnp.sqrt(-_expm1(2.0 * log_a))
    b = mult * (ig * xc)
    row = lax.broadcasted_iota(jnp.int32, (n, 128), 0)
    b = b + jnp.where(row == 0, a * h0, 0.0)
    h = _scan(a, b)
    out = h * _gelu(gate)
    h_last = jnp.sum(jnp.where(row == n - 1, h, 0.0), axis=0, keepdims=True)
    return out, h_last


def _lru_param_specs():
    vec = pl.BlockSpec((1, 128), lambda g: (0, g))
    mat = pl.BlockSpec((None, 128, 128), lambda g: (g, 0, 0))
    return [pl.BlockSpec((4, 128), lambda g: (0, g)), vec, mat, vec, mat, vec, vec]


def _lru_load_params(cw_ref, cb_ref, wa_ref, ba_ref, wx_ref, bx_ref, lam_ref):
    return (cw_ref[0:1, :], cw_ref[1:2, :], cw_ref[2:3, :], cw_ref[3:4, :], cb_ref[...], wa_ref[...], ba_ref[...],
            wx_ref[...], bx_ref[...], lam_ref[...])


def _lru_fwd(name, proj, cw, cb, wa, ba, wx, bx, lam):
    T = proj.shape[0]
    rblk = _pick(T, (256, 128))
    nblk = T // rblk

    def body(x_ref, gt_ref, cw_ref, cb_ref, wa_ref, ba_ref, wx_ref, bx_ref, lam_ref, y_ref, hc_ref):
        params = _lru_load_params(cw_ref, cb_ref, wa_ref, ba_ref, wx_ref, bx_ref, lam_ref)

        def step(i, h0):
            r0 = pl.multiple_of(i * rblk, rblk)
            out, h_last = _lru_block(_halo_block(x_ref, i, rblk, 8), gt_ref[pl.ds(r0, rblk), :], h0, *params)
            y_ref[pl.ds(r0, rblk), :] = out.astype(BF16)
            hc_ref[pl.ds(pl.multiple_of(i * 8, 8), 8), :] = jnp.broadcast_to(h0, (8, 128))
            return h_last

        lax.fori_loop(0, nblk, step, jnp.zeros((1, 128), F32))

    return pl.pallas_call(
        body, name=name, grid=(4,),
        in_specs=[pl.BlockSpec((T, 128), lambda g: (0, COL_BX // 128 + g)),
                  pl.BlockSpec((T, 128), lambda g: (0, COL_BG // 128 + g))] + _lru_param_specs(),
        out_specs=[pl.BlockSpec((T, 128), lambda g: (0, g)), pl.BlockSpec((nblk * 8, 128), lambda g: (0, g))],
        out_shape=[jax.ShapeDtypeStruct((T, BW), BF16), jax.ShapeDtypeStruct((nblk * 8, BW), F32)],
        compiler_params=_cp("parallel"),
    )(proj, proj, cw, cb, wa, ba, wx, bx, lam)


def _lru_bwd(name, proj, dy, hc, cw, cb, wa, ba, wx, bx, lam):
    T = proj.shape[0]
    rblk = _pick(T, (256, 128))
    nblk = T // rblk

    def body(x_ref, gt_ref, dy_ref, hc_ref, cw_ref, cb_ref, wa_ref, ba_ref, wx_ref, bx_ref, lam_ref,
             dx_ref, dgt_ref, dcw_ref, dcb_ref, dwa_ref, dba_ref, dwx_ref, dbx_ref, dlam_ref):
        params = _lru_load_params(cw_ref, cb_ref, wa_ref, ba_ref, wx_ref, bx_ref, lam_ref)

        def step(it, carry):
            dh_last, pending, acc = carry
            i = nblk - 1 - it
            r0 = pl.multiple_of(i * rblk, rblk)
            h0 = hc_ref[pl.ds(pl.multiple_of(i * 8, 8), 1), :]
            _, vjp = jax.vjp(_lru_block, _halo_block(x_ref, i, rblk, 8), gt_ref[pl.ds(r0, rblk), :], h0, *params)
            dfull, dgate, dh0, *dp = vjp((dy_ref[pl.ds(r0, rblk), :], dh_last))
            dx_ref[pl.ds(r0, rblk), :] = _with_halo_grad(dfull, pending, 8, rblk).astype(BF16)
            dgt_ref[pl.ds(r0, rblk), :] = dgate.astype(BF16)
            return dh0, dfull[:8], tuple(p + q for p, q in zip(acc, dp))

        zeros = tuple(jnp.zeros(p.shape, F32) for p in params)
        _, _, acc = lax.fori_loop(0, nblk, step, (jnp.zeros((1, 128), F32), jnp.zeros((8, 128), F32), zeros))
        for k in range(4):
            dcw_ref[k:k + 1, :] = acc[k]
        for r, p in zip((dcb_ref, dwa_ref, dba_ref, dwx_ref, dbx_ref, dlam_ref), acc[4:]):
            r[...] = p

    col = pl.BlockSpec((T, 128), lambda g: (0, g))
    return pl.pallas_call(
        body, name=name, grid=(4,),
        in_specs=[pl.BlockSpec((T, 128), lambda g: (0, COL_BX // 128 + g)),
                  pl.BlockSpec((T, 128), lambda g: (0, COL_BG // 128 + g)), col,
                  pl.BlockSpec((nblk * 8, 128), lambda g: (0, g))] + _lru_param_specs(),
        out_specs=[col, col] + _lru_param_specs(),
        out_shape=[jax.ShapeDtypeStruct((T, BW), BF16), jax.ShapeDtypeStruct((T, BW), BF16),
                   jax.ShapeDtypeStruct((4, BW), F32), jax.ShapeDtypeStruct((1, BW), F32),
                   jax.ShapeDtypeStruct((4, 128, 128), F32), jax.ShapeDtypeStruct((1, BW), F32),
                   jax.ShapeDtypeStruct((4, 128, 128), F32), jax.ShapeDtypeStruct((1, BW), F32),
                   jax.ShapeDtypeStruct((1, BW), F32)],
        compiler_params=_cp("parallel"),
    )(proj, proj, dy, hc, cw, cb, wa, ba, wx, bx, lam)


def _conv_block(xfull, c0, c1, c2, c3):
    return _silu(_conv4(xfull, (c0, c1, c2, c3)))


def _conv_fwd(name, proj, col0, cw, cw_col0):
    T = proj.shape[0]
    rblk = _pick(T, (256, 128))
    nblk = T // rblk

    def body(x_ref, cw_ref, y_ref):
        rows = (cw_ref[0:1, :], cw_ref[1:2, :], cw_ref[2:3, :], cw_ref[3:4, :])

        def step(i, c):
            r0 = pl.multiple_of(i * rblk, rblk)
            y_ref[pl.ds(r0, rblk), :] = _conv_block(_halo_block(x_ref, i, rblk, 8), *rows)
            return c

        lax.fori_loop(0, nblk, step, 0)

    return pl.pallas_call(
        body, name=name, grid=(4,),
        in_specs=[pl.BlockSpec((T, 128), lambda g: (0, col0 // 128 + g)),
                  pl.BlockSpec((4, 128), lambda g: (0, cw_col0 // 128 + g))],
        out_specs=pl.BlockSpec((T, 128), lambda g: (0, g)),
        out_shape=jax.ShapeDtypeStruct((T, BW), F32),
        compiler_params=_cp("parallel"),
    )(proj, cw)


def _conv_bwd(name, proj, col0, dy, cw, cw_col0):
    T = proj.shape[0]
    rblk = _pick(T, (256, 128))
    nblk = T // rblk

    def body(x_ref, dy_ref, cw_ref, dx_ref, dcw_ref):
        rows = (cw_ref[0:1, :], cw_ref[1:2, :], cw_ref[2:3, :], cw_ref[3:4, :])

        def step(it, carry):
            pending, acc = carry
            i = nblk - 1 - it
            r0 = pl.multiple_of(i * rblk, rblk)
            _, vjp = jax.vjp(_conv_block, _halo_block(x_ref, i, rblk, 8), *rows)
            dfull, *dp = vjp(dy_ref[pl.ds(r0, rblk), :])
            dx_ref[pl.ds(r0, rblk), :] = _with_halo_grad(dfull, pending, 8, rblk).astype(BF16)
            return dfull[:8], tuple(p + q for p, q in zip(acc, dp))

        zeros = tuple(jnp.zeros((1, 128), F32) for _ in range(4))
        _, acc = lax.fori_loop(0, nblk, step, (jnp.zeros((8, 128), F32), zeros))
        for k in range(4):
            dcw_ref[k:k + 1, :] = acc[k]

    col = pl.BlockSpec((T, 128), lambda g: (0, g))
    return pl.pallas_call(
        body, name=name, grid=(4,),
        in_specs=[pl.BlockSpec((T, 128), lambda g: (0, col0 // 128 + g)), col,
                  pl.BlockSpec((4, 128), lambda g: (0, cw_col0 // 128 + g))],
        out_specs=[col, pl.BlockSpec((4, 128), lambda g: (0, g))],
        out_shape=[jax.ShapeDtypeStruct((T, BW), BF16), jax.ShapeDtypeStruct((4, BW), F32)],
        compiler_params=_cp("parallel"),
    )(proj, dy, cw)


def _pool_block(xfull, pw, sc, t0, gi):
    n = xfull.shape[0] - 16
    s2 = xfull + _shift(xfull, 1)
    s4 = s2 + _shift(s2, 2)
    s8 = s4 + _shift(s4, 4)
    s16 = s8 + _shift(s8, 8)
    s = jnp.where(gi == 0, s2, jnp.where(gi == 1, s4, jnp.where(gi == 2, s8, s16)))[16:]
    t = t0 + lax.broadcasted_iota(jnp.int32, (n, 128), 0)
    cnt = jnp.minimum(t + 1, lax.shift_left(jnp.int32(2), gi)).astype(F32)
    pooled = s / cnt - xfull[16:]
    return _dot(pooled, pw) * sc


def _pool_fwd(name, proj, pw, sc):
    T = proj.shape[0]
    rblk = _pick(T, (256, 128))
    nblk = T // rblk

    def body(x_ref, pw_ref, sc_ref, y_ref):
        gi = pl.program_id(0)

        def step(i, c):
            r0 = pl.multiple_of(i * rblk, rblk)
            y = _pool_block(_halo_block(x_ref, i, rblk, 16), pw_ref[...], sc_ref[...], r0, gi)
            y_ref[pl.ds(r0, rblk), :] = y.astype(BF16)
            return c

        lax.fori_loop(0, nblk, step, 0)

    return pl.pallas_call(
        body, name=name, grid=(4,),
        in_specs=[pl.BlockSpec((T, 128), lambda g: (0, COL_DX // 128 + g)),
                  pl.BlockSpec((None, 128, 128), lambda g: (g, 0, 0)), pl.BlockSpec((1, 128), lambda g: (0, g))],
        out_specs=pl.BlockSpec((T, 128), lambda g: (0, g)),
        out_shape=jax.ShapeDtypeStruct((T, BW), BF16),
        compiler_params=_cp("parallel"),
    )(proj, pw, sc)


def _pool_bwd(name, proj, dy, pw, sc):
    T = proj.shape[0]
    rblk = _pick(T, (256, 128))
    nblk = T // rblk

    def body(x_ref, dy_ref, pw_ref, sc_ref, dx_ref, dpw_ref, dsc_ref):
        gi = pl.program_id(0)

        def step(it, carry):
            pending, apw, asc = carry
            i = nblk - 1 - it
            r0 = pl.multiple_of(i * rblk, rblk)
            _, vjp = jax.vjp(lambda xf, w, s: _pool_block(xf, w, s, r0, gi), _halo_block(x_ref, i, rblk, 16),
                             pw_ref[...], sc_ref[...])
            dfull, dw, ds = vjp(dy_ref[pl.ds(r0, rblk), :])
            dx_ref[pl.ds(r0, rblk), :] = _with_halo_grad(dfull, pending, 16, rblk).astype(BF16)
            return dfull[:16], apw + dw, asc + ds

        _, apw, asc = lax.fori_loop(0, nblk, step, (jnp.zeros((16, 128), F32), jnp.zeros((128, 128), F32),
                                                    jnp.zeros((1, 128), F32)))
        dpw_ref[...] = apw
        dsc_ref[...] = asc

    col = pl.BlockSpec((T, 128), lambda g: (0, g))
    mat = pl.BlockSpec((None, 128, 128), lambda g: (g, 0, 0))
    vec = pl.BlockSpec((1, 128), lambda g: (0, g))
    return pl.pallas_call(
        body, name=name, grid=(4,),
        in_specs=[pl.BlockSpec((T, 128), lambda g: (0, COL_DX // 128 + g)), col, mat, vec],
        out_specs=[col, mat, vec],
        out_shape=[jax.ShapeDtypeStruct((T, BW), BF16), jax.ShapeDtypeStruct((4, 128, 128), F32),
                   jax.ShapeDtypeStruct((1, BW), F32)],
        compiler_params=_cp("parallel"),
    )(proj, dy, pw, sc)


@jax.custom_vjp
def _dot3(a, b):
    ah = a.astype(BF16)
    al = (a - ah.astype(F32)).astype(BF16)
    bh = b.astype(BF16)
    bl = (b - bh.astype(F32)).astype(BF16)

    def d(x, y):
        return lax.dot_general(x, y, _NN, preferred_element_type=F32)

    return d(ah, bh) + (d(ah, bl) + d(al, bh))


def _dot3_fwd(a, b):
    return _dot3(a, b), (a, b)


def _dot3_bwd(res, g):
    a, b = res
    return _dot(g, b, _NT), _dot(a, g, _TN)


_dot3.defvjp(_dot3_fwd, _dot3_bwd)


def _pad_rows2(x):
    return jnp.concatenate([x, jnp.zeros_like(x)], axis=0)


@jax.custom_vjp
def _tri_inv(mats):
    n = mats[0].shape[0]
    eye = (lax.broadcasted_iota(jnp.int32, (n, n), 0) == lax.broadcasted_iota(jnp.int32, (n, n), 1)).astype(F32)
    ps = [eye - a for a in mats]
    ms = list(mats)
    k = 2
    while k < n:
        ms = [_dot3(t, t) for t in ms]
        ps = [p + _dot3(p, t) for p, t in zip(ps, ms)]
        k *= 2
    return ps


def _tri_inv_fwd(mats):
    ts = _tri_inv(mats)
    return ts, ts


def _tri_inv_bwd(ts, gs):
    half = [_dot(t, g, _TN) for t, g in zip(ts, gs)]
    return ([-_dot(h, t, _NT) for h, t in zip(half, ts)],)


_tri_inv.defvjp(_tri_inv_fwd, _tri_inv_bwd)


def _cumsum_rows(x):
    n = x.shape[0]
    row = lax.broadcasted_iota(jnp.int32, x.shape, 0)
    k = 1
    while k < n:
        x = x + jnp.where(row >= k, _shift(x, k), 0.0)
        k *= 2
    return x


def _gdn_prep(qcs, kcs, vcs, tails, alog, dtb):
    C = CHUNK
    pairs = [(c, h) for c in range(len(qcs)) for h in range(4)]
    lane = lax.broadcasted_iota(jnp.int32, (C, 128), 1)
    row = lax.broadcasted_iota(jnp.int32, (C, 128), 0)
    incl = row >= lane
    sig = [_sigmoid(t) for t in tails]
    gfull = [-jnp.exp(alog) * _softplus(t + dtb) for t in tails]
    beta = [jnp.sum(jnp.where(lane == h, sig[c], 0.0), axis=1, keepdims=True) for c, h in pairs]
    g = [jnp.sum(jnp.where(lane == h + 4, gfull[c], 0.0), axis=1, keepdims=True) for c, h in pairs]
    qs = [qcs[c][:, h * 128:(h + 1) * 128] for c, h in pairs]
    ks = [kcs[c][:, h * 128:(h + 1) * 128] for c, h in pairs]
    vs = [vcs[c][:, h * 128:(h + 1) * 128] for c, h in pairs]
    q = [t * lax.rsqrt(jnp.sum(t * t, axis=-1, keepdims=True) + EPS) * (GDN_DK ** -0.5) for t in qs]
    k = [t * lax.rsqrt(jnp.sum(t * t, axis=-1, keepdims=True) + EPS) for t in ks]
    gc = [_cumsum_rows(jnp.broadcast_to(t, (C, 128))) for t in g]
    gc_t = [jnp.transpose(jnp.concatenate([t, t], axis=0)) for t in gc]
    gc_col = [jnp.sum(jnp.where(lane == 0, t, 0.0), axis=1, keepdims=True) for t in gc]
    ri = lax.broadcasted_iota(jnp.int32, (C, C), 0)
    ci = lax.broadcasted_iota(jnp.int32, (C, C), 1)
    decay = [jnp.exp(jnp.where(incl, a - b[:C, :], -1e30)) for a, b in zip(gc, gc_t)]
    decay_sq = [jnp.exp(jnp.where(ri > ci, a - jnp.transpose(b)[:C, :], -1e30)) for a, b in zip(gc_col, gc)]
    kb = [a * b for a, b in zip(k, beta)]
    kk = [_dot(a, b, _NT) for a, b in zip(kb, k)]
    t_mat = _tri_inv([jnp.where(ri > ci, a * b, 0.0) for a, b in zip(kk, decay_sq)])
    egc = [jnp.exp(t) for t in gc]
    u = [_dot(t, a * b) for t, a, b in zip(t_mat, vs, beta)]
    w = [_dot(t, a * b) for t, a, b in zip(t_mat, kb, egc)]
    qk = [_dot(a, _pad_rows2(b), _NT) for a, b in zip(q, k)]
    attn = [jnp.where(incl, a * b, 0.0) for a, b in zip(qk, decay)]
    g_last = [jnp.sum(jnp.where(row == C - 1, t, 0.0), axis=0, keepdims=True) for t in gc]
    qe = [a * b for a, b in zip(q, egc)]
    kd = [a * jnp.exp(b - c_) for a, b, c_ in zip(k, g_last, gc)]
    egl = [jnp.exp(t) for t in g_last]

    def per_chunk(vals):
        return [jnp.concatenate(vals[4 * c:4 * c + 4], axis=1) for c in range(len(qcs))]

    return tuple(per_chunk(t) for t in (u, w, qe, kd, attn, egl))


def _gdn_scan_chunk(states, u, w, qe, kd, attn, egl, z, ng):
    hs = range(4)

    def sl(t, h):
        return t[:, h * 128:(h + 1) * 128]

    ws = [_dot(sl(w, h), states[h]) for h in hs]
    qs = [_dot(sl(qe, h), states[h]) for h in hs]
    v_new = [sl(u, h) - ws[h] for h in hs]
    av = [_dot(sl(attn, h), _pad_rows2(v_new[h])) for h in hs]
    kv = [_dot(sl(kd, h), v_new[h], _TN) for h in hs]
    nxt = tuple(states[h] * sl(egl, h) + kv[h] for h in hs)
    o = [qs[h] + av[h] for h in hs]
    on = [t * lax.rsqrt(jnp.mean(t * t, axis=-1, keepdims=True) + EPS) * ng for t in o]
    return nxt, jnp.concatenate(on, axis=1) * _silu(z)


def _gdn_blocks(T):
    tb = _pick(T, (512, 256, 128, 64))
    return tb, T // tb, tb // CHUNK


PREP_CHUNKS = 4


def _chunk_rows(i, n):
    return [pl.ds(pl.multiple_of((i * n + j) * CHUNK, CHUNK), CHUNK) for j in range(n)]


def _egl_rows(i, n, size):
    return [pl.ds(pl.multiple_of((i * n + j) * 8, 8), size) for j in range(n)]


def _gdn_prep_fwd(name, qa, ka, va, proj, alog, dtb):
    T = proj.shape[0]
    tb, nb, ncb = _gdn_blocks(T)
    n = PREP_CHUNKS if ncb % PREP_CHUNKS == 0 else 1

    def body(q_ref, k_ref, v_ref, tail_ref, alog_ref, dtb_ref, u_ref, w_ref, qe_ref, kd_ref, at_ref, egl_ref):
        def step(i, c):
            rows = _chunk_rows(i, n)
            u, w, qe, kd, at, egl = _gdn_prep([q_ref[r, :] for r in rows], [k_ref[r, :] for r in rows],
                                              [v_ref[r, :] for r in rows], [tail_ref[r, :] for r in rows],
                                              alog_ref[...], dtb_ref[...])
            for j, (r, e) in enumerate(zip(rows, _egl_rows(i, n, 8))):
                u_ref[r, :] = u[j]
                w_ref[r, :] = w[j].astype(BF16)
                qe_ref[r, :] = qe[j].astype(BF16)
                kd_ref[r, :] = kd[j].astype(BF16)
                at_ref[r, :] = at[j].astype(BF16)
                egl_ref[e, :] = jnp.broadcast_to(egl[j], (8, BW))
            return c

        lax.fori_loop(0, ncb // n, step, 0)

    blk = pl.BlockSpec((tb, BW), lambda j: (j, 0))
    vec = pl.BlockSpec((1, 128), lambda j: (0, 0))
    return pl.pallas_call(
        body, name=name, grid=(nb,),
        in_specs=[blk, blk, blk, pl.BlockSpec((tb, 128), lambda j: (j, COL_TAIL // 128)), vec, vec],
        out_specs=[blk] * 5 + [pl.BlockSpec((ncb * 8, BW), lambda j: (j, 0))],
        out_shape=[jax.ShapeDtypeStruct((T, BW), F32)] + [jax.ShapeDtypeStruct((T, BW), BF16)] * 4
        + [jax.ShapeDtypeStruct((T // 8, BW), F32)],
        compiler_params=_cp("parallel"),
    )(qa, ka, va, proj, alog, dtb)


def _gdn_prep_bwd(name, qa, ka, va, proj, alog, dtb, du, dw, dqe, dkd, dat, degl):
    T = proj.shape[0]
    tb, nb, ncb = _gdn_blocks(T)
    n = PREP_CHUNKS if ncb % PREP_CHUNKS == 0 else 1

    def body(q_ref, k_ref, v_ref, tail_ref, alog_ref, dtb_ref, du_ref, dw_ref, dqe_ref, dkd_ref, dat_ref, degl_ref,
             dq_ref, dk_ref, dv_ref, dtail_ref, dalog_ref, ddtb_ref):
        first = pl.program_id(0) == 0

        def step(i, carry):
            pa, pd = carry
            rows = _chunk_rows(i, n)
            _, vjp = jax.vjp(_gdn_prep, [q_ref[r, :] for r in rows], [k_ref[r, :] for r in rows],
                             [v_ref[r, :] for r in rows], [tail_ref[r, :] for r in rows], alog_ref[...], dtb_ref[...])
            cot = tuple([ref[r, :] for r in rows] for ref in (du_ref, dw_ref, dqe_ref, dkd_ref, dat_ref))
            dq, dk, dv, dtail, da, dd = vjp(cot + ([degl_ref[e, :] for e in _egl_rows(i, n, 1)],))
            for j, r in enumerate(rows):
                dq_ref[r, :] = dq[j]
                dk_ref[r, :] = dk[j]
                dv_ref[r, :] = dv[j]
                dtail_ref[r, :] = dtail[j].astype(BF16)
            return pa + da, pd + dd

        zv = jnp.zeros((1, 128), F32)
        pa, pd = lax.fori_loop(0, ncb // n, step, (zv, zv))

        @pl.when(first)
        def _():
            dalog_ref[...] = pa
            ddtb_ref[...] = pd

        @pl.when(jnp.logical_not(first))
        def _():
            dalog_ref[...] += pa
            ddtb_ref[...] += pd

    blk = pl.BlockSpec((tb, BW), lambda j: (j, 0))
    vec = pl.BlockSpec((1, 128), lambda j: (0, 0))
    return pl.pallas_call(
        body, name=name, grid=(nb,),
        in_specs=[blk, blk, blk, pl.BlockSpec((tb, 128), lambda j: (j, COL_TAIL // 128)), vec, vec]
        + [blk] * 5 + [pl.BlockSpec((ncb * 8, BW), lambda j: (j, 0))],
        out_specs=[blk, blk, blk, pl.BlockSpec((tb, 128), lambda j: (j, 0)), vec, vec],
        out_shape=[jax.ShapeDtypeStruct((T, BW), F32)] * 3 + [jax.ShapeDtypeStruct((T, 128), BF16)]
        + [jax.ShapeDtypeStruct((1, 128), F32)] * 2,
        compiler_params=_cp("arbitrary"),
    )(qa, ka, va, proj, alog, dtb, du, dw, dqe, dkd, dat, degl)


def _gdn_fwd(name, u, w, qe, kd, at, egl, proj, ng):
    T = proj.shape[0]
    tb, nb, ncb = _gdn_blocks(T)

    def body(u_ref, w_ref, qe_ref, kd_ref, at_ref, egl_ref, z_ref, ng_ref, y_ref, sh_ref, state):
        @pl.when(pl.program_id(0) == 0)
        def _():
            state[...] = jnp.zeros((4, 128, 128), F32)

        def step(c, states):
            rows = pl.ds(pl.multiple_of(c * CHUNK, CHUNK), CHUNK)
            for h in range(4):
                sh_ref[h, c] = states[h]
            nxt, y = _gdn_scan_chunk(states, u_ref[rows, :], w_ref[rows, :], qe_ref[rows, :], kd_ref[rows, :],
                                     at_ref[rows, :], egl_ref[pl.ds(pl.multiple_of(c * 8, 8), 1), :], z_ref[rows, :],
                                     ng_ref[...])
            y_ref[rows, :] = y.astype(BF16)
            return nxt

        states = lax.fori_loop(0, ncb, step, tuple(state[h] for h in range(4)))
        for h in range(4):
            state[h] = states[h]

    blk = pl.BlockSpec((tb, BW), lambda j: (j, 0))
    vec = pl.BlockSpec((1, 128), lambda j: (0, 0))
    return pl.pallas_call(
        body, name=name, grid=(nb,),
        in_specs=[blk] * 5 + [pl.BlockSpec((ncb * 8, BW), lambda j: (j, 0)),
                              pl.BlockSpec((tb, BW), lambda j: (j, COL_CZ // BW)), vec],
        out_specs=[blk, pl.BlockSpec((4, ncb, 128, 128), lambda j: (0, j, 0, 0))],
        out_shape=[jax.ShapeDtypeStruct((T, BW), BF16), jax.ShapeDtypeStruct((4, T // CHUNK, 128, 128), F32)],
        scratch_shapes=[pltpu.VMEM((4, 128, 128), F32)],
        compiler_params=_cp("arbitrary"),
    )(u, w, qe, kd, at, egl, proj, ng)


def _gdn_bwd(name, u, w, qe, kd, at, egl, proj, dy, sh, ng):
    T = proj.shape[0]
    tb, nb, ncb = _gdn_blocks(T)

    def body(u_ref, w_ref, qe_ref, kd_ref, at_ref, egl_ref, z_ref, dy_ref, sh_ref, ng_ref,
             du_ref, dw_ref, dqe_ref, dkd_ref, dat_ref, degl_ref, dz_ref, dng_ref, dstate):
        first = pl.program_id(0) == 0

        @pl.when(first)
        def _():
            dstate[...] = jnp.zeros((4, 128, 128), F32)

        def step(it, carry):
            dstates, pn = carry
            c = ncb - 1 - it
            rows = pl.ds(pl.multiple_of(c * CHUNK, CHUNK), CHUNK)
            erow = pl.multiple_of(c * 8, 8)
            _, vjp = jax.vjp(_gdn_scan_chunk, tuple(sh_ref[h, c] for h in range(4)), u_ref[rows, :],
                             w_ref[rows, :].astype(F32), qe_ref[rows, :].astype(F32), kd_ref[rows, :].astype(F32),
                             at_ref[rows, :].astype(F32), egl_ref[pl.ds(erow, 1), :], z_ref[rows, :], ng_ref[...])
            nxt, du, dw, dqe, dkd, dat, degl, dz, dn = vjp((dstates, dy_ref[rows, :]))
            du_ref[rows, :] = du
            dw_ref[rows, :] = dw
            dqe_ref[rows, :] = dqe
            dkd_ref[rows, :] = dkd
            dat_ref[rows, :] = dat
            degl_ref[pl.ds(erow, 8), :] = jnp.broadcast_to(degl, (8, BW))
            dz_ref[rows, :] = dz.astype(BF16)
            return nxt, pn + dn

        dstates, pn = lax.fori_loop(0, ncb, step, (tuple(dstate[h] for h in range(4)), jnp.zeros((1, 128), F32)))
        for h in range(4):
            dstate[h] = dstates[h]

        @pl.when(first)
        def _():
            dng_ref[...] = pn

        @pl.when(jnp.logical_not(first))
        def _():
            dng_ref[...] += pn

    blk = pl.BlockSpec((tb, BW), lambda j: (nb - 1 - j, 0))
    eblk = pl.BlockSpec((ncb * 8, BW), lambda j: (nb - 1 - j, 0))
    vec = pl.BlockSpec((1, 128), lambda j: (0, 0))
    return pl.pallas_call(
        body, name=name, grid=(nb,),
        in_specs=[blk] * 5 + [eblk, pl.BlockSpec((tb, BW), lambda j: (nb - 1 - j, COL_CZ // BW)), blk,
                              pl.BlockSpec((4, ncb, 128, 128), lambda j: (0, nb - 1 - j, 0, 0)), vec],
        out_specs=[blk] * 5 + [eblk, blk, vec],
        out_shape=[jax.ShapeDtypeStruct((T, BW), F32)] * 5 + [jax.ShapeDtypeStruct((T // 8, BW), F32),
                                                              jax.ShapeDtypeStruct((T, BW), BF16),
                                                              jax.ShapeDtypeStruct((1, 128), F32)],
        scratch_shapes=[pltpu.VMEM((4, 128, 128), F32)],
        compiler_params=_cp("arbitrary"),
    )(u, w, qe, kd, at, egl, proj, dy, sh, ng)


def _adamw_update(w_ref, g_ref, m_ref, v_ref, d_ref, nm_ref, nv_ref):
    gv = g_ref[...]
    m2 = ADAM_B1 * m_ref[...] + (1.0 - ADAM_B1) * gv
    v2 = ADAM_B2 * v_ref[...] + (1.0 - ADAM_B2) * jnp.square(gv)
    m_hat = m2 / (1.0 - ADAM_B1 ** ADAM_STEP)
    v_hat = v2 / (1.0 - ADAM_B2 ** ADAM_STEP)
    d_ref[...] = -ADAM_LR * (m_hat / (jnp.sqrt(v_hat) + ADAM_EPS) + ADAM_WD * w_ref[...])
    nm_ref[...] = m2
    nv_ref[...] = v2


def _adamw_many(name, ws, gs, ms, vs):
    n = len(ws)

    def body(*refs):
        for i in range(n):
            _adamw_update(*[refs[k * n + i] for k in range(7)])

    return pl.pallas_call(
        body, name=name,
        out_shape=[jax.ShapeDtypeStruct(a.shape, F32) for a in ws] * 3,
        compiler_params=_cp(),
    )(*ws, *gs, *ms, *vs)


def _adamw(name, w, g, m, v):
    R, C = w.shape
    br = _pick(R, (512, 256, 240, 128, 64, 8))
    body = functools.partial(_adamw_update)
    spec = pl.BlockSpec((br, C), lambda i: (i, 0))
    return pl.pallas_call(
        body, name=name, grid=(R // br,),
        in_specs=[spec] * 4, out_specs=[spec] * 3,
        out_shape=[jax.ShapeDtypeStruct((R, C), F32)] * 3,
        compiler_params=_cp("parallel"),
    )(w, g, m, v)


def _sum8(name, parts):
    _, R, C = parts.shape
    br = _pick(R, (352, 368, 256, 128, 64, 16, 8))

    def body(p_ref, o_ref):
        acc = p_ref[0].astype(F32)
        for d in range(1, N_DEV):
            acc = acc + p_ref[d].astype(F32)
        o_ref[...] = acc

    return pl.pallas_call(
        body, name=name, grid=(R // br,),
        in_specs=[pl.BlockSpec((N_DEV, br, C), lambda i: (0, i, 0))],
        out_specs=pl.BlockSpec((br, C), lambda i: (i, 0)),
        out_shape=jax.ShapeDtypeStruct((R, C), F32),
        compiler_params=_cp("parallel"),
    )(parts)


_ANY = pl.BlockSpec(memory_space=pl.ANY)
_MESH = pl.DeviceIdType.MESH


def _all_gather(name, shard):
    R, C = shard.shape

    def body(x_ref, out_ref, send_sems, recv_sems, local_sem):
        x, y, c = lax.axis_index("x"), lax.axis_index("y"), lax.axis_index("c")
        me, sibling = (x, y, c), (x, y, 1 - c)
        chips = [(1 - x, y), (x, 1 - y), (1 - x, 1 - y)]

        def slot(px, py, pc):
            return out_ref.at[4 * px + 2 * py + pc]

        def copy(k, block, to, src=None):
            return pltpu.make_async_remote_copy(
                src_ref=slot(*block) if src is None else src, dst_ref=slot(*block),
                send_sem=send_sems.at[k], recv_sem=recv_sems.at[k], device_id=to, device_id_type=_MESH)

        mine = pltpu.make_async_copy(x_ref, slot(*me), local_sem)
        mine.start()
        first = [copy(0, me, sibling, src=x_ref)]
        first += [copy(1 + j, me, (*chip, c), src=x_ref) for j, chip in enumerate(chips)]
        for cp in first:
            cp.start()
        passed = [copy(4 + j, (*chip, c), sibling) for j, chip in enumerate(chips)]
        for j, chip in enumerate(chips):
            copy(1 + j, (*chip, c), me).wait_recv()
            passed[j].start()
        copy(0, sibling, me).wait_recv()
        for j, chip in enumerate(chips):
            copy(4 + j, (*chip, 1 - c), me).wait_recv()
        for cp in first + passed:
            cp.wait_send()
        mine.wait()

    return pl.pallas_call(
        body, name=name,
        in_specs=[_ANY], out_specs=_ANY,
        out_shape=jax.ShapeDtypeStruct((N_DEV, R, C), shard.dtype),
        scratch_shapes=[pltpu.SemaphoreType.DMA((7,)), pltpu.SemaphoreType.DMA((7,)), pltpu.SemaphoreType.DMA],
    )(shard)


_HBM = pl.BlockSpec(memory_space=pltpu.HBM)
_SEM = pl.BlockSpec(memory_space=pltpu.SEMAPHORE)
_EFFECT = pltpu.SideEffectType.DATAFLOW_SIDE_EFFECTING


def _exchange_copies(src_ref, land_ref, send_sems, recv_sems, scatter):
    x, y, c = lax.axis_index("x"), lax.axis_index("y"), lax.axis_index("c")
    me = 4 * x + 2 * y + c
    copies = []
    for k in range(1, N_DEV):
        px, py, pc = x ^ ((k >> 2) & 1), y ^ ((k >> 1) & 1), c ^ (k & 1)
        src = src_ref.at[4 * px + 2 * py + pc] if scatter else src_ref
        copies.append(pltpu.make_async_remote_copy(
            src_ref=src, dst_ref=land_ref.at[me], send_sem=send_sems.at[k - 1], recv_sem=recv_sems.at[k - 1],
            device_id=(px, py, pc), device_id_type=_MESH))
    return copies


def _exchange_start(name, srcs, lands, scatter, after=None):
    n = len(srcs)

    def body(*refs):
        src_refs, land_refs = refs[:n], refs[n:2 * n]
        outs = refs[2 * n + (after is not None):]
        send, recv = outs[:n], outs[n:2 * n]
        token = refs[-1]
        for g in range(n):
            for cp in _exchange_copies(src_refs[g], land_refs[g], send[g], recv[g], scatter):
                cp.start()
        token[...] = jnp.zeros_like(token)

    outs = pl.pallas_call(
        body, name=name,
        out_shape=tuple([pltpu.SemaphoreType.DMA((N_DEV - 1,))] * (2 * n)
                        + [pltpu.HBM(a.shape, a.dtype) for a in list(srcs) + list(lands)]
                        + [jax.ShapeDtypeStruct((8, 128), F32)]),
        in_specs=[_HBM] * (2 * n) + [_ANY] * (after is not None),
        out_specs=tuple([_SEM] * (2 * n) + [_HBM] * (2 * n) + [pl.BlockSpec(memory_space=pltpu.VMEM)]),
        input_output_aliases={i: 2 * n + i for i in range(2 * n)},
        compiler_params=pltpu.CompilerParams(has_side_effects=_EFFECT),
    )(*[pltpu.with_memory_space_constraint(a, pltpu.HBM) for a in list(srcs) + list(lands)],
      *([after] if after is not None else []))
    handles = [(outs[2 * n + g], outs[3 * n + g], outs[g], outs[n + g]) for g in range(n)]
    return handles, outs[-1]


def _exchange_wait(name, handles, after, scatter):
    n = len(handles)
    srcs, lands, sends, recvs = ([h[i] for h in handles] for i in range(4))

    def body(*refs):
        src_refs, land_refs = refs[:n], refs[n:2 * n]
        send, recv = refs[2 * n:3 * n], refs[3 * n:4 * n]
        for g in range(n):
            for cp in _exchange_copies(src_refs[g], land_refs[g], send[g], recv[g], scatter):
                cp.wait_send()
                cp.wait_recv()

    outs = pl.pallas_call(
        body, name=name,
        out_shape=tuple(pltpu.HBM(a.shape, a.dtype) for a in srcs + lands),
        in_specs=tuple([_HBM] * (2 * n) + [_SEM] * (2 * n) + [_ANY]), out_specs=tuple([_HBM] * (2 * n)),
        input_output_aliases={i: i for i in range(2 * n)},
        compiler_params=pltpu.CompilerParams(has_side_effects=_EFFECT),
    )(*srcs, *lands, *sends, *recvs, after)
    return list(outs[n:])


def _rows(a):
    return a.reshape(-1, 1024)


def _rows_to_parts(full):
    n = full.shape[-2] // N_DEV
    t = full.reshape(full.shape[:-2] + (N_DEV, n, full.shape[-1]))
    return jnp.moveaxis(t, -3, 0)


def _parts_to_rows(parts):
    t = jnp.moveaxis(parts, 0, -3)
    return t.reshape(t.shape[:-3] + (t.shape[-3] * t.shape[-2], t.shape[-1]))


def _parts_to_cols(parts):
    t = jnp.moveaxis(parts, 0, -2)
    return t.reshape(t.shape[:-2] + (t.shape[-2] * t.shape[-1],))


def _join(parts, axis=0):
    total = sum(p.shape[axis] for p in parts)
    out, off = None, 0
    for p in parts:
        cfg = [(0, 0)] * p.ndim
        cfg[axis] = (off, total - off - p.shape[axis])
        t = jnp.pad(p, cfg)
        out = t if out is None else out + t
        off += p.shape[axis]
    return out


def _w_in_to_layout(w):
    tail = jnp.pad(w[4096:4104], ((0, PW - COL_TAIL - 8), (0, 0)))
    return jnp.concatenate([w[:4096], w[4104:P_IN], tail], axis=0)


def _w_in_from_layout(g):
    return _join([g[:4096], g[COL_TAIL:COL_TAIL + 8], g[4096:COL_TAIL]], axis=0)


def _block_diag(w):
    w = w.reshape(4, 2, 64, 64)
    return jnp.pad(w[:, 0], ((0, 0), (0, 64), (0, 64))) + jnp.pad(w[:, 1], ((0, 0), (64, 0), (64, 0)))


def _block_diag_grad(g):
    return jnp.stack([g[:, :64, :64], g[:, 64:, 64:]], axis=1).reshape(8, 64, 64)


def _ffn_forward(tag, x, norm, wg, wu, wd):
    h = _rms_fwd(tag + "_norm", x, norm)
    a, b, act = _ffn_up(tag + "_up", h, wg, wu)
    if callable(wd):
        wd = wd(act)
    x_out = _mm(tag + "_down", [(act, wd)], "nn", F32, res=x, scale=0.5)
    return x_out, (x, h, a, b, act), wd


def _ffn_backward(tag, dx_out, saved, norm, wg, wu, wd, put, names, split=False):
    x, h, a, b, act = saved
    n_wg, n_wu, n_wd = names
    dwd = _mm(tag + "_dwd", [(act, dx_out)], "tn", BF16, scale=0.5, bm=FF // 2)
    tok = put({n_wd: dwd}) if split else None
    da, db = _ffn_dact(tag + "_dact", dx_out, wd, a, b, after=tok)
    dwg = _mm(tag + "_dwg", [(da, h)], "tn", BF16, bm=FF // 2)
    if split:
        tok = tok + put({n_wg: dwg})
    dwu = _mm(tag + "_dwu", [(db, h)], "tn", BF16, bm=FF // 2, after=tok)
    tok = tok + put({n_wu: dwu}) if split else put({n_wg: dwg, n_wu: dwu, n_wd: dwd})
    dh = _mm(tag + "_dh", [(da, wg), (db, wu)], "nn", F32, after=tok)
    dx, dnorm = _rms_bwd(tag + "_dnorm", x, norm + tok, dh, dx_out)
    return dx, dnorm


def _mixer_params(p):
    alog = jnp.pad(p["gdn_a_log"], (4, 120))[None]
    dtb = jnp.pad(p["gdn_dt_bias"], (4, 120))[None]
    bias = jnp.repeat(p["sgu_b"].T, 128, axis=1)
    return dict(
        ln_g=p["sgu_ln_g"][None], ln_b=p["sgu_ln_b"][None], sgu_w=p["sgu_w"], sgu_bias=bias,
        lru_cw=p["lru_conv_w"], lru_cb=p["lru_conv_b"][None], wa=_block_diag(p["lru_wa"]), ba=p["lru_ba"][None],
        wx=_block_diag(p["lru_wx"]), bx=p["lru_bx"][None], lam=p["lru_lambda"][None],
        gdn_cw=p["gdn_conv_w"], alog=alog, dtb=dtb, ng=p["gdn_norm_g"][None],
        pool_w=p["pool_w"], pool_sc=p["pool_scale"][None])


def _mix_forward(tag, x, p, mp):
    h = _rms_fwd(tag + "_norm", x, p["mix_norm"][None])
    proj = _mm(tag + "_proj", [(h, p["w_in"])], "nt", F32, bm=_pick(x.shape[0], (2048, 1024, 512, 256, 128)))
    y_a = _sgu_fwd(tag + "_sgu", proj, mp["ln_g"], mp["ln_b"], mp["sgu_w"], mp["sgu_bias"])
    y_b, hc = _lru_fwd(tag + "_lru", proj, mp["lru_cw"], mp["lru_cb"], mp["wa"], mp["ba"], mp["wx"], mp["bx"],
                       mp["lam"])
    qa = _conv_fwd(tag + "_convq", proj, COL_CQ, mp["gdn_cw"], 0)
    ka = _conv_fwd(tag + "_convk", proj, COL_CK, mp["gdn_cw"], 512)
    va = _conv_fwd(tag + "_convv", proj, COL_CV, mp["gdn_cw"], 1024)
    prep = _gdn_prep_fwd(tag + "_gdnprep", qa, ka, va, proj, mp["alog"], mp["dtb"])
    y_c, sh = _gdn_fwd(tag + "_gdn", *prep, proj, mp["ng"])
    y_d = _pool_fwd(tag + "_pool", proj, mp["pool_w"], mp["pool_sc"])
    ys = (y_a, y_b, y_c, y_d)
    merged = _merge_fwd(tag + "_merge", ys, p["w_branch"], proj)
    x_out = _mm(tag + "_out", [(merged, p["w_out"])], "nn", F32, res=x)
    return x_out, (x, h, proj, hc, qa, ka, va, prep, sh, ys, merged)


def _mix_backward(tag, dx_out, saved, p, mp, put):
    x, h, proj, hc, qa, ka, va, prep, sh, ys, merged = saved
    T = x.shape[0]
    g = {}
    dmerged = _mm(tag + "_dmerged", [(dx_out, p["w_out"])], "nt", F32)
    g["w_out"] = _mm(tag + "_dwout", [(merged, dx_out)], "tn", BF16)
    outs = _merge_bwd(tag + "_dmerge", dmerged, ys, p["w_branch"], proj)
    dgates, dbrs, dys = outs[:NBR], outs[NBR:2 * NBR], outs[2 * NBR:]
    g["w_branch"] = jnp.stack([_mm(f"{tag}_dwb{i}", [(dbrs[i], ys[i])], "tn", BF16) for i in range(NBR)])

    du, dv, dln_g, dln_b, dsgu_w, dbias = _sgu_bwd(tag + "_dsgu", proj, dys[0], mp["ln_g"], mp["ln_b"], mp["sgu_w"],
                                                  mp["sgu_bias"])
    g["sgu_ln_g"], g["sgu_ln_b"], g["sgu_w"] = dln_g[0], dln_b[0], dsgu_w
    g["sgu_b"] = dbias.reshape(128, 4, 128).sum(axis=2).T

    (dbx, dbg, dcw, dcb, dwa, dba, dwx, dbxb, dlam) = _lru_bwd(
        tag + "_dlru", proj, dys[1], hc, mp["lru_cw"], mp["lru_cb"], mp["wa"], mp["ba"], mp["wx"], mp["bx"], mp["lam"])
    g["lru_conv_w"], g["lru_conv_b"], g["lru_ba"], g["lru_bx"], g["lru_lambda"] = dcw, dcb[0], dba[0], dbxb[0], dlam[0]
    g["lru_wa"], g["lru_wx"] = _block_diag_grad(dwa), _block_diag_grad(dwx)

    *dprep, dz, dng = _gdn_bwd(tag + "_dgdn", *prep, proj, dys[2], sh, mp["ng"])
    dqa, dka, dva, dtail, dalog, ddtb = _gdn_prep_bwd(tag + "_dgdnprep", qa, ka, va, proj, mp["alog"], mp["dtb"], *dprep)
    g["gdn_a_log"], g["gdn_dt_bias"], g["gdn_norm_g"] = dalog[0, 4:8], ddtb[0, 4:8], dng[0]
    dq, dcwq = _conv_bwd(tag + "_dconvq", proj, COL_CQ, dqa, mp["gdn_cw"], 0)
    dk, dcwk = _conv_bwd(tag + "_dconvk", proj, COL_CK, dka, mp["gdn_cw"], 512)
    dv_, dcwv = _conv_bwd(tag + "_dconvv", proj, COL_CV, dva, mp["gdn_cw"], 1024)
    g["gdn_conv_w"] = jnp.concatenate([dcwq, dcwk, dcwv], axis=1)

    dd, dpw, dsc = _pool_bwd(tag + "_dpool", proj, dys[3], mp["pool_w"], mp["pool_sc"])
    g["pool_w"], g["pool_scale"] = dpw, dsc[0]

    dproj = jnp.concatenate([du, dv, dbx, dbg, dq, dk, dv_, dz, dd, *dgates, dtail,
                             jnp.zeros((T, PW - COL_TAIL - 128), BF16)], axis=1)
    dw_in = _mm(tag + "_dwin", [(dproj, h)], "tn", BF16)
    tok = put(dict(w_in=_w_in_from_layout(dw_in), w_branch=g.pop("w_branch"), w_out=g.pop("w_out")))
    dh = _mm(tag + "_dh", [(dproj, p["w_in"])], "nn", F32, bm=_pick(T, (2048, 1024, 512, 256, 128)), after=tok)
    dx, dnorm = _rms_bwd(tag + "_dnorm", x, p["mix_norm"][None] + tok, dh, dx_out)
    g["mix_norm"] = dnorm[0]
    return dx, g


_BIG = ("ff1_wg", "ff1_wu", "ff1_wd", "w_in", "w_branch", "w_out", "ff2_wg", "ff2_wu", "ff2_wd")
_COL_SHARDED = ("ff1_wg", "ff1_wu", "w_in", "w_branch", "ff2_wg", "ff2_wu")
_SMALL = ("ff1_norm", "mix_norm", "sgu_ln_g", "sgu_ln_b", "sgu_w", "sgu_b", "lru_conv_w", "lru_conv_b", "lru_wa",
          "lru_ba", "lru_wx", "lru_bx", "lru_lambda", "gdn_conv_w", "gdn_a_log", "gdn_dt_bias", "gdn_norm_g", "pool_w",
          "pool_scale", "ff2_norm", "final_norm")
_WEIGHTS = ("ff1_norm", "ff1_wg", "ff1_wu", "ff1_wd", "mix_norm", "w_in", "sgu_ln_g", "sgu_ln_b", "sgu_w", "sgu_b",
            "lru_conv_w", "lru_conv_b", "lru_wa", "lru_ba", "lru_wx", "lru_bx", "lru_lambda", "gdn_conv_w", "gdn_a_log",
            "gdn_dt_bias", "gdn_norm_g", "pool_w", "pool_scale", "w_branch", "w_out", "ff2_norm", "ff2_wg", "ff2_wu",
            "ff2_wd", "final_norm")
_CONV_SHARDED = ("lru_conv_w", "gdn_conv_w")
PACK_ROW_ALIGN = 16
_GROUPS = (("ff1", ("ff1_wg", "ff1_wu", "ff1_wd")), ("mix", ("w_in", "w_branch", "w_out")),
           ("ff2", ("ff2_wg", "ff2_wu", "ff2_wd")))


def _pad_rows(a, mult):
    pad = (-a.shape[-2]) % mult
    if pad == 0:
        return a
    return jnp.pad(a, [(0, 0)] * (a.ndim - 2) + [(0, pad), (0, 0)])


def _my_index():
    return 4 * lax.axis_index("x") + 2 * lax.axis_index("y") + lax.axis_index("c")


def _landing(own):
    return lax.dynamic_update_index_in_dim(lax.empty((N_DEV,) + own.shape, own.dtype), own, _my_index(), 0)


def _stored(n, a):
    return jnp.swapaxes(a, -1, -2) if n in _COL_SHARDED else a


_FIRST = ("ff1_wg", "ff1_wu")


def _gather_first(w):
    names = _FIRST
    shards = [_rows(_stored(n, w[n][0]).astype(BF16)) for n in names]
    got = _all_gather("gather_first", jnp.concatenate(shards, axis=0))
    out, r = {}, 0
    for n, s in zip(names, shards):
        out[n] = got[:, r:r + s.shape[0]].reshape(-1, 1024)
        r += s.shape[0]
    return out, got


def _gather_start(w, after):
    conv = _pad_rows(jnp.concatenate([w[n].reshape(1, -1) for n in _CONV_SHARDED], axis=1), 8)
    keys, srcs = ["conv"], [conv]
    for l in range(2):
        for sub, (_, names) in enumerate(_GROUPS):
            for n in names:
                if l > 0 or n not in _FIRST:
                    keys.append((l, sub, n))
                    srcs.append(_stored(n, w[n][l]).astype(BF16))
    handles, token = _exchange_start("gather_start", srcs, [_landing(s) for s in srcs], scatter=False, after=after)
    return dict(zip(keys, handles)), token


def _gather_finish(l, sub, handles, first, after):
    names = _GROUPS[sub][1]
    if (l, sub) == (0, 0):
        out = dict(first)
        out["ff1_wd"] = lambda later: _parts_to_rows(
            _exchange_wait("gather_wait_00", [handles[(0, 0, "ff1_wd")]], later, scatter=False)[0])
    else:
        lands = _exchange_wait(f"gather_wait_{l}{sub}", [handles[(l, sub, n)] for n in names], after, scatter=False)
        out = {n: _parts_to_rows(land) for n, land in zip(names, lands)}
    if "w_in" in out:
        out["w_in"] = _w_in_to_layout(out["w_in"])
    return out


def _scatter_start(l, sub, grads):
    srcs, shapes = [], []
    for n in grads:
        parts = _rows_to_parts(grads[n])
        shapes.append(parts.shape[1:])
        srcs.append(_pad_rows(parts.reshape(N_DEV, -1, 1024), PACK_ROW_ALIGN))
    me = _my_index()
    lands = [_landing(lax.dynamic_index_in_dim(s, me, 0, keepdims=False)) for s in srcs]
    tag = f"{l}{sub}" + ("" if len(grads) == len(_GROUPS[sub][1]) else "_" + "_".join(grads))
    handles, token = _exchange_start(f"scatter_start_{tag}", srcs, lands, scatter=True)
    return handles, (tag, tuple(grads), shapes), token


def _scatter_finish(l, sub, handles, meta, after):
    tag, names, shapes = meta
    lands = _exchange_wait(f"scatter_wait_{tag}", handles, after, scatter=True)
    out = {}
    for n, land, shape in zip(names, lands, shapes):
        size = 1
        for s in shape:
            size *= s
        summed = _sum8(f"sum_{l}{sub}_{n}", land)
        out[n] = _stored(n, summed[:size // 1024].reshape(shape))
    return out


def _gather_conv_finish(w, handles, after):
    gconv = _exchange_wait("gather_wait_conv", [handles["conv"]], after, scatter=False)[0][:, 0]
    full, r = {}, 0
    for n in _CONV_SHARDED:
        sz = w[n].size
        full[n] = _parts_to_cols(gconv[:, r:r + sz].reshape((N_DEV,) + w[n].shape))
        r += sz
    return full


def _forward_backward(x, tgt, w, conv, get_weights, put_grads, put_small, token):
    saved, params = [], []
    for l in range(2):
        p = {n: w[n][l] for n in _SMALL if n != "final_norm"}
        for n in _CONV_SHARDED:
            p[n] = conv[n][l]
        mp = _mixer_params(p)
        tok = token[:1, :1] if l == 0 else 0.0
        p.update(get_weights(l, 0, x))
        x, s1, p["ff1_wd"] = _ffn_forward(f"l{l}_ff1", x, p["ff1_norm"][None] + tok, p["ff1_wg"], p["ff1_wu"],
                                          p["ff1_wd"])
        p.update(get_weights(l, 1, x))
        x, s2 = _mix_forward(f"l{l}_mix", x, p, mp)
        p.update(get_weights(l, 2, x))
        x, s3, _ = _ffn_forward(f"l{l}_ff2", x, p["ff2_norm"][None], p["ff2_wg"], p["ff2_wu"], p["ff2_wd"])
        saved.append((s1, s2, s3))
        params.append((p, mp))
    loss, dx, dfinal = _final_loss("loss_head", x, w["final_norm"][None], tgt)
    tok = 0.0
    for l in (1, 0):
        p, mp = params[l]
        s1, s2, s3 = saved[l]
        g = {}

        def put(sub):
            return lambda grads, l=l: put_grads(l, sub, grads)[:1, :1]

        dx, dn = _ffn_backward(f"l{l}_ff2", dx, s3, p["ff2_norm"][None] + tok, p["ff2_wg"], p["ff2_wu"], p["ff2_wd"],
                               put(2), _GROUPS[2][1])
        g["ff2_norm"] = dn[0]
        dx, gm = _mix_backward(f"l{l}_mix", dx, s2, p, mp, put(1))
        g.update(gm)
        tok = 0.0
        if l == 0:
            tok = put_small("0a", g)[:1, :1]
            g = {}
        dx, dn = _ffn_backward(f"l{l}_ff1", dx, s1, p["ff1_norm"][None] + tok, p["ff1_wg"], p["ff1_wu"], p["ff1_wd"],
                               put(0), _GROUPS[0][1], split=(l == 0))
        g["ff1_norm"] = dn[0]
        if l == 1:
            g["final_norm"] = dfinal[0]
            g["loss"] = loss[0, :1]
        tok = put_small("1" if l == 1 else "0b", g)[:1, :1]
    return dx


SMALL_PIECE = 8 * 1024


def _pack_small(d, names):
    pieces = []
    for n in names:
        flat = d[n].reshape(-1)
        pieces.append(jnp.pad(flat, (0, (-flat.size) % SMALL_PIECE)).reshape(-1, 1024))
    return jnp.concatenate(pieces, axis=0)


def _unpack_small(pack, shapes, names):
    out, r = {}, 0
    for n in names:
        size = 1
        for s in shapes[n]:
            size *= s
        rows = -(-size // SMALL_PIECE) * 8
        out[n] = pack[r:r + rows].reshape(-1)[:size].reshape(shapes[n])
        r += rows
    return out


def _small_names(grads):
    return tuple(n for n in _SMALL + ("loss",) if n in grads)


def _small_start(tag, grads):
    pack = _pack_small(grads, _small_names(grads))
    handles, token = _exchange_start(f"small_start_{tag}", [pack], [_landing(pack)], scatter=False)
    return handles, {n: grads[n].shape for n in _small_names(grads)}, token


def _small_finish(tag, handles, shapes, after):
    landed = _exchange_wait(f"small_wait_{tag}", handles, after, scatter=False)[0]
    return _unpack_small(_sum8(f"sum_small_{tag}", landed), shapes, _small_names(shapes))


def _as2d(a):
    if a.ndim == 1:
        return a.reshape(1, -1)
    return a.reshape(-1, a.shape[-1])


def kernel(x, ff1_norm, ff1_wg, ff1_wu, ff1_wd, mix_norm, w_in, sgu_ln_g, sgu_ln_b, sgu_w, sgu_b, lru_conv_w, lru_conv_b, lru_wa, lru_ba, lru_wx, lru_bx, lru_lambda, gdn_conv_w, gdn_a_log, gdn_dt_bias, gdn_norm_g, pool_w, pool_scale, w_branch, w_out, ff2_norm, ff2_wg, ff2_wu, ff2_wd, final_norm, loss_target, m_ff1_norm, m_ff1_wg, m_ff1_wu, m_ff1_wd, m_mix_norm, m_w_in, m_sgu_ln_g, m_sgu_ln_b, m_sgu_w, m_sgu_b, m_lru_conv_w, m_lru_conv_b, m_lru_wa, m_lru_ba, m_lru_wx, m_lru_bx, m_lru_lambda, m_gdn_conv_w, m_gdn_a_log, m_gdn_dt_bias, m_gdn_norm_g, m_pool_w, m_pool_scale, m_w_branch, m_w_out, m_ff2_norm, m_ff2_wg, m_ff2_wu, m_ff2_wd, m_final_norm, v_ff1_norm, v_ff1_wg, v_ff1_wu, v_ff1_wd, v_mix_norm, v_w_in, v_sgu_ln_g, v_sgu_ln_b, v_sgu_w, v_sgu_b, v_lru_conv_w, v_lru_conv_b, v_lru_wa, v_lru_ba, v_lru_wx, v_lru_bx, v_lru_lambda, v_gdn_conv_w, v_gdn_a_log, v_gdn_dt_bias, v_gdn_norm_g, v_pool_w, v_pool_scale, v_w_branch, v_w_out, v_ff2_norm, v_ff2_wg, v_ff2_wu, v_ff2_wd, v_final_norm):
    w = dict(ff1_norm=ff1_norm, ff1_wg=ff1_wg, ff1_wu=ff1_wu, ff1_wd=ff1_wd, mix_norm=mix_norm, w_in=w_in,
             sgu_ln_g=sgu_ln_g, sgu_ln_b=sgu_ln_b, sgu_w=sgu_w, sgu_b=sgu_b, lru_conv_w=lru_conv_w,
             lru_conv_b=lru_conv_b, lru_wa=lru_wa, lru_ba=lru_ba, lru_wx=lru_wx, lru_bx=lru_bx, lru_lambda=lru_lambda,
             gdn_conv_w=gdn_conv_w, gdn_a_log=gdn_a_log, gdn_dt_bias=gdn_dt_bias, gdn_norm_g=gdn_norm_g, pool_w=pool_w,
             pool_scale=pool_scale, w_branch=w_branch, w_out=w_out, ff2_norm=ff2_norm, ff2_wg=ff2_wg, ff2_wu=ff2_wu,
             ff2_wd=ff2_wd, final_norm=final_norm)
    m = dict(ff1_norm=m_ff1_norm, ff1_wg=m_ff1_wg, ff1_wu=m_ff1_wu, ff1_wd=m_ff1_wd, mix_norm=m_mix_norm, w_in=m_w_in,
             sgu_ln_g=m_sgu_ln_g, sgu_ln_b=m_sgu_ln_b, sgu_w=m_sgu_w, sgu_b=m_sgu_b, lru_conv_w=m_lru_conv_w,
             lru_conv_b=m_lru_conv_b, lru_wa=m_lru_wa, lru_ba=m_lru_ba, lru_wx=m_lru_wx, lru_bx=m_lru_bx,
             lru_lambda=m_lru_lambda, gdn_conv_w=m_gdn_conv_w, gdn_a_log=m_gdn_a_log, gdn_dt_bias=m_gdn_dt_bias,
             gdn_norm_g=m_gdn_norm_g, pool_w=m_pool_w, pool_scale=m_pool_scale, w_branch=m_w_branch, w_out=m_w_out,
             ff2_norm=m_ff2_norm, ff2_wg=m_ff2_wg, ff2_wu=m_ff2_wu, ff2_wd=m_ff2_wd, final_norm=m_final_norm)
    v = dict(ff1_norm=v_ff1_norm, ff1_wg=v_ff1_wg, ff1_wu=v_ff1_wu, ff1_wd=v_ff1_wd, mix_norm=v_mix_norm, w_in=v_w_in,
             sgu_ln_g=v_sgu_ln_g, sgu_ln_b=v_sgu_ln_b, sgu_w=v_sgu_w, sgu_b=v_sgu_b, lru_conv_w=v_lru_conv_w,
             lru_conv_b=v_lru_conv_b, lru_wa=v_lru_wa, lru_ba=v_lru_ba, lru_wx=v_lru_wx, lru_bx=v_lru_bx,
             lru_lambda=v_lru_lambda, gdn_conv_w=v_gdn_conv_w, gdn_a_log=v_gdn_a_log, gdn_dt_bias=v_gdn_dt_bias,
             gdn_norm_g=v_gdn_norm_g, pool_w=v_pool_w, pool_scale=v_pool_scale, w_branch=v_w_branch, w_out=v_w_out,
             ff2_norm=v_ff2_norm, ff2_wg=v_ff2_wg, ff2_wu=v_ff2_wu, ff2_wd=v_ff2_wd, final_norm=v_final_norm)

    first, got_first = _gather_first(w)
    handles, token = _gather_start(w, got_first)
    conv = _gather_conv_finish(w, handles, token)
    pending = {}

    def get_weights(l, sub, after):
        return _gather_finish(l, sub, handles, first, after)

    def put_grads(l, sub, grads):
        hs, meta, tok = _scatter_start(l, sub, grads)
        pending[(l, sub, meta[0])] = (hs, meta)
        return tok

    def put_small(tag, grads):
        hs, shapes, tok = _small_start(tag, grads)
        pending[tag] = (hs, shapes)
        return tok

    T = x.shape[1]
    dx = _forward_backward(x.reshape(T, D), loss_target.reshape(T, D), w, conv, get_weights, put_grads, put_small,
                           token)
    per = {}
    for key in pending:
        if isinstance(key, tuple):
            per.setdefault(key[:2], {}).update(_scatter_finish(*key[:2], *pending[key], dx))
        else:
            per[key] = _small_finish(key, *pending[key], dx)
    grad = {n: jnp.stack([per[(0, sub)][n], per[(1, sub)][n]]) for sub, (_, names) in enumerate(_GROUPS) for n in names}
    layer0 = {**per["0a"], **per["0b"]}
    small = {n: _join([layer0[n].reshape(-1), per["1"][n].reshape(-1)]).reshape((2,) + layer0[n].shape)
             for n in layer0}
    small["final_norm"] = per["1"]["final_norm"]
    loss = per["1"]["loss"][0]
    me = _my_index()
    for n in _SMALL:
        if n in _CONV_SHARDED:
            width = w[n].shape[-1]
            grad[n] = lax.dynamic_slice_in_dim(small[n], me * width, width, axis=2)
        else:
            grad[n] = small[n]

    delta, new_m, new_v = {}, {}, {}
    for n in _BIG:
        d_, m_, v_ = _adamw("adamw_" + n, _as2d(w[n]), _as2d(grad[n]), _as2d(m[n]), _as2d(v[n]))
        delta[n], new_m[n], new_v[n] = (t.reshape(w[n].shape) for t in (d_, m_, v_))

    outs = _adamw_many("adamw_small", *[[_as2d(t[n]) for n in _SMALL] for t in (w, grad, m, v)])
    for k, dst in enumerate((delta, new_m, new_v)):
        for i, n in enumerate(_SMALL):
            dst[n] = outs[k * len(_SMALL) + i].reshape(w[n].shape)

    return (loss, dx.reshape(x.shape), *[grad[n] for n in _WEIGHTS], *[delta[n] for n in _WEIGHTS],
            *[new_m[n] for n in _WEIGHTS], *[new_v[n] for n in _WEIGHTS])
```

```python
import functools

import jax
import jax.numpy as jnp
from jax import lax
from jax.experimental import pallas as pl
from jax.experimental.pallas import tpu as pltpu

F32 = jnp.float32
BF16 = jnp.bfloat16
HI = lax.Precision.HIGHEST

N_DEV = 8
D = 1024
FF = 2816
BW = 512
NBR = 4
CHUNK = 64
EPS = 1e-6
LRU_C = 8.0
GDN_DK = 128

COL_AU, COL_AV, COL_BX, COL_BG = 0, 512, 1024, 1536
COL_CQ, COL_CK, COL_CV, COL_CZ = 2048, 2560, 3072, 3584
COL_DX, COL_GATE, COL_TAIL = 4096, 4608, 8704
PW = 9216
P_IN = 8712

ADAM_LR, ADAM_B1, ADAM_B2, ADAM_EPS, ADAM_WD, ADAM_STEP = 0.001, 0.9, 0.999, 1e-08, 0.01, 10

VMEM_LIMIT_V7X = 56 * 1024 * 1024

_NN = (((1,), (0,)), ((), ()))
_NT = (((1,), (1,)), ((), ()))
_TN = (((0,), (0,)), ((), ()))


def _cp(*sem):
    return pltpu.CompilerParams(dimension_semantics=tuple(sem), vmem_limit_bytes=VMEM_LIMIT_V7X)


def _dot(a, b, dims=_NN):
    return lax.dot_general(a.astype(BF16), b.astype(BF16), dims, preferred_element_type=F32)


def _dot_hi(a, b, dims=_NN):
    return lax.dot_general(a, b, dims, precision=HI, preferred_element_type=F32)


def _pick(n, cands):
    for c in cands:
        if n % c == 0:
            return c
    return n


@jax.custom_jvp
def _log1p(x):
    u = 1.0 + x
    return jnp.where(u == 1.0, x, x * jnp.log(u) / jnp.where(u == 1.0, 1.0, u - 1.0))


@_log1p.defjvp
def _log1p_jvp(p, t):
    (x,), (dx,) = p, t
    return _log1p(x), dx / (1.0 + x)


@jax.custom_jvp
def _expm1(x):
    u = jnp.exp(x)
    lu = jnp.log(u)
    small = (u == 1.0) | (lu == 0.0)
    return jnp.where(small, x, (u - 1.0) * x / jnp.where(small, 1.0, lu))


@_expm1.defjvp
def _expm1_jvp(p, t):
    (x,), (dx,) = p, t
    return _expm1(x), dx * jnp.exp(x)


def _softplus(x):
    return jnp.maximum(x, 0.0) + _log1p(jnp.exp(-jnp.abs(x)))


def _sigmoid(x):
    return jax.nn.sigmoid(x)


def _silu(x):
    return x * jax.nn.sigmoid(x)


def _gelu(x):
    return jax.nn.gelu(x)


@functools.partial(jax.custom_vjp, nondiff_argnums=(1,))
def _shift(x, s):
    return x if s == 0 else pltpu.roll(x, s, 0)


def _shift_fwd(x, s):
    return _shift(x, s), None


def _shift_bwd(s, _, g):
    n = g.shape[0]
    return (g if s == 0 else pltpu.roll(g, n - s, 0),)


_shift.defvjp(_shift_fwd, _shift_bwd)


def _scan_steps(a, b, reverse):
    n = a.shape[0]
    row = lax.broadcasted_iota(jnp.int32, a.shape, 0)
    k = 1
    while k < n:
        sh = n - k if reverse else k
        m = (row < n - k) if reverse else (row >= k)
        a_s = jnp.where(m, pltpu.roll(a, sh, 0), 1.0)
        b_s = jnp.where(m, pltpu.roll(b, sh, 0), 0.0)
        b = a * b_s + b
        a = a * a_s
        k *= 2
    return b


@jax.custom_vjp
def _scan(a, b):
    return _scan_steps(a, b, False)


def _scan_fwd(a, b):
    h = _scan_steps(a, b, False)
    return h, (a, h)


def _scan_bwd(res, dh):
    a, h = res
    n = a.shape[0]
    row = lax.broadcasted_iota(jnp.int32, a.shape, 0)
    a_next = jnp.where(row < n - 1, pltpu.roll(a, n - 1, 0), 0.0)
    g = _scan_steps(a_next, dh, True)
    h_prev = jnp.where(row >= 1, pltpu.roll(h, 1, 0), 0.0)
    return g * h_prev, g


_scan.defvjp(_scan_fwd, _scan_bwd)


def _mm(name, pairs, mode, out_dtype, *, res=None, scale=1.0, bm=None, bn=None, bk=None, after=None):
    a0, b0 = pairs[0]
    if mode == "nn":
        (M, K), N = a0.shape, b0.shape[1]
    elif mode == "nt":
        (M, K), N = a0.shape, b0.shape[0]
    else:
        (K, M), N = a0.shape, b0.shape[1]
    bm = bm or _pick(M, (1024, 512, 256, 128))
    bn = bn or _pick(N, (1024, 512, 256, 128))
    bk = bk or _pick(K, (1024, 512, 1408, 256, 128))
    nk = K // bk
    npair = len(pairs)
    dims = {"nn": _NN, "nt": _NT, "tn": _TN}[mode]

    def body(*refs):
        ab = refs[:2 * npair]
        pos = 2 * npair
        r_ref = None
        if res is not None:
            r_ref = refs[pos]
            pos += 1
        pos += after is not None
        o_ref = refs[pos]
        part = None
        for p in range(npair):
            d = _dot(ab[2 * p][...], ab[2 * p + 1][...], dims)
            part = d if part is None else part + d

        def finish(acc):
            out = acc if scale == 1.0 else acc * scale
            if r_ref is not None:
                out = out + r_ref[...]
            o_ref[...] = out.astype(out_dtype)

        if nk == 1:
            finish(part)
        else:
            acc_ref = refs[pos + 1]
            k = pl.program_id(2)

            @pl.when(k == 0)
            def _():
                acc_ref[...] = part

            @pl.when(k > 0)
            def _():
                acc_ref[...] += part

            @pl.when(k == nk - 1)
            def _():
                finish(acc_ref[...])

    if mode == "nn":
        a_spec = pl.BlockSpec((bm, bk), lambda i, j, k: (i, k))
        b_spec = pl.BlockSpec((bk, bn), lambda i, j, k: (k, j))
    elif mode == "nt":
        a_spec = pl.BlockSpec((bm, bk), lambda i, j, k: (i, k))
        b_spec = pl.BlockSpec((bn, bk), lambda i, j, k: (j, k))
    else:
        a_spec = pl.BlockSpec((bk, bm), lambda i, j, k: (k, i))
        b_spec = pl.BlockSpec((bk, bn), lambda i, j, k: (k, j))
    o_spec = pl.BlockSpec((bm, bn), lambda i, j, k: (i, j))
    in_specs, args = [], []
    for a, b in pairs:
        in_specs += [a_spec, b_spec]
        args += [a, b]
    if res is not None:
        in_specs.append(o_spec)
        args.append(res)
    if after is not None:
        in_specs.append(_ANY)
        args.append(after)
    return pl.pallas_call(
        body, name=name, grid=(M // bm, N // bn, nk),
        in_specs=in_specs, out_specs=o_spec,
        out_shape=jax.ShapeDtypeStruct((M, N), out_dtype),
        scratch_shapes=[pltpu.VMEM((bm, bn), F32)] if nk > 1 else [],
        compiler_params=_cp("parallel", "parallel", "arbitrary"),
    )(*args)


def _rms_fwd(name, x, g):
    T = x.shape[0]
    bm = _pick(T, (512, 256, 128))

    def body(x_ref, g_ref, o_ref):
        xv = x_ref[...]
        r = lax.rsqrt(jnp.mean(xv * xv, axis=-1, keepdims=True) + EPS)
        o_ref[...] = (xv * r * g_ref[...]).astype(BF16)

    return pl.pallas_call(
        body, name=name, grid=(T // bm,),
        in_specs=[pl.BlockSpec((bm, D), lambda i: (i, 0)), pl.BlockSpec((1, D), lambda i: (0, 0))],
        out_specs=pl.BlockSpec((bm, D), lambda i: (i, 0)),
        out_shape=jax.ShapeDtypeStruct((T, D), BF16),
        compiler_params=_cp("parallel"),
    )(x, g)


def _rms_bwd(name, x, g, dh, dres):
    T = x.shape[0]
    bm = _pick(T, (512, 256, 128))

    def body(x_ref, g_ref, dh_ref, dres_ref, dx_ref, dg_ref):
        xv = x_ref[...]
        r = lax.rsqrt(jnp.mean(xv * xv, axis=-1, keepdims=True) + EPS)
        xh = xv * r
        dhv = dh_ref[...]
        dxh = dhv * g_ref[...]
        dx_ref[...] = dres_ref[...] + r * (dxh - xh * jnp.mean(dxh * xh, axis=-1, keepdims=True))
        part = jnp.sum(dhv * xh, axis=0, keepdims=True)

        @pl.when(pl.program_id(0) == 0)
        def _():
            dg_ref[...] = part

        @pl.when(pl.program_id(0) > 0)
        def _():
            dg_ref[...] += part

    row = pl.BlockSpec((bm, D), lambda i: (i, 0))
    vec = pl.BlockSpec((1, D), lambda i: (0, 0))
    return pl.pallas_call(
        body, name=name, grid=(T // bm,),
        in_specs=[row, vec, row, row], out_specs=[row, vec],
        out_shape=[jax.ShapeDtypeStruct((T, D), F32), jax.ShapeDtypeStruct((1, D), F32)],
        compiler_params=_cp("arbitrary"),
    )(x, g, dh, dres)


def _final_loss(name, x, g, tgt):
    T = x.shape[0]
    bm = _pick(T, (512, 256, 128))

    def body(x_ref, g_ref, t_ref, loss_ref, dx_ref, dg_ref):
        xv = x_ref[...]
        gv = g_ref[...]
        r = lax.rsqrt(jnp.mean(xv * xv, axis=-1, keepdims=True) + EPS)
        xh = xv * r
        e = xh * gv - t_ref[...]
        lpart = jnp.broadcast_to(0.5 * jnp.sum(jnp.mean(e * e, axis=-1, keepdims=True), axis=0, keepdims=True), (1, 128))
        dy = e * (1.0 / D)
        dxh = dy * gv
        dx_ref[...] = r * (dxh - xh * jnp.mean(dxh * xh, axis=-1, keepdims=True))
        gpart = jnp.sum(dy * xh, axis=0, keepdims=True)

        @pl.when(pl.program_id(0) == 0)
        def _():
            loss_ref[...] = lpart
            dg_ref[...] = gpart

        @pl.when(pl.program_id(0) > 0)
        def _():
            loss_ref[...] += lpart
            dg_ref[...] += gpart

    row = pl.BlockSpec((bm, D), lambda i: (i, 0))
    vec = pl.BlockSpec((1, D), lambda i: (0, 0))
    return pl.pallas_call(
        body, name=name, grid=(T // bm,),
        in_specs=[row, vec, row],
        out_specs=[pl.BlockSpec((1, 128), lambda i: (0, 0)), row, vec],
        out_shape=[jax.ShapeDtypeStruct((1, 128), F32), jax.ShapeDtypeStruct((T, D), F32),
                   jax.ShapeDtypeStruct((1, D), F32)],
        compiler_params=_cp("arbitrary"),
    )(x, g, tgt)


def _ffn_up(name, h, wg, wu):
    T = h.shape[0]
    bm = _pick(T, (2048, 1024, 512, 256, 128))
    bn = 256

    def body(h_ref, wg_ref, wu_ref, a_ref, b_ref, act_ref):
        hv = h_ref[...]
        a = _dot(hv, wg_ref[...], _NT)
        b = _dot(hv, wu_ref[...], _NT)
        a_ref[...] = a.astype(BF16)
        b_ref[...] = b.astype(BF16)
        act_ref[...] = (_silu(a) * b).astype(BF16)

    w_spec = pl.BlockSpec((bn, D), lambda i, j: (j, 0))
    o_spec = pl.BlockSpec((bm, bn), lambda i, j: (i, j))
    return pl.pallas_call(
        body, name=name, grid=(T // bm, FF // bn),
        in_specs=[pl.BlockSpec((bm, D), lambda i, j: (i, 0)), w_spec, w_spec],
        out_specs=[o_spec, o_spec, o_spec],
        out_shape=[jax.ShapeDtypeStruct((T, FF), BF16)] * 3,
        compiler_params=_cp("parallel", "parallel"),
    )(h, wg, wu)


def _ffn_dact(name, dy, wd, a, b, after=None):
    T = dy.shape[0]
    bm = _pick(T, (2048, 1024, 512, 256, 128))
    bn = 256

    def body(dy_ref, wd_ref, a_ref, b_ref, *rest):
        da_ref, db_ref, dy_bf = rest[-3:]

        @pl.when(pl.program_id(1) == 0)
        def _():
            dy_bf[...] = dy_ref[...].astype(BF16)

        dact = 0.5 * _dot(dy_bf[...], wd_ref[...], _NT)
        av = a_ref[...].astype(F32)
        s = _sigmoid(av)
        da_ref[...] = (dact * b_ref[...].astype(F32) * (s * (1.0 + av * (1.0 - s)))).astype(BF16)
        db_ref[...] = (dact * (av * s)).astype(BF16)

    t_spec = pl.BlockSpec((bm, bn), lambda i, j: (i, j))
    return pl.pallas_call(
        body, name=name, grid=(T // bm, FF // bn),
        in_specs=[pl.BlockSpec((bm, D), lambda i, j: (i, 0)), pl.BlockSpec((bn, D), lambda i, j: (j, 0)),
                  t_spec, t_spec] + [_ANY] * (after is not None),
        out_specs=[t_spec, t_spec],
        out_shape=[jax.ShapeDtypeStruct((T, FF), BF16), jax.ShapeDtypeStruct((T, FF), BF16)],
        scratch_shapes=[pltpu.VMEM((bm, D), BF16)],
        compiler_params=_cp("parallel", "arbitrary"),
    )(dy, wd, a, b, *([after] if after is not None else []))


def _merge_specs(T, bm, bn):
    y_spec = pl.BlockSpec((bm, BW), lambda i, j: (i, 0))
    wb_spec = pl.BlockSpec((NBR, bn, BW), lambda i, j: (0, j, 0))
    gate_specs = [pl.BlockSpec((bm, bn), functools.partial(lambda i, j, o: (i, o + j), o=(COL_GATE + g * D) // bn))
                  for g in range(NBR)]
    t_spec = pl.BlockSpec((bm, bn), lambda i, j: (i, j))
    return y_spec, wb_spec, gate_specs, t_spec


def _merge_fwd(name, ys, wb, proj):
    T = proj.shape[0]
    bm = _pick(T, (512, 256, 128))
    bn = 512
    y_spec, wb_spec, gate_specs, t_spec = _merge_specs(T, bm, bn)

    def body(y0, y1, y2, y3, wb_ref, g0, g1, g2, g3, o_ref):
        acc = None
        for g, (y_ref, g_ref) in enumerate(((y0, g0), (y1, g1), (y2, g2), (y3, g3))):
            t = _sigmoid(g_ref[...]) * _dot(y_ref[...], wb_ref[g], _NT)
            acc = t if acc is None else acc + t
        o_ref[...] = acc.astype(BF16)

    return pl.pallas_call(
        body, name=name, grid=(T // bm, D // bn),
        in_specs=[y_spec] * NBR + [wb_spec] + gate_specs, out_specs=t_spec,
        out_shape=jax.ShapeDtypeStruct((T, D), BF16),
        compiler_params=_cp("parallel", "parallel"),
    )(*ys, wb, proj, proj, proj, proj)


def _merge_bwd(name, dm, ys, wb, proj):
    T = proj.shape[0]
    bm = _pick(T, (512, 256, 128))
    bn = 512
    y_spec, wb_spec, gate_specs, t_spec = _merge_specs(T, bm, bn)

    def body(dm_ref, y0, y1, y2, y3, wb_ref, g0, g1, g2, g3, *outs):
        dmv = dm_ref[...]
        j = pl.program_id(1)
        for g, (y_ref, g_ref) in enumerate(((y0, g0), (y1, g1), (y2, g2), (y3, g3))):
            br = _dot(y_ref[...], wb_ref[g], _NT)
            s = _sigmoid(g_ref[...])
            outs[g][...] = (dmv * br * (s * (1.0 - s))).astype(BF16)
            dbr = (dmv * s).astype(BF16)
            outs[NBR + g][...] = dbr
            part = _dot(dbr, wb_ref[g])
            dy_ref = outs[2 * NBR + g]

            @pl.when(j == 0)
            def _():
                dy_ref[...] = part

            @pl.when(j > 0)
            def _():
                dy_ref[...] += part

    return pl.pallas_call(
        body, name=name, grid=(T // bm, D // bn),
        in_specs=[t_spec] + [y_spec] * NBR + [wb_spec] + gate_specs, out_specs=[t_spec] * (2 * NBR) + [y_spec] * NBR,
        out_shape=[jax.ShapeDtypeStruct((T, D), BF16)] * (2 * NBR) + [jax.ShapeDtypeStruct((T, BW), F32)] * NBR,
        compiler_params=_cp("parallel", "arbitrary"),
    )(dm, *ys, wb, proj, proj, proj, proj)


def _sgu_block(u_pre, v_pre, ln_g, ln_b, w, bias):
    u = _gelu(u_pre)
    vf = _gelu(v_pre)
    mu = jnp.mean(vf, axis=-1, keepdims=True)
    var = jnp.mean(jnp.square(vf - mu), axis=-1, keepdims=True)
    vn = (vf - mu) * lax.rsqrt(var + EPS) * ln_g + ln_b
    ri = lax.broadcasted_iota(jnp.int32, (128, 128), 0)
    ci = lax.broadcasted_iota(jnp.int32, (128, 128), 1)
    mask = (ri // CHUNK) >= (ci // CHUNK)
    outs = [_dot(jnp.where(mask, w[g], 0.0), vn[:, g * 128:(g + 1) * 128]) for g in range(4)]
    mixed = jnp.concatenate(outs, axis=1) + bias
    return u * mixed


def _sgu_param_specs():
    return [pl.BlockSpec((1, BW), lambda i: (0, 0)), pl.BlockSpec((1, BW), lambda i: (0, 0)),
            pl.BlockSpec((4, 128, 128), lambda i: (0, 0, 0)), pl.BlockSpec((128, BW), lambda i: (0, 0))]


def _sgu_fwd(name, proj, ln_g, ln_b, w, bias):
    T = proj.shape[0]
    rb = _pick(T, (256, 128))

    def body(u_ref, v_ref, g_ref, b_ref, w_ref, bias_ref, y_ref):
        for n in range(rb // 128):
            rows = slice(n * 128, (n + 1) * 128)
            y = _sgu_block(u_ref[rows, :], v_ref[rows, :], g_ref[...], b_ref[...], w_ref[...], bias_ref[...])
            y_ref[rows, :] = y.astype(BF16)

    return pl.pallas_call(
        body, name=name, grid=(T // rb,),
        in_specs=[pl.BlockSpec((rb, BW), lambda i: (i, COL_AU // BW)), pl.BlockSpec((rb, BW), lambda i: (i, COL_AV // BW))]
        + _sgu_param_specs(),
        out_specs=pl.BlockSpec((rb, BW), lambda i: (i, 0)),
        out_shape=jax.ShapeDtypeStruct((T, BW), BF16),
        compiler_params=_cp("parallel"),
    )(proj, proj, ln_g, ln_b, w, bias)


def _sgu_bwd(name, proj, dy, ln_g, ln_b, w, bias):
    T = proj.shape[0]
    rb = _pick(T, (256, 128))

    def body(u_ref, v_ref, dy_ref, g_ref, b_ref, w_ref, bias_ref, du_ref, dv_ref, dg_ref, db_ref, dw_ref, dbias_ref):
        acc = None
        for n in range(rb // 128):
            rows = slice(n * 128, (n + 1) * 128)
            _, vjp = jax.vjp(_sgu_block, u_ref[rows, :], v_ref[rows, :], g_ref[...], b_ref[...], w_ref[...],
                             bias_ref[...])
            du, dv, *dp = vjp(dy_ref[rows, :])
            du_ref[rows, :] = du.astype(BF16)
            dv_ref[rows, :] = dv.astype(BF16)
            acc = dp if acc is None else [p + q for p, q in zip(acc, dp)]

        @pl.when(pl.program_id(0) == 0)
        def _():
            for r, p in zip((dg_ref, db_ref, dw_ref, dbias_ref), acc):
                r[...] = p

        @pl.when(pl.program_id(0) > 0)
        def _():
            for r, p in zip((dg_ref, db_ref, dw_ref, dbias_ref), acc):
                r[...] += p

    row = pl.BlockSpec((rb, BW), lambda i: (i, 0))
    return pl.pallas_call(
        body, name=name, grid=(T // rb,),
        in_specs=[pl.BlockSpec((rb, BW), lambda i: (i, COL_AU // BW)), pl.BlockSpec((rb, BW), lambda i: (i, COL_AV // BW)),
                  row] + _sgu_param_specs(),
        out_specs=[row, row] + _sgu_param_specs(),
        out_shape=[jax.ShapeDtypeStruct((T, BW), BF16), jax.ShapeDtypeStruct((T, BW), BF16),
                   jax.ShapeDtypeStruct((1, BW), F32), jax.ShapeDtypeStruct((1, BW), F32),
                   jax.ShapeDtypeStruct((4, 128, 128), F32), jax.ShapeDtypeStruct((128, BW), F32)],
        compiler_params=_cp("arbitrary"),
    )(proj, proj, dy, ln_g, ln_b, w, bias)


def _halo_block(ref, i, rblk, halo):
    r0 = pl.multiple_of(i * rblk, rblk)
    h0 = pl.multiple_of(jnp.maximum(r0 - halo, 0), halo)
    top = jnp.where(i > 0, ref[pl.ds(h0, halo), :], 0.0)
    return jnp.concatenate([top, ref[pl.ds(r0, rblk), :]], axis=0)


def _with_halo_grad(dfull, pending, halo, rblk):
    tail = jnp.concatenate([jnp.zeros((rblk - halo, 128), F32), pending], axis=0)
    return dfull[halo:] + tail


def _conv4(xfull, rows):
    acc = None
    for k in range(4):
        t = rows[k] * _shift(xfull, 3 - k)[8:]
        acc = t if acc is None else acc + t
    return acc


def _lru_block(xfull, gate, h0, c0, c1, c2, c3, cb, wa, ba, wx, bx, lam):
    n = gate.shape[0]
    xc = _conv4(xfull, (c0, c1, c2, c3)) + cb
    r = _sigmoid(_dot(xc, wa) + ba)
    ig = _sigmoid(_dot(xc, wx) + bx)
    log_a = -LRU_C * r * _softplus(-lam)
    a = jnp.exp(log_a)
    mult = jnp.sqrt(-_expm1(2.0 * log_a))
    b = mult * (ig * xc)
    row = lax.broadcasted_iota(jnp.int32, (n, 128), 0)
    b = b + jnp.where(row == 0, a * h0, 0.0)
    h = _scan(a, b)
    out = h * _gelu(gate)
    h_last = jnp.sum(jnp.where(row == n - 1, h, 0.0), axis=0, keepdims=True)
    return out, h_last


def _lru_param_specs():
    vec = pl.BlockSpec((1, 128), lambda g: (0, g))
    mat = pl.BlockSpec((None, 128, 128), lambda g: (g, 0, 0))
    return [pl.BlockSpec((4, 128), lambda g: (0, g)), vec, mat, vec, mat, vec, vec]


def _lru_load_params(cw_ref, cb_ref, wa_ref, ba_ref, wx_ref, bx_ref, lam_ref):
    return (cw_ref[0:1, :], cw_ref[1:2, :], cw_ref[2:3, :], cw_ref[3:4, :], cb_ref[...], wa_ref[...], ba_ref[...],
            wx_ref[...], bx_ref[...], lam_ref[...])


def _lru_fwd(name, proj, cw, cb, wa, ba, wx, bx, lam):
    T = proj.shape[0]
    rblk = _pick(T, (256, 128))
    nblk = T // rblk

    def body(x_ref, gt_ref, cw_ref, cb_ref, wa_ref, ba_ref, wx_ref, bx_ref, lam_ref, y_ref, hc_ref):
        params = _lru_load_params(cw_ref, cb_ref, wa_ref, ba_ref, wx_ref, bx_ref, lam_ref)

        def step(i, h0):
            r0 = pl.multiple_of(i * rblk, rblk)
            out, h_last = _lru_block(_halo_block(x_ref, i, rblk, 8), gt_ref[pl.ds(r0, rblk), :], h0, *params)
            y_ref[pl.ds(r0, rblk), :] = out.astype(BF16)
            hc_ref[pl.ds(pl.multiple_of(i * 8, 8), 8), :] = jnp.broadcast_to(h0, (8, 128))
            return h_last

        lax.fori_loop(0, nblk, step, jnp.zeros((1, 128), F32))

    return pl.pallas_call(
        body, name=name, grid=(4,),
        in_specs=[pl.BlockSpec((T, 128), lambda g: (0, COL_BX // 128 + g)),
                  pl.BlockSpec((T, 128), lambda g: (0, COL_BG // 128 + g))] + _lru_param_specs(),
        out_specs=[pl.BlockSpec((T, 128), lambda g: (0, g)), pl.BlockSpec((nblk * 8, 128), lambda g: (0, g))],
        out_shape=[jax.ShapeDtypeStruct((T, BW), BF16), jax.ShapeDtypeStruct((nblk * 8, BW), F32)],
        compiler_params=_cp("parallel"),
    )(proj, proj, cw, cb, wa, ba, wx, bx, lam)


def _lru_bwd(name, proj, dy, hc, cw, cb, wa, ba, wx, bx, lam):
    T = proj.shape[0]
    rblk = _pick(T, (256, 128))
    nblk = T // rblk

    def body(x_ref, gt_ref, dy_ref, hc_ref, cw_ref, cb_ref, wa_ref, ba_ref, wx_ref, bx_ref, lam_ref,
             dx_ref, dgt_ref, dcw_ref, dcb_ref, dwa_ref, dba_ref, dwx_ref, dbx_ref, dlam_ref):
        params = _lru_load_params(cw_ref, cb_ref, wa_ref, ba_ref, wx_ref, bx_ref, lam_ref)

        def step(it, carry):
            dh_last, pending, acc = carry
            i = nblk - 1 - it
            r0 = pl.multiple_of(i * rblk, rblk)
            h0 = hc_ref[pl.ds(pl.multiple_of(i * 8, 8), 1), :]
            _, vjp = jax.vjp(_lru_block, _halo_block(x_ref, i, rblk, 8), gt_ref[pl.ds(r0, rblk), :], h0, *params)
            dfull, dgate, dh0, *dp = vjp((dy_ref[pl.ds(r0, rblk), :], dh_last))
            dx_ref[pl.ds(r0, rblk), :] = _with_halo_grad(dfull, pending, 8, rblk).astype(BF16)
            dgt_ref[pl.ds(r0, rblk), :] = dgate.astype(BF16)
            return dh0, dfull[:8], tuple(p + q for p, q in zip(acc, dp))

        zeros = tuple(jnp.zeros(p.shape, F32) for p in params)
        _, _, acc = lax.fori_loop(0, nblk, step, (jnp.zeros((1, 128), F32), jnp.zeros((8, 128), F32), zeros))
        for k in range(4):
            dcw_ref[k:k + 1, :] = acc[k]
        for r, p in zip((dcb_ref, dwa_ref, dba_ref, dwx_ref, dbx_ref, dlam_ref), acc[4:]):
            r[...] = p

    col = pl.BlockSpec((T, 128), lambda g: (0, g))
    return pl.pallas_call(
        body, name=name, grid=(4,),
        in_specs=[pl.BlockSpec((T, 128), lambda g: (0, COL_BX // 128 + g)),
                  pl.BlockSpec((T, 128), lambda g: (0, COL_BG // 128 + g)), col,
                  pl.BlockSpec((nblk * 8, 128), lambda g: (0, g))] + _lru_param_specs(),
        out_specs=[col, col] + _lru_param_specs(),
        out_shape=[jax.ShapeDtypeStruct((T, BW), BF16), jax.ShapeDtypeStruct((T, BW), BF16),
                   jax.ShapeDtypeStruct((4, BW), F32), jax.ShapeDtypeStruct((1, BW), F32),
                   jax.ShapeDtypeStruct((4, 128, 128), F32), jax.ShapeDtypeStruct((1, BW), F32),
                   jax.ShapeDtypeStruct((4, 128, 128), F32), jax.ShapeDtypeStruct((1, BW), F32),
                   jax.ShapeDtypeStruct((1, BW), F32)],
        compiler_params=_cp("parallel"),
    )(proj, proj, dy, hc, cw, cb, wa, ba, wx, bx, lam)


def _conv_block(xfull, c0, c1, c2, c3):
    return _silu(_conv4(xfull, (c0, c1, c2, c3)))


def _conv_fwd(name, proj, col0, cw, cw_col0):
    T = proj.shape[0]
    rblk = _pick(T, (256, 128))
    nblk = T // rblk

    def body(x_ref, cw_ref, y_ref):
        rows = (cw_ref[0:1, :], cw_ref[1:2, :], cw_ref[2:3, :], cw_ref[3:4, :])

        def step(i, c):
            r0 = pl.multiple_of(i * rblk, rblk)
            y_ref[pl.ds(r0, rblk), :] = _conv_block(_halo_block(x_ref, i, rblk, 8), *rows)
            return c

        lax.fori_loop(0, nblk, step, 0)

    return pl.pallas_call(
        body, name=name, grid=(4,),
        in_specs=[pl.BlockSpec((T, 128), lambda g: (0, col0 // 128 + g)),
                  pl.BlockSpec((4, 128), lambda g: (0, cw_col0 // 128 + g))],
        out_specs=pl.BlockSpec((T, 128), lambda g: (0, g)),
        out_shape=jax.ShapeDtypeStruct((T, BW), F32),
        compiler_params=_cp("parallel"),
    )(proj, cw)


def _conv_bwd(name, proj, col0, dy, cw, cw_col0):
    T = proj.shape[0]
    rblk = _pick(T, (256, 128))
    nblk = T // rblk

    def body(x_ref, dy_ref, cw_ref, dx_ref, dcw_ref):
        rows = (cw_ref[0:1, :], cw_ref[1:2, :], cw_ref[2:3, :], cw_ref[3:4, :])

        def step(it, carry):
            pending, acc = carry
            i = nblk - 1 - it
            r0 = pl.multiple_of(i * rblk, rblk)
            _, vjp = jax.vjp(_conv_block, _halo_block(x_ref, i, rblk, 8), *rows)
            dfull, *dp = vjp(dy_ref[pl.ds(r0, rblk), :])
            dx_ref[pl.ds(r0, rblk), :] = _with_halo_grad(dfull, pending, 8, rblk).astype(BF16)
            return dfull[:8], tuple(p + q for p, q in zip(acc, dp))

        zeros = tuple(jnp.zeros((1, 128), F32) for _ in range(4))
        _, acc = lax.fori_loop(0, nblk, step, (jnp.zeros((8, 128), F32), zeros))
        for k in range(4):
            dcw_ref[k:k + 1, :] = acc[k]

    col = pl.BlockSpec((T, 128), lambda g: (0, g))
    return pl.pallas_call(
        body, name=name, grid=(4,),
        in_specs=[pl.BlockSpec((T, 128), lambda g: (0, col0 // 128 + g)), col,
                  pl.BlockSpec((4, 128), lambda g: (0, cw_col0 // 128 + g))],
        out_specs=[col, pl.BlockSpec((4, 128), lambda g: (0, g))],
        out_shape=[jax.ShapeDtypeStruct((T, BW), BF16), jax.ShapeDtypeStruct((4, BW), F32)],
        compiler_params=_cp("parallel"),
    )(proj, dy, cw)


def _pool_block(xfull, pw, sc, t0, gi):
    n = xfull.shape[0] - 16
    s2 = xfull + _shift(xfull, 1)
    s4 = s2 + _shift(s2, 2)
    s8 = s4 + _shift(s4, 4)
    s16 = s8 + _shift(s8, 8)
    s = jnp.where(gi == 0, s2, jnp.where(gi == 1, s4, jnp.where(gi == 2, s8, s16)))[16:]
    t = t0 + lax.broadcasted_iota(jnp.int32, (n, 128), 0)
    cnt = jnp.minimum(t + 1, lax.shift_left(jnp.int32(2), gi)).astype(F32)
    pooled = s / cnt - xfull[16:]
    return _dot(pooled, pw) * sc


def _pool_fwd(name, proj, pw, sc):
    T = proj.shape[0]
    rblk = _pick(T, (256, 128))
    nblk = T // rblk

    def body(x_ref, pw_ref, sc_ref, y_ref):
        gi = pl.program_id(0)

        def step(i, c):
            r0 = pl.multiple_of(i * rblk, rblk)
            y = _pool_block(_halo_block(x_ref, i, rblk, 16), pw_ref[...], sc_ref[...], r0, gi)
            y_ref[pl.ds(r0, rblk), :] = y.astype(BF16)
            return c

        lax.fori_loop(0, nblk, step, 0)

    return pl.pallas_call(
        body, name=name, grid=(4,),
        in_specs=[pl.BlockSpec((T, 128), lambda g: (0, COL_DX // 128 + g)),
                  pl.BlockSpec((None, 128, 128), lambda g: (g, 0, 0)), pl.BlockSpec((1, 128), lambda g: (0, g))],
        out_specs=pl.BlockSpec((T, 128), lambda g: (0, g)),
        out_shape=jax.ShapeDtypeStruct((T, BW), BF16),
        compiler_params=_cp("parallel"),
    )(proj, pw, sc)


def _pool_bwd(name, proj, dy, pw, sc):
    T = proj.shape[0]
    rblk = _pick(T, (256, 128))
    nblk = T // rblk

    def body(x_ref, dy_ref, pw_ref, sc_ref, dx_ref, dpw_ref, dsc_ref):
        gi = pl.program_id(0)

        def step(it, carry):
            pending, apw, asc = carry
            i = nblk - 1 - it
            r0 = pl.multiple_of(i * rblk, rblk)
            _, vjp = jax.vjp(lambda xf, w, s: _pool_block(xf, w, s, r0, gi), _halo_block(x_ref, i, rblk, 16),
                             pw_ref[...], sc_ref[...])
            dfull, dw, ds = vjp(dy_ref[pl.ds(r0, rblk), :])
            dx_ref[pl.ds(r0, rblk), :] = _with_halo_grad(dfull, pending, 16, rblk).astype(BF16)
            return dfull[:16], apw + dw, asc + ds

        _, apw, asc = lax.fori_loop(0, nblk, step, (jnp.zeros((16, 128), F32), jnp.zeros((128, 128), F32),
                                                    jnp.zeros((1, 128), F32)))
        dpw_ref[...] = apw
        dsc_ref[...] = asc

    col = pl.BlockSpec((T, 128), lambda g: (0, g))
    mat = pl.BlockSpec((None, 128, 128), lambda g: (g, 0, 0))
    vec = pl.BlockSpec((1, 128), lambda g: (0, g))
    return pl.pallas_call(
        body, name=name, grid=(4,),
        in_specs=[pl.BlockSpec((T, 128), lambda g: (0, COL_DX // 128 + g)), col, mat, vec],
        out_specs=[col, mat, vec],
        out_shape=[jax.ShapeDtypeStruct((T, BW), BF16), jax.ShapeDtypeStruct((4, 128, 128), F32),
                   jax.ShapeDtypeStruct((1, BW), F32)],
        compiler_params=_cp("parallel"),
    )(proj, dy, pw, sc)


@jax.custom_vjp
def _dot3(a, b):
    ah = a.astype(BF16)
    al = (a - ah.astype(F32)).astype(BF16)
    bh = b.astype(BF16)
    bl = (b - bh.astype(F32)).astype(BF16)

    def d(x, y):
        return lax.dot_general(x, y, _NN, preferred_element_type=F32)

    return d(ah, bh) + (d(ah, bl) + d(al, bh))


def _dot3_fwd(a, b):
    return _dot3(a, b), (a, b)


def _dot3_bwd(res, g):
    a, b = res
    return _dot(g, b, _NT), _dot(a, g, _TN)


_dot3.defvjp(_dot3_fwd, _dot3_bwd)


def _pad_rows2(x):
    return jnp.concatenate([x, jnp.zeros_like(x)], axis=0)


@jax.custom_vjp
def _tri_inv(mats):
    n = mats[0].shape[0]
    eye = (lax.broadcasted_iota(jnp.int32, (n, n), 0) == lax.broadcasted_iota(jnp.int32, (n, n), 1)).astype(F32)
    ps = [eye - a for a in mats]
    ms = list(mats)
    k = 2
    while k < n:
        ms = [_dot3(t, t) for t in ms]
        ps = [p + _dot3(p, t) for p, t in zip(ps, ms)]
        k *= 2
    return ps


def _tri_inv_fwd(mats):
    ts = _tri_inv(mats)
    return ts, ts


def _tri_inv_bwd(ts, gs):
    half = [_dot(t, g, _TN) for t, g in zip(ts, gs)]
    return ([-_dot(h, t, _NT) for h, t in zip(half, ts)],)


_tri_inv.defvjp(_tri_inv_fwd, _tri_inv_bwd)


def _cumsum_rows(x):
    n = x.shape[0]
    row = lax.broadcasted_iota(jnp.int32, x.shape, 0)
    k = 1
    while k < n:
        x = x + jnp.where(row >= k, _shift(x, k), 0.0)
        k *= 2
    return x


def _gdn_prep(qcs, kcs, vcs, tails, alog, dtb):
    C = CHUNK
    pairs = [(c, h) for c in range(len(qcs)) for h in range(4)]
    lane = lax.broadcasted_iota(jnp.int32, (C, 128), 1)
    row = lax.broadcasted_iota(jnp.int32, (C, 128), 0)
    incl = row >= lane
    sig = [_sigmoid(t) for t in tails]
    gfull = [-jnp.exp(alog) * _softplus(t + dtb) for t in tails]
    beta = [jnp.sum(jnp.where(lane == h, sig[c], 0.0), axis=1, keepdims=True) for c, h in pairs]
    g = [jnp.sum(jnp.where(lane == h + 4, gfull[c], 0.0), axis=1, keepdims=True) for c, h in pairs]
    qs = [qcs[c][:, h * 128:(h + 1) * 128] for c, h in pairs]
    ks = [kcs[c][:, h * 128:(h + 1) * 128] for c, h in pairs]
    vs = [vcs[c][:, h * 128:(h + 1) * 128] for c, h in pairs]
    q = [t * lax.rsqrt(jnp.sum(t * t, axis=-1, keepdims=True) + EPS) * (GDN_DK ** -0.5) for t in qs]
    k = [t * lax.rsqrt(jnp.sum(t * t, axis=-1, keepdims=True) + EPS) for t in ks]
    gc = [_cumsum_rows(jnp.broadcast_to(t, (C, 128))) for t in g]
    gc_t = [jnp.transpose(jnp.concatenate([t, t], axis=0)) for t in gc]
    gc_col = [jnp.sum(jnp.where(lane == 0, t, 0.0), axis=1, keepdims=True) for t in gc]
    ri = lax.broadcasted_iota(jnp.int32, (C, C), 0)
    ci = lax.broadcasted_iota(jnp.int32, (C, C), 1)
    decay = [jnp.exp(jnp.where(incl, a - b[:C, :], -1e30)) for a, b in zip(gc, gc_t)]
    decay_sq = [jnp.exp(jnp.where(ri > ci, a - jnp.transpose(b)[:C, :], -1e30)) for a, b in zip(gc_col, gc)]
    kb = [a * b for a, b in zip(k, beta)]
    kk = [_dot(a, b, _NT) for a, b in zip(kb, k)]
    t_mat = _tri_inv([jnp.where(ri > ci, a * b, 0.0) for a, b in zip(kk, decay_sq)])
    egc = [jnp.exp(t) for t in gc]
    u = [_dot(t, a * b) for t, a, b in zip(t_mat, vs, beta)]
    w = [_dot(t, a * b) for t, a, b in zip(t_mat, kb, egc)]
    qk = [_dot(a, _pad_rows2(b), _NT) for a, b in zip(q, k)]
    attn = [jnp.where(incl, a * b, 0.0) for a, b in zip(qk, decay)]
    g_last = [jnp.sum(jnp.where(row == C - 1, t, 0.0), axis=0, keepdims=True) for t in gc]
    qe = [a * b for a, b in zip(q, egc)]
    kd = [a * jnp.exp(b - c_) for a, b, c_ in zip(k, g_last, gc)]
    egl = [jnp.exp(t) for t in g_last]

    def per_chunk(vals):
        return [jnp.concatenate(vals[4 * c:4 * c + 4], axis=1) for c in range(len(qcs))]

    return tuple(per_chunk(t) for t in (u, w, qe, kd, attn, egl))


def _gdn_scan_chunk(states, u, w, qe, kd, attn, egl, z, ng):
    hs = range(4)

    def sl(t, h):
        return t[:, h * 128:(h + 1) * 128]

    ws = [_dot(sl(w, h), states[h]) for h in hs]
    qs = [_dot(sl(qe, h), states[h]) for h in hs]
    v_new = [sl(u, h) - ws[h] for h in hs]
    av = [_dot(sl(attn, h), _pad_rows2(v_new[h])) for h in hs]
    kv = [_dot(sl(kd, h), v_new[h], _TN) for h in hs]
    nxt = tuple(states[h] * sl(egl, h) + kv[h] for h in hs)
    o = [qs[h] + av[h] for h in hs]
    on = [t * lax.rsqrt(jnp.mean(t * t, axis=-1, keepdims=True) + EPS) * ng for t in o]
    return nxt, jnp.concatenate(on, axis=1) * _silu(z)


def _gdn_blocks(T):
    tb = _pick(T, (512, 256, 128, 64))
    return tb, T // tb, tb // CHUNK


PREP_CHUNKS = 4


def _chunk_rows(i, n):
    return [pl.ds(pl.multiple_of((i * n + j) * CHUNK, CHUNK), CHUNK) for j in range(n)]


def _egl_rows(i, n, size):
    return [pl.ds(pl.multiple_of((i * n + j) * 8, 8), size) for j in range(n)]


def _gdn_prep_fwd(name, qa, ka, va, proj, alog, dtb):
    T = proj.shape[0]
    tb, nb, ncb = _gdn_blocks(T)
    n = PREP_CHUNKS if ncb % PREP_CHUNKS == 0 else 1

    def body(q_ref, k_ref, v_ref, tail_ref, alog_ref, dtb_ref, u_ref, w_ref, qe_ref, kd_ref, at_ref, egl_ref):
        def step(i, c):
            rows = _chunk_rows(i, n)
            u, w, qe, kd, at, egl = _gdn_prep([q_ref[r, :] for r in rows], [k_ref[r, :] for r in rows],
                                              [v_ref[r, :] for r in rows], [tail_ref[r, :] for r in rows],
                                              alog_ref[...], dtb_ref[...])
            for j, (r, e) in enumerate(zip(rows, _egl_rows(i, n, 8))):
                u_ref[r, :] = u[j]
                w_ref[r, :] = w[j].astype(BF16)
                qe_ref[r, :] = qe[j].astype(BF16)
                kd_ref[r, :] = kd[j].astype(BF16)
                at_ref[r, :] = at[j].astype(BF16)
                egl_ref[e, :] = jnp.broadcast_to(egl[j], (8, BW))
            return c

        lax.fori_loop(0, ncb // n, step, 0)

    blk = pl.BlockSpec((tb, BW), lambda j: (j, 0))
    vec = pl.BlockSpec((1, 128), lambda j: (0, 0))
    return pl.pallas_call(
        body, name=name, grid=(nb,),
        in_specs=[blk, blk, blk, pl.BlockSpec((tb, 128), lambda j: (j, COL_TAIL // 128)), vec, vec],
        out_specs=[blk] * 5 + [pl.BlockSpec((ncb * 8, BW), lambda j: (j, 0))],
        out_shape=[jax.ShapeDtypeStruct((T, BW), F32)] + [jax.ShapeDtypeStruct((T, BW), BF16)] * 4
        + [jax.ShapeDtypeStruct((T // 8, BW), F32)],
        compiler_params=_cp("parallel"),
    )(qa, ka, va, proj, alog, dtb)


def _gdn_prep_bwd(name, qa, ka, va, proj, alog, dtb, du, dw, dqe, dkd, dat, degl):
    T = proj.shape[0]
    tb, nb, ncb = _gdn_blocks(T)
    n = PREP_CHUNKS if ncb % PREP_CHUNKS == 0 else 1

    def body(q_ref, k_ref, v_ref, tail_ref, alog_ref, dtb_ref, du_ref, dw_ref, dqe_ref, dkd_ref, dat_ref, degl_ref,
             dq_ref, dk_ref, dv_ref, dtail_ref, dalog_ref, ddtb_ref):
        first = pl.program_id(0) == 0

        def step(i, carry):
            pa, pd = carry
            rows = _chunk_rows(i, n)
            _, vjp = jax.vjp(_gdn_prep, [q_ref[r, :] for r in rows], [k_ref[r, :] for r in rows],
                             [v_ref[r, :] for r in rows], [tail_ref[r, :] for r in rows], alog_ref[...], dtb_ref[...])
            cot = tuple([ref[r, :] for r in rows] for ref in (du_ref, dw_ref, dqe_ref, dkd_ref, dat_ref))
            dq, dk, dv, dtail, da, dd = vjp(cot + ([degl_ref[e, :] for e in _egl_rows(i, n, 1)],))
            for j, r in enumerate(rows):
                dq_ref[r, :] = dq[j]
                dk_ref[r, :] = dk[j]
                dv_ref[r, :] = dv[j]
                dtail_ref[r, :] = dtail[j].astype(BF16)
            return pa + da, pd + dd

        zv = jnp.zeros((1, 128), F32)
        pa, pd = lax.fori_loop(0, ncb // n, step, (zv, zv))

        @pl.when(first)
        def _():
            dalog_ref[...] = pa
            ddtb_ref[...] = pd

        @pl.when(jnp.logical_not(first))
        def _():
            dalog_ref[...] += pa
            ddtb_ref[...] += pd

    blk = pl.BlockSpec((tb, BW), lambda j: (j, 0))
    vec = pl.BlockSpec((1, 128), lambda j: (0, 0))
    return pl.pallas_call(
        body, name=name, grid=(nb,),
        in_specs=[blk, blk, blk, pl.BlockSpec((tb, 128), lambda j: (j, COL_TAIL // 128)), vec, vec]
        + [blk] * 5 + [pl.BlockSpec((ncb * 8, BW), lambda j: (j, 0))],
        out_specs=[blk, blk, blk, pl.BlockSpec((tb, 128), lambda j: (j, 0)), vec, vec],
        out_shape=[jax.ShapeDtypeStruct((T, BW), F32)] * 3 + [jax.ShapeDtypeStruct((T, 128), BF16)]
        + [jax.ShapeDtypeStruct((1, 128), F32)] * 2,
        compiler_params=_cp("arbitrary"),
    )(qa, ka, va, proj, alog, dtb, du, dw, dqe, dkd, dat, degl)


def _gdn_fwd(name, u, w, qe, kd, at, egl, proj, ng):
    T = proj.shape[0]
    tb, nb, ncb = _gdn_blocks(T)

    def body(u_ref, w_ref, qe_ref, kd_ref, at_ref, egl_ref, z_ref, ng_ref, y_ref, sh_ref, state):
        @pl.when(pl.program_id(0) == 0)
        def _():
            state[...] = jnp.zeros((4, 128, 128), F32)

        def step(c, states):
            rows = pl.ds(pl.multiple_of(c * CHUNK, CHUNK), CHUNK)
            for h in range(4):
                sh_ref[h, c] = states[h]
            nxt, y = _gdn_scan_chunk(states, u_ref[rows, :], w_ref[rows, :], qe_ref[rows, :], kd_ref[rows, :],
                                     at_ref[rows, :], egl_ref[pl.ds(pl.multiple_of(c * 8, 8), 1), :], z_ref[rows, :],
                                     ng_ref[...])
            y_ref[rows, :] = y.astype(BF16)
            return nxt

        states = lax.fori_loop(0, ncb, step, tuple(state[h] for h in range(4)))
        for h in range(4):
            state[h] = states[h]

    blk = pl.BlockSpec((tb, BW), lambda j: (j, 0))
    vec = pl.BlockSpec((1, 128), lambda j: (0, 0))
    return pl.pallas_call(
        body, name=name, grid=(nb,),
        in_specs=[blk] * 5 + [pl.BlockSpec((ncb * 8, BW), lambda j: (j, 0)),
                              pl.BlockSpec((tb, BW), lambda j: (j, COL_CZ // BW)), vec],
        out_specs=[blk, pl.BlockSpec((4, ncb, 128, 128), lambda j: (0, j, 0, 0))],
        out_shape=[jax.ShapeDtypeStruct((T, BW), BF16), jax.ShapeDtypeStruct((4, T // CHUNK, 128, 128), F32)],
        scratch_shapes=[pltpu.VMEM((4, 128, 128), F32)],
        compiler_params=_cp("arbitrary"),
    )(u, w, qe, kd, at, egl, proj, ng)


def _gdn_bwd(name, u, w, qe, kd, at, egl, proj, dy, sh, ng):
    T = proj.shape[0]
    tb, nb, ncb = _gdn_blocks(T)

    def body(u_ref, w_ref, qe_ref, kd_ref, at_ref, egl_ref, z_ref, dy_ref, sh_ref, ng_ref,
             du_ref, dw_ref, dqe_ref, dkd_ref, dat_ref, degl_ref, dz_ref, dng_ref, dstate):
        first = pl.program_id(0) == 0

        @pl.when(first)
        def _():
            dstate[...] = jnp.zeros((4, 128, 128), F32)

        def step(it, carry):
            dstates, pn = carry
            c = ncb - 1 - it
            rows = pl.ds(pl.multiple_of(c * CHUNK, CHUNK), CHUNK)
            erow = pl.multiple_of(c * 8, 8)
            _, vjp = jax.vjp(_gdn_scan_chunk, tuple(sh_ref[h, c] for h in range(4)), u_ref[rows, :],
                             w_ref[rows, :].astype(F32), qe_ref[rows, :].astype(F32), kd_ref[rows, :].astype(F32),
                             at_ref[rows, :].astype(F32), egl_ref[pl.ds(erow, 1), :], z_ref[rows, :], ng_ref[...])
            nxt, du, dw, dqe, dkd, dat, degl, dz, dn = vjp((dstates, dy_ref[rows, :]))
            du_ref[rows, :] = du
            dw_ref[rows, :] = dw
            dqe_ref[rows, :] = dqe
            dkd_ref[rows, :] = dkd
            dat_ref[rows, :] = dat
            degl_ref[pl.ds(erow, 8), :] = jnp.broadcast_to(degl, (8, BW))
            dz_ref[rows, :] = dz.astype(BF16)
            return nxt, pn + dn

        dstates, pn = lax.fori_loop(0, ncb, step, (tuple(dstate[h] for h in range(4)), jnp.zeros((1, 128), F32)))
        for h in range(4):
            dstate[h] = dstates[h]

        @pl.when(first)
        def _():
            dng_ref[...] = pn

        @pl.when(jnp.logical_not(first))
        def _():
            dng_ref[...] += pn

    blk = pl.BlockSpec((tb, BW), lambda j: (nb - 1 - j, 0))
    eblk = pl.BlockSpec((ncb * 8, BW), lambda j: (nb - 1 - j, 0))
    vec = pl.BlockSpec((1, 128), lambda j: (0, 0))
    return pl.pallas_call(
        body, name=name, grid=(nb,),
        in_specs=[blk] * 5 + [eblk, pl.BlockSpec((tb, BW), lambda j: (nb - 1 - j, COL_CZ // BW)), blk,
                              pl.BlockSpec((4, ncb, 128, 128), lambda j: (0, nb - 1 - j, 0, 0)), vec],
        out_specs=[blk] * 5 + [eblk, blk, vec],
        out_shape=[jax.ShapeDtypeStruct((T, BW), F32)] * 5 + [jax.ShapeDtypeStruct((T // 8, BW), F32),
                                                              jax.ShapeDtypeStruct((T, BW), BF16),
                                                              jax.ShapeDtypeStruct((1, 128), F32)],
        scratch_shapes=[pltpu.VMEM((4, 128, 128), F32)],
        compiler_params=_cp("arbitrary"),
    )(u, w, qe, kd, at, egl, proj, dy, sh, ng)


def _adamw_update(w_ref, g_ref, m_ref, v_ref, d_ref, nm_ref, nv_ref):
    gv = g_ref[...]
    m2 = ADAM_B1 * m_ref[...] + (1.0 - ADAM_B1) * gv
    v2 = ADAM_B2 * v_ref[...] + (1.0 - ADAM_B2) * jnp.square(gv)
    m_hat = m2 / (1.0 - ADAM_B1 ** ADAM_STEP)
    v_hat = v2 / (1.0 - ADAM_B2 ** ADAM_STEP)
    d_ref[...] = -ADAM_LR * (m_hat / (jnp.sqrt(v_hat) + ADAM_EPS) + ADAM_WD * w_ref[...])
    nm_ref[...] = m2
    nv_ref[...] = v2


def _adamw_many(name, ws, gs, ms, vs):
    n = len(ws)

    def body(*refs):
        for i in range(n):
            _adamw_update(*[refs[k * n + i] for k in range(7)])

    return pl.pallas_call(
        body, name=name,
        out_shape=[jax.ShapeDtypeStruct(a.shape, F32) for a in ws] * 3,
        compiler_params=_cp(),
    )(*ws, *gs, *ms, *vs)


def _adamw(name, w, g, m, v):
    R, C = w.shape
    br = _pick(R, (512, 256, 240, 128, 64, 8))
    body = functools.partial(_adamw_update)
    spec = pl.BlockSpec((br, C), lambda i: (i, 0))
    return pl.pallas_call(
        body, name=name, grid=(R // br,),
        in_specs=[spec] * 4, out_specs=[spec] * 3,
        out_shape=[jax.ShapeDtypeStruct((R, C), F32)] * 3,
        compiler_params=_cp("parallel"),
    )(w, g, m, v)


def _sum8(name, parts):
    _, R, C = parts.shape
    br = _pick(R, (352, 368, 256, 128, 64, 16, 8))

    def body(p_ref, o_ref):
        acc = p_ref[0].astype(F32)
        for d in range(1, N_DEV):
            acc = acc + p_ref[d].astype(F32)
        o_ref[...] = acc

    return pl.pallas_call(
        body, name=name, grid=(R // br,),
        in_specs=[pl.BlockSpec((N_DEV, br, C), lambda i: (0, i, 0))],
        out_specs=pl.BlockSpec((br, C), lambda i: (i, 0)),
        out_shape=jax.ShapeDtypeStruct((R, C), F32),
        compiler_params=_cp("parallel"),
    )(parts)


_ANY = pl.BlockSpec(memory_space=pl.ANY)
_MESH = pl.DeviceIdType.MESH


def _all_gather(name, shard):
    R, C = shard.shape

    def body(x_ref, out_ref, send_sems, recv_sems, local_sem):
        x, y, c = lax.axis_index("x"), lax.axis_index("y"), lax.axis_index("c")
        me, sibling = (x, y, c), (x, y, 1 - c)
        chips = [(1 - x, y), (x, 1 - y), (1 - x, 1 - y)]

        def slot(px, py, pc):
            return out_ref.at[4 * px + 2 * py + pc]

        def copy(k, block, to, src=None):
            return pltpu.make_async_remote_copy(
                src_ref=slot(*block) if src is None else src, dst_ref=slot(*block),
                send_sem=send_sems.at[k], recv_sem=recv_sems.at[k], device_id=to, device_id_type=_MESH)

        mine = pltpu.make_async_copy(x_ref, slot(*me), local_sem)
        mine.start()
        first = [copy(0, me, sibling, src=x_ref)]
        first += [copy(1 + j, me, (*chip, c), src=x_ref) for j, chip in enumerate(chips)]
        for cp in first:
            cp.start()
        passed = [copy(4 + j, (*chip, c), sibling) for j, chip in enumerate(chips)]
        for j, chip in enumerate(chips):
            copy(1 + j, (*chip, c), me).wait_recv()
            passed[j].start()
        copy(0, sibling, me).wait_recv()
        for j, chip in enumerate(chips):
            copy(4 + j, (*chip, 1 - c), me).wait_recv()
        for cp in first + passed:
            cp.wait_send()
        mine.wait()

    return pl.pallas_call(
        body, name=name,
        in_specs=[_ANY], out_specs=_ANY,
        out_shape=jax.ShapeDtypeStruct((N_DEV, R, C), shard.dtype),
        scratch_shapes=[pltpu.SemaphoreType.DMA((7,)), pltpu.SemaphoreType.DMA((7,)), pltpu.SemaphoreType.DMA],
    )(shard)


_HBM = pl.BlockSpec(memory_space=pltpu.HBM)
_SEM = pl.BlockSpec(memory_space=pltpu.SEMAPHORE)
_EFFECT = pltpu.SideEffectType.DATAFLOW_SIDE_EFFECTING


def _exchange_copies(src_ref, land_ref, send_sems, recv_sems, scatter):
    x, y, c = lax.axis_index("x"), lax.axis_index("y"), lax.axis_index("c")
    me = 4 * x + 2 * y + c
    copies = []
    for k in range(1, N_DEV):
        px, py, pc = x ^ ((k >> 2) & 1), y ^ ((k >> 1) & 1), c ^ (k & 1)
        src = src_ref.at[4 * px + 2 * py + pc] if scatter else src_ref
        copies.append(pltpu.make_async_remote_copy(
            src_ref=src, dst_ref=land_ref.at[me], send_sem=send_sems.at[k - 1], recv_sem=recv_sems.at[k - 1],
            device_id=(px, py, pc), device_id_type=_MESH))
    return copies


def _own_copy(src_ref, land_ref, send_sems, scatter):
    me = 4 * lax.axis_index("x") + 2 * lax.axis_index("y") + lax.axis_index("c")
    return pltpu.make_async_copy(src_ref.at[me] if scatter else src_ref, land_ref.at[me], send_sems.at[N_DEV - 1])


def _exchange_start(name, srcs, lands, scatter, after=None):
    n = len(srcs)

    def body(*refs):
        src_refs, land_refs = refs[:n], refs[n:2 * n]
        outs = refs[2 * n + (after is not None):]
        send, recv = outs[:n], outs[n:2 * n]
        token = refs[-1]
        for g in range(n):
            for cp in _exchange_copies(src_refs[g], land_refs[g], send[g], recv[g], scatter):
                cp.start()
            _own_copy(src_refs[g], land_refs[g], send[g], scatter).start()
        token[...] = jnp.zeros_like(token)

    outs = pl.pallas_call(
        body, name=name,
        out_shape=tuple([pltpu.SemaphoreType.DMA((N_DEV,))] * (2 * n)
                        + [pltpu.HBM(a.shape, a.dtype) for a in list(srcs) + list(lands)]
                        + [jax.ShapeDtypeStruct((8, 128), F32)]),
        in_specs=[_HBM] * (2 * n) + [_ANY] * (after is not None),
        out_specs=tuple([_SEM] * (2 * n) + [_HBM] * (2 * n) + [pl.BlockSpec(memory_space=pltpu.VMEM)]),
        input_output_aliases={i: 2 * n + i for i in range(2 * n)},
        compiler_params=pltpu.CompilerParams(has_side_effects=_EFFECT),
    )(*[pltpu.with_memory_space_constraint(a, pltpu.HBM) for a in list(srcs) + list(lands)],
      *([after] if after is not None else []))
    handles = [(outs[2 * n + g], outs[3 * n + g], outs[g], outs[n + g]) for g in range(n)]
    return handles, outs[-1]


def _exchange_wait(name, handles, after, scatter):
    n = len(handles)
    srcs, lands, sends, recvs = ([h[i] for h in handles] for i in range(4))

    def body(*refs):
        src_refs, land_refs = refs[:n], refs[n:2 * n]
        send, recv = refs[2 * n:3 * n], refs[3 * n:4 * n]
        for g in range(n):
            for cp in _exchange_copies(src_refs[g], land_refs[g], send[g], recv[g], scatter):
                cp.wait_send()
                cp.wait_recv()
            _own_copy(src_refs[g], land_refs[g], send[g], scatter).wait()

    outs = pl.pallas_call(
        body, name=name,
        out_shape=tuple(pltpu.HBM(a.shape, a.dtype) for a in srcs + lands),
        in_specs=tuple([_HBM] * (2 * n) + [_SEM] * (2 * n) + [_ANY]), out_specs=tuple([_HBM] * (2 * n)),
        input_output_aliases={i: i for i in range(2 * n)},
        compiler_params=pltpu.CompilerParams(has_side_effects=_EFFECT),
    )(*srcs, *lands, *sends, *recvs, after)
    return list(outs[n:])


def _rows(a):
    return a.reshape(-1, 1024)


def _rows_to_parts(full):
    n = full.shape[-2] // N_DEV
    t = full.reshape(full.shape[:-2] + (N_DEV, n, full.shape[-1]))
    return jnp.moveaxis(t, -3, 0)


def _parts_to_rows(parts):
    t = jnp.moveaxis(parts, 0, -3)
    return t.reshape(t.shape[:-3] + (t.shape[-3] * t.shape[-2], t.shape[-1]))


def _parts_to_cols(parts):
    t = jnp.moveaxis(parts, 0, -2)
    return t.reshape(t.shape[:-2] + (t.shape[-2] * t.shape[-1],))


def _join(parts, axis=0):
    total = sum(p.shape[axis] for p in parts)
    out, off = None, 0
    for p in parts:
        cfg = [(0, 0)] * p.ndim
        cfg[axis] = (off, total - off - p.shape[axis])
        t = jnp.pad(p, cfg)
        out = t if out is None else out + t
        off += p.shape[axis]
    return out


def _w_in_to_layout(w):
    tail = jnp.pad(w[4096:4104], ((0, PW - COL_TAIL - 8), (0, 0)))
    return jnp.concatenate([w[:4096], w[4104:P_IN], tail], axis=0)


def _w_in_from_layout(g):
    return _join([g[:4096], g[COL_TAIL:COL_TAIL + 8], g[4096:COL_TAIL]], axis=0)


def _block_diag(w):
    w = w.reshape(4, 2, 64, 64)
    return jnp.pad(w[:, 0], ((0, 0), (0, 64), (0, 64))) + jnp.pad(w[:, 1], ((0, 0), (64, 0), (64, 0)))


def _block_diag_grad(g):
    return jnp.stack([g[:, :64, :64], g[:, 64:, 64:]], axis=1).reshape(8, 64, 64)


def _ffn_forward(tag, x, norm, wg, wu, wd):
    h = _rms_fwd(tag + "_norm", x, norm)
    a, b, act = _ffn_up(tag + "_up", h, wg, wu)
    if callable(wd):
        wd = wd(act)
    x_out = _mm(tag + "_down", [(act, wd)], "nn", F32, res=x, scale=0.5)
    return x_out, (x, h, a, b, act), wd


def _ffn_backward(tag, dx_out, saved, norm, wg, wu, wd, put, names, split=False):
    x, h, a, b, act = saved
    n_wg, n_wu, n_wd = names
    dwd = _mm(tag + "_dwd", [(act, dx_out)], "tn", BF16, scale=0.5, bm=FF // 2)
    tok = put({n_wd: dwd}) if split else None
    da, db = _ffn_dact(tag + "_dact", dx_out, wd, a, b, after=tok)
    dwg = _mm(tag + "_dwg", [(da, h)], "tn", BF16, bm=FF // 2)
    if split:
        tok = tok + put({n_wg: dwg})
    dwu = _mm(tag + "_dwu", [(db, h)], "tn", BF16, bm=FF // 2, after=tok)
    tok = tok + put({n_wu: dwu}) if split else put({n_wg: dwg, n_wu: dwu, n_wd: dwd})
    dh = _mm(tag + "_dh", [(da, wg), (db, wu)], "nn", F32, after=tok)
    dx, dnorm = _rms_bwd(tag + "_dnorm", x, norm + tok, dh, dx_out)
    return dx, dnorm


def _mixer_params(p):
    alog = jnp.pad(p["gdn_a_log"], (4, 120))[None]
    dtb = jnp.pad(p["gdn_dt_bias"], (4, 120))[None]
    bias = jnp.repeat(p["sgu_b"].T, 128, axis=1)
    return dict(
        ln_g=p["sgu_ln_g"][None], ln_b=p["sgu_ln_b"][None], sgu_w=p["sgu_w"], sgu_bias=bias,
        lru_cw=p["lru_conv_w"], lru_cb=p["lru_conv_b"][None], wa=_block_diag(p["lru_wa"]), ba=p["lru_ba"][None],
        wx=_block_diag(p["lru_wx"]), bx=p["lru_bx"][None], lam=p["lru_lambda"][None],
        gdn_cw=p["gdn_conv_w"], alog=alog, dtb=dtb, ng=p["gdn_norm_g"][None],
        pool_w=p["pool_w"], pool_sc=p["pool_scale"][None])


def _mix_forward(tag, x, p, mp):
    h = _rms_fwd(tag + "_norm", x, p["mix_norm"][None])
    proj = _mm(tag + "_proj", [(h, p["w_in"])], "nt", F32, bm=_pick(x.shape[0], (2048, 1024, 512, 256, 128)))
    y_a = _sgu_fwd(tag + "_sgu", proj, mp["ln_g"], mp["ln_b"], mp["sgu_w"], mp["sgu_bias"])
    y_b, hc = _lru_fwd(tag + "_lru", proj, mp["lru_cw"], mp["lru_cb"], mp["wa"], mp["ba"], mp["wx"], mp["bx"],
                       mp["lam"])
    qa = _conv_fwd(tag + "_convq", proj, COL_CQ, mp["gdn_cw"], 0)
    ka = _conv_fwd(tag + "_convk", proj, COL_CK, mp["gdn_cw"], 512)
    va = _conv_fwd(tag + "_convv", proj, COL_CV, mp["gdn_cw"], 1024)
    prep = _gdn_prep_fwd(tag + "_gdnprep", qa, ka, va, proj, mp["alog"], mp["dtb"])
    y_c, sh = _gdn_fwd(tag + "_gdn", *prep, proj, mp["ng"])
    y_d = _pool_fwd(tag + "_pool", proj, mp["pool_w"], mp["pool_sc"])
    ys = (y_a, y_b, y_c, y_d)
    merged = _merge_fwd(tag + "_merge", ys, p["w_branch"], proj)
    x_out = _mm(tag + "_out", [(merged, p["w_out"])], "nn", F32, res=x)
    return x_out, (x, h, proj, hc, qa, ka, va, prep, sh, ys, merged)


def _mix_backward(tag, dx_out, saved, p, mp, put):
    x, h, proj, hc, qa, ka, va, prep, sh, ys, merged = saved
    T = x.shape[0]
    g = {}
    dmerged = _mm(tag + "_dmerged", [(dx_out, p["w_out"])], "nt", F32)
    g["w_out"] = _mm(tag + "_dwout", [(merged, dx_out)], "tn", BF16)
    outs = _merge_bwd(tag + "_dmerge", dmerged, ys, p["w_branch"], proj)
    dgates, dbrs, dys = outs[:NBR], outs[NBR:2 * NBR], outs[2 * NBR:]
    g["w_branch"] = jnp.stack([_mm(f"{tag}_dwb{i}", [(dbrs[i], ys[i])], "tn", BF16) for i in range(NBR)])

    du, dv, dln_g, dln_b, dsgu_w, dbias = _sgu_bwd(tag + "_dsgu", proj, dys[0], mp["ln_g"], mp["ln_b"], mp["sgu_w"],
                                                  mp["sgu_bias"])
    g["sgu_ln_g"], g["sgu_ln_b"], g["sgu_w"] = dln_g[0], dln_b[0], dsgu_w
    g["sgu_b"] = dbias.reshape(128, 4, 128).sum(axis=2).T

    (dbx, dbg, dcw, dcb, dwa, dba, dwx, dbxb, dlam) = _lru_bwd(
        tag + "_dlru", proj, dys[1], hc, mp["lru_cw"], mp["lru_cb"], mp["wa"], mp["ba"], mp["wx"], mp["bx"], mp["lam"])
    g["lru_conv_w"], g["lru_conv_b"], g["lru_ba"], g["lru_bx"], g["lru_lambda"] = dcw, dcb[0], dba[0], dbxb[0], dlam[0]
    g["lru_wa"], g["lru_wx"] = _block_diag_grad(dwa), _block_diag_grad(dwx)

    *dprep, dz, dng = _gdn_bwd(tag + "_dgdn", *prep, proj, dys[2], sh, mp["ng"])
    dqa, dka, dva, dtail, dalog, ddtb = _gdn_prep_bwd(tag + "_dgdnprep", qa, ka, va, proj, mp["alog"], mp["dtb"], *dprep)
    g["gdn_a_log"], g["gdn_dt_bias"], g["gdn_norm_g"] = dalog[0, 4:8], ddtb[0, 4:8], dng[0]
    dq, dcwq = _conv_bwd(tag + "_dconvq", proj, COL_CQ, dqa, mp["gdn_cw"], 0)
    dk, dcwk = _conv_bwd(tag + "_dconvk", proj, COL_CK, dka, mp["gdn_cw"], 512)
    dv_, dcwv = _conv_bwd(tag + "_dconvv", proj, COL_CV, dva, mp["gdn_cw"], 1024)
    g["gdn_conv_w"] = jnp.concatenate([dcwq, dcwk, dcwv], axis=1)

    dd, dpw, dsc = _pool_bwd(tag + "_dpool", proj, dys[3], mp["pool_w"], mp["pool_sc"])
    g["pool_w"], g["pool_scale"] = dpw, dsc[0]

    dproj = jnp.concatenate([du, dv, dbx, dbg, dq, dk, dv_, dz, dd, *dgates, dtail,
                             jnp.zeros((T, PW - COL_TAIL - 128), BF16)], axis=1)
    dw_in = _mm(tag + "_dwin", [(dproj, h)], "tn", BF16)
    tok = put(dict(w_in=_w_in_from_layout(dw_in), w_branch=g.pop("w_branch"), w_out=g.pop("w_out")))
    dh = _mm(tag + "_dh", [(dproj, p["w_in"])], "nn", F32, bm=_pick(T, (2048, 1024, 512, 256, 128)), after=tok)
    dx, dnorm = _rms_bwd(tag + "_dnorm", x, p["mix_norm"][None] + tok, dh, dx_out)
    g["mix_norm"] = dnorm[0]
    return dx, g


_BIG = ("ff1_wg", "ff1_wu", "ff1_wd", "w_in", "w_branch", "w_out", "ff2_wg", "ff2_wu", "ff2_wd")
_COL_SHARDED = ("ff1_wg", "ff1_wu", "w_in", "w_branch", "ff2_wg", "ff2_wu")
_SMALL = ("ff1_norm", "mix_norm", "sgu_ln_g", "sgu_ln_b", "sgu_w", "sgu_b", "lru_conv_w", "lru_conv_b", "lru_wa",
          "lru_ba", "lru_wx", "lru_bx", "lru_lambda", "gdn_conv_w", "gdn_a_log", "gdn_dt_bias", "gdn_norm_g", "pool_w",
          "pool_scale", "ff2_norm", "final_norm")
_WEIGHTS = ("ff1_norm", "ff1_wg", "ff1_wu", "ff1_wd", "mix_norm", "w_in", "sgu_ln_g", "sgu_ln_b", "sgu_w", "sgu_b",
            "lru_conv_w", "lru_conv_b", "lru_wa", "lru_ba", "lru_wx", "lru_bx", "lru_lambda", "gdn_conv_w", "gdn_a_log",
            "gdn_dt_bias", "gdn_norm_g", "pool_w", "pool_scale", "w_branch", "w_out", "ff2_norm", "ff2_wg", "ff2_wu",
            "ff2_wd", "final_norm")
_CONV_SHARDED = ("lru_conv_w", "gdn_conv_w")
PACK_ROW_ALIGN = 16
_GROUPS = (("ff1", ("ff1_wg", "ff1_wu", "ff1_wd")), ("mix", ("w_in", "w_branch", "w_out")),
           ("ff2", ("ff2_wg", "ff2_wu", "ff2_wd")))


def _pad_rows(a, mult):
    pad = (-a.shape[-2]) % mult
    if pad == 0:
        return a
    return jnp.pad(a, [(0, 0)] * (a.ndim - 2) + [(0, pad), (0, 0)])


def _my_index():
    return 4 * lax.axis_index("x") + 2 * lax.axis_index("y") + lax.axis_index("c")


def _landing(shape, dtype):
    return lax.empty((N_DEV,) + tuple(shape), dtype)


def _stored(n, a):
    return jnp.swapaxes(a, -1, -2) if n in _COL_SHARDED else a


_FIRST = ("ff1_wg", "ff1_wu", "ff1_wd")


def _gather_first(w):
    names = _FIRST
    shards = [_rows(_stored(n, w[n][0]).astype(BF16)) for n in names]
    got = _all_gather("gather_first", jnp.concatenate(shards, axis=0))
    out, r = {}, 0
    for n, s in zip(names, shards):
        out[n] = got[:, r:r + s.shape[0]].reshape(-1, 1024)
        r += s.shape[0]
    return out, got


def _gather_start(w, after):
    conv = _pad_rows(jnp.concatenate([w[n].reshape(1, -1) for n in _CONV_SHARDED], axis=1), 8)
    keys, srcs = ["conv"], [conv]
    for l in range(2):
        for sub, (_, names) in enumerate(_GROUPS):
            for n in names:
                if l > 0 or n not in _FIRST:
                    keys.append((l, sub, n))
                    srcs.append(_stored(n, w[n][l]).astype(BF16))
    lands = [_landing(s.shape, s.dtype) for s in srcs]
    handles, token = _exchange_start("gather_start", srcs, lands, scatter=False, after=after)
    return dict(zip(keys, handles)), token


def _gather_finish(l, sub, handles, first, after):
    names = _GROUPS[sub][1]
    if (l, sub) == (0, 0):
        out = dict(first)
        for n in names:
            if n not in _FIRST:
                out[n] = lambda later, n=n: _parts_to_rows(
                    _exchange_wait(f"gather_wait_00_{n}", [handles[(0, 0, n)]], later, scatter=False)[0])
    else:
        lands = _exchange_wait(f"gather_wait_{l}{sub}", [handles[(l, sub, n)] for n in names], after, scatter=False)
        out = {n: _parts_to_rows(land) for n, land in zip(names, lands)}
    if "w_in" in out:
        out["w_in"] = _w_in_to_layout(out["w_in"])
    return out


def _scatter_start(l, sub, grads):
    srcs, shapes = [], []
    for n in grads:
        parts = _rows_to_parts(grads[n])
        shapes.append(parts.shape[1:])
        srcs.append(_pad_rows(parts.reshape(N_DEV, -1, 1024), PACK_ROW_ALIGN))
    lands = [_landing(s.shape[1:], s.dtype) for s in srcs]
    tag = f"{l}{sub}" + ("" if len(grads) == len(_GROUPS[sub][1]) else "_" + "_".join(grads))
    handles, token = _exchange_start(f"scatter_start_{tag}", srcs, lands, scatter=True)
    return handles, (tag, tuple(grads), shapes), token


def _scatter_finish(l, sub, handles, meta, after):
    tag, names, shapes = meta
    lands = _exchange_wait(f"scatter_wait_{tag}", handles, after, scatter=True)
    out = {}
    for n, land, shape in zip(names, lands, shapes):
        size = 1
        for s in shape:
            size *= s
        summed = _sum8(f"sum_{l}{sub}_{n}", land)
        out[n] = _stored(n, summed[:size // 1024].reshape(shape))
    return out


def _gather_conv_finish(w, handles, after):
    gconv = _exchange_wait("gather_wait_conv", [handles["conv"]], after, scatter=False)[0][:, 0]
    full, r = {}, 0
    for n in _CONV_SHARDED:
        sz = w[n].size
        full[n] = _parts_to_cols(gconv[:, r:r + sz].reshape((N_DEV,) + w[n].shape))
        r += sz
    return full


def _forward_backward(x, tgt, w, conv, get_weights, put_grads, put_small, token):
    saved, params = [], []
    for l in range(2):
        p = {n: w[n][l] for n in _SMALL if n != "final_norm"}
        for n in _CONV_SHARDED:
            p[n] = conv[n][l]
        mp = _mixer_params(p)
        tok = token[:1, :1] if l == 0 else 0.0
        p.update(get_weights(l, 0, x))
        x, s1, p["ff1_wd"] = _ffn_forward(f"l{l}_ff1", x, p["ff1_norm"][None] + tok, p["ff1_wg"], p["ff1_wu"],
                                          p["ff1_wd"])
        p.update(get_weights(l, 1, x))
        x, s2 = _mix_forward(f"l{l}_mix", x, p, mp)
        p.update(get_weights(l, 2, x))
        x, s3, _ = _ffn_forward(f"l{l}_ff2", x, p["ff2_norm"][None], p["ff2_wg"], p["ff2_wu"], p["ff2_wd"])
        saved.append((s1, s2, s3))
        params.append((p, mp))
    loss, dx, dfinal = _final_loss("loss_head", x, w["final_norm"][None], tgt)
    tok = 0.0
    for l in (1, 0):
        p, mp = params[l]
        s1, s2, s3 = saved[l]
        g = {}

        def put(sub):
            return lambda grads, l=l: put_grads(l, sub, grads)[:1, :1]

        dx, dn = _ffn_backward(f"l{l}_ff2", dx, s3, p["ff2_norm"][None] + tok, p["ff2_wg"], p["ff2_wu"], p["ff2_wd"],
                               put(2), _GROUPS[2][1])
        g["ff2_norm"] = dn[0]
        dx, gm = _mix_backward(f"l{l}_mix", dx, s2, p, mp, put(1))
        g.update(gm)
        tok = 0.0
        if l == 0:
            tok = put_small("0a", g)[:1, :1]
            g = {}
        dx, dn = _ffn_backward(f"l{l}_ff1", dx, s1, p["ff1_norm"][None] + tok, p["ff1_wg"], p["ff1_wu"], p["ff1_wd"],
                               put(0), _GROUPS[0][1], split=(l == 0))
        g["ff1_norm"] = dn[0]
        if l == 1:
            g["final_norm"] = dfinal[0]
            g["loss"] = loss[0, :1]
        tok = put_small("1" if l == 1 else "0b", g)[:1, :1]
    return dx


SMALL_PIECE = 8 * 1024


def _pack_small(d, names):
    pieces = []
    for n in names:
        flat = d[n].reshape(-1)
        pieces.append(jnp.pad(flat, (0, (-flat.size) % SMALL_PIECE)).reshape(-1, 1024))
    return jnp.concatenate(pieces, axis=0)


def _unpack_small(pack, shapes, names):
    out, r = {}, 0
    for n in names:
        size = 1
        for s in shapes[n]:
            size *= s
        rows = -(-size // SMALL_PIECE) * 8
        out[n] = pack[r:r + rows].reshape(-1)[:size].reshape(shapes[n])
        r += rows
    return out


def _small_names(grads):
    return tuple(n for n in _SMALL + ("loss",) if n in grads)


def _small_start(tag, grads):
    pack = _pack_small(grads, _small_names(grads))
    handles, token = _exchange_start(f"small_start_{tag}", [pack], [_landing(pack.shape, pack.dtype)], scatter=False)
    return handles, {n: grads[n].shape for n in _small_names(grads)}, token


def _small_finish(tag, handles, shapes, after):
    landed = _exchange_wait(f"small_wait_{tag}", handles, after, scatter=False)[0]
    return _unpack_small(_sum8(f"sum_small_{tag}", landed), shapes, _small_names(shapes))


def _as2d(a):
    if a.ndim == 1:
        return a.reshape(1, -1)
    return a.reshape(-1, a.shape[-1])


def kernel(x, ff1_norm, ff1_wg, ff1_wu, ff1_wd, mix_norm, w_in, sgu_ln_g, sgu_ln_b, sgu_w, sgu_b, lru_conv_w, lru_conv_b, lru_wa, lru_ba, lru_wx, lru_bx, lru_lambda, gdn_conv_w, gdn_a_log, gdn_dt_bias, gdn_norm_g, pool_w, pool_scale, w_branch, w_out, ff2_norm, ff2_wg, ff2_wu, ff2_wd, final_norm, loss_target, m_ff1_norm, m_ff1_wg, m_ff1_wu, m_ff1_wd, m_mix_norm, m_w_in, m_sgu_ln_g, m_sgu_ln_b, m_sgu_w, m_sgu_b, m_lru_conv_w, m_lru_conv_b, m_lru_wa, m_lru_ba, m_lru_wx, m_lru_bx, m_lru_lambda, m_gdn_conv_w, m_gdn_a_log, m_gdn_dt_bias, m_gdn_norm_g, m_pool_w, m_pool_scale, m_w_branch, m_w_out, m_ff2_norm, m_ff2_wg, m_ff2_wu, m_ff2_wd, m_final_norm, v_ff1_norm, v_ff1_wg, v_ff1_wu, v_ff1_wd, v_mix_norm, v_w_in, v_sgu_ln_g, v_sgu_ln_b, v_sgu_w, v_sgu_b, v_lru_conv_w, v_lru_conv_b, v_lru_wa, v_lru_ba, v_lru_wx, v_lru_bx, v_lru_lambda, v_gdn_conv_w, v_gdn_a_log, v_gdn_dt_bias, v_gdn_norm_g, v_pool_w, v_pool_scale, v_w_branch, v_w_out, v_ff2_norm, v_ff2_wg, v_ff2_wu, v_ff2_wd, v_final_norm):
    w = dict(ff1_norm=ff1_norm, ff1_wg=ff1_wg, ff1_wu=ff1_wu, ff1_wd=ff1_wd, mix_norm=mix_norm, w_in=w_in,
             sgu_ln_g=sgu_ln_g, sgu_ln_b=sgu_ln_b, sgu_w=sgu_w, sgu_b=sgu_b, lru_conv_w=lru_conv_w,
             lru_conv_b=lru_conv_b, lru_wa=lru_wa, lru_ba=lru_ba, lru_wx=lru_wx, lru_bx=lru_bx, lru_lambda=lru_lambda,
             gdn_conv_w=gdn_conv_w, gdn_a_log=gdn_a_log, gdn_dt_bias=gdn_dt_bias, gdn_norm_g=gdn_norm_g, pool_w=pool_w,
             pool_scale=pool_scale, w_branch=w_branch, w_out=w_out, ff2_norm=ff2_norm, ff2_wg=ff2_wg, ff2_wu=ff2_wu,
             ff2_wd=ff2_wd, final_norm=final_norm)
    m = dict(ff1_norm=m_ff1_norm, ff1_wg=m_ff1_wg, ff1_wu=m_ff1_wu, ff1_wd=m_ff1_wd, mix_norm=m_mix_norm, w_in=m_w_in,
             sgu_ln_g=m_sgu_ln_g, sgu_ln_b=m_sgu_ln_b, sgu_w=m_sgu_w, sgu_b=m_sgu_b, lru_conv_w=m_lru_conv_w,
             lru_conv_b=m_lru_conv_b, lru_wa=m_lru_wa, lru_ba=m_lru_ba, lru_wx=m_lru_wx, lru_bx=m_lru_bx,
             lru_lambda=m_lru_lambda, gdn_conv_w=m_gdn_conv_w, gdn_a_log=m_gdn_a_log, gdn_dt_bias=m_gdn_dt_bias,
             gdn_norm_g=m_gdn_norm_g, pool_w=m_pool_w, pool_scale=m_pool_scale, w_branch=m_w_branch, w_out=m_w_out,
             ff2_norm=m_ff2_norm, ff2_wg=m_ff2_wg, ff2_wu=m_ff2_wu, ff2_wd=m_ff2_wd, final_norm=m_final_norm)
    v = dict(ff1_norm=v_ff1_norm, ff1_wg=v_ff1_wg, ff1_wu=v_ff1_wu, ff1_wd=v_ff1_wd, mix_norm=v_mix_norm, w_in=v_w_in,
             sgu_ln_g=v_sgu_ln_g, sgu_ln_b=v_sgu_ln_b, sgu_w=v_sgu_w, sgu_b=v_sgu_b, lru_conv_w=v_lru_conv_w,
             lru_conv_b=v_lru_conv_b, lru_wa=v_lru_wa, lru_ba=v_lru_ba, lru_wx=v_lru_wx, lru_bx=v_lru_bx,
             lru_lambda=v_lru_lambda, gdn_conv_w=v_gdn_conv_w, gdn_a_log=v_gdn_a_log, gdn_dt_bias=v_gdn_dt_bias,
             gdn_norm_g=v_gdn_norm_g, pool_w=v_pool_w, pool_scale=v_pool_scale, w_branch=v_w_branch, w_out=v_w_out,
             ff2_norm=v_ff2_norm, ff2_wg=v_ff2_wg, ff2_wu=v_ff2_wu, ff2_wd=v_ff2_wd, final_norm=v_final_norm)

    first, got_first = _gather_first(w)
    handles, token = _gather_start(w, got_first)
    conv = _gather_conv_finish(w, handles, token)
    pending = {}

    def get_weights(l, sub, after):
        return _gather_finish(l, sub, handles, first, after)

    def put_grads(l, sub, grads):
        hs, meta, tok = _scatter_start(l, sub, grads)
        pending[(l, sub, meta[0])] = (hs, meta)
        return tok

    def put_small(tag, grads):
        hs, shapes, tok = _small_start(tag, grads)
        pending[tag] = (hs, shapes)
        return tok

    T = x.shape[1]
    dx = _forward_backward(x.reshape(T, D), loss_target.reshape(T, D), w, conv, get_weights, put_grads, put_small,
                           token)
    per = {}
    for key in pending:
        if isinstance(key, tuple):
            per.setdefault(key[:2], {}).update(_scatter_finish(*key[:2], *pending[key], dx))
        else:
            per[key] = _small_finish(key, *pending[key], dx)
    grad = {n: jnp.stack([per[(0, sub)][n], per[(1, sub)][n]]) for sub, (_, names) in enumerate(_GROUPS) for n in names}
    layer0 = {**per["0a"], **per["0b"]}
    small = {n: _join([layer0[n].reshape(-1), per["1"][n].reshape(-1)]).reshape((2,) + layer0[n].shape)
             for n in layer0}
    small["final_norm"] = per["1"]["final_norm"]
    loss = per["1"]["loss"][0]
    me = _my_index()
    for n in _SMALL:
        if n in _CONV_SHARDED:
            width = w[n].shape[-1]
            grad[n] = lax.dynamic_slice_in_dim(small[n], me * width, width, axis=2)
        else:
            grad[n] = small[n]

    delta, new_m, new_v = {}, {}, {}
    for n in _BIG:
        d_, m_, v_ = _adamw("adamw_" + n, _as2d(w[n]), _as2d(grad[n]), _as2d(m[n]), _as2d(v[n]))
        delta[n], new_m[n], new_v[n] = (t.reshape(w[n].shape) for t in (d_, m_, v_))

    outs = _adamw_many("adamw_small", *[[_as2d(t[n]) for n in _SMALL] for t in (w, grad, m, v)])
    for k, dst in enumerate((delta, new_m, new_v)):
        for i, n in enumerate(_SMALL):
            dst[n] = outs[k * len(_SMALL) + i].reshape(w[n].shape)

    return (loss, dx.reshape(x.shape), *[grad[n] for n in _WEIGHTS], *[delta[n] for n in _WEIGHTS],
            *[new_m[n] for n in _WEIGHTS], *[new_v[n] for n in _WEIGHTS])
```

```python
import functools

import jax
import jax.numpy as jnp
from jax import lax
from jax.experimental import pallas as pl
from jax.experimental.pallas import tpu as pltpu

F32 = jnp.float32
BF16 = jnp.bfloat16
HI = lax.Precision.HIGHEST

N_DEV = 8
D = 1024
FF = 2816
BW = 512
NBR = 4
CHUNK = 64
EPS = 1e-6
LRU_C = 8.0
GDN_DK = 128

COL_AU, COL_AV, COL_BX, COL_BG = 0, 512, 1024, 1536
COL_CQ, COL_CK, COL_CV, COL_CZ = 2048, 2560, 3072, 3584
COL_DX, COL_GATE, COL_TAIL = 4096, 4608, 8704
PW = 9216
P_IN = 8712

ADAM_LR, ADAM_B1, ADAM_B2, ADAM_EPS, ADAM_WD, ADAM_STEP = 0.001, 0.9, 0.999, 1e-08, 0.01, 10

VMEM_LIMIT_V7X = 56 * 1024 * 1024

_NN = (((1,), (0,)), ((), ()))
_NT = (((1,), (1,)), ((), ()))
_TN = (((0,), (0,)), ((), ()))


def _cp(*sem):
    return pltpu.CompilerParams(dimension_semantics=tuple(sem), vmem_limit_bytes=VMEM_LIMIT_V7X)


def _dot(a, b, dims=_NN):
    return lax.dot_general(a.astype(BF16), b.astype(BF16), dims, preferred_element_type=F32)


def _dot_hi(a, b, dims=_NN):
    return lax.dot_general(a, b, dims, precision=HI, preferred_element_type=F32)


def _pick(n, cands):
    for c in cands:
        if n % c == 0:
            return c
    return n


@jax.custom_jvp
def _log1p(x):
    u = 1.0 + x
    return jnp.where(u == 1.0, x, x * jnp.log(u) / jnp.where(u == 1.0, 1.0, u - 1.0))


@_log1p.defjvp
def _log1p_jvp(p, t):
    (x,), (dx,) = p, t
    return _log1p(x), dx / (1.0 + x)


@jax.custom_jvp
def _expm1(x):
    u = jnp.exp(x)
    lu = jnp.log(u)
    small = (u == 1.0) | (lu == 0.0)
    return jnp.where(small, x, (u - 1.0) * x / jnp.where(small, 1.0, lu))


@_expm1.defjvp
def _expm1_jvp(p, t):
    (x,), (dx,) = p, t
    return _expm1(x), dx * jnp.exp(x)


def _softplus(x):
    return jnp.maximum(x, 0.0) + _log1p(jnp.exp(-jnp.abs(x)))


def _sigmoid(x):
    return jax.nn.sigmoid(x)


def _silu(x):
    return x * jax.nn.sigmoid(x)


def _gelu(x):
    return jax.nn.gelu(x)


@functools.partial(jax.custom_vjp, nondiff_argnums=(1,))
def _shift(x, s):
    return x if s == 0 else pltpu.roll(x, s, 0)


def _shift_fwd(x, s):
    return _shift(x, s), None


def _shift_bwd(s, _, g):
    n = g.shape[0]
    return (g if s == 0 else pltpu.roll(g, n - s, 0),)


_shift.defvjp(_shift_fwd, _shift_bwd)


def _scan_steps(a, b, reverse):
    n = a.shape[0]
    row = lax.broadcasted_iota(jnp.int32, a.shape, 0)
    k = 1
    while k < n:
        sh = n - k if reverse else k
        m = (row < n - k) if reverse else (row >= k)
        a_s = jnp.where(m, pltpu.roll(a, sh, 0), 1.0)
        b_s = jnp.where(m, pltpu.roll(b, sh, 0), 0.0)
        b = a * b_s + b
        a = a * a_s
        k *= 2
    return b


@jax.custom_vjp
def _scan(a, b):
    return _scan_steps(a, b, False)


def _scan_fwd(a, b):
    h = _scan_steps(a, b, False)
    return h, (a, h)


def _scan_bwd(res, dh):
    a, h = res
    n = a.shape[0]
    row = lax.broadcasted_iota(jnp.int32, a.shape, 0)
    a_next = jnp.where(row < n - 1, pltpu.roll(a, n - 1, 0), 0.0)
    g = _scan_steps(a_next, dh, True)
    h_prev = jnp.where(row >= 1, pltpu.roll(h, 1, 0), 0.0)
    return g * h_prev, g


_scan.defvjp(_scan_fwd, _scan_bwd)


def _mm(name, pairs, mode, out_dtype, *, res=None, scale=1.0, bm=None, bn=None, bk=None, after=None):
    a0, b0 = pairs[0]
    if mode == "nn":
        (M, K), N = a0.shape, b0.shape[1]
    elif mode == "nt":
        (M, K), N = a0.shape, b0.shape[0]
    else:
        (K, M), N = a0.shape, b0.shape[1]
    bm = bm or _pick(M, (1024, 512, 256, 128))
    bn = bn or _pick(N, (1024, 512, 256, 128))
    bk = bk or _pick(K, (1024, 512, 1408, 256, 128))
    nk = K // bk
    npair = len(pairs)
    dims = {"nn": _NN, "nt": _NT, "tn": _TN}[mode]

    def body(*refs):
        ab = refs[:2 * npair]
        pos = 2 * npair
        r_ref = None
        if res is not None:
            r_ref = refs[pos]
            pos += 1
        pos += after is not None
        o_ref = refs[pos]
        part = None
        for p in range(npair):
            d = _dot(ab[2 * p][...], ab[2 * p + 1][...], dims)
            part = d if part is None else part + d

        def finish(acc):
            out = acc if scale == 1.0 else acc * scale
            if r_ref is not None:
                out = out + r_ref[...]
            o_ref[...] = out.astype(out_dtype)

        if nk == 1:
            finish(part)
        else:
            acc_ref = refs[pos + 1]
            k = pl.program_id(2)

            @pl.when(k == 0)
            def _():
                acc_ref[...] = part

            @pl.when(k > 0)
            def _():
                acc_ref[...] += part

            @pl.when(k == nk - 1)
            def _():
                finish(acc_ref[...])

    if mode == "nn":
        a_spec = pl.BlockSpec((bm, bk), lambda i, j, k: (i, k))
        b_spec = pl.BlockSpec((bk, bn), lambda i, j, k: (k, j))
    elif mode == "nt":
        a_spec = pl.BlockSpec((bm, bk), lambda i, j, k: (i, k))
        b_spec = pl.BlockSpec((bn, bk), lambda i, j, k: (j, k))
    else:
        a_spec = pl.BlockSpec((bk, bm), lambda i, j, k: (k, i))
        b_spec = pl.BlockSpec((bk, bn), lambda i, j, k: (k, j))
    o_spec = pl.BlockSpec((bm, bn), lambda i, j, k: (i, j))
    in_specs, args = [], []
    for a, b in pairs:
        in_specs += [a_spec, b_spec]
        args += [a, b]
    if res is not None:
        in_specs.append(o_spec)
        args.append(res)
    if after is not None:
        in_specs.append(_ANY)
        args.append(after)
    return pl.pallas_call(
        body, name=name, grid=(M // bm, N // bn, nk),
        in_specs=in_specs, out_specs=o_spec,
        out_shape=jax.ShapeDtypeStruct((M, N), out_dtype),
        scratch_shapes=[pltpu.VMEM((bm, bn), F32)] if nk > 1 else [],
        compiler_params=_cp("parallel", "parallel", "arbitrary"),
    )(*args)


def _rms_fwd(name, x, g):
    T = x.shape[0]
    bm = _pick(T, (512, 256, 128))

    def body(x_ref, g_ref, o_ref):
        xv = x_ref[...]
        r = lax.rsqrt(jnp.mean(xv * xv, axis=-1, keepdims=True) + EPS)
        o_ref[...] = (xv * r * g_ref[...]).astype(BF16)

    return pl.pallas_call(
        body, name=name, grid=(T // bm,),
        in_specs=[pl.BlockSpec((bm, D), lambda i: (i, 0)), pl.BlockSpec((1, D), lambda i: (0, 0))],
        out_specs=pl.BlockSpec((bm, D), lambda i: (i, 0)),
        out_shape=jax.ShapeDtypeStruct((T, D), BF16),
        compiler_params=_cp("parallel"),
    )(x, g)


def _rms_bwd(name, x, g, dh, dres):
    T = x.shape[0]
    bm = _pick(T, (512, 256, 128))

    def body(x_ref, g_ref, dh_ref, dres_ref, dx_ref, dg_ref):
        xv = x_ref[...]
        r = lax.rsqrt(jnp.mean(xv * xv, axis=-1, keepdims=True) + EPS)
        xh = xv * r
        dhv = dh_ref[...]
        dxh = dhv * g_ref[...]
        dx_ref[...] = dres_ref[...] + r * (dxh - xh * jnp.mean(dxh * xh, axis=-1, keepdims=True))
        part = jnp.sum(dhv * xh, axis=0, keepdims=True)

        @pl.when(pl.program_id(0) == 0)
        def _():
            dg_ref[...] = part

        @pl.when(pl.program_id(0) > 0)
        def _():
            dg_ref[...] += part

    row = pl.BlockSpec((bm, D), lambda i: (i, 0))
    vec = pl.BlockSpec((1, D), lambda i: (0, 0))
    return pl.pallas_call(
        body, name=name, grid=(T // bm,),
        in_specs=[row, vec, row, row], out_specs=[row, vec],
        out_shape=[jax.ShapeDtypeStruct((T, D), F32), jax.ShapeDtypeStruct((1, D), F32)],
        compiler_params=_cp("arbitrary"),
    )(x, g, dh, dres)


def _final_loss(name, x, g, tgt):
    T = x.shape[0]
    bm = _pick(T, (512, 256, 128))

    def body(x_ref, g_ref, t_ref, loss_ref, dx_ref, dg_ref):
        xv = x_ref[...]
        gv = g_ref[...]
        r = lax.rsqrt(jnp.mean(xv * xv, axis=-1, keepdims=True) + EPS)
        xh = xv * r
        e = xh * gv - t_ref[...]
        lpart = jnp.broadcast_to(0.5 * jnp.sum(jnp.mean(e * e, axis=-1, keepdims=True), axis=0, keepdims=True), (1, 128))
        dy = e * (1.0 / D)
        dxh = dy * gv
        dx_ref[...] = r * (dxh - xh * jnp.mean(dxh * xh, axis=-1, keepdims=True))
        gpart = jnp.sum(dy * xh, axis=0, keepdims=True)

        @pl.when(pl.program_id(0) == 0)
        def _():
            loss_ref[...] = lpart
            dg_ref[...] = gpart

        @pl.when(pl.program_id(0) > 0)
        def _():
            loss_ref[...] += lpart
            dg_ref[...] += gpart

    row = pl.BlockSpec((bm, D), lambda i: (i, 0))
    vec = pl.BlockSpec((1, D), lambda i: (0, 0))
    return pl.pallas_call(
        body, name=name, grid=(T // bm,),
        in_specs=[row, vec, row],
        out_specs=[pl.BlockSpec((1, 128), lambda i: (0, 0)), row, vec],
        out_shape=[jax.ShapeDtypeStruct((1, 128), F32), jax.ShapeDtypeStruct((T, D), F32),
                   jax.ShapeDtypeStruct((1, D), F32)],
        compiler_params=_cp("arbitrary"),
    )(x, g, tgt)


def _ffn_up(name, h, wg, wu):
    T = h.shape[0]
    bm = _pick(T, (2048, 1024, 512, 256, 128))
    bn = 256

    def body(h_ref, wg_ref, wu_ref, a_ref, b_ref, act_ref):
        hv = h_ref[...]
        a = _dot(hv, wg_ref[...], _NT)
        b = _dot(hv, wu_ref[...], _NT)
        a_ref[...] = a.astype(BF16)
        b_ref[...] = b.astype(BF16)
        act_ref[...] = (_silu(a) * b).astype(BF16)

    w_spec = pl.BlockSpec((bn, D), lambda i, j: (j, 0))
    o_spec = pl.BlockSpec((bm, bn), lambda i, j: (i, j))
    return pl.pallas_call(
        body, name=name, grid=(T // bm, FF // bn),
        in_specs=[pl.BlockSpec((bm, D), lambda i, j: (i, 0)), w_spec, w_spec],
        out_specs=[o_spec, o_spec, o_spec],
        out_shape=[jax.ShapeDtypeStruct((T, FF), BF16)] * 3,
        compiler_params=_cp("parallel", "parallel"),
    )(h, wg, wu)


def _ffn_dact(name, dy, wd, a, b, after=None):
    T = dy.shape[0]
    bm = _pick(T, (2048, 1024, 512, 256, 128))
    bn = 256

    def body(dy_ref, wd_ref, a_ref, b_ref, *rest):
        da_ref, db_ref, dy_bf = rest[-3:]

        @pl.when(pl.program_id(1) == 0)
        def _():
            dy_bf[...] = dy_ref[...].astype(BF16)

        dact = 0.5 * _dot(dy_bf[...], wd_ref[...], _NT)
        av = a_ref[...].astype(F32)
        s = _sigmoid(av)
        da_ref[...] = (dact * b_ref[...].astype(F32) * (s * (1.0 + av * (1.0 - s)))).astype(BF16)
        db_ref[...] = (dact * (av * s)).astype(BF16)

    t_spec = pl.BlockSpec((bm, bn), lambda i, j: (i, j))
    return pl.pallas_call(
        body, name=name, grid=(T // bm, FF // bn),
        in_specs=[pl.BlockSpec((bm, D), lambda i, j: (i, 0)), pl.BlockSpec((bn, D), lambda i, j: (j, 0)),
                  t_spec, t_spec] + [_ANY] * (after is not None),
        out_specs=[t_spec, t_spec],
        out_shape=[jax.ShapeDtypeStruct((T, FF), BF16), jax.ShapeDtypeStruct((T, FF), BF16)],
        scratch_shapes=[pltpu.VMEM((bm, D), BF16)],
        compiler_params=_cp("parallel", "arbitrary"),
    )(dy, wd, a, b, *([after] if after is not None else []))


def _merge_specs(T, bm, bn):
    y_spec = pl.BlockSpec((bm, BW), lambda i, j: (i, 0))
    wb_spec = pl.BlockSpec((NBR, bn, BW), lambda i, j: (0, j, 0))
    gate_specs = [pl.BlockSpec((bm, bn), functools.partial(lambda i, j, o: (i, o + j), o=(COL_GATE + g * D) // bn))
                  for g in range(NBR)]
    t_spec = pl.BlockSpec((bm, bn), lambda i, j: (i, j))
    return y_spec, wb_spec, gate_specs, t_spec


def _merge_fwd(name, ys, wb, proj):
    T = proj.shape[0]
    bm = _pick(T, (512, 256, 128))
    bn = 512
    y_spec, wb_spec, gate_specs, t_spec = _merge_specs(T, bm, bn)

    def body(y0, y1, y2, y3, wb_ref, g0, g1, g2, g3, o_ref):
        acc = None
        for g, (y_ref, g_ref) in enumerate(((y0, g0), (y1, g1), (y2, g2), (y3, g3))):
            t = _sigmoid(g_ref[...]) * _dot(y_ref[...], wb_ref[g], _NT)
            acc = t if acc is None else acc + t
        o_ref[...] = acc.astype(BF16)

    return pl.pallas_call(
        body, name=name, grid=(T // bm, D // bn),
        in_specs=[y_spec] * NBR + [wb_spec] + gate_specs, out_specs=t_spec,
        out_shape=jax.ShapeDtypeStruct((T, D), BF16),
        compiler_params=_cp("parallel", "parallel"),
    )(*ys, wb, proj, proj, proj, proj)


def _merge_bwd(name, dm, ys, wb, proj):
    T = proj.shape[0]
    bm = _pick(T, (512, 256, 128))
    bn = 512
    y_spec, wb_spec, gate_specs, t_spec = _merge_specs(T, bm, bn)

    def body(dm_ref, y0, y1, y2, y3, wb_ref, g0, g1, g2, g3, *outs):
        dmv = dm_ref[...]
        j = pl.program_id(1)
        for g, (y_ref, g_ref) in enumerate(((y0, g0), (y1, g1), (y2, g2), (y3, g3))):
            br = _dot(y_ref[...], wb_ref[g], _NT)
            s = _sigmoid(g_ref[...])
            outs[g][...] = (dmv * br * (s * (1.0 - s))).astype(BF16)
            dbr = (dmv * s).astype(BF16)
            outs[NBR + g][...] = dbr
            part = _dot(dbr, wb_ref[g])
            dy_ref = outs[2 * NBR + g]

            @pl.when(j == 0)
            def _():
                dy_ref[...] = part

            @pl.when(j > 0)
            def _():
                dy_ref[...] += part

    return pl.pallas_call(
        body, name=name, grid=(T // bm, D // bn),
        in_specs=[t_spec] + [y_spec] * NBR + [wb_spec] + gate_specs, out_specs=[t_spec] * (2 * NBR) + [y_spec] * NBR,
        out_shape=[jax.ShapeDtypeStruct((T, D), BF16)] * (2 * NBR) + [jax.ShapeDtypeStruct((T, BW), F32)] * NBR,
        compiler_params=_cp("parallel", "arbitrary"),
    )(dm, *ys, wb, proj, proj, proj, proj)


def _sgu_block(u_pre, v_pre, ln_g, ln_b, w, bias):
    u = _gelu(u_pre)
    vf = _gelu(v_pre)
    mu = jnp.mean(vf, axis=-1, keepdims=True)
    var = jnp.mean(jnp.square(vf - mu), axis=-1, keepdims=True)
    vn = (vf - mu) * lax.rsqrt(var + EPS) * ln_g + ln_b
    ri = lax.broadcasted_iota(jnp.int32, (128, 128), 0)
    ci = lax.broadcasted_iota(jnp.int32, (128, 128), 1)
    mask = (ri // CHUNK) >= (ci // CHUNK)
    outs = [_dot(jnp.where(mask, w[g], 0.0), vn[:, g * 128:(g + 1) * 128]) for g in range(4)]
    mixed = jnp.concatenate(outs, axis=1) + bias
    return u * mixed


def _sgu_param_specs():
    return [pl.BlockSpec((1, BW), lambda i: (0, 0)), pl.BlockSpec((1, BW), lambda i: (0, 0)),
            pl.BlockSpec((4, 128, 128), lambda i: (0, 0, 0)), pl.BlockSpec((128, BW), lambda i: (0, 0))]


def _sgu_fwd(name, proj, ln_g, ln_b, w, bias):
    T = proj.shape[0]
    rb = _pick(T, (256, 128))

    def body(u_ref, v_ref, g_ref, b_ref, w_ref, bias_ref, y_ref):
        for n in range(rb // 128):
            rows = slice(n * 128, (n + 1) * 128)
            y = _sgu_block(u_ref[rows, :], v_ref[rows, :], g_ref[...], b_ref[...], w_ref[...], bias_ref[...])
            y_ref[rows, :] = y.astype(BF16)

    return pl.pallas_call(
        body, name=name, grid=(T // rb,),
        in_specs=[pl.BlockSpec((rb, BW), lambda i: (i, COL_AU // BW)), pl.BlockSpec((rb, BW), lambda i: (i, COL_AV // BW))]
        + _sgu_param_specs(),
        out_specs=pl.BlockSpec((rb, BW), lambda i: (i, 0)),
        out_shape=jax.ShapeDtypeStruct((T, BW), BF16),
        compiler_params=_cp("parallel"),
    )(proj, proj, ln_g, ln_b, w, bias)


def _sgu_bwd(name, proj, dy, ln_g, ln_b, w, bias):
    T = proj.shape[0]
    rb = _pick(T, (256, 128))

    def body(u_ref, v_ref, dy_ref, g_ref, b_ref, w_ref, bias_ref, du_ref, dv_ref, dg_ref, db_ref, dw_ref, dbias_ref):
        acc = None
        for n in range(rb // 128):
            rows = slice(n * 128, (n + 1) * 128)
            _, vjp = jax.vjp(_sgu_block, u_ref[rows, :], v_ref[rows, :], g_ref[...], b_ref[...], w_ref[...],
                             bias_ref[...])
            du, dv, *dp = vjp(dy_ref[rows, :])
            du_ref[rows, :] = du.astype(BF16)
            dv_ref[rows, :] = dv.astype(BF16)
            acc = dp if acc is None else [p + q for p, q in zip(acc, dp)]

        @pl.when(pl.program_id(0) == 0)
        def _():
            for r, p in zip((dg_ref, db_ref, dw_ref, dbias_ref), acc):
                r[...] = p

        @pl.when(pl.program_id(0) > 0)
        def _():
            for r, p in zip((dg_ref, db_ref, dw_ref, dbias_ref), acc):
                r[...] += p

    row = pl.BlockSpec((rb, BW), lambda i: (i, 0))
    return pl.pallas_call(
        body, name=name, grid=(T // rb,),
        in_specs=[pl.BlockSpec((rb, BW), lambda i: (i, COL_AU // BW)), pl.BlockSpec((rb, BW), lambda i: (i, COL_AV // BW)),
                  row] + _sgu_param_specs(),
        out_specs=[row, row] + _sgu_param_specs(),
        out_shape=[jax.ShapeDtypeStruct((T, BW), BF16), jax.ShapeDtypeStruct((T, BW), BF16),
                   jax.ShapeDtypeStruct((1, BW), F32), jax.ShapeDtypeStruct((1, BW), F32),
                   jax.ShapeDtypeStruct((4, 128, 128), F32), jax.ShapeDtypeStruct((128, BW), F32)],
        compiler_params=_cp("arbitrary"),
    )(proj, proj, dy, ln_g, ln_b, w, bias)


def _halo_block(ref, i, rblk, halo):
    r0 = pl.multiple_of(i * rblk, rblk)
    h0 = pl.multiple_of(jnp.maximum(r0 - halo, 0), halo)
    top = jnp.where(i > 0, ref[pl.ds(h0, halo), :], 0.0)
    return jnp.concatenate([top, ref[pl.ds(r0, rblk), :]], axis=0)


def _with_halo_grad(dfull, pending, halo, rblk):
    tail = jnp.concatenate([jnp.zeros((rblk - halo, 128), F32), pending], axis=0)
    return dfull[halo:] + tail


def _conv4(xfull, rows):
    acc = None
    for k in range(4):
        t = rows[k] * _shift(xfull, 3 - k)[8:]
        acc = t if acc is None else acc + t
    return acc


def _lru_block(xfull, gate, h0, c0, c1, c2, c3, cb, wa, ba, wx, bx, lam):
    n = gate.shape[0]
    xc = _conv4(xfull, (c0, c1, c2, c3)) + cb
    r = _sigmoid(_dot(xc, wa) + ba)
    ig = _sigmoid(_dot(xc, wx) + bx)
    log_a = -LRU_C * r * _softplus(-lam)
    a = jnp.exp(log_a)
    mult = jnp.sqrt(-_expm1(2.0 * log_a))
    b = mult * (ig * xc)
    row = lax.broadcasted_iota(jnp.int32, (n, 128), 0)
    b = b + jnp.where(row == 0, a * h0, 0.0)
    h = _scan(a, b)
    out = h * _gelu(gate)
    h_last = jnp.sum(jnp.where(row == n - 1, h, 0.0), axis=0, keepdims=True)
    return out, h_last


def _lru_param_specs():
    vec = pl.BlockSpec((1, 128), lambda g: (0, g))
    mat = pl.BlockSpec((None, 128, 128), lambda g: (g, 0, 0))
    return [pl.BlockSpec((4, 128), lambda g: (0, g)), vec, mat, vec, mat, vec, vec]


def _lru_load_params(cw_ref, cb_ref, wa_ref, ba_ref, wx_ref, bx_ref, lam_ref):
    return (cw_ref[0:1, :], cw_ref[1:2, :], cw_ref[2:3, :], cw_ref[3:4, :], cb_ref[...], wa_ref[...], ba_ref[...],
            wx_ref[...], bx_ref[...], lam_ref[...])


def _lru_fwd(name, proj, cw, cb, wa, ba, wx, bx, lam):
    T = proj.shape[0]
    rblk = _pick(T, (256, 128))
    nblk = T // rblk

    def body(x_ref, gt_ref, cw_ref, cb_ref, wa_ref, ba_ref, wx_ref, bx_ref, lam_ref, y_ref, hc_ref):
        params = _lru_load_params(cw_ref, cb_ref, wa_ref, ba_ref, wx_ref, bx_ref, lam_ref)

        def step(i, h0):
            r0 = pl.multiple_of(i * rblk, rblk)
            out, h_last = _lru_block(_halo_block(x_ref, i, rblk, 8), gt_ref[pl.ds(r0, rblk), :], h0, *params)
            y_ref[pl.ds(r0, rblk), :] = out.astype(BF16)
            hc_ref[pl.ds(pl.multiple_of(i * 8, 8), 8), :] = jnp.broadcast_to(h0, (8, 128))
            return h_last

        lax.fori_loop(0, nblk, step, jnp.zeros((1, 128), F32))

    return pl.pallas_call(
        body, name=name, grid=(4,),
        in_specs=[pl.BlockSpec((T, 128), lambda g: (0, COL_BX // 128 + g)),
                  pl.BlockSpec((T, 128), lambda g: (0, COL_BG // 128 + g))] + _lru_param_specs(),
        out_specs=[pl.BlockSpec((T, 128), lambda g: (0, g)), pl.BlockSpec((nblk * 8, 128), lambda g: (0, g))],
        out_shape=[jax.ShapeDtypeStruct((T, BW), BF16), jax.ShapeDtypeStruct((nblk * 8, BW), F32)],
        compiler_params=_cp("parallel"),
    )(proj, proj, cw, cb, wa, ba, wx, bx, lam)


def _lru_bwd(name, proj, dy, hc, cw, cb, wa, ba, wx, bx, lam):
    T = proj.shape[0]
    rblk = _pick(T, (256, 128))
    nblk = T // rblk

    def body(x_ref, gt_ref, dy_ref, hc_ref, cw_ref, cb_ref, wa_ref, ba_ref, wx_ref, bx_ref, lam_ref,
             dx_ref, dgt_ref, dcw_ref, dcb_ref, dwa_ref, dba_ref, dwx_ref, dbx_ref, dlam_ref):
        params = _lru_load_params(cw_ref, cb_ref, wa_ref, ba_ref, wx_ref, bx_ref, lam_ref)

        def step(it, carry):
            dh_last, pending, acc = carry
            i = nblk - 1 - it
            r0 = pl.multiple_of(i * rblk, rblk)
            h0 = hc_ref[pl.ds(pl.multiple_of(i * 8, 8), 1), :]
            _, vjp = jax.vjp(_lru_block, _halo_block(x_ref, i, rblk, 8), gt_ref[pl.ds(r0, rblk), :], h0, *params)
            dfull, dgate, dh0, *dp = vjp((dy_ref[pl.ds(r0, rblk), :], dh_last))
            dx_ref[pl.ds(r0, rblk), :] = _with_halo_grad(dfull, pending, 8, rblk).astype(BF16)
            dgt_ref[pl.ds(r0, rblk), :] = dgate.astype(BF16)
            return dh0, dfull[:8], tuple(p + q for p, q in zip(acc, dp))

        zeros = tuple(jnp.zeros(p.shape, F32) for p in params)
        _, _, acc = lax.fori_loop(0, nblk, step, (jnp.zeros((1, 128), F32), jnp.zeros((8, 128), F32), zeros))
        for k in range(4):
            dcw_ref[k:k + 1, :] = acc[k]
        for r, p in zip((dcb_ref, dwa_ref, dba_ref, dwx_ref, dbx_ref, dlam_ref), acc[4:]):
            r[...] = p

    col = pl.BlockSpec((T, 128), lambda g: (0, g))
    return pl.pallas_call(
        body, name=name, grid=(4,),
        in_specs=[pl.BlockSpec((T, 128), lambda g: (0, COL_BX // 128 + g)),
                  pl.BlockSpec((T, 128), lambda g: (0, COL_BG // 128 + g)), col,
                  pl.BlockSpec((nblk * 8, 128), lambda g: (0, g))] + _lru_param_specs(),
        out_specs=[col, col] + _lru_param_specs(),
        out_shape=[jax.ShapeDtypeStruct((T, BW), BF16), jax.ShapeDtypeStruct((T, BW), BF16),
                   jax.ShapeDtypeStruct((4, BW), F32), jax.ShapeDtypeStruct((1, BW), F32),
                   jax.ShapeDtypeStruct((4, 128, 128), F32), jax.ShapeDtypeStruct((1, BW), F32),
                   jax.ShapeDtypeStruct((4, 128, 128), F32), jax.ShapeDtypeStruct((1, BW), F32),
                   jax.ShapeDtypeStruct((1, BW), F32)],
        compiler_params=_cp("parallel"),
    )(proj, proj, dy, hc, cw, cb, wa, ba, wx, bx, lam)


def _conv_block(xfull, c0, c1, c2, c3):
    return _silu(_conv4(xfull, (c0, c1, c2, c3)))


def _conv_fwd(name, proj, col0, cw, cw_col0):
    T = proj.shape[0]
    rblk = _pick(T, (256, 128))
    nblk = T // rblk

    def body(x_ref, cw_ref, y_ref):
        rows = (cw_ref[0:1, :], cw_ref[1:2, :], cw_ref[2:3, :], cw_ref[3:4, :])

        def step(i, c):
            r0 = pl.multiple_of(i * rblk, rblk)
            y_ref[pl.ds(r0, rblk), :] = _conv_block(_halo_block(x_ref, i, rblk, 8), *rows)
            return c

        lax.fori_loop(0, nblk, step, 0)

    return pl.pallas_call(
        body, name=name, grid=(4,),
        in_specs=[pl.BlockSpec((T, 128), lambda g: (0, col0 // 128 + g)),
                  pl.BlockSpec((4, 128), lambda g: (0, cw_col0 // 128 + g))],
        out_specs=pl.BlockSpec((T, 128), lambda g: (0, g)),
        out_shape=jax.ShapeDtypeStruct((T, BW), F32),
        compiler_params=_cp("parallel"),
    )(proj, cw)


def _conv_bwd(name, proj, col0, dy, cw, cw_col0):
    T = proj.shape[0]
    rblk = _pick(T, (256, 128))
    nblk = T // rblk

    def body(x_ref, dy_ref, cw_ref, dx_ref, dcw_ref):
        rows = (cw_ref[0:1, :], cw_ref[1:2, :], cw_ref[2:3, :], cw_ref[3:4, :])

        def step(it, carry):
            pending, acc = carry
            i = nblk - 1 - it
            r0 = pl.multiple_of(i * rblk, rblk)
            _, vjp = jax.vjp(_conv_block, _halo_block(x_ref, i, rblk, 8), *rows)
            dfull, *dp = vjp(dy_ref[pl.ds(r0, rblk), :])
            dx_ref[pl.ds(r0, rblk), :] = _with_halo_grad(dfull, pending, 8, rblk).astype(BF16)
            return dfull[:8], tuple(p + q for p, q in zip(acc, dp))

        zeros = tuple(jnp.zeros((1, 128), F32) for _ in range(4))
        _, acc = lax.fori_loop(0, nblk, step, (jnp.zeros((8, 128), F32), zeros))
        for k in range(4):
            dcw_ref[k:k + 1, :] = acc[k]

    col = pl.BlockSpec((T, 128), lambda g: (0, g))
    return pl.pallas_call(
        body, name=name, grid=(4,),
        in_specs=[pl.BlockSpec((T, 128), lambda g: (0, col0 // 128 + g)), col,
                  pl.BlockSpec((4, 128), lambda g: (0, cw_col0 // 128 + g))],
        out_specs=[col, pl.BlockSpec((4, 128), lambda g: (0, g))],
        out_shape=[jax.ShapeDtypeStruct((T, BW), BF16), jax.ShapeDtypeStruct((4, BW), F32)],
        compiler_params=_cp("parallel"),
    )(proj, dy, cw)


def _pool_block(xfull, pw, sc, t0, gi):
    n = xfull.shape[0] - 16
    s2 = xfull + _shift(xfull, 1)
    s4 = s2 + _shift(s2, 2)
    s8 = s4 + _shift(s4, 4)
    s16 = s8 + _shift(s8, 8)
    s = jnp.where(gi == 0, s2, jnp.where(gi == 1, s4, jnp.where(gi == 2, s8, s16)))[16:]
    t = t0 + lax.broadcasted_iota(jnp.int32, (n, 128), 0)
    cnt = jnp.minimum(t + 1, lax.shift_left(jnp.int32(2), gi)).astype(F32)
    pooled = s / cnt - xfull[16:]
    return _dot(pooled, pw) * sc


def _pool_fwd(name, proj, pw, sc):
    T = proj.shape[0]
    rblk = _pick(T, (256, 128))
    nblk = T // rblk

    def body(x_ref, pw_ref, sc_ref, y_ref):
        gi = pl.program_id(0)

        def step(i, c):
            r0 = pl.multiple_of(i * rblk, rblk)
            y = _pool_block(_halo_block(x_ref, i, rblk, 16), pw_ref[...], sc_ref[...], r0, gi)
            y_ref[pl.ds(r0, rblk), :] = y.astype(BF16)
            return c

        lax.fori_loop(0, nblk, step, 0)

    return pl.pallas_call(
        body, name=name, grid=(4,),
        in_specs=[pl.BlockSpec((T, 128), lambda g: (0, COL_DX // 128 + g)),
                  pl.BlockSpec((None, 128, 128), lambda g: (g, 0, 0)), pl.BlockSpec((1, 128), lambda g: (0, g))],
        out_specs=pl.BlockSpec((T, 128), lambda g: (0, g)),
        out_shape=jax.ShapeDtypeStruct((T, BW), BF16),
        compiler_params=_cp("parallel"),
    )(proj, pw, sc)


def _pool_bwd(name, proj, dy, pw, sc):
    T = proj.shape[0]
    rblk = _pick(T, (256, 128))
    nblk = T // rblk

    def body(x_ref, dy_ref, pw_ref, sc_ref, dx_ref, dpw_ref, dsc_ref):
        gi = pl.program_id(0)

        def step(it, carry):
            pending, apw, asc = carry
            i = nblk - 1 - it
            r0 = pl.multiple_of(i * rblk, rblk)
            _, vjp = jax.vjp(lambda xf, w, s: _pool_block(xf, w, s, r0, gi), _halo_block(x_ref, i, rblk, 16),
                             pw_ref[...], sc_ref[...])
            dfull, dw, ds = vjp(dy_ref[pl.ds(r0, rblk), :])
            dx_ref[pl.ds(r0, rblk), :] = _with_halo_grad(dfull, pending, 16, rblk).astype(BF16)
            return dfull[:16], apw + dw, asc + ds

        _, apw, asc = lax.fori_loop(0, nblk, step, (jnp.zeros((16, 128), F32), jnp.zeros((128, 128), F32),
                                                    jnp.zeros((1, 128), F32)))
        dpw_ref[...] = apw
        dsc_ref[...] = asc

    col = pl.BlockSpec((T, 128), lambda g: (0, g))
    mat = pl.BlockSpec((None, 128, 128), lambda g: (g, 0, 0))
    vec = pl.BlockSpec((1, 128), lambda g: (0, g))
    return pl.pallas_call(
        body, name=name, grid=(4,),
        in_specs=[pl.BlockSpec((T, 128), lambda g: (0, COL_DX // 128 + g)), col, mat, vec],
        out_specs=[col, mat, vec],
        out_shape=[jax.ShapeDtypeStruct((T, BW), BF16), jax.ShapeDtypeStruct((4, 128, 128), F32),
                   jax.ShapeDtypeStruct((1, BW), F32)],
        compiler_params=_cp("parallel"),
    )(proj, dy, pw, sc)


@jax.custom_vjp
def _dot3(a, b):
    ah = a.astype(BF16)
    al = (a - ah.astype(F32)).astype(BF16)
    bh = b.astype(BF16)
    bl = (b - bh.astype(F32)).astype(BF16)

    def d(x, y):
        return lax.dot_general(x, y, _NN, preferred_element_type=F32)

    return d(ah, bh) + (d(ah, bl) + d(al, bh))


def _dot3_fwd(a, b):
    return _dot3(a, b), (a, b)


def _dot3_bwd(res, g):
    a, b = res
    return _dot(g, b, _NT), _dot(a, g, _TN)


_dot3.defvjp(_dot3_fwd, _dot3_bwd)


def _pad_rows2(x):
    return jnp.concatenate([x, jnp.zeros_like(x)], axis=0)


@jax.custom_vjp
def _tri_inv(mats):
    n = mats[0].shape[0]
    eye = (lax.broadcasted_iota(jnp.int32, (n, n), 0) == lax.broadcasted_iota(jnp.int32, (n, n), 1)).astype(F32)
    ps = [eye - a for a in mats]
    ms = list(mats)
    k = 2
    while k < n:
        ms = [_dot3(t, t) for t in ms]
        ps = [p + _dot3(p, t) for p, t in zip(ps, ms)]
        k *= 2
    return ps


def _tri_inv_fwd(mats):
    ts = _tri_inv(mats)
    return ts, ts


def _tri_inv_bwd(ts, gs):
    half = [_dot(t, g, _TN) for t, g in zip(ts, gs)]
    return ([-_dot(h, t, _NT) for h, t in zip(half, ts)],)


_tri_inv.defvjp(_tri_inv_fwd, _tri_inv_bwd)


def _cumsum_rows(x):
    n = x.shape[0]
    row = lax.broadcasted_iota(jnp.int32, x.shape, 0)
    k = 1
    while k < n:
        x = x + jnp.where(row >= k, _shift(x, k), 0.0)
        k *= 2
    return x


def _gdn_prep(qcs, kcs, vcs, tails, alog, dtb):
    C = CHUNK
    pairs = [(c, h) for c in range(len(qcs)) for h in range(4)]
    lane = lax.broadcasted_iota(jnp.int32, (C, 128), 1)
    row = lax.broadcasted_iota(jnp.int32, (C, 128), 0)
    incl = row >= lane
    sig = [_sigmoid(t) for t in tails]
    gfull = [-jnp.exp(alog) * _softplus(t + dtb) for t in tails]
    beta = [jnp.sum(jnp.where(lane == h, sig[c], 0.0), axis=1, keepdims=True) for c, h in pairs]
    g = [jnp.sum(jnp.where(lane == h + 4, gfull[c], 0.0), axis=1, keepdims=True) for c, h in pairs]
    qs = [qcs[c][:, h * 128:(h + 1) * 128] for c, h in pairs]
    ks = [kcs[c][:, h * 128:(h + 1) * 128] for c, h in pairs]
    vs = [vcs[c][:, h * 128:(h + 1) * 128] for c, h in pairs]
    q = [t * lax.rsqrt(jnp.sum(t * t, axis=-1, keepdims=True) + EPS) * (GDN_DK ** -0.5) for t in qs]
    k = [t * lax.rsqrt(jnp.sum(t * t, axis=-1, keepdims=True) + EPS) for t in ks]
    gc = [_cumsum_rows(jnp.broadcast_to(t, (C, 128))) for t in g]
    gc_t = [jnp.transpose(jnp.concatenate([t, t], axis=0)) for t in gc]
    gc_col = [jnp.sum(jnp.where(lane == 0, t, 0.0), axis=1, keepdims=True) for t in gc]
    ri = lax.broadcasted_iota(jnp.int32, (C, C), 0)
    ci = lax.broadcasted_iota(jnp.int32, (C, C), 1)
    decay = [jnp.exp(jnp.where(incl, a - b[:C, :], -1e30)) for a, b in zip(gc, gc_t)]
    decay_sq = [jnp.exp(jnp.where(ri > ci, a - jnp.transpose(b)[:C, :], -1e30)) for a, b in zip(gc_col, gc)]
    kb = [a * b for a, b in zip(k, beta)]
    kk = [_dot(a, b, _NT) for a, b in zip(kb, k)]
    t_mat = _tri_inv([jnp.where(ri > ci, a * b, 0.0) for a, b in zip(kk, decay_sq)])
    egc = [jnp.exp(t) for t in gc]
    u = [_dot(t, a * b) for t, a, b in zip(t_mat, vs, beta)]
    w = [_dot(t, a * b) for t, a, b in zip(t_mat, kb, egc)]
    qk = [_dot(a, _pad_rows2(b), _NT) for a, b in zip(q, k)]
    attn = [jnp.where(incl, a * b, 0.0) for a, b in zip(qk, decay)]
    g_last = [jnp.sum(jnp.where(row == C - 1, t, 0.0), axis=0, keepdims=True) for t in gc]
    qe = [a * b for a, b in zip(q, egc)]
    kd = [a * jnp.exp(b - c_) for a, b, c_ in zip(k, g_last, gc)]
    egl = [jnp.exp(t) for t in g_last]

    def per_chunk(vals):
        return [jnp.concatenate(vals[4 * c:4 * c + 4], axis=1) for c in range(len(qcs))]

    return tuple(per_chunk(t) for t in (u, w, qe, kd, attn, egl))


def _gdn_scan_chunk(states, u, w, qe, kd, attn, egl, z, ng):
    hs = range(4)

    def sl(t, h):
        return t[:, h * 128:(h + 1) * 128]

    ws = [_dot(sl(w, h), states[h]) for h in hs]
    qs = [_dot(sl(qe, h), states[h]) for h in hs]
    v_new = [sl(u, h) - ws[h] for h in hs]
    av = [_dot(sl(attn, h), _pad_rows2(v_new[h])) for h in hs]
    kv = [_dot(sl(kd, h), v_new[h], _TN) for h in hs]
    nxt = tuple(states[h] * sl(egl, h) + kv[h] for h in hs)
    o = [qs[h] + av[h] for h in hs]
    on = [t * lax.rsqrt(jnp.mean(t * t, axis=-1, keepdims=True) + EPS) * ng for t in o]
    return nxt, jnp.concatenate(on, axis=1) * _silu(z)


def _gdn_blocks(T):
    tb = _pick(T, (512, 256, 128, 64))
    return tb, T // tb, tb // CHUNK


PREP_CHUNKS = 4


def _chunk_rows(i, n):
    return [pl.ds(pl.multiple_of((i * n + j) * CHUNK, CHUNK), CHUNK) for j in range(n)]


def _egl_rows(i, n, size):
    return [pl.ds(pl.multiple_of((i * n + j) * 8, 8), size) for j in range(n)]


def _gdn_prep_fwd(name, qa, ka, va, proj, alog, dtb):
    T = proj.shape[0]
    tb, nb, ncb = _gdn_blocks(T)
    n = PREP_CHUNKS if ncb % PREP_CHUNKS == 0 else 1

    def body(q_ref, k_ref, v_ref, tail_ref, alog_ref, dtb_ref, u_ref, w_ref, qe_ref, kd_ref, at_ref, egl_ref):
        def step(i, c):
            rows = _chunk_rows(i, n)
            u, w, qe, kd, at, egl = _gdn_prep([q_ref[r, :] for r in rows], [k_ref[r, :] for r in rows],
                                              [v_ref[r, :] for r in rows], [tail_ref[r, :] for r in rows],
                                              alog_ref[...], dtb_ref[...])
            for j, (r, e) in enumerate(zip(rows, _egl_rows(i, n, 8))):
                u_ref[r, :] = u[j]
                w_ref[r, :] = w[j].astype(BF16)
                qe_ref[r, :] = qe[j].astype(BF16)
                kd_ref[r, :] = kd[j].astype(BF16)
                at_ref[r, :] = at[j].astype(BF16)
                egl_ref[e, :] = jnp.broadcast_to(egl[j], (8, BW))
            return c

        lax.fori_loop(0, ncb // n, step, 0)

    blk = pl.BlockSpec((tb, BW), lambda j: (j, 0))
    vec = pl.BlockSpec((1, 128), lambda j: (0, 0))
    return pl.pallas_call(
        body, name=name, grid=(nb,),
        in_specs=[blk, blk, blk, pl.BlockSpec((tb, 128), lambda j: (j, COL_TAIL // 128)), vec, vec],
        out_specs=[blk] * 5 + [pl.BlockSpec((ncb * 8, BW), lambda j: (j, 0))],
        out_shape=[jax.ShapeDtypeStruct((T, BW), F32)] + [jax.ShapeDtypeStruct((T, BW), BF16)] * 4
        + [jax.ShapeDtypeStruct((T // 8, BW), F32)],
        compiler_params=_cp("parallel"),
    )(qa, ka, va, proj, alog, dtb)


def _gdn_prep_bwd(name, qa, ka, va, proj, alog, dtb, du, dw, dqe, dkd, dat, degl):
    T = proj.shape[0]
    tb, nb, ncb = _gdn_blocks(T)
    n = PREP_CHUNKS if ncb % PREP_CHUNKS == 0 else 1

    def body(q_ref, k_ref, v_ref, tail_ref, alog_ref, dtb_ref, du_ref, dw_ref, dqe_ref, dkd_ref, dat_ref, degl_ref,
             dq_ref, dk_ref, dv_ref, dtail_ref, dalog_ref, ddtb_ref):
        first = pl.program_id(0) == 0

        def step(i, carry):
            pa, pd = carry
            rows = _chunk_rows(i, n)
            _, vjp = jax.vjp(_gdn_prep, [q_ref[r, :] for r in rows], [k_ref[r, :] for r in rows],
                             [v_ref[r, :] for r in rows], [tail_ref[r, :] for r in rows], alog_ref[...], dtb_ref[...])
            cot = tuple([ref[r, :] for r in rows] for ref in (du_ref, dw_ref, dqe_ref, dkd_ref, dat_ref))
            dq, dk, dv, dtail, da, dd = vjp(cot + ([degl_ref[e, :] for e in _egl_rows(i, n, 1)],))
            for j, r in enumerate(rows):
                dq_ref[r, :] = dq[j]
                dk_ref[r, :] = dk[j]
                dv_ref[r, :] = dv[j]
                dtail_ref[r, :] = dtail[j].astype(BF16)
            return pa + da, pd + dd

        zv = jnp.zeros((1, 128), F32)
        pa, pd = lax.fori_loop(0, ncb // n, step, (zv, zv))

        @pl.when(first)
        def _():
            dalog_ref[...] = pa
            ddtb_ref[...] = pd

        @pl.when(jnp.logical_not(first))
        def _():
            dalog_ref[...] += pa
            ddtb_ref[...] += pd

    blk = pl.BlockSpec((tb, BW), lambda j: (j, 0))
    vec = pl.BlockSpec((1, 128), lambda j: (0, 0))
    return pl.pallas_call(
        body, name=name, grid=(nb,),
        in_specs=[blk, blk, blk, pl.BlockSpec((tb, 128), lambda j: (j, COL_TAIL // 128)), vec, vec]
        + [blk] * 5 + [pl.BlockSpec((ncb * 8, BW), lambda j: (j, 0))],
        out_specs=[blk, blk, blk, pl.BlockSpec((tb, 128), lambda j: (j, 0)), vec, vec],
        out_shape=[jax.ShapeDtypeStruct((T, BW), F32)] * 3 + [jax.ShapeDtypeStruct((T, 128), BF16)]
        + [jax.ShapeDtypeStruct((1, 128), F32)] * 2,
        compiler_params=_cp("arbitrary"),
    )(qa, ka, va, proj, alog, dtb, du, dw, dqe, dkd, dat, degl)


def _gdn_fwd(name, u, w, qe, kd, at, egl, proj, ng):
    T = proj.shape[0]
    tb, nb, ncb = _gdn_blocks(T)

    def body(u_ref, w_ref, qe_ref, kd_ref, at_ref, egl_ref, z_ref, ng_ref, y_ref, sh_ref, state):
        @pl.when(pl.program_id(0) == 0)
        def _():
            state[...] = jnp.zeros((4, 128, 128), F32)

        def step(c, states):
            rows = pl.ds(pl.multiple_of(c * CHUNK, CHUNK), CHUNK)
            for h in range(4):
                sh_ref[h, c] = states[h]
            nxt, y = _gdn_scan_chunk(states, u_ref[rows, :], w_ref[rows, :], qe_ref[rows, :], kd_ref[rows, :],
                                     at_ref[rows, :], egl_ref[pl.ds(pl.multiple_of(c * 8, 8), 1), :], z_ref[rows, :],
                                     ng_ref[...])
            y_ref[rows, :] = y.astype(BF16)
            return nxt

        states = lax.fori_loop(0, ncb, step, tuple(state[h] for h in range(4)))
        for h in range(4):
            state[h] = states[h]

    blk = pl.BlockSpec((tb, BW), lambda j: (j, 0))
    vec = pl.BlockSpec((1, 128), lambda j: (0, 0))
    return pl.pallas_call(
        body, name=name, grid=(nb,),
        in_specs=[blk] * 5 + [pl.BlockSpec((ncb * 8, BW), lambda j: (j, 0)),
                              pl.BlockSpec((tb, BW), lambda j: (j, COL_CZ // BW)), vec],
        out_specs=[blk, pl.BlockSpec((4, ncb, 128, 128), lambda j: (0, j, 0, 0))],
        out_shape=[jax.ShapeDtypeStruct((T, BW), BF16), jax.ShapeDtypeStruct((4, T // CHUNK, 128, 128), F32)],
        scratch_shapes=[pltpu.VMEM((4, 128, 128), F32)],
        compiler_params=_cp("arbitrary"),
    )(u, w, qe, kd, at, egl, proj, ng)


def _gdn_bwd(name, u, w, qe, kd, at, egl, proj, dy, sh, ng):
    T = proj.shape[0]
    tb, nb, ncb = _gdn_blocks(T)

    def body(u_ref, w_ref, qe_ref, kd_ref, at_ref, egl_ref, z_ref, dy_ref, sh_ref, ng_ref,
             du_ref, dw_ref, dqe_ref, dkd_ref, dat_ref, degl_ref, dz_ref, dng_ref, dstate):
        first = pl.program_id(0) == 0

        @pl.when(first)
        def _():
            dstate[...] = jnp.zeros((4, 128, 128), F32)

        def step(it, carry):
            dstates, pn = carry
            c = ncb - 1 - it
            rows = pl.ds(pl.multiple_of(c * CHUNK, CHUNK), CHUNK)
            erow = pl.multiple_of(c * 8, 8)
            _, vjp = jax.vjp(_gdn_scan_chunk, tuple(sh_ref[h, c] for h in range(4)), u_ref[rows, :],
                             w_ref[rows, :].astype(F32), qe_ref[rows, :].astype(F32), kd_ref[rows, :].astype(F32),
                             at_ref[rows, :].astype(F32), egl_ref[pl.ds(erow, 1), :], z_ref[rows, :], ng_ref[...])
            nxt, du, dw, dqe, dkd, dat, degl, dz, dn = vjp((dstates, dy_ref[rows, :]))
            du_ref[rows, :] = du
            dw_ref[rows, :] = dw
            dqe_ref[rows, :] = dqe
            dkd_ref[rows, :] = dkd
            dat_ref[rows, :] = dat
            degl_ref[pl.ds(erow, 8), :] = jnp.broadcast_to(degl, (8, BW))
            dz_ref[rows, :] = dz.astype(BF16)
            return nxt, pn + dn

        dstates, pn = lax.fori_loop(0, ncb, step, (tuple(dstate[h] for h in range(4)), jnp.zeros((1, 128), F32)))
        for h in range(4):
            dstate[h] = dstates[h]

        @pl.when(first)
        def _():
            dng_ref[...] = pn

        @pl.when(jnp.logical_not(first))
        def _():
            dng_ref[...] += pn

    blk = pl.BlockSpec((tb, BW), lambda j: (nb - 1 - j, 0))
    eblk = pl.BlockSpec((ncb * 8, BW), lambda j: (nb - 1 - j, 0))
    vec = pl.BlockSpec((1, 128), lambda j: (0, 0))
    return pl.pallas_call(
        body, name=name, grid=(nb,),
        in_specs=[blk] * 5 + [eblk, pl.BlockSpec((tb, BW), lambda j: (nb - 1 - j, COL_CZ // BW)), blk,
                              pl.BlockSpec((4, ncb, 128, 128), lambda j: (0, nb - 1 - j, 0, 0)), vec],
        out_specs=[blk] * 5 + [eblk, blk, vec],
        out_shape=[jax.ShapeDtypeStruct((T, BW), F32)] * 5 + [jax.ShapeDtypeStruct((T // 8, BW), F32),
                                                              jax.ShapeDtypeStruct((T, BW), BF16),
                                                              jax.ShapeDtypeStruct((1, 128), F32)],
        scratch_shapes=[pltpu.VMEM((4, 128, 128), F32)],
        compiler_params=_cp("arbitrary"),
    )(u, w, qe, kd, at, egl, proj, dy, sh, ng)


def _adamw_update(w_ref, g_ref, m_ref, v_ref, d_ref, nm_ref, nv_ref):
    gv = g_ref[...]
    m2 = ADAM_B1 * m_ref[...] + (1.0 - ADAM_B1) * gv
    v2 = ADAM_B2 * v_ref[...] + (1.0 - ADAM_B2) * jnp.square(gv)
    m_hat = m2 / (1.0 - ADAM_B1 ** ADAM_STEP)
    v_hat = v2 / (1.0 - ADAM_B2 ** ADAM_STEP)
    d_ref[...] = -ADAM_LR * (m_hat / (jnp.sqrt(v_hat) + ADAM_EPS) + ADAM_WD * w_ref[...])
    nm_ref[...] = m2
    nv_ref[...] = v2


def _adamw_many(name, ws, gs, ms, vs):
    n = len(ws)

    def body(*refs):
        for i in range(n):
            _adamw_update(*[refs[k * n + i] for k in range(7)])

    return pl.pallas_call(
        body, name=name,
        out_shape=[jax.ShapeDtypeStruct(a.shape, F32) for a in ws] * 3,
        compiler_params=_cp(),
    )(*ws, *gs, *ms, *vs)


def _adamw(name, w, g, m, v):
    R, C = w.shape
    br = _pick(R, (512, 256, 240, 128, 64, 8))
    body = functools.partial(_adamw_update)
    spec = pl.BlockSpec((br, C), lambda i: (i, 0))
    return pl.pallas_call(
        body, name=name, grid=(R // br,),
        in_specs=[spec] * 4, out_specs=[spec] * 3,
        out_shape=[jax.ShapeDtypeStruct((R, C), F32)] * 3,
        compiler_params=_cp("parallel"),
    )(w, g, m, v)


def _sum8(name, parts):
    _, R, C = parts.shape
    br = _pick(R, (352, 368, 256, 128, 64, 16, 8))

    def body(p_ref, o_ref):
        acc = p_ref[0].astype(F32)
        for d in range(1, N_DEV):
            acc = acc + p_ref[d].astype(F32)
        o_ref[...] = acc

    return pl.pallas_call(
        body, name=name, grid=(R // br,),
        in_specs=[pl.BlockSpec((N_DEV, br, C), lambda i: (0, i, 0))],
        out_specs=pl.BlockSpec((br, C), lambda i: (i, 0)),
        out_shape=jax.ShapeDtypeStruct((R, C), F32),
        compiler_params=_cp("parallel"),
    )(parts)


_ANY = pl.BlockSpec(memory_space=pl.ANY)
_MESH = pl.DeviceIdType.MESH


def _all_gather(name, shard):
    R, C = shard.shape

    def body(x_ref, out_ref, send_sems, recv_sems, local_sem):
        x, y, c = lax.axis_index("x"), lax.axis_index("y"), lax.axis_index("c")
        me, sibling = (x, y, c), (x, y, 1 - c)
        chips = [(1 - x, y), (x, 1 - y), (1 - x, 1 - y)]

        def slot(px, py, pc):
            return out_ref.at[4 * px + 2 * py + pc]

        def copy(k, block, to, src=None):
            return pltpu.make_async_remote_copy(
                src_ref=slot(*block) if src is None else src, dst_ref=slot(*block),
                send_sem=send_sems.at[k], recv_sem=recv_sems.at[k], device_id=to, device_id_type=_MESH)

        mine = pltpu.make_async_copy(x_ref, slot(*me), local_sem)
        mine.start()
        first = [copy(0, me, sibling, src=x_ref)]
        first += [copy(1 + j, me, (*chip, c), src=x_ref) for j, chip in enumerate(chips)]
        for cp in first:
            cp.start()
        passed = [copy(4 + j, (*chip, c), sibling) for j, chip in enumerate(chips)]
        for j, chip in enumerate(chips):
            copy(1 + j, (*chip, c), me).wait_recv()
            passed[j].start()
        copy(0, sibling, me).wait_recv()
        for j, chip in enumerate(chips):
            copy(4 + j, (*chip, 1 - c), me).wait_recv()
        for cp in first + passed:
            cp.wait_send()
        mine.wait()

    return pl.pallas_call(
        body, name=name,
        in_specs=[_ANY], out_specs=_ANY,
        out_shape=jax.ShapeDtypeStruct((N_DEV, R, C), shard.dtype),
        scratch_shapes=[pltpu.SemaphoreType.DMA((7,)), pltpu.SemaphoreType.DMA((7,)), pltpu.SemaphoreType.DMA],
    )(shard)


_HBM = pl.BlockSpec(memory_space=pltpu.HBM)
_SEM = pl.BlockSpec(memory_space=pltpu.SEMAPHORE)
_EFFECT = pltpu.SideEffectType.DATAFLOW_SIDE_EFFECTING


def _exchange_copies(src_ref, land_ref, send_sems, recv_sems, scatter):
    x, y, c = lax.axis_index("x"), lax.axis_index("y"), lax.axis_index("c")
    me = 4 * x + 2 * y + c
    copies = []
    for k in range(1, N_DEV):
        px, py, pc = x ^ ((k >> 2) & 1), y ^ ((k >> 1) & 1), c ^ (k & 1)
        src = src_ref.at[4 * px + 2 * py + pc] if scatter else src_ref
        copies.append(pltpu.make_async_remote_copy(
            src_ref=src, dst_ref=land_ref.at[me], send_sem=send_sems.at[k - 1], recv_sem=recv_sems.at[k - 1],
            device_id=(px, py, pc), device_id_type=_MESH))
    return copies


def _own_copy(src_ref, land_ref, send_sems, scatter):
    me = 4 * lax.axis_index("x") + 2 * lax.axis_index("y") + lax.axis_index("c")
    return pltpu.make_async_copy(src_ref.at[me] if scatter else src_ref, land_ref.at[me], send_sems.at[N_DEV - 1])


def _exchange_start(name, srcs, lands, scatter, after=None):
    n = len(srcs)

    def body(*refs):
        src_refs, land_refs = refs[:n], refs[n:2 * n]
        outs = refs[2 * n + (after is not None):]
        send, recv = outs[:n], outs[n:2 * n]
        token = refs[-1]
        for g in range(n):
            for cp in _exchange_copies(src_refs[g], land_refs[g], send[g], recv[g], scatter):
                cp.start()
            _own_copy(src_refs[g], land_refs[g], send[g], scatter).start()
        token[...] = jnp.zeros_like(token)

    outs = pl.pallas_call(
        body, name=name,
        out_shape=tuple([pltpu.SemaphoreType.DMA((N_DEV,))] * (2 * n)
                        + [pltpu.HBM(a.shape, a.dtype) for a in list(srcs) + list(lands)]
                        + [jax.ShapeDtypeStruct((8, 128), F32)]),
        in_specs=[_HBM] * (2 * n) + [_ANY] * (after is not None),
        out_specs=tuple([_SEM] * (2 * n) + [_HBM] * (2 * n) + [pl.BlockSpec(memory_space=pltpu.VMEM)]),
        input_output_aliases={i: 2 * n + i for i in range(2 * n)},
        compiler_params=pltpu.CompilerParams(has_side_effects=_EFFECT),
    )(*[pltpu.with_memory_space_constraint(a, pltpu.HBM) for a in list(srcs) + list(lands)],
      *([after] if after is not None else []))
    handles = [(outs[2 * n + g], outs[3 * n + g], outs[g], outs[n + g]) for g in range(n)]
    return handles, outs[-1]


def _exchange_wait(name, handles, after, scatter):
    n = len(handles)
    srcs, lands, sends, recvs = ([h[i] for h in handles] for i in range(4))

    def body(*refs):
        src_refs, land_refs = refs[:n], refs[n:2 * n]
        send, recv = refs[2 * n:3 * n], refs[3 * n:4 * n]
        for g in range(n):
            for cp in _exchange_copies(src_refs[g], land_refs[g], send[g], recv[g], scatter):
                cp.wait_send()
                cp.wait_recv()
            _own_copy(src_refs[g], land_refs[g], send[g], scatter).wait()

    outs = pl.pallas_call(
        body, name=name,
        out_shape=tuple(pltpu.HBM(a.shape, a.dtype) for a in srcs + lands),
        in_specs=tuple([_HBM] * (2 * n) + [_SEM] * (2 * n) + [_ANY]), out_specs=tuple([_HBM] * (2 * n)),
        input_output_aliases={i: i for i in range(2 * n)},
        compiler_params=pltpu.CompilerParams(has_side_effects=_EFFECT),
    )(*srcs, *lands, *sends, *recvs, after)
    return list(outs[n:])


def _rows(a):
    return a.reshape(-1, 1024)


def _rows_to_parts(full):
    n = full.shape[-2] // N_DEV
    t = full.reshape(full.shape[:-2] + (N_DEV, n, full.shape[-1]))
    return jnp.moveaxis(t, -3, 0)


def _parts_to_rows(parts):
    t = jnp.moveaxis(parts, 0, -3)
    return t.reshape(t.shape[:-3] + (t.shape[-3] * t.shape[-2], t.shape[-1]))


def _parts_to_cols(parts):
    t = jnp.moveaxis(parts, 0, -2)
    return t.reshape(t.shape[:-2] + (t.shape[-2] * t.shape[-1],))


def _join(parts, axis=0):
    total = sum(p.shape[axis] for p in parts)
    out, off = None, 0
    for p in parts:
        cfg = [(0, 0)] * p.ndim
        cfg[axis] = (off, total - off - p.shape[axis])
        t = jnp.pad(p, cfg)
        out = t if out is None else out + t
        off += p.shape[axis]
    return out


def _w_in_to_layout(w):
    tail = jnp.pad(w[4096:4104], ((0, PW - COL_TAIL - 8), (0, 0)))
    return jnp.concatenate([w[:4096], w[4104:P_IN], tail], axis=0)


def _w_in_from_layout(g):
    return _join([g[:4096], g[COL_TAIL:COL_TAIL + 8], g[4096:COL_TAIL]], axis=0)


def _block_diag(w):
    w = w.reshape(4, 2, 64, 64)
    return jnp.pad(w[:, 0], ((0, 0), (0, 64), (0, 64))) + jnp.pad(w[:, 1], ((0, 0), (64, 0), (64, 0)))


def _block_diag_grad(g):
    return jnp.stack([g[:, :64, :64], g[:, 64:, 64:]], axis=1).reshape(8, 64, 64)


def _ffn_forward(tag, x, norm, wg, wu, wd):
    h = _rms_fwd(tag + "_norm", x, norm)
    a, b, act = _ffn_up(tag + "_up", h, wg, wu)
    if callable(wd):
        wd = wd(act)
    x_out = _mm(tag + "_down", [(act, wd)], "nn", F32, res=x, scale=0.5)
    return x_out, (x, h, a, b, act), wd


def _ffn_backward(tag, dx_out, saved, norm, wg, wu, wd, put, names, split=False):
    x, h, a, b, act = saved
    n_wg, n_wu, n_wd = names
    dwd = _mm(tag + "_dwd", [(act, dx_out)], "tn", BF16, scale=0.5, bm=FF // 2)
    tok = put({n_wd: dwd}) if split else None
    da, db = _ffn_dact(tag + "_dact", dx_out, wd, a, b, after=tok)
    dwg = _mm(tag + "_dwg", [(da, h)], "tn", BF16, bm=FF // 2)
    if split:
        tok = tok + put({n_wg: dwg})
    dwu = _mm(tag + "_dwu", [(db, h)], "tn", BF16, bm=FF // 2, after=tok)
    tok = tok + put({n_wu: dwu}) if split else put({n_wg: dwg, n_wu: dwu, n_wd: dwd})
    dh = _mm(tag + "_dh", [(da, wg), (db, wu)], "nn", F32, after=tok)
    dx, dnorm = _rms_bwd(tag + "_dnorm", x, norm + tok, dh, dx_out)
    return dx, dnorm


def _mixer_params(p):
    alog = jnp.pad(p["gdn_a_log"], (4, 120))[None]
    dtb = jnp.pad(p["gdn_dt_bias"], (4, 120))[None]
    bias = jnp.repeat(p["sgu_b"].T, 128, axis=1)
    return dict(
        ln_g=p["sgu_ln_g"][None], ln_b=p["sgu_ln_b"][None], sgu_w=p["sgu_w"], sgu_bias=bias,
        lru_cw=p["lru_conv_w"], lru_cb=p["lru_conv_b"][None], wa=_block_diag(p["lru_wa"]), ba=p["lru_ba"][None],
        wx=_block_diag(p["lru_wx"]), bx=p["lru_bx"][None], lam=p["lru_lambda"][None],
        gdn_cw=p["gdn_conv_w"], alog=alog, dtb=dtb, ng=p["gdn_norm_g"][None],
        pool_w=p["pool_w"], pool_sc=p["pool_scale"][None])


def _mix_forward(tag, x, p, mp):
    h = _rms_fwd(tag + "_norm", x, p["mix_norm"][None])
    proj = _mm(tag + "_proj", [(h, p["w_in"])], "nt", F32, bm=_pick(x.shape[0], (2048, 1024, 512, 256, 128)))
    y_a = _sgu_fwd(tag + "_sgu", proj, mp["ln_g"], mp["ln_b"], mp["sgu_w"], mp["sgu_bias"])
    y_b, hc = _lru_fwd(tag + "_lru", proj, mp["lru_cw"], mp["lru_cb"], mp["wa"], mp["ba"], mp["wx"], mp["bx"],
                       mp["lam"])
    qa = _conv_fwd(tag + "_convq", proj, COL_CQ, mp["gdn_cw"], 0)
    ka = _conv_fwd(tag + "_convk", proj, COL_CK, mp["gdn_cw"], 512)
    va = _conv_fwd(tag + "_convv", proj, COL_CV, mp["gdn_cw"], 1024)
    prep = _gdn_prep_fwd(tag + "_gdnprep", qa, ka, va, proj, mp["alog"], mp["dtb"])
    y_c, sh = _gdn_fwd(tag + "_gdn", *prep, proj, mp["ng"])
    y_d = _pool_fwd(tag + "_pool", proj, mp["pool_w"], mp["pool_sc"])
    ys = (y_a, y_b, y_c, y_d)
    if callable(p["w_branch"]):
        p["w_branch"] = p["w_branch"](y_d)
    merged = _merge_fwd(tag + "_merge", ys, p["w_branch"], proj)
    if callable(p["w_out"]):
        p["w_out"] = p["w_out"](merged)
    x_out = _mm(tag + "_out", [(merged, p["w_out"])], "nn", F32, res=x)
    return x_out, (x, h, proj, hc, qa, ka, va, prep, sh, ys, merged)


def _mix_backward(tag, dx_out, saved, p, mp, put):
    x, h, proj, hc, qa, ka, va, prep, sh, ys, merged = saved
    T = x.shape[0]
    g = {}
    dmerged = _mm(tag + "_dmerged", [(dx_out, p["w_out"])], "nt", F32)
    g["w_out"] = _mm(tag + "_dwout", [(merged, dx_out)], "tn", BF16)
    outs = _merge_bwd(tag + "_dmerge", dmerged, ys, p["w_branch"], proj)
    dgates, dbrs, dys = outs[:NBR], outs[NBR:2 * NBR], outs[2 * NBR:]
    g["w_branch"] = jnp.stack([_mm(f"{tag}_dwb{i}", [(dbrs[i], ys[i])], "tn", BF16) for i in range(NBR)])

    du, dv, dln_g, dln_b, dsgu_w, dbias = _sgu_bwd(tag + "_dsgu", proj, dys[0], mp["ln_g"], mp["ln_b"], mp["sgu_w"],
                                                  mp["sgu_bias"])
    g["sgu_ln_g"], g["sgu_ln_b"], g["sgu_w"] = dln_g[0], dln_b[0], dsgu_w
    g["sgu_b"] = dbias.reshape(128, 4, 128).sum(axis=2).T

    (dbx, dbg, dcw, dcb, dwa, dba, dwx, dbxb, dlam) = _lru_bwd(
        tag + "_dlru", proj, dys[1], hc, mp["lru_cw"], mp["lru_cb"], mp["wa"], mp["ba"], mp["wx"], mp["bx"], mp["lam"])
    g["lru_conv_w"], g["lru_conv_b"], g["lru_ba"], g["lru_bx"], g["lru_lambda"] = dcw, dcb[0], dba[0], dbxb[0], dlam[0]
    g["lru_wa"], g["lru_wx"] = _block_diag_grad(dwa), _block_diag_grad(dwx)

    *dprep, dz, dng = _gdn_bwd(tag + "_dgdn", *prep, proj, dys[2], sh, mp["ng"])
    dqa, dka, dva, dtail, dalog, ddtb = _gdn_prep_bwd(tag + "_dgdnprep", qa, ka, va, proj, mp["alog"], mp["dtb"], *dprep)
    g["gdn_a_log"], g["gdn_dt_bias"], g["gdn_norm_g"] = dalog[0, 4:8], ddtb[0, 4:8], dng[0]
    dq, dcwq = _conv_bwd(tag + "_dconvq", proj, COL_CQ, dqa, mp["gdn_cw"], 0)
    dk, dcwk = _conv_bwd(tag + "_dconvk", proj, COL_CK, dka, mp["gdn_cw"], 512)
    dv_, dcwv = _conv_bwd(tag + "_dconvv", proj, COL_CV, dva, mp["gdn_cw"], 1024)
    g["gdn_conv_w"] = jnp.concatenate([dcwq, dcwk, dcwv], axis=1)

    dd, dpw, dsc = _pool_bwd(tag + "_dpool", proj, dys[3], mp["pool_w"], mp["pool_sc"])
    g["pool_w"], g["pool_scale"] = dpw, dsc[0]

    dproj = jnp.concatenate([du, dv, dbx, dbg, dq, dk, dv_, dz, dd, *dgates, dtail,
                             jnp.zeros((T, PW - COL_TAIL - 128), BF16)], axis=1)
    dw_in = _mm(tag + "_dwin", [(dproj, h)], "tn", BF16)
    tok = put(dict(w_in=_w_in_from_layout(dw_in), w_branch=g.pop("w_branch"), w_out=g.pop("w_out")))
    dh = _mm(tag + "_dh", [(dproj, p["w_in"])], "nn", F32, bm=_pick(T, (2048, 1024, 512, 256, 128)), after=tok)
    dx, dnorm = _rms_bwd(tag + "_dnorm", x, p["mix_norm"][None] + tok, dh, dx_out)
    g["mix_norm"] = dnorm[0]
    return dx, g


_BIG = ("ff1_wg", "ff1_wu", "ff1_wd", "w_in", "w_branch", "w_out", "ff2_wg", "ff2_wu", "ff2_wd")
_COL_SHARDED = ("ff1_wg", "ff1_wu", "w_in", "w_branch", "ff2_wg", "ff2_wu")
_SMALL = ("ff1_norm", "mix_norm", "sgu_ln_g", "sgu_ln_b", "sgu_w", "sgu_b", "lru_conv_w", "lru_conv_b", "lru_wa",
          "lru_ba", "lru_wx", "lru_bx", "lru_lambda", "gdn_conv_w", "gdn_a_log", "gdn_dt_bias", "gdn_norm_g", "pool_w",
          "pool_scale", "ff2_norm", "final_norm")
_WEIGHTS = ("ff1_norm", "ff1_wg", "ff1_wu", "ff1_wd", "mix_norm", "w_in", "sgu_ln_g", "sgu_ln_b", "sgu_w", "sgu_b",
            "lru_conv_w", "lru_conv_b", "lru_wa", "lru_ba", "lru_wx", "lru_bx", "lru_lambda", "gdn_conv_w", "gdn_a_log",
            "gdn_dt_bias", "gdn_norm_g", "pool_w", "pool_scale", "w_branch", "w_out", "ff2_norm", "ff2_wg", "ff2_wu",
            "ff2_wd", "final_norm")
_CONV_SHARDED = ("lru_conv_w", "gdn_conv_w")
PACK_ROW_ALIGN = 16
_GROUPS = (("ff1", ("ff1_wg", "ff1_wu", "ff1_wd")), ("mix", ("w_in", "w_branch", "w_out")),
           ("ff2", ("ff2_wg", "ff2_wu", "ff2_wd")))


def _pad_rows(a, mult):
    pad = (-a.shape[-2]) % mult
    if pad == 0:
        return a
    return jnp.pad(a, [(0, 0)] * (a.ndim - 2) + [(0, pad), (0, 0)])


def _my_index():
    return 4 * lax.axis_index("x") + 2 * lax.axis_index("y") + lax.axis_index("c")


def _landing(shape, dtype):
    return lax.empty((N_DEV,) + tuple(shape), dtype)


def _stored(n, a):
    return jnp.swapaxes(a, -1, -2) if n in _COL_SHARDED else a


_FIRST = ("ff1_wg", "ff1_wu", "ff1_wd")


def _gather_first(w):
    names = _FIRST
    shards = [_rows(_stored(n, w[n][0]).astype(BF16)) for n in names]
    got = _all_gather("gather_first", jnp.concatenate(shards, axis=0))
    out, r = {}, 0
    for n, s in zip(names, shards):
        out[n] = got[:, r:r + s.shape[0]].reshape(-1, 1024)
        r += s.shape[0]
    return out, got


def _gather_start(w, after):
    conv = _pad_rows(jnp.concatenate([w[n].reshape(1, -1) for n in _CONV_SHARDED], axis=1), 8)
    keys, srcs = ["conv"], [conv]
    for l in range(2):
        for sub, (_, names) in enumerate(_GROUPS):
            for n in names:
                if l > 0 or n not in _FIRST:
                    keys.append((l, sub, n))
                    srcs.append(_stored(n, w[n][l]).astype(BF16))
    lands = [_landing(s.shape, s.dtype) for s in srcs]
    handles, token = _exchange_start("gather_start", srcs, lands, scatter=False, after=after)
    return dict(zip(keys, handles)), token


def _gather_finish(l, sub, handles, first, after):
    names = _GROUPS[sub][1]
    if (l, sub) == (0, 0):
        out = dict(first)
        for n in names:
            if n not in _FIRST:
                out[n] = lambda later, n=n: _parts_to_rows(
                    _exchange_wait(f"gather_wait_00_{n}", [handles[(0, 0, n)]], later, scatter=False)[0])
    elif sub == 1:
        out = {n: (lambda later, n=n: _parts_to_rows(
            _exchange_wait(f"gather_wait_{l}{sub}_{n}", [handles[(l, sub, n)]], later, scatter=False)[0])) for n in names}
        out["w_in"] = out["w_in"](after)
    else:
        lands = _exchange_wait(f"gather_wait_{l}{sub}", [handles[(l, sub, n)] for n in names], after, scatter=False)
        out = {n: _parts_to_rows(land) for n, land in zip(names, lands)}
    if "w_in" in out:
        out["w_in"] = _w_in_to_layout(out["w_in"])
    return out


def _scatter_start(l, sub, grads):
    srcs, shapes = [], []
    for n in grads:
        parts = _rows_to_parts(grads[n])
        shapes.append(parts.shape[1:])
        srcs.append(_pad_rows(parts.reshape(N_DEV, -1, 1024), PACK_ROW_ALIGN))
    lands = [_landing(s.shape[1:], s.dtype) for s in srcs]
    tag = f"{l}{sub}" + ("" if len(grads) == len(_GROUPS[sub][1]) else "_" + "_".join(grads))
    handles, token = _exchange_start(f"scatter_start_{tag}", srcs, lands, scatter=True)
    return handles, (tag, tuple(grads), shapes), token


def _scatter_finish(l, sub, handles, meta, after):
    tag, names, shapes = meta
    lands = _exchange_wait(f"scatter_wait_{tag}", handles, after, scatter=True)
    out = {}
    for n, land, shape in zip(names, lands, shapes):
        size = 1
        for s in shape:
            size *= s
        summed = _sum8(f"sum_{l}{sub}_{n}", land)
        out[n] = _stored(n, summed[:size // 1024].reshape(shape))
    return out


def _gather_conv_finish(w, handles, after):
    gconv = _exchange_wait("gather_wait_conv", [handles["conv"]], after, scatter=False)[0][:, 0]
    full, r = {}, 0
    for n in _CONV_SHARDED:
        sz = w[n].size
        full[n] = _parts_to_cols(gconv[:, r:r + sz].reshape((N_DEV,) + w[n].shape))
        r += sz
    return full


def _forward_backward(x, tgt, w, conv, get_weights, put_grads, put_small, token):
    saved, params = [], []
    for l in range(2):
        p = {n: w[n][l] for n in _SMALL if n != "final_norm"}
        for n in _CONV_SHARDED:
            p[n] = conv[n][l]
        mp = _mixer_params(p)
        tok = token[:1, :1] if l == 0 else 0.0
        p.update(get_weights(l, 0, x))
        x, s1, p["ff1_wd"] = _ffn_forward(f"l{l}_ff1", x, p["ff1_norm"][None] + tok, p["ff1_wg"], p["ff1_wu"],
                                          p["ff1_wd"])
        p.update(get_weights(l, 1, x))
        x, s2 = _mix_forward(f"l{l}_mix", x, p, mp)
        p.update(get_weights(l, 2, x))
        x, s3, _ = _ffn_forward(f"l{l}_ff2", x, p["ff2_norm"][None], p["ff2_wg"], p["ff2_wu"], p["ff2_wd"])
        saved.append((s1, s2, s3))
        params.append((p, mp))
    loss, dx, dfinal = _final_loss("loss_head", x, w["final_norm"][None], tgt)
    tok = 0.0
    for l in (1, 0):
        p, mp = params[l]
        s1, s2, s3 = saved[l]
        g = {}

        def put(sub):
            return lambda grads, l=l: put_grads(l, sub, grads)[:1, :1]

        dx, dn = _ffn_backward(f"l{l}_ff2", dx, s3, p["ff2_norm"][None] + tok, p["ff2_wg"], p["ff2_wu"], p["ff2_wd"],
                               put(2), _GROUPS[2][1])
        g["ff2_norm"] = dn[0]
        dx, gm = _mix_backward(f"l{l}_mix", dx, s2, p, mp, put(1))
        g.update(gm)
        tok = 0.0
        if l == 0:
            tok = put_small("0a", g)[:1, :1]
            g = {}
        dx, dn = _ffn_backward(f"l{l}_ff1", dx, s1, p["ff1_norm"][None] + tok, p["ff1_wg"], p["ff1_wu"], p["ff1_wd"],
                               put(0), _GROUPS[0][1], split=(l == 0))
        g["ff1_norm"] = dn[0]
        if l == 1:
            g["final_norm"] = dfinal[0]
            g["loss"] = loss[0, :1]
        tok = put_small("1" if l == 1 else "0b", g)[:1, :1]
    return dx


SMALL_PIECE = 8 * 1024


def _pack_small(d, names):
    pieces = []
    for n in names:
        flat = d[n].reshape(-1)
        pieces.append(jnp.pad(flat, (0, (-flat.size) % SMALL_PIECE)).reshape(-1, 1024))
    return jnp.concatenate(pieces, axis=0)


def _unpack_small(pack, shapes, names):
    out, r = {}, 0
    for n in names:
        size = 1
        for s in shapes[n]:
            size *= s
        rows = -(-size // SMALL_PIECE) * 8
        out[n] = pack[r:r + rows].reshape(-1)[:size].reshape(shapes[n])
        r += rows
    return out


def _small_names(grads):
    return tuple(n for n in _SMALL + ("loss",) if n in grads)


def _small_start(tag, grads):
    pack = _pack_small(grads, _small_names(grads))
    handles, token = _exchange_start(f"small_start_{tag}", [pack], [_landing(pack.shape, pack.dtype)], scatter=False)
    return handles, {n: grads[n].shape for n in _small_names(grads)}, token


def _small_finish(tag, handles, shapes, after):
    landed = _exchange_wait(f"small_wait_{tag}", handles, after, scatter=False)[0]
    return _unpack_small(_sum8(f"sum_small_{tag}", landed), shapes, _small_names(shapes))


def _as2d(a):
    if a.ndim == 1:
        return a.reshape(1, -1)
    return a.reshape(-1, a.shape[-1])


def kernel(x, ff1_norm, ff1_wg, ff1_wu, ff1_wd, mix_norm, w_in, sgu_ln_g, sgu_ln_b, sgu_w, sgu_b, lru_conv_w, lru_conv_b, lru_wa, lru_ba, lru_wx, lru_bx, lru_lambda, gdn_conv_w, gdn_a_log, gdn_dt_bias, gdn_norm_g, pool_w, pool_scale, w_branch, w_out, ff2_norm, ff2_wg, ff2_wu, ff2_wd, final_norm, loss_target, m_ff1_norm, m_ff1_wg, m_ff1_wu, m_ff1_wd, m_mix_norm, m_w_in, m_sgu_ln_g, m_sgu_ln_b, m_sgu_w, m_sgu_b, m_lru_conv_w, m_lru_conv_b, m_lru_wa, m_lru_ba, m_lru_wx, m_lru_bx, m_lru_lambda, m_gdn_conv_w, m_gdn_a_log, m_gdn_dt_bias, m_gdn_norm_g, m_pool_w, m_pool_scale, m_w_branch, m_w_out, m_ff2_norm, m_ff2_wg, m_ff2_wu, m_ff2_wd, m_final_norm, v_ff1_norm, v_ff1_wg, v_ff1_wu, v_ff1_wd, v_mix_norm, v_w_in, v_sgu_ln_g, v_sgu_ln_b, v_sgu_w, v_sgu_b, v_lru_conv_w, v_lru_conv_b, v_lru_wa, v_lru_ba, v_lru_wx, v_lru_bx, v_lru_lambda, v_gdn_conv_w, v_gdn_a_log, v_gdn_dt_bias, v_gdn_norm_g, v_pool_w, v_pool_scale, v_w_branch, v_w_out, v_ff2_norm, v_ff2_wg, v_ff2_wu, v_ff2_wd, v_final_norm):
    w = dict(ff1_norm=ff1_norm, ff1_wg=ff1_wg, ff1_wu=ff1_wu, ff1_wd=ff1_wd, mix_norm=mix_norm, w_in=w_in,
             sgu_ln_g=sgu_ln_g, sgu_ln_b=sgu_ln_b, sgu_w=sgu_w, sgu_b=sgu_b, lru_conv_w=lru_conv_w,
             lru_conv_b=lru_conv_b, lru_wa=lru_wa, lru_ba=lru_ba, lru_wx=lru_wx, lru_bx=lru_bx, lru_lambda=lru_lambda,
             gdn_conv_w=gdn_conv_w, gdn_a_log=gdn_a_log, gdn_dt_bias=gdn_dt_bias, gdn_norm_g=gdn_norm_g, pool_w=pool_w,
             pool_scale=pool_scale, w_branch=w_branch, w_out=w_out, ff2_norm=ff2_norm, ff2_wg=ff2_wg, ff2_wu=ff2_wu,
             ff2_wd=ff2_wd, final_norm=final_norm)
    m = dict(ff1_norm=m_ff1_norm, ff1_wg=m_ff1_wg, ff1_wu=m_ff1_wu, ff1_wd=m_ff1_wd, mix_norm=m_mix_norm, w_in=m_w_in,
             sgu_ln_g=m_sgu_ln_g, sgu_ln_b=m_sgu_ln_b, sgu_w=m_sgu_w, sgu_b=m_sgu_b, lru_conv_w=m_lru_conv_w,
             lru_conv_b=m_lru_conv_b, lru_wa=m_lru_wa, lru_ba=m_lru_ba, lru_wx=m_lru_wx, lru_bx=m_lru_bx,
             lru_lambda=m_lru_lambda, gdn_conv_w=m_gdn_conv_w, gdn_a_log=m_gdn_a_log, gdn_dt_bias=m_gdn_dt_bias,
             gdn_norm_g=m_gdn_norm_g, pool_w=m_pool_w, pool_scale=m_pool_scale, w_branch=m_w_branch, w_out=m_w_out,
             ff2_norm=m_ff2_norm, ff2_wg=m_ff2_wg, ff2_wu=m_ff2_wu, ff2_wd=m_ff2_wd, final_norm=m_final_norm)
    v = dict(ff1_norm=v_ff1_norm, ff1_wg=v_ff1_wg, ff1_wu=v_ff1_wu, ff1_wd=v_ff1_wd, mix_norm=v_mix_norm, w_in=v_w_in,
             sgu_ln_g=v_sgu_ln_g, sgu_ln_b=v_sgu_ln_b, sgu_w=v_sgu_w, sgu_b=v_sgu_b, lru_conv_w=v_lru_conv_w,
             lru_conv_b=v_lru_conv_b, lru_wa=v_lru_wa, lru_ba=v_lru_ba, lru_wx=v_lru_wx, lru_bx=v_lru_bx,
             lru_lambda=v_lru_lambda, gdn_conv_w=v_gdn_conv_w, gdn_a_log=v_gdn_a_log, gdn_dt_bias=v_gdn_dt_bias,
             gdn_norm_g=v_gdn_norm_g, pool_w=v_pool_w, pool_scale=v_pool_scale, w_branch=v_w_branch, w_out=v_w_out,
             ff2_norm=v_ff2_norm, ff2_wg=v_ff2_wg, ff2_wu=v_ff2_wu, ff2_wd=v_ff2_wd, final_norm=v_final_norm)

    first, got_first = _gather_first(w)
    handles, token = _gather_start(w, got_first)
    conv = _gather_conv_finish(w, handles, token)
    pending = {}

    def get_weights(l, sub, after):
        return _gather_finish(l, sub, handles, first, after)

    def put_grads(l, sub, grads):
        hs, meta, tok = _scatter_start(l, sub, grads)
        pending[(l, sub, meta[0])] = (hs, meta)
        return tok

    def put_small(tag, grads):
        hs, shapes, tok = _small_start(tag, grads)
        pending[tag] = (hs, shapes)
        return tok

    T = x.shape[1]
    dx = _forward_backward(x.reshape(T, D), loss_target.reshape(T, D), w, conv, get_weights, put_grads, put_small,
                           token)
    per = {}
    for key in pending:
        if isinstance(key, tuple):
            per.setdefault(key[:2], {}).update(_scatter_finish(*key[:2], *pending[key], dx))
        else:
            per[key] = _small_finish(key, *pending[key], dx)
    grad = {n: jnp.stack([per[(0, sub)][n], per[(1, sub)][n]]) for sub, (_, names) in enumerate(_GROUPS) for n in names}
    layer0 = {**per["0a"], **per["0b"]}
    small = {n: _join([layer0[n].reshape(-1), per["1"][n].reshape(-1)]).reshape((2,) + layer0[n].shape)
             for n in layer0}
    small["final_norm"] = per["1"]["final_norm"]
    loss = per["1"]["loss"][0]
    me = _my_index()
    for n in _SMALL:
        if n in _CONV_SHARDED:
            width = w[n].shape[-1]
            grad[n] = lax.dynamic_slice_in_dim(small[n], me * width, width, axis=2)
        else:
            grad[n] = small[n]

    delta, new_m, new_v = {}, {}, {}
    for n in _BIG:
        d_, m_, v_ = _adamw("adamw_" + n, _as2d(w[n]), _as2d(grad[n]), _as2d(m[n]), _as2d(v[n]))
        delta[n], new_m[n], new_v[n] = (t.reshape(w[n].shape) for t in (d_, m_, v_))

    outs = _adamw_many("adamw_small", *[[_as2d(t[n]) for n in _SMALL] for t in (w, grad, m, v)])
    for k, dst in enumerate((delta, new_m, new_v)):
        for i, n in enumerate(_SMALL):
            dst[n] = outs[k * len(_SMALL) + i].reshape(w[n].shape)

    return (loss, dx.reshape(x.shape), *[grad[n] for n in _WEIGHTS], *[delta[n] for n in _WEIGHTS],
            *[new_m[n] for n in _WEIGHTS], *[new_v[n] for n in _WEIGHTS])
```

```python
import functools

import jax
import jax.numpy as jnp
from jax import lax
from jax.experimental import pallas as pl
from jax.experimental.pallas import tpu as pltpu

F32 = jnp.float32
BF16 = jnp.bfloat16
HI = lax.Precision.HIGHEST

N_DEV = 8
D = 1024
FF = 2816
BW = 512
NBR = 4
CHUNK = 64
EPS = 1e-6
LRU_C = 8.0
GDN_DK = 128

COL_AU, COL_AV, COL_BX, COL_BG = 0, 512, 1024, 1536
COL_CQ, COL_CK, COL_CV, COL_CZ = 2048, 2560, 3072, 3584
COL_DX, COL_GATE, COL_TAIL = 4096, 4608, 8704
PW = 9216
P_IN = 8712

ADAM_LR, ADAM_B1, ADAM_B2, ADAM_EPS, ADAM_WD, ADAM_STEP = 0.001, 0.9, 0.999, 1e-08, 0.01, 10

VMEM_LIMIT_V7X = 56 * 1024 * 1024

_NN = (((1,), (0,)), ((), ()))
_NT = (((1,), (1,)), ((), ()))
_TN = (((0,), (0,)), ((), ()))


def _cp(*sem):
    return pltpu.CompilerParams(dimension_semantics=tuple(sem), vmem_limit_bytes=VMEM_LIMIT_V7X)


def _dot(a, b, dims=_NN):
    return lax.dot_general(a.astype(BF16), b.astype(BF16), dims, preferred_element_type=F32)


def _dot_hi(a, b, dims=_NN):
    return lax.dot_general(a, b, dims, precision=HI, preferred_element_type=F32)


def _pick(n, cands):
    for c in cands:
        if n % c == 0:
            return c
    return n


@jax.custom_jvp
def _log1p(x):
    u = 1.0 + x
    return jnp.where(u == 1.0, x, x * jnp.log(u) / jnp.where(u == 1.0, 1.0, u - 1.0))


@_log1p.defjvp
def _log1p_jvp(p, t):
    (x,), (dx,) = p, t
    return _log1p(x), dx / (1.0 + x)


@jax.custom_jvp
def _expm1(x):
    u = jnp.exp(x)
    lu = jnp.log(u)
    small = (u == 1.0) | (lu == 0.0)
    return jnp.where(small, x, (u - 1.0) * x / jnp.where(small, 1.0, lu))


@_expm1.defjvp
def _expm1_jvp(p, t):
    (x,), (dx,) = p, t
    return _expm1(x), dx * jnp.exp(x)


def _softplus(x):
    return jnp.maximum(x, 0.0) + _log1p(jnp.exp(-jnp.abs(x)))


def _sigmoid(x):
    return jax.nn.sigmoid(x)


def _silu(x):
    return x * jax.nn.sigmoid(x)


def _gelu(x):
    return jax.nn.gelu(x)


@functools.partial(jax.custom_vjp, nondiff_argnums=(1,))
def _shift(x, s):
    return x if s == 0 else pltpu.roll(x, s, 0)


def _shift_fwd(x, s):
    return _shift(x, s), None


def _shift_bwd(s, _, g):
    n = g.shape[0]
    return (g if s == 0 else pltpu.roll(g, n - s, 0),)


_shift.defvjp(_shift_fwd, _shift_bwd)


def _scan_steps(a, b, reverse):
    n = a.shape[0]
    row = lax.broadcasted_iota(jnp.int32, a.shape, 0)
    k = 1
    while k < n:
        sh = n - k if reverse else k
        m = (row < n - k) if reverse else (row >= k)
        a_s = jnp.where(m, pltpu.roll(a, sh, 0), 1.0)
        b_s = jnp.where(m, pltpu.roll(b, sh, 0), 0.0)
        b = a * b_s + b
        a = a * a_s
        k *= 2
    return b


@jax.custom_vjp
def _scan(a, b):
    return _scan_steps(a, b, False)


def _scan_fwd(a, b):
    h = _scan_steps(a, b, False)
    return h, (a, h)


def _scan_bwd(res, dh):
    a, h = res
    n = a.shape[0]
    row = lax.broadcasted_iota(jnp.int32, a.shape, 0)
    a_next = jnp.where(row < n - 1, pltpu.roll(a, n - 1, 0), 0.0)
    g = _scan_steps(a_next, dh, True)
    h_prev = jnp.where(row >= 1, pltpu.roll(h, 1, 0), 0.0)
    return g * h_prev, g


_scan.defvjp(_scan_fwd, _scan_bwd)


def _mm(name, pairs, mode, out_dtype, *, res=None, scale=1.0, bm=None, bn=None, bk=None, after=None):
    a0, b0 = pairs[0]
    if mode == "nn":
        (M, K), N = a0.shape, b0.shape[1]
    elif mode == "nt":
        (M, K), N = a0.shape, b0.shape[0]
    else:
        (K, M), N = a0.shape, b0.shape[1]
    bm = bm or _pick(M, (1024, 512, 256, 128))
    bn = bn or _pick(N, (1024, 512, 256, 128))
    bk = bk or _pick(K, (1024, 512, 1408, 256, 128))
    nk = K // bk
    npair = len(pairs)
    dims = {"nn": _NN, "nt": _NT, "tn": _TN}[mode]

    def body(*refs):
        ab = refs[:2 * npair]
        pos = 2 * npair
        r_ref = None
        if res is not None:
            r_ref = refs[pos]
            pos += 1
        pos += after is not None
        o_ref = refs[pos]
        part = None
        for p in range(npair):
            d = _dot(ab[2 * p][...], ab[2 * p + 1][...], dims)
            part = d if part is None else part + d

        def finish(acc):
            out = acc if scale == 1.0 else acc * scale
            if r_ref is not None:
                out = out + r_ref[...]
            o_ref[...] = out.astype(out_dtype)

        if nk == 1:
            finish(part)
        else:
            acc_ref = refs[pos + 1]
            k = pl.program_id(2)

            @pl.when(k == 0)
            def _():
                acc_ref[...] = part

            @pl.when(k > 0)
            def _():
                acc_ref[...] += part

            @pl.when(k == nk - 1)
            def _():
                finish(acc_ref[...])

    if mode == "nn":
        a_spec = pl.BlockSpec((bm, bk), lambda i, j, k: (i, k))
        b_spec = pl.BlockSpec((bk, bn), lambda i, j, k: (k, j))
    elif mode == "nt":
        a_spec = pl.BlockSpec((bm, bk), lambda i, j, k: (i, k))
        b_spec = pl.BlockSpec((bn, bk), lambda i, j, k: (j, k))
    else:
        a_spec = pl.BlockSpec((bk, bm), lambda i, j, k: (k, i))
        b_spec = pl.BlockSpec((bk, bn), lambda i, j, k: (k, j))
    o_spec = pl.BlockSpec((bm, bn), lambda i, j, k: (i, j))
    in_specs, args = [], []
    for a, b in pairs:
        in_specs += [a_spec, b_spec]
        args += [a, b]
    if res is not None:
        in_specs.append(o_spec)
        args.append(res)
    if after is not None:
        in_specs.append(_ANY)
        args.append(after)
    return pl.pallas_call(
        body, name=name, grid=(M // bm, N // bn, nk),
        in_specs=in_specs, out_specs=o_spec,
        out_shape=jax.ShapeDtypeStruct((M, N), out_dtype),
        scratch_shapes=[pltpu.VMEM((bm, bn), F32)] if nk > 1 else [],
        compiler_params=_cp("parallel", "parallel", "arbitrary"),
    )(*args)


def _rms_fwd(name, x, g):
    T = x.shape[0]
    bm = _pick(T, (512, 256, 128))

    def body(x_ref, g_ref, o_ref):
        xv = x_ref[...]
        r = lax.rsqrt(jnp.mean(xv * xv, axis=-1, keepdims=True) + EPS)
        o_ref[...] = (xv * r * g_ref[...]).astype(BF16)

    return pl.pallas_call(
        body, name=name, grid=(T // bm,),
        in_specs=[pl.BlockSpec((bm, D), lambda i: (i, 0)), pl.BlockSpec((1, D), lambda i: (0, 0))],
        out_specs=pl.BlockSpec((bm, D), lambda i: (i, 0)),
        out_shape=jax.ShapeDtypeStruct((T, D), BF16),
        compiler_params=_cp("parallel"),
    )(x, g)


def _rms_bwd(name, x, g, dh, dres):
    T = x.shape[0]
    bm = _pick(T, (512, 256, 128))

    def body(x_ref, g_ref, dh_ref, dres_ref, dx_ref, dg_ref):
        xv = x_ref[...]
        r = lax.rsqrt(jnp.mean(xv * xv, axis=-1, keepdims=True) + EPS)
        xh = xv * r
        dhv = dh_ref[...]
        dxh = dhv * g_ref[...]
        dx_ref[...] = dres_ref[...] + r * (dxh - xh * jnp.mean(dxh * xh, axis=-1, keepdims=True))
        part = jnp.sum(dhv * xh, axis=0, keepdims=True)

        @pl.when(pl.program_id(0) == 0)
        def _():
            dg_ref[...] = part

        @pl.when(pl.program_id(0) > 0)
        def _():
            dg_ref[...] += part

    row = pl.BlockSpec((bm, D), lambda i: (i, 0))
    vec = pl.BlockSpec((1, D), lambda i: (0, 0))
    return pl.pallas_call(
        body, name=name, grid=(T // bm,),
        in_specs=[row, vec, row, row], out_specs=[row, vec],
        out_shape=[jax.ShapeDtypeStruct((T, D), F32), jax.ShapeDtypeStruct((1, D), F32)],
        compiler_params=_cp("arbitrary"),
    )(x, g, dh, dres)


def _final_loss(name, x, g, tgt):
    T = x.shape[0]
    bm = _pick(T, (512, 256, 128))

    def body(x_ref, g_ref, t_ref, loss_ref, dx_ref, dg_ref):
        xv = x_ref[...]
        gv = g_ref[...]
        r = lax.rsqrt(jnp.mean(xv * xv, axis=-1, keepdims=True) + EPS)
        xh = xv * r
        e = xh * gv - t_ref[...]
        lpart = jnp.broadcast_to(0.5 * jnp.sum(jnp.mean(e * e, axis=-1, keepdims=True), axis=0, keepdims=True), (1, 128))
        dy = e * (1.0 / D)
        dxh = dy * gv
        dx_ref[...] = r * (dxh - xh * jnp.mean(dxh * xh, axis=-1, keepdims=True))
        gpart = jnp.sum(dy * xh, axis=0, keepdims=True)

        @pl.when(pl.program_id(0) == 0)
        def _():
            loss_ref[...] = lpart
            dg_ref[...] = gpart

        @pl.when(pl.program_id(0) > 0)
        def _():
            loss_ref[...] += lpart
            dg_ref[...] += gpart

    row = pl.BlockSpec((bm, D), lambda i: (i, 0))
    vec = pl.BlockSpec((1, D), lambda i: (0, 0))
    return pl.pallas_call(
        body, name=name, grid=(T // bm,),
        in_specs=[row, vec, row],
        out_specs=[pl.BlockSpec((1, 128), lambda i: (0, 0)), row, vec],
        out_shape=[jax.ShapeDtypeStruct((1, 128), F32), jax.ShapeDtypeStruct((T, D), F32),
                   jax.ShapeDtypeStruct((1, D), F32)],
        compiler_params=_cp("arbitrary"),
    )(x, g, tgt)


def _ffn_up(name, h, wg, wu):
    T = h.shape[0]
    bm = _pick(T, (2048, 1024, 512, 256, 128))
    bn = 256

    def body(h_ref, wg_ref, wu_ref, sa_ref, ds_ref, act_ref):
        hv = h_ref[...]
        a = _dot(hv, wg_ref[...], _NT)
        b = _dot(hv, wu_ref[...], _NT)
        s = _sigmoid(a)
        sa = a * s
        sa_ref[...] = sa.astype(BF16)
        ds_ref[...] = (b * (s * (1.0 + a * (1.0 - s)))).astype(BF16)
        act_ref[...] = (sa * b).astype(BF16)

    w_spec = pl.BlockSpec((bn, D), lambda i, j: (j, 0))
    o_spec = pl.BlockSpec((bm, bn), lambda i, j: (i, j))
    return pl.pallas_call(
        body, name=name, grid=(T // bm, FF // bn),
        in_specs=[pl.BlockSpec((bm, D), lambda i, j: (i, 0)), w_spec, w_spec],
        out_specs=[o_spec, o_spec, o_spec],
        out_shape=[jax.ShapeDtypeStruct((T, FF), BF16)] * 3,
        compiler_params=_cp("parallel", "parallel"),
    )(h, wg, wu)


def _ffn_dact(name, dy, wd, sa, ds, after=None):
    T = dy.shape[0]
    bm = _pick(T, (2048, 1024, 512, 256, 128))
    bn = 256

    def body(dy_ref, wd_ref, sa_ref, ds_ref, *rest):
        da_ref, db_ref, dy_bf = rest[-3:]

        @pl.when(pl.program_id(1) == 0)
        def _():
            dy_bf[...] = dy_ref[...].astype(BF16)

        dact = 0.5 * _dot(dy_bf[...], wd_ref[...], _NT)
        da_ref[...] = (dact * ds_ref[...].astype(F32)).astype(BF16)
        db_ref[...] = (dact * sa_ref[...].astype(F32)).astype(BF16)

    t_spec = pl.BlockSpec((bm, bn), lambda i, j: (i, j))
    return pl.pallas_call(
        body, name=name, grid=(T // bm, FF // bn),
        in_specs=[pl.BlockSpec((bm, D), lambda i, j: (i, 0)), pl.BlockSpec((bn, D), lambda i, j: (j, 0)),
                  t_spec, t_spec] + [_ANY] * (after is not None),
        out_specs=[t_spec, t_spec],
        out_shape=[jax.ShapeDtypeStruct((T, FF), BF16), jax.ShapeDtypeStruct((T, FF), BF16)],
        scratch_shapes=[pltpu.VMEM((bm, D), BF16)],
        compiler_params=_cp("parallel", "arbitrary"),
    )(dy, wd, sa, ds, *([after] if after is not None else []))


def _merge_specs(T, bm, bn):
    y_spec = pl.BlockSpec((bm, BW), lambda i, j: (i, 0))
    wb_spec = pl.BlockSpec((NBR, bn, BW), lambda i, j: (0, j, 0))
    gate_specs = [pl.BlockSpec((bm, bn), functools.partial(lambda i, j, o: (i, o + j), o=(COL_GATE + g * D) // bn))
                  for g in range(NBR)]
    t_spec = pl.BlockSpec((bm, bn), lambda i, j: (i, j))
    return y_spec, wb_spec, gate_specs, t_spec


def _merge_fwd(name, ys, wb, proj):
    T = proj.shape[0]
    bm = _pick(T, (512, 256, 128))
    bn = 512
    y_spec, wb_spec, gate_specs, t_spec = _merge_specs(T, bm, bn)

    def body(y0, y1, y2, y3, wb_ref, g0, g1, g2, g3, o_ref):
        acc = None
        for g, (y_ref, g_ref) in enumerate(((y0, g0), (y1, g1), (y2, g2), (y3, g3))):
            t = _sigmoid(g_ref[...]) * _dot(y_ref[...], wb_ref[g], _NT)
            acc = t if acc is None else acc + t
        o_ref[...] = acc.astype(BF16)

    return pl.pallas_call(
        body, name=name, grid=(T // bm, D // bn),
        in_specs=[y_spec] * NBR + [wb_spec] + gate_specs, out_specs=t_spec,
        out_shape=jax.ShapeDtypeStruct((T, D), BF16),
        compiler_params=_cp("parallel", "parallel"),
    )(*ys, wb, proj, proj, proj, proj)


def _merge_bwd(name, dm, ys, wb, proj):
    T = proj.shape[0]
    bm = _pick(T, (512, 256, 128))
    bn = 512
    y_spec, wb_spec, gate_specs, t_spec = _merge_specs(T, bm, bn)

    def body(dm_ref, y0, y1, y2, y3, wb_ref, g0, g1, g2, g3, *outs):
        dmv = dm_ref[...]
        j = pl.program_id(1)
        for g, (y_ref, g_ref) in enumerate(((y0, g0), (y1, g1), (y2, g2), (y3, g3))):
            br = _dot(y_ref[...], wb_ref[g], _NT)
            s = _sigmoid(g_ref[...])
            outs[g][...] = (dmv * br * (s * (1.0 - s))).astype(BF16)
            dbr = (dmv * s).astype(BF16)
            outs[NBR + g][...] = dbr
            part = _dot(dbr, wb_ref[g])
            dy_ref = outs[2 * NBR + g]

            @pl.when(j == 0)
            def _():
                dy_ref[...] = part

            @pl.when(j > 0)
            def _():
                dy_ref[...] += part

    return pl.pallas_call(
        body, name=name, grid=(T // bm, D // bn),
        in_specs=[t_spec] + [y_spec] * NBR + [wb_spec] + gate_specs, out_specs=[t_spec] * (2 * NBR) + [y_spec] * NBR,
        out_shape=[jax.ShapeDtypeStruct((T, D), BF16)] * (2 * NBR) + [jax.ShapeDtypeStruct((T, BW), F32)] * NBR,
        compiler_params=_cp("parallel", "arbitrary"),
    )(dm, *ys, wb, proj, proj, proj, proj)


def _sgu_block(u_pre, v_pre, ln_g, ln_b, w, bias):
    u = _gelu(u_pre)
    vf = _gelu(v_pre)
    mu = jnp.mean(vf, axis=-1, keepdims=True)
    var = jnp.mean(jnp.square(vf - mu), axis=-1, keepdims=True)
    vn = (vf - mu) * lax.rsqrt(var + EPS) * ln_g + ln_b
    ri = lax.broadcasted_iota(jnp.int32, (128, 128), 0)
    ci = lax.broadcasted_iota(jnp.int32, (128, 128), 1)
    mask = (ri // CHUNK) >= (ci // CHUNK)
    outs = [_dot(jnp.where(mask, w[g], 0.0), vn[:, g * 128:(g + 1) * 128]) for g in range(4)]
    mixed = jnp.concatenate(outs, axis=1) + bias
    return u * mixed


def _sgu_param_specs():
    return [pl.BlockSpec((1, BW), lambda i: (0, 0)), pl.BlockSpec((1, BW), lambda i: (0, 0)),
            pl.BlockSpec((4, 128, 128), lambda i: (0, 0, 0)), pl.BlockSpec((128, BW), lambda i: (0, 0))]


def _sgu_fwd(name, proj, ln_g, ln_b, w, bias):
    T = proj.shape[0]
    rb = _pick(T, (256, 128))

    def body(u_ref, v_ref, g_ref, b_ref, w_ref, bias_ref, y_ref):
        for n in range(rb // 128):
            rows = slice(n * 128, (n + 1) * 128)
            y = _sgu_block(u_ref[rows, :], v_ref[rows, :], g_ref[...], b_ref[...], w_ref[...], bias_ref[...])
            y_ref[rows, :] = y.astype(BF16)

    return pl.pallas_call(
        body, name=name, grid=(T // rb,),
        in_specs=[pl.BlockSpec((rb, BW), lambda i: (i, COL_AU // BW)), pl.BlockSpec((rb, BW), lambda i: (i, COL_AV // BW))]
        + _sgu_param_specs(),
        out_specs=pl.BlockSpec((rb, BW), lambda i: (i, 0)),
        out_shape=jax.ShapeDtypeStruct((T, BW), BF16),
        compiler_params=_cp("parallel"),
    )(proj, proj, ln_g, ln_b, w, bias)


def _sgu_bwd(name, proj, dy, ln_g, ln_b, w, bias):
    T = proj.shape[0]
    rb = _pick(T, (256, 128))

    def body(u_ref, v_ref, dy_ref, g_ref, b_ref, w_ref, bias_ref, du_ref, dv_ref, dg_ref, db_ref, dw_ref, dbias_ref):
        acc = None
        for n in range(rb // 128):
            rows = slice(n * 128, (n + 1) * 128)
            _, vjp = jax.vjp(_sgu_block, u_ref[rows, :], v_ref[rows, :], g_ref[...], b_ref[...], w_ref[...],
                             bias_ref[...])
            du, dv, *dp = vjp(dy_ref[rows, :])
            du_ref[rows, :] = du.astype(BF16)
            dv_ref[rows, :] = dv.astype(BF16)
            acc = dp if acc is None else [p + q for p, q in zip(acc, dp)]

        @pl.when(pl.program_id(0) == 0)
        def _():
            for r, p in zip((dg_ref, db_ref, dw_ref, dbias_ref), acc):
                r[...] = p

        @pl.when(pl.program_id(0) > 0)
        def _():
            for r, p in zip((dg_ref, db_ref, dw_ref, dbias_ref), acc):
                r[...] += p

    row = pl.BlockSpec((rb, BW), lambda i: (i, 0))
    return pl.pallas_call(
        body, name=name, grid=(T // rb,),
        in_specs=[pl.BlockSpec((rb, BW), lambda i: (i, COL_AU // BW)), pl.BlockSpec((rb, BW), lambda i: (i, COL_AV // BW)),
                  row] + _sgu_param_specs(),
        out_specs=[row, row] + _sgu_param_specs(),
        out_shape=[jax.ShapeDtypeStruct((T, BW), BF16), jax.ShapeDtypeStruct((T, BW), BF16),
                   jax.ShapeDtypeStruct((1, BW), F32), jax.ShapeDtypeStruct((1, BW), F32),
                   jax.ShapeDtypeStruct((4, 128, 128), F32), jax.ShapeDtypeStruct((128, BW), F32)],
        compiler_params=_cp("arbitrary"),
    )(proj, proj, dy, ln_g, ln_b, w, bias)


def _halo_block(ref, i, rblk, halo):
    r0 = pl.multiple_of(i * rblk, rblk)
    h0 = pl.multiple_of(jnp.maximum(r0 - halo, 0), halo)
    top = jnp.where(i > 0, ref[pl.ds(h0, halo), :], 0.0)
    return jnp.concatenate([top, ref[pl.ds(r0, rblk), :]], axis=0)


def _with_halo_grad(dfull, pending, halo, rblk):
    tail = jnp.concatenate([jnp.zeros((rblk - halo, 128), F32), pending], axis=0)
    return dfull[halo:] + tail


def _conv4(xfull, rows):
    acc = None
    for k in range(4):
        t = rows[k] * _shift(xfull, 3 - k)[8:]
        acc = t if acc is None else acc + t
    return acc


def _lru_block(xfull, gate, h0, c0, c1, c2, c3, cb, wa, ba, wx, bx, lam):
    n = gate.shape[0]
    xc = _conv4(xfull, (c0, c1, c2, c3)) + cb
    r = _sigmoid(_dot(xc, wa) + ba)
    ig = _sigmoid(_dot(xc, wx) + bx)
    log_a = -LRU_C * r * _softplus(-lam)
    a = jnp.exp(log_a)
    mult = jnp.sqrt(-_expm1(2.0 * log_a))
    b = mult * (ig * xc)
    row = lax.broadcasted_iota(jnp.int32, (n, 128), 0)
    b = b + jnp.where(row == 0, a * h0, 0.0)
    h = _scan(a, b)
    out = h * _gelu(gate)
    h_last = jnp.sum(jnp.where(row == n - 1, h, 0.0), axis=0, keepdims=True)
    return out, h_last


def _lru_param_specs():
    vec = pl.BlockSpec((1, 128), lambda g: (0, g))
    mat = pl.BlockSpec((None, 128, 128), lambda g: (g, 0, 0))
    return [pl.BlockSpec((4, 128), lambda g: (0, g)), vec, mat, vec, mat, vec, vec]


def _lru_load_params(cw_ref, cb_ref, wa_ref, ba_ref, wx_ref, bx_ref, lam_ref):
    return (cw_ref[0:1, :], cw_ref[1:2, :], cw_ref[2:3, :], cw_ref[3:4, :], cb_ref[...], wa_ref[...], ba_ref[...],
            wx_ref[...], bx_ref[...], lam_ref[...])


def _lru_fwd(name, proj, cw, cb, wa, ba, wx, bx, lam):
    T = proj.shape[0]
    rblk = _pick(T, (256, 128))
    nblk = T // rblk

    def body(x_ref, gt_ref, cw_ref, cb_ref, wa_ref, ba_ref, wx_ref, bx_ref, lam_ref, y_ref, hc_ref):
        params = _lru_load_params(cw_ref, cb_ref, wa_ref, ba_ref, wx_ref, bx_ref, lam_ref)

        def step(i, h0):
            r0 = pl.multiple_of(i * rblk, rblk)
            out, h_last = _lru_block(_halo_block(x_ref, i, rblk, 8), gt_ref[pl.ds(r0, rblk), :], h0, *params)
            y_ref[pl.ds(r0, rblk), :] = out.astype(BF16)
            hc_ref[pl.ds(pl.multiple_of(i * 8, 8), 8), :] = jnp.broadcast_to(h0, (8, 128))
            return h_last

        lax.fori_loop(0, nblk, step, jnp.zeros((1, 128), F32))

    return pl.pallas_call(
        body, name=name, grid=(4,),
        in_specs=[pl.BlockSpec((T, 128), lambda g: (0, COL_BX // 128 + g)),
                  pl.BlockSpec((T, 128), lambda g: (0, COL_BG // 128 + g))] + _lru_param_specs(),
        out_specs=[pl.BlockSpec((T, 128), lambda g: (0, g)), pl.BlockSpec((nblk * 8, 128), lambda g: (0, g))],
        out_shape=[jax.ShapeDtypeStruct((T, BW), BF16), jax.ShapeDtypeStruct((nblk * 8, BW), F32)],
        compiler_params=_cp("parallel"),
    )(proj, proj, cw, cb, wa, ba, wx, bx, lam)


def _lru_bwd(name, proj, dy, hc, cw, cb, wa, ba, wx, bx, lam):
    T = proj.shape[0]
    rblk = _pick(T, (256, 128))
    nblk = T // rblk

    def body(x_ref, gt_ref, dy_ref, hc_ref, cw_ref, cb_ref, wa_ref, ba_ref, wx_ref, bx_ref, lam_ref,
             dx_ref, dgt_ref, dcw_ref, dcb_ref, dwa_ref, dba_ref, dwx_ref, dbx_ref, dlam_ref):
        params = _lru_load_params(cw_ref, cb_ref, wa_ref, ba_ref, wx_ref, bx_ref, lam_ref)

        def step(it, carry):
            dh_last, pending, acc = carry
            i = nblk - 1 - it
            r0 = pl.multiple_of(i * rblk, rblk)
            h0 = hc_ref[pl.ds(pl.multiple_of(i * 8, 8), 1), :]
            _, vjp = jax.vjp(_lru_block, _halo_block(x_ref, i, rblk, 8), gt_ref[pl.ds(r0, rblk), :], h0, *params)
            dfull, dgate, dh0, *dp = vjp((dy_ref[pl.ds(r0, rblk), :], dh_last))
            dx_ref[pl.ds(r0, rblk), :] = _with_halo_grad(dfull, pending, 8, rblk).astype(BF16)
            dgt_ref[pl.ds(r0, rblk), :] = dgate.astype(BF16)
            return dh0, dfull[:8], tuple(p + q for p, q in zip(acc, dp))

        zeros = tuple(jnp.zeros(p.shape, F32) for p in params)
        _, _, acc = lax.fori_loop(0, nblk, step, (jnp.zeros((1, 128), F32), jnp.zeros((8, 128), F32), zeros))
        for k in range(4):
            dcw_ref[k:k + 1, :] = acc[k]
        for r, p in zip((dcb_ref, dwa_ref, dba_ref, dwx_ref, dbx_ref, dlam_ref), acc[4:]):
            r[...] = p

    col = pl.BlockSpec((T, 128), lambda g: (0, g))
    return pl.pallas_call(
        body, name=name, grid=(4,),
        in_specs=[pl.BlockSpec((T, 128), lambda g: (0, COL_BX // 128 + g)),
                  pl.BlockSpec((T, 128), lambda g: (0, COL_BG // 128 + g)), col,
                  pl.BlockSpec((nblk * 8, 128), lambda g: (0, g))] + _lru_param_specs(),
        out_specs=[col, col] + _lru_param_specs(),
        out_shape=[jax.ShapeDtypeStruct((T, BW), BF16), jax.ShapeDtypeStruct((T, BW), BF16),
                   jax.ShapeDtypeStruct((4, BW), F32), jax.ShapeDtypeStruct((1, BW), F32),
                   jax.ShapeDtypeStruct((4, 128, 128), F32), jax.ShapeDtypeStruct((1, BW), F32),
                   jax.ShapeDtypeStruct((4, 128, 128), F32), jax.ShapeDtypeStruct((1, BW), F32),
                   jax.ShapeDtypeStruct((1, BW), F32)],
        compiler_params=_cp("parallel"),
    )(proj, proj, dy, hc, cw, cb, wa, ba, wx, bx, lam)


def _conv_block(xfull, c0, c1, c2, c3):
    return _silu(_conv4(xfull, (c0, c1, c2, c3)))


def _conv_fwd(name, proj, col0, cw, cw_col0):
    T = proj.shape[0]
    rblk = _pick(T, (256, 128))
    nblk = T // rblk

    def body(x_ref, cw_ref, y_ref):
        rows = (cw_ref[0:1, :], cw_ref[1:2, :], cw_ref[2:3, :], cw_ref[3:4, :])

        def step(i, c):
            r0 = pl.multiple_of(i * rblk, rblk)
            y_ref[pl.ds(r0, rblk), :] = _conv_block(_halo_block(x_ref, i, rblk, 8), *rows)
            return c

        lax.fori_loop(0, nblk, step, 0)

    return pl.pallas_call(
        body, name=name, grid=(4,),
        in_specs=[pl.BlockSpec((T, 128), lambda g: (0, col0 // 128 + g)),
                  pl.BlockSpec((4, 128), lambda g: (0, cw_col0 // 128 + g))],
        out_specs=pl.BlockSpec((T, 128), lambda g: (0, g)),
        out_shape=jax.ShapeDtypeStruct((T, BW), F32),
        compiler_params=_cp("parallel"),
    )(proj, cw)


def _conv_bwd(name, proj, col0, dy, cw, cw_col0):
    T = proj.shape[0]
    rblk = _pick(T, (256, 128))
    nblk = T // rblk

    def body(x_ref, dy_ref, cw_ref, dx_ref, dcw_ref):
        rows = (cw_ref[0:1, :], cw_ref[1:2, :], cw_ref[2:3, :], cw_ref[3:4, :])

        def step(it, carry):
            pending, acc = carry
            i = nblk - 1 - it
            r0 = pl.multiple_of(i * rblk, rblk)
            _, vjp = jax.vjp(_conv_block, _halo_block(x_ref, i, rblk, 8), *rows)
            dfull, *dp = vjp(dy_ref[pl.ds(r0, rblk), :])
            dx_ref[pl.ds(r0, rblk), :] = _with_halo_grad(dfull, pending, 8, rblk).astype(BF16)
            return dfull[:8], tuple(p + q for p, q in zip(acc, dp))

        zeros = tuple(jnp.zeros((1, 128), F32) for _ in range(4))
        _, acc = lax.fori_loop(0, nblk, step, (jnp.zeros((8, 128), F32), zeros))
        for k in range(4):
            dcw_ref[k:k + 1, :] = acc[k]

    col = pl.BlockSpec((T, 128), lambda g: (0, g))
    return pl.pallas_call(
        body, name=name, grid=(4,),
        in_specs=[pl.BlockSpec((T, 128), lambda g: (0, col0 // 128 + g)), col,
                  pl.BlockSpec((4, 128), lambda g: (0, cw_col0 // 128 + g))],
        out_specs=[col, pl.BlockSpec((4, 128), lambda g: (0, g))],
        out_shape=[jax.ShapeDtypeStruct((T, BW), BF16), jax.ShapeDtypeStruct((4, BW), F32)],
        compiler_params=_cp("parallel"),
    )(proj, dy, cw)


def _pool_block(xfull, pw, sc, t0, gi):
    n = xfull.shape[0] - 16
    s2 = xfull + _shift(xfull, 1)
    s4 = s2 + _shift(s2, 2)
    s8 = s4 + _shift(s4, 4)
    s16 = s8 + _shift(s8, 8)
    s = jnp.where(gi == 0, s2, jnp.where(gi == 1, s4, jnp.where(gi == 2, s8, s16)))[16:]
    t = t0 + lax.broadcasted_iota(jnp.int32, (n, 128), 0)
    cnt = jnp.minimum(t + 1, lax.shift_left(jnp.int32(2), gi)).astype(F32)
    pooled = s / cnt - xfull[16:]
    return _dot(pooled, pw) * sc


def _pool_fwd(name, proj, pw, sc):
    T = proj.shape[0]
    rblk = _pick(T, (256, 128))
    nblk = T // rblk

    def body(x_ref, pw_ref, sc_ref, y_ref):
        gi = pl.program_id(0)

        def step(i, c):
            r0 = pl.multiple_of(i * rblk, rblk)
            y = _pool_block(_halo_block(x_ref, i, rblk, 16), pw_ref[...], sc_ref[...], r0, gi)
            y_ref[pl.ds(r0, rblk), :] = y.astype(BF16)
            return c

        lax.fori_loop(0, nblk, step, 0)

    return pl.pallas_call(
        body, name=name, grid=(4,),
        in_specs=[pl.BlockSpec((T, 128), lambda g: (0, COL_DX // 128 + g)),
                  pl.BlockSpec((None, 128, 128), lambda g: (g, 0, 0)), pl.BlockSpec((1, 128), lambda g: (0, g))],
        out_specs=pl.BlockSpec((T, 128), lambda g: (0, g)),
        out_shape=jax.ShapeDtypeStruct((T, BW), BF16),
        compiler_params=_cp("parallel"),
    )(proj, pw, sc)


def _pool_bwd(name, proj, dy, pw, sc):
    T = proj.shape[0]
    rblk = _pick(T, (256, 128))
    nblk = T // rblk

    def body(x_ref, dy_ref, pw_ref, sc_ref, dx_ref, dpw_ref, dsc_ref):
        gi = pl.program_id(0)

        def step(it, carry):
            pending, apw, asc = carry
            i = nblk - 1 - it
            r0 = pl.multiple_of(i * rblk, rblk)
            _, vjp = jax.vjp(lambda xf, w, s: _pool_block(xf, w, s, r0, gi), _halo_block(x_ref, i, rblk, 16),
                             pw_ref[...], sc_ref[...])
            dfull, dw, ds = vjp(dy_ref[pl.ds(r0, rblk), :])
            dx_ref[pl.ds(r0, rblk), :] = _with_halo_grad(dfull, pending, 16, rblk).astype(BF16)
            return dfull[:16], apw + dw, asc + ds

        _, apw, asc = lax.fori_loop(0, nblk, step, (jnp.zeros((16, 128), F32), jnp.zeros((128, 128), F32),
                                                    jnp.zeros((1, 128), F32)))
        dpw_ref[...] = apw
        dsc_ref[...] = asc

    col = pl.BlockSpec((T, 128), lambda g: (0, g))
    mat = pl.BlockSpec((None, 128, 128), lambda g: (g, 0, 0))
    vec = pl.BlockSpec((1, 128), lambda g: (0, g))
    return pl.pallas_call(
        body, name=name, grid=(4,),
        in_specs=[pl.BlockSpec((T, 128), lambda g: (0, COL_DX // 128 + g)), col, mat, vec],
        out_specs=[col, mat, vec],
        out_shape=[jax.ShapeDtypeStruct((T, BW), BF16), jax.ShapeDtypeStruct((4, 128, 128), F32),
                   jax.ShapeDtypeStruct((1, BW), F32)],
        compiler_params=_cp("parallel"),
    )(proj, dy, pw, sc)


@jax.custom_vjp
def _dot3(a, b):
    ah = a.astype(BF16)
    al = (a - ah.astype(F32)).astype(BF16)
    bh = b.astype(BF16)
    bl = (b - bh.astype(F32)).astype(BF16)

    def d(x, y):
        return lax.dot_general(x, y, _NN, preferred_element_type=F32)

    return d(ah, bh) + (d(ah, bl) + d(al, bh))


def _dot3_fwd(a, b):
    return _dot3(a, b), (a, b)


def _dot3_bwd(res, g):
    a, b = res
    return _dot(g, b, _NT), _dot(a, g, _TN)


_dot3.defvjp(_dot3_fwd, _dot3_bwd)


def _pad_rows2(x):
    return jnp.concatenate([x, jnp.zeros_like(x)], axis=0)


@jax.custom_vjp
def _tri_inv(mats):
    n = mats[0].shape[0]
    eye = (lax.broadcasted_iota(jnp.int32, (n, n), 0) == lax.broadcasted_iota(jnp.int32, (n, n), 1)).astype(F32)
    ps = [eye - a for a in mats]
    ms = list(mats)
    k = 2
    while k < n:
        ms = [_dot3(t, t) for t in ms]
        ps = [p + _dot3(p, t) for p, t in zip(ps, ms)]
        k *= 2
    return ps


def _tri_inv_fwd(mats):
    ts = _tri_inv(mats)
    return ts, ts


def _tri_inv_bwd(ts, gs):
    half = [_dot(t, g, _TN) for t, g in zip(ts, gs)]
    return ([-_dot(h, t, _NT) for h, t in zip(half, ts)],)


_tri_inv.defvjp(_tri_inv_fwd, _tri_inv_bwd)


def _cumsum_rows(x):
    n = x.shape[0]
    row = lax.broadcasted_iota(jnp.int32, x.shape, 0)
    k = 1
    while k < n:
        x = x + jnp.where(row >= k, _shift(x, k), 0.0)
        k *= 2
    return x


def _gdn_prep(qcs, kcs, vcs, tails, alog, dtb):
    C = CHUNK
    pairs = [(c, h) for c in range(len(qcs)) for h in range(4)]
    lane = lax.broadcasted_iota(jnp.int32, (C, 128), 1)
    row = lax.broadcasted_iota(jnp.int32, (C, 128), 0)
    incl = row >= lane
    sig = [_sigmoid(t) for t in tails]
    gfull = [-jnp.exp(alog) * _softplus(t + dtb) for t in tails]
    beta = [jnp.sum(jnp.where(lane == h, sig[c], 0.0), axis=1, keepdims=True) for c, h in pairs]
    g = [jnp.sum(jnp.where(lane == h + 4, gfull[c], 0.0), axis=1, keepdims=True) for c, h in pairs]
    qs = [qcs[c][:, h * 128:(h + 1) * 128] for c, h in pairs]
    ks = [kcs[c][:, h * 128:(h + 1) * 128] for c, h in pairs]
    vs = [vcs[c][:, h * 128:(h + 1) * 128] for c, h in pairs]
    q = [t * lax.rsqrt(jnp.sum(t * t, axis=-1, keepdims=True) + EPS) * (GDN_DK ** -0.5) for t in qs]
    k = [t * lax.rsqrt(jnp.sum(t * t, axis=-1, keepdims=True) + EPS) for t in ks]
    gc = [_cumsum_rows(jnp.broadcast_to(t, (C, 128))) for t in g]
    gc_t = [jnp.transpose(jnp.concatenate([t, t], axis=0)) for t in gc]
    gc_col = [jnp.sum(jnp.where(lane == 0, t, 0.0), axis=1, keepdims=True) for t in gc]
    ri = lax.broadcasted_iota(jnp.int32, (C, C), 0)
    ci = lax.broadcasted_iota(jnp.int32, (C, C), 1)
    decay = [jnp.exp(jnp.where(incl, a - b[:C, :], -1e30)) for a, b in zip(gc, gc_t)]
    decay_sq = [jnp.exp(jnp.where(ri > ci, a - jnp.transpose(b)[:C, :], -1e30)) for a, b in zip(gc_col, gc)]
    kb = [a * b for a, b in zip(k, beta)]
    kk = [_dot(a, b, _NT) for a, b in zip(kb, k)]
    t_mat = _tri_inv([jnp.where(ri > ci, a * b, 0.0) for a, b in zip(kk, decay_sq)])
    egc = [jnp.exp(t) for t in gc]
    u = [_dot(t, a * b) for t, a, b in zip(t_mat, vs, beta)]
    w = [_dot(t, a * b) for t, a, b in zip(t_mat, kb, egc)]
    qk = [_dot(a, _pad_rows2(b), _NT) for a, b in zip(q, k)]
    attn = [jnp.where(incl, a * b, 0.0) for a, b in zip(qk, decay)]
    g_last = [jnp.sum(jnp.where(row == C - 1, t, 0.0), axis=0, keepdims=True) for t in gc]
    qe = [a * b for a, b in zip(q, egc)]
    kd = [a * jnp.exp(b - c_) for a, b, c_ in zip(k, g_last, gc)]
    egl = [jnp.exp(t) for t in g_last]

    def per_chunk(vals):
        return [jnp.concatenate(vals[4 * c:4 * c + 4], axis=1) for c in range(len(qcs))]

    return tuple(per_chunk(t) for t in (u, w, qe, kd, attn, egl))


def _gdn_scan_chunk(states, u, w, qe, kd, attn, egl, z, ng):
    hs = range(4)

    def sl(t, h):
        return t[:, h * 128:(h + 1) * 128]

    ws = [_dot(sl(w, h), states[h]) for h in hs]
    qs = [_dot(sl(qe, h), states[h]) for h in hs]
    v_new = [sl(u, h) - ws[h] for h in hs]
    av = [_dot(sl(attn, h), _pad_rows2(v_new[h])) for h in hs]
    kv = [_dot(sl(kd, h), v_new[h], _TN) for h in hs]
    nxt = tuple(states[h] * sl(egl, h) + kv[h] for h in hs)
    o = [qs[h] + av[h] for h in hs]
    on = [t * lax.rsqrt(jnp.mean(t * t, axis=-1, keepdims=True) + EPS) * ng for t in o]
    return nxt, jnp.concatenate(on, axis=1) * _silu(z)


def _gdn_blocks(T):
    tb = _pick(T, (512, 256, 128, 64))
    return tb, T // tb, tb // CHUNK


PREP_CHUNKS = 4


def _chunk_rows(i, n):
    return [pl.ds(pl.multiple_of((i * n + j) * CHUNK, CHUNK), CHUNK) for j in range(n)]


def _egl_rows(i, n, size):
    return [pl.ds(pl.multiple_of((i * n + j) * 8, 8), size) for j in range(n)]


def _gdn_prep_fwd(name, qa, ka, va, proj, alog, dtb):
    T = proj.shape[0]
    tb, nb, ncb = _gdn_blocks(T)
    n = PREP_CHUNKS if ncb % PREP_CHUNKS == 0 else 1

    def body(q_ref, k_ref, v_ref, tail_ref, alog_ref, dtb_ref, u_ref, w_ref, qe_ref, kd_ref, at_ref, egl_ref):
        def step(i, c):
            rows = _chunk_rows(i, n)
            u, w, qe, kd, at, egl = _gdn_prep([q_ref[r, :] for r in rows], [k_ref[r, :] for r in rows],
                                              [v_ref[r, :] for r in rows], [tail_ref[r, :] for r in rows],
                                              alog_ref[...], dtb_ref[...])
            for j, (r, e) in enumerate(zip(rows, _egl_rows(i, n, 8))):
                u_ref[r, :] = u[j]
                w_ref[r, :] = w[j].astype(BF16)
                qe_ref[r, :] = qe[j].astype(BF16)
                kd_ref[r, :] = kd[j].astype(BF16)
                at_ref[r, :] = at[j].astype(BF16)
                egl_ref[e, :] = jnp.broadcast_to(egl[j], (8, BW))
            return c

        lax.fori_loop(0, ncb // n, step, 0)

    blk = pl.BlockSpec((tb, BW), lambda j: (j, 0))
    vec = pl.BlockSpec((1, 128), lambda j: (0, 0))
    return pl.pallas_call(
        body, name=name, grid=(nb,),
        in_specs=[blk, blk, blk, pl.BlockSpec((tb, 128), lambda j: (j, COL_TAIL // 128)), vec, vec],
        out_specs=[blk] * 5 + [pl.BlockSpec((ncb * 8, BW), lambda j: (j, 0))],
        out_shape=[jax.ShapeDtypeStruct((T, BW), F32)] + [jax.ShapeDtypeStruct((T, BW), BF16)] * 4
        + [jax.ShapeDtypeStruct((T // 8, BW), F32)],
        compiler_params=_cp("parallel"),
    )(qa, ka, va, proj, alog, dtb)


def _gdn_prep_bwd(name, qa, ka, va, proj, alog, dtb, du, dw, dqe, dkd, dat, degl):
    T = proj.shape[0]
    tb, nb, ncb = _gdn_blocks(T)
    n = PREP_CHUNKS if ncb % PREP_CHUNKS == 0 else 1

    def body(q_ref, k_ref, v_ref, tail_ref, alog_ref, dtb_ref, du_ref, dw_ref, dqe_ref, dkd_ref, dat_ref, degl_ref,
             dq_ref, dk_ref, dv_ref, dtail_ref, dalog_ref, ddtb_ref):
        first = pl.program_id(0) == 0

        def step(i, carry):
            pa, pd = carry
            rows = _chunk_rows(i, n)
            _, vjp = jax.vjp(_gdn_prep, [q_ref[r, :] for r in rows], [k_ref[r, :] for r in rows],
                             [v_ref[r, :] for r in rows], [tail_ref[r, :] for r in rows], alog_ref[...], dtb_ref[...])
            cot = tuple([ref[r, :] for r in rows] for ref in (du_ref, dw_ref, dqe_ref, dkd_ref, dat_ref))
            dq, dk, dv, dtail, da, dd = vjp(cot + ([degl_ref[e, :] for e in _egl_rows(i, n, 1)],))
            for j, r in enumerate(rows):
                dq_ref[r, :] = dq[j]
                dk_ref[r, :] = dk[j]
                dv_ref[r, :] = dv[j]
                dtail_ref[r, :] = dtail[j].astype(BF16)
            return pa + da, pd + dd

        zv = jnp.zeros((1, 128), F32)
        pa, pd = lax.fori_loop(0, ncb // n, step, (zv, zv))

        @pl.when(first)
        def _():
            dalog_ref[...] = pa
            ddtb_ref[...] = pd

        @pl.when(jnp.logical_not(first))
        def _():
            dalog_ref[...] += pa
            ddtb_ref[...] += pd

    blk = pl.BlockSpec((tb, BW), lambda j: (j, 0))
    vec = pl.BlockSpec((1, 128), lambda j: (0, 0))
    return pl.pallas_call(
        body, name=name, grid=(nb,),
        in_specs=[blk, blk, blk, pl.BlockSpec((tb, 128), lambda j: (j, COL_TAIL // 128)), vec, vec]
        + [blk] * 5 + [pl.BlockSpec((ncb * 8, BW), lambda j: (j, 0))],
        out_specs=[blk, blk, blk, pl.BlockSpec((tb, 128), lambda j: (j, 0)), vec, vec],
        out_shape=[jax.ShapeDtypeStruct((T, BW), F32)] * 3 + [jax.ShapeDtypeStruct((T, 128), BF16)]
        + [jax.ShapeDtypeStruct((1, 128), F32)] * 2,
        compiler_params=_cp("arbitrary"),
    )(qa, ka, va, proj, alog, dtb, du, dw, dqe, dkd, dat, degl)


def _gdn_fwd(name, u, w, qe, kd, at, egl, proj, ng):
    T = proj.shape[0]
    tb, nb, ncb = _gdn_blocks(T)

    def body(u_ref, w_ref, qe_ref, kd_ref, at_ref, egl_ref, z_ref, ng_ref, y_ref, sh_ref, state):
        @pl.when(pl.program_id(0) == 0)
        def _():
            state[...] = jnp.zeros((4, 128, 128), F32)

        def step(c, states):
            rows = pl.ds(pl.multiple_of(c * CHUNK, CHUNK), CHUNK)
            for h in range(4):
                sh_ref[h, c] = states[h]
            nxt, y = _gdn_scan_chunk(states, u_ref[rows, :], w_ref[rows, :], qe_ref[rows, :], kd_ref[rows, :],
                                     at_ref[rows, :], egl_ref[pl.ds(pl.multiple_of(c * 8, 8), 1), :], z_ref[rows, :],
                                     ng_ref[...])
            y_ref[rows, :] = y.astype(BF16)
            return nxt

        states = lax.fori_loop(0, ncb, step, tuple(state[h] for h in range(4)))
        for h in range(4):
            state[h] = states[h]

    blk = pl.BlockSpec((tb, BW), lambda j: (j, 0))
    vec = pl.BlockSpec((1, 128), lambda j: (0, 0))
    return pl.pallas_call(
        body, name=name, grid=(nb,),
        in_specs=[blk] * 5 + [pl.BlockSpec((ncb * 8, BW), lambda j: (j, 0)),
                              pl.BlockSpec((tb, BW), lambda j: (j, COL_CZ // BW)), vec],
        out_specs=[blk, pl.BlockSpec((4, ncb, 128, 128), lambda j: (0, j, 0, 0))],
        out_shape=[jax.ShapeDtypeStruct((T, BW), BF16), jax.ShapeDtypeStruct((4, T // CHUNK, 128, 128), F32)],
        scratch_shapes=[pltpu.VMEM((4, 128, 128), F32)],
        compiler_params=_cp("arbitrary"),
    )(u, w, qe, kd, at, egl, proj, ng)


def _gdn_bwd(name, u, w, qe, kd, at, egl, proj, dy, sh, ng):
    T = proj.shape[0]
    tb, nb, ncb = _gdn_blocks(T)

    def body(u_ref, w_ref, qe_ref, kd_ref, at_ref, egl_ref, z_ref, dy_ref, sh_ref, ng_ref,
             du_ref, dw_ref, dqe_ref, dkd_ref, dat_ref, degl_ref, dz_ref, dng_ref, dstate):
        first = pl.program_id(0) == 0

        @pl.when(first)
        def _():
            dstate[...] = jnp.zeros((4, 128, 128), F32)

        def step(it, carry):
            dstates, pn = carry
            c = ncb - 1 - it
            rows = pl.ds(pl.multiple_of(c * CHUNK, CHUNK), CHUNK)
            erow = pl.multiple_of(c * 8, 8)
            _, vjp = jax.vjp(_gdn_scan_chunk, tuple(sh_ref[h, c] for h in range(4)), u_ref[rows, :],
                             w_ref[rows, :].astype(F32), qe_ref[rows, :].astype(F32), kd_ref[rows, :].astype(F32),
                             at_ref[rows, :].astype(F32), egl_ref[pl.ds(erow, 1), :], z_ref[rows, :], ng_ref[...])
            nxt, du, dw, dqe, dkd, dat, degl, dz, dn = vjp((dstates, dy_ref[rows, :]))
            du_ref[rows, :] = du
            dw_ref[rows, :] = dw
            dqe_ref[rows, :] = dqe
            dkd_ref[rows, :] = dkd
            dat_ref[rows, :] = dat
            degl_ref[pl.ds(erow, 8), :] = jnp.broadcast_to(degl, (8, BW))
            dz_ref[rows, :] = dz.astype(BF16)
            return nxt, pn + dn

        dstates, pn = lax.fori_loop(0, ncb, step, (tuple(dstate[h] for h in range(4)), jnp.zeros((1, 128), F32)))
        for h in range(4):
            dstate[h] = dstates[h]

        @pl.when(first)
        def _():
            dng_ref[...] = pn

        @pl.when(jnp.logical_not(first))
        def _():
            dng_ref[...] += pn

    blk = pl.BlockSpec((tb, BW), lambda j: (nb - 1 - j, 0))
    eblk = pl.BlockSpec((ncb * 8, BW), lambda j: (nb - 1 - j, 0))
    vec = pl.BlockSpec((1, 128), lambda j: (0, 0))
    return pl.pallas_call(
        body, name=name, grid=(nb,),
        in_specs=[blk] * 5 + [eblk, pl.BlockSpec((tb, BW), lambda j: (nb - 1 - j, COL_CZ // BW)), blk,
                              pl.BlockSpec((4, ncb, 128, 128), lambda j: (0, nb - 1 - j, 0, 0)), vec],
        out_specs=[blk] * 5 + [eblk, blk, vec],
        out_shape=[jax.ShapeDtypeStruct((T, BW), F32)] * 5 + [jax.ShapeDtypeStruct((T // 8, BW), F32),
                                                              jax.ShapeDtypeStruct((T, BW), BF16),
                                                              jax.ShapeDtypeStruct((1, 128), F32)],
        scratch_shapes=[pltpu.VMEM((4, 128, 128), F32)],
        compiler_params=_cp("arbitrary"),
    )(u, w, qe, kd, at, egl, proj, dy, sh, ng)


def _adamw_update(w_ref, g_ref, m_ref, v_ref, d_ref, nm_ref, nv_ref):
    gv = g_ref[...]
    m2 = ADAM_B1 * m_ref[...] + (1.0 - ADAM_B1) * gv
    v2 = ADAM_B2 * v_ref[...] + (1.0 - ADAM_B2) * jnp.square(gv)
    m_hat = m2 / (1.0 - ADAM_B1 ** ADAM_STEP)
    v_hat = v2 / (1.0 - ADAM_B2 ** ADAM_STEP)
    d_ref[...] = -ADAM_LR * (m_hat / (jnp.sqrt(v_hat) + ADAM_EPS) + ADAM_WD * w_ref[...])
    nm_ref[...] = m2
    nv_ref[...] = v2


def _adamw_many(name, ws, gs, ms, vs):
    n = len(ws)

    def body(*refs):
        for i in range(n):
            _adamw_update(*[refs[k * n + i] for k in range(7)])

    return pl.pallas_call(
        body, name=name,
        out_shape=[jax.ShapeDtypeStruct(a.shape, F32) for a in ws] * 3,
        compiler_params=_cp(),
    )(*ws, *gs, *ms, *vs)


def _adamw(name, w, g, m, v):
    R, C = w.shape
    br = _pick(R, (512, 256, 240, 128, 64, 8))
    body = functools.partial(_adamw_update)
    spec = pl.BlockSpec((br, C), lambda i: (i, 0))
    return pl.pallas_call(
        body, name=name, grid=(R // br,),
        in_specs=[spec] * 4, out_specs=[spec] * 3,
        out_shape=[jax.ShapeDtypeStruct((R, C), F32)] * 3,
        compiler_params=_cp("parallel"),
    )(w, g, m, v)


def _sum8(name, parts):
    _, R, C = parts.shape
    br = _pick(R, (352, 368, 256, 128, 64, 16, 8))

    def body(p_ref, o_ref):
        acc = p_ref[0].astype(F32)
        for d in range(1, N_DEV):
            acc = acc + p_ref[d].astype(F32)
        o_ref[...] = acc

    return pl.pallas_call(
        body, name=name, grid=(R // br,),
        in_specs=[pl.BlockSpec((N_DEV, br, C), lambda i: (0, i, 0))],
        out_specs=pl.BlockSpec((br, C), lambda i: (i, 0)),
        out_shape=jax.ShapeDtypeStruct((R, C), F32),
        compiler_params=_cp("parallel"),
    )(parts)


_ANY = pl.BlockSpec(memory_space=pl.ANY)
_MESH = pl.DeviceIdType.MESH


def _all_gather(name, shard):
    R, C = shard.shape

    def body(x_ref, out_ref, send_sems, recv_sems, local_sem):
        x, y, c = lax.axis_index("x"), lax.axis_index("y"), lax.axis_index("c")
        me, sibling = (x, y, c), (x, y, 1 - c)
        chips = [(1 - x, y), (x, 1 - y), (1 - x, 1 - y)]

        def slot(px, py, pc):
            return out_ref.at[4 * px + 2 * py + pc]

        def copy(k, block, to, src=None):
            return pltpu.make_async_remote_copy(
                src_ref=slot(*block) if src is None else src, dst_ref=slot(*block),
                send_sem=send_sems.at[k], recv_sem=recv_sems.at[k], device_id=to, device_id_type=_MESH)

        mine = pltpu.make_async_copy(x_ref, slot(*me), local_sem)
        mine.start()
        first = [copy(0, me, sibling, src=x_ref)]
        first += [copy(1 + j, me, (*chip, c), src=x_ref) for j, chip in enumerate(chips)]
        for cp in first:
            cp.start()
        passed = [copy(4 + j, (*chip, c), sibling) for j, chip in enumerate(chips)]
        for j, chip in enumerate(chips):
            copy(1 + j, (*chip, c), me).wait_recv()
            passed[j].start()
        copy(0, sibling, me).wait_recv()
        for j, chip in enumerate(chips):
            copy(4 + j, (*chip, 1 - c), me).wait_recv()
        for cp in first + passed:
            cp.wait_send()
        mine.wait()

    return pl.pallas_call(
        body, name=name,
        in_specs=[_ANY], out_specs=_ANY,
        out_shape=jax.ShapeDtypeStruct((N_DEV, R, C), shard.dtype),
        scratch_shapes=[pltpu.SemaphoreType.DMA((7,)), pltpu.SemaphoreType.DMA((7,)), pltpu.SemaphoreType.DMA],
    )(shard)


_HBM = pl.BlockSpec(memory_space=pltpu.HBM)
_SEM = pl.BlockSpec(memory_space=pltpu.SEMAPHORE)
_EFFECT = pltpu.SideEffectType.DATAFLOW_SIDE_EFFECTING


def _exchange_copies(src_ref, land_ref, send_sems, recv_sems, scatter):
    x, y, c = lax.axis_index("x"), lax.axis_index("y"), lax.axis_index("c")
    me = 4 * x + 2 * y + c
    copies = []
    for k in range(1, N_DEV):
        px, py, pc = x ^ ((k >> 2) & 1), y ^ ((k >> 1) & 1), c ^ (k & 1)
        src = src_ref.at[4 * px + 2 * py + pc] if scatter else src_ref
        copies.append(pltpu.make_async_remote_copy(
            src_ref=src, dst_ref=land_ref.at[me], send_sem=send_sems.at[k - 1], recv_sem=recv_sems.at[k - 1],
            device_id=(px, py, pc), device_id_type=_MESH))
    return copies


def _own_copy(src_ref, land_ref, send_sems, scatter):
    me = 4 * lax.axis_index("x") + 2 * lax.axis_index("y") + lax.axis_index("c")
    return pltpu.make_async_copy(src_ref.at[me] if scatter else src_ref, land_ref.at[me], send_sems.at[N_DEV - 1])


def _exchange_start(name, srcs, lands, scatter, after=None):
    n = len(srcs)

    def body(*refs):
        src_refs, land_refs = refs[:n], refs[n:2 * n]
        outs = refs[2 * n + (after is not None):]
        send, recv = outs[:n], outs[n:2 * n]
        token = refs[-1]
        for g in range(n):
            for cp in _exchange_copies(src_refs[g], land_refs[g], send[g], recv[g], scatter):
                cp.start()
            _own_copy(src_refs[g], land_refs[g], send[g], scatter).start()
        token[...] = jnp.zeros_like(token)

    outs = pl.pallas_call(
        body, name=name,
        out_shape=tuple([pltpu.SemaphoreType.DMA((N_DEV,))] * (2 * n)
                        + [pltpu.HBM(a.shape, a.dtype) for a in list(srcs) + list(lands)]
                        + [jax.ShapeDtypeStruct((8, 128), F32)]),
        in_specs=[_HBM] * (2 * n) + [_ANY] * (after is not None),
        out_specs=tuple([_SEM] * (2 * n) + [_HBM] * (2 * n) + [pl.BlockSpec(memory_space=pltpu.VMEM)]),
        input_output_aliases={i: 2 * n + i for i in range(2 * n)},
        compiler_params=pltpu.CompilerParams(has_side_effects=_EFFECT),
    )(*[pltpu.with_memory_space_constraint(a, pltpu.HBM) for a in list(srcs) + list(lands)],
      *([after] if after is not None else []))
    handles = [(outs[2 * n + g], outs[3 * n + g], outs[g], outs[n + g]) for g in range(n)]
    return handles, outs[-1]


def _exchange_wait(name, handles, after, scatter):
    n = len(handles)
    srcs, lands, sends, recvs = ([h[i] for h in handles] for i in range(4))

    def body(*refs):
        src_refs, land_refs = refs[:n], refs[n:2 * n]
        send, recv = refs[2 * n:3 * n], refs[3 * n:4 * n]
        for g in range(n):
            for cp in _exchange_copies(src_refs[g], land_refs[g], send[g], recv[g], scatter):
                cp.wait_send()
                cp.wait_recv()
            _own_copy(src_refs[g], land_refs[g], send[g], scatter).wait()

    outs = pl.pallas_call(
        body, name=name,
        out_shape=tuple(pltpu.HBM(a.shape, a.dtype) for a in srcs + lands),
        in_specs=tuple([_HBM] * (2 * n) + [_SEM] * (2 * n) + [_ANY]), out_specs=tuple([_HBM] * (2 * n)),
        input_output_aliases={i: i for i in range(2 * n)},
        compiler_params=pltpu.CompilerParams(has_side_effects=_EFFECT),
    )(*srcs, *lands, *sends, *recvs, after)
    return list(outs[n:])


def _rows(a):
    return a.reshape(-1, 1024)


def _rows_to_parts(full):
    n = full.shape[-2] // N_DEV
    t = full.reshape(full.shape[:-2] + (N_DEV, n, full.shape[-1]))
    return jnp.moveaxis(t, -3, 0)


def _parts_to_rows(parts):
    t = jnp.moveaxis(parts, 0, -3)
    return t.reshape(t.shape[:-3] + (t.shape[-3] * t.shape[-2], t.shape[-1]))


def _parts_to_cols(parts):
    t = jnp.moveaxis(parts, 0, -2)
    return t.reshape(t.shape[:-2] + (t.shape[-2] * t.shape[-1],))


def _join(parts, axis=0):
    total = sum(p.shape[axis] for p in parts)
    out, off = None, 0
    for p in parts:
        cfg = [(0, 0)] * p.ndim
        cfg[axis] = (off, total - off - p.shape[axis])
        t = jnp.pad(p, cfg)
        out = t if out is None else out + t
        off += p.shape[axis]
    return out


def _w_in_to_layout(w):
    tail = jnp.pad(w[4096:4104], ((0, PW - COL_TAIL - 8), (0, 0)))
    return jnp.concatenate([w[:4096], w[4104:P_IN], tail], axis=0)


def _w_in_from_layout(g):
    return _join([g[:4096], g[COL_TAIL:COL_TAIL + 8], g[4096:COL_TAIL]], axis=0)


def _block_diag(w):
    w = w.reshape(4, 2, 64, 64)
    return jnp.pad(w[:, 0], ((0, 0), (0, 64), (0, 64))) + jnp.pad(w[:, 1], ((0, 0), (64, 0), (64, 0)))


def _block_diag_grad(g):
    return jnp.stack([g[:, :64, :64], g[:, 64:, 64:]], axis=1).reshape(8, 64, 64)


def _ffn_forward(tag, x, norm, wg, wu, wd):
    h = _rms_fwd(tag + "_norm", x, norm)
    sa, ds, act = _ffn_up(tag + "_up", h, wg, wu)
    if callable(wd):
        wd = wd(act)
    x_out = _mm(tag + "_down", [(act, wd)], "nn", F32, res=x, scale=0.5)
    return x_out, (x, h, sa, ds, act), wd


def _ffn_backward(tag, dx_out, saved, norm, wg, wu, wd, put, names, split=False):
    x, h, sa, ds, act = saved
    n_wg, n_wu, n_wd = names
    dwd = _mm(tag + "_dwd", [(act, dx_out)], "tn", BF16, scale=0.5, bm=FF // 2)
    tok = put({n_wd: dwd}) if split else None
    da, db = _ffn_dact(tag + "_dact", dx_out, wd, sa, ds, after=tok)
    dwg = _mm(tag + "_dwg", [(da, h)], "tn", BF16, bm=FF // 2)
    if split:
        tok = tok + put({n_wg: dwg})
    dwu = _mm(tag + "_dwu", [(db, h)], "tn", BF16, bm=FF // 2, after=tok)
    tok = tok + put({n_wu: dwu}) if split else put({n_wg: dwg, n_wu: dwu, n_wd: dwd})
    dh = _mm(tag + "_dh", [(da, wg), (db, wu)], "nn", F32, after=tok)
    dx, dnorm = _rms_bwd(tag + "_dnorm", x, norm + tok, dh, dx_out)
    return dx, dnorm


def _mixer_params(p):
    alog = jnp.pad(p["gdn_a_log"], (4, 120))[None]
    dtb = jnp.pad(p["gdn_dt_bias"], (4, 120))[None]
    bias = jnp.repeat(p["sgu_b"].T, 128, axis=1)
    return dict(
        ln_g=p["sgu_ln_g"][None], ln_b=p["sgu_ln_b"][None], sgu_w=p["sgu_w"], sgu_bias=bias,
        lru_cw=p["lru_conv_w"], lru_cb=p["lru_conv_b"][None], wa=_block_diag(p["lru_wa"]), ba=p["lru_ba"][None],
        wx=_block_diag(p["lru_wx"]), bx=p["lru_bx"][None], lam=p["lru_lambda"][None],
        gdn_cw=p["gdn_conv_w"], alog=alog, dtb=dtb, ng=p["gdn_norm_g"][None],
        pool_w=p["pool_w"], pool_sc=p["pool_scale"][None])


def _mix_forward(tag, x, p, mp):
    h = _rms_fwd(tag + "_norm", x, p["mix_norm"][None])
    proj = _mm(tag + "_proj", [(h, p["w_in"])], "nt", F32, bm=_pick(x.shape[0], (2048, 1024, 512, 256, 128)))
    y_a = _sgu_fwd(tag + "_sgu", proj, mp["ln_g"], mp["ln_b"], mp["sgu_w"], mp["sgu_bias"])
    y_b, hc = _lru_fwd(tag + "_lru", proj, mp["lru_cw"], mp["lru_cb"], mp["wa"], mp["ba"], mp["wx"], mp["bx"],
                       mp["lam"])
    qa = _conv_fwd(tag + "_convq", proj, COL_CQ, mp["gdn_cw"], 0)
    ka = _conv_fwd(tag + "_convk", proj, COL_CK, mp["gdn_cw"], 512)
    va = _conv_fwd(tag + "_convv", proj, COL_CV, mp["gdn_cw"], 1024)
    prep = _gdn_prep_fwd(tag + "_gdnprep", qa, ka, va, proj, mp["alog"], mp["dtb"])
    y_c, sh = _gdn_fwd(tag + "_gdn", *prep, proj, mp["ng"])
    y_d = _pool_fwd(tag + "_pool", proj, mp["pool_w"], mp["pool_sc"])
    ys = (y_a, y_b, y_c, y_d)
    if callable(p["w_branch"]):
        p["w_branch"] = p["w_branch"](y_d)
    merged = _merge_fwd(tag + "_merge", ys, p["w_branch"], proj)
    if callable(p["w_out"]):
        p["w_out"] = p["w_out"](merged)
    x_out = _mm(tag + "_out", [(merged, p["w_out"])], "nn", F32, res=x)
    return x_out, (x, h, proj, hc, qa, ka, va, prep, sh, ys, merged)


def _mix_backward(tag, dx_out, saved, p, mp, put):
    x, h, proj, hc, qa, ka, va, prep, sh, ys, merged = saved
    T = x.shape[0]
    g = {}
    dmerged = _mm(tag + "_dmerged", [(dx_out, p["w_out"])], "nt", F32)
    g["w_out"] = _mm(tag + "_dwout", [(merged, dx_out)], "tn", BF16)
    outs = _merge_bwd(tag + "_dmerge", dmerged, ys, p["w_branch"], proj)
    dgates, dbrs, dys = outs[:NBR], outs[NBR:2 * NBR], outs[2 * NBR:]
    g["w_branch"] = jnp.stack([_mm(f"{tag}_dwb{i}", [(dbrs[i], ys[i])], "tn", BF16) for i in range(NBR)])

    du, dv, dln_g, dln_b, dsgu_w, dbias = _sgu_bwd(tag + "_dsgu", proj, dys[0], mp["ln_g"], mp["ln_b"], mp["sgu_w"],
                                                  mp["sgu_bias"])
    g["sgu_ln_g"], g["sgu_ln_b"], g["sgu_w"] = dln_g[0], dln_b[0], dsgu_w
    g["sgu_b"] = dbias.reshape(128, 4, 128).sum(axis=2).T

    (dbx, dbg, dcw, dcb, dwa, dba, dwx, dbxb, dlam) = _lru_bwd(
        tag + "_dlru", proj, dys[1], hc, mp["lru_cw"], mp["lru_cb"], mp["wa"], mp["ba"], mp["wx"], mp["bx"], mp["lam"])
    g["lru_conv_w"], g["lru_conv_b"], g["lru_ba"], g["lru_bx"], g["lru_lambda"] = dcw, dcb[0], dba[0], dbxb[0], dlam[0]
    g["lru_wa"], g["lru_wx"] = _block_diag_grad(dwa), _block_diag_grad(dwx)

    *dprep, dz, dng = _gdn_bwd(tag + "_dgdn", *prep, proj, dys[2], sh, mp["ng"])
    dqa, dka, dva, dtail, dalog, ddtb = _gdn_prep_bwd(tag + "_dgdnprep", qa, ka, va, proj, mp["alog"], mp["dtb"], *dprep)
    g["gdn_a_log"], g["gdn_dt_bias"], g["gdn_norm_g"] = dalog[0, 4:8], ddtb[0, 4:8], dng[0]
    dq, dcwq = _conv_bwd(tag + "_dconvq", proj, COL_CQ, dqa, mp["gdn_cw"], 0)
    dk, dcwk = _conv_bwd(tag + "_dconvk", proj, COL_CK, dka, mp["gdn_cw"], 512)
    dv_, dcwv = _conv_bwd(tag + "_dconvv", proj, COL_CV, dva, mp["gdn_cw"], 1024)
    g["gdn_conv_w"] = jnp.concatenate([dcwq, dcwk, dcwv], axis=1)

    dd, dpw, dsc = _pool_bwd(tag + "_dpool", proj, dys[3], mp["pool_w"], mp["pool_sc"])
    g["pool_w"], g["pool_scale"] = dpw, dsc[0]

    dproj = jnp.concatenate([du, dv, dbx, dbg, dq, dk, dv_, dz, dd, *dgates, dtail,
                             jnp.zeros((T, PW - COL_TAIL - 128), BF16)], axis=1)
    dw_in = _mm(tag + "_dwin", [(dproj, h)], "tn", BF16)
    tok = put(dict(w_in=_w_in_from_layout(dw_in), w_branch=g.pop("w_branch"), w_out=g.pop("w_out")))
    dh = _mm(tag + "_dh", [(dproj, p["w_in"])], "nn", F32, bm=_pick(T, (2048, 1024, 512, 256, 128)), after=tok)
    dx, dnorm = _rms_bwd(tag + "_dnorm", x, p["mix_norm"][None] + tok, dh, dx_out)
    g["mix_norm"] = dnorm[0]
    return dx, g


_BIG = ("ff1_wg", "ff1_wu", "ff1_wd", "w_in", "w_branch", "w_out", "ff2_wg", "ff2_wu", "ff2_wd")
_COL_SHARDED = ("ff1_wg", "ff1_wu", "w_in", "w_branch", "ff2_wg", "ff2_wu")
_SMALL = ("ff1_norm", "mix_norm", "sgu_ln_g", "sgu_ln_b", "sgu_w", "sgu_b", "lru_conv_w", "lru_conv_b", "lru_wa",
          "lru_ba", "lru_wx", "lru_bx", "lru_lambda", "gdn_conv_w", "gdn_a_log", "gdn_dt_bias", "gdn_norm_g", "pool_w",
          "pool_scale", "ff2_norm", "final_norm")
_WEIGHTS = ("ff1_norm", "ff1_wg", "ff1_wu", "ff1_wd", "mix_norm", "w_in", "sgu_ln_g", "sgu_ln_b", "sgu_w", "sgu_b",
            "lru_conv_w", "lru_conv_b", "lru_wa", "lru_ba", "lru_wx", "lru_bx", "lru_lambda", "gdn_conv_w", "gdn_a_log",
            "gdn_dt_bias", "gdn_norm_g", "pool_w", "pool_scale", "w_branch", "w_out", "ff2_norm", "ff2_wg", "ff2_wu",
            "ff2_wd", "final_norm")
_CONV_SHARDED = ("lru_conv_w", "gdn_conv_w")
PACK_ROW_ALIGN = 16
_GROUPS = (("ff1", ("ff1_wg", "ff1_wu", "ff1_wd")), ("mix", ("w_in", "w_branch", "w_out")),
           ("ff2", ("ff2_wg", "ff2_wu", "ff2_wd")))


def _pad_rows(a, mult):
    pad = (-a.shape[-2]) % mult
    if pad == 0:
        return a
    return jnp.pad(a, [(0, 0)] * (a.ndim - 2) + [(0, pad), (0, 0)])


def _my_index():
    return 4 * lax.axis_index("x") + 2 * lax.axis_index("y") + lax.axis_index("c")


def _landing(shape, dtype):
    return lax.empty((N_DEV,) + tuple(shape), dtype)


def _stored(n, a):
    return jnp.swapaxes(a, -1, -2) if n in _COL_SHARDED else a


_FIRST = ("ff1_wg", "ff1_wu", "ff1_wd")


def _gather_first(w):
    names = _FIRST
    shards = [_rows(_stored(n, w[n][0]).astype(BF16)) for n in names]
    got = _all_gather("gather_first", jnp.concatenate(shards, axis=0))
    out, r = {}, 0
    for n, s in zip(names, shards):
        out[n] = got[:, r:r + s.shape[0]].reshape(-1, 1024)
        r += s.shape[0]
    return out, got


def _gather_start(w, after):
    conv = _pad_rows(jnp.concatenate([w[n].reshape(1, -1) for n in _CONV_SHARDED], axis=1), 8)
    keys, srcs = ["conv"], [conv]
    for l in range(2):
        for sub, (_, names) in enumerate(_GROUPS):
            for n in names:
                if l > 0 or n not in _FIRST:
                    keys.append((l, sub, n))
                    srcs.append(_stored(n, w[n][l]).astype(BF16))
    lands = [_landing(s.shape, s.dtype) for s in srcs]
    handles, token = _exchange_start("gather_start", srcs, lands, scatter=False, after=after)
    return dict(zip(keys, handles)), token


def _gather_finish(l, sub, handles, first, after):
    names = _GROUPS[sub][1]
    if (l, sub) == (0, 0):
        out = dict(first)
        for n in names:
            if n not in _FIRST:
                out[n] = lambda later, n=n: _parts_to_rows(
                    _exchange_wait(f"gather_wait_00_{n}", [handles[(0, 0, n)]], later, scatter=False)[0])
    elif sub == 1:
        out = {n: (lambda later, n=n: _parts_to_rows(
            _exchange_wait(f"gather_wait_{l}{sub}_{n}", [handles[(l, sub, n)]], later, scatter=False)[0])) for n in names}
        out["w_in"] = out["w_in"](after)
    else:
        lands = _exchange_wait(f"gather_wait_{l}{sub}", [handles[(l, sub, n)] for n in names], after, scatter=False)
        out = {n: _parts_to_rows(land) for n, land in zip(names, lands)}
    if "w_in" in out:
        out["w_in"] = _w_in_to_layout(out["w_in"])
    return out


def _scatter_start(l, sub, grads):
    srcs, shapes = [], []
    for n in grads:
        parts = _rows_to_parts(grads[n])
        shapes.append(parts.shape[1:])
        srcs.append(_pad_rows(parts.reshape(N_DEV, -1, 1024), PACK_ROW_ALIGN))
    lands = [_landing(s.shape[1:], s.dtype) for s in srcs]
    tag = f"{l}{sub}" + ("" if len(grads) == len(_GROUPS[sub][1]) else "_" + "_".join(grads))
    handles, token = _exchange_start(f"scatter_start_{tag}", srcs, lands, scatter=True)
    return handles, (tag, tuple(grads), shapes), token


def _scatter_finish(l, sub, handles, meta, after):
    tag, names, shapes = meta
    lands = _exchange_wait(f"scatter_wait_{tag}", handles, after, scatter=True)
    out = {}
    for n, land, shape in zip(names, lands, shapes):
        size = 1
        for s in shape:
            size *= s
        summed = _sum8(f"sum_{l}{sub}_{n}", land)
        out[n] = _stored(n, summed[:size // 1024].reshape(shape))
    return out


def _gather_conv_finish(w, handles, after):
    gconv = _exchange_wait("gather_wait_conv", [handles["conv"]], after, scatter=False)[0][:, 0]
    full, r = {}, 0
    for n in _CONV_SHARDED:
        sz = w[n].size
        full[n] = _parts_to_cols(gconv[:, r:r + sz].reshape((N_DEV,) + w[n].shape))
        r += sz
    return full


def _forward_backward(x, tgt, w, conv, get_weights, put_grads, put_small, token):
    saved, params = [], []
    for l in range(2):
        p = {n: w[n][l] for n in _SMALL if n != "final_norm"}
        for n in _CONV_SHARDED:
            p[n] = conv[n][l]
        mp = _mixer_params(p)
        tok = token[:1, :1] if l == 0 else 0.0
        p.update(get_weights(l, 0, x))
        x, s1, p["ff1_wd"] = _ffn_forward(f"l{l}_ff1", x, p["ff1_norm"][None] + tok, p["ff1_wg"], p["ff1_wu"],
                                          p["ff1_wd"])
        p.update(get_weights(l, 1, x))
        x, s2 = _mix_forward(f"l{l}_mix", x, p, mp)
        p.update(get_weights(l, 2, x))
        x, s3, _ = _ffn_forward(f"l{l}_ff2", x, p["ff2_norm"][None], p["ff2_wg"], p["ff2_wu"], p["ff2_wd"])
        saved.append((s1, s2, s3))
        params.append((p, mp))
    loss, dx, dfinal = _final_loss("loss_head", x, w["final_norm"][None], tgt)
    tok = 0.0
    for l in (1, 0):
        p, mp = params[l]
        s1, s2, s3 = saved[l]
        g = {}

        def put(sub):
            return lambda grads, l=l: put_grads(l, sub, grads)[:1, :1]

        dx, dn = _ffn_backward(f"l{l}_ff2", dx, s3, p["ff2_norm"][None] + tok, p["ff2_wg"], p["ff2_wu"], p["ff2_wd"],
                               put(2), _GROUPS[2][1])
        g["ff2_norm"] = dn[0]
        dx, gm = _mix_backward(f"l{l}_mix", dx, s2, p, mp, put(1))
        g.update(gm)
        tok = 0.0
        if l == 0:
            tok = put_small("0a", g)[:1, :1]
            g = {}
        dx, dn = _ffn_backward(f"l{l}_ff1", dx, s1, p["ff1_norm"][None] + tok, p["ff1_wg"], p["ff1_wu"], p["ff1_wd"],
                               put(0), _GROUPS[0][1], split=(l == 0))
        g["ff1_norm"] = dn[0]
        if l == 1:
            g["final_norm"] = dfinal[0]
            g["loss"] = loss[0, :1]
        tok = put_small("1" if l == 1 else "0b", g)[:1, :1]
    return dx


SMALL_PIECE = 8 * 1024


def _pack_small(d, names):
    pieces = []
    for n in names:
        flat = d[n].reshape(-1)
        pieces.append(jnp.pad(flat, (0, (-flat.size) % SMALL_PIECE)).reshape(-1, 1024))
    return jnp.concatenate(pieces, axis=0)


def _unpack_small(pack, shapes, names):
    out, r = {}, 0
    for n in names:
        size = 1
        for s in shapes[n]:
            size *= s
        rows = -(-size // SMALL_PIECE) * 8
        out[n] = pack[r:r + rows].reshape(-1)[:size].reshape(shapes[n])
        r += rows
    return out


def _small_names(grads):
    return tuple(n for n in _SMALL + ("loss",) if n in grads)


def _small_start(tag, grads):
    pack = _pack_small(grads, _small_names(grads))
    handles, token = _exchange_start(f"small_start_{tag}", [pack], [_landing(pack.shape, pack.dtype)], scatter=False)
    return handles, {n: grads[n].shape for n in _small_names(grads)}, token


def _small_finish(tag, handles, shapes, after):
    landed = _exchange_wait(f"small_wait_{tag}", handles, after, scatter=False)[0]
    return _unpack_small(_sum8(f"sum_small_{tag}", landed), shapes, _small_names(shapes))


def _as2d(a):
    if a.ndim == 1:
        return a.reshape(1, -1)
    return a.reshape(-1, a.shape[-1])


def kernel(x, ff1_norm, ff1_wg, ff1_wu, ff1_wd, mix_norm, w_in, sgu_ln_g, sgu_ln_b, sgu_w, sgu_b, lru_conv_w, lru_conv_b, lru_wa, lru_ba, lru_wx, lru_bx, lru_lambda, gdn_conv_w, gdn_a_log, gdn_dt_bias, gdn_norm_g, pool_w, pool_scale, w_branch, w_out, ff2_norm, ff2_wg, ff2_wu, ff2_wd, final_norm, loss_target, m_ff1_norm, m_ff1_wg, m_ff1_wu, m_ff1_wd, m_mix_norm, m_w_in, m_sgu_ln_g, m_sgu_ln_b, m_sgu_w, m_sgu_b, m_lru_conv_w, m_lru_conv_b, m_lru_wa, m_lru_ba, m_lru_wx, m_lru_bx, m_lru_lambda, m_gdn_conv_w, m_gdn_a_log, m_gdn_dt_bias, m_gdn_norm_g, m_pool_w, m_pool_scale, m_w_branch, m_w_out, m_ff2_norm, m_ff2_wg, m_ff2_wu, m_ff2_wd, m_final_norm, v_ff1_norm, v_ff1_wg, v_ff1_wu, v_ff1_wd, v_mix_norm, v_w_in, v_sgu_ln_g, v_sgu_ln_b, v_sgu_w, v_sgu_b, v_lru_conv_w, v_lru_conv_b, v_lru_wa, v_lru_ba, v_lru_wx, v_lru_bx, v_lru_lambda, v_gdn_conv_w, v_gdn_a_log, v_gdn_dt_bias, v_gdn_norm_g, v_pool_w, v_pool_scale, v_w_branch, v_w_out, v_ff2_norm, v_ff2_wg, v_ff2_wu, v_ff2_wd, v_final_norm):
    w = dict(ff1_norm=ff1_norm, ff1_wg=ff1_wg, ff1_wu=ff1_wu, ff1_wd=ff1_wd, mix_norm=mix_norm, w_in=w_in,
             sgu_ln_g=sgu_ln_g, sgu_ln_b=sgu_ln_b, sgu_w=sgu_w, sgu_b=sgu_b, lru_conv_w=lru_conv_w,
             lru_conv_b=lru_conv_b, lru_wa=lru_wa, lru_ba=lru_ba, lru_wx=lru_wx, lru_bx=lru_bx, lru_lambda=lru_lambda,
             gdn_conv_w=gdn_conv_w, gdn_a_log=gdn_a_log, gdn_dt_bias=gdn_dt_bias, gdn_norm_g=gdn_norm_g, pool_w=pool_w,
             pool_scale=pool_scale, w_branch=w_branch, w_out=w_out, ff2_norm=ff2_norm, ff2_wg=ff2_wg, ff2_wu=ff2_wu,
             ff2_wd=ff2_wd, final_norm=final_norm)
    m = dict(ff1_norm=m_ff1_norm, ff1_wg=m_ff1_wg, ff1_wu=m_ff1_wu, ff1_wd=m_ff1_wd, mix_norm=m_mix_norm, w_in=m_w_in,
             sgu_ln_g=m_sgu_ln_g, sgu_ln_b=m_sgu_ln_b, sgu_w=m_sgu_w, sgu_b=m_sgu_b, lru_conv_w=m_lru_conv_w,
             lru_conv_b=m_lru_conv_b, lru_wa=m_lru_wa, lru_ba=m_lru_ba, lru_wx=m_lru_wx, lru_bx=m_lru_bx,
             lru_lambda=m_lru_lambda, gdn_conv_w=m_gdn_conv_w, gdn_a_log=m_gdn_a_log, gdn_dt_bias=m_gdn_dt_bias,
             gdn_norm_g=m_gdn_norm_g, pool_w=m_pool_w, pool_scale=m_pool_scale, w_branch=m_w_branch, w_out=m_w_out,
             ff2_norm=m_ff2_norm, ff2_wg=m_ff2_wg, ff2_wu=m_ff2_wu, ff2_wd=m_ff2_wd, final_norm=m_final_norm)
    v = dict(ff1_norm=v_ff1_norm, ff1_wg=v_ff1_wg, ff1_wu=v_ff1_wu, ff1_wd=v_ff1_wd, mix_norm=v_mix_norm, w_in=v_w_in,
             sgu_ln_g=v_sgu_ln_g, sgu_ln_b=v_sgu_ln_b, sgu_w=v_sgu_w, sgu_b=v_sgu_b, lru_conv_w=v_lru_conv_w,
             lru_conv_b=v_lru_conv_b, lru_wa=v_lru_wa, lru_ba=v_lru_ba, lru_wx=v_lru_wx, lru_bx=v_lru_bx,
             lru_lambda=v_lru_lambda, gdn_conv_w=v_gdn_conv_w, gdn_a_log=v_gdn_a_log, gdn_dt_bias=v_gdn_dt_bias,
             gdn_norm_g=v_gdn_norm_g, pool_w=v_pool_w, pool_scale=v_pool_scale, w_branch=v_w_branch, w_out=v_w_out,
             ff2_norm=v_ff2_norm, ff2_wg=v_ff2_wg, ff2_wu=v_ff2_wu, ff2_wd=v_ff2_wd, final_norm=v_final_norm)

    first, got_first = _gather_first(w)
    handles, token = _gather_start(w, got_first)
    conv = _gather_conv_finish(w, handles, token)
    pending = {}

    def get_weights(l, sub, after):
        return _gather_finish(l, sub, handles, first, after)

    def put_grads(l, sub, grads):
        hs, meta, tok = _scatter_start(l, sub, grads)
        pending[(l, sub, meta[0])] = (hs, meta)
        return tok

    def put_small(tag, grads):
        hs, shapes, tok = _small_start(tag, grads)
        pending[tag] = (hs, shapes)
        return tok

    T = x.shape[1]
    dx = _forward_backward(x.reshape(T, D), loss_target.reshape(T, D), w, conv, get_weights, put_grads, put_small,
                           token)
    per = {}
    for key in pending:
        if isinstance(key, tuple):
            per.setdefault(key[:2], {}).update(_scatter_finish(*key[:2], *pending[key], dx))
        else:
            per[key] = _small_finish(key, *pending[key], dx)
    grad = {n: jnp.stack([per[(0, sub)][n], per[(1, sub)][n]]) for sub, (_, names) in enumerate(_GROUPS) for n in names}
    layer0 = {**per["0a"], **per["0b"]}
    small = {n: _join([layer0[n].reshape(-1), per["1"][n].reshape(-1)]).reshape((2,) + layer0[n].shape)
             for n in layer0}
    small["final_norm"] = per["1"]["final_norm"]
    loss = per["1"]["loss"][0]
    me = _my_index()
    for n in _SMALL:
        if n in _CONV_SHARDED:
            width = w[n].shape[-1]
            grad[n] = lax.dynamic_slice_in_dim(small[n], me * width, width, axis=2)
        else:
            grad[n] = small[n]

    delta, new_m, new_v = {}, {}, {}
    for n in _BIG:
        d_, m_, v_ = _adamw("adamw_" + n, _as2d(w[n]), _as2d(grad[n]), _as2d(m[n]), _as2d(v[n]))
        delta[n], new_m[n], new_v[n] = (t.reshape(w[n].shape) for t in (d_, m_, v_))

    outs = _adamw_many("adamw_small", *[[_as2d(t[n]) for n in _SMALL] for t in (w, grad, m, v)])
    for k, dst in enumerate((delta, new_m, new_v)):
        for i, n in enumerate(_SMALL):
            dst[n] = outs[k * len(_SMALL) + i].reshape(w[n].shape)

    return (loss, dx.reshape(x.shape), *[grad[n] for n in _WEIGHTS], *[delta[n] for n in _WEIGHTS],
            *[new_m[n] for n in _WEIGHTS], *[new_v[n] for n in _WEIGHTS])
```

```python
import functools

import jax
import jax.numpy as jnp
from jax import lax
from jax.experimental import pallas as pl
from jax.experimental.pallas import tpu as pltpu

F32 = jnp.float32
BF16 = jnp.bfloat16
HI = lax.Precision.HIGHEST

N_DEV = 8
D = 1024
FF = 2816
BW = 512
NBR = 4
CHUNK = 64
EPS = 1e-6
LRU_C = 8.0
GDN_DK = 128

COL_AU, COL_AV, COL_BX, COL_BG = 0, 512, 1024, 1536
COL_CQ, COL_CK, COL_CV, COL_CZ = 2048, 2560, 3072, 3584
COL_DX, COL_GATE, COL_TAIL = 4096, 4608, 8704
PW = 9216
P_IN = 8712

ADAM_LR, ADAM_B1, ADAM_B2, ADAM_EPS, ADAM_WD, ADAM_STEP = 0.001, 0.9, 0.999, 1e-08, 0.01, 10

VMEM_LIMIT_V7X = 56 * 1024 * 1024

_NN = (((1,), (0,)), ((), ()))
_NT = (((1,), (1,)), ((), ()))
_TN = (((0,), (0,)), ((), ()))


def _cp(*sem):
    return pltpu.CompilerParams(dimension_semantics=tuple(sem), vmem_limit_bytes=VMEM_LIMIT_V7X)


def _dot(a, b, dims=_NN):
    return lax.dot_general(a.astype(BF16), b.astype(BF16), dims, preferred_element_type=F32)


def _dot_hi(a, b, dims=_NN):
    return lax.dot_general(a, b, dims, precision=HI, preferred_element_type=F32)


def _pick(n, cands):
    for c in cands:
        if n % c == 0:
            return c
    return n


@jax.custom_jvp
def _log1p(x):
    u = 1.0 + x
    return jnp.where(u == 1.0, x, x * jnp.log(u) / jnp.where(u == 1.0, 1.0, u - 1.0))


@_log1p.defjvp
def _log1p_jvp(p, t):
    (x,), (dx,) = p, t
    return _log1p(x), dx / (1.0 + x)


@jax.custom_jvp
def _expm1(x):
    u = jnp.exp(x)
    lu = jnp.log(u)
    small = (u == 1.0) | (lu == 0.0)
    return jnp.where(small, x, (u - 1.0) * x / jnp.where(small, 1.0, lu))


@_expm1.defjvp
def _expm1_jvp(p, t):
    (x,), (dx,) = p, t
    return _expm1(x), dx * jnp.exp(x)


def _softplus(x):
    return jnp.maximum(x, 0.0) + _log1p(jnp.exp(-jnp.abs(x)))


def _sigmoid(x):
    return jax.nn.sigmoid(x)


def _silu(x):
    return x * jax.nn.sigmoid(x)


def _gelu(x):
    return jax.nn.gelu(x)


@functools.partial(jax.custom_vjp, nondiff_argnums=(1,))
def _shift(x, s):
    return x if s == 0 else pltpu.roll(x, s, 0)


def _shift_fwd(x, s):
    return _shift(x, s), None


def _shift_bwd(s, _, g):
    n = g.shape[0]
    return (g if s == 0 else pltpu.roll(g, n - s, 0),)


_shift.defvjp(_shift_fwd, _shift_bwd)


def _scan_steps(a, b, reverse):
    n = a.shape[0]
    row = lax.broadcasted_iota(jnp.int32, a.shape, 0)
    k = 1
    while k < n:
        sh = n - k if reverse else k
        m = (row < n - k) if reverse else (row >= k)
        a_s = jnp.where(m, pltpu.roll(a, sh, 0), 1.0)
        b_s = jnp.where(m, pltpu.roll(b, sh, 0), 0.0)
        b = a * b_s + b
        a = a * a_s
        k *= 2
    return b


@jax.custom_vjp
def _scan(a, b):
    return _scan_steps(a, b, False)


def _scan_fwd(a, b):
    h = _scan_steps(a, b, False)
    return h, (a, h)


def _scan_bwd(res, dh):
    a, h = res
    n = a.shape[0]
    row = lax.broadcasted_iota(jnp.int32, a.shape, 0)
    a_next = jnp.where(row < n - 1, pltpu.roll(a, n - 1, 0), 0.0)
    g = _scan_steps(a_next, dh, True)
    h_prev = jnp.where(row >= 1, pltpu.roll(h, 1, 0), 0.0)
    return g * h_prev, g


_scan.defvjp(_scan_fwd, _scan_bwd)


def _mm(name, pairs, mode, out_dtype, *, res=None, scale=1.0, bm=None, bn=None, bk=None, after=None):
    a0, b0 = pairs[0]
    if mode == "nn":
        (M, K), N = a0.shape, b0.shape[1]
    elif mode == "nt":
        (M, K), N = a0.shape, b0.shape[0]
    else:
        (K, M), N = a0.shape, b0.shape[1]
    bm = bm or _pick(M, (1024, 512, 256, 128))
    bn = bn or _pick(N, (1024, 512, 256, 128))
    bk = bk or _pick(K, (1024, 512, 1408, 256, 128))
    nk = K // bk
    npair = len(pairs)
    dims = {"nn": _NN, "nt": _NT, "tn": _TN}[mode]

    def body(*refs):
        ab = refs[:2 * npair]
        pos = 2 * npair
        r_ref = None
        if res is not None:
            r_ref = refs[pos]
            pos += 1
        pos += after is not None
        o_ref = refs[pos]
        part = None
        for p in range(npair):
            d = _dot(ab[2 * p][...], ab[2 * p + 1][...], dims)
            part = d if part is None else part + d

        def finish(acc):
            out = acc if scale == 1.0 else acc * scale
            if r_ref is not None:
                out = out + r_ref[...]
            o_ref[...] = out.astype(out_dtype)

        if nk == 1:
            finish(part)
        else:
            acc_ref = refs[pos + 1]
            k = pl.program_id(2)

            @pl.when(k == 0)
            def _():
                acc_ref[...] = part

            @pl.when(k > 0)
            def _():
                acc_ref[...] += part

            @pl.when(k == nk - 1)
            def _():
                finish(acc_ref[...])

    if mode == "nn":
        a_spec = pl.BlockSpec((bm, bk), lambda i, j, k: (i, k))
        b_spec = pl.BlockSpec((bk, bn), lambda i, j, k: (k, j))
    elif mode == "nt":
        a_spec = pl.BlockSpec((bm, bk), lambda i, j, k: (i, k))
        b_spec = pl.BlockSpec((bn, bk), lambda i, j, k: (j, k))
    else:
        a_spec = pl.BlockSpec((bk, bm), lambda i, j, k: (k, i))
        b_spec = pl.BlockSpec((bk, bn), lambda i, j, k: (k, j))
    o_spec = pl.BlockSpec((bm, bn), lambda i, j, k: (i, j))
    in_specs, args = [], []
    for a, b in pairs:
        in_specs += [a_spec, b_spec]
        args += [a, b]
    if res is not None:
        in_specs.append(o_spec)
        args.append(res)
    if after is not None:
        in_specs.append(_ANY)
        args.append(after)
    return pl.pallas_call(
        body, name=name, grid=(M // bm, N // bn, nk),
        in_specs=in_specs, out_specs=o_spec,
        out_shape=jax.ShapeDtypeStruct((M, N), out_dtype),
        scratch_shapes=[pltpu.VMEM((bm, bn), F32)] if nk > 1 else [],
        compiler_params=_cp("parallel", "parallel", "arbitrary"),
    )(*args)


def _rms_fwd(name, x, g):
    T = x.shape[0]
    bm = _pick(T, (512, 256, 128))

    def body(x_ref, g_ref, o_ref):
        xv = x_ref[...]
        r = lax.rsqrt(jnp.mean(xv * xv, axis=-1, keepdims=True) + EPS)
        o_ref[...] = (xv * r * g_ref[...]).astype(BF16)

    return pl.pallas_call(
        body, name=name, grid=(T // bm,),
        in_specs=[pl.BlockSpec((bm, D), lambda i: (i, 0)), pl.BlockSpec((1, D), lambda i: (0, 0))],
        out_specs=pl.BlockSpec((bm, D), lambda i: (i, 0)),
        out_shape=jax.ShapeDtypeStruct((T, D), BF16),
        compiler_params=_cp("parallel"),
    )(x, g)


def _rms_bwd(name, x, g, dh, dres):
    T = x.shape[0]
    bm = _pick(T, (512, 256, 128))

    def body(x_ref, g_ref, dh_ref, dres_ref, dx_ref, dg_ref):
        xv = x_ref[...]
        r = lax.rsqrt(jnp.mean(xv * xv, axis=-1, keepdims=True) + EPS)
        xh = xv * r
        dhv = dh_ref[...]
        dxh = dhv * g_ref[...]
        dx_ref[...] = dres_ref[...] + r * (dxh - xh * jnp.mean(dxh * xh, axis=-1, keepdims=True))
        part = jnp.sum(dhv * xh, axis=0, keepdims=True)

        @pl.when(pl.program_id(0) == 0)
        def _():
            dg_ref[...] = part

        @pl.when(pl.program_id(0) > 0)
        def _():
            dg_ref[...] += part

    row = pl.BlockSpec((bm, D), lambda i: (i, 0))
    vec = pl.BlockSpec((1, D), lambda i: (0, 0))
    return pl.pallas_call(
        body, name=name, grid=(T // bm,),
        in_specs=[row, vec, row, row], out_specs=[row, vec],
        out_shape=[jax.ShapeDtypeStruct((T, D), F32), jax.ShapeDtypeStruct((1, D), F32)],
        compiler_params=_cp("arbitrary"),
    )(x, g, dh, dres)


def _final_loss(name, x, g, tgt):
    T = x.shape[0]
    bm = _pick(T, (512, 256, 128))

    def body(x_ref, g_ref, t_ref, loss_ref, dx_ref, dg_ref):
        xv = x_ref[...]
        gv = g_ref[...]
        r = lax.rsqrt(jnp.mean(xv * xv, axis=-1, keepdims=True) + EPS)
        xh = xv * r
        e = xh * gv - t_ref[...]
        lpart = jnp.broadcast_to(0.5 * jnp.sum(jnp.mean(e * e, axis=-1, keepdims=True), axis=0, keepdims=True), (1, 128))
        dy = e * (1.0 / D)
        dxh = dy * gv
        dx_ref[...] = r * (dxh - xh * jnp.mean(dxh * xh, axis=-1, keepdims=True))
        gpart = jnp.sum(dy * xh, axis=0, keepdims=True)

        @pl.when(pl.program_id(0) == 0)
        def _():
            loss_ref[...] = lpart
            dg_ref[...] = gpart

        @pl.when(pl.program_id(0) > 0)
        def _():
            loss_ref[...] += lpart
            dg_ref[...] += gpart

    row = pl.BlockSpec((bm, D), lambda i: (i, 0))
    vec = pl.BlockSpec((1, D), lambda i: (0, 0))
    return pl.pallas_call(
        body, name=name, grid=(T // bm,),
        in_specs=[row, vec, row],
        out_specs=[pl.BlockSpec((1, 128), lambda i: (0, 0)), row, vec],
        out_shape=[jax.ShapeDtypeStruct((1, 128), F32), jax.ShapeDtypeStruct((T, D), F32),
                   jax.ShapeDtypeStruct((1, D), F32)],
        compiler_params=_cp("arbitrary"),
    )(x, g, tgt)


def _ffn_up(name, h, wg, wu):
    T = h.shape[0]
    bm = _pick(T, (2048, 1024, 512, 256, 128))
    bn = 256

    def body(h_ref, wg_ref, wu_ref, sa_ref, ds_ref, act_ref):
        hv = h_ref[...]
        a = _dot(hv, wg_ref[...], _NT)
        b = _dot(hv, wu_ref[...], _NT)
        s = _sigmoid(a)
        sa = a * s
        sa_ref[...] = sa.astype(BF16)
        ds_ref[...] = (b * (s * (1.0 + a * (1.0 - s)))).astype(BF16)
        act_ref[...] = (sa * b).astype(BF16)

    w_spec = pl.BlockSpec((bn, D), lambda i, j: (j, 0))
    o_spec = pl.BlockSpec((bm, bn), lambda i, j: (i, j))
    return pl.pallas_call(
        body, name=name, grid=(T // bm, FF // bn),
        in_specs=[pl.BlockSpec((bm, D), lambda i, j: (i, 0)), w_spec, w_spec],
        out_specs=[o_spec, o_spec, o_spec],
        out_shape=[jax.ShapeDtypeStruct((T, FF), BF16)] * 3,
        compiler_params=_cp("parallel", "parallel"),
    )(h, wg, wu)


def _ffn_dact(name, dy, wd, sa, ds, after=None):
    T = dy.shape[0]
    bm = _pick(T, (2048, 1024, 512, 256, 128))
    bn = 256

    def body(dy_ref, wd_ref, sa_ref, ds_ref, *rest):
        da_ref, db_ref, dy_bf = rest[-3:]

        @pl.when(pl.program_id(1) == 0)
        def _():
            dy_bf[...] = dy_ref[...].astype(BF16)

        dact = 0.5 * _dot(dy_bf[...], wd_ref[...], _NT)
        da_ref[...] = (dact * ds_ref[...].astype(F32)).astype(BF16)
        db_ref[...] = (dact * sa_ref[...].astype(F32)).astype(BF16)

    t_spec = pl.BlockSpec((bm, bn), lambda i, j: (i, j))
    return pl.pallas_call(
        body, name=name, grid=(T // bm, FF // bn),
        in_specs=[pl.BlockSpec((bm, D), lambda i, j: (i, 0)), pl.BlockSpec((bn, D), lambda i, j: (j, 0)),
                  t_spec, t_spec] + [_ANY] * (after is not None),
        out_specs=[t_spec, t_spec],
        out_shape=[jax.ShapeDtypeStruct((T, FF), BF16), jax.ShapeDtypeStruct((T, FF), BF16)],
        scratch_shapes=[pltpu.VMEM((bm, D), BF16)],
        compiler_params=_cp("parallel", "arbitrary"),
    )(dy, wd, sa, ds, *([after] if after is not None else []))


def _merge_specs(T, bm, bn):
    y_spec = pl.BlockSpec((bm, BW), lambda i, j: (i, 0))
    wb_spec = pl.BlockSpec((NBR, bn, BW), lambda i, j: (0, j, 0))
    gate_specs = [pl.BlockSpec((bm, bn), functools.partial(lambda i, j, o: (i, o + j), o=(COL_GATE + g * D) // bn))
                  for g in range(NBR)]
    t_spec = pl.BlockSpec((bm, bn), lambda i, j: (i, j))
    return y_spec, wb_spec, gate_specs, t_spec


def _merge_fwd(name, ys, wb, proj):
    T = proj.shape[0]
    bm = _pick(T, (512, 256, 128))
    bn = 512
    y_spec, wb_spec, gate_specs, t_spec = _merge_specs(T, bm, bn)

    def body(y0, y1, y2, y3, wb_ref, g0, g1, g2, g3, o_ref):
        acc = None
        for g, (y_ref, g_ref) in enumerate(((y0, g0), (y1, g1), (y2, g2), (y3, g3))):
            t = _sigmoid(g_ref[...].astype(F32)) * _dot(y_ref[...], wb_ref[g], _NT)
            acc = t if acc is None else acc + t
        o_ref[...] = acc.astype(BF16)

    return pl.pallas_call(
        body, name=name, grid=(T // bm, D // bn),
        in_specs=[y_spec] * NBR + [wb_spec] + gate_specs, out_specs=t_spec,
        out_shape=jax.ShapeDtypeStruct((T, D), BF16),
        compiler_params=_cp("parallel", "parallel"),
    )(*ys, wb, proj, proj, proj, proj)


def _merge_bwd(name, dm, ys, wb, proj):
    T = proj.shape[0]
    bm = _pick(T, (512, 256, 128))
    bn = 512
    y_spec, wb_spec, gate_specs, t_spec = _merge_specs(T, bm, bn)

    def body(dm_ref, y0, y1, y2, y3, wb_ref, g0, g1, g2, g3, *outs):
        dmv = dm_ref[...]
        j = pl.program_id(1)
        for g, (y_ref, g_ref) in enumerate(((y0, g0), (y1, g1), (y2, g2), (y3, g3))):
            br = _dot(y_ref[...], wb_ref[g], _NT)
            s = _sigmoid(g_ref[...].astype(F32))
            outs[g][...] = (dmv * br * (s * (1.0 - s))).astype(BF16)
            dbr = (dmv * s).astype(BF16)
            outs[NBR + g][...] = dbr
            part = _dot(dbr, wb_ref[g])
            dy_ref = outs[2 * NBR + g]

            @pl.when(j == 0)
            def _():
                dy_ref[...] = part

            @pl.when(j > 0)
            def _():
                dy_ref[...] += part

    return pl.pallas_call(
        body, name=name, grid=(T // bm, D // bn),
        in_specs=[t_spec] + [y_spec] * NBR + [wb_spec] + gate_specs, out_specs=[t_spec] * (2 * NBR) + [y_spec] * NBR,
        out_shape=[jax.ShapeDtypeStruct((T, D), BF16)] * (2 * NBR) + [jax.ShapeDtypeStruct((T, BW), F32)] * NBR,
        compiler_params=_cp("parallel", "arbitrary"),
    )(dm, *ys, wb, proj, proj, proj, proj)


def _sgu_block(u_pre, v_pre, ln_g, ln_b, w, bias):
    u = _gelu(u_pre)
    vf = _gelu(v_pre)
    mu = jnp.mean(vf, axis=-1, keepdims=True)
    var = jnp.mean(jnp.square(vf - mu), axis=-1, keepdims=True)
    vn = (vf - mu) * lax.rsqrt(var + EPS) * ln_g + ln_b
    ri = lax.broadcasted_iota(jnp.int32, (128, 128), 0)
    ci = lax.broadcasted_iota(jnp.int32, (128, 128), 1)
    mask = (ri // CHUNK) >= (ci // CHUNK)
    outs = [_dot(jnp.where(mask, w[g], 0.0), vn[:, g * 128:(g + 1) * 128]) for g in range(4)]
    mixed = jnp.concatenate(outs, axis=1) + bias
    return u * mixed


def _sgu_param_specs():
    return [pl.BlockSpec((1, BW), lambda i: (0, 0)), pl.BlockSpec((1, BW), lambda i: (0, 0)),
            pl.BlockSpec((4, 128, 128), lambda i: (0, 0, 0)), pl.BlockSpec((128, BW), lambda i: (0, 0))]


def _sgu_fwd(name, proj, ln_g, ln_b, w, bias):
    T = proj.shape[0]
    rb = _pick(T, (256, 128))

    def body(u_ref, v_ref, g_ref, b_ref, w_ref, bias_ref, y_ref):
        for n in range(rb // 128):
            rows = slice(n * 128, (n + 1) * 128)
            y = _sgu_block(u_ref[rows, :].astype(F32), v_ref[rows, :].astype(F32), g_ref[...], b_ref[...], w_ref[...],
                           bias_ref[...])
            y_ref[rows, :] = y.astype(BF16)

    return pl.pallas_call(
        body, name=name, grid=(T // rb,),
        in_specs=[pl.BlockSpec((rb, BW), lambda i: (i, COL_AU // BW)), pl.BlockSpec((rb, BW), lambda i: (i, COL_AV // BW))]
        + _sgu_param_specs(),
        out_specs=pl.BlockSpec((rb, BW), lambda i: (i, 0)),
        out_shape=jax.ShapeDtypeStruct((T, BW), BF16),
        compiler_params=_cp("parallel"),
    )(proj, proj, ln_g, ln_b, w, bias)


def _sgu_bwd(name, proj, dy, ln_g, ln_b, w, bias):
    T = proj.shape[0]
    rb = _pick(T, (256, 128))

    def body(u_ref, v_ref, dy_ref, g_ref, b_ref, w_ref, bias_ref, du_ref, dv_ref, dg_ref, db_ref, dw_ref, dbias_ref):
        acc = None
        for n in range(rb // 128):
            rows = slice(n * 128, (n + 1) * 128)
            _, vjp = jax.vjp(_sgu_block, u_ref[rows, :].astype(F32), v_ref[rows, :].astype(F32), g_ref[...], b_ref[...],
                             w_ref[...],
                             bias_ref[...])
            du, dv, *dp = vjp(dy_ref[rows, :])
            du_ref[rows, :] = du.astype(BF16)
            dv_ref[rows, :] = dv.astype(BF16)
            acc = dp if acc is None else [p + q for p, q in zip(acc, dp)]

        @pl.when(pl.program_id(0) == 0)
        def _():
            for r, p in zip((dg_ref, db_ref, dw_ref, dbias_ref), acc):
                r[...] = p

        @pl.when(pl.program_id(0) > 0)
        def _():
            for r, p in zip((dg_ref, db_ref, dw_ref, dbias_ref), acc):
                r[...] += p

    row = pl.BlockSpec((rb, BW), lambda i: (i, 0))
    return pl.pallas_call(
        body, name=name, grid=(T // rb,),
        in_specs=[pl.BlockSpec((rb, BW), lambda i: (i, COL_AU // BW)), pl.BlockSpec((rb, BW), lambda i: (i, COL_AV // BW)),
                  row] + _sgu_param_specs(),
        out_specs=[row, row] + _sgu_param_specs(),
        out_shape=[jax.ShapeDtypeStruct((T, BW), BF16), jax.ShapeDtypeStruct((T, BW), BF16),
                   jax.ShapeDtypeStruct((1, BW), F32), jax.ShapeDtypeStruct((1, BW), F32),
                   jax.ShapeDtypeStruct((4, 128, 128), F32), jax.ShapeDtypeStruct((128, BW), F32)],
        compiler_params=_cp("arbitrary"),
    )(proj, proj, dy, ln_g, ln_b, w, bias)


def _halo_block(ref, i, rblk, halo):
    r0 = pl.multiple_of(i * rblk, rblk)
    h0 = pl.multiple_of(jnp.maximum(r0 - 16, 0), 16)
    top = jnp.where(i > 0, ref[pl.ds(h0, 16), :].astype(F32), 0.0)[16 - halo:]
    return jnp.concatenate([top, ref[pl.ds(r0, rblk), :].astype(F32)], axis=0)


def _with_halo_grad(dfull, pending, halo, rblk):
    tail = jnp.concatenate([jnp.zeros((rblk - halo, 128), F32), pending], axis=0)
    return dfull[halo:] + tail


def _conv4(xfull, rows):
    acc = None
    for k in range(4):
        t = rows[k] * _shift(xfull, 3 - k)[8:]
        acc = t if acc is None else acc + t
    return acc


def _lru_block(xfull, gate, h0, c0, c1, c2, c3, cb, wa, ba, wx, bx, lam):
    n = gate.shape[0]
    xc = _conv4(xfull, (c0, c1, c2, c3)) + cb
    r = _sigmoid(_dot(xc, wa) + ba)
    ig = _sigmoid(_dot(xc, wx) + bx)
    log_a = -LRU_C * r * _softplus(-lam)
    a = jnp.exp(log_a)
    mult = jnp.sqrt(-_expm1(2.0 * log_a))
    b = mult * (ig * xc)
    row = lax.broadcasted_iota(jnp.int32, (n, 128), 0)
    b = b + jnp.where(row == 0, a * h0, 0.0)
    h = _scan(a, b)
    out = h * _gelu(gate)
    h_last = jnp.sum(jnp.where(row == n - 1, h, 0.0), axis=0, keepdims=True)
    return out, h_last


def _lru_param_specs():
    vec = pl.BlockSpec((1, 128), lambda g: (0, g))
    mat = pl.BlockSpec((None, 128, 128), lambda g: (g, 0, 0))
    return [pl.BlockSpec((4, 128), lambda g: (0, g)), vec, mat, vec, mat, vec, vec]


def _lru_load_params(cw_ref, cb_ref, wa_ref, ba_ref, wx_ref, bx_ref, lam_ref):
    return (cw_ref[0:1, :], cw_ref[1:2, :], cw_ref[2:3, :], cw_ref[3:4, :], cb_ref[...], wa_ref[...], ba_ref[...],
            wx_ref[...], bx_ref[...], lam_ref[...])


def _lru_fwd(name, proj, cw, cb, wa, ba, wx, bx, lam):
    T = proj.shape[0]
    rblk = _pick(T, (256, 128))
    nblk = T // rblk

    def body(x_ref, gt_ref, cw_ref, cb_ref, wa_ref, ba_ref, wx_ref, bx_ref, lam_ref, y_ref, hc_ref):
        params = _lru_load_params(cw_ref, cb_ref, wa_ref, ba_ref, wx_ref, bx_ref, lam_ref)

        def step(i, h0):
            r0 = pl.multiple_of(i * rblk, rblk)
            out, h_last = _lru_block(_halo_block(x_ref, i, rblk, 8), gt_ref[pl.ds(r0, rblk), :].astype(F32), h0,
                                     *params)
            y_ref[pl.ds(r0, rblk), :] = out.astype(BF16)
            hc_ref[pl.ds(pl.multiple_of(i * 8, 8), 8), :] = jnp.broadcast_to(h0, (8, 128))
            return h_last

        lax.fori_loop(0, nblk, step, jnp.zeros((1, 128), F32))

    return pl.pallas_call(
        body, name=name, grid=(4,),
        in_specs=[pl.BlockSpec((T, 128), lambda g: (0, COL_BX // 128 + g)),
                  pl.BlockSpec((T, 128), lambda g: (0, COL_BG // 128 + g))] + _lru_param_specs(),
        out_specs=[pl.BlockSpec((T, 128), lambda g: (0, g)), pl.BlockSpec((nblk * 8, 128), lambda g: (0, g))],
        out_shape=[jax.ShapeDtypeStruct((T, BW), BF16), jax.ShapeDtypeStruct((nblk * 8, BW), F32)],
        compiler_params=_cp("parallel"),
    )(proj, proj, cw, cb, wa, ba, wx, bx, lam)


def _lru_bwd(name, proj, dy, hc, cw, cb, wa, ba, wx, bx, lam):
    T = proj.shape[0]
    rblk = _pick(T, (256, 128))
    nblk = T // rblk

    def body(x_ref, gt_ref, dy_ref, hc_ref, cw_ref, cb_ref, wa_ref, ba_ref, wx_ref, bx_ref, lam_ref,
             dx_ref, dgt_ref, dcw_ref, dcb_ref, dwa_ref, dba_ref, dwx_ref, dbx_ref, dlam_ref):
        params = _lru_load_params(cw_ref, cb_ref, wa_ref, ba_ref, wx_ref, bx_ref, lam_ref)

        def step(it, carry):
            dh_last, pending, acc = carry
            i = nblk - 1 - it
            r0 = pl.multiple_of(i * rblk, rblk)
            h0 = hc_ref[pl.ds(pl.multiple_of(i * 8, 8), 1), :]
            _, vjp = jax.vjp(_lru_block, _halo_block(x_ref, i, rblk, 8), gt_ref[pl.ds(r0, rblk), :].astype(F32), h0,
                             *params)
            dfull, dgate, dh0, *dp = vjp((dy_ref[pl.ds(r0, rblk), :], dh_last))
            dx_ref[pl.ds(r0, rblk), :] = _with_halo_grad(dfull, pending, 8, rblk).astype(BF16)
            dgt_ref[pl.ds(r0, rblk), :] = dgate.astype(BF16)
            return dh0, dfull[:8], tuple(p + q for p, q in zip(acc, dp))

        zeros = tuple(jnp.zeros(p.shape, F32) for p in params)
        _, _, acc = lax.fori_loop(0, nblk, step, (jnp.zeros((1, 128), F32), jnp.zeros((8, 128), F32), zeros))
        for k in range(4):
            dcw_ref[k:k + 1, :] = acc[k]
        for r, p in zip((dcb_ref, dwa_ref, dba_ref, dwx_ref, dbx_ref, dlam_ref), acc[4:]):
            r[...] = p

    col = pl.BlockSpec((T, 128), lambda g: (0, g))
    return pl.pallas_call(
        body, name=name, grid=(4,),
        in_specs=[pl.BlockSpec((T, 128), lambda g: (0, COL_BX // 128 + g)),
                  pl.BlockSpec((T, 128), lambda g: (0, COL_BG // 128 + g)), col,
                  pl.BlockSpec((nblk * 8, 128), lambda g: (0, g))] + _lru_param_specs(),
        out_specs=[col, col] + _lru_param_specs(),
        out_shape=[jax.ShapeDtypeStruct((T, BW), BF16), jax.ShapeDtypeStruct((T, BW), BF16),
                   jax.ShapeDtypeStruct((4, BW), F32), jax.ShapeDtypeStruct((1, BW), F32),
                   jax.ShapeDtypeStruct((4, 128, 128), F32), jax.ShapeDtypeStruct((1, BW), F32),
                   jax.ShapeDtypeStruct((4, 128, 128), F32), jax.ShapeDtypeStruct((1, BW), F32),
                   jax.ShapeDtypeStruct((1, BW), F32)],
        compiler_params=_cp("parallel"),
    )(proj, proj, dy, hc, cw, cb, wa, ba, wx, bx, lam)


def _conv_block(xfull, c0, c1, c2, c3):
    return _silu(_conv4(xfull, (c0, c1, c2, c3)))


def _conv_fwd(name, proj, col0, cw, cw_col0):
    T = proj.shape[0]
    rblk = _pick(T, (256, 128))
    nblk = T // rblk

    def body(x_ref, cw_ref, y_ref):
        rows = (cw_ref[0:1, :], cw_ref[1:2, :], cw_ref[2:3, :], cw_ref[3:4, :])

        def step(i, c):
            r0 = pl.multiple_of(i * rblk, rblk)
            y_ref[pl.ds(r0, rblk), :] = _conv_block(_halo_block(x_ref, i, rblk, 8), *rows)
            return c

        lax.fori_loop(0, nblk, step, 0)

    return pl.pallas_call(
        body, name=name, grid=(4,),
        in_specs=[pl.BlockSpec((T, 128), lambda g: (0, col0 // 128 + g)),
                  pl.BlockSpec((4, 128), lambda g: (0, cw_col0 // 128 + g))],
        out_specs=pl.BlockSpec((T, 128), lambda g: (0, g)),
        out_shape=jax.ShapeDtypeStruct((T, BW), F32),
        compiler_params=_cp("parallel"),
    )(proj, cw)


def _conv_bwd(name, proj, col0, dy, cw, cw_col0):
    T = proj.shape[0]
    rblk = _pick(T, (256, 128))
    nblk = T // rblk

    def body(x_ref, dy_ref, cw_ref, dx_ref, dcw_ref):
        rows = (cw_ref[0:1, :], cw_ref[1:2, :], cw_ref[2:3, :], cw_ref[3:4, :])

        def step(it, carry):
            pending, acc = carry
            i = nblk - 1 - it
            r0 = pl.multiple_of(i * rblk, rblk)
            _, vjp = jax.vjp(_conv_block, _halo_block(x_ref, i, rblk, 8), *rows)
            dfull, *dp = vjp(dy_ref[pl.ds(r0, rblk), :])
            dx_ref[pl.ds(r0, rblk), :] = _with_halo_grad(dfull, pending, 8, rblk).astype(BF16)
            return dfull[:8], tuple(p + q for p, q in zip(acc, dp))

        zeros = tuple(jnp.zeros((1, 128), F32) for _ in range(4))
        _, acc = lax.fori_loop(0, nblk, step, (jnp.zeros((8, 128), F32), zeros))
        for k in range(4):
            dcw_ref[k:k + 1, :] = acc[k]

    col = pl.BlockSpec((T, 128), lambda g: (0, g))
    return pl.pallas_call(
        body, name=name, grid=(4,),
        in_specs=[pl.BlockSpec((T, 128), lambda g: (0, col0 // 128 + g)), col,
                  pl.BlockSpec((4, 128), lambda g: (0, cw_col0 // 128 + g))],
        out_specs=[col, pl.BlockSpec((4, 128), lambda g: (0, g))],
        out_shape=[jax.ShapeDtypeStruct((T, BW), BF16), jax.ShapeDtypeStruct((4, BW), F32)],
        compiler_params=_cp("parallel"),
    )(proj, dy, cw)


def _pool_block(xfull, pw, sc, t0, gi):
    n = xfull.shape[0] - 16
    s2 = xfull + _shift(xfull, 1)
    s4 = s2 + _shift(s2, 2)
    s8 = s4 + _shift(s4, 4)
    s16 = s8 + _shift(s8, 8)
    s = jnp.where(gi == 0, s2, jnp.where(gi == 1, s4, jnp.where(gi == 2, s8, s16)))[16:]
    t = t0 + lax.broadcasted_iota(jnp.int32, (n, 128), 0)
    cnt = jnp.minimum(t + 1, lax.shift_left(jnp.int32(2), gi)).astype(F32)
    pooled = s / cnt - xfull[16:]
    return _dot(pooled, pw) * sc


def _pool_fwd(name, proj, pw, sc):
    T = proj.shape[0]
    rblk = _pick(T, (256, 128))
    nblk = T // rblk

    def body(x_ref, pw_ref, sc_ref, y_ref):
        gi = pl.program_id(0)

        def step(i, c):
            r0 = pl.multiple_of(i * rblk, rblk)
            y = _pool_block(_halo_block(x_ref, i, rblk, 16), pw_ref[...], sc_ref[...], r0, gi)
            y_ref[pl.ds(r0, rblk), :] = y.astype(BF16)
            return c

        lax.fori_loop(0, nblk, step, 0)

    return pl.pallas_call(
        body, name=name, grid=(4,),
        in_specs=[pl.BlockSpec((T, 128), lambda g: (0, COL_DX // 128 + g)),
                  pl.BlockSpec((None, 128, 128), lambda g: (g, 0, 0)), pl.BlockSpec((1, 128), lambda g: (0, g))],
        out_specs=pl.BlockSpec((T, 128), lambda g: (0, g)),
        out_shape=jax.ShapeDtypeStruct((T, BW), BF16),
        compiler_params=_cp("parallel"),
    )(proj, pw, sc)


def _pool_bwd(name, proj, dy, pw, sc):
    T = proj.shape[0]
    rblk = _pick(T, (256, 128))
    nblk = T // rblk

    def body(x_ref, dy_ref, pw_ref, sc_ref, dx_ref, dpw_ref, dsc_ref):
        gi = pl.program_id(0)

        def step(it, carry):
            pending, apw, asc = carry
            i = nblk - 1 - it
            r0 = pl.multiple_of(i * rblk, rblk)
            _, vjp = jax.vjp(lambda xf, w, s: _pool_block(xf, w, s, r0, gi), _halo_block(x_ref, i, rblk, 16),
                             pw_ref[...], sc_ref[...])
            dfull, dw, ds = vjp(dy_ref[pl.ds(r0, rblk), :])
            dx_ref[pl.ds(r0, rblk), :] = _with_halo_grad(dfull, pending, 16, rblk).astype(BF16)
            return dfull[:16], apw + dw, asc + ds

        _, apw, asc = lax.fori_loop(0, nblk, step, (jnp.zeros((16, 128), F32), jnp.zeros((128, 128), F32),
                                                    jnp.zeros((1, 128), F32)))
        dpw_ref[...] = apw
        dsc_ref[...] = asc

    col = pl.BlockSpec((T, 128), lambda g: (0, g))
    mat = pl.BlockSpec((None, 128, 128), lambda g: (g, 0, 0))
    vec = pl.BlockSpec((1, 128), lambda g: (0, g))
    return pl.pallas_call(
        body, name=name, grid=(4,),
        in_specs=[pl.BlockSpec((T, 128), lambda g: (0, COL_DX // 128 + g)), col, mat, vec],
        out_specs=[col, mat, vec],
        out_shape=[jax.ShapeDtypeStruct((T, BW), BF16), jax.ShapeDtypeStruct((4, 128, 128), F32),
                   jax.ShapeDtypeStruct((1, BW), F32)],
        compiler_params=_cp("parallel"),
    )(proj, dy, pw, sc)


@jax.custom_vjp
def _dot3(a, b):
    ah = a.astype(BF16)
    al = (a - ah.astype(F32)).astype(BF16)
    bh = b.astype(BF16)
    bl = (b - bh.astype(F32)).astype(BF16)

    def d(x, y):
        return lax.dot_general(x, y, _NN, preferred_element_type=F32)

    return d(ah, bh) + (d(ah, bl) + d(al, bh))


def _dot3_fwd(a, b):
    return _dot3(a, b), (a, b)


def _dot3_bwd(res, g):
    a, b = res
    return _dot(g, b, _NT), _dot(a, g, _TN)


_dot3.defvjp(_dot3_fwd, _dot3_bwd)


def _pad_rows2(x):
    return jnp.concatenate([x, jnp.zeros_like(x)], axis=0)


@jax.custom_vjp
def _tri_inv(mats):
    n = mats[0].shape[0]
    eye = (lax.broadcasted_iota(jnp.int32, (n, n), 0) == lax.broadcasted_iota(jnp.int32, (n, n), 1)).astype(F32)
    ps = [eye - a for a in mats]
    ms = list(mats)
    k = 2
    while k < n:
        ms = [_dot3(t, t) for t in ms]
        ps = [p + _dot3(p, t) for p, t in zip(ps, ms)]
        k *= 2
    return ps


def _tri_inv_fwd(mats):
    ts = _tri_inv(mats)
    return ts, ts


def _tri_inv_bwd(ts, gs):
    half = [_dot(t, g, _TN) for t, g in zip(ts, gs)]
    return ([-_dot(h, t, _NT) for h, t in zip(half, ts)],)


_tri_inv.defvjp(_tri_inv_fwd, _tri_inv_bwd)


def _cumsum_rows(x):
    n = x.shape[0]
    row = lax.broadcasted_iota(jnp.int32, x.shape, 0)
    k = 1
    while k < n:
        x = x + jnp.where(row >= k, _shift(x, k), 0.0)
        k *= 2
    return x


def _gdn_prep(qcs, kcs, vcs, tails, alog, dtb):
    C = CHUNK
    pairs = [(c, h) for c in range(len(qcs)) for h in range(4)]
    lane = lax.broadcasted_iota(jnp.int32, (C, 128), 1)
    row = lax.broadcasted_iota(jnp.int32, (C, 128), 0)
    incl = row >= lane
    sig = [_sigmoid(t) for t in tails]
    gfull = [-jnp.exp(alog) * _softplus(t + dtb) for t in tails]
    beta = [jnp.sum(jnp.where(lane == h, sig[c], 0.0), axis=1, keepdims=True) for c, h in pairs]
    g = [jnp.sum(jnp.where(lane == h + 4, gfull[c], 0.0), axis=1, keepdims=True) for c, h in pairs]
    qs = [qcs[c][:, h * 128:(h + 1) * 128] for c, h in pairs]
    ks = [kcs[c][:, h * 128:(h + 1) * 128] for c, h in pairs]
    vs = [vcs[c][:, h * 128:(h + 1) * 128] for c, h in pairs]
    q = [t * lax.rsqrt(jnp.sum(t * t, axis=-1, keepdims=True) + EPS) * (GDN_DK ** -0.5) for t in qs]
    k = [t * lax.rsqrt(jnp.sum(t * t, axis=-1, keepdims=True) + EPS) for t in ks]
    gc = [_cumsum_rows(jnp.broadcast_to(t, (C, 128))) for t in g]
    gc_t = [jnp.transpose(jnp.concatenate([t, t], axis=0)) for t in gc]
    gc_col = [jnp.sum(jnp.where(lane == 0, t, 0.0), axis=1, keepdims=True) for t in gc]
    ri = lax.broadcasted_iota(jnp.int32, (C, C), 0)
    ci = lax.broadcasted_iota(jnp.int32, (C, C), 1)
    decay = [jnp.exp(jnp.where(incl, a - b[:C, :], -1e30)) for a, b in zip(gc, gc_t)]
    decay_sq = [jnp.exp(jnp.where(ri > ci, a - jnp.transpose(b)[:C, :], -1e30)) for a, b in zip(gc_col, gc)]
    kb = [a * b for a, b in zip(k, beta)]
    kk = [_dot(a, b, _NT) for a, b in zip(kb, k)]
    t_mat = _tri_inv([jnp.where(ri > ci, a * b, 0.0) for a, b in zip(kk, decay_sq)])
    egc = [jnp.exp(t) for t in gc]
    u = [_dot(t, a * b) for t, a, b in zip(t_mat, vs, beta)]
    w = [_dot(t, a * b) for t, a, b in zip(t_mat, kb, egc)]
    qk = [_dot(a, _pad_rows2(b), _NT) for a, b in zip(q, k)]
    attn = [jnp.where(incl, a * b, 0.0) for a, b in zip(qk, decay)]
    g_last = [jnp.sum(jnp.where(row == C - 1, t, 0.0), axis=0, keepdims=True) for t in gc]
    qe = [a * b for a, b in zip(q, egc)]
    kd = [a * jnp.exp(b - c_) for a, b, c_ in zip(k, g_last, gc)]
    egl = [jnp.exp(t) for t in g_last]

    def per_chunk(vals):
        return [jnp.concatenate(vals[4 * c:4 * c + 4], axis=1) for c in range(len(qcs))]

    return tuple(per_chunk(t) for t in (u, w, qe, kd, attn, egl))


def _gdn_scan_chunk(states, u, w, qe, kd, attn, egl, z, ng):
    hs = range(4)

    def sl(t, h):
        return t[:, h * 128:(h + 1) * 128]

    ws = [_dot(sl(w, h), states[h]) for h in hs]
    qs = [_dot(sl(qe, h), states[h]) for h in hs]
    v_new = [sl(u, h) - ws[h] for h in hs]
    av = [_dot(sl(attn, h), _pad_rows2(v_new[h])) for h in hs]
    kv = [_dot(sl(kd, h), v_new[h], _TN) for h in hs]
    nxt = tuple(states[h] * sl(egl, h) + kv[h] for h in hs)
    o = [qs[h] + av[h] for h in hs]
    on = [t * lax.rsqrt(jnp.mean(t * t, axis=-1, keepdims=True) + EPS) * ng for t in o]
    return nxt, jnp.concatenate(on, axis=1) * _silu(z)


def _gdn_blocks(T):
    tb = _pick(T, (512, 256, 128, 64))
    return tb, T // tb, tb // CHUNK


PREP_CHUNKS = 4


def _chunk_rows(i, n):
    return [pl.ds(pl.multiple_of((i * n + j) * CHUNK, CHUNK), CHUNK) for j in range(n)]


def _egl_rows(i, n, size):
    return [pl.ds(pl.multiple_of((i * n + j) * 8, 8), size) for j in range(n)]


def _gdn_prep_fwd(name, qa, ka, va, proj, alog, dtb):
    T = proj.shape[0]
    tb, nb, ncb = _gdn_blocks(T)
    n = PREP_CHUNKS if ncb % PREP_CHUNKS == 0 else 1

    def body(q_ref, k_ref, v_ref, tail_ref, alog_ref, dtb_ref, u_ref, w_ref, qe_ref, kd_ref, at_ref, egl_ref):
        def step(i, c):
            rows = _chunk_rows(i, n)
            u, w, qe, kd, at, egl = _gdn_prep([q_ref[r, :] for r in rows], [k_ref[r, :] for r in rows],
                                              [v_ref[r, :] for r in rows], [tail_ref[r, :].astype(F32) for r in rows],
                                              alog_ref[...], dtb_ref[...])
            for j, (r, e) in enumerate(zip(rows, _egl_rows(i, n, 8))):
                u_ref[r, :] = u[j]
                w_ref[r, :] = w[j].astype(BF16)
                qe_ref[r, :] = qe[j].astype(BF16)
                kd_ref[r, :] = kd[j].astype(BF16)
                at_ref[r, :] = at[j].astype(BF16)
                egl_ref[e, :] = jnp.broadcast_to(egl[j], (8, BW))
            return c

        lax.fori_loop(0, ncb // n, step, 0)

    blk = pl.BlockSpec((tb, BW), lambda j: (j, 0))
    vec = pl.BlockSpec((1, 128), lambda j: (0, 0))
    return pl.pallas_call(
        body, name=name, grid=(nb,),
        in_specs=[blk, blk, blk, pl.BlockSpec((tb, 128), lambda j: (j, COL_TAIL // 128)), vec, vec],
        out_specs=[blk] * 5 + [pl.BlockSpec((ncb * 8, BW), lambda j: (j, 0))],
        out_shape=[jax.ShapeDtypeStruct((T, BW), F32)] + [jax.ShapeDtypeStruct((T, BW), BF16)] * 4
        + [jax.ShapeDtypeStruct((T // 8, BW), F32)],
        compiler_params=_cp("parallel"),
    )(qa, ka, va, proj, alog, dtb)


def _gdn_prep_bwd(name, qa, ka, va, proj, alog, dtb, du, dw, dqe, dkd, dat, degl):
    T = proj.shape[0]
    tb, nb, ncb = _gdn_blocks(T)
    n = PREP_CHUNKS if ncb % PREP_CHUNKS == 0 else 1

    def body(q_ref, k_ref, v_ref, tail_ref, alog_ref, dtb_ref, du_ref, dw_ref, dqe_ref, dkd_ref, dat_ref, degl_ref,
             dq_ref, dk_ref, dv_ref, dtail_ref, dalog_ref, ddtb_ref):
        first = pl.program_id(0) == 0

        def step(i, carry):
            pa, pd = carry
            rows = _chunk_rows(i, n)
            _, vjp = jax.vjp(_gdn_prep, [q_ref[r, :] for r in rows], [k_ref[r, :] for r in rows],
                             [v_ref[r, :] for r in rows], [tail_ref[r, :].astype(F32) for r in rows], alog_ref[...], dtb_ref[...])
            cot = tuple([ref[r, :] for r in rows] for ref in (du_ref, dw_ref, dqe_ref, dkd_ref, dat_ref))
            dq, dk, dv, dtail, da, dd = vjp(cot + ([degl_ref[e, :] for e in _egl_rows(i, n, 1)],))
            for j, r in enumerate(rows):
                dq_ref[r, :] = dq[j]
                dk_ref[r, :] = dk[j]
                dv_ref[r, :] = dv[j]
                dtail_ref[r, :] = dtail[j].astype(BF16)
            return pa + da, pd + dd

        zv = jnp.zeros((1, 128), F32)
        pa, pd = lax.fori_loop(0, ncb // n, step, (zv, zv))

        @pl.when(first)
        def _():
            dalog_ref[...] = pa
            ddtb_ref[...] = pd

        @pl.when(jnp.logical_not(first))
        def _():
            dalog_ref[...] += pa
            ddtb_ref[...] += pd

    blk = pl.BlockSpec((tb, BW), lambda j: (j, 0))
    vec = pl.BlockSpec((1, 128), lambda j: (0, 0))
    return pl.pallas_call(
        body, name=name, grid=(nb,),
        in_specs=[blk, blk, blk, pl.BlockSpec((tb, 128), lambda j: (j, COL_TAIL // 128)), vec, vec]
        + [blk] * 5 + [pl.BlockSpec((ncb * 8, BW), lambda j: (j, 0))],
        out_specs=[blk, blk, blk, pl.BlockSpec((tb, 128), lambda j: (j, 0)), vec, vec],
        out_shape=[jax.ShapeDtypeStruct((T, BW), F32)] * 3 + [jax.ShapeDtypeStruct((T, 128), BF16)]
        + [jax.ShapeDtypeStruct((1, 128), F32)] * 2,
        compiler_params=_cp("arbitrary"),
    )(qa, ka, va, proj, alog, dtb, du, dw, dqe, dkd, dat, degl)


def _gdn_fwd(name, u, w, qe, kd, at, egl, proj, ng):
    T = proj.shape[0]
    tb, nb, ncb = _gdn_blocks(T)

    def body(u_ref, w_ref, qe_ref, kd_ref, at_ref, egl_ref, z_ref, ng_ref, y_ref, sh_ref, state):
        @pl.when(pl.program_id(0) == 0)
        def _():
            state[...] = jnp.zeros((4, 128, 128), F32)

        def step(c, states):
            rows = pl.ds(pl.multiple_of(c * CHUNK, CHUNK), CHUNK)
            for h in range(4):
                sh_ref[h, c] = states[h]
            nxt, y = _gdn_scan_chunk(states, u_ref[rows, :], w_ref[rows, :], qe_ref[rows, :], kd_ref[rows, :],
                                     at_ref[rows, :], egl_ref[pl.ds(pl.multiple_of(c * 8, 8), 1), :],
                                     z_ref[rows, :].astype(F32),
                                     ng_ref[...])
            y_ref[rows, :] = y.astype(BF16)
            return nxt

        states = lax.fori_loop(0, ncb, step, tuple(state[h] for h in range(4)))
        for h in range(4):
            state[h] = states[h]

    blk = pl.BlockSpec((tb, BW), lambda j: (j, 0))
    vec = pl.BlockSpec((1, 128), lambda j: (0, 0))
    return pl.pallas_call(
        body, name=name, grid=(nb,),
        in_specs=[blk] * 5 + [pl.BlockSpec((ncb * 8, BW), lambda j: (j, 0)),
                              pl.BlockSpec((tb, BW), lambda j: (j, COL_CZ // BW)), vec],
        out_specs=[blk, pl.BlockSpec((4, ncb, 128, 128), lambda j: (0, j, 0, 0))],
        out_shape=[jax.ShapeDtypeStruct((T, BW), BF16), jax.ShapeDtypeStruct((4, T // CHUNK, 128, 128), F32)],
        scratch_shapes=[pltpu.VMEM((4, 128, 128), F32)],
        compiler_params=_cp("arbitrary"),
    )(u, w, qe, kd, at, egl, proj, ng)


def _gdn_bwd(name, u, w, qe, kd, at, egl, proj, dy, sh, ng):
    T = proj.shape[0]
    tb, nb, ncb = _gdn_blocks(T)

    def body(u_ref, w_ref, qe_ref, kd_ref, at_ref, egl_ref, z_ref, dy_ref, sh_ref, ng_ref,
             du_ref, dw_ref, dqe_ref, dkd_ref, dat_ref, degl_ref, dz_ref, dng_ref, dstate):
        first = pl.program_id(0) == 0

        @pl.when(first)
        def _():
            dstate[...] = jnp.zeros((4, 128, 128), F32)

        def step(it, carry):
            dstates, pn = carry
            c = ncb - 1 - it
            rows = pl.ds(pl.multiple_of(c * CHUNK, CHUNK), CHUNK)
            erow = pl.multiple_of(c * 8, 8)
            _, vjp = jax.vjp(_gdn_scan_chunk, tuple(sh_ref[h, c] for h in range(4)), u_ref[rows, :],
                             w_ref[rows, :].astype(F32), qe_ref[rows, :].astype(F32), kd_ref[rows, :].astype(F32),
                             at_ref[rows, :].astype(F32), egl_ref[pl.ds(erow, 1), :], z_ref[rows, :].astype(F32),
                             ng_ref[...])
            nxt, du, dw, dqe, dkd, dat, degl, dz, dn = vjp((dstates, dy_ref[rows, :]))
            du_ref[rows, :] = du
            dw_ref[rows, :] = dw
            dqe_ref[rows, :] = dqe
            dkd_ref[rows, :] = dkd
            dat_ref[rows, :] = dat
            degl_ref[pl.ds(erow, 8), :] = jnp.broadcast_to(degl, (8, BW))
            dz_ref[rows, :] = dz.astype(BF16)
            return nxt, pn + dn

        dstates, pn = lax.fori_loop(0, ncb, step, (tuple(dstate[h] for h in range(4)), jnp.zeros((1, 128), F32)))
        for h in range(4):
            dstate[h] = dstates[h]

        @pl.when(first)
        def _():
            dng_ref[...] = pn

        @pl.when(jnp.logical_not(first))
        def _():
            dng_ref[...] += pn

    blk = pl.BlockSpec((tb, BW), lambda j: (nb - 1 - j, 0))
    eblk = pl.BlockSpec((ncb * 8, BW), lambda j: (nb - 1 - j, 0))
    vec = pl.BlockSpec((1, 128), lambda j: (0, 0))
    return pl.pallas_call(
        body, name=name, grid=(nb,),
        in_specs=[blk] * 5 + [eblk, pl.BlockSpec((tb, BW), lambda j: (nb - 1 - j, COL_CZ // BW)), blk,
                              pl.BlockSpec((4, ncb, 128, 128), lambda j: (0, nb - 1 - j, 0, 0)), vec],
        out_specs=[blk] * 5 + [eblk, blk, vec],
        out_shape=[jax.ShapeDtypeStruct((T, BW), F32)] * 5 + [jax.ShapeDtypeStruct((T // 8, BW), F32),
                                                              jax.ShapeDtypeStruct((T, BW), BF16),
                                                              jax.ShapeDtypeStruct((1, 128), F32)],
        scratch_shapes=[pltpu.VMEM((4, 128, 128), F32)],
        compiler_params=_cp("arbitrary"),
    )(u, w, qe, kd, at, egl, proj, dy, sh, ng)


def _adamw_update(w_ref, g_ref, m_ref, v_ref, d_ref, nm_ref, nv_ref):
    gv = g_ref[...]
    m2 = ADAM_B1 * m_ref[...] + (1.0 - ADAM_B1) * gv
    v2 = ADAM_B2 * v_ref[...] + (1.0 - ADAM_B2) * jnp.square(gv)
    m_hat = m2 / (1.0 - ADAM_B1 ** ADAM_STEP)
    v_hat = v2 / (1.0 - ADAM_B2 ** ADAM_STEP)
    d_ref[...] = -ADAM_LR * (m_hat / (jnp.sqrt(v_hat) + ADAM_EPS) + ADAM_WD * w_ref[...])
    nm_ref[...] = m2
    nv_ref[...] = v2


def _adamw_many(name, ws, gs, ms, vs):
    n = len(ws)

    def body(*refs):
        for i in range(n):
            _adamw_update(*[refs[k * n + i] for k in range(7)])

    return pl.pallas_call(
        body, name=name,
        out_shape=[jax.ShapeDtypeStruct(a.shape, F32) for a in ws] * 3,
        compiler_params=_cp(),
    )(*ws, *gs, *ms, *vs)


def _adamw(name, w, g, m, v):
    R, C = w.shape
    br = _pick(R, (512, 256, 240, 128, 64, 8))
    body = functools.partial(_adamw_update)
    spec = pl.BlockSpec((br, C), lambda i: (i, 0))
    return pl.pallas_call(
        body, name=name, grid=(R // br,),
        in_specs=[spec] * 4, out_specs=[spec] * 3,
        out_shape=[jax.ShapeDtypeStruct((R, C), F32)] * 3,
        compiler_params=_cp("parallel"),
    )(w, g, m, v)


def _sum8(name, parts):
    _, R, C = parts.shape
    br = _pick(R, (352, 368, 256, 128, 64, 16, 8))

    def body(p_ref, o_ref):
        acc = p_ref[0].astype(F32)
        for d in range(1, N_DEV):
            acc = acc + p_ref[d].astype(F32)
        o_ref[...] = acc

    return pl.pallas_call(
        body, name=name, grid=(R // br,),
        in_specs=[pl.BlockSpec((N_DEV, br, C), lambda i: (0, i, 0))],
        out_specs=pl.BlockSpec((br, C), lambda i: (i, 0)),
        out_shape=jax.ShapeDtypeStruct((R, C), F32),
        compiler_params=_cp("parallel"),
    )(parts)


_ANY = pl.BlockSpec(memory_space=pl.ANY)
_MESH = pl.DeviceIdType.MESH


def _all_gather(name, shard):
    R, C = shard.shape

    def body(x_ref, out_ref, send_sems, recv_sems, local_sem):
        x, y, c = lax.axis_index("x"), lax.axis_index("y"), lax.axis_index("c")
        me, sibling = (x, y, c), (x, y, 1 - c)
        chips = [(1 - x, y), (x, 1 - y), (1 - x, 1 - y)]

        def slot(px, py, pc):
            return out_ref.at[4 * px + 2 * py + pc]

        def copy(k, block, to, src=None):
            return pltpu.make_async_remote_copy(
                src_ref=slot(*block) if src is None else src, dst_ref=slot(*block),
                send_sem=send_sems.at[k], recv_sem=recv_sems.at[k], device_id=to, device_id_type=_MESH)

        mine = pltpu.make_async_copy(x_ref, slot(*me), local_sem)
        mine.start()
        first = [copy(0, me, sibling, src=x_ref)]
        first += [copy(1 + j, me, (*chip, c), src=x_ref) for j, chip in enumerate(chips)]
        for cp in first:
            cp.start()
        passed = [copy(4 + j, (*chip, c), sibling) for j, chip in enumerate(chips)]
        for j, chip in enumerate(chips):
            copy(1 + j, (*chip, c), me).wait_recv()
            passed[j].start()
        copy(0, sibling, me).wait_recv()
        for j, chip in enumerate(chips):
            copy(4 + j, (*chip, 1 - c), me).wait_recv()
        for cp in first + passed:
            cp.wait_send()
        mine.wait()

    return pl.pallas_call(
        body, name=name,
        in_specs=[_ANY], out_specs=_ANY,
        out_shape=jax.ShapeDtypeStruct((N_DEV, R, C), shard.dtype),
        scratch_shapes=[pltpu.SemaphoreType.DMA((7,)), pltpu.SemaphoreType.DMA((7,)), pltpu.SemaphoreType.DMA],
    )(shard)


_HBM = pl.BlockSpec(memory_space=pltpu.HBM)
_SEM = pl.BlockSpec(memory_space=pltpu.SEMAPHORE)
_EFFECT = pltpu.SideEffectType.DATAFLOW_SIDE_EFFECTING


def _exchange_copies(src_ref, land_ref, send_sems, recv_sems, scatter):
    x, y, c = lax.axis_index("x"), lax.axis_index("y"), lax.axis_index("c")
    me = 4 * x + 2 * y + c
    copies = []
    for k in range(1, N_DEV):
        px, py, pc = x ^ ((k >> 2) & 1), y ^ ((k >> 1) & 1), c ^ (k & 1)
        src = src_ref.at[4 * px + 2 * py + pc] if scatter else src_ref
        copies.append(pltpu.make_async_remote_copy(
            src_ref=src, dst_ref=land_ref.at[me], send_sem=send_sems.at[k - 1], recv_sem=recv_sems.at[k - 1],
            device_id=(px, py, pc), device_id_type=_MESH))
    return copies


def _own_copy(src_ref, land_ref, send_sems, scatter):
    me = 4 * lax.axis_index("x") + 2 * lax.axis_index("y") + lax.axis_index("c")
    return pltpu.make_async_copy(src_ref.at[me] if scatter else src_ref, land_ref.at[me], send_sems.at[N_DEV - 1])


def _exchange_start(name, srcs, lands, scatter, after=None):
    n = len(srcs)

    def body(*refs):
        src_refs, land_refs = refs[:n], refs[n:2 * n]
        outs = refs[2 * n + (after is not None):]
        send, recv = outs[:n], outs[n:2 * n]
        token = refs[-1]
        for g in range(n):
            for cp in _exchange_copies(src_refs[g], land_refs[g], send[g], recv[g], scatter):
                cp.start()
            _own_copy(src_refs[g], land_refs[g], send[g], scatter).start()
        token[...] = jnp.zeros_like(token)

    outs = pl.pallas_call(
        body, name=name,
        out_shape=tuple([pltpu.SemaphoreType.DMA((N_DEV,))] * (2 * n)
                        + [pltpu.HBM(a.shape, a.dtype) for a in list(srcs) + list(lands)]
                        + [jax.ShapeDtypeStruct((8, 128), F32)]),
        in_specs=[_HBM] * (2 * n) + [_ANY] * (after is not None),
        out_specs=tuple([_SEM] * (2 * n) + [_HBM] * (2 * n) + [pl.BlockSpec(memory_space=pltpu.VMEM)]),
        input_output_aliases={i: 2 * n + i for i in range(2 * n)},
        compiler_params=pltpu.CompilerParams(has_side_effects=_EFFECT),
    )(*[pltpu.with_memory_space_constraint(a, pltpu.HBM) for a in list(srcs) + list(lands)],
      *([after] if after is not None else []))
    handles = [(outs[2 * n + g], outs[3 * n + g], outs[g], outs[n + g]) for g in range(n)]
    return handles, outs[-1]


def _exchange_wait(name, handles, after, scatter):
    n = len(handles)
    srcs, lands, sends, recvs = ([h[i] for h in handles] for i in range(4))

    def body(*refs):
        src_refs, land_refs = refs[:n], refs[n:2 * n]
        send, recv = refs[2 * n:3 * n], refs[3 * n:4 * n]
        for g in range(n):
            for cp in _exchange_copies(src_refs[g], land_refs[g], send[g], recv[g], scatter):
                cp.wait_send()
                cp.wait_recv()
            _own_copy(src_refs[g], land_refs[g], send[g], scatter).wait()

    outs = pl.pallas_call(
        body, name=name,
        out_shape=tuple(pltpu.HBM(a.shape, a.dtype) for a in srcs + lands),
        in_specs=tuple([_HBM] * (2 * n) + [_SEM] * (2 * n) + [_ANY]), out_specs=tuple([_HBM] * (2 * n)),
        input_output_aliases={i: i for i in range(2 * n)},
        compiler_params=pltpu.CompilerParams(has_side_effects=_EFFECT),
    )(*srcs, *lands, *sends, *recvs, after)
    return list(outs[n:])


def _rows(a):
    return a.reshape(-1, 1024)


def _rows_to_parts(full):
    n = full.shape[-2] // N_DEV
    t = full.reshape(full.shape[:-2] + (N_DEV, n, full.shape[-1]))
    return jnp.moveaxis(t, -3, 0)


def _parts_to_rows(parts):
    t = jnp.moveaxis(parts, 0, -3)
    return t.reshape(t.shape[:-3] + (t.shape[-3] * t.shape[-2], t.shape[-1]))


def _parts_to_cols(parts):
    t = jnp.moveaxis(parts, 0, -2)
    return t.reshape(t.shape[:-2] + (t.shape[-2] * t.shape[-1],))


def _join(parts, axis=0):
    total = sum(p.shape[axis] for p in parts)
    out, off = None, 0
    for p in parts:
        cfg = [(0, 0)] * p.ndim
        cfg[axis] = (off, total - off - p.shape[axis])
        t = jnp.pad(p, cfg)
        out = t if out is None else out + t
        off += p.shape[axis]
    return out


def _w_in_to_layout(w):
    tail = jnp.pad(w[4096:4104], ((0, PW - COL_TAIL - 8), (0, 0)))
    return jnp.concatenate([w[:4096], w[4104:P_IN], tail], axis=0)


def _w_in_from_layout(g):
    return _join([g[:4096], g[COL_TAIL:COL_TAIL + 8], g[4096:COL_TAIL]], axis=0)


def _block_diag(w):
    w = w.reshape(4, 2, 64, 64)
    return jnp.pad(w[:, 0], ((0, 0), (0, 64), (0, 64))) + jnp.pad(w[:, 1], ((0, 0), (64, 0), (64, 0)))


def _block_diag_grad(g):
    return jnp.stack([g[:, :64, :64], g[:, 64:, 64:]], axis=1).reshape(8, 64, 64)


def _ffn_forward(tag, x, norm, wg, wu, wd):
    h = _rms_fwd(tag + "_norm", x, norm)
    sa, ds, act = _ffn_up(tag + "_up", h, wg, wu)
    if callable(wd):
        wd = wd(act)
    x_out = _mm(tag + "_down", [(act, wd)], "nn", F32, res=x, scale=0.5)
    return x_out, (x, h, sa, ds, act), wd


def _ffn_backward(tag, dx_out, saved, norm, wg, wu, wd, put, names, split=False):
    x, h, sa, ds, act = saved
    n_wg, n_wu, n_wd = names
    dwd = _mm(tag + "_dwd", [(act, dx_out)], "tn", BF16, scale=0.5, bm=FF // 2)
    tok = put({n_wd: dwd}) if split else None
    da, db = _ffn_dact(tag + "_dact", dx_out, wd, sa, ds, after=tok)
    dwg = _mm(tag + "_dwg", [(da, h)], "tn", BF16, bm=FF // 2)
    if split:
        tok = tok + put({n_wg: dwg})
    dwu = _mm(tag + "_dwu", [(db, h)], "tn", BF16, bm=FF // 2, after=tok)
    tok = tok + put({n_wu: dwu}) if split else put({n_wg: dwg, n_wu: dwu, n_wd: dwd})
    dh = _mm(tag + "_dh", [(da, wg), (db, wu)], "nn", F32, after=tok)
    dx, dnorm = _rms_bwd(tag + "_dnorm", x, norm + tok, dh, dx_out)
    return dx, dnorm


def _mixer_params(p):
    alog = jnp.pad(p["gdn_a_log"], (4, 120))[None]
    dtb = jnp.pad(p["gdn_dt_bias"], (4, 120))[None]
    bias = jnp.repeat(p["sgu_b"].T, 128, axis=1)
    return dict(
        ln_g=p["sgu_ln_g"][None], ln_b=p["sgu_ln_b"][None], sgu_w=p["sgu_w"], sgu_bias=bias,
        lru_cw=p["lru_conv_w"], lru_cb=p["lru_conv_b"][None], wa=_block_diag(p["lru_wa"]), ba=p["lru_ba"][None],
        wx=_block_diag(p["lru_wx"]), bx=p["lru_bx"][None], lam=p["lru_lambda"][None],
        gdn_cw=p["gdn_conv_w"], alog=alog, dtb=dtb, ng=p["gdn_norm_g"][None],
        pool_w=p["pool_w"], pool_sc=p["pool_scale"][None])


def _mix_forward(tag, x, p, mp):
    h = _rms_fwd(tag + "_norm", x, p["mix_norm"][None])
    proj = _mm(tag + "_proj", [(h, p["w_in"])], "nt", BF16, bm=_pick(x.shape[0], (2048, 1024, 512, 256, 128)))
    y_a = _sgu_fwd(tag + "_sgu", proj, mp["ln_g"], mp["ln_b"], mp["sgu_w"], mp["sgu_bias"])
    y_b, hc = _lru_fwd(tag + "_lru", proj, mp["lru_cw"], mp["lru_cb"], mp["wa"], mp["ba"], mp["wx"], mp["bx"],
                       mp["lam"])
    qa = _conv_fwd(tag + "_convq", proj, COL_CQ, mp["gdn_cw"], 0)
    ka = _conv_fwd(tag + "_convk", proj, COL_CK, mp["gdn_cw"], 512)
    va = _conv_fwd(tag + "_convv", proj, COL_CV, mp["gdn_cw"], 1024)
    prep = _gdn_prep_fwd(tag + "_gdnprep", qa, ka, va, proj, mp["alog"], mp["dtb"])
    y_c, sh = _gdn_fwd(tag + "_gdn", *prep, proj, mp["ng"])
    y_d = _pool_fwd(tag + "_pool", proj, mp["pool_w"], mp["pool_sc"])
    ys = (y_a, y_b, y_c, y_d)
    if callable(p["w_branch"]):
        p["w_branch"] = p["w_branch"](y_d)
    merged = _merge_fwd(tag + "_merge", ys, p["w_branch"], proj)
    if callable(p["w_out"]):
        p["w_out"] = p["w_out"](merged)
    x_out = _mm(tag + "_out", [(merged, p["w_out"])], "nn", F32, res=x)
    return x_out, (x, h, proj, hc, qa, ka, va, prep, sh, ys, merged)


def _mix_backward(tag, dx_out, saved, p, mp, put):
    x, h, proj, hc, qa, ka, va, prep, sh, ys, merged = saved
    T = x.shape[0]
    g = {}
    dmerged = _mm(tag + "_dmerged", [(dx_out, p["w_out"])], "nt", F32)
    g["w_out"] = _mm(tag + "_dwout", [(merged, dx_out)], "tn", BF16)
    outs = _merge_bwd(tag + "_dmerge", dmerged, ys, p["w_branch"], proj)
    dgates, dbrs, dys = outs[:NBR], outs[NBR:2 * NBR], outs[2 * NBR:]
    g["w_branch"] = jnp.stack([_mm(f"{tag}_dwb{i}", [(dbrs[i], ys[i])], "tn", BF16) for i in range(NBR)])

    du, dv, dln_g, dln_b, dsgu_w, dbias = _sgu_bwd(tag + "_dsgu", proj, dys[0], mp["ln_g"], mp["ln_b"], mp["sgu_w"],
                                                  mp["sgu_bias"])
    g["sgu_ln_g"], g["sgu_ln_b"], g["sgu_w"] = dln_g[0], dln_b[0], dsgu_w
    g["sgu_b"] = dbias.reshape(128, 4, 128).sum(axis=2).T

    (dbx, dbg, dcw, dcb, dwa, dba, dwx, dbxb, dlam) = _lru_bwd(
        tag + "_dlru", proj, dys[1], hc, mp["lru_cw"], mp["lru_cb"], mp["wa"], mp["ba"], mp["wx"], mp["bx"], mp["lam"])
    g["lru_conv_w"], g["lru_conv_b"], g["lru_ba"], g["lru_bx"], g["lru_lambda"] = dcw, dcb[0], dba[0], dbxb[0], dlam[0]
    g["lru_wa"], g["lru_wx"] = _block_diag_grad(dwa), _block_diag_grad(dwx)

    *dprep, dz, dng = _gdn_bwd(tag + "_dgdn", *prep, proj, dys[2], sh, mp["ng"])
    dqa, dka, dva, dtail, dalog, ddtb = _gdn_prep_bwd(tag + "_dgdnprep", qa, ka, va, proj, mp["alog"], mp["dtb"], *dprep)
    g["gdn_a_log"], g["gdn_dt_bias"], g["gdn_norm_g"] = dalog[0, 4:8], ddtb[0, 4:8], dng[0]
    dq, dcwq = _conv_bwd(tag + "_dconvq", proj, COL_CQ, dqa, mp["gdn_cw"], 0)
    dk, dcwk = _conv_bwd(tag + "_dconvk", proj, COL_CK, dka, mp["gdn_cw"], 512)
    dv_, dcwv = _conv_bwd(tag + "_dconvv", proj, COL_CV, dva, mp["gdn_cw"], 1024)
    g["gdn_conv_w"] = jnp.concatenate([dcwq, dcwk, dcwv], axis=1)

    dd, dpw, dsc = _pool_bwd(tag + "_dpool", proj, dys[3], mp["pool_w"], mp["pool_sc"])
    g["pool_w"], g["pool_scale"] = dpw, dsc[0]

    dproj = jnp.concatenate([du, dv, dbx, dbg, dq, dk, dv_, dz, dd, *dgates, dtail,
                             jnp.zeros((T, PW - COL_TAIL - 128), BF16)], axis=1)
    dw_in = _mm(tag + "_dwin", [(dproj, h)], "tn", BF16)
    tok = put(dict(w_in=_w_in_from_layout(dw_in), w_branch=g.pop("w_branch"), w_out=g.pop("w_out")))
    dh = _mm(tag + "_dh", [(dproj, p["w_in"])], "nn", F32, bm=_pick(T, (2048, 1024, 512, 256, 128)), after=tok)
    dx, dnorm = _rms_bwd(tag + "_dnorm", x, p["mix_norm"][None] + tok, dh, dx_out)
    g["mix_norm"] = dnorm[0]
    return dx, g


_BIG = ("ff1_wg", "ff1_wu", "ff1_wd", "w_in", "w_branch", "w_out", "ff2_wg", "ff2_wu", "ff2_wd")
_COL_SHARDED = ("ff1_wg", "ff1_wu", "w_in", "w_branch", "ff2_wg", "ff2_wu")
_SMALL = ("ff1_norm", "mix_norm", "sgu_ln_g", "sgu_ln_b", "sgu_w", "sgu_b", "lru_conv_w", "lru_conv_b", "lru_wa",
          "lru_ba", "lru_wx", "lru_bx", "lru_lambda", "gdn_conv_w", "gdn_a_log", "gdn_dt_bias", "gdn_norm_g", "pool_w",
          "pool_scale", "ff2_norm", "final_norm")
_WEIGHTS = ("ff1_norm", "ff1_wg", "ff1_wu", "ff1_wd", "mix_norm", "w_in", "sgu_ln_g", "sgu_ln_b", "sgu_w", "sgu_b",
            "lru_conv_w", "lru_conv_b", "lru_wa", "lru_ba", "lru_wx", "lru_bx", "lru_lambda", "gdn_conv_w", "gdn_a_log",
            "gdn_dt_bias", "gdn_norm_g", "pool_w", "pool_scale", "w_branch", "w_out", "ff2_norm", "ff2_wg", "ff2_wu",
            "ff2_wd", "final_norm")
_CONV_SHARDED = ("lru_conv_w", "gdn_conv_w")
PACK_ROW_ALIGN = 16
_GROUPS = (("ff1", ("ff1_wg", "ff1_wu", "ff1_wd")), ("mix", ("w_in", "w_branch", "w_out")),
           ("ff2", ("ff2_wg", "ff2_wu", "ff2_wd")))


def _pad_rows(a, mult):
    pad = (-a.shape[-2]) % mult
    if pad == 0:
        return a
    return jnp.pad(a, [(0, 0)] * (a.ndim - 2) + [(0, pad), (0, 0)])


def _my_index():
    return 4 * lax.axis_index("x") + 2 * lax.axis_index("y") + lax.axis_index("c")


def _landing(shape, dtype):
    return lax.empty((N_DEV,) + tuple(shape), dtype)


def _stored(n, a):
    return jnp.swapaxes(a, -1, -2) if n in _COL_SHARDED else a


_FIRST = ("ff1_wg", "ff1_wu", "ff1_wd")


def _gather_first(w):
    names = _FIRST
    shards = [_rows(_stored(n, w[n][0]).astype(BF16)) for n in names]
    got = _all_gather("gather_first", jnp.concatenate(shards, axis=0))
    out, r = {}, 0
    for n, s in zip(names, shards):
        out[n] = got[:, r:r + s.shape[0]].reshape(-1, 1024)
        r += s.shape[0]
    return out, got


def _gather_start(w, after):
    conv = _pad_rows(jnp.concatenate([w[n].reshape(1, -1) for n in _CONV_SHARDED], axis=1), 8)
    keys, srcs = ["conv"], [conv]
    for l in range(2):
        for sub, (_, names) in enumerate(_GROUPS):
            for n in names:
                if l > 0 or n not in _FIRST:
                    keys.append((l, sub, n))
                    srcs.append(_stored(n, w[n][l]).astype(BF16))
    lands = [_landing(s.shape, s.dtype) for s in srcs]
    handles, token = _exchange_start("gather_start", srcs, lands, scatter=False, after=after)
    return dict(zip(keys, handles)), token


def _gather_finish(l, sub, handles, first, after):
    names = _GROUPS[sub][1]
    if (l, sub) == (0, 0):
        out = dict(first)
        for n in names:
            if n not in _FIRST:
                out[n] = lambda later, n=n: _parts_to_rows(
                    _exchange_wait(f"gather_wait_00_{n}", [handles[(0, 0, n)]], later, scatter=False)[0])
    elif sub == 1:
        out = {n: (lambda later, n=n: _parts_to_rows(
            _exchange_wait(f"gather_wait_{l}{sub}_{n}", [handles[(l, sub, n)]], later, scatter=False)[0])) for n in names}
        out["w_in"] = out["w_in"](after)
    else:
        lands = _exchange_wait(f"gather_wait_{l}{sub}", [handles[(l, sub, n)] for n in names], after, scatter=False)
        out = {n: _parts_to_rows(land) for n, land in zip(names, lands)}
    if "w_in" in out:
        out["w_in"] = _w_in_to_layout(out["w_in"])
    return out


def _scatter_start(l, sub, grads):
    srcs, shapes = [], []
    for n in grads:
        parts = _rows_to_parts(grads[n])
        shapes.append(parts.shape[1:])
        srcs.append(_pad_rows(parts.reshape(N_DEV, -1, 1024), PACK_ROW_ALIGN))
    lands = [_landing(s.shape[1:], s.dtype) for s in srcs]
    tag = f"{l}{sub}" + ("" if len(grads) == len(_GROUPS[sub][1]) else "_" + "_".join(grads))
    handles, token = _exchange_start(f"scatter_start_{tag}", srcs, lands, scatter=True)
    return handles, (tag, tuple(grads), shapes), token


def _scatter_finish(l, sub, handles, meta, after):
    tag, names, shapes = meta
    lands = _exchange_wait(f"scatter_wait_{tag}", handles, after, scatter=True)
    out = {}
    for n, land, shape in zip(names, lands, shapes):
        size = 1
        for s in shape:
            size *= s
        summed = _sum8(f"sum_{l}{sub}_{n}", land)
        out[n] = _stored(n, summed[:size // 1024].reshape(shape))
    return out


def _gather_conv_finish(w, handles, after):
    gconv = _exchange_wait("gather_wait_conv", [handles["conv"]], after, scatter=False)[0][:, 0]
    full, r = {}, 0
    for n in _CONV_SHARDED:
        sz = w[n].size
        full[n] = _parts_to_cols(gconv[:, r:r + sz].reshape((N_DEV,) + w[n].shape))
        r += sz
    return full


def _forward_backward(x, tgt, w, conv, get_weights, put_grads, put_small, token):
    saved, params = [], []
    for l in range(2):
        p = {n: w[n][l] for n in _SMALL if n != "final_norm"}
        for n in _CONV_SHARDED:
            p[n] = conv[n][l]
        mp = _mixer_params(p)
        tok = token[:1, :1] if l == 0 else 0.0
        p.update(get_weights(l, 0, x))
        x, s1, p["ff1_wd"] = _ffn_forward(f"l{l}_ff1", x, p["ff1_norm"][None] + tok, p["ff1_wg"], p["ff1_wu"],
                                          p["ff1_wd"])
        p.update(get_weights(l, 1, x))
        x, s2 = _mix_forward(f"l{l}_mix", x, p, mp)
        p.update(get_weights(l, 2, x))
        x, s3, _ = _ffn_forward(f"l{l}_ff2", x, p["ff2_norm"][None], p["ff2_wg"], p["ff2_wu"], p["ff2_wd"])
        saved.append((s1, s2, s3))
        params.append((p, mp))
    loss, dx, dfinal = _final_loss("loss_head", x, w["final_norm"][None], tgt)
    tok = 0.0
    for l in (1, 0):
        p, mp = params[l]
        s1, s2, s3 = saved[l]
        g = {}

        def put(sub):
            return lambda grads, l=l: put_grads(l, sub, grads)[:1, :1]

        dx, dn = _ffn_backward(f"l{l}_ff2", dx, s3, p["ff2_norm"][None] + tok, p["ff2_wg"], p["ff2_wu"], p["ff2_wd"],
                               put(2), _GROUPS[2][1])
        g["ff2_norm"] = dn[0]
        dx, gm = _mix_backward(f"l{l}_mix", dx, s2, p, mp, put(1))
        g.update(gm)
        tok = 0.0
        if l == 0:
            tok = put_small("0a", g)[:1, :1]
            g = {}
        dx, dn = _ffn_backward(f"l{l}_ff1", dx, s1, p["ff1_norm"][None] + tok, p["ff1_wg"], p["ff1_wu"], p["ff1_wd"],
                               put(0), _GROUPS[0][1], split=(l == 0))
        g["ff1_norm"] = dn[0]
        if l == 1:
            g["final_norm"] = dfinal[0]
            g["loss"] = loss[0, :1]
        tok = put_small("1" if l == 1 else "0b", g)[:1, :1]
    return dx


SMALL_PIECE = 8 * 1024


def _pack_small(d, names):
    pieces = []
    for n in names:
        flat = d[n].reshape(-1)
        pieces.append(jnp.pad(flat, (0, (-flat.size) % SMALL_PIECE)).reshape(-1, 1024))
    return jnp.concatenate(pieces, axis=0)


def _unpack_small(pack, shapes, names):
    out, r = {}, 0
    for n in names:
        size = 1
        for s in shapes[n]:
            size *= s
        rows = -(-size // SMALL_PIECE) * 8
        out[n] = pack[r:r + rows].reshape(-1)[:size].reshape(shapes[n])
        r += rows
    return out


def _small_names(grads):
    return tuple(n for n in _SMALL + ("loss",) if n in grads)


def _small_start(tag, grads):
    pack = _pack_small(grads, _small_names(grads))
    handles, token = _exchange_start(f"small_start_{tag}", [pack], [_landing(pack.shape, pack.dtype)], scatter=False)
    return handles, {n: grads[n].shape for n in _small_names(grads)}, token


def _small_finish(tag, handles, shapes, after):
    landed = _exchange_wait(f"small_wait_{tag}", handles, after, scatter=False)[0]
    return _unpack_small(_sum8(f"sum_small_{tag}", landed), shapes, _small_names(shapes))


def _as2d(a):
    if a.ndim == 1:
        return a.reshape(1, -1)
    return a.reshape(-1, a.shape[-1])


def kernel(x, ff1_norm, ff1_wg, ff1_wu, ff1_wd, mix_norm, w_in, sgu_ln_g, sgu_ln_b, sgu_w, sgu_b, lru_conv_w, lru_conv_b, lru_wa, lru_ba, lru_wx, lru_bx, lru_lambda, gdn_conv_w, gdn_a_log, gdn_dt_bias, gdn_norm_g, pool_w, pool_scale, w_branch, w_out, ff2_norm, ff2_wg, ff2_wu, ff2_wd, final_norm, loss_target, m_ff1_norm, m_ff1_wg, m_ff1_wu, m_ff1_wd, m_mix_norm, m_w_in, m_sgu_ln_g, m_sgu_ln_b, m_sgu_w, m_sgu_b, m_lru_conv_w, m_lru_conv_b, m_lru_wa, m_lru_ba, m_lru_wx, m_lru_bx, m_lru_lambda, m_gdn_conv_w, m_gdn_a_log, m_gdn_dt_bias, m_gdn_norm_g, m_pool_w, m_pool_scale, m_w_branch, m_w_out, m_ff2_norm, m_ff2_wg, m_ff2_wu, m_ff2_wd, m_final_norm, v_ff1_norm, v_ff1_wg, v_ff1_wu, v_ff1_wd, v_mix_norm, v_w_in, v_sgu_ln_g, v_sgu_ln_b, v_sgu_w, v_sgu_b, v_lru_conv_w, v_lru_conv_b, v_lru_wa, v_lru_ba, v_lru_wx, v_lru_bx, v_lru_lambda, v_gdn_conv_w, v_gdn_a_log, v_gdn_dt_bias, v_gdn_norm_g, v_pool_w, v_pool_scale, v_w_branch, v_w_out, v_ff2_norm, v_ff2_wg, v_ff2_wu, v_ff2_wd, v_final_norm):
    w = dict(ff1_norm=ff1_norm, ff1_wg=ff1_wg, ff1_wu=ff1_wu, ff1_wd=ff1_wd, mix_norm=mix_norm, w_in=w_in,
             sgu_ln_g=sgu_ln_g, sgu_ln_b=sgu_ln_b, sgu_w=sgu_w, sgu_b=sgu_b, lru_conv_w=lru_conv_w,
             lru_conv_b=lru_conv_b, lru_wa=lru_wa, lru_ba=lru_ba, lru_wx=lru_wx, lru_bx=lru_bx, lru_lambda=lru_lambda,
             gdn_conv_w=gdn_conv_w, gdn_a_log=gdn_a_log, gdn_dt_bias=gdn_dt_bias, gdn_norm_g=gdn_norm_g, pool_w=pool_w,
             pool_scale=pool_scale, w_branch=w_branch, w_out=w_out, ff2_norm=ff2_norm, ff2_wg=ff2_wg, ff2_wu=ff2_wu,
             ff2_wd=ff2_wd, final_norm=final_norm)
    m = dict(ff1_norm=m_ff1_norm, ff1_wg=m_ff1_wg, ff1_wu=m_ff1_wu, ff1_wd=m_ff1_wd, mix_norm=m_mix_norm, w_in=m_w_in,
             sgu_ln_g=m_sgu_ln_g, sgu_ln_b=m_sgu_ln_b, sgu_w=m_sgu_w, sgu_b=m_sgu_b, lru_conv_w=m_lru_conv_w,
             lru_conv_b=m_lru_conv_b, lru_wa=m_lru_wa, lru_ba=m_lru_ba, lru_wx=m_lru_wx, lru_bx=m_lru_bx,
             lru_lambda=m_lru_lambda, gdn_conv_w=m_gdn_conv_w, gdn_a_log=m_gdn_a_log, gdn_dt_bias=m_gdn_dt_bias,
             gdn_norm_g=m_gdn_norm_g, pool_w=m_pool_w, pool_scale=m_pool_scale, w_branch=m_w_branch, w_out=m_w_out,
             ff2_norm=m_ff2_norm, ff2_wg=m_ff2_wg, ff2_wu=m_ff2_wu, ff2_wd=m_ff2_wd, final_norm=m_final_norm)
    v = dict(ff1_norm=v_ff1_norm, ff1_wg=v_ff1_wg, ff1_wu=v_ff1_wu, ff1_wd=v_ff1_wd, mix_norm=v_mix_norm, w_in=v_w_in,
             sgu_ln_g=v_sgu_ln_g, sgu_ln_b=v_sgu_ln_b, sgu_w=v_sgu_w, sgu_b=v_sgu_b, lru_conv_w=v_lru_conv_w,
             lru_conv_b=v_lru_conv_b, lru_wa=v_lru_wa, lru_ba=v_lru_ba, lru_wx=v_lru_wx, lru_bx=v_lru_bx,
             lru_lambda=v_lru_lambda, gdn_conv_w=v_gdn_conv_w, gdn_a_log=v_gdn_a_log, gdn_dt_bias=v_gdn_dt_bias,
             gdn_norm_g=v_gdn_norm_g, pool_w=v_pool_w, pool_scale=v_pool_scale, w_branch=v_w_branch, w_out=v_w_out,
             ff2_norm=v_ff2_norm, ff2_wg=v_ff2_wg, ff2_wu=v_ff2_wu, ff2_wd=v_ff2_wd, final_norm=v_final_norm)

    first, got_first = _gather_first(w)
    handles, token = _gather_start(w, got_first)
    conv = _gather_conv_finish(w, handles, token)
    pending = {}

    def get_weights(l, sub, after):
        return _gather_finish(l, sub, handles, first, after)

    def put_grads(l, sub, grads):
        hs, meta, tok = _scatter_start(l, sub, grads)
        pending[(l, sub, meta[0])] = (hs, meta)
        return tok

    def put_small(tag, grads):
        hs, shapes, tok = _small_start(tag, grads)
        pending[tag] = (hs, shapes)
        return tok

    T = x.shape[1]
    dx = _forward_backward(x.reshape(T, D), loss_target.reshape(T, D), w, conv, get_weights, put_grads, put_small,
                           token)
    per = {}
    for key in pending:
        if isinstance(key, tuple):
            per.setdefault(key[:2], {}).update(_scatter_finish(*key[:2], *pending[key], dx))
        else:
            per[key] = _small_finish(key, *pending[key], dx)
    grad = {n: jnp.stack([per[(0, sub)][n], per[(1, sub)][n]]) for sub, (_, names) in enumerate(_GROUPS) for n in names}
    layer0 = {**per["0a"], **per["0b"]}
    small = {n: _join([layer0[n].reshape(-1), per["1"][n].reshape(-1)]).reshape((2,) + layer0[n].shape)
             for n in layer0}
    small["final_norm"] = per["1"]["final_norm"]
    loss = per["1"]["loss"][0]
    me = _my_index()
    for n in _SMALL:
        if n in _CONV_SHARDED:
            width = w[n].shape[-1]
            grad[n] = lax.dynamic_slice_in_dim(small[n], me * width, width, axis=2)
        else:
            grad[n] = small[n]

    delta, new_m, new_v = {}, {}, {}
    for n in _BIG:
        d_, m_, v_ = _adamw("adamw_" + n, _as2d(w[n]), _as2d(grad[n]), _as2d(m[n]), _as2d(v[n]))
        delta[n], new_m[n], new_v[n] = (t.reshape(w[n].shape) for t in (d_, m_, v_))

    outs = _adamw_many("adamw_small", *[[_as2d(t[n]) for n in _SMALL] for t in (w, grad, m, v)])
    for k, dst in enumerate((delta, new_m, new_v)):
        for i, n in enumerate(_SMALL):
            dst[n] = outs[k * len(_SMALL) + i].reshape(w[n].shape)

    return (loss, dx.reshape(x.shape), *[grad[n] for n in _WEIGHTS], *[delta[n] for n in _WEIGHTS],
            *[new_m[n] for n in _WEIGHTS], *[new_v[n] for n in _WEIGHTS])
```

```python
import functools

import jax
import jax.numpy as jnp
from jax import lax
from jax.experimental import pallas as pl
from jax.experimental.pallas import tpu as pltpu

F32 = jnp.float32
BF16 = jnp.bfloat16
HI = lax.Precision.HIGHEST

N_DEV = 8
D = 1024
FF = 2816
BW = 512
NBR = 4
CHUNK = 64
EPS = 1e-6
LRU_C = 8.0
GDN_DK = 128

COL_AU, COL_AV, COL_BX, COL_BG = 0, 512, 1024, 1536
COL_CQ, COL_CK, COL_CV, COL_CZ = 2048, 2560, 3072, 3584
COL_DX, COL_GATE, COL_TAIL = 4096, 4608, 8704
PW = 9216
P_IN = 8712

ADAM_LR, ADAM_B1, ADAM_B2, ADAM_EPS, ADAM_WD, ADAM_STEP = 0.001, 0.9, 0.999, 1e-08, 0.01, 10

VMEM_LIMIT_V7X = 56 * 1024 * 1024

_NN = (((1,), (0,)), ((), ()))
_NT = (((1,), (1,)), ((), ()))
_TN = (((0,), (0,)), ((), ()))


def _cp(*sem):
    return pltpu.CompilerParams(dimension_semantics=tuple(sem), vmem_limit_bytes=VMEM_LIMIT_V7X)


def _dot(a, b, dims=_NN):
    return lax.dot_general(a.astype(BF16), b.astype(BF16), dims, preferred_element_type=F32)


def _dot_hi(a, b, dims=_NN):
    return lax.dot_general(a, b, dims, precision=HI, preferred_element_type=F32)


def _pick(n, cands):
    for c in cands:
        if n % c == 0:
            return c
    return n


@jax.custom_jvp
def _log1p(x):
    u = 1.0 + x
    return jnp.where(u == 1.0, x, x * jnp.log(u) / jnp.where(u == 1.0, 1.0, u - 1.0))


@_log1p.defjvp
def _log1p_jvp(p, t):
    (x,), (dx,) = p, t
    return _log1p(x), dx / (1.0 + x)


@jax.custom_jvp
def _expm1(x):
    u = jnp.exp(x)
    lu = jnp.log(u)
    small = (u == 1.0) | (lu == 0.0)
    return jnp.where(small, x, (u - 1.0) * x / jnp.where(small, 1.0, lu))


@_expm1.defjvp
def _expm1_jvp(p, t):
    (x,), (dx,) = p, t
    return _expm1(x), dx * jnp.exp(x)


def _softplus(x):
    return jnp.maximum(x, 0.0) + _log1p(jnp.exp(-jnp.abs(x)))


def _sigmoid(x):
    return jax.nn.sigmoid(x)


def _silu(x):
    return x * jax.nn.sigmoid(x)


def _gelu(x):
    return jax.nn.gelu(x)


@functools.partial(jax.custom_vjp, nondiff_argnums=(1,))
def _shift(x, s):
    return x if s == 0 else pltpu.roll(x, s, 0)


def _shift_fwd(x, s):
    return _shift(x, s), None


def _shift_bwd(s, _, g):
    n = g.shape[0]
    return (g if s == 0 else pltpu.roll(g, n - s, 0),)


_shift.defvjp(_shift_fwd, _shift_bwd)


def _scan_steps(a, b, reverse):
    n = a.shape[0]
    row = lax.broadcasted_iota(jnp.int32, a.shape, 0)
    k = 1
    while k < n:
        sh = n - k if reverse else k
        m = (row < n - k) if reverse else (row >= k)
        a_s = jnp.where(m, pltpu.roll(a, sh, 0), 1.0)
        b_s = jnp.where(m, pltpu.roll(b, sh, 0), 0.0)
        b = a * b_s + b
        a = a * a_s
        k *= 2
    return b


@jax.custom_vjp
def _scan(a, b):
    return _scan_steps(a, b, False)


def _scan_fwd(a, b):
    h = _scan_steps(a, b, False)
    return h, (a, h)


def _scan_bwd(res, dh):
    a, h = res
    n = a.shape[0]
    row = lax.broadcasted_iota(jnp.int32, a.shape, 0)
    a_next = jnp.where(row < n - 1, pltpu.roll(a, n - 1, 0), 0.0)
    g = _scan_steps(a_next, dh, True)
    h_prev = jnp.where(row >= 1, pltpu.roll(h, 1, 0), 0.0)
    return g * h_prev, g


_scan.defvjp(_scan_fwd, _scan_bwd)


def _mm(name, pairs, mode, out_dtype, *, res=None, scale=1.0, bm=None, bn=None, bk=None, after=None):
    a0, b0 = pairs[0]
    if mode == "nn":
        (M, K), N = a0.shape, b0.shape[1]
    elif mode == "nt":
        (M, K), N = a0.shape, b0.shape[0]
    else:
        (K, M), N = a0.shape, b0.shape[1]
    bm = bm or _pick(M, (1024, 512, 256, 128))
    bn = bn or _pick(N, (1024, 512, 256, 128))
    bk = bk or _pick(K, (1024, 512, 1408, 256, 128))
    nk = K // bk
    npair = len(pairs)
    dims = {"nn": _NN, "nt": _NT, "tn": _TN}[mode]

    def body(*refs):
        ab = refs[:2 * npair]
        pos = 2 * npair
        r_ref = None
        if res is not None:
            r_ref = refs[pos]
            pos += 1
        pos += after is not None
        o_ref = refs[pos]
        part = None
        for p in range(npair):
            d = _dot(ab[2 * p][...], ab[2 * p + 1][...], dims)
            part = d if part is None else part + d

        def finish(acc):
            out = acc if scale == 1.0 else acc * scale
            if r_ref is not None:
                out = out + r_ref[...]
            o_ref[...] = out.astype(out_dtype)

        if nk == 1:
            finish(part)
        else:
            acc_ref = refs[pos + 1]
            k = pl.program_id(2)

            @pl.when(k == 0)
            def _():
                acc_ref[...] = part

            @pl.when(k > 0)
            def _():
                acc_ref[...] += part

            @pl.when(k == nk - 1)
            def _():
                finish(acc_ref[...])

    if mode == "nn":
        a_spec = pl.BlockSpec((bm, bk), lambda i, j, k: (i, k))
        b_spec = pl.BlockSpec((bk, bn), lambda i, j, k: (k, j))
    elif mode == "nt":
        a_spec = pl.BlockSpec((bm, bk), lambda i, j, k: (i, k))
        b_spec = pl.BlockSpec((bn, bk), lambda i, j, k: (j, k))
    else:
        a_spec = pl.BlockSpec((bk, bm), lambda i, j, k: (k, i))
        b_spec = pl.BlockSpec((bk, bn), lambda i, j, k: (k, j))
    o_spec = pl.BlockSpec((bm, bn), lambda i, j, k: (i, j))
    in_specs, args = [], []
    for a, b in pairs:
        in_specs += [a_spec, b_spec]
        args += [a, b]
    if res is not None:
        in_specs.append(o_spec)
        args.append(res)
    if after is not None:
        in_specs.append(_ANY)
        args.append(after)
    return pl.pallas_call(
        body, name=name, grid=(M // bm, N // bn, nk),
        in_specs=in_specs, out_specs=o_spec,
        out_shape=jax.ShapeDtypeStruct((M, N), out_dtype),
        scratch_shapes=[pltpu.VMEM((bm, bn), F32)] if nk > 1 else [],
        compiler_params=_cp("parallel", "parallel", "arbitrary"),
    )(*args)


def _rms_fwd(name, x, g):
    T = x.shape[0]
    bm = _pick(T, (512, 256, 128))

    def body(x_ref, g_ref, o_ref):
        xv = x_ref[...]
        r = lax.rsqrt(jnp.mean(xv * xv, axis=-1, keepdims=True) + EPS)
        o_ref[...] = (xv * r * g_ref[...]).astype(BF16)

    return pl.pallas_call(
        body, name=name, grid=(T // bm,),
        in_specs=[pl.BlockSpec((bm, D), lambda i: (i, 0)), pl.BlockSpec((1, D), lambda i: (0, 0))],
        out_specs=pl.BlockSpec((bm, D), lambda i: (i, 0)),
        out_shape=jax.ShapeDtypeStruct((T, D), BF16),
        compiler_params=_cp("parallel"),
    )(x, g)


def _rms_bwd(name, x, g, dh, dres):
    T = x.shape[0]
    bm = _pick(T, (512, 256, 128))

    def body(x_ref, g_ref, dh_ref, dres_ref, dx_ref, dg_ref):
        xv = x_ref[...]
        r = lax.rsqrt(jnp.mean(xv * xv, axis=-1, keepdims=True) + EPS)
        xh = xv * r
        dhv = dh_ref[...]
        dxh = dhv * g_ref[...]
        dx_ref[...] = dres_ref[...] + r * (dxh - xh * jnp.mean(dxh * xh, axis=-1, keepdims=True))
        part = jnp.sum(dhv * xh, axis=0, keepdims=True)

        @pl.when(pl.program_id(0) == 0)
        def _():
            dg_ref[...] = part

        @pl.when(pl.program_id(0) > 0)
        def _():
            dg_ref[...] += part

    row = pl.BlockSpec((bm, D), lambda i: (i, 0))
    vec = pl.BlockSpec((1, D), lambda i: (0, 0))
    return pl.pallas_call(
        body, name=name, grid=(T // bm,),
        in_specs=[row, vec, row, row], out_specs=[row, vec],
        out_shape=[jax.ShapeDtypeStruct((T, D), F32), jax.ShapeDtypeStruct((1, D), F32)],
        compiler_params=_cp("arbitrary"),
    )(x, g, dh, dres)


def _final_loss(name, x, g, tgt):
    T = x.shape[0]
    bm = _pick(T, (512, 256, 128))

    def body(x_ref, g_ref, t_ref, loss_ref, dx_ref, dg_ref):
        xv = x_ref[...]
        gv = g_ref[...]
        r = lax.rsqrt(jnp.mean(xv * xv, axis=-1, keepdims=True) + EPS)
        xh = xv * r
        e = xh * gv - t_ref[...]
        lpart = jnp.broadcast_to(0.5 * jnp.sum(jnp.mean(e * e, axis=-1, keepdims=True), axis=0, keepdims=True), (1, 128))
        dy = e * (1.0 / D)
        dxh = dy * gv
        dx_ref[...] = r * (dxh - xh * jnp.mean(dxh * xh, axis=-1, keepdims=True))
        gpart = jnp.sum(dy * xh, axis=0, keepdims=True)

        @pl.when(pl.program_id(0) == 0)
        def _():
            loss_ref[...] = lpart
            dg_ref[...] = gpart

        @pl.when(pl.program_id(0) > 0)
        def _():
            loss_ref[...] += lpart
            dg_ref[...] += gpart

    row = pl.BlockSpec((bm, D), lambda i: (i, 0))
    vec = pl.BlockSpec((1, D), lambda i: (0, 0))
    return pl.pallas_call(
        body, name=name, grid=(T // bm,),
        in_specs=[row, vec, row],
        out_specs=[pl.BlockSpec((1, 128), lambda i: (0, 0)), row, vec],
        out_shape=[jax.ShapeDtypeStruct((1, 128), F32), jax.ShapeDtypeStruct((T, D), F32),
                   jax.ShapeDtypeStruct((1, D), F32)],
        compiler_params=_cp("arbitrary"),
    )(x, g, tgt)


def _ffn_up(name, h, wg, wu):
    T = h.shape[0]
    bm = _pick(T, (2048, 1024, 512, 256, 128))
    bn = 256

    def body(h_ref, wg_ref, wu_ref, sa_ref, ds_ref, act_ref):
        hv = h_ref[...]
        a = _dot(hv, wg_ref[...], _NT)
        b = _dot(hv, wu_ref[...], _NT)
        s = _sigmoid(a)
        sa = a * s
        sa_ref[...] = sa.astype(BF16)
        ds_ref[...] = (b * (s * (1.0 + a * (1.0 - s)))).astype(BF16)
        act_ref[...] = (sa * b).astype(BF16)

    w_spec = pl.BlockSpec((bn, D), lambda i, j: (j, 0))
    o_spec = pl.BlockSpec((bm, bn), lambda i, j: (i, j))
    return pl.pallas_call(
        body, name=name, grid=(T // bm, FF // bn),
        in_specs=[pl.BlockSpec((bm, D), lambda i, j: (i, 0)), w_spec, w_spec],
        out_specs=[o_spec, o_spec, o_spec],
        out_shape=[jax.ShapeDtypeStruct((T, FF), BF16)] * 3,
        compiler_params=_cp("parallel", "parallel"),
    )(h, wg, wu)


def _ffn_dact(name, dy, wd, sa, ds, after=None):
    T = dy.shape[0]
    bm = _pick(T, (2048, 1024, 512, 256, 128))
    bn = 256

    def body(dy_ref, wd_ref, sa_ref, ds_ref, *rest):
        da_ref, db_ref, dy_bf = rest[-3:]

        @pl.when(pl.program_id(1) == 0)
        def _():
            dy_bf[...] = dy_ref[...].astype(BF16)

        dact = 0.5 * _dot(dy_bf[...], wd_ref[...], _NT)
        da_ref[...] = (dact * ds_ref[...].astype(F32)).astype(BF16)
        db_ref[...] = (dact * sa_ref[...].astype(F32)).astype(BF16)

    t_spec = pl.BlockSpec((bm, bn), lambda i, j: (i, j))
    return pl.pallas_call(
        body, name=name, grid=(T // bm, FF // bn),
        in_specs=[pl.BlockSpec((bm, D), lambda i, j: (i, 0)), pl.BlockSpec((bn, D), lambda i, j: (j, 0)),
                  t_spec, t_spec] + [_ANY] * (after is not None),
        out_specs=[t_spec, t_spec],
        out_shape=[jax.ShapeDtypeStruct((T, FF), BF16), jax.ShapeDtypeStruct((T, FF), BF16)],
        scratch_shapes=[pltpu.VMEM((bm, D), BF16)],
        compiler_params=_cp("parallel", "arbitrary"),
    )(dy, wd, sa, ds, *([after] if after is not None else []))


def _merge_specs(T, bm, bn):
    y_spec = pl.BlockSpec((bm, BW), lambda i, j: (i, 0))
    wb_spec = pl.BlockSpec((NBR, bn, BW), lambda i, j: (0, j, 0))
    gate_specs = [pl.BlockSpec((bm, bn), functools.partial(lambda i, j, o: (i, o + j), o=(COL_GATE + g * D) // bn))
                  for g in range(NBR)]
    t_spec = pl.BlockSpec((bm, bn), lambda i, j: (i, j))
    return y_spec, wb_spec, gate_specs, t_spec


def _merge_fwd(name, ys, wb, proj):
    T = proj.shape[0]
    bm = _pick(T, (512, 256, 128))
    bn = 512
    y_spec, wb_spec, gate_specs, t_spec = _merge_specs(T, bm, bn)

    def body(y0, y1, y2, y3, wb_ref, g0, g1, g2, g3, o_ref):
        acc = None
        for g, (y_ref, g_ref) in enumerate(((y0, g0), (y1, g1), (y2, g2), (y3, g3))):
            t = _sigmoid(g_ref[...].astype(F32)) * _dot(y_ref[...], wb_ref[g], _NT)
            acc = t if acc is None else acc + t
        o_ref[...] = acc.astype(BF16)

    return pl.pallas_call(
        body, name=name, grid=(T // bm, D // bn),
        in_specs=[y_spec] * NBR + [wb_spec] + gate_specs, out_specs=t_spec,
        out_shape=jax.ShapeDtypeStruct((T, D), BF16),
        compiler_params=_cp("parallel", "parallel"),
    )(*ys, wb, proj, proj, proj, proj)


def _merge_bwd(name, dm, ys, wb, proj):
    T = proj.shape[0]
    bm = _pick(T, (512, 256, 128))
    bn = 512
    y_spec, wb_spec, gate_specs, t_spec = _merge_specs(T, bm, bn)

    def body(dm_ref, y0, y1, y2, y3, wb_ref, g0, g1, g2, g3, *outs):
        dmv = dm_ref[...]
        j = pl.program_id(1)
        for g, (y_ref, g_ref) in enumerate(((y0, g0), (y1, g1), (y2, g2), (y3, g3))):
            br = _dot(y_ref[...], wb_ref[g], _NT)
            s = _sigmoid(g_ref[...].astype(F32))
            outs[g][...] = (dmv * br * (s * (1.0 - s))).astype(BF16)
            dbr = (dmv * s).astype(BF16)
            outs[NBR + g][...] = dbr
            part = _dot(dbr, wb_ref[g])
            dy_ref = outs[2 * NBR + g]

            @pl.when(j == 0)
            def _():
                dy_ref[...] = part

            @pl.when(j > 0)
            def _():
                dy_ref[...] += part

    return pl.pallas_call(
        body, name=name, grid=(T // bm, D // bn),
        in_specs=[t_spec] + [y_spec] * NBR + [wb_spec] + gate_specs, out_specs=[t_spec] * (2 * NBR) + [y_spec] * NBR,
        out_shape=[jax.ShapeDtypeStruct((T, D), BF16)] * (2 * NBR) + [jax.ShapeDtypeStruct((T, BW), F32)] * NBR,
        compiler_params=_cp("parallel", "arbitrary"),
    )(dm, *ys, wb, proj, proj, proj, proj)


def _sgu_block(u_pre, v_pre, ln_g, ln_b, w, bias):
    u = _gelu(u_pre)
    vf = _gelu(v_pre)
    mu = jnp.mean(vf, axis=-1, keepdims=True)
    var = jnp.mean(jnp.square(vf - mu), axis=-1, keepdims=True)
    vn = (vf - mu) * lax.rsqrt(var + EPS) * ln_g + ln_b
    ri = lax.broadcasted_iota(jnp.int32, (128, 128), 0)
    ci = lax.broadcasted_iota(jnp.int32, (128, 128), 1)
    mask = (ri // CHUNK) >= (ci // CHUNK)
    outs = [_dot(jnp.where(mask, w[g], 0.0), vn[:, g * 128:(g + 1) * 128]) for g in range(4)]
    mixed = jnp.concatenate(outs, axis=1) + bias
    return u * mixed


def _sgu_param_specs():
    return [pl.BlockSpec((1, BW), lambda i: (0, 0)), pl.BlockSpec((1, BW), lambda i: (0, 0)),
            pl.BlockSpec((4, 128, 128), lambda i: (0, 0, 0)), pl.BlockSpec((128, BW), lambda i: (0, 0))]


def _sgu_fwd(name, proj, ln_g, ln_b, w, bias):
    T = proj.shape[0]
    rb = _pick(T, (256, 128))

    def body(u_ref, v_ref, g_ref, b_ref, w_ref, bias_ref, y_ref):
        for n in range(rb // 128):
            rows = slice(n * 128, (n + 1) * 128)
            y = _sgu_block(u_ref[rows, :].astype(F32), v_ref[rows, :].astype(F32), g_ref[...], b_ref[...], w_ref[...],
                           bias_ref[...])
            y_ref[rows, :] = y.astype(BF16)

    return pl.pallas_call(
        body, name=name, grid=(T // rb,),
        in_specs=[pl.BlockSpec((rb, BW), lambda i: (i, COL_AU // BW)), pl.BlockSpec((rb, BW), lambda i: (i, COL_AV // BW))]
        + _sgu_param_specs(),
        out_specs=pl.BlockSpec((rb, BW), lambda i: (i, 0)),
        out_shape=jax.ShapeDtypeStruct((T, BW), BF16),
        compiler_params=_cp("parallel"),
    )(proj, proj, ln_g, ln_b, w, bias)


def _sgu_bwd(name, proj, dy, ln_g, ln_b, w, bias):
    T = proj.shape[0]
    rb = _pick(T, (256, 128))

    def body(u_ref, v_ref, dy_ref, g_ref, b_ref, w_ref, bias_ref, du_ref, dv_ref, dg_ref, db_ref, dw_ref, dbias_ref):
        acc = None
        for n in range(rb // 128):
            rows = slice(n * 128, (n + 1) * 128)
            _, vjp = jax.vjp(_sgu_block, u_ref[rows, :].astype(F32), v_ref[rows, :].astype(F32), g_ref[...], b_ref[...],
                             w_ref[...],
                             bias_ref[...])
            du, dv, *dp = vjp(dy_ref[rows, :])
            du_ref[rows, :] = du.astype(BF16)
            dv_ref[rows, :] = dv.astype(BF16)
            acc = dp if acc is None else [p + q for p, q in zip(acc, dp)]

        @pl.when(pl.program_id(0) == 0)
        def _():
            for r, p in zip((dg_ref, db_ref, dw_ref, dbias_ref), acc):
                r[...] = p

        @pl.when(pl.program_id(0) > 0)
        def _():
            for r, p in zip((dg_ref, db_ref, dw_ref, dbias_ref), acc):
                r[...] += p

    row = pl.BlockSpec((rb, BW), lambda i: (i, 0))
    return pl.pallas_call(
        body, name=name, grid=(T // rb,),
        in_specs=[pl.BlockSpec((rb, BW), lambda i: (i, COL_AU // BW)), pl.BlockSpec((rb, BW), lambda i: (i, COL_AV // BW)),
                  row] + _sgu_param_specs(),
        out_specs=[row, row] + _sgu_param_specs(),
        out_shape=[jax.ShapeDtypeStruct((T, BW), BF16), jax.ShapeDtypeStruct((T, BW), BF16),
                   jax.ShapeDtypeStruct((1, BW), F32), jax.ShapeDtypeStruct((1, BW), F32),
                   jax.ShapeDtypeStruct((4, 128, 128), F32), jax.ShapeDtypeStruct((128, BW), F32)],
        compiler_params=_cp("arbitrary"),
    )(proj, proj, dy, ln_g, ln_b, w, bias)


def _halo_block(ref, i, rblk, halo):
    r0 = pl.multiple_of(i * rblk, rblk)
    h0 = pl.multiple_of(jnp.maximum(r0 - 16, 0), 16)
    top = jnp.where(i > 0, ref[pl.ds(h0, 16), :].astype(F32), 0.0)[16 - halo:]
    return jnp.concatenate([top, ref[pl.ds(r0, rblk), :].astype(F32)], axis=0)


def _with_halo_grad(dfull, pending, halo, rblk):
    tail = jnp.concatenate([jnp.zeros((rblk - halo, 128), F32), pending], axis=0)
    return dfull[halo:] + tail


def _conv4(xfull, rows):
    acc = None
    for k in range(4):
        t = rows[k] * _shift(xfull, 3 - k)[8:]
        acc = t if acc is None else acc + t
    return acc


def _lru_block(xfull, gate, h0, c0, c1, c2, c3, cb, wa, ba, wx, bx, lam):
    n = gate.shape[0]
    xc = _conv4(xfull, (c0, c1, c2, c3)) + cb
    r = _sigmoid(_dot(xc, wa) + ba)
    ig = _sigmoid(_dot(xc, wx) + bx)
    log_a = -LRU_C * r * _softplus(-lam)
    a = jnp.exp(log_a)
    mult = jnp.sqrt(-_expm1(2.0 * log_a))
    b = mult * (ig * xc)
    row = lax.broadcasted_iota(jnp.int32, (n, 128), 0)
    b = b + jnp.where(row == 0, a * h0, 0.0)
    h = _scan(a, b)
    out = h * _gelu(gate)
    h_last = jnp.sum(jnp.where(row == n - 1, h, 0.0), axis=0, keepdims=True)
    return out, h_last


def _lru_param_specs():
    vec = pl.BlockSpec((1, 128), lambda g: (0, g))
    mat = pl.BlockSpec((None, 128, 128), lambda g: (g, 0, 0))
    return [pl.BlockSpec((4, 128), lambda g: (0, g)), vec, mat, vec, mat, vec, vec]


def _lru_load_params(cw_ref, cb_ref, wa_ref, ba_ref, wx_ref, bx_ref, lam_ref):
    return (cw_ref[0:1, :], cw_ref[1:2, :], cw_ref[2:3, :], cw_ref[3:4, :], cb_ref[...], wa_ref[...], ba_ref[...],
            wx_ref[...], bx_ref[...], lam_ref[...])


def _lru_fwd(name, proj, cw, cb, wa, ba, wx, bx, lam):
    T = proj.shape[0]
    rblk = _pick(T, (256, 128))
    nblk = T // rblk

    def body(x_ref, gt_ref, cw_ref, cb_ref, wa_ref, ba_ref, wx_ref, bx_ref, lam_ref, y_ref, hc_ref):
        params = _lru_load_params(cw_ref, cb_ref, wa_ref, ba_ref, wx_ref, bx_ref, lam_ref)

        def step(i, h0):
            r0 = pl.multiple_of(i * rblk, rblk)
            out, h_last = _lru_block(_halo_block(x_ref, i, rblk, 8), gt_ref[pl.ds(r0, rblk), :].astype(F32), h0,
                                     *params)
            y_ref[pl.ds(r0, rblk), :] = out.astype(BF16)
            hc_ref[pl.ds(pl.multiple_of(i * 8, 8), 8), :] = jnp.broadcast_to(h0, (8, 128))
            return h_last

        lax.fori_loop(0, nblk, step, jnp.zeros((1, 128), F32))

    return pl.pallas_call(
        body, name=name, grid=(4,),
        in_specs=[pl.BlockSpec((T, 128), lambda g: (0, COL_BX // 128 + g)),
                  pl.BlockSpec((T, 128), lambda g: (0, COL_BG // 128 + g))] + _lru_param_specs(),
        out_specs=[pl.BlockSpec((T, 128), lambda g: (0, g)), pl.BlockSpec((nblk * 8, 128), lambda g: (0, g))],
        out_shape=[jax.ShapeDtypeStruct((T, BW), BF16), jax.ShapeDtypeStruct((nblk * 8, BW), F32)],
        compiler_params=_cp("parallel"),
    )(proj, proj, cw, cb, wa, ba, wx, bx, lam)


def _lru_bwd(name, proj, dy, hc, cw, cb, wa, ba, wx, bx, lam):
    T = proj.shape[0]
    rblk = _pick(T, (256, 128))
    nblk = T // rblk

    def body(x_ref, gt_ref, dy_ref, hc_ref, cw_ref, cb_ref, wa_ref, ba_ref, wx_ref, bx_ref, lam_ref,
             dx_ref, dgt_ref, dcw_ref, dcb_ref, dwa_ref, dba_ref, dwx_ref, dbx_ref, dlam_ref):
        params = _lru_load_params(cw_ref, cb_ref, wa_ref, ba_ref, wx_ref, bx_ref, lam_ref)

        def step(it, carry):
            dh_last, pending, acc = carry
            i = nblk - 1 - it
            r0 = pl.multiple_of(i * rblk, rblk)
            h0 = hc_ref[pl.ds(pl.multiple_of(i * 8, 8), 1), :]
            _, vjp = jax.vjp(_lru_block, _halo_block(x_ref, i, rblk, 8), gt_ref[pl.ds(r0, rblk), :].astype(F32), h0,
                             *params)
            dfull, dgate, dh0, *dp = vjp((dy_ref[pl.ds(r0, rblk), :], dh_last))
            dx_ref[pl.ds(r0, rblk), :] = _with_halo_grad(dfull, pending, 8, rblk).astype(BF16)
            dgt_ref[pl.ds(r0, rblk), :] = dgate.astype(BF16)
            return dh0, dfull[:8], tuple(p + q for p, q in zip(acc, dp))

        zeros = tuple(jnp.zeros(p.shape, F32) for p in params)
        _, _, acc = lax.fori_loop(0, nblk, step, (jnp.zeros((1, 128), F32), jnp.zeros((8, 128), F32), zeros))
        for k in range(4):
            dcw_ref[k:k + 1, :] = acc[k]
        for r, p in zip((dcb_ref, dwa_ref, dba_ref, dwx_ref, dbx_ref, dlam_ref), acc[4:]):
            r[...] = p

    col = pl.BlockSpec((T, 128), lambda g: (0, g))
    return pl.pallas_call(
        body, name=name, grid=(4,),
        in_specs=[pl.BlockSpec((T, 128), lambda g: (0, COL_BX // 128 + g)),
                  pl.BlockSpec((T, 128), lambda g: (0, COL_BG // 128 + g)), col,
                  pl.BlockSpec((nblk * 8, 128), lambda g: (0, g))] + _lru_param_specs(),
        out_specs=[col, col] + _lru_param_specs(),
        out_shape=[jax.ShapeDtypeStruct((T, BW), BF16), jax.ShapeDtypeStruct((T, BW), BF16),
                   jax.ShapeDtypeStruct((4, BW), F32), jax.ShapeDtypeStruct((1, BW), F32),
                   jax.ShapeDtypeStruct((4, 128, 128), F32), jax.ShapeDtypeStruct((1, BW), F32),
                   jax.ShapeDtypeStruct((4, 128, 128), F32), jax.ShapeDtypeStruct((1, BW), F32),
                   jax.ShapeDtypeStruct((1, BW), F32)],
        compiler_params=_cp("parallel"),
    )(proj, proj, dy, hc, cw, cb, wa, ba, wx, bx, lam)


def _conv_block(xfull, c0, c1, c2, c3):
    return _silu(_conv4(xfull, (c0, c1, c2, c3)))


def _conv_fwd(name, proj, col0, cw, cw_col0):
    T = proj.shape[0]
    rblk = _pick(T, (256, 128))
    nblk = T // rblk

    def body(x_ref, cw_ref, y_ref):
        rows = (cw_ref[0:1, :], cw_ref[1:2, :], cw_ref[2:3, :], cw_ref[3:4, :])

        def step(i, c):
            r0 = pl.multiple_of(i * rblk, rblk)
            y_ref[pl.ds(r0, rblk), :] = _conv_block(_halo_block(x_ref, i, rblk, 8), *rows)
            return c

        lax.fori_loop(0, nblk, step, 0)

    return pl.pallas_call(
        body, name=name, grid=(4,),
        in_specs=[pl.BlockSpec((T, 128), lambda g: (0, col0 // 128 + g)),
                  pl.BlockSpec((4, 128), lambda g: (0, cw_col0 // 128 + g))],
        out_specs=pl.BlockSpec((T, 128), lambda g: (0, g)),
        out_shape=jax.ShapeDtypeStruct((T, BW), F32),
        compiler_params=_cp("parallel"),
    )(proj, cw)


def _conv_bwd(name, proj, col0, dy, cw, cw_col0):
    T = proj.shape[0]
    rblk = _pick(T, (256, 128))
    nblk = T // rblk

    def body(x_ref, dy_ref, cw_ref, dx_ref, dcw_ref):
        rows = (cw_ref[0:1, :], cw_ref[1:2, :], cw_ref[2:3, :], cw_ref[3:4, :])

        def step(it, carry):
            pending, acc = carry
            i = nblk - 1 - it
            r0 = pl.multiple_of(i * rblk, rblk)
            _, vjp = jax.vjp(_conv_block, _halo_block(x_ref, i, rblk, 8), *rows)
            dfull, *dp = vjp(dy_ref[pl.ds(r0, rblk), :])
            dx_ref[pl.ds(r0, rblk), :] = _with_halo_grad(dfull, pending, 8, rblk).astype(BF16)
            return dfull[:8], tuple(p + q for p, q in zip(acc, dp))

        zeros = tuple(jnp.zeros((1, 128), F32) for _ in range(4))
        _, acc = lax.fori_loop(0, nblk, step, (jnp.zeros((8, 128), F32), zeros))
        for k in range(4):
            dcw_ref[k:k + 1, :] = acc[k]

    col = pl.BlockSpec((T, 128), lambda g: (0, g))
    return pl.pallas_call(
        body, name=name, grid=(4,),
        in_specs=[pl.BlockSpec((T, 128), lambda g: (0, col0 // 128 + g)), col,
                  pl.BlockSpec((4, 128), lambda g: (0, cw_col0 // 128 + g))],
        out_specs=[col, pl.BlockSpec((4, 128), lambda g: (0, g))],
        out_shape=[jax.ShapeDtypeStruct((T, BW), BF16), jax.ShapeDtypeStruct((4, BW), F32)],
        compiler_params=_cp("parallel"),
    )(proj, dy, cw)


def _pool_block(xfull, pw, sc, t0, gi):
    n = xfull.shape[0] - 16
    s2 = xfull + _shift(xfull, 1)
    s4 = s2 + _shift(s2, 2)
    s8 = s4 + _shift(s4, 4)
    s16 = s8 + _shift(s8, 8)
    s = jnp.where(gi == 0, s2, jnp.where(gi == 1, s4, jnp.where(gi == 2, s8, s16)))[16:]
    t = t0 + lax.broadcasted_iota(jnp.int32, (n, 128), 0)
    cnt = jnp.minimum(t + 1, lax.shift_left(jnp.int32(2), gi)).astype(F32)
    pooled = s / cnt - xfull[16:]
    return _dot(pooled, pw) * sc


def _pool_fwd(name, proj, pw, sc):
    T = proj.shape[0]
    rblk = _pick(T, (256, 128))
    nblk = T // rblk

    def body(x_ref, pw_ref, sc_ref, y_ref):
        gi = pl.program_id(0)

        def step(i, c):
            r0 = pl.multiple_of(i * rblk, rblk)
            y = _pool_block(_halo_block(x_ref, i, rblk, 16), pw_ref[...], sc_ref[...], r0, gi)
            y_ref[pl.ds(r0, rblk), :] = y.astype(BF16)
            return c

        lax.fori_loop(0, nblk, step, 0)

    return pl.pallas_call(
        body, name=name, grid=(4,),
        in_specs=[pl.BlockSpec((T, 128), lambda g: (0, COL_DX // 128 + g)),
                  pl.BlockSpec((None, 128, 128), lambda g: (g, 0, 0)), pl.BlockSpec((1, 128), lambda g: (0, g))],
        out_specs=pl.BlockSpec((T, 128), lambda g: (0, g)),
        out_shape=jax.ShapeDtypeStruct((T, BW), BF16),
        compiler_params=_cp("parallel"),
    )(proj, pw, sc)


def _pool_bwd(name, proj, dy, pw, sc):
    T = proj.shape[0]
    rblk = _pick(T, (256, 128))
    nblk = T // rblk

    def body(x_ref, dy_ref, pw_ref, sc_ref, dx_ref, dpw_ref, dsc_ref):
        gi = pl.program_id(0)

        def step(it, carry):
            pending, apw, asc = carry
            i = nblk - 1 - it
            r0 = pl.multiple_of(i * rblk, rblk)
            _, vjp = jax.vjp(lambda xf, w, s: _pool_block(xf, w, s, r0, gi), _halo_block(x_ref, i, rblk, 16),
                             pw_ref[...], sc_ref[...])
            dfull, dw, ds = vjp(dy_ref[pl.ds(r0, rblk), :])
            dx_ref[pl.ds(r0, rblk), :] = _with_halo_grad(dfull, pending, 16, rblk).astype(BF16)
            return dfull[:16], apw + dw, asc + ds

        _, apw, asc = lax.fori_loop(0, nblk, step, (jnp.zeros((16, 128), F32), jnp.zeros((128, 128), F32),
                                                    jnp.zeros((1, 128), F32)))
        dpw_ref[...] = apw
        dsc_ref[...] = asc

    col = pl.BlockSpec((T, 128), lambda g: (0, g))
    mat = pl.BlockSpec((None, 128, 128), lambda g: (g, 0, 0))
    vec = pl.BlockSpec((1, 128), lambda g: (0, g))
    return pl.pallas_call(
        body, name=name, grid=(4,),
        in_specs=[pl.BlockSpec((T, 128), lambda g: (0, COL_DX // 128 + g)), col, mat, vec],
        out_specs=[col, mat, vec],
        out_shape=[jax.ShapeDtypeStruct((T, BW), BF16), jax.ShapeDtypeStruct((4, 128, 128), F32),
                   jax.ShapeDtypeStruct((1, BW), F32)],
        compiler_params=_cp("parallel"),
    )(proj, dy, pw, sc)


@jax.custom_vjp
def _dot3(a, b):
    ah = a.astype(BF16)
    al = (a - ah.astype(F32)).astype(BF16)
    bh = b.astype(BF16)
    bl = (b - bh.astype(F32)).astype(BF16)

    def d(x, y):
        return lax.dot_general(x, y, _NN, preferred_element_type=F32)

    return d(ah, bh) + (d(ah, bl) + d(al, bh))


def _dot3_fwd(a, b):
    return _dot3(a, b), (a, b)


def _dot3_bwd(res, g):
    a, b = res
    return _dot(g, b, _NT), _dot(a, g, _TN)


_dot3.defvjp(_dot3_fwd, _dot3_bwd)


def _pad_rows2(x):
    return jnp.concatenate([x, jnp.zeros_like(x)], axis=0)


@jax.custom_vjp
def _tri_inv(mats):
    n = mats[0].shape[0]
    eye = (lax.broadcasted_iota(jnp.int32, (n, n), 0) == lax.broadcasted_iota(jnp.int32, (n, n), 1)).astype(F32)
    ps = [eye - a for a in mats]
    ms = list(mats)
    k = 2
    while k < n:
        ms = [_dot3(t, t) for t in ms]
        ps = [p + _dot3(p, t) for p, t in zip(ps, ms)]
        k *= 2
    return ps


def _tri_inv_fwd(mats):
    ts = _tri_inv(mats)
    return ts, ts


def _tri_inv_bwd(ts, gs):
    half = [_dot(t, g, _TN) for t, g in zip(ts, gs)]
    return ([-_dot(h, t, _NT) for h, t in zip(half, ts)],)


_tri_inv.defvjp(_tri_inv_fwd, _tri_inv_bwd)


def _cumsum_rows(x):
    n = x.shape[0]
    row = lax.broadcasted_iota(jnp.int32, x.shape, 0)
    k = 1
    while k < n:
        x = x + jnp.where(row >= k, _shift(x, k), 0.0)
        k *= 2
    return x


def _gdn_prep(qcs, kcs, vcs, tails, alog, dtb):
    C = CHUNK
    pairs = [(c, h) for c in range(len(qcs)) for h in range(4)]
    lane = lax.broadcasted_iota(jnp.int32, (C, 128), 1)
    row = lax.broadcasted_iota(jnp.int32, (C, 128), 0)
    incl = row >= lane
    sig = [_sigmoid(t) for t in tails]
    gfull = [-jnp.exp(alog) * _softplus(t + dtb) for t in tails]
    beta = [jnp.sum(jnp.where(lane == h, sig[c], 0.0), axis=1, keepdims=True) for c, h in pairs]
    g = [jnp.sum(jnp.where(lane == h + 4, gfull[c], 0.0), axis=1, keepdims=True) for c, h in pairs]
    qs = [qcs[c][:, h * 128:(h + 1) * 128] for c, h in pairs]
    ks = [kcs[c][:, h * 128:(h + 1) * 128] for c, h in pairs]
    vs = [vcs[c][:, h * 128:(h + 1) * 128] for c, h in pairs]
    q = [t * lax.rsqrt(jnp.sum(t * t, axis=-1, keepdims=True) + EPS) * (GDN_DK ** -0.5) for t in qs]
    k = [t * lax.rsqrt(jnp.sum(t * t, axis=-1, keepdims=True) + EPS) for t in ks]
    gc = [_cumsum_rows(jnp.broadcast_to(t, (C, 128))) for t in g]
    gc_t = [jnp.transpose(jnp.concatenate([t, t], axis=0)) for t in gc]
    gc_col = [jnp.sum(jnp.where(lane == 0, t, 0.0), axis=1, keepdims=True) for t in gc]
    ri = lax.broadcasted_iota(jnp.int32, (C, C), 0)
    ci = lax.broadcasted_iota(jnp.int32, (C, C), 1)
    decay = [jnp.exp(jnp.where(incl, a - b[:C, :], -1e30)) for a, b in zip(gc, gc_t)]
    decay_sq = [jnp.exp(jnp.where(ri > ci, a - jnp.transpose(b)[:C, :], -1e30)) for a, b in zip(gc_col, gc)]
    kb = [a * b for a, b in zip(k, beta)]
    kk = [_dot(a, b, _NT) for a, b in zip(kb, k)]
    t_mat = _tri_inv([jnp.where(ri > ci, a * b, 0.0) for a, b in zip(kk, decay_sq)])
    egc = [jnp.exp(t) for t in gc]
    u = [_dot(t, a * b) for t, a, b in zip(t_mat, vs, beta)]
    w = [_dot(t, a * b) for t, a, b in zip(t_mat, kb, egc)]
    qk = [_dot(a, _pad_rows2(b), _NT) for a, b in zip(q, k)]
    attn = [jnp.where(incl, a * b, 0.0) for a, b in zip(qk, decay)]
    g_last = [jnp.sum(jnp.where(row == C - 1, t, 0.0), axis=0, keepdims=True) for t in gc]
    qe = [a * b for a, b in zip(q, egc)]
    kd = [a * jnp.exp(b - c_) for a, b, c_ in zip(k, g_last, gc)]
    egl = [jnp.exp(t) for t in g_last]

    def per_chunk(vals):
        return [jnp.concatenate(vals[4 * c:4 * c + 4], axis=1) for c in range(len(qcs))]

    return tuple(per_chunk(t) for t in (u, w, qe, kd, attn, egl))


def _gdn_scan_chunk(states, u, w, qe, kd, attn, egl, z, ng):
    hs = range(4)

    def sl(t, h):
        return t[:, h * 128:(h + 1) * 128]

    ws = [_dot(sl(w, h), states[h]) for h in hs]
    qs = [_dot(sl(qe, h), states[h]) for h in hs]
    v_new = [sl(u, h) - ws[h] for h in hs]
    av = [_dot(sl(attn, h), _pad_rows2(v_new[h])) for h in hs]
    kv = [_dot(sl(kd, h), v_new[h], _TN) for h in hs]
    nxt = tuple(states[h] * sl(egl, h) + kv[h] for h in hs)
    o = [qs[h] + av[h] for h in hs]
    on = [t * lax.rsqrt(jnp.mean(t * t, axis=-1, keepdims=True) + EPS) * ng for t in o]
    return nxt, jnp.concatenate(on, axis=1) * _silu(z)


def _gdn_blocks(T):
    tb = _pick(T, (512, 256, 128, 64))
    return tb, T // tb, tb // CHUNK


PREP_CHUNKS = 4


def _chunk_rows(i, n):
    return [pl.ds(pl.multiple_of((i * n + j) * CHUNK, CHUNK), CHUNK) for j in range(n)]


def _egl_rows(i, n, size):
    return [pl.ds(pl.multiple_of((i * n + j) * 8, 8), size) for j in range(n)]


def _gdn_prep_fwd(name, qa, ka, va, proj, alog, dtb):
    T = proj.shape[0]
    tb, nb, ncb = _gdn_blocks(T)
    n = PREP_CHUNKS if ncb % PREP_CHUNKS == 0 else 1

    def body(q_ref, k_ref, v_ref, tail_ref, alog_ref, dtb_ref, u_ref, w_ref, qe_ref, kd_ref, at_ref, egl_ref):
        def step(i, c):
            rows = _chunk_rows(i, n)
            u, w, qe, kd, at, egl = _gdn_prep([q_ref[r, :] for r in rows], [k_ref[r, :] for r in rows],
                                              [v_ref[r, :] for r in rows], [tail_ref[r, :].astype(F32) for r in rows],
                                              alog_ref[...], dtb_ref[...])
            for j, (r, e) in enumerate(zip(rows, _egl_rows(i, n, 8))):
                u_ref[r, :] = u[j]
                w_ref[r, :] = w[j].astype(BF16)
                qe_ref[r, :] = qe[j].astype(BF16)
                kd_ref[r, :] = kd[j].astype(BF16)
                at_ref[r, :] = at[j].astype(BF16)
                egl_ref[e, :] = jnp.broadcast_to(egl[j], (8, BW))
            return c

        lax.fori_loop(0, ncb // n, step, 0)

    blk = pl.BlockSpec((tb, BW), lambda j: (j, 0))
    vec = pl.BlockSpec((1, 128), lambda j: (0, 0))
    return pl.pallas_call(
        body, name=name, grid=(nb,),
        in_specs=[blk, blk, blk, pl.BlockSpec((tb, 128), lambda j: (j, COL_TAIL // 128)), vec, vec],
        out_specs=[blk] * 5 + [pl.BlockSpec((ncb * 8, BW), lambda j: (j, 0))],
        out_shape=[jax.ShapeDtypeStruct((T, BW), F32)] + [jax.ShapeDtypeStruct((T, BW), BF16)] * 4
        + [jax.ShapeDtypeStruct((T // 8, BW), F32)],
        compiler_params=_cp("parallel"),
    )(qa, ka, va, proj, alog, dtb)


def _gdn_prep_bwd(name, qa, ka, va, proj, alog, dtb, du, dw, dqe, dkd, dat, degl):
    T = proj.shape[0]
    tb, nb, ncb = _gdn_blocks(T)
    n = PREP_CHUNKS if ncb % PREP_CHUNKS == 0 else 1

    def body(q_ref, k_ref, v_ref, tail_ref, alog_ref, dtb_ref, du_ref, dw_ref, dqe_ref, dkd_ref, dat_ref, degl_ref,
             dq_ref, dk_ref, dv_ref, dtail_ref, dalog_ref, ddtb_ref):
        first = pl.program_id(0) == 0

        def step(i, carry):
            pa, pd = carry
            rows = _chunk_rows(i, n)
            _, vjp = jax.vjp(_gdn_prep, [q_ref[r, :] for r in rows], [k_ref[r, :] for r in rows],
                             [v_ref[r, :] for r in rows], [tail_ref[r, :].astype(F32) for r in rows], alog_ref[...], dtb_ref[...])
            cot = tuple([ref[r, :] for r in rows] for ref in (du_ref, dw_ref, dqe_ref, dkd_ref, dat_ref))
            dq, dk, dv, dtail, da, dd = vjp(cot + ([degl_ref[e, :] for e in _egl_rows(i, n, 1)],))
            for j, r in enumerate(rows):
                dq_ref[r, :] = dq[j]
                dk_ref[r, :] = dk[j]
                dv_ref[r, :] = dv[j]
                dtail_ref[r, :] = dtail[j].astype(BF16)
            return pa + da, pd + dd

        zv = jnp.zeros((1, 128), F32)
        pa, pd = lax.fori_loop(0, ncb // n, step, (zv, zv))

        @pl.when(first)
        def _():
            dalog_ref[...] = pa
            ddtb_ref[...] = pd

        @pl.when(jnp.logical_not(first))
        def _():
            dalog_ref[...] += pa
            ddtb_ref[...] += pd

    blk = pl.BlockSpec((tb, BW), lambda j: (j, 0))
    vec = pl.BlockSpec((1, 128), lambda j: (0, 0))
    return pl.pallas_call(
        body, name=name, grid=(nb,),
        in_specs=[blk, blk, blk, pl.BlockSpec((tb, 128), lambda j: (j, COL_TAIL // 128)), vec, vec]
        + [blk] * 5 + [pl.BlockSpec((ncb * 8, BW), lambda j: (j, 0))],
        out_specs=[blk, blk, blk, pl.BlockSpec((tb, 128), lambda j: (j, 0)), vec, vec],
        out_shape=[jax.ShapeDtypeStruct((T, BW), F32)] * 3 + [jax.ShapeDtypeStruct((T, 128), BF16)]
        + [jax.ShapeDtypeStruct((1, 128), F32)] * 2,
        compiler_params=_cp("arbitrary"),
    )(qa, ka, va, proj, alog, dtb, du, dw, dqe, dkd, dat, degl)


def _gdn_fwd(name, u, w, qe, kd, at, egl, proj, ng):
    T = proj.shape[0]
    tb, nb, ncb = _gdn_blocks(T)

    def body(u_ref, w_ref, qe_ref, kd_ref, at_ref, egl_ref, z_ref, ng_ref, y_ref, sh_ref, state):
        @pl.when(pl.program_id(0) == 0)
        def _():
            state[...] = jnp.zeros((4, 128, 128), F32)

        def step(c, states):
            rows = pl.ds(pl.multiple_of(c * CHUNK, CHUNK), CHUNK)
            for h in range(4):
                sh_ref[h, c] = states[h]
            nxt, y = _gdn_scan_chunk(states, u_ref[rows, :], w_ref[rows, :], qe_ref[rows, :], kd_ref[rows, :],
                                     at_ref[rows, :], egl_ref[pl.ds(pl.multiple_of(c * 8, 8), 1), :],
                                     z_ref[rows, :].astype(F32),
                                     ng_ref[...])
            y_ref[rows, :] = y.astype(BF16)
            return nxt

        states = lax.fori_loop(0, ncb, step, tuple(state[h] for h in range(4)))
        for h in range(4):
            state[h] = states[h]

    blk = pl.BlockSpec((tb, BW), lambda j: (j, 0))
    vec = pl.BlockSpec((1, 128), lambda j: (0, 0))
    return pl.pallas_call(
        body, name=name, grid=(nb,),
        in_specs=[blk] * 5 + [pl.BlockSpec((ncb * 8, BW), lambda j: (j, 0)),
                              pl.BlockSpec((tb, BW), lambda j: (j, COL_CZ // BW)), vec],
        out_specs=[blk, pl.BlockSpec((4, ncb, 128, 128), lambda j: (0, j, 0, 0))],
        out_shape=[jax.ShapeDtypeStruct((T, BW), BF16), jax.ShapeDtypeStruct((4, T // CHUNK, 128, 128), F32)],
        scratch_shapes=[pltpu.VMEM((4, 128, 128), F32)],
        compiler_params=_cp("arbitrary"),
    )(u, w, qe, kd, at, egl, proj, ng)


def _gdn_bwd(name, u, w, qe, kd, at, egl, proj, dy, sh, ng):
    T = proj.shape[0]
    tb, nb, ncb = _gdn_blocks(T)

    def body(u_ref, w_ref, qe_ref, kd_ref, at_ref, egl_ref, z_ref, dy_ref, sh_ref, ng_ref,
             du_ref, dw_ref, dqe_ref, dkd_ref, dat_ref, degl_ref, dz_ref, dng_ref, dstate):
        first = pl.program_id(0) == 0

        @pl.when(first)
        def _():
            dstate[...] = jnp.zeros((4, 128, 128), F32)

        def step(it, carry):
            dstates, pn = carry
            c = ncb - 1 - it
            rows = pl.ds(pl.multiple_of(c * CHUNK, CHUNK), CHUNK)
            erow = pl.multiple_of(c * 8, 8)
            _, vjp = jax.vjp(_gdn_scan_chunk, tuple(sh_ref[h, c] for h in range(4)), u_ref[rows, :],
                             w_ref[rows, :].astype(F32), qe_ref[rows, :].astype(F32), kd_ref[rows, :].astype(F32),
                             at_ref[rows, :].astype(F32), egl_ref[pl.ds(erow, 1), :], z_ref[rows, :].astype(F32),
                             ng_ref[...])
            nxt, du, dw, dqe, dkd, dat, degl, dz, dn = vjp((dstates, dy_ref[rows, :]))
            du_ref[rows, :] = du
            dw_ref[rows, :] = dw
            dqe_ref[rows, :] = dqe
            dkd_ref[rows, :] = dkd
            dat_ref[rows, :] = dat
            degl_ref[pl.ds(erow, 8), :] = jnp.broadcast_to(degl, (8, BW))
            dz_ref[rows, :] = dz.astype(BF16)
            return nxt, pn + dn

        dstates, pn = lax.fori_loop(0, ncb, step, (tuple(dstate[h] for h in range(4)), jnp.zeros((1, 128), F32)))
        for h in range(4):
            dstate[h] = dstates[h]

        @pl.when(first)
        def _():
            dng_ref[...] = pn

        @pl.when(jnp.logical_not(first))
        def _():
            dng_ref[...] += pn

    blk = pl.BlockSpec((tb, BW), lambda j: (nb - 1 - j, 0))
    eblk = pl.BlockSpec((ncb * 8, BW), lambda j: (nb - 1 - j, 0))
    vec = pl.BlockSpec((1, 128), lambda j: (0, 0))
    return pl.pallas_call(
        body, name=name, grid=(nb,),
        in_specs=[blk] * 5 + [eblk, pl.BlockSpec((tb, BW), lambda j: (nb - 1 - j, COL_CZ // BW)), blk,
                              pl.BlockSpec((4, ncb, 128, 128), lambda j: (0, nb - 1 - j, 0, 0)), vec],
        out_specs=[blk] * 5 + [eblk, blk, vec],
        out_shape=[jax.ShapeDtypeStruct((T, BW), F32)] * 5 + [jax.ShapeDtypeStruct((T // 8, BW), F32),
                                                              jax.ShapeDtypeStruct((T, BW), BF16),
                                                              jax.ShapeDtypeStruct((1, 128), F32)],
        scratch_shapes=[pltpu.VMEM((4, 128, 128), F32)],
        compiler_params=_cp("arbitrary"),
    )(u, w, qe, kd, at, egl, proj, dy, sh, ng)


def _adamw_update(w_ref, g_ref, m_ref, v_ref, d_ref, nm_ref, nv_ref):
    gv = g_ref[...]
    m2 = ADAM_B1 * m_ref[...] + (1.0 - ADAM_B1) * gv
    v2 = ADAM_B2 * v_ref[...] + (1.0 - ADAM_B2) * jnp.square(gv)
    m_hat = m2 / (1.0 - ADAM_B1 ** ADAM_STEP)
    v_hat = v2 / (1.0 - ADAM_B2 ** ADAM_STEP)
    d_ref[...] = -ADAM_LR * (m_hat / (jnp.sqrt(v_hat) + ADAM_EPS) + ADAM_WD * w_ref[...])
    nm_ref[...] = m2
    nv_ref[...] = v2


def _adamw_many(name, ws, gs, ms, vs):
    n = len(ws)

    def body(*refs):
        for i in range(n):
            _adamw_update(*[refs[k * n + i] for k in range(7)])

    return pl.pallas_call(
        body, name=name,
        out_shape=[jax.ShapeDtypeStruct(a.shape, F32) for a in ws] * 3,
        compiler_params=_cp(),
    )(*ws, *gs, *ms, *vs)


def _adamw(name, w, g, m, v):
    R, C = w.shape
    br = _pick(R, (512, 256, 240, 128, 64, 8))
    body = functools.partial(_adamw_update)
    spec = pl.BlockSpec((br, C), lambda i: (i, 0))
    return pl.pallas_call(
        body, name=name, grid=(R // br,),
        in_specs=[spec] * 4, out_specs=[spec] * 3,
        out_shape=[jax.ShapeDtypeStruct((R, C), F32)] * 3,
        compiler_params=_cp("parallel"),
    )(w, g, m, v)


def _sum8(name, parts):
    _, R, C = parts.shape
    br = _pick(R, (352, 368, 256, 128, 64, 16, 8))

    def body(p_ref, o_ref):
        acc = p_ref[0].astype(F32)
        for d in range(1, N_DEV):
            acc = acc + p_ref[d].astype(F32)
        o_ref[...] = acc

    return pl.pallas_call(
        body, name=name, grid=(R // br,),
        in_specs=[pl.BlockSpec((N_DEV, br, C), lambda i: (0, i, 0))],
        out_specs=pl.BlockSpec((br, C), lambda i: (i, 0)),
        out_shape=jax.ShapeDtypeStruct((R, C), F32),
        compiler_params=_cp("parallel"),
    )(parts)


_ANY = pl.BlockSpec(memory_space=pl.ANY)
_MESH = pl.DeviceIdType.MESH


def _all_gather(name, shard):
    R, C = shard.shape

    def body(x_ref, out_ref, send_sems, recv_sems, local_sem):
        x, y, c = lax.axis_index("x"), lax.axis_index("y"), lax.axis_index("c")
        me, sibling = (x, y, c), (x, y, 1 - c)
        chips = [(1 - x, y), (x, 1 - y), (1 - x, 1 - y)]

        def slot(px, py, pc):
            return out_ref.at[4 * px + 2 * py + pc]

        def copy(k, block, to, src=None):
            return pltpu.make_async_remote_copy(
                src_ref=slot(*block) if src is None else src, dst_ref=slot(*block),
                send_sem=send_sems.at[k], recv_sem=recv_sems.at[k], device_id=to, device_id_type=_MESH)

        mine = pltpu.make_async_copy(x_ref, slot(*me), local_sem)
        mine.start()
        first = [copy(0, me, sibling, src=x_ref)]
        first += [copy(1 + j, me, (*chip, c), src=x_ref) for j, chip in enumerate(chips)]
        for cp in first:
            cp.start()
        passed = [copy(4 + j, (*chip, c), sibling) for j, chip in enumerate(chips)]
        for j, chip in enumerate(chips):
            copy(1 + j, (*chip, c), me).wait_recv()
            passed[j].start()
        copy(0, sibling, me).wait_recv()
        for j, chip in enumerate(chips):
            copy(4 + j, (*chip, 1 - c), me).wait_recv()
        for cp in first + passed:
            cp.wait_send()
        mine.wait()

    return pl.pallas_call(
        body, name=name,
        in_specs=[_ANY], out_specs=_ANY,
        out_shape=jax.ShapeDtypeStruct((N_DEV, R, C), shard.dtype),
        scratch_shapes=[pltpu.SemaphoreType.DMA((7,)), pltpu.SemaphoreType.DMA((7,)), pltpu.SemaphoreType.DMA],
    )(shard)


_HBM = pl.BlockSpec(memory_space=pltpu.HBM)
_SEM = pl.BlockSpec(memory_space=pltpu.SEMAPHORE)
_EFFECT = pltpu.SideEffectType.DATAFLOW_SIDE_EFFECTING


def _exchange_copies(src_ref, land_ref, send_sems, recv_sems, scatter):
    x, y, c = lax.axis_index("x"), lax.axis_index("y"), lax.axis_index("c")
    me = 4 * x + 2 * y + c
    copies = []
    for k in range(1, N_DEV):
        px, py, pc = x ^ ((k >> 2) & 1), y ^ ((k >> 1) & 1), c ^ (k & 1)
        src = src_ref.at[4 * px + 2 * py + pc] if scatter else src_ref
        copies.append(pltpu.make_async_remote_copy(
            src_ref=src, dst_ref=land_ref.at[me], send_sem=send_sems.at[k - 1], recv_sem=recv_sems.at[k - 1],
            device_id=(px, py, pc), device_id_type=_MESH))
    return copies


def _own_copy(src_ref, land_ref, send_sems, scatter):
    me = 4 * lax.axis_index("x") + 2 * lax.axis_index("y") + lax.axis_index("c")
    return pltpu.make_async_copy(src_ref.at[me] if scatter else src_ref, land_ref.at[me], send_sems.at[N_DEV - 1])


def _exchange_start(name, srcs, lands, scatter, after=None):
    n = len(srcs)

    def body(*refs):
        src_refs, land_refs = refs[:n], refs[n:2 * n]
        outs = refs[2 * n + (after is not None):]
        send, recv = outs[:n], outs[n:2 * n]
        token = refs[-1]
        for g in range(n):
            for cp in _exchange_copies(src_refs[g], land_refs[g], send[g], recv[g], scatter):
                cp.start()
            _own_copy(src_refs[g], land_refs[g], send[g], scatter).start()
        token[...] = jnp.zeros_like(token)

    outs = pl.pallas_call(
        body, name=name,
        out_shape=tuple([pltpu.SemaphoreType.DMA((N_DEV,))] * (2 * n)
                        + [pltpu.HBM(a.shape, a.dtype) for a in list(srcs) + list(lands)]
                        + [jax.ShapeDtypeStruct((8, 128), F32)]),
        in_specs=[_HBM] * (2 * n) + [_ANY] * (after is not None),
        out_specs=tuple([_SEM] * (2 * n) + [_HBM] * (2 * n) + [pl.BlockSpec(memory_space=pltpu.VMEM)]),
        input_output_aliases={i: 2 * n + i for i in range(2 * n)},
        compiler_params=pltpu.CompilerParams(has_side_effects=_EFFECT),
    )(*[pltpu.with_memory_space_constraint(a, pltpu.HBM) for a in list(srcs) + list(lands)],
      *([after] if after is not None else []))
    handles = [(outs[2 * n + g], outs[3 * n + g], outs[g], outs[n + g]) for g in range(n)]
    return handles, outs[-1]


def _exchange_wait(name, handles, after, scatter):
    n = len(handles)
    srcs, lands, sends, recvs = ([h[i] for h in handles] for i in range(4))

    def body(*refs):
        src_refs, land_refs = refs[:n], refs[n:2 * n]
        send, recv = refs[2 * n:3 * n], refs[3 * n:4 * n]
        for g in range(n):
            for cp in _exchange_copies(src_refs[g], land_refs[g], send[g], recv[g], scatter):
                cp.wait_send()
                cp.wait_recv()
            _own_copy(src_refs[g], land_refs[g], send[g], scatter).wait()

    outs = pl.pallas_call(
        body, name=name,
        out_shape=tuple(pltpu.HBM(a.shape, a.dtype) for a in srcs + lands),
        in_specs=tuple([_HBM] * (2 * n) + [_SEM] * (2 * n) + [_ANY]), out_specs=tuple([_HBM] * (2 * n)),
        input_output_aliases={i: i for i in range(2 * n)},
        compiler_params=pltpu.CompilerParams(has_side_effects=_EFFECT),
    )(*srcs, *lands, *sends, *recvs, after)
    return list(outs[n:])


def _rows(a):
    return a.reshape(-1, 1024)


def _rows_to_parts(full):
    n = full.shape[-2] // N_DEV
    t = full.reshape(full.shape[:-2] + (N_DEV, n, full.shape[-1]))
    return jnp.moveaxis(t, -3, 0)


def _parts_to_rows(parts):
    t = jnp.moveaxis(parts, 0, -3)
    return t.reshape(t.shape[:-3] + (t.shape[-3] * t.shape[-2], t.shape[-1]))


def _parts_to_cols(parts):
    t = jnp.moveaxis(parts, 0, -2)
    return t.reshape(t.shape[:-2] + (t.shape[-2] * t.shape[-1],))


def _join(parts, axis=0):
    total = sum(p.shape[axis] for p in parts)
    out, off = None, 0
    for p in parts:
        cfg = [(0, 0)] * p.ndim
        cfg[axis] = (off, total - off - p.shape[axis])
        t = jnp.pad(p, cfg)
        out = t if out is None else out + t
        off += p.shape[axis]
    return out


def _w_in_to_layout(w):
    tail = jnp.pad(w[4096:4104], ((0, PW - COL_TAIL - 8), (0, 0)))
    return jnp.concatenate([w[:4096], w[4104:P_IN], tail], axis=0)


def _w_in_from_layout(g):
    return _join([g[:4096], g[COL_TAIL:COL_TAIL + 8], g[4096:COL_TAIL]], axis=0)


def _block_diag(w):
    w = w.reshape(4, 2, 64, 64)
    return jnp.pad(w[:, 0], ((0, 0), (0, 64), (0, 64))) + jnp.pad(w[:, 1], ((0, 0), (64, 0), (64, 0)))


def _block_diag_grad(g):
    return jnp.stack([g[:, :64, :64], g[:, 64:, 64:]], axis=1).reshape(8, 64, 64)


def _ffn_forward(tag, x, norm, wg, wu, wd):
    h = _rms_fwd(tag + "_norm", x, norm)
    sa, ds, act = _ffn_up(tag + "_up", h, wg, wu)
    if callable(wd):
        wd = wd(act)
    x_out = _mm(tag + "_down", [(act, wd)], "nn", F32, res=x, scale=0.5)
    return x_out, (x, h, sa, ds, act), wd


def _ffn_backward(tag, dx_out, saved, norm, wg, wu, wd, put, names, split=False):
    x, h, sa, ds, act = saved
    n_wg, n_wu, n_wd = names
    dwd = _mm(tag + "_dwd", [(act, dx_out)], "tn", BF16, scale=0.5, bm=FF // 2)
    tok = put({n_wd: dwd}) if split else None
    da, db = _ffn_dact(tag + "_dact", dx_out, wd, sa, ds, after=tok)
    dwg = _mm(tag + "_dwg", [(da, h)], "tn", BF16, bm=FF // 2)
    if split:
        tok = tok + put({n_wg: dwg})
    dwu = _mm(tag + "_dwu", [(db, h)], "tn", BF16, bm=FF // 2, after=tok)
    tok = tok + put({n_wu: dwu}) if split else put({n_wg: dwg, n_wu: dwu, n_wd: dwd})
    dh = _mm(tag + "_dh", [(da, wg), (db, wu)], "nn", F32, after=tok)
    dx, dnorm = _rms_bwd(tag + "_dnorm", x, norm + tok, dh, dx_out)
    return dx, dnorm


def _mixer_params(p):
    alog = jnp.pad(p["gdn_a_log"], (4, 120))[None]
    dtb = jnp.pad(p["gdn_dt_bias"], (4, 120))[None]
    bias = jnp.repeat(p["sgu_b"].T, 128, axis=1)
    return dict(
        ln_g=p["sgu_ln_g"][None], ln_b=p["sgu_ln_b"][None], sgu_w=p["sgu_w"], sgu_bias=bias,
        lru_cw=p["lru_conv_w"], lru_cb=p["lru_conv_b"][None], wa=_block_diag(p["lru_wa"]), ba=p["lru_ba"][None],
        wx=_block_diag(p["lru_wx"]), bx=p["lru_bx"][None], lam=p["lru_lambda"][None],
        gdn_cw=p["gdn_conv_w"], alog=alog, dtb=dtb, ng=p["gdn_norm_g"][None],
        pool_w=p["pool_w"], pool_sc=p["pool_scale"][None])


def _mix_forward(tag, x, p, mp):
    h = _rms_fwd(tag + "_norm", x, p["mix_norm"][None])
    proj = _mm(tag + "_proj", [(h, p["w_in"])], "nt", BF16, bm=_pick(x.shape[0], (2048, 1024, 512, 256, 128)))
    y_a = _sgu_fwd(tag + "_sgu", proj, mp["ln_g"], mp["ln_b"], mp["sgu_w"], mp["sgu_bias"])
    y_b, hc = _lru_fwd(tag + "_lru", proj, mp["lru_cw"], mp["lru_cb"], mp["wa"], mp["ba"], mp["wx"], mp["bx"],
                       mp["lam"])
    qa = _conv_fwd(tag + "_convq", proj, COL_CQ, mp["gdn_cw"], 0)
    ka = _conv_fwd(tag + "_convk", proj, COL_CK, mp["gdn_cw"], 512)
    va = _conv_fwd(tag + "_convv", proj, COL_CV, mp["gdn_cw"], 1024)
    prep = _gdn_prep_fwd(tag + "_gdnprep", qa, ka, va, proj, mp["alog"], mp["dtb"])
    y_c, sh = _gdn_fwd(tag + "_gdn", *prep, proj, mp["ng"])
    y_d = _pool_fwd(tag + "_pool", proj, mp["pool_w"], mp["pool_sc"])
    ys = (y_a, y_b, y_c, y_d)
    if callable(p["w_branch"]):
        p["w_branch"] = p["w_branch"](y_d)
    merged = _merge_fwd(tag + "_merge", ys, p["w_branch"], proj)
    if callable(p["w_out"]):
        p["w_out"] = p["w_out"](merged)
    x_out = _mm(tag + "_out", [(merged, p["w_out"])], "nn", F32, res=x)
    return x_out, (x, h, proj, hc, qa, ka, va, prep, sh, ys, merged)


def _mix_backward(tag, dx_out, saved, p, mp, put):
    x, h, proj, hc, qa, ka, va, prep, sh, ys, merged = saved
    T = x.shape[0]
    g = {}
    dmerged = _mm(tag + "_dmerged", [(dx_out, p["w_out"])], "nt", F32)
    g["w_out"] = _mm(tag + "_dwout", [(merged, dx_out)], "tn", BF16)
    outs = _merge_bwd(tag + "_dmerge", dmerged, ys, p["w_branch"], proj)
    dgates, dbrs, dys = outs[:NBR], outs[NBR:2 * NBR], outs[2 * NBR:]
    g["w_branch"] = jnp.stack([_mm(f"{tag}_dwb{i}", [(dbrs[i], ys[i])], "tn", BF16) for i in range(NBR)])

    du, dv, dln_g, dln_b, dsgu_w, dbias = _sgu_bwd(tag + "_dsgu", proj, dys[0], mp["ln_g"], mp["ln_b"], mp["sgu_w"],
                                                  mp["sgu_bias"])
    g["sgu_ln_g"], g["sgu_ln_b"], g["sgu_w"] = dln_g[0], dln_b[0], dsgu_w
    g["sgu_b"] = dbias.reshape(128, 4, 128).sum(axis=2).T

    (dbx, dbg, dcw, dcb, dwa, dba, dwx, dbxb, dlam) = _lru_bwd(
        tag + "_dlru", proj, dys[1], hc, mp["lru_cw"], mp["lru_cb"], mp["wa"], mp["ba"], mp["wx"], mp["bx"], mp["lam"])
    g["lru_conv_w"], g["lru_conv_b"], g["lru_ba"], g["lru_bx"], g["lru_lambda"] = dcw, dcb[0], dba[0], dbxb[0], dlam[0]
    g["lru_wa"], g["lru_wx"] = _block_diag_grad(dwa), _block_diag_grad(dwx)

    *dprep, dz, dng = _gdn_bwd(tag + "_dgdn", *prep, proj, dys[2], sh, mp["ng"])
    dqa, dka, dva, dtail, dalog, ddtb = _gdn_prep_bwd(tag + "_dgdnprep", qa, ka, va, proj, mp["alog"], mp["dtb"], *dprep)
    g["gdn_a_log"], g["gdn_dt_bias"], g["gdn_norm_g"] = dalog[0, 4:8], ddtb[0, 4:8], dng[0]
    dq, dcwq = _conv_bwd(tag + "_dconvq", proj, COL_CQ, dqa, mp["gdn_cw"], 0)
    dk, dcwk = _conv_bwd(tag + "_dconvk", proj, COL_CK, dka, mp["gdn_cw"], 512)
    dv_, dcwv = _conv_bwd(tag + "_dconvv", proj, COL_CV, dva, mp["gdn_cw"], 1024)
    g["gdn_conv_w"] = jnp.concatenate([dcwq, dcwk, dcwv], axis=1)

    dd, dpw, dsc = _pool_bwd(tag + "_dpool", proj, dys[3], mp["pool_w"], mp["pool_sc"])
    g["pool_w"], g["pool_scale"] = dpw, dsc[0]

    dproj = jnp.concatenate([du, dv, dbx, dbg, dq, dk, dv_, dz, dd, *dgates, dtail,
                             jnp.zeros((T, PW - COL_TAIL - 128), BF16)], axis=1)
    dw_in = _mm(tag + "_dwin", [(dproj, h)], "tn", BF16)
    tok = put(dict(w_in=_w_in_from_layout(dw_in), w_branch=g.pop("w_branch"), w_out=g.pop("w_out")))
    dh = _mm(tag + "_dh", [(dproj, p["w_in"])], "nn", F32, bm=_pick(T, (2048, 1024, 512, 256, 128)), after=tok)
    dx, dnorm = _rms_bwd(tag + "_dnorm", x, p["mix_norm"][None] + tok, dh, dx_out)
    g["mix_norm"] = dnorm[0]
    return dx, g


_BIG = ("ff1_wg", "ff1_wu", "ff1_wd", "w_in", "w_branch", "w_out", "ff2_wg", "ff2_wu", "ff2_wd")
_COL_SHARDED = ("ff1_wg", "ff1_wu", "w_in", "w_branch", "ff2_wg", "ff2_wu")
_SMALL = ("ff1_norm", "mix_norm", "sgu_ln_g", "sgu_ln_b", "sgu_w", "sgu_b", "lru_conv_w", "lru_conv_b", "lru_wa",
          "lru_ba", "lru_wx", "lru_bx", "lru_lambda", "gdn_conv_w", "gdn_a_log", "gdn_dt_bias", "gdn_norm_g", "pool_w",
          "pool_scale", "ff2_norm", "final_norm")
_WEIGHTS = ("ff1_norm", "ff1_wg", "ff1_wu", "ff1_wd", "mix_norm", "w_in", "sgu_ln_g", "sgu_ln_b", "sgu_w", "sgu_b",
            "lru_conv_w", "lru_conv_b", "lru_wa", "lru_ba", "lru_wx", "lru_bx", "lru_lambda", "gdn_conv_w", "gdn_a_log",
            "gdn_dt_bias", "gdn_norm_g", "pool_w", "pool_scale", "w_branch", "w_out", "ff2_norm", "ff2_wg", "ff2_wu",
            "ff2_wd", "final_norm")
_CONV_SHARDED = ("lru_conv_w", "gdn_conv_w")
PACK_ROW_ALIGN = 16
_GROUPS = (("ff1", ("ff1_wg", "ff1_wu", "ff1_wd")), ("mix", ("w_in", "w_branch", "w_out")),
           ("ff2", ("ff2_wg", "ff2_wu", "ff2_wd")))


def _pad_rows(a, mult):
    pad = (-a.shape[-2]) % mult
    if pad == 0:
        return a
    return jnp.pad(a, [(0, 0)] * (a.ndim - 2) + [(0, pad), (0, 0)])


def _my_index():
    return 4 * lax.axis_index("x") + 2 * lax.axis_index("y") + lax.axis_index("c")


def _landing(shape, dtype):
    return lax.empty((N_DEV,) + tuple(shape), dtype)


def _stored(n, a):
    return jnp.swapaxes(a, -1, -2) if n in _COL_SHARDED else a


_FIRST = ("ff1_wg", "ff1_wu", "ff1_wd")


def _gather_first(w):
    names = _FIRST
    shards = [_rows(_stored(n, w[n][0]).astype(BF16)) for n in names]
    got = _all_gather("gather_first", jnp.concatenate(shards, axis=0))
    out, r = {}, 0
    for n, s in zip(names, shards):
        out[n] = got[:, r:r + s.shape[0]].reshape(-1, 1024)
        r += s.shape[0]
    return out, got


def _gather_start(w, after):
    conv = _pad_rows(jnp.concatenate([w[n].reshape(1, -1) for n in _CONV_SHARDED], axis=1), 8)
    keys, srcs = ["conv"], [conv]
    for l in range(2):
        for sub, (_, names) in enumerate(_GROUPS):
            for n in names:
                if l > 0 or n not in _FIRST:
                    keys.append((l, sub, n))
                    srcs.append(_stored(n, w[n][l]).astype(BF16))
    lands = [_landing(s.shape, s.dtype) for s in srcs]
    handles, token = _exchange_start("gather_start", srcs, lands, scatter=False, after=after)
    return dict(zip(keys, handles)), token


def _gather_finish(l, sub, handles, first, after):
    names = _GROUPS[sub][1]
    if (l, sub) == (0, 0):
        out = dict(first)
        for n in names:
            if n not in _FIRST:
                out[n] = lambda later, n=n: _parts_to_rows(
                    _exchange_wait(f"gather_wait_00_{n}", [handles[(0, 0, n)]], later, scatter=False)[0])
    elif sub == 1:
        out = {n: (lambda later, n=n: _parts_to_rows(
            _exchange_wait(f"gather_wait_{l}{sub}_{n}", [handles[(l, sub, n)]], later, scatter=False)[0])) for n in names}
        out["w_in"] = out["w_in"](after)
    else:
        lands = _exchange_wait(f"gather_wait_{l}{sub}", [handles[(l, sub, n)] for n in names], after, scatter=False)
        out = {n: _parts_to_rows(land) for n, land in zip(names, lands)}
    if "w_in" in out:
        out["w_in"] = _w_in_to_layout(out["w_in"])
    return out


def _scatter_start(l, sub, grads):
    srcs, shapes = [], []
    for n in grads:
        parts = _rows_to_parts(grads[n])
        shapes.append(parts.shape[1:])
        srcs.append(_pad_rows(parts.reshape(N_DEV, -1, 1024), PACK_ROW_ALIGN))
    lands = [_landing(s.shape[1:], s.dtype) for s in srcs]
    tag = f"{l}{sub}" + ("" if len(grads) == len(_GROUPS[sub][1]) else "_" + "_".join(grads))
    handles, token = _exchange_start(f"scatter_start_{tag}", srcs, lands, scatter=True)
    return handles, (tag, tuple(grads), shapes), token


def _scatter_finish(l, sub, handles, meta, after):
    tag, names, shapes = meta
    lands = _exchange_wait(f"scatter_wait_{tag}", handles, after, scatter=True)
    out = {}
    for n, land, shape in zip(names, lands, shapes):
        size = 1
        for s in shape:
            size *= s
        summed = _sum8(f"sum_{l}{sub}_{n}", land)
        out[n] = _stored(n, summed[:size // 1024].reshape(shape))
    return out


def _gather_conv_finish(w, handles, after):
    gconv = _exchange_wait("gather_wait_conv", [handles["conv"]], after, scatter=False)[0][:, 0]
    full, r = {}, 0
    for n in _CONV_SHARDED:
        sz = w[n].size
        full[n] = _parts_to_cols(gconv[:, r:r + sz].reshape((N_DEV,) + w[n].shape))
        r += sz
    return full


def _forward_backward(x, tgt, w, conv, get_weights, put_grads, put_small, token):
    saved, params = [], []
    for l in range(2):
        p = {n: w[n][l] for n in _SMALL if n != "final_norm"}
        for n in _CONV_SHARDED:
            p[n] = conv[n][l]
        mp = _mixer_params(p)
        tok = token[:1, :1] if l == 0 else 0.0
        p.update(get_weights(l, 0, x))
        x, s1, p["ff1_wd"] = _ffn_forward(f"l{l}_ff1", x, p["ff1_norm"][None] + tok, p["ff1_wg"], p["ff1_wu"],
                                          p["ff1_wd"])
        p.update(get_weights(l, 1, x))
        x, s2 = _mix_forward(f"l{l}_mix", x, p, mp)
        p.update(get_weights(l, 2, x))
        x, s3, _ = _ffn_forward(f"l{l}_ff2", x, p["ff2_norm"][None], p["ff2_wg"], p["ff2_wu"], p["ff2_wd"])
        saved.append((s1, s2, s3))
        params.append((p, mp))
    loss, dx, dfinal = _final_loss("loss_head", x, w["final_norm"][None], tgt)
    tok = 0.0
    for l in (1, 0):
        p, mp = params[l]
        s1, s2, s3 = saved[l]
        g = {}

        def put(sub):
            return lambda grads, l=l: put_grads(l, sub, grads)[:1, :1]

        dx, dn = _ffn_backward(f"l{l}_ff2", dx, s3, p["ff2_norm"][None] + tok, p["ff2_wg"], p["ff2_wu"], p["ff2_wd"],
                               put(2), _GROUPS[2][1])
        g["ff2_norm"] = dn[0]
        dx, gm = _mix_backward(f"l{l}_mix", dx, s2, p, mp, put(1))
        g.update(gm)
        tok = 0.0
        if l == 0:
            tok = put_small("0a", g)[:1, :1]
            g = {}
        dx, dn = _ffn_backward(f"l{l}_ff1", dx, s1, p["ff1_norm"][None] + tok, p["ff1_wg"], p["ff1_wu"], p["ff1_wd"],
                               put(0), _GROUPS[0][1], split=(l == 0))
        g["ff1_norm"] = dn[0]
        if l == 1:
            g["final_norm"] = dfinal[0]
            g["loss"] = loss[0, :1]
        tok = put_small("1" if l == 1 else "0b", g)[:1, :1]
    return dx


SMALL_PIECE = 8 * 1024


def _pack_small(d, names):
    pieces = []
    for n in names:
        flat = d[n].reshape(-1)
        pieces.append(jnp.pad(flat, (0, (-flat.size) % SMALL_PIECE)).reshape(-1, 1024))
    return jnp.concatenate(pieces, axis=0)


def _unpack_small(pack, shapes, names):
    out, r = {}, 0
    for n in names:
        size = 1
        for s in shapes[n]:
            size *= s
        rows = -(-size // SMALL_PIECE) * 8
        out[n] = pack[r:r + rows].reshape(-1)[:size].reshape(shapes[n])
        r += rows
    return out


def _small_names(grads):
    return tuple(n for n in _SMALL + ("loss",) if n in grads)


def _small_start(tag, grads):
    pack = _pack_small(grads, _small_names(grads))
    if "loss" not in grads:
        pack = _pad_rows(pack.astype(BF16), PACK_ROW_ALIGN)
    handles, token = _exchange_start(f"small_start_{tag}", [pack], [_landing(pack.shape, pack.dtype)], scatter=False)
    return handles, {n: grads[n].shape for n in _small_names(grads)}, token


def _small_finish(tag, handles, shapes, after):
    landed = _exchange_wait(f"small_wait_{tag}", handles, after, scatter=False)[0]
    return _unpack_small(_sum8(f"sum_small_{tag}", landed), shapes, _small_names(shapes))


def _as2d(a):
    if a.ndim == 1:
        return a.reshape(1, -1)
    return a.reshape(-1, a.shape[-1])


def kernel(x, ff1_norm, ff1_wg, ff1_wu, ff1_wd, mix_norm, w_in, sgu_ln_g, sgu_ln_b, sgu_w, sgu_b, lru_conv_w, lru_conv_b, lru_wa, lru_ba, lru_wx, lru_bx, lru_lambda, gdn_conv_w, gdn_a_log, gdn_dt_bias, gdn_norm_g, pool_w, pool_scale, w_branch, w_out, ff2_norm, ff2_wg, ff2_wu, ff2_wd, final_norm, loss_target, m_ff1_norm, m_ff1_wg, m_ff1_wu, m_ff1_wd, m_mix_norm, m_w_in, m_sgu_ln_g, m_sgu_ln_b, m_sgu_w, m_sgu_b, m_lru_conv_w, m_lru_conv_b, m_lru_wa, m_lru_ba, m_lru_wx, m_lru_bx, m_lru_lambda, m_gdn_conv_w, m_gdn_a_log, m_gdn_dt_bias, m_gdn_norm_g, m_pool_w, m_pool_scale, m_w_branch, m_w_out, m_ff2_norm, m_ff2_wg, m_ff2_wu, m_ff2_wd, m_final_norm, v_ff1_norm, v_ff1_wg, v_ff1_wu, v_ff1_wd, v_mix_norm, v_w_in, v_sgu_ln_g, v_sgu_ln_b, v_sgu_w, v_sgu_b, v_lru_conv_w, v_lru_conv_b, v_lru_wa, v_lru_ba, v_lru_wx, v_lru_bx, v_lru_lambda, v_gdn_conv_w, v_gdn_a_log, v_gdn_dt_bias, v_gdn_norm_g, v_pool_w, v_pool_scale, v_w_branch, v_w_out, v_ff2_norm, v_ff2_wg, v_ff2_wu, v_ff2_wd, v_final_norm):
    w = dict(ff1_norm=ff1_norm, ff1_wg=ff1_wg, ff1_wu=ff1_wu, ff1_wd=ff1_wd, mix_norm=mix_norm, w_in=w_in,
             sgu_ln_g=sgu_ln_g, sgu_ln_b=sgu_ln_b, sgu_w=sgu_w, sgu_b=sgu_b, lru_conv_w=lru_conv_w,
             lru_conv_b=lru_conv_b, lru_wa=lru_wa, lru_ba=lru_ba, lru_wx=lru_wx, lru_bx=lru_bx, lru_lambda=lru_lambda,
             gdn_conv_w=gdn_conv_w, gdn_a_log=gdn_a_log, gdn_dt_bias=gdn_dt_bias, gdn_norm_g=gdn_norm_g, pool_w=pool_w,
             pool_scale=pool_scale, w_branch=w_branch, w_out=w_out, ff2_norm=ff2_norm, ff2_wg=ff2_wg, ff2_wu=ff2_wu,
             ff2_wd=ff2_wd, final_norm=final_norm)
    m = dict(ff1_norm=m_ff1_norm, ff1_wg=m_ff1_wg, ff1_wu=m_ff1_wu, ff1_wd=m_ff1_wd, mix_norm=m_mix_norm, w_in=m_w_in,
             sgu_ln_g=m_sgu_ln_g, sgu_ln_b=m_sgu_ln_b, sgu_w=m_sgu_w, sgu_b=m_sgu_b, lru_conv_w=m_lru_conv_w,
             lru_conv_b=m_lru_conv_b, lru_wa=m_lru_wa, lru_ba=m_lru_ba, lru_wx=m_lru_wx, lru_bx=m_lru_bx,
             lru_lambda=m_lru_lambda, gdn_conv_w=m_gdn_conv_w, gdn_a_log=m_gdn_a_log, gdn_dt_bias=m_gdn_dt_bias,
             gdn_norm_g=m_gdn_norm_g, pool_w=m_pool_w, pool_scale=m_pool_scale, w_branch=m_w_branch, w_out=m_w_out,
             ff2_norm=m_ff2_norm, ff2_wg=m_ff2_wg, ff2_wu=m_ff2_wu, ff2_wd=m_ff2_wd, final_norm=m_final_norm)
    v = dict(ff1_norm=v_ff1_norm, ff1_wg=v_ff1_wg, ff1_wu=v_ff1_wu, ff1_wd=v_ff1_wd, mix_norm=v_mix_norm, w_in=v_w_in,
             sgu_ln_g=v_sgu_ln_g, sgu_ln_b=v_sgu_ln_b, sgu_w=v_sgu_w, sgu_b=v_sgu_b, lru_conv_w=v_lru_conv_w,
             lru_conv_b=v_lru_conv_b, lru_wa=v_lru_wa, lru_ba=v_lru_ba, lru_wx=v_lru_wx, lru_bx=v_lru_bx,
             lru_lambda=v_lru_lambda, gdn_conv_w=v_gdn_conv_w, gdn_a_log=v_gdn_a_log, gdn_dt_bias=v_gdn_dt_bias,
             gdn_norm_g=v_gdn_norm_g, pool_w=v_pool_w, pool_scale=v_pool_scale, w_branch=v_w_branch, w_out=v_w_out,
             ff2_norm=v_ff2_norm, ff2_wg=v_ff2_wg, ff2_wu=v_ff2_wu, ff2_wd=v_ff2_wd, final_norm=v_final_norm)

    first, got_first = _gather_first(w)
    handles, token = _gather_start(w, got_first)
    conv = _gather_conv_finish(w, handles, token)
    pending = {}

    def get_weights(l, sub, after):
        return _gather_finish(l, sub, handles, first, after)

    def put_grads(l, sub, grads):
        hs, meta, tok = _scatter_start(l, sub, grads)
        pending[(l, sub, meta[0])] = (hs, meta)
        return tok

    def put_small(tag, grads):
        hs, shapes, tok = _small_start(tag, grads)
        pending[tag] = (hs, shapes)
        return tok

    T = x.shape[1]
    dx = _forward_backward(x.reshape(T, D), loss_target.reshape(T, D), w, conv, get_weights, put_grads, put_small,
                           token)
    per = {}
    for key in pending:
        if isinstance(key, tuple):
            per.setdefault(key[:2], {}).update(_scatter_finish(*key[:2], *pending[key], dx))
        else:
            per[key] = _small_finish(key, *pending[key], dx)
    grad = {n: jnp.stack([per[(0, sub)][n], per[(1, sub)][n]]) for sub, (_, names) in enumerate(_GROUPS) for n in names}
    layer0 = {**per["0a"], **per["0b"]}
    small = {n: _join([layer0[n].reshape(-1), per["1"][n].reshape(-1)]).reshape((2,) + layer0[n].shape)
             for n in layer0}
    small["final_norm"] = per["1"]["final_norm"]
    loss = per["1"]["loss"][0]
    me = _my_index()
    for n in _SMALL:
        if n in _CONV_SHARDED:
            width = w[n].shape[-1]
            grad[n] = lax.dynamic_slice_in_dim(small[n], me * width, width, axis=2)
        else:
            grad[n] = small[n]

    delta, new_m, new_v = {}, {}, {}
    for n in _BIG:
        d_, m_, v_ = _adamw("adamw_" + n, _as2d(w[n]), _as2d(grad[n]), _as2d(m[n]), _as2d(v[n]))
        delta[n], new_m[n], new_v[n] = (t.reshape(w[n].shape) for t in (d_, m_, v_))

    outs = _adamw_many("adamw_small", *[[_as2d(t[n]) for n in _SMALL] for t in (w, grad, m, v)])
    for k, dst in enumerate((delta, new_m, new_v)):
        for i, n in enumerate(_SMALL):
            dst[n] = outs[k * len(_SMALL) + i].reshape(w[n].shape)

    return (loss, dx.reshape(x.shape), *[grad[n] for n in _WEIGHTS], *[delta[n] for n in _WEIGHTS],
            *[new_m[n] for n in _WEIGHTS], *[new_v[n] for n in _WEIGHTS])
```

```python
import functools

import jax
import jax.numpy as jnp
from jax import lax
from jax.experimental import pallas as pl
from jax.experimental.pallas import tpu as pltpu

F32 = jnp.float32
BF16 = jnp.bfloat16
HI = lax.Precision.HIGHEST

N_DEV = 8
D = 1024
FF = 2816
BW = 512
NBR = 4
CHUNK = 64
EPS = 1e-6
LRU_C = 8.0
GDN_DK = 128

COL_AU, COL_AV, COL_BX, COL_BG = 0, 512, 1024, 1536
COL_CQ, COL_CK, COL_CV, COL_CZ = 2048, 2560, 3072, 3584
COL_DX, COL_GATE, COL_TAIL = 4096, 4608, 8704
PW = 9216
P_IN = 8712

ADAM_LR, ADAM_B1, ADAM_B2, ADAM_EPS, ADAM_WD, ADAM_STEP = 0.001, 0.9, 0.999, 1e-08, 0.01, 10

VMEM_LIMIT_V7X = 56 * 1024 * 1024

_NN = (((1,), (0,)), ((), ()))
_NT = (((1,), (1,)), ((), ()))
_TN = (((0,), (0,)), ((), ()))


def _cp(*sem):
    return pltpu.CompilerParams(dimension_semantics=tuple(sem), vmem_limit_bytes=VMEM_LIMIT_V7X)


def _dot(a, b, dims=_NN):
    return lax.dot_general(a.astype(BF16), b.astype(BF16), dims, preferred_element_type=F32)


def _dot_hi(a, b, dims=_NN):
    return lax.dot_general(a, b, dims, precision=HI, preferred_element_type=F32)


def _pick(n, cands):
    for c in cands:
        if n % c == 0:
            return c
    return n


@jax.custom_jvp
def _log1p(x):
    u = 1.0 + x
    return jnp.where(u == 1.0, x, x * jnp.log(u) / jnp.where(u == 1.0, 1.0, u - 1.0))


@_log1p.defjvp
def _log1p_jvp(p, t):
    (x,), (dx,) = p, t
    return _log1p(x), dx / (1.0 + x)


@jax.custom_jvp
def _expm1(x):
    u = jnp.exp(x)
    lu = jnp.log(u)
    small = (u == 1.0) | (lu == 0.0)
    return jnp.where(small, x, (u - 1.0) * x / jnp.where(small, 1.0, lu))


@_expm1.defjvp
def _expm1_jvp(p, t):
    (x,), (dx,) = p, t
    return _expm1(x), dx * jnp.exp(x)


def _softplus(x):
    return jnp.maximum(x, 0.0) + _log1p(jnp.exp(-jnp.abs(x)))


def _sigmoid(x):
    return jax.nn.sigmoid(x)


def _silu(x):
    return x * jax.nn.sigmoid(x)


def _gelu(x):
    return jax.nn.gelu(x)


@functools.partial(jax.custom_vjp, nondiff_argnums=(1,))
def _shift(x, s):
    return x if s == 0 else pltpu.roll(x, s, 0)


def _shift_fwd(x, s):
    return _shift(x, s), None


def _shift_bwd(s, _, g):
    n = g.shape[0]
    return (g if s == 0 else pltpu.roll(g, n - s, 0),)


_shift.defvjp(_shift_fwd, _shift_bwd)


def _scan_steps(a, b, reverse):
    n = a.shape[0]
    row = lax.broadcasted_iota(jnp.int32, a.shape, 0)
    k = 1
    while k < n:
        sh = n - k if reverse else k
        m = (row < n - k) if reverse else (row >= k)
        a_s = jnp.where(m, pltpu.roll(a, sh, 0), 1.0)
        b_s = jnp.where(m, pltpu.roll(b, sh, 0), 0.0)
        b = a * b_s + b
        a = a * a_s
        k *= 2
    return b


@jax.custom_vjp
def _scan(a, b):
    return _scan_steps(a, b, False)


def _scan_fwd(a, b):
    h = _scan_steps(a, b, False)
    return h, (a, h)


def _scan_bwd(res, dh):
    a, h = res
    n = a.shape[0]
    row = lax.broadcasted_iota(jnp.int32, a.shape, 0)
    a_next = jnp.where(row < n - 1, pltpu.roll(a, n - 1, 0), 0.0)
    g = _scan_steps(a_next, dh, True)
    h_prev = jnp.where(row >= 1, pltpu.roll(h, 1, 0), 0.0)
    return g * h_prev, g


_scan.defvjp(_scan_fwd, _scan_bwd)


def _mm(name, pairs, mode, out_dtype, *, res=None, scale=1.0, bm=None, bn=None, bk=None, after=None):
    a0, b0 = pairs[0]
    if mode == "nn":
        (M, K), N = a0.shape, b0.shape[1]
    elif mode == "nt":
        (M, K), N = a0.shape, b0.shape[0]
    else:
        (K, M), N = a0.shape, b0.shape[1]
    bm = bm or _pick(M, (1024, 512, 256, 128))
    bn = bn or _pick(N, (1024, 512, 256, 128))
    bk = bk or _pick(K, (1024, 512, 1408, 256, 128))
    nk = K // bk
    npair = len(pairs)
    dims = {"nn": _NN, "nt": _NT, "tn": _TN}[mode]

    def body(*refs):
        ab = refs[:2 * npair]
        pos = 2 * npair
        r_ref = None
        if res is not None:
            r_ref = refs[pos]
            pos += 1
        pos += after is not None
        o_ref = refs[pos]
        part = None
        for p in range(npair):
            d = _dot(ab[2 * p][...], ab[2 * p + 1][...], dims)
            part = d if part is None else part + d

        def finish(acc):
            out = acc if scale == 1.0 else acc * scale
            if r_ref is not None:
                out = out + r_ref[...]
            o_ref[...] = out.astype(out_dtype)

        if nk == 1:
            finish(part)
        else:
            acc_ref = refs[pos + 1]
            k = pl.program_id(2)

            @pl.when(k == 0)
            def _():
                acc_ref[...] = part

            @pl.when(k > 0)
            def _():
                acc_ref[...] += part

            @pl.when(k == nk - 1)
            def _():
                finish(acc_ref[...])

    if mode == "nn":
        a_spec = pl.BlockSpec((bm, bk), lambda i, j, k: (i, k))
        b_spec = pl.BlockSpec((bk, bn), lambda i, j, k: (k, j))
    elif mode == "nt":
        a_spec = pl.BlockSpec((bm, bk), lambda i, j, k: (i, k))
        b_spec = pl.BlockSpec((bn, bk), lambda i, j, k: (j, k))
    else:
        a_spec = pl.BlockSpec((bk, bm), lambda i, j, k: (k, i))
        b_spec = pl.BlockSpec((bk, bn), lambda i, j, k: (k, j))
    o_spec = pl.BlockSpec((bm, bn), lambda i, j, k: (i, j))
    in_specs, args = [], []
    for a, b in pairs:
        in_specs += [a_spec, b_spec]
        args += [a, b]
    if res is not None:
        in_specs.append(o_spec)
        args.append(res)
    if after is not None:
        in_specs.append(_ANY)
        args.append(after)
    return pl.pallas_call(
        body, name=name, grid=(M // bm, N // bn, nk),
        in_specs=in_specs, out_specs=o_spec,
        out_shape=jax.ShapeDtypeStruct((M, N), out_dtype),
        scratch_shapes=[pltpu.VMEM((bm, bn), F32)] if nk > 1 else [],
        compiler_params=_cp("parallel", "parallel", "arbitrary"),
    )(*args)


def _rms_fwd(name, x, g):
    T = x.shape[0]
    bm = _pick(T, (512, 256, 128))

    def body(x_ref, g_ref, o_ref):
        xv = x_ref[...]
        r = lax.rsqrt(jnp.mean(xv * xv, axis=-1, keepdims=True) + EPS)
        o_ref[...] = (xv * r * g_ref[...]).astype(BF16)

    return pl.pallas_call(
        body, name=name, grid=(T // bm,),
        in_specs=[pl.BlockSpec((bm, D), lambda i: (i, 0)), pl.BlockSpec((1, D), lambda i: (0, 0))],
        out_specs=pl.BlockSpec((bm, D), lambda i: (i, 0)),
        out_shape=jax.ShapeDtypeStruct((T, D), BF16),
        compiler_params=_cp("parallel"),
    )(x, g)


def _rms_bwd(name, x, g, dh, dres):
    T = x.shape[0]
    bm = _pick(T, (512, 256, 128))

    def body(x_ref, g_ref, dh_ref, dres_ref, dx_ref, dg_ref):
        xv = x_ref[...]
        r = lax.rsqrt(jnp.mean(xv * xv, axis=-1, keepdims=True) + EPS)
        xh = xv * r
        dhv = dh_ref[...]
        dxh = dhv * g_ref[...]
        dx_ref[...] = dres_ref[...] + r * (dxh - xh * jnp.mean(dxh * xh, axis=-1, keepdims=True))
        part = jnp.sum(dhv * xh, axis=0, keepdims=True)

        @pl.when(pl.program_id(0) == 0)
        def _():
            dg_ref[...] = part

        @pl.when(pl.program_id(0) > 0)
        def _():
            dg_ref[...] += part

    row = pl.BlockSpec((bm, D), lambda i: (i, 0))
    vec = pl.BlockSpec((1, D), lambda i: (0, 0))
    return pl.pallas_call(
        body, name=name, grid=(T // bm,),
        in_specs=[row, vec, row, row], out_specs=[row, vec],
        out_shape=[jax.ShapeDtypeStruct((T, D), F32), jax.ShapeDtypeStruct((1, D), F32)],
        compiler_params=_cp("arbitrary"),
    )(x, g, dh, dres)


def _final_loss(name, x, g, tgt):
    T = x.shape[0]
    bm = _pick(T, (512, 256, 128))

    def body(x_ref, g_ref, t_ref, loss_ref, dx_ref, dg_ref):
        xv = x_ref[...]
        gv = g_ref[...]
        r = lax.rsqrt(jnp.mean(xv * xv, axis=-1, keepdims=True) + EPS)
        xh = xv * r
        e = xh * gv - t_ref[...]
        lpart = jnp.broadcast_to(0.5 * jnp.sum(jnp.mean(e * e, axis=-1, keepdims=True), axis=0, keepdims=True), (1, 128))
        dy = e * (1.0 / D)
        dxh = dy * gv
        dx_ref[...] = r * (dxh - xh * jnp.mean(dxh * xh, axis=-1, keepdims=True))
        gpart = jnp.sum(dy * xh, axis=0, keepdims=True)

        @pl.when(pl.program_id(0) == 0)
        def _():
            loss_ref[...] = lpart
            dg_ref[...] = gpart

        @pl.when(pl.program_id(0) > 0)
        def _():
            loss_ref[...] += lpart
            dg_ref[...] += gpart

    row = pl.BlockSpec((bm, D), lambda i: (i, 0))
    vec = pl.BlockSpec((1, D), lambda i: (0, 0))
    return pl.pallas_call(
        body, name=name, grid=(T // bm,),
        in_specs=[row, vec, row],
        out_specs=[pl.BlockSpec((1, 128), lambda i: (0, 0)), row, vec],
        out_shape=[jax.ShapeDtypeStruct((1, 128), F32), jax.ShapeDtypeStruct((T, D), F32),
                   jax.ShapeDtypeStruct((1, D), F32)],
        compiler_params=_cp("arbitrary"),
    )(x, g, tgt)


def _ffn_up(name, h, wg, wu):
    T = h.shape[0]
    bm = _pick(T, (2048, 1024, 512, 256, 128))
    bn = 256

    def body(h_ref, wg_ref, wu_ref, sa_ref, ds_ref, act_ref):
        hv = h_ref[...]
        a = _dot(hv, wg_ref[...], _NT)
        b = _dot(hv, wu_ref[...], _NT)
        s = _sigmoid(a)
        sa = a * s
        sa_ref[...] = sa.astype(BF16)
        ds_ref[...] = (b * (s * (1.0 + a * (1.0 - s)))).astype(BF16)
        act_ref[...] = (sa * b).astype(BF16)

    w_spec = pl.BlockSpec((bn, D), lambda i, j: (j, 0))
    o_spec = pl.BlockSpec((bm, bn), lambda i, j: (i, j))
    return pl.pallas_call(
        body, name=name, grid=(T // bm, FF // bn),
        in_specs=[pl.BlockSpec((bm, D), lambda i, j: (i, 0)), w_spec, w_spec],
        out_specs=[o_spec, o_spec, o_spec],
        out_shape=[jax.ShapeDtypeStruct((T, FF), BF16)] * 3,
        compiler_params=_cp("parallel", "parallel"),
    )(h, wg, wu)


def _ffn_dact(name, dy, wd, sa, ds, after=None):
    T = dy.shape[0]
    bm = _pick(T, (2048, 1024, 512, 256, 128))
    bn = 256

    def body(dy_ref, wd_ref, sa_ref, ds_ref, *rest):
        da_ref, db_ref, dy_bf = rest[-3:]

        @pl.when(pl.program_id(1) == 0)
        def _():
            dy_bf[...] = dy_ref[...].astype(BF16)

        dact = 0.5 * _dot(dy_bf[...], wd_ref[...], _NT)
        da_ref[...] = (dact * ds_ref[...].astype(F32)).astype(BF16)
        db_ref[...] = (dact * sa_ref[...].astype(F32)).astype(BF16)

    t_spec = pl.BlockSpec((bm, bn), lambda i, j: (i, j))
    return pl.pallas_call(
        body, name=name, grid=(T // bm, FF // bn),
        in_specs=[pl.BlockSpec((bm, D), lambda i, j: (i, 0)), pl.BlockSpec((bn, D), lambda i, j: (j, 0)),
                  t_spec, t_spec] + [_ANY] * (after is not None),
        out_specs=[t_spec, t_spec],
        out_shape=[jax.ShapeDtypeStruct((T, FF), BF16), jax.ShapeDtypeStruct((T, FF), BF16)],
        scratch_shapes=[pltpu.VMEM((bm, D), BF16)],
        compiler_params=_cp("parallel", "arbitrary"),
    )(dy, wd, sa, ds, *([after] if after is not None else []))


def _merge_specs(T, bm, bn):
    y_spec = pl.BlockSpec((bm, BW), lambda i, j: (i, 0))
    wb_spec = pl.BlockSpec((NBR, bn, BW), lambda i, j: (0, j, 0))
    gate_specs = [pl.BlockSpec((bm, bn), functools.partial(lambda i, j, o: (i, o + j), o=(COL_GATE + g * D) // bn))
                  for g in range(NBR)]
    t_spec = pl.BlockSpec((bm, bn), lambda i, j: (i, j))
    return y_spec, wb_spec, gate_specs, t_spec


def _merge_fwd(name, ys, wb, proj):
    T = proj.shape[0]
    bm = _pick(T, (512, 256, 128))
    bn = 512
    y_spec, wb_spec, gate_specs, t_spec = _merge_specs(T, bm, bn)

    def body(y0, y1, y2, y3, wb_ref, g0, g1, g2, g3, o_ref):
        acc = None
        for g, (y_ref, g_ref) in enumerate(((y0, g0), (y1, g1), (y2, g2), (y3, g3))):
            t = _sigmoid(g_ref[...].astype(F32)) * _dot(y_ref[...], wb_ref[g], _NT)
            acc = t if acc is None else acc + t
        o_ref[...] = acc.astype(BF16)

    return pl.pallas_call(
        body, name=name, grid=(T // bm, D // bn),
        in_specs=[y_spec] * NBR + [wb_spec] + gate_specs, out_specs=t_spec,
        out_shape=jax.ShapeDtypeStruct((T, D), BF16),
        compiler_params=_cp("parallel", "parallel"),
    )(*ys, wb, proj, proj, proj, proj)


def _merge_bwd(name, dm, ys, wb, proj):
    T = proj.shape[0]
    bm = _pick(T, (512, 256, 128))
    bn = 512
    y_spec, wb_spec, gate_specs, t_spec = _merge_specs(T, bm, bn)

    def body(dm_ref, y0, y1, y2, y3, wb_ref, g0, g1, g2, g3, *outs):
        dmv = dm_ref[...]
        j = pl.program_id(1)
        for g, (y_ref, g_ref) in enumerate(((y0, g0), (y1, g1), (y2, g2), (y3, g3))):
            br = _dot(y_ref[...], wb_ref[g], _NT)
            s = _sigmoid(g_ref[...].astype(F32))
            outs[g][...] = (dmv * br * (s * (1.0 - s))).astype(BF16)
            dbr = (dmv * s).astype(BF16)
            outs[NBR + g][...] = dbr
            part = _dot(dbr, wb_ref[g])
            dy_ref = outs[2 * NBR + g]

            @pl.when(j == 0)
            def _():
                dy_ref[...] = part

            @pl.when(j > 0)
            def _():
                dy_ref[...] += part

    return pl.pallas_call(
        body, name=name, grid=(T // bm, D // bn),
        in_specs=[t_spec] + [y_spec] * NBR + [wb_spec] + gate_specs, out_specs=[t_spec] * (2 * NBR) + [y_spec] * NBR,
        out_shape=[jax.ShapeDtypeStruct((T, D), BF16)] * (2 * NBR) + [jax.ShapeDtypeStruct((T, BW), F32)] * NBR,
        compiler_params=_cp("parallel", "arbitrary"),
    )(dm, *ys, wb, proj, proj, proj, proj)


def _sgu_block(u_pre, v_pre, ln_g, ln_b, w, bias):
    u = _gelu(u_pre)
    vf = _gelu(v_pre)
    mu = jnp.mean(vf, axis=-1, keepdims=True)
    var = jnp.mean(jnp.square(vf - mu), axis=-1, keepdims=True)
    vn = (vf - mu) * lax.rsqrt(var + EPS) * ln_g + ln_b
    ri = lax.broadcasted_iota(jnp.int32, (128, 128), 0)
    ci = lax.broadcasted_iota(jnp.int32, (128, 128), 1)
    mask = (ri // CHUNK) >= (ci // CHUNK)
    outs = [_dot(jnp.where(mask, w[g], 0.0), vn[:, g * 128:(g + 1) * 128]) for g in range(4)]
    mixed = jnp.concatenate(outs, axis=1) + bias
    return u * mixed


def _sgu_param_specs():
    return [pl.BlockSpec((1, BW), lambda i: (0, 0)), pl.BlockSpec((1, BW), lambda i: (0, 0)),
            pl.BlockSpec((4, 128, 128), lambda i: (0, 0, 0)), pl.BlockSpec((128, BW), lambda i: (0, 0))]


def _sgu_fwd(name, proj, ln_g, ln_b, w, bias):
    T = proj.shape[0]
    rb = _pick(T, (256, 128))

    def body(u_ref, v_ref, g_ref, b_ref, w_ref, bias_ref, y_ref):
        for n in range(rb // 128):
            rows = slice(n * 128, (n + 1) * 128)
            y = _sgu_block(u_ref[rows, :].astype(F32), v_ref[rows, :].astype(F32), g_ref[...], b_ref[...], w_ref[...],
                           bias_ref[...])
            y_ref[rows, :] = y.astype(BF16)

    return pl.pallas_call(
        body, name=name, grid=(T // rb,),
        in_specs=[pl.BlockSpec((rb, BW), lambda i: (i, COL_AU // BW)), pl.BlockSpec((rb, BW), lambda i: (i, COL_AV // BW))]
        + _sgu_param_specs(),
        out_specs=pl.BlockSpec((rb, BW), lambda i: (i, 0)),
        out_shape=jax.ShapeDtypeStruct((T, BW), BF16),
        compiler_params=_cp("parallel"),
    )(proj, proj, ln_g, ln_b, w, bias)


def _sgu_bwd(name, proj, dy, ln_g, ln_b, w, bias):
    T = proj.shape[0]
    rb = _pick(T, (256, 128))

    def body(u_ref, v_ref, dy_ref, g_ref, b_ref, w_ref, bias_ref, du_ref, dv_ref, dg_ref, db_ref, dw_ref, dbias_ref):
        acc = None
        for n in range(rb // 128):
            rows = slice(n * 128, (n + 1) * 128)
            _, vjp = jax.vjp(_sgu_block, u_ref[rows, :].astype(F32), v_ref[rows, :].astype(F32), g_ref[...], b_ref[...],
                             w_ref[...],
                             bias_ref[...])
            du, dv, *dp = vjp(dy_ref[rows, :])
            du_ref[rows, :] = du.astype(BF16)
            dv_ref[rows, :] = dv.astype(BF16)
            acc = dp if acc is None else [p + q for p, q in zip(acc, dp)]

        @pl.when(pl.program_id(0) == 0)
        def _():
            for r, p in zip((dg_ref, db_ref, dw_ref, dbias_ref), acc):
                r[...] = p

        @pl.when(pl.program_id(0) > 0)
        def _():
            for r, p in zip((dg_ref, db_ref, dw_ref, dbias_ref), acc):
                r[...] += p

    row = pl.BlockSpec((rb, BW), lambda i: (i, 0))
    return pl.pallas_call(
        body, name=name, grid=(T // rb,),
        in_specs=[pl.BlockSpec((rb, BW), lambda i: (i, COL_AU // BW)), pl.BlockSpec((rb, BW), lambda i: (i, COL_AV // BW)),
                  row] + _sgu_param_specs(),
        out_specs=[row, row] + _sgu_param_specs(),
        out_shape=[jax.ShapeDtypeStruct((T, BW), BF16), jax.ShapeDtypeStruct((T, BW), BF16),
                   jax.ShapeDtypeStruct((1, BW), F32), jax.ShapeDtypeStruct((1, BW), F32),
                   jax.ShapeDtypeStruct((4, 128, 128), F32), jax.ShapeDtypeStruct((128, BW), F32)],
        compiler_params=_cp("arbitrary"),
    )(proj, proj, dy, ln_g, ln_b, w, bias)


def _halo_block(ref, i, rblk, halo):
    r0 = pl.multiple_of(i * rblk, rblk)
    h0 = pl.multiple_of(jnp.maximum(r0 - 16, 0), 16)
    top = jnp.where(i > 0, ref[pl.ds(h0, 16), :].astype(F32), 0.0)[16 - halo:]
    return jnp.concatenate([top, ref[pl.ds(r0, rblk), :].astype(F32)], axis=0)


def _with_halo_grad(dfull, pending, halo, rblk):
    tail = jnp.concatenate([jnp.zeros((rblk - halo, 128), F32), pending], axis=0)
    return dfull[halo:] + tail


def _conv4(xfull, rows):
    acc = None
    for k in range(4):
        t = rows[k] * _shift(xfull, 3 - k)[8:]
        acc = t if acc is None else acc + t
    return acc


def _lru_block(xfull, gate, h0, c0, c1, c2, c3, cb, wa, ba, wx, bx, lam):
    n = gate.shape[0]
    xc = _conv4(xfull, (c0, c1, c2, c3)) + cb
    r = _sigmoid(_dot(xc, wa) + ba)
    ig = _sigmoid(_dot(xc, wx) + bx)
    log_a = -LRU_C * r * _softplus(-lam)
    a = jnp.exp(log_a)
    mult = jnp.sqrt(-_expm1(2.0 * log_a))
    b = mult * (ig * xc)
    row = lax.broadcasted_iota(jnp.int32, (n, 128), 0)
    b = b + jnp.where(row == 0, a * h0, 0.0)
    h = _scan(a, b)
    out = h * _gelu(gate)
    h_last = jnp.sum(jnp.where(row == n - 1, h, 0.0), axis=0, keepdims=True)
    return out, h_last


def _lru_param_specs():
    vec = pl.BlockSpec((1, 128), lambda g: (0, g))
    mat = pl.BlockSpec((None, 128, 128), lambda g: (g, 0, 0))
    return [pl.BlockSpec((4, 128), lambda g: (0, g)), vec, mat, vec, mat, vec, vec]


def _lru_load_params(cw_ref, cb_ref, wa_ref, ba_ref, wx_ref, bx_ref, lam_ref):
    return (cw_ref[0:1, :], cw_ref[1:2, :], cw_ref[2:3, :], cw_ref[3:4, :], cb_ref[...], wa_ref[...], ba_ref[...],
            wx_ref[...], bx_ref[...], lam_ref[...])


def _lru_fwd(name, proj, cw, cb, wa, ba, wx, bx, lam):
    T = proj.shape[0]
    rblk = _pick(T, (256, 128))
    nblk = T // rblk

    def body(x_ref, gt_ref, cw_ref, cb_ref, wa_ref, ba_ref, wx_ref, bx_ref, lam_ref, y_ref, hc_ref):
        params = _lru_load_params(cw_ref, cb_ref, wa_ref, ba_ref, wx_ref, bx_ref, lam_ref)

        def step(i, h0):
            r0 = pl.multiple_of(i * rblk, rblk)
            out, h_last = _lru_block(_halo_block(x_ref, i, rblk, 8), gt_ref[pl.ds(r0, rblk), :].astype(F32), h0,
                                     *params)
            y_ref[pl.ds(r0, rblk), :] = out.astype(BF16)
            hc_ref[pl.ds(pl.multiple_of(i * 8, 8), 8), :] = jnp.broadcast_to(h0, (8, 128))
            return h_last

        lax.fori_loop(0, nblk, step, jnp.zeros((1, 128), F32))

    return pl.pallas_call(
        body, name=name, grid=(4,),
        in_specs=[pl.BlockSpec((T, 128), lambda g: (0, COL_BX // 128 + g)),
                  pl.BlockSpec((T, 128), lambda g: (0, COL_BG // 128 + g))] + _lru_param_specs(),
        out_specs=[pl.BlockSpec((T, 128), lambda g: (0, g)), pl.BlockSpec((nblk * 8, 128), lambda g: (0, g))],
        out_shape=[jax.ShapeDtypeStruct((T, BW), BF16), jax.ShapeDtypeStruct((nblk * 8, BW), F32)],
        compiler_params=_cp("parallel"),
    )(proj, proj, cw, cb, wa, ba, wx, bx, lam)


def _lru_bwd(name, proj, dy, hc, cw, cb, wa, ba, wx, bx, lam):
    T = proj.shape[0]
    rblk = _pick(T, (256, 128))
    nblk = T // rblk

    def body(x_ref, gt_ref, dy_ref, hc_ref, cw_ref, cb_ref, wa_ref, ba_ref, wx_ref, bx_ref, lam_ref,
             dx_ref, dgt_ref, dcw_ref, dcb_ref, dwa_ref, dba_ref, dwx_ref, dbx_ref, dlam_ref):
        params = _lru_load_params(cw_ref, cb_ref, wa_ref, ba_ref, wx_ref, bx_ref, lam_ref)

        def step(it, carry):
            dh_last, pending, acc = carry
            i = nblk - 1 - it
            r0 = pl.multiple_of(i * rblk, rblk)
            h0 = hc_ref[pl.ds(pl.multiple_of(i * 8, 8), 1), :]
            _, vjp = jax.vjp(_lru_block, _halo_block(x_ref, i, rblk, 8), gt_ref[pl.ds(r0, rblk), :].astype(F32), h0,
                             *params)
            dfull, dgate, dh0, *dp = vjp((dy_ref[pl.ds(r0, rblk), :], dh_last))
            dx_ref[pl.ds(r0, rblk), :] = _with_halo_grad(dfull, pending, 8, rblk).astype(BF16)
            dgt_ref[pl.ds(r0, rblk), :] = dgate.astype(BF16)
            return dh0, dfull[:8], tuple(p + q for p, q in zip(acc, dp))

        zeros = tuple(jnp.zeros(p.shape, F32) for p in params)
        _, _, acc = lax.fori_loop(0, nblk, step, (jnp.zeros((1, 128), F32), jnp.zeros((8, 128), F32), zeros))
        for k in range(4):
            dcw_ref[k:k + 1, :] = acc[k]
        for r, p in zip((dcb_ref, dwa_ref, dba_ref, dwx_ref, dbx_ref, dlam_ref), acc[4:]):
            r[...] = p

    col = pl.BlockSpec((T, 128), lambda g: (0, g))
    return pl.pallas_call(
        body, name=name, grid=(4,),
        in_specs=[pl.BlockSpec((T, 128), lambda g: (0, COL_BX // 128 + g)),
                  pl.BlockSpec((T, 128), lambda g: (0, COL_BG // 128 + g)), col,
                  pl.BlockSpec((nblk * 8, 128), lambda g: (0, g))] + _lru_param_specs(),
        out_specs=[col, col] + _lru_param_specs(),
        out_shape=[jax.ShapeDtypeStruct((T, BW), BF16), jax.ShapeDtypeStruct((T, BW), BF16),
                   jax.ShapeDtypeStruct((4, BW), F32), jax.ShapeDtypeStruct((1, BW), F32),
                   jax.ShapeDtypeStruct((4, 128, 128), F32), jax.ShapeDtypeStruct((1, BW), F32),
                   jax.ShapeDtypeStruct((4, 128, 128), F32), jax.ShapeDtypeStruct((1, BW), F32),
                   jax.ShapeDtypeStruct((1, BW), F32)],
        compiler_params=_cp("parallel"),
    )(proj, proj, dy, hc, cw, cb, wa, ba, wx, bx, lam)


def _conv_block(xfull, c0, c1, c2, c3):
    return _silu(_conv4(xfull, (c0, c1, c2, c3)))


def _conv_fwd(name, proj, col0, cw, cw_col0):
    T = proj.shape[0]
    rblk = _pick(T, (256, 128))
    nblk = T // rblk

    def body(x_ref, cw_ref, y_ref):
        rows = (cw_ref[0:1, :], cw_ref[1:2, :], cw_ref[2:3, :], cw_ref[3:4, :])

        def step(i, c):
            r0 = pl.multiple_of(i * rblk, rblk)
            y_ref[pl.ds(r0, rblk), :] = _conv_block(_halo_block(x_ref, i, rblk, 8), *rows)
            return c

        lax.fori_loop(0, nblk, step, 0)

    return pl.pallas_call(
        body, name=name, grid=(4,),
        in_specs=[pl.BlockSpec((T, 128), lambda g: (0, col0 // 128 + g)),
                  pl.BlockSpec((4, 128), lambda g: (0, cw_col0 // 128 + g))],
        out_specs=pl.BlockSpec((T, 128), lambda g: (0, g)),
        out_shape=jax.ShapeDtypeStruct((T, BW), F32),
        compiler_params=_cp("parallel"),
    )(proj, cw)


def _conv_bwd(name, proj, col0, dy, cw, cw_col0):
    T = proj.shape[0]
    rblk = _pick(T, (256, 128))
    nblk = T // rblk

    def body(x_ref, dy_ref, cw_ref, dx_ref, dcw_ref):
        rows = (cw_ref[0:1, :], cw_ref[1:2, :], cw_ref[2:3, :], cw_ref[3:4, :])

        def step(it, carry):
            pending, acc = carry
            i = nblk - 1 - it
            r0 = pl.multiple_of(i * rblk, rblk)
            _, vjp = jax.vjp(_conv_block, _halo_block(x_ref, i, rblk, 8), *rows)
            dfull, *dp = vjp(dy_ref[pl.ds(r0, rblk), :])
            dx_ref[pl.ds(r0, rblk), :] = _with_halo_grad(dfull, pending, 8, rblk).astype(BF16)
            return dfull[:8], tuple(p + q for p, q in zip(acc, dp))

        zeros = tuple(jnp.zeros((1, 128), F32) for _ in range(4))
        _, acc = lax.fori_loop(0, nblk, step, (jnp.zeros((8, 128), F32), zeros))
        for k in range(4):
            dcw_ref[k:k + 1, :] = acc[k]

    col = pl.BlockSpec((T, 128), lambda g: (0, g))
    return pl.pallas_call(
        body, name=name, grid=(4,),
        in_specs=[pl.BlockSpec((T, 128), lambda g: (0, col0 // 128 + g)), col,
                  pl.BlockSpec((4, 128), lambda g: (0, cw_col0 // 128 + g))],
        out_specs=[col, pl.BlockSpec((4, 128), lambda g: (0, g))],
        out_shape=[jax.ShapeDtypeStruct((T, BW), BF16), jax.ShapeDtypeStruct((4, BW), F32)],
        compiler_params=_cp("parallel"),
    )(proj, dy, cw)


def _pool_block(xfull, pw, sc, t0, gi):
    n = xfull.shape[0] - 16
    s2 = xfull + _shift(xfull, 1)
    s4 = s2 + _shift(s2, 2)
    s8 = s4 + _shift(s4, 4)
    s16 = s8 + _shift(s8, 8)
    s = jnp.where(gi == 0, s2, jnp.where(gi == 1, s4, jnp.where(gi == 2, s8, s16)))[16:]
    t = t0 + lax.broadcasted_iota(jnp.int32, (n, 128), 0)
    cnt = jnp.minimum(t + 1, lax.shift_left(jnp.int32(2), gi)).astype(F32)
    pooled = s / cnt - xfull[16:]
    return _dot(pooled, pw) * sc


def _pool_fwd(name, proj, pw, sc):
    T = proj.shape[0]
    rblk = _pick(T, (256, 128))
    nblk = T // rblk

    def body(x_ref, pw_ref, sc_ref, y_ref):
        gi = pl.program_id(0)

        def step(i, c):
            r0 = pl.multiple_of(i * rblk, rblk)
            y = _pool_block(_halo_block(x_ref, i, rblk, 16), pw_ref[...], sc_ref[...], r0, gi)
            y_ref[pl.ds(r0, rblk), :] = y.astype(BF16)
            return c

        lax.fori_loop(0, nblk, step, 0)

    return pl.pallas_call(
        body, name=name, grid=(4,),
        in_specs=[pl.BlockSpec((T, 128), lambda g: (0, COL_DX // 128 + g)),
                  pl.BlockSpec((None, 128, 128), lambda g: (g, 0, 0)), pl.BlockSpec((1, 128), lambda g: (0, g))],
        out_specs=pl.BlockSpec((T, 128), lambda g: (0, g)),
        out_shape=jax.ShapeDtypeStruct((T, BW), BF16),
        compiler_params=_cp("parallel"),
    )(proj, pw, sc)


def _pool_bwd(name, proj, dy, pw, sc):
    T = proj.shape[0]
    rblk = _pick(T, (256, 128))
    nblk = T // rblk

    def body(x_ref, dy_ref, pw_ref, sc_ref, dx_ref, dpw_ref, dsc_ref):
        gi = pl.program_id(0)

        def step(it, carry):
            pending, apw, asc = carry
            i = nblk - 1 - it
            r0 = pl.multiple_of(i * rblk, rblk)
            _, vjp = jax.vjp(lambda xf, w, s: _pool_block(xf, w, s, r0, gi), _halo_block(x_ref, i, rblk, 16),
                             pw_ref[...], sc_ref[...])
            dfull, dw, ds = vjp(dy_ref[pl.ds(r0, rblk), :])
            dx_ref[pl.ds(r0, rblk), :] = _with_halo_grad(dfull, pending, 16, rblk).astype(BF16)
            return dfull[:16], apw + dw, asc + ds

        _, apw, asc = lax.fori_loop(0, nblk, step, (jnp.zeros((16, 128), F32), jnp.zeros((128, 128), F32),
                                                    jnp.zeros((1, 128), F32)))
        dpw_ref[...] = apw
        dsc_ref[...] = asc

    col = pl.BlockSpec((T, 128), lambda g: (0, g))
    mat = pl.BlockSpec((None, 128, 128), lambda g: (g, 0, 0))
    vec = pl.BlockSpec((1, 128), lambda g: (0, g))
    return pl.pallas_call(
        body, name=name, grid=(4,),
        in_specs=[pl.BlockSpec((T, 128), lambda g: (0, COL_DX // 128 + g)), col, mat, vec],
        out_specs=[col, mat, vec],
        out_shape=[jax.ShapeDtypeStruct((T, BW), BF16), jax.ShapeDtypeStruct((4, 128, 128), F32),
                   jax.ShapeDtypeStruct((1, BW), F32)],
        compiler_params=_cp("parallel"),
    )(proj, dy, pw, sc)


@jax.custom_vjp
def _dot3(a, b):
    ah = a.astype(BF16)
    al = (a - ah.astype(F32)).astype(BF16)
    bh = b.astype(BF16)
    bl = (b - bh.astype(F32)).astype(BF16)

    def d(x, y):
        return lax.dot_general(x, y, _NN, preferred_element_type=F32)

    return d(ah, bh) + (d(ah, bl) + d(al, bh))


def _dot3_fwd(a, b):
    return _dot3(a, b), (a, b)


def _dot3_bwd(res, g):
    a, b = res
    return _dot(g, b, _NT), _dot(a, g, _TN)


_dot3.defvjp(_dot3_fwd, _dot3_bwd)


def _pad_rows2(x):
    return jnp.concatenate([x, jnp.zeros_like(x)], axis=0)


@jax.custom_vjp
def _tri_inv(mats):
    n = mats[0].shape[0]
    eye = (lax.broadcasted_iota(jnp.int32, (n, n), 0) == lax.broadcasted_iota(jnp.int32, (n, n), 1)).astype(F32)
    ps = [eye - a for a in mats]
    ms = list(mats)
    k = 2
    while k < n:
        ms = [_dot3(t, t) for t in ms]
        ps = [p + _dot3(p, t) for p, t in zip(ps, ms)]
        k *= 2
    return ps


def _tri_inv_fwd(mats):
    ts = _tri_inv(mats)
    return ts, ts


def _tri_inv_bwd(ts, gs):
    half = [_dot(t, g, _TN) for t, g in zip(ts, gs)]
    return ([-_dot(h, t, _NT) for h, t in zip(half, ts)],)


_tri_inv.defvjp(_tri_inv_fwd, _tri_inv_bwd)


def _cumsum_rows(x):
    n = x.shape[0]
    row = lax.broadcasted_iota(jnp.int32, x.shape, 0)
    k = 1
    while k < n:
        x = x + jnp.where(row >= k, _shift(x, k), 0.0)
        k *= 2
    return x


def _gdn_prep(qcs, kcs, vcs, tails, alog, dtb):
    C = CHUNK
    pairs = [(c, h) for c in range(len(qcs)) for h in range(4)]
    lane = lax.broadcasted_iota(jnp.int32, (C, 128), 1)
    row = lax.broadcasted_iota(jnp.int32, (C, 128), 0)
    incl = row >= lane
    sig = [_sigmoid(t) for t in tails]
    gfull = [-jnp.exp(alog) * _softplus(t + dtb) for t in tails]
    beta = [jnp.sum(jnp.where(lane == h, sig[c], 0.0), axis=1, keepdims=True) for c, h in pairs]
    g = [jnp.sum(jnp.where(lane == h + 4, gfull[c], 0.0), axis=1, keepdims=True) for c, h in pairs]
    qs = [qcs[c][:, h * 128:(h + 1) * 128] for c, h in pairs]
    ks = [kcs[c][:, h * 128:(h + 1) * 128] for c, h in pairs]
    vs = [vcs[c][:, h * 128:(h + 1) * 128] for c, h in pairs]
    q = [t * lax.rsqrt(jnp.sum(t * t, axis=-1, keepdims=True) + EPS) * (GDN_DK ** -0.5) for t in qs]
    k = [t * lax.rsqrt(jnp.sum(t * t, axis=-1, keepdims=True) + EPS) for t in ks]
    gc = [_cumsum_rows(jnp.broadcast_to(t, (C, 128))) for t in g]
    gc_t = [jnp.transpose(jnp.concatenate([t, t], axis=0)) for t in gc]
    gc_col = [jnp.sum(jnp.where(lane == 0, t, 0.0), axis=1, keepdims=True) for t in gc]
    ri = lax.broadcasted_iota(jnp.int32, (C, C), 0)
    ci = lax.broadcasted_iota(jnp.int32, (C, C), 1)
    decay = [jnp.exp(jnp.where(incl, a - b[:C, :], -1e30)) for a, b in zip(gc, gc_t)]
    decay_sq = [jnp.exp(jnp.where(ri > ci, a - jnp.transpose(b)[:C, :], -1e30)) for a, b in zip(gc_col, gc)]
    kb = [a * b for a, b in zip(k, beta)]
    kk = [_dot(a, b, _NT) for a, b in zip(kb, k)]
    t_mat = _tri_inv([jnp.where(ri > ci, a * b, 0.0) for a, b in zip(kk, decay_sq)])
    egc = [jnp.exp(t) for t in gc]
    u = [_dot(t, a * b) for t, a, b in zip(t_mat, vs, beta)]
    w = [_dot(t, a * b) for t, a, b in zip(t_mat, kb, egc)]
    qk = [_dot(a, _pad_rows2(b), _NT) for a, b in zip(q, k)]
    attn = [jnp.where(incl, a * b, 0.0) for a, b in zip(qk, decay)]
    g_last = [jnp.sum(jnp.where(row == C - 1, t, 0.0), axis=0, keepdims=True) for t in gc]
    qe = [a * b for a, b in zip(q, egc)]
    kd = [a * jnp.exp(b - c_) for a, b, c_ in zip(k, g_last, gc)]
    egl = [jnp.exp(t) for t in g_last]

    def per_chunk(vals):
        return [jnp.concatenate(vals[4 * c:4 * c + 4], axis=1) for c in range(len(qcs))]

    return tuple(per_chunk(t) for t in (u, w, qe, kd, attn, egl))


def _gdn_scan_chunk(states, u, w, qe, kd, attn, egl, z, ng):
    hs = range(4)

    def sl(t, h):
        return t[:, h * 128:(h + 1) * 128]

    ws = [_dot(sl(w, h), states[h]) for h in hs]
    qs = [_dot(sl(qe, h), states[h]) for h in hs]
    v_new = [sl(u, h) - ws[h] for h in hs]
    av = [_dot(sl(attn, h), _pad_rows2(v_new[h])) for h in hs]
    kv = [_dot(sl(kd, h), v_new[h], _TN) for h in hs]
    nxt = tuple(states[h] * sl(egl, h) + kv[h] for h in hs)
    o = [qs[h] + av[h] for h in hs]
    on = [t * lax.rsqrt(jnp.mean(t * t, axis=-1, keepdims=True) + EPS) * ng for t in o]
    return nxt, jnp.concatenate(on, axis=1) * _silu(z)


def _gdn_blocks(T):
    tb = _pick(T, (512, 256, 128, 64))
    return tb, T // tb, tb // CHUNK


PREP_CHUNKS = 4


def _chunk_rows(i, n):
    return [pl.ds(pl.multiple_of((i * n + j) * CHUNK, CHUNK), CHUNK) for j in range(n)]


def _egl_rows(i, n, size):
    return [pl.ds(pl.multiple_of((i * n + j) * 8, 8), size) for j in range(n)]


def _gdn_prep_fwd(name, qa, ka, va, proj, alog, dtb):
    T = proj.shape[0]
    tb, nb, ncb = _gdn_blocks(T)
    n = PREP_CHUNKS if ncb % PREP_CHUNKS == 0 else 1

    def body(q_ref, k_ref, v_ref, tail_ref, alog_ref, dtb_ref, u_ref, w_ref, qe_ref, kd_ref, at_ref, egl_ref):
        def step(i, c):
            rows = _chunk_rows(i, n)
            u, w, qe, kd, at, egl = _gdn_prep([q_ref[r, :] for r in rows], [k_ref[r, :] for r in rows],
                                              [v_ref[r, :] for r in rows], [tail_ref[r, :].astype(F32) for r in rows],
                                              alog_ref[...], dtb_ref[...])
            for j, (r, e) in enumerate(zip(rows, _egl_rows(i, n, 8))):
                u_ref[r, :] = u[j]
                w_ref[r, :] = w[j].astype(BF16)
                qe_ref[r, :] = qe[j].astype(BF16)
                kd_ref[r, :] = kd[j].astype(BF16)
                at_ref[r, :] = at[j].astype(BF16)
                egl_ref[e, :] = jnp.broadcast_to(egl[j], (8, BW))
            return c

        lax.fori_loop(0, ncb // n, step, 0)

    blk = pl.BlockSpec((tb, BW), lambda j: (j, 0))
    vec = pl.BlockSpec((1, 128), lambda j: (0, 0))
    return pl.pallas_call(
        body, name=name, grid=(nb,),
        in_specs=[blk, blk, blk, pl.BlockSpec((tb, 128), lambda j: (j, COL_TAIL // 128)), vec, vec],
        out_specs=[blk] * 5 + [pl.BlockSpec((ncb * 8, BW), lambda j: (j, 0))],
        out_shape=[jax.ShapeDtypeStruct((T, BW), F32)] + [jax.ShapeDtypeStruct((T, BW), BF16)] * 4
        + [jax.ShapeDtypeStruct((T // 8, BW), F32)],
        compiler_params=_cp("parallel"),
    )(qa, ka, va, proj, alog, dtb)


def _gdn_prep_bwd(name, qa, ka, va, proj, alog, dtb, du, dw, dqe, dkd, dat, degl):
    T = proj.shape[0]
    tb, nb, ncb = _gdn_blocks(T)
    n = PREP_CHUNKS if ncb % PREP_CHUNKS == 0 else 1

    def body(q_ref, k_ref, v_ref, tail_ref, alog_ref, dtb_ref, du_ref, dw_ref, dqe_ref, dkd_ref, dat_ref, degl_ref,
             dq_ref, dk_ref, dv_ref, dtail_ref, dalog_ref, ddtb_ref):
        first = pl.program_id(0) == 0

        def step(i, carry):
            pa, pd = carry
            rows = _chunk_rows(i, n)
            _, vjp = jax.vjp(_gdn_prep, [q_ref[r, :] for r in rows], [k_ref[r, :] for r in rows],
                             [v_ref[r, :] for r in rows], [tail_ref[r, :].astype(F32) for r in rows], alog_ref[...], dtb_ref[...])
            cot = tuple([ref[r, :] for r in rows] for ref in (du_ref, dw_ref, dqe_ref, dkd_ref, dat_ref))
            dq, dk, dv, dtail, da, dd = vjp(cot + ([degl_ref[e, :] for e in _egl_rows(i, n, 1)],))
            for j, r in enumerate(rows):
                dq_ref[r, :] = dq[j]
                dk_ref[r, :] = dk[j]
                dv_ref[r, :] = dv[j]
                dtail_ref[r, :] = dtail[j].astype(BF16)
            return pa + da, pd + dd

        zv = jnp.zeros((1, 128), F32)
        pa, pd = lax.fori_loop(0, ncb // n, step, (zv, zv))

        @pl.when(first)
        def _():
            dalog_ref[...] = pa
            ddtb_ref[...] = pd

        @pl.when(jnp.logical_not(first))
        def _():
            dalog_ref[...] += pa
            ddtb_ref[...] += pd

    blk = pl.BlockSpec((tb, BW), lambda j: (j, 0))
    vec = pl.BlockSpec((1, 128), lambda j: (0, 0))
    return pl.pallas_call(
        body, name=name, grid=(nb,),
        in_specs=[blk, blk, blk, pl.BlockSpec((tb, 128), lambda j: (j, COL_TAIL // 128)), vec, vec]
        + [blk] * 5 + [pl.BlockSpec((ncb * 8, BW), lambda j: (j, 0))],
        out_specs=[blk, blk, blk, pl.BlockSpec((tb, 128), lambda j: (j, 0)), vec, vec],
        out_shape=[jax.ShapeDtypeStruct((T, BW), F32)] * 3 + [jax.ShapeDtypeStruct((T, 128), BF16)]
        + [jax.ShapeDtypeStruct((1, 128), F32)] * 2,
        compiler_params=_cp("arbitrary"),
    )(qa, ka, va, proj, alog, dtb, du, dw, dqe, dkd, dat, degl)


def _gdn_fwd(name, u, w, qe, kd, at, egl, proj, ng):
    T = proj.shape[0]
    tb, nb, ncb = _gdn_blocks(T)

    def body(u_ref, w_ref, qe_ref, kd_ref, at_ref, egl_ref, z_ref, ng_ref, y_ref, sh_ref, state):
        @pl.when(pl.program_id(0) == 0)
        def _():
            state[...] = jnp.zeros((4, 128, 128), F32)

        def step(c, states):
            rows = pl.ds(pl.multiple_of(c * CHUNK, CHUNK), CHUNK)
            for h in range(4):
                sh_ref[h, c] = states[h]
            nxt, y = _gdn_scan_chunk(states, u_ref[rows, :], w_ref[rows, :], qe_ref[rows, :], kd_ref[rows, :],
                                     at_ref[rows, :], egl_ref[pl.ds(pl.multiple_of(c * 8, 8), 1), :],
                                     z_ref[rows, :].astype(F32),
                                     ng_ref[...])
            y_ref[rows, :] = y.astype(BF16)
            return nxt

        states = lax.fori_loop(0, ncb, step, tuple(state[h] for h in range(4)))
        for h in range(4):
            state[h] = states[h]

    blk = pl.BlockSpec((tb, BW), lambda j: (j, 0))
    vec = pl.BlockSpec((1, 128), lambda j: (0, 0))
    return pl.pallas_call(
        body, name=name, grid=(nb,),
        in_specs=[blk] * 5 + [pl.BlockSpec((ncb * 8, BW), lambda j: (j, 0)),
                              pl.BlockSpec((tb, BW), lambda j: (j, COL_CZ // BW)), vec],
        out_specs=[blk, pl.BlockSpec((4, ncb, 128, 128), lambda j: (0, j, 0, 0))],
        out_shape=[jax.ShapeDtypeStruct((T, BW), BF16), jax.ShapeDtypeStruct((4, T // CHUNK, 128, 128), F32)],
        scratch_shapes=[pltpu.VMEM((4, 128, 128), F32)],
        compiler_params=_cp("arbitrary"),
    )(u, w, qe, kd, at, egl, proj, ng)


def _gdn_bwd(name, u, w, qe, kd, at, egl, proj, dy, sh, ng):
    T = proj.shape[0]
    tb, nb, ncb = _gdn_blocks(T)

    def body(u_ref, w_ref, qe_ref, kd_ref, at_ref, egl_ref, z_ref, dy_ref, sh_ref, ng_ref,
             du_ref, dw_ref, dqe_ref, dkd_ref, dat_ref, degl_ref, dz_ref, dng_ref, dstate):
        first = pl.program_id(0) == 0

        @pl.when(first)
        def _():
            dstate[...] = jnp.zeros((4, 128, 128), F32)

        def step(it, carry):
            dstates, pn = carry
            c = ncb - 1 - it
            rows = pl.ds(pl.multiple_of(c * CHUNK, CHUNK), CHUNK)
            erow = pl.multiple_of(c * 8, 8)
            _, vjp = jax.vjp(_gdn_scan_chunk, tuple(sh_ref[h, c] for h in range(4)), u_ref[rows, :],
                             w_ref[rows, :].astype(F32), qe_ref[rows, :].astype(F32), kd_ref[rows, :].astype(F32),
                             at_ref[rows, :].astype(F32), egl_ref[pl.ds(erow, 1), :], z_ref[rows, :].astype(F32),
                             ng_ref[...])
            nxt, du, dw, dqe, dkd, dat, degl, dz, dn = vjp((dstates, dy_ref[rows, :]))
            du_ref[rows, :] = du
            dw_ref[rows, :] = dw
            dqe_ref[rows, :] = dqe
            dkd_ref[rows, :] = dkd
            dat_ref[rows, :] = dat
            degl_ref[pl.ds(erow, 8), :] = jnp.broadcast_to(degl, (8, BW))
            dz_ref[rows, :] = dz.astype(BF16)
            return nxt, pn + dn

        dstates, pn = lax.fori_loop(0, ncb, step, (tuple(dstate[h] for h in range(4)), jnp.zeros((1, 128), F32)))
        for h in range(4):
            dstate[h] = dstates[h]

        @pl.when(first)
        def _():
            dng_ref[...] = pn

        @pl.when(jnp.logical_not(first))
        def _():
            dng_ref[...] += pn

    blk = pl.BlockSpec((tb, BW), lambda j: (nb - 1 - j, 0))
    eblk = pl.BlockSpec((ncb * 8, BW), lambda j: (nb - 1 - j, 0))
    vec = pl.BlockSpec((1, 128), lambda j: (0, 0))
    return pl.pallas_call(
        body, name=name, grid=(nb,),
        in_specs=[blk] * 5 + [eblk, pl.BlockSpec((tb, BW), lambda j: (nb - 1 - j, COL_CZ // BW)), blk,
                              pl.BlockSpec((4, ncb, 128, 128), lambda j: (0, nb - 1 - j, 0, 0)), vec],
        out_specs=[blk] * 5 + [eblk, blk, vec],
        out_shape=[jax.ShapeDtypeStruct((T, BW), F32)] * 5 + [jax.ShapeDtypeStruct((T // 8, BW), F32),
                                                              jax.ShapeDtypeStruct((T, BW), BF16),
                                                              jax.ShapeDtypeStruct((1, 128), F32)],
        scratch_shapes=[pltpu.VMEM((4, 128, 128), F32)],
        compiler_params=_cp("arbitrary"),
    )(u, w, qe, kd, at, egl, proj, dy, sh, ng)


def _adamw_update(w_ref, g_ref, m_ref, v_ref, d_ref, nm_ref, nv_ref):
    gv = g_ref[...]
    m2 = ADAM_B1 * m_ref[...] + (1.0 - ADAM_B1) * gv
    v2 = ADAM_B2 * v_ref[...] + (1.0 - ADAM_B2) * jnp.square(gv)
    m_hat = m2 / (1.0 - ADAM_B1 ** ADAM_STEP)
    v_hat = v2 / (1.0 - ADAM_B2 ** ADAM_STEP)
    d_ref[...] = -ADAM_LR * (m_hat / (jnp.sqrt(v_hat) + ADAM_EPS) + ADAM_WD * w_ref[...])
    nm_ref[...] = m2
    nv_ref[...] = v2


def _adamw_many(name, ws, gs, ms, vs):
    n = len(ws)

    def body(*refs):
        for i in range(n):
            _adamw_update(*[refs[k * n + i] for k in range(7)])

    return pl.pallas_call(
        body, name=name,
        out_shape=[jax.ShapeDtypeStruct(a.shape, F32) for a in ws] * 3,
        compiler_params=_cp(),
    )(*ws, *gs, *ms, *vs)


def _adamw(name, w, g, m, v):
    R, C = w.shape
    br = _pick(R, (512, 256, 240, 128, 64, 8))
    body = functools.partial(_adamw_update)
    spec = pl.BlockSpec((br, C), lambda i: (i, 0))
    return pl.pallas_call(
        body, name=name, grid=(R // br,),
        in_specs=[spec] * 4, out_specs=[spec] * 3,
        out_shape=[jax.ShapeDtypeStruct((R, C), F32)] * 3,
        compiler_params=_cp("parallel"),
    )(w, g, m, v)


def _sum8(name, parts):
    _, R, C = parts.shape
    br = _pick(R, (352, 368, 256, 128, 64, 16, 8))

    def body(p_ref, o_ref):
        acc = p_ref[0].astype(F32)
        for d in range(1, N_DEV):
            acc = acc + p_ref[d].astype(F32)
        o_ref[...] = acc

    return pl.pallas_call(
        body, name=name, grid=(R // br,),
        in_specs=[pl.BlockSpec((N_DEV, br, C), lambda i: (0, i, 0))],
        out_specs=pl.BlockSpec((br, C), lambda i: (i, 0)),
        out_shape=jax.ShapeDtypeStruct((R, C), F32),
        compiler_params=_cp("parallel"),
    )(parts)


_ANY = pl.BlockSpec(memory_space=pl.ANY)
_MESH = pl.DeviceIdType.MESH


def _all_gather(name, shard):
    R, C = shard.shape

    def body(x_ref, out_ref, send_sems, recv_sems, local_sem):
        x, y, c = lax.axis_index("x"), lax.axis_index("y"), lax.axis_index("c")
        me, sibling = (x, y, c), (x, y, 1 - c)
        chips = [(1 - x, y), (x, 1 - y), (1 - x, 1 - y)]

        def slot(px, py, pc):
            return out_ref.at[4 * px + 2 * py + pc]

        def copy(k, block, to, src=None):
            return pltpu.make_async_remote_copy(
                src_ref=slot(*block) if src is None else src, dst_ref=slot(*block),
                send_sem=send_sems.at[k], recv_sem=recv_sems.at[k], device_id=to, device_id_type=_MESH)

        mine = pltpu.make_async_copy(x_ref, slot(*me), local_sem)
        mine.start()
        first = [copy(0, me, sibling, src=x_ref)]
        first += [copy(1 + j, me, (*chip, c), src=x_ref) for j, chip in enumerate(chips)]
        for cp in first:
            cp.start()
        passed = [copy(4 + j, (*chip, c), sibling) for j, chip in enumerate(chips)]
        for j, chip in enumerate(chips):
            copy(1 + j, (*chip, c), me).wait_recv()
            passed[j].start()
        copy(0, sibling, me).wait_recv()
        for j, chip in enumerate(chips):
            copy(4 + j, (*chip, 1 - c), me).wait_recv()
        for cp in first + passed:
            cp.wait_send()
        mine.wait()

    return pl.pallas_call(
        body, name=name,
        in_specs=[_ANY], out_specs=_ANY,
        out_shape=jax.ShapeDtypeStruct((N_DEV, R, C), shard.dtype),
        scratch_shapes=[pltpu.SemaphoreType.DMA((7,)), pltpu.SemaphoreType.DMA((7,)), pltpu.SemaphoreType.DMA],
    )(shard)


_HBM = pl.BlockSpec(memory_space=pltpu.HBM)
_SEM = pl.BlockSpec(memory_space=pltpu.SEMAPHORE)
_EFFECT = pltpu.SideEffectType.DATAFLOW_SIDE_EFFECTING


def _exchange_copies(src_ref, land_ref, send_sems, recv_sems, scatter):
    x, y, c = lax.axis_index("x"), lax.axis_index("y"), lax.axis_index("c")
    me = 4 * x + 2 * y + c
    copies = []
    for k in range(1, N_DEV):
        px, py, pc = x ^ ((k >> 2) & 1), y ^ ((k >> 1) & 1), c ^ (k & 1)
        src = src_ref.at[4 * px + 2 * py + pc] if scatter else src_ref
        copies.append(pltpu.make_async_remote_copy(
            src_ref=src, dst_ref=land_ref.at[me], send_sem=send_sems.at[k - 1], recv_sem=recv_sems.at[k - 1],
            device_id=(px, py, pc), device_id_type=_MESH))
    return copies


def _own_copy(src_ref, land_ref, send_sems, scatter):
    me = 4 * lax.axis_index("x") + 2 * lax.axis_index("y") + lax.axis_index("c")
    return pltpu.make_async_copy(src_ref.at[me] if scatter else src_ref, land_ref.at[me], send_sems.at[N_DEV - 1])


def _exchange_start(name, srcs, lands, scatter, after=None):
    n = len(srcs)

    def body(*refs):
        src_refs, land_refs = refs[:n], refs[n:2 * n]
        outs = refs[2 * n + (after is not None):]
        send, recv = outs[:n], outs[n:2 * n]
        token = refs[-1]
        for g in range(n):
            for cp in _exchange_copies(src_refs[g], land_refs[g], send[g], recv[g], scatter):
                cp.start()
            _own_copy(src_refs[g], land_refs[g], send[g], scatter).start()
        token[...] = jnp.zeros_like(token)

    outs = pl.pallas_call(
        body, name=name,
        out_shape=tuple([pltpu.SemaphoreType.DMA((N_DEV,))] * (2 * n)
                        + [pltpu.HBM(a.shape, a.dtype) for a in list(srcs) + list(lands)]
                        + [jax.ShapeDtypeStruct((8, 128), F32)]),
        in_specs=[_HBM] * (2 * n) + [_ANY] * (after is not None),
        out_specs=tuple([_SEM] * (2 * n) + [_HBM] * (2 * n) + [pl.BlockSpec(memory_space=pltpu.VMEM)]),
        input_output_aliases={i: 2 * n + i for i in range(2 * n)},
        compiler_params=pltpu.CompilerParams(has_side_effects=_EFFECT),
    )(*[pltpu.with_memory_space_constraint(a, pltpu.HBM) for a in list(srcs) + list(lands)],
      *([after] if after is not None else []))
    handles = [(outs[2 * n + g], outs[3 * n + g], outs[g], outs[n + g]) for g in range(n)]
    return handles, outs[-1]


def _exchange_wait(name, handles, after, scatter):
    n = len(handles)
    srcs, lands, sends, recvs = ([h[i] for h in handles] for i in range(4))

    def body(*refs):
        src_refs, land_refs = refs[:n], refs[n:2 * n]
        send, recv = refs[2 * n:3 * n], refs[3 * n:4 * n]
        for g in range(n):
            for cp in _exchange_copies(src_refs[g], land_refs[g], send[g], recv[g], scatter):
                cp.wait_send()
                cp.wait_recv()
            _own_copy(src_refs[g], land_refs[g], send[g], scatter).wait()

    outs = pl.pallas_call(
        body, name=name,
        out_shape=tuple(pltpu.HBM(a.shape, a.dtype) for a in srcs + lands),
        in_specs=tuple([_HBM] * (2 * n) + [_SEM] * (2 * n) + [_ANY]), out_specs=tuple([_HBM] * (2 * n)),
        input_output_aliases={i: i for i in range(2 * n)},
        compiler_params=pltpu.CompilerParams(has_side_effects=_EFFECT),
    )(*srcs, *lands, *sends, *recvs, after)
    return list(outs[n:])


def _rows(a):
    return a.reshape(-1, 1024)


def _rows_to_parts(full):
    n = full.shape[-2] // N_DEV
    t = full.reshape(full.shape[:-2] + (N_DEV, n, full.shape[-1]))
    return jnp.moveaxis(t, -3, 0)


def _parts_to_rows(parts):
    t = jnp.moveaxis(parts, 0, -3)
    return t.reshape(t.shape[:-3] + (t.shape[-3] * t.shape[-2], t.shape[-1]))


def _parts_to_cols(parts):
    t = jnp.moveaxis(parts, 0, -2)
    return t.reshape(t.shape[:-2] + (t.shape[-2] * t.shape[-1],))


def _join(parts, axis=0):
    total = sum(p.shape[axis] for p in parts)
    out, off = None, 0
    for p in parts:
        cfg = [(0, 0)] * p.ndim
        cfg[axis] = (off, total - off - p.shape[axis])
        t = jnp.pad(p, cfg)
        out = t if out is None else out + t
        off += p.shape[axis]
    return out


W_IN_SHARD = P_IN // N_DEV
W_IN_SHARD_PAD = -(-W_IN_SHARD // 16) * 16
W_IN_STACK = N_DEV * W_IN_SHARD_PAD


def _w_in_runs():
    spans = [(0, 4096, 0), (4096, 4104, COL_TAIL), (4104, P_IN, 4096)]
    runs = []
    for d in range(N_DEV):
        lo, hi = d * W_IN_SHARD, (d + 1) * W_IN_SHARD
        for a, b, lay in spans:
            s, e = max(a, lo), min(b, hi)
            if s < e:
                runs.append((lay + s - a, d * W_IN_SHARD_PAD + s - lo, e - s))
    return runs


def _w_in_to_layout(stack):
    out = None
    for lay, src, n in _w_in_runs():
        t = jnp.pad(stack[src:src + n], ((lay, PW - lay - n), (0, 0)))
        out = t if out is None else out + t
    return out


def _w_in_from_layout(g):
    out = None
    for lay, src, n in _w_in_runs():
        t = jnp.pad(g[lay:lay + n], ((src, W_IN_STACK - src - n), (0, 0)))
        out = t if out is None else out + t
    return out


def _block_diag(w):
    w = w.reshape(4, 2, 64, 64)
    return jnp.pad(w[:, 0], ((0, 0), (0, 64), (0, 64))) + jnp.pad(w[:, 1], ((0, 0), (64, 0), (64, 0)))


def _block_diag_grad(g):
    return jnp.stack([g[:, :64, :64], g[:, 64:, 64:]], axis=1).reshape(8, 64, 64)


def _ffn_forward(tag, x, norm, wg, wu, wd):
    h = _rms_fwd(tag + "_norm", x, norm)
    sa, ds, act = _ffn_up(tag + "_up", h, wg, wu)
    if callable(wd):
        wd = wd(act)
    x_out = _mm(tag + "_down", [(act, wd)], "nn", F32, res=x, scale=0.5)
    return x_out, (x, h, sa, ds, act), wd


def _ffn_backward(tag, dx_out, saved, norm, wg, wu, wd, put, names, split=False):
    x, h, sa, ds, act = saved
    n_wg, n_wu, n_wd = names
    dwd = _mm(tag + "_dwd", [(act, dx_out)], "tn", BF16, scale=0.5, bm=FF // 2)
    tok = put({n_wd: dwd}) if split else None
    da, db = _ffn_dact(tag + "_dact", dx_out, wd, sa, ds, after=tok)
    dwg = _mm(tag + "_dwg", [(da, h)], "tn", BF16, bm=FF // 2)
    if split:
        tok = tok + put({n_wg: dwg})
    dwu = _mm(tag + "_dwu", [(db, h)], "tn", BF16, bm=FF // 2, after=tok)
    tok = tok + put({n_wu: dwu}) if split else put({n_wg: dwg, n_wu: dwu, n_wd: dwd})
    dh = _mm(tag + "_dh", [(da, wg), (db, wu)], "nn", F32, after=tok)
    dx, dnorm = _rms_bwd(tag + "_dnorm", x, norm + tok, dh, dx_out)
    return dx, dnorm


def _mixer_params(p):
    alog = jnp.pad(p["gdn_a_log"], (4, 120))[None]
    dtb = jnp.pad(p["gdn_dt_bias"], (4, 120))[None]
    bias = jnp.repeat(p["sgu_b"].T, 128, axis=1)
    return dict(
        ln_g=p["sgu_ln_g"][None], ln_b=p["sgu_ln_b"][None], sgu_w=p["sgu_w"], sgu_bias=bias,
        lru_cw=p["lru_conv_w"], lru_cb=p["lru_conv_b"][None], wa=_block_diag(p["lru_wa"]), ba=p["lru_ba"][None],
        wx=_block_diag(p["lru_wx"]), bx=p["lru_bx"][None], lam=p["lru_lambda"][None],
        gdn_cw=p["gdn_conv_w"], alog=alog, dtb=dtb, ng=p["gdn_norm_g"][None],
        pool_w=p["pool_w"], pool_sc=p["pool_scale"][None])


def _mix_forward(tag, x, p, mp):
    h = _rms_fwd(tag + "_norm", x, p["mix_norm"][None])
    proj = _mm(tag + "_proj", [(h, p["w_in"])], "nt", BF16, bm=_pick(x.shape[0], (2048, 1024, 512, 256, 128)))
    y_a = _sgu_fwd(tag + "_sgu", proj, mp["ln_g"], mp["ln_b"], mp["sgu_w"], mp["sgu_bias"])
    y_b, hc = _lru_fwd(tag + "_lru", proj, mp["lru_cw"], mp["lru_cb"], mp["wa"], mp["ba"], mp["wx"], mp["bx"],
                       mp["lam"])
    qa = _conv_fwd(tag + "_convq", proj, COL_CQ, mp["gdn_cw"], 0)
    ka = _conv_fwd(tag + "_convk", proj, COL_CK, mp["gdn_cw"], 512)
    va = _conv_fwd(tag + "_convv", proj, COL_CV, mp["gdn_cw"], 1024)
    prep = _gdn_prep_fwd(tag + "_gdnprep", qa, ka, va, proj, mp["alog"], mp["dtb"])
    y_c, sh = _gdn_fwd(tag + "_gdn", *prep, proj, mp["ng"])
    y_d = _pool_fwd(tag + "_pool", proj, mp["pool_w"], mp["pool_sc"])
    ys = (y_a, y_b, y_c, y_d)
    if callable(p["w_branch"]):
        p["w_branch"] = p["w_branch"](y_d)
    merged = _merge_fwd(tag + "_merge", ys, p["w_branch"], proj)
    if callable(p["w_out"]):
        p["w_out"] = p["w_out"](merged)
    x_out = _mm(tag + "_out", [(merged, p["w_out"])], "nn", F32, res=x)
    return x_out, (x, h, proj, hc, qa, ka, va, prep, sh, ys, merged)


def _mix_backward(tag, dx_out, saved, p, mp, put):
    x, h, proj, hc, qa, ka, va, prep, sh, ys, merged = saved
    T = x.shape[0]
    g = {}
    dmerged = _mm(tag + "_dmerged", [(dx_out, p["w_out"])], "nt", F32)
    g["w_out"] = _mm(tag + "_dwout", [(merged, dx_out)], "tn", BF16)
    outs = _merge_bwd(tag + "_dmerge", dmerged, ys, p["w_branch"], proj)
    dgates, dbrs, dys = outs[:NBR], outs[NBR:2 * NBR], outs[2 * NBR:]
    g["w_branch"] = jnp.stack([_mm(f"{tag}_dwb{i}", [(dbrs[i], ys[i])], "tn", BF16) for i in range(NBR)])

    du, dv, dln_g, dln_b, dsgu_w, dbias = _sgu_bwd(tag + "_dsgu", proj, dys[0], mp["ln_g"], mp["ln_b"], mp["sgu_w"],
                                                  mp["sgu_bias"])
    g["sgu_ln_g"], g["sgu_ln_b"], g["sgu_w"] = dln_g[0], dln_b[0], dsgu_w
    g["sgu_b"] = dbias.reshape(128, 4, 128).sum(axis=2).T

    (dbx, dbg, dcw, dcb, dwa, dba, dwx, dbxb, dlam) = _lru_bwd(
        tag + "_dlru", proj, dys[1], hc, mp["lru_cw"], mp["lru_cb"], mp["wa"], mp["ba"], mp["wx"], mp["bx"], mp["lam"])
    g["lru_conv_w"], g["lru_conv_b"], g["lru_ba"], g["lru_bx"], g["lru_lambda"] = dcw, dcb[0], dba[0], dbxb[0], dlam[0]
    g["lru_wa"], g["lru_wx"] = _block_diag_grad(dwa), _block_diag_grad(dwx)

    *dprep, dz, dng = _gdn_bwd(tag + "_dgdn", *prep, proj, dys[2], sh, mp["ng"])
    dqa, dka, dva, dtail, dalog, ddtb = _gdn_prep_bwd(tag + "_dgdnprep", qa, ka, va, proj, mp["alog"], mp["dtb"], *dprep)
    g["gdn_a_log"], g["gdn_dt_bias"], g["gdn_norm_g"] = dalog[0, 4:8], ddtb[0, 4:8], dng[0]
    dq, dcwq = _conv_bwd(tag + "_dconvq", proj, COL_CQ, dqa, mp["gdn_cw"], 0)
    dk, dcwk = _conv_bwd(tag + "_dconvk", proj, COL_CK, dka, mp["gdn_cw"], 512)
    dv_, dcwv = _conv_bwd(tag + "_dconvv", proj, COL_CV, dva, mp["gdn_cw"], 1024)
    g["gdn_conv_w"] = jnp.concatenate([dcwq, dcwk, dcwv], axis=1)

    dd, dpw, dsc = _pool_bwd(tag + "_dpool", proj, dys[3], mp["pool_w"], mp["pool_sc"])
    g["pool_w"], g["pool_scale"] = dpw, dsc[0]

    dproj = jnp.concatenate([du, dv, dbx, dbg, dq, dk, dv_, dz, dd, *dgates, dtail,
                             jnp.zeros((T, PW - COL_TAIL - 128), BF16)], axis=1)
    dw_in = _mm(tag + "_dwin", [(dproj, h)], "tn", BF16)
    tok = put(dict(w_in=_w_in_from_layout(dw_in), w_branch=g.pop("w_branch"), w_out=g.pop("w_out")))
    dh = _mm(tag + "_dh", [(dproj, p["w_in"])], "nn", F32, bm=_pick(T, (2048, 1024, 512, 256, 128)), after=tok)
    dx, dnorm = _rms_bwd(tag + "_dnorm", x, p["mix_norm"][None] + tok, dh, dx_out)
    g["mix_norm"] = dnorm[0]
    return dx, g


_BIG = ("ff1_wg", "ff1_wu", "ff1_wd", "w_in", "w_branch", "w_out", "ff2_wg", "ff2_wu", "ff2_wd")
_COL_SHARDED = ("ff1_wg", "ff1_wu", "w_in", "w_branch", "ff2_wg", "ff2_wu")
_SMALL = ("ff1_norm", "mix_norm", "sgu_ln_g", "sgu_ln_b", "sgu_w", "sgu_b", "lru_conv_w", "lru_conv_b", "lru_wa",
          "lru_ba", "lru_wx", "lru_bx", "lru_lambda", "gdn_conv_w", "gdn_a_log", "gdn_dt_bias", "gdn_norm_g", "pool_w",
          "pool_scale", "ff2_norm", "final_norm")
_WEIGHTS = ("ff1_norm", "ff1_wg", "ff1_wu", "ff1_wd", "mix_norm", "w_in", "sgu_ln_g", "sgu_ln_b", "sgu_w", "sgu_b",
            "lru_conv_w", "lru_conv_b", "lru_wa", "lru_ba", "lru_wx", "lru_bx", "lru_lambda", "gdn_conv_w", "gdn_a_log",
            "gdn_dt_bias", "gdn_norm_g", "pool_w", "pool_scale", "w_branch", "w_out", "ff2_norm", "ff2_wg", "ff2_wu",
            "ff2_wd", "final_norm")
_CONV_SHARDED = ("lru_conv_w", "gdn_conv_w")
PACK_ROW_ALIGN = 16
_GROUPS = (("ff1", ("ff1_wg", "ff1_wu", "ff1_wd")), ("mix", ("w_in", "w_branch", "w_out")),
           ("ff2", ("ff2_wg", "ff2_wu", "ff2_wd")))


def _pad_rows(a, mult):
    pad = (-a.shape[-2]) % mult
    if pad == 0:
        return a
    return jnp.pad(a, [(0, 0)] * (a.ndim - 2) + [(0, pad), (0, 0)])


def _my_index():
    return 4 * lax.axis_index("x") + 2 * lax.axis_index("y") + lax.axis_index("c")


def _landing(shape, dtype):
    return lax.empty((N_DEV,) + tuple(shape), dtype)


def _stored(n, a):
    return jnp.swapaxes(a, -1, -2) if n in _COL_SHARDED else a


_FIRST = ("ff1_wg", "ff1_wu", "ff1_wd")


def _gather_first(w):
    names = _FIRST
    shards = [_rows(_stored(n, w[n][0]).astype(BF16)) for n in names]
    got = _all_gather("gather_first", jnp.concatenate(shards, axis=0))
    out, r = {}, 0
    for n, s in zip(names, shards):
        out[n] = got[:, r:r + s.shape[0]].reshape(-1, 1024)
        r += s.shape[0]
    return out, got


def _gather_start(w, after):
    conv = _pad_rows(jnp.concatenate([w[n].reshape(1, -1) for n in _CONV_SHARDED], axis=1), 8)
    keys, srcs = ["conv"], [conv]
    for l in range(2):
        for sub, (_, names) in enumerate(_GROUPS):
            for n in names:
                if l > 0 or n not in _FIRST:
                    keys.append((l, sub, n))
                    srcs.append(_pad_rows(_stored(n, w[n][l]).astype(BF16), PACK_ROW_ALIGN))
    lands = [_landing(s.shape, s.dtype) for s in srcs]
    handles, token = _exchange_start("gather_start", srcs, lands, scatter=False, after=after)
    return dict(zip(keys, handles)), token


def _gather_finish(l, sub, handles, first, after):
    names = _GROUPS[sub][1]
    if (l, sub) == (0, 0):
        out = dict(first)
        for n in names:
            if n not in _FIRST:
                out[n] = lambda later, n=n: _parts_to_rows(
                    _exchange_wait(f"gather_wait_00_{n}", [handles[(0, 0, n)]], later, scatter=False)[0])
    elif sub == 1:
        out = {n: (lambda later, n=n: _parts_to_rows(
            _exchange_wait(f"gather_wait_{l}{sub}_{n}", [handles[(l, sub, n)]], later, scatter=False)[0])) for n in names}
        out["w_in"] = out["w_in"](after)
    else:
        lands = _exchange_wait(f"gather_wait_{l}{sub}", [handles[(l, sub, n)] for n in names], after, scatter=False)
        out = {n: _parts_to_rows(land) for n, land in zip(names, lands)}
    if "w_in" in out:
        out["w_in"] = _w_in_to_layout(out["w_in"])
    return out


def _scatter_start(l, sub, grads):
    srcs, shapes = [], []
    for n in grads:
        parts = _rows_to_parts(grads[n])
        shapes.append((W_IN_SHARD, 1024) if n == "w_in" else parts.shape[1:])
        srcs.append(parts.reshape(N_DEV, -1, 1024))
    lands = [_landing(s.shape[1:], s.dtype) for s in srcs]
    tag = f"{l}{sub}" + ("" if len(grads) == len(_GROUPS[sub][1]) else "_" + "_".join(grads))
    handles, token = _exchange_start(f"scatter_start_{tag}", srcs, lands, scatter=True)
    return handles, (tag, tuple(grads), shapes), token


def _scatter_finish(l, sub, handles, meta, after):
    tag, names, shapes = meta
    lands = _exchange_wait(f"scatter_wait_{tag}", handles, after, scatter=True)
    out = {}
    for n, land, shape in zip(names, lands, shapes):
        size = 1
        for s in shape:
            size *= s
        summed = _sum8(f"sum_{l}{sub}_{n}", land)
        out[n] = _stored(n, summed[:size // 1024].reshape(shape))
    return out


def _gather_conv_finish(w, handles, after):
    gconv = _exchange_wait("gather_wait_conv", [handles["conv"]], after, scatter=False)[0][:, 0]
    full, r = {}, 0
    for n in _CONV_SHARDED:
        sz = w[n].size
        full[n] = _parts_to_cols(gconv[:, r:r + sz].reshape((N_DEV,) + w[n].shape))
        r += sz
    return full


def _forward_backward(x, tgt, w, conv, get_weights, put_grads, put_small, token):
    saved, params = [], []
    for l in range(2):
        p = {n: w[n][l] for n in _SMALL if n != "final_norm"}
        for n in _CONV_SHARDED:
            p[n] = conv[n][l]
        mp = _mixer_params(p)
        tok = token[:1, :1] if l == 0 else 0.0
        p.update(get_weights(l, 0, x))
        x, s1, p["ff1_wd"] = _ffn_forward(f"l{l}_ff1", x, p["ff1_norm"][None] + tok, p["ff1_wg"], p["ff1_wu"],
                                          p["ff1_wd"])
        p.update(get_weights(l, 1, x))
        x, s2 = _mix_forward(f"l{l}_mix", x, p, mp)
        p.update(get_weights(l, 2, x))
        x, s3, _ = _ffn_forward(f"l{l}_ff2", x, p["ff2_norm"][None], p["ff2_wg"], p["ff2_wu"], p["ff2_wd"])
        saved.append((s1, s2, s3))
        params.append((p, mp))
    loss, dx, dfinal = _final_loss("loss_head", x, w["final_norm"][None], tgt)
    tok = 0.0
    for l in (1, 0):
        p, mp = params[l]
        s1, s2, s3 = saved[l]
        g = {}

        def put(sub):
            return lambda grads, l=l: put_grads(l, sub, grads)[:1, :1]

        dx, dn = _ffn_backward(f"l{l}_ff2", dx, s3, p["ff2_norm"][None] + tok, p["ff2_wg"], p["ff2_wu"], p["ff2_wd"],
                               put(2), _GROUPS[2][1])
        g["ff2_norm"] = dn[0]
        dx, gm = _mix_backward(f"l{l}_mix", dx, s2, p, mp, put(1))
        g.update(gm)
        tok = 0.0
        if l == 0:
            keep = {n: g.pop(n) for n in ("gdn_a_log", "gdn_dt_bias")}
            tok = put_small("0a", g, True)[:1, :1]
            g = keep
        dx, dn = _ffn_backward(f"l{l}_ff1", dx, s1, p["ff1_norm"][None] + tok, p["ff1_wg"], p["ff1_wu"], p["ff1_wd"],
                               put(0), _GROUPS[0][1], split=(l == 0))
        g["ff1_norm"] = dn[0]
        if l == 1:
            g["final_norm"] = dfinal[0]
            g["loss"] = loss[0, :1]
        tok = put_small("1" if l == 1 else "0b", g, False)[:1, :1]
    return dx


SMALL_PIECE = 8 * 1024


def _pack_small(d, names):
    pieces = []
    for n in names:
        flat = d[n].reshape(-1)
        pieces.append(jnp.pad(flat, (0, (-flat.size) % SMALL_PIECE)).reshape(-1, 1024))
    return jnp.concatenate(pieces, axis=0)


def _unpack_small(pack, shapes, names):
    out, r = {}, 0
    for n in names:
        size = 1
        for s in shapes[n]:
            size *= s
        rows = -(-size // SMALL_PIECE) * 8
        out[n] = pack[r:r + rows].reshape(-1)[:size].reshape(shapes[n])
        r += rows
    return out


def _small_names(grads):
    return tuple(n for n in _SMALL + ("loss",) if n in grads)


def _small_start(tag, grads, narrow):
    pack = _pack_small(grads, _small_names(grads))
    if narrow:
        pack = _pad_rows(pack.astype(BF16), PACK_ROW_ALIGN)
    handles, token = _exchange_start(f"small_start_{tag}", [pack], [_landing(pack.shape, pack.dtype)], scatter=False)
    return handles, {n: grads[n].shape for n in _small_names(grads)}, token


def _small_finish(tag, handles, shapes, after):
    landed = _exchange_wait(f"small_wait_{tag}", handles, after, scatter=False)[0]
    return _unpack_small(_sum8(f"sum_small_{tag}", landed), shapes, _small_names(shapes))


def _as2d(a):
    if a.ndim == 1:
        return a.reshape(1, -1)
    return a.reshape(-1, a.shape[-1])


def kernel(x, ff1_norm, ff1_wg, ff1_wu, ff1_wd, mix_norm, w_in, sgu_ln_g, sgu_ln_b, sgu_w, sgu_b, lru_conv_w, lru_conv_b, lru_wa, lru_ba, lru_wx, lru_bx, lru_lambda, gdn_conv_w, gdn_a_log, gdn_dt_bias, gdn_norm_g, pool_w, pool_scale, w_branch, w_out, ff2_norm, ff2_wg, ff2_wu, ff2_wd, final_norm, loss_target, m_ff1_norm, m_ff1_wg, m_ff1_wu, m_ff1_wd, m_mix_norm, m_w_in, m_sgu_ln_g, m_sgu_ln_b, m_sgu_w, m_sgu_b, m_lru_conv_w, m_lru_conv_b, m_lru_wa, m_lru_ba, m_lru_wx, m_lru_bx, m_lru_lambda, m_gdn_conv_w, m_gdn_a_log, m_gdn_dt_bias, m_gdn_norm_g, m_pool_w, m_pool_scale, m_w_branch, m_w_out, m_ff2_norm, m_ff2_wg, m_ff2_wu, m_ff2_wd, m_final_norm, v_ff1_norm, v_ff1_wg, v_ff1_wu, v_ff1_wd, v_mix_norm, v_w_in, v_sgu_ln_g, v_sgu_ln_b, v_sgu_w, v_sgu_b, v_lru_conv_w, v_lru_conv_b, v_lru_wa, v_lru_ba, v_lru_wx, v_lru_bx, v_lru_lambda, v_gdn_conv_w, v_gdn_a_log, v_gdn_dt_bias, v_gdn_norm_g, v_pool_w, v_pool_scale, v_w_branch, v_w_out, v_ff2_norm, v_ff2_wg, v_ff2_wu, v_ff2_wd, v_final_norm):
    w = dict(ff1_norm=ff1_norm, ff1_wg=ff1_wg, ff1_wu=ff1_wu, ff1_wd=ff1_wd, mix_norm=mix_norm, w_in=w_in,
             sgu_ln_g=sgu_ln_g, sgu_ln_b=sgu_ln_b, sgu_w=sgu_w, sgu_b=sgu_b, lru_conv_w=lru_conv_w,
             lru_conv_b=lru_conv_b, lru_wa=lru_wa, lru_ba=lru_ba, lru_wx=lru_wx, lru_bx=lru_bx, lru_lambda=lru_lambda,
             gdn_conv_w=gdn_conv_w, gdn_a_log=gdn_a_log, gdn_dt_bias=gdn_dt_bias, gdn_norm_g=gdn_norm_g, pool_w=pool_w,
             pool_scale=pool_scale, w_branch=w_branch, w_out=w_out, ff2_norm=ff2_norm, ff2_wg=ff2_wg, ff2_wu=ff2_wu,
             ff2_wd=ff2_wd, final_norm=final_norm)
    m = dict(ff1_norm=m_ff1_norm, ff1_wg=m_ff1_wg, ff1_wu=m_ff1_wu, ff1_wd=m_ff1_wd, mix_norm=m_mix_norm, w_in=m_w_in,
             sgu_ln_g=m_sgu_ln_g, sgu_ln_b=m_sgu_ln_b, sgu_w=m_sgu_w, sgu_b=m_sgu_b, lru_conv_w=m_lru_conv_w,
             lru_conv_b=m_lru_conv_b, lru_wa=m_lru_wa, lru_ba=m_lru_ba, lru_wx=m_lru_wx, lru_bx=m_lru_bx,
             lru_lambda=m_lru_lambda, gdn_conv_w=m_gdn_conv_w, gdn_a_log=m_gdn_a_log, gdn_dt_bias=m_gdn_dt_bias,
             gdn_norm_g=m_gdn_norm_g, pool_w=m_pool_w, pool_scale=m_pool_scale, w_branch=m_w_branch, w_out=m_w_out,
             ff2_norm=m_ff2_norm, ff2_wg=m_ff2_wg, ff2_wu=m_ff2_wu, ff2_wd=m_ff2_wd, final_norm=m_final_norm)
    v = dict(ff1_norm=v_ff1_norm, ff1_wg=v_ff1_wg, ff1_wu=v_ff1_wu, ff1_wd=v_ff1_wd, mix_norm=v_mix_norm, w_in=v_w_in,
             sgu_ln_g=v_sgu_ln_g, sgu_ln_b=v_sgu_ln_b, sgu_w=v_sgu_w, sgu_b=v_sgu_b, lru_conv_w=v_lru_conv_w,
             lru_conv_b=v_lru_conv_b, lru_wa=v_lru_wa, lru_ba=v_lru_ba, lru_wx=v_lru_wx, lru_bx=v_lru_bx,
             lru_lambda=v_lru_lambda, gdn_conv_w=v_gdn_conv_w, gdn_a_log=v_gdn_a_log, gdn_dt_bias=v_gdn_dt_bias,
             gdn_norm_g=v_gdn_norm_g, pool_w=v_pool_w, pool_scale=v_pool_scale, w_branch=v_w_branch, w_out=v_w_out,
             ff2_norm=v_ff2_norm, ff2_wg=v_ff2_wg, ff2_wu=v_ff2_wu, ff2_wd=v_ff2_wd, final_norm=v_final_norm)

    first, got_first = _gather_first(w)
    handles, token = _gather_start(w, got_first)
    conv = _gather_conv_finish(w, handles, token)
    pending = {}

    def get_weights(l, sub, after):
        return _gather_finish(l, sub, handles, first, after)

    def put_grads(l, sub, grads):
        hs, meta, tok = _scatter_start(l, sub, grads)
        pending[(l, sub, meta[0])] = (hs, meta)
        return tok

    def put_small(tag, grads, narrow):
        hs, shapes, tok = _small_start(tag, grads, narrow)
        pending[tag] = (hs, shapes)
        return tok

    T = x.shape[1]
    dx = _forward_backward(x.reshape(T, D), loss_target.reshape(T, D), w, conv, get_weights, put_grads, put_small,
                           token)
    per = {}
    for key in pending:
        if isinstance(key, tuple):
            per.setdefault(key[:2], {}).update(_scatter_finish(*key[:2], *pending[key], dx))
        else:
            per[key] = _small_finish(key, *pending[key], dx)
    grad = {n: jnp.stack([per[(0, sub)][n], per[(1, sub)][n]]) for sub, (_, names) in enumerate(_GROUPS) for n in names}
    layer0 = {**per["0a"], **per["0b"]}
    small = {n: _join([layer0[n].reshape(-1), per["1"][n].reshape(-1)]).reshape((2,) + layer0[n].shape)
             for n in layer0}
    small["final_norm"] = per["1"]["final_norm"]
    loss = per["1"]["loss"][0]
    me = _my_index()
    for n in _SMALL:
        if n in _CONV_SHARDED:
            width = w[n].shape[-1]
            grad[n] = lax.dynamic_slice_in_dim(small[n], me * width, width, axis=2)
        else:
            grad[n] = small[n]

    delta, new_m, new_v = {}, {}, {}
    for n in _BIG:
        d_, m_, v_ = _adamw("adamw_" + n, _as2d(w[n]), _as2d(grad[n]), _as2d(m[n]), _as2d(v[n]))
        delta[n], new_m[n], new_v[n] = (t.reshape(w[n].shape) for t in (d_, m_, v_))

    outs = _adamw_many("adamw_small", *[[_as2d(t[n]) for n in _SMALL] for t in (w, grad, m, v)])
    for k, dst in enumerate((delta, new_m, new_v)):
        for i, n in enumerate(_SMALL):
            dst[n] = outs[k * len(_SMALL) + i].reshape(w[n].shape)

    return (loss, dx.reshape(x.shape), *[grad[n] for n in _WEIGHTS], *[delta[n] for n in _WEIGHTS],
            *[new_m[n] for n in _WEIGHTS], *[new_v[n] for n in _WEIGHTS])
```

```python
import functools

import jax
import jax.numpy as jnp
from jax import lax
from jax.experimental import pallas as pl
from jax.experimental.pallas import tpu as pltpu

F32 = jnp.float32
BF16 = jnp.bfloat16
HI = lax.Precision.HIGHEST

N_DEV = 8
D = 1024
FF = 2816
BW = 512
NBR = 4
CHUNK = 64
EPS = 1e-6
LRU_C = 8.0
GDN_DK = 128

COL_AU, COL_AV, COL_BX, COL_BG = 0, 512, 1024, 1536
COL_CQ, COL_CK, COL_CV, COL_CZ = 2048, 2560, 3072, 3584
COL_DX, COL_GATE, COL_TAIL = 4096, 4608, 8704
PW = 9216
P_IN = 8712

ADAM_LR, ADAM_B1, ADAM_B2, ADAM_EPS, ADAM_WD, ADAM_STEP = 0.001, 0.9, 0.999, 1e-08, 0.01, 10

VMEM_LIMIT_V7X = 56 * 1024 * 1024

_NN = (((1,), (0,)), ((), ()))
_NT = (((1,), (1,)), ((), ()))
_TN = (((0,), (0,)), ((), ()))


def _cp(*sem):
    return pltpu.CompilerParams(dimension_semantics=tuple(sem), vmem_limit_bytes=VMEM_LIMIT_V7X)


def _dot(a, b, dims=_NN):
    return lax.dot_general(a.astype(BF16), b.astype(BF16), dims, preferred_element_type=F32)


def _dot_hi(a, b, dims=_NN):
    return lax.dot_general(a, b, dims, precision=HI, preferred_element_type=F32)


def _pick(n, cands):
    for c in cands:
        if n % c == 0:
            return c
    return n


@jax.custom_jvp
def _log1p(x):
    u = 1.0 + x
    return jnp.where(u == 1.0, x, x * jnp.log(u) / jnp.where(u == 1.0, 1.0, u - 1.0))


@_log1p.defjvp
def _log1p_jvp(p, t):
    (x,), (dx,) = p, t
    return _log1p(x), dx / (1.0 + x)


@jax.custom_jvp
def _expm1(x):
    u = jnp.exp(x)
    lu = jnp.log(u)
    small = (u == 1.0) | (lu == 0.0)
    return jnp.where(small, x, (u - 1.0) * x / jnp.where(small, 1.0, lu))


@_expm1.defjvp
def _expm1_jvp(p, t):
    (x,), (dx,) = p, t
    return _expm1(x), dx * jnp.exp(x)


def _softplus(x):
    return jnp.maximum(x, 0.0) + _log1p(jnp.exp(-jnp.abs(x)))


def _sigmoid(x):
    return jax.nn.sigmoid(x)


def _silu(x):
    return x * jax.nn.sigmoid(x)


def _gelu(x):
    return jax.nn.gelu(x)


@functools.partial(jax.custom_vjp, nondiff_argnums=(1,))
def _shift(x, s):
    return x if s == 0 else pltpu.roll(x, s, 0)


def _shift_fwd(x, s):
    return _shift(x, s), None


def _shift_bwd(s, _, g):
    n = g.shape[0]
    return (g if s == 0 else pltpu.roll(g, n - s, 0),)


_shift.defvjp(_shift_fwd, _shift_bwd)


def _scan_steps(a, b, reverse):
    n = a.shape[0]
    row = lax.broadcasted_iota(jnp.int32, a.shape, 0)
    k = 1
    while k < n:
        sh = n - k if reverse else k
        m = (row < n - k) if reverse else (row >= k)
        a_s = jnp.where(m, pltpu.roll(a, sh, 0), 1.0)
        b_s = jnp.where(m, pltpu.roll(b, sh, 0), 0.0)
        b = a * b_s + b
        a = a * a_s
        k *= 2
    return b


@jax.custom_vjp
def _scan(a, b):
    return _scan_steps(a, b, False)


def _scan_fwd(a, b):
    h = _scan_steps(a, b, False)
    return h, (a, h)


def _scan_bwd(res, dh):
    a, h = res
    n = a.shape[0]
    row = lax.broadcasted_iota(jnp.int32, a.shape, 0)
    a_next = jnp.where(row < n - 1, pltpu.roll(a, n - 1, 0), 0.0)
    g = _scan_steps(a_next, dh, True)
    h_prev = jnp.where(row >= 1, pltpu.roll(h, 1, 0), 0.0)
    return g * h_prev, g


_scan.defvjp(_scan_fwd, _scan_bwd)


def _mm(name, pairs, mode, out_dtype, *, res=None, scale=1.0, bm=None, bn=None, bk=None, after=None):
    a0, b0 = pairs[0]
    if mode == "nn":
        (M, K), N = a0.shape, b0.shape[1]
    elif mode == "nt":
        (M, K), N = a0.shape, b0.shape[0]
    else:
        (K, M), N = a0.shape, b0.shape[1]
    bm = bm or _pick(M, (1024, 512, 256, 128))
    bn = bn or _pick(N, (1024, 512, 256, 128))
    bk = bk or _pick(K, (1024, 512, 1408, 256, 128))
    nk = K // bk
    npair = len(pairs)
    dims = {"nn": _NN, "nt": _NT, "tn": _TN}[mode]

    def body(*refs):
        ab = refs[:2 * npair]
        pos = 2 * npair
        r_ref = None
        if res is not None:
            r_ref = refs[pos]
            pos += 1
        pos += after is not None
        o_ref = refs[pos]
        part = None
        for p in range(npair):
            d = _dot(ab[2 * p][...], ab[2 * p + 1][...], dims)
            part = d if part is None else part + d

        def finish(acc):
            out = acc if scale == 1.0 else acc * scale
            if r_ref is not None:
                out = out + r_ref[...]
            o_ref[...] = out.astype(out_dtype)

        if nk == 1:
            finish(part)
        else:
            acc_ref = refs[pos + 1]
            k = pl.program_id(2)

            @pl.when(k == 0)
            def _():
                acc_ref[...] = part

            @pl.when(k > 0)
            def _():
                acc_ref[...] += part

            @pl.when(k == nk - 1)
            def _():
                finish(acc_ref[...])

    if mode == "nn":
        a_spec = pl.BlockSpec((bm, bk), lambda i, j, k: (i, k))
        b_spec = pl.BlockSpec((bk, bn), lambda i, j, k: (k, j))
    elif mode == "nt":
        a_spec = pl.BlockSpec((bm, bk), lambda i, j, k: (i, k))
        b_spec = pl.BlockSpec((bn, bk), lambda i, j, k: (j, k))
    else:
        a_spec = pl.BlockSpec((bk, bm), lambda i, j, k: (k, i))
        b_spec = pl.BlockSpec((bk, bn), lambda i, j, k: (k, j))
    o_spec = pl.BlockSpec((bm, bn), lambda i, j, k: (i, j))
    in_specs, args = [], []
    for a, b in pairs:
        in_specs += [a_spec, b_spec]
        args += [a, b]
    if res is not None:
        in_specs.append(o_spec)
        args.append(res)
    if after is not None:
        in_specs.append(_ANY)
        args.append(after)
    return pl.pallas_call(
        body, name=name, grid=(M // bm, N // bn, nk),
        in_specs=in_specs, out_specs=o_spec,
        out_shape=jax.ShapeDtypeStruct((M, N), out_dtype),
        scratch_shapes=[pltpu.VMEM((bm, bn), F32)] if nk > 1 else [],
        compiler_params=_cp("parallel", "parallel", "arbitrary"),
    )(*args)


def _rms_fwd(name, x, g):
    T = x.shape[0]
    bm = _pick(T, (512, 256, 128))

    def body(x_ref, g_ref, o_ref):
        xv = x_ref[...]
        r = lax.rsqrt(jnp.mean(xv * xv, axis=-1, keepdims=True) + EPS)
        o_ref[...] = (xv * r * g_ref[...]).astype(BF16)

    return pl.pallas_call(
        body, name=name, grid=(T // bm,),
        in_specs=[pl.BlockSpec((bm, D), lambda i: (i, 0)), pl.BlockSpec((1, D), lambda i: (0, 0))],
        out_specs=pl.BlockSpec((bm, D), lambda i: (i, 0)),
        out_shape=jax.ShapeDtypeStruct((T, D), BF16),
        compiler_params=_cp("parallel"),
    )(x, g)


def _rms_bwd(name, x, g, dh, dres):
    T = x.shape[0]
    bm = _pick(T, (512, 256, 128))

    def body(x_ref, g_ref, dh_ref, dres_ref, dx_ref, dg_ref):
        xv = x_ref[...]
        r = lax.rsqrt(jnp.mean(xv * xv, axis=-1, keepdims=True) + EPS)
        xh = xv * r
        dhv = dh_ref[...]
        dxh = dhv * g_ref[...]
        dx_ref[...] = dres_ref[...] + r * (dxh - xh * jnp.mean(dxh * xh, axis=-1, keepdims=True))
        part = jnp.sum(dhv * xh, axis=0, keepdims=True)

        @pl.when(pl.program_id(0) == 0)
        def _():
            dg_ref[...] = part

        @pl.when(pl.program_id(0) > 0)
        def _():
            dg_ref[...] += part

    row = pl.BlockSpec((bm, D), lambda i: (i, 0))
    vec = pl.BlockSpec((1, D), lambda i: (0, 0))
    return pl.pallas_call(
        body, name=name, grid=(T // bm,),
        in_specs=[row, vec, row, row], out_specs=[row, vec],
        out_shape=[jax.ShapeDtypeStruct((T, D), F32), jax.ShapeDtypeStruct((1, D), F32)],
        compiler_params=_cp("arbitrary"),
    )(x, g, dh, dres)


def _final_loss(name, x, g, tgt):
    T = x.shape[0]
    bm = _pick(T, (512, 256, 128))

    def body(x_ref, g_ref, t_ref, loss_ref, dx_ref, dg_ref):
        xv = x_ref[...]
        gv = g_ref[...]
        r = lax.rsqrt(jnp.mean(xv * xv, axis=-1, keepdims=True) + EPS)
        xh = xv * r
        e = xh * gv - t_ref[...]
        lpart = jnp.broadcast_to(0.5 * jnp.sum(jnp.mean(e * e, axis=-1, keepdims=True), axis=0, keepdims=True), (1, 128))
        dy = e * (1.0 / D)
        dxh = dy * gv
        dx_ref[...] = r * (dxh - xh * jnp.mean(dxh * xh, axis=-1, keepdims=True))
        gpart = jnp.sum(dy * xh, axis=0, keepdims=True)

        @pl.when(pl.program_id(0) == 0)
        def _():
            loss_ref[...] = lpart
            dg_ref[...] = gpart

        @pl.when(pl.program_id(0) > 0)
        def _():
            loss_ref[...] += lpart
            dg_ref[...] += gpart

    row = pl.BlockSpec((bm, D), lambda i: (i, 0))
    vec = pl.BlockSpec((1, D), lambda i: (0, 0))
    return pl.pallas_call(
        body, name=name, grid=(T // bm,),
        in_specs=[row, vec, row],
        out_specs=[pl.BlockSpec((1, 128), lambda i: (0, 0)), row, vec],
        out_shape=[jax.ShapeDtypeStruct((1, 128), F32), jax.ShapeDtypeStruct((T, D), F32),
                   jax.ShapeDtypeStruct((1, D), F32)],
        compiler_params=_cp("arbitrary"),
    )(x, g, tgt)


def _ffn_up(name, h, wg, wu):
    T = h.shape[0]
    bm = _pick(T, (2048, 1024, 512, 256, 128))
    bn = 256

    def body(h_ref, wg_ref, wu_ref, sa_ref, ds_ref, act_ref):
        hv = h_ref[...]
        a = _dot(hv, wg_ref[...], _NT)
        b = _dot(hv, wu_ref[...], _NT)
        s = _sigmoid(a)
        sa = a * s
        sa_ref[...] = sa.astype(BF16)
        ds_ref[...] = (b * (s * (1.0 + a * (1.0 - s)))).astype(BF16)
        act_ref[...] = (sa * b).astype(BF16)

    w_spec = pl.BlockSpec((bn, D), lambda i, j: (j, 0))
    o_spec = pl.BlockSpec((bm, bn), lambda i, j: (i, j))
    return pl.pallas_call(
        body, name=name, grid=(T // bm, FF // bn),
        in_specs=[pl.BlockSpec((bm, D), lambda i, j: (i, 0)), w_spec, w_spec],
        out_specs=[o_spec, o_spec, o_spec],
        out_shape=[jax.ShapeDtypeStruct((T, FF), BF16)] * 3,
        compiler_params=_cp("parallel", "parallel"),
    )(h, wg, wu)


def _ffn_dact(name, dy, wd, sa, ds, after=None):
    T = dy.shape[0]
    bm = _pick(T, (2048, 1024, 512, 256, 128))
    bn = 256

    def body(dy_ref, wd_ref, sa_ref, ds_ref, *rest):
        da_ref, db_ref, dy_bf = rest[-3:]

        @pl.when(pl.program_id(1) == 0)
        def _():
            dy_bf[...] = dy_ref[...].astype(BF16)

        dact = 0.5 * _dot(dy_bf[...], wd_ref[...], _NT)
        da_ref[...] = (dact * ds_ref[...].astype(F32)).astype(BF16)
        db_ref[...] = (dact * sa_ref[...].astype(F32)).astype(BF16)

    t_spec = pl.BlockSpec((bm, bn), lambda i, j: (i, j))
    return pl.pallas_call(
        body, name=name, grid=(T // bm, FF // bn),
        in_specs=[pl.BlockSpec((bm, D), lambda i, j: (i, 0)), pl.BlockSpec((bn, D), lambda i, j: (j, 0)),
                  t_spec, t_spec] + [_ANY] * (after is not None),
        out_specs=[t_spec, t_spec],
        out_shape=[jax.ShapeDtypeStruct((T, FF), BF16), jax.ShapeDtypeStruct((T, FF), BF16)],
        scratch_shapes=[pltpu.VMEM((bm, D), BF16)],
        compiler_params=_cp("parallel", "arbitrary"),
    )(dy, wd, sa, ds, *([after] if after is not None else []))


def _merge_specs(T, bm, bn):
    y_spec = pl.BlockSpec((bm, BW), lambda i, j: (i, 0))
    wb_spec = pl.BlockSpec((NBR, bn, BW), lambda i, j: (0, j, 0))
    gate_specs = [pl.BlockSpec((bm, bn), functools.partial(lambda i, j, o: (i, o + j), o=(COL_GATE + g * D) // bn))
                  for g in range(NBR)]
    t_spec = pl.BlockSpec((bm, bn), lambda i, j: (i, j))
    return y_spec, wb_spec, gate_specs, t_spec


def _merge_fwd(name, ys, wb, proj):
    T = proj.shape[0]
    bm = _pick(T, (512, 256, 128))
    bn = 512
    y_spec, wb_spec, gate_specs, t_spec = _merge_specs(T, bm, bn)

    def body(y0, y1, y2, y3, wb_ref, g0, g1, g2, g3, o_ref):
        acc = None
        for g, (y_ref, g_ref) in enumerate(((y0, g0), (y1, g1), (y2, g2), (y3, g3))):
            t = _sigmoid(g_ref[...].astype(F32)) * _dot(y_ref[...], wb_ref[g], _NT)
            acc = t if acc is None else acc + t
        o_ref[...] = acc.astype(BF16)

    return pl.pallas_call(
        body, name=name, grid=(T // bm, D // bn),
        in_specs=[y_spec] * NBR + [wb_spec] + gate_specs, out_specs=t_spec,
        out_shape=jax.ShapeDtypeStruct((T, D), BF16),
        compiler_params=_cp("parallel", "parallel"),
    )(*ys, wb, proj, proj, proj, proj)


def _merge_bwd(name, dm, ys, wb, proj):
    T = proj.shape[0]
    bm = _pick(T, (512, 256, 128))
    bn = 512
    y_spec, wb_spec, gate_specs, t_spec = _merge_specs(T, bm, bn)

    def body(dm_ref, y0, y1, y2, y3, wb_ref, g0, g1, g2, g3, *outs):
        dmv = dm_ref[...]
        j = pl.program_id(1)
        for g, (y_ref, g_ref) in enumerate(((y0, g0), (y1, g1), (y2, g2), (y3, g3))):
            br = _dot(y_ref[...], wb_ref[g], _NT)
            s = _sigmoid(g_ref[...].astype(F32))
            outs[g][...] = (dmv * br * (s * (1.0 - s))).astype(BF16)
            dbr = (dmv * s).astype(BF16)
            outs[NBR + g][...] = dbr
            part = _dot(dbr, wb_ref[g])
            dy_ref = outs[2 * NBR + g]

            @pl.when(j == 0)
            def _():
                dy_ref[...] = part

            @pl.when(j > 0)
            def _():
                dy_ref[...] += part

    return pl.pallas_call(
        body, name=name, grid=(T // bm, D // bn),
        in_specs=[t_spec] + [y_spec] * NBR + [wb_spec] + gate_specs, out_specs=[t_spec] * (2 * NBR) + [y_spec] * NBR,
        out_shape=[jax.ShapeDtypeStruct((T, D), BF16)] * (2 * NBR) + [jax.ShapeDtypeStruct((T, BW), F32)] * NBR,
        compiler_params=_cp("parallel", "arbitrary"),
    )(dm, *ys, wb, proj, proj, proj, proj)


def _sgu_block(u_pre, v_pre, ln_g, ln_b, w, bias):
    u = _gelu(u_pre)
    vf = _gelu(v_pre)
    mu = jnp.mean(vf, axis=-1, keepdims=True)
    var = jnp.mean(jnp.square(vf - mu), axis=-1, keepdims=True)
    vn = (vf - mu) * lax.rsqrt(var + EPS) * ln_g + ln_b
    ri = lax.broadcasted_iota(jnp.int32, (128, 128), 0)
    ci = lax.broadcasted_iota(jnp.int32, (128, 128), 1)
    mask = (ri // CHUNK) >= (ci // CHUNK)
    outs = [_dot(jnp.where(mask, w[g], 0.0), vn[:, g * 128:(g + 1) * 128]) for g in range(4)]
    mixed = jnp.concatenate(outs, axis=1) + bias
    return u * mixed


def _sgu_param_specs():
    return [pl.BlockSpec((1, BW), lambda i: (0, 0)), pl.BlockSpec((1, BW), lambda i: (0, 0)),
            pl.BlockSpec((4, 128, 128), lambda i: (0, 0, 0)), pl.BlockSpec((128, BW), lambda i: (0, 0))]


def _sgu_fwd(name, proj, ln_g, ln_b, w, bias):
    T = proj.shape[0]
    rb = _pick(T, (256, 128))

    def body(u_ref, v_ref, g_ref, b_ref, w_ref, bias_ref, y_ref):
        for n in range(rb // 128):
            rows = slice(n * 128, (n + 1) * 128)
            y = _sgu_block(u_ref[rows, :].astype(F32), v_ref[rows, :].astype(F32), g_ref[...], b_ref[...], w_ref[...],
                           bias_ref[...])
            y_ref[rows, :] = y.astype(BF16)

    return pl.pallas_call(
        body, name=name, grid=(T // rb,),
        in_specs=[pl.BlockSpec((rb, BW), lambda i: (i, COL_AU // BW)), pl.BlockSpec((rb, BW), lambda i: (i, COL_AV // BW))]
        + _sgu_param_specs(),
        out_specs=pl.BlockSpec((rb, BW), lambda i: (i, 0)),
        out_shape=jax.ShapeDtypeStruct((T, BW), BF16),
        compiler_params=_cp("parallel"),
    )(proj, proj, ln_g, ln_b, w, bias)


def _sgu_bwd(name, proj, dy, ln_g, ln_b, w, bias):
    T = proj.shape[0]
    rb = _pick(T, (256, 128))

    def body(u_ref, v_ref, dy_ref, g_ref, b_ref, w_ref, bias_ref, du_ref, dv_ref, dg_ref, db_ref, dw_ref, dbias_ref):
        acc = None
        for n in range(rb // 128):
            rows = slice(n * 128, (n + 1) * 128)
            _, vjp = jax.vjp(_sgu_block, u_ref[rows, :].astype(F32), v_ref[rows, :].astype(F32), g_ref[...], b_ref[...],
                             w_ref[...],
                             bias_ref[...])
            du, dv, *dp = vjp(dy_ref[rows, :])
            du_ref[rows, :] = du.astype(BF16)
            dv_ref[rows, :] = dv.astype(BF16)
            acc = dp if acc is None else [p + q for p, q in zip(acc, dp)]

        @pl.when(pl.program_id(0) == 0)
        def _():
            for r, p in zip((dg_ref, db_ref, dw_ref, dbias_ref), acc):
                r[...] = p

        @pl.when(pl.program_id(0) > 0)
        def _():
            for r, p in zip((dg_ref, db_ref, dw_ref, dbias_ref), acc):
                r[...] += p

    row = pl.BlockSpec((rb, BW), lambda i: (i, 0))
    return pl.pallas_call(
        body, name=name, grid=(T // rb,),
        in_specs=[pl.BlockSpec((rb, BW), lambda i: (i, COL_AU // BW)), pl.BlockSpec((rb, BW), lambda i: (i, COL_AV // BW)),
                  row] + _sgu_param_specs(),
        out_specs=[row, row] + _sgu_param_specs(),
        out_shape=[jax.ShapeDtypeStruct((T, BW), BF16), jax.ShapeDtypeStruct((T, BW), BF16),
                   jax.ShapeDtypeStruct((1, BW), F32), jax.ShapeDtypeStruct((1, BW), F32),
                   jax.ShapeDtypeStruct((4, 128, 128), F32), jax.ShapeDtypeStruct((128, BW), F32)],
        compiler_params=_cp("arbitrary"),
    )(proj, proj, dy, ln_g, ln_b, w, bias)


def _halo_block(ref, i, rblk, halo):
    r0 = pl.multiple_of(i * rblk, rblk)
    h0 = pl.multiple_of(jnp.maximum(r0 - 16, 0), 16)
    top = jnp.where(i > 0, ref[pl.ds(h0, 16), :].astype(F32), 0.0)[16 - halo:]
    return jnp.concatenate([top, ref[pl.ds(r0, rblk), :].astype(F32)], axis=0)


def _with_halo_grad(dfull, pending, halo, rblk):
    tail = jnp.concatenate([jnp.zeros((rblk - halo, 128), F32), pending], axis=0)
    return dfull[halo:] + tail


def _conv4(xfull, rows):
    acc = None
    for k in range(4):
        t = rows[k] * _shift(xfull, 3 - k)[8:]
        acc = t if acc is None else acc + t
    return acc


def _lru_block(xfull, gate, h0, c0, c1, c2, c3, cb, wa, ba, wx, bx, lam):
    n = gate.shape[0]
    xc = _conv4(xfull, (c0, c1, c2, c3)) + cb
    r = _sigmoid(_dot(xc, wa) + ba)
    ig = _sigmoid(_dot(xc, wx) + bx)
    log_a = -LRU_C * r * _softplus(-lam)
    a = jnp.exp(log_a)
    mult = jnp.sqrt(-_expm1(2.0 * log_a))
    b = mult * (ig * xc)
    row = lax.broadcasted_iota(jnp.int32, (n, 128), 0)
    b = b + jnp.where(row == 0, a * h0, 0.0)
    h = _scan(a, b)
    out = h * _gelu(gate)
    h_last = jnp.sum(jnp.where(row == n - 1, h, 0.0), axis=0, keepdims=True)
    return out, h_last


def _lru_param_specs():
    vec = pl.BlockSpec((1, 128), lambda g: (0, g))
    mat = pl.BlockSpec((None, 128, 128), lambda g: (g, 0, 0))
    return [pl.BlockSpec((4, 128), lambda g: (0, g)), vec, mat, vec, mat, vec, vec]


def _lru_load_params(cw_ref, cb_ref, wa_ref, ba_ref, wx_ref, bx_ref, lam_ref):
    return (cw_ref[0:1, :], cw_ref[1:2, :], cw_ref[2:3, :], cw_ref[3:4, :], cb_ref[...], wa_ref[...], ba_ref[...],
            wx_ref[...], bx_ref[...], lam_ref[...])


def _lru_fwd(name, proj, cw, cb, wa, ba, wx, bx, lam):
    T = proj.shape[0]
    rblk = _pick(T, (256, 128))
    nblk = T // rblk

    def body(x_ref, gt_ref, cw_ref, cb_ref, wa_ref, ba_ref, wx_ref, bx_ref, lam_ref, y_ref, hc_ref):
        params = _lru_load_params(cw_ref, cb_ref, wa_ref, ba_ref, wx_ref, bx_ref, lam_ref)

        def step(i, h0):
            r0 = pl.multiple_of(i * rblk, rblk)
            out, h_last = _lru_block(_halo_block(x_ref, i, rblk, 8), gt_ref[pl.ds(r0, rblk), :].astype(F32), h0,
                                     *params)
            y_ref[pl.ds(r0, rblk), :] = out.astype(BF16)
            hc_ref[pl.ds(pl.multiple_of(i * 8, 8), 8), :] = jnp.broadcast_to(h0, (8, 128))
            return h_last

        lax.fori_loop(0, nblk, step, jnp.zeros((1, 128), F32))

    return pl.pallas_call(
        body, name=name, grid=(4,),
        in_specs=[pl.BlockSpec((T, 128), lambda g: (0, COL_BX // 128 + g)),
                  pl.BlockSpec((T, 128), lambda g: (0, COL_BG // 128 + g))] + _lru_param_specs(),
        out_specs=[pl.BlockSpec((T, 128), lambda g: (0, g)), pl.BlockSpec((nblk * 8, 128), lambda g: (0, g))],
        out_shape=[jax.ShapeDtypeStruct((T, BW), BF16), jax.ShapeDtypeStruct((nblk * 8, BW), F32)],
        compiler_params=_cp("parallel"),
    )(proj, proj, cw, cb, wa, ba, wx, bx, lam)


def _lru_bwd(name, proj, dy, hc, cw, cb, wa, ba, wx, bx, lam):
    T = proj.shape[0]
    rblk = _pick(T, (256, 128))
    nblk = T // rblk

    def body(x_ref, gt_ref, dy_ref, hc_ref, cw_ref, cb_ref, wa_ref, ba_ref, wx_ref, bx_ref, lam_ref,
             dx_ref, dgt_ref, dcw_ref, dcb_ref, dwa_ref, dba_ref, dwx_ref, dbx_ref, dlam_ref):
        params = _lru_load_params(cw_ref, cb_ref, wa_ref, ba_ref, wx_ref, bx_ref, lam_ref)

        def step(it, carry):
            dh_last, pending, acc = carry
            i = nblk - 1 - it
            r0 = pl.multiple_of(i * rblk, rblk)
            h0 = hc_ref[pl.ds(pl.multiple_of(i * 8, 8), 1), :]
            _, vjp = jax.vjp(_lru_block, _halo_block(x_ref, i, rblk, 8), gt_ref[pl.ds(r0, rblk), :].astype(F32), h0,
                             *params)
            dfull, dgate, dh0, *dp = vjp((dy_ref[pl.ds(r0, rblk), :], dh_last))
            dx_ref[pl.ds(r0, rblk), :] = _with_halo_grad(dfull, pending, 8, rblk).astype(BF16)
            dgt_ref[pl.ds(r0, rblk), :] = dgate.astype(BF16)
            return dh0, dfull[:8], tuple(p + q for p, q in zip(acc, dp))

        zeros = tuple(jnp.zeros(p.shape, F32) for p in params)
        _, _, acc = lax.fori_loop(0, nblk, step, (jnp.zeros((1, 128), F32), jnp.zeros((8, 128), F32), zeros))
        for k in range(4):
            dcw_ref[k:k + 1, :] = acc[k]
        for r, p in zip((dcb_ref, dwa_ref, dba_ref, dwx_ref, dbx_ref, dlam_ref), acc[4:]):
            r[...] = p

    col = pl.BlockSpec((T, 128), lambda g: (0, g))
    return pl.pallas_call(
        body, name=name, grid=(4,),
        in_specs=[pl.BlockSpec((T, 128), lambda g: (0, COL_BX // 128 + g)),
                  pl.BlockSpec((T, 128), lambda g: (0, COL_BG // 128 + g)), col,
                  pl.BlockSpec((nblk * 8, 128), lambda g: (0, g))] + _lru_param_specs(),
        out_specs=[col, col] + _lru_param_specs(),
        out_shape=[jax.ShapeDtypeStruct((T, BW), BF16), jax.ShapeDtypeStruct((T, BW), BF16),
                   jax.ShapeDtypeStruct((4, BW), F32), jax.ShapeDtypeStruct((1, BW), F32),
                   jax.ShapeDtypeStruct((4, 128, 128), F32), jax.ShapeDtypeStruct((1, BW), F32),
                   jax.ShapeDtypeStruct((4, 128, 128), F32), jax.ShapeDtypeStruct((1, BW), F32),
                   jax.ShapeDtypeStruct((1, BW), F32)],
        compiler_params=_cp("parallel"),
    )(proj, proj, dy, hc, cw, cb, wa, ba, wx, bx, lam)


def _conv_block(xfull, c0, c1, c2, c3):
    return _silu(_conv4(xfull, (c0, c1, c2, c3)))


def _conv_fwd(name, proj, col0, cw, cw_col0):
    T = proj.shape[0]
    rblk = _pick(T, (256, 128))
    nblk = T // rblk

    def body(x_ref, cw_ref, y_ref):
        rows = (cw_ref[0:1, :], cw_ref[1:2, :], cw_ref[2:3, :], cw_ref[3:4, :])

        def step(i, c):
            r0 = pl.multiple_of(i * rblk, rblk)
            y_ref[pl.ds(r0, rblk), :] = _conv_block(_halo_block(x_ref, i, rblk, 8), *rows)
            return c

        lax.fori_loop(0, nblk, step, 0)

    return pl.pallas_call(
        body, name=name, grid=(4,),
        in_specs=[pl.BlockSpec((T, 128), lambda g: (0, col0 // 128 + g)),
                  pl.BlockSpec((4, 128), lambda g: (0, cw_col0 // 128 + g))],
        out_specs=pl.BlockSpec((T, 128), lambda g: (0, g)),
        out_shape=jax.ShapeDtypeStruct((T, BW), F32),
        compiler_params=_cp("parallel"),
    )(proj, cw)


def _conv_bwd(name, proj, col0, dy, cw, cw_col0):
    T = proj.shape[0]
    rblk = _pick(T, (256, 128))
    nblk = T // rblk

    def body(x_ref, dy_ref, cw_ref, dx_ref, dcw_ref):
        rows = (cw_ref[0:1, :], cw_ref[1:2, :], cw_ref[2:3, :], cw_ref[3:4, :])

        def step(it, carry):
            pending, acc = carry
            i = nblk - 1 - it
            r0 = pl.multiple_of(i * rblk, rblk)
            _, vjp = jax.vjp(_conv_block, _halo_block(x_ref, i, rblk, 8), *rows)
            dfull, *dp = vjp(dy_ref[pl.ds(r0, rblk), :])
            dx_ref[pl.ds(r0, rblk), :] = _with_halo_grad(dfull, pending, 8, rblk).astype(BF16)
            return dfull[:8], tuple(p + q for p, q in zip(acc, dp))

        zeros = tuple(jnp.zeros((1, 128), F32) for _ in range(4))
        _, acc = lax.fori_loop(0, nblk, step, (jnp.zeros((8, 128), F32), zeros))
        for k in range(4):
            dcw_ref[k:k + 1, :] = acc[k]

    col = pl.BlockSpec((T, 128), lambda g: (0, g))
    return pl.pallas_call(
        body, name=name, grid=(4,),
        in_specs=[pl.BlockSpec((T, 128), lambda g: (0, col0 // 128 + g)), col,
                  pl.BlockSpec((4, 128), lambda g: (0, cw_col0 // 128 + g))],
        out_specs=[col, pl.BlockSpec((4, 128), lambda g: (0, g))],
        out_shape=[jax.ShapeDtypeStruct((T, BW), BF16), jax.ShapeDtypeStruct((4, BW), F32)],
        compiler_params=_cp("parallel"),
    )(proj, dy, cw)


def _pool_block(xfull, pw, sc, t0, gi):
    n = xfull.shape[0] - 16
    s2 = xfull + _shift(xfull, 1)
    s4 = s2 + _shift(s2, 2)
    s8 = s4 + _shift(s4, 4)
    s16 = s8 + _shift(s8, 8)
    s = jnp.where(gi == 0, s2, jnp.where(gi == 1, s4, jnp.where(gi == 2, s8, s16)))[16:]
    t = t0 + lax.broadcasted_iota(jnp.int32, (n, 128), 0)
    cnt = jnp.minimum(t + 1, lax.shift_left(jnp.int32(2), gi)).astype(F32)
    pooled = s / cnt - xfull[16:]
    return _dot(pooled, pw) * sc


def _pool_fwd(name, proj, pw, sc):
    T = proj.shape[0]
    rblk = _pick(T, (256, 128))
    nblk = T // rblk

    def body(x_ref, pw_ref, sc_ref, y_ref):
        gi = pl.program_id(0)

        def step(i, c):
            r0 = pl.multiple_of(i * rblk, rblk)
            y = _pool_block(_halo_block(x_ref, i, rblk, 16), pw_ref[...], sc_ref[...], r0, gi)
            y_ref[pl.ds(r0, rblk), :] = y.astype(BF16)
            return c

        lax.fori_loop(0, nblk, step, 0)

    return pl.pallas_call(
        body, name=name, grid=(4,),
        in_specs=[pl.BlockSpec((T, 128), lambda g: (0, COL_DX // 128 + g)),
                  pl.BlockSpec((None, 128, 128), lambda g: (g, 0, 0)), pl.BlockSpec((1, 128), lambda g: (0, g))],
        out_specs=pl.BlockSpec((T, 128), lambda g: (0, g)),
        out_shape=jax.ShapeDtypeStruct((T, BW), BF16),
        compiler_params=_cp("parallel"),
    )(proj, pw, sc)


def _pool_bwd(name, proj, dy, pw, sc):
    T = proj.shape[0]
    rblk = _pick(T, (256, 128))
    nblk = T // rblk

    def body(x_ref, dy_ref, pw_ref, sc_ref, dx_ref, dpw_ref, dsc_ref):
        gi = pl.program_id(0)

        def step(it, carry):
            pending, apw, asc = carry
            i = nblk - 1 - it
            r0 = pl.multiple_of(i * rblk, rblk)
            _, vjp = jax.vjp(lambda xf, w, s: _pool_block(xf, w, s, r0, gi), _halo_block(x_ref, i, rblk, 16),
                             pw_ref[...], sc_ref[...])
            dfull, dw, ds = vjp(dy_ref[pl.ds(r0, rblk), :])
            dx_ref[pl.ds(r0, rblk), :] = _with_halo_grad(dfull, pending, 16, rblk).astype(BF16)
            return dfull[:16], apw + dw, asc + ds

        _, apw, asc = lax.fori_loop(0, nblk, step, (jnp.zeros((16, 128), F32), jnp.zeros((128, 128), F32),
                                                    jnp.zeros((1, 128), F32)))
        dpw_ref[...] = apw
        dsc_ref[...] = asc

    col = pl.BlockSpec((T, 128), lambda g: (0, g))
    mat = pl.BlockSpec((None, 128, 128), lambda g: (g, 0, 0))
    vec = pl.BlockSpec((1, 128), lambda g: (0, g))
    return pl.pallas_call(
        body, name=name, grid=(4,),
        in_specs=[pl.BlockSpec((T, 128), lambda g: (0, COL_DX // 128 + g)), col, mat, vec],
        out_specs=[col, mat, vec],
        out_shape=[jax.ShapeDtypeStruct((T, BW), BF16), jax.ShapeDtypeStruct((4, 128, 128), F32),
                   jax.ShapeDtypeStruct((1, BW), F32)],
        compiler_params=_cp("parallel"),
    )(proj, dy, pw, sc)


@jax.custom_vjp
def _dot3(a, b):
    ah = a.astype(BF16)
    al = (a - ah.astype(F32)).astype(BF16)
    bh = b.astype(BF16)
    bl = (b - bh.astype(F32)).astype(BF16)

    def d(x, y):
        return lax.dot_general(x, y, _NN, preferred_element_type=F32)

    return d(ah, bh) + (d(ah, bl) + d(al, bh))


def _dot3_fwd(a, b):
    return _dot3(a, b), (a, b)


def _dot3_bwd(res, g):
    a, b = res
    return _dot(g, b, _NT), _dot(a, g, _TN)


_dot3.defvjp(_dot3_fwd, _dot3_bwd)


def _pad_rows2(x):
    return jnp.concatenate([x, jnp.zeros_like(x)], axis=0)


@jax.custom_vjp
def _tri_inv(mats):
    n = mats[0].shape[0]
    eye = (lax.broadcasted_iota(jnp.int32, (n, n), 0) == lax.broadcasted_iota(jnp.int32, (n, n), 1)).astype(F32)
    ps = [eye - a for a in mats]
    ms = list(mats)
    k = 2
    while k < n:
        ms = [_dot3(t, t) for t in ms]
        ps = [p + _dot3(p, t) for p, t in zip(ps, ms)]
        k *= 2
    return ps


def _tri_inv_fwd(mats):
    ts = _tri_inv(mats)
    return ts, ts


def _tri_inv_bwd(ts, gs):
    half = [_dot(t, g, _TN) for t, g in zip(ts, gs)]
    return ([-_dot(h, t, _NT) for h, t in zip(half, ts)],)


_tri_inv.defvjp(_tri_inv_fwd, _tri_inv_bwd)


def _cumsum_rows(x):
    n = x.shape[0]
    row = lax.broadcasted_iota(jnp.int32, x.shape, 0)
    k = 1
    while k < n:
        x = x + jnp.where(row >= k, _shift(x, k), 0.0)
        k *= 2
    return x


def _gdn_prep(qcs, kcs, vcs, tails, alog, dtb):
    C = CHUNK
    pairs = [(c, h) for c in range(len(qcs)) for h in range(4)]
    lane = lax.broadcasted_iota(jnp.int32, (C, 128), 1)
    row = lax.broadcasted_iota(jnp.int32, (C, 128), 0)
    incl = row >= lane
    sig = [_sigmoid(t) for t in tails]
    gfull = [-jnp.exp(alog) * _softplus(t + dtb) for t in tails]
    beta = [jnp.sum(jnp.where(lane == h, sig[c], 0.0), axis=1, keepdims=True) for c, h in pairs]
    g = [jnp.sum(jnp.where(lane == h + 4, gfull[c], 0.0), axis=1, keepdims=True) for c, h in pairs]
    qs = [qcs[c][:, h * 128:(h + 1) * 128] for c, h in pairs]
    ks = [kcs[c][:, h * 128:(h + 1) * 128] for c, h in pairs]
    vs = [vcs[c][:, h * 128:(h + 1) * 128] for c, h in pairs]
    q = [t * lax.rsqrt(jnp.sum(t * t, axis=-1, keepdims=True) + EPS) * (GDN_DK ** -0.5) for t in qs]
    k = [t * lax.rsqrt(jnp.sum(t * t, axis=-1, keepdims=True) + EPS) for t in ks]
    gc = [_cumsum_rows(jnp.broadcast_to(t, (C, 128))) for t in g]
    gc_t = [jnp.transpose(jnp.concatenate([t, t], axis=0)) for t in gc]
    gc_col = [jnp.sum(jnp.where(lane == 0, t, 0.0), axis=1, keepdims=True) for t in gc]
    ri = lax.broadcasted_iota(jnp.int32, (C, C), 0)
    ci = lax.broadcasted_iota(jnp.int32, (C, C), 1)
    decay = [jnp.exp(jnp.where(incl, a - b[:C, :], -1e30)) for a, b in zip(gc, gc_t)]
    decay_sq = [jnp.exp(jnp.where(ri > ci, a - jnp.transpose(b)[:C, :], -1e30)) for a, b in zip(gc_col, gc)]
    kb = [a * b for a, b in zip(k, beta)]
    kk = [_dot(a, b, _NT) for a, b in zip(kb, k)]
    t_mat = _tri_inv([jnp.where(ri > ci, a * b, 0.0) for a, b in zip(kk, decay_sq)])
    egc = [jnp.exp(t) for t in gc]
    u = [_dot(t, a * b) for t, a, b in zip(t_mat, vs, beta)]
    w = [_dot(t, a * b) for t, a, b in zip(t_mat, kb, egc)]
    qk = [_dot(a, _pad_rows2(b), _NT) for a, b in zip(q, k)]
    attn = [jnp.where(incl, a * b, 0.0) for a, b in zip(qk, decay)]
    g_last = [jnp.sum(jnp.where(row == C - 1, t, 0.0), axis=0, keepdims=True) for t in gc]
    qe = [a * b for a, b in zip(q, egc)]
    kd = [a * jnp.exp(b - c_) for a, b, c_ in zip(k, g_last, gc)]
    egl = [jnp.exp(t) for t in g_last]

    def per_chunk(vals):
        return [jnp.concatenate(vals[4 * c:4 * c + 4], axis=1) for c in range(len(qcs))]

    return tuple(per_chunk(t) for t in (u, w, qe, kd, attn, egl))


def _gdn_scan_chunk(states, u, w, qe, kd, attn, egl, z, ng):
    hs = range(4)

    def sl(t, h):
        return t[:, h * 128:(h + 1) * 128]

    ws = [_dot(sl(w, h), states[h]) for h in hs]
    qs = [_dot(sl(qe, h), states[h]) for h in hs]
    v_new = [sl(u, h) - ws[h] for h in hs]
    av = [_dot(sl(attn, h), _pad_rows2(v_new[h])) for h in hs]
    kv = [_dot(sl(kd, h), v_new[h], _TN) for h in hs]
    nxt = tuple(states[h] * sl(egl, h) + kv[h] for h in hs)
    o = [qs[h] + av[h] for h in hs]
    on = [t * lax.rsqrt(jnp.mean(t * t, axis=-1, keepdims=True) + EPS) * ng for t in o]
    return nxt, jnp.concatenate(on, axis=1) * _silu(z)


def _gdn_blocks(T):
    tb = _pick(T, (512, 256, 128, 64))
    return tb, T // tb, tb // CHUNK


PREP_CHUNKS = 4


def _chunk_rows(i, n):
    return [pl.ds(pl.multiple_of((i * n + j) * CHUNK, CHUNK), CHUNK) for j in range(n)]


def _egl_rows(i, n, size):
    return [pl.ds(pl.multiple_of((i * n + j) * 8, 8), size) for j in range(n)]


def _gdn_prep_fwd(name, qa, ka, va, proj, alog, dtb):
    T = proj.shape[0]
    tb, nb, ncb = _gdn_blocks(T)
    n = PREP_CHUNKS if ncb % PREP_CHUNKS == 0 else 1

    def body(q_ref, k_ref, v_ref, tail_ref, alog_ref, dtb_ref, u_ref, w_ref, qe_ref, kd_ref, at_ref, egl_ref):
        def step(i, c):
            rows = _chunk_rows(i, n)
            u, w, qe, kd, at, egl = _gdn_prep([q_ref[r, :] for r in rows], [k_ref[r, :] for r in rows],
                                              [v_ref[r, :] for r in rows], [tail_ref[r, :].astype(F32) for r in rows],
                                              alog_ref[...], dtb_ref[...])
            for j, (r, e) in enumerate(zip(rows, _egl_rows(i, n, 8))):
                u_ref[r, :] = u[j]
                w_ref[r, :] = w[j].astype(BF16)
                qe_ref[r, :] = qe[j].astype(BF16)
                kd_ref[r, :] = kd[j].astype(BF16)
                at_ref[r, :] = at[j].astype(BF16)
                egl_ref[e, :] = jnp.broadcast_to(egl[j], (8, BW))
            return c

        lax.fori_loop(0, ncb // n, step, 0)

    blk = pl.BlockSpec((tb, BW), lambda j: (j, 0))
    vec = pl.BlockSpec((1, 128), lambda j: (0, 0))
    return pl.pallas_call(
        body, name=name, grid=(nb,),
        in_specs=[blk, blk, blk, pl.BlockSpec((tb, 128), lambda j: (j, COL_TAIL // 128)), vec, vec],
        out_specs=[blk] * 5 + [pl.BlockSpec((ncb * 8, BW), lambda j: (j, 0))],
        out_shape=[jax.ShapeDtypeStruct((T, BW), F32)] + [jax.ShapeDtypeStruct((T, BW), BF16)] * 4
        + [jax.ShapeDtypeStruct((T // 8, BW), F32)],
        compiler_params=_cp("parallel"),
    )(qa, ka, va, proj, alog, dtb)


def _gdn_prep_bwd(name, qa, ka, va, proj, alog, dtb, du, dw, dqe, dkd, dat, degl):
    T = proj.shape[0]
    tb, nb, ncb = _gdn_blocks(T)
    n = PREP_CHUNKS if ncb % PREP_CHUNKS == 0 else 1

    def body(q_ref, k_ref, v_ref, tail_ref, alog_ref, dtb_ref, du_ref, dw_ref, dqe_ref, dkd_ref, dat_ref, degl_ref,
             dq_ref, dk_ref, dv_ref, dtail_ref, dalog_ref, ddtb_ref):
        first = pl.program_id(0) == 0

        def step(i, carry):
            pa, pd = carry
            rows = _chunk_rows(i, n)
            _, vjp = jax.vjp(_gdn_prep, [q_ref[r, :] for r in rows], [k_ref[r, :] for r in rows],
                             [v_ref[r, :] for r in rows], [tail_ref[r, :].astype(F32) for r in rows], alog_ref[...], dtb_ref[...])
            cot = tuple([ref[r, :] for r in rows] for ref in (du_ref, dw_ref, dqe_ref, dkd_ref, dat_ref))
            dq, dk, dv, dtail, da, dd = vjp(cot + ([degl_ref[e, :] for e in _egl_rows(i, n, 1)],))
            for j, r in enumerate(rows):
                dq_ref[r, :] = dq[j]
                dk_ref[r, :] = dk[j]
                dv_ref[r, :] = dv[j]
                dtail_ref[r, :] = dtail[j].astype(BF16)
            return pa + da, pd + dd

        zv = jnp.zeros((1, 128), F32)
        pa, pd = lax.fori_loop(0, ncb // n, step, (zv, zv))

        @pl.when(first)
        def _():
            dalog_ref[...] = pa
            ddtb_ref[...] = pd

        @pl.when(jnp.logical_not(first))
        def _():
            dalog_ref[...] += pa
            ddtb_ref[...] += pd

    blk = pl.BlockSpec((tb, BW), lambda j: (j, 0))
    vec = pl.BlockSpec((1, 128), lambda j: (0, 0))
    return pl.pallas_call(
        body, name=name, grid=(nb,),
        in_specs=[blk, blk, blk, pl.BlockSpec((tb, 128), lambda j: (j, COL_TAIL // 128)), vec, vec]
        + [blk] * 5 + [pl.BlockSpec((ncb * 8, BW), lambda j: (j, 0))],
        out_specs=[blk, blk, blk, pl.BlockSpec((tb, 128), lambda j: (j, 0)), vec, vec],
        out_shape=[jax.ShapeDtypeStruct((T, BW), F32)] * 3 + [jax.ShapeDtypeStruct((T, 128), BF16)]
        + [jax.ShapeDtypeStruct((1, 128), F32)] * 2,
        compiler_params=_cp("arbitrary"),
    )(qa, ka, va, proj, alog, dtb, du, dw, dqe, dkd, dat, degl)


def _gdn_fwd(name, u, w, qe, kd, at, egl, proj, ng):
    T = proj.shape[0]
    tb, nb, ncb = _gdn_blocks(T)

    def body(u_ref, w_ref, qe_ref, kd_ref, at_ref, egl_ref, z_ref, ng_ref, y_ref, sh_ref, state):
        @pl.when(pl.program_id(0) == 0)
        def _():
            state[...] = jnp.zeros((4, 128, 128), F32)

        def step(c, states):
            rows = pl.ds(pl.multiple_of(c * CHUNK, CHUNK), CHUNK)
            for h in range(4):
                sh_ref[h, c] = states[h]
            nxt, y = _gdn_scan_chunk(states, u_ref[rows, :], w_ref[rows, :], qe_ref[rows, :], kd_ref[rows, :],
                                     at_ref[rows, :], egl_ref[pl.ds(pl.multiple_of(c * 8, 8), 1), :],
                                     z_ref[rows, :].astype(F32),
                                     ng_ref[...])
            y_ref[rows, :] = y.astype(BF16)
            return nxt

        states = lax.fori_loop(0, ncb, step, tuple(state[h] for h in range(4)))
        for h in range(4):
            state[h] = states[h]

    blk = pl.BlockSpec((tb, BW), lambda j: (j, 0))
    vec = pl.BlockSpec((1, 128), lambda j: (0, 0))
    return pl.pallas_call(
        body, name=name, grid=(nb,),
        in_specs=[blk] * 5 + [pl.BlockSpec((ncb * 8, BW), lambda j: (j, 0)),
                              pl.BlockSpec((tb, BW), lambda j: (j, COL_CZ // BW)), vec],
        out_specs=[blk, pl.BlockSpec((4, ncb, 128, 128), lambda j: (0, j, 0, 0))],
        out_shape=[jax.ShapeDtypeStruct((T, BW), BF16), jax.ShapeDtypeStruct((4, T // CHUNK, 128, 128), F32)],
        scratch_shapes=[pltpu.VMEM((4, 128, 128), F32)],
        compiler_params=_cp("arbitrary"),
    )(u, w, qe, kd, at, egl, proj, ng)


def _gdn_bwd(name, u, w, qe, kd, at, egl, proj, dy, sh, ng):
    T = proj.shape[0]
    tb, nb, ncb = _gdn_blocks(T)

    def body(u_ref, w_ref, qe_ref, kd_ref, at_ref, egl_ref, z_ref, dy_ref, sh_ref, ng_ref,
             du_ref, dw_ref, dqe_ref, dkd_ref, dat_ref, degl_ref, dz_ref, dng_ref, dstate):
        first = pl.program_id(0) == 0

        @pl.when(first)
        def _():
            dstate[...] = jnp.zeros((4, 128, 128), F32)

        def step(it, carry):
            dstates, pn = carry
            c = ncb - 1 - it
            rows = pl.ds(pl.multiple_of(c * CHUNK, CHUNK), CHUNK)
            erow = pl.multiple_of(c * 8, 8)
            _, vjp = jax.vjp(_gdn_scan_chunk, tuple(sh_ref[h, c] for h in range(4)), u_ref[rows, :],
                             w_ref[rows, :].astype(F32), qe_ref[rows, :].astype(F32), kd_ref[rows, :].astype(F32),
                             at_ref[rows, :].astype(F32), egl_ref[pl.ds(erow, 1), :], z_ref[rows, :].astype(F32),
                             ng_ref[...])
            nxt, du, dw, dqe, dkd, dat, degl, dz, dn = vjp((dstates, dy_ref[rows, :]))
            du_ref[rows, :] = du
            dw_ref[rows, :] = dw
            dqe_ref[rows, :] = dqe
            dkd_ref[rows, :] = dkd
            dat_ref[rows, :] = dat
            degl_ref[pl.ds(erow, 8), :] = jnp.broadcast_to(degl, (8, BW))
            dz_ref[rows, :] = dz.astype(BF16)
            return nxt, pn + dn

        dstates, pn = lax.fori_loop(0, ncb, step, (tuple(dstate[h] for h in range(4)), jnp.zeros((1, 128), F32)))
        for h in range(4):
            dstate[h] = dstates[h]

        @pl.when(first)
        def _():
            dng_ref[...] = pn

        @pl.when(jnp.logical_not(first))
        def _():
            dng_ref[...] += pn

    blk = pl.BlockSpec((tb, BW), lambda j: (nb - 1 - j, 0))
    eblk = pl.BlockSpec((ncb * 8, BW), lambda j: (nb - 1 - j, 0))
    vec = pl.BlockSpec((1, 128), lambda j: (0, 0))
    return pl.pallas_call(
        body, name=name, grid=(nb,),
        in_specs=[blk] * 5 + [eblk, pl.BlockSpec((tb, BW), lambda j: (nb - 1 - j, COL_CZ // BW)), blk,
                              pl.BlockSpec((4, ncb, 128, 128), lambda j: (0, nb - 1 - j, 0, 0)), vec],
        out_specs=[blk] * 5 + [eblk, blk, vec],
        out_shape=[jax.ShapeDtypeStruct((T, BW), F32)] * 5 + [jax.ShapeDtypeStruct((T // 8, BW), F32),
                                                              jax.ShapeDtypeStruct((T, BW), BF16),
                                                              jax.ShapeDtypeStruct((1, 128), F32)],
        scratch_shapes=[pltpu.VMEM((4, 128, 128), F32)],
        compiler_params=_cp("arbitrary"),
    )(u, w, qe, kd, at, egl, proj, dy, sh, ng)


def _adamw_update(w_ref, g_ref, m_ref, v_ref, d_ref, nm_ref, nv_ref):
    gv = g_ref[...]
    m2 = ADAM_B1 * m_ref[...] + (1.0 - ADAM_B1) * gv
    v2 = ADAM_B2 * v_ref[...] + (1.0 - ADAM_B2) * jnp.square(gv)
    m_hat = m2 / (1.0 - ADAM_B1 ** ADAM_STEP)
    v_hat = v2 / (1.0 - ADAM_B2 ** ADAM_STEP)
    d_ref[...] = -ADAM_LR * (m_hat / (jnp.sqrt(v_hat) + ADAM_EPS) + ADAM_WD * w_ref[...])
    nm_ref[...] = m2
    nv_ref[...] = v2


def _adamw_many(name, ws, gs, ms, vs):
    n = len(ws)

    def body(*refs):
        for i in range(n):
            _adamw_update(*[refs[k * n + i] for k in range(7)])

    return pl.pallas_call(
        body, name=name,
        out_shape=[jax.ShapeDtypeStruct(a.shape, F32) for a in ws] * 3,
        compiler_params=_cp(),
    )(*ws, *gs, *ms, *vs)


def _adamw(name, w, g, m, v):
    R, C = w.shape
    br = _pick(R, (512, 256, 240, 128, 64, 8))
    body = functools.partial(_adamw_update)
    spec = pl.BlockSpec((br, C), lambda i: (i, 0))
    return pl.pallas_call(
        body, name=name, grid=(R // br,),
        in_specs=[spec] * 4, out_specs=[spec] * 3,
        out_shape=[jax.ShapeDtypeStruct((R, C), F32)] * 3,
        compiler_params=_cp("parallel"),
    )(w, g, m, v)


def _sum8(name, parts):
    _, R, C = parts.shape
    br = _pick(R, (352, 368, 256, 128, 64, 16, 8))

    def body(p_ref, o_ref):
        acc = p_ref[0].astype(F32)
        for d in range(1, N_DEV):
            acc = acc + p_ref[d].astype(F32)
        o_ref[...] = acc

    return pl.pallas_call(
        body, name=name, grid=(R // br,),
        in_specs=[pl.BlockSpec((N_DEV, br, C), lambda i: (0, i, 0))],
        out_specs=pl.BlockSpec((br, C), lambda i: (i, 0)),
        out_shape=jax.ShapeDtypeStruct((R, C), F32),
        compiler_params=_cp("parallel"),
    )(parts)


_ANY = pl.BlockSpec(memory_space=pl.ANY)
_MESH = pl.DeviceIdType.MESH


def _all_gather(name, shard):
    R, C = shard.shape

    def body(x_ref, out_ref, send_sems, recv_sems, local_sem):
        x, y, c = lax.axis_index("x"), lax.axis_index("y"), lax.axis_index("c")
        me, sibling = (x, y, c), (x, y, 1 - c)
        chips = [(1 - x, y), (x, 1 - y), (1 - x, 1 - y)]

        def slot(px, py, pc):
            return out_ref.at[4 * px + 2 * py + pc]

        def copy(k, block, to, src=None):
            return pltpu.make_async_remote_copy(
                src_ref=slot(*block) if src is None else src, dst_ref=slot(*block),
                send_sem=send_sems.at[k], recv_sem=recv_sems.at[k], device_id=to, device_id_type=_MESH)

        mine = pltpu.make_async_copy(x_ref, slot(*me), local_sem)
        mine.start()
        first = [copy(0, me, sibling, src=x_ref)]
        first += [copy(1 + j, me, (*chip, c), src=x_ref) for j, chip in enumerate(chips)]
        for cp in first:
            cp.start()
        passed = [copy(4 + j, (*chip, c), sibling) for j, chip in enumerate(chips)]
        for j, chip in enumerate(chips):
            copy(1 + j, (*chip, c), me).wait_recv()
            passed[j].start()
        copy(0, sibling, me).wait_recv()
        for j, chip in enumerate(chips):
            copy(4 + j, (*chip, 1 - c), me).wait_recv()
        for cp in first + passed:
            cp.wait_send()
        mine.wait()

    return pl.pallas_call(
        body, name=name,
        in_specs=[_ANY], out_specs=_ANY,
        out_shape=jax.ShapeDtypeStruct((N_DEV, R, C), shard.dtype),
        scratch_shapes=[pltpu.SemaphoreType.DMA((7,)), pltpu.SemaphoreType.DMA((7,)), pltpu.SemaphoreType.DMA],
    )(shard)


_HBM = pl.BlockSpec(memory_space=pltpu.HBM)
_SEM = pl.BlockSpec(memory_space=pltpu.SEMAPHORE)
_EFFECT = pltpu.SideEffectType.DATAFLOW_SIDE_EFFECTING


def _exchange_copies(src_ref, land_ref, send_sems, recv_sems, scatter):
    x, y, c = lax.axis_index("x"), lax.axis_index("y"), lax.axis_index("c")
    me = 4 * x + 2 * y + c
    copies = []
    for k in range(1, N_DEV):
        px, py, pc = x ^ ((k >> 2) & 1), y ^ ((k >> 1) & 1), c ^ (k & 1)
        src = src_ref.at[4 * px + 2 * py + pc] if scatter else src_ref
        copies.append(pltpu.make_async_remote_copy(
            src_ref=src, dst_ref=land_ref.at[me], send_sem=send_sems.at[k - 1], recv_sem=recv_sems.at[k - 1],
            device_id=(px, py, pc), device_id_type=_MESH))
    return copies


def _own_copy(src_ref, land_ref, send_sems, scatter):
    me = 4 * lax.axis_index("x") + 2 * lax.axis_index("y") + lax.axis_index("c")
    return pltpu.make_async_copy(src_ref.at[me] if scatter else src_ref, land_ref.at[me], send_sems.at[N_DEV - 1])


def _exchange_start(name, srcs, lands, scatter, after=None):
    n = len(srcs)

    def body(*refs):
        src_refs, land_refs = refs[:n], refs[n:2 * n]
        outs = refs[2 * n + (after is not None):]
        send, recv = outs[:n], outs[n:2 * n]
        token = refs[-1]
        for g in range(n):
            for cp in _exchange_copies(src_refs[g], land_refs[g], send[g], recv[g], scatter):
                cp.start()
            _own_copy(src_refs[g], land_refs[g], send[g], scatter).start()
        token[...] = jnp.zeros_like(token)

    outs = pl.pallas_call(
        body, name=name,
        out_shape=tuple([pltpu.SemaphoreType.DMA((N_DEV,))] * (2 * n)
                        + [pltpu.HBM(a.shape, a.dtype) for a in list(srcs) + list(lands)]
                        + [jax.ShapeDtypeStruct((8, 128), F32)]),
        in_specs=[_HBM] * (2 * n) + [_ANY] * (after is not None),
        out_specs=tuple([_SEM] * (2 * n) + [_HBM] * (2 * n) + [pl.BlockSpec(memory_space=pltpu.VMEM)]),
        input_output_aliases={i: 2 * n + i for i in range(2 * n)},
        compiler_params=pltpu.CompilerParams(has_side_effects=_EFFECT),
    )(*[pltpu.with_memory_space_constraint(a, pltpu.HBM) for a in list(srcs) + list(lands)],
      *([after] if after is not None else []))
    handles = [(outs[2 * n + g], outs[3 * n + g], outs[g], outs[n + g]) for g in range(n)]
    return handles, outs[-1]


def _exchange_wait(name, handles, after, scatter):
    n = len(handles)
    srcs, lands, sends, recvs = ([h[i] for h in handles] for i in range(4))

    def body(*refs):
        src_refs, land_refs = refs[:n], refs[n:2 * n]
        send, recv = refs[2 * n:3 * n], refs[3 * n:4 * n]
        for g in range(n):
            for cp in _exchange_copies(src_refs[g], land_refs[g], send[g], recv[g], scatter):
                cp.wait_send()
                cp.wait_recv()
            _own_copy(src_refs[g], land_refs[g], send[g], scatter).wait()

    outs = pl.pallas_call(
        body, name=name,
        out_shape=tuple(pltpu.HBM(a.shape, a.dtype) for a in srcs + lands),
        in_specs=tuple([_HBM] * (2 * n) + [_SEM] * (2 * n) + [_ANY]), out_specs=tuple([_HBM] * (2 * n)),
        input_output_aliases={i: i for i in range(2 * n)},
        compiler_params=pltpu.CompilerParams(has_side_effects=_EFFECT),
    )(*srcs, *lands, *sends, *recvs, after)
    return list(outs[n:])


def _rows(a):
    return a.reshape(-1, 1024)


def _rows_to_parts(full):
    n = full.shape[-2] // N_DEV
    t = full.reshape(full.shape[:-2] + (N_DEV, n, full.shape[-1]))
    return jnp.moveaxis(t, -3, 0)


def _parts_to_rows(parts):
    t = jnp.moveaxis(parts, 0, -3)
    return t.reshape(t.shape[:-3] + (t.shape[-3] * t.shape[-2], t.shape[-1]))


def _parts_to_cols(parts):
    t = jnp.moveaxis(parts, 0, -2)
    return t.reshape(t.shape[:-2] + (t.shape[-2] * t.shape[-1],))


def _join(parts, axis=0):
    total = sum(p.shape[axis] for p in parts)
    out, off = None, 0
    for p in parts:
        cfg = [(0, 0)] * p.ndim
        cfg[axis] = (off, total - off - p.shape[axis])
        t = jnp.pad(p, cfg)
        out = t if out is None else out + t
        off += p.shape[axis]
    return out


def _w_in_to_layout(w):
    tail = jnp.pad(w[4096:4104], ((0, PW - COL_TAIL - 8), (0, 0)))
    return jnp.concatenate([w[:4096], w[4104:P_IN], tail], axis=0)


def _w_in_from_layout(g):
    return _join([g[:4096], g[COL_TAIL:COL_TAIL + 8], g[4096:COL_TAIL]], axis=0)


def _block_diag(w):
    w = w.reshape(4, 2, 64, 64)
    return jnp.pad(w[:, 0], ((0, 0), (0, 64), (0, 64))) + jnp.pad(w[:, 1], ((0, 0), (64, 0), (64, 0)))


def _block_diag_grad(g):
    return jnp.stack([g[:, :64, :64], g[:, 64:, 64:]], axis=1).reshape(8, 64, 64)


def _ffn_forward(tag, x, norm, wg, wu, wd):
    h = _rms_fwd(tag + "_norm", x, norm)
    sa, ds, act = _ffn_up(tag + "_up", h, wg, wu)
    if callable(wd):
        wd = wd(act)
    x_out = _mm(tag + "_down", [(act, wd)], "nn", F32, res=x, scale=0.5)
    return x_out, (x, h, sa, ds, act), wd


def _ffn_backward(tag, dx_out, saved, norm, wg, wu, wd, put, names, split=False):
    x, h, sa, ds, act = saved
    n_wg, n_wu, n_wd = names
    dwd = _mm(tag + "_dwd", [(act, dx_out)], "tn", BF16, scale=0.5, bm=FF // 2)
    tok = put({n_wd: dwd}) if split else None
    da, db = _ffn_dact(tag + "_dact", dx_out, wd, sa, ds, after=tok)
    dwg = _mm(tag + "_dwg", [(da, h)], "tn", BF16, bm=FF // 2)
    if split:
        tok = tok + put({n_wg: dwg})
    dwu = _mm(tag + "_dwu", [(db, h)], "tn", BF16, bm=FF // 2, after=tok)
    tok = tok + put({n_wu: dwu}) if split else put({n_wg: dwg, n_wu: dwu, n_wd: dwd})
    dh = _mm(tag + "_dh", [(da, wg), (db, wu)], "nn", F32, after=tok)
    dx, dnorm = _rms_bwd(tag + "_dnorm", x, norm + tok, dh, dx_out)
    return dx, dnorm


def _mixer_params(p):
    alog = jnp.pad(p["gdn_a_log"], (4, 120))[None]
    dtb = jnp.pad(p["gdn_dt_bias"], (4, 120))[None]
    bias = jnp.repeat(p["sgu_b"].T, 128, axis=1)
    return dict(
        ln_g=p["sgu_ln_g"][None], ln_b=p["sgu_ln_b"][None], sgu_w=p["sgu_w"], sgu_bias=bias,
        lru_cw=p["lru_conv_w"], lru_cb=p["lru_conv_b"][None], wa=_block_diag(p["lru_wa"]), ba=p["lru_ba"][None],
        wx=_block_diag(p["lru_wx"]), bx=p["lru_bx"][None], lam=p["lru_lambda"][None],
        gdn_cw=p["gdn_conv_w"], alog=alog, dtb=dtb, ng=p["gdn_norm_g"][None],
        pool_w=p["pool_w"], pool_sc=p["pool_scale"][None])


def _mix_forward(tag, x, p, mp):
    h = _rms_fwd(tag + "_norm", x, p["mix_norm"][None])
    proj = _mm(tag + "_proj", [(h, p["w_in"])], "nt", BF16, bm=_pick(x.shape[0], (2048, 1024, 512, 256, 128)))
    y_a = _sgu_fwd(tag + "_sgu", proj, mp["ln_g"], mp["ln_b"], mp["sgu_w"], mp["sgu_bias"])
    y_b, hc = _lru_fwd(tag + "_lru", proj, mp["lru_cw"], mp["lru_cb"], mp["wa"], mp["ba"], mp["wx"], mp["bx"],
                       mp["lam"])
    qa = _conv_fwd(tag + "_convq", proj, COL_CQ, mp["gdn_cw"], 0)
    ka = _conv_fwd(tag + "_convk", proj, COL_CK, mp["gdn_cw"], 512)
    va = _conv_fwd(tag + "_convv", proj, COL_CV, mp["gdn_cw"], 1024)
    prep = _gdn_prep_fwd(tag + "_gdnprep", qa, ka, va, proj, mp["alog"], mp["dtb"])
    y_c, sh = _gdn_fwd(tag + "_gdn", *prep, proj, mp["ng"])
    y_d = _pool_fwd(tag + "_pool", proj, mp["pool_w"], mp["pool_sc"])
    ys = (y_a, y_b, y_c, y_d)
    if callable(p["w_branch"]):
        p["w_branch"] = p["w_branch"](y_d)
    merged = _merge_fwd(tag + "_merge", ys, p["w_branch"], proj)
    if callable(p["w_out"]):
        p["w_out"] = p["w_out"](merged)
    x_out = _mm(tag + "_out", [(merged, p["w_out"])], "nn", F32, res=x)
    return x_out, (x, h, proj, hc, qa, ka, va, prep, sh, ys, merged)


def _mix_backward(tag, dx_out, saved, p, mp, put):
    x, h, proj, hc, qa, ka, va, prep, sh, ys, merged = saved
    T = x.shape[0]
    g = {}
    dmerged = _mm(tag + "_dmerged", [(dx_out, p["w_out"])], "nt", F32)
    g["w_out"] = _mm(tag + "_dwout", [(merged, dx_out)], "tn", BF16)
    outs = _merge_bwd(tag + "_dmerge", dmerged, ys, p["w_branch"], proj)
    dgates, dbrs, dys = outs[:NBR], outs[NBR:2 * NBR], outs[2 * NBR:]
    g["w_branch"] = jnp.stack([_mm(f"{tag}_dwb{i}", [(dbrs[i], ys[i])], "tn", BF16) for i in range(NBR)])

    du, dv, dln_g, dln_b, dsgu_w, dbias = _sgu_bwd(tag + "_dsgu", proj, dys[0], mp["ln_g"], mp["ln_b"], mp["sgu_w"],
                                                  mp["sgu_bias"])
    g["sgu_ln_g"], g["sgu_ln_b"], g["sgu_w"] = dln_g[0], dln_b[0], dsgu_w
    g["sgu_b"] = dbias.reshape(128, 4, 128).sum(axis=2).T

    (dbx, dbg, dcw, dcb, dwa, dba, dwx, dbxb, dlam) = _lru_bwd(
        tag + "_dlru", proj, dys[1], hc, mp["lru_cw"], mp["lru_cb"], mp["wa"], mp["ba"], mp["wx"], mp["bx"], mp["lam"])
    g["lru_conv_w"], g["lru_conv_b"], g["lru_ba"], g["lru_bx"], g["lru_lambda"] = dcw, dcb[0], dba[0], dbxb[0], dlam[0]
    g["lru_wa"], g["lru_wx"] = _block_diag_grad(dwa), _block_diag_grad(dwx)

    *dprep, dz, dng = _gdn_bwd(tag + "_dgdn", *prep, proj, dys[2], sh, mp["ng"])
    dqa, dka, dva, dtail, dalog, ddtb = _gdn_prep_bwd(tag + "_dgdnprep", qa, ka, va, proj, mp["alog"], mp["dtb"], *dprep)
    g["gdn_a_log"], g["gdn_dt_bias"], g["gdn_norm_g"] = dalog[0, 4:8], ddtb[0, 4:8], dng[0]
    dq, dcwq = _conv_bwd(tag + "_dconvq", proj, COL_CQ, dqa, mp["gdn_cw"], 0)
    dk, dcwk = _conv_bwd(tag + "_dconvk", proj, COL_CK, dka, mp["gdn_cw"], 512)
    dv_, dcwv = _conv_bwd(tag + "_dconvv", proj, COL_CV, dva, mp["gdn_cw"], 1024)
    g["gdn_conv_w"] = jnp.concatenate([dcwq, dcwk, dcwv], axis=1)

    dd, dpw, dsc = _pool_bwd(tag + "_dpool", proj, dys[3], mp["pool_w"], mp["pool_sc"])
    g["pool_w"], g["pool_scale"] = dpw, dsc[0]

    dproj = jnp.concatenate([du, dv, dbx, dbg, dq, dk, dv_, dz, dd, *dgates, dtail,
                             jnp.zeros((T, PW - COL_TAIL - 128), BF16)], axis=1)
    dw_in = _mm(tag + "_dwin", [(dproj, h)], "tn", BF16)
    tok = put(dict(w_in=_w_in_from_layout(dw_in), w_branch=g.pop("w_branch"), w_out=g.pop("w_out")))
    dh = _mm(tag + "_dh", [(dproj, p["w_in"])], "nn", F32, bm=_pick(T, (2048, 1024, 512, 256, 128)), after=tok)
    dx, dnorm = _rms_bwd(tag + "_dnorm", x, p["mix_norm"][None] + tok, dh, dx_out)
    g["mix_norm"] = dnorm[0]
    return dx, g


_BIG = ("ff1_wg", "ff1_wu", "ff1_wd", "w_in", "w_branch", "w_out", "ff2_wg", "ff2_wu", "ff2_wd")
_COL_SHARDED = ("ff1_wg", "ff1_wu", "w_in", "w_branch", "ff2_wg", "ff2_wu")
_SMALL = ("ff1_norm", "mix_norm", "sgu_ln_g", "sgu_ln_b", "sgu_w", "sgu_b", "lru_conv_w", "lru_conv_b", "lru_wa",
          "lru_ba", "lru_wx", "lru_bx", "lru_lambda", "gdn_conv_w", "gdn_a_log", "gdn_dt_bias", "gdn_norm_g", "pool_w",
          "pool_scale", "ff2_norm", "final_norm")
_WEIGHTS = ("ff1_norm", "ff1_wg", "ff1_wu", "ff1_wd", "mix_norm", "w_in", "sgu_ln_g", "sgu_ln_b", "sgu_w", "sgu_b",
            "lru_conv_w", "lru_conv_b", "lru_wa", "lru_ba", "lru_wx", "lru_bx", "lru_lambda", "gdn_conv_w", "gdn_a_log",
            "gdn_dt_bias", "gdn_norm_g", "pool_w", "pool_scale", "w_branch", "w_out", "ff2_norm", "ff2_wg", "ff2_wu",
            "ff2_wd", "final_norm")
_CONV_SHARDED = ("lru_conv_w", "gdn_conv_w")
PACK_ROW_ALIGN = 16
_GROUPS = (("ff1", ("ff1_wg", "ff1_wu", "ff1_wd")), ("mix", ("w_in", "w_branch", "w_out")),
           ("ff2", ("ff2_wg", "ff2_wu", "ff2_wd")))


def _pad_rows(a, mult):
    pad = (-a.shape[-2]) % mult
    if pad == 0:
        return a
    return jnp.pad(a, [(0, 0)] * (a.ndim - 2) + [(0, pad), (0, 0)])


def _my_index():
    return 4 * lax.axis_index("x") + 2 * lax.axis_index("y") + lax.axis_index("c")


def _landing(shape, dtype):
    return lax.empty((N_DEV,) + tuple(shape), dtype)


def _stored(n, a):
    return jnp.swapaxes(a, -1, -2) if n in _COL_SHARDED else a


_FIRST = ("ff1_wg", "ff1_wu", "ff1_wd")


def _gather_first(w):
    names = _FIRST
    shards = [_rows(_stored(n, w[n][0]).astype(BF16)) for n in names]
    got = _all_gather("gather_first", jnp.concatenate(shards, axis=0))
    out, r = {}, 0
    for n, s in zip(names, shards):
        out[n] = got[:, r:r + s.shape[0]].reshape(-1, 1024)
        r += s.shape[0]
    return out, got


def _gather_start(w, after):
    conv = _pad_rows(jnp.concatenate([w[n].reshape(1, -1) for n in _CONV_SHARDED], axis=1), 8)
    keys, srcs = ["conv"], [conv]
    for l in range(2):
        for sub, (_, names) in enumerate(_GROUPS):
            for n in names:
                if l > 0 or n not in _FIRST:
                    keys.append((l, sub, n))
                    srcs.append(_stored(n, w[n][l]).astype(BF16))
    lands = [_landing(s.shape, s.dtype) for s in srcs]
    handles, token = _exchange_start("gather_start", srcs, lands, scatter=False, after=after)
    return dict(zip(keys, handles)), token


def _gather_finish(l, sub, handles, first, after):
    names = _GROUPS[sub][1]
    if (l, sub) == (0, 0):
        out = dict(first)
        for n in names:
            if n not in _FIRST:
                out[n] = lambda later, n=n: _parts_to_rows(
                    _exchange_wait(f"gather_wait_00_{n}", [handles[(0, 0, n)]], later, scatter=False)[0])
    elif sub == 1:
        out = {n: (lambda later, n=n: _parts_to_rows(
            _exchange_wait(f"gather_wait_{l}{sub}_{n}", [handles[(l, sub, n)]], later, scatter=False)[0])) for n in names}
        out["w_in"] = out["w_in"](after)
    else:
        lands = _exchange_wait(f"gather_wait_{l}{sub}", [handles[(l, sub, n)] for n in names], after, scatter=False)
        out = {n: _parts_to_rows(land) for n, land in zip(names, lands)}
    if "w_in" in out:
        out["w_in"] = _w_in_to_layout(out["w_in"])
    return out


def _scatter_start(l, sub, grads):
    srcs, shapes = [], []
    for n in grads:
        parts = _rows_to_parts(grads[n])
        shapes.append(parts.shape[1:])
        srcs.append(_pad_rows(parts.reshape(N_DEV, -1, 1024), PACK_ROW_ALIGN))
    lands = [_landing(s.shape[1:], s.dtype) for s in srcs]
    tag = f"{l}{sub}" + ("" if len(grads) == len(_GROUPS[sub][1]) else "_" + "_".join(grads))
    handles, token = _exchange_start(f"scatter_start_{tag}", srcs, lands, scatter=True)
    return handles, (tag, tuple(grads), shapes), token


def _scatter_finish(l, sub, handles, meta, after):
    tag, names, shapes = meta
    lands = _exchange_wait(f"scatter_wait_{tag}", handles, after, scatter=True)
    out = {}
    for n, land, shape in zip(names, lands, shapes):
        size = 1
        for s in shape:
            size *= s
        summed = _sum8(f"sum_{l}{sub}_{n}", land)
        out[n] = _stored(n, summed[:size // 1024].reshape(shape))
    return out


def _gather_conv_finish(w, handles, after):
    gconv = _exchange_wait("gather_wait_conv", [handles["conv"]], after, scatter=False)[0][:, 0]
    full, r = {}, 0
    for n in _CONV_SHARDED:
        sz = w[n].size
        full[n] = _parts_to_cols(gconv[:, r:r + sz].reshape((N_DEV,) + w[n].shape))
        r += sz
    return full


def _forward_backward(x, tgt, w, conv, get_weights, put_grads, put_small, token):
    saved, params = [], []
    for l in range(2):
        p = {n: w[n][l] for n in _SMALL if n != "final_norm"}
        for n in _CONV_SHARDED:
            p[n] = conv[n][l]
        mp = _mixer_params(p)
        tok = token[:1, :1] if l == 0 else 0.0
        p.update(get_weights(l, 0, x))
        x, s1, p["ff1_wd"] = _ffn_forward(f"l{l}_ff1", x, p["ff1_norm"][None] + tok, p["ff1_wg"], p["ff1_wu"],
                                          p["ff1_wd"])
        p.update(get_weights(l, 1, x))
        x, s2 = _mix_forward(f"l{l}_mix", x, p, mp)
        p.update(get_weights(l, 2, x))
        x, s3, _ = _ffn_forward(f"l{l}_ff2", x, p["ff2_norm"][None], p["ff2_wg"], p["ff2_wu"], p["ff2_wd"])
        saved.append((s1, s2, s3))
        params.append((p, mp))
    loss, dx, dfinal = _final_loss("loss_head", x, w["final_norm"][None], tgt)
    tok = 0.0
    for l in (1, 0):
        p, mp = params[l]
        s1, s2, s3 = saved[l]
        g = {}

        def put(sub):
            return lambda grads, l=l: put_grads(l, sub, grads)[:1, :1]

        dx, dn = _ffn_backward(f"l{l}_ff2", dx, s3, p["ff2_norm"][None] + tok, p["ff2_wg"], p["ff2_wu"], p["ff2_wd"],
                               put(2), _GROUPS[2][1])
        g["ff2_norm"] = dn[0]
        dx, gm = _mix_backward(f"l{l}_mix", dx, s2, p, mp, put(1))
        g.update(gm)
        tok = 0.0
        if l == 0:
            keep = {n: g.pop(n) for n in ("gdn_a_log", "gdn_dt_bias")}
            tok = put_small("0a", g, True)[:1, :1]
            g = keep
        dx, dn = _ffn_backward(f"l{l}_ff1", dx, s1, p["ff1_norm"][None] + tok, p["ff1_wg"], p["ff1_wu"], p["ff1_wd"],
                               put(0), _GROUPS[0][1], split=(l == 0))
        g["ff1_norm"] = dn[0]
        if l == 1:
            g["final_norm"] = dfinal[0]
            g["loss"] = loss[0, :1]
        tok = put_small("1" if l == 1 else "0b", g, False)[:1, :1]
    return dx


SMALL_PIECE = 8 * 1024


def _pack_small(d, names):
    pieces = []
    for n in names:
        flat = d[n].reshape(-1)
        pieces.append(jnp.pad(flat, (0, (-flat.size) % SMALL_PIECE)).reshape(-1, 1024))
    return jnp.concatenate(pieces, axis=0)


def _unpack_small(pack, shapes, names):
    out, r = {}, 0
    for n in names:
        size = 1
        for s in shapes[n]:
            size *= s
        rows = -(-size // SMALL_PIECE) * 8
        out[n] = pack[r:r + rows].reshape(-1)[:size].reshape(shapes[n])
        r += rows
    return out


def _small_names(grads):
    return tuple(n for n in _SMALL + ("loss",) if n in grads)


def _small_start(tag, grads, narrow):
    pack = _pack_small(grads, _small_names(grads))
    if narrow:
        pack = _pad_rows(pack.astype(BF16), PACK_ROW_ALIGN)
    handles, token = _exchange_start(f"small_start_{tag}", [pack], [_landing(pack.shape, pack.dtype)], scatter=False)
    return handles, {n: grads[n].shape for n in _small_names(grads)}, token


def _small_finish(tag, handles, shapes, after):
    landed = _exchange_wait(f"small_wait_{tag}", handles, after, scatter=False)[0]
    return _unpack_small(_sum8(f"sum_small_{tag}", landed), shapes, _small_names(shapes))


def _as2d(a):
    if a.ndim == 1:
        return a.reshape(1, -1)
    return a.reshape(-1, a.shape[-1])


def kernel(x, ff1_norm, ff1_wg, ff1_wu, ff1_wd, mix_norm, w_in, sgu_ln_g, sgu_ln_b, sgu_w, sgu_b, lru_conv_w, lru_conv_b, lru_wa, lru_ba, lru_wx, lru_bx, lru_lambda, gdn_conv_w, gdn_a_log, gdn_dt_bias, gdn_norm_g, pool_w, pool_scale, w_branch, w_out, ff2_norm, ff2_wg, ff2_wu, ff2_wd, final_norm, loss_target, m_ff1_norm, m_ff1_wg, m_ff1_wu, m_ff1_wd, m_mix_norm, m_w_in, m_sgu_ln_g, m_sgu_ln_b, m_sgu_w, m_sgu_b, m_lru_conv_w, m_lru_conv_b, m_lru_wa, m_lru_ba, m_lru_wx, m_lru_bx, m_lru_lambda, m_gdn_conv_w, m_gdn_a_log, m_gdn_dt_bias, m_gdn_norm_g, m_pool_w, m_pool_scale, m_w_branch, m_w_out, m_ff2_norm, m_ff2_wg, m_ff2_wu, m_ff2_wd, m_final_norm, v_ff1_norm, v_ff1_wg, v_ff1_wu, v_ff1_wd, v_mix_norm, v_w_in, v_sgu_ln_g, v_sgu_ln_b, v_sgu_w, v_sgu_b, v_lru_conv_w, v_lru_conv_b, v_lru_wa, v_lru_ba, v_lru_wx, v_lru_bx, v_lru_lambda, v_gdn_conv_w, v_gdn_a_log, v_gdn_dt_bias, v_gdn_norm_g, v_pool_w, v_pool_scale, v_w_branch, v_w_out, v_ff2_norm, v_ff2_wg, v_ff2_wu, v_ff2_wd, v_final_norm):
    w = dict(ff1_norm=ff1_norm, ff1_wg=ff1_wg, ff1_wu=ff1_wu, ff1_wd=ff1_wd, mix_norm=mix_norm, w_in=w_in,
             sgu_ln_g=sgu_ln_g, sgu_ln_b=sgu_ln_b, sgu_w=sgu_w, sgu_b=sgu_b, lru_conv_w=lru_conv_w,
             lru_conv_b=lru_conv_b, lru_wa=lru_wa, lru_ba=lru_ba, lru_wx=lru_wx, lru_bx=lru_bx, lru_lambda=lru_lambda,
             gdn_conv_w=gdn_conv_w, gdn_a_log=gdn_a_log, gdn_dt_bias=gdn_dt_bias, gdn_norm_g=gdn_norm_g, pool_w=pool_w,
             pool_scale=pool_scale, w_branch=w_branch, w_out=w_out, ff2_norm=ff2_norm, ff2_wg=ff2_wg, ff2_wu=ff2_wu,
             ff2_wd=ff2_wd, final_norm=final_norm)
    m = dict(ff1_norm=m_ff1_norm, ff1_wg=m_ff1_wg, ff1_wu=m_ff1_wu, ff1_wd=m_ff1_wd, mix_norm=m_mix_norm, w_in=m_w_in,
             sgu_ln_g=m_sgu_ln_g, sgu_ln_b=m_sgu_ln_b, sgu_w=m_sgu_w, sgu_b=m_sgu_b, lru_conv_w=m_lru_conv_w,
             lru_conv_b=m_lru_conv_b, lru_wa=m_lru_wa, lru_ba=m_lru_ba, lru_wx=m_lru_wx, lru_bx=m_lru_bx,
             lru_lambda=m_lru_lambda, gdn_conv_w=m_gdn_conv_w, gdn_a_log=m_gdn_a_log, gdn_dt_bias=m_gdn_dt_bias,
             gdn_norm_g=m_gdn_norm_g, pool_w=m_pool_w, pool_scale=m_pool_scale, w_branch=m_w_branch, w_out=m_w_out,
             ff2_norm=m_ff2_norm, ff2_wg=m_ff2_wg, ff2_wu=m_ff2_wu, ff2_wd=m_ff2_wd, final_norm=m_final_norm)
    v = dict(ff1_norm=v_ff1_norm, ff1_wg=v_ff1_wg, ff1_wu=v_ff1_wu, ff1_wd=v_ff1_wd, mix_norm=v_mix_norm, w_in=v_w_in,
             sgu_ln_g=v_sgu_ln_g, sgu_ln_b=v_sgu_ln_b, sgu_w=v_sgu_w, sgu_b=v_sgu_b, lru_conv_w=v_lru_conv_w,
             lru_conv_b=v_lru_conv_b, lru_wa=v_lru_wa, lru_ba=v_lru_ba, lru_wx=v_lru_wx, lru_bx=v_lru_bx,
             lru_lambda=v_lru_lambda, gdn_conv_w=v_gdn_conv_w, gdn_a_log=v_gdn_a_log, gdn_dt_bias=v_gdn_dt_bias,
             gdn_norm_g=v_gdn_norm_g, pool_w=v_pool_w, pool_scale=v_pool_scale, w_branch=v_w_branch, w_out=v_w_out,
             ff2_norm=v_ff2_norm, ff2_wg=v_ff2_wg, ff2_wu=v_ff2_wu, ff2_wd=v_ff2_wd, final_norm=v_final_norm)

    first, got_first = _gather_first(w)
    handles, token = _gather_start(w, got_first)
    conv = _gather_conv_finish(w, handles, token)
    pending = {}

    def get_weights(l, sub, after):
        return _gather_finish(l, sub, handles, first, after)

    def put_grads(l, sub, grads):
        hs, meta, tok = _scatter_start(l, sub, grads)
        pending[(l, sub, meta[0])] = (hs, meta)
        return tok

    def put_small(tag, grads, narrow):
        hs, shapes, tok = _small_start(tag, grads, narrow)
        pending[tag] = (hs, shapes)
        return tok

    T = x.shape[1]
    dx = _forward_backward(x.reshape(T, D), loss_target.reshape(T, D), w, conv, get_weights, put_grads, put_small,
                           token)
    per = {}
    for key in pending:
        if isinstance(key, tuple):
            per.setdefault(key[:2], {}).update(_scatter_finish(*key[:2], *pending[key], dx))
        else:
            per[key] = _small_finish(key, *pending[key], dx)
    grad = {n: jnp.stack([per[(0, sub)][n], per[(1, sub)][n]]) for sub, (_, names) in enumerate(_GROUPS) for n in names}
    layer0 = {**per["0a"], **per["0b"]}
    small = {n: _join([layer0[n].reshape(-1), per["1"][n].reshape(-1)]).reshape((2,) + layer0[n].shape)
             for n in layer0}
    small["final_norm"] = per["1"]["final_norm"]
    loss = per["1"]["loss"][0]
    me = _my_index()
    for n in _SMALL:
        if n in _CONV_SHARDED:
            width = w[n].shape[-1]
            grad[n] = lax.dynamic_slice_in_dim(small[n], me * width, width, axis=2)
        else:
            grad[n] = small[n]

    delta, new_m, new_v = {}, {}, {}
    for n in _BIG:
        d_, m_, v_ = _adamw("adamw_" + n, _as2d(w[n]), _as2d(grad[n]), _as2d(m[n]), _as2d(v[n]))
        delta[n], new_m[n], new_v[n] = (t.reshape(w[n].shape) for t in (d_, m_, v_))

    outs = _adamw_many("adamw_small", *[[_as2d(t[n]) for n in _SMALL] for t in (w, grad, m, v)])
    for k, dst in enumerate((delta, new_m, new_v)):
        for i, n in enumerate(_SMALL):
            dst[n] = outs[k * len(_SMALL) + i].reshape(w[n].shape)

    return (loss, dx.reshape(x.shape), *[grad[n] for n in _WEIGHTS], *[delta[n] for n in _WEIGHTS],
            *[new_m[n] for n in _WEIGHTS], *[new_v[n] for n in _WEIGHTS])
```

```python
import functools

import jax
import jax.numpy as jnp
from jax import lax
from jax.experimental import pallas as pl
from jax.experimental.pallas import tpu as pltpu

F32 = jnp.float32
BF16 = jnp.bfloat16
HI = lax.Precision.HIGHEST

N_DEV = 8
D = 1024
FF = 2816
BW = 512
NBR = 4
CHUNK = 64
EPS = 1e-6
LRU_C = 8.0
GDN_DK = 128

COL_AU, COL_AV, COL_BX, COL_BG = 0, 512, 1024, 1536
COL_CQ, COL_CK, COL_CV, COL_CZ = 2048, 2560, 3072, 3584
COL_DX, COL_GATE, COL_TAIL = 4096, 4608, 8704
PW = 9216
P_IN = 8712

ADAM_LR, ADAM_B1, ADAM_B2, ADAM_EPS, ADAM_WD, ADAM_STEP = 0.001, 0.9, 0.999, 1e-08, 0.01, 10

VMEM_LIMIT_V7X = 56 * 1024 * 1024

_NN = (((1,), (0,)), ((), ()))
_NT = (((1,), (1,)), ((), ()))
_TN = (((0,), (0,)), ((), ()))


def _cp(*sem):
    return pltpu.CompilerParams(dimension_semantics=tuple(sem), vmem_limit_bytes=VMEM_LIMIT_V7X)


def _dot(a, b, dims=_NN):
    return lax.dot_general(a.astype(BF16), b.astype(BF16), dims, preferred_element_type=F32)


def _dot_hi(a, b, dims=_NN):
    return lax.dot_general(a, b, dims, precision=HI, preferred_element_type=F32)


def _pick(n, cands):
    for c in cands:
        if n % c == 0:
            return c
    return n


@jax.custom_jvp
def _log1p(x):
    u = 1.0 + x
    return jnp.where(u == 1.0, x, x * jnp.log(u) / jnp.where(u == 1.0, 1.0, u - 1.0))


@_log1p.defjvp
def _log1p_jvp(p, t):
    (x,), (dx,) = p, t
    return _log1p(x), dx / (1.0 + x)


@jax.custom_jvp
def _expm1(x):
    u = jnp.exp(x)
    lu = jnp.log(u)
    small = (u == 1.0) | (lu == 0.0)
    return jnp.where(small, x, (u - 1.0) * x / jnp.where(small, 1.0, lu))


@_expm1.defjvp
def _expm1_jvp(p, t):
    (x,), (dx,) = p, t
    return _expm1(x), dx * jnp.exp(x)


def _softplus(x):
    return jnp.maximum(x, 0.0) + _log1p(jnp.exp(-jnp.abs(x)))


def _sigmoid(x):
    return jax.nn.sigmoid(x)


def _silu(x):
    return x * jax.nn.sigmoid(x)


def _gelu(x):
    return jax.nn.gelu(x)


@functools.partial(jax.custom_vjp, nondiff_argnums=(1,))
def _shift(x, s):
    return x if s == 0 else pltpu.roll(x, s, 0)


def _shift_fwd(x, s):
    return _shift(x, s), None


def _shift_bwd(s, _, g):
    n = g.shape[0]
    return (g if s == 0 else pltpu.roll(g, n - s, 0),)


_shift.defvjp(_shift_fwd, _shift_bwd)


def _scan_steps(a, b, reverse):
    n = a.shape[0]
    row = lax.broadcasted_iota(jnp.int32, a.shape, 0)
    k = 1
    while k < n:
        sh = n - k if reverse else k
        m = (row < n - k) if reverse else (row >= k)
        a_s = jnp.where(m, pltpu.roll(a, sh, 0), 1.0)
        b_s = jnp.where(m, pltpu.roll(b, sh, 0), 0.0)
        b = a * b_s + b
        a = a * a_s
        k *= 2
    return b


@jax.custom_vjp
def _scan(a, b):
    return _scan_steps(a, b, False)


def _scan_fwd(a, b):
    h = _scan_steps(a, b, False)
    return h, (a, h)


def _scan_bwd(res, dh):
    a, h = res
    n = a.shape[0]
    row = lax.broadcasted_iota(jnp.int32, a.shape, 0)
    a_next = jnp.where(row < n - 1, pltpu.roll(a, n - 1, 0), 0.0)
    g = _scan_steps(a_next, dh, True)
    h_prev = jnp.where(row >= 1, pltpu.roll(h, 1, 0), 0.0)
    return g * h_prev, g


_scan.defvjp(_scan_fwd, _scan_bwd)


def _mm(name, pairs, mode, out_dtype, *, res=None, scale=1.0, bm=None, bn=None, bk=None, after=None):
    a0, b0 = pairs[0]
    if mode == "nn":
        (M, K), N = a0.shape, b0.shape[1]
    elif mode == "nt":
        (M, K), N = a0.shape, b0.shape[0]
    else:
        (K, M), N = a0.shape, b0.shape[1]
    bm = bm or _pick(M, (1024, 512, 256, 128))
    bn = bn or _pick(N, (1024, 512, 256, 128))
    bk = bk or _pick(K, (1024, 512, 1408, 256, 128))
    nk = K // bk
    npair = len(pairs)
    dims = {"nn": _NN, "nt": _NT, "tn": _TN}[mode]

    def body(*refs):
        ab = refs[:2 * npair]
        pos = 2 * npair
        r_ref = None
        if res is not None:
            r_ref = refs[pos]
            pos += 1
        pos += after is not None
        o_ref = refs[pos]
        part = None
        for p in range(npair):
            d = _dot(ab[2 * p][...], ab[2 * p + 1][...], dims)
            part = d if part is None else part + d

        def finish(acc):
            out = acc if scale == 1.0 else acc * scale
            if r_ref is not None:
                out = out + r_ref[...]
            o_ref[...] = out.astype(out_dtype)

        if nk == 1:
            finish(part)
        else:
            acc_ref = refs[pos + 1]
            k = pl.program_id(2)

            @pl.when(k == 0)
            def _():
                acc_ref[...] = part

            @pl.when(k > 0)
            def _():
                acc_ref[...] += part

            @pl.when(k == nk - 1)
            def _():
                finish(acc_ref[...])

    if mode == "nn":
        a_spec = pl.BlockSpec((bm, bk), lambda i, j, k: (i, k))
        b_spec = pl.BlockSpec((bk, bn), lambda i, j, k: (k, j))
    elif mode == "nt":
        a_spec = pl.BlockSpec((bm, bk), lambda i, j, k: (i, k))
        b_spec = pl.BlockSpec((bn, bk), lambda i, j, k: (j, k))
    else:
        a_spec = pl.BlockSpec((bk, bm), lambda i, j, k: (k, i))
        b_spec = pl.BlockSpec((bk, bn), lambda i, j, k: (k, j))
    o_spec = pl.BlockSpec((bm, bn), lambda i, j, k: (i, j))
    in_specs, args = [], []
    for a, b in pairs:
        in_specs += [a_spec, b_spec]
        args += [a, b]
    if res is not None:
        in_specs.append(o_spec)
        args.append(res)
    if after is not None:
        in_specs.append(_ANY)
        args.append(after)
    return pl.pallas_call(
        body, name=name, grid=(M // bm, N // bn, nk),
        in_specs=in_specs, out_specs=o_spec,
        out_shape=jax.ShapeDtypeStruct((M, N), out_dtype),
        scratch_shapes=[pltpu.VMEM((bm, bn), F32)] if nk > 1 else [],
        compiler_params=_cp("parallel", "parallel", "arbitrary"),
    )(*args)


def _rms_fwd(name, x, g):
    T = x.shape[0]
    bm = _pick(T, (512, 256, 128))

    def body(x_ref, g_ref, o_ref):
        xv = x_ref[...]
        r = lax.rsqrt(jnp.mean(xv * xv, axis=-1, keepdims=True) + EPS)
        o_ref[...] = (xv * r * g_ref[...]).astype(BF16)

    return pl.pallas_call(
        body, name=name, grid=(T // bm,),
        in_specs=[pl.BlockSpec((bm, D), lambda i: (i, 0)), pl.BlockSpec((1, D), lambda i: (0, 0))],
        out_specs=pl.BlockSpec((bm, D), lambda i: (i, 0)),
        out_shape=jax.ShapeDtypeStruct((T, D), BF16),
        compiler_params=_cp("parallel"),
    )(x, g)


def _rms_bwd(name, x, g, dh, dres):
    T = x.shape[0]
    bm = _pick(T, (512, 256, 128))

    def body(x_ref, g_ref, dh_ref, dres_ref, dx_ref, dg_ref):
        xv = x_ref[...]
        r = lax.rsqrt(jnp.mean(xv * xv, axis=-1, keepdims=True) + EPS)
        xh = xv * r
        dhv = dh_ref[...]
        dxh = dhv * g_ref[...]
        dx_ref[...] = dres_ref[...] + r * (dxh - xh * jnp.mean(dxh * xh, axis=-1, keepdims=True))
        part = jnp.sum(dhv * xh, axis=0, keepdims=True)

        @pl.when(pl.program_id(0) == 0)
        def _():
            dg_ref[...] = part

        @pl.when(pl.program_id(0) > 0)
        def _():
            dg_ref[...] += part

    row = pl.BlockSpec((bm, D), lambda i: (i, 0))
    vec = pl.BlockSpec((1, D), lambda i: (0, 0))
    return pl.pallas_call(
        body, name=name, grid=(T // bm,),
        in_specs=[row, vec, row, row], out_specs=[row, vec],
        out_shape=[jax.ShapeDtypeStruct((T, D), F32), jax.ShapeDtypeStruct((1, D), F32)],
        compiler_params=_cp("arbitrary"),
    )(x, g, dh, dres)


def _final_loss(name, x, g, tgt):
    T = x.shape[0]
    bm = _pick(T, (512, 256, 128))

    def body(x_ref, g_ref, t_ref, loss_ref, dx_ref, dg_ref):
        xv = x_ref[...]
        gv = g_ref[...]
        r = lax.rsqrt(jnp.mean(xv * xv, axis=-1, keepdims=True) + EPS)
        xh = xv * r
        e = xh * gv - t_ref[...]
        lpart = jnp.broadcast_to(0.5 * jnp.sum(jnp.mean(e * e, axis=-1, keepdims=True), axis=0, keepdims=True), (1, 128))
        dy = e * (1.0 / D)
        dxh = dy * gv
        dx_ref[...] = r * (dxh - xh * jnp.mean(dxh * xh, axis=-1, keepdims=True))
        gpart = jnp.sum(dy * xh, axis=0, keepdims=True)

        @pl.when(pl.program_id(0) == 0)
        def _():
            loss_ref[...] = lpart
            dg_ref[...] = gpart

        @pl.when(pl.program_id(0) > 0)
        def _():
            loss_ref[...] += lpart
            dg_ref[...] += gpart

    row = pl.BlockSpec((bm, D), lambda i: (i, 0))
    vec = pl.BlockSpec((1, D), lambda i: (0, 0))
    return pl.pallas_call(
        body, name=name, grid=(T // bm,),
        in_specs=[row, vec, row],
        out_specs=[pl.BlockSpec((1, 128), lambda i: (0, 0)), row, vec],
        out_shape=[jax.ShapeDtypeStruct((1, 128), F32), jax.ShapeDtypeStruct((T, D), F32),
                   jax.ShapeDtypeStruct((1, D), F32)],
        compiler_params=_cp("arbitrary"),
    )(x, g, tgt)


def _ffn_up(name, h, wg, wu):
    T = h.shape[0]
    bm = _pick(T, (2048, 1024, 512, 256, 128))
    bn = 256

    def body(h_ref, wg_ref, wu_ref, sa_ref, ds_ref, act_ref):
        hv = h_ref[...]
        a = _dot(hv, wg_ref[...], _NT)
        b = _dot(hv, wu_ref[...], _NT)
        s = _sigmoid(a)
        sa = a * s
        sa_ref[...] = sa.astype(BF16)
        ds_ref[...] = (b * (s * (1.0 + a * (1.0 - s)))).astype(BF16)
        act_ref[...] = (sa * b).astype(BF16)

    w_spec = pl.BlockSpec((bn, D), lambda i, j: (j, 0))
    o_spec = pl.BlockSpec((bm, bn), lambda i, j: (i, j))
    return pl.pallas_call(
        body, name=name, grid=(T // bm, FF // bn),
        in_specs=[pl.BlockSpec((bm, D), lambda i, j: (i, 0)), w_spec, w_spec],
        out_specs=[o_spec, o_spec, o_spec],
        out_shape=[jax.ShapeDtypeStruct((T, FF), BF16)] * 3,
        compiler_params=_cp("parallel", "parallel"),
    )(h, wg, wu)


def _ffn_dact(name, dy, wd, sa, ds, after=None):
    T = dy.shape[0]
    bm = _pick(T, (2048, 1024, 512, 256, 128))
    bn = 256

    def body(dy_ref, wd_ref, sa_ref, ds_ref, *rest):
        da_ref, db_ref, dy_bf = rest[-3:]

        @pl.when(pl.program_id(1) == 0)
        def _():
            dy_bf[...] = dy_ref[...].astype(BF16)

        dact = 0.5 * _dot(dy_bf[...], wd_ref[...], _NT)
        da_ref[...] = (dact * ds_ref[...].astype(F32)).astype(BF16)
        db_ref[...] = (dact * sa_ref[...].astype(F32)).astype(BF16)

    t_spec = pl.BlockSpec((bm, bn), lambda i, j: (i, j))
    return pl.pallas_call(
        body, name=name, grid=(T // bm, FF // bn),
        in_specs=[pl.BlockSpec((bm, D), lambda i, j: (i, 0)), pl.BlockSpec((bn, D), lambda i, j: (j, 0)),
                  t_spec, t_spec] + [_ANY] * (after is not None),
        out_specs=[t_spec, t_spec],
        out_shape=[jax.ShapeDtypeStruct((T, FF), BF16), jax.ShapeDtypeStruct((T, FF), BF16)],
        scratch_shapes=[pltpu.VMEM((bm, D), BF16)],
        compiler_params=_cp("parallel", "arbitrary"),
    )(dy, wd, sa, ds, *([after] if after is not None else []))


def _merge_specs(T, bm, bn):
    y_spec = pl.BlockSpec((bm, BW), lambda i, j: (i, 0))
    wb_spec = pl.BlockSpec((NBR, bn, BW), lambda i, j: (0, j, 0))
    gate_specs = [pl.BlockSpec((bm, bn), functools.partial(lambda i, j, o: (i, o + j), o=(COL_GATE + g * D) // bn))
                  for g in range(NBR)]
    t_spec = pl.BlockSpec((bm, bn), lambda i, j: (i, j))
    return y_spec, wb_spec, gate_specs, t_spec


def _merge_fwd(name, ys, wb, proj):
    T = proj.shape[0]
    bm = _pick(T, (512, 256, 128))
    bn = 512
    y_spec, wb_spec, gate_specs, t_spec = _merge_specs(T, bm, bn)

    def body(y0, y1, y2, y3, wb_ref, g0, g1, g2, g3, o_ref):
        acc = None
        for g, (y_ref, g_ref) in enumerate(((y0, g0), (y1, g1), (y2, g2), (y3, g3))):
            t = _sigmoid(g_ref[...].astype(F32)) * _dot(y_ref[...], wb_ref[g], _NT)
            acc = t if acc is None else acc + t
        o_ref[...] = acc.astype(BF16)

    return pl.pallas_call(
        body, name=name, grid=(T // bm, D // bn),
        in_specs=[y_spec] * NBR + [wb_spec] + gate_specs, out_specs=t_spec,
        out_shape=jax.ShapeDtypeStruct((T, D), BF16),
        compiler_params=_cp("parallel", "parallel"),
    )(*ys, wb, proj, proj, proj, proj)


def _merge_bwd(name, dm, ys, wb, proj):
    T = proj.shape[0]
    bm = _pick(T, (512, 256, 128))
    bn = 512
    y_spec, wb_spec, gate_specs, t_spec = _merge_specs(T, bm, bn)

    def body(dm_ref, y0, y1, y2, y3, wb_ref, g0, g1, g2, g3, *outs):
        dmv = dm_ref[...]
        j = pl.program_id(1)
        for g, (y_ref, g_ref) in enumerate(((y0, g0), (y1, g1), (y2, g2), (y3, g3))):
            br = _dot(y_ref[...], wb_ref[g], _NT)
            s = _sigmoid(g_ref[...].astype(F32))
            outs[g][...] = (dmv * br * (s * (1.0 - s))).astype(BF16)
            dbr = (dmv * s).astype(BF16)
            outs[NBR + g][...] = dbr
            part = _dot(dbr, wb_ref[g])
            dy_ref = outs[2 * NBR + g]

            @pl.when(j == 0)
            def _():
                dy_ref[...] = part

            @pl.when(j > 0)
            def _():
                dy_ref[...] += part

    return pl.pallas_call(
        body, name=name, grid=(T // bm, D // bn),
        in_specs=[t_spec] + [y_spec] * NBR + [wb_spec] + gate_specs, out_specs=[t_spec] * (2 * NBR) + [y_spec] * NBR,
        out_shape=[jax.ShapeDtypeStruct((T, D), BF16)] * (2 * NBR) + [jax.ShapeDtypeStruct((T, BW), F32)] * NBR,
        compiler_params=_cp("parallel", "arbitrary"),
    )(dm, *ys, wb, proj, proj, proj, proj)


def _dwb(name, dbrs, ys):
    T = ys[0].shape[0]
    bk = _pick(T, (1024, 512, 256, 128))
    nk = T // bk

    def body(*refs):
        d_refs, y_refs, o_ref, acc = refs[:NBR], refs[NBR:2 * NBR], refs[2 * NBR], refs[2 * NBR + 1]
        k = pl.program_id(0)
        for g in range(NBR):
            part = _dot(d_refs[g][...], y_refs[g][...], _TN)

            @pl.when(k == 0)
            def _(g=g, part=part):
                acc[g] = part

            @pl.when(k > 0)
            def _(g=g, part=part):
                acc[g] += part

        @pl.when(k == nk - 1)
        def _():
            o_ref[...] = acc[...].astype(BF16)

    return pl.pallas_call(
        body, name=name, grid=(nk,),
        in_specs=[pl.BlockSpec((bk, D), lambda k: (k, 0))] * NBR + [pl.BlockSpec((bk, BW), lambda k: (k, 0))] * NBR,
        out_specs=pl.BlockSpec((NBR, D, BW), lambda k: (0, 0, 0)),
        out_shape=jax.ShapeDtypeStruct((NBR, D, BW), BF16),
        scratch_shapes=[pltpu.VMEM((NBR, D, BW), F32)],
        compiler_params=_cp("arbitrary"),
    )(*dbrs, *ys)


def _sgu_block(u_pre, v_pre, ln_g, ln_b, w, bias):
    u = _gelu(u_pre)
    vf = _gelu(v_pre)
    mu = jnp.mean(vf, axis=-1, keepdims=True)
    var = jnp.mean(jnp.square(vf - mu), axis=-1, keepdims=True)
    vn = (vf - mu) * lax.rsqrt(var + EPS) * ln_g + ln_b
    ri = lax.broadcasted_iota(jnp.int32, (128, 128), 0)
    ci = lax.broadcasted_iota(jnp.int32, (128, 128), 1)
    mask = (ri // CHUNK) >= (ci // CHUNK)
    outs = [_dot(jnp.where(mask, w[g], 0.0), vn[:, g * 128:(g + 1) * 128]) for g in range(4)]
    mixed = jnp.concatenate(outs, axis=1) + bias
    return u * mixed


def _sgu_param_specs():
    return [pl.BlockSpec((1, BW), lambda i: (0, 0)), pl.BlockSpec((1, BW), lambda i: (0, 0)),
            pl.BlockSpec((4, 128, 128), lambda i: (0, 0, 0)), pl.BlockSpec((128, BW), lambda i: (0, 0))]


def _sgu_fwd(name, proj, ln_g, ln_b, w, bias):
    T = proj.shape[0]
    rb = _pick(T, (256, 128))

    def body(u_ref, v_ref, g_ref, b_ref, w_ref, bias_ref, y_ref):
        for n in range(rb // 128):
            rows = slice(n * 128, (n + 1) * 128)
            y = _sgu_block(u_ref[rows, :].astype(F32), v_ref[rows, :].astype(F32), g_ref[...], b_ref[...], w_ref[...],
                           bias_ref[...])
            y_ref[rows, :] = y.astype(BF16)

    return pl.pallas_call(
        body, name=name, grid=(T // rb,),
        in_specs=[pl.BlockSpec((rb, BW), lambda i: (i, COL_AU // BW)), pl.BlockSpec((rb, BW), lambda i: (i, COL_AV // BW))]
        + _sgu_param_specs(),
        out_specs=pl.BlockSpec((rb, BW), lambda i: (i, 0)),
        out_shape=jax.ShapeDtypeStruct((T, BW), BF16),
        compiler_params=_cp("parallel"),
    )(proj, proj, ln_g, ln_b, w, bias)


def _sgu_bwd(name, proj, dy, ln_g, ln_b, w, bias):
    T = proj.shape[0]
    rb = _pick(T, (256, 128))

    def body(u_ref, v_ref, dy_ref, g_ref, b_ref, w_ref, bias_ref, du_ref, dv_ref, dg_ref, db_ref, dw_ref, dbias_ref):
        acc = None
        for n in range(rb // 128):
            rows = slice(n * 128, (n + 1) * 128)
            _, vjp = jax.vjp(_sgu_block, u_ref[rows, :].astype(F32), v_ref[rows, :].astype(F32), g_ref[...], b_ref[...],
                             w_ref[...],
                             bias_ref[...])
            du, dv, *dp = vjp(dy_ref[rows, :])
            du_ref[rows, :] = du.astype(BF16)
            dv_ref[rows, :] = dv.astype(BF16)
            acc = dp if acc is None else [p + q for p, q in zip(acc, dp)]

        @pl.when(pl.program_id(0) == 0)
        def _():
            for r, p in zip((dg_ref, db_ref, dw_ref, dbias_ref), acc):
                r[...] = p

        @pl.when(pl.program_id(0) > 0)
        def _():
            for r, p in zip((dg_ref, db_ref, dw_ref, dbias_ref), acc):
                r[...] += p

    row = pl.BlockSpec((rb, BW), lambda i: (i, 0))
    return pl.pallas_call(
        body, name=name, grid=(T // rb,),
        in_specs=[pl.BlockSpec((rb, BW), lambda i: (i, COL_AU // BW)), pl.BlockSpec((rb, BW), lambda i: (i, COL_AV // BW)),
                  row] + _sgu_param_specs(),
        out_specs=[row, row] + _sgu_param_specs(),
        out_shape=[jax.ShapeDtypeStruct((T, BW), BF16), jax.ShapeDtypeStruct((T, BW), BF16),
                   jax.ShapeDtypeStruct((1, BW), F32), jax.ShapeDtypeStruct((1, BW), F32),
                   jax.ShapeDtypeStruct((4, 128, 128), F32), jax.ShapeDtypeStruct((128, BW), F32)],
        compiler_params=_cp("arbitrary"),
    )(proj, proj, dy, ln_g, ln_b, w, bias)


def _halo_block(ref, i, rblk, halo):
    r0 = pl.multiple_of(i * rblk, rblk)
    h0 = pl.multiple_of(jnp.maximum(r0 - 16, 0), 16)
    top = jnp.where(i > 0, ref[pl.ds(h0, 16), :].astype(F32), 0.0)[16 - halo:]
    return jnp.concatenate([top, ref[pl.ds(r0, rblk), :].astype(F32)], axis=0)


def _with_halo_grad(dfull, pending, halo, rblk):
    tail = jnp.concatenate([jnp.zeros((rblk - halo, 128), F32), pending], axis=0)
    return dfull[halo:] + tail


def _conv4(xfull, rows):
    acc = None
    for k in range(4):
        t = rows[k] * _shift(xfull, 3 - k)[8:]
        acc = t if acc is None else acc + t
    return acc


def _lru_block(xfull, gate, h0, c0, c1, c2, c3, cb, wa, ba, wx, bx, lam):
    n = gate.shape[0]
    xc = _conv4(xfull, (c0, c1, c2, c3)) + cb
    r = _sigmoid(_dot(xc, wa) + ba)
    ig = _sigmoid(_dot(xc, wx) + bx)
    log_a = -LRU_C * r * _softplus(-lam)
    a = jnp.exp(log_a)
    mult = jnp.sqrt(-_expm1(2.0 * log_a))
    b = mult * (ig * xc)
    row = lax.broadcasted_iota(jnp.int32, (n, 128), 0)
    b = b + jnp.where(row == 0, a * h0, 0.0)
    h = _scan(a, b)
    out = h * _gelu(gate)
    h_last = jnp.sum(jnp.where(row == n - 1, h, 0.0), axis=0, keepdims=True)
    return out, h_last


def _lru_param_specs():
    vec = pl.BlockSpec((1, 128), lambda g: (0, g))
    mat = pl.BlockSpec((None, 128, 128), lambda g: (g, 0, 0))
    return [pl.BlockSpec((4, 128), lambda g: (0, g)), vec, mat, vec, mat, vec, vec]


def _lru_load_params(cw_ref, cb_ref, wa_ref, ba_ref, wx_ref, bx_ref, lam_ref):
    return (cw_ref[0:1, :], cw_ref[1:2, :], cw_ref[2:3, :], cw_ref[3:4, :], cb_ref[...], wa_ref[...], ba_ref[...],
            wx_ref[...], bx_ref[...], lam_ref[...])


def _lru_fwd(name, proj, cw, cb, wa, ba, wx, bx, lam):
    T = proj.shape[0]
    rblk = _pick(T, (256, 128))
    nblk = T // rblk

    def body(x_ref, gt_ref, cw_ref, cb_ref, wa_ref, ba_ref, wx_ref, bx_ref, lam_ref, y_ref, hc_ref):
        params = _lru_load_params(cw_ref, cb_ref, wa_ref, ba_ref, wx_ref, bx_ref, lam_ref)

        def step(i, h0):
            r0 = pl.multiple_of(i * rblk, rblk)
            out, h_last = _lru_block(_halo_block(x_ref, i, rblk, 8), gt_ref[pl.ds(r0, rblk), :].astype(F32), h0,
                                     *params)
            y_ref[pl.ds(r0, rblk), :] = out.astype(BF16)
            hc_ref[pl.ds(pl.multiple_of(i * 8, 8), 8), :] = jnp.broadcast_to(h0, (8, 128))
            return h_last

        lax.fori_loop(0, nblk, step, jnp.zeros((1, 128), F32))

    return pl.pallas_call(
        body, name=name, grid=(4,),
        in_specs=[pl.BlockSpec((T, 128), lambda g: (0, COL_BX // 128 + g)),
                  pl.BlockSpec((T, 128), lambda g: (0, COL_BG // 128 + g))] + _lru_param_specs(),
        out_specs=[pl.BlockSpec((T, 128), lambda g: (0, g)), pl.BlockSpec((nblk * 8, 128), lambda g: (0, g))],
        out_shape=[jax.ShapeDtypeStruct((T, BW), BF16), jax.ShapeDtypeStruct((nblk * 8, BW), F32)],
        compiler_params=_cp("parallel"),
    )(proj, proj, cw, cb, wa, ba, wx, bx, lam)


def _lru_bwd(name, proj, dy, hc, cw, cb, wa, ba, wx, bx, lam):
    T = proj.shape[0]
    rblk = _pick(T, (256, 128))
    nblk = T // rblk

    def body(x_ref, gt_ref, dy_ref, hc_ref, cw_ref, cb_ref, wa_ref, ba_ref, wx_ref, bx_ref, lam_ref,
             dx_ref, dgt_ref, dcw_ref, dcb_ref, dwa_ref, dba_ref, dwx_ref, dbx_ref, dlam_ref):
        params = _lru_load_params(cw_ref, cb_ref, wa_ref, ba_ref, wx_ref, bx_ref, lam_ref)

        def step(it, carry):
            dh_last, pending, acc = carry
            i = nblk - 1 - it
            r0 = pl.multiple_of(i * rblk, rblk)
            h0 = hc_ref[pl.ds(pl.multiple_of(i * 8, 8), 1), :]
            _, vjp = jax.vjp(_lru_block, _halo_block(x_ref, i, rblk, 8), gt_ref[pl.ds(r0, rblk), :].astype(F32), h0,
                             *params)
            dfull, dgate, dh0, *dp = vjp((dy_ref[pl.ds(r0, rblk), :], dh_last))
            dx_ref[pl.ds(r0, rblk), :] = _with_halo_grad(dfull, pending, 8, rblk).astype(BF16)
            dgt_ref[pl.ds(r0, rblk), :] = dgate.astype(BF16)
            return dh0, dfull[:8], tuple(p + q for p, q in zip(acc, dp))

        zeros = tuple(jnp.zeros(p.shape, F32) for p in params)
        _, _, acc = lax.fori_loop(0, nblk, step, (jnp.zeros((1, 128), F32), jnp.zeros((8, 128), F32), zeros))
        for k in range(4):
            dcw_ref[k:k + 1, :] = acc[k]
        for r, p in zip((dcb_ref, dwa_ref, dba_ref, dwx_ref, dbx_ref, dlam_ref), acc[4:]):
            r[...] = p

    col = pl.BlockSpec((T, 128), lambda g: (0, g))
    return pl.pallas_call(
        body, name=name, grid=(4,),
        in_specs=[pl.BlockSpec((T, 128), lambda g: (0, COL_BX // 128 + g)),
                  pl.BlockSpec((T, 128), lambda g: (0, COL_BG // 128 + g)), col,
                  pl.BlockSpec((nblk * 8, 128), lambda g: (0, g))] + _lru_param_specs(),
        out_specs=[col, col] + _lru_param_specs(),
        out_shape=[jax.ShapeDtypeStruct((T, BW), BF16), jax.ShapeDtypeStruct((T, BW), BF16),
                   jax.ShapeDtypeStruct((4, BW), F32), jax.ShapeDtypeStruct((1, BW), F32),
                   jax.ShapeDtypeStruct((4, 128, 128), F32), jax.ShapeDtypeStruct((1, BW), F32),
                   jax.ShapeDtypeStruct((4, 128, 128), F32), jax.ShapeDtypeStruct((1, BW), F32),
                   jax.ShapeDtypeStruct((1, BW), F32)],
        compiler_params=_cp("parallel"),
    )(proj, proj, dy, hc, cw, cb, wa, ba, wx, bx, lam)


def _conv_block(xfull, c0, c1, c2, c3):
    return _silu(_conv4(xfull, (c0, c1, c2, c3)))


def _conv_fwd(name, proj, col0, cw, cw_col0):
    T = proj.shape[0]
    rblk = _pick(T, (256, 128))
    nblk = T // rblk

    def body(x_ref, cw_ref, y_ref):
        rows = (cw_ref[0:1, :], cw_ref[1:2, :], cw_ref[2:3, :], cw_ref[3:4, :])

        def step(i, c):
            r0 = pl.multiple_of(i * rblk, rblk)
            y_ref[pl.ds(r0, rblk), :] = _conv_block(_halo_block(x_ref, i, rblk, 8), *rows)
            return c

        lax.fori_loop(0, nblk, step, 0)

    return pl.pallas_call(
        body, name=name, grid=(4,),
        in_specs=[pl.BlockSpec((T, 128), lambda g: (0, col0 // 128 + g)),
                  pl.BlockSpec((4, 128), lambda g: (0, cw_col0 // 128 + g))],
        out_specs=pl.BlockSpec((T, 128), lambda g: (0, g)),
        out_shape=jax.ShapeDtypeStruct((T, BW), F32),
        compiler_params=_cp("parallel"),
    )(proj, cw)


def _conv_bwd(name, proj, col0, dy, cw, cw_col0):
    T = proj.shape[0]
    rblk = _pick(T, (256, 128))
    nblk = T // rblk

    def body(x_ref, dy_ref, cw_ref, dx_ref, dcw_ref):
        rows = (cw_ref[0:1, :], cw_ref[1:2, :], cw_ref[2:3, :], cw_ref[3:4, :])

        def step(it, carry):
            pending, acc = carry
            i = nblk - 1 - it
            r0 = pl.multiple_of(i * rblk, rblk)
            _, vjp = jax.vjp(_conv_block, _halo_block(x_ref, i, rblk, 8), *rows)
            dfull, *dp = vjp(dy_ref[pl.ds(r0, rblk), :])
            dx_ref[pl.ds(r0, rblk), :] = _with_halo_grad(dfull, pending, 8, rblk).astype(BF16)
            return dfull[:8], tuple(p + q for p, q in zip(acc, dp))

        zeros = tuple(jnp.zeros((1, 128), F32) for _ in range(4))
        _, acc = lax.fori_loop(0, nblk, step, (jnp.zeros((8, 128), F32), zeros))
        for k in range(4):
            dcw_ref[k:k + 1, :] = acc[k]

    col = pl.BlockSpec((T, 128), lambda g: (0, g))
    return pl.pallas_call(
        body, name=name, grid=(4,),
        in_specs=[pl.BlockSpec((T, 128), lambda g: (0, col0 // 128 + g)), col,
                  pl.BlockSpec((4, 128), lambda g: (0, cw_col0 // 128 + g))],
        out_specs=[col, pl.BlockSpec((4, 128), lambda g: (0, g))],
        out_shape=[jax.ShapeDtypeStruct((T, BW), BF16), jax.ShapeDtypeStruct((4, BW), F32)],
        compiler_params=_cp("parallel"),
    )(proj, dy, cw)


def _pool_block(xfull, pw, sc, t0, gi):
    n = xfull.shape[0] - 16
    s2 = xfull + _shift(xfull, 1)
    s4 = s2 + _shift(s2, 2)
    s8 = s4 + _shift(s4, 4)
    s16 = s8 + _shift(s8, 8)
    s = jnp.where(gi == 0, s2, jnp.where(gi == 1, s4, jnp.where(gi == 2, s8, s16)))[16:]
    t = t0 + lax.broadcasted_iota(jnp.int32, (n, 128), 0)
    cnt = jnp.minimum(t + 1, lax.shift_left(jnp.int32(2), gi)).astype(F32)
    pooled = s / cnt - xfull[16:]
    return _dot(pooled, pw) * sc


def _pool_fwd(name, proj, pw, sc):
    T = proj.shape[0]
    rblk = _pick(T, (256, 128))
    nblk = T // rblk

    def body(x_ref, pw_ref, sc_ref, y_ref):
        gi = pl.program_id(0)

        def step(i, c):
            r0 = pl.multiple_of(i * rblk, rblk)
            y = _pool_block(_halo_block(x_ref, i, rblk, 16), pw_ref[...], sc_ref[...], r0, gi)
            y_ref[pl.ds(r0, rblk), :] = y.astype(BF16)
            return c

        lax.fori_loop(0, nblk, step, 0)

    return pl.pallas_call(
        body, name=name, grid=(4,),
        in_specs=[pl.BlockSpec((T, 128), lambda g: (0, COL_DX // 128 + g)),
                  pl.BlockSpec((None, 128, 128), lambda g: (g, 0, 0)), pl.BlockSpec((1, 128), lambda g: (0, g))],
        out_specs=pl.BlockSpec((T, 128), lambda g: (0, g)),
        out_shape=jax.ShapeDtypeStruct((T, BW), BF16),
        compiler_params=_cp("parallel"),
    )(proj, pw, sc)


def _pool_bwd(name, proj, dy, pw, sc):
    T = proj.shape[0]
    rblk = _pick(T, (256, 128))
    nblk = T // rblk

    def body(x_ref, dy_ref, pw_ref, sc_ref, dx_ref, dpw_ref, dsc_ref):
        gi = pl.program_id(0)

        def step(it, carry):
            pending, apw, asc = carry
            i = nblk - 1 - it
            r0 = pl.multiple_of(i * rblk, rblk)
            _, vjp = jax.vjp(lambda xf, w, s: _pool_block(xf, w, s, r0, gi), _halo_block(x_ref, i, rblk, 16),
                             pw_ref[...], sc_ref[...])
            dfull, dw, ds = vjp(dy_ref[pl.ds(r0, rblk), :])
            dx_ref[pl.ds(r0, rblk), :] = _with_halo_grad(dfull, pending, 16, rblk).astype(BF16)
            return dfull[:16], apw + dw, asc + ds

        _, apw, asc = lax.fori_loop(0, nblk, step, (jnp.zeros((16, 128), F32), jnp.zeros((128, 128), F32),
                                                    jnp.zeros((1, 128), F32)))
        dpw_ref[...] = apw
        dsc_ref[...] = asc

    col = pl.BlockSpec((T, 128), lambda g: (0, g))
    mat = pl.BlockSpec((None, 128, 128), lambda g: (g, 0, 0))
    vec = pl.BlockSpec((1, 128), lambda g: (0, g))
    return pl.pallas_call(
        body, name=name, grid=(4,),
        in_specs=[pl.BlockSpec((T, 128), lambda g: (0, COL_DX // 128 + g)), col, mat, vec],
        out_specs=[col, mat, vec],
        out_shape=[jax.ShapeDtypeStruct((T, BW), BF16), jax.ShapeDtypeStruct((4, 128, 128), F32),
                   jax.ShapeDtypeStruct((1, BW), F32)],
        compiler_params=_cp("parallel"),
    )(proj, dy, pw, sc)


@jax.custom_vjp
def _dot3(a, b):
    ah = a.astype(BF16)
    al = (a - ah.astype(F32)).astype(BF16)
    bh = b.astype(BF16)
    bl = (b - bh.astype(F32)).astype(BF16)

    def d(x, y):
        return lax.dot_general(x, y, _NN, preferred_element_type=F32)

    return d(ah, bh) + (d(ah, bl) + d(al, bh))


def _dot3_fwd(a, b):
    return _dot3(a, b), (a, b)


def _dot3_bwd(res, g):
    a, b = res
    return _dot(g, b, _NT), _dot(a, g, _TN)


_dot3.defvjp(_dot3_fwd, _dot3_bwd)


def _pad_rows2(x):
    return jnp.concatenate([x, jnp.zeros_like(x)], axis=0)


@jax.custom_vjp
def _tri_inv(mats):
    n = mats[0].shape[0]
    eye = (lax.broadcasted_iota(jnp.int32, (n, n), 0) == lax.broadcasted_iota(jnp.int32, (n, n), 1)).astype(F32)
    ps = [eye - a for a in mats]
    ms = list(mats)
    k = 2
    while k < n:
        ms = [_dot3(t, t) for t in ms]
        ps = [p + _dot3(p, t) for p, t in zip(ps, ms)]
        k *= 2
    return ps


def _tri_inv_fwd(mats):
    ts = _tri_inv(mats)
    return ts, ts


def _tri_inv_bwd(ts, gs):
    half = [_dot(t, g, _TN) for t, g in zip(ts, gs)]
    return ([-_dot(h, t, _NT) for h, t in zip(half, ts)],)


_tri_inv.defvjp(_tri_inv_fwd, _tri_inv_bwd)


def _cumsum_rows(x):
    n = x.shape[0]
    row = lax.broadcasted_iota(jnp.int32, x.shape, 0)
    k = 1
    while k < n:
        x = x + jnp.where(row >= k, _shift(x, k), 0.0)
        k *= 2
    return x


def _gdn_prep(qcs, kcs, vcs, tails, alog, dtb):
    C = CHUNK
    pairs = [(c, h) for c in range(len(qcs)) for h in range(4)]
    lane = lax.broadcasted_iota(jnp.int32, (C, 128), 1)
    row = lax.broadcasted_iota(jnp.int32, (C, 128), 0)
    incl = row >= lane
    sig = [_sigmoid(t) for t in tails]
    gfull = [-jnp.exp(alog) * _softplus(t + dtb) for t in tails]
    beta = [jnp.sum(jnp.where(lane == h, sig[c], 0.0), axis=1, keepdims=True) for c, h in pairs]
    g = [jnp.sum(jnp.where(lane == h + 4, gfull[c], 0.0), axis=1, keepdims=True) for c, h in pairs]
    qs = [qcs[c][:, h * 128:(h + 1) * 128] for c, h in pairs]
    ks = [kcs[c][:, h * 128:(h + 1) * 128] for c, h in pairs]
    vs = [vcs[c][:, h * 128:(h + 1) * 128] for c, h in pairs]
    q = [t * lax.rsqrt(jnp.sum(t * t, axis=-1, keepdims=True) + EPS) * (GDN_DK ** -0.5) for t in qs]
    k = [t * lax.rsqrt(jnp.sum(t * t, axis=-1, keepdims=True) + EPS) for t in ks]
    gc = [_cumsum_rows(jnp.broadcast_to(t, (C, 128))) for t in g]
    gc_t = [jnp.transpose(jnp.concatenate([t, t], axis=0)) for t in gc]
    gc_col = [jnp.sum(jnp.where(lane == 0, t, 0.0), axis=1, keepdims=True) for t in gc]
    ri = lax.broadcasted_iota(jnp.int32, (C, C), 0)
    ci = lax.broadcasted_iota(jnp.int32, (C, C), 1)
    decay = [jnp.exp(jnp.where(incl, a - b[:C, :], -1e30)) for a, b in zip(gc, gc_t)]
    decay_sq = [jnp.exp(jnp.where(ri > ci, a - jnp.transpose(b)[:C, :], -1e30)) for a, b in zip(gc_col, gc)]
    kb = [a * b for a, b in zip(k, beta)]
    kk = [_dot(a, b, _NT) for a, b in zip(kb, k)]
    t_mat = _tri_inv([jnp.where(ri > ci, a * b, 0.0) for a, b in zip(kk, decay_sq)])
    egc = [jnp.exp(t) for t in gc]
    u = [_dot(t, a * b) for t, a, b in zip(t_mat, vs, beta)]
    w = [_dot(t, a * b) for t, a, b in zip(t_mat, kb, egc)]
    qk = [_dot(a, _pad_rows2(b), _NT) for a, b in zip(q, k)]
    attn = [jnp.where(incl, a * b, 0.0) for a, b in zip(qk, decay)]
    g_last = [jnp.sum(jnp.where(row == C - 1, t, 0.0), axis=0, keepdims=True) for t in gc]
    qe = [a * b for a, b in zip(q, egc)]
    kd = [a * jnp.exp(b - c_) for a, b, c_ in zip(k, g_last, gc)]
    egl = [jnp.exp(t) for t in g_last]

    def per_chunk(vals):
        return [jnp.concatenate(vals[4 * c:4 * c + 4], axis=1) for c in range(len(qcs))]

    return tuple(per_chunk(t) for t in (u, w, qe, kd, attn, egl))


def _gdn_scan_chunk(states, u, w, qe, kd, attn, egl, z, ng):
    hs = range(4)

    def sl(t, h):
        return t[:, h * 128:(h + 1) * 128]

    ws = [_dot(sl(w, h), states[h]) for h in hs]
    qs = [_dot(sl(qe, h), states[h]) for h in hs]
    v_new = [sl(u, h) - ws[h] for h in hs]
    av = [_dot(sl(attn, h), _pad_rows2(v_new[h])) for h in hs]
    kv = [_dot(sl(kd, h), v_new[h], _TN) for h in hs]
    nxt = tuple(states[h] * sl(egl, h) + kv[h] for h in hs)
    o = [qs[h] + av[h] for h in hs]
    on = [t * lax.rsqrt(jnp.mean(t * t, axis=-1, keepdims=True) + EPS) * ng for t in o]
    return nxt, jnp.concatenate(on, axis=1) * _silu(z)


def _gdn_blocks(T):
    tb = _pick(T, (512, 256, 128, 64))
    return tb, T // tb, tb // CHUNK


PREP_CHUNKS = 4


def _chunk_rows(i, n):
    return [pl.ds(pl.multiple_of((i * n + j) * CHUNK, CHUNK), CHUNK) for j in range(n)]


def _egl_rows(i, n, size):
    return [pl.ds(pl.multiple_of((i * n + j) * 8, 8), size) for j in range(n)]


def _gdn_prep_fwd(name, qa, ka, va, proj, alog, dtb):
    T = proj.shape[0]
    tb, nb, ncb = _gdn_blocks(T)
    n = PREP_CHUNKS if ncb % PREP_CHUNKS == 0 else 1

    def body(q_ref, k_ref, v_ref, tail_ref, alog_ref, dtb_ref, u_ref, w_ref, qe_ref, kd_ref, at_ref, egl_ref):
        def step(i, c):
            rows = _chunk_rows(i, n)
            u, w, qe, kd, at, egl = _gdn_prep([q_ref[r, :] for r in rows], [k_ref[r, :] for r in rows],
                                              [v_ref[r, :] for r in rows], [tail_ref[r, :].astype(F32) for r in rows],
                                              alog_ref[...], dtb_ref[...])
            for j, (r, e) in enumerate(zip(rows, _egl_rows(i, n, 8))):
                u_ref[r, :] = u[j]
                w_ref[r, :] = w[j].astype(BF16)
                qe_ref[r, :] = qe[j].astype(BF16)
                kd_ref[r, :] = kd[j].astype(BF16)
                at_ref[r, :] = at[j].astype(BF16)
                egl_ref[e, :] = jnp.broadcast_to(egl[j], (8, BW))
            return c

        lax.fori_loop(0, ncb // n, step, 0)

    blk = pl.BlockSpec((tb, BW), lambda j: (j, 0))
    vec = pl.BlockSpec((1, 128), lambda j: (0, 0))
    return pl.pallas_call(
        body, name=name, grid=(nb,),
        in_specs=[blk, blk, blk, pl.BlockSpec((tb, 128), lambda j: (j, COL_TAIL // 128)), vec, vec],
        out_specs=[blk] * 5 + [pl.BlockSpec((ncb * 8, BW), lambda j: (j, 0))],
        out_shape=[jax.ShapeDtypeStruct((T, BW), F32)] + [jax.ShapeDtypeStruct((T, BW), BF16)] * 4
        + [jax.ShapeDtypeStruct((T // 8, BW), F32)],
        compiler_params=_cp("parallel"),
    )(qa, ka, va, proj, alog, dtb)


def _gdn_prep_bwd(name, qa, ka, va, proj, alog, dtb, du, dw, dqe, dkd, dat, degl):
    T = proj.shape[0]
    tb, nb, ncb = _gdn_blocks(T)
    n = PREP_CHUNKS if ncb % PREP_CHUNKS == 0 else 1

    def body(q_ref, k_ref, v_ref, tail_ref, alog_ref, dtb_ref, du_ref, dw_ref, dqe_ref, dkd_ref, dat_ref, degl_ref,
             dq_ref, dk_ref, dv_ref, dtail_ref, dalog_ref, ddtb_ref):
        first = pl.program_id(0) == 0

        def step(i, carry):
            pa, pd = carry
            rows = _chunk_rows(i, n)
            _, vjp = jax.vjp(_gdn_prep, [q_ref[r, :] for r in rows], [k_ref[r, :] for r in rows],
                             [v_ref[r, :] for r in rows], [tail_ref[r, :].astype(F32) for r in rows], alog_ref[...], dtb_ref[...])
            cot = tuple([ref[r, :] for r in rows] for ref in (du_ref, dw_ref, dqe_ref, dkd_ref, dat_ref))
            dq, dk, dv, dtail, da, dd = vjp(cot + ([degl_ref[e, :] for e in _egl_rows(i, n, 1)],))
            for j, r in enumerate(rows):
                dq_ref[r, :] = dq[j]
                dk_ref[r, :] = dk[j]
                dv_ref[r, :] = dv[j]
                dtail_ref[r, :] = dtail[j].astype(BF16)
            return pa + da, pd + dd

        zv = jnp.zeros((1, 128), F32)
        pa, pd = lax.fori_loop(0, ncb // n, step, (zv, zv))

        @pl.when(first)
        def _():
            dalog_ref[...] = pa
            ddtb_ref[...] = pd

        @pl.when(jnp.logical_not(first))
        def _():
            dalog_ref[...] += pa
            ddtb_ref[...] += pd

    blk = pl.BlockSpec((tb, BW), lambda j: (j, 0))
    vec = pl.BlockSpec((1, 128), lambda j: (0, 0))
    return pl.pallas_call(
        body, name=name, grid=(nb,),
        in_specs=[blk, blk, blk, pl.BlockSpec((tb, 128), lambda j: (j, COL_TAIL // 128)), vec, vec]
        + [blk] * 5 + [pl.BlockSpec((ncb * 8, BW), lambda j: (j, 0))],
        out_specs=[blk, blk, blk, pl.BlockSpec((tb, 128), lambda j: (j, 0)), vec, vec],
        out_shape=[jax.ShapeDtypeStruct((T, BW), F32)] * 3 + [jax.ShapeDtypeStruct((T, 128), BF16)]
        + [jax.ShapeDtypeStruct((1, 128), F32)] * 2,
        compiler_params=_cp("arbitrary"),
    )(qa, ka, va, proj, alog, dtb, du, dw, dqe, dkd, dat, degl)


def _gdn_fwd(name, u, w, qe, kd, at, egl, proj, ng):
    T = proj.shape[0]
    tb, nb, ncb = _gdn_blocks(T)

    def body(u_ref, w_ref, qe_ref, kd_ref, at_ref, egl_ref, z_ref, ng_ref, y_ref, sh_ref, state):
        @pl.when(pl.program_id(0) == 0)
        def _():
            state[...] = jnp.zeros((4, 128, 128), F32)

        def step(c, states):
            rows = pl.ds(pl.multiple_of(c * CHUNK, CHUNK), CHUNK)
            for h in range(4):
                sh_ref[h, c] = states[h]
            nxt, y = _gdn_scan_chunk(states, u_ref[rows, :], w_ref[rows, :], qe_ref[rows, :], kd_ref[rows, :],
                                     at_ref[rows, :], egl_ref[pl.ds(pl.multiple_of(c * 8, 8), 1), :],
                                     z_ref[rows, :].astype(F32),
                                     ng_ref[...])
            y_ref[rows, :] = y.astype(BF16)
            return nxt

        states = lax.fori_loop(0, ncb, step, tuple(state[h] for h in range(4)))
        for h in range(4):
            state[h] = states[h]

    blk = pl.BlockSpec((tb, BW), lambda j: (j, 0))
    vec = pl.BlockSpec((1, 128), lambda j: (0, 0))
    return pl.pallas_call(
        body, name=name, grid=(nb,),
        in_specs=[blk] * 5 + [pl.BlockSpec((ncb * 8, BW), lambda j: (j, 0)),
                              pl.BlockSpec((tb, BW), lambda j: (j, COL_CZ // BW)), vec],
        out_specs=[blk, pl.BlockSpec((4, ncb, 128, 128), lambda j: (0, j, 0, 0))],
        out_shape=[jax.ShapeDtypeStruct((T, BW), BF16), jax.ShapeDtypeStruct((4, T // CHUNK, 128, 128), F32)],
        scratch_shapes=[pltpu.VMEM((4, 128, 128), F32)],
        compiler_params=_cp("arbitrary"),
    )(u, w, qe, kd, at, egl, proj, ng)


def _gdn_bwd(name, u, w, qe, kd, at, egl, proj, dy, sh, ng):
    T = proj.shape[0]
    tb, nb, ncb = _gdn_blocks(T)

    def body(u_ref, w_ref, qe_ref, kd_ref, at_ref, egl_ref, z_ref, dy_ref, sh_ref, ng_ref,
             du_ref, dw_ref, dqe_ref, dkd_ref, dat_ref, degl_ref, dz_ref, dng_ref, dstate):
        first = pl.program_id(0) == 0

        @pl.when(first)
        def _():
            dstate[...] = jnp.zeros((4, 128, 128), F32)

        def step(it, carry):
            dstates, pn = carry
            c = ncb - 1 - it
            rows = pl.ds(pl.multiple_of(c * CHUNK, CHUNK), CHUNK)
            erow = pl.multiple_of(c * 8, 8)
            _, vjp = jax.vjp(_gdn_scan_chunk, tuple(sh_ref[h, c] for h in range(4)), u_ref[rows, :],
                             w_ref[rows, :].astype(F32), qe_ref[rows, :].astype(F32), kd_ref[rows, :].astype(F32),
                             at_ref[rows, :].astype(F32), egl_ref[pl.ds(erow, 1), :], z_ref[rows, :].astype(F32),
                             ng_ref[...])
            nxt, du, dw, dqe, dkd, dat, degl, dz, dn = vjp((dstates, dy_ref[rows, :]))
            du_ref[rows, :] = du
            dw_ref[rows, :] = dw
            dqe_ref[rows, :] = dqe
            dkd_ref[rows, :] = dkd
            dat_ref[rows, :] = dat
            degl_ref[pl.ds(erow, 8), :] = jnp.broadcast_to(degl, (8, BW))
            dz_ref[rows, :] = dz.astype(BF16)
            return nxt, pn + dn

        dstates, pn = lax.fori_loop(0, ncb, step, (tuple(dstate[h] for h in range(4)), jnp.zeros((1, 128), F32)))
        for h in range(4):
            dstate[h] = dstates[h]

        @pl.when(first)
        def _():
            dng_ref[...] = pn

        @pl.when(jnp.logical_not(first))
        def _():
            dng_ref[...] += pn

    blk = pl.BlockSpec((tb, BW), lambda j: (nb - 1 - j, 0))
    eblk = pl.BlockSpec((ncb * 8, BW), lambda j: (nb - 1 - j, 0))
    vec = pl.BlockSpec((1, 128), lambda j: (0, 0))
    return pl.pallas_call(
        body, name=name, grid=(nb,),
        in_specs=[blk] * 5 + [eblk, pl.BlockSpec((tb, BW), lambda j: (nb - 1 - j, COL_CZ // BW)), blk,
                              pl.BlockSpec((4, ncb, 128, 128), lambda j: (0, nb - 1 - j, 0, 0)), vec],
        out_specs=[blk] * 5 + [eblk, blk, vec],
        out_shape=[jax.ShapeDtypeStruct((T, BW), F32)] * 5 + [jax.ShapeDtypeStruct((T // 8, BW), F32),
                                                              jax.ShapeDtypeStruct((T, BW), BF16),
                                                              jax.ShapeDtypeStruct((1, 128), F32)],
        scratch_shapes=[pltpu.VMEM((4, 128, 128), F32)],
        compiler_params=_cp("arbitrary"),
    )(u, w, qe, kd, at, egl, proj, dy, sh, ng)


def _adamw_update(w_ref, g_ref, m_ref, v_ref, d_ref, nm_ref, nv_ref):
    gv = g_ref[...]
    m2 = ADAM_B1 * m_ref[...] + (1.0 - ADAM_B1) * gv
    v2 = ADAM_B2 * v_ref[...] + (1.0 - ADAM_B2) * jnp.square(gv)
    m_hat = m2 / (1.0 - ADAM_B1 ** ADAM_STEP)
    v_hat = v2 / (1.0 - ADAM_B2 ** ADAM_STEP)
    d_ref[...] = -ADAM_LR * (m_hat / (jnp.sqrt(v_hat) + ADAM_EPS) + ADAM_WD * w_ref[...])
    nm_ref[...] = m2
    nv_ref[...] = v2


def _adamw_many(name, ws, gs, ms, vs):
    n = len(ws)

    def body(*refs):
        for i in range(n):
            _adamw_update(*[refs[k * n + i] for k in range(7)])

    return pl.pallas_call(
        body, name=name,
        out_shape=[jax.ShapeDtypeStruct(a.shape, F32) for a in ws] * 3,
        compiler_params=_cp(),
    )(*ws, *gs, *ms, *vs)


def _adamw(name, w, g, m, v):
    R, C = w.shape
    br = _pick(R, (512, 256, 240, 128, 64, 8))
    body = functools.partial(_adamw_update)
    spec = pl.BlockSpec((br, C), lambda i: (i, 0))
    return pl.pallas_call(
        body, name=name, grid=(R // br,),
        in_specs=[spec] * 4, out_specs=[spec] * 3,
        out_shape=[jax.ShapeDtypeStruct((R, C), F32)] * 3,
        compiler_params=_cp("parallel"),
    )(w, g, m, v)


def _sum8(name, parts):
    _, R, C = parts.shape
    br = _pick(R, (352, 368, 256, 128, 64, 16, 8))

    def body(p_ref, o_ref):
        acc = p_ref[0].astype(F32)
        for d in range(1, N_DEV):
            acc = acc + p_ref[d].astype(F32)
        o_ref[...] = acc

    return pl.pallas_call(
        body, name=name, grid=(R // br,),
        in_specs=[pl.BlockSpec((N_DEV, br, C), lambda i: (0, i, 0))],
        out_specs=pl.BlockSpec((br, C), lambda i: (i, 0)),
        out_shape=jax.ShapeDtypeStruct((R, C), F32),
        compiler_params=_cp("parallel"),
    )(parts)


_ANY = pl.BlockSpec(memory_space=pl.ANY)
_MESH = pl.DeviceIdType.MESH


def _all_gather(name, shard):
    R, C = shard.shape

    def body(x_ref, out_ref, send_sems, recv_sems, local_sem):
        x, y, c = lax.axis_index("x"), lax.axis_index("y"), lax.axis_index("c")
        me, sibling = (x, y, c), (x, y, 1 - c)
        chips = [(1 - x, y), (x, 1 - y), (1 - x, 1 - y)]

        def slot(px, py, pc):
            return out_ref.at[4 * px + 2 * py + pc]

        def copy(k, block, to, src=None):
            return pltpu.make_async_remote_copy(
                src_ref=slot(*block) if src is None else src, dst_ref=slot(*block),
                send_sem=send_sems.at[k], recv_sem=recv_sems.at[k], device_id=to, device_id_type=_MESH)

        mine = pltpu.make_async_copy(x_ref, slot(*me), local_sem)
        mine.start()
        first = [copy(0, me, sibling, src=x_ref)]
        first += [copy(1 + j, me, (*chip, c), src=x_ref) for j, chip in enumerate(chips)]
        for cp in first:
            cp.start()
        passed = [copy(4 + j, (*chip, c), sibling) for j, chip in enumerate(chips)]
        for j, chip in enumerate(chips):
            copy(1 + j, (*chip, c), me).wait_recv()
            passed[j].start()
        copy(0, sibling, me).wait_recv()
        for j, chip in enumerate(chips):
            copy(4 + j, (*chip, 1 - c), me).wait_recv()
        for cp in first + passed:
            cp.wait_send()
        mine.wait()

    return pl.pallas_call(
        body, name=name,
        in_specs=[_ANY], out_specs=_ANY,
        out_shape=jax.ShapeDtypeStruct((N_DEV, R, C), shard.dtype),
        scratch_shapes=[pltpu.SemaphoreType.DMA((7,)), pltpu.SemaphoreType.DMA((7,)), pltpu.SemaphoreType.DMA],
    )(shard)


_HBM = pl.BlockSpec(memory_space=pltpu.HBM)
_SEM = pl.BlockSpec(memory_space=pltpu.SEMAPHORE)
_EFFECT = pltpu.SideEffectType.DATAFLOW_SIDE_EFFECTING


def _exchange_copies(src_ref, land_ref, send_sems, recv_sems, scatter):
    x, y, c = lax.axis_index("x"), lax.axis_index("y"), lax.axis_index("c")
    me = 4 * x + 2 * y + c
    copies = []
    for k in range(1, N_DEV):
        px, py, pc = x ^ ((k >> 2) & 1), y ^ ((k >> 1) & 1), c ^ (k & 1)
        src = src_ref.at[4 * px + 2 * py + pc] if scatter else src_ref
        copies.append(pltpu.make_async_remote_copy(
            src_ref=src, dst_ref=land_ref.at[me], send_sem=send_sems.at[k - 1], recv_sem=recv_sems.at[k - 1],
            device_id=(px, py, pc), device_id_type=_MESH))
    return copies


def _own_copy(src_ref, land_ref, send_sems, scatter):
    me = 4 * lax.axis_index("x") + 2 * lax.axis_index("y") + lax.axis_index("c")
    return pltpu.make_async_copy(src_ref.at[me] if scatter else src_ref, land_ref.at[me], send_sems.at[N_DEV - 1])


def _exchange_start(name, srcs, lands, scatter, after=None):
    n = len(srcs)

    def body(*refs):
        src_refs, land_refs = refs[:n], refs[n:2 * n]
        outs = refs[2 * n + (after is not None):]
        send, recv = outs[:n], outs[n:2 * n]
        token = refs[-1]
        for g in range(n):
            for cp in _exchange_copies(src_refs[g], land_refs[g], send[g], recv[g], scatter):
                cp.start()
            _own_copy(src_refs[g], land_refs[g], send[g], scatter).start()
        token[...] = jnp.zeros_like(token)

    outs = pl.pallas_call(
        body, name=name,
        out_shape=tuple([pltpu.SemaphoreType.DMA((N_DEV,))] * (2 * n)
                        + [pltpu.HBM(a.shape, a.dtype) for a in list(srcs) + list(lands)]
                        + [jax.ShapeDtypeStruct((8, 128), F32)]),
        in_specs=[_HBM] * (2 * n) + [_ANY] * (after is not None),
        out_specs=tuple([_SEM] * (2 * n) + [_HBM] * (2 * n) + [pl.BlockSpec(memory_space=pltpu.VMEM)]),
        input_output_aliases={i: 2 * n + i for i in range(2 * n)},
        compiler_params=pltpu.CompilerParams(has_side_effects=_EFFECT),
    )(*[pltpu.with_memory_space_constraint(a, pltpu.HBM) for a in list(srcs) + list(lands)],
      *([after] if after is not None else []))
    handles = [(outs[2 * n + g], outs[3 * n + g], outs[g], outs[n + g]) for g in range(n)]
    return handles, outs[-1]


def _exchange_wait(name, handles, after, scatter):
    n = len(handles)
    srcs, lands, sends, recvs = ([h[i] for h in handles] for i in range(4))

    def body(*refs):
        src_refs, land_refs = refs[:n], refs[n:2 * n]
        send, recv = refs[2 * n:3 * n], refs[3 * n:4 * n]
        for g in range(n):
            for cp in _exchange_copies(src_refs[g], land_refs[g], send[g], recv[g], scatter):
                cp.wait_send()
                cp.wait_recv()
            _own_copy(src_refs[g], land_refs[g], send[g], scatter).wait()

    outs = pl.pallas_call(
        body, name=name,
        out_shape=tuple(pltpu.HBM(a.shape, a.dtype) for a in srcs + lands),
        in_specs=tuple([_HBM] * (2 * n) + [_SEM] * (2 * n) + [_ANY]), out_specs=tuple([_HBM] * (2 * n)),
        input_output_aliases={i: i for i in range(2 * n)},
        compiler_params=pltpu.CompilerParams(has_side_effects=_EFFECT),
    )(*srcs, *lands, *sends, *recvs, after)
    return list(outs[n:])


def _rows(a):
    return a.reshape(-1, 1024)


def _rows_to_parts(full):
    n = full.shape[-2] // N_DEV
    t = full.reshape(full.shape[:-2] + (N_DEV, n, full.shape[-1]))
    return jnp.moveaxis(t, -3, 0)


def _parts_to_rows(parts):
    t = jnp.moveaxis(parts, 0, -3)
    return t.reshape(t.shape[:-3] + (t.shape[-3] * t.shape[-2], t.shape[-1]))


def _parts_to_cols(parts):
    t = jnp.moveaxis(parts, 0, -2)
    return t.reshape(t.shape[:-2] + (t.shape[-2] * t.shape[-1],))


def _join(parts, axis=0):
    total = sum(p.shape[axis] for p in parts)
    out, off = None, 0
    for p in parts:
        cfg = [(0, 0)] * p.ndim
        cfg[axis] = (off, total - off - p.shape[axis])
        t = jnp.pad(p, cfg)
        out = t if out is None else out + t
        off += p.shape[axis]
    return out


def _w_in_to_layout(w):
    tail = jnp.pad(w[4096:4104], ((0, PW - COL_TAIL - 8), (0, 0)))
    return jnp.concatenate([w[:4096], w[4104:P_IN], tail], axis=0)


def _w_in_from_layout(g):
    return _join([g[:4096], g[COL_TAIL:COL_TAIL + 8], g[4096:COL_TAIL]], axis=0)


def _block_diag(w):
    w = w.reshape(4, 2, 64, 64)
    return jnp.pad(w[:, 0], ((0, 0), (0, 64), (0, 64))) + jnp.pad(w[:, 1], ((0, 0), (64, 0), (64, 0)))


def _block_diag_grad(g):
    return jnp.stack([g[:, :64, :64], g[:, 64:, 64:]], axis=1).reshape(8, 64, 64)


def _ffn_forward(tag, x, norm, wg, wu, wd):
    h = _rms_fwd(tag + "_norm", x, norm)
    sa, ds, act = _ffn_up(tag + "_up", h, wg, wu)
    if callable(wd):
        wd = wd(act)
    x_out = _mm(tag + "_down", [(act, wd)], "nn", F32, res=x, scale=0.5)
    return x_out, (x, h, sa, ds, act), wd


def _ffn_backward(tag, dx_out, saved, norm, wg, wu, wd, put, names, split=False):
    x, h, sa, ds, act = saved
    n_wg, n_wu, n_wd = names
    dwd = _mm(tag + "_dwd", [(act, dx_out)], "tn", BF16, scale=0.5, bm=FF // 2)
    tok = put({n_wd: dwd}) if split else None
    da, db = _ffn_dact(tag + "_dact", dx_out, wd, sa, ds, after=tok)
    dwg = _mm(tag + "_dwg", [(da, h)], "tn", BF16, bm=FF // 2)
    if split:
        tok = tok + put({n_wg: dwg})
    dwu = _mm(tag + "_dwu", [(db, h)], "tn", BF16, bm=FF // 2, after=tok)
    tok = tok + put({n_wu: dwu}) if split else put({n_wg: dwg, n_wu: dwu, n_wd: dwd})
    dh = _mm(tag + "_dh", [(da, wg), (db, wu)], "nn", F32, after=tok)
    dx, dnorm = _rms_bwd(tag + "_dnorm", x, norm + tok, dh, dx_out)
    return dx, dnorm


def _mixer_params(p):
    alog = jnp.pad(p["gdn_a_log"], (4, 120))[None]
    dtb = jnp.pad(p["gdn_dt_bias"], (4, 120))[None]
    bias = jnp.repeat(p["sgu_b"].T, 128, axis=1)
    return dict(
        ln_g=p["sgu_ln_g"][None], ln_b=p["sgu_ln_b"][None], sgu_w=p["sgu_w"], sgu_bias=bias,
        lru_cw=p["lru_conv_w"], lru_cb=p["lru_conv_b"][None], wa=_block_diag(p["lru_wa"]), ba=p["lru_ba"][None],
        wx=_block_diag(p["lru_wx"]), bx=p["lru_bx"][None], lam=p["lru_lambda"][None],
        gdn_cw=p["gdn_conv_w"], alog=alog, dtb=dtb, ng=p["gdn_norm_g"][None],
        pool_w=p["pool_w"], pool_sc=p["pool_scale"][None])


def _mix_forward(tag, x, p, mp):
    h = _rms_fwd(tag + "_norm", x, p["mix_norm"][None])
    proj = _mm(tag + "_proj", [(h, p["w_in"])], "nt", BF16, bm=_pick(x.shape[0], (2048, 1024, 512, 256, 128)))
    y_a = _sgu_fwd(tag + "_sgu", proj, mp["ln_g"], mp["ln_b"], mp["sgu_w"], mp["sgu_bias"])
    y_b, hc = _lru_fwd(tag + "_lru", proj, mp["lru_cw"], mp["lru_cb"], mp["wa"], mp["ba"], mp["wx"], mp["bx"],
                       mp["lam"])
    qa = _conv_fwd(tag + "_convq", proj, COL_CQ, mp["gdn_cw"], 0)
    ka = _conv_fwd(tag + "_convk", proj, COL_CK, mp["gdn_cw"], 512)
    va = _conv_fwd(tag + "_convv", proj, COL_CV, mp["gdn_cw"], 1024)
    prep = _gdn_prep_fwd(tag + "_gdnprep", qa, ka, va, proj, mp["alog"], mp["dtb"])
    y_c, sh = _gdn_fwd(tag + "_gdn", *prep, proj, mp["ng"])
    y_d = _pool_fwd(tag + "_pool", proj, mp["pool_w"], mp["pool_sc"])
    ys = (y_a, y_b, y_c, y_d)
    if callable(p["w_branch"]):
        p["w_branch"] = p["w_branch"](y_d)
    merged = _merge_fwd(tag + "_merge", ys, p["w_branch"], proj)
    if callable(p["w_out"]):
        p["w_out"] = p["w_out"](merged)
    x_out = _mm(tag + "_out", [(merged, p["w_out"])], "nn", F32, res=x)
    return x_out, (x, h, proj, hc, qa, ka, va, prep, sh, ys, merged)


def _mix_backward(tag, dx_out, saved, p, mp, put):
    x, h, proj, hc, qa, ka, va, prep, sh, ys, merged = saved
    T = x.shape[0]
    g = {}
    dmerged = _mm(tag + "_dmerged", [(dx_out, p["w_out"])], "nt", F32)
    g["w_out"] = _mm(tag + "_dwout", [(merged, dx_out)], "tn", BF16)
    outs = _merge_bwd(tag + "_dmerge", dmerged, ys, p["w_branch"], proj)
    dgates, dbrs, dys = outs[:NBR], outs[NBR:2 * NBR], outs[2 * NBR:]
    g["w_branch"] = _dwb(tag + "_dwb", dbrs, ys)

    du, dv, dln_g, dln_b, dsgu_w, dbias = _sgu_bwd(tag + "_dsgu", proj, dys[0], mp["ln_g"], mp["ln_b"], mp["sgu_w"],
                                                  mp["sgu_bias"])
    g["sgu_ln_g"], g["sgu_ln_b"], g["sgu_w"] = dln_g[0], dln_b[0], dsgu_w
    g["sgu_b"] = dbias.reshape(128, 4, 128).sum(axis=2).T

    (dbx, dbg, dcw, dcb, dwa, dba, dwx, dbxb, dlam) = _lru_bwd(
        tag + "_dlru", proj, dys[1], hc, mp["lru_cw"], mp["lru_cb"], mp["wa"], mp["ba"], mp["wx"], mp["bx"], mp["lam"])
    g["lru_conv_w"], g["lru_conv_b"], g["lru_ba"], g["lru_bx"], g["lru_lambda"] = dcw, dcb[0], dba[0], dbxb[0], dlam[0]
    g["lru_wa"], g["lru_wx"] = _block_diag_grad(dwa), _block_diag_grad(dwx)

    *dprep, dz, dng = _gdn_bwd(tag + "_dgdn", *prep, proj, dys[2], sh, mp["ng"])
    dqa, dka, dva, dtail, dalog, ddtb = _gdn_prep_bwd(tag + "_dgdnprep", qa, ka, va, proj, mp["alog"], mp["dtb"], *dprep)
    g["gdn_a_log"], g["gdn_dt_bias"], g["gdn_norm_g"] = dalog[0, 4:8], ddtb[0, 4:8], dng[0]
    dq, dcwq = _conv_bwd(tag + "_dconvq", proj, COL_CQ, dqa, mp["gdn_cw"], 0)
    dk, dcwk = _conv_bwd(tag + "_dconvk", proj, COL_CK, dka, mp["gdn_cw"], 512)
    dv_, dcwv = _conv_bwd(tag + "_dconvv", proj, COL_CV, dva, mp["gdn_cw"], 1024)
    g["gdn_conv_w"] = jnp.concatenate([dcwq, dcwk, dcwv], axis=1)

    dd, dpw, dsc = _pool_bwd(tag + "_dpool", proj, dys[3], mp["pool_w"], mp["pool_sc"])
    g["pool_w"], g["pool_scale"] = dpw, dsc[0]

    dproj = jnp.concatenate([du, dv, dbx, dbg, dq, dk, dv_, dz, dd, *dgates, dtail,
                             jnp.zeros((T, PW - COL_TAIL - 128), BF16)], axis=1)
    dw_in = _mm(tag + "_dwin", [(dproj, h)], "tn", BF16)
    tok = put(dict(w_in=_w_in_from_layout(dw_in), w_branch=g.pop("w_branch"), w_out=g.pop("w_out")))
    dh = _mm(tag + "_dh", [(dproj, p["w_in"])], "nn", F32, bm=_pick(T, (2048, 1024, 512, 256, 128)), after=tok)
    dx, dnorm = _rms_bwd(tag + "_dnorm", x, p["mix_norm"][None] + tok, dh, dx_out)
    g["mix_norm"] = dnorm[0]
    return dx, g


_BIG = ("ff1_wg", "ff1_wu", "ff1_wd", "w_in", "w_branch", "w_out", "ff2_wg", "ff2_wu", "ff2_wd")
_COL_SHARDED = ("ff1_wg", "ff1_wu", "w_in", "w_branch", "ff2_wg", "ff2_wu")
_SMALL = ("ff1_norm", "mix_norm", "sgu_ln_g", "sgu_ln_b", "sgu_w", "sgu_b", "lru_conv_w", "lru_conv_b", "lru_wa",
          "lru_ba", "lru_wx", "lru_bx", "lru_lambda", "gdn_conv_w", "gdn_a_log", "gdn_dt_bias", "gdn_norm_g", "pool_w",
          "pool_scale", "ff2_norm", "final_norm")
_WEIGHTS = ("ff1_norm", "ff1_wg", "ff1_wu", "ff1_wd", "mix_norm", "w_in", "sgu_ln_g", "sgu_ln_b", "sgu_w", "sgu_b",
            "lru_conv_w", "lru_conv_b", "lru_wa", "lru_ba", "lru_wx", "lru_bx", "lru_lambda", "gdn_conv_w", "gdn_a_log",
            "gdn_dt_bias", "gdn_norm_g", "pool_w", "pool_scale", "w_branch", "w_out", "ff2_norm", "ff2_wg", "ff2_wu",
            "ff2_wd", "final_norm")
_CONV_SHARDED = ("lru_conv_w", "gdn_conv_w")
PACK_ROW_ALIGN = 16
_GROUPS = (("ff1", ("ff1_wg", "ff1_wu", "ff1_wd")), ("mix", ("w_in", "w_branch", "w_out")),
           ("ff2", ("ff2_wg", "ff2_wu", "ff2_wd")))


def _pad_rows(a, mult):
    pad = (-a.shape[-2]) % mult
    if pad == 0:
        return a
    return jnp.pad(a, [(0, 0)] * (a.ndim - 2) + [(0, pad), (0, 0)])


def _my_index():
    return 4 * lax.axis_index("x") + 2 * lax.axis_index("y") + lax.axis_index("c")


def _landing(shape, dtype):
    return lax.empty((N_DEV,) + tuple(shape), dtype)


def _stored(n, a):
    return jnp.swapaxes(a, -1, -2) if n in _COL_SHARDED else a


_FIRST = ("ff1_wg", "ff1_wu", "ff1_wd")


def _gather_first(w):
    names = _FIRST
    shards = [_rows(_stored(n, w[n][0]).astype(BF16)) for n in names]
    got = _all_gather("gather_first", jnp.concatenate(shards, axis=0))
    out, r = {}, 0
    for n, s in zip(names, shards):
        out[n] = got[:, r:r + s.shape[0]].reshape(-1, 1024)
        r += s.shape[0]
    return out, got


def _gather_start(w, after):
    conv = _pad_rows(jnp.concatenate([w[n].reshape(1, -1) for n in _CONV_SHARDED], axis=1), 8)
    keys, srcs = ["conv"], [conv]
    for l in range(2):
        for sub, (_, names) in enumerate(_GROUPS):
            for n in names:
                if l > 0 or n not in _FIRST:
                    keys.append((l, sub, n))
                    srcs.append(_stored(n, w[n][l]).astype(BF16))
    lands = [_landing(s.shape, s.dtype) for s in srcs]
    handles, token = _exchange_start("gather_start", srcs, lands, scatter=False, after=after)
    return dict(zip(keys, handles)), token


def _gather_finish(l, sub, handles, first, after):
    names = _GROUPS[sub][1]
    if (l, sub) == (0, 0):
        out = dict(first)
        for n in names:
            if n not in _FIRST:
                out[n] = lambda later, n=n: _parts_to_rows(
                    _exchange_wait(f"gather_wait_00_{n}", [handles[(0, 0, n)]], later, scatter=False)[0])
    elif sub == 1:
        out = {n: (lambda later, n=n: _parts_to_rows(
            _exchange_wait(f"gather_wait_{l}{sub}_{n}", [handles[(l, sub, n)]], later, scatter=False)[0])) for n in names}
        out["w_in"] = out["w_in"](after)
    else:
        lands = _exchange_wait(f"gather_wait_{l}{sub}", [handles[(l, sub, n)] for n in names], after, scatter=False)
        out = {n: _parts_to_rows(land) for n, land in zip(names, lands)}
    if "w_in" in out:
        out["w_in"] = _w_in_to_layout(out["w_in"])
    return out


def _scatter_start(l, sub, grads):
    srcs, shapes = [], []
    for n in grads:
        parts = _rows_to_parts(grads[n])
        shapes.append(parts.shape[1:])
        srcs.append(_pad_rows(parts.reshape(N_DEV, -1, 1024), PACK_ROW_ALIGN))
    lands = [_landing(s.shape[1:], s.dtype) for s in srcs]
    tag = f"{l}{sub}" + ("" if len(grads) == len(_GROUPS[sub][1]) else "_" + "_".join(grads))
    handles, token = _exchange_start(f"scatter_start_{tag}", srcs, lands, scatter=True)
    return handles, (tag, tuple(grads), shapes), token


def _scatter_finish(l, sub, handles, meta, after):
    tag, names, shapes = meta
    lands = _exchange_wait(f"scatter_wait_{tag}", handles, after, scatter=True)
    out = {}
    for n, land, shape in zip(names, lands, shapes):
        size = 1
        for s in shape:
            size *= s
        summed = _sum8(f"sum_{l}{sub}_{n}", land)
        out[n] = _stored(n, summed[:size // 1024].reshape(shape))
    return out


def _gather_conv_finish(w, handles, after):
    gconv = _exchange_wait("gather_wait_conv", [handles["conv"]], after, scatter=False)[0][:, 0]
    full, r = {}, 0
    for n in _CONV_SHARDED:
        sz = w[n].size
        full[n] = _parts_to_cols(gconv[:, r:r + sz].reshape((N_DEV,) + w[n].shape))
        r += sz
    return full


def _forward_backward(x, tgt, w, conv, get_weights, put_grads, put_small, token):
    saved, params = [], []
    for l in range(2):
        p = {n: w[n][l] for n in _SMALL if n != "final_norm"}
        for n in _CONV_SHARDED:
            p[n] = conv[n][l]
        mp = _mixer_params(p)
        tok = token[:1, :1] if l == 0 else 0.0
        p.update(get_weights(l, 0, x))
        x, s1, p["ff1_wd"] = _ffn_forward(f"l{l}_ff1", x, p["ff1_norm"][None] + tok, p["ff1_wg"], p["ff1_wu"],
                                          p["ff1_wd"])
        p.update(get_weights(l, 1, x))
        x, s2 = _mix_forward(f"l{l}_mix", x, p, mp)
        p.update(get_weights(l, 2, x))
        x, s3, _ = _ffn_forward(f"l{l}_ff2", x, p["ff2_norm"][None], p["ff2_wg"], p["ff2_wu"], p["ff2_wd"])
        saved.append((s1, s2, s3))
        params.append((p, mp))
    loss, dx, dfinal = _final_loss("loss_head", x, w["final_norm"][None], tgt)
    tok = 0.0
    for l in (1, 0):
        p, mp = params[l]
        s1, s2, s3 = saved[l]
        g = {}

        def put(sub):
            return lambda grads, l=l: put_grads(l, sub, grads)[:1, :1]

        dx, dn = _ffn_backward(f"l{l}_ff2", dx, s3, p["ff2_norm"][None] + tok, p["ff2_wg"], p["ff2_wu"], p["ff2_wd"],
                               put(2), _GROUPS[2][1])
        g["ff2_norm"] = dn[0]
        dx, gm = _mix_backward(f"l{l}_mix", dx, s2, p, mp, put(1))
        g.update(gm)
        tok = 0.0
        if l == 0:
            keep = {n: g.pop(n) for n in ("gdn_a_log", "gdn_dt_bias")}
            tok = put_small("0a", g, True)[:1, :1]
            g = keep
        dx, dn = _ffn_backward(f"l{l}_ff1", dx, s1, p["ff1_norm"][None] + tok, p["ff1_wg"], p["ff1_wu"], p["ff1_wd"],
                               put(0), _GROUPS[0][1], split=(l == 0))
        g["ff1_norm"] = dn[0]
        if l == 1:
            g["final_norm"] = dfinal[0]
            g["loss"] = loss[0, :1]
        tok = put_small("1" if l == 1 else "0b", g, False)[:1, :1]
    return dx


SMALL_PIECE = 8 * 1024


def _pack_small(d, names):
    pieces = []
    for n in names:
        flat = d[n].reshape(-1)
        pieces.append(jnp.pad(flat, (0, (-flat.size) % SMALL_PIECE)).reshape(-1, 1024))
    return jnp.concatenate(pieces, axis=0)


def _unpack_small(pack, shapes, names):
    out, r = {}, 0
    for n in names:
        size = 1
        for s in shapes[n]:
            size *= s
        rows = -(-size // SMALL_PIECE) * 8
        out[n] = pack[r:r + rows].reshape(-1)[:size].reshape(shapes[n])
        r += rows
    return out


def _small_names(grads):
    return tuple(n for n in _SMALL + ("loss",) if n in grads)


def _small_start(tag, grads, narrow):
    pack = _pack_small(grads, _small_names(grads))
    if narrow:
        pack = _pad_rows(pack.astype(BF16), PACK_ROW_ALIGN)
    handles, token = _exchange_start(f"small_start_{tag}", [pack], [_landing(pack.shape, pack.dtype)], scatter=False)
    return handles, {n: grads[n].shape for n in _small_names(grads)}, token


def _small_finish(tag, handles, shapes, after):
    landed = _exchange_wait(f"small_wait_{tag}", handles, after, scatter=False)[0]
    return _unpack_small(_sum8(f"sum_small_{tag}", landed), shapes, _small_names(shapes))


def _as2d(a):
    if a.ndim == 1:
        return a.reshape(1, -1)
    return a.reshape(-1, a.shape[-1])


def kernel(x, ff1_norm, ff1_wg, ff1_wu, ff1_wd, mix_norm, w_in, sgu_ln_g, sgu_ln_b, sgu_w, sgu_b, lru_conv_w, lru_conv_b, lru_wa, lru_ba, lru_wx, lru_bx, lru_lambda, gdn_conv_w, gdn_a_log, gdn_dt_bias, gdn_norm_g, pool_w, pool_scale, w_branch, w_out, ff2_norm, ff2_wg, ff2_wu, ff2_wd, final_norm, loss_target, m_ff1_norm, m_ff1_wg, m_ff1_wu, m_ff1_wd, m_mix_norm, m_w_in, m_sgu_ln_g, m_sgu_ln_b, m_sgu_w, m_sgu_b, m_lru_conv_w, m_lru_conv_b, m_lru_wa, m_lru_ba, m_lru_wx, m_lru_bx, m_lru_lambda, m_gdn_conv_w, m_gdn_a_log, m_gdn_dt_bias, m_gdn_norm_g, m_pool_w, m_pool_scale, m_w_branch, m_w_out, m_ff2_norm, m_ff2_wg, m_ff2_wu, m_ff2_wd, m_final_norm, v_ff1_norm, v_ff1_wg, v_ff1_wu, v_ff1_wd, v_mix_norm, v_w_in, v_sgu_ln_g, v_sgu_ln_b, v_sgu_w, v_sgu_b, v_lru_conv_w, v_lru_conv_b, v_lru_wa, v_lru_ba, v_lru_wx, v_lru_bx, v_lru_lambda, v_gdn_conv_w, v_gdn_a_log, v_gdn_dt_bias, v_gdn_norm_g, v_pool_w, v_pool_scale, v_w_branch, v_w_out, v_ff2_norm, v_ff2_wg, v_ff2_wu, v_ff2_wd, v_final_norm):
    w = dict(ff1_norm=ff1_norm, ff1_wg=ff1_wg, ff1_wu=ff1_wu, ff1_wd=ff1_wd, mix_norm=mix_norm, w_in=w_in,
             sgu_ln_g=sgu_ln_g, sgu_ln_b=sgu_ln_b, sgu_w=sgu_w, sgu_b=sgu_b, lru_conv_w=lru_conv_w,
             lru_conv_b=lru_conv_b, lru_wa=lru_wa, lru_ba=lru_ba, lru_wx=lru_wx, lru_bx=lru_bx, lru_lambda=lru_lambda,
             gdn_conv_w=gdn_conv_w, gdn_a_log=gdn_a_log, gdn_dt_bias=gdn_dt_bias, gdn_norm_g=gdn_norm_g, pool_w=pool_w,
             pool_scale=pool_scale, w_branch=w_branch, w_out=w_out, ff2_norm=ff2_norm, ff2_wg=ff2_wg, ff2_wu=ff2_wu,
             ff2_wd=ff2_wd, final_norm=final_norm)
    m = dict(ff1_norm=m_ff1_norm, ff1_wg=m_ff1_wg, ff1_wu=m_ff1_wu, ff1_wd=m_ff1_wd, mix_norm=m_mix_norm, w_in=m_w_in,
             sgu_ln_g=m_sgu_ln_g, sgu_ln_b=m_sgu_ln_b, sgu_w=m_sgu_w, sgu_b=m_sgu_b, lru_conv_w=m_lru_conv_w,
             lru_conv_b=m_lru_conv_b, lru_wa=m_lru_wa, lru_ba=m_lru_ba, lru_wx=m_lru_wx, lru_bx=m_lru_bx,
             lru_lambda=m_lru_lambda, gdn_conv_w=m_gdn_conv_w, gdn_a_log=m_gdn_a_log, gdn_dt_bias=m_gdn_dt_bias,
             gdn_norm_g=m_gdn_norm_g, pool_w=m_pool_w, pool_scale=m_pool_scale, w_branch=m_w_branch, w_out=m_w_out,
             ff2_norm=m_ff2_norm, ff2_wg=m_ff2_wg, ff2_wu=m_ff2_wu, ff2_wd=m_ff2_wd, final_norm=m_final_norm)
    v = dict(ff1_norm=v_ff1_norm, ff1_wg=v_ff1_wg, ff1_wu=v_ff1_wu, ff1_wd=v_ff1_wd, mix_norm=v_mix_norm, w_in=v_w_in,
             sgu_ln_g=v_sgu_ln_g, sgu_ln_b=v_sgu_ln_b, sgu_w=v_sgu_w, sgu_b=v_sgu_b, lru_conv_w=v_lru_conv_w,
             lru_conv_b=v_lru_conv_b, lru_wa=v_lru_wa, lru_ba=v_lru_ba, lru_wx=v_lru_wx, lru_bx=v_lru_bx,
             lru_lambda=v_lru_lambda, gdn_conv_w=v_gdn_conv_w, gdn_a_log=v_gdn_a_log, gdn_dt_bias=v_gdn_dt_bias,
             gdn_norm_g=v_gdn_norm_g, pool_w=v_pool_w, pool_scale=v_pool_scale, w_branch=v_w_branch, w_out=v_w_out,
             ff2_norm=v_ff2_norm, ff2_wg=v_ff2_wg, ff2_wu=v_ff2_wu, ff2_wd=v_ff2_wd, final_norm=v_final_norm)

    first, got_first = _gather_first(w)
    handles, token = _gather_start(w, got_first)
    conv = _gather_conv_finish(w, handles, token)
    pending = {}

    def get_weights(l, sub, after):
        return _gather_finish(l, sub, handles, first, after)

    def put_grads(l, sub, grads):
        hs, meta, tok = _scatter_start(l, sub, grads)
        pending[(l, sub, meta[0])] = (hs, meta)
        return tok

    def put_small(tag, grads, narrow):
        hs, shapes, tok = _small_start(tag, grads, narrow)
        pending[tag] = (hs, shapes)
        return tok

    T = x.shape[1]
    dx = _forward_backward(x.reshape(T, D), loss_target.reshape(T, D), w, conv, get_weights, put_grads, put_small,
                           token)
    per = {}
    for key in pending:
        if isinstance(key, tuple):
            per.setdefault(key[:2], {}).update(_scatter_finish(*key[:2], *pending[key], dx))
        else:
            per[key] = _small_finish(key, *pending[key], dx)
    grad = {n: jnp.stack([per[(0, sub)][n], per[(1, sub)][n]]) for sub, (_, names) in enumerate(_GROUPS) for n in names}
    layer0 = {**per["0a"], **per["0b"]}
    small = {n: _join([layer0[n].reshape(-1), per["1"][n].reshape(-1)]).reshape((2,) + layer0[n].shape)
             for n in layer0}
    small["final_norm"] = per["1"]["final_norm"]
    loss = per["1"]["loss"][0]
    me = _my_index()
    for n in _SMALL:
        if n in _CONV_SHARDED:
            width = w[n].shape[-1]
            grad[n] = lax.dynamic_slice_in_dim(small[n], me * width, width, axis=2)
        else:
            grad[n] = small[n]

    delta, new_m, new_v = {}, {}, {}
    for n in _BIG:
        d_, m_, v_ = _adamw("adamw_" + n, _as2d(w[n]), _as2d(grad[n]), _as2d(m[n]), _as2d(v[n]))
        delta[n], new_m[n], new_v[n] = (t.reshape(w[n].shape) for t in (d_, m_, v_))

    outs = _adamw_many("adamw_small", *[[_as2d(t[n]) for n in _SMALL] for t in (w, grad, m, v)])
    for k, dst in enumerate((delta, new_m, new_v)):
        for i, n in enumerate(_SMALL):
            dst[n] = outs[k * len(_SMALL) + i].reshape(w[n].shape)

    return (loss, dx.reshape(x.shape), *[grad[n] for n in _WEIGHTS], *[delta[n] for n in _WEIGHTS],
            *[new_m[n] for n in _WEIGHTS], *[new_v[n] for n in _WEIGHTS])
```

```python
import functools

import jax
import jax.numpy as jnp
from jax import lax
from jax.experimental import pallas as pl
from jax.experimental.pallas import tpu as pltpu

F32 = jnp.float32
BF16 = jnp.bfloat16
HI = lax.Precision.HIGHEST

N_DEV = 8
D = 1024
FF = 2816
BW = 512
NBR = 4
CHUNK = 64
EPS = 1e-6
LRU_C = 8.0
GDN_DK = 128

COL_AU, COL_AV, COL_BX, COL_BG = 0, 512, 1024, 1536
COL_CQ, COL_CK, COL_CV, COL_CZ = 2048, 2560, 3072, 3584
COL_DX, COL_GATE, COL_TAIL = 4096, 4608, 8704
PW = 9216
P_IN = 8712

ADAM_LR, ADAM_B1, ADAM_B2, ADAM_EPS, ADAM_WD, ADAM_STEP = 0.001, 0.9, 0.999, 1e-08, 0.01, 10

VMEM_LIMIT_V7X = 56 * 1024 * 1024

_NN = (((1,), (0,)), ((), ()))
_NT = (((1,), (1,)), ((), ()))
_TN = (((0,), (0,)), ((), ()))


def _cp(*sem):
    return pltpu.CompilerParams(dimension_semantics=tuple(sem), vmem_limit_bytes=VMEM_LIMIT_V7X)


def _dot(a, b, dims=_NN):
    return lax.dot_general(a.astype(BF16), b.astype(BF16), dims, preferred_element_type=F32)


def _dot_hi(a, b, dims=_NN):
    return lax.dot_general(a, b, dims, precision=HI, preferred_element_type=F32)


def _pick(n, cands):
    for c in cands:
        if n % c == 0:
            return c
    return n


@jax.custom_jvp
def _log1p(x):
    u = 1.0 + x
    return jnp.where(u == 1.0, x, x * jnp.log(u) / jnp.where(u == 1.0, 1.0, u - 1.0))


@_log1p.defjvp
def _log1p_jvp(p, t):
    (x,), (dx,) = p, t
    return _log1p(x), dx / (1.0 + x)


@jax.custom_jvp
def _expm1(x):
    u = jnp.exp(x)
    lu = jnp.log(u)
    small = (u == 1.0) | (lu == 0.0)
    return jnp.where(small, x, (u - 1.0) * x / jnp.where(small, 1.0, lu))


@_expm1.defjvp
def _expm1_jvp(p, t):
    (x,), (dx,) = p, t
    return _expm1(x), dx * jnp.exp(x)


def _softplus(x):
    return jnp.maximum(x, 0.0) + _log1p(jnp.exp(-jnp.abs(x)))


def _sigmoid(x):
    return jax.nn.sigmoid(x)


def _silu(x):
    return x * jax.nn.sigmoid(x)


def _gelu(x):
    return jax.nn.gelu(x)


@functools.partial(jax.custom_vjp, nondiff_argnums=(1,))
def _shift(x, s):
    return x if s == 0 else pltpu.roll(x, s, 0)


def _shift_fwd(x, s):
    return _shift(x, s), None


def _shift_bwd(s, _, g):
    n = g.shape[0]
    return (g if s == 0 else pltpu.roll(g, n - s, 0),)


_shift.defvjp(_shift_fwd, _shift_bwd)


def _scan_steps(a, b, reverse):
    n = a.shape[0]
    row = lax.broadcasted_iota(jnp.int32, a.shape, 0)
    k = 1
    while k < n:
        sh = n - k if reverse else k
        m = (row < n - k) if reverse else (row >= k)
        a_s = jnp.where(m, pltpu.roll(a, sh, 0), 1.0)
        b_s = jnp.where(m, pltpu.roll(b, sh, 0), 0.0)
        b = a * b_s + b
        a = a * a_s
        k *= 2
    return b


@jax.custom_vjp
def _scan(a, b):
    return _scan_steps(a, b, False)


def _scan_fwd(a, b):
    h = _scan_steps(a, b, False)
    return h, (a, h)


def _scan_bwd(res, dh):
    a, h = res
    n = a.shape[0]
    row = lax.broadcasted_iota(jnp.int32, a.shape, 0)
    a_next = jnp.where(row < n - 1, pltpu.roll(a, n - 1, 0), 0.0)
    g = _scan_steps(a_next, dh, True)
    h_prev = jnp.where(row >= 1, pltpu.roll(h, 1, 0), 0.0)
    return g * h_prev, g


_scan.defvjp(_scan_fwd, _scan_bwd)


def _mm(name, pairs, mode, out_dtype, *, res=None, scale=1.0, bm=None, bn=None, bk=None, after=None):
    a0, b0 = pairs[0]
    if mode == "nn":
        (M, K), N = a0.shape, b0.shape[1]
    elif mode == "nt":
        (M, K), N = a0.shape, b0.shape[0]
    else:
        (K, M), N = a0.shape, b0.shape[1]
    bm = bm or _pick(M, (1024, 512, 256, 128))
    bn = bn or _pick(N, (1024, 512, 256, 128))
    bk = bk or _pick(K, (1024, 512, 1408, 256, 128))
    nk = K // bk
    npair = len(pairs)
    dims = {"nn": _NN, "nt": _NT, "tn": _TN}[mode]

    def body(*refs):
        ab = refs[:2 * npair]
        pos = 2 * npair
        r_ref = None
        if res is not None:
            r_ref = refs[pos]
            pos += 1
        pos += after is not None
        o_ref = refs[pos]
        part = None
        for p in range(npair):
            d = _dot(ab[2 * p][...], ab[2 * p + 1][...], dims)
            part = d if part is None else part + d

        def finish(acc):
            out = acc if scale == 1.0 else acc * scale
            if r_ref is not None:
                out = out + r_ref[...]
            o_ref[...] = out.astype(out_dtype)

        if nk == 1:
            finish(part)
        else:
            acc_ref = refs[pos + 1]
            k = pl.program_id(2)

            @pl.when(k == 0)
            def _():
                acc_ref[...] = part

            @pl.when(k > 0)
            def _():
                acc_ref[...] += part

            @pl.when(k == nk - 1)
            def _():
                finish(acc_ref[...])

    if mode == "nn":
        a_spec = pl.BlockSpec((bm, bk), lambda i, j, k: (i, k))
        b_spec = pl.BlockSpec((bk, bn), lambda i, j, k: (k, j))
    elif mode == "nt":
        a_spec = pl.BlockSpec((bm, bk), lambda i, j, k: (i, k))
        b_spec = pl.BlockSpec((bn, bk), lambda i, j, k: (j, k))
    else:
        a_spec = pl.BlockSpec((bk, bm), lambda i, j, k: (k, i))
        b_spec = pl.BlockSpec((bk, bn), lambda i, j, k: (k, j))
    o_spec = pl.BlockSpec((bm, bn), lambda i, j, k: (i, j))
    in_specs, args = [], []
    for a, b in pairs:
        in_specs += [a_spec, b_spec]
        args += [a, b]
    if res is not None:
        in_specs.append(o_spec)
        args.append(res)
    if after is not None:
        in_specs.append(_ANY)
        args.append(after)
    return pl.pallas_call(
        body, name=name, grid=(M // bm, N // bn, nk),
        in_specs=in_specs, out_specs=o_spec,
        out_shape=jax.ShapeDtypeStruct((M, N), out_dtype),
        scratch_shapes=[pltpu.VMEM((bm, bn), F32)] if nk > 1 else [],
        compiler_params=_cp("parallel", "parallel", "arbitrary"),
    )(*args)


def _rms_fwd(name, x, g):
    T = x.shape[0]
    bm = _pick(T, (1024, 512, 256, 128))

    def body(x_ref, g_ref, o_ref):
        xv = x_ref[...]
        r = lax.rsqrt(jnp.mean(xv * xv, axis=-1, keepdims=True) + EPS)
        o_ref[...] = (xv * r * g_ref[...]).astype(BF16)

    return pl.pallas_call(
        body, name=name, grid=(T // bm,),
        in_specs=[pl.BlockSpec((bm, D), lambda i: (i, 0)), pl.BlockSpec((1, D), lambda i: (0, 0))],
        out_specs=pl.BlockSpec((bm, D), lambda i: (i, 0)),
        out_shape=jax.ShapeDtypeStruct((T, D), BF16),
        compiler_params=_cp("parallel"),
    )(x, g)


def _rms_bwd(name, x, g, dh, dres):
    T = x.shape[0]
    bm = _pick(T, (1024, 512, 256, 128))

    def body(x_ref, g_ref, dh_ref, dres_ref, dx_ref, dg_ref):
        xv = x_ref[...]
        r = lax.rsqrt(jnp.mean(xv * xv, axis=-1, keepdims=True) + EPS)
        xh = xv * r
        dhv = dh_ref[...]
        dxh = dhv * g_ref[...]
        dx_ref[...] = dres_ref[...] + r * (dxh - xh * jnp.mean(dxh * xh, axis=-1, keepdims=True))
        part = jnp.sum(dhv * xh, axis=0, keepdims=True)

        @pl.when(pl.program_id(0) == 0)
        def _():
            dg_ref[...] = part

        @pl.when(pl.program_id(0) > 0)
        def _():
            dg_ref[...] += part

    row = pl.BlockSpec((bm, D), lambda i: (i, 0))
    vec = pl.BlockSpec((1, D), lambda i: (0, 0))
    return pl.pallas_call(
        body, name=name, grid=(T // bm,),
        in_specs=[row, vec, row, row], out_specs=[row, vec],
        out_shape=[jax.ShapeDtypeStruct((T, D), F32), jax.ShapeDtypeStruct((1, D), F32)],
        compiler_params=_cp("arbitrary"),
    )(x, g, dh, dres)


def _final_loss(name, x, g, tgt):
    T = x.shape[0]
    bm = _pick(T, (512, 256, 128))

    def body(x_ref, g_ref, t_ref, loss_ref, dx_ref, dg_ref):
        xv = x_ref[...]
        gv = g_ref[...]
        r = lax.rsqrt(jnp.mean(xv * xv, axis=-1, keepdims=True) + EPS)
        xh = xv * r
        e = xh * gv - t_ref[...]
        lpart = jnp.broadcast_to(0.5 * jnp.sum(jnp.mean(e * e, axis=-1, keepdims=True), axis=0, keepdims=True), (1, 128))
        dy = e * (1.0 / D)
        dxh = dy * gv
        dx_ref[...] = r * (dxh - xh * jnp.mean(dxh * xh, axis=-1, keepdims=True))
        gpart = jnp.sum(dy * xh, axis=0, keepdims=True)

        @pl.when(pl.program_id(0) == 0)
        def _():
            loss_ref[...] = lpart
            dg_ref[...] = gpart

        @pl.when(pl.program_id(0) > 0)
        def _():
            loss_ref[...] += lpart
            dg_ref[...] += gpart

    row = pl.BlockSpec((bm, D), lambda i: (i, 0))
    vec = pl.BlockSpec((1, D), lambda i: (0, 0))
    return pl.pallas_call(
        body, name=name, grid=(T // bm,),
        in_specs=[row, vec, row],
        out_specs=[pl.BlockSpec((1, 128), lambda i: (0, 0)), row, vec],
        out_shape=[jax.ShapeDtypeStruct((1, 128), F32), jax.ShapeDtypeStruct((T, D), F32),
                   jax.ShapeDtypeStruct((1, D), F32)],
        compiler_params=_cp("arbitrary"),
    )(x, g, tgt)


def _ffn_up(name, h, wg, wu):
    T = h.shape[0]
    bm = _pick(T, (2048, 1024, 512, 256, 128))
    bn = 256

    def body(h_ref, wg_ref, wu_ref, sa_ref, ds_ref, act_ref):
        hv = h_ref[...]
        a = _dot(hv, wg_ref[...], _NT)
        b = _dot(hv, wu_ref[...], _NT)
        s = _sigmoid(a)
        sa = a * s
        sa_ref[...] = sa.astype(BF16)
        ds_ref[...] = (b * (s * (1.0 + a * (1.0 - s)))).astype(BF16)
        act_ref[...] = (sa * b).astype(BF16)

    w_spec = pl.BlockSpec((bn, D), lambda i, j: (j, 0))
    o_spec = pl.BlockSpec((bm, bn), lambda i, j: (i, j))
    return pl.pallas_call(
        body, name=name, grid=(T // bm, FF // bn),
        in_specs=[pl.BlockSpec((bm, D), lambda i, j: (i, 0)), w_spec, w_spec],
        out_specs=[o_spec, o_spec, o_spec],
        out_shape=[jax.ShapeDtypeStruct((T, FF), BF16)] * 3,
        compiler_params=_cp("parallel", "parallel"),
    )(h, wg, wu)


def _ffn_dact(name, dy, wd, sa, ds, after=None):
    T = dy.shape[0]
    bm = _pick(T, (2048, 1024, 512, 256, 128))
    bn = 256

    def body(dy_ref, wd_ref, sa_ref, ds_ref, *rest):
        da_ref, db_ref, dy_bf = rest[-3:]

        @pl.when(pl.program_id(1) == 0)
        def _():
            dy_bf[...] = dy_ref[...].astype(BF16)

        dact = 0.5 * _dot(dy_bf[...], wd_ref[...], _NT)
        da_ref[...] = (dact * ds_ref[...].astype(F32)).astype(BF16)
        db_ref[...] = (dact * sa_ref[...].astype(F32)).astype(BF16)

    t_spec = pl.BlockSpec((bm, bn), lambda i, j: (i, j))
    return pl.pallas_call(
        body, name=name, grid=(T // bm, FF // bn),
        in_specs=[pl.BlockSpec((bm, D), lambda i, j: (i, 0)), pl.BlockSpec((bn, D), lambda i, j: (j, 0)),
                  t_spec, t_spec] + [_ANY] * (after is not None),
        out_specs=[t_spec, t_spec],
        out_shape=[jax.ShapeDtypeStruct((T, FF), BF16), jax.ShapeDtypeStruct((T, FF), BF16)],
        scratch_shapes=[pltpu.VMEM((bm, D), BF16)],
        compiler_params=_cp("parallel", "arbitrary"),
    )(dy, wd, sa, ds, *([after] if after is not None else []))


def _merge_specs(T, bm, bn):
    y_spec = pl.BlockSpec((bm, BW), lambda i, j: (i, 0))
    wb_spec = pl.BlockSpec((NBR, bn, BW), lambda i, j: (0, j, 0))
    gate_specs = [pl.BlockSpec((bm, bn), functools.partial(lambda i, j, o: (i, o + j), o=(COL_GATE + g * D) // bn))
                  for g in range(NBR)]
    t_spec = pl.BlockSpec((bm, bn), lambda i, j: (i, j))
    return y_spec, wb_spec, gate_specs, t_spec


def _merge_fwd(name, ys, wb, proj):
    T = proj.shape[0]
    bm = _pick(T, (512, 256, 128))
    bn = 512
    y_spec, wb_spec, gate_specs, t_spec = _merge_specs(T, bm, bn)

    def body(y0, y1, y2, y3, wb_ref, g0, g1, g2, g3, o_ref):
        acc = None
        for g, (y_ref, g_ref) in enumerate(((y0, g0), (y1, g1), (y2, g2), (y3, g3))):
            t = _sigmoid(g_ref[...].astype(F32)) * _dot(y_ref[...], wb_ref[g], _NT)
            acc = t if acc is None else acc + t
        o_ref[...] = acc.astype(BF16)

    return pl.pallas_call(
        body, name=name, grid=(T // bm, D // bn),
        in_specs=[y_spec] * NBR + [wb_spec] + gate_specs, out_specs=t_spec,
        out_shape=jax.ShapeDtypeStruct((T, D), BF16),
        compiler_params=_cp("parallel", "parallel"),
    )(*ys, wb, proj, proj, proj, proj)


def _merge_bwd(name, dm, ys, wb, proj):
    T = proj.shape[0]
    bm = _pick(T, (512, 256, 128))
    bn = 512
    y_spec, wb_spec, gate_specs, t_spec = _merge_specs(T, bm, bn)

    def body(dm_ref, y0, y1, y2, y3, wb_ref, g0, g1, g2, g3, *outs):
        dmv = dm_ref[...]
        j = pl.program_id(1)
        for g, (y_ref, g_ref) in enumerate(((y0, g0), (y1, g1), (y2, g2), (y3, g3))):
            br = _dot(y_ref[...], wb_ref[g], _NT)
            s = _sigmoid(g_ref[...].astype(F32))
            outs[g][...] = (dmv * br * (s * (1.0 - s))).astype(BF16)
            dbr = (dmv * s).astype(BF16)
            outs[NBR + g][...] = dbr
            part = _dot(dbr, wb_ref[g])
            dy_ref = outs[2 * NBR + g]

            @pl.when(j == 0)
            def _():
                dy_ref[...] = part

            @pl.when(j > 0)
            def _():
                dy_ref[...] += part

    return pl.pallas_call(
        body, name=name, grid=(T // bm, D // bn),
        in_specs=[t_spec] + [y_spec] * NBR + [wb_spec] + gate_specs, out_specs=[t_spec] * (2 * NBR) + [y_spec] * NBR,
        out_shape=[jax.ShapeDtypeStruct((T, D), BF16)] * (2 * NBR) + [jax.ShapeDtypeStruct((T, BW), F32)] * NBR,
        compiler_params=_cp("parallel", "arbitrary"),
    )(dm, *ys, wb, proj, proj, proj, proj)


def _dwb(name, dbrs, ys):
    T = ys[0].shape[0]
    bk = _pick(T, (1024, 512, 256, 128))
    nk = T // bk

    def body(*refs):
        d_refs, y_refs, o_ref, acc = refs[:NBR], refs[NBR:2 * NBR], refs[2 * NBR], refs[2 * NBR + 1]
        k = pl.program_id(0)
        for g in range(NBR):
            part = _dot(d_refs[g][...], y_refs[g][...], _TN)

            @pl.when(k == 0)
            def _(g=g, part=part):
                acc[g] = part

            @pl.when(k > 0)
            def _(g=g, part=part):
                acc[g] += part

        @pl.when(k == nk - 1)
        def _():
            o_ref[...] = acc[...].astype(BF16)

    return pl.pallas_call(
        body, name=name, grid=(nk,),
        in_specs=[pl.BlockSpec((bk, D), lambda k: (k, 0))] * NBR + [pl.BlockSpec((bk, BW), lambda k: (k, 0))] * NBR,
        out_specs=pl.BlockSpec((NBR, D, BW), lambda k: (0, 0, 0)),
        out_shape=jax.ShapeDtypeStruct((NBR, D, BW), BF16),
        scratch_shapes=[pltpu.VMEM((NBR, D, BW), F32)],
        compiler_params=_cp("arbitrary"),
    )(*dbrs, *ys)


def _sgu_block(u_pre, v_pre, ln_g, ln_b, w, bias):
    u = _gelu(u_pre)
    vf = _gelu(v_pre)
    mu = jnp.mean(vf, axis=-1, keepdims=True)
    var = jnp.mean(jnp.square(vf - mu), axis=-1, keepdims=True)
    vn = (vf - mu) * lax.rsqrt(var + EPS) * ln_g + ln_b
    ri = lax.broadcasted_iota(jnp.int32, (128, 128), 0)
    ci = lax.broadcasted_iota(jnp.int32, (128, 128), 1)
    mask = (ri // CHUNK) >= (ci // CHUNK)
    outs = [_dot(jnp.where(mask, w[g], 0.0), vn[:, g * 128:(g + 1) * 128]) for g in range(4)]
    mixed = jnp.concatenate(outs, axis=1) + bias
    return u * mixed


def _sgu_param_specs():
    return [pl.BlockSpec((1, BW), lambda i: (0, 0)), pl.BlockSpec((1, BW), lambda i: (0, 0)),
            pl.BlockSpec((4, 128, 128), lambda i: (0, 0, 0)), pl.BlockSpec((128, BW), lambda i: (0, 0))]


def _sgu_fwd(name, proj, ln_g, ln_b, w, bias):
    T = proj.shape[0]
    rb = _pick(T, (256, 128))

    def body(u_ref, v_ref, g_ref, b_ref, w_ref, bias_ref, y_ref):
        for n in range(rb // 128):
            rows = slice(n * 128, (n + 1) * 128)
            y = _sgu_block(u_ref[rows, :].astype(F32), v_ref[rows, :].astype(F32), g_ref[...], b_ref[...], w_ref[...],
                           bias_ref[...])
            y_ref[rows, :] = y.astype(BF16)

    return pl.pallas_call(
        body, name=name, grid=(T // rb,),
        in_specs=[pl.BlockSpec((rb, BW), lambda i: (i, COL_AU // BW)), pl.BlockSpec((rb, BW), lambda i: (i, COL_AV // BW))]
        + _sgu_param_specs(),
        out_specs=pl.BlockSpec((rb, BW), lambda i: (i, 0)),
        out_shape=jax.ShapeDtypeStruct((T, BW), BF16),
        compiler_params=_cp("parallel"),
    )(proj, proj, ln_g, ln_b, w, bias)


def _sgu_bwd(name, proj, dy, ln_g, ln_b, w, bias):
    T = proj.shape[0]
    rb = _pick(T, (256, 128))

    def body(u_ref, v_ref, dy_ref, g_ref, b_ref, w_ref, bias_ref, du_ref, dv_ref, dg_ref, db_ref, dw_ref, dbias_ref):
        acc = None
        for n in range(rb // 128):
            rows = slice(n * 128, (n + 1) * 128)
            _, vjp = jax.vjp(_sgu_block, u_ref[rows, :].astype(F32), v_ref[rows, :].astype(F32), g_ref[...], b_ref[...],
                             w_ref[...],
                             bias_ref[...])
            du, dv, *dp = vjp(dy_ref[rows, :])
            du_ref[rows, :] = du.astype(BF16)
            dv_ref[rows, :] = dv.astype(BF16)
            acc = dp if acc is None else [p + q for p, q in zip(acc, dp)]

        @pl.when(pl.program_id(0) == 0)
        def _():
            for r, p in zip((dg_ref, db_ref, dw_ref, dbias_ref), acc):
                r[...] = p

        @pl.when(pl.program_id(0) > 0)
        def _():
            for r, p in zip((dg_ref, db_ref, dw_ref, dbias_ref), acc):
                r[...] += p

    row = pl.BlockSpec((rb, BW), lambda i: (i, 0))
    return pl.pallas_call(
        body, name=name, grid=(T // rb,),
        in_specs=[pl.BlockSpec((rb, BW), lambda i: (i, COL_AU // BW)), pl.BlockSpec((rb, BW), lambda i: (i, COL_AV // BW)),
                  row] + _sgu_param_specs(),
        out_specs=[row, row] + _sgu_param_specs(),
        out_shape=[jax.ShapeDtypeStruct((T, BW), BF16), jax.ShapeDtypeStruct((T, BW), BF16),
                   jax.ShapeDtypeStruct((1, BW), F32), jax.ShapeDtypeStruct((1, BW), F32),
                   jax.ShapeDtypeStruct((4, 128, 128), F32), jax.ShapeDtypeStruct((128, BW), F32)],
        compiler_params=_cp("arbitrary"),
    )(proj, proj, dy, ln_g, ln_b, w, bias)


def _halo_block(ref, i, rblk, halo):
    r0 = pl.multiple_of(i * rblk, rblk)
    h0 = pl.multiple_of(jnp.maximum(r0 - 16, 0), 16)
    top = jnp.where(i > 0, ref[pl.ds(h0, 16), :].astype(F32), 0.0)[16 - halo:]
    return jnp.concatenate([top, ref[pl.ds(r0, rblk), :].astype(F32)], axis=0)


def _with_halo_grad(dfull, pending, halo, rblk):
    tail = jnp.concatenate([jnp.zeros((rblk - halo, 128), F32), pending], axis=0)
    return dfull[halo:] + tail


def _conv4(xfull, rows):
    acc = None
    for k in range(4):
        t = rows[k] * _shift(xfull, 3 - k)[8:]
        acc = t if acc is None else acc + t
    return acc


def _lru_block(xfull, gate, h0, c0, c1, c2, c3, cb, wa, ba, wx, bx, lam):
    n = gate.shape[0]
    xc = _conv4(xfull, (c0, c1, c2, c3)) + cb
    r = _sigmoid(_dot(xc, wa) + ba)
    ig = _sigmoid(_dot(xc, wx) + bx)
    log_a = -LRU_C * r * _softplus(-lam)
    a = jnp.exp(log_a)
    mult = jnp.sqrt(-_expm1(2.0 * log_a))
    b = mult * (ig * xc)
    row = lax.broadcasted_iota(jnp.int32, (n, 128), 0)
    b = b + jnp.where(row == 0, a * h0, 0.0)
    h = _scan(a, b)
    out = h * _gelu(gate)
    h_last = jnp.sum(jnp.where(row == n - 1, h, 0.0), axis=0, keepdims=True)
    return out, h_last


def _lru_param_specs():
    vec = pl.BlockSpec((1, 128), lambda g: (0, g))
    mat = pl.BlockSpec((None, 128, 128), lambda g: (g, 0, 0))
    return [pl.BlockSpec((4, 128), lambda g: (0, g)), vec, mat, vec, mat, vec, vec]


def _lru_load_params(cw_ref, cb_ref, wa_ref, ba_ref, wx_ref, bx_ref, lam_ref):
    return (cw_ref[0:1, :], cw_ref[1:2, :], cw_ref[2:3, :], cw_ref[3:4, :], cb_ref[...], wa_ref[...], ba_ref[...],
            wx_ref[...], bx_ref[...], lam_ref[...])


def _lru_fwd(name, proj, cw, cb, wa, ba, wx, bx, lam):
    T = proj.shape[0]
    rblk = _pick(T, (256, 128))
    nblk = T // rblk

    def body(x_ref, gt_ref, cw_ref, cb_ref, wa_ref, ba_ref, wx_ref, bx_ref, lam_ref, y_ref, hc_ref):
        params = _lru_load_params(cw_ref, cb_ref, wa_ref, ba_ref, wx_ref, bx_ref, lam_ref)

        def step(i, h0):
            r0 = pl.multiple_of(i * rblk, rblk)
            out, h_last = _lru_block(_halo_block(x_ref, i, rblk, 8), gt_ref[pl.ds(r0, rblk), :].astype(F32), h0,
                                     *params)
            y_ref[pl.ds(r0, rblk), :] = out.astype(BF16)
            hc_ref[pl.ds(pl.multiple_of(i * 8, 8), 8), :] = jnp.broadcast_to(h0, (8, 128))
            return h_last

        lax.fori_loop(0, nblk, step, jnp.zeros((1, 128), F32))

    return pl.pallas_call(
        body, name=name, grid=(4,),
        in_specs=[pl.BlockSpec((T, 128), lambda g: (0, COL_BX // 128 + g)),
                  pl.BlockSpec((T, 128), lambda g: (0, COL_BG // 128 + g))] + _lru_param_specs(),
        out_specs=[pl.BlockSpec((T, 128), lambda g: (0, g)), pl.BlockSpec((nblk * 8, 128), lambda g: (0, g))],
        out_shape=[jax.ShapeDtypeStruct((T, BW), BF16), jax.ShapeDtypeStruct((nblk * 8, BW), F32)],
        compiler_params=_cp("parallel"),
    )(proj, proj, cw, cb, wa, ba, wx, bx, lam)


def _lru_bwd(name, proj, dy, hc, cw, cb, wa, ba, wx, bx, lam):
    T = proj.shape[0]
    rblk = _pick(T, (256, 128))
    nblk = T // rblk

    def body(x_ref, gt_ref, dy_ref, hc_ref, cw_ref, cb_ref, wa_ref, ba_ref, wx_ref, bx_ref, lam_ref,
             dx_ref, dgt_ref, dcw_ref, dcb_ref, dwa_ref, dba_ref, dwx_ref, dbx_ref, dlam_ref):
        params = _lru_load_params(cw_ref, cb_ref, wa_ref, ba_ref, wx_ref, bx_ref, lam_ref)

        def step(it, carry):
            dh_last, pending, acc = carry
            i = nblk - 1 - it
            r0 = pl.multiple_of(i * rblk, rblk)
            h0 = hc_ref[pl.ds(pl.multiple_of(i * 8, 8), 1), :]
            _, vjp = jax.vjp(_lru_block, _halo_block(x_ref, i, rblk, 8), gt_ref[pl.ds(r0, rblk), :].astype(F32), h0,
                             *params)
            dfull, dgate, dh0, *dp = vjp((dy_ref[pl.ds(r0, rblk), :], dh_last))
            dx_ref[pl.ds(r0, rblk), :] = _with_halo_grad(dfull, pending, 8, rblk).astype(BF16)
            dgt_ref[pl.ds(r0, rblk), :] = dgate.astype(BF16)
            return dh0, dfull[:8], tuple(p + q for p, q in zip(acc, dp))

        zeros = tuple(jnp.zeros(p.shape, F32) for p in params)
        _, _, acc = lax.fori_loop(0, nblk, step, (jnp.zeros((1, 128), F32), jnp.zeros((8, 128), F32), zeros))
        for k in range(4):
            dcw_ref[k:k + 1, :] = acc[k]
        for r, p in zip((dcb_ref, dwa_ref, dba_ref, dwx_ref, dbx_ref, dlam_ref), acc[4:]):
            r[...] = p

    col = pl.BlockSpec((T, 128), lambda g: (0, g))
    return pl.pallas_call(
        body, name=name, grid=(4,),
        in_specs=[pl.BlockSpec((T, 128), lambda g: (0, COL_BX // 128 + g)),
                  pl.BlockSpec((T, 128), lambda g: (0, COL_BG // 128 + g)), col,
                  pl.BlockSpec((nblk * 8, 128), lambda g: (0, g))] + _lru_param_specs(),
        out_specs=[col, col] + _lru_param_specs(),
        out_shape=[jax.ShapeDtypeStruct((T, BW), BF16), jax.ShapeDtypeStruct((T, BW), BF16),
                   jax.ShapeDtypeStruct((4, BW), F32), jax.ShapeDtypeStruct((1, BW), F32),
                   jax.ShapeDtypeStruct((4, 128, 128), F32), jax.ShapeDtypeStruct((1, BW), F32),
                   jax.ShapeDtypeStruct((4, 128, 128), F32), jax.ShapeDtypeStruct((1, BW), F32),
                   jax.ShapeDtypeStruct((1, BW), F32)],
        compiler_params=_cp("parallel"),
    )(proj, proj, dy, hc, cw, cb, wa, ba, wx, bx, lam)


def _conv_block(xfull, c0, c1, c2, c3):
    return _silu(_conv4(xfull, (c0, c1, c2, c3)))


def _conv_fwd(name, proj, col0, cw, cw_col0):
    T = proj.shape[0]
    rblk = _pick(T, (256, 128))
    nblk = T // rblk

    def body(x_ref, cw_ref, y_ref):
        rows = (cw_ref[0:1, :], cw_ref[1:2, :], cw_ref[2:3, :], cw_ref[3:4, :])

        def step(i, c):
            r0 = pl.multiple_of(i * rblk, rblk)
            y_ref[pl.ds(r0, rblk), :] = _conv_block(_halo_block(x_ref, i, rblk, 8), *rows)
            return c

        lax.fori_loop(0, nblk, step, 0)

    return pl.pallas_call(
        body, name=name, grid=(4,),
        in_specs=[pl.BlockSpec((T, 128), lambda g: (0, col0 // 128 + g)),
                  pl.BlockSpec((4, 128), lambda g: (0, cw_col0 // 128 + g))],
        out_specs=pl.BlockSpec((T, 128), lambda g: (0, g)),
        out_shape=jax.ShapeDtypeStruct((T, BW), F32),
        compiler_params=_cp("parallel"),
    )(proj, cw)


def _conv_bwd(name, proj, col0, dy, cw, cw_col0):
    T = proj.shape[0]
    rblk = _pick(T, (256, 128))
    nblk = T // rblk

    def body(x_ref, dy_ref, cw_ref, dx_ref, dcw_ref):
        rows = (cw_ref[0:1, :], cw_ref[1:2, :], cw_ref[2:3, :], cw_ref[3:4, :])

        def step(it, carry):
            pending, acc = carry
            i = nblk - 1 - it
            r0 = pl.multiple_of(i * rblk, rblk)
            _, vjp = jax.vjp(_conv_block, _halo_block(x_ref, i, rblk, 8), *rows)
            dfull, *dp = vjp(dy_ref[pl.ds(r0, rblk), :])
            dx_ref[pl.ds(r0, rblk), :] = _with_halo_grad(dfull, pending, 8, rblk).astype(BF16)
            return dfull[:8], tuple(p + q for p, q in zip(acc, dp))

        zeros = tuple(jnp.zeros((1, 128), F32) for _ in range(4))
        _, acc = lax.fori_loop(0, nblk, step, (jnp.zeros((8, 128), F32), zeros))
        for k in range(4):
            dcw_ref[k:k + 1, :] = acc[k]

    col = pl.BlockSpec((T, 128), lambda g: (0, g))
    return pl.pallas_call(
        body, name=name, grid=(4,),
        in_specs=[pl.BlockSpec((T, 128), lambda g: (0, col0 // 128 + g)), col,
                  pl.BlockSpec((4, 128), lambda g: (0, cw_col0 // 128 + g))],
        out_specs=[col, pl.BlockSpec((4, 128), lambda g: (0, g))],
        out_shape=[jax.ShapeDtypeStruct((T, BW), BF16), jax.ShapeDtypeStruct((4, BW), F32)],
        compiler_params=_cp("parallel"),
    )(proj, dy, cw)


def _pool_block(xfull, pw, sc, t0, gi):
    n = xfull.shape[0] - 16
    s2 = xfull + _shift(xfull, 1)
    s4 = s2 + _shift(s2, 2)
    s8 = s4 + _shift(s4, 4)
    s16 = s8 + _shift(s8, 8)
    s = jnp.where(gi == 0, s2, jnp.where(gi == 1, s4, jnp.where(gi == 2, s8, s16)))[16:]
    t = t0 + lax.broadcasted_iota(jnp.int32, (n, 128), 0)
    cnt = jnp.minimum(t + 1, lax.shift_left(jnp.int32(2), gi)).astype(F32)
    pooled = s / cnt - xfull[16:]
    return _dot(pooled, pw) * sc


def _pool_fwd(name, proj, pw, sc):
    T = proj.shape[0]
    rblk = _pick(T, (256, 128))
    nblk = T // rblk

    def body(x_ref, pw_ref, sc_ref, y_ref):
        gi = pl.program_id(0)

        def step(i, c):
            r0 = pl.multiple_of(i * rblk, rblk)
            y = _pool_block(_halo_block(x_ref, i, rblk, 16), pw_ref[...], sc_ref[...], r0, gi)
            y_ref[pl.ds(r0, rblk), :] = y.astype(BF16)
            return c

        lax.fori_loop(0, nblk, step, 0)

    return pl.pallas_call(
        body, name=name, grid=(4,),
        in_specs=[pl.BlockSpec((T, 128), lambda g: (0, COL_DX // 128 + g)),
                  pl.BlockSpec((None, 128, 128), lambda g: (g, 0, 0)), pl.BlockSpec((1, 128), lambda g: (0, g))],
        out_specs=pl.BlockSpec((T, 128), lambda g: (0, g)),
        out_shape=jax.ShapeDtypeStruct((T, BW), BF16),
        compiler_params=_cp("parallel"),
    )(proj, pw, sc)


def _pool_bwd(name, proj, dy, pw, sc):
    T = proj.shape[0]
    rblk = _pick(T, (256, 128))
    nblk = T // rblk

    def body(x_ref, dy_ref, pw_ref, sc_ref, dx_ref, dpw_ref, dsc_ref):
        gi = pl.program_id(0)

        def step(it, carry):
            pending, apw, asc = carry
            i = nblk - 1 - it
            r0 = pl.multiple_of(i * rblk, rblk)
            _, vjp = jax.vjp(lambda xf, w, s: _pool_block(xf, w, s, r0, gi), _halo_block(x_ref, i, rblk, 16),
                             pw_ref[...], sc_ref[...])
            dfull, dw, ds = vjp(dy_ref[pl.ds(r0, rblk), :])
            dx_ref[pl.ds(r0, rblk), :] = _with_halo_grad(dfull, pending, 16, rblk).astype(BF16)
            return dfull[:16], apw + dw, asc + ds

        _, apw, asc = lax.fori_loop(0, nblk, step, (jnp.zeros((16, 128), F32), jnp.zeros((128, 128), F32),
                                                    jnp.zeros((1, 128), F32)))
        dpw_ref[...] = apw
        dsc_ref[...] = asc

    col = pl.BlockSpec((T, 128), lambda g: (0, g))
    mat = pl.BlockSpec((None, 128, 128), lambda g: (g, 0, 0))
    vec = pl.BlockSpec((1, 128), lambda g: (0, g))
    return pl.pallas_call(
        body, name=name, grid=(4,),
        in_specs=[pl.BlockSpec((T, 128), lambda g: (0, COL_DX // 128 + g)), col, mat, vec],
        out_specs=[col, mat, vec],
        out_shape=[jax.ShapeDtypeStruct((T, BW), BF16), jax.ShapeDtypeStruct((4, 128, 128), F32),
                   jax.ShapeDtypeStruct((1, BW), F32)],
        compiler_params=_cp("parallel"),
    )(proj, dy, pw, sc)


@jax.custom_vjp
def _dot3(a, b):
    ah = a.astype(BF16)
    al = (a - ah.astype(F32)).astype(BF16)
    bh = b.astype(BF16)
    bl = (b - bh.astype(F32)).astype(BF16)

    def d(x, y):
        return lax.dot_general(x, y, _NN, preferred_element_type=F32)

    return d(ah, bh) + (d(ah, bl) + d(al, bh))


def _dot3_fwd(a, b):
    return _dot3(a, b), (a, b)


def _dot3_bwd(res, g):
    a, b = res
    return _dot(g, b, _NT), _dot(a, g, _TN)


_dot3.defvjp(_dot3_fwd, _dot3_bwd)


def _pad_rows2(x):
    return jnp.concatenate([x, jnp.zeros_like(x)], axis=0)


@jax.custom_vjp
def _tri_inv(mats):
    n = mats[0].shape[0]
    eye = (lax.broadcasted_iota(jnp.int32, (n, n), 0) == lax.broadcasted_iota(jnp.int32, (n, n), 1)).astype(F32)
    ps = [eye - a for a in mats]
    ms = list(mats)
    k = 2
    while k < n:
        ms = [_dot3(t, t) for t in ms]
        ps = [p + _dot3(p, t) for p, t in zip(ps, ms)]
        k *= 2
    return ps


def _tri_inv_fwd(mats):
    ts = _tri_inv(mats)
    return ts, ts


def _tri_inv_bwd(ts, gs):
    half = [_dot(t, g, _TN) for t, g in zip(ts, gs)]
    return ([-_dot(h, t, _NT) for h, t in zip(half, ts)],)


_tri_inv.defvjp(_tri_inv_fwd, _tri_inv_bwd)


def _cumsum_rows(x):
    n = x.shape[0]
    row = lax.broadcasted_iota(jnp.int32, x.shape, 0)
    k = 1
    while k < n:
        x = x + jnp.where(row >= k, _shift(x, k), 0.0)
        k *= 2
    return x


def _gdn_prep(qcs, kcs, vcs, tails, alog, dtb):
    C = CHUNK
    pairs = [(c, h) for c in range(len(qcs)) for h in range(4)]
    lane = lax.broadcasted_iota(jnp.int32, (C, 128), 1)
    row = lax.broadcasted_iota(jnp.int32, (C, 128), 0)
    incl = row >= lane
    sig = [_sigmoid(t) for t in tails]
    gfull = [-jnp.exp(alog) * _softplus(t + dtb) for t in tails]
    beta = [jnp.sum(jnp.where(lane == h, sig[c], 0.0), axis=1, keepdims=True) for c, h in pairs]
    g = [jnp.sum(jnp.where(lane == h + 4, gfull[c], 0.0), axis=1, keepdims=True) for c, h in pairs]
    qs = [qcs[c][:, h * 128:(h + 1) * 128] for c, h in pairs]
    ks = [kcs[c][:, h * 128:(h + 1) * 128] for c, h in pairs]
    vs = [vcs[c][:, h * 128:(h + 1) * 128] for c, h in pairs]
    q = [t * lax.rsqrt(jnp.sum(t * t, axis=-1, keepdims=True) + EPS) * (GDN_DK ** -0.5) for t in qs]
    k = [t * lax.rsqrt(jnp.sum(t * t, axis=-1, keepdims=True) + EPS) for t in ks]
    gc = [_cumsum_rows(jnp.broadcast_to(t, (C, 128))) for t in g]
    gc_t = [jnp.transpose(jnp.concatenate([t, t], axis=0)) for t in gc]
    gc_col = [jnp.sum(jnp.where(lane == 0, t, 0.0), axis=1, keepdims=True) for t in gc]
    ri = lax.broadcasted_iota(jnp.int32, (C, C), 0)
    ci = lax.broadcasted_iota(jnp.int32, (C, C), 1)
    decay = [jnp.exp(jnp.where(incl, a - b[:C, :], -1e30)) for a, b in zip(gc, gc_t)]
    decay_sq = [jnp.exp(jnp.where(ri > ci, a - jnp.transpose(b)[:C, :], -1e30)) for a, b in zip(gc_col, gc)]
    kb = [a * b for a, b in zip(k, beta)]
    kk = [_dot(a, b, _NT) for a, b in zip(kb, k)]
    t_mat = _tri_inv([jnp.where(ri > ci, a * b, 0.0) for a, b in zip(kk, decay_sq)])
    egc = [jnp.exp(t) for t in gc]
    u = [_dot(t, a * b) for t, a, b in zip(t_mat, vs, beta)]
    w = [_dot(t, a * b) for t, a, b in zip(t_mat, kb, egc)]
    qk = [_dot(a, _pad_rows2(b), _NT) for a, b in zip(q, k)]
    attn = [jnp.where(incl, a * b, 0.0) for a, b in zip(qk, decay)]
    g_last = [jnp.sum(jnp.where(row == C - 1, t, 0.0), axis=0, keepdims=True) for t in gc]
    qe = [a * b for a, b in zip(q, egc)]
    kd = [a * jnp.exp(b - c_) for a, b, c_ in zip(k, g_last, gc)]
    egl = [jnp.exp(t) for t in g_last]

    def per_chunk(vals):
        return [jnp.concatenate(vals[4 * c:4 * c + 4], axis=1) for c in range(len(qcs))]

    return tuple(per_chunk(t) for t in (u, w, qe, kd, attn, egl))


def _gdn_scan_chunk(states, u, w, qe, kd, attn, egl, z, ng):
    hs = range(4)

    def sl(t, h):
        return t[:, h * 128:(h + 1) * 128]

    ws = [_dot(sl(w, h), states[h]) for h in hs]
    qs = [_dot(sl(qe, h), states[h]) for h in hs]
    v_new = [sl(u, h) - ws[h] for h in hs]
    av = [_dot(sl(attn, h), _pad_rows2(v_new[h])) for h in hs]
    kv = [_dot(sl(kd, h), v_new[h], _TN) for h in hs]
    nxt = tuple(states[h] * sl(egl, h) + kv[h] for h in hs)
    o = [qs[h] + av[h] for h in hs]
    on = [t * lax.rsqrt(jnp.mean(t * t, axis=-1, keepdims=True) + EPS) * ng for t in o]
    return nxt, jnp.concatenate(on, axis=1) * _silu(z)


def _gdn_blocks(T):
    tb = _pick(T, (512, 256, 128, 64))
    return tb, T // tb, tb // CHUNK


PREP_CHUNKS = 4


def _chunk_rows(i, n):
    return [pl.ds(pl.multiple_of((i * n + j) * CHUNK, CHUNK), CHUNK) for j in range(n)]


def _egl_rows(i, n, size):
    return [pl.ds(pl.multiple_of((i * n + j) * 8, 8), size) for j in range(n)]


def _gdn_prep_fwd(name, qa, ka, va, proj, alog, dtb):
    T = proj.shape[0]
    tb, nb, ncb = _gdn_blocks(T)
    n = PREP_CHUNKS if ncb % PREP_CHUNKS == 0 else 1

    def body(q_ref, k_ref, v_ref, tail_ref, alog_ref, dtb_ref, u_ref, w_ref, qe_ref, kd_ref, at_ref, egl_ref):
        def step(i, c):
            rows = _chunk_rows(i, n)
            u, w, qe, kd, at, egl = _gdn_prep([q_ref[r, :] for r in rows], [k_ref[r, :] for r in rows],
                                              [v_ref[r, :] for r in rows], [tail_ref[r, :].astype(F32) for r in rows],
                                              alog_ref[...], dtb_ref[...])
            for j, (r, e) in enumerate(zip(rows, _egl_rows(i, n, 8))):
                u_ref[r, :] = u[j]
                w_ref[r, :] = w[j].astype(BF16)
                qe_ref[r, :] = qe[j].astype(BF16)
                kd_ref[r, :] = kd[j].astype(BF16)
                at_ref[r, :] = at[j].astype(BF16)
                egl_ref[e, :] = jnp.broadcast_to(egl[j], (8, BW))
            return c

        lax.fori_loop(0, ncb // n, step, 0)

    blk = pl.BlockSpec((tb, BW), lambda j: (j, 0))
    vec = pl.BlockSpec((1, 128), lambda j: (0, 0))
    return pl.pallas_call(
        body, name=name, grid=(nb,),
        in_specs=[blk, blk, blk, pl.BlockSpec((tb, 128), lambda j: (j, COL_TAIL // 128)), vec, vec],
        out_specs=[blk] * 5 + [pl.BlockSpec((ncb * 8, BW), lambda j: (j, 0))],
        out_shape=[jax.ShapeDtypeStruct((T, BW), F32)] + [jax.ShapeDtypeStruct((T, BW), BF16)] * 4
        + [jax.ShapeDtypeStruct((T // 8, BW), F32)],
        compiler_params=_cp("parallel"),
    )(qa, ka, va, proj, alog, dtb)


def _gdn_prep_bwd(name, qa, ka, va, proj, alog, dtb, du, dw, dqe, dkd, dat, degl):
    T = proj.shape[0]
    tb, nb, ncb = _gdn_blocks(T)
    n = PREP_CHUNKS if ncb % PREP_CHUNKS == 0 else 1

    def body(q_ref, k_ref, v_ref, tail_ref, alog_ref, dtb_ref, du_ref, dw_ref, dqe_ref, dkd_ref, dat_ref, degl_ref,
             dq_ref, dk_ref, dv_ref, dtail_ref, dalog_ref, ddtb_ref):
        first = pl.program_id(0) == 0

        def step(i, carry):
            pa, pd = carry
            rows = _chunk_rows(i, n)
            _, vjp = jax.vjp(_gdn_prep, [q_ref[r, :] for r in rows], [k_ref[r, :] for r in rows],
                             [v_ref[r, :] for r in rows], [tail_ref[r, :].astype(F32) for r in rows], alog_ref[...], dtb_ref[...])
            cot = tuple([ref[r, :] for r in rows] for ref in (du_ref, dw_ref, dqe_ref, dkd_ref, dat_ref))
            dq, dk, dv, dtail, da, dd = vjp(cot + ([degl_ref[e, :] for e in _egl_rows(i, n, 1)],))
            for j, r in enumerate(rows):
                dq_ref[r, :] = dq[j]
                dk_ref[r, :] = dk[j]
                dv_ref[r, :] = dv[j]
                dtail_ref[r, :] = dtail[j].astype(BF16)
            return pa + da, pd + dd

        zv = jnp.zeros((1, 128), F32)
        pa, pd = lax.fori_loop(0, ncb // n, step, (zv, zv))

        @pl.when(first)
        def _():
            dalog_ref[...] = pa
            ddtb_ref[...] = pd

        @pl.when(jnp.logical_not(first))
        def _():
            dalog_ref[...] += pa
            ddtb_ref[...] += pd

    blk = pl.BlockSpec((tb, BW), lambda j: (j, 0))
    vec = pl.BlockSpec((1, 128), lambda j: (0, 0))
    return pl.pallas_call(
        body, name=name, grid=(nb,),
        in_specs=[blk, blk, blk, pl.BlockSpec((tb, 128), lambda j: (j, COL_TAIL // 128)), vec, vec]
        + [blk] * 5 + [pl.BlockSpec((ncb * 8, BW), lambda j: (j, 0))],
        out_specs=[blk, blk, blk, pl.BlockSpec((tb, 128), lambda j: (j, 0)), vec, vec],
        out_shape=[jax.ShapeDtypeStruct((T, BW), F32)] * 3 + [jax.ShapeDtypeStruct((T, 128), BF16)]
        + [jax.ShapeDtypeStruct((1, 128), F32)] * 2,
        compiler_params=_cp("arbitrary"),
    )(qa, ka, va, proj, alog, dtb, du, dw, dqe, dkd, dat, degl)


def _gdn_fwd(name, u, w, qe, kd, at, egl, proj, ng):
    T = proj.shape[0]
    tb, nb, ncb = _gdn_blocks(T)

    def body(u_ref, w_ref, qe_ref, kd_ref, at_ref, egl_ref, z_ref, ng_ref, y_ref, sh_ref, state):
        @pl.when(pl.program_id(0) == 0)
        def _():
            state[...] = jnp.zeros((4, 128, 128), F32)

        def step(c, states):
            rows = pl.ds(pl.multiple_of(c * CHUNK, CHUNK), CHUNK)
            for h in range(4):
                sh_ref[h, c] = states[h]
            nxt, y = _gdn_scan_chunk(states, u_ref[rows, :], w_ref[rows, :], qe_ref[rows, :], kd_ref[rows, :],
                                     at_ref[rows, :], egl_ref[pl.ds(pl.multiple_of(c * 8, 8), 1), :],
                                     z_ref[rows, :].astype(F32),
                                     ng_ref[...])
            y_ref[rows, :] = y.astype(BF16)
            return nxt

        states = lax.fori_loop(0, ncb, step, tuple(state[h] for h in range(4)))
        for h in range(4):
            state[h] = states[h]

    blk = pl.BlockSpec((tb, BW), lambda j: (j, 0))
    vec = pl.BlockSpec((1, 128), lambda j: (0, 0))
    return pl.pallas_call(
        body, name=name, grid=(nb,),
        in_specs=[blk] * 5 + [pl.BlockSpec((ncb * 8, BW), lambda j: (j, 0)),
                              pl.BlockSpec((tb, BW), lambda j: (j, COL_CZ // BW)), vec],
        out_specs=[blk, pl.BlockSpec((4, ncb, 128, 128), lambda j: (0, j, 0, 0))],
        out_shape=[jax.ShapeDtypeStruct((T, BW), BF16), jax.ShapeDtypeStruct((4, T // CHUNK, 128, 128), F32)],
        scratch_shapes=[pltpu.VMEM((4, 128, 128), F32)],
        compiler_params=_cp("arbitrary"),
    )(u, w, qe, kd, at, egl, proj, ng)


def _gdn_bwd(name, u, w, qe, kd, at, egl, proj, dy, sh, ng):
    T = proj.shape[0]
    tb, nb, ncb = _gdn_blocks(T)

    def body(u_ref, w_ref, qe_ref, kd_ref, at_ref, egl_ref, z_ref, dy_ref, sh_ref, ng_ref,
             du_ref, dw_ref, dqe_ref, dkd_ref, dat_ref, degl_ref, dz_ref, dng_ref, dstate):
        first = pl.program_id(0) == 0

        @pl.when(first)
        def _():
            dstate[...] = jnp.zeros((4, 128, 128), F32)

        def step(it, carry):
            dstates, pn = carry
            c = ncb - 1 - it
            rows = pl.ds(pl.multiple_of(c * CHUNK, CHUNK), CHUNK)
            erow = pl.multiple_of(c * 8, 8)
            _, vjp = jax.vjp(_gdn_scan_chunk, tuple(sh_ref[h, c] for h in range(4)), u_ref[rows, :],
                             w_ref[rows, :].astype(F32), qe_ref[rows, :].astype(F32), kd_ref[rows, :].astype(F32),
                             at_ref[rows, :].astype(F32), egl_ref[pl.ds(erow, 1), :], z_ref[rows, :].astype(F32),
                             ng_ref[...])
            nxt, du, dw, dqe, dkd, dat, degl, dz, dn = vjp((dstates, dy_ref[rows, :]))
            du_ref[rows, :] = du
            dw_ref[rows, :] = dw
            dqe_ref[rows, :] = dqe
            dkd_ref[rows, :] = dkd
            dat_ref[rows, :] = dat
            degl_ref[pl.ds(erow, 8), :] = jnp.broadcast_to(degl, (8, BW))
            dz_ref[rows, :] = dz.astype(BF16)
            return nxt, pn + dn

        dstates, pn = lax.fori_loop(0, ncb, step, (tuple(dstate[h] for h in range(4)), jnp.zeros((1, 128), F32)))
        for h in range(4):
            dstate[h] = dstates[h]

        @pl.when(first)
        def _():
            dng_ref[...] = pn

        @pl.when(jnp.logical_not(first))
        def _():
            dng_ref[...] += pn

    blk = pl.BlockSpec((tb, BW), lambda j: (nb - 1 - j, 0))
    eblk = pl.BlockSpec((ncb * 8, BW), lambda j: (nb - 1 - j, 0))
    vec = pl.BlockSpec((1, 128), lambda j: (0, 0))
    return pl.pallas_call(
        body, name=name, grid=(nb,),
        in_specs=[blk] * 5 + [eblk, pl.BlockSpec((tb, BW), lambda j: (nb - 1 - j, COL_CZ // BW)), blk,
                              pl.BlockSpec((4, ncb, 128, 128), lambda j: (0, nb - 1 - j, 0, 0)), vec],
        out_specs=[blk] * 5 + [eblk, blk, vec],
        out_shape=[jax.ShapeDtypeStruct((T, BW), F32)] * 5 + [jax.ShapeDtypeStruct((T // 8, BW), F32),
                                                              jax.ShapeDtypeStruct((T, BW), BF16),
                                                              jax.ShapeDtypeStruct((1, 128), F32)],
        scratch_shapes=[pltpu.VMEM((4, 128, 128), F32)],
        compiler_params=_cp("arbitrary"),
    )(u, w, qe, kd, at, egl, proj, dy, sh, ng)


def _adamw_update(w_ref, g_ref, m_ref, v_ref, d_ref, nm_ref, nv_ref):
    gv = g_ref[...]
    m2 = ADAM_B1 * m_ref[...] + (1.0 - ADAM_B1) * gv
    v2 = ADAM_B2 * v_ref[...] + (1.0 - ADAM_B2) * jnp.square(gv)
    m_hat = m2 / (1.0 - ADAM_B1 ** ADAM_STEP)
    v_hat = v2 / (1.0 - ADAM_B2 ** ADAM_STEP)
    d_ref[...] = -ADAM_LR * (m_hat / (jnp.sqrt(v_hat) + ADAM_EPS) + ADAM_WD * w_ref[...])
    nm_ref[...] = m2
    nv_ref[...] = v2


def _adamw_many(name, ws, gs, ms, vs):
    n = len(ws)

    def body(*refs):
        for i in range(n):
            _adamw_update(*[refs[k * n + i] for k in range(7)])

    return pl.pallas_call(
        body, name=name,
        out_shape=[jax.ShapeDtypeStruct(a.shape, F32) for a in ws] * 3,
        compiler_params=_cp(),
    )(*ws, *gs, *ms, *vs)


def _adamw(name, w, g, m, v):
    R, C = w.shape
    br = _pick(R, (512, 256, 240, 128, 64, 8))
    body = functools.partial(_adamw_update)
    spec = pl.BlockSpec((br, C), lambda i: (i, 0))
    return pl.pallas_call(
        body, name=name, grid=(R // br,),
        in_specs=[spec] * 4, out_specs=[spec] * 3,
        out_shape=[jax.ShapeDtypeStruct((R, C), F32)] * 3,
        compiler_params=_cp("parallel"),
    )(w, g, m, v)


def _sum8(name, parts):
    _, R, C = parts.shape
    br = _pick(R, (176, 368, 64, 16, 8))

    def body(p_ref, o_ref):
        acc = p_ref[0].astype(F32)
        for d in range(1, N_DEV):
            acc = acc + p_ref[d].astype(F32)
        o_ref[...] = acc

    return pl.pallas_call(
        body, name=name, grid=(R // br,),
        in_specs=[pl.BlockSpec((N_DEV, br, C), lambda i: (0, i, 0))],
        out_specs=pl.BlockSpec((br, C), lambda i: (i, 0)),
        out_shape=jax.ShapeDtypeStruct((R, C), F32),
        compiler_params=_cp("parallel"),
    )(parts)


_ANY = pl.BlockSpec(memory_space=pl.ANY)
_MESH = pl.DeviceIdType.MESH


def _all_gather(name, shard):
    R, C = shard.shape

    def body(x_ref, out_ref, send_sems, recv_sems, local_sem):
        x, y, c = lax.axis_index("x"), lax.axis_index("y"), lax.axis_index("c")
        me, sibling = (x, y, c), (x, y, 1 - c)
        chips = [(1 - x, y), (x, 1 - y), (1 - x, 1 - y)]

        def slot(px, py, pc):
            return out_ref.at[4 * px + 2 * py + pc]

        def copy(k, block, to, src=None):
            return pltpu.make_async_remote_copy(
                src_ref=slot(*block) if src is None else src, dst_ref=slot(*block),
                send_sem=send_sems.at[k], recv_sem=recv_sems.at[k], device_id=to, device_id_type=_MESH)

        mine = pltpu.make_async_copy(x_ref, slot(*me), local_sem)
        mine.start()
        first = [copy(0, me, sibling, src=x_ref)]
        first += [copy(1 + j, me, (*chip, c), src=x_ref) for j, chip in enumerate(chips)]
        for cp in first:
            cp.start()
        passed = [copy(4 + j, (*chip, c), sibling) for j, chip in enumerate(chips)]
        for j, chip in enumerate(chips):
            copy(1 + j, (*chip, c), me).wait_recv()
            passed[j].start()
        copy(0, sibling, me).wait_recv()
        for j, chip in enumerate(chips):
            copy(4 + j, (*chip, 1 - c), me).wait_recv()
        for cp in first + passed:
            cp.wait_send()
        mine.wait()

    return pl.pallas_call(
        body, name=name,
        in_specs=[_ANY], out_specs=_ANY,
        out_shape=jax.ShapeDtypeStruct((N_DEV, R, C), shard.dtype),
        scratch_shapes=[pltpu.SemaphoreType.DMA((7,)), pltpu.SemaphoreType.DMA((7,)), pltpu.SemaphoreType.DMA],
    )(shard)


_HBM = pl.BlockSpec(memory_space=pltpu.HBM)
_SEM = pl.BlockSpec(memory_space=pltpu.SEMAPHORE)
_EFFECT = pltpu.SideEffectType.DATAFLOW_SIDE_EFFECTING


def _exchange_copies(src_ref, land_ref, send_sems, recv_sems, scatter):
    x, y, c = lax.axis_index("x"), lax.axis_index("y"), lax.axis_index("c")
    me = 4 * x + 2 * y + c
    copies = []
    for k in range(1, N_DEV):
        px, py, pc = x ^ ((k >> 2) & 1), y ^ ((k >> 1) & 1), c ^ (k & 1)
        src = src_ref.at[4 * px + 2 * py + pc] if scatter else src_ref
        copies.append(pltpu.make_async_remote_copy(
            src_ref=src, dst_ref=land_ref.at[me], send_sem=send_sems.at[k - 1], recv_sem=recv_sems.at[k - 1],
            device_id=(px, py, pc), device_id_type=_MESH))
    return copies


def _own_copy(src_ref, land_ref, send_sems, scatter):
    me = 4 * lax.axis_index("x") + 2 * lax.axis_index("y") + lax.axis_index("c")
    return pltpu.make_async_copy(src_ref.at[me] if scatter else src_ref, land_ref.at[me], send_sems.at[N_DEV - 1])


def _exchange_start(name, srcs, lands, scatter, after=None):
    n = len(srcs)

    def body(*refs):
        src_refs, land_refs = refs[:n], refs[n:2 * n]
        outs = refs[2 * n + (after is not None):]
        send, recv = outs[:n], outs[n:2 * n]
        token = refs[-1]
        for g in range(n):
            for cp in _exchange_copies(src_refs[g], land_refs[g], send[g], recv[g], scatter):
                cp.start()
            _own_copy(src_refs[g], land_refs[g], send[g], scatter).start()
        token[...] = jnp.zeros_like(token)

    outs = pl.pallas_call(
        body, name=name,
        out_shape=tuple([pltpu.SemaphoreType.DMA((N_DEV,))] * (2 * n)
                        + [pltpu.HBM(a.shape, a.dtype) for a in list(srcs) + list(lands)]
                        + [jax.ShapeDtypeStruct((8, 128), F32)]),
        in_specs=[_HBM] * (2 * n) + [_ANY] * (after is not None),
        out_specs=tuple([_SEM] * (2 * n) + [_HBM] * (2 * n) + [pl.BlockSpec(memory_space=pltpu.VMEM)]),
        input_output_aliases={i: 2 * n + i for i in range(2 * n)},
        compiler_params=pltpu.CompilerParams(has_side_effects=_EFFECT),
    )(*[pltpu.with_memory_space_constraint(a, pltpu.HBM) for a in list(srcs) + list(lands)],
      *([after] if after is not None else []))
    handles = [(outs[2 * n + g], outs[3 * n + g], outs[g], outs[n + g]) for g in range(n)]
    return handles, outs[-1]


def _exchange_wait(name, handles, after, scatter):
    n = len(handles)
    srcs, lands, sends, recvs = ([h[i] for h in handles] for i in range(4))

    def body(*refs):
        src_refs, land_refs = refs[:n], refs[n:2 * n]
        send, recv = refs[2 * n:3 * n], refs[3 * n:4 * n]
        for g in range(n):
            for cp in _exchange_copies(src_refs[g], land_refs[g], send[g], recv[g], scatter):
                cp.wait_send()
                cp.wait_recv()
            _own_copy(src_refs[g], land_refs[g], send[g], scatter).wait()

    outs = pl.pallas_call(
        body, name=name,
        out_shape=tuple(pltpu.HBM(a.shape, a.dtype) for a in srcs + lands),
        in_specs=tuple([_HBM] * (2 * n) + [_SEM] * (2 * n) + [_ANY]), out_specs=tuple([_HBM] * (2 * n)),
        input_output_aliases={i: i for i in range(2 * n)},
        compiler_params=pltpu.CompilerParams(has_side_effects=_EFFECT),
    )(*srcs, *lands, *sends, *recvs, after)
    return list(outs[n:])


def _rows(a):
    return a.reshape(-1, 1024)


def _rows_to_parts(full):
    n = full.shape[-2] // N_DEV
    t = full.reshape(full.shape[:-2] + (N_DEV, n, full.shape[-1]))
    return jnp.moveaxis(t, -3, 0)


def _parts_to_rows(parts):
    t = jnp.moveaxis(parts, 0, -3)
    return t.reshape(t.shape[:-3] + (t.shape[-3] * t.shape[-2], t.shape[-1]))


def _parts_to_cols(parts):
    t = jnp.moveaxis(parts, 0, -2)
    return t.reshape(t.shape[:-2] + (t.shape[-2] * t.shape[-1],))


def _join(parts, axis=0):
    total = sum(p.shape[axis] for p in parts)
    out, off = None, 0
    for p in parts:
        cfg = [(0, 0)] * p.ndim
        cfg[axis] = (off, total - off - p.shape[axis])
        t = jnp.pad(p, cfg)
        out = t if out is None else out + t
        off += p.shape[axis]
    return out


def _w_in_to_layout(w):
    tail = jnp.pad(w[4096:4104], ((0, PW - COL_TAIL - 8), (0, 0)))
    return jnp.concatenate([w[:4096], w[4104:P_IN], tail], axis=0)


def _w_in_from_layout(g):
    return _join([g[:4096], g[COL_TAIL:COL_TAIL + 8], g[4096:COL_TAIL]], axis=0)


def _block_diag(w):
    w = w.reshape(4, 2, 64, 64)
    return jnp.pad(w[:, 0], ((0, 0), (0, 64), (0, 64))) + jnp.pad(w[:, 1], ((0, 0), (64, 0), (64, 0)))


def _block_diag_grad(g):
    return jnp.stack([g[:, :64, :64], g[:, 64:, 64:]], axis=1).reshape(8, 64, 64)


def _ffn_forward(tag, x, norm, wg, wu, wd):
    h = _rms_fwd(tag + "_norm", x, norm)
    sa, ds, act = _ffn_up(tag + "_up", h, wg, wu)
    if callable(wd):
        wd = wd(act)
    x_out = _mm(tag + "_down", [(act, wd)], "nn", F32, res=x, scale=0.5)
    return x_out, (x, h, sa, ds, act), wd


def _ffn_backward(tag, dx_out, saved, norm, wg, wu, wd, put, names, split=False):
    x, h, sa, ds, act = saved
    n_wg, n_wu, n_wd = names
    dwd = _mm(tag + "_dwd", [(act, dx_out)], "tn", BF16, scale=0.5, bm=FF // 2)
    tok = put({n_wd: dwd}) if split else None
    da, db = _ffn_dact(tag + "_dact", dx_out, wd, sa, ds, after=tok)
    dwg = _mm(tag + "_dwg", [(da, h)], "tn", BF16, bm=FF // 2)
    if split:
        tok = tok + put({n_wg: dwg})
    dwu = _mm(tag + "_dwu", [(db, h)], "tn", BF16, bm=FF // 2, after=tok)
    tok = tok + put({n_wu: dwu}) if split else put({n_wg: dwg, n_wu: dwu, n_wd: dwd})
    dh = _mm(tag + "_dh", [(da, wg), (db, wu)], "nn", F32, after=tok)
    dx, dnorm = _rms_bwd(tag + "_dnorm", x, norm + tok, dh, dx_out)
    return dx, dnorm


def _mixer_params(p):
    alog = jnp.pad(p["gdn_a_log"], (4, 120))[None]
    dtb = jnp.pad(p["gdn_dt_bias"], (4, 120))[None]
    bias = jnp.repeat(p["sgu_b"].T, 128, axis=1)
    return dict(
        ln_g=p["sgu_ln_g"][None], ln_b=p["sgu_ln_b"][None], sgu_w=p["sgu_w"], sgu_bias=bias,
        lru_cw=p["lru_conv_w"], lru_cb=p["lru_conv_b"][None], wa=_block_diag(p["lru_wa"]), ba=p["lru_ba"][None],
        wx=_block_diag(p["lru_wx"]), bx=p["lru_bx"][None], lam=p["lru_lambda"][None],
        gdn_cw=p["gdn_conv_w"], alog=alog, dtb=dtb, ng=p["gdn_norm_g"][None],
        pool_w=p["pool_w"], pool_sc=p["pool_scale"][None])


def _mix_forward(tag, x, p, mp):
    h = _rms_fwd(tag + "_norm", x, p["mix_norm"][None])
    proj = _mm(tag + "_proj", [(h, p["w_in"])], "nt", BF16, bm=_pick(x.shape[0], (2048, 1024, 512, 256, 128)))
    y_a = _sgu_fwd(tag + "_sgu", proj, mp["ln_g"], mp["ln_b"], mp["sgu_w"], mp["sgu_bias"])
    y_b, hc = _lru_fwd(tag + "_lru", proj, mp["lru_cw"], mp["lru_cb"], mp["wa"], mp["ba"], mp["wx"], mp["bx"],
                       mp["lam"])
    qa = _conv_fwd(tag + "_convq", proj, COL_CQ, mp["gdn_cw"], 0)
    ka = _conv_fwd(tag + "_convk", proj, COL_CK, mp["gdn_cw"], 512)
    va = _conv_fwd(tag + "_convv", proj, COL_CV, mp["gdn_cw"], 1024)
    prep = _gdn_prep_fwd(tag + "_gdnprep", qa, ka, va, proj, mp["alog"], mp["dtb"])
    y_c, sh = _gdn_fwd(tag + "_gdn", *prep, proj, mp["ng"])
    y_d = _pool_fwd(tag + "_pool", proj, mp["pool_w"], mp["pool_sc"])
    ys = (y_a, y_b, y_c, y_d)
    if callable(p["w_branch"]):
        p["w_branch"] = p["w_branch"](y_d)
    merged = _merge_fwd(tag + "_merge", ys, p["w_branch"], proj)
    if callable(p["w_out"]):
        p["w_out"] = p["w_out"](merged)
    x_out = _mm(tag + "_out", [(merged, p["w_out"])], "nn", F32, res=x)
    return x_out, (x, h, proj, hc, qa, ka, va, prep, sh, ys, merged)


def _mix_backward(tag, dx_out, saved, p, mp, put):
    x, h, proj, hc, qa, ka, va, prep, sh, ys, merged = saved
    T = x.shape[0]
    g = {}
    dmerged = _mm(tag + "_dmerged", [(dx_out, p["w_out"])], "nt", F32)
    g["w_out"] = _mm(tag + "_dwout", [(merged, dx_out)], "tn", BF16)
    outs = _merge_bwd(tag + "_dmerge", dmerged, ys, p["w_branch"], proj)
    dgates, dbrs, dys = outs[:NBR], outs[NBR:2 * NBR], outs[2 * NBR:]
    g["w_branch"] = _dwb(tag + "_dwb", dbrs, ys)

    du, dv, dln_g, dln_b, dsgu_w, dbias = _sgu_bwd(tag + "_dsgu", proj, dys[0], mp["ln_g"], mp["ln_b"], mp["sgu_w"],
                                                  mp["sgu_bias"])
    g["sgu_ln_g"], g["sgu_ln_b"], g["sgu_w"] = dln_g[0], dln_b[0], dsgu_w
    g["sgu_b"] = dbias.reshape(128, 4, 128).sum(axis=2).T

    (dbx, dbg, dcw, dcb, dwa, dba, dwx, dbxb, dlam) = _lru_bwd(
        tag + "_dlru", proj, dys[1], hc, mp["lru_cw"], mp["lru_cb"], mp["wa"], mp["ba"], mp["wx"], mp["bx"], mp["lam"])
    g["lru_conv_w"], g["lru_conv_b"], g["lru_ba"], g["lru_bx"], g["lru_lambda"] = dcw, dcb[0], dba[0], dbxb[0], dlam[0]
    g["lru_wa"], g["lru_wx"] = _block_diag_grad(dwa), _block_diag_grad(dwx)

    *dprep, dz, dng = _gdn_bwd(tag + "_dgdn", *prep, proj, dys[2], sh, mp["ng"])
    dqa, dka, dva, dtail, dalog, ddtb = _gdn_prep_bwd(tag + "_dgdnprep", qa, ka, va, proj, mp["alog"], mp["dtb"], *dprep)
    g["gdn_a_log"], g["gdn_dt_bias"], g["gdn_norm_g"] = dalog[0, 4:8], ddtb[0, 4:8], dng[0]
    dq, dcwq = _conv_bwd(tag + "_dconvq", proj, COL_CQ, dqa, mp["gdn_cw"], 0)
    dk, dcwk = _conv_bwd(tag + "_dconvk", proj, COL_CK, dka, mp["gdn_cw"], 512)
    dv_, dcwv = _conv_bwd(tag + "_dconvv", proj, COL_CV, dva, mp["gdn_cw"], 1024)
    g["gdn_conv_w"] = jnp.concatenate([dcwq, dcwk, dcwv], axis=1)

    dd, dpw, dsc = _pool_bwd(tag + "_dpool", proj, dys[3], mp["pool_w"], mp["pool_sc"])
    g["pool_w"], g["pool_scale"] = dpw, dsc[0]

    dproj = jnp.concatenate([du, dv, dbx, dbg, dq, dk, dv_, dz, dd, *dgates, dtail,
                             jnp.zeros((T, PW - COL_TAIL - 128), BF16)], axis=1)
    dw_in = _mm(tag + "_dwin", [(dproj, h)], "tn", BF16)
    tok = put(dict(w_in=_w_in_from_layout(dw_in), w_branch=g.pop("w_branch"), w_out=g.pop("w_out")))
    dh = _mm(tag + "_dh", [(dproj, p["w_in"])], "nn", F32, bm=_pick(T, (2048, 1024, 512, 256, 128)), after=tok)
    dx, dnorm = _rms_bwd(tag + "_dnorm", x, p["mix_norm"][None] + tok, dh, dx_out)
    g["mix_norm"] = dnorm[0]
    return dx, g


_BIG = ("ff1_wg", "ff1_wu", "ff1_wd", "w_in", "w_branch", "w_out", "ff2_wg", "ff2_wu", "ff2_wd")
_COL_SHARDED = ("ff1_wg", "ff1_wu", "w_in", "w_branch", "ff2_wg", "ff2_wu")
_SMALL = ("ff1_norm", "mix_norm", "sgu_ln_g", "sgu_ln_b", "sgu_w", "sgu_b", "lru_conv_w", "lru_conv_b", "lru_wa",
          "lru_ba", "lru_wx", "lru_bx", "lru_lambda", "gdn_conv_w", "gdn_a_log", "gdn_dt_bias", "gdn_norm_g", "pool_w",
          "pool_scale", "ff2_norm", "final_norm")
_WEIGHTS = ("ff1_norm", "ff1_wg", "ff1_wu", "ff1_wd", "mix_norm", "w_in", "sgu_ln_g", "sgu_ln_b", "sgu_w", "sgu_b",
            "lru_conv_w", "lru_conv_b", "lru_wa", "lru_ba", "lru_wx", "lru_bx", "lru_lambda", "gdn_conv_w", "gdn_a_log",
            "gdn_dt_bias", "gdn_norm_g", "pool_w", "pool_scale", "w_branch", "w_out", "ff2_norm", "ff2_wg", "ff2_wu",
            "ff2_wd", "final_norm")
_CONV_SHARDED = ("lru_conv_w", "gdn_conv_w")
PACK_ROW_ALIGN = 16
_GROUPS = (("ff1", ("ff1_wg", "ff1_wu", "ff1_wd")), ("mix", ("w_in", "w_branch", "w_out")),
           ("ff2", ("ff2_wg", "ff2_wu", "ff2_wd")))


def _pad_rows(a, mult):
    pad = (-a.shape[-2]) % mult
    if pad == 0:
        return a
    return jnp.pad(a, [(0, 0)] * (a.ndim - 2) + [(0, pad), (0, 0)])


def _my_index():
    return 4 * lax.axis_index("x") + 2 * lax.axis_index("y") + lax.axis_index("c")


def _landing(shape, dtype):
    return lax.empty((N_DEV,) + tuple(shape), dtype)


def _stored(n, a):
    return jnp.swapaxes(a, -1, -2) if n in _COL_SHARDED else a


_FIRST = ("ff1_wg", "ff1_wu", "ff1_wd")


def _gather_first(w):
    names = _FIRST
    shards = [_rows(_stored(n, w[n][0]).astype(BF16)) for n in names]
    got = _all_gather("gather_first", jnp.concatenate(shards, axis=0))
    out, r = {}, 0
    for n, s in zip(names, shards):
        out[n] = got[:, r:r + s.shape[0]].reshape(-1, 1024)
        r += s.shape[0]
    return out, got


def _gather_start(w, after):
    conv = _pad_rows(jnp.concatenate([w[n].reshape(1, -1) for n in _CONV_SHARDED], axis=1), 8)
    keys, srcs = ["conv"], [conv]
    for l in range(2):
        for sub, (_, names) in enumerate(_GROUPS):
            for n in names:
                if l > 0 or n not in _FIRST:
                    keys.append((l, sub, n))
                    srcs.append(_stored(n, w[n][l]).astype(BF16))
    lands = [_landing(s.shape, s.dtype) for s in srcs]
    handles, token = _exchange_start("gather_start", srcs, lands, scatter=False, after=after)
    return dict(zip(keys, handles)), token


def _gather_finish(l, sub, handles, first, after):
    names = _GROUPS[sub][1]
    if (l, sub) == (0, 0):
        out = dict(first)
        for n in names:
            if n not in _FIRST:
                out[n] = lambda later, n=n: _parts_to_rows(
                    _exchange_wait(f"gather_wait_00_{n}", [handles[(0, 0, n)]], later, scatter=False)[0])
    elif sub == 1:
        out = {n: (lambda later, n=n: _parts_to_rows(
            _exchange_wait(f"gather_wait_{l}{sub}_{n}", [handles[(l, sub, n)]], later, scatter=False)[0])) for n in names}
        out["w_in"] = out["w_in"](after)
    else:
        lands = _exchange_wait(f"gather_wait_{l}{sub}", [handles[(l, sub, n)] for n in names], after, scatter=False)
        out = {n: _parts_to_rows(land) for n, land in zip(names, lands)}
    if "w_in" in out:
        out["w_in"] = _w_in_to_layout(out["w_in"])
    return out


def _scatter_start(l, sub, grads):
    srcs, shapes = [], []
    for n in grads:
        parts = _rows_to_parts(grads[n])
        shapes.append(parts.shape[1:])
        srcs.append(_pad_rows(parts.reshape(N_DEV, -1, 1024), PACK_ROW_ALIGN))
    lands = [_landing(s.shape[1:], s.dtype) for s in srcs]
    tag = f"{l}{sub}" + ("" if len(grads) == len(_GROUPS[sub][1]) else "_" + "_".join(grads))
    handles, token = _exchange_start(f"scatter_start_{tag}", srcs, lands, scatter=True)
    return handles, (tag, tuple(grads), shapes), token


def _scatter_finish(l, sub, handles, meta, after):
    tag, names, shapes = meta
    lands = _exchange_wait(f"scatter_wait_{tag}", handles, after, scatter=True)
    out = {}
    for n, land, shape in zip(names, lands, shapes):
        size = 1
        for s in shape:
            size *= s
        summed = _sum8(f"sum_{l}{sub}_{n}", land)
        out[n] = _stored(n, summed[:size // 1024].reshape(shape))
    return out


def _gather_conv_finish(w, handles, after):
    gconv = _exchange_wait("gather_wait_conv", [handles["conv"]], after, scatter=False)[0][:, 0]
    full, r = {}, 0
    for n in _CONV_SHARDED:
        sz = w[n].size
        full[n] = _parts_to_cols(gconv[:, r:r + sz].reshape((N_DEV,) + w[n].shape))
        r += sz
    return full


def _forward_backward(x, tgt, w, conv, get_weights, put_grads, put_small, token):
    saved, params = [], []
    for l in range(2):
        p = {n: w[n][l] for n in _SMALL if n != "final_norm"}
        for n in _CONV_SHARDED:
            p[n] = conv[n][l]
        mp = _mixer_params(p)
        tok = token[:1, :1] if l == 0 else 0.0
        p.update(get_weights(l, 0, x))
        x, s1, p["ff1_wd"] = _ffn_forward(f"l{l}_ff1", x, p["ff1_norm"][None] + tok, p["ff1_wg"], p["ff1_wu"],
                                          p["ff1_wd"])
        p.update(get_weights(l, 1, x))
        x, s2 = _mix_forward(f"l{l}_mix", x, p, mp)
        p.update(get_weights(l, 2, x))
        x, s3, _ = _ffn_forward(f"l{l}_ff2", x, p["ff2_norm"][None], p["ff2_wg"], p["ff2_wu"], p["ff2_wd"])
        saved.append((s1, s2, s3))
        params.append((p, mp))
    loss, dx, dfinal = _final_loss("loss_head", x, w["final_norm"][None], tgt)
    tok = 0.0
    for l in (1, 0):
        p, mp = params[l]
        s1, s2, s3 = saved[l]
        g = {}

        def put(sub):
            return lambda grads, l=l: put_grads(l, sub, grads)[:1, :1]

        dx, dn = _ffn_backward(f"l{l}_ff2", dx, s3, p["ff2_norm"][None] + tok, p["ff2_wg"], p["ff2_wu"], p["ff2_wd"],
                               put(2), _GROUPS[2][1])
        g["ff2_norm"] = dn[0]
        dx, gm = _mix_backward(f"l{l}_mix", dx, s2, p, mp, put(1))
        g.update(gm)
        tok = 0.0
        if l == 0:
            keep = {n: g.pop(n) for n in ("gdn_a_log", "gdn_dt_bias")}
            tok = put_small("0a", g, True)[:1, :1]
            g = keep
        dx, dn = _ffn_backward(f"l{l}_ff1", dx, s1, p["ff1_norm"][None] + tok, p["ff1_wg"], p["ff1_wu"], p["ff1_wd"],
                               put(0), _GROUPS[0][1], split=(l == 0))
        g["ff1_norm"] = dn[0]
        if l == 1:
            g["final_norm"] = dfinal[0]
            g["loss"] = loss[0, :1]
        tok = put_small("1" if l == 1 else "0b", g, False)[:1, :1]
    return dx


SMALL_PIECE = 8 * 1024


def _pack_small(d, names):
    pieces = []
    for n in names:
        flat = d[n].reshape(-1)
        pieces.append(jnp.pad(flat, (0, (-flat.size) % SMALL_PIECE)).reshape(-1, 1024))
    return jnp.concatenate(pieces, axis=0)


def _unpack_small(pack, shapes, names):
    out, r = {}, 0
    for n in names:
        size = 1
        for s in shapes[n]:
            size *= s
        rows = -(-size // SMALL_PIECE) * 8
        out[n] = pack[r:r + rows].reshape(-1)[:size].reshape(shapes[n])
        r += rows
    return out


def _small_names(grads):
    return tuple(n for n in _SMALL + ("loss",) if n in grads)


def _small_start(tag, grads, narrow):
    pack = _pack_small(grads, _small_names(grads))
    if narrow:
        pack = _pad_rows(pack.astype(BF16), PACK_ROW_ALIGN)
    handles, token = _exchange_start(f"small_start_{tag}", [pack], [_landing(pack.shape, pack.dtype)], scatter=False)
    return handles, {n: grads[n].shape for n in _small_names(grads)}, token


def _small_finish(tag, handles, shapes, after):
    landed = _exchange_wait(f"small_wait_{tag}", handles, after, scatter=False)[0]
    return _unpack_small(_sum8(f"sum_small_{tag}", landed), shapes, _small_names(shapes))


def _as2d(a):
    if a.ndim == 1:
        return a.reshape(1, -1)
    return a.reshape(-1, a.shape[-1])


def kernel(x, ff1_norm, ff1_wg, ff1_wu, ff1_wd, mix_norm, w_in, sgu_ln_g, sgu_ln_b, sgu_w, sgu_b, lru_conv_w, lru_conv_b, lru_wa, lru_ba, lru_wx, lru_bx, lru_lambda, gdn_conv_w, gdn_a_log, gdn_dt_bias, gdn_norm_g, pool_w, pool_scale, w_branch, w_out, ff2_norm, ff2_wg, ff2_wu, ff2_wd, final_norm, loss_target, m_ff1_norm, m_ff1_wg, m_ff1_wu, m_ff1_wd, m_mix_norm, m_w_in, m_sgu_ln_g, m_sgu_ln_b, m_sgu_w, m_sgu_b, m_lru_conv_w, m_lru_conv_b, m_lru_wa, m_lru_ba, m_lru_wx, m_lru_bx, m_lru_lambda, m_gdn_conv_w, m_gdn_a_log, m_gdn_dt_bias, m_gdn_norm_g, m_pool_w, m_pool_scale, m_w_branch, m_w_out, m_ff2_norm, m_ff2_wg, m_ff2_wu, m_ff2_wd, m_final_norm, v_ff1_norm, v_ff1_wg, v_ff1_wu, v_ff1_wd, v_mix_norm, v_w_in, v_sgu_ln_g, v_sgu_ln_b, v_sgu_w, v_sgu_b, v_lru_conv_w, v_lru_conv_b, v_lru_wa, v_lru_ba, v_lru_wx, v_lru_bx, v_lru_lambda, v_gdn_conv_w, v_gdn_a_log, v_gdn_dt_bias, v_gdn_norm_g, v_pool_w, v_pool_scale, v_w_branch, v_w_out, v_ff2_norm, v_ff2_wg, v_ff2_wu, v_ff2_wd, v_final_norm):
    w = dict(ff1_norm=ff1_norm, ff1_wg=ff1_wg, ff1_wu=ff1_wu, ff1_wd=ff1_wd, mix_norm=mix_norm, w_in=w_in,
             sgu_ln_g=sgu_ln_g, sgu_ln_b=sgu_ln_b, sgu_w=sgu_w, sgu_b=sgu_b, lru_conv_w=lru_conv_w,
             lru_conv_b=lru_conv_b, lru_wa=lru_wa, lru_ba=lru_ba, lru_wx=lru_wx, lru_bx=lru_bx, lru_lambda=lru_lambda,
             gdn_conv_w=gdn_conv_w, gdn_a_log=gdn_a_log, gdn_dt_bias=gdn_dt_bias, gdn_norm_g=gdn_norm_g, pool_w=pool_w,
             pool_scale=pool_scale, w_branch=w_branch, w_out=w_out, ff2_norm=ff2_norm, ff2_wg=ff2_wg, ff2_wu=ff2_wu,
             ff2_wd=ff2_wd, final_norm=final_norm)
    m = dict(ff1_norm=m_ff1_norm, ff1_wg=m_ff1_wg, ff1_wu=m_ff1_wu, ff1_wd=m_ff1_wd, mix_norm=m_mix_norm, w_in=m_w_in,
             sgu_ln_g=m_sgu_ln_g, sgu_ln_b=m_sgu_ln_b, sgu_w=m_sgu_w, sgu_b=m_sgu_b, lru_conv_w=m_lru_conv_w,
             lru_conv_b=m_lru_conv_b, lru_wa=m_lru_wa, lru_ba=m_lru_ba, lru_wx=m_lru_wx, lru_bx=m_lru_bx,
             lru_lambda=m_lru_lambda, gdn_conv_w=m_gdn_conv_w, gdn_a_log=m_gdn_a_log, gdn_dt_bias=m_gdn_dt_bias,
             gdn_norm_g=m_gdn_norm_g, pool_w=m_pool_w, pool_scale=m_pool_scale, w_branch=m_w_branch, w_out=m_w_out,
             ff2_norm=m_ff2_norm, ff2_wg=m_ff2_wg, ff2_wu=m_ff2_wu, ff2_wd=m_ff2_wd, final_norm=m_final_norm)
    v = dict(ff1_norm=v_ff1_norm, ff1_wg=v_ff1_wg, ff1_wu=v_ff1_wu, ff1_wd=v_ff1_wd, mix_norm=v_mix_norm, w_in=v_w_in,
             sgu_ln_g=v_sgu_ln_g, sgu_ln_b=v_sgu_ln_b, sgu_w=v_sgu_w, sgu_b=v_sgu_b, lru_conv_w=v_lru_conv_w,
             lru_conv_b=v_lru_conv_b, lru_wa=v_lru_wa, lru_ba=v_lru_ba, lru_wx=v_lru_wx, lru_bx=v_lru_bx,
             lru_lambda=v_lru_lambda, gdn_conv_w=v_gdn_conv_w, gdn_a_log=v_gdn_a_log, gdn_dt_bias=v_gdn_dt_bias,
             gdn_norm_g=v_gdn_norm_g, pool_w=v_pool_w, pool_scale=v_pool_scale, w_branch=v_w_branch, w_out=v_w_out,
             ff2_norm=v_ff2_norm, ff2_wg=v_ff2_wg, ff2_wu=v_ff2_wu, ff2_wd=v_ff2_wd, final_norm=v_final_norm)

    first, got_first = _gather_first(w)
    handles, token = _gather_start(w, got_first)
    conv = _gather_conv_finish(w, handles, token)
    pending = {}

    def get_weights(l, sub, after):
        return _gather_finish(l, sub, handles, first, after)

    def put_grads(l, sub, grads):
        hs, meta, tok = _scatter_start(l, sub, grads)
        pending[(l, sub, meta[0])] = (hs, meta)
        return tok

    def put_small(tag, grads, narrow):
        hs, shapes, tok = _small_start(tag, grads, narrow)
        pending[tag] = (hs, shapes)
        return tok

    T = x.shape[1]
    dx = _forward_backward(x.reshape(T, D), loss_target.reshape(T, D), w, conv, get_weights, put_grads, put_small,
                           token)
    per = {}
    for key in pending:
        if isinstance(key, tuple):
            per.setdefault(key[:2], {}).update(_scatter_finish(*key[:2], *pending[key], dx))
        else:
            per[key] = _small_finish(key, *pending[key], dx)
    grad = {n: jnp.stack([per[(0, sub)][n], per[(1, sub)][n]]) for sub, (_, names) in enumerate(_GROUPS) for n in names}
    layer0 = {**per["0a"], **per["0b"]}
    small = {n: _join([layer0[n].reshape(-1), per["1"][n].reshape(-1)]).reshape((2,) + layer0[n].shape)
             for n in layer0}
    small["final_norm"] = per["1"]["final_norm"]
    loss = per["1"]["loss"][0]
    me = _my_index()
    for n in _SMALL:
        if n in _CONV_SHARDED:
            width = w[n].shape[-1]
            grad[n] = lax.dynamic_slice_in_dim(small[n], me * width, width, axis=2)
        else:
            grad[n] = small[n]

    delta, new_m, new_v = {}, {}, {}
    for n in _BIG:
        d_, m_, v_ = _adamw("adamw_" + n, _as2d(w[n]), _as2d(grad[n]), _as2d(m[n]), _as2d(v[n]))
        delta[n], new_m[n], new_v[n] = (t.reshape(w[n].shape) for t in (d_, m_, v_))

    outs = _adamw_many("adamw_small", *[[_as2d(t[n]) for n in _SMALL] for t in (w, grad, m, v)])
    for k, dst in enumerate((delta, new_m, new_v)):
        for i, n in enumerate(_SMALL):
            dst[n] = outs[k * len(_SMALL) + i].reshape(w[n].shape)

    return (loss, dx.reshape(x.shape), *[grad[n] for n in _WEIGHTS], *[delta[n] for n in _WEIGHTS],
            *[new_m[n] for n in _WEIGHTS], *[new_v[n] for n in _WEIGHTS])
```

```python
import functools

import jax
import jax.numpy as jnp
from jax import lax
from jax.experimental import pallas as pl
from jax.experimental.pallas import tpu as pltpu

F32 = jnp.float32
BF16 = jnp.bfloat16
HI = lax.Precision.HIGHEST

N_DEV = 8
D = 1024
FF = 2816
BW = 512
NBR = 4
CHUNK = 64
EPS = 1e-6
LRU_C = 8.0
GDN_DK = 128

COL_AU, COL_AV, COL_BX, COL_BG = 0, 512, 1024, 1536
COL_CQ, COL_CK, COL_CV, COL_CZ = 2048, 2560, 3072, 3584
COL_DX, COL_GATE, COL_TAIL = 4096, 4608, 8704
PW = 9216
P_IN = 8712

ADAM_LR, ADAM_B1, ADAM_B2, ADAM_EPS, ADAM_WD, ADAM_STEP = 0.001, 0.9, 0.999, 1e-08, 0.01, 10

VMEM_LIMIT_V7X = 56 * 1024 * 1024

_NN = (((1,), (0,)), ((), ()))
_NT = (((1,), (1,)), ((), ()))
_TN = (((0,), (0,)), ((), ()))


def _cp(*sem):
    return pltpu.CompilerParams(dimension_semantics=tuple(sem), vmem_limit_bytes=VMEM_LIMIT_V7X)


def _dot(a, b, dims=_NN):
    return lax.dot_general(a.astype(BF16), b.astype(BF16), dims, preferred_element_type=F32)


def _dot_hi(a, b, dims=_NN):
    return lax.dot_general(a, b, dims, precision=HI, preferred_element_type=F32)


def _pick(n, cands):
    for c in cands:
        if n % c == 0:
            return c
    return n


@jax.custom_jvp
def _log1p(x):
    u = 1.0 + x
    return jnp.where(u == 1.0, x, x * jnp.log(u) / jnp.where(u == 1.0, 1.0, u - 1.0))


@_log1p.defjvp
def _log1p_jvp(p, t):
    (x,), (dx,) = p, t
    return _log1p(x), dx / (1.0 + x)


@jax.custom_jvp
def _expm1(x):
    u = jnp.exp(x)
    lu = jnp.log(u)
    small = (u == 1.0) | (lu == 0.0)
    return jnp.where(small, x, (u - 1.0) * x / jnp.where(small, 1.0, lu))


@_expm1.defjvp
def _expm1_jvp(p, t):
    (x,), (dx,) = p, t
    return _expm1(x), dx * jnp.exp(x)


def _softplus(x):
    return jnp.maximum(x, 0.0) + _log1p(jnp.exp(-jnp.abs(x)))


def _sigmoid(x):
    return jax.nn.sigmoid(x)


def _silu(x):
    return x * jax.nn.sigmoid(x)


def _gelu(x):
    return jax.nn.gelu(x)


@functools.partial(jax.custom_vjp, nondiff_argnums=(1,))
def _shift(x, s):
    return x if s == 0 else pltpu.roll(x, s, 0)


def _shift_fwd(x, s):
    return _shift(x, s), None


def _shift_bwd(s, _, g):
    n = g.shape[0]
    return (g if s == 0 else pltpu.roll(g, n - s, 0),)


_shift.defvjp(_shift_fwd, _shift_bwd)


def _scan_steps(a, b, reverse):
    n = a.shape[0]
    row = lax.broadcasted_iota(jnp.int32, a.shape, 0)
    k = 1
    while k < n:
        sh = n - k if reverse else k
        m = (row < n - k) if reverse else (row >= k)
        a_s = jnp.where(m, pltpu.roll(a, sh, 0), 1.0)
        b_s = jnp.where(m, pltpu.roll(b, sh, 0), 0.0)
        b = a * b_s + b
        a = a * a_s
        k *= 2
    return b


@jax.custom_vjp
def _scan(a, b):
    return _scan_steps(a, b, False)


def _scan_fwd(a, b):
    h = _scan_steps(a, b, False)
    return h, (a, h)


def _scan_bwd(res, dh):
    a, h = res
    n = a.shape[0]
    row = lax.broadcasted_iota(jnp.int32, a.shape, 0)
    a_next = jnp.where(row < n - 1, pltpu.roll(a, n - 1, 0), 0.0)
    g = _scan_steps(a_next, dh, True)
    h_prev = jnp.where(row >= 1, pltpu.roll(h, 1, 0), 0.0)
    return g * h_prev, g


_scan.defvjp(_scan_fwd, _scan_bwd)


def _mm(name, pairs, mode, out_dtype, *, res=None, scale=1.0, bm=None, bn=None, bk=None, after=None):
    a0, b0 = pairs[0]
    if mode == "nn":
        (M, K), N = a0.shape, b0.shape[1]
    elif mode == "nt":
        (M, K), N = a0.shape, b0.shape[0]
    else:
        (K, M), N = a0.shape, b0.shape[1]
    bm = bm or _pick(M, (1024, 512, 256, 128))
    bn = bn or _pick(N, (1024, 512, 256, 128))
    bk = bk or _pick(K, (1024, 512, 1408, 256, 128))
    nk = K // bk
    npair = len(pairs)
    dims = {"nn": _NN, "nt": _NT, "tn": _TN}[mode]

    def body(*refs):
        ab = refs[:2 * npair]
        pos = 2 * npair
        r_ref = None
        if res is not None:
            r_ref = refs[pos]
            pos += 1
        pos += after is not None
        o_ref = refs[pos]
        part = None
        for p in range(npair):
            d = _dot(ab[2 * p][...], ab[2 * p + 1][...], dims)
            part = d if part is None else part + d

        def finish(acc):
            out = acc if scale == 1.0 else acc * scale
            if r_ref is not None:
                out = out + r_ref[...]
            o_ref[...] = out.astype(out_dtype)

        if nk == 1:
            finish(part)
        else:
            acc_ref = refs[pos + 1]
            k = pl.program_id(2)

            @pl.when(k == 0)
            def _():
                acc_ref[...] = part

            @pl.when(k > 0)
            def _():
                acc_ref[...] += part

            @pl.when(k == nk - 1)
            def _():
                finish(acc_ref[...])

    if mode == "nn":
        a_spec = pl.BlockSpec((bm, bk), lambda i, j, k: (i, k))
        b_spec = pl.BlockSpec((bk, bn), lambda i, j, k: (k, j))
    elif mode == "nt":
        a_spec = pl.BlockSpec((bm, bk), lambda i, j, k: (i, k))
        b_spec = pl.BlockSpec((bn, bk), lambda i, j, k: (j, k))
    else:
        a_spec = pl.BlockSpec((bk, bm), lambda i, j, k: (k, i))
        b_spec = pl.BlockSpec((bk, bn), lambda i, j, k: (k, j))
    o_spec = pl.BlockSpec((bm, bn), lambda i, j, k: (i, j))
    in_specs, args = [], []
    for a, b in pairs:
        in_specs += [a_spec, b_spec]
        args += [a, b]
    if res is not None:
        in_specs.append(o_spec)
        args.append(res)
    if after is not None:
        in_specs.append(_ANY)
        args.append(after)
    return pl.pallas_call(
        body, name=name, grid=(M // bm, N // bn, nk),
        in_specs=in_specs, out_specs=o_spec,
        out_shape=jax.ShapeDtypeStruct((M, N), out_dtype),
        scratch_shapes=[pltpu.VMEM((bm, bn), F32)] if nk > 1 else [],
        compiler_params=_cp("parallel", "parallel", "arbitrary"),
    )(*args)


def _rms_fwd(name, x, g):
    T = x.shape[0]
    bm = _pick(T, (1024, 512, 256, 128))

    def body(x_ref, g_ref, o_ref):
        xv = x_ref[...]
        r = lax.rsqrt(jnp.mean(xv * xv, axis=-1, keepdims=True) + EPS)
        o_ref[...] = (xv * r * g_ref[...]).astype(BF16)

    return pl.pallas_call(
        body, name=name, grid=(T // bm,),
        in_specs=[pl.BlockSpec((bm, D), lambda i: (i, 0)), pl.BlockSpec((1, D), lambda i: (0, 0))],
        out_specs=pl.BlockSpec((bm, D), lambda i: (i, 0)),
        out_shape=jax.ShapeDtypeStruct((T, D), BF16),
        compiler_params=_cp("parallel"),
    )(x, g)


def _rms_bwd(name, x, g, dh, dres):
    T = x.shape[0]
    bm = _pick(T, (1024, 512, 256, 128))

    def body(x_ref, g_ref, dh_ref, dres_ref, dx_ref, dg_ref):
        xv = x_ref[...]
        r = lax.rsqrt(jnp.mean(xv * xv, axis=-1, keepdims=True) + EPS)
        xh = xv * r
        dhv = dh_ref[...]
        dxh = dhv * g_ref[...]
        dx_ref[...] = dres_ref[...] + r * (dxh - xh * jnp.mean(dxh * xh, axis=-1, keepdims=True))
        part = jnp.sum(dhv * xh, axis=0, keepdims=True)

        @pl.when(pl.program_id(0) == 0)
        def _():
            dg_ref[...] = part

        @pl.when(pl.program_id(0) > 0)
        def _():
            dg_ref[...] += part

    row = pl.BlockSpec((bm, D), lambda i: (i, 0))
    vec = pl.BlockSpec((1, D), lambda i: (0, 0))
    return pl.pallas_call(
        body, name=name, grid=(T // bm,),
        in_specs=[row, vec, row, row], out_specs=[row, vec],
        out_shape=[jax.ShapeDtypeStruct((T, D), F32), jax.ShapeDtypeStruct((1, D), F32)],
        compiler_params=_cp("arbitrary"),
    )(x, g, dh, dres)


def _final_loss(name, x, g, tgt):
    T = x.shape[0]
    bm = _pick(T, (512, 256, 128))

    def body(x_ref, g_ref, t_ref, loss_ref, dx_ref, dg_ref):
        xv = x_ref[...]
        gv = g_ref[...]
        r = lax.rsqrt(jnp.mean(xv * xv, axis=-1, keepdims=True) + EPS)
        xh = xv * r
        e = xh * gv - t_ref[...]
        lpart = jnp.broadcast_to(0.5 * jnp.sum(jnp.mean(e * e, axis=-1, keepdims=True), axis=0, keepdims=True), (1, 128))
        dy = e * (1.0 / D)
        dxh = dy * gv
        dx_ref[...] = r * (dxh - xh * jnp.mean(dxh * xh, axis=-1, keepdims=True))
        gpart = jnp.sum(dy * xh, axis=0, keepdims=True)

        @pl.when(pl.program_id(0) == 0)
        def _():
            loss_ref[...] = lpart
            dg_ref[...] = gpart

        @pl.when(pl.program_id(0) > 0)
        def _():
            loss_ref[...] += lpart
            dg_ref[...] += gpart

    row = pl.BlockSpec((bm, D), lambda i: (i, 0))
    vec = pl.BlockSpec((1, D), lambda i: (0, 0))
    return pl.pallas_call(
        body, name=name, grid=(T // bm,),
        in_specs=[row, vec, row],
        out_specs=[pl.BlockSpec((1, 128), lambda i: (0, 0)), row, vec],
        out_shape=[jax.ShapeDtypeStruct((1, 128), F32), jax.ShapeDtypeStruct((T, D), F32),
                   jax.ShapeDtypeStruct((1, D), F32)],
        compiler_params=_cp("arbitrary"),
    )(x, g, tgt)


def _ffn_up(name, h, wg, wu):
    T = h.shape[0]
    bm = _pick(T, (2048, 1024, 512, 256, 128))
    bn = 256

    def body(h_ref, wg_ref, wu_ref, sa_ref, ds_ref, act_ref):
        hv = h_ref[...]
        a = _dot(hv, wg_ref[...], _NT)
        b = _dot(hv, wu_ref[...], _NT)
        s = _sigmoid(a)
        sa = a * s
        sa_ref[...] = sa.astype(BF16)
        ds_ref[...] = (b * (s * (1.0 + a * (1.0 - s)))).astype(BF16)
        act_ref[...] = (sa * b).astype(BF16)

    w_spec = pl.BlockSpec((bn, D), lambda i, j: (j, 0))
    o_spec = pl.BlockSpec((bm, bn), lambda i, j: (i, j))
    return pl.pallas_call(
        body, name=name, grid=(T // bm, FF // bn),
        in_specs=[pl.BlockSpec((bm, D), lambda i, j: (i, 0)), w_spec, w_spec],
        out_specs=[o_spec, o_spec, o_spec],
        out_shape=[jax.ShapeDtypeStruct((T, FF), BF16)] * 3,
        compiler_params=_cp("parallel", "parallel"),
    )(h, wg, wu)


def _ffn_dact(name, dy, wd, sa, ds, after=None):
    T = dy.shape[0]
    bm = _pick(T, (2048, 1024, 512, 256, 128))
    bn = 256

    def body(dy_ref, wd_ref, sa_ref, ds_ref, *rest):
        da_ref, db_ref, dy_bf = rest[-3:]

        @pl.when(pl.program_id(1) == 0)
        def _():
            dy_bf[...] = dy_ref[...].astype(BF16)

        dact = 0.5 * _dot(dy_bf[...], wd_ref[...], _NT)
        da_ref[...] = (dact * ds_ref[...].astype(F32)).astype(BF16)
        db_ref[...] = (dact * sa_ref[...].astype(F32)).astype(BF16)

    t_spec = pl.BlockSpec((bm, bn), lambda i, j: (i, j))
    return pl.pallas_call(
        body, name=name, grid=(T // bm, FF // bn),
        in_specs=[pl.BlockSpec((bm, D), lambda i, j: (i, 0)), pl.BlockSpec((bn, D), lambda i, j: (j, 0)),
                  t_spec, t_spec] + [_ANY] * (after is not None),
        out_specs=[t_spec, t_spec],
        out_shape=[jax.ShapeDtypeStruct((T, FF), BF16), jax.ShapeDtypeStruct((T, FF), BF16)],
        scratch_shapes=[pltpu.VMEM((bm, D), BF16)],
        compiler_params=_cp("parallel", "arbitrary"),
    )(dy, wd, sa, ds, *([after] if after is not None else []))


def _merge_specs(T, bm, bn):
    y_spec = pl.BlockSpec((bm, BW), lambda i, j: (i, 0))
    wb_spec = pl.BlockSpec((NBR, bn, BW), lambda i, j: (0, j, 0))
    gate_specs = [pl.BlockSpec((bm, bn), functools.partial(lambda i, j, o: (i, o + j), o=(COL_GATE + g * D) // bn))
                  for g in range(NBR)]
    t_spec = pl.BlockSpec((bm, bn), lambda i, j: (i, j))
    return y_spec, wb_spec, gate_specs, t_spec


def _merge_fwd(name, ys, wb, proj):
    T = proj.shape[0]
    bm = _pick(T, (512, 256, 128))
    bn = 512
    y_spec, wb_spec, gate_specs, t_spec = _merge_specs(T, bm, bn)

    def body(y0, y1, y2, y3, wb_ref, g0, g1, g2, g3, o_ref):
        acc = None
        for g, (y_ref, g_ref) in enumerate(((y0, g0), (y1, g1), (y2, g2), (y3, g3))):
            t = _sigmoid(g_ref[...].astype(F32)) * _dot(y_ref[...], wb_ref[g], _NT)
            acc = t if acc is None else acc + t
        o_ref[...] = acc.astype(BF16)

    return pl.pallas_call(
        body, name=name, grid=(T // bm, D // bn),
        in_specs=[y_spec] * NBR + [wb_spec] + gate_specs, out_specs=t_spec,
        out_shape=jax.ShapeDtypeStruct((T, D), BF16),
        compiler_params=_cp("parallel", "parallel"),
    )(*ys, wb, proj, proj, proj, proj)


def _merge_bwd(name, dm, ys, wb, proj):
    T = proj.shape[0]
    bm = _pick(T, (512, 256, 128))
    bn = 512
    y_spec, wb_spec, gate_specs, t_spec = _merge_specs(T, bm, bn)

    def body(dm_ref, y0, y1, y2, y3, wb_ref, g0, g1, g2, g3, *outs):
        dmv = dm_ref[...]
        j = pl.program_id(1)
        for g, (y_ref, g_ref) in enumerate(((y0, g0), (y1, g1), (y2, g2), (y3, g3))):
            br = _dot(y_ref[...], wb_ref[g], _NT)
            s = _sigmoid(g_ref[...].astype(F32))
            outs[g][...] = (dmv * br * (s * (1.0 - s))).astype(BF16)
            dbr = (dmv * s).astype(BF16)
            outs[NBR + g][...] = dbr
            part = _dot(dbr, wb_ref[g])
            dy_ref = outs[2 * NBR + g]

            @pl.when(j == 0)
            def _():
                dy_ref[...] = part

            @pl.when(j > 0)
            def _():
                dy_ref[...] += part

    return pl.pallas_call(
        body, name=name, grid=(T // bm, D // bn),
        in_specs=[t_spec] + [y_spec] * NBR + [wb_spec] + gate_specs, out_specs=[t_spec] * (2 * NBR) + [y_spec] * NBR,
        out_shape=[jax.ShapeDtypeStruct((T, D), BF16)] * (2 * NBR) + [jax.ShapeDtypeStruct((T, BW), F32)] * NBR,
        compiler_params=_cp("parallel", "arbitrary"),
    )(dm, *ys, wb, proj, proj, proj, proj)


def _dwb(name, dbrs, ys):
    T = ys[0].shape[0]
    bk = _pick(T, (1024, 512, 256, 128))
    nk = T // bk

    def body(*refs):
        d_refs, y_refs, o_ref, acc = refs[:NBR], refs[NBR:2 * NBR], refs[2 * NBR], refs[2 * NBR + 1]
        k = pl.program_id(0)
        for g in range(NBR):
            part = _dot(d_refs[g][...], y_refs[g][...], _TN)

            @pl.when(k == 0)
            def _(g=g, part=part):
                acc[g] = part

            @pl.when(k > 0)
            def _(g=g, part=part):
                acc[g] += part

        @pl.when(k == nk - 1)
        def _():
            o_ref[...] = acc[...].astype(BF16)

    return pl.pallas_call(
        body, name=name, grid=(nk,),
        in_specs=[pl.BlockSpec((bk, D), lambda k: (k, 0))] * NBR + [pl.BlockSpec((bk, BW), lambda k: (k, 0))] * NBR,
        out_specs=pl.BlockSpec((NBR, D, BW), lambda k: (0, 0, 0)),
        out_shape=jax.ShapeDtypeStruct((NBR, D, BW), BF16),
        scratch_shapes=[pltpu.VMEM((NBR, D, BW), F32)],
        compiler_params=_cp("arbitrary"),
    )(*dbrs, *ys)


def _sgu_block(u_pre, v_pre, ln_g, ln_b, w, bias):
    u = _gelu(u_pre)
    vf = _gelu(v_pre)
    mu = jnp.mean(vf, axis=-1, keepdims=True)
    var = jnp.mean(jnp.square(vf - mu), axis=-1, keepdims=True)
    vn = (vf - mu) * lax.rsqrt(var + EPS) * ln_g + ln_b
    ri = lax.broadcasted_iota(jnp.int32, (128, 128), 0)
    ci = lax.broadcasted_iota(jnp.int32, (128, 128), 1)
    mask = (ri // CHUNK) >= (ci // CHUNK)
    outs = [_dot(jnp.where(mask, w[g], 0.0), vn[:, g * 128:(g + 1) * 128]) for g in range(4)]
    mixed = jnp.concatenate(outs, axis=1) + bias
    return u * mixed


def _sgu_param_specs():
    return [pl.BlockSpec((1, BW), lambda i: (0, 0)), pl.BlockSpec((1, BW), lambda i: (0, 0)),
            pl.BlockSpec((4, 128, 128), lambda i: (0, 0, 0)), pl.BlockSpec((128, BW), lambda i: (0, 0))]


def _sgu_fwd(name, proj, ln_g, ln_b, w, bias):
    T = proj.shape[0]
    rb = _pick(T, (256, 128))

    def body(u_ref, v_ref, g_ref, b_ref, w_ref, bias_ref, y_ref):
        for n in range(rb // 128):
            rows = slice(n * 128, (n + 1) * 128)
            y = _sgu_block(u_ref[rows, :].astype(F32), v_ref[rows, :].astype(F32), g_ref[...], b_ref[...], w_ref[...],
                           bias_ref[...])
            y_ref[rows, :] = y.astype(BF16)

    return pl.pallas_call(
        body, name=name, grid=(T // rb,),
        in_specs=[pl.BlockSpec((rb, BW), lambda i: (i, COL_AU // BW)), pl.BlockSpec((rb, BW), lambda i: (i, COL_AV // BW))]
        + _sgu_param_specs(),
        out_specs=pl.BlockSpec((rb, BW), lambda i: (i, 0)),
        out_shape=jax.ShapeDtypeStruct((T, BW), BF16),
        compiler_params=_cp("parallel"),
    )(proj, proj, ln_g, ln_b, w, bias)


def _sgu_bwd(name, proj, dy, ln_g, ln_b, w, bias):
    T = proj.shape[0]
    rb = _pick(T, (256, 128))

    def body(u_ref, v_ref, dy_ref, g_ref, b_ref, w_ref, bias_ref, du_ref, dv_ref, dg_ref, db_ref, dw_ref, dbias_ref):
        acc = None
        for n in range(rb // 128):
            rows = slice(n * 128, (n + 1) * 128)
            _, vjp = jax.vjp(_sgu_block, u_ref[rows, :].astype(F32), v_ref[rows, :].astype(F32), g_ref[...], b_ref[...],
                             w_ref[...],
                             bias_ref[...])
            du, dv, *dp = vjp(dy_ref[rows, :])
            du_ref[rows, :] = du.astype(BF16)
            dv_ref[rows, :] = dv.astype(BF16)
            acc = dp if acc is None else [p + q for p, q in zip(acc, dp)]

        @pl.when(pl.program_id(0) == 0)
        def _():
            for r, p in zip((dg_ref, db_ref, dw_ref, dbias_ref), acc):
                r[...] = p

        @pl.when(pl.program_id(0) > 0)
        def _():
            for r, p in zip((dg_ref, db_ref, dw_ref, dbias_ref), acc):
                r[...] += p

    row = pl.BlockSpec((rb, BW), lambda i: (i, 0))
    return pl.pallas_call(
        body, name=name, grid=(T // rb,),
        in_specs=[pl.BlockSpec((rb, BW), lambda i: (i, COL_AU // BW)), pl.BlockSpec((rb, BW), lambda i: (i, COL_AV // BW)),
                  row] + _sgu_param_specs(),
        out_specs=[row, row] + _sgu_param_specs(),
        out_shape=[jax.ShapeDtypeStruct((T, BW), BF16), jax.ShapeDtypeStruct((T, BW), BF16),
                   jax.ShapeDtypeStruct((1, BW), F32), jax.ShapeDtypeStruct((1, BW), F32),
                   jax.ShapeDtypeStruct((4, 128, 128), F32), jax.ShapeDtypeStruct((128, BW), F32)],
        compiler_params=_cp("arbitrary"),
    )(proj, proj, dy, ln_g, ln_b, w, bias)


def _halo_block(ref, i, rblk, halo):
    r0 = pl.multiple_of(i * rblk, rblk)
    h0 = pl.multiple_of(jnp.maximum(r0 - 16, 0), 16)
    top = jnp.where(i > 0, ref[pl.ds(h0, 16), :].astype(F32), 0.0)[16 - halo:]
    return jnp.concatenate([top, ref[pl.ds(r0, rblk), :].astype(F32)], axis=0)


def _with_halo_grad(dfull, pending, halo, rblk):
    tail = jnp.concatenate([jnp.zeros((rblk - halo, 128), F32), pending], axis=0)
    return dfull[halo:] + tail


def _conv4(xfull, rows):
    acc = None
    for k in range(4):
        t = rows[k] * _shift(xfull, 3 - k)[8:]
        acc = t if acc is None else acc + t
    return acc


def _lru_block(xfull, gate, h0, c0, c1, c2, c3, cb, wa, ba, wx, bx, lam):
    n = gate.shape[0]
    xc = _conv4(xfull, (c0, c1, c2, c3)) + cb
    r = _sigmoid(_dot(xc, wa) + ba)
    ig = _sigmoid(_dot(xc, wx) + bx)
    log_a = -LRU_C * r * _softplus(-lam)
    a = jnp.exp(log_a)
    mult = jnp.sqrt(-_expm1(2.0 * log_a))
    b = mult * (ig * xc)
    row = lax.broadcasted_iota(jnp.int32, (n, 128), 0)
    b = b + jnp.where(row == 0, a * h0, 0.0)
    h = _scan(a, b)
    out = h * _gelu(gate)
    h_last = jnp.sum(jnp.where(row == n - 1, h, 0.0), axis=0, keepdims=True)
    return out, h_last


def _lru_param_specs():
    vec = pl.BlockSpec((1, 128), lambda g: (0, g))
    mat = pl.BlockSpec((None, 128, 128), lambda g: (g, 0, 0))
    return [pl.BlockSpec((4, 128), lambda g: (0, g)), vec, mat, vec, mat, vec, vec]


def _lru_load_params(cw_ref, cb_ref, wa_ref, ba_ref, wx_ref, bx_ref, lam_ref):
    return (cw_ref[0:1, :], cw_ref[1:2, :], cw_ref[2:3, :], cw_ref[3:4, :], cb_ref[...], wa_ref[...], ba_ref[...],
            wx_ref[...], bx_ref[...], lam_ref[...])


def _lru_fwd(name, proj, cw, cb, wa, ba, wx, bx, lam):
    T = proj.shape[0]
    rblk = _pick(T, (256, 128))
    nblk = T // rblk

    def body(x_ref, gt_ref, cw_ref, cb_ref, wa_ref, ba_ref, wx_ref, bx_ref, lam_ref, y_ref, hc_ref):
        params = _lru_load_params(cw_ref, cb_ref, wa_ref, ba_ref, wx_ref, bx_ref, lam_ref)

        def step(i, h0):
            r0 = pl.multiple_of(i * rblk, rblk)
            out, h_last = _lru_block(_halo_block(x_ref, i, rblk, 8), gt_ref[pl.ds(r0, rblk), :].astype(F32), h0,
                                     *params)
            y_ref[pl.ds(r0, rblk), :] = out.astype(BF16)
            hc_ref[pl.ds(pl.multiple_of(i * 8, 8), 8), :] = jnp.broadcast_to(h0, (8, 128))
            return h_last

        lax.fori_loop(0, nblk, step, jnp.zeros((1, 128), F32))

    return pl.pallas_call(
        body, name=name, grid=(4,),
        in_specs=[pl.BlockSpec((T, 128), lambda g: (0, COL_BX // 128 + g)),
                  pl.BlockSpec((T, 128), lambda g: (0, COL_BG // 128 + g))] + _lru_param_specs(),
        out_specs=[pl.BlockSpec((T, 128), lambda g: (0, g)), pl.BlockSpec((nblk * 8, 128), lambda g: (0, g))],
        out_shape=[jax.ShapeDtypeStruct((T, BW), BF16), jax.ShapeDtypeStruct((nblk * 8, BW), F32)],
        compiler_params=_cp("parallel"),
    )(proj, proj, cw, cb, wa, ba, wx, bx, lam)


def _lru_bwd(name, proj, dy, hc, cw, cb, wa, ba, wx, bx, lam):
    T = proj.shape[0]
    rblk = _pick(T, (256, 128))
    nblk = T // rblk

    def body(x_ref, gt_ref, dy_ref, hc_ref, cw_ref, cb_ref, wa_ref, ba_ref, wx_ref, bx_ref, lam_ref,
             dx_ref, dgt_ref, dcw_ref, dcb_ref, dwa_ref, dba_ref, dwx_ref, dbx_ref, dlam_ref):
        params = _lru_load_params(cw_ref, cb_ref, wa_ref, ba_ref, wx_ref, bx_ref, lam_ref)

        def step(it, carry):
            dh_last, pending, acc = carry
            i = nblk - 1 - it
            r0 = pl.multiple_of(i * rblk, rblk)
            h0 = hc_ref[pl.ds(pl.multiple_of(i * 8, 8), 1), :]
            _, vjp = jax.vjp(_lru_block, _halo_block(x_ref, i, rblk, 8), gt_ref[pl.ds(r0, rblk), :].astype(F32), h0,
                             *params)
            dfull, dgate, dh0, *dp = vjp((dy_ref[pl.ds(r0, rblk), :], dh_last))
            dx_ref[pl.ds(r0, rblk), :] = _with_halo_grad(dfull, pending, 8, rblk).astype(BF16)
            dgt_ref[pl.ds(r0, rblk), :] = dgate.astype(BF16)
            return dh0, dfull[:8], tuple(p + q for p, q in zip(acc, dp))

        zeros = tuple(jnp.zeros(p.shape, F32) for p in params)
        _, _, acc = lax.fori_loop(0, nblk, step, (jnp.zeros((1, 128), F32), jnp.zeros((8, 128), F32), zeros))
        for k in range(4):
            dcw_ref[k:k + 1, :] = acc[k]
        for r, p in zip((dcb_ref, dwa_ref, dba_ref, dwx_ref, dbx_ref, dlam_ref), acc[4:]):
            r[...] = p

    col = pl.BlockSpec((T, 128), lambda g: (0, g))
    return pl.pallas_call(
        body, name=name, grid=(4,),
        in_specs=[pl.BlockSpec((T, 128), lambda g: (0, COL_BX // 128 + g)),
                  pl.BlockSpec((T, 128), lambda g: (0, COL_BG // 128 + g)), col,
                  pl.BlockSpec((nblk * 8, 128), lambda g: (0, g))] + _lru_param_specs(),
        out_specs=[col, col] + _lru_param_specs(),
        out_shape=[jax.ShapeDtypeStruct((T, BW), BF16), jax.ShapeDtypeStruct((T, BW), BF16),
                   jax.ShapeDtypeStruct((4, BW), F32), jax.ShapeDtypeStruct((1, BW), F32),
                   jax.ShapeDtypeStruct((4, 128, 128), F32), jax.ShapeDtypeStruct((1, BW), F32),
                   jax.ShapeDtypeStruct((4, 128, 128), F32), jax.ShapeDtypeStruct((1, BW), F32),
                   jax.ShapeDtypeStruct((1, BW), F32)],
        compiler_params=_cp("parallel"),
    )(proj, proj, dy, hc, cw, cb, wa, ba, wx, bx, lam)


def _conv_block(xfull, c0, c1, c2, c3):
    return _silu(_conv4(xfull, (c0, c1, c2, c3)))


def _conv_fwd(name, proj, col0, cw, cw_col0):
    T = proj.shape[0]
    rblk = _pick(T, (256, 128))
    nblk = T // rblk

    def body(x_ref, cw_ref, y_ref):
        rows = (cw_ref[0:1, :], cw_ref[1:2, :], cw_ref[2:3, :], cw_ref[3:4, :])

        def step(i, c):
            r0 = pl.multiple_of(i * rblk, rblk)
            y_ref[pl.ds(r0, rblk), :] = _conv_block(_halo_block(x_ref, i, rblk, 8), *rows)
            return c

        lax.fori_loop(0, nblk, step, 0)

    return pl.pallas_call(
        body, name=name, grid=(4,),
        in_specs=[pl.BlockSpec((T, 128), lambda g: (0, col0 // 128 + g)),
                  pl.BlockSpec((4, 128), lambda g: (0, cw_col0 // 128 + g))],
        out_specs=pl.BlockSpec((T, 128), lambda g: (0, g)),
        out_shape=jax.ShapeDtypeStruct((T, BW), F32),
        compiler_params=_cp("parallel"),
    )(proj, cw)


def _conv_bwd(name, proj, col0, dy, cw, cw_col0):
    T = proj.shape[0]
    rblk = _pick(T, (256, 128))
    nblk = T // rblk

    def body(x_ref, dy_ref, cw_ref, dx_ref, dcw_ref):
        rows = (cw_ref[0:1, :], cw_ref[1:2, :], cw_ref[2:3, :], cw_ref[3:4, :])

        def step(it, carry):
            pending, acc = carry
            i = nblk - 1 - it
            r0 = pl.multiple_of(i * rblk, rblk)
            _, vjp = jax.vjp(_conv_block, _halo_block(x_ref, i, rblk, 8), *rows)
            dfull, *dp = vjp(dy_ref[pl.ds(r0, rblk), :])
            dx_ref[pl.ds(r0, rblk), :] = _with_halo_grad(dfull, pending, 8, rblk).astype(BF16)
            return dfull[:8], tuple(p + q for p, q in zip(acc, dp))

        zeros = tuple(jnp.zeros((1, 128), F32) for _ in range(4))
        _, acc = lax.fori_loop(0, nblk, step, (jnp.zeros((8, 128), F32), zeros))
        for k in range(4):
            dcw_ref[k:k + 1, :] = acc[k]

    col = pl.BlockSpec((T, 128), lambda g: (0, g))
    return pl.pallas_call(
        body, name=name, grid=(4,),
        in_specs=[pl.BlockSpec((T, 128), lambda g: (0, col0 // 128 + g)), col,
                  pl.BlockSpec((4, 128), lambda g: (0, cw_col0 // 128 + g))],
        out_specs=[col, pl.BlockSpec((4, 128), lambda g: (0, g))],
        out_shape=[jax.ShapeDtypeStruct((T, BW), BF16), jax.ShapeDtypeStruct((4, BW), F32)],
        compiler_params=_cp("parallel"),
    )(proj, dy, cw)


def _pool_block(xfull, pw, sc, t0, gi):
    n = xfull.shape[0] - 16
    s2 = xfull + _shift(xfull, 1)
    s4 = s2 + _shift(s2, 2)
    s8 = s4 + _shift(s4, 4)
    s16 = s8 + _shift(s8, 8)
    s = jnp.where(gi == 0, s2, jnp.where(gi == 1, s4, jnp.where(gi == 2, s8, s16)))[16:]
    t = t0 + lax.broadcasted_iota(jnp.int32, (n, 128), 0)
    cnt = jnp.minimum(t + 1, lax.shift_left(jnp.int32(2), gi)).astype(F32)
    pooled = s / cnt - xfull[16:]
    return _dot(pooled, pw) * sc


def _pool_fwd(name, proj, pw, sc):
    T = proj.shape[0]
    rblk = _pick(T, (256, 128))
    nblk = T // rblk

    def body(x_ref, pw_ref, sc_ref, y_ref):
        gi = pl.program_id(0)

        def step(i, c):
            r0 = pl.multiple_of(i * rblk, rblk)
            y = _pool_block(_halo_block(x_ref, i, rblk, 16), pw_ref[...], sc_ref[...], r0, gi)
            y_ref[pl.ds(r0, rblk), :] = y.astype(BF16)
            return c

        lax.fori_loop(0, nblk, step, 0)

    return pl.pallas_call(
        body, name=name, grid=(4,),
        in_specs=[pl.BlockSpec((T, 128), lambda g: (0, COL_DX // 128 + g)),
                  pl.BlockSpec((None, 128, 128), lambda g: (g, 0, 0)), pl.BlockSpec((1, 128), lambda g: (0, g))],
        out_specs=pl.BlockSpec((T, 128), lambda g: (0, g)),
        out_shape=jax.ShapeDtypeStruct((T, BW), BF16),
        compiler_params=_cp("parallel"),
    )(proj, pw, sc)


def _pool_bwd(name, proj, dy, pw, sc):
    T = proj.shape[0]
    rblk = _pick(T, (256, 128))
    nblk = T // rblk

    def body(x_ref, dy_ref, pw_ref, sc_ref, dx_ref, dpw_ref, dsc_ref):
        gi = pl.program_id(0)

        def step(it, carry):
            pending, apw, asc = carry
            i = nblk - 1 - it
            r0 = pl.multiple_of(i * rblk, rblk)
            _, vjp = jax.vjp(lambda xf, w, s: _pool_block(xf, w, s, r0, gi), _halo_block(x_ref, i, rblk, 16),
                             pw_ref[...], sc_ref[...])
            dfull, dw, ds = vjp(dy_ref[pl.ds(r0, rblk), :])
            dx_ref[pl.ds(r0, rblk), :] = _with_halo_grad(dfull, pending, 16, rblk).astype(BF16)
            return dfull[:16], apw + dw, asc + ds

        _, apw, asc = lax.fori_loop(0, nblk, step, (jnp.zeros((16, 128), F32), jnp.zeros((128, 128), F32),
                                                    jnp.zeros((1, 128), F32)))
        dpw_ref[...] = apw
        dsc_ref[...] = asc

    col = pl.BlockSpec((T, 128), lambda g: (0, g))
    mat = pl.BlockSpec((None, 128, 128), lambda g: (g, 0, 0))
    vec = pl.BlockSpec((1, 128), lambda g: (0, g))
    return pl.pallas_call(
        body, name=name, grid=(4,),
        in_specs=[pl.BlockSpec((T, 128), lambda g: (0, COL_DX // 128 + g)), col, mat, vec],
        out_specs=[col, mat, vec],
        out_shape=[jax.ShapeDtypeStruct((T, BW), BF16), jax.ShapeDtypeStruct((4, 128, 128), F32),
                   jax.ShapeDtypeStruct((1, BW), F32)],
        compiler_params=_cp("parallel"),
    )(proj, dy, pw, sc)


@jax.custom_vjp
def _dot3(a, b):
    ah = a.astype(BF16)
    al = (a - ah.astype(F32)).astype(BF16)
    bh = b.astype(BF16)
    bl = (b - bh.astype(F32)).astype(BF16)

    def d(x, y):
        return lax.dot_general(x, y, _NN, preferred_element_type=F32)

    return d(ah, bh) + (d(ah, bl) + d(al, bh))


def _dot3_fwd(a, b):
    return _dot3(a, b), (a, b)


def _dot3_bwd(res, g):
    a, b = res
    return _dot(g, b, _NT), _dot(a, g, _TN)


_dot3.defvjp(_dot3_fwd, _dot3_bwd)


def _pad_rows2(x):
    return jnp.concatenate([x, jnp.zeros_like(x)], axis=0)


@jax.custom_vjp
def _tri_inv(mats):
    n = mats[0].shape[0]
    eye = (lax.broadcasted_iota(jnp.int32, (n, n), 0) == lax.broadcasted_iota(jnp.int32, (n, n), 1)).astype(F32)
    ps = [eye - a for a in mats]
    ms = list(mats)
    k = 2
    while k < n:
        ms = [_dot3(t, t) for t in ms]
        ps = [p + _dot3(p, t) for p, t in zip(ps, ms)]
        k *= 2
    return ps


def _tri_inv_fwd(mats):
    ts = _tri_inv(mats)
    return ts, ts


def _tri_inv_bwd(ts, gs):
    half = [_dot(t, g, _TN) for t, g in zip(ts, gs)]
    return ([-_dot(h, t, _NT) for h, t in zip(half, ts)],)


_tri_inv.defvjp(_tri_inv_fwd, _tri_inv_bwd)


def _cumsum_rows(x):
    n = x.shape[0]
    row = lax.broadcasted_iota(jnp.int32, x.shape, 0)
    k = 1
    while k < n:
        x = x + jnp.where(row >= k, _shift(x, k), 0.0)
        k *= 2
    return x


def _gdn_prep(qcs, kcs, vcs, tails, alog, dtb):
    C = CHUNK
    pairs = [(c, h) for c in range(len(qcs)) for h in range(4)]
    lane = lax.broadcasted_iota(jnp.int32, (C, 128), 1)
    row = lax.broadcasted_iota(jnp.int32, (C, 128), 0)
    incl = row >= lane
    sig = [_sigmoid(t) for t in tails]
    gfull = [-jnp.exp(alog) * _softplus(t + dtb) for t in tails]
    beta = [jnp.sum(jnp.where(lane == h, sig[c], 0.0), axis=1, keepdims=True) for c, h in pairs]
    g = [jnp.sum(jnp.where(lane == h + 4, gfull[c], 0.0), axis=1, keepdims=True) for c, h in pairs]
    qs = [qcs[c][:, h * 128:(h + 1) * 128] for c, h in pairs]
    ks = [kcs[c][:, h * 128:(h + 1) * 128] for c, h in pairs]
    vs = [vcs[c][:, h * 128:(h + 1) * 128] for c, h in pairs]
    q = [t * lax.rsqrt(jnp.sum(t * t, axis=-1, keepdims=True) + EPS) * (GDN_DK ** -0.5) for t in qs]
    k = [t * lax.rsqrt(jnp.sum(t * t, axis=-1, keepdims=True) + EPS) for t in ks]
    gc = [_cumsum_rows(jnp.broadcast_to(t, (C, 128))) for t in g]
    gc_t = [jnp.transpose(jnp.concatenate([t, t], axis=0)) for t in gc]
    gc_col = [jnp.sum(jnp.where(lane == 0, t, 0.0), axis=1, keepdims=True) for t in gc]
    ri = lax.broadcasted_iota(jnp.int32, (C, C), 0)
    ci = lax.broadcasted_iota(jnp.int32, (C, C), 1)
    decay = [jnp.exp(jnp.where(incl, a - b[:C, :], -1e30)) for a, b in zip(gc, gc_t)]
    decay_sq = [jnp.exp(jnp.where(ri > ci, a - jnp.transpose(b)[:C, :], -1e30)) for a, b in zip(gc_col, gc)]
    kb = [a * b for a, b in zip(k, beta)]
    kk = [_dot(a, b, _NT) for a, b in zip(kb, k)]
    t_mat = _tri_inv([jnp.where(ri > ci, a * b, 0.0) for a, b in zip(kk, decay_sq)])
    egc = [jnp.exp(t) for t in gc]
    u = [_dot(t, a * b) for t, a, b in zip(t_mat, vs, beta)]
    w = [_dot(t, a * b) for t, a, b in zip(t_mat, kb, egc)]
    qk = [_dot(a, _pad_rows2(b), _NT) for a, b in zip(q, k)]
    attn = [jnp.where(incl, a * b, 0.0) for a, b in zip(qk, decay)]
    g_last = [jnp.sum(jnp.where(row == C - 1, t, 0.0), axis=0, keepdims=True) for t in gc]
    qe = [a * b for a, b in zip(q, egc)]
    kd = [a * jnp.exp(b - c_) for a, b, c_ in zip(k, g_last, gc)]
    egl = [jnp.exp(t) for t in g_last]

    def per_chunk(vals):
        return [jnp.concatenate(vals[4 * c:4 * c + 4], axis=1) for c in range(len(qcs))]

    return tuple(per_chunk(t) for t in (u, w, qe, kd, attn, egl))


def _gdn_scan_chunk(states, u, w, qe, kd, attn, egl, z, ng):
    hs = range(4)

    def sl(t, h):
        return t[:, h * 128:(h + 1) * 128]

    ws = [_dot(sl(w, h), states[h]) for h in hs]
    qs = [_dot(sl(qe, h), states[h]) for h in hs]
    v_new = [sl(u, h) - ws[h] for h in hs]
    av = [_dot(sl(attn, h), _pad_rows2(v_new[h])) for h in hs]
    kv = [_dot(sl(kd, h), v_new[h], _TN) for h in hs]
    nxt = tuple(states[h] * sl(egl, h) + kv[h] for h in hs)
    o = [qs[h] + av[h] for h in hs]
    on = [t * lax.rsqrt(jnp.mean(t * t, axis=-1, keepdims=True) + EPS) * ng for t in o]
    return nxt, jnp.concatenate(on, axis=1) * _silu(z)


def _gdn_blocks(T):
    tb = _pick(T, (512, 256, 128, 64))
    return tb, T // tb, tb // CHUNK


PREP_CHUNKS = 4


def _chunk_rows(i, n):
    return [pl.ds(pl.multiple_of((i * n + j) * CHUNK, CHUNK), CHUNK) for j in range(n)]


def _egl_rows(i, n, size):
    return [pl.ds(pl.multiple_of((i * n + j) * 8, 8), size) for j in range(n)]


def _gdn_prep_fwd(name, qa, ka, va, proj, alog, dtb):
    T = proj.shape[0]
    tb, nb, ncb = _gdn_blocks(T)
    n = PREP_CHUNKS if ncb % PREP_CHUNKS == 0 else 1

    def body(q_ref, k_ref, v_ref, tail_ref, alog_ref, dtb_ref, u_ref, w_ref, qe_ref, kd_ref, at_ref, egl_ref):
        def step(i, c):
            rows = _chunk_rows(i, n)
            u, w, qe, kd, at, egl = _gdn_prep([q_ref[r, :] for r in rows], [k_ref[r, :] for r in rows],
                                              [v_ref[r, :] for r in rows], [tail_ref[r, :].astype(F32) for r in rows],
                                              alog_ref[...], dtb_ref[...])
            for j, (r, e) in enumerate(zip(rows, _egl_rows(i, n, 8))):
                u_ref[r, :] = u[j]
                w_ref[r, :] = w[j].astype(BF16)
                qe_ref[r, :] = qe[j].astype(BF16)
                kd_ref[r, :] = kd[j].astype(BF16)
                at_ref[r, :] = at[j].astype(BF16)
                egl_ref[e, :] = jnp.broadcast_to(egl[j], (8, BW))
            return c

        lax.fori_loop(0, ncb // n, step, 0)

    blk = pl.BlockSpec((tb, BW), lambda j: (j, 0))
    vec = pl.BlockSpec((1, 128), lambda j: (0, 0))
    return pl.pallas_call(
        body, name=name, grid=(nb,),
        in_specs=[blk, blk, blk, pl.BlockSpec((tb, 128), lambda j: (j, COL_TAIL // 128)), vec, vec],
        out_specs=[blk] * 5 + [pl.BlockSpec((ncb * 8, BW), lambda j: (j, 0))],
        out_shape=[jax.ShapeDtypeStruct((T, BW), F32)] + [jax.ShapeDtypeStruct((T, BW), BF16)] * 4
        + [jax.ShapeDtypeStruct((T // 8, BW), F32)],
        compiler_params=_cp("parallel"),
    )(qa, ka, va, proj, alog, dtb)


def _gdn_prep_bwd(name, qa, ka, va, proj, alog, dtb, du, dw, dqe, dkd, dat, degl):
    T = proj.shape[0]
    tb, nb, ncb = _gdn_blocks(T)
    n = PREP_CHUNKS if ncb % PREP_CHUNKS == 0 else 1

    def body(q_ref, k_ref, v_ref, tail_ref, alog_ref, dtb_ref, du_ref, dw_ref, dqe_ref, dkd_ref, dat_ref, degl_ref,
             dq_ref, dk_ref, dv_ref, dtail_ref, dalog_ref, ddtb_ref):
        first = pl.program_id(0) == 0

        def step(i, carry):
            pa, pd = carry
            rows = _chunk_rows(i, n)
            _, vjp = jax.vjp(_gdn_prep, [q_ref[r, :] for r in rows], [k_ref[r, :] for r in rows],
                             [v_ref[r, :] for r in rows], [tail_ref[r, :].astype(F32) for r in rows], alog_ref[...], dtb_ref[...])
            cot = tuple([ref[r, :] for r in rows] for ref in (du_ref, dw_ref, dqe_ref, dkd_ref, dat_ref))
            dq, dk, dv, dtail, da, dd = vjp(cot + ([degl_ref[e, :] for e in _egl_rows(i, n, 1)],))
            for j, r in enumerate(rows):
                dq_ref[r, :] = dq[j]
                dk_ref[r, :] = dk[j]
                dv_ref[r, :] = dv[j]
                dtail_ref[r, :] = dtail[j].astype(BF16)
            return pa + da, pd + dd

        zv = jnp.zeros((1, 128), F32)
        pa, pd = lax.fori_loop(0, ncb // n, step, (zv, zv))

        @pl.when(first)
        def _():
            dalog_ref[...] = pa
            ddtb_ref[...] = pd

        @pl.when(jnp.logical_not(first))
        def _():
            dalog_ref[...] += pa
            ddtb_ref[...] += pd

    blk = pl.BlockSpec((tb, BW), lambda j: (j, 0))
    vec = pl.BlockSpec((1, 128), lambda j: (0, 0))
    return pl.pallas_call(
        body, name=name, grid=(nb,),
        in_specs=[blk, blk, blk, pl.BlockSpec((tb, 128), lambda j: (j, COL_TAIL // 128)), vec, vec]
        + [blk] * 5 + [pl.BlockSpec((ncb * 8, BW), lambda j: (j, 0))],
        out_specs=[blk, blk, blk, pl.BlockSpec((tb, 128), lambda j: (j, 0)), vec, vec],
        out_shape=[jax.ShapeDtypeStruct((T, BW), F32)] * 3 + [jax.ShapeDtypeStruct((T, 128), BF16)]
        + [jax.ShapeDtypeStruct((1, 128), F32)] * 2,
        compiler_params=_cp("arbitrary"),
    )(qa, ka, va, proj, alog, dtb, du, dw, dqe, dkd, dat, degl)


def _gdn_fwd(name, u, w, qe, kd, at, egl, proj, ng):
    T = proj.shape[0]
    tb, nb, ncb = _gdn_blocks(T)

    def body(u_ref, w_ref, qe_ref, kd_ref, at_ref, egl_ref, z_ref, ng_ref, y_ref, sh_ref, state):
        @pl.when(pl.program_id(0) == 0)
        def _():
            state[...] = jnp.zeros((4, 128, 128), F32)

        def step(c, states):
            rows = pl.ds(pl.multiple_of(c * CHUNK, CHUNK), CHUNK)
            for h in range(4):
                sh_ref[h, c] = states[h]
            nxt, y = _gdn_scan_chunk(states, u_ref[rows, :], w_ref[rows, :], qe_ref[rows, :], kd_ref[rows, :],
                                     at_ref[rows, :], egl_ref[pl.ds(pl.multiple_of(c * 8, 8), 1), :],
                                     z_ref[rows, :].astype(F32),
                                     ng_ref[...])
            y_ref[rows, :] = y.astype(BF16)
            return nxt

        states = lax.fori_loop(0, ncb, step, tuple(state[h] for h in range(4)))
        for h in range(4):
            state[h] = states[h]

    blk = pl.BlockSpec((tb, BW), lambda j: (j, 0))
    vec = pl.BlockSpec((1, 128), lambda j: (0, 0))
    return pl.pallas_call(
        body, name=name, grid=(nb,),
        in_specs=[blk] * 5 + [pl.BlockSpec((ncb * 8, BW), lambda j: (j, 0)),
                              pl.BlockSpec((tb, BW), lambda j: (j, COL_CZ // BW)), vec],
        out_specs=[blk, pl.BlockSpec((4, ncb, 128, 128), lambda j: (0, j, 0, 0))],
        out_shape=[jax.ShapeDtypeStruct((T, BW), BF16), jax.ShapeDtypeStruct((4, T // CHUNK, 128, 128), F32)],
        scratch_shapes=[pltpu.VMEM((4, 128, 128), F32)],
        compiler_params=_cp("arbitrary"),
    )(u, w, qe, kd, at, egl, proj, ng)


def _gdn_bwd(name, u, w, qe, kd, at, egl, proj, dy, sh, ng):
    T = proj.shape[0]
    tb, nb, ncb = _gdn_blocks(T)

    def body(u_ref, w_ref, qe_ref, kd_ref, at_ref, egl_ref, z_ref, dy_ref, sh_ref, ng_ref,
             du_ref, dw_ref, dqe_ref, dkd_ref, dat_ref, degl_ref, dz_ref, dng_ref, dstate):
        first = pl.program_id(0) == 0

        @pl.when(first)
        def _():
            dstate[...] = jnp.zeros((4, 128, 128), F32)

        def step(it, carry):
            dstates, pn = carry
            c = ncb - 1 - it
            rows = pl.ds(pl.multiple_of(c * CHUNK, CHUNK), CHUNK)
            erow = pl.multiple_of(c * 8, 8)
            _, vjp = jax.vjp(_gdn_scan_chunk, tuple(sh_ref[h, c] for h in range(4)), u_ref[rows, :],
                             w_ref[rows, :].astype(F32), qe_ref[rows, :].astype(F32), kd_ref[rows, :].astype(F32),
                             at_ref[rows, :].astype(F32), egl_ref[pl.ds(erow, 1), :], z_ref[rows, :].astype(F32),
                             ng_ref[...])
            nxt, du, dw, dqe, dkd, dat, degl, dz, dn = vjp((dstates, dy_ref[rows, :]))
            du_ref[rows, :] = du
            dw_ref[rows, :] = dw
            dqe_ref[rows, :] = dqe
            dkd_ref[rows, :] = dkd
            dat_ref[rows, :] = dat
            degl_ref[pl.ds(erow, 8), :] = jnp.broadcast_to(degl, (8, BW))
            dz_ref[rows, :] = dz.astype(BF16)
            return nxt, pn + dn

        dstates, pn = lax.fori_loop(0, ncb, step, (tuple(dstate[h] for h in range(4)), jnp.zeros((1, 128), F32)))
        for h in range(4):
            dstate[h] = dstates[h]

        @pl.when(first)
        def _():
            dng_ref[...] = pn

        @pl.when(jnp.logical_not(first))
        def _():
            dng_ref[...] += pn

    blk = pl.BlockSpec((tb, BW), lambda j: (nb - 1 - j, 0))
    eblk = pl.BlockSpec((ncb * 8, BW), lambda j: (nb - 1 - j, 0))
    vec = pl.BlockSpec((1, 128), lambda j: (0, 0))
    return pl.pallas_call(
        body, name=name, grid=(nb,),
        in_specs=[blk] * 5 + [eblk, pl.BlockSpec((tb, BW), lambda j: (nb - 1 - j, COL_CZ // BW)), blk,
                              pl.BlockSpec((4, ncb, 128, 128), lambda j: (0, nb - 1 - j, 0, 0)), vec],
        out_specs=[blk] * 5 + [eblk, blk, vec],
        out_shape=[jax.ShapeDtypeStruct((T, BW), F32)] * 5 + [jax.ShapeDtypeStruct((T // 8, BW), F32),
                                                              jax.ShapeDtypeStruct((T, BW), BF16),
                                                              jax.ShapeDtypeStruct((1, 128), F32)],
        scratch_shapes=[pltpu.VMEM((4, 128, 128), F32)],
        compiler_params=_cp("arbitrary"),
    )(u, w, qe, kd, at, egl, proj, dy, sh, ng)


def _adamw_update(w_ref, g_ref, m_ref, v_ref, d_ref, nm_ref, nv_ref):
    gv = g_ref[...]
    m2 = ADAM_B1 * m_ref[...] + (1.0 - ADAM_B1) * gv
    v2 = ADAM_B2 * v_ref[...] + (1.0 - ADAM_B2) * jnp.square(gv)
    m_hat = m2 / (1.0 - ADAM_B1 ** ADAM_STEP)
    v_hat = v2 / (1.0 - ADAM_B2 ** ADAM_STEP)
    d_ref[...] = -ADAM_LR * (m_hat / (jnp.sqrt(v_hat) + ADAM_EPS) + ADAM_WD * w_ref[...])
    nm_ref[...] = m2
    nv_ref[...] = v2


def _adamw_many(name, ws, gs, ms, vs):
    n = len(ws)

    def body(*refs):
        for i in range(n):
            _adamw_update(*[refs[k * n + i] for k in range(7)])

    return pl.pallas_call(
        body, name=name,
        out_shape=[jax.ShapeDtypeStruct(a.shape, F32) for a in ws] * 3,
        compiler_params=_cp(),
    )(*ws, *gs, *ms, *vs)


def _adamw(name, w, g, m, v):
    R, C = w.shape
    br = _pick(R, (512, 256, 240, 128, 64, 8))
    body = functools.partial(_adamw_update)
    spec = pl.BlockSpec((br, C), lambda i: (i, 0))
    return pl.pallas_call(
        body, name=name, grid=(R // br,),
        in_specs=[spec] * 4, out_specs=[spec] * 3,
        out_shape=[jax.ShapeDtypeStruct((R, C), F32)] * 3,
        compiler_params=_cp("parallel"),
    )(w, g, m, v)


def _sum8(name, parts):
    _, R, C = parts.shape
    br = _pick(R, (176, 368, 64, 16, 8))

    def body(p_ref, o_ref):
        acc = p_ref[0].astype(F32)
        for d in range(1, N_DEV):
            acc = acc + p_ref[d].astype(F32)
        o_ref[...] = acc

    return pl.pallas_call(
        body, name=name, grid=(R // br,),
        in_specs=[pl.BlockSpec((N_DEV, br, C), lambda i: (0, i, 0))],
        out_specs=pl.BlockSpec((br, C), lambda i: (i, 0)),
        out_shape=jax.ShapeDtypeStruct((R, C), F32),
        compiler_params=_cp("parallel"),
    )(parts)


_ANY = pl.BlockSpec(memory_space=pl.ANY)
_MESH = pl.DeviceIdType.MESH


def _all_gather(name, shard):
    R, C = shard.shape

    def body(x_ref, out_ref, send_sems, recv_sems, local_sem):
        x, y, c = lax.axis_index("x"), lax.axis_index("y"), lax.axis_index("c")
        me, sibling = (x, y, c), (x, y, 1 - c)
        chips = [(1 - x, y), (x, 1 - y), (1 - x, 1 - y)]

        def slot(px, py, pc):
            return out_ref.at[4 * px + 2 * py + pc]

        def copy(k, block, to, src=None):
            return pltpu.make_async_remote_copy(
                src_ref=slot(*block) if src is None else src, dst_ref=slot(*block),
                send_sem=send_sems.at[k], recv_sem=recv_sems.at[k], device_id=to, device_id_type=_MESH)

        mine = pltpu.make_async_copy(x_ref, slot(*me), local_sem)
        mine.start()
        first = [copy(0, me, sibling, src=x_ref)]
        first += [copy(1 + j, me, (*chip, c), src=x_ref) for j, chip in enumerate(chips)]
        for cp in first:
            cp.start()
        passed = [copy(4 + j, (*chip, c), sibling) for j, chip in enumerate(chips)]
        for j, chip in enumerate(chips):
            copy(1 + j, (*chip, c), me).wait_recv()
            passed[j].start()
        copy(0, sibling, me).wait_recv()
        for j, chip in enumerate(chips):
            copy(4 + j, (*chip, 1 - c), me).wait_recv()
        for cp in first + passed:
            cp.wait_send()
        mine.wait()

    return pl.pallas_call(
        body, name=name,
        in_specs=[_ANY], out_specs=_ANY,
        out_shape=jax.ShapeDtypeStruct((N_DEV, R, C), shard.dtype),
        scratch_shapes=[pltpu.SemaphoreType.DMA((7,)), pltpu.SemaphoreType.DMA((7,)), pltpu.SemaphoreType.DMA],
    )(shard)


_HBM = pl.BlockSpec(memory_space=pltpu.HBM)
_SEM = pl.BlockSpec(memory_space=pltpu.SEMAPHORE)
_EFFECT = pltpu.SideEffectType.DATAFLOW_SIDE_EFFECTING


def _exchange_copies(src_ref, land_ref, send_sems, recv_sems, scatter):
    x, y, c = lax.axis_index("x"), lax.axis_index("y"), lax.axis_index("c")
    me = 4 * x + 2 * y + c
    copies = []
    for k in range(1, N_DEV):
        px, py, pc = x ^ ((k >> 2) & 1), y ^ ((k >> 1) & 1), c ^ (k & 1)
        src = src_ref.at[4 * px + 2 * py + pc] if scatter else src_ref
        copies.append(pltpu.make_async_remote_copy(
            src_ref=src, dst_ref=land_ref.at[me], send_sem=send_sems.at[k - 1], recv_sem=recv_sems.at[k - 1],
            device_id=(px, py, pc), device_id_type=_MESH))
    return copies


def _own_copy(src_ref, land_ref, send_sems, scatter):
    me = 4 * lax.axis_index("x") + 2 * lax.axis_index("y") + lax.axis_index("c")
    return pltpu.make_async_copy(src_ref.at[me] if scatter else src_ref, land_ref.at[me], send_sems.at[N_DEV - 1])


def _exchange_start(name, srcs, lands, scatter, after=None):
    n = len(srcs)

    def body(*refs):
        src_refs, land_refs = refs[:n], refs[n:2 * n]
        outs = refs[2 * n + (after is not None):]
        send, recv = outs[:n], outs[n:2 * n]
        token = refs[-1]
        for g in range(n):
            for cp in _exchange_copies(src_refs[g], land_refs[g], send[g], recv[g], scatter):
                cp.start()
            _own_copy(src_refs[g], land_refs[g], send[g], scatter).start()
        token[...] = jnp.zeros_like(token)

    outs = pl.pallas_call(
        body, name=name,
        out_shape=tuple([pltpu.SemaphoreType.DMA((N_DEV,))] * (2 * n)
                        + [pltpu.HBM(a.shape, a.dtype) for a in list(srcs) + list(lands)]
                        + [jax.ShapeDtypeStruct((8, 128), F32)]),
        in_specs=[_HBM] * (2 * n) + [_ANY] * (after is not None),
        out_specs=tuple([_SEM] * (2 * n) + [_HBM] * (2 * n) + [pl.BlockSpec(memory_space=pltpu.VMEM)]),
        input_output_aliases={i: 2 * n + i for i in range(2 * n)},
        compiler_params=pltpu.CompilerParams(has_side_effects=_EFFECT),
    )(*[pltpu.with_memory_space_constraint(a, pltpu.HBM) for a in list(srcs) + list(lands)],
      *([after] if after is not None else []))
    handles = [(outs[2 * n + g], outs[3 * n + g], outs[g], outs[n + g]) for g in range(n)]
    return handles, outs[-1]


def _exchange_wait(name, handles, after, scatter):
    n = len(handles)
    srcs, lands, sends, recvs = ([h[i] for h in handles] for i in range(4))

    def body(*refs):
        src_refs, land_refs = refs[:n], refs[n:2 * n]
        send, recv = refs[2 * n:3 * n], refs[3 * n:4 * n]
        for g in range(n):
            for cp in _exchange_copies(src_refs[g], land_refs[g], send[g], recv[g], scatter):
                cp.wait_send()
                cp.wait_recv()
            _own_copy(src_refs[g], land_refs[g], send[g], scatter).wait()

    outs = pl.pallas_call(
        body, name=name,
        out_shape=tuple(pltpu.HBM(a.shape, a.dtype) for a in srcs + lands),
        in_specs=tuple([_HBM] * (2 * n) + [_SEM] * (2 * n) + [_ANY]), out_specs=tuple([_HBM] * (2 * n)),
        input_output_aliases={i: i for i in range(2 * n)},
        compiler_params=pltpu.CompilerParams(has_side_effects=_EFFECT),
    )(*srcs, *lands, *sends, *recvs, after)
    return list(outs[n:])


def _rows(a):
    return a.reshape(-1, 1024)


def _rows_to_parts(full):
    n = full.shape[-2] // N_DEV
    t = full.reshape(full.shape[:-2] + (N_DEV, n, full.shape[-1]))
    return jnp.moveaxis(t, -3, 0)


def _parts_to_rows(parts):
    t = jnp.moveaxis(parts, 0, -3)
    return t.reshape(t.shape[:-3] + (t.shape[-3] * t.shape[-2], t.shape[-1]))


def _parts_to_cols(parts):
    t = jnp.moveaxis(parts, 0, -2)
    return t.reshape(t.shape[:-2] + (t.shape[-2] * t.shape[-1],))


def _join(parts, axis=0):
    total = sum(p.shape[axis] for p in parts)
    out, off = None, 0
    for p in parts:
        cfg = [(0, 0)] * p.ndim
        cfg[axis] = (off, total - off - p.shape[axis])
        t = jnp.pad(p, cfg)
        out = t if out is None else out + t
        off += p.shape[axis]
    return out


def _w_in_to_layout(w):
    tail = jnp.pad(w[4096:4104], ((0, PW - COL_TAIL - 8), (0, 0)))
    return jnp.concatenate([w[:4096], w[4104:P_IN], tail], axis=0)


def _w_in_from_layout(g):
    return _join([g[:4096], g[COL_TAIL:COL_TAIL + 8], g[4096:COL_TAIL]], axis=0)


def _block_diag(w):
    w = w.reshape(4, 2, 64, 64)
    return jnp.pad(w[:, 0], ((0, 0), (0, 64), (0, 64))) + jnp.pad(w[:, 1], ((0, 0), (64, 0), (64, 0)))


def _block_diag_grad(g):
    return jnp.stack([g[:, :64, :64], g[:, 64:, 64:]], axis=1).reshape(8, 64, 64)


def _ffn_forward(tag, x, norm, wg, wu, wd):
    h = _rms_fwd(tag + "_norm", x, norm)
    sa, ds, act = _ffn_up(tag + "_up", h, wg, wu)
    if callable(wd):
        wd = wd(act)
    x_out = _mm(tag + "_down", [(act, wd)], "nn", F32, res=x, scale=0.5)
    return x_out, (x, h, sa, ds, act), wd


def _dw_pair(name, da, db, h, after=None):
    T = h.shape[0]
    bm = FF // 2
    bk = _pick(T, (1024, 512, 256, 128))
    nk = T // bk

    def body(da_ref, db_ref, h_ref, *rest):
        og_ref, ou_ref, accg, accu = rest[-4:]
        k = pl.program_id(1)
        hv = h_ref[...]
        pg = _dot(da_ref[...], hv, _TN)
        pu = _dot(db_ref[...], hv, _TN)

        @pl.when(k == 0)
        def _():
            accg[...] = pg
            accu[...] = pu

        @pl.when(k > 0)
        def _():
            accg[...] += pg
            accu[...] += pu

        @pl.when(k == nk - 1)
        def _():
            og_ref[...] = accg[...].astype(BF16)
            ou_ref[...] = accu[...].astype(BF16)

    a_spec = pl.BlockSpec((bk, bm), lambda i, k: (k, i))
    o_spec = pl.BlockSpec((bm, D), lambda i, k: (i, 0))
    return pl.pallas_call(
        body, name=name, grid=(FF // bm, nk),
        in_specs=[a_spec, a_spec, pl.BlockSpec((bk, D), lambda i, k: (k, 0))] + [_ANY] * (after is not None),
        out_specs=[o_spec, o_spec],
        out_shape=[jax.ShapeDtypeStruct((FF, D), BF16)] * 2,
        scratch_shapes=[pltpu.VMEM((bm, D), F32)] * 2,
        compiler_params=_cp("parallel", "arbitrary"),
    )(da, db, h, *([after] if after is not None else []))


def _ffn_backward(tag, dx_out, saved, norm, wg, wu, wd, put, names, split=False):
    x, h, sa, ds, act = saved
    n_wg, n_wu, n_wd = names
    dwd = _mm(tag + "_dwd", [(act, dx_out)], "tn", BF16, scale=0.5, bm=FF // 2)
    tok = put({n_wd: dwd}) if split else None
    da, db = _ffn_dact(tag + "_dact", dx_out, wd, sa, ds, after=tok)
    dwg, dwu = _dw_pair(tag + "_dwgu", da, db, h, after=tok)
    tok = tok + put({n_wg: dwg, n_wu: dwu}) if split else put({n_wg: dwg, n_wu: dwu, n_wd: dwd})
    dh = _mm(tag + "_dh", [(da, wg), (db, wu)], "nn", F32, after=tok)
    dx, dnorm = _rms_bwd(tag + "_dnorm", x, norm + tok, dh, dx_out)
    return dx, dnorm


def _mixer_params(p):
    alog = jnp.pad(p["gdn_a_log"], (4, 120))[None]
    dtb = jnp.pad(p["gdn_dt_bias"], (4, 120))[None]
    bias = jnp.repeat(p["sgu_b"].T, 128, axis=1)
    return dict(
        ln_g=p["sgu_ln_g"][None], ln_b=p["sgu_ln_b"][None], sgu_w=p["sgu_w"], sgu_bias=bias,
        lru_cw=p["lru_conv_w"], lru_cb=p["lru_conv_b"][None], wa=_block_diag(p["lru_wa"]), ba=p["lru_ba"][None],
        wx=_block_diag(p["lru_wx"]), bx=p["lru_bx"][None], lam=p["lru_lambda"][None],
        gdn_cw=p["gdn_conv_w"], alog=alog, dtb=dtb, ng=p["gdn_norm_g"][None],
        pool_w=p["pool_w"], pool_sc=p["pool_scale"][None])


def _mix_forward(tag, x, p, mp):
    h = _rms_fwd(tag + "_norm", x, p["mix_norm"][None])
    proj = _mm(tag + "_proj", [(h, p["w_in"])], "nt", BF16, bm=_pick(x.shape[0], (2048, 1024, 512, 256, 128)))
    y_a = _sgu_fwd(tag + "_sgu", proj, mp["ln_g"], mp["ln_b"], mp["sgu_w"], mp["sgu_bias"])
    y_b, hc = _lru_fwd(tag + "_lru", proj, mp["lru_cw"], mp["lru_cb"], mp["wa"], mp["ba"], mp["wx"], mp["bx"],
                       mp["lam"])
    qa = _conv_fwd(tag + "_convq", proj, COL_CQ, mp["gdn_cw"], 0)
    ka = _conv_fwd(tag + "_convk", proj, COL_CK, mp["gdn_cw"], 512)
    va = _conv_fwd(tag + "_convv", proj, COL_CV, mp["gdn_cw"], 1024)
    prep = _gdn_prep_fwd(tag + "_gdnprep", qa, ka, va, proj, mp["alog"], mp["dtb"])
    y_c, sh = _gdn_fwd(tag + "_gdn", *prep, proj, mp["ng"])
    y_d = _pool_fwd(tag + "_pool", proj, mp["pool_w"], mp["pool_sc"])
    ys = (y_a, y_b, y_c, y_d)
    if callable(p["w_branch"]):
        p["w_branch"] = p["w_branch"](y_d)
    merged = _merge_fwd(tag + "_merge", ys, p["w_branch"], proj)
    if callable(p["w_out"]):
        p["w_out"] = p["w_out"](merged)
    x_out = _mm(tag + "_out", [(merged, p["w_out"])], "nn", F32, res=x)
    return x_out, (x, h, proj, hc, qa, ka, va, prep, sh, ys, merged)


def _mix_backward(tag, dx_out, saved, p, mp, put):
    x, h, proj, hc, qa, ka, va, prep, sh, ys, merged = saved
    T = x.shape[0]
    g = {}
    dmerged = _mm(tag + "_dmerged", [(dx_out, p["w_out"])], "nt", F32)
    g["w_out"] = _mm(tag + "_dwout", [(merged, dx_out)], "tn", BF16)
    outs = _merge_bwd(tag + "_dmerge", dmerged, ys, p["w_branch"], proj)
    dgates, dbrs, dys = outs[:NBR], outs[NBR:2 * NBR], outs[2 * NBR:]
    g["w_branch"] = _dwb(tag + "_dwb", dbrs, ys)

    du, dv, dln_g, dln_b, dsgu_w, dbias = _sgu_bwd(tag + "_dsgu", proj, dys[0], mp["ln_g"], mp["ln_b"], mp["sgu_w"],
                                                  mp["sgu_bias"])
    g["sgu_ln_g"], g["sgu_ln_b"], g["sgu_w"] = dln_g[0], dln_b[0], dsgu_w
    g["sgu_b"] = dbias.reshape(128, 4, 128).sum(axis=2).T

    (dbx, dbg, dcw, dcb, dwa, dba, dwx, dbxb, dlam) = _lru_bwd(
        tag + "_dlru", proj, dys[1], hc, mp["lru_cw"], mp["lru_cb"], mp["wa"], mp["ba"], mp["wx"], mp["bx"], mp["lam"])
    g["lru_conv_w"], g["lru_conv_b"], g["lru_ba"], g["lru_bx"], g["lru_lambda"] = dcw, dcb[0], dba[0], dbxb[0], dlam[0]
    g["lru_wa"], g["lru_wx"] = _block_diag_grad(dwa), _block_diag_grad(dwx)

    *dprep, dz, dng = _gdn_bwd(tag + "_dgdn", *prep, proj, dys[2], sh, mp["ng"])
    dqa, dka, dva, dtail, dalog, ddtb = _gdn_prep_bwd(tag + "_dgdnprep", qa, ka, va, proj, mp["alog"], mp["dtb"], *dprep)
    g["gdn_a_log"], g["gdn_dt_bias"], g["gdn_norm_g"] = dalog[0, 4:8], ddtb[0, 4:8], dng[0]
    dq, dcwq = _conv_bwd(tag + "_dconvq", proj, COL_CQ, dqa, mp["gdn_cw"], 0)
    dk, dcwk = _conv_bwd(tag + "_dconvk", proj, COL_CK, dka, mp["gdn_cw"], 512)
    dv_, dcwv = _conv_bwd(tag + "_dconvv", proj, COL_CV, dva, mp["gdn_cw"], 1024)
    g["gdn_conv_w"] = jnp.concatenate([dcwq, dcwk, dcwv], axis=1)

    dd, dpw, dsc = _pool_bwd(tag + "_dpool", proj, dys[3], mp["pool_w"], mp["pool_sc"])
    g["pool_w"], g["pool_scale"] = dpw, dsc[0]

    dproj = jnp.concatenate([du, dv, dbx, dbg, dq, dk, dv_, dz, dd, *dgates, dtail,
                             jnp.zeros((T, PW - COL_TAIL - 128), BF16)], axis=1)
    dw_in = _mm(tag + "_dwin", [(dproj, h)], "tn", BF16)
    tok = put(dict(w_in=_w_in_from_layout(dw_in), w_branch=g.pop("w_branch"), w_out=g.pop("w_out")))
    dh = _mm(tag + "_dh", [(dproj, p["w_in"])], "nn", F32, bm=_pick(T, (2048, 1024, 512, 256, 128)), after=tok)
    dx, dnorm = _rms_bwd(tag + "_dnorm", x, p["mix_norm"][None] + tok, dh, dx_out)
    g["mix_norm"] = dnorm[0]
    return dx, g


_BIG = ("ff1_wg", "ff1_wu", "ff1_wd", "w_in", "w_branch", "w_out", "ff2_wg", "ff2_wu", "ff2_wd")
_COL_SHARDED = ("ff1_wg", "ff1_wu", "w_in", "w_branch", "ff2_wg", "ff2_wu")
_SMALL = ("ff1_norm", "mix_norm", "sgu_ln_g", "sgu_ln_b", "sgu_w", "sgu_b", "lru_conv_w", "lru_conv_b", "lru_wa",
          "lru_ba", "lru_wx", "lru_bx", "lru_lambda", "gdn_conv_w", "gdn_a_log", "gdn_dt_bias", "gdn_norm_g", "pool_w",
          "pool_scale", "ff2_norm", "final_norm")
_WEIGHTS = ("ff1_norm", "ff1_wg", "ff1_wu", "ff1_wd", "mix_norm", "w_in", "sgu_ln_g", "sgu_ln_b", "sgu_w", "sgu_b",
            "lru_conv_w", "lru_conv_b", "lru_wa", "lru_ba", "lru_wx", "lru_bx", "lru_lambda", "gdn_conv_w", "gdn_a_log",
            "gdn_dt_bias", "gdn_norm_g", "pool_w", "pool_scale", "w_branch", "w_out", "ff2_norm", "ff2_wg", "ff2_wu",
            "ff2_wd", "final_norm")
_CONV_SHARDED = ("lru_conv_w", "gdn_conv_w")
PACK_ROW_ALIGN = 16
_GROUPS = (("ff1", ("ff1_wg", "ff1_wu", "ff1_wd")), ("mix", ("w_in", "w_branch", "w_out")),
           ("ff2", ("ff2_wg", "ff2_wu", "ff2_wd")))


def _pad_rows(a, mult):
    pad = (-a.shape[-2]) % mult
    if pad == 0:
        return a
    return jnp.pad(a, [(0, 0)] * (a.ndim - 2) + [(0, pad), (0, 0)])


def _my_index():
    return 4 * lax.axis_index("x") + 2 * lax.axis_index("y") + lax.axis_index("c")


def _landing(shape, dtype):
    return lax.empty((N_DEV,) + tuple(shape), dtype)


def _stored(n, a):
    return jnp.swapaxes(a, -1, -2) if n in _COL_SHARDED else a


_FIRST = ("ff1_wg", "ff1_wu", "ff1_wd")


def _gather_first(w):
    names = _FIRST
    shards = [_rows(_stored(n, w[n][0]).astype(BF16)) for n in names]
    got = _all_gather("gather_first", jnp.concatenate(shards, axis=0))
    out, r = {}, 0
    for n, s in zip(names, shards):
        out[n] = got[:, r:r + s.shape[0]].reshape(-1, 1024)
        r += s.shape[0]
    return out, got


def _gather_start(w, after):
    conv = _pad_rows(jnp.concatenate([w[n].reshape(1, -1) for n in _CONV_SHARDED], axis=1), 8)
    keys, srcs = ["conv"], [conv]
    for l in range(2):
        for sub, (_, names) in enumerate(_GROUPS):
            for n in names:
                if l > 0 or n not in _FIRST:
                    keys.append((l, sub, n))
                    srcs.append(_stored(n, w[n][l]).astype(BF16))
    lands = [_landing(s.shape, s.dtype) for s in srcs]
    handles, token = _exchange_start("gather_start", srcs, lands, scatter=False, after=after)
    return dict(zip(keys, handles)), token


def _gather_finish(l, sub, handles, first, after):
    names = _GROUPS[sub][1]
    if (l, sub) == (0, 0):
        out = dict(first)
        for n in names:
            if n not in _FIRST:
                out[n] = lambda later, n=n: _parts_to_rows(
                    _exchange_wait(f"gather_wait_00_{n}", [handles[(0, 0, n)]], later, scatter=False)[0])
    elif sub == 1:
        out = {n: (lambda later, n=n: _parts_to_rows(
            _exchange_wait(f"gather_wait_{l}{sub}_{n}", [handles[(l, sub, n)]], later, scatter=False)[0])) for n in names}
        out["w_in"] = out["w_in"](after)
    else:
        lands = _exchange_wait(f"gather_wait_{l}{sub}", [handles[(l, sub, n)] for n in names], after, scatter=False)
        out = {n: _parts_to_rows(land) for n, land in zip(names, lands)}
    if "w_in" in out:
        out["w_in"] = _w_in_to_layout(out["w_in"])
    return out


def _scatter_start(l, sub, grads):
    srcs, shapes = [], []
    for n in grads:
        parts = _rows_to_parts(grads[n])
        shapes.append(parts.shape[1:])
        srcs.append(_pad_rows(parts.reshape(N_DEV, -1, 1024), PACK_ROW_ALIGN))
    lands = [_landing(s.shape[1:], s.dtype) for s in srcs]
    tag = f"{l}{sub}" + ("" if len(grads) == len(_GROUPS[sub][1]) else "_" + "_".join(grads))
    handles, token = _exchange_start(f"scatter_start_{tag}", srcs, lands, scatter=True)
    return handles, (tag, tuple(grads), shapes), token


def _scatter_finish(l, sub, handles, meta, after):
    tag, names, shapes = meta
    lands = _exchange_wait(f"scatter_wait_{tag}", handles, after, scatter=True)
    out = {}
    for n, land, shape in zip(names, lands, shapes):
        size = 1
        for s in shape:
            size *= s
        summed = _sum8(f"sum_{l}{sub}_{n}", land)
        out[n] = _stored(n, summed[:size // 1024].reshape(shape))
    return out


def _gather_conv_finish(w, handles, after):
    gconv = _exchange_wait("gather_wait_conv", [handles["conv"]], after, scatter=False)[0][:, 0]
    full, r = {}, 0
    for n in _CONV_SHARDED:
        sz = w[n].size
        full[n] = _parts_to_cols(gconv[:, r:r + sz].reshape((N_DEV,) + w[n].shape))
        r += sz
    return full


def _forward_backward(x, tgt, w, conv, get_weights, put_grads, put_small, token):
    saved, params = [], []
    for l in range(2):
        p = {n: w[n][l] for n in _SMALL if n != "final_norm"}
        for n in _CONV_SHARDED:
            p[n] = conv[n][l]
        mp = _mixer_params(p)
        tok = token[:1, :1] if l == 0 else 0.0
        p.update(get_weights(l, 0, x))
        x, s1, p["ff1_wd"] = _ffn_forward(f"l{l}_ff1", x, p["ff1_norm"][None] + tok, p["ff1_wg"], p["ff1_wu"],
                                          p["ff1_wd"])
        p.update(get_weights(l, 1, x))
        x, s2 = _mix_forward(f"l{l}_mix", x, p, mp)
        p.update(get_weights(l, 2, x))
        x, s3, _ = _ffn_forward(f"l{l}_ff2", x, p["ff2_norm"][None], p["ff2_wg"], p["ff2_wu"], p["ff2_wd"])
        saved.append((s1, s2, s3))
        params.append((p, mp))
    loss, dx, dfinal = _final_loss("loss_head", x, w["final_norm"][None], tgt)
    tok = 0.0
    for l in (1, 0):
        p, mp = params[l]
        s1, s2, s3 = saved[l]
        g = {}

        def put(sub):
            return lambda grads, l=l: put_grads(l, sub, grads)[:1, :1]

        dx, dn = _ffn_backward(f"l{l}_ff2", dx, s3, p["ff2_norm"][None] + tok, p["ff2_wg"], p["ff2_wu"], p["ff2_wd"],
                               put(2), _GROUPS[2][1])
        g["ff2_norm"] = dn[0]
        dx, gm = _mix_backward(f"l{l}_mix", dx, s2, p, mp, put(1))
        g.update(gm)
        tok = 0.0
        if l == 0:
            keep = {n: g.pop(n) for n in ("gdn_a_log", "gdn_dt_bias")}
            tok = put_small("0a", g, True)[:1, :1]
            g = keep
        dx, dn = _ffn_backward(f"l{l}_ff1", dx, s1, p["ff1_norm"][None] + tok, p["ff1_wg"], p["ff1_wu"], p["ff1_wd"],
                               put(0), _GROUPS[0][1], split=(l == 0))
        g["ff1_norm"] = dn[0]
        if l == 1:
            g["final_norm"] = dfinal[0]
            g["loss"] = loss[0, :1]
        tok = put_small("1" if l == 1 else "0b", g, False)[:1, :1]
    return dx


SMALL_PIECE = 8 * 1024


def _pack_small(d, names):
    pieces = []
    for n in names:
        flat = d[n].reshape(-1)
        pieces.append(jnp.pad(flat, (0, (-flat.size) % SMALL_PIECE)).reshape(-1, 1024))
    return jnp.concatenate(pieces, axis=0)


def _unpack_small(pack, shapes, names):
    out, r = {}, 0
    for n in names:
        size = 1
        for s in shapes[n]:
            size *= s
        rows = -(-size // SMALL_PIECE) * 8
        out[n] = pack[r:r + rows].reshape(-1)[:size].reshape(shapes[n])
        r += rows
    return out


def _small_names(grads):
    return tuple(n for n in _SMALL + ("loss",) if n in grads)


def _small_start(tag, grads, narrow):
    pack = _pack_small(grads, _small_names(grads))
    if narrow:
        pack = _pad_rows(pack.astype(BF16), PACK_ROW_ALIGN)
    handles, token = _exchange_start(f"small_start_{tag}", [pack], [_landing(pack.shape, pack.dtype)], scatter=False)
    return handles, {n: grads[n].shape for n in _small_names(grads)}, token


def _small_finish(tag, handles, shapes, after):
    landed = _exchange_wait(f"small_wait_{tag}", handles, after, scatter=False)[0]
    return _unpack_small(_sum8(f"sum_small_{tag}", landed), shapes, _small_names(shapes))


def _as2d(a):
    if a.ndim == 1:
        return a.reshape(1, -1)
    return a.reshape(-1, a.shape[-1])


def kernel(x, ff1_norm, ff1_wg, ff1_wu, ff1_wd, mix_norm, w_in, sgu_ln_g, sgu_ln_b, sgu_w, sgu_b, lru_conv_w, lru_conv_b, lru_wa, lru_ba, lru_wx, lru_bx, lru_lambda, gdn_conv_w, gdn_a_log, gdn_dt_bias, gdn_norm_g, pool_w, pool_scale, w_branch, w_out, ff2_norm, ff2_wg, ff2_wu, ff2_wd, final_norm, loss_target, m_ff1_norm, m_ff1_wg, m_ff1_wu, m_ff1_wd, m_mix_norm, m_w_in, m_sgu_ln_g, m_sgu_ln_b, m_sgu_w, m_sgu_b, m_lru_conv_w, m_lru_conv_b, m_lru_wa, m_lru_ba, m_lru_wx, m_lru_bx, m_lru_lambda, m_gdn_conv_w, m_gdn_a_log, m_gdn_dt_bias, m_gdn_norm_g, m_pool_w, m_pool_scale, m_w_branch, m_w_out, m_ff2_norm, m_ff2_wg, m_ff2_wu, m_ff2_wd, m_final_norm, v_ff1_norm, v_ff1_wg, v_ff1_wu, v_ff1_wd, v_mix_norm, v_w_in, v_sgu_ln_g, v_sgu_ln_b, v_sgu_w, v_sgu_b, v_lru_conv_w, v_lru_conv_b, v_lru_wa, v_lru_ba, v_lru_wx, v_lru_bx, v_lru_lambda, v_gdn_conv_w, v_gdn_a_log, v_gdn_dt_bias, v_gdn_norm_g, v_pool_w, v_pool_scale, v_w_branch, v_w_out, v_ff2_norm, v_ff2_wg, v_ff2_wu, v_ff2_wd, v_final_norm):
    w = dict(ff1_norm=ff1_norm, ff1_wg=ff1_wg, ff1_wu=ff1_wu, ff1_wd=ff1_wd, mix_norm=mix_norm, w_in=w_in,
             sgu_ln_g=sgu_ln_g, sgu_ln_b=sgu_ln_b, sgu_w=sgu_w, sgu_b=sgu_b, lru_conv_w=lru_conv_w,
             lru_conv_b=lru_conv_b, lru_wa=lru_wa, lru_ba=lru_ba, lru_wx=lru_wx, lru_bx=lru_bx, lru_lambda=lru_lambda,
             gdn_conv_w=gdn_conv_w, gdn_a_log=gdn_a_log, gdn_dt_bias=gdn_dt_bias, gdn_norm_g=gdn_norm_g, pool_w=pool_w,
             pool_scale=pool_scale, w_branch=w_branch, w_out=w_out, ff2_norm=ff2_norm, ff2_wg=ff2_wg, ff2_wu=ff2_wu,
             ff2_wd=ff2_wd, final_norm=final_norm)
    m = dict(ff1_norm=m_ff1_norm, ff1_wg=m_ff1_wg, ff1_wu=m_ff1_wu, ff1_wd=m_ff1_wd, mix_norm=m_mix_norm, w_in=m_w_in,
             sgu_ln_g=m_sgu_ln_g, sgu_ln_b=m_sgu_ln_b, sgu_w=m_sgu_w, sgu_b=m_sgu_b, lru_conv_w=m_lru_conv_w,
             lru_conv_b=m_lru_conv_b, lru_wa=m_lru_wa, lru_ba=m_lru_ba, lru_wx=m_lru_wx, lru_bx=m_lru_bx,
             lru_lambda=m_lru_lambda, gdn_conv_w=m_gdn_conv_w, gdn_a_log=m_gdn_a_log, gdn_dt_bias=m_gdn_dt_bias,
             gdn_norm_g=m_gdn_norm_g, pool_w=m_pool_w, pool_scale=m_pool_scale, w_branch=m_w_branch, w_out=m_w_out,
             ff2_norm=m_ff2_norm, ff2_wg=m_ff2_wg, ff2_wu=m_ff2_wu, ff2_wd=m_ff2_wd, final_norm=m_final_norm)
    v = dict(ff1_norm=v_ff1_norm, ff1_wg=v_ff1_wg, ff1_wu=v_ff1_wu, ff1_wd=v_ff1_wd, mix_norm=v_mix_norm, w_in=v_w_in,
             sgu_ln_g=v_sgu_ln_g, sgu_ln_b=v_sgu_ln_b, sgu_w=v_sgu_w, sgu_b=v_sgu_b, lru_conv_w=v_lru_conv_w,
             lru_conv_b=v_lru_conv_b, lru_wa=v_lru_wa, lru_ba=v_lru_ba, lru_wx=v_lru_wx, lru_bx=v_lru_bx,
             lru_lambda=v_lru_lambda, gdn_conv_w=v_gdn_conv_w, gdn_a_log=v_gdn_a_log, gdn_dt_bias=v_gdn_dt_bias,
             gdn_norm_g=v_gdn_norm_g, pool_w=v_pool_w, pool_scale=v_pool_scale, w_branch=v_w_branch, w_out=v_w_out,
             ff2_norm=v_ff2_norm, ff2_wg=v_ff2_wg, ff2_wu=v_ff2_wu, ff2_wd=v_ff2_wd, final_norm=v_final_norm)

    first, got_first = _gather_first(w)
    handles, token = _gather_start(w, got_first)
    conv = _gather_conv_finish(w, handles, token)
    pending = {}

    def get_weights(l, sub, after):
        return _gather_finish(l, sub, handles, first, after)

    def put_grads(l, sub, grads):
        hs, meta, tok = _scatter_start(l, sub, grads)
        pending[(l, sub, meta[0])] = (hs, meta)
        return tok

    def put_small(tag, grads, narrow):
        hs, shapes, tok = _small_start(tag, grads, narrow)
        pending[tag] = (hs, shapes)
        return tok

    T = x.shape[1]
    dx = _forward_backward(x.reshape(T, D), loss_target.reshape(T, D), w, conv, get_weights, put_grads, put_small,
                           token)
    per = {}
    for key in pending:
        if isinstance(key, tuple):
            per.setdefault(key[:2], {}).update(_scatter_finish(*key[:2], *pending[key], dx))
        else:
            per[key] = _small_finish(key, *pending[key], dx)
    grad = {n: jnp.stack([per[(0, sub)][n], per[(1, sub)][n]]) for sub, (_, names) in enumerate(_GROUPS) for n in names}
    layer0 = {**per["0a"], **per["0b"]}
    small = {n: _join([layer0[n].reshape(-1), per["1"][n].reshape(-1)]).reshape((2,) + layer0[n].shape)
             for n in layer0}
    small["final_norm"] = per["1"]["final_norm"]
    loss = per["1"]["loss"][0]
    me = _my_index()
    for n in _SMALL:
        if n in _CONV_SHARDED:
            width = w[n].shape[-1]
            grad[n] = lax.dynamic_slice_in_dim(small[n], me * width, width, axis=2)
        else:
            grad[n] = small[n]

    delta, new_m, new_v = {}, {}, {}
    for n in _BIG:
        d_, m_, v_ = _adamw("adamw_" + n, _as2d(w[n]), _as2d(grad[n]), _as2d(m[n]), _as2d(v[n]))
        delta[n], new_m[n], new_v[n] = (t.reshape(w[n].shape) for t in (d_, m_, v_))

    outs = _adamw_many("adamw_small", *[[_as2d(t[n]) for n in _SMALL] for t in (w, grad, m, v)])
    for k, dst in enumerate((delta, new_m, new_v)):
        for i, n in enumerate(_SMALL):
            dst[n] = outs[k * len(_SMALL) + i].reshape(w[n].shape)

    return (loss, dx.reshape(x.shape), *[grad[n] for n in _WEIGHTS], *[delta[n] for n in _WEIGHTS],
            *[new_m[n] for n in _WEIGHTS], *[new_v[n] for n in _WEIGHTS])
```

```python
import functools

import jax
import jax.numpy as jnp
from jax import lax
from jax.experimental import pallas as pl
from jax.experimental.pallas import tpu as pltpu

F32 = jnp.float32
BF16 = jnp.bfloat16
HI = lax.Precision.HIGHEST

N_DEV = 8
D = 1024
FF = 2816
BW = 512
NBR = 4
CHUNK = 64
EPS = 1e-6
LRU_C = 8.0
GDN_DK = 128

COL_AU, COL_AV, COL_BX, COL_BG = 0, 512, 1024, 1536
COL_CQ, COL_CK, COL_CV, COL_CZ = 2048, 2560, 3072, 3584
COL_DX, COL_GATE, COL_TAIL = 4096, 4608, 8704
PW = 9216
P_IN = 8712

ADAM_LR, ADAM_B1, ADAM_B2, ADAM_EPS, ADAM_WD, ADAM_STEP = 0.001, 0.9, 0.999, 1e-08, 0.01, 10

VMEM_LIMIT_V7X = 56 * 1024 * 1024

_NN = (((1,), (0,)), ((), ()))
_NT = (((1,), (1,)), ((), ()))
_TN = (((0,), (0,)), ((), ()))


def _cp(*sem):
    return pltpu.CompilerParams(dimension_semantics=tuple(sem), vmem_limit_bytes=VMEM_LIMIT_V7X)


def _dot(a, b, dims=_NN):
    return lax.dot_general(a.astype(BF16), b.astype(BF16), dims, preferred_element_type=F32)


def _dot_hi(a, b, dims=_NN):
    return lax.dot_general(a, b, dims, precision=HI, preferred_element_type=F32)


def _pick(n, cands):
    for c in cands:
        if n % c == 0:
            return c
    return n


@jax.custom_jvp
def _log1p(x):
    u = 1.0 + x
    return jnp.where(u == 1.0, x, x * jnp.log(u) / jnp.where(u == 1.0, 1.0, u - 1.0))


@_log1p.defjvp
def _log1p_jvp(p, t):
    (x,), (dx,) = p, t
    return _log1p(x), dx / (1.0 + x)


@jax.custom_jvp
def _expm1(x):
    u = jnp.exp(x)
    lu = jnp.log(u)
    small = (u == 1.0) | (lu == 0.0)
    return jnp.where(small, x, (u - 1.0) * x / jnp.where(small, 1.0, lu))


@_expm1.defjvp
def _expm1_jvp(p, t):
    (x,), (dx,) = p, t
    return _expm1(x), dx * jnp.exp(x)


def _softplus(x):
    return jnp.maximum(x, 0.0) + _log1p(jnp.exp(-jnp.abs(x)))


def _sigmoid(x):
    return jax.nn.sigmoid(x)


def _silu(x):
    return x * jax.nn.sigmoid(x)


def _gelu(x):
    return jax.nn.gelu(x)


@functools.partial(jax.custom_vjp, nondiff_argnums=(1,))
def _shift(x, s):
    return x if s == 0 else pltpu.roll(x, s, 0)


def _shift_fwd(x, s):
    return _shift(x, s), None


def _shift_bwd(s, _, g):
    n = g.shape[0]
    return (g if s == 0 else pltpu.roll(g, n - s, 0),)


_shift.defvjp(_shift_fwd, _shift_bwd)


def _scan_steps(a, b, reverse):
    n = a.shape[0]
    row = lax.broadcasted_iota(jnp.int32, a.shape, 0)
    k = 1
    while k < n:
        sh = n - k if reverse else k
        m = (row < n - k) if reverse else (row >= k)
        a_s = jnp.where(m, pltpu.roll(a, sh, 0), 1.0)
        b_s = jnp.where(m, pltpu.roll(b, sh, 0), 0.0)
        b = a * b_s + b
        a = a * a_s
        k *= 2
    return b


@jax.custom_vjp
def _scan(a, b):
    return _scan_steps(a, b, False)


def _scan_fwd(a, b):
    h = _scan_steps(a, b, False)
    return h, (a, h)


def _scan_bwd(res, dh):
    a, h = res
    n = a.shape[0]
    row = lax.broadcasted_iota(jnp.int32, a.shape, 0)
    a_next = jnp.where(row < n - 1, pltpu.roll(a, n - 1, 0), 0.0)
    g = _scan_steps(a_next, dh, True)
    h_prev = jnp.where(row >= 1, pltpu.roll(h, 1, 0), 0.0)
    return g * h_prev, g


_scan.defvjp(_scan_fwd, _scan_bwd)


def _mm(name, pairs, mode, out_dtype, *, res=None, scale=1.0, bm=None, bn=None, bk=None, after=None):
    a0, b0 = pairs[0]
    if mode == "nn":
        (M, K), N = a0.shape, b0.shape[1]
    elif mode == "nt":
        (M, K), N = a0.shape, b0.shape[0]
    else:
        (K, M), N = a0.shape, b0.shape[1]
    bm = bm or _pick(M, (1024, 512, 256, 128))
    bn = bn or _pick(N, (1024, 512, 256, 128))
    bk = bk or _pick(K, (1024, 512, 1408, 256, 128))
    nk = K // bk
    npair = len(pairs)
    dims = {"nn": _NN, "nt": _NT, "tn": _TN}[mode]

    def body(*refs):
        ab = refs[:2 * npair]
        pos = 2 * npair
        r_ref = None
        if res is not None:
            r_ref = refs[pos]
            pos += 1
        pos += after is not None
        o_ref = refs[pos]
        part = None
        for p in range(npair):
            d = _dot(ab[2 * p][...], ab[2 * p + 1][...], dims)
            part = d if part is None else part + d

        def finish(acc):
            out = acc if scale == 1.0 else acc * scale
            if r_ref is not None:
                out = out + r_ref[...]
            o_ref[...] = out.astype(out_dtype)

        if nk == 1:
            finish(part)
        else:
            acc_ref = refs[pos + 1]
            k = pl.program_id(2)

            @pl.when(k == 0)
            def _():
                acc_ref[...] = part

            @pl.when(k > 0)
            def _():
                acc_ref[...] += part

            @pl.when(k == nk - 1)
            def _():
                finish(acc_ref[...])

    if mode == "nn":
        a_spec = pl.BlockSpec((bm, bk), lambda i, j, k: (i, k))
        b_spec = pl.BlockSpec((bk, bn), lambda i, j, k: (k, j))
    elif mode == "nt":
        a_spec = pl.BlockSpec((bm, bk), lambda i, j, k: (i, k))
        b_spec = pl.BlockSpec((bn, bk), lambda i, j, k: (j, k))
    else:
        a_spec = pl.BlockSpec((bk, bm), lambda i, j, k: (k, i))
        b_spec = pl.BlockSpec((bk, bn), lambda i, j, k: (k, j))
    o_spec = pl.BlockSpec((bm, bn), lambda i, j, k: (i, j))
    in_specs, args = [], []
    for a, b in pairs:
        in_specs += [a_spec, b_spec]
        args += [a, b]
    if res is not None:
        in_specs.append(o_spec)
        args.append(res)
    if after is not None:
        in_specs.append(_ANY)
        args.append(after)
    return pl.pallas_call(
        body, name=name, grid=(M // bm, N // bn, nk),
        in_specs=in_specs, out_specs=o_spec,
        out_shape=jax.ShapeDtypeStruct((M, N), out_dtype),
        scratch_shapes=[pltpu.VMEM((bm, bn), F32)] if nk > 1 else [],
        compiler_params=_cp("parallel", "parallel", "arbitrary"),
    )(*args)


def _rms_fwd(name, x, g):
    T = x.shape[0]
    bm = _pick(T, (1024, 512, 256, 128))

    def body(x_ref, g_ref, o_ref):
        xv = x_ref[...]
        r = lax.rsqrt(jnp.mean(xv * xv, axis=-1, keepdims=True) + EPS)
        o_ref[...] = (xv * r * g_ref[...]).astype(BF16)

    return pl.pallas_call(
        body, name=name, grid=(T // bm,),
        in_specs=[pl.BlockSpec((bm, D), lambda i: (i, 0)), pl.BlockSpec((1, D), lambda i: (0, 0))],
        out_specs=pl.BlockSpec((bm, D), lambda i: (i, 0)),
        out_shape=jax.ShapeDtypeStruct((T, D), BF16),
        compiler_params=_cp("parallel"),
    )(x, g)


def _rms_bwd(name, x, g, dh, dres):
    T = x.shape[0]
    bm = _pick(T, (1024, 512, 256, 128))

    def body(x_ref, g_ref, dh_ref, dres_ref, dx_ref, dg_ref):
        xv = x_ref[...]
        r = lax.rsqrt(jnp.mean(xv * xv, axis=-1, keepdims=True) + EPS)
        xh = xv * r
        dhv = dh_ref[...]
        dxh = dhv * g_ref[...]
        dx_ref[...] = dres_ref[...] + r * (dxh - xh * jnp.mean(dxh * xh, axis=-1, keepdims=True))
        part = jnp.sum(dhv * xh, axis=0, keepdims=True)

        @pl.when(pl.program_id(0) == 0)
        def _():
            dg_ref[...] = part

        @pl.when(pl.program_id(0) > 0)
        def _():
            dg_ref[...] += part

    row = pl.BlockSpec((bm, D), lambda i: (i, 0))
    vec = pl.BlockSpec((1, D), lambda i: (0, 0))
    return pl.pallas_call(
        body, name=name, grid=(T // bm,),
        in_specs=[row, vec, row, row], out_specs=[row, vec],
        out_shape=[jax.ShapeDtypeStruct((T, D), F32), jax.ShapeDtypeStruct((1, D), F32)],
        compiler_params=_cp("arbitrary"),
    )(x, g, dh, dres)


def _final_loss(name, x, g, tgt):
    T = x.shape[0]
    bm = _pick(T, (512, 256, 128))

    def body(x_ref, g_ref, t_ref, loss_ref, dx_ref, dg_ref):
        xv = x_ref[...]
        gv = g_ref[...]
        r = lax.rsqrt(jnp.mean(xv * xv, axis=-1, keepdims=True) + EPS)
        xh = xv * r
        e = xh * gv - t_ref[...]
        lpart = jnp.broadcast_to(0.5 * jnp.sum(jnp.mean(e * e, axis=-1, keepdims=True), axis=0, keepdims=True), (1, 128))
        dy = e * (1.0 / D)
        dxh = dy * gv
        dx_ref[...] = r * (dxh - xh * jnp.mean(dxh * xh, axis=-1, keepdims=True))
        gpart = jnp.sum(dy * xh, axis=0, keepdims=True)

        @pl.when(pl.program_id(0) == 0)
        def _():
            loss_ref[...] = lpart
            dg_ref[...] = gpart

        @pl.when(pl.program_id(0) > 0)
        def _():
            loss_ref[...] += lpart
            dg_ref[...] += gpart

    row = pl.BlockSpec((bm, D), lambda i: (i, 0))
    vec = pl.BlockSpec((1, D), lambda i: (0, 0))
    return pl.pallas_call(
        body, name=name, grid=(T // bm,),
        in_specs=[row, vec, row],
        out_specs=[pl.BlockSpec((1, 128), lambda i: (0, 0)), row, vec],
        out_shape=[jax.ShapeDtypeStruct((1, 128), F32), jax.ShapeDtypeStruct((T, D), F32),
                   jax.ShapeDtypeStruct((1, D), F32)],
        compiler_params=_cp("arbitrary"),
    )(x, g, tgt)


def _ffn_up(name, h, wg, wu):
    T = h.shape[0]
    bm = _pick(T, (2048, 1024, 512, 256, 128))
    bn = 256

    def body(h_ref, wg_ref, wu_ref, sa_ref, ds_ref, act_ref):
        hv = h_ref[...]
        a = _dot(hv, wg_ref[...], _NT)
        b = _dot(hv, wu_ref[...], _NT)
        s = _sigmoid(a)
        sa = a * s
        sa_ref[...] = sa.astype(BF16)
        ds_ref[...] = (b * (s * (1.0 + a * (1.0 - s)))).astype(BF16)
        act_ref[...] = (sa * b).astype(BF16)

    w_spec = pl.BlockSpec((bn, D), lambda i, j: (j, 0))
    o_spec = pl.BlockSpec((bm, bn), lambda i, j: (i, j))
    return pl.pallas_call(
        body, name=name, grid=(T // bm, FF // bn),
        in_specs=[pl.BlockSpec((bm, D), lambda i, j: (i, 0)), w_spec, w_spec],
        out_specs=[o_spec, o_spec, o_spec],
        out_shape=[jax.ShapeDtypeStruct((T, FF), BF16)] * 3,
        compiler_params=_cp("parallel", "parallel"),
    )(h, wg, wu)


def _ffn_dact(name, dy, wd, sa, ds, after=None):
    T = dy.shape[0]
    bm = _pick(T, (2048, 1024, 512, 256, 128))
    bn = 256

    def body(dy_ref, wd_ref, sa_ref, ds_ref, *rest):
        da_ref, db_ref, dy_bf = rest[-3:]

        @pl.when(pl.program_id(1) == 0)
        def _():
            dy_bf[...] = dy_ref[...].astype(BF16)

        dact = 0.5 * _dot(dy_bf[...], wd_ref[...], _NT)
        da_ref[...] = (dact * ds_ref[...].astype(F32)).astype(BF16)
        db_ref[...] = (dact * sa_ref[...].astype(F32)).astype(BF16)

    t_spec = pl.BlockSpec((bm, bn), lambda i, j: (i, j))
    return pl.pallas_call(
        body, name=name, grid=(T // bm, FF // bn),
        in_specs=[pl.BlockSpec((bm, D), lambda i, j: (i, 0)), pl.BlockSpec((bn, D), lambda i, j: (j, 0)),
                  t_spec, t_spec] + [_ANY] * (after is not None),
        out_specs=[t_spec, t_spec],
        out_shape=[jax.ShapeDtypeStruct((T, FF), BF16), jax.ShapeDtypeStruct((T, FF), BF16)],
        scratch_shapes=[pltpu.VMEM((bm, D), BF16)],
        compiler_params=_cp("parallel", "arbitrary"),
    )(dy, wd, sa, ds, *([after] if after is not None else []))


def _merge_specs(T, bm, bn):
    y_spec = pl.BlockSpec((bm, BW), lambda i, j: (i, 0))
    wb_spec = pl.BlockSpec((NBR, bn, BW), lambda i, j: (0, j, 0))
    gate_specs = [pl.BlockSpec((bm, bn), functools.partial(lambda i, j, o: (i, o + j), o=(COL_GATE + g * D) // bn))
                  for g in range(NBR)]
    t_spec = pl.BlockSpec((bm, bn), lambda i, j: (i, j))
    return y_spec, wb_spec, gate_specs, t_spec


def _merge_fwd(name, ys, wb, proj):
    T = proj.shape[0]
    bm = _pick(T, (1024, 512, 256, 128))
    bn = 512
    y_spec, wb_spec, gate_specs, t_spec = _merge_specs(T, bm, bn)

    def body(y0, y1, y2, y3, wb_ref, g0, g1, g2, g3, o_ref):
        acc = None
        for g, (y_ref, g_ref) in enumerate(((y0, g0), (y1, g1), (y2, g2), (y3, g3))):
            t = _sigmoid(g_ref[...].astype(F32)) * _dot(y_ref[...], wb_ref[g], _NT)
            acc = t if acc is None else acc + t
        o_ref[...] = acc.astype(BF16)

    return pl.pallas_call(
        body, name=name, grid=(T // bm, D // bn),
        in_specs=[y_spec] * NBR + [wb_spec] + gate_specs, out_specs=t_spec,
        out_shape=jax.ShapeDtypeStruct((T, D), BF16),
        compiler_params=_cp("parallel", "parallel"),
    )(*ys, wb, proj, proj, proj, proj)


def _merge_bwd(name, dm, ys, wb, proj):
    T = proj.shape[0]
    bm = _pick(T, (512, 256, 128))
    bn = 512
    y_spec, wb_spec, gate_specs, t_spec = _merge_specs(T, bm, bn)

    def body(dm_ref, y0, y1, y2, y3, wb_ref, g0, g1, g2, g3, *outs):
        dmv = dm_ref[...]
        j = pl.program_id(1)
        for g, (y_ref, g_ref) in enumerate(((y0, g0), (y1, g1), (y2, g2), (y3, g3))):
            br = _dot(y_ref[...], wb_ref[g], _NT)
            s = _sigmoid(g_ref[...].astype(F32))
            outs[g][...] = (dmv * br * (s * (1.0 - s))).astype(BF16)
            dbr = (dmv * s).astype(BF16)
            outs[NBR + g][...] = dbr
            part = _dot(dbr, wb_ref[g])
            dy_ref = outs[2 * NBR + g]

            @pl.when(j == 0)
            def _():
                dy_ref[...] = part

            @pl.when(j > 0)
            def _():
                dy_ref[...] += part

    return pl.pallas_call(
        body, name=name, grid=(T // bm, D // bn),
        in_specs=[t_spec] + [y_spec] * NBR + [wb_spec] + gate_specs, out_specs=[t_spec] * (2 * NBR) + [y_spec] * NBR,
        out_shape=[jax.ShapeDtypeStruct((T, D), BF16)] * (2 * NBR) + [jax.ShapeDtypeStruct((T, BW), F32)] * NBR,
        compiler_params=_cp("parallel", "arbitrary"),
    )(dm, *ys, wb, proj, proj, proj, proj)


def _dwb(name, dbrs, ys):
    T = ys[0].shape[0]
    bk = _pick(T, (1024, 512, 256, 128))
    nk = T // bk

    def body(*refs):
        d_refs, y_refs, o_ref, acc = refs[:NBR], refs[NBR:2 * NBR], refs[2 * NBR], refs[2 * NBR + 1]
        k = pl.program_id(0)
        for g in range(NBR):
            part = _dot(d_refs[g][...], y_refs[g][...], _TN)

            @pl.when(k == 0)
            def _(g=g, part=part):
                acc[g] = part

            @pl.when(k > 0)
            def _(g=g, part=part):
                acc[g] += part

        @pl.when(k == nk - 1)
        def _():
            o_ref[...] = acc[...].astype(BF16)

    return pl.pallas_call(
        body, name=name, grid=(nk,),
        in_specs=[pl.BlockSpec((bk, D), lambda k: (k, 0))] * NBR + [pl.BlockSpec((bk, BW), lambda k: (k, 0))] * NBR,
        out_specs=pl.BlockSpec((NBR, D, BW), lambda k: (0, 0, 0)),
        out_shape=jax.ShapeDtypeStruct((NBR, D, BW), BF16),
        scratch_shapes=[pltpu.VMEM((NBR, D, BW), F32)],
        compiler_params=_cp("arbitrary"),
    )(*dbrs, *ys)


def _sgu_block(u_pre, v_pre, ln_g, ln_b, w, bias):
    u = _gelu(u_pre)
    vf = _gelu(v_pre)
    mu = jnp.mean(vf, axis=-1, keepdims=True)
    var = jnp.mean(jnp.square(vf - mu), axis=-1, keepdims=True)
    vn = (vf - mu) * lax.rsqrt(var + EPS) * ln_g + ln_b
    ri = lax.broadcasted_iota(jnp.int32, (128, 128), 0)
    ci = lax.broadcasted_iota(jnp.int32, (128, 128), 1)
    mask = (ri // CHUNK) >= (ci // CHUNK)
    outs = [_dot(jnp.where(mask, w[g], 0.0), vn[:, g * 128:(g + 1) * 128]) for g in range(4)]
    mixed = jnp.concatenate(outs, axis=1) + bias
    return u * mixed


def _sgu_param_specs():
    return [pl.BlockSpec((1, BW), lambda i: (0, 0)), pl.BlockSpec((1, BW), lambda i: (0, 0)),
            pl.BlockSpec((4, 128, 128), lambda i: (0, 0, 0)), pl.BlockSpec((128, BW), lambda i: (0, 0))]


def _sgu_fwd(name, proj, ln_g, ln_b, w, bias):
    T = proj.shape[0]
    rb = _pick(T, (512, 256, 128))

    def body(u_ref, v_ref, g_ref, b_ref, w_ref, bias_ref, y_ref):
        for n in range(rb // 128):
            rows = slice(n * 128, (n + 1) * 128)
            y = _sgu_block(u_ref[rows, :].astype(F32), v_ref[rows, :].astype(F32), g_ref[...], b_ref[...], w_ref[...],
                           bias_ref[...])
            y_ref[rows, :] = y.astype(BF16)

    return pl.pallas_call(
        body, name=name, grid=(T // rb,),
        in_specs=[pl.BlockSpec((rb, BW), lambda i: (i, COL_AU // BW)), pl.BlockSpec((rb, BW), lambda i: (i, COL_AV // BW))]
        + _sgu_param_specs(),
        out_specs=pl.BlockSpec((rb, BW), lambda i: (i, 0)),
        out_shape=jax.ShapeDtypeStruct((T, BW), BF16),
        compiler_params=_cp("parallel"),
    )(proj, proj, ln_g, ln_b, w, bias)


def _sgu_bwd(name, proj, dy, ln_g, ln_b, w, bias):
    T = proj.shape[0]
    rb = _pick(T, (512, 256, 128))

    def body(u_ref, v_ref, dy_ref, g_ref, b_ref, w_ref, bias_ref, du_ref, dv_ref, dg_ref, db_ref, dw_ref, dbias_ref):
        acc = None
        for n in range(rb // 128):
            rows = slice(n * 128, (n + 1) * 128)
            _, vjp = jax.vjp(_sgu_block, u_ref[rows, :].astype(F32), v_ref[rows, :].astype(F32), g_ref[...], b_ref[...],
                             w_ref[...],
                             bias_ref[...])
            du, dv, *dp = vjp(dy_ref[rows, :])
            du_ref[rows, :] = du.astype(BF16)
            dv_ref[rows, :] = dv.astype(BF16)
            acc = dp if acc is None else [p + q for p, q in zip(acc, dp)]

        @pl.when(pl.program_id(0) == 0)
        def _():
            for r, p in zip((dg_ref, db_ref, dw_ref, dbias_ref), acc):
                r[...] = p

        @pl.when(pl.program_id(0) > 0)
        def _():
            for r, p in zip((dg_ref, db_ref, dw_ref, dbias_ref), acc):
                r[...] += p

    row = pl.BlockSpec((rb, BW), lambda i: (i, 0))
    return pl.pallas_call(
        body, name=name, grid=(T // rb,),
        in_specs=[pl.BlockSpec((rb, BW), lambda i: (i, COL_AU // BW)), pl.BlockSpec((rb, BW), lambda i: (i, COL_AV // BW)),
                  row] + _sgu_param_specs(),
        out_specs=[row, row] + _sgu_param_specs(),
        out_shape=[jax.ShapeDtypeStruct((T, BW), BF16), jax.ShapeDtypeStruct((T, BW), BF16),
                   jax.ShapeDtypeStruct((1, BW), F32), jax.ShapeDtypeStruct((1, BW), F32),
                   jax.ShapeDtypeStruct((4, 128, 128), F32), jax.ShapeDtypeStruct((128, BW), F32)],
        compiler_params=_cp("arbitrary"),
    )(proj, proj, dy, ln_g, ln_b, w, bias)


def _halo_block(ref, i, rblk, halo):
    r0 = pl.multiple_of(i * rblk, rblk)
    h0 = pl.multiple_of(jnp.maximum(r0 - 16, 0), 16)
    top = jnp.where(i > 0, ref[pl.ds(h0, 16), :].astype(F32), 0.0)[16 - halo:]
    return jnp.concatenate([top, ref[pl.ds(r0, rblk), :].astype(F32)], axis=0)


def _with_halo_grad(dfull, pending, halo, rblk):
    tail = jnp.concatenate([jnp.zeros((rblk - halo, 128), F32), pending], axis=0)
    return dfull[halo:] + tail


def _conv4(xfull, rows):
    acc = None
    for k in range(4):
        t = rows[k] * _shift(xfull, 3 - k)[8:]
        acc = t if acc is None else acc + t
    return acc


def _lru_block(xfull, gate, h0, c0, c1, c2, c3, cb, wa, ba, wx, bx, lam):
    n = gate.shape[0]
    xc = _conv4(xfull, (c0, c1, c2, c3)) + cb
    r = _sigmoid(_dot(xc, wa) + ba)
    ig = _sigmoid(_dot(xc, wx) + bx)
    log_a = -LRU_C * r * _softplus(-lam)
    a = jnp.exp(log_a)
    mult = jnp.sqrt(-_expm1(2.0 * log_a))
    b = mult * (ig * xc)
    row = lax.broadcasted_iota(jnp.int32, (n, 128), 0)
    b = b + jnp.where(row == 0, a * h0, 0.0)
    h = _scan(a, b)
    out = h * _gelu(gate)
    h_last = jnp.sum(jnp.where(row == n - 1, h, 0.0), axis=0, keepdims=True)
    return out, h_last


def _lru_param_specs():
    vec = pl.BlockSpec((1, 128), lambda g: (0, g))
    mat = pl.BlockSpec((None, 128, 128), lambda g: (g, 0, 0))
    return [pl.BlockSpec((4, 128), lambda g: (0, g)), vec, mat, vec, mat, vec, vec]


def _lru_load_params(cw_ref, cb_ref, wa_ref, ba_ref, wx_ref, bx_ref, lam_ref):
    return (cw_ref[0:1, :], cw_ref[1:2, :], cw_ref[2:3, :], cw_ref[3:4, :], cb_ref[...], wa_ref[...], ba_ref[...],
            wx_ref[...], bx_ref[...], lam_ref[...])


def _lru_fwd(name, proj, cw, cb, wa, ba, wx, bx, lam):
    T = proj.shape[0]
    rblk = _pick(T, (256, 128))
    nblk = T // rblk

    def body(x_ref, gt_ref, cw_ref, cb_ref, wa_ref, ba_ref, wx_ref, bx_ref, lam_ref, y_ref, hc_ref):
        params = _lru_load_params(cw_ref, cb_ref, wa_ref, ba_ref, wx_ref, bx_ref, lam_ref)

        def step(i, h0):
            r0 = pl.multiple_of(i * rblk, rblk)
            out, h_last = _lru_block(_halo_block(x_ref, i, rblk, 8), gt_ref[pl.ds(r0, rblk), :].astype(F32), h0,
                                     *params)
            y_ref[pl.ds(r0, rblk), :] = out.astype(BF16)
            hc_ref[pl.ds(pl.multiple_of(i * 8, 8), 8), :] = jnp.broadcast_to(h0, (8, 128))
            return h_last

        lax.fori_loop(0, nblk, step, jnp.zeros((1, 128), F32))

    return pl.pallas_call(
        body, name=name, grid=(4,),
        in_specs=[pl.BlockSpec((T, 128), lambda g: (0, COL_BX // 128 + g)),
                  pl.BlockSpec((T, 128), lambda g: (0, COL_BG // 128 + g))] + _lru_param_specs(),
        out_specs=[pl.BlockSpec((T, 128), lambda g: (0, g)), pl.BlockSpec((nblk * 8, 128), lambda g: (0, g))],
        out_shape=[jax.ShapeDtypeStruct((T, BW), BF16), jax.ShapeDtypeStruct((nblk * 8, BW), F32)],
        compiler_params=_cp("parallel"),
    )(proj, proj, cw, cb, wa, ba, wx, bx, lam)


def _lru_bwd(name, proj, dy, hc, cw, cb, wa, ba, wx, bx, lam):
    T = proj.shape[0]
    rblk = _pick(T, (256, 128))
    nblk = T // rblk

    def body(x_ref, gt_ref, dy_ref, hc_ref, cw_ref, cb_ref, wa_ref, ba_ref, wx_ref, bx_ref, lam_ref,
             dx_ref, dgt_ref, dcw_ref, dcb_ref, dwa_ref, dba_ref, dwx_ref, dbx_ref, dlam_ref):
        params = _lru_load_params(cw_ref, cb_ref, wa_ref, ba_ref, wx_ref, bx_ref, lam_ref)

        def step(it, carry):
            dh_last, pending, acc = carry
            i = nblk - 1 - it
            r0 = pl.multiple_of(i * rblk, rblk)
            h0 = hc_ref[pl.ds(pl.multiple_of(i * 8, 8), 1), :]
            _, vjp = jax.vjp(_lru_block, _halo_block(x_ref, i, rblk, 8), gt_ref[pl.ds(r0, rblk), :].astype(F32), h0,
                             *params)
            dfull, dgate, dh0, *dp = vjp((dy_ref[pl.ds(r0, rblk), :], dh_last))
            dx_ref[pl.ds(r0, rblk), :] = _with_halo_grad(dfull, pending, 8, rblk).astype(BF16)
            dgt_ref[pl.ds(r0, rblk), :] = dgate.astype(BF16)
            return dh0, dfull[:8], tuple(p + q for p, q in zip(acc, dp))

        zeros = tuple(jnp.zeros(p.shape, F32) for p in params)
        _, _, acc = lax.fori_loop(0, nblk, step, (jnp.zeros((1, 128), F32), jnp.zeros((8, 128), F32), zeros))
        for k in range(4):
            dcw_ref[k:k + 1, :] = acc[k]
        for r, p in zip((dcb_ref, dwa_ref, dba_ref, dwx_ref, dbx_ref, dlam_ref), acc[4:]):
            r[...] = p

    col = pl.BlockSpec((T, 128), lambda g: (0, g))
    return pl.pallas_call(
        body, name=name, grid=(4,),
        in_specs=[pl.BlockSpec((T, 128), lambda g: (0, COL_BX // 128 + g)),
                  pl.BlockSpec((T, 128), lambda g: (0, COL_BG // 128 + g)), col,
                  pl.BlockSpec((nblk * 8, 128), lambda g: (0, g))] + _lru_param_specs(),
        out_specs=[col, col] + _lru_param_specs(),
        out_shape=[jax.ShapeDtypeStruct((T, BW), BF16), jax.ShapeDtypeStruct((T, BW), BF16),
                   jax.ShapeDtypeStruct((4, BW), F32), jax.ShapeDtypeStruct((1, BW), F32),
                   jax.ShapeDtypeStruct((4, 128, 128), F32), jax.ShapeDtypeStruct((1, BW), F32),
                   jax.ShapeDtypeStruct((4, 128, 128), F32), jax.ShapeDtypeStruct((1, BW), F32),
                   jax.ShapeDtypeStruct((1, BW), F32)],
        compiler_params=_cp("parallel"),
    )(proj, proj, dy, hc, cw, cb, wa, ba, wx, bx, lam)


def _conv_block(xfull, c0, c1, c2, c3):
    return _silu(_conv4(xfull, (c0, c1, c2, c3)))


def _conv_fwd(name, proj, col0, cw, cw_col0):
    T = proj.shape[0]
    rblk = _pick(T, (256, 128))
    nblk = T // rblk

    def body(x_ref, cw_ref, y_ref):
        rows = (cw_ref[0:1, :], cw_ref[1:2, :], cw_ref[2:3, :], cw_ref[3:4, :])

        def step(i, c):
            r0 = pl.multiple_of(i * rblk, rblk)
            y_ref[pl.ds(r0, rblk), :] = _conv_block(_halo_block(x_ref, i, rblk, 8), *rows)
            return c

        lax.fori_loop(0, nblk, step, 0)

    return pl.pallas_call(
        body, name=name, grid=(4,),
        in_specs=[pl.BlockSpec((T, 128), lambda g: (0, col0 // 128 + g)),
                  pl.BlockSpec((4, 128), lambda g: (0, cw_col0 // 128 + g))],
        out_specs=pl.BlockSpec((T, 128), lambda g: (0, g)),
        out_shape=jax.ShapeDtypeStruct((T, BW), F32),
        compiler_params=_cp("parallel"),
    )(proj, cw)


def _conv_bwd(name, proj, col0, dy, cw, cw_col0):
    T = proj.shape[0]
    rblk = _pick(T, (256, 128))
    nblk = T // rblk

    def body(x_ref, dy_ref, cw_ref, dx_ref, dcw_ref):
        rows = (cw_ref[0:1, :], cw_ref[1:2, :], cw_ref[2:3, :], cw_ref[3:4, :])

        def step(it, carry):
            pending, acc = carry
            i = nblk - 1 - it
            r0 = pl.multiple_of(i * rblk, rblk)
            _, vjp = jax.vjp(_conv_block, _halo_block(x_ref, i, rblk, 8), *rows)
            dfull, *dp = vjp(dy_ref[pl.ds(r0, rblk), :])
            dx_ref[pl.ds(r0, rblk), :] = _with_halo_grad(dfull, pending, 8, rblk).astype(BF16)
            return dfull[:8], tuple(p + q for p, q in zip(acc, dp))

        zeros = tuple(jnp.zeros((1, 128), F32) for _ in range(4))
        _, acc = lax.fori_loop(0, nblk, step, (jnp.zeros((8, 128), F32), zeros))
        for k in range(4):
            dcw_ref[k:k + 1, :] = acc[k]

    col = pl.BlockSpec((T, 128), lambda g: (0, g))
    return pl.pallas_call(
        body, name=name, grid=(4,),
        in_specs=[pl.BlockSpec((T, 128), lambda g: (0, col0 // 128 + g)), col,
                  pl.BlockSpec((4, 128), lambda g: (0, cw_col0 // 128 + g))],
        out_specs=[col, pl.BlockSpec((4, 128), lambda g: (0, g))],
        out_shape=[jax.ShapeDtypeStruct((T, BW), BF16), jax.ShapeDtypeStruct((4, BW), F32)],
        compiler_params=_cp("parallel"),
    )(proj, dy, cw)


def _pool_block(xfull, pw, sc, t0, gi):
    n = xfull.shape[0] - 16
    s2 = xfull + _shift(xfull, 1)
    s4 = s2 + _shift(s2, 2)
    s8 = s4 + _shift(s4, 4)
    s16 = s8 + _shift(s8, 8)
    s = jnp.where(gi == 0, s2, jnp.where(gi == 1, s4, jnp.where(gi == 2, s8, s16)))[16:]
    t = t0 + lax.broadcasted_iota(jnp.int32, (n, 128), 0)
    cnt = jnp.minimum(t + 1, lax.shift_left(jnp.int32(2), gi)).astype(F32)
    pooled = s / cnt - xfull[16:]
    return _dot(pooled, pw) * sc


def _pool_fwd(name, proj, pw, sc):
    T = proj.shape[0]
    rblk = _pick(T, (256, 128))
    nblk = T // rblk

    def body(x_ref, pw_ref, sc_ref, y_ref):
        gi = pl.program_id(0)

        def step(i, c):
            r0 = pl.multiple_of(i * rblk, rblk)
            y = _pool_block(_halo_block(x_ref, i, rblk, 16), pw_ref[...], sc_ref[...], r0, gi)
            y_ref[pl.ds(r0, rblk), :] = y.astype(BF16)
            return c

        lax.fori_loop(0, nblk, step, 0)

    return pl.pallas_call(
        body, name=name, grid=(4,),
        in_specs=[pl.BlockSpec((T, 128), lambda g: (0, COL_DX // 128 + g)),
                  pl.BlockSpec((None, 128, 128), lambda g: (g, 0, 0)), pl.BlockSpec((1, 128), lambda g: (0, g))],
        out_specs=pl.BlockSpec((T, 128), lambda g: (0, g)),
        out_shape=jax.ShapeDtypeStruct((T, BW), BF16),
        compiler_params=_cp("parallel"),
    )(proj, pw, sc)


def _pool_bwd(name, proj, dy, pw, sc):
    T = proj.shape[0]
    rblk = _pick(T, (256, 128))
    nblk = T // rblk

    def body(x_ref, dy_ref, pw_ref, sc_ref, dx_ref, dpw_ref, dsc_ref):
        gi = pl.program_id(0)

        def step(it, carry):
            pending, apw, asc = carry
            i = nblk - 1 - it
            r0 = pl.multiple_of(i * rblk, rblk)
            _, vjp = jax.vjp(lambda xf, w, s: _pool_block(xf, w, s, r0, gi), _halo_block(x_ref, i, rblk, 16),
                             pw_ref[...], sc_ref[...])
            dfull, dw, ds = vjp(dy_ref[pl.ds(r0, rblk), :])
            dx_ref[pl.ds(r0, rblk), :] = _with_halo_grad(dfull, pending, 16, rblk).astype(BF16)
            return dfull[:16], apw + dw, asc + ds

        _, apw, asc = lax.fori_loop(0, nblk, step, (jnp.zeros((16, 128), F32), jnp.zeros((128, 128), F32),
                                                    jnp.zeros((1, 128), F32)))
        dpw_ref[...] = apw
        dsc_ref[...] = asc

    col = pl.BlockSpec((T, 128), lambda g: (0, g))
    mat = pl.BlockSpec((None, 128, 128), lambda g: (g, 0, 0))
    vec = pl.BlockSpec((1, 128), lambda g: (0, g))
    return pl.pallas_call(
        body, name=name, grid=(4,),
        in_specs=[pl.BlockSpec((T, 128), lambda g: (0, COL_DX // 128 + g)), col, mat, vec],
        out_specs=[col, mat, vec],
        out_shape=[jax.ShapeDtypeStruct((T, BW), BF16), jax.ShapeDtypeStruct((4, 128, 128), F32),
                   jax.ShapeDtypeStruct((1, BW), F32)],
        compiler_params=_cp("parallel"),
    )(proj, dy, pw, sc)


@jax.custom_vjp
def _dot3(a, b):
    ah = a.astype(BF16)
    al = (a - ah.astype(F32)).astype(BF16)
    bh = b.astype(BF16)
    bl = (b - bh.astype(F32)).astype(BF16)

    def d(x, y):
        return lax.dot_general(x, y, _NN, preferred_element_type=F32)

    return d(ah, bh) + (d(ah, bl) + d(al, bh))


def _dot3_fwd(a, b):
    return _dot3(a, b), (a, b)


def _dot3_bwd(res, g):
    a, b = res
    return _dot(g, b, _NT), _dot(a, g, _TN)


_dot3.defvjp(_dot3_fwd, _dot3_bwd)


def _pad_rows2(x):
    return jnp.concatenate([x, jnp.zeros_like(x)], axis=0)


@jax.custom_vjp
def _tri_inv(mats):
    n = mats[0].shape[0]
    eye = (lax.broadcasted_iota(jnp.int32, (n, n), 0) == lax.broadcasted_iota(jnp.int32, (n, n), 1)).astype(F32)
    ps = [eye - a for a in mats]
    ms = list(mats)
    k = 2
    while k < n:
        ms = [_dot3(t, t) for t in ms]
        ps = [p + _dot3(p, t) for p, t in zip(ps, ms)]
        k *= 2
    return ps


def _tri_inv_fwd(mats):
    ts = _tri_inv(mats)
    return ts, ts


def _tri_inv_bwd(ts, gs):
    half = [_dot(t, g, _TN) for t, g in zip(ts, gs)]
    return ([-_dot(h, t, _NT) for h, t in zip(half, ts)],)


_tri_inv.defvjp(_tri_inv_fwd, _tri_inv_bwd)


def _cumsum_rows(x):
    n = x.shape[0]
    row = lax.broadcasted_iota(jnp.int32, x.shape, 0)
    k = 1
    while k < n:
        x = x + jnp.where(row >= k, _shift(x, k), 0.0)
        k *= 2
    return x


def _gdn_prep(qcs, kcs, vcs, tails, alog, dtb):
    C = CHUNK
    pairs = [(c, h) for c in range(len(qcs)) for h in range(4)]
    lane = lax.broadcasted_iota(jnp.int32, (C, 128), 1)
    row = lax.broadcasted_iota(jnp.int32, (C, 128), 0)
    incl = row >= lane
    sig = [_sigmoid(t) for t in tails]
    gfull = [-jnp.exp(alog) * _softplus(t + dtb) for t in tails]
    beta = [jnp.sum(jnp.where(lane == h, sig[c], 0.0), axis=1, keepdims=True) for c, h in pairs]
    g = [jnp.sum(jnp.where(lane == h + 4, gfull[c], 0.0), axis=1, keepdims=True) for c, h in pairs]
    qs = [qcs[c][:, h * 128:(h + 1) * 128] for c, h in pairs]
    ks = [kcs[c][:, h * 128:(h + 1) * 128] for c, h in pairs]
    vs = [vcs[c][:, h * 128:(h + 1) * 128] for c, h in pairs]
    q = [t * lax.rsqrt(jnp.sum(t * t, axis=-1, keepdims=True) + EPS) * (GDN_DK ** -0.5) for t in qs]
    k = [t * lax.rsqrt(jnp.sum(t * t, axis=-1, keepdims=True) + EPS) for t in ks]
    gc = [_cumsum_rows(jnp.broadcast_to(t, (C, 128))) for t in g]
    gc_t = [jnp.transpose(jnp.concatenate([t, t], axis=0)) for t in gc]
    gc_col = [jnp.sum(jnp.where(lane == 0, t, 0.0), axis=1, keepdims=True) for t in gc]
    ri = lax.broadcasted_iota(jnp.int32, (C, C), 0)
    ci = lax.broadcasted_iota(jnp.int32, (C, C), 1)
    decay = [jnp.exp(jnp.where(incl, a - b[:C, :], -1e30)) for a, b in zip(gc, gc_t)]
    decay_sq = [jnp.exp(jnp.where(ri > ci, a - jnp.transpose(b)[:C, :], -1e30)) for a, b in zip(gc_col, gc)]
    kb = [a * b for a, b in zip(k, beta)]
    kk = [_dot(a, b, _NT) for a, b in zip(kb, k)]
    t_mat = _tri_inv([jnp.where(ri > ci, a * b, 0.0) for a, b in zip(kk, decay_sq)])
    egc = [jnp.exp(t) for t in gc]
    u = [_dot(t, a * b) for t, a, b in zip(t_mat, vs, beta)]
    w = [_dot(t, a * b) for t, a, b in zip(t_mat, kb, egc)]
    qk = [_dot(a, _pad_rows2(b), _NT) for a, b in zip(q, k)]
    attn = [jnp.where(incl, a * b, 0.0) for a, b in zip(qk, decay)]
    g_last = [jnp.sum(jnp.where(row == C - 1, t, 0.0), axis=0, keepdims=True) for t in gc]
    qe = [a * b for a, b in zip(q, egc)]
    kd = [a * jnp.exp(b - c_) for a, b, c_ in zip(k, g_last, gc)]
    egl = [jnp.exp(t) for t in g_last]

    def per_chunk(vals):
        return [jnp.concatenate(vals[4 * c:4 * c + 4], axis=1) for c in range(len(qcs))]

    return tuple(per_chunk(t) for t in (u, w, qe, kd, attn, egl))


def _gdn_scan_chunk(states, u, w, qe, kd, attn, egl, z, ng):
    hs = range(4)

    def sl(t, h):
        return t[:, h * 128:(h + 1) * 128]

    ws = [_dot(sl(w, h), states[h]) for h in hs]
    qs = [_dot(sl(qe, h), states[h]) for h in hs]
    v_new = [sl(u, h) - ws[h] for h in hs]
    av = [_dot(sl(attn, h), _pad_rows2(v_new[h])) for h in hs]
    kv = [_dot(sl(kd, h), v_new[h], _TN) for h in hs]
    nxt = tuple(states[h] * sl(egl, h) + kv[h] for h in hs)
    o = [qs[h] + av[h] for h in hs]
    on = [t * lax.rsqrt(jnp.mean(t * t, axis=-1, keepdims=True) + EPS) * ng for t in o]
    return nxt, jnp.concatenate(on, axis=1) * _silu(z)


def _gdn_blocks(T):
    tb = _pick(T, (512, 256, 128, 64))
    return tb, T // tb, tb // CHUNK


PREP_CHUNKS = 4


def _chunk_rows(i, n):
    return [pl.ds(pl.multiple_of((i * n + j) * CHUNK, CHUNK), CHUNK) for j in range(n)]


def _egl_rows(i, n, size):
    return [pl.ds(pl.multiple_of((i * n + j) * 8, 8), size) for j in range(n)]


def _gdn_prep_fwd(name, qa, ka, va, proj, alog, dtb):
    T = proj.shape[0]
    tb, nb, ncb = _gdn_blocks(T)
    n = PREP_CHUNKS if ncb % PREP_CHUNKS == 0 else 1

    def body(q_ref, k_ref, v_ref, tail_ref, alog_ref, dtb_ref, u_ref, w_ref, qe_ref, kd_ref, at_ref, egl_ref):
        def step(i, c):
            rows = _chunk_rows(i, n)
            u, w, qe, kd, at, egl = _gdn_prep([q_ref[r, :] for r in rows], [k_ref[r, :] for r in rows],
                                              [v_ref[r, :] for r in rows], [tail_ref[r, :].astype(F32) for r in rows],
                                              alog_ref[...], dtb_ref[...])
            for j, (r, e) in enumerate(zip(rows, _egl_rows(i, n, 8))):
                u_ref[r, :] = u[j]
                w_ref[r, :] = w[j].astype(BF16)
                qe_ref[r, :] = qe[j].astype(BF16)
                kd_ref[r, :] = kd[j].astype(BF16)
                at_ref[r, :] = at[j].astype(BF16)
                egl_ref[e, :] = jnp.broadcast_to(egl[j], (8, BW))
            return c

        lax.fori_loop(0, ncb // n, step, 0)

    blk = pl.BlockSpec((tb, BW), lambda j: (j, 0))
    vec = pl.BlockSpec((1, 128), lambda j: (0, 0))
    return pl.pallas_call(
        body, name=name, grid=(nb,),
        in_specs=[blk, blk, blk, pl.BlockSpec((tb, 128), lambda j: (j, COL_TAIL // 128)), vec, vec],
        out_specs=[blk] * 5 + [pl.BlockSpec((ncb * 8, BW), lambda j: (j, 0))],
        out_shape=[jax.ShapeDtypeStruct((T, BW), F32)] + [jax.ShapeDtypeStruct((T, BW), BF16)] * 4
        + [jax.ShapeDtypeStruct((T // 8, BW), F32)],
        compiler_params=_cp("parallel"),
    )(qa, ka, va, proj, alog, dtb)


def _gdn_prep_bwd(name, qa, ka, va, proj, alog, dtb, du, dw, dqe, dkd, dat, degl):
    T = proj.shape[0]
    tb, nb, ncb = _gdn_blocks(T)
    n = PREP_CHUNKS if ncb % PREP_CHUNKS == 0 else 1

    def body(q_ref, k_ref, v_ref, tail_ref, alog_ref, dtb_ref, du_ref, dw_ref, dqe_ref, dkd_ref, dat_ref, degl_ref,
             dq_ref, dk_ref, dv_ref, dtail_ref, dalog_ref, ddtb_ref):
        first = pl.program_id(0) == 0

        def step(i, carry):
            pa, pd = carry
            rows = _chunk_rows(i, n)
            _, vjp = jax.vjp(_gdn_prep, [q_ref[r, :] for r in rows], [k_ref[r, :] for r in rows],
                             [v_ref[r, :] for r in rows], [tail_ref[r, :].astype(F32) for r in rows], alog_ref[...], dtb_ref[...])
            cot = tuple([ref[r, :] for r in rows] for ref in (du_ref, dw_ref, dqe_ref, dkd_ref, dat_ref))
            dq, dk, dv, dtail, da, dd = vjp(cot + ([degl_ref[e, :] for e in _egl_rows(i, n, 1)],))
            for j, r in enumerate(rows):
                dq_ref[r, :] = dq[j]
                dk_ref[r, :] = dk[j]
                dv_ref[r, :] = dv[j]
                dtail_ref[r, :] = dtail[j].astype(BF16)
            return pa + da, pd + dd

        zv = jnp.zeros((1, 128), F32)
        pa, pd = lax.fori_loop(0, ncb // n, step, (zv, zv))

        @pl.when(first)
        def _():
            dalog_ref[...] = pa
            ddtb_ref[...] = pd

        @pl.when(jnp.logical_not(first))
        def _():
            dalog_ref[...] += pa
            ddtb_ref[...] += pd

    blk = pl.BlockSpec((tb, BW), lambda j: (j, 0))
    vec = pl.BlockSpec((1, 128), lambda j: (0, 0))
    return pl.pallas_call(
        body, name=name, grid=(nb,),
        in_specs=[blk, blk, blk, pl.BlockSpec((tb, 128), lambda j: (j, COL_TAIL // 128)), vec, vec]
        + [blk] * 5 + [pl.BlockSpec((ncb * 8, BW), lambda j: (j, 0))],
        out_specs=[blk, blk, blk, pl.BlockSpec((tb, 128), lambda j: (j, 0)), vec, vec],
        out_shape=[jax.ShapeDtypeStruct((T, BW), F32)] * 3 + [jax.ShapeDtypeStruct((T, 128), BF16)]
        + [jax.ShapeDtypeStruct((1, 128), F32)] * 2,
        compiler_params=_cp("arbitrary"),
    )(qa, ka, va, proj, alog, dtb, du, dw, dqe, dkd, dat, degl)


def _gdn_fwd(name, u, w, qe, kd, at, egl, proj, ng):
    T = proj.shape[0]
    tb, nb, ncb = _gdn_blocks(T)

    def body(u_ref, w_ref, qe_ref, kd_ref, at_ref, egl_ref, z_ref, ng_ref, y_ref, sh_ref, state):
        @pl.when(pl.program_id(0) == 0)
        def _():
            state[...] = jnp.zeros((4, 128, 128), F32)

        def step(c, states):
            rows = pl.ds(pl.multiple_of(c * CHUNK, CHUNK), CHUNK)
            for h in range(4):
                sh_ref[h, c] = states[h]
            nxt, y = _gdn_scan_chunk(states, u_ref[rows, :], w_ref[rows, :], qe_ref[rows, :], kd_ref[rows, :],
                                     at_ref[rows, :], egl_ref[pl.ds(pl.multiple_of(c * 8, 8), 1), :],
                                     z_ref[rows, :].astype(F32),
                                     ng_ref[...])
            y_ref[rows, :] = y.astype(BF16)
            return nxt

        states = lax.fori_loop(0, ncb, step, tuple(state[h] for h in range(4)))
        for h in range(4):
            state[h] = states[h]

    blk = pl.BlockSpec((tb, BW), lambda j: (j, 0))
    vec = pl.BlockSpec((1, 128), lambda j: (0, 0))
    return pl.pallas_call(
        body, name=name, grid=(nb,),
        in_specs=[blk] * 5 + [pl.BlockSpec((ncb * 8, BW), lambda j: (j, 0)),
                              pl.BlockSpec((tb, BW), lambda j: (j, COL_CZ // BW)), vec],
        out_specs=[blk, pl.BlockSpec((4, ncb, 128, 128), lambda j: (0, j, 0, 0))],
        out_shape=[jax.ShapeDtypeStruct((T, BW), BF16), jax.ShapeDtypeStruct((4, T // CHUNK, 128, 128), F32)],
        scratch_shapes=[pltpu.VMEM((4, 128, 128), F32)],
        compiler_params=_cp("arbitrary"),
    )(u, w, qe, kd, at, egl, proj, ng)


def _gdn_bwd(name, u, w, qe, kd, at, egl, proj, dy, sh, ng):
    T = proj.shape[0]
    tb, nb, ncb = _gdn_blocks(T)

    def body(u_ref, w_ref, qe_ref, kd_ref, at_ref, egl_ref, z_ref, dy_ref, sh_ref, ng_ref,
             du_ref, dw_ref, dqe_ref, dkd_ref, dat_ref, degl_ref, dz_ref, dng_ref, dstate):
        first = pl.program_id(0) == 0

        @pl.when(first)
        def _():
            dstate[...] = jnp.zeros((4, 128, 128), F32)

        def step(it, carry):
            dstates, pn = carry
            c = ncb - 1 - it
            rows = pl.ds(pl.multiple_of(c * CHUNK, CHUNK), CHUNK)
            erow = pl.multiple_of(c * 8, 8)
            _, vjp = jax.vjp(_gdn_scan_chunk, tuple(sh_ref[h, c] for h in range(4)), u_ref[rows, :],
                             w_ref[rows, :].astype(F32), qe_ref[rows, :].astype(F32), kd_ref[rows, :].astype(F32),
                             at_ref[rows, :].astype(F32), egl_ref[pl.ds(erow, 1), :], z_ref[rows, :].astype(F32),
                             ng_ref[...])
            nxt, du, dw, dqe, dkd, dat, degl, dz, dn = vjp((dstates, dy_ref[rows, :]))
            du_ref[rows, :] = du
            dw_ref[rows, :] = dw
            dqe_ref[rows, :] = dqe
            dkd_ref[rows, :] = dkd
            dat_ref[rows, :] = dat
            degl_ref[pl.ds(erow, 8), :] = jnp.broadcast_to(degl, (8, BW))
            dz_ref[rows, :] = dz.astype(BF16)
            return nxt, pn + dn

        dstates, pn = lax.fori_loop(0, ncb, step, (tuple(dstate[h] for h in range(4)), jnp.zeros((1, 128), F32)))
        for h in range(4):
            dstate[h] = dstates[h]

        @pl.when(first)
        def _():
            dng_ref[...] = pn

        @pl.when(jnp.logical_not(first))
        def _():
            dng_ref[...] += pn

    blk = pl.BlockSpec((tb, BW), lambda j: (nb - 1 - j, 0))
    eblk = pl.BlockSpec((ncb * 8, BW), lambda j: (nb - 1 - j, 0))
    vec = pl.BlockSpec((1, 128), lambda j: (0, 0))
    return pl.pallas_call(
        body, name=name, grid=(nb,),
        in_specs=[blk] * 5 + [eblk, pl.BlockSpec((tb, BW), lambda j: (nb - 1 - j, COL_CZ // BW)), blk,
                              pl.BlockSpec((4, ncb, 128, 128), lambda j: (0, nb - 1 - j, 0, 0)), vec],
        out_specs=[blk] * 5 + [eblk, blk, vec],
        out_shape=[jax.ShapeDtypeStruct((T, BW), F32)] * 5 + [jax.ShapeDtypeStruct((T // 8, BW), F32),
                                                              jax.ShapeDtypeStruct((T, BW), BF16),
                                                              jax.ShapeDtypeStruct((1, 128), F32)],
        scratch_shapes=[pltpu.VMEM((4, 128, 128), F32)],
        compiler_params=_cp("arbitrary"),
    )(u, w, qe, kd, at, egl, proj, dy, sh, ng)


def _adamw_update(w_ref, g_ref, m_ref, v_ref, d_ref, nm_ref, nv_ref):
    gv = g_ref[...]
    m2 = ADAM_B1 * m_ref[...] + (1.0 - ADAM_B1) * gv
    v2 = ADAM_B2 * v_ref[...] + (1.0 - ADAM_B2) * jnp.square(gv)
    m_hat = m2 / (1.0 - ADAM_B1 ** ADAM_STEP)
    v_hat = v2 / (1.0 - ADAM_B2 ** ADAM_STEP)
    d_ref[...] = -ADAM_LR * (m_hat / (jnp.sqrt(v_hat) + ADAM_EPS) + ADAM_WD * w_ref[...])
    nm_ref[...] = m2
    nv_ref[...] = v2


def _adamw_many(name, ws, gs, ms, vs):
    n = len(ws)

    def body(*refs):
        for i in range(n):
            _adamw_update(*[refs[k * n + i] for k in range(7)])

    return pl.pallas_call(
        body, name=name,
        out_shape=[jax.ShapeDtypeStruct(a.shape, F32) for a in ws] * 3,
        compiler_params=_cp(),
    )(*ws, *gs, *ms, *vs)


def _adamw(name, w, g, m, v):
    R, C = w.shape
    br = _pick(R, (512, 256, 240, 128, 64, 8))
    body = functools.partial(_adamw_update)
    spec = pl.BlockSpec((br, C), lambda i: (i, 0))
    return pl.pallas_call(
        body, name=name, grid=(R // br,),
        in_specs=[spec] * 4, out_specs=[spec] * 3,
        out_shape=[jax.ShapeDtypeStruct((R, C), F32)] * 3,
        compiler_params=_cp("parallel"),
    )(w, g, m, v)


def _sum8(name, parts):
    _, R, C = parts.shape
    br = _pick(R, (176, 368, 64, 16, 8))

    def body(p_ref, o_ref):
        acc = p_ref[0].astype(F32)
        for d in range(1, N_DEV):
            acc = acc + p_ref[d].astype(F32)
        o_ref[...] = acc

    return pl.pallas_call(
        body, name=name, grid=(R // br,),
        in_specs=[pl.BlockSpec((N_DEV, br, C), lambda i: (0, i, 0))],
        out_specs=pl.BlockSpec((br, C), lambda i: (i, 0)),
        out_shape=jax.ShapeDtypeStruct((R, C), F32),
        compiler_params=_cp("parallel"),
    )(parts)


_ANY = pl.BlockSpec(memory_space=pl.ANY)
_MESH = pl.DeviceIdType.MESH


def _all_gather(name, shard):
    R, C = shard.shape

    def body(x_ref, out_ref, send_sems, recv_sems, local_sem):
        x, y, c = lax.axis_index("x"), lax.axis_index("y"), lax.axis_index("c")
        me, sibling = (x, y, c), (x, y, 1 - c)
        chips = [(1 - x, y), (x, 1 - y), (1 - x, 1 - y)]

        def slot(px, py, pc):
            return out_ref.at[4 * px + 2 * py + pc]

        def copy(k, block, to, src=None):
            return pltpu.make_async_remote_copy(
                src_ref=slot(*block) if src is None else src, dst_ref=slot(*block),
                send_sem=send_sems.at[k], recv_sem=recv_sems.at[k], device_id=to, device_id_type=_MESH)

        mine = pltpu.make_async_copy(x_ref, slot(*me), local_sem)
        mine.start()
        first = [copy(0, me, sibling, src=x_ref)]
        first += [copy(1 + j, me, (*chip, c), src=x_ref) for j, chip in enumerate(chips)]
        for cp in first:
            cp.start()
        passed = [copy(4 + j, (*chip, c), sibling) for j, chip in enumerate(chips)]
        for j, chip in enumerate(chips):
            copy(1 + j, (*chip, c), me).wait_recv()
            passed[j].start()
        copy(0, sibling, me).wait_recv()
        for j, chip in enumerate(chips):
            copy(4 + j, (*chip, 1 - c), me).wait_recv()
        for cp in first + passed:
            cp.wait_send()
        mine.wait()

    return pl.pallas_call(
        body, name=name,
        in_specs=[_ANY], out_specs=_ANY,
        out_shape=jax.ShapeDtypeStruct((N_DEV, R, C), shard.dtype),
        scratch_shapes=[pltpu.SemaphoreType.DMA((7,)), pltpu.SemaphoreType.DMA((7,)), pltpu.SemaphoreType.DMA],
    )(shard)


_HBM = pl.BlockSpec(memory_space=pltpu.HBM)
_SEM = pl.BlockSpec(memory_space=pltpu.SEMAPHORE)
_EFFECT = pltpu.SideEffectType.DATAFLOW_SIDE_EFFECTING


def _exchange_copies(src_ref, land_ref, send_sems, recv_sems, scatter):
    x, y, c = lax.axis_index("x"), lax.axis_index("y"), lax.axis_index("c")
    me = 4 * x + 2 * y + c
    copies = []
    for k in range(1, N_DEV):
        px, py, pc = x ^ ((k >> 2) & 1), y ^ ((k >> 1) & 1), c ^ (k & 1)
        src = src_ref.at[4 * px + 2 * py + pc] if scatter else src_ref
        copies.append(pltpu.make_async_remote_copy(
            src_ref=src, dst_ref=land_ref.at[me], send_sem=send_sems.at[k - 1], recv_sem=recv_sems.at[k - 1],
            device_id=(px, py, pc), device_id_type=_MESH))
    return copies


def _own_copy(src_ref, land_ref, send_sems, scatter):
    me = 4 * lax.axis_index("x") + 2 * lax.axis_index("y") + lax.axis_index("c")
    return pltpu.make_async_copy(src_ref.at[me] if scatter else src_ref, land_ref.at[me], send_sems.at[N_DEV - 1])


def _exchange_start(name, srcs, lands, scatter, after=None):
    n = len(srcs)

    def body(*refs):
        src_refs, land_refs = refs[:n], refs[n:2 * n]
        outs = refs[2 * n + (after is not None):]
        send, recv = outs[:n], outs[n:2 * n]
        token = refs[-1]
        for g in range(n):
            for cp in _exchange_copies(src_refs[g], land_refs[g], send[g], recv[g], scatter):
                cp.start()
            _own_copy(src_refs[g], land_refs[g], send[g], scatter).start()
        token[...] = jnp.zeros_like(token)

    outs = pl.pallas_call(
        body, name=name,
        out_shape=tuple([pltpu.SemaphoreType.DMA((N_DEV,))] * (2 * n)
                        + [pltpu.HBM(a.shape, a.dtype) for a in list(srcs) + list(lands)]
                        + [jax.ShapeDtypeStruct((8, 128), F32)]),
        in_specs=[_HBM] * (2 * n) + [_ANY] * (after is not None),
        out_specs=tuple([_SEM] * (2 * n) + [_HBM] * (2 * n) + [pl.BlockSpec(memory_space=pltpu.VMEM)]),
        input_output_aliases={i: 2 * n + i for i in range(2 * n)},
        compiler_params=pltpu.CompilerParams(has_side_effects=_EFFECT),
    )(*[pltpu.with_memory_space_constraint(a, pltpu.HBM) for a in list(srcs) + list(lands)],
      *([after] if after is not None else []))
    handles = [(outs[2 * n + g], outs[3 * n + g], outs[g], outs[n + g]) for g in range(n)]
    return handles, outs[-1]


def _exchange_wait(name, handles, after, scatter):
    n = len(handles)
    srcs, lands, sends, recvs = ([h[i] for h in handles] for i in range(4))

    def body(*refs):
        src_refs, land_refs = refs[:n], refs[n:2 * n]
        send, recv = refs[2 * n:3 * n], refs[3 * n:4 * n]
        for g in range(n):
            for cp in _exchange_copies(src_refs[g], land_refs[g], send[g], recv[g], scatter):
                cp.wait_send()
                cp.wait_recv()
            _own_copy(src_refs[g], land_refs[g], send[g], scatter).wait()

    outs = pl.pallas_call(
        body, name=name,
        out_shape=tuple(pltpu.HBM(a.shape, a.dtype) for a in srcs + lands),
        in_specs=tuple([_HBM] * (2 * n) + [_SEM] * (2 * n) + [_ANY]), out_specs=tuple([_HBM] * (2 * n)),
        input_output_aliases={i: i for i in range(2 * n)},
        compiler_params=pltpu.CompilerParams(has_side_effects=_EFFECT),
    )(*srcs, *lands, *sends, *recvs, after)
    return list(outs[n:])


def _rows(a):
    return a.reshape(-1, 1024)


def _rows_to_parts(full):
    n = full.shape[-2] // N_DEV
    t = full.reshape(full.shape[:-2] + (N_DEV, n, full.shape[-1]))
    return jnp.moveaxis(t, -3, 0)


def _parts_to_rows(parts):
    t = jnp.moveaxis(parts, 0, -3)
    return t.reshape(t.shape[:-3] + (t.shape[-3] * t.shape[-2], t.shape[-1]))


def _parts_to_cols(parts):
    t = jnp.moveaxis(parts, 0, -2)
    return t.reshape(t.shape[:-2] + (t.shape[-2] * t.shape[-1],))


def _join(parts, axis=0):
    total = sum(p.shape[axis] for p in parts)
    out, off = None, 0
    for p in parts:
        cfg = [(0, 0)] * p.ndim
        cfg[axis] = (off, total - off - p.shape[axis])
        t = jnp.pad(p, cfg)
        out = t if out is None else out + t
        off += p.shape[axis]
    return out


def _w_in_to_layout(w):
    tail = jnp.pad(w[4096:4104], ((0, PW - COL_TAIL - 8), (0, 0)))
    return jnp.concatenate([w[:4096], w[4104:P_IN], tail], axis=0)


def _w_in_from_layout(g):
    return _join([g[:4096], g[COL_TAIL:COL_TAIL + 8], g[4096:COL_TAIL]], axis=0)


def _block_diag(w):
    w = w.reshape(4, 2, 64, 64)
    return jnp.pad(w[:, 0], ((0, 0), (0, 64), (0, 64))) + jnp.pad(w[:, 1], ((0, 0), (64, 0), (64, 0)))


def _block_diag_grad(g):
    return jnp.stack([g[:, :64, :64], g[:, 64:, 64:]], axis=1).reshape(8, 64, 64)


def _ffn_forward(tag, x, norm, wg, wu, wd):
    h = _rms_fwd(tag + "_norm", x, norm)
    sa, ds, act = _ffn_up(tag + "_up", h, wg, wu)
    if callable(wd):
        wd = wd(act)
    x_out = _mm(tag + "_down", [(act, wd)], "nn", F32, res=x, scale=0.5)
    return x_out, (x, h, sa, ds, act), wd


def _dw_pair(name, da, db, h, after=None):
    T = h.shape[0]
    bm = FF // 2
    bk = _pick(T, (1024, 512, 256, 128))
    nk = T // bk

    def body(da_ref, db_ref, h_ref, *rest):
        og_ref, ou_ref, accg, accu = rest[-4:]
        k = pl.program_id(1)
        hv = h_ref[...]
        pg = _dot(da_ref[...], hv, _TN)
        pu = _dot(db_ref[...], hv, _TN)

        @pl.when(k == 0)
        def _():
            accg[...] = pg
            accu[...] = pu

        @pl.when(k > 0)
        def _():
            accg[...] += pg
            accu[...] += pu

        @pl.when(k == nk - 1)
        def _():
            og_ref[...] = accg[...].astype(BF16)
            ou_ref[...] = accu[...].astype(BF16)

    a_spec = pl.BlockSpec((bk, bm), lambda i, k: (k, i))
    o_spec = pl.BlockSpec((bm, D), lambda i, k: (i, 0))
    return pl.pallas_call(
        body, name=name, grid=(FF // bm, nk),
        in_specs=[a_spec, a_spec, pl.BlockSpec((bk, D), lambda i, k: (k, 0))] + [_ANY] * (after is not None),
        out_specs=[o_spec, o_spec],
        out_shape=[jax.ShapeDtypeStruct((FF, D), BF16)] * 2,
        scratch_shapes=[pltpu.VMEM((bm, D), F32)] * 2,
        compiler_params=_cp("parallel", "arbitrary"),
    )(da, db, h, *([after] if after is not None else []))


def _ffn_backward(tag, dx_out, saved, norm, wg, wu, wd, put, names, split=False):
    x, h, sa, ds, act = saved
    n_wg, n_wu, n_wd = names
    dwd = _mm(tag + "_dwd", [(act, dx_out)], "tn", BF16, scale=0.5, bm=FF // 2)
    tok = put({n_wd: dwd}) if split else None
    da, db = _ffn_dact(tag + "_dact", dx_out, wd, sa, ds, after=tok)
    dwg, dwu = _dw_pair(tag + "_dwgu", da, db, h, after=tok)
    tok = tok + put({n_wg: dwg, n_wu: dwu}) if split else put({n_wg: dwg, n_wu: dwu, n_wd: dwd})
    dh = _mm(tag + "_dh", [(da, wg), (db, wu)], "nn", F32, after=tok)
    dx, dnorm = _rms_bwd(tag + "_dnorm", x, norm + tok, dh, dx_out)
    return dx, dnorm


def _mixer_params(p):
    alog = jnp.pad(p["gdn_a_log"], (4, 120))[None]
    dtb = jnp.pad(p["gdn_dt_bias"], (4, 120))[None]
    bias = jnp.repeat(p["sgu_b"].T, 128, axis=1)
    return dict(
        ln_g=p["sgu_ln_g"][None], ln_b=p["sgu_ln_b"][None], sgu_w=p["sgu_w"], sgu_bias=bias,
        lru_cw=p["lru_conv_w"], lru_cb=p["lru_conv_b"][None], wa=_block_diag(p["lru_wa"]), ba=p["lru_ba"][None],
        wx=_block_diag(p["lru_wx"]), bx=p["lru_bx"][None], lam=p["lru_lambda"][None],
        gdn_cw=p["gdn_conv_w"], alog=alog, dtb=dtb, ng=p["gdn_norm_g"][None],
        pool_w=p["pool_w"], pool_sc=p["pool_scale"][None])


def _mix_forward(tag, x, p, mp):
    h = _rms_fwd(tag + "_norm", x, p["mix_norm"][None])
    proj = _mm(tag + "_proj", [(h, p["w_in"])], "nt", BF16, bm=_pick(x.shape[0], (2048, 1024, 512, 256, 128)))
    y_a = _sgu_fwd(tag + "_sgu", proj, mp["ln_g"], mp["ln_b"], mp["sgu_w"], mp["sgu_bias"])
    y_b, hc = _lru_fwd(tag + "_lru", proj, mp["lru_cw"], mp["lru_cb"], mp["wa"], mp["ba"], mp["wx"], mp["bx"],
                       mp["lam"])
    qa = _conv_fwd(tag + "_convq", proj, COL_CQ, mp["gdn_cw"], 0)
    ka = _conv_fwd(tag + "_convk", proj, COL_CK, mp["gdn_cw"], 512)
    va = _conv_fwd(tag + "_convv", proj, COL_CV, mp["gdn_cw"], 1024)
    prep = _gdn_prep_fwd(tag + "_gdnprep", qa, ka, va, proj, mp["alog"], mp["dtb"])
    y_c, sh = _gdn_fwd(tag + "_gdn", *prep, proj, mp["ng"])
    y_d = _pool_fwd(tag + "_pool", proj, mp["pool_w"], mp["pool_sc"])
    ys = (y_a, y_b, y_c, y_d)
    if callable(p["w_branch"]):
        p["w_branch"] = p["w_branch"](y_d)
    merged = _merge_fwd(tag + "_merge", ys, p["w_branch"], proj)
    if callable(p["w_out"]):
        p["w_out"] = p["w_out"](merged)
    x_out = _mm(tag + "_out", [(merged, p["w_out"])], "nn", F32, res=x)
    return x_out, (x, h, proj, hc, qa, ka, va, prep, sh, ys, merged)


def _mix_backward(tag, dx_out, saved, p, mp, put):
    x, h, proj, hc, qa, ka, va, prep, sh, ys, merged = saved
    T = x.shape[0]
    g = {}
    dmerged = _mm(tag + "_dmerged", [(dx_out, p["w_out"])], "nt", F32)
    g["w_out"] = _mm(tag + "_dwout", [(merged, dx_out)], "tn", BF16)
    outs = _merge_bwd(tag + "_dmerge", dmerged, ys, p["w_branch"], proj)
    dgates, dbrs, dys = outs[:NBR], outs[NBR:2 * NBR], outs[2 * NBR:]
    g["w_branch"] = _dwb(tag + "_dwb", dbrs, ys)

    du, dv, dln_g, dln_b, dsgu_w, dbias = _sgu_bwd(tag + "_dsgu", proj, dys[0], mp["ln_g"], mp["ln_b"], mp["sgu_w"],
                                                  mp["sgu_bias"])
    g["sgu_ln_g"], g["sgu_ln_b"], g["sgu_w"] = dln_g[0], dln_b[0], dsgu_w
    g["sgu_b"] = dbias.reshape(128, 4, 128).sum(axis=2).T

    (dbx, dbg, dcw, dcb, dwa, dba, dwx, dbxb, dlam) = _lru_bwd(
        tag + "_dlru", proj, dys[1], hc, mp["lru_cw"], mp["lru_cb"], mp["wa"], mp["ba"], mp["wx"], mp["bx"], mp["lam"])
    g["lru_conv_w"], g["lru_conv_b"], g["lru_ba"], g["lru_bx"], g["lru_lambda"] = dcw, dcb[0], dba[0], dbxb[0], dlam[0]
    g["lru_wa"], g["lru_wx"] = _block_diag_grad(dwa), _block_diag_grad(dwx)

    *dprep, dz, dng = _gdn_bwd(tag + "_dgdn", *prep, proj, dys[2], sh, mp["ng"])
    dqa, dka, dva, dtail, dalog, ddtb = _gdn_prep_bwd(tag + "_dgdnprep", qa, ka, va, proj, mp["alog"], mp["dtb"], *dprep)
    g["gdn_a_log"], g["gdn_dt_bias"], g["gdn_norm_g"] = dalog[0, 4:8], ddtb[0, 4:8], dng[0]
    dq, dcwq = _conv_bwd(tag + "_dconvq", proj, COL_CQ, dqa, mp["gdn_cw"], 0)
    dk, dcwk = _conv_bwd(tag + "_dconvk", proj, COL_CK, dka, mp["gdn_cw"], 512)
    dv_, dcwv = _conv_bwd(tag + "_dconvv", proj, COL_CV, dva, mp["gdn_cw"], 1024)
    g["gdn_conv_w"] = jnp.concatenate([dcwq, dcwk, dcwv], axis=1)

    dd, dpw, dsc = _pool_bwd(tag + "_dpool", proj, dys[3], mp["pool_w"], mp["pool_sc"])
    g["pool_w"], g["pool_scale"] = dpw, dsc[0]

    dproj = jnp.concatenate([du, dv, dbx, dbg, dq, dk, dv_, dz, dd, *dgates, dtail,
                             jnp.zeros((T, PW - COL_TAIL - 128), BF16)], axis=1)
    dw_in = _mm(tag + "_dwin", [(dproj, h)], "tn", BF16)
    tok = put(dict(w_in=_w_in_from_layout(dw_in), w_branch=g.pop("w_branch"), w_out=g.pop("w_out")))
    dh = _mm(tag + "_dh", [(dproj, p["w_in"])], "nn", F32, bm=_pick(T, (2048, 1024, 512, 256, 128)), after=tok)
    dx, dnorm = _rms_bwd(tag + "_dnorm", x, p["mix_norm"][None] + tok, dh, dx_out)
    g["mix_norm"] = dnorm[0]
    return dx, g


_BIG = ("ff1_wg", "ff1_wu", "ff1_wd", "w_in", "w_branch", "w_out", "ff2_wg", "ff2_wu", "ff2_wd")
_COL_SHARDED = ("ff1_wg", "ff1_wu", "w_in", "w_branch", "ff2_wg", "ff2_wu")
_SMALL = ("ff1_norm", "mix_norm", "sgu_ln_g", "sgu_ln_b", "sgu_w", "sgu_b", "lru_conv_w", "lru_conv_b", "lru_wa",
          "lru_ba", "lru_wx", "lru_bx", "lru_lambda", "gdn_conv_w", "gdn_a_log", "gdn_dt_bias", "gdn_norm_g", "pool_w",
          "pool_scale", "ff2_norm", "final_norm")
_WEIGHTS = ("ff1_norm", "ff1_wg", "ff1_wu", "ff1_wd", "mix_norm", "w_in", "sgu_ln_g", "sgu_ln_b", "sgu_w", "sgu_b",
            "lru_conv_w", "lru_conv_b", "lru_wa", "lru_ba", "lru_wx", "lru_bx", "lru_lambda", "gdn_conv_w", "gdn_a_log",
            "gdn_dt_bias", "gdn_norm_g", "pool_w", "pool_scale", "w_branch", "w_out", "ff2_norm", "ff2_wg", "ff2_wu",
            "ff2_wd", "final_norm")
_CONV_SHARDED = ("lru_conv_w", "gdn_conv_w")
PACK_ROW_ALIGN = 16
_GROUPS = (("ff1", ("ff1_wg", "ff1_wu", "ff1_wd")), ("mix", ("w_in", "w_branch", "w_out")),
           ("ff2", ("ff2_wg", "ff2_wu", "ff2_wd")))


def _pad_rows(a, mult):
    pad = (-a.shape[-2]) % mult
    if pad == 0:
        return a
    return jnp.pad(a, [(0, 0)] * (a.ndim - 2) + [(0, pad), (0, 0)])


def _my_index():
    return 4 * lax.axis_index("x") + 2 * lax.axis_index("y") + lax.axis_index("c")


def _landing(shape, dtype):
    return lax.empty((N_DEV,) + tuple(shape), dtype)


def _stored(n, a):
    return jnp.swapaxes(a, -1, -2) if n in _COL_SHARDED else a


_FIRST = ("ff1_wg", "ff1_wu", "ff1_wd")


def _gather_first(w):
    names = _FIRST
    shards = [_rows(_stored(n, w[n][0]).astype(BF16)) for n in names]
    got = _all_gather("gather_first", jnp.concatenate(shards, axis=0))
    out, r = {}, 0
    for n, s in zip(names, shards):
        out[n] = got[:, r:r + s.shape[0]].reshape(-1, 1024)
        r += s.shape[0]
    return out, got


def _gather_start(w, after):
    conv = _pad_rows(jnp.concatenate([w[n].reshape(1, -1) for n in _CONV_SHARDED], axis=1), 8)
    keys, srcs = ["conv"], [conv]
    for l in range(2):
        for sub, (_, names) in enumerate(_GROUPS):
            for n in names:
                if l > 0 or n not in _FIRST:
                    keys.append((l, sub, n))
                    srcs.append(_stored(n, w[n][l]).astype(BF16))
    lands = [_landing(s.shape, s.dtype) for s in srcs]
    handles, token = _exchange_start("gather_start", srcs, lands, scatter=False, after=after)
    return dict(zip(keys, handles)), token


def _gather_finish(l, sub, handles, first, after):
    names = _GROUPS[sub][1]
    if (l, sub) == (0, 0):
        out = dict(first)
        for n in names:
            if n not in _FIRST:
                out[n] = lambda later, n=n: _parts_to_rows(
                    _exchange_wait(f"gather_wait_00_{n}", [handles[(0, 0, n)]], later, scatter=False)[0])
    elif sub == 1:
        out = {n: (lambda later, n=n: _parts_to_rows(
            _exchange_wait(f"gather_wait_{l}{sub}_{n}", [handles[(l, sub, n)]], later, scatter=False)[0])) for n in names}
        out["w_in"] = out["w_in"](after)
    else:
        lands = _exchange_wait(f"gather_wait_{l}{sub}", [handles[(l, sub, n)] for n in names], after, scatter=False)
        out = {n: _parts_to_rows(land) for n, land in zip(names, lands)}
    if "w_in" in out:
        out["w_in"] = _w_in_to_layout(out["w_in"])
    return out


def _scatter_start(l, sub, grads):
    srcs, shapes = [], []
    for n in grads:
        parts = _rows_to_parts(grads[n])
        shapes.append(parts.shape[1:])
        srcs.append(_pad_rows(parts.reshape(N_DEV, -1, 1024), PACK_ROW_ALIGN))
    lands = [_landing(s.shape[1:], s.dtype) for s in srcs]
    tag = f"{l}{sub}" + ("" if len(grads) == len(_GROUPS[sub][1]) else "_" + "_".join(grads))
    handles, token = _exchange_start(f"scatter_start_{tag}", srcs, lands, scatter=True)
    return handles, (tag, tuple(grads), shapes), token


def _scatter_finish(l, sub, handles, meta, after):
    tag, names, shapes = meta
    lands = _exchange_wait(f"scatter_wait_{tag}", handles, after, scatter=True)
    out = {}
    for n, land, shape in zip(names, lands, shapes):
        size = 1
        for s in shape:
            size *= s
        summed = _sum8(f"sum_{l}{sub}_{n}", land)
        out[n] = _stored(n, summed[:size // 1024].reshape(shape))
    return out


def _gather_conv_finish(w, handles, after):
    gconv = _exchange_wait("gather_wait_conv", [handles["conv"]], after, scatter=False)[0][:, 0]
    full, r = {}, 0
    for n in _CONV_SHARDED:
        sz = w[n].size
        full[n] = _parts_to_cols(gconv[:, r:r + sz].reshape((N_DEV,) + w[n].shape))
        r += sz
    return full


def _forward_backward(x, tgt, w, conv, get_weights, put_grads, put_small, token):
    saved, params = [], []
    for l in range(2):
        p = {n: w[n][l] for n in _SMALL if n != "final_norm"}
        for n in _CONV_SHARDED:
            p[n] = conv[n][l]
        mp = _mixer_params(p)
        tok = token[:1, :1] if l == 0 else 0.0
        p.update(get_weights(l, 0, x))
        x, s1, p["ff1_wd"] = _ffn_forward(f"l{l}_ff1", x, p["ff1_norm"][None] + tok, p["ff1_wg"], p["ff1_wu"],
                                          p["ff1_wd"])
        p.update(get_weights(l, 1, x))
        x, s2 = _mix_forward(f"l{l}_mix", x, p, mp)
        p.update(get_weights(l, 2, x))
        x, s3, _ = _ffn_forward(f"l{l}_ff2", x, p["ff2_norm"][None], p["ff2_wg"], p["ff2_wu"], p["ff2_wd"])
        saved.append((s1, s2, s3))
        params.append((p, mp))
    loss, dx, dfinal = _final_loss("loss_head", x, w["final_norm"][None], tgt)
    tok = 0.0
    for l in (1, 0):
        p, mp = params[l]
        s1, s2, s3 = saved[l]
        g = {}

        def put(sub):
            return lambda grads, l=l: put_grads(l, sub, grads)[:1, :1]

        dx, dn = _ffn_backward(f"l{l}_ff2", dx, s3, p["ff2_norm"][None] + tok, p["ff2_wg"], p["ff2_wu"], p["ff2_wd"],
                               put(2), _GROUPS[2][1])
        g["ff2_norm"] = dn[0]
        dx, gm = _mix_backward(f"l{l}_mix", dx, s2, p, mp, put(1))
        g.update(gm)
        tok = 0.0
        if l == 0:
            keep = {n: g.pop(n) for n in ("gdn_a_log", "gdn_dt_bias")}
            tok = put_small("0a", g, True)[:1, :1]
            g = keep
        dx, dn = _ffn_backward(f"l{l}_ff1", dx, s1, p["ff1_norm"][None] + tok, p["ff1_wg"], p["ff1_wu"], p["ff1_wd"],
                               put(0), _GROUPS[0][1], split=(l == 0))
        g["ff1_norm"] = dn[0]
        if l == 1:
            g["final_norm"] = dfinal[0]
            g["loss"] = loss[0, :1]
        tok = put_small("1" if l == 1 else "0b", g, False)[:1, :1]
    return dx


SMALL_PIECE = 8 * 1024


def _pack_small(d, names):
    pieces = []
    for n in names:
        flat = d[n].reshape(-1)
        pieces.append(jnp.pad(flat, (0, (-flat.size) % SMALL_PIECE)).reshape(-1, 1024))
    return jnp.concatenate(pieces, axis=0)


def _unpack_small(pack, shapes, names):
    out, r = {}, 0
    for n in names:
        size = 1
        for s in shapes[n]:
            size *= s
        rows = -(-size // SMALL_PIECE) * 8
        out[n] = pack[r:r + rows].reshape(-1)[:size].reshape(shapes[n])
        r += rows
    return out


def _small_names(grads):
    return tuple(n for n in _SMALL + ("loss",) if n in grads)


def _small_start(tag, grads, narrow):
    pack = _pack_small(grads, _small_names(grads))
    if narrow:
        pack = _pad_rows(pack.astype(BF16), PACK_ROW_ALIGN)
    handles, token = _exchange_start(f"small_start_{tag}", [pack], [_landing(pack.shape, pack.dtype)], scatter=False)
    return handles, {n: grads[n].shape for n in _small_names(grads)}, token


def _small_finish(tag, handles, shapes, after):
    landed = _exchange_wait(f"small_wait_{tag}", handles, after, scatter=False)[0]
    return _unpack_small(_sum8(f"sum_small_{tag}", landed), shapes, _small_names(shapes))


def _as2d(a):
    if a.ndim == 1:
        return a.reshape(1, -1)
    return a.reshape(-1, a.shape[-1])


def kernel(x, ff1_norm, ff1_wg, ff1_wu, ff1_wd, mix_norm, w_in, sgu_ln_g, sgu_ln_b, sgu_w, sgu_b, lru_conv_w, lru_conv_b, lru_wa, lru_ba, lru_wx, lru_bx, lru_lambda, gdn_conv_w, gdn_a_log, gdn_dt_bias, gdn_norm_g, pool_w, pool_scale, w_branch, w_out, ff2_norm, ff2_wg, ff2_wu, ff2_wd, final_norm, loss_target, m_ff1_norm, m_ff1_wg, m_ff1_wu, m_ff1_wd, m_mix_norm, m_w_in, m_sgu_ln_g, m_sgu_ln_b, m_sgu_w, m_sgu_b, m_lru_conv_w, m_lru_conv_b, m_lru_wa, m_lru_ba, m_lru_wx, m_lru_bx, m_lru_lambda, m_gdn_conv_w, m_gdn_a_log, m_gdn_dt_bias, m_gdn_norm_g, m_pool_w, m_pool_scale, m_w_branch, m_w_out, m_ff2_norm, m_ff2_wg, m_ff2_wu, m_ff2_wd, m_final_norm, v_ff1_norm, v_ff1_wg, v_ff1_wu, v_ff1_wd, v_mix_norm, v_w_in, v_sgu_ln_g, v_sgu_ln_b, v_sgu_w, v_sgu_b, v_lru_conv_w, v_lru_conv_b, v_lru_wa, v_lru_ba, v_lru_wx, v_lru_bx, v_lru_lambda, v_gdn_conv_w, v_gdn_a_log, v_gdn_dt_bias, v_gdn_norm_g, v_pool_w, v_pool_scale, v_w_branch, v_w_out, v_ff2_norm, v_ff2_wg, v_ff2_wu, v_ff2_wd, v_final_norm):
    w = dict(ff1_norm=ff1_norm, ff1_wg=ff1_wg, ff1_wu=ff1_wu, ff1_wd=ff1_wd, mix_norm=mix_norm, w_in=w_in,
             sgu_ln_g=sgu_ln_g, sgu_ln_b=sgu_ln_b, sgu_w=sgu_w, sgu_b=sgu_b, lru_conv_w=lru_conv_w,
             lru_conv_b=lru_conv_b, lru_wa=lru_wa, lru_ba=lru_ba, lru_wx=lru_wx, lru_bx=lru_bx, lru_lambda=lru_lambda,
             gdn_conv_w=gdn_conv_w, gdn_a_log=gdn_a_log, gdn_dt_bias=gdn_dt_bias, gdn_norm_g=gdn_norm_g, pool_w=pool_w,
             pool_scale=pool_scale, w_branch=w_branch, w_out=w_out, ff2_norm=ff2_norm, ff2_wg=ff2_wg, ff2_wu=ff2_wu,
             ff2_wd=ff2_wd, final_norm=final_norm)
    m = dict(ff1_norm=m_ff1_norm, ff1_wg=m_ff1_wg, ff1_wu=m_ff1_wu, ff1_wd=m_ff1_wd, mix_norm=m_mix_norm, w_in=m_w_in,
             sgu_ln_g=m_sgu_ln_g, sgu_ln_b=m_sgu_ln_b, sgu_w=m_sgu_w, sgu_b=m_sgu_b, lru_conv_w=m_lru_conv_w,
             lru_conv_b=m_lru_conv_b, lru_wa=m_lru_wa, lru_ba=m_lru_ba, lru_wx=m_lru_wx, lru_bx=m_lru_bx,
             lru_lambda=m_lru_lambda, gdn_conv_w=m_gdn_conv_w, gdn_a_log=m_gdn_a_log, gdn_dt_bias=m_gdn_dt_bias,
             gdn_norm_g=m_gdn_norm_g, pool_w=m_pool_w, pool_scale=m_pool_scale, w_branch=m_w_branch, w_out=m_w_out,
             ff2_norm=m_ff2_norm, ff2_wg=m_ff2_wg, ff2_wu=m_ff2_wu, ff2_wd=m_ff2_wd, final_norm=m_final_norm)
    v = dict(ff1_norm=v_ff1_norm, ff1_wg=v_ff1_wg, ff1_wu=v_ff1_wu, ff1_wd=v_ff1_wd, mix_norm=v_mix_norm, w_in=v_w_in,
             sgu_ln_g=v_sgu_ln_g, sgu_ln_b=v_sgu_ln_b, sgu_w=v_sgu_w, sgu_b=v_sgu_b, lru_conv_w=v_lru_conv_w,
             lru_conv_b=v_lru_conv_b, lru_wa=v_lru_wa, lru_ba=v_lru_ba, lru_wx=v_lru_wx, lru_bx=v_lru_bx,
             lru_lambda=v_lru_lambda, gdn_conv_w=v_gdn_conv_w, gdn_a_log=v_gdn_a_log, gdn_dt_bias=v_gdn_dt_bias,
             gdn_norm_g=v_gdn_norm_g, pool_w=v_pool_w, pool_scale=v_pool_scale, w_branch=v_w_branch, w_out=v_w_out,
             ff2_norm=v_ff2_norm, ff2_wg=v_ff2_wg, ff2_wu=v_ff2_wu, ff2_wd=v_ff2_wd, final_norm=v_final_norm)

    first, got_first = _gather_first(w)
    handles, token = _gather_start(w, got_first)
    conv = _gather_conv_finish(w, handles, token)
    pending = {}

    def get_weights(l, sub, after):
        return _gather_finish(l, sub, handles, first, after)

    def put_grads(l, sub, grads):
        hs, meta, tok = _scatter_start(l, sub, grads)
        pending[(l, sub, meta[0])] = (hs, meta)
        return tok

    def put_small(tag, grads, narrow):
        hs, shapes, tok = _small_start(tag, grads, narrow)
        pending[tag] = (hs, shapes)
        return tok

    T = x.shape[1]
    dx = _forward_backward(x.reshape(T, D), loss_target.reshape(T, D), w, conv, get_weights, put_grads, put_small,
                           token)
    per = {}
    for key in pending:
        if isinstance(key, tuple):
            per.setdefault(key[:2], {}).update(_scatter_finish(*key[:2], *pending[key], dx))
        else:
            per[key] = _small_finish(key, *pending[key], dx)
    grad = {n: jnp.stack([per[(0, sub)][n], per[(1, sub)][n]]) for sub, (_, names) in enumerate(_GROUPS) for n in names}
    layer0 = {**per["0a"], **per["0b"]}
    small = {n: _join([layer0[n].reshape(-1), per["1"][n].reshape(-1)]).reshape((2,) + layer0[n].shape)
             for n in layer0}
    small["final_norm"] = per["1"]["final_norm"]
    loss = per["1"]["loss"][0]
    me = _my_index()
    for n in _SMALL:
        if n in _CONV_SHARDED:
            width = w[n].shape[-1]
            grad[n] = lax.dynamic_slice_in_dim(small[n], me * width, width, axis=2)
        else:
            grad[n] = small[n]

    delta, new_m, new_v = {}, {}, {}
    for n in _BIG:
        d_, m_, v_ = _adamw("adamw_" + n, _as2d(w[n]), _as2d(grad[n]), _as2d(m[n]), _as2d(v[n]))
        delta[n], new_m[n], new_v[n] = (t.reshape(w[n].shape) for t in (d_, m_, v_))

    outs = _adamw_many("adamw_small", *[[_as2d(t[n]) for n in _SMALL] for t in (w, grad, m, v)])
    for k, dst in enumerate((delta, new_m, new_v)):
        for i, n in enumerate(_SMALL):
            dst[n] = outs[k * len(_SMALL) + i].reshape(w[n].shape)

    return (loss, dx.reshape(x.shape), *[grad[n] for n in _WEIGHTS], *[delta[n] for n in _WEIGHTS],
            *[new_m[n] for n in _WEIGHTS], *[new_v[n] for n in _WEIGHTS])
```
